```python
import jax, jax.numpy as jnp
from jax import lax
import numpy as np

D_MODEL = 1024
BATCH = 8
SEQ = 4096
DEPTH = 2

N_A_LAYERS = DEPTH // 2
N_B_LAYERS = DEPTH - N_A_LAYERS

A_KEY_DIM = 128
A_HEADS = D_MODEL // A_KEY_DIM
A_VAL_DIM = D_MODEL // A_HEADS
A_DK = A_HEADS * A_KEY_DIM
A_DV = A_HEADS * A_VAL_DIM
A_CHUNK = 64

B_HEAD_DIM = 128
B_HEADS = D_MODEL // B_HEAD_DIM
B_WIDTH = B_HEADS * B_HEAD_DIM
B_Q_BLOCK = 128

D_FF = ((8 * D_MODEL // 3 + 127) // 128) * 128
CONV_W = 3
EPS = 1e-6
NEG_INF = -1e30

kernel_name = "yoco_hgrn2_fox_adaln_convffn"


def rms_norm(x):
    xf = x.astype(jnp.float32)
    y = xf * lax.rsqrt(jnp.mean(xf * xf, axis=-1, keepdims=True) + EPS)
    return y.astype(x.dtype)


def modulate(x, shift, scale):
    return rms_norm(x) * (1 + scale[:, None, :]) + shift[:, None, :]


def hgrn2_mixer(h, w_in, lb, norm_g, w_out):
    bsz, seq, _ = h.shape
    n_chunks = seq // A_CHUNK
    proj = (h @ w_in).astype(jnp.float32)
    q, f, i, g = jnp.split(proj, [A_DK, 2 * A_DK, 2 * A_DK + A_DV], axis=-1)
    q = jax.nn.silu(q)
    fg = lb + (1 - lb) * jax.nn.sigmoid(f)
    log_f = jnp.log(fg)
    k = 1 - fg

    def to_chunks(t, d):
        return t.reshape(bsz, n_chunks, A_CHUNK, A_HEADS, d).transpose(0, 3, 1, 2, 4)

    q, k, log_f = to_chunks(q, A_KEY_DIM), to_chunks(k, A_KEY_DIM), to_chunks(log_f, A_KEY_DIM)
    v = to_chunks(i, A_VAL_DIM)
    b = jnp.cumsum(log_f, axis=3)
    b_mid = b[:, :, :, A_CHUNK // 2:A_CHUNK // 2 + 1, :]
    q_intra = q * jnp.exp(b - b_mid)
    k_intra = k * jnp.exp(b_mid - b)
    scores = jnp.einsum('bhnck,bhnsk->bhncs', q_intra, k_intra)
    causal = jnp.tril(jnp.ones((A_CHUNK, A_CHUNK), dtype=bool))
    scores = jnp.where(causal, scores, 0.0)
    o_intra = jnp.einsum('bhncs,bhnsv->bhncv', scores, v)
    b_last = b[:, :, :, -1:, :]
    q_inter = q * jnp.exp(b)
    k_state = k * jnp.exp(b_last - b)
    chunk_decay = jnp.exp(b_last[:, :, :, 0, :])

    def step(state, xs):
        qc, kc, vc, dc = xs
        o = jnp.einsum('bhck,bhkv->bhcv', qc, state)
        state = dc[..., None] * state + jnp.einsum('bhck,bhcv->bhkv', kc, vc)
        return state, o

    xs = (jnp.moveaxis(q_inter, 2, 0), jnp.moveaxis(k_state, 2, 0),
          jnp.moveaxis(v, 2, 0), jnp.moveaxis(chunk_decay, 2, 0))
    state0 = jnp.zeros((bsz, A_HEADS, A_KEY_DIM, A_VAL_DIM), jnp.float32)
    _, o_inter = lax.scan(step, state0, xs)
    o = o_intra + jnp.moveaxis(o_inter, 0, 2)
    o = o.transpose(0, 2, 3, 1, 4).reshape(bsz, seq, A_HEADS, A_VAL_DIM)
    gate = jax.nn.silu(g.reshape(bsz, seq, A_HEADS, A_VAL_DIM))
    o = rms_norm(o) * norm_g * gate
    return o.reshape(bsz, seq, A_DV).astype(h.dtype) @ w_out


def shared_kv(x, c, kv_ada_w, kv_ada_b, kv_w, kv_b_f, k_norm_g):
    bsz, seq, _ = x.shape
    shift, scale = jnp.split(jax.nn.silu(c) @ kv_ada_w + kv_ada_b, 2, axis=-1)
    h = modulate(x, shift, scale)
    proj = h @ kv_w
    k, v, f_logit = jnp.split(proj, [B_WIDTH, 2 * B_WIDTH], axis=-1)
    k = rms_norm(k.reshape(bsz, seq, B_HEADS, B_HEAD_DIM)) * k_norm_g
    k = k.transpose(0, 2, 1, 3)
    v = v.reshape(bsz, seq, B_HEADS, B_HEAD_DIM).transpose(0, 2, 1, 3)
    log_f = jax.nn.log_sigmoid((f_logit + kv_b_f).astype(jnp.float32))
    cum_log_f = jnp.cumsum(log_f.transpose(0, 2, 1), axis=-1)
    return k, v, cum_log_f


def fox_mixer(h, k, v, cum_log_f, w_q, q_norm_g, w_out):
    bsz, seq, _ = h.shape
    n_blocks = seq // B_Q_BLOCK
    q, og = jnp.split(h @ w_q, 2, axis=-1)
    q = rms_norm(q.reshape(bsz, seq, B_HEADS, B_HEAD_DIM)) * q_norm_g * (B_HEAD_DIM ** -0.5)
    q_blocks = q.reshape(bsz, n_blocks, B_Q_BLOCK, B_HEADS, B_HEAD_DIM).transpose(1, 0, 3, 2, 4)
    f_q = cum_log_f.reshape(bsz, B_HEADS, n_blocks, B_Q_BLOCK).transpose(2, 0, 1, 3)
    starts = jnp.arange(n_blocks) * B_Q_BLOCK
    key_pos = jnp.arange(seq)

    def attend_block(args):
        qb, fq, start = args
        q_pos = start + jnp.arange(B_Q_BLOCK)
        logits = jnp.einsum('bhqd,bhkd->bhqk', qb, k).astype(jnp.float32)
        logits = logits + (fq[..., :, None] - cum_log_f[:, :, None, :])
        logits = jnp.where(key_pos[None, :] <= q_pos[:, None], logits, NEG_INF)
        p = jax.nn.softmax(logits, axis=-1)
        return jnp.einsum('bhqk,bhkd->bhqd', p.astype(v.dtype), v)

    o = lax.map(attend_block, (q_blocks, f_q, starts))
    o = o.transpose(1, 0, 3, 2, 4).reshape(bsz, seq, B_WIDTH)
    o = o * jax.nn.sigmoid(og)
    return o @ w_out


def conv_glu_ffn(h, w_up, conv_w, conv_b, w_down):
    u = h @ w_up
    u = lax.conv_general_dilated(
        u, conv_w[:, None, :], window_strides=(1,), padding=[(CONV_W - 1, 0)],
        dimension_numbers=('NWC', 'WIO', 'NWC'), feature_group_count=2 * D_FF) + conv_b
    gate, val = jnp.split(u, 2, axis=-1)
    return (jax.nn.silu(gate) * val) @ w_down


def _fwd_setup_inputs(seed: int = 0) -> dict:
    key = jax.random.key(seed)
    ks = jax.random.split(key, 20)
    f32 = jnp.float32
    D = D_MODEL

    def nrm(k, shape, scale):
        return jax.random.normal(k, shape, f32) * scale

    return {
        "x": nrm(ks[0], (BATCH, SEQ, D), 1.0),
        "c": nrm(ks[1], (BATCH, D), 1.0),
        "ada_w": nrm(ks[2], (DEPTH, D, 6 * D), 0.5 * D ** -0.5),
        "ada_b": nrm(ks[3], (DEPTH, 6 * D), 0.02),
        "a_w_in": nrm(ks[4], (N_A_LAYERS, D, 2 * A_DK + 2 * A_DV), D ** -0.5),
        "a_lb_logits": nrm(ks[5], (N_A_LAYERS + 1, A_DK), 0.1),
        "a_norm_g": 1.0 + nrm(ks[6], (N_A_LAYERS, A_VAL_DIM), 0.02),
        "a_w_out": nrm(ks[7], (N_A_LAYERS, A_DV, D), A_DV ** -0.5),
        "kv_ada_w": nrm(ks[8], (D, 2 * D), 0.5 * D ** -0.5),
        "kv_ada_b": nrm(ks[9], (2 * D,), 0.02),
        "kv_w": nrm(ks[10], (D, 2 * B_WIDTH + B_HEADS), D ** -0.5),
        "kv_b_f": 3.0 + nrm(ks[11], (B_HEADS,), 0.5),
        "k_norm_g": 1.0 + nrm(ks[12], (B_HEAD_DIM,), 0.02),
        "b_w_q": nrm(ks[13], (N_B_LAYERS, D, 2 * B_WIDTH), D ** -0.5),
        "q_norm_g": 1.0 + nrm(ks[14], (N_B_LAYERS, B_HEAD_DIM), 0.02),
        "b_w_out": nrm(ks[15], (N_B_LAYERS, B_WIDTH, D), B_WIDTH ** -0.5),
        "ffn_w_up": nrm(ks[16], (DEPTH, D, 2 * D_FF), D ** -0.5),
        "ffn_conv_w": nrm(ks[17], (DEPTH, CONV_W, 2 * D_FF), CONV_W ** -0.5),
        "ffn_conv_b": nrm(ks[18], (DEPTH, 2 * D_FF), 0.02),
        "ffn_w_down": nrm(ks[19], (DEPTH, D_FF, D), D_FF ** -0.5),
    }


def _fwd_reference(x, c, ada_w, ada_b, a_w_in, a_lb_logits, a_norm_g, a_w_out,
              kv_ada_w, kv_ada_b, kv_w, kv_b_f, k_norm_g,
              b_w_q, q_norm_g, b_w_out,
              ffn_w_up, ffn_conv_w, ffn_conv_b, ffn_w_down):
    lb_all = jnp.cumsum(jax.nn.softmax(a_lb_logits.astype(jnp.float32), axis=0), axis=0)[:N_A_LAYERS]
    c_act = jax.nn.silu(c)
    k_sh = v_sh = cum_log_f = None
    for l in range(DEPTH):
        mod = c_act @ ada_w[l] + ada_b[l]
        sh1, sc1, g1, sh2, sc2, g2 = jnp.split(mod, 6, axis=-1)
        if l == N_A_LAYERS:
            k_sh, v_sh, cum_log_f = shared_kv(x, c, kv_ada_w, kv_ada_b, kv_w, kv_b_f, k_norm_g)
        h = modulate(x, sh1, sc1)
        if l < N_A_LAYERS:
            y = hgrn2_mixer(h, a_w_in[l], lb_all[l], a_norm_g[l], a_w_out[l])
        else:
            j = l - N_A_LAYERS
            y = fox_mixer(h, k_sh, v_sh, cum_log_f, b_w_q[j], q_norm_g[j], b_w_out[j])
        x = x + g1[:, None, :] * y
        h = modulate(x, sh2, sc2)
        x = x + g2[:, None, :] * conv_glu_ffn(h, ffn_w_up[l], ffn_conv_w[l], ffn_conv_b[l], ffn_w_down[l])
    return x


import jax as _jax
import jax.numpy as _jnp

TWIN_FORMAT = 'train_step'
FWD_PARAMS = ['x', 'c', 'ada_w', 'ada_b', 'a_w_in', 'a_lb_logits', 'a_norm_g', 'a_w_out', 'kv_ada_w', 'kv_ada_b', 'kv_w', 'kv_b_f', 'k_norm_g', 'b_w_q', 'q_norm_g', 'b_w_out', 'ffn_w_up', 'ffn_conv_w', 'ffn_conv_b', 'ffn_w_down']
TWIN_WEIGHTS = ['ada_w', 'ada_b', 'a_w_in', 'a_lb_logits', 'a_norm_g', 'a_w_out', 'kv_ada_w', 'kv_ada_b', 'kv_w', 'kv_b_f', 'k_norm_g', 'b_w_q', 'q_norm_g', 'b_w_out', 'ffn_w_up', 'ffn_conv_w', 'ffn_conv_b', 'ffn_w_down']
TWIN_DIFF_INPUT = 'x'
TWIN_INPUTS = ['x', 'c', 'ada_w', 'ada_b', 'a_w_in', 'a_lb_logits', 'a_norm_g', 'a_w_out', 'kv_ada_w', 'kv_ada_b', 'kv_w', 'kv_b_f', 'k_norm_g', 'b_w_q', 'q_norm_g', 'b_w_out', 'ffn_w_up', 'ffn_conv_w', 'ffn_conv_b', 'ffn_w_down', 'loss_target', 'm_ada_w', 'm_ada_b', 'm_a_w_in', 'm_a_lb_logits', 'm_a_norm_g', 'm_a_w_out', 'm_kv_ada_w', 'm_kv_ada_b', 'm_kv_w', 'm_kv_b_f', 'm_k_norm_g', 'm_b_w_q', 'm_q_norm_g', 'm_b_w_out', 'm_ffn_w_up', 'm_ffn_conv_w', 'm_ffn_conv_b', 'm_ffn_w_down', 'v_ada_w', 'v_ada_b', 'v_a_w_in', 'v_a_lb_logits', 'v_a_norm_g', 'v_a_w_out', 'v_kv_ada_w', 'v_kv_ada_b', 'v_kv_w', 'v_kv_b_f', 'v_k_norm_g', 'v_b_w_q', 'v_q_norm_g', 'v_b_w_out', 'v_ffn_w_up', 'v_ffn_conv_w', 'v_ffn_conv_b', 'v_ffn_w_down']
TWIN_OUTPUTS = ['loss', 'grad_x', 'grad_ada_w', 'grad_ada_b', 'grad_a_w_in', 'grad_a_lb_logits', 'grad_a_norm_g', 'grad_a_w_out', 'grad_kv_ada_w', 'grad_kv_ada_b', 'grad_kv_w', 'grad_kv_b_f', 'grad_k_norm_g', 'grad_b_w_q', 'grad_q_norm_g', 'grad_b_w_out', 'grad_ffn_w_up', 'grad_ffn_conv_w', 'grad_ffn_conv_b', 'grad_ffn_w_down', 'delta_ada_w', 'delta_ada_b', 'delta_a_w_in', 'delta_a_lb_logits', 'delta_a_norm_g', 'delta_a_w_out', 'delta_kv_ada_w', 'delta_kv_ada_b', 'delta_kv_w', 'delta_kv_b_f', 'delta_k_norm_g', 'delta_b_w_q', 'delta_q_norm_g', 'delta_b_w_out', 'delta_ffn_w_up', 'delta_ffn_conv_w', 'delta_ffn_conv_b', 'delta_ffn_w_down', 'new_m_ada_w', 'new_m_ada_b', 'new_m_a_w_in', 'new_m_a_lb_logits', 'new_m_a_norm_g', 'new_m_a_w_out', 'new_m_kv_ada_w', 'new_m_kv_ada_b', 'new_m_kv_w', 'new_m_kv_b_f', 'new_m_k_norm_g', 'new_m_b_w_q', 'new_m_q_norm_g', 'new_m_b_w_out', 'new_m_ffn_w_up', 'new_m_ffn_conv_w', 'new_m_ffn_conv_b', 'new_m_ffn_w_down', 'new_v_ada_w', 'new_v_ada_b', 'new_v_a_w_in', 'new_v_a_lb_logits', 'new_v_a_norm_g', 'new_v_a_w_out', 'new_v_kv_ada_w', 'new_v_kv_ada_b', 'new_v_kv_w', 'new_v_kv_b_f', 'new_v_k_norm_g', 'new_v_b_w_q', 'new_v_q_norm_g', 'new_v_b_w_out', 'new_v_ffn_w_up', 'new_v_ffn_conv_w', 'new_v_ffn_conv_b', 'new_v_ffn_w_down']
TWIN_LEAF_KINDS = {'loss': 'loss', 'grad_x': 'grad_x', 'grad_ada_w': 'grad_w', 'grad_ada_b': 'grad_w', 'grad_a_w_in': 'grad_w', 'grad_a_lb_logits': 'grad_w', 'grad_a_norm_g': 'grad_w', 'grad_a_w_out': 'grad_w', 'grad_kv_ada_w': 'grad_w', 'grad_kv_ada_b': 'grad_w', 'grad_kv_w': 'grad_w', 'grad_kv_b_f': 'grad_w', 'grad_k_norm_g': 'grad_w', 'grad_b_w_q': 'grad_w', 'grad_q_norm_g': 'grad_w', 'grad_b_w_out': 'grad_w', 'grad_ffn_w_up': 'grad_w', 'grad_ffn_conv_w': 'grad_w', 'grad_ffn_conv_b': 'grad_w', 'grad_ffn_w_down': 'grad_w', 'delta_ada_w': 'delta_w', 'delta_ada_b': 'delta_w', 'delta_a_w_in': 'delta_w', 'delta_a_lb_logits': 'delta_w', 'delta_a_norm_g': 'delta_w', 'delta_a_w_out': 'delta_w', 'delta_kv_ada_w': 'delta_w', 'delta_kv_ada_b': 'delta_w', 'delta_kv_w': 'delta_w', 'delta_kv_b_f': 'delta_w', 'delta_k_norm_g': 'delta_w', 'delta_b_w_q': 'delta_w', 'delta_q_norm_g': 'delta_w', 'delta_b_w_out': 'delta_w', 'delta_ffn_w_up': 'delta_w', 'delta_ffn_conv_w': 'delta_w', 'delta_ffn_conv_b': 'delta_w', 'delta_ffn_w_down': 'delta_w', 'new_m_ada_w': 'new_m', 'new_m_ada_b': 'new_m', 'new_m_a_w_in': 'new_m', 'new_m_a_lb_logits': 'new_m', 'new_m_a_norm_g': 'new_m', 'new_m_a_w_out': 'new_m', 'new_m_kv_ada_w': 'new_m', 'new_m_kv_ada_b': 'new_m', 'new_m_kv_w': 'new_m', 'new_m_kv_b_f': 'new_m', 'new_m_k_norm_g': 'new_m', 'new_m_b_w_q': 'new_m', 'new_m_q_norm_g': 'new_m', 'new_m_b_w_out': 'new_m', 'new_m_ffn_w_up': 'new_m', 'new_m_ffn_conv_w': 'new_m', 'new_m_ffn_conv_b': 'new_m', 'new_m_ffn_w_down': 'new_m', 'new_v_ada_w': 'new_v', 'new_v_ada_b': 'new_v', 'new_v_a_w_in': 'new_v', 'new_v_a_lb_logits': 'new_v', 'new_v_a_norm_g': 'new_v', 'new_v_a_w_out': 'new_v', 'new_v_kv_ada_w': 'new_v', 'new_v_kv_ada_b': 'new_v', 'new_v_kv_w': 'new_v', 'new_v_kv_b_f': 'new_v', 'new_v_k_norm_g': 'new_v', 'new_v_b_w_q': 'new_v', 'new_v_q_norm_g': 'new_v', 'new_v_b_w_out': 'new_v', 'new_v_ffn_w_up': 'new_v', 'new_v_ffn_conv_w': 'new_v', 'new_v_ffn_conv_b': 'new_v', 'new_v_ffn_w_down': 'new_v'}


def _forward(args):
    return _fwd_reference(*[args[k] for k in FWD_PARAMS])


def _output_shape():
    out = _jax.eval_shape(lambda: _forward(_fwd_setup_inputs(0)))
    return out.shape, out.dtype

N_MICROBATCH = 1
ADAM_LR = 0.001
ADAM_B1 = 0.9
ADAM_B2 = 0.999
ADAM_EPS = 1e-08
ADAM_WD = 0.01
ADAM_STEP = 10
PER_EXAMPLE_BATCH_AXIS = {'x': 0, 'c': 0, 'loss_target': 0}
SHARED_INPUTS = []
_WEIGHT_DTYPES = {'ada_w': _jnp.float32, 'ada_b': _jnp.float32, 'a_w_in': _jnp.float32, 'a_lb_logits': _jnp.float32, 'a_norm_g': _jnp.float32, 'a_w_out': _jnp.float32, 'kv_ada_w': _jnp.float32, 'kv_ada_b': _jnp.float32, 'kv_w': _jnp.float32, 'kv_b_f': _jnp.float32, 'k_norm_g': _jnp.float32, 'b_w_q': _jnp.float32, 'q_norm_g': _jnp.float32, 'b_w_out': _jnp.float32, 'ffn_w_up': _jnp.float32, 'ffn_conv_w': _jnp.float32, 'ffn_conv_b': _jnp.float32, 'ffn_w_down': _jnp.float32}
MOMENT_SCALE = {'ada_w': 9.403169e-01, 'ada_b': 2.069659e+00, 'a_w_in': 1.127818e-01, 'a_lb_logits': 6.763894e-03, 'a_norm_g': 1.063938e+01, 'a_w_out': 1.157708e-01, 'kv_ada_w': 1.774269e-01, 'kv_ada_b': 4.087711e-01, 'kv_w': 7.425871e-02, 'kv_b_f': 3.478652e+00, 'k_norm_g': 2.906478e-01, 'b_w_q': 1.281269e-02, 'q_norm_g': 2.925523e-01, 'b_w_out': 7.507123e-02, 'ffn_w_up': 1.036974e-01, 'ffn_conv_w': 5.014719e-01, 'ffn_conv_b': 4.116130e-01, 'ffn_w_down': 8.260221e-02}


def _to_microbatches(a, axis):
    t = _jnp.moveaxis(a, axis, 0)
    t = t.reshape((N_MICROBATCH, t.shape[0] // N_MICROBATCH) + t.shape[1:])
    return _jnp.moveaxis(t, 1, axis + 1)


def setup_inputs(seed: int = 0) -> dict:
    inp = _fwd_setup_inputs(seed)
    key = _jax.random.fold_in(_jax.random.key(seed), 7919)
    shape, _ = _output_shape()
    out = dict(inp)
    out["loss_target"] = _jax.random.normal(_jax.random.fold_in(key, 0), shape, _jnp.float32)
    for i, name in enumerate(TWIN_WEIGHTS):
        w = inp[name].astype(_jnp.float32)
        if MOMENT_SCALE is None:
            s = _jnp.sqrt(_jnp.mean(_jnp.square(w)) + 1e-30)
        else:
            s = MOMENT_SCALE[name]
        km, kv = _jax.random.split(_jax.random.fold_in(key, i + 1))
        out[name] = w
        out["m_" + name] = s * _jax.random.normal(km, w.shape, _jnp.float32)
        out["v_" + name] = (s * s) * _jax.random.uniform(kv, w.shape, _jnp.float32, 0.5, 1.5)
    if N_MICROBATCH > 1:
        for name, axis in PER_EXAMPLE_BATCH_AXIS.items():
            out[name] = _to_microbatches(out[name], axis)
    return {'x': out['x'], 'c': out['c'], 'ada_w': out['ada_w'], 'ada_b': out['ada_b'], 'a_w_in': out['a_w_in'], 'a_lb_logits': out['a_lb_logits'], 'a_norm_g': out['a_norm_g'], 'a_w_out': out['a_w_out'], 'kv_ada_w': out['kv_ada_w'], 'kv_ada_b': out['kv_ada_b'], 'kv_w': out['kv_w'], 'kv_b_f': out['kv_b_f'], 'k_norm_g': out['k_norm_g'], 'b_w_q': out['b_w_q'], 'q_norm_g': out['q_norm_g'], 'b_w_out': out['b_w_out'], 'ffn_w_up': out['ffn_w_up'], 'ffn_conv_w': out['ffn_conv_w'], 'ffn_conv_b': out['ffn_conv_b'], 'ffn_w_down': out['ffn_w_down'], 'loss_target': out['loss_target'], 'm_ada_w': out['m_ada_w'], 'm_ada_b': out['m_ada_b'], 'm_a_w_in': out['m_a_w_in'], 'm_a_lb_logits': out['m_a_lb_logits'], 'm_a_norm_g': out['m_a_norm_g'], 'm_a_w_out': out['m_a_w_out'], 'm_kv_ada_w': out['m_kv_ada_w'], 'm_kv_ada_b': out['m_kv_ada_b'], 'm_kv_w': out['m_kv_w'], 'm_kv_b_f': out['m_kv_b_f'], 'm_k_norm_g': out['m_k_norm_g'], 'm_b_w_q': out['m_b_w_q'], 'm_q_norm_g': out['m_q_norm_g'], 'm_b_w_out': out['m_b_w_out'], 'm_ffn_w_up': out['m_ffn_w_up'], 'm_ffn_conv_w': out['m_ffn_conv_w'], 'm_ffn_conv_b': out['m_ffn_conv_b'], 'm_ffn_w_down': out['m_ffn_w_down'], 'v_ada_w': out['v_ada_w'], 'v_ada_b': out['v_ada_b'], 'v_a_w_in': out['v_a_w_in'], 'v_a_lb_logits': out['v_a_lb_logits'], 'v_a_norm_g': out['v_a_norm_g'], 'v_a_w_out': out['v_a_w_out'], 'v_kv_ada_w': out['v_kv_ada_w'], 'v_kv_ada_b': out['v_kv_ada_b'], 'v_kv_w': out['v_kv_w'], 'v_kv_b_f': out['v_kv_b_f'], 'v_k_norm_g': out['v_k_norm_g'], 'v_b_w_q': out['v_b_w_q'], 'v_q_norm_g': out['v_q_norm_g'], 'v_b_w_out': out['v_b_w_out'], 'v_ffn_w_up': out['v_ffn_w_up'], 'v_ffn_conv_w': out['v_ffn_conv_w'], 'v_ffn_conv_b': out['v_ffn_conv_b'], 'v_ffn_w_down': out['v_ffn_w_down']}


def _loss(weights, diff, rest, loss_target):
    with _jax.named_scope("forward"):
        args = {**rest, TWIN_DIFF_INPUT: diff, **{k: w.astype(_WEIGHT_DTYPES[k]) for k, w in weights.items()}}
        y = _forward(args)
    with _jax.named_scope("loss_head"):
        err = _jnp.square(y.astype(_jnp.float32) - loss_target)
        return 0.5 * _jnp.sum(_jnp.mean(err, axis=-1)) if err.ndim else 0.5 * err


def _adamw(w, g, m, v):
    m = ADAM_B1 * m + (1.0 - ADAM_B1) * g
    v = ADAM_B2 * v + (1.0 - ADAM_B2) * _jnp.square(g)
    m_hat = m / (1.0 - ADAM_B1 ** ADAM_STEP)
    v_hat = v / (1.0 - ADAM_B2 ** ADAM_STEP)
    delta = -ADAM_LR * (m_hat / (_jnp.sqrt(v_hat) + ADAM_EPS) + ADAM_WD * w)
    return delta, m, v


def reference(x, c, ada_w, ada_b, a_w_in, a_lb_logits, a_norm_g, a_w_out, kv_ada_w, kv_ada_b, kv_w, kv_b_f, k_norm_g, b_w_q, q_norm_g, b_w_out, ffn_w_up, ffn_conv_w, ffn_conv_b, ffn_w_down, loss_target, m_ada_w, m_ada_b, m_a_w_in, m_a_lb_logits, m_a_norm_g, m_a_w_out, m_kv_ada_w, m_kv_ada_b, m_kv_w, m_kv_b_f, m_k_norm_g, m_b_w_q, m_q_norm_g, m_b_w_out, m_ffn_w_up, m_ffn_conv_w, m_ffn_conv_b, m_ffn_w_down, v_ada_w, v_ada_b, v_a_w_in, v_a_lb_logits, v_a_norm_g, v_a_w_out, v_kv_ada_w, v_kv_ada_b, v_kv_w, v_kv_b_f, v_k_norm_g, v_b_w_q, v_q_norm_g, v_b_w_out, v_ffn_w_up, v_ffn_conv_w, v_ffn_conv_b, v_ffn_w_down):
    given = dict(x=x, c=c, ada_w=ada_w, ada_b=ada_b, a_w_in=a_w_in, a_lb_logits=a_lb_logits, a_norm_g=a_norm_g, a_w_out=a_w_out, kv_ada_w=kv_ada_w, kv_ada_b=kv_ada_b, kv_w=kv_w, kv_b_f=kv_b_f, k_norm_g=k_norm_g, b_w_q=b_w_q, q_norm_g=q_norm_g, b_w_out=b_w_out, ffn_w_up=ffn_w_up, ffn_conv_w=ffn_conv_w, ffn_conv_b=ffn_conv_b, ffn_w_down=ffn_w_down, loss_target=loss_target, m_ada_w=m_ada_w, m_ada_b=m_ada_b, m_a_w_in=m_a_w_in, m_a_lb_logits=m_a_lb_logits, m_a_norm_g=m_a_norm_g, m_a_w_out=m_a_w_out, m_kv_ada_w=m_kv_ada_w, m_kv_ada_b=m_kv_ada_b, m_kv_w=m_kv_w, m_kv_b_f=m_kv_b_f, m_k_norm_g=m_k_norm_g, m_b_w_q=m_b_w_q, m_q_norm_g=m_q_norm_g, m_b_w_out=m_b_w_out, m_ffn_w_up=m_ffn_w_up, m_ffn_conv_w=m_ffn_conv_w, m_ffn_conv_b=m_ffn_conv_b, m_ffn_w_down=m_ffn_w_down, v_ada_w=v_ada_w, v_ada_b=v_ada_b, v_a_w_in=v_a_w_in, v_a_lb_logits=v_a_lb_logits, v_a_norm_g=v_a_norm_g, v_a_w_out=v_a_w_out, v_kv_ada_w=v_kv_ada_w, v_kv_ada_b=v_kv_ada_b, v_kv_w=v_kv_w, v_kv_b_f=v_kv_b_f, v_k_norm_g=v_k_norm_g, v_b_w_q=v_b_w_q, v_q_norm_g=v_q_norm_g, v_b_w_out=v_b_w_out, v_ffn_w_up=v_ffn_w_up, v_ffn_conv_w=v_ffn_conv_w, v_ffn_conv_b=v_ffn_conv_b, v_ffn_w_down=v_ffn_w_down)
    weights = {n: given[n] for n in TWIN_WEIGHTS}
    shared = {n: given[n] for n in SHARED_INPUTS}
    per_example = {n: given[n] for n in ['x', 'c']}
    grad_fn = _jax.value_and_grad(_loss, argnums=(0, 1))

    def one_microbatch(ex, loss_target):
        ex = dict(ex)
        diff = ex.pop(TWIN_DIFF_INPUT)
        return grad_fn(weights, diff, {**shared, **ex}, loss_target)

    if N_MICROBATCH == 1:
        loss, (grad_w, grad_x) = one_microbatch(per_example, given["loss_target"])
    else:
        def body(carry, xs):
            loss_sum, grad_sum = carry
            l_k, (gw_k, gx_k) = one_microbatch(xs[0], xs[1])
            with _jax.named_scope("update"):
                return (loss_sum + l_k, _jax.tree.map(_jnp.add, grad_sum, gw_k)), gx_k

        init = (_jnp.zeros((), _jnp.float32), _jax.tree.map(_jnp.zeros_like, weights))
        (loss, grad_w), grad_x = _jax.lax.scan(body, init, (per_example, given["loss_target"]))
    with _jax.named_scope("update"):
        delta_w, new_m, new_v = {}, {}, {}
        for n in TWIN_WEIGHTS:
            delta_w[n], new_m[n], new_v[n] = _adamw(weights[n], grad_w[n], given["m_" + n], given["v_" + n])
    return (loss, grad_x, *[grad_w[n] for n in TWIN_WEIGHTS], *[delta_w[n] for n in TWIN_WEIGHTS],
            *[new_m[n] for n in TWIN_WEIGHTS], *[new_v[n] for n in TWIN_WEIGHTS])
```

```python
import jax
import jax.numpy as jnp
from jax import lax
from jax.experimental import pallas as pl
from jax.experimental.pallas import tpu as pltpu

F32 = jnp.float32
BF16 = jnp.bfloat16

D_MODEL = 1024
HEADS = 8
HEAD_DIM = 128
A_CHUNK = 64
D_FF = 2816
CONV_W = 3
EPS = 1e-6
NEG_INF = -1e30
N_CHIPS = 4
N_DEV = 8

ADAM_LR = 0.001
ADAM_B1 = 0.9
ADAM_B2 = 0.999
ADAM_EPS = 1e-08
ADAM_WD = 0.01
ADAM_STEP = 10

SUBLANES = 8
BF16_ROWS = 16
LANES = 128
HALO = BF16_ROWS
ROW_TILE = 512
FFN_COLS = 1408
HGRN_ROWS = 256
ATT_TILE = 512
MESH = pl.DeviceIdType.MESH


def _sig(x):
    return jax.nn.sigmoid(x)


def _dot(a, b):
    return jnp.dot(a, b, preferred_element_type=F32)


def _dot_nt(a, b):
    return lax.dot_general(a, b, (((1,), (1,)), ((), ())), preferred_element_type=F32)


def _dot_tn(a, b):
    return lax.dot_general(a, b, (((0,), (0,)), ((), ())), preferred_element_type=F32)


def _split2(x):
    hi = x.astype(BF16)
    lo = (x - hi.astype(F32)).astype(BF16)
    return hi, lo


def _dot_f32(a, b):
    ah, al = _split2(a)
    bh, bl = _split2(b)
    return _dot(ah, bh) + _dot(ah, bl) + _dot(al, bh)


def _tri_dot(tri, x):
    hi = x.astype(BF16)
    r = x - hi.astype(F32)
    mid = r.astype(BF16)
    lo = (r - mid.astype(F32)).astype(BF16)
    return _dot(tri, hi) + _dot(tri, mid) + _dot(tri, lo)


def _tri(n, upper=False):
    r = lax.broadcasted_iota(jnp.int32, (n, n), 0)
    c = lax.broadcasted_iota(jnp.int32, (n, n), 1)
    keep = (c >= r) if upper else (c <= r)
    return jnp.where(keep, 1.0, 0.0).astype(BF16)


def _colsum8(v):
    rows, n = v.shape
    return v.reshape(rows // SUBLANES, SUBLANES, n).sum(axis=0)


def _full(shape):
    nd = len(shape)
    return pl.BlockSpec(shape, lambda *_: (0,) * nd)


def _tile(n, want):
    t = min(n, want)
    assert n % t == 0, (n, t)
    return t


def _mm_nn(a, w, groups, out_dtype, name):
    m_rows, k = a.shape
    p_n, _, n = w.shape
    per = p_n // groups
    tm = _tile(m_rows, ROW_TILE)

    def body(a_ref, w_ref, o_ref):
        av = a_ref[...]
        for p in range(p_n):
            o_ref[p // per, :, (p % per) * n:(p % per + 1) * n] = _dot(av, w_ref[p]).astype(out_dtype)

    return pl.pallas_call(
        body, name=name, grid=(m_rows // tm,),
        in_specs=[pl.BlockSpec((tm, k), lambda i: (i, 0)), _full((p_n, k, n))],
        out_specs=pl.BlockSpec((groups, tm, per * n), lambda i: (0, i, 0)),
        out_shape=jax.ShapeDtypeStruct((groups, m_rows, per * n), out_dtype),
    )(a, w)


def _mm_nt(d, w, out_dtype, name, add=None):
    g_n, m_rows, _ = d.shape
    p_n, k, n = w.shape
    per = p_n // g_n
    tm = _tile(m_rows, ROW_TILE)

    def body(*refs):
        d_ref, w_ref = refs[0], refs[1]
        o_ref = refs[-1]
        acc = refs[2][...] if add is not None else None
        for p in range(p_n):
            t = _dot_nt(d_ref[p // per, :, (p % per) * n:(p % per + 1) * n], w_ref[p])
            acc = t if acc is None else acc + t
        o_ref[...] = acc.astype(out_dtype)

    ins = [d, w] + ([add] if add is not None else [])
    specs = [pl.BlockSpec((g_n, tm, per * n), lambda i: (0, i, 0)), _full((p_n, k, n))]
    if add is not None:
        specs.append(pl.BlockSpec((tm, k), lambda i: (i, 0)))
    return pl.pallas_call(
        body, name=name, grid=(m_rows // tm,), in_specs=specs,
        out_specs=pl.BlockSpec((tm, k), lambda i: (i, 0)),
        out_shape=jax.ShapeDtypeStruct((m_rows, k), out_dtype),
    )(*ins)


def _mm_tn(a, d, p_n, name):
    m_rows, k = a.shape
    g_n, _, w_cols = d.shape
    per = p_n // g_n
    n = w_cols // per
    tm = _tile(m_rows, ROW_TILE)
    steps = m_rows // tm

    def body(a_ref, d_ref, o_ref, acc):
        m = pl.program_id(1)

        @pl.when(m == 0)
        def _():
            acc[...] = jnp.zeros_like(acc)

        acc[...] += _dot_tn(a_ref[...], d_ref[...])

        @pl.when(m == steps - 1)
        def _():
            o_ref[...] = acc[...].astype(BF16)

    return pl.pallas_call(
        body, name=name, grid=(p_n, steps),
        in_specs=[pl.BlockSpec((tm, k), lambda p, m: (m, 0)),
                  pl.BlockSpec((None, tm, n), lambda p, m: (p // per, m, p % per))],
        out_specs=pl.BlockSpec((None, k, n), lambda p, m: (p, 0, 0)),
        out_shape=jax.ShapeDtypeStruct((p_n, k, n), BF16),
        scratch_shapes=[pltpu.VMEM((k, n), F32)],
    )(a, d)


def _premix(x, shift, scale, name, branch=None, gate=None):
    s, dm = x.shape
    tm = _tile(s, ROW_TILE)
    with_branch = branch is not None

    def body(*refs):
        x_ref, sh_ref, sc_ref = refs[:3]
        xv = x_ref[...]
        if with_branch:
            xv = xv + refs[4][...] * refs[3][...]
            refs[-2][...] = xv
        inv = lax.rsqrt(jnp.mean(xv * xv, axis=-1, keepdims=True) + EPS)
        refs[-1][...] = (xv * inv * (1.0 + sc_ref[...]) + sh_ref[...]).astype(BF16)

    row = pl.BlockSpec((tm, dm), lambda i: (i, 0))
    vec = _full((1, dm))
    ins, specs = [x, shift, scale], [row, vec, vec]
    out_shape, out_specs = [jax.ShapeDtypeStruct((s, dm), BF16)], [row]
    if with_branch:
        ins += [branch, gate]
        specs += [row, vec]
        out_shape.insert(0, jax.ShapeDtypeStruct((s, dm), F32))
        out_specs.insert(0, row)
    outs = pl.pallas_call(body, name=name, grid=(s // tm,), in_specs=specs, out_specs=out_specs,
                          out_shape=out_shape)(*ins)
    return tuple(outs) if with_branch else outs[0]


def _premix_bwd(x, dh, scale, dres, name):
    s, dm = x.shape
    tm = _tile(s, ROW_TILE)

    def body(x_ref, dh_ref, sc_ref, dres_ref, dx_ref, dsh_ref, dsc_ref):
        i = pl.program_id(0)

        @pl.when(i == 0)
        def _():
            dsh_ref[...] = jnp.zeros_like(dsh_ref)
            dsc_ref[...] = jnp.zeros_like(dsc_ref)

        xv = x_ref[...]
        dhv = dh_ref[...]
        inv = lax.rsqrt(jnp.mean(xv * xv, axis=-1, keepdims=True) + EPS)
        r = xv * inv
        dr = dhv * (1.0 + sc_ref[...])
        dx_ref[...] = dres_ref[...] + inv * (dr - r * jnp.mean(dr * r, axis=-1, keepdims=True))
        dsh_ref[...] += _colsum8(dhv)
        dsc_ref[...] += _colsum8(dhv * r)

    row = pl.BlockSpec((tm, dm), lambda i: (i, 0))
    acc = _full((SUBLANES, dm))
    return pl.pallas_call(
        body, name=name, grid=(s // tm,), in_specs=[row, row, _full((1, dm)), row],
        out_specs=[row, acc, acc],
        out_shape=[jax.ShapeDtypeStruct((s, dm), F32), jax.ShapeDtypeStruct((SUBLANES, dm), F32),
                   jax.ShapeDtypeStruct((SUBLANES, dm), F32)],
    )(x, dh, scale, dres)


def _branch_bwd(dx, y, gate, name):
    s, dm = dx.shape
    tm = _tile(s, ROW_TILE)

    def body(dx_ref, y_ref, g_ref, dy_ref, dg_ref):
        @pl.when(pl.program_id(0) == 0)
        def _():
            dg_ref[...] = jnp.zeros_like(dg_ref)

        dxv = dx_ref[...]
        dy_ref[0] = (dxv * g_ref[...]).astype(BF16)
        dg_ref[...] += _colsum8(dxv * y_ref[...])

    row = pl.BlockSpec((tm, dm), lambda i: (i, 0))
    return pl.pallas_call(
        body, name=name, grid=(s // tm,), in_specs=[row, row, _full((1, dm))],
        out_specs=[pl.BlockSpec((1, tm, dm), lambda i: (0, i, 0)), _full((SUBLANES, dm))],
        out_shape=[jax.ShapeDtypeStruct((1, s, dm), BF16), jax.ShapeDtypeStruct((SUBLANES, dm), F32)],
    )(dx, y, gate)


def _loss_head(x, branch, gate, target, name):
    s, dm = x.shape
    tm = _tile(s, ROW_TILE)

    def body(x_ref, b_ref, g_ref, t_ref, sq_ref, dy_ref):
        @pl.when(pl.program_id(0) == 0)
        def _():
            sq_ref[...] = jnp.zeros_like(sq_ref)

        err = x_ref[...] + g_ref[...] * b_ref[...] - t_ref[...]
        sq_ref[...] += _colsum8(err * err)
        dy_ref[...] = err * (1.0 / dm)

    row = pl.BlockSpec((tm, dm), lambda i: (i, 0))
    return pl.pallas_call(
        body, name=name, grid=(s // tm,), in_specs=[row, row, _full((1, dm)), row],
        out_specs=[_full((SUBLANES, dm)), row],
        out_shape=[jax.ShapeDtypeStruct((SUBLANES, dm), F32), jax.ShapeDtypeStruct((s, dm), F32)],
    )(x, branch, gate, target)


def _conv_taps(e, w, b):
    return w[2:3] * e + w[1:2] * pltpu.roll(e, 1, 0) + w[0:1] * pltpu.roll(e, 2, 0) + b


def _ffn_specs(s, tm, cb):
    hb = tm // HALO
    last = s // HALO - 1
    main = pl.BlockSpec((2, tm, cb), lambda j, i: (0, i, j))
    prev = pl.BlockSpec((2, HALO, cb), lambda j, i: (0, jnp.maximum(i * hb - 1, 0), j))
    nxt = pl.BlockSpec((2, HALO, cb), lambda j, i: (0, jnp.minimum((i + 1) * hb, last), j))
    wspec = pl.BlockSpec((2, CONV_W, cb), lambda j, i: (0, 0, j))
    bspec = pl.BlockSpec((2, 1, cb), lambda j, i: (0, 0, j))
    return main, prev, nxt, wspec, bspec


def _convglu_fwd(u, w, b, name):
    _, s, f = u.shape
    tm = _tile(s, 256)
    cb = _tile(f, FFN_COLS)
    main, prev, _, wspec, bspec = _ffn_specs(s, tm, cb)

    def body(u_ref, up_ref, w_ref, b_ref, a_ref):
        first = jnp.where(pl.program_id(1) > 0, 1.0, 0.0)

        def conv(g):
            e = jnp.concatenate([up_ref[g].astype(F32) * first, u_ref[g].astype(F32)], axis=0)
            return _conv_taps(e, w_ref[g], b_ref[g])[HALO:]

        gate = conv(0)
        a_ref[...] = (gate * _sig(gate) * conv(1)).astype(BF16)

    return pl.pallas_call(
        body, name=name, grid=(f // cb, s // tm), in_specs=[main, prev, wspec, bspec],
        out_specs=pl.BlockSpec((tm, cb), lambda j, i: (i, j)),
        out_shape=jax.ShapeDtypeStruct((s, f), BF16),
    )(u, u, w, b)


def _convglu_bwd(u, da, w, b, name):
    _, s, f = u.shape
    tm = _tile(s, 256)
    cb = _tile(f, FFN_COLS)
    steps = s // tm
    n_ext = tm + 2 * HALO
    main, prev, nxt, wspec, bspec = _ffn_specs(s, tm, cb)
    hb = tm // HALO
    last = s // HALO - 1
    da_main = pl.BlockSpec((tm, cb), lambda j, i: (i, j))
    da_next = pl.BlockSpec((HALO, cb), lambda j, i: (jnp.minimum((i + 1) * hb, last), j))

    def body(u_ref, up_ref, un_ref, da_ref, dan_ref, w_ref, b_ref, du_ref, acc_ref):
        i = pl.program_id(1)
        first = jnp.where(i > 0, 1.0, 0.0)
        notlast = jnp.where(i < steps - 1, 1.0, 0.0)

        @pl.when(i == 0)
        def _():
            acc_ref[...] = jnp.zeros_like(acc_ref)

        def ext(g):
            return jnp.concatenate([up_ref[g].astype(F32) * first, u_ref[g].astype(F32), un_ref[g].astype(F32)], axis=0)

        ug, uv = ext(0), ext(1)
        gate = _conv_taps(ug, w_ref[0], b_ref[0])
        val = _conv_taps(uv, w_ref[1], b_ref[1])
        da_e = jnp.concatenate([jnp.zeros((HALO, cb), F32), da_ref[...].astype(F32),
                                dan_ref[...].astype(F32) * notlast], axis=0)
        sg = _sig(gate)
        d_val = da_e * gate * sg
        d_gate = da_e * val * (sg * (1.0 + gate * (1.0 - sg)))

        def finish(g, d, e):
            wv = w_ref[g]
            du = wv[2:3] * d + wv[1:2] * pltpu.roll(d, n_ext - 1, 0) + wv[0:1] * pltpu.roll(d, n_ext - 2, 0)
            du_ref[g] = du[HALO:HALO + tm].astype(BF16)
            dm = d[HALO:HALO + tm]
            acc_ref[g, 2] += _colsum8(dm * e[HALO:HALO + tm])
            acc_ref[g, 1] += _colsum8(dm * pltpu.roll(e, 1, 0)[HALO:HALO + tm])
            acc_ref[g, 0] += _colsum8(dm * pltpu.roll(e, 2, 0)[HALO:HALO + tm])
            acc_ref[g, 3] += _colsum8(dm)

        finish(0, d_gate, ug)
        finish(1, d_val, uv)

    return pl.pallas_call(
        body, name=name, grid=(f // cb, steps),
        in_specs=[main, prev, nxt, da_main, da_next, wspec, bspec],
        out_specs=[main, pl.BlockSpec((2, 4, SUBLANES, cb), lambda j, i: (0, 0, 0, j))],
        out_shape=[jax.ShapeDtypeStruct((2, s, f), BF16), jax.ShapeDtypeStruct((2, 4, SUBLANES, f), F32)],
    )(u, u, u, da, da, w, b)


def _hgrn_gates(q_raw, f_raw, lb, tri):
    sf = _sig(f_raw)
    fg = lb + (1.0 - lb) * sf
    b = _tri_dot(tri, jnp.log(fg))
    return q_raw * _sig(q_raw), 1.0 - fg, b, fg, sf


def _hgrn_fwd(proj, lb, norm_g, name):
    s = proj.shape[0]
    tb = _tile(s, HGRN_ROWS)
    n_c = tb // A_CHUNK
    half = A_CHUNK // 2

    def body(q_ref, f_ref, v_ref, g_ref, lb_ref, ng_ref, o_ref, yp_ref, st_ref, state):
        @pl.when(pl.program_id(0) == 0)
        def _():
            state[...] = jnp.zeros_like(state)

        tri = _tri(A_CHUNK)
        causal = lax.broadcasted_iota(jnp.int32, (A_CHUNK, A_CHUNK), 1) <= lax.broadcasted_iota(
            jnp.int32, (A_CHUNK, A_CHUNK), 0)

        def chunk(ci, carry):
            rows = pl.ds(pl.multiple_of(ci * A_CHUNK, A_CHUNK), A_CHUNK)
            for h in range(HEADS):
                cs = slice(h * HEAD_DIM, (h + 1) * HEAD_DIM)
                qs, k, b, _, _ = _hgrn_gates(q_ref[rows, cs], f_ref[rows, cs], lb_ref[:, cs], tri)
                b_mid, b_last = b[half:half + 1], b[A_CHUNK - 1:A_CHUNK]
                vb = v_ref[rows, cs].astype(BF16)
                scores = _dot_nt((qs * jnp.exp(b - b_mid)).astype(BF16), (k * jnp.exp(b_mid - b)).astype(BF16))
                scores = jnp.where(causal, scores, 0.0)
                st = state[h]
                st_ref[ci, h] = st
                o = _dot(scores.astype(BF16), vb) + _dot_nt((qs * jnp.exp(b)).astype(BF16), st.astype(BF16))
                state[h] = st * jnp.exp(b_last) + _dot_tn(vb, (k * jnp.exp(b_last - b)).astype(BF16))
                o_ref[rows, cs] = o
                inv = lax.rsqrt(jnp.mean(o * o, axis=-1, keepdims=True) + EPS)
                g_raw = g_ref[rows, cs]
                yp_ref[rows, cs] = (o * inv * ng_ref[:, cs] * (g_raw * _sig(g_raw))).astype(BF16)
            return carry

        lax.fori_loop(0, n_c, chunk, 0)

    col = lambda j: pl.BlockSpec((tb, D_MODEL), lambda i: (i, j))
    vec = _full((1, D_MODEL))
    return pl.pallas_call(
        body, name=name, grid=(s // tb,), in_specs=[col(0), col(1), col(2), col(3), vec, vec],
        out_specs=[col(0), col(0), pl.BlockSpec((n_c, HEADS, HEAD_DIM, HEAD_DIM), lambda i: (i, 0, 0, 0))],
        out_shape=[jax.ShapeDtypeStruct((s, D_MODEL), F32), jax.ShapeDtypeStruct((s, D_MODEL), BF16),
                   jax.ShapeDtypeStruct((s // A_CHUNK, HEADS, HEAD_DIM, HEAD_DIM), F32)],
        scratch_shapes=[pltpu.VMEM((HEADS, HEAD_DIM, HEAD_DIM), F32)],
    )(proj, proj, proj, proj, lb, norm_g)


def _hgrn_bwd(proj, lb, norm_g, o, states, dyp, name):
    s = proj.shape[0]
    tb = _tile(s, HGRN_ROWS)
    n_c = tb // A_CHUNK
    n_b = s // tb
    half = A_CHUNK // 2

    def body(q_ref, f_ref, v_ref, g_ref, lb_ref, ng_ref, o_ref, st_ref, dyp_ref, dp_ref, dlb_ref, dng_ref, dstate):
        @pl.when(pl.program_id(0) == 0)
        def _():
            dstate[...] = jnp.zeros_like(dstate)
            dlb_ref[...] = jnp.zeros_like(dlb_ref)
            dng_ref[...] = jnp.zeros_like(dng_ref)

        tri = _tri(A_CHUNK)
        tri_up = _tri(A_CHUNK, upper=True)
        row_id = lax.broadcasted_iota(jnp.int32, (A_CHUNK, HEAD_DIM), 0)
        causal = lax.broadcasted_iota(jnp.int32, (A_CHUNK, A_CHUNK), 1) <= lax.broadcasted_iota(
            jnp.int32, (A_CHUNK, A_CHUNK), 0)

        def chunk(cj, carry):
            ci = n_c - 1 - cj
            rows = pl.ds(pl.multiple_of(ci * A_CHUNK, A_CHUNK), A_CHUNK)
            for h in range(HEADS):
                cs = slice(h * HEAD_DIM, (h + 1) * HEAD_DIM)
                q_raw, lbh = q_ref[rows, cs], lb_ref[:, cs]
                qs, k, b, fg, sf = _hgrn_gates(q_raw, f_ref[rows, cs], lbh, tri)
                b_mid, b_last = b[half:half + 1], b[A_CHUNK - 1:A_CHUNK]
                e_qi, e_ki, e_q, e_ks = jnp.exp(b - b_mid), jnp.exp(b_mid - b), jnp.exp(b), jnp.exp(b_last - b)
                q_i, k_i, q_e, k_s = qs * e_qi, k * e_ki, qs * e_q, k * e_ks
                vb = v_ref[rows, cs].astype(BF16)
                scores = jnp.where(causal, _dot_nt(q_i.astype(BF16), k_i.astype(BF16)), 0.0)
                ov, g_raw, dy, ng = o_ref[rows, cs], g_ref[rows, cs], dyp_ref[rows, cs], ng_ref[:, cs]
                inv = lax.rsqrt(jnp.mean(ov * ov, axis=-1, keepdims=True) + EPS)
                nrm = ov * inv
                sg = _sig(g_raw)
                gs = g_raw * sg
                dn = dy * ng * gs
                dng_ref[0:1, cs] += jnp.sum(dy * nrm * gs, axis=0, keepdims=True)
                dg_raw = dy * nrm * ng * (sg * (1.0 + g_raw * (1.0 - sg)))
                do = (inv * (dn - nrm * jnp.mean(dn * nrm, axis=-1, keepdims=True))).astype(BF16)
                st_prev = st_ref[ci, h]
                dst = dstate[h]
                dstb = dst.astype(BF16)
                d_scores = jnp.where(causal, _dot_nt(do, vb), 0.0).astype(BF16)
                dv = _dot_tn(scores.astype(BF16), do) + _dot_nt(k_s.astype(BF16), dstb)
                dq_i = _dot(d_scores, k_i.astype(BF16))
                dk_i = _dot_tn(d_scores, q_i.astype(BF16))
                dq_e = _dot(do, st_prev.astype(BF16))
                dk_s = _dot(vb, dstb)
                d_decay = jnp.sum(st_prev * dst, axis=0, keepdims=True)
                dstate[h] = dst * jnp.exp(b_last) + _dot_tn(do, q_e.astype(BF16))
                dq = dq_i * e_qi + dq_e * e_q
                dk = dk_i * e_ki + dk_s * e_ks
                t_qi, t_ki, t_ks = dq_i * q_i, dk_i * k_i, dk_s * k_s
                db = t_qi - t_ki + dq_e * q_e - t_ks
                db_mid = jnp.sum(t_ki - t_qi, axis=0, keepdims=True)
                db_last = jnp.sum(t_ks, axis=0, keepdims=True) + d_decay * jnp.exp(b_last)
                db = db + jnp.where(row_id == half, db_mid, 0.0) + jnp.where(row_id == A_CHUNK - 1, db_last, 0.0)
                dfg = _tri_dot(tri_up, db) / fg - dk
                dlb_ref[0:1, cs] += jnp.sum(dfg * (1.0 - sf), axis=0, keepdims=True)
                sq = _sig(q_raw)
                dp_ref[0, rows, cs] = (dq * (sq * (1.0 + q_raw * (1.0 - sq)))).astype(BF16)
                dp_ref[1, rows, cs] = (dfg * (1.0 - lbh) * sf * (1.0 - sf)).astype(BF16)
                dp_ref[2, rows, cs] = dv.astype(BF16)
                dp_ref[3, rows, cs] = dg_raw.astype(BF16)
            return carry

        lax.fori_loop(0, n_c, chunk, 0)

    col = lambda j: pl.BlockSpec((tb, D_MODEL), lambda i: (n_b - 1 - i, j))
    vec = _full((1, D_MODEL))
    acc = _full((SUBLANES, D_MODEL))
    return pl.pallas_call(
        body, name=name, grid=(n_b,),
        in_specs=[col(0), col(1), col(2), col(3), vec, vec, col(0),
                  pl.BlockSpec((n_c, HEADS, HEAD_DIM, HEAD_DIM), lambda i: (n_b - 1 - i, 0, 0, 0)), col(0)],
        out_specs=[pl.BlockSpec((4, tb, D_MODEL), lambda i: (0, n_b - 1 - i, 0)), acc, acc],
        out_shape=[jax.ShapeDtypeStruct((4, s, D_MODEL), BF16), jax.ShapeDtypeStruct((SUBLANES, D_MODEL), F32),
                   jax.ShapeDtypeStruct((SUBLANES, D_MODEL), F32)],
        scratch_shapes=[pltpu.VMEM((HEADS, HEAD_DIM, HEAD_DIM), F32)],
    )(proj, proj, proj, proj, lb, norm_g, o, states, dyp)


def _headnorm(x, g, mult, name, col0=0):
    s = x.shape[0]
    tm = _tile(s, ROW_TILE)

    def body(x_ref, g_ref, y_ref):
        for h in range(HEADS):
            cs = slice(h * HEAD_DIM, (h + 1) * HEAD_DIM)
            xv = x_ref[:, cs]
            inv = lax.rsqrt(jnp.mean(xv * xv, axis=-1, keepdims=True) + EPS)
            y_ref[:, cs] = (xv * inv * g_ref[:, cs] * mult).astype(BF16)

    return pl.pallas_call(
        body, name=name, grid=(s // tm,),
        in_specs=[pl.BlockSpec((tm, D_MODEL), lambda i: (i, col0)), _full((1, D_MODEL))],
        out_specs=pl.BlockSpec((tm, D_MODEL), lambda i: (i, 0)),
        out_shape=jax.ShapeDtypeStruct((s, D_MODEL), BF16),
    )(x, g)


def _headnorm_bwd(x, g, mult, dy, name, col0=0, extra=None):
    s = x.shape[0]
    tm = _tile(s, ROW_TILE)
    groups = 2 if extra is not None else 1

    def body(*refs):
        x_ref, g_ref, dy_ref = refs[:3]
        dx_ref, dg_ref = refs[-2:]

        @pl.when(pl.program_id(0) == 0)
        def _():
            dg_ref[...] = jnp.zeros_like(dg_ref)

        for h in range(HEADS):
            cs = slice(h * HEAD_DIM, (h + 1) * HEAD_DIM)
            xv, dyv, gv = x_ref[:, cs], dy_ref[:, cs], g_ref[:, cs]
            inv = lax.rsqrt(jnp.mean(xv * xv, axis=-1, keepdims=True) + EPS)
            nrm = xv * inv
            dn = dyv * gv * mult
            dg_ref[:, cs] += _colsum8(dyv * nrm * mult)
            dx_ref[0, :, cs] = (inv * (dn - nrm * jnp.mean(dn * nrm, axis=-1, keepdims=True))).astype(BF16)
        if extra is not None:
            dx_ref[1] = refs[3][...]

    row = pl.BlockSpec((tm, D_MODEL), lambda i: (i, 0))
    ins = [x, g, dy] + ([extra] if extra is not None else [])
    specs = [pl.BlockSpec((tm, D_MODEL), lambda i: (i, col0)), _full((1, D_MODEL)), row] + ([row] if extra is not None else [])
    return pl.pallas_call(
        body, name=name, grid=(s // tm,), in_specs=specs,
        out_specs=[pl.BlockSpec((groups, tm, D_MODEL), lambda i: (0, i, 0)), _full((SUBLANES, D_MODEL))],
        out_shape=[jax.ShapeDtypeStruct((groups, s, D_MODEL), BF16), jax.ShapeDtypeStruct((SUBLANES, D_MODEL), F32)],
    )(*ins)


def _log_sigmoid(z):
    return jnp.minimum(z, 0.0) - jnp.log(1.0 + jnp.exp(-jnp.abs(z)))


def _fcum_fwd(f, bias, name):
    s = f.shape[0]
    tm = _tile(s, ROW_TILE)

    def body(f_ref, b_ref, cq_ref, ck_ref, carry):
        @pl.when(pl.program_id(0) == 0)
        def _():
            carry[...] = jnp.zeros_like(carry)

        cum = _tri_dot(_tri(tm), _log_sigmoid(f_ref[...] + b_ref[...])) + carry[...]
        carry[...] = cum[tm - 1:tm]
        cum_t = cum.T
        for h in range(HEADS):
            cq_ref[h] = jnp.broadcast_to(cum[:, h:h + 1], (tm, LANES))
            ck_ref[h] = cum_t[h:h + 1]

    return pl.pallas_call(
        body, name=name, grid=(s // tm,),
        in_specs=[pl.BlockSpec((tm, LANES), lambda i: (i, 0)), _full((1, LANES))],
        out_specs=[pl.BlockSpec((HEADS, tm, LANES), lambda i: (0, i, 0)),
                   pl.BlockSpec((HEADS, 1, tm), lambda i: (0, 0, i))],
        out_shape=[jax.ShapeDtypeStruct((HEADS, s, LANES), F32), jax.ShapeDtypeStruct((HEADS, 1, s), F32)],
        scratch_shapes=[pltpu.VMEM((1, LANES), F32)],
    )(f, bias)


def _fcum_bwd(f, bias, dck, dcq, name):
    s = f.shape[0]
    tm = _tile(s, ROW_TILE)
    n_b = s // tm

    def body(f_ref, b_ref, dck_ref, dcq_ref, dz_ref, db_ref, carry):
        @pl.when(pl.program_id(0) == 0)
        def _():
            carry[...] = jnp.zeros_like(carry)
            db_ref[...] = jnp.zeros_like(db_ref)

        rows = jnp.concatenate([dck_ref[h] for h in range(HEADS)] + [jnp.zeros((LANES - HEADS, tm), F32)], axis=0)
        lane = lax.broadcasted_iota(jnp.int32, (tm, LANES), 1)
        dcum = rows.T
        for h in range(HEADS):
            dcum = dcum + jnp.where(lane == h, dcq_ref[h], 0.0)
        dlf = _tri_dot(_tri(tm, upper=True), dcum) + carry[...]
        carry[...] = dlf[0:1]
        dz = dlf * _sig(-(f_ref[...] + b_ref[...]))
        dz_ref[0] = dz.astype(BF16)
        db_ref[...] += _colsum8(dz)

    return pl.pallas_call(
        body, name=name, grid=(n_b,),
        in_specs=[pl.BlockSpec((tm, LANES), lambda i: (n_b - 1 - i, 0)), _full((1, LANES)),
                  pl.BlockSpec((HEADS, 1, tm), lambda i: (0, 0, n_b - 1 - i)),
                  pl.BlockSpec((HEADS, tm, LANES), lambda i: (0, n_b - 1 - i, 0))],
        out_specs=[pl.BlockSpec((1, tm, LANES), lambda i: (0, n_b - 1 - i, 0)), _full((SUBLANES, LANES))],
        out_shape=[jax.ShapeDtypeStruct((1, s, LANES), BF16), jax.ShapeDtypeStruct((SUBLANES, LANES), F32)],
        scratch_shapes=[pltpu.VMEM((1, LANES), F32)],
    )(f, bias, dck, dcq)


def _att_logits(q, k, cq, ck, qi, ki, t):
    sc = _dot_nt(q, k) + cq[:, 0:1] - ck
    rows = qi * t + lax.broadcasted_iota(jnp.int32, (t, t), 0)
    cols = ki * t + lax.broadcasted_iota(jnp.int32, (t, t), 1)
    return jnp.where(cols <= rows, sc, NEG_INF)


def _fox_fwd(q, k, v, cq, ck, qo, name):
    s = q.shape[0]
    t = _tile(s, ATT_TILE)
    n_t = s // t

    def body(q_ref, k_ref, v_ref, cq_ref, ck_ref, og_ref, o_ref, y_ref, lse_ref, m_s, l_s, acc_s):
        qi, ki = pl.program_id(1), pl.program_id(2)

        @pl.when(ki == 0)
        def _():
            m_s[...] = jnp.full_like(m_s, NEG_INF)
            l_s[...] = jnp.zeros_like(l_s)
            acc_s[...] = jnp.zeros_like(acc_s)

        @pl.when(ki <= qi)
        def _():
            sc = _att_logits(q_ref[...], k_ref[...], cq_ref[...], ck_ref[...], qi, ki, t)
            m_old = m_s[...]
            m_new = jnp.maximum(m_old, jnp.max(sc, axis=-1, keepdims=True))
            alpha = jnp.exp(m_old - m_new)
            p = jnp.exp(sc - m_new[:, 0:1])
            l_s[...] = alpha * l_s[...] + jnp.sum(p, axis=-1, keepdims=True)
            acc_s[...] = alpha * acc_s[...] + _dot(p.astype(BF16), v_ref[...])
            m_s[...] = m_new

        @pl.when(ki == n_t - 1)
        def _():
            o = acc_s[...] / l_s[...]
            o_ref[...] = o
            y_ref[...] = (o * _sig(og_ref[...])).astype(BF16)
            lse_ref[...] = m_s[...] + jnp.log(l_s[...])

    qspec = pl.BlockSpec((t, HEAD_DIM), lambda h, qi, ki: (qi, h))
    kspec = pl.BlockSpec((t, HEAD_DIM), lambda h, qi, ki: (jnp.minimum(ki, qi), h))
    stat = pl.BlockSpec((None, t, LANES), lambda h, qi, ki: (h, qi, 0))
    return pl.pallas_call(
        body, name=name, grid=(HEADS, n_t, n_t),
        in_specs=[qspec, kspec, kspec, stat, pl.BlockSpec((None, 1, t), lambda h, qi, ki: (h, 0, jnp.minimum(ki, qi))),
                  pl.BlockSpec((t, HEAD_DIM), lambda h, qi, ki: (qi, HEADS + h))],
        out_specs=[qspec, qspec, stat],
        out_shape=[jax.ShapeDtypeStruct((s, D_MODEL), F32), jax.ShapeDtypeStruct((s, D_MODEL), BF16),
                   jax.ShapeDtypeStruct((HEADS, s, LANES), F32)],
        scratch_shapes=[pltpu.VMEM((t, LANES), F32), pltpu.VMEM((t, LANES), F32), pltpu.VMEM((t, HEAD_DIM), F32)],
    )(q, k, v, cq, ck, qo)


def _fox_gate_bwd(o, qo, dy, name):
    s = o.shape[0]
    tm = _tile(s, ROW_TILE)

    def body(o_ref, og_ref, dy_ref, do_ref, dg_ref, dl_ref):
        ov, dyv = o_ref[...], dy_ref[...]
        sg = _sig(og_ref[...])
        do = (dyv * sg).astype(BF16)
        do_ref[...] = do
        dg_ref[...] = (dyv * ov * sg * (1.0 - sg)).astype(BF16)
        prod = do.astype(F32) * ov
        for h in range(HEADS):
            dl_ref[h] = jnp.broadcast_to(
                jnp.sum(prod[:, h * HEAD_DIM:(h + 1) * HEAD_DIM], axis=-1, keepdims=True), (tm, LANES))

    row = pl.BlockSpec((tm, D_MODEL), lambda i: (i, 0))
    return pl.pallas_call(
        body, name=name, grid=(s // tm,),
        in_specs=[row, pl.BlockSpec((tm, D_MODEL), lambda i: (i, 1)), row],
        out_specs=[row, row, pl.BlockSpec((HEADS, tm, LANES), lambda i: (0, i, 0))],
        out_shape=[jax.ShapeDtypeStruct((s, D_MODEL), BF16), jax.ShapeDtypeStruct((s, D_MODEL), BF16),
                   jax.ShapeDtypeStruct((HEADS, s, LANES), F32)],
    )(o, qo, dy)


def _fox_bwd_kv(q, k, v, do, cq, ck, lse, delta, name):
    s = q.shape[0]
    t = _tile(s, ATT_TILE)
    n_t = s // t

    def body(q_ref, k_ref, v_ref, do_ref, cq_ref, ck_ref, lse_ref, dl_ref, dk_ref, dv_ref, dck_ref, dk_s, dv_s, dc_s):
        ki, qi = pl.program_id(1), pl.program_id(2)

        @pl.when(qi == 0)
        def _():
            dk_s[...] = jnp.zeros_like(dk_s)
            dv_s[...] = jnp.zeros_like(dv_s)
            dc_s[...] = jnp.zeros_like(dc_s)

        @pl.when(qi >= ki)
        def _():
            qv, dov = q_ref[...], do_ref[...]
            sc = _att_logits(qv, k_ref[...], cq_ref[...], ck_ref[...], qi, ki, t)
            p = jnp.exp(sc - lse_ref[:, 0:1])
            dv_s[...] += _dot_tn(p.astype(BF16), dov)
            ds = p * (_dot_nt(dov, v_ref[...]) - dl_ref[:, 0:1])
            dk_s[...] += _dot_tn(ds.astype(BF16), qv)
            dc_s[...] -= jnp.sum(ds, axis=0, keepdims=True)

        @pl.when(qi == n_t - 1)
        def _():
            dk_ref[...] = dk_s[...]
            dv_ref[...] = dv_s[...].astype(BF16)
            dck_ref[...] = dc_s[...]

    qspec = pl.BlockSpec((t, HEAD_DIM), lambda h, ki, qi: (jnp.maximum(qi, ki), h))
    kspec = pl.BlockSpec((t, HEAD_DIM), lambda h, ki, qi: (ki, h))
    qstat = pl.BlockSpec((None, t, LANES), lambda h, ki, qi: (h, jnp.maximum(qi, ki), 0))
    krow = pl.BlockSpec((None, 1, t), lambda h, ki, qi: (h, 0, ki))
    return pl.pallas_call(
        body, name=name, grid=(HEADS, n_t, n_t),
        in_specs=[qspec, kspec, kspec, qspec, qstat, krow, qstat, qstat],
        out_specs=[kspec, pl.BlockSpec((None, t, HEAD_DIM), lambda h, ki, qi: (0, ki, h)), krow],
        out_shape=[jax.ShapeDtypeStruct((s, D_MODEL), F32), jax.ShapeDtypeStruct((1, s, D_MODEL), BF16),
                   jax.ShapeDtypeStruct((HEADS, 1, s), F32)],
        scratch_shapes=[pltpu.VMEM((t, HEAD_DIM), F32), pltpu.VMEM((t, HEAD_DIM), F32), pltpu.VMEM((1, t), F32)],
    )(q, k, v, do, cq, ck, lse, delta)


def _fox_bwd_q(q, k, v, do, cq, ck, lse, delta, name):
    s = q.shape[0]
    t = _tile(s, ATT_TILE)
    n_t = s // t

    def body(q_ref, k_ref, v_ref, do_ref, cq_ref, ck_ref, lse_ref, dl_ref, dq_ref, dcq_ref, dq_s, dc_s):
        qi, ki = pl.program_id(1), pl.program_id(2)

        @pl.when(ki == 0)
        def _():
            dq_s[...] = jnp.zeros_like(dq_s)
            dc_s[...] = jnp.zeros_like(dc_s)

        @pl.when(ki <= qi)
        def _():
            kv = k_ref[...]
            sc = _att_logits(q_ref[...], kv, cq_ref[...], ck_ref[...], qi, ki, t)
            p = jnp.exp(sc - lse_ref[:, 0:1])
            ds = p * (_dot_nt(do_ref[...], v_ref[...]) - dl_ref[:, 0:1])
            dq_s[...] += _dot(ds.astype(BF16), kv)
            dc_s[...] += jnp.sum(ds, axis=-1, keepdims=True)

        @pl.when(ki == n_t - 1)
        def _():
            dq_ref[...] = dq_s[...]
            dcq_ref[...] = dc_s[...]

    qspec = pl.BlockSpec((t, HEAD_DIM), lambda h, qi, ki: (qi, h))
    kspec = pl.BlockSpec((t, HEAD_DIM), lambda h, qi, ki: (jnp.minimum(ki, qi), h))
    qstat = pl.BlockSpec((None, t, LANES), lambda h, qi, ki: (h, qi, 0))
    return pl.pallas_call(
        body, name=name, grid=(HEADS, n_t, n_t),
        in_specs=[qspec, kspec, kspec, qspec, qstat,
                  pl.BlockSpec((None, 1, t), lambda h, qi, ki: (h, 0, jnp.minimum(ki, qi))), qstat, qstat],
        out_specs=[qspec, qstat],
        out_shape=[jax.ShapeDtypeStruct((s, D_MODEL), F32), jax.ShapeDtypeStruct((HEADS, s, LANES), F32)],
        scratch_shapes=[pltpu.VMEM((t, HEAD_DIM), F32), pltpu.VMEM((t, LANES), F32)],
    )(q, k, v, do, cq, ck, lse, delta)


def _ffn_forward(x_in, branch, gate, shift, scale, w_up, conv_w, conv_b, w_down, tag):
    x_mid, h = _premix(x_in, shift, scale, tag + "_premix", branch=branch, gate=gate)
    u = _mm_nn(h, w_up, 2, BF16, tag + "_up")
    a = _convglu_fwd(u, conv_w, conv_b, tag + "_convglu")
    ffn = _mm_nn(a, w_down, 1, F32, tag + "_down")[0]
    return x_mid, ffn, (h, u, a)


def _ffn_backward(dx_out, x_mid, ffn, gate, scale, saved, w_up, conv_w, conv_b, w_down, tag):
    h, u, a = saved
    dffn, dgate = _branch_bwd(dx_out, ffn, gate, tag + "_gate_bwd")
    da = _mm_nt(dffn, w_down, BF16, tag + "_down_dx")
    dw_down = _mm_tn(a, dffn, 1, tag + "_down_dw")
    du, dconv = _convglu_bwd(u, da, conv_w, conv_b, tag + "_convglu_bwd")
    dh = _mm_nt(du, w_up, F32, tag + "_up_dx")
    dw_up = _mm_tn(h, du, N_CHIPS, tag + "_up_dw")
    dx_mid, dshift, dscale = _premix_bwd(x_mid, dh, scale, dx_out, tag + "_premix_bwd")
    return dx_mid, dw_up, dw_down, dict(gate=dgate, shift=dshift, scale=dscale, conv=dconv)


def _local_step(x, target, mods, lb, vecs, wts):
    m0, m1, mk = mods["l0"], mods["l1"], mods["kv"]
    h0 = _premix(x, m0[0], m0[1], "l0_premix")
    proj = _mm_nn(h0, wts["a_w_in"], 1, F32, "l0_in")[0]
    o_a, yp, states = _hgrn_fwd(proj, lb, vecs["a_norm_g"], "l0_hgrn")
    y0 = _mm_nn(yp, wts["a_w_out"], 1, F32, "l0_out")[0]
    x1, ffn0, saved0 = _ffn_forward(x, y0, m0[2], m0[3], m0[4], wts["up0"], vecs["conv_w0"], vecs["conv_b0"],
                                    wts["down0"], "l0_ffn")
    x2, hk = _premix(x1, mk[0], mk[1], "kv_premix", branch=ffn0, gate=m0[5])
    k_raw = _mm_nn(hk, wts["kv_k"], 1, F32, "kv_k")[0]
    v_sh = _mm_nn(hk, wts["kv_v"], 1, BF16, "kv_v")[0]
    f_raw = _mm_nn(hk, wts["kv_f"], 1, F32, "kv_f")[0]
    k_sh = _headnorm(k_raw, vecs["k_norm_g"], 1.0, "kv_knorm")
    cq, ck = _fcum_fwd(f_raw, vecs["kv_b_f"], "kv_fcum")
    h1 = _premix(x2, m1[0], m1[1], "l1_premix")
    qo = _mm_nn(h1, wts["b_w_q"], 1, F32, "l1_q")[0]
    q_scale = HEAD_DIM ** -0.5
    q = _headnorm(qo, vecs["q_norm_g"], q_scale, "l1_qnorm")
    o_b, og, lse = _fox_fwd(q, k_sh, v_sh, cq, ck, qo, "l1_fox")
    y1 = _mm_nn(og, wts["b_w_out"], 1, F32, "l1_out")[0]
    x3, ffn1, saved1 = _ffn_forward(x2, y1, m1[2], m1[3], m1[4], wts["up1"], vecs["conv_w1"], vecs["conv_b1"],
                                    wts["down1"], "l1_ffn")
    sq, dx4 = _loss_head(x3, ffn1, m1[5], target, "loss_head")

    big, small = {}, {}
    dx3, big["up1"], big["down1"], s_ffn1 = _ffn_backward(dx4, x3, ffn1, m1[5], m1[4], saved1, wts["up1"],
                                                          vecs["conv_w1"], vecs["conv_b1"], wts["down1"], "l1_ffn")
    dy1, dg1_1 = _branch_bwd(dx3, y1, m1[2], "l1_mix_gate_bwd")
    d_og = _mm_nt(dy1, wts["b_w_out"], F32, "l1_out_dx")
    big["b_w_out"] = _mm_tn(og, dy1, 1, "l1_out_dw")
    do_b, dgate_b, delta = _fox_gate_bwd(o_b, qo, d_og, "l1_fox_gate_bwd")
    dk, dv, dck = _fox_bwd_kv(q, k_sh, v_sh, do_b, cq, ck, lse, delta, "l1_fox_bwd_kv")
    dq, dcq = _fox_bwd_q(q, k_sh, v_sh, do_b, cq, ck, lse, delta, "l1_fox_bwd_q")
    dqo, dqg = _headnorm_bwd(qo, vecs["q_norm_g"], q_scale, dq, "l1_qnorm_bwd", extra=dgate_b)
    dh1 = _mm_nt(dqo, wts["b_w_q"], F32, "l1_q_dx")
    big["b_w_q"] = _mm_tn(h1, dqo, N_CHIPS, "l1_q_dw")
    dx2, dsh1_1, dsc1_1 = _premix_bwd(x2, dh1, m1[1], dx3, "l1_premix_bwd")
    dk_raw, dkg = _headnorm_bwd(k_raw, vecs["k_norm_g"], 1.0, dk, "kv_knorm_bwd")
    dz, dbf = _fcum_bwd(f_raw, vecs["kv_b_f"], dck, dcq, "kv_fcum_bwd")
    dhk = _mm_nt(dk_raw, wts["kv_k"], F32, "kv_k_dx")
    dhk = _mm_nt(dv, wts["kv_v"], F32, "kv_v_dx", add=dhk)
    dhk = _mm_nt(dz, wts["kv_f"], F32, "kv_f_dx", add=dhk)
    big["kv_k"] = _mm_tn(hk, dk_raw, 1, "kv_k_dw")
    big["kv_v"] = _mm_tn(hk, dv, 1, "kv_v_dw")
    big["kv_f"] = _mm_tn(hk, dz, 1, "kv_f_dw")
    dx2, dshk, dsck = _premix_bwd(x2, dhk, mk[1], dx2, "kv_premix_bwd")
    dx1, big["up0"], big["down0"], s_ffn0 = _ffn_backward(dx2, x1, ffn0, m0[5], m0[4], saved0, wts["up0"],
                                                          vecs["conv_w0"], vecs["conv_b0"], wts["down0"], "l0_ffn")
    dy0, dg1_0 = _branch_bwd(dx1, y0, m0[2], "l0_mix_gate_bwd")
    dyp = _mm_nt(dy0, wts["a_w_out"], F32, "l0_out_dx")
    big["a_w_out"] = _mm_tn(yp, dy0, 1, "l0_out_dw")
    dproj, dlb, dng = _hgrn_bwd(proj, lb, vecs["a_norm_g"], o_a, states, dyp, "l0_hgrn_bwd")
    dh0 = _mm_nt(dproj, wts["a_w_in"], F32, "l0_in_dx")
    big["a_w_in"] = _mm_tn(h0, dproj, N_CHIPS, "l0_in_dw")
    grad_x, dsh1_0, dsc1_0 = _premix_bwd(x, dh0, m0[1], dx1, "l0_premix_bwd")

    small["mod_l0"] = [dsh1_0, dsc1_0, dg1_0, s_ffn0["shift"], s_ffn0["scale"], s_ffn0["gate"]]
    small["mod_l1"] = [dsh1_1, dsc1_1, dg1_1, s_ffn1["shift"], s_ffn1["scale"], s_ffn1["gate"]]
    small["mod_kv"] = [dshk, dsck]
    small["conv0"], small["conv1"] = s_ffn0["conv"], s_ffn1["conv"]
    small["a_norm_g"], small["k_norm_g"], small["q_norm_g"] = dng, dkg, dqg
    small["kv_b_f"], small["lb"] = dbf, dlb
    return sq, grad_x, big, small


HBM = pl.BlockSpec(memory_space=pltpu.HBM)
COMM_CHUNK_ELEMS = 256 * 1024


def _place():
    x, y, c = lax.axis_index("x"), lax.axis_index("y"), lax.axis_index("c")
    chips = [(1 - x, y), (x, 1 - y), (1 - x, 1 - y)]
    return x, y, c, (x, y, 1 - c), chips


def _chunk_rows(rows, cols):
    best = BF16_ROWS
    for r in range(BF16_ROWS, rows + 1, BF16_ROWS):
        if rows % r == 0 and r * cols <= COMM_CHUNK_ELEMS:
            best = r
    assert rows % best == 0, (rows, cols)
    return best


def _allgather8(block, name):
    m_per, n = block.shape

    def body(x_ref, out_ref, send_sems, recv_sems, local_sem):
        x, y, c, sibling, chips = _place()
        me = (x, y, c)

        def rows(px, py, pc):
            return out_ref.at[pl.ds((4 * px + 2 * py + pc) * m_per, m_per), :]

        def copy(k, blk, to, src=None):
            return pltpu.make_async_remote_copy(
                src_ref=rows(*blk) if src is None else src, dst_ref=rows(*blk),
                send_sem=send_sems.at[k], recv_sem=recv_sems.at[k], device_id=to, device_id_type=MESH)

        mine = pltpu.make_async_copy(x_ref, rows(*me), local_sem)
        mine.start()
        first = [copy(0, me, sibling, src=x_ref)]
        first += [copy(1 + j, me, (*chip, c), src=x_ref) for j, chip in enumerate(chips)]
        for cp in first:
            cp.start()
        passed = [copy(4 + j, (*chip, c), sibling) for j, chip in enumerate(chips)]
        for j, chip in enumerate(chips):
            copy(1 + j, (*chip, c), me).wait_recv()
            passed[j].start()
        copy(0, sibling, me).wait_recv()
        for j, chip in enumerate(chips):
            copy(4 + j, (*chip, 1 - c), me).wait_recv()
        for cp in first + passed:
            cp.wait_send()
        mine.wait()

    return pl.pallas_call(
        body, name=name, out_shape=jax.ShapeDtypeStruct((N_DEV * m_per, n), block.dtype),
        in_specs=[pl.BlockSpec(memory_space=pltpu.VMEM)], out_specs=pl.BlockSpec(memory_space=pltpu.VMEM),
        scratch_shapes=[pltpu.SemaphoreType.DMA((7,)), pltpu.SemaphoreType.DMA((7,)), pltpu.SemaphoreType.DMA],
    )(block)


def _gather_weights(shards, name):
    n_t = len(shards)
    dims = [s.shape for s in shards]

    def body(*refs):
        ins, outs = refs[:n_t], refs[n_t:2 * n_t]
        send_ici, recv_ici, send_d2d, recv_d2d = refs[2 * n_t:]
        x, y, c, sibling, chips = _place()
        p_me = 2 * x + y

        def halves(t, count):
            return outs[t].at[pl.ds(0, count), pl.ds(0, dims[t][0] // 2), :]

        def waiter(t, sem_s, sem_r):
            win = halves(t, 3)
            return pltpu.make_async_remote_copy(src_ref=win, dst_ref=win, send_sem=sem_s.at[t], recv_sem=sem_r.at[t],
                                                device_id=sibling, device_id_type=MESH)

        def half_copy(t, chip_idx, to, sem_s, sem_r):
            r2 = dims[t][0] // 2
            win = outs[t].at[chip_idx, pl.ds(c * r2, r2), :]
            return pltpu.make_async_remote_copy(src_ref=win, dst_ref=win, send_sem=sem_s.at[t], recv_sem=sem_r.at[t],
                                                device_id=to, device_id_type=MESH)

        for t in range(n_t):
            r, cols = dims[t]
            rows = _chunk_rows(r, cols)

            def cast(fbuf, bbuf, t=t, r=r, rows=rows):
                for k in range(r // rows):
                    pltpu.sync_copy(ins[t].at[pl.ds(k * rows, rows), :], fbuf)
                    bbuf[...] = fbuf[...].astype(BF16)
                    pltpu.sync_copy(bbuf, outs[t].at[p_me, pl.ds(k * rows, rows), :])

            pl.run_scoped(cast, pltpu.VMEM((rows, cols), F32), pltpu.VMEM((rows, cols), BF16))
            for chip in chips:
                half_copy(t, p_me, (*chip, c), send_ici, recv_ici).start()
        for t in range(n_t):
            waiter(t, send_ici, recv_ici).wait_recv()
            for cx, cy in chips:
                half_copy(t, 2 * cx + cy, sibling, send_d2d, recv_d2d).start()
        for t in range(n_t):
            waiter(t, send_d2d, recv_d2d).wait_recv()
            waiter(t, send_ici, recv_ici).wait_send()
            waiter(t, send_d2d, recv_d2d).wait_send()

    return pl.pallas_call(
        body, name=name, in_specs=[HBM] * n_t, out_specs=[HBM] * n_t,
        out_shape=[jax.ShapeDtypeStruct((N_CHIPS, r, cols), BF16) for r, cols in dims],
        scratch_shapes=[pltpu.SemaphoreType.DMA((n_t,))] * 4,
    )(*shards)


def _reduce_scatter(parts, name):
    n_t = len(parts)
    dims = [p.shape[1:] for p in parts]

    def body(*refs):
        ins = refs[:n_t]
        outs, from_sib, chip_sum, from_chips = (refs[(1 + k) * n_t:(2 + k) * n_t] for k in range(4))
        s1, r1, s2, r2, s3, r3 = refs[5 * n_t:]
        x, y, c, sibling, chips = _place()
        p_me = 2 * x + y

        def remote(src, dst, sem_s, sem_r, t, to):
            return pltpu.make_async_remote_copy(src_ref=src, dst_ref=dst, send_sem=sem_s.at[t], recv_sem=sem_r.at[t],
                                                device_id=to, device_id_type=MESH)

        def swap1(t):
            h = dims[t][0] // 2
            return remote(ins[t].at[:, pl.ds((1 - c) * h, h), :], from_sib[t], s1, r1, t, sibling)

        def to_chips(t):
            return remote(from_chips[t], from_chips[t], s2, r2, t, sibling)

        def swap3(t):
            h = dims[t][0] // 2
            win = outs[t].at[pl.ds(c * h, h), :]
            return remote(win, win, s3, r3, t, sibling)

        for t in range(n_t):
            swap1(t).start()
        for t in range(n_t):
            r, cols = dims[t]
            h = r // 2
            rows = _chunk_rows(h, cols)
            swap1(t).wait_recv()

            def pair_sum(a, b, o, t=t, h=h, rows=rows):
                for p in range(N_CHIPS):
                    for k in range(h // rows):
                        pltpu.sync_copy(ins[t].at[p, pl.ds(c * h + k * rows, rows), :], a)
                        pltpu.sync_copy(from_sib[t].at[p, pl.ds(k * rows, rows), :], b)
                        o[...] = (a[...].astype(F32) + b[...].astype(F32)).astype(BF16)
                        pltpu.sync_copy(o, chip_sum[t].at[p, pl.ds(k * rows, rows), :])

            pl.run_scoped(pair_sum, *[pltpu.VMEM((rows, cols), BF16)] * 3)
            for j, (cx, cy) in enumerate(chips):
                remote(chip_sum[t].at[2 * cx + cy], from_chips[t].at[j], s2, r2, t, (cx, cy, c)).start()
        for t in range(n_t):
            r, cols = dims[t]
            h = r // 2
            rows = _chunk_rows(h, cols)
            to_chips(t).wait_recv()

            def total(a, b0, b1, b2, o, t=t, h=h, rows=rows):
                for k in range(h // rows):
                    pltpu.sync_copy(chip_sum[t].at[p_me, pl.ds(k * rows, rows), :], a)
                    for j, b in enumerate((b0, b1, b2)):
                        pltpu.sync_copy(from_chips[t].at[j, pl.ds(k * rows, rows), :], b)
                    o[...] = ((a[...].astype(F32) + b0[...].astype(F32)) + b1[...].astype(F32)) + b2[...].astype(F32)
                    pltpu.sync_copy(o, outs[t].at[pl.ds(c * h + k * rows, rows), :])

            pl.run_scoped(total, *([pltpu.VMEM((rows, cols), BF16)] * 4 + [pltpu.VMEM((rows, cols), F32)]))
            swap3(t).start()
        for t in range(n_t):
            swap3(t).wait_recv()
            swap1(t).wait_send()
            to_chips(t).wait_send()
            swap3(t).wait_send()

    half = lambda n, rc: jax.ShapeDtypeStruct((n, rc[0] // 2, rc[1]), BF16)
    out_shape = ([jax.ShapeDtypeStruct(rc, F32) for rc in dims] + [half(N_CHIPS, rc) for rc in dims]
                 + [half(N_CHIPS, rc) for rc in dims] + [half(N_CHIPS - 1, rc) for rc in dims])
    outs = pl.pallas_call(
        body, name=name, in_specs=[HBM] * n_t, out_specs=[HBM] * (4 * n_t), out_shape=out_shape,
        scratch_shapes=[pltpu.SemaphoreType.DMA((n_t,))] * 6,
    )(*parts)
    return outs[:n_t]


def _cond_rows(c16, w, act, name):
    n_l, dm, wid = w.shape

    def body(c_ref, w_ref, o_ref, a_ref):
        cv = c_ref[...]
        if act:
            cv = cv * _sig(cv)
        a_ref[...] = cv
        o_ref[...] = _dot_f32(cv, w_ref[...])

    return pl.pallas_call(
        body, name=name, grid=(n_l,),
        in_specs=[_full((16, dm)), pl.BlockSpec((None, dm, wid), lambda l: (l, 0, 0))],
        out_specs=[pl.BlockSpec((None, 16, wid), lambda l: (l, 0, 0)), _full((16, dm))],
        out_shape=[jax.ShapeDtypeStruct((n_l, 16, wid), F32), jax.ShapeDtypeStruct((16, dm), F32)],
    )(c16, w)


def _outer_grad(ct, dm, name):
    n_l, kk, wid = dm.shape
    d_rows = ct.shape[0]

    def body(c_ref, d_ref, o_ref):
        o_ref[...] = _dot_f32(c_ref[...], d_ref[...])

    return pl.pallas_call(
        body, name=name, grid=(n_l,),
        in_specs=[_full((d_rows, kk)), pl.BlockSpec((None, kk, wid), lambda l: (l, 0, 0))],
        out_specs=pl.BlockSpec((None, d_rows, wid), lambda l: (l, 0, 0)),
        out_shape=jax.ShapeDtypeStruct((n_l, d_rows, wid), F32),
    )(ct, dm)


def _sum_devices(g, name):
    rows, n = g.shape

    def body(g_ref, o_ref):
        acc = g_ref[0:SUBLANES, :]
        for dev in range(1, N_DEV):
            acc = acc + g_ref[dev * SUBLANES:(dev + 1) * SUBLANES, :]
        o_ref[...] = acc

    return pl.pallas_call(body, name=name, out_shape=jax.ShapeDtypeStruct((SUBLANES, n), F32))(g)


def _adamw(w, g, m, v, name):
    shape = w.shape
    cols = shape[-1]
    rows = w.size // cols
    tr = rows
    for cand in range(SUBLANES, min(rows, 256) + 1, SUBLANES):
        if rows % cand == 0:
            tr = cand
    if rows * cols <= COMM_CHUNK_ELEMS:
        tr = rows
    c1 = 1.0 / (1.0 - ADAM_B1 ** ADAM_STEP)
    c2 = 1.0 / (1.0 - ADAM_B2 ** ADAM_STEP)

    def body(w_ref, g_ref, m_ref, v_ref, d_ref, mo_ref, vo_ref):
        gv = g_ref[...]
        m_new = ADAM_B1 * m_ref[...] + (1.0 - ADAM_B1) * gv
        v_new = ADAM_B2 * v_ref[...] + (1.0 - ADAM_B2) * (gv * gv)
        mo_ref[...] = m_new
        vo_ref[...] = v_new
        d_ref[...] = -ADAM_LR * ((m_new * c1) / (jnp.sqrt(v_new * c2) + ADAM_EPS) + ADAM_WD * w_ref[...])

    spec = pl.BlockSpec((tr, cols), lambda i: (i, 0))
    outs = pl.pallas_call(
        body, name=name, grid=(rows // tr,), in_specs=[spec] * 4, out_specs=[spec] * 3,
        out_shape=[jax.ShapeDtypeStruct((rows, cols), F32)] * 3,
    )(*[a.reshape(rows, cols) for a in (w, g, m, v)])
    return tuple(o.reshape(shape) for o in outs)


def _pad_cols(a, cols):
    return jnp.pad(a, [(0, 0)] * (a.ndim - 1) + [(0, cols - a.shape[-1])])


def _flat8(parts, width):
    v = jnp.concatenate([p.reshape(-1) for p in parts])
    return jnp.pad(v, (0, width - v.shape[0])).reshape(SUBLANES, width // SUBLANES)


KV_SHARD = 514
KV_SHARD_PAD = 640
BIG = ("a_w_in", "a_w_out", "kv_w", "b_w_q", "b_w_out", "up0", "up1", "down0", "down1")


def kernel(x, c, ada_w, ada_b, a_w_in, a_lb_logits, a_norm_g, a_w_out, kv_ada_w, kv_ada_b, kv_w, kv_b_f, k_norm_g, b_w_q, q_norm_g, b_w_out, ffn_w_up, ffn_conv_w, ffn_conv_b, ffn_w_down, loss_target, m_ada_w, m_ada_b, m_a_w_in, m_a_lb_logits, m_a_norm_g, m_a_w_out, m_kv_ada_w, m_kv_ada_b, m_kv_w, m_kv_b_f, m_k_norm_g, m_b_w_q, m_q_norm_g, m_b_w_out, m_ffn_w_up, m_ffn_conv_w, m_ffn_conv_b, m_ffn_w_down, v_ada_w, v_ada_b, v_a_w_in, v_a_lb_logits, v_a_norm_g, v_a_w_out, v_kv_ada_w, v_kv_ada_b, v_kv_w, v_kv_b_f, v_k_norm_g, v_b_w_q, v_q_norm_g, v_b_w_out, v_ffn_w_up, v_ffn_conv_w, v_ffn_conv_b, v_ffn_w_down):
    dm, ff = D_MODEL, D_FF
    ix, iy, ic = lax.axis_index("x"), lax.axis_index("y"), lax.axis_index("c")
    chip = 2 * ix + iy
    dev = 2 * chip + ic

    w1 = 10240
    g1 = _allgather8(_flat8([c, a_lb_logits, ffn_conv_w], w1), "gather_cond").reshape(N_DEV, w1)
    c_all = g1[:, :dm]
    per_chip = g1[0::2]
    lb_logits = per_chip[:, dm:dm + 512].reshape(N_CHIPS, 2, 256).transpose(1, 0, 2).reshape(2, dm)
    conv_w = per_chip[:, dm + 512:dm + 512 + 2 * CONV_W * FFN_COLS].reshape(N_CHIPS, 2, CONV_W, FFN_COLS)
    conv_w = conv_w.transpose(1, 2, 0, 3).reshape(2, CONV_W, 2, ff).transpose(0, 2, 1, 3)
    conv_b = ffn_conv_b.reshape(2, 2, 1, ff)
    lb = jax.nn.softmax(lb_logits, axis=0)[0:1]

    c16 = jnp.pad(c_all, ((0, 8), (0, 0)))
    mod_ada, c_act16 = _cond_rows(c16, ada_w, True, "mod_ada")
    mod_kv, _ = _cond_rows(c16, kv_ada_w[None], True, "mod_kv")
    mine = jnp.concatenate([mod_ada[0, :8], mod_ada[1, :8], mod_kv[0, :8]], axis=1)
    w2 = mine.shape[1]
    g2 = _allgather8(mine, "gather_mod").reshape(N_DEV, 8, w2)[0::2]
    my_rows = lax.dynamic_index_in_dim(g2, dev, axis=1, keepdims=False)
    mod0 = my_rows[:, 0:1536].reshape(6 * dm) + ada_b[0]
    mod1 = my_rows[:, 1536:3072].reshape(6 * dm) + ada_b[1]
    modk = my_rows[:, 3072:3584].reshape(2 * dm) + kv_ada_b
    mods = {"l0": [v.reshape(1, dm) for v in jnp.split(mod0, 6)],
            "l1": [v.reshape(1, dm) for v in jnp.split(mod1, 6)],
            "kv": [v.reshape(1, dm) for v in jnp.split(modk, 2)]}

    local = [a_w_in[0], a_w_out[0], _pad_cols(kv_w, KV_SHARD_PAD), b_w_q[0], b_w_out[0], ffn_w_up[0], ffn_w_up[1],
             ffn_w_down[0], ffn_w_down[1]]
    gathered = dict(zip(BIG, _gather_weights(local, "gather_weights")))
    kv_full = gathered["kv_w"][:, :, :KV_SHARD].transpose(1, 0, 2).reshape(dm, N_CHIPS * KV_SHARD)
    rowwise = lambda g: g.reshape(1, -1, dm)
    wts = {"a_w_in": gathered["a_w_in"], "a_w_out": rowwise(gathered["a_w_out"]),
           "kv_k": kv_full[None, :, :dm], "kv_v": kv_full[None, :, dm:2 * dm],
           "kv_f": _pad_cols(kv_full[None, :, 2 * dm:], LANES),
           "b_w_q": gathered["b_w_q"], "b_w_out": rowwise(gathered["b_w_out"]),
           "up0": gathered["up0"], "up1": gathered["up1"],
           "down0": rowwise(gathered["down0"]), "down1": rowwise(gathered["down1"])}
    vecs = {"a_norm_g": jnp.tile(a_norm_g, (1, HEADS)), "k_norm_g": jnp.tile(k_norm_g[None], (1, HEADS)),
            "q_norm_g": jnp.tile(q_norm_g, (1, HEADS)), "kv_b_f": _pad_cols(kv_b_f[None], LANES),
            "conv_w0": conv_w[0], "conv_b0": conv_b[0], "conv_w1": conv_w[1], "conv_b1": conv_b[1]}

    sq, grad_x, big, small = _local_step(x[0], loss_target[0], mods, lb, vecs, wts)
    loss = lax.psum(0.5 * jnp.sum(sq) / dm, ("x", "y", "c"))

    kv_grad = jnp.concatenate([big["kv_k"][0], big["kv_v"][0], big["kv_f"][0][:, :HEADS]], axis=1)
    kv_grad = _pad_cols(kv_grad.reshape(dm, N_CHIPS, KV_SHARD).transpose(1, 0, 2), KV_SHARD_PAD)
    chipwise = lambda g: g.reshape(N_CHIPS, -1, dm)
    parts = [big["a_w_in"], chipwise(big["a_w_out"]), kv_grad, big["b_w_q"], chipwise(big["b_w_out"]),
             big["up0"], big["up1"], chipwise(big["down0"]), chipwise(big["down1"])]
    rs = dict(zip(BIG, _reduce_scatter(parts, "reduce_grads")))

    fold = lambda a: a.sum(axis=0)
    heads = lambda a: fold(a).reshape(HEADS, HEAD_DIM).sum(axis=0)
    conv_flat = lambda a: a.sum(axis=2).transpose(1, 0, 2)
    pieces = ([fold(a) for a in small["mod_l0"]] + [fold(a) for a in small["mod_l1"]] + [fold(a) for a in small["mod_kv"]]
              + [conv_flat(small["conv0"]), conv_flat(small["conv1"]), heads(small["a_norm_g"]), heads(small["k_norm_g"]),
                 heads(small["q_norm_g"]), fold(small["kv_b_f"]), fold(small["lb"])])
    w3 = 61440
    g3 = _allgather8(_flat8(pieces, w3), "gather_small")
    tot = _sum_devices(g3, "sum_small").reshape(w3)
    n_mod = 14 * dm
    dmod_all = g3.reshape(N_DEV, w3)[:, :n_mod]
    o = n_mod
    conv_tot = [tot[o + l * 8 * ff: o + (l + 1) * 8 * ff].reshape(4, 2 * ff) for l in range(2)]
    o += 16 * ff
    g_a_norm, g_k_norm, g_q_norm = (tot[o + i * HEAD_DIM: o + (i + 1) * HEAD_DIM] for i in range(3))
    o += 3 * HEAD_DIM
    g_kv_b_f = tot[o:o + HEADS]
    dlb = tot[o + LANES:o + LANES + dm]

    ct = _pad_cols(c_act16[:8].T, LANES)
    dmod_pad = jnp.pad(dmod_all, ((0, LANES - N_DEV), (0, 0)))
    cols_ada = jnp.stack([lax.dynamic_slice_in_dim(dmod_pad, l * 6 * dm + chip * 1536, 1536, axis=1) for l in range(2)])
    cols_kv = lax.dynamic_slice_in_dim(dmod_pad, 12 * dm + chip * 512, 512, axis=1)[None]
    g_ada_w = _outer_grad(ct, cols_ada, "grad_ada_w")
    g_kv_ada_w = _outer_grad(ct, cols_kv, "grad_kv_ada_w")[0]

    my_lb = lax.dynamic_slice_in_dim(lb[0], chip * 256, 256)
    l0 = lax.dynamic_slice_in_dim(dlb, chip * 256, 256) * my_lb * (1.0 - my_lb)
    grads = {
        "ada_w": g_ada_w, "ada_b": jnp.stack([tot[:6 * dm], tot[6 * dm:12 * dm]]),
        "a_w_in": rs["a_w_in"][None], "a_lb_logits": jnp.stack([l0, -l0]), "a_norm_g": g_a_norm[None],
        "a_w_out": rs["a_w_out"][None], "kv_ada_w": g_kv_ada_w, "kv_ada_b": tot[12 * dm:14 * dm],
        "kv_w": rs["kv_w"][:, :KV_SHARD], "kv_b_f": g_kv_b_f, "k_norm_g": g_k_norm,
        "b_w_q": rs["b_w_q"][None], "q_norm_g": g_q_norm[None], "b_w_out": rs["b_w_out"][None],
        "ffn_w_up": jnp.stack([rs["up0"], rs["up1"]]),
        "ffn_conv_w": jnp.stack([lax.dynamic_slice_in_dim(ct_l[:CONV_W], chip * FFN_COLS, FFN_COLS, axis=1) for ct_l in conv_tot]),
        "ffn_conv_b": jnp.stack([ct_l[CONV_W] for ct_l in conv_tot]),
        "ffn_w_down": jnp.stack([rs["down0"], rs["down1"]]),
    }
    weights = dict(ada_w=ada_w, ada_b=ada_b, a_w_in=a_w_in, a_lb_logits=a_lb_logits, a_norm_g=a_norm_g, a_w_out=a_w_out,
                   kv_ada_w=kv_ada_w, kv_ada_b=kv_ada_b, kv_w=kv_w, kv_b_f=kv_b_f, k_norm_g=k_norm_g, b_w_q=b_w_q,
                   q_norm_g=q_norm_g, b_w_out=b_w_out, ffn_w_up=ffn_w_up, ffn_conv_w=ffn_conv_w, ffn_conv_b=ffn_conv_b,
                   ffn_w_down=ffn_w_down)
    m_in = dict(ada_w=m_ada_w, ada_b=m_ada_b, a_w_in=m_a_w_in, a_lb_logits=m_a_lb_logits, a_norm_g=m_a_norm_g,
                a_w_out=m_a_w_out, kv_ada_w=m_kv_ada_w, kv_ada_b=m_kv_ada_b, kv_w=m_kv_w, kv_b_f=m_kv_b_f,
                k_norm_g=m_k_norm_g, b_w_q=m_b_w_q, q_norm_g=m_q_norm_g, b_w_out=m_b_w_out, ffn_w_up=m_ffn_w_up,
                ffn_conv_w=m_ffn_conv_w, ffn_conv_b=m_ffn_conv_b, ffn_w_down=m_ffn_w_down)
    v_in = dict(ada_w=v_ada_w, ada_b=v_ada_b, a_w_in=v_a_w_in, a_lb_logits=v_a_lb_logits, a_norm_g=v_a_norm_g,
                a_w_out=v_a_w_out, kv_ada_w=v_kv_ada_w, kv_ada_b=v_kv_ada_b, kv_w=v_kv_w, kv_b_f=v_kv_b_f,
                k_norm_g=v_k_norm_g, b_w_q=v_b_w_q, q_norm_g=v_q_norm_g, b_w_out=v_b_w_out, ffn_w_up=v_ffn_w_up,
                ffn_conv_w=v_ffn_conv_w, ffn_conv_b=v_ffn_conv_b, ffn_w_down=v_ffn_w_down)

    names = list(weights)
    grads = {n: grads[n].reshape(weights[n].shape) for n in names}
    upd = {n: _adamw(weights[n], grads[n], m_in[n], v_in[n], "adamw_" + n) for n in names}
    return (loss, grad_x[None], *[grads[n] for n in names], *[upd[n][0] for n in names],
            *[upd[n][1] for n in names], *[upd[n][2] for n in names])
```

```python
import jax
import jax.numpy as jnp
from jax import lax
from jax.experimental import pallas as pl
from jax.experimental.pallas import tpu as pltpu

F32 = jnp.float32
BF16 = jnp.bfloat16

D_MODEL = 1024
HEADS = 8
HEAD_DIM = 128
A_CHUNK = 64
D_FF = 2816
CONV_W = 3
EPS = 1e-6
NEG_INF = -1e30
N_CHIPS = 4
N_DEV = 8

ADAM_LR = 0.001
ADAM_B1 = 0.9
ADAM_B2 = 0.999
ADAM_EPS = 1e-08
ADAM_WD = 0.01
ADAM_STEP = 10

SUBLANES = 8
BF16_ROWS = 16
LANES = 128
HALO = BF16_ROWS
ROW_TILE = 512
FFN_COLS = 1408
HGRN_ROWS = 256
ATT_TILE = 512
MESH = pl.DeviceIdType.MESH


def _sig(x):
    return jax.nn.sigmoid(x)


def _dot(a, b):
    return jnp.dot(a, b, preferred_element_type=F32)


def _dot_nt(a, b):
    return lax.dot_general(a, b, (((1,), (1,)), ((), ())), preferred_element_type=F32)


def _dot_tn(a, b):
    return lax.dot_general(a, b, (((0,), (0,)), ((), ())), preferred_element_type=F32)


def _split2(x):
    hi = x.astype(BF16)
    lo = (x - hi.astype(F32)).astype(BF16)
    return hi, lo


def _dot_f32(a, b):
    ah, al = _split2(a)
    bh, bl = _split2(b)
    return _dot(ah, bh) + _dot(ah, bl) + _dot(al, bh)


def _tri_dot(tri, x):
    hi = x.astype(BF16)
    r = x - hi.astype(F32)
    mid = r.astype(BF16)
    lo = (r - mid.astype(F32)).astype(BF16)
    return _dot(tri, hi) + _dot(tri, mid) + _dot(tri, lo)


def _tri(n, upper=False):
    r = lax.broadcasted_iota(jnp.int32, (n, n), 0)
    c = lax.broadcasted_iota(jnp.int32, (n, n), 1)
    keep = (c >= r) if upper else (c <= r)
    return jnp.where(keep, 1.0, 0.0).astype(BF16)


def _colsum8(v):
    rows, n = v.shape
    return v.reshape(rows // SUBLANES, SUBLANES, n).sum(axis=0)


def _full(shape):
    nd = len(shape)
    return pl.BlockSpec(shape, lambda *_: (0,) * nd)


def _tile(n, want):
    t = min(n, want)
    assert n % t == 0, (n, t)
    return t


def _mm_nn(a, w, groups, out_dtype, name):
    m_rows, k = a.shape
    p_n, _, n = w.shape
    per = p_n // groups
    tm = _tile(m_rows, ROW_TILE)

    def body(a_ref, w_ref, o_ref):
        av = a_ref[...]
        for p in range(p_n):
            o_ref[p // per, :, (p % per) * n:(p % per + 1) * n] = _dot(av, w_ref[p]).astype(out_dtype)

    return pl.pallas_call(
        body, name=name, grid=(m_rows // tm,),
        in_specs=[pl.BlockSpec((tm, k), lambda i: (i, 0)), _full((p_n, k, n))],
        out_specs=pl.BlockSpec((groups, tm, per * n), lambda i: (0, i, 0)),
        out_shape=jax.ShapeDtypeStruct((groups, m_rows, per * n), out_dtype),
    )(a, w)


def _mm_nt(d, w, out_dtype, name, add=None):
    g_n, m_rows, _ = d.shape
    p_n, k, n = w.shape
    per = p_n // g_n
    tm = _tile(m_rows, ROW_TILE)

    def body(*refs):
        d_ref, w_ref = refs[0], refs[1]
        o_ref = refs[-1]
        acc = refs[2][...] if add is not None else None
        for p in range(p_n):
            t = _dot_nt(d_ref[p // per, :, (p % per) * n:(p % per + 1) * n], w_ref[p])
            acc = t if acc is None else acc + t
        o_ref[...] = acc.astype(out_dtype)

    ins = [d, w] + ([add] if add is not None else [])
    specs = [pl.BlockSpec((g_n, tm, per * n), lambda i: (0, i, 0)), _full((p_n, k, n))]
    if add is not None:
        specs.append(pl.BlockSpec((tm, k), lambda i: (i, 0)))
    return pl.pallas_call(
        body, name=name, grid=(m_rows // tm,), in_specs=specs,
        out_specs=pl.BlockSpec((tm, k), lambda i: (i, 0)),
        out_shape=jax.ShapeDtypeStruct((m_rows, k), out_dtype),
    )(*ins)


def _mm_tn(a, d, p_n, name):
    m_rows, k = a.shape
    g_n, _, w_cols = d.shape
    per = p_n // g_n
    n = w_cols // per
    tm = _tile(m_rows, ROW_TILE)
    steps = m_rows // tm

    def body(a_ref, d_ref, o_ref, acc):
        m = pl.program_id(1)

        @pl.when(m == 0)
        def _():
            acc[...] = jnp.zeros_like(acc)

        acc[...] += _dot_tn(a_ref[...], d_ref[...])

        @pl.when(m == steps - 1)
        def _():
            o_ref[...] = acc[...].astype(BF16)

    return pl.pallas_call(
        body, name=name, grid=(p_n, steps),
        in_specs=[pl.BlockSpec((tm, k), lambda p, m: (m, 0)),
                  pl.BlockSpec((None, tm, n), lambda p, m: (p // per, m, p % per))],
        out_specs=pl.BlockSpec((None, k, n), lambda p, m: (p, 0, 0)),
        out_shape=jax.ShapeDtypeStruct((p_n, k, n), BF16),
        scratch_shapes=[pltpu.VMEM((k, n), F32)],
    )(a, d)


def _premix(x, shift, scale, name, branch=None, gate=None):
    s, dm = x.shape
    tm = _tile(s, ROW_TILE)
    with_branch = branch is not None

    def body(*refs):
        x_ref, sh_ref, sc_ref = refs[:3]
        xv = x_ref[...]
        if with_branch:
            xv = xv + refs[4][...] * refs[3][...]
            refs[-2][...] = xv
        inv = lax.rsqrt(jnp.mean(xv * xv, axis=-1, keepdims=True) + EPS)
        refs[-1][...] = (xv * inv * (1.0 + sc_ref[...]) + sh_ref[...]).astype(BF16)

    row = pl.BlockSpec((tm, dm), lambda i: (i, 0))
    vec = _full((1, dm))
    ins, specs = [x, shift, scale], [row, vec, vec]
    out_shape, out_specs = [jax.ShapeDtypeStruct((s, dm), BF16)], [row]
    if with_branch:
        ins += [branch, gate]
        specs += [row, vec]
        out_shape.insert(0, jax.ShapeDtypeStruct((s, dm), F32))
        out_specs.insert(0, row)
    outs = pl.pallas_call(body, name=name, grid=(s // tm,), in_specs=specs, out_specs=out_specs,
                          out_shape=out_shape)(*ins)
    return tuple(outs) if with_branch else outs[0]


def _premix_bwd(x, dh, scale, dres, name):
    s, dm = x.shape
    tm = _tile(s, ROW_TILE)

    def body(x_ref, dh_ref, sc_ref, dres_ref, dx_ref, dsh_ref, dsc_ref):
        i = pl.program_id(0)

        @pl.when(i == 0)
        def _():
            dsh_ref[...] = jnp.zeros_like(dsh_ref)
            dsc_ref[...] = jnp.zeros_like(dsc_ref)

        xv = x_ref[...]
        dhv = dh_ref[...]
        inv = lax.rsqrt(jnp.mean(xv * xv, axis=-1, keepdims=True) + EPS)
        r = xv * inv
        dr = dhv * (1.0 + sc_ref[...])
        dx_ref[...] = dres_ref[...] + inv * (dr - r * jnp.mean(dr * r, axis=-1, keepdims=True))
        dsh_ref[...] += _colsum8(dhv)
        dsc_ref[...] += _colsum8(dhv * r)

    row = pl.BlockSpec((tm, dm), lambda i: (i, 0))
    acc = _full((SUBLANES, dm))
    return pl.pallas_call(
        body, name=name, grid=(s // tm,), in_specs=[row, row, _full((1, dm)), row],
        out_specs=[row, acc, acc],
        out_shape=[jax.ShapeDtypeStruct((s, dm), F32), jax.ShapeDtypeStruct((SUBLANES, dm), F32),
                   jax.ShapeDtypeStruct((SUBLANES, dm), F32)],
    )(x, dh, scale, dres)


def _branch_bwd(dx, y, gate, name):
    s, dm = dx.shape
    tm = _tile(s, ROW_TILE)

    def body(dx_ref, y_ref, g_ref, dy_ref, dg_ref):
        @pl.when(pl.program_id(0) == 0)
        def _():
            dg_ref[...] = jnp.zeros_like(dg_ref)

        dxv = dx_ref[...]
        dy_ref[0] = (dxv * g_ref[...]).astype(BF16)
        dg_ref[...] += _colsum8(dxv * y_ref[...])

    row = pl.BlockSpec((tm, dm), lambda i: (i, 0))
    return pl.pallas_call(
        body, name=name, grid=(s // tm,), in_specs=[row, row, _full((1, dm))],
        out_specs=[pl.BlockSpec((1, tm, dm), lambda i: (0, i, 0)), _full((SUBLANES, dm))],
        out_shape=[jax.ShapeDtypeStruct((1, s, dm), BF16), jax.ShapeDtypeStruct((SUBLANES, dm), F32)],
    )(dx, y, gate)


def _loss_head(x, branch, gate, target, name):
    s, dm = x.shape
    tm = _tile(s, ROW_TILE)

    def body(x_ref, b_ref, g_ref, t_ref, sq_ref, dy_ref):
        @pl.when(pl.program_id(0) == 0)
        def _():
            sq_ref[...] = jnp.zeros_like(sq_ref)

        err = x_ref[...] + g_ref[...] * b_ref[...] - t_ref[...]
        sq_ref[...] += _colsum8(err * err)
        dy_ref[...] = err * (1.0 / dm)

    row = pl.BlockSpec((tm, dm), lambda i: (i, 0))
    return pl.pallas_call(
        body, name=name, grid=(s // tm,), in_specs=[row, row, _full((1, dm)), row],
        out_specs=[_full((SUBLANES, dm)), row],
        out_shape=[jax.ShapeDtypeStruct((SUBLANES, dm), F32), jax.ShapeDtypeStruct((s, dm), F32)],
    )(x, branch, gate, target)


def _conv_taps(e, w, b):
    return w[2:3] * e + w[1:2] * pltpu.roll(e, 1, 0) + w[0:1] * pltpu.roll(e, 2, 0) + b


def _ffn_specs(s, tm, cb):
    hb = tm // HALO
    last = s // HALO - 1
    main = pl.BlockSpec((2, tm, cb), lambda j, i: (0, i, j))
    prev = pl.BlockSpec((2, HALO, cb), lambda j, i: (0, jnp.maximum(i * hb - 1, 0), j))
    nxt = pl.BlockSpec((2, HALO, cb), lambda j, i: (0, jnp.minimum((i + 1) * hb, last), j))
    wspec = pl.BlockSpec((2, CONV_W, cb), lambda j, i: (0, 0, j))
    bspec = pl.BlockSpec((2, 1, cb), lambda j, i: (0, 0, j))
    return main, prev, nxt, wspec, bspec


def _convglu_fwd(u, w, b, name):
    _, s, f = u.shape
    tm = _tile(s, 256)
    cb = _tile(f, FFN_COLS)
    main, prev, _, wspec, bspec = _ffn_specs(s, tm, cb)

    def body(u_ref, up_ref, w_ref, b_ref, a_ref):
        first = jnp.where(pl.program_id(1) > 0, 1.0, 0.0)

        def conv(g):
            e = jnp.concatenate([up_ref[g].astype(F32) * first, u_ref[g].astype(F32)], axis=0)
            return _conv_taps(e, w_ref[g], b_ref[g])[HALO:]

        gate = conv(0)
        a_ref[...] = (gate * _sig(gate) * conv(1)).astype(BF16)

    return pl.pallas_call(
        body, name=name, grid=(f // cb, s // tm), in_specs=[main, prev, wspec, bspec],
        out_specs=pl.BlockSpec((tm, cb), lambda j, i: (i, j)),
        out_shape=jax.ShapeDtypeStruct((s, f), BF16),
    )(u, u, w, b)


def _convglu_bwd(u, da, w, b, name):
    _, s, f = u.shape
    tm = _tile(s, 256)
    cb = _tile(f, FFN_COLS)
    steps = s // tm
    n_ext = tm + 2 * HALO
    main, prev, nxt, wspec, bspec = _ffn_specs(s, tm, cb)
    hb = tm // HALO
    last = s // HALO - 1
    da_main = pl.BlockSpec((tm, cb), lambda j, i: (i, j))
    da_next = pl.BlockSpec((HALO, cb), lambda j, i: (jnp.minimum((i + 1) * hb, last), j))

    def body(u_ref, up_ref, un_ref, da_ref, dan_ref, w_ref, b_ref, du_ref, acc_ref):
        i = pl.program_id(1)
        first = jnp.where(i > 0, 1.0, 0.0)
        notlast = jnp.where(i < steps - 1, 1.0, 0.0)

        @pl.when(i == 0)
        def _():
            acc_ref[...] = jnp.zeros_like(acc_ref)

        def ext(g):
            return jnp.concatenate([up_ref[g].astype(F32) * first, u_ref[g].astype(F32), un_ref[g].astype(F32)], axis=0)

        ug, uv = ext(0), ext(1)
        gate = _conv_taps(ug, w_ref[0], b_ref[0])
        val = _conv_taps(uv, w_ref[1], b_ref[1])
        da_e = jnp.concatenate([jnp.zeros((HALO, cb), F32), da_ref[...].astype(F32),
                                dan_ref[...].astype(F32) * notlast], axis=0)
        sg = _sig(gate)
        d_val = da_e * gate * sg
        d_gate = da_e * val * (sg * (1.0 + gate * (1.0 - sg)))

        def finish(g, d, e):
            wv = w_ref[g]
            du = wv[2:3] * d + wv[1:2] * pltpu.roll(d, n_ext - 1, 0) + wv[0:1] * pltpu.roll(d, n_ext - 2, 0)
            du_ref[g] = du[HALO:HALO + tm].astype(BF16)
            dm = d[HALO:HALO + tm]
            acc_ref[g, 2] += _colsum8(dm * e[HALO:HALO + tm])
            acc_ref[g, 1] += _colsum8(dm * pltpu.roll(e, 1, 0)[HALO:HALO + tm])
            acc_ref[g, 0] += _colsum8(dm * pltpu.roll(e, 2, 0)[HALO:HALO + tm])
            acc_ref[g, 3] += _colsum8(dm)

        finish(0, d_gate, ug)
        finish(1, d_val, uv)

    return pl.pallas_call(
        body, name=name, grid=(f // cb, steps),
        in_specs=[main, prev, nxt, da_main, da_next, wspec, bspec],
        out_specs=[main, pl.BlockSpec((2, 4, SUBLANES, cb), lambda j, i: (0, 0, 0, j))],
        out_shape=[jax.ShapeDtypeStruct((2, s, f), BF16), jax.ShapeDtypeStruct((2, 4, SUBLANES, f), F32)],
    )(u, u, u, da, da, w, b)


def _hgrn_gates(q_raw, f_raw, lb, tri):
    sf = _sig(f_raw)
    fg = lb + (1.0 - lb) * sf
    b = _tri_dot(tri, jnp.log(fg))
    return q_raw * _sig(q_raw), 1.0 - fg, b, fg, sf


def _hgrn_fwd(proj, lb, norm_g, name):
    s = proj.shape[0]
    tb = _tile(s, HGRN_ROWS)
    n_c = tb // A_CHUNK
    half = A_CHUNK // 2

    def body(q_ref, f_ref, v_ref, g_ref, lb_ref, ng_ref, o_ref, yp_ref, st_ref, state):
        @pl.when(pl.program_id(0) == 0)
        def _():
            state[...] = jnp.zeros_like(state)

        tri = _tri(A_CHUNK)
        causal = lax.broadcasted_iota(jnp.int32, (A_CHUNK, A_CHUNK), 1) <= lax.broadcasted_iota(
            jnp.int32, (A_CHUNK, A_CHUNK), 0)

        def chunk(ci, carry):
            rows = pl.ds(pl.multiple_of(ci * A_CHUNK, A_CHUNK), A_CHUNK)
            for h in range(HEADS):
                cs = slice(h * HEAD_DIM, (h + 1) * HEAD_DIM)
                qs, k, b, _, _ = _hgrn_gates(q_ref[rows, cs], f_ref[rows, cs], lb_ref[:, cs], tri)
                b_mid, b_last = b[half:half + 1], b[A_CHUNK - 1:A_CHUNK]
                vb = v_ref[rows, cs].astype(BF16)
                scores = _dot_nt((qs * jnp.exp(b - b_mid)).astype(BF16), (k * jnp.exp(b_mid - b)).astype(BF16))
                scores = jnp.where(causal, scores, 0.0)
                st = state[h]
                st_ref[ci, h] = st
                o = _dot(scores.astype(BF16), vb) + _dot_nt((qs * jnp.exp(b)).astype(BF16), st.astype(BF16))
                state[h] = st * jnp.exp(b_last) + _dot_tn(vb, (k * jnp.exp(b_last - b)).astype(BF16))
                o_ref[rows, cs] = o
                inv = lax.rsqrt(jnp.mean(o * o, axis=-1, keepdims=True) + EPS)
                g_raw = g_ref[rows, cs]
                yp_ref[rows, cs] = (o * inv * ng_ref[:, cs] * (g_raw * _sig(g_raw))).astype(BF16)
            return carry

        lax.fori_loop(0, n_c, chunk, 0)

    col = lambda j: pl.BlockSpec((tb, D_MODEL), lambda i: (i, j))
    vec = _full((1, D_MODEL))
    return pl.pallas_call(
        body, name=name, grid=(s // tb,), in_specs=[col(0), col(1), col(2), col(3), vec, vec],
        out_specs=[col(0), col(0), pl.BlockSpec((n_c, HEADS, HEAD_DIM, HEAD_DIM), lambda i: (i, 0, 0, 0))],
        out_shape=[jax.ShapeDtypeStruct((s, D_MODEL), F32), jax.ShapeDtypeStruct((s, D_MODEL), BF16),
                   jax.ShapeDtypeStruct((s // A_CHUNK, HEADS, HEAD_DIM, HEAD_DIM), F32)],
        scratch_shapes=[pltpu.VMEM((HEADS, HEAD_DIM, HEAD_DIM), F32)],
    )(proj, proj, proj, proj, lb, norm_g)


def _hgrn_bwd(proj, lb, norm_g, o, states, dyp, name):
    s = proj.shape[0]
    tb = _tile(s, HGRN_ROWS)
    n_c = tb // A_CHUNK
    n_b = s // tb
    half = A_CHUNK // 2

    def body(q_ref, f_ref, v_ref, g_ref, lb_ref, ng_ref, o_ref, st_ref, dyp_ref, dp_ref, dlb_ref, dng_ref, dstate):
        @pl.when(pl.program_id(0) == 0)
        def _():
            dstate[...] = jnp.zeros_like(dstate)
            dlb_ref[...] = jnp.zeros_like(dlb_ref)
            dng_ref[...] = jnp.zeros_like(dng_ref)

        tri = _tri(A_CHUNK)
        tri_up = _tri(A_CHUNK, upper=True)
        row_id = lax.broadcasted_iota(jnp.int32, (A_CHUNK, HEAD_DIM), 0)
        causal = lax.broadcasted_iota(jnp.int32, (A_CHUNK, A_CHUNK), 1) <= lax.broadcasted_iota(
            jnp.int32, (A_CHUNK, A_CHUNK), 0)

        def chunk(cj, carry):
            ci = n_c - 1 - cj
            rows = pl.ds(pl.multiple_of(ci * A_CHUNK, A_CHUNK), A_CHUNK)
            for h in range(HEADS):
                cs = slice(h * HEAD_DIM, (h + 1) * HEAD_DIM)
                q_raw, lbh = q_ref[rows, cs], lb_ref[:, cs]
                qs, k, b, fg, sf = _hgrn_gates(q_raw, f_ref[rows, cs], lbh, tri)
                b_mid, b_last = b[half:half + 1], b[A_CHUNK - 1:A_CHUNK]
                e_qi, e_ki, e_q, e_ks = jnp.exp(b - b_mid), jnp.exp(b_mid - b), jnp.exp(b), jnp.exp(b_last - b)
                q_i, k_i, q_e, k_s = qs * e_qi, k * e_ki, qs * e_q, k * e_ks
                vb = v_ref[rows, cs].astype(BF16)
                scores = jnp.where(causal, _dot_nt(q_i.astype(BF16), k_i.astype(BF16)), 0.0)
                ov, g_raw, dy, ng = o_ref[rows, cs], g_ref[rows, cs], dyp_ref[rows, cs], ng_ref[:, cs]
                inv = lax.rsqrt(jnp.mean(ov * ov, axis=-1, keepdims=True) + EPS)
                nrm = ov * inv
                sg = _sig(g_raw)
                gs = g_raw * sg
                dn = dy * ng * gs
                dng_ref[0:1, cs] += jnp.sum(dy * nrm * gs, axis=0, keepdims=True)
                dg_raw = dy * nrm * ng * (sg * (1.0 + g_raw * (1.0 - sg)))
                do = (inv * (dn - nrm * jnp.mean(dn * nrm, axis=-1, keepdims=True))).astype(BF16)
                st_prev = st_ref[ci, h]
                dst = dstate[h]
                dstb = dst.astype(BF16)
                d_scores = jnp.where(causal, _dot_nt(do, vb), 0.0).astype(BF16)
                dv = _dot_tn(scores.astype(BF16), do) + _dot_nt(k_s.astype(BF16), dstb)
                dq_i = _dot(d_scores, k_i.astype(BF16))
                dk_i = _dot_tn(d_scores, q_i.astype(BF16))
                dq_e = _dot(do, st_prev.astype(BF16))
                dk_s = _dot(vb, dstb)
                d_decay = jnp.sum(st_prev * dst, axis=0, keepdims=True)
                dstate[h] = dst * jnp.exp(b_last) + _dot_tn(do, q_e.astype(BF16))
                dq = dq_i * e_qi + dq_e * e_q
                dk = dk_i * e_ki + dk_s * e_ks
                t_qi, t_ki, t_ks = dq_i * q_i, dk_i * k_i, dk_s * k_s
                db = t_qi - t_ki + dq_e * q_e - t_ks
                db_mid = jnp.sum(t_ki - t_qi, axis=0, keepdims=True)
                db_last = jnp.sum(t_ks, axis=0, keepdims=True) + d_decay * jnp.exp(b_last)
                db = db + jnp.where(row_id == half, db_mid, 0.0) + jnp.where(row_id == A_CHUNK - 1, db_last, 0.0)
                dfg = _tri_dot(tri_up, db) / fg - dk
                dlb_ref[0:1, cs] += jnp.sum(dfg * (1.0 - sf), axis=0, keepdims=True)
                sq = _sig(q_raw)
                dp_ref[0, rows, cs] = (dq * (sq * (1.0 + q_raw * (1.0 - sq)))).astype(BF16)
                dp_ref[1, rows, cs] = (dfg * (1.0 - lbh) * sf * (1.0 - sf)).astype(BF16)
                dp_ref[2, rows, cs] = dv.astype(BF16)
                dp_ref[3, rows, cs] = dg_raw.astype(BF16)
            return carry

        lax.fori_loop(0, n_c, chunk, 0)

    col = lambda j: pl.BlockSpec((tb, D_MODEL), lambda i: (n_b - 1 - i, j))
    vec = _full((1, D_MODEL))
    acc = _full((SUBLANES, D_MODEL))
    return pl.pallas_call(
        body, name=name, grid=(n_b,),
        in_specs=[col(0), col(1), col(2), col(3), vec, vec, col(0),
                  pl.BlockSpec((n_c, HEADS, HEAD_DIM, HEAD_DIM), lambda i: (n_b - 1 - i, 0, 0, 0)), col(0)],
        out_specs=[pl.BlockSpec((4, tb, D_MODEL), lambda i: (0, n_b - 1 - i, 0)), acc, acc],
        out_shape=[jax.ShapeDtypeStruct((4, s, D_MODEL), BF16), jax.ShapeDtypeStruct((SUBLANES, D_MODEL), F32),
                   jax.ShapeDtypeStruct((SUBLANES, D_MODEL), F32)],
        scratch_shapes=[pltpu.VMEM((HEADS, HEAD_DIM, HEAD_DIM), F32)],
    )(proj, proj, proj, proj, lb, norm_g, o, states, dyp)


def _headnorm(x, g, mult, name, col0=0):
    s = x.shape[0]
    tm = _tile(s, ROW_TILE)

    def body(x_ref, g_ref, y_ref):
        for h in range(HEADS):
            cs = slice(h * HEAD_DIM, (h + 1) * HEAD_DIM)
            xv = x_ref[:, cs]
            inv = lax.rsqrt(jnp.mean(xv * xv, axis=-1, keepdims=True) + EPS)
            y_ref[:, cs] = (xv * inv * g_ref[:, cs] * mult).astype(BF16)

    return pl.pallas_call(
        body, name=name, grid=(s // tm,),
        in_specs=[pl.BlockSpec((tm, D_MODEL), lambda i: (i, col0)), _full((1, D_MODEL))],
        out_specs=pl.BlockSpec((tm, D_MODEL), lambda i: (i, 0)),
        out_shape=jax.ShapeDtypeStruct((s, D_MODEL), BF16),
    )(x, g)


def _headnorm_bwd(x, g, mult, dy, name, col0=0, extra=None):
    s = x.shape[0]
    tm = _tile(s, ROW_TILE)
    groups = 2 if extra is not None else 1

    def body(*refs):
        x_ref, g_ref, dy_ref = refs[:3]
        dx_ref, dg_ref = refs[-2:]

        @pl.when(pl.program_id(0) == 0)
        def _():
            dg_ref[...] = jnp.zeros_like(dg_ref)

        for h in range(HEADS):
            cs = slice(h * HEAD_DIM, (h + 1) * HEAD_DIM)
            xv, dyv, gv = x_ref[:, cs], dy_ref[:, cs], g_ref[:, cs]
            inv = lax.rsqrt(jnp.mean(xv * xv, axis=-1, keepdims=True) + EPS)
            nrm = xv * inv
            dn = dyv * gv * mult
            dg_ref[:, cs] += _colsum8(dyv * nrm * mult)
            dx_ref[0, :, cs] = (inv * (dn - nrm * jnp.mean(dn * nrm, axis=-1, keepdims=True))).astype(BF16)
        if extra is not None:
            dx_ref[1] = refs[3][...]

    row = pl.BlockSpec((tm, D_MODEL), lambda i: (i, 0))
    ins = [x, g, dy] + ([extra] if extra is not None else [])
    specs = [pl.BlockSpec((tm, D_MODEL), lambda i: (i, col0)), _full((1, D_MODEL)), row] + ([row] if extra is not None else [])
    return pl.pallas_call(
        body, name=name, grid=(s // tm,), in_specs=specs,
        out_specs=[pl.BlockSpec((groups, tm, D_MODEL), lambda i: (0, i, 0)), _full((SUBLANES, D_MODEL))],
        out_shape=[jax.ShapeDtypeStruct((groups, s, D_MODEL), BF16), jax.ShapeDtypeStruct((SUBLANES, D_MODEL), F32)],
    )(*ins)


def _log_sigmoid(z):
    return jnp.minimum(z, 0.0) - jnp.log(1.0 + jnp.exp(-jnp.abs(z)))


Q_CUM, Q_ONE, Q_LSE = 0, 3, 6
LOG2E = 1.4426950408889634


def _pieces(v):
    hi = v.astype(BF16).astype(F32)
    mid = (v - hi).astype(BF16).astype(F32)
    lo = ((v - hi) - mid).astype(BF16).astype(F32)
    return hi, mid, lo


def _side(lane, at, v):
    hi, mid, lo = _pieces(v)
    return jnp.where(lane == at, hi, jnp.where(lane == at + 1, mid, jnp.where(lane == at + 2, lo, 0.0)))


def _fcum_fwd(f, bias, name):
    s = f.shape[0]
    tm = _tile(s, ROW_TILE)

    def body(f_ref, b_ref, qa_ref, ka_ref, carry):
        @pl.when(pl.program_id(0) == 0)
        def _():
            carry[...] = jnp.zeros_like(carry)

        cum = _tri_dot(_tri(tm), _log_sigmoid(f_ref[...] + b_ref[...])) + carry[...]
        carry[...] = cum[tm - 1:tm]
        lane = lax.broadcasted_iota(jnp.int32, (tm, LANES), 1)
        ones_q = jnp.where((lane >= Q_ONE) & (lane < Q_LSE), 1.0, 0.0)
        ones_k = jnp.where((lane < Q_ONE) | ((lane >= Q_LSE) & (lane < Q_LSE + 3)), 1.0, 0.0)
        for h in range(HEADS):
            c2 = cum[:, h:h + 1] * LOG2E
            qa_ref[h] = (_side(lane, Q_CUM, c2) + ones_q).astype(BF16)
            ka_ref[h] = (_side(lane, Q_ONE, -c2) + ones_k).astype(BF16)

    side = pl.BlockSpec((HEADS, tm, LANES), lambda i: (0, i, 0))
    return pl.pallas_call(
        body, name=name, grid=(s // tm,),
        in_specs=[pl.BlockSpec((tm, LANES), lambda i: (i, 0)), _full((1, LANES))],
        out_specs=[side, side],
        out_shape=[jax.ShapeDtypeStruct((HEADS, s, LANES), BF16)] * 2,
        scratch_shapes=[pltpu.VMEM((1, LANES), F32)],
    )(f, bias)


def _fcum_bwd(f, bias, dka, dqa, name):
    s = f.shape[0]
    tm = _tile(s, ROW_TILE)
    n_b = s // tm

    def body(f_ref, b_ref, dka_ref, dqa_ref, dz_ref, db_ref, carry):
        @pl.when(pl.program_id(0) == 0)
        def _():
            carry[...] = jnp.zeros_like(carry)
            db_ref[...] = jnp.zeros_like(db_ref)

        lane = lax.broadcasted_iota(jnp.int32, (tm, LANES), 1)
        dcum = jnp.zeros((tm, LANES), F32)
        for h in range(HEADS):
            dcum = dcum + jnp.where(lane == h, dqa_ref[h][:, 0:1] - dka_ref[h][:, Q_ONE:Q_ONE + 1], 0.0)
        dlf = _tri_dot(_tri(tm, upper=True), dcum) + carry[...]
        carry[...] = dlf[0:1]
        dz = dlf * _sig(-(f_ref[...] + b_ref[...]))
        dz_ref[0] = dz.astype(BF16)
        db_ref[...] += _colsum8(dz)

    return pl.pallas_call(
        body, name=name, grid=(n_b,),
        in_specs=[pl.BlockSpec((tm, LANES), lambda i: (n_b - 1 - i, 0)), _full((1, LANES)),
                  pl.BlockSpec((HEADS, tm, LANES), lambda i: (0, n_b - 1 - i, 0)),
                  pl.BlockSpec((HEADS, tm, LANES), lambda i: (0, n_b - 1 - i, 0))],
        out_specs=[pl.BlockSpec((1, tm, LANES), lambda i: (0, n_b - 1 - i, 0)), _full((SUBLANES, LANES))],
        out_shape=[jax.ShapeDtypeStruct((1, s, LANES), BF16), jax.ShapeDtypeStruct((SUBLANES, LANES), F32)],
        scratch_shapes=[pltpu.VMEM((1, LANES), F32)],
    )(f, bias, dka, dqa)


def _causal_pairs(n_t, key_major):
    if key_major:
        pairs = [(qi, ki) for ki in range(n_t) for qi in range(ki, n_t)]
    else:
        pairs = [(qi, ki) for qi in range(n_t) for ki in range(qi + 1)]
    return (jnp.array([p[0] for p in pairs], jnp.int32), jnp.array([p[1] for p in pairs], jnp.int32))


def _with_side(main_ref, side_ref):
    return jnp.concatenate([main_ref[...], side_ref[...]], axis=1)


def _lane_const(t, lo, hi, value):
    lane = lax.broadcasted_iota(jnp.int32, (t, LANES), 1)
    return jnp.where((lane >= lo) & (lane < hi), value, 0.0).astype(BF16)


def _att_specs(t):
    qmain = pl.BlockSpec((t, HEAD_DIM), lambda h, p, qt, kt: (qt[p], h))
    kmain = pl.BlockSpec((t, HEAD_DIM), lambda h, p, qt, kt: (kt[p], h))
    qside = pl.BlockSpec((None, t, LANES), lambda h, p, qt, kt: (h, qt[p], 0))
    kside = pl.BlockSpec((None, t, LANES), lambda h, p, qt, kt: (h, kt[p], 0))
    return qmain, kmain, qside, kside


def _fox_fwd(q, qa, k, ka, v, qo, name):
    s = q.shape[0]
    t = _tile(s, ATT_TILE)
    qt, kt = _causal_pairs(s // t, key_major=False)

    def body(qt_ref, kt_ref, q_ref, qa_ref, k_ref, ka_ref, v_ref, og_ref, o_ref, y_ref, qab_ref, m_s, l_s, acc_s):
        pid = pl.program_id(1)
        qi, ki = qt_ref[pid], kt_ref[pid]

        @pl.when(ki == 0)
        def _():
            m_s[...] = jnp.full_like(m_s, NEG_INF)
            l_s[...] = jnp.zeros_like(l_s)
            acc_s[...] = jnp.zeros_like(acc_s)

        def step(diagonal):
            sc = _dot_nt(_with_side(q_ref, qa_ref), _with_side(k_ref, ka_ref))
            if diagonal:
                sc = jnp.where(lax.broadcasted_iota(jnp.int32, (t, t), 1) <= lax.broadcasted_iota(jnp.int32, (t, t), 0),
                               sc, NEG_INF)
            m_old = m_s[...]
            m_new = jnp.maximum(m_old, jnp.max(sc, axis=-1, keepdims=True))
            alpha = jnp.exp2(m_old - m_new)
            p = jnp.exp2(sc - m_new[:, 0:1]).astype(BF16)
            pv = _dot(p, jnp.concatenate([v_ref[...], _lane_const(t, 0, 1, 1.0)], axis=1))
            acc_s[...] = alpha * acc_s[...] + pv[:, :HEAD_DIM]
            l_s[...] = alpha * l_s[...] + pv[:, HEAD_DIM:]
            m_s[...] = m_new

        @pl.when(ki < qi)
        def _():
            step(False)

        @pl.when(ki == qi)
        def _():
            step(True)
            l = l_s[:, 0:1]
            o = acc_s[...] / l
            o_ref[...] = o
            y_ref[...] = (o * _sig(og_ref[...])).astype(BF16)
            lane = lax.broadcasted_iota(jnp.int32, (t, LANES), 1)
            qab_ref[...] = qa_ref[...] + _side(lane, Q_LSE, -(m_s[:, 0:1] + jnp.log2(l))).astype(BF16)

    qmain, kmain, qside, kside = _att_specs(t)
    return pl.pallas_call(
        body, name=name,
        grid_spec=pltpu.PrefetchScalarGridSpec(
            num_scalar_prefetch=2, grid=(HEADS, qt.shape[0]),
            in_specs=[qmain, qside, kmain, kside, kmain,
                      pl.BlockSpec((t, HEAD_DIM), lambda h, p, qt, kt: (qt[p], HEADS + h))],
            out_specs=[qmain, qmain, qside],
            scratch_shapes=[pltpu.VMEM((t, LANES), F32), pltpu.VMEM((t, LANES), F32), pltpu.VMEM((t, HEAD_DIM), F32)]),
        out_shape=[jax.ShapeDtypeStruct((s, D_MODEL), F32), jax.ShapeDtypeStruct((s, D_MODEL), BF16),
                   jax.ShapeDtypeStruct((HEADS, s, LANES), BF16)],
    )(qt, kt, q, qa, k, ka, v, qo)


def _fox_gate_bwd(o, qo, dy, name):
    s = o.shape[0]
    tm = _tile(s, ROW_TILE)

    def body(o_ref, og_ref, dy_ref, do_ref, dg_ref, dl_ref):
        ov, dyv = o_ref[...], dy_ref[...]
        sg = _sig(og_ref[...])
        do = (dyv * sg).astype(BF16)
        do_ref[...] = do
        dg_ref[...] = (dyv * ov * sg * (1.0 - sg)).astype(BF16)
        prod = do.astype(F32) * ov
        lane = lax.broadcasted_iota(jnp.int32, (tm, LANES), 1)
        for h in range(HEADS):
            delta = jnp.sum(prod[:, h * HEAD_DIM:(h + 1) * HEAD_DIM], axis=-1, keepdims=True)
            dl_ref[h] = _side(lane, 0, delta).astype(BF16)

    row = pl.BlockSpec((tm, D_MODEL), lambda i: (i, 0))
    return pl.pallas_call(
        body, name=name, grid=(s // tm,),
        in_specs=[row, pl.BlockSpec((tm, D_MODEL), lambda i: (i, 1)), row],
        out_specs=[row, row, pl.BlockSpec((HEADS, tm, LANES), lambda i: (0, i, 0))],
        out_shape=[jax.ShapeDtypeStruct((s, D_MODEL), BF16), jax.ShapeDtypeStruct((s, D_MODEL), BF16),
                   jax.ShapeDtypeStruct((HEADS, s, LANES), BF16)],
    )(o, qo, dy)


def _fox_bwd_kv(q, qab, k, ka, v, do, doa, name):
    s = q.shape[0]
    t = _tile(s, ATT_TILE)
    n_t = s // t
    qt, kt = _causal_pairs(n_t, key_major=True)

    def body(qt_ref, kt_ref, q_ref, qab_ref, k_ref, ka_ref, v_ref, do_ref, doa_ref, dk_ref, dv_ref, dka_ref, dk_s, dv_s):
        pid = pl.program_id(1)
        qi, ki = qt_ref[pid], kt_ref[pid]

        @pl.when(qi == ki)
        def _():
            dk_s[...] = jnp.zeros_like(dk_s)
            dv_s[...] = jnp.zeros_like(dv_s)

        def step(diagonal):
            qc = _with_side(q_ref, qab_ref)
            sc = _dot_nt(_with_side(k_ref, ka_ref), qc)
            if diagonal:
                sc = jnp.where(lax.broadcasted_iota(jnp.int32, (t, t), 0) <= lax.broadcasted_iota(jnp.int32, (t, t), 1),
                               sc, NEG_INF)
            p = jnp.exp2(sc)
            dp = _dot_nt(jnp.concatenate([v_ref[...], _lane_const(t, 0, 3, -1.0)], axis=1), _with_side(do_ref, doa_ref))
            dv_s[...] += _dot(p.astype(BF16), do_ref[...])
            dk_s[...] += _dot((p * dp).astype(BF16), qc)

        @pl.when(qi > ki)
        def _():
            step(False)

        @pl.when(qi == ki)
        def _():
            step(True)

        @pl.when(qi == n_t - 1)
        def _():
            dk_ref[...] = dk_s[:, :HEAD_DIM] * (1.0 / LOG2E)
            dka_ref[...] = dk_s[:, HEAD_DIM:]
            dv_ref[...] = dv_s[...].astype(BF16)

    qmain, kmain, qside, kside = _att_specs(t)
    return pl.pallas_call(
        body, name=name,
        grid_spec=pltpu.PrefetchScalarGridSpec(
            num_scalar_prefetch=2, grid=(HEADS, qt.shape[0]),
            in_specs=[qmain, qside, kmain, kside, kmain, qmain, qside],
            out_specs=[kmain, pl.BlockSpec((None, t, HEAD_DIM), lambda h, p, qt, kt: (0, kt[p], h)), kside],
            scratch_shapes=[pltpu.VMEM((t, 2 * HEAD_DIM), F32), pltpu.VMEM((t, HEAD_DIM), F32)]),
        out_shape=[jax.ShapeDtypeStruct((s, D_MODEL), F32), jax.ShapeDtypeStruct((1, s, D_MODEL), BF16),
                   jax.ShapeDtypeStruct((HEADS, s, LANES), F32)],
    )(qt, kt, q, qab, k, ka, v, do, doa)


def _fox_bwd_q(q, qab, k, ka, v, do, doa, name):
    s = q.shape[0]
    t = _tile(s, ATT_TILE)
    qt, kt = _causal_pairs(s // t, key_major=False)

    def body(qt_ref, kt_ref, q_ref, qab_ref, k_ref, ka_ref, v_ref, do_ref, doa_ref, dq_ref, dqa_ref, dq_s):
        pid = pl.program_id(1)
        qi, ki = qt_ref[pid], kt_ref[pid]

        @pl.when(ki == 0)
        def _():
            dq_s[...] = jnp.zeros_like(dq_s)

        def step(diagonal):
            kc = _with_side(k_ref, ka_ref)
            sc = _dot_nt(_with_side(q_ref, qab_ref), kc)
            if diagonal:
                sc = jnp.where(lax.broadcasted_iota(jnp.int32, (t, t), 1) <= lax.broadcasted_iota(jnp.int32, (t, t), 0),
                               sc, NEG_INF)
            dp = _dot_nt(_with_side(do_ref, doa_ref), jnp.concatenate([v_ref[...], _lane_const(t, 0, 3, -1.0)], axis=1))
            dq_s[...] += _dot((jnp.exp2(sc) * dp).astype(BF16), kc)

        @pl.when(ki < qi)
        def _():
            step(False)

        @pl.when(ki == qi)
        def _():
            step(True)
            dq_ref[...] = dq_s[:, :HEAD_DIM]
            dqa_ref[...] = dq_s[:, HEAD_DIM:]

    qmain, kmain, qside, kside = _att_specs(t)
    return pl.pallas_call(
        body, name=name,
        grid_spec=pltpu.PrefetchScalarGridSpec(
            num_scalar_prefetch=2, grid=(HEADS, qt.shape[0]),
            in_specs=[qmain, qside, kmain, kside, kmain, qmain, qside],
            out_specs=[qmain, qside],
            scratch_shapes=[pltpu.VMEM((t, 2 * HEAD_DIM), F32)]),
        out_shape=[jax.ShapeDtypeStruct((s, D_MODEL), F32), jax.ShapeDtypeStruct((HEADS, s, LANES), F32)],
    )(qt, kt, q, qab, k, ka, v, do, doa)


def _ffn_forward(x_in, branch, gate, shift, scale, w_up, conv_w, conv_b, w_down, tag):
    x_mid, h = _premix(x_in, shift, scale, tag + "_premix", branch=branch, gate=gate)
    u = _mm_nn(h, w_up, 2, BF16, tag + "_up")
    a = _convglu_fwd(u, conv_w, conv_b, tag + "_convglu")
    ffn = _mm_nn(a, w_down, 1, F32, tag + "_down")[0]
    return x_mid, ffn, (h, u, a)


def _ffn_backward(dx_out, x_mid, ffn, gate, scale, saved, w_up, conv_w, conv_b, w_down, tag):
    h, u, a = saved
    dffn, dgate = _branch_bwd(dx_out, ffn, gate, tag + "_gate_bwd")
    da = _mm_nt(dffn, w_down, BF16, tag + "_down_dx")
    dw_down = _mm_tn(a, dffn, 1, tag + "_down_dw")
    du, dconv = _convglu_bwd(u, da, conv_w, conv_b, tag + "_convglu_bwd")
    dh = _mm_nt(du, w_up, F32, tag + "_up_dx")
    dw_up = _mm_tn(h, du, N_CHIPS, tag + "_up_dw")
    dx_mid, dshift, dscale = _premix_bwd(x_mid, dh, scale, dx_out, tag + "_premix_bwd")
    return dx_mid, dw_up, dw_down, dict(gate=dgate, shift=dshift, scale=dscale, conv=dconv)


def _local_step(x, target, mods, lb, vecs, wts):
    m0, m1, mk = mods["l0"], mods["l1"], mods["kv"]
    h0 = _premix(x, m0[0], m0[1], "l0_premix")
    proj = _mm_nn(h0, wts["a_w_in"], 1, F32, "l0_in")[0]
    o_a, yp, states = _hgrn_fwd(proj, lb, vecs["a_norm_g"], "l0_hgrn")
    y0 = _mm_nn(yp, wts["a_w_out"], 1, F32, "l0_out")[0]
    x1, ffn0, saved0 = _ffn_forward(x, y0, m0[2], m0[3], m0[4], wts["up0"], vecs["conv_w0"], vecs["conv_b0"],
                                    wts["down0"], "l0_ffn")
    x2, hk = _premix(x1, mk[0], mk[1], "kv_premix", branch=ffn0, gate=m0[5])
    k_raw = _mm_nn(hk, wts["kv_k"], 1, F32, "kv_k")[0]
    v_sh = _mm_nn(hk, wts["kv_v"], 1, BF16, "kv_v")[0]
    f_raw = _mm_nn(hk, wts["kv_f"], 1, F32, "kv_f")[0]
    k_sh = _headnorm(k_raw, vecs["k_norm_g"], 1.0, "kv_knorm")
    qa, ka = _fcum_fwd(f_raw, vecs["kv_b_f"], "kv_fcum")
    h1 = _premix(x2, m1[0], m1[1], "l1_premix")
    qo = _mm_nn(h1, wts["b_w_q"], 1, F32, "l1_q")[0]
    q_scale = HEAD_DIM ** -0.5
    q = _headnorm(qo, vecs["q_norm_g"], q_scale * LOG2E, "l1_qnorm")
    o_b, og, qab = _fox_fwd(q, qa, k_sh, ka, v_sh, qo, "l1_fox")
    y1 = _mm_nn(og, wts["b_w_out"], 1, F32, "l1_out")[0]
    x3, ffn1, saved1 = _ffn_forward(x2, y1, m1[2], m1[3], m1[4], wts["up1"], vecs["conv_w1"], vecs["conv_b1"],
                                    wts["down1"], "l1_ffn")
    sq, dx4 = _loss_head(x3, ffn1, m1[5], target, "loss_head")

    big, small = {}, {}
    dx3, big["up1"], big["down1"], s_ffn1 = _ffn_backward(dx4, x3, ffn1, m1[5], m1[4], saved1, wts["up1"],
                                                          vecs["conv_w1"], vecs["conv_b1"], wts["down1"], "l1_ffn")
    dy1, dg1_1 = _branch_bwd(dx3, y1, m1[2], "l1_mix_gate_bwd")
    d_og = _mm_nt(dy1, wts["b_w_out"], F32, "l1_out_dx")
    big["b_w_out"] = _mm_tn(og, dy1, 1, "l1_out_dw")
    do_b, dgate_b, doa = _fox_gate_bwd(o_b, qo, d_og, "l1_fox_gate_bwd")
    dk, dv, dka = _fox_bwd_kv(q, qab, k_sh, ka, v_sh, do_b, doa, "l1_fox_bwd_kv")
    dq, dqa = _fox_bwd_q(q, qab, k_sh, ka, v_sh, do_b, doa, "l1_fox_bwd_q")
    dqo, dqg = _headnorm_bwd(qo, vecs["q_norm_g"], q_scale, dq, "l1_qnorm_bwd", extra=dgate_b)
    dh1 = _mm_nt(dqo, wts["b_w_q"], F32, "l1_q_dx")
    big["b_w_q"] = _mm_tn(h1, dqo, N_CHIPS, "l1_q_dw")
    dx2, dsh1_1, dsc1_1 = _premix_bwd(x2, dh1, m1[1], dx3, "l1_premix_bwd")
    dk_raw, dkg = _headnorm_bwd(k_raw, vecs["k_norm_g"], 1.0, dk, "kv_knorm_bwd")
    dz, dbf = _fcum_bwd(f_raw, vecs["kv_b_f"], dka, dqa, "kv_fcum_bwd")
    dhk = _mm_nt(dk_raw, wts["kv_k"], F32, "kv_k_dx")
    dhk = _mm_nt(dv, wts["kv_v"], F32, "kv_v_dx", add=dhk)
    dhk = _mm_nt(dz, wts["kv_f"], F32, "kv_f_dx", add=dhk)
    big["kv_k"] = _mm_tn(hk, dk_raw, 1, "kv_k_dw")
    big["kv_v"] = _mm_tn(hk, dv, 1, "kv_v_dw")
    big["kv_f"] = _mm_tn(hk, dz, 1, "kv_f_dw")
    dx2, dshk, dsck = _premix_bwd(x2, dhk, mk[1], dx2, "kv_premix_bwd")
    dx1, big["up0"], big["down0"], s_ffn0 = _ffn_backward(dx2, x1, ffn0, m0[5], m0[4], saved0, wts["up0"],
                                                          vecs["conv_w0"], vecs["conv_b0"], wts["down0"], "l0_ffn")
    dy0, dg1_0 = _branch_bwd(dx1, y0, m0[2], "l0_mix_gate_bwd")
    dyp = _mm_nt(dy0, wts["a_w_out"], F32, "l0_out_dx")
    big["a_w_out"] = _mm_tn(yp, dy0, 1, "l0_out_dw")
    dproj, dlb, dng = _hgrn_bwd(proj, lb, vecs["a_norm_g"], o_a, states, dyp, "l0_hgrn_bwd")
    dh0 = _mm_nt(dproj, wts["a_w_in"], F32, "l0_in_dx")
    big["a_w_in"] = _mm_tn(h0, dproj, N_CHIPS, "l0_in_dw")
    grad_x, dsh1_0, dsc1_0 = _premix_bwd(x, dh0, m0[1], dx1, "l0_premix_bwd")

    small["mod_l0"] = [dsh1_0, dsc1_0, dg1_0, s_ffn0["shift"], s_ffn0["scale"], s_ffn0["gate"]]
    small["mod_l1"] = [dsh1_1, dsc1_1, dg1_1, s_ffn1["shift"], s_ffn1["scale"], s_ffn1["gate"]]
    small["mod_kv"] = [dshk, dsck]
    small["conv0"], small["conv1"] = s_ffn0["conv"], s_ffn1["conv"]
    small["a_norm_g"], small["k_norm_g"], small["q_norm_g"] = dng, dkg, dqg
    small["kv_b_f"], small["lb"] = dbf, dlb
    return sq, grad_x, big, small


HBM = pl.BlockSpec(memory_space=pltpu.HBM)
COMM_CHUNK_ELEMS = 256 * 1024


def _place():
    x, y, c = lax.axis_index("x"), lax.axis_index("y"), lax.axis_index("c")
    chips = [(1 - x, y), (x, 1 - y), (1 - x, 1 - y)]
    return x, y, c, (x, y, 1 - c), chips


def _chunk_rows(rows, cols):
    best = BF16_ROWS
    for r in range(BF16_ROWS, rows + 1, BF16_ROWS):
        if rows % r == 0 and r * cols <= COMM_CHUNK_ELEMS:
            best = r
    assert rows % best == 0, (rows, cols)
    return best


def _allgather8(block, name):
    m_per, n = block.shape

    def body(x_ref, out_ref, send_sems, recv_sems, local_sem):
        x, y, c, sibling, chips = _place()
        me = (x, y, c)

        def rows(px, py, pc):
            return out_ref.at[pl.ds((4 * px + 2 * py + pc) * m_per, m_per), :]

        def copy(k, blk, to, src=None):
            return pltpu.make_async_remote_copy(
                src_ref=rows(*blk) if src is None else src, dst_ref=rows(*blk),
                send_sem=send_sems.at[k], recv_sem=recv_sems.at[k], device_id=to, device_id_type=MESH)

        mine = pltpu.make_async_copy(x_ref, rows(*me), local_sem)
        mine.start()
        first = [copy(0, me, sibling, src=x_ref)]
        first += [copy(1 + j, me, (*chip, c), src=x_ref) for j, chip in enumerate(chips)]
        for cp in first:
            cp.start()
        passed = [copy(4 + j, (*chip, c), sibling) for j, chip in enumerate(chips)]
        for j, chip in enumerate(chips):
            copy(1 + j, (*chip, c), me).wait_recv()
            passed[j].start()
        copy(0, sibling, me).wait_recv()
        for j, chip in enumerate(chips):
            copy(4 + j, (*chip, 1 - c), me).wait_recv()
        for cp in first + passed:
            cp.wait_send()
        mine.wait()

    return pl.pallas_call(
        body, name=name, out_shape=jax.ShapeDtypeStruct((N_DEV * m_per, n), block.dtype),
        in_specs=[pl.BlockSpec(memory_space=pltpu.VMEM)], out_specs=pl.BlockSpec(memory_space=pltpu.VMEM),
        scratch_shapes=[pltpu.SemaphoreType.DMA((7,)), pltpu.SemaphoreType.DMA((7,)), pltpu.SemaphoreType.DMA],
    )(block)


def _gather_weights(shards, name):
    n_t = len(shards)
    dims = [s.shape for s in shards]

    def body(*refs):
        ins, outs = refs[:n_t], refs[n_t:2 * n_t]
        send_ici, recv_ici, send_d2d, recv_d2d = refs[2 * n_t:]
        x, y, c, sibling, chips = _place()
        p_me = 2 * x + y

        def halves(t, count):
            return outs[t].at[pl.ds(0, count), pl.ds(0, dims[t][0] // 2), :]

        def waiter(t, sem_s, sem_r):
            win = halves(t, 3)
            return pltpu.make_async_remote_copy(src_ref=win, dst_ref=win, send_sem=sem_s.at[t], recv_sem=sem_r.at[t],
                                                device_id=sibling, device_id_type=MESH)

        def half_copy(t, chip_idx, to, sem_s, sem_r):
            r2 = dims[t][0] // 2
            win = outs[t].at[chip_idx, pl.ds(c * r2, r2), :]
            return pltpu.make_async_remote_copy(src_ref=win, dst_ref=win, send_sem=sem_s.at[t], recv_sem=sem_r.at[t],
                                                device_id=to, device_id_type=MESH)

        for t in range(n_t):
            r, cols = dims[t]
            rows = _chunk_rows(r, cols)

            def cast(fbuf, bbuf, t=t, r=r, rows=rows):
                for k in range(r // rows):
                    pltpu.sync_copy(ins[t].at[pl.ds(k * rows, rows), :], fbuf)
                    bbuf[...] = fbuf[...].astype(BF16)
                    pltpu.sync_copy(bbuf, outs[t].at[p_me, pl.ds(k * rows, rows), :])

            pl.run_scoped(cast, pltpu.VMEM((rows, cols), F32), pltpu.VMEM((rows, cols), BF16))
            for chip in chips:
                half_copy(t, p_me, (*chip, c), send_ici, recv_ici).start()
        for t in range(n_t):
            waiter(t, send_ici, recv_ici).wait_recv()
            for cx, cy in chips:
                half_copy(t, 2 * cx + cy, sibling, send_d2d, recv_d2d).start()
        for t in range(n_t):
            waiter(t, send_d2d, recv_d2d).wait_recv()
            waiter(t, send_ici, recv_ici).wait_send()
            waiter(t, send_d2d, recv_d2d).wait_send()

    return pl.pallas_call(
        body, name=name, in_specs=[HBM] * n_t, out_specs=[HBM] * n_t,
        out_shape=[jax.ShapeDtypeStruct((N_CHIPS, r, cols), BF16) for r, cols in dims],
        scratch_shapes=[pltpu.SemaphoreType.DMA((n_t,))] * 4,
    )(*shards)


def _reduce_scatter(parts, name):
    n_t = len(parts)
    dims = [p.shape[1:] for p in parts]

    def body(*refs):
        ins = refs[:n_t]
        outs, from_sib, chip_sum, from_chips = (refs[(1 + k) * n_t:(2 + k) * n_t] for k in range(4))
        s1, r1, s2, r2, s3, r3 = refs[5 * n_t:]
        x, y, c, sibling, chips = _place()
        p_me = 2 * x + y

        def remote(src, dst, sem_s, sem_r, t, to):
            return pltpu.make_async_remote_copy(src_ref=src, dst_ref=dst, send_sem=sem_s.at[t], recv_sem=sem_r.at[t],
                                                device_id=to, device_id_type=MESH)

        def swap1(t):
            h = dims[t][0] // 2
            return remote(ins[t].at[:, pl.ds((1 - c) * h, h), :], from_sib[t], s1, r1, t, sibling)

        def to_chips(t):
            return remote(from_chips[t], from_chips[t], s2, r2, t, sibling)

        def swap3(t):
            h = dims[t][0] // 2
            win = outs[t].at[pl.ds(c * h, h), :]
            return remote(win, win, s3, r3, t, sibling)

        for t in range(n_t):
            swap1(t).start()
        for t in range(n_t):
            r, cols = dims[t]
            h = r // 2
            rows = _chunk_rows(h, cols)
            swap1(t).wait_recv()

            def pair_sum(a, b, o, t=t, h=h, rows=rows):
                for p in range(N_CHIPS):
                    for k in range(h // rows):
                        pltpu.sync_copy(ins[t].at[p, pl.ds(c * h + k * rows, rows), :], a)
                        pltpu.sync_copy(from_sib[t].at[p, pl.ds(k * rows, rows), :], b)
                        o[...] = (a[...].astype(F32) + b[...].astype(F32)).astype(BF16)
                        pltpu.sync_copy(o, chip_sum[t].at[p, pl.ds(k * rows, rows), :])

            pl.run_scoped(pair_sum, *[pltpu.VMEM((rows, cols), BF16)] * 3)
            for j, (cx, cy) in enumerate(chips):
                remote(chip_sum[t].at[2 * cx + cy], from_chips[t].at[j], s2, r2, t, (cx, cy, c)).start()
        for t in range(n_t):
            r, cols = dims[t]
            h = r // 2
            rows = _chunk_rows(h, cols)
            to_chips(t).wait_recv()

            def total(a, b0, b1, b2, o, t=t, h=h, rows=rows):
                for k in range(h // rows):
                    pltpu.sync_copy(chip_sum[t].at[p_me, pl.ds(k * rows, rows), :], a)
                    for j, b in enumerate((b0, b1, b2)):
                        pltpu.sync_copy(from_chips[t].at[j, pl.ds(k * rows, rows), :], b)
                    o[...] = ((a[...].astype(F32) + b0[...].astype(F32)) + b1[...].astype(F32)) + b2[...].astype(F32)
                    pltpu.sync_copy(o, outs[t].at[pl.ds(c * h + k * rows, rows), :])

            pl.run_scoped(total, *([pltpu.VMEM((rows, cols), BF16)] * 4 + [pltpu.VMEM((rows, cols), F32)]))
            swap3(t).start()
        for t in range(n_t):
            swap3(t).wait_recv()
            swap1(t).wait_send()
            to_chips(t).wait_send()
            swap3(t).wait_send()

    half = lambda n, rc: jax.ShapeDtypeStruct((n, rc[0] // 2, rc[1]), BF16)
    out_shape = ([jax.ShapeDtypeStruct(rc, F32) for rc in dims] + [half(N_CHIPS, rc) for rc in dims]
                 + [half(N_CHIPS, rc) for rc in dims] + [half(N_CHIPS - 1, rc) for rc in dims])
    outs = pl.pallas_call(
        body, name=name, in_specs=[HBM] * n_t, out_specs=[HBM] * (4 * n_t), out_shape=out_shape,
        scratch_shapes=[pltpu.SemaphoreType.DMA((n_t,))] * 6,
    )(*parts)
    return outs[:n_t]


def _cond_rows(c16, w, act, name):
    n_l, dm, wid = w.shape

    def body(c_ref, w_ref, o_ref, a_ref):
        cv = c_ref[...]
        if act:
            cv = cv * _sig(cv)
        a_ref[...] = cv
        o_ref[...] = _dot_f32(cv, w_ref[...])

    return pl.pallas_call(
        body, name=name, grid=(n_l,),
        in_specs=[_full((16, dm)), pl.BlockSpec((None, dm, wid), lambda l: (l, 0, 0))],
        out_specs=[pl.BlockSpec((None, 16, wid), lambda l: (l, 0, 0)), _full((16, dm))],
        out_shape=[jax.ShapeDtypeStruct((n_l, 16, wid), F32), jax.ShapeDtypeStruct((16, dm), F32)],
    )(c16, w)


def _outer_grad(ct, dm, name):
    n_l, kk, wid = dm.shape
    d_rows = ct.shape[0]

    def body(c_ref, d_ref, o_ref):
        o_ref[...] = _dot_f32(c_ref[...], d_ref[...])

    return pl.pallas_call(
        body, name=name, grid=(n_l,),
        in_specs=[_full((d_rows, kk)), pl.BlockSpec((None, kk, wid), lambda l: (l, 0, 0))],
        out_specs=pl.BlockSpec((None, d_rows, wid), lambda l: (l, 0, 0)),
        out_shape=jax.ShapeDtypeStruct((n_l, d_rows, wid), F32),
    )(ct, dm)


def _sum_devices(g, name):
    rows, n = g.shape

    def body(g_ref, o_ref):
        acc = g_ref[0:SUBLANES, :]
        for dev in range(1, N_DEV):
            acc = acc + g_ref[dev * SUBLANES:(dev + 1) * SUBLANES, :]
        o_ref[...] = acc

    return pl.pallas_call(body, name=name, out_shape=jax.ShapeDtypeStruct((SUBLANES, n), F32))(g)


def _adamw(w, g, m, v, name):
    shape = w.shape
    cols = shape[-1]
    rows = w.size // cols
    tr = rows
    for cand in range(SUBLANES, min(rows, 256) + 1, SUBLANES):
        if rows % cand == 0:
            tr = cand
    if rows * cols <= COMM_CHUNK_ELEMS:
        tr = rows
    c1 = 1.0 / (1.0 - ADAM_B1 ** ADAM_STEP)
    c2 = 1.0 / (1.0 - ADAM_B2 ** ADAM_STEP)

    def body(w_ref, g_ref, m_ref, v_ref, d_ref, mo_ref, vo_ref):
        gv = g_ref[...]
        m_new = ADAM_B1 * m_ref[...] + (1.0 - ADAM_B1) * gv
        v_new = ADAM_B2 * v_ref[...] + (1.0 - ADAM_B2) * (gv * gv)
        mo_ref[...] = m_new
        vo_ref[...] = v_new
        d_ref[...] = -ADAM_LR * ((m_new * c1) / (jnp.sqrt(v_new * c2) + ADAM_EPS) + ADAM_WD * w_ref[...])

    spec = pl.BlockSpec((tr, cols), lambda i: (i, 0))
    outs = pl.pallas_call(
        body, name=name, grid=(rows // tr,), in_specs=[spec] * 4, out_specs=[spec] * 3,
        out_shape=[jax.ShapeDtypeStruct((rows, cols), F32)] * 3,
    )(*[a.reshape(rows, cols) for a in (w, g, m, v)])
    return tuple(o.reshape(shape) for o in outs)


def _pad_cols(a, cols):
    return jnp.pad(a, [(0, 0)] * (a.ndim - 1) + [(0, cols - a.shape[-1])])


def _flat8(parts, width):
    v = jnp.concatenate([p.reshape(-1) for p in parts])
    return jnp.pad(v, (0, width - v.shape[0])).reshape(SUBLANES, width // SUBLANES)


KV_SHARD = 514
KV_SHARD_PAD = 640
BIG = ("a_w_in", "a_w_out", "kv_w", "b_w_q", "b_w_out", "up0", "up1", "down0", "down1")


def kernel(x, c, ada_w, ada_b, a_w_in, a_lb_logits, a_norm_g, a_w_out, kv_ada_w, kv_ada_b, kv_w, kv_b_f, k_norm_g, b_w_q, q_norm_g, b_w_out, ffn_w_up, ffn_conv_w, ffn_conv_b, ffn_w_down, loss_target, m_ada_w, m_ada_b, m_a_w_in, m_a_lb_logits, m_a_norm_g, m_a_w_out, m_kv_ada_w, m_kv_ada_b, m_kv_w, m_kv_b_f, m_k_norm_g, m_b_w_q, m_q_norm_g, m_b_w_out, m_ffn_w_up, m_ffn_conv_w, m_ffn_conv_b, m_ffn_w_down, v_ada_w, v_ada_b, v_a_w_in, v_a_lb_logits, v_a_norm_g, v_a_w_out, v_kv_ada_w, v_kv_ada_b, v_kv_w, v_kv_b_f, v_k_norm_g, v_b_w_q, v_q_norm_g, v_b_w_out, v_ffn_w_up, v_ffn_conv_w, v_ffn_conv_b, v_ffn_w_down):
    dm, ff = D_MODEL, D_FF
    ix, iy, ic = lax.axis_index("x"), lax.axis_index("y"), lax.axis_index("c")
    chip = 2 * ix + iy
    dev = 2 * chip + ic

    w1 = 10240
    g1 = _allgather8(_flat8([c, a_lb_logits, ffn_conv_w], w1), "gather_cond").reshape(N_DEV, w1)
    c_all = g1[:, :dm]
    per_chip = g1[0::2]
    lb_logits = per_chip[:, dm:dm + 512].reshape(N_CHIPS, 2, 256).transpose(1, 0, 2).reshape(2, dm)
    conv_w = per_chip[:, dm + 512:dm + 512 + 2 * CONV_W * FFN_COLS].reshape(N_CHIPS, 2, CONV_W, FFN_COLS)
    conv_w = conv_w.transpose(1, 2, 0, 3).reshape(2, CONV_W, 2, ff).transpose(0, 2, 1, 3)
    conv_b = ffn_conv_b.reshape(2, 2, 1, ff)
    lb = jax.nn.softmax(lb_logits, axis=0)[0:1]

    c16 = jnp.pad(c_all, ((0, 8), (0, 0)))
    mod_ada, c_act16 = _cond_rows(c16, ada_w, True, "mod_ada")
    mod_kv, _ = _cond_rows(c16, kv_ada_w[None], True, "mod_kv")
    mine = jnp.concatenate([mod_ada[0, :8], mod_ada[1, :8], mod_kv[0, :8]], axis=1)
    w2 = mine.shape[1]
    g2 = _allgather8(mine, "gather_mod").reshape(N_DEV, 8, w2)[0::2]
    my_rows = lax.dynamic_index_in_dim(g2, dev, axis=1, keepdims=False)
    mod0 = my_rows[:, 0:1536].reshape(6 * dm) + ada_b[0]
    mod1 = my_rows[:, 1536:3072].reshape(6 * dm) + ada_b[1]
    modk = my_rows[:, 3072:3584].reshape(2 * dm) + kv_ada_b
    mods = {"l0": [v.reshape(1, dm) for v in jnp.split(mod0, 6)],
            "l1": [v.reshape(1, dm) for v in jnp.split(mod1, 6)],
            "kv": [v.reshape(1, dm) for v in jnp.split(modk, 2)]}

    local = [a_w_in[0], a_w_out[0], _pad_cols(kv_w, KV_SHARD_PAD), b_w_q[0], b_w_out[0], ffn_w_up[0], ffn_w_up[1],
             ffn_w_down[0], ffn_w_down[1]]
    gathered = dict(zip(BIG, _gather_weights(local, "gather_weights")))
    kv_full = gathered["kv_w"][:, :, :KV_SHARD].transpose(1, 0, 2).reshape(dm, N_CHIPS * KV_SHARD)
    rowwise = lambda g: g.reshape(1, -1, dm)
    wts = {"a_w_in": gathered["a_w_in"], "a_w_out": rowwise(gathered["a_w_out"]),
           "kv_k": kv_full[None, :, :dm], "kv_v": kv_full[None, :, dm:2 * dm],
           "kv_f": _pad_cols(kv_full[None, :, 2 * dm:], LANES),
           "b_w_q": gathered["b_w_q"], "b_w_out": rowwise(gathered["b_w_out"]),
           "up0": gathered["up0"], "up1": gathered["up1"],
           "down0": rowwise(gathered["down0"]), "down1": rowwise(gathered["down1"])}
    vecs = {"a_norm_g": jnp.tile(a_norm_g, (1, HEADS)), "k_norm_g": jnp.tile(k_norm_g[None], (1, HEADS)),
            "q_norm_g": jnp.tile(q_norm_g, (1, HEADS)), "kv_b_f": _pad_cols(kv_b_f[None], LANES),
            "conv_w0": conv_w[0], "conv_b0": conv_b[0], "conv_w1": conv_w[1], "conv_b1": conv_b[1]}

    sq, grad_x, big, small = _local_step(x[0], loss_target[0], mods, lb, vecs, wts)
    loss = lax.psum(0.5 * jnp.sum(sq) / dm, ("x", "y", "c"))

    kv_grad = jnp.concatenate([big["kv_k"][0], big["kv_v"][0], big["kv_f"][0][:, :HEADS]], axis=1)
    kv_grad = _pad_cols(kv_grad.reshape(dm, N_CHIPS, KV_SHARD).transpose(1, 0, 2), KV_SHARD_PAD)
    chipwise = lambda g: g.reshape(N_CHIPS, -1, dm)
    parts = [big["a_w_in"], chipwise(big["a_w_out"]), kv_grad, big["b_w_q"], chipwise(big["b_w_out"]),
             big["up0"], big["up1"], chipwise(big["down0"]), chipwise(big["down1"])]
    rs = dict(zip(BIG, _reduce_scatter(parts, "reduce_grads")))

    fold = lambda a: a.sum(axis=0)
    heads = lambda a: fold(a).reshape(HEADS, HEAD_DIM).sum(axis=0)
    conv_flat = lambda a: a.sum(axis=2).transpose(1, 0, 2)
    pieces = ([fold(a) for a in small["mod_l0"]] + [fold(a) for a in small["mod_l1"]] + [fold(a) for a in small["mod_kv"]]
              + [conv_flat(small["conv0"]), conv_flat(small["conv1"]), heads(small["a_norm_g"]), heads(small["k_norm_g"]),
                 heads(small["q_norm_g"]), fold(small["kv_b_f"]), fold(small["lb"])])
    w3 = 61440
    g3 = _allgather8(_flat8(pieces, w3), "gather_small")
    tot = _sum_devices(g3, "sum_small").reshape(w3)
    n_mod = 14 * dm
    dmod_all = g3.reshape(N_DEV, w3)[:, :n_mod]
    o = n_mod
    conv_tot = [tot[o + l * 8 * ff: o + (l + 1) * 8 * ff].reshape(4, 2 * ff) for l in range(2)]
    o += 16 * ff
    g_a_norm, g_k_norm, g_q_norm = (tot[o + i * HEAD_DIM: o + (i + 1) * HEAD_DIM] for i in range(3))
    o += 3 * HEAD_DIM
    g_kv_b_f = tot[o:o + HEADS]
    dlb = tot[o + LANES:o + LANES + dm]

    ct = _pad_cols(c_act16[:8].T, LANES)
    dmod_pad = jnp.pad(dmod_all, ((0, LANES - N_DEV), (0, 0)))
    cols_ada = jnp.stack([lax.dynamic_slice_in_dim(dmod_pad, l * 6 * dm + chip * 1536, 1536, axis=1) for l in range(2)])
    cols_kv = lax.dynamic_slice_in_dim(dmod_pad, 12 * dm + chip * 512, 512, axis=1)[None]
    g_ada_w = _outer_grad(ct, cols_ada, "grad_ada_w")
    g_kv_ada_w = _outer_grad(ct, cols_kv, "grad_kv_ada_w")[0]

    my_lb = lax.dynamic_slice_in_dim(lb[0], chip * 256, 256)
    l0 = lax.dynamic_slice_in_dim(dlb, chip * 256, 256) * my_lb * (1.0 - my_lb)
    grads = {
        "ada_w": g_ada_w, "ada_b": jnp.stack([tot[:6 * dm], tot[6 * dm:12 * dm]]),
        "a_w_in": rs["a_w_in"][None], "a_lb_logits": jnp.stack([l0, -l0]), "a_norm_g": g_a_norm[None],
        "a_w_out": rs["a_w_out"][None], "kv_ada_w": g_kv_ada_w, "kv_ada_b": tot[12 * dm:14 * dm],
        "kv_w": rs["kv_w"][:, :KV_SHARD], "kv_b_f": g_kv_b_f, "k_norm_g": g_k_norm,
        "b_w_q": rs["b_w_q"][None], "q_norm_g": g_q_norm[None], "b_w_out": rs["b_w_out"][None],
        "ffn_w_up": jnp.stack([rs["up0"], rs["up1"]]),
        "ffn_conv_w": jnp.stack([lax.dynamic_slice_in_dim(ct_l[:CONV_W], chip * FFN_COLS, FFN_COLS, axis=1) for ct_l in conv_tot]),
        "ffn_conv_b": jnp.stack([ct_l[CONV_W] for ct_l in conv_tot]),
        "ffn_w_down": jnp.stack([rs["down0"], rs["down1"]]),
    }
    weights = dict(ada_w=ada_w, ada_b=ada_b, a_w_in=a_w_in, a_lb_logits=a_lb_logits, a_norm_g=a_norm_g, a_w_out=a_w_out,
                   kv_ada_w=kv_ada_w, kv_ada_b=kv_ada_b, kv_w=kv_w, kv_b_f=kv_b_f, k_norm_g=k_norm_g, b_w_q=b_w_q,
                   q_norm_g=q_norm_g, b_w_out=b_w_out, ffn_w_up=ffn_w_up, ffn_conv_w=ffn_conv_w, ffn_conv_b=ffn_conv_b,
                   ffn_w_down=ffn_w_down)
    m_in = dict(ada_w=m_ada_w, ada_b=m_ada_b, a_w_in=m_a_w_in, a_lb_logits=m_a_lb_logits, a_norm_g=m_a_norm_g,
                a_w_out=m_a_w_out, kv_ada_w=m_kv_ada_w, kv_ada_b=m_kv_ada_b, kv_w=m_kv_w, kv_b_f=m_kv_b_f,
                k_norm_g=m_k_norm_g, b_w_q=m_b_w_q, q_norm_g=m_q_norm_g, b_w_out=m_b_w_out, ffn_w_up=m_ffn_w_up,
                ffn_conv_w=m_ffn_conv_w, ffn_conv_b=m_ffn_conv_b, ffn_w_down=m_ffn_w_down)
    v_in = dict(ada_w=v_ada_w, ada_b=v_ada_b, a_w_in=v_a_w_in, a_lb_logits=v_a_lb_logits, a_norm_g=v_a_norm_g,
                a_w_out=v_a_w_out, kv_ada_w=v_kv_ada_w, kv_ada_b=v_kv_ada_b, kv_w=v_kv_w, kv_b_f=v_kv_b_f,
                k_norm_g=v_k_norm_g, b_w_q=v_b_w_q, q_norm_g=v_q_norm_g, b_w_out=v_b_w_out, ffn_w_up=v_ffn_w_up,
                ffn_conv_w=v_ffn_conv_w, ffn_conv_b=v_ffn_conv_b, ffn_w_down=v_ffn_w_down)

    names = list(weights)
    grads = {n: grads[n].reshape(weights[n].shape) for n in names}
    upd = {n: _adamw(weights[n], grads[n], m_in[n], v_in[n], "adamw_" + n) for n in names}
    return (loss, grad_x[None], *[grads[n] for n in names], *[upd[n][0] for n in names],
            *[upd[n][1] for n in names], *[upd[n][2] for n in names])
```

```python
import jax
import jax.numpy as jnp
from jax import lax
from jax.experimental import pallas as pl
from jax.experimental.pallas import tpu as pltpu
from jax.experimental.pallas import tpu_sc as plsc

F32 = jnp.float32
BF16 = jnp.bfloat16

D_MODEL = 1024
HEADS = 8
HEAD_DIM = 128
A_CHUNK = 64
D_FF = 2816
CONV_W = 3
EPS = 1e-6
NEG_INF = -1e30
N_CHIPS = 4
N_DEV = 8

ADAM_LR = 0.001
ADAM_B1 = 0.9
ADAM_B2 = 0.999
ADAM_EPS = 1e-08
ADAM_WD = 0.01
ADAM_STEP = 10

SUBLANES = 8
BF16_ROWS = 16
LANES = 128
HALO = BF16_ROWS
ROW_TILE = 512
FFN_COLS = 1408
HGRN_ROWS = 256
ATT_TILE = 512
ATT_SPLIT = 2
MESH = pl.DeviceIdType.MESH


def _sig(x):
    return jax.nn.sigmoid(x)


def _dot(a, b):
    return jnp.dot(a, b, preferred_element_type=F32)


def _dot_nt(a, b):
    return lax.dot_general(a, b, (((1,), (1,)), ((), ())), preferred_element_type=F32)


def _dot_tn(a, b):
    return lax.dot_general(a, b, (((0,), (0,)), ((), ())), preferred_element_type=F32)


def _split2(x):
    hi = x.astype(BF16)
    lo = (x - hi.astype(F32)).astype(BF16)
    return hi, lo


def _dot_f32(a, b):
    ah, al = _split2(a)
    bh, bl = _split2(b)
    return _dot(ah, bh) + _dot(ah, bl) + _dot(al, bh)


def _tri_dot(tri, x):
    hi = x.astype(BF16)
    r = x - hi.astype(F32)
    mid = r.astype(BF16)
    lo = (r - mid.astype(F32)).astype(BF16)
    return _dot(tri, hi) + _dot(tri, mid) + _dot(tri, lo)


def _tri(n, upper=False):
    r = lax.broadcasted_iota(jnp.int32, (n, n), 0)
    c = lax.broadcasted_iota(jnp.int32, (n, n), 1)
    keep = (c >= r) if upper else (c <= r)
    return jnp.where(keep, 1.0, 0.0).astype(BF16)


def _colsum8(v):
    rows, n = v.shape
    return v.reshape(rows // SUBLANES, SUBLANES, n).sum(axis=0)


def _full(shape):
    nd = len(shape)
    return pl.BlockSpec(shape, lambda *_: (0,) * nd)


def _tile(n, want):
    t = min(n, want)
    assert n % t == 0, (n, t)
    return t


def _mm_nn(a, w, groups, out_dtype, name):
    m_rows, k = a.shape
    p_n, _, n = w.shape
    per = p_n // groups
    tm = _tile(m_rows, ROW_TILE)

    def body(a_ref, w_ref, o_ref):
        av = a_ref[...]
        for p in range(p_n):
            o_ref[p // per, :, (p % per) * n:(p % per + 1) * n] = _dot(av, w_ref[p]).astype(out_dtype)

    return pl.pallas_call(
        body, name=name, grid=(m_rows // tm,),
        in_specs=[pl.BlockSpec((tm, k), lambda i: (i, 0)), _full((p_n, k, n))],
        out_specs=pl.BlockSpec((groups, tm, per * n), lambda i: (0, i, 0)),
        out_shape=jax.ShapeDtypeStruct((groups, m_rows, per * n), out_dtype),
    )(a, w)


def _mm_nt(d, w, out_dtype, name, add=None):
    g_n, m_rows, _ = d.shape
    p_n, k, n = w.shape
    per = p_n // g_n
    tm = _tile(m_rows, ROW_TILE)

    def body(*refs):
        d_ref, w_ref = refs[0], refs[1]
        o_ref = refs[-1]
        acc = refs[2][...] if add is not None else None
        for p in range(p_n):
            t = _dot_nt(d_ref[p // per, :, (p % per) * n:(p % per + 1) * n], w_ref[p])
            acc = t if acc is None else acc + t
        o_ref[...] = acc.astype(out_dtype)

    ins = [d, w] + ([add] if add is not None else [])
    specs = [pl.BlockSpec((g_n, tm, per * n), lambda i: (0, i, 0)), _full((p_n, k, n))]
    if add is not None:
        specs.append(pl.BlockSpec((tm, k), lambda i: (i, 0)))
    return pl.pallas_call(
        body, name=name, grid=(m_rows // tm,), in_specs=specs,
        out_specs=pl.BlockSpec((tm, k), lambda i: (i, 0)),
        out_shape=jax.ShapeDtypeStruct((m_rows, k), out_dtype),
    )(*ins)


def _mm_tn(a, d, p_n, name):
    m_rows, k = a.shape
    g_n, _, w_cols = d.shape
    per = p_n // g_n
    n = w_cols // per
    tm = _tile(m_rows, ROW_TILE)
    steps = m_rows // tm

    def body(a_ref, d_ref, o_ref, acc):
        m = pl.program_id(1)

        @pl.when(m == 0)
        def _():
            acc[...] = jnp.zeros_like(acc)

        acc[...] += _dot_tn(a_ref[...], d_ref[...])

        @pl.when(m == steps - 1)
        def _():
            o_ref[...] = acc[...].astype(BF16)

    return pl.pallas_call(
        body, name=name, grid=(p_n, steps),
        in_specs=[pl.BlockSpec((tm, k), lambda p, m: (m, 0)),
                  pl.BlockSpec((None, tm, n), lambda p, m: (p // per, m, p % per))],
        out_specs=pl.BlockSpec((None, k, n), lambda p, m: (p, 0, 0)),
        out_shape=jax.ShapeDtypeStruct((p_n, k, n), BF16),
        scratch_shapes=[pltpu.VMEM((k, n), F32)],
    )(a, d)


def _premix(x, shift, scale, name, branch=None, gate=None):
    s, dm = x.shape
    tm = _tile(s, ROW_TILE)
    with_branch = branch is not None

    def body(*refs):
        x_ref, sh_ref, sc_ref = refs[:3]
        xv = x_ref[...]
        if with_branch:
            xv = xv + refs[4][...] * refs[3][...]
            refs[-2][...] = xv
        inv = lax.rsqrt(jnp.mean(xv * xv, axis=-1, keepdims=True) + EPS)
        refs[-1][...] = (xv * inv * (1.0 + sc_ref[...]) + sh_ref[...]).astype(BF16)

    row = pl.BlockSpec((tm, dm), lambda i: (i, 0))
    vec = _full((1, dm))
    ins, specs = [x, shift, scale], [row, vec, vec]
    out_shape, out_specs = [jax.ShapeDtypeStruct((s, dm), BF16)], [row]
    if with_branch:
        ins += [branch, gate]
        specs += [row, vec]
        out_shape.insert(0, jax.ShapeDtypeStruct((s, dm), F32))
        out_specs.insert(0, row)
    outs = pl.pallas_call(body, name=name, grid=(s // tm,), in_specs=specs, out_specs=out_specs,
                          out_shape=out_shape)(*ins)
    return tuple(outs) if with_branch else outs[0]


def _premix_bwd(x, dh, scale, dres, name):
    s, dm = x.shape
    tm = _tile(s, ROW_TILE)

    def body(x_ref, dh_ref, sc_ref, dres_ref, dx_ref, dsh_ref, dsc_ref):
        i = pl.program_id(0)

        @pl.when(i == 0)
        def _():
            dsh_ref[...] = jnp.zeros_like(dsh_ref)
            dsc_ref[...] = jnp.zeros_like(dsc_ref)

        xv = x_ref[...]
        dhv = dh_ref[...]
        inv = lax.rsqrt(jnp.mean(xv * xv, axis=-1, keepdims=True) + EPS)
        r = xv * inv
        dr = dhv * (1.0 + sc_ref[...])
        dx_ref[...] = dres_ref[...] + inv * (dr - r * jnp.mean(dr * r, axis=-1, keepdims=True))
        dsh_ref[...] += _colsum8(dhv)
        dsc_ref[...] += _colsum8(dhv * r)

    row = pl.BlockSpec((tm, dm), lambda i: (i, 0))
    acc = _full((SUBLANES, dm))
    return pl.pallas_call(
        body, name=name, grid=(s // tm,), in_specs=[row, row, _full((1, dm)), row],
        out_specs=[row, acc, acc],
        out_shape=[jax.ShapeDtypeStruct((s, dm), F32), jax.ShapeDtypeStruct((SUBLANES, dm), F32),
                   jax.ShapeDtypeStruct((SUBLANES, dm), F32)],
    )(x, dh, scale, dres)


def _branch_bwd(dx, y, gate, name):
    s, dm = dx.shape
    tm = _tile(s, ROW_TILE)

    def body(dx_ref, y_ref, g_ref, dy_ref, dg_ref):
        @pl.when(pl.program_id(0) == 0)
        def _():
            dg_ref[...] = jnp.zeros_like(dg_ref)

        dxv = dx_ref[...]
        dy_ref[0] = (dxv * g_ref[...]).astype(BF16)
        dg_ref[...] += _colsum8(dxv * y_ref[...])

    row = pl.BlockSpec((tm, dm), lambda i: (i, 0))
    return pl.pallas_call(
        body, name=name, grid=(s // tm,), in_specs=[row, row, _full((1, dm))],
        out_specs=[pl.BlockSpec((1, tm, dm), lambda i: (0, i, 0)), _full((SUBLANES, dm))],
        out_shape=[jax.ShapeDtypeStruct((1, s, dm), BF16), jax.ShapeDtypeStruct((SUBLANES, dm), F32)],
    )(dx, y, gate)


def _loss_head(x, branch, gate, target, name):
    s, dm = x.shape
    tm = _tile(s, ROW_TILE)

    def body(x_ref, b_ref, g_ref, t_ref, sq_ref, dy_ref):
        @pl.when(pl.program_id(0) == 0)
        def _():
            sq_ref[...] = jnp.zeros_like(sq_ref)

        err = x_ref[...] + g_ref[...] * b_ref[...] - t_ref[...]
        sq_ref[...] += _colsum8(err * err)
        dy_ref[...] = err * (1.0 / dm)

    row = pl.BlockSpec((tm, dm), lambda i: (i, 0))
    return pl.pallas_call(
        body, name=name, grid=(s // tm,), in_specs=[row, row, _full((1, dm)), row],
        out_specs=[_full((SUBLANES, dm)), row],
        out_shape=[jax.ShapeDtypeStruct((SUBLANES, dm), F32), jax.ShapeDtypeStruct((s, dm), F32)],
    )(x, branch, gate, target)


def _conv_taps(e, w, b):
    return w[2:3] * e + w[1:2] * pltpu.roll(e, 1, 0) + w[0:1] * pltpu.roll(e, 2, 0) + b


def _ffn_specs(s, tm, cb):
    hb = tm // HALO
    last = s // HALO - 1
    main = pl.BlockSpec((2, tm, cb), lambda j, i: (0, i, j))
    prev = pl.BlockSpec((2, HALO, cb), lambda j, i: (0, jnp.maximum(i * hb - 1, 0), j))
    nxt = pl.BlockSpec((2, HALO, cb), lambda j, i: (0, jnp.minimum((i + 1) * hb, last), j))
    wspec = pl.BlockSpec((2, CONV_W, cb), lambda j, i: (0, 0, j))
    bspec = pl.BlockSpec((2, 1, cb), lambda j, i: (0, 0, j))
    return main, prev, nxt, wspec, bspec


def _convglu_fwd(u, w, b, name):
    _, s, f = u.shape
    tm = _tile(s, 256)
    cb = _tile(f, FFN_COLS)
    main, prev, _, wspec, bspec = _ffn_specs(s, tm, cb)

    def body(u_ref, up_ref, w_ref, b_ref, a_ref):
        first = jnp.where(pl.program_id(1) > 0, 1.0, 0.0)

        def conv(g):
            e = jnp.concatenate([up_ref[g].astype(F32) * first, u_ref[g].astype(F32)], axis=0)
            return _conv_taps(e, w_ref[g], b_ref[g])[HALO:]

        gate = conv(0)
        a_ref[...] = (gate * _sig(gate) * conv(1)).astype(BF16)

    return pl.pallas_call(
        body, name=name, grid=(f // cb, s // tm), in_specs=[main, prev, wspec, bspec],
        out_specs=pl.BlockSpec((tm, cb), lambda j, i: (i, j)),
        out_shape=jax.ShapeDtypeStruct((s, f), BF16),
    )(u, u, w, b)


def _convglu_bwd(u, da, w, b, name):
    _, s, f = u.shape
    tm = _tile(s, 256)
    cb = _tile(f, FFN_COLS)
    steps = s // tm
    n_ext = tm + 2 * HALO
    main, prev, nxt, wspec, bspec = _ffn_specs(s, tm, cb)
    hb = tm // HALO
    last = s // HALO - 1
    da_main = pl.BlockSpec((tm, cb), lambda j, i: (i, j))
    da_next = pl.BlockSpec((HALO, cb), lambda j, i: (jnp.minimum((i + 1) * hb, last), j))

    def body(u_ref, up_ref, un_ref, da_ref, dan_ref, w_ref, b_ref, du_ref, acc_ref):
        i = pl.program_id(1)
        first = jnp.where(i > 0, 1.0, 0.0)
        notlast = jnp.where(i < steps - 1, 1.0, 0.0)

        @pl.when(i == 0)
        def _():
            acc_ref[...] = jnp.zeros_like(acc_ref)

        def ext(g):
            return jnp.concatenate([up_ref[g].astype(F32) * first, u_ref[g].astype(F32), un_ref[g].astype(F32)], axis=0)

        ug, uv = ext(0), ext(1)
        gate = _conv_taps(ug, w_ref[0], b_ref[0])
        val = _conv_taps(uv, w_ref[1], b_ref[1])
        da_e = jnp.concatenate([jnp.zeros((HALO, cb), F32), da_ref[...].astype(F32),
                                dan_ref[...].astype(F32) * notlast], axis=0)
        sg = _sig(gate)
        d_val = da_e * gate * sg
        d_gate = da_e * val * (sg * (1.0 + gate * (1.0 - sg)))

        def finish(g, d, e):
            wv = w_ref[g]
            du = wv[2:3] * d + wv[1:2] * pltpu.roll(d, n_ext - 1, 0) + wv[0:1] * pltpu.roll(d, n_ext - 2, 0)
            du_ref[g] = du[HALO:HALO + tm].astype(BF16)
            dm = d[HALO:HALO + tm]
            acc_ref[g, 2] += _colsum8(dm * e[HALO:HALO + tm])
            acc_ref[g, 1] += _colsum8(dm * pltpu.roll(e, 1, 0)[HALO:HALO + tm])
            acc_ref[g, 0] += _colsum8(dm * pltpu.roll(e, 2, 0)[HALO:HALO + tm])
            acc_ref[g, 3] += _colsum8(dm)

        finish(0, d_gate, ug)
        finish(1, d_val, uv)

    return pl.pallas_call(
        body, name=name, grid=(f // cb, steps),
        in_specs=[main, prev, nxt, da_main, da_next, wspec, bspec],
        out_specs=[main, pl.BlockSpec((2, 4, SUBLANES, cb), lambda j, i: (0, 0, 0, j))],
        out_shape=[jax.ShapeDtypeStruct((2, s, f), BF16), jax.ShapeDtypeStruct((2, 4, SUBLANES, f), F32)],
    )(u, u, u, da, da, w, b)


def _hgrn_gates(q_raw, f_raw, lb, tri):
    sf = _sig(f_raw)
    fg = lb + (1.0 - lb) * sf
    b = _tri_dot(tri, jnp.log(fg))
    return q_raw * _sig(q_raw), 1.0 - fg, b, fg, sf


def _hgrn_fwd(proj, lb, norm_g, name):
    s = proj.shape[0]
    tb = _tile(s, HGRN_ROWS)
    n_c = tb // A_CHUNK
    half = A_CHUNK // 2

    def body(q_ref, f_ref, v_ref, g_ref, lb_ref, ng_ref, o_ref, yp_ref, st_ref, state):
        @pl.when(pl.program_id(0) == 0)
        def _():
            state[...] = jnp.zeros_like(state)

        tri = _tri(A_CHUNK)
        causal = lax.broadcasted_iota(jnp.int32, (A_CHUNK, A_CHUNK), 1) <= lax.broadcasted_iota(
            jnp.int32, (A_CHUNK, A_CHUNK), 0)

        def chunk(ci, carry):
            rows = pl.ds(pl.multiple_of(ci * A_CHUNK, A_CHUNK), A_CHUNK)
            for h in range(HEADS):
                cs = slice(h * HEAD_DIM, (h + 1) * HEAD_DIM)
                qs, k, b, _, _ = _hgrn_gates(q_ref[rows, cs], f_ref[rows, cs], lb_ref[:, cs], tri)
                b_mid, b_last = b[half:half + 1], b[A_CHUNK - 1:A_CHUNK]
                vb = v_ref[rows, cs].astype(BF16)
                scores = _dot_nt((qs * jnp.exp(b - b_mid)).astype(BF16), (k * jnp.exp(b_mid - b)).astype(BF16))
                scores = jnp.where(causal, scores, 0.0)
                st = state[h]
                st_ref[ci, h] = st
                o = _dot(scores.astype(BF16), vb) + _dot_nt((qs * jnp.exp(b)).astype(BF16), st.astype(BF16))
                state[h] = st * jnp.exp(b_last) + _dot_tn(vb, (k * jnp.exp(b_last - b)).astype(BF16))
                o_ref[rows, cs] = o
                inv = lax.rsqrt(jnp.mean(o * o, axis=-1, keepdims=True) + EPS)
                g_raw = g_ref[rows, cs]
                yp_ref[rows, cs] = (o * inv * ng_ref[:, cs] * (g_raw * _sig(g_raw))).astype(BF16)
            return carry

        lax.fori_loop(0, n_c, chunk, 0)

    col = lambda j: pl.BlockSpec((tb, D_MODEL), lambda i: (i, j))
    vec = _full((1, D_MODEL))
    return pl.pallas_call(
        body, name=name, grid=(s // tb,), in_specs=[col(0), col(1), col(2), col(3), vec, vec],
        out_specs=[col(0), col(0), pl.BlockSpec((n_c, HEADS, HEAD_DIM, HEAD_DIM), lambda i: (i, 0, 0, 0))],
        out_shape=[jax.ShapeDtypeStruct((s, D_MODEL), F32), jax.ShapeDtypeStruct((s, D_MODEL), BF16),
                   jax.ShapeDtypeStruct((s // A_CHUNK, HEADS, HEAD_DIM, HEAD_DIM), F32)],
        scratch_shapes=[pltpu.VMEM((HEADS, HEAD_DIM, HEAD_DIM), F32)],
    )(proj, proj, proj, proj, lb, norm_g)


def _hgrn_bwd(proj, lb, norm_g, o, states, dyp, name):
    s = proj.shape[0]
    tb = _tile(s, HGRN_ROWS)
    n_c = tb // A_CHUNK
    n_b = s // tb
    half = A_CHUNK // 2

    def body(q_ref, f_ref, v_ref, g_ref, lb_ref, ng_ref, o_ref, st_ref, dyp_ref, dp_ref, dlb_ref, dng_ref, dstate):
        @pl.when(pl.program_id(0) == 0)
        def _():
            dstate[...] = jnp.zeros_like(dstate)
            dlb_ref[...] = jnp.zeros_like(dlb_ref)
            dng_ref[...] = jnp.zeros_like(dng_ref)

        tri = _tri(A_CHUNK)
        tri_up = _tri(A_CHUNK, upper=True)
        row_id = lax.broadcasted_iota(jnp.int32, (A_CHUNK, HEAD_DIM), 0)
        causal = lax.broadcasted_iota(jnp.int32, (A_CHUNK, A_CHUNK), 1) <= lax.broadcasted_iota(
            jnp.int32, (A_CHUNK, A_CHUNK), 0)

        def chunk(cj, carry):
            ci = n_c - 1 - cj
            rows = pl.ds(pl.multiple_of(ci * A_CHUNK, A_CHUNK), A_CHUNK)
            for h in range(HEADS):
                cs = slice(h * HEAD_DIM, (h + 1) * HEAD_DIM)
                q_raw, lbh = q_ref[rows, cs], lb_ref[:, cs]
                qs, k, b, fg, sf = _hgrn_gates(q_raw, f_ref[rows, cs], lbh, tri)
                b_mid, b_last = b[half:half + 1], b[A_CHUNK - 1:A_CHUNK]
                e_qi, e_ki, e_q, e_ks = jnp.exp(b - b_mid), jnp.exp(b_mid - b), jnp.exp(b), jnp.exp(b_last - b)
                q_i, k_i, q_e, k_s = qs * e_qi, k * e_ki, qs * e_q, k * e_ks
                vb = v_ref[rows, cs].astype(BF16)
                scores = jnp.where(causal, _dot_nt(q_i.astype(BF16), k_i.astype(BF16)), 0.0)
                ov, g_raw, dy, ng = o_ref[rows, cs], g_ref[rows, cs], dyp_ref[rows, cs], ng_ref[:, cs]
                inv = lax.rsqrt(jnp.mean(ov * ov, axis=-1, keepdims=True) + EPS)
                nrm = ov * inv
                sg = _sig(g_raw)
                gs = g_raw * sg
                dn = dy * ng * gs
                dng_ref[0:1, cs] += jnp.sum(dy * nrm * gs, axis=0, keepdims=True)
                dg_raw = dy * nrm * ng * (sg * (1.0 + g_raw * (1.0 - sg)))
                do = (inv * (dn - nrm * jnp.mean(dn * nrm, axis=-1, keepdims=True))).astype(BF16)
                st_prev = st_ref[ci, h]
                dst = dstate[h]
                dstb = dst.astype(BF16)
                d_scores = jnp.where(causal, _dot_nt(do, vb), 0.0).astype(BF16)
                dv = _dot_tn(scores.astype(BF16), do) + _dot_nt(k_s.astype(BF16), dstb)
                dq_i = _dot(d_scores, k_i.astype(BF16))
                dk_i = _dot_tn(d_scores, q_i.astype(BF16))
                dq_e = _dot(do, st_prev.astype(BF16))
                dk_s = _dot(vb, dstb)
                d_decay = jnp.sum(st_prev * dst, axis=0, keepdims=True)
                dstate[h] = dst * jnp.exp(b_last) + _dot_tn(do, q_e.astype(BF16))
                dq = dq_i * e_qi + dq_e * e_q
                dk = dk_i * e_ki + dk_s * e_ks
                t_qi, t_ki, t_ks = dq_i * q_i, dk_i * k_i, dk_s * k_s
                db = t_qi - t_ki + dq_e * q_e - t_ks
                db_mid = jnp.sum(t_ki - t_qi, axis=0, keepdims=True)
                db_last = jnp.sum(t_ks, axis=0, keepdims=True) + d_decay * jnp.exp(b_last)
                db = db + jnp.where(row_id == half, db_mid, 0.0) + jnp.where(row_id == A_CHUNK - 1, db_last, 0.0)
                dfg = _tri_dot(tri_up, db) / fg - dk
                dlb_ref[0:1, cs] += jnp.sum(dfg * (1.0 - sf), axis=0, keepdims=True)
                sq = _sig(q_raw)
                dp_ref[0, rows, cs] = (dq * (sq * (1.0 + q_raw * (1.0 - sq)))).astype(BF16)
                dp_ref[1, rows, cs] = (dfg * (1.0 - lbh) * sf * (1.0 - sf)).astype(BF16)
                dp_ref[2, rows, cs] = dv.astype(BF16)
                dp_ref[3, rows, cs] = dg_raw.astype(BF16)
            return carry

        lax.fori_loop(0, n_c, chunk, 0)

    col = lambda j: pl.BlockSpec((tb, D_MODEL), lambda i: (n_b - 1 - i, j))
    vec = _full((1, D_MODEL))
    acc = _full((SUBLANES, D_MODEL))
    return pl.pallas_call(
        body, name=name, grid=(n_b,),
        in_specs=[col(0), col(1), col(2), col(3), vec, vec, col(0),
                  pl.BlockSpec((n_c, HEADS, HEAD_DIM, HEAD_DIM), lambda i: (n_b - 1 - i, 0, 0, 0)), col(0)],
        out_specs=[pl.BlockSpec((4, tb, D_MODEL), lambda i: (0, n_b - 1 - i, 0)), acc, acc],
        out_shape=[jax.ShapeDtypeStruct((4, s, D_MODEL), BF16), jax.ShapeDtypeStruct((SUBLANES, D_MODEL), F32),
                   jax.ShapeDtypeStruct((SUBLANES, D_MODEL), F32)],
        scratch_shapes=[pltpu.VMEM((HEADS, HEAD_DIM, HEAD_DIM), F32)],
    )(proj, proj, proj, proj, lb, norm_g, o, states, dyp)


def _headnorm(x, g, mult, name, col0=0):
    s = x.shape[0]
    tm = _tile(s, ROW_TILE)

    def body(x_ref, g_ref, y_ref):
        for h in range(HEADS):
            cs = slice(h * HEAD_DIM, (h + 1) * HEAD_DIM)
            xv = x_ref[:, cs]
            inv = lax.rsqrt(jnp.mean(xv * xv, axis=-1, keepdims=True) + EPS)
            y_ref[:, cs] = (xv * inv * g_ref[:, cs] * mult).astype(BF16)

    return pl.pallas_call(
        body, name=name, grid=(s // tm,),
        in_specs=[pl.BlockSpec((tm, D_MODEL), lambda i: (i, col0)), _full((1, D_MODEL))],
        out_specs=pl.BlockSpec((tm, D_MODEL), lambda i: (i, 0)),
        out_shape=jax.ShapeDtypeStruct((s, D_MODEL), BF16),
    )(x, g)


def _headnorm_bwd(x, g, mult, dy, name, col0=0, extra=None):
    s = x.shape[0]
    tm = _tile(s, ROW_TILE)
    groups = 2 if extra is not None else 1

    def body(*refs):
        x_ref, g_ref, dy_ref = refs[:3]
        dx_ref, dg_ref = refs[-2:]

        @pl.when(pl.program_id(0) == 0)
        def _():
            dg_ref[...] = jnp.zeros_like(dg_ref)

        for h in range(HEADS):
            cs = slice(h * HEAD_DIM, (h + 1) * HEAD_DIM)
            xv, dyv, gv = x_ref[:, cs], dy_ref[:, cs], g_ref[:, cs]
            inv = lax.rsqrt(jnp.mean(xv * xv, axis=-1, keepdims=True) + EPS)
            nrm = xv * inv
            dn = dyv * gv * mult
            dg_ref[:, cs] += _colsum8(dyv * nrm * mult)
            dx_ref[0, :, cs] = (inv * (dn - nrm * jnp.mean(dn * nrm, axis=-1, keepdims=True))).astype(BF16)
        if extra is not None:
            dx_ref[1] = refs[3][...]

    row = pl.BlockSpec((tm, D_MODEL), lambda i: (i, 0))
    ins = [x, g, dy] + ([extra] if extra is not None else [])
    specs = [pl.BlockSpec((tm, D_MODEL), lambda i: (i, col0)), _full((1, D_MODEL)), row] + ([row] if extra is not None else [])
    return pl.pallas_call(
        body, name=name, grid=(s // tm,), in_specs=specs,
        out_specs=[pl.BlockSpec((groups, tm, D_MODEL), lambda i: (0, i, 0)), _full((SUBLANES, D_MODEL))],
        out_shape=[jax.ShapeDtypeStruct((groups, s, D_MODEL), BF16), jax.ShapeDtypeStruct((SUBLANES, D_MODEL), F32)],
    )(*ins)


def _log_sigmoid(z):
    return jnp.minimum(z, 0.0) - jnp.log(1.0 + jnp.exp(-jnp.abs(z)))


Q_CUM, Q_ONE, Q_LSE = 0, 3, 6
LOG2E = 1.4426950408889634


def _pieces(v):
    hi = v.astype(BF16).astype(F32)
    mid = (v - hi).astype(BF16).astype(F32)
    lo = ((v - hi) - mid).astype(BF16).astype(F32)
    return hi, mid, lo


def _side(lane, at, v):
    hi, mid, lo = _pieces(v)
    return jnp.where(lane == at, hi, jnp.where(lane == at + 1, mid, jnp.where(lane == at + 2, lo, 0.0)))


def _fcum_fwd(f, bias, name):
    s = f.shape[0]
    tm = _tile(s, ROW_TILE)

    def body(f_ref, b_ref, qa_ref, ka_ref, carry):
        @pl.when(pl.program_id(0) == 0)
        def _():
            carry[...] = jnp.zeros_like(carry)

        cum = _tri_dot(_tri(tm), _log_sigmoid(f_ref[...] + b_ref[...])) + carry[...]
        carry[...] = cum[tm - 1:tm]
        lane = lax.broadcasted_iota(jnp.int32, (tm, LANES), 1)
        ones_q = jnp.where((lane >= Q_ONE) & (lane < Q_LSE), 1.0, 0.0)
        ones_k = jnp.where((lane < Q_ONE) | ((lane >= Q_LSE) & (lane < Q_LSE + 3)), 1.0, 0.0)
        for h in range(HEADS):
            c2 = cum[:, h:h + 1] * LOG2E
            qa_ref[h] = (_side(lane, Q_CUM, c2) + ones_q).astype(BF16)
            ka_ref[h] = (_side(lane, Q_ONE, -c2) + ones_k).astype(BF16)

    side = pl.BlockSpec((HEADS, tm, LANES), lambda i: (0, i, 0))
    return pl.pallas_call(
        body, name=name, grid=(s // tm,),
        in_specs=[pl.BlockSpec((tm, LANES), lambda i: (i, 0)), _full((1, LANES))],
        out_specs=[side, side],
        out_shape=[jax.ShapeDtypeStruct((HEADS, s, LANES), BF16)] * 2,
        scratch_shapes=[pltpu.VMEM((1, LANES), F32)],
    )(f, bias)


def _fcum_bwd(f, bias, dka, dqa, name):
    s = f.shape[0]
    tm = _tile(s, ROW_TILE)
    n_b = s // tm

    def body(f_ref, b_ref, dka_ref, dqa_ref, dz_ref, db_ref, carry):
        @pl.when(pl.program_id(0) == 0)
        def _():
            carry[...] = jnp.zeros_like(carry)
            db_ref[...] = jnp.zeros_like(db_ref)

        lane = lax.broadcasted_iota(jnp.int32, (tm, LANES), 1)
        dcum = jnp.zeros((tm, LANES), F32)
        for h in range(HEADS):
            dcum = dcum + jnp.where(lane == h, dqa_ref[h][:, 0:1] - dka_ref[h][:, Q_ONE:Q_ONE + 1], 0.0)
        dlf = _tri_dot(_tri(tm, upper=True), dcum) + carry[...]
        carry[...] = dlf[0:1]
        dz = dlf * _sig(-(f_ref[...] + b_ref[...]))
        dz_ref[0] = dz.astype(BF16)
        db_ref[...] += _colsum8(dz)

    return pl.pallas_call(
        body, name=name, grid=(n_b,),
        in_specs=[pl.BlockSpec((tm, LANES), lambda i: (n_b - 1 - i, 0)), _full((1, LANES)),
                  pl.BlockSpec((HEADS, tm, LANES), lambda i: (0, n_b - 1 - i, 0)),
                  pl.BlockSpec((HEADS, tm, LANES), lambda i: (0, n_b - 1 - i, 0))],
        out_specs=[pl.BlockSpec((1, tm, LANES), lambda i: (0, n_b - 1 - i, 0)), _full((SUBLANES, LANES))],
        out_shape=[jax.ShapeDtypeStruct((1, s, LANES), BF16), jax.ShapeDtypeStruct((SUBLANES, LANES), F32)],
        scratch_shapes=[pltpu.VMEM((1, LANES), F32)],
    )(f, bias, dka, dqa)


def _causal_pairs(n_t, key_major):
    if key_major:
        pairs = [(qi, ki) for ki in range(n_t) for qi in range(ki, n_t)]
    else:
        pairs = [(qi, ki) for qi in range(n_t) for ki in range(qi + 1)]
    return (jnp.array([p[0] for p in pairs], jnp.int32), jnp.array([p[1] for p in pairs], jnp.int32))


def _with_side(main_ref, side_ref):
    return jnp.concatenate([main_ref[...], side_ref[...]], axis=1)


def _lane_const(t, lo, hi, value):
    lane = lax.broadcasted_iota(jnp.int32, (t, LANES), 1)
    return jnp.where((lane >= lo) & (lane < hi), value, 0.0).astype(BF16)


def _att_specs(t):
    qmain = pl.BlockSpec((t, HEAD_DIM), lambda h, p, qt, kt: (qt[p], h))
    kmain = pl.BlockSpec((t, HEAD_DIM), lambda h, p, qt, kt: (kt[p], h))
    qside = pl.BlockSpec((None, t, LANES), lambda h, p, qt, kt: (h, qt[p], 0))
    kside = pl.BlockSpec((None, t, LANES), lambda h, p, qt, kt: (h, kt[p], 0))
    return qmain, kmain, qside, kside


def _fox_fwd(q, qa, k, ka, v, qo, name):
    s = q.shape[0]
    t = _tile(s, ATT_TILE)
    sub = t // ATT_SPLIT
    qt, kt = _causal_pairs(s // t, key_major=False)

    def body(qt_ref, kt_ref, q_ref, qa_ref, k_ref, ka_ref, v_ref, og_ref, o_ref, y_ref, qab_ref, m_s, l_s, acc_s):
        pid = pl.program_id(1)
        qi, ki = qt_ref[pid], kt_ref[pid]

        @pl.when(ki == 0)
        def _():
            m_s[...] = jnp.full_like(m_s, NEG_INF)
            l_s[...] = jnp.zeros_like(l_s)
            acc_s[...] = jnp.zeros_like(acc_s)

        def step(diagonal):
            kc = _with_side(k_ref, ka_ref)
            vc = jnp.concatenate([v_ref[...], _lane_const(t, 0, 1, 1.0)], axis=1)
            for r in range(ATT_SPLIT):
                rows = slice(r * sub, (r + 1) * sub)
                n_k = (r + 1) * sub if diagonal else t
                sc = _dot_nt(jnp.concatenate([q_ref[rows], qa_ref[rows]], axis=1), kc[:n_k])
                if diagonal:
                    sc = jnp.where(lax.broadcasted_iota(jnp.int32, (sub, n_k), 1)
                                   <= lax.broadcasted_iota(jnp.int32, (sub, n_k), 0) + r * sub, sc, NEG_INF)
                m_old = m_s[rows]
                m_new = jnp.maximum(m_old, jnp.max(sc, axis=-1, keepdims=True))
                alpha = jnp.exp2(m_old - m_new)
                pv = _dot(jnp.exp2(sc - m_new[:, 0:1]).astype(BF16), vc[:n_k])
                acc_s[rows] = alpha * acc_s[rows] + pv[:, :HEAD_DIM]
                l_s[rows] = alpha * l_s[rows] + pv[:, HEAD_DIM:]
                m_s[rows] = m_new

        @pl.when(ki < qi)
        def _():
            step(False)

        @pl.when(ki == qi)
        def _():
            step(True)
            l = l_s[:, 0:1]
            o = acc_s[...] / l
            o_ref[...] = o
            y_ref[...] = (o * _sig(og_ref[...])).astype(BF16)
            lane = lax.broadcasted_iota(jnp.int32, (t, LANES), 1)
            qab_ref[...] = qa_ref[...] + _side(lane, Q_LSE, -(m_s[:, 0:1] + jnp.log2(l))).astype(BF16)

    qmain, kmain, qside, kside = _att_specs(t)
    return pl.pallas_call(
        body, name=name,
        grid_spec=pltpu.PrefetchScalarGridSpec(
            num_scalar_prefetch=2, grid=(HEADS, qt.shape[0]),
            in_specs=[qmain, qside, kmain, kside, kmain,
                      pl.BlockSpec((t, HEAD_DIM), lambda h, p, qt, kt: (qt[p], HEADS + h))],
            out_specs=[qmain, qmain, qside],
            scratch_shapes=[pltpu.VMEM((t, LANES), F32), pltpu.VMEM((t, LANES), F32), pltpu.VMEM((t, HEAD_DIM), F32)]),
        out_shape=[jax.ShapeDtypeStruct((s, D_MODEL), F32), jax.ShapeDtypeStruct((s, D_MODEL), BF16),
                   jax.ShapeDtypeStruct((HEADS, s, LANES), BF16)],
    )(qt, kt, q, qa, k, ka, v, qo)


def _fox_gate_bwd(o, qo, dy, name):
    s = o.shape[0]
    tm = _tile(s, ROW_TILE)

    def body(o_ref, og_ref, dy_ref, do_ref, dg_ref, dl_ref):
        ov, dyv = o_ref[...], dy_ref[...]
        sg = _sig(og_ref[...])
        do = (dyv * sg).astype(BF16)
        do_ref[...] = do
        dg_ref[...] = (dyv * ov * sg * (1.0 - sg)).astype(BF16)
        prod = do.astype(F32) * ov
        lane = lax.broadcasted_iota(jnp.int32, (tm, LANES), 1)
        for h in range(HEADS):
            delta = jnp.sum(prod[:, h * HEAD_DIM:(h + 1) * HEAD_DIM], axis=-1, keepdims=True)
            dl_ref[h] = _side(lane, 0, delta).astype(BF16)

    row = pl.BlockSpec((tm, D_MODEL), lambda i: (i, 0))
    return pl.pallas_call(
        body, name=name, grid=(s // tm,),
        in_specs=[row, pl.BlockSpec((tm, D_MODEL), lambda i: (i, 1)), row],
        out_specs=[row, row, pl.BlockSpec((HEADS, tm, LANES), lambda i: (0, i, 0))],
        out_shape=[jax.ShapeDtypeStruct((s, D_MODEL), BF16), jax.ShapeDtypeStruct((s, D_MODEL), BF16),
                   jax.ShapeDtypeStruct((HEADS, s, LANES), BF16)],
    )(o, qo, dy)


def _fox_bwd_kv(q, qab, k, ka, v, do, doa, name):
    s = q.shape[0]
    t = _tile(s, ATT_TILE)
    n_t = s // t
    sub = t // ATT_SPLIT
    qt, kt = _causal_pairs(n_t, key_major=True)

    def body(qt_ref, kt_ref, q_ref, qab_ref, k_ref, ka_ref, v_ref, do_ref, doa_ref, dk_ref, dv_ref, dka_ref, dk_s, dv_s):
        pid = pl.program_id(1)
        qi, ki = qt_ref[pid], kt_ref[pid]

        @pl.when(qi == ki)
        def _():
            dk_s[...] = jnp.zeros_like(dk_s)
            dv_s[...] = jnp.zeros_like(dv_s)

        def step(diagonal):
            kc = _with_side(k_ref, ka_ref)
            vc = jnp.concatenate([v_ref[...], _lane_const(t, 0, 3, -1.0)], axis=1)
            for r in range(ATT_SPLIT):
                cols = slice(r * sub, (r + 1) * sub)
                n_k = (r + 1) * sub if diagonal else t
                qc = jnp.concatenate([q_ref[cols], qab_ref[cols]], axis=1)
                sc = _dot_nt(kc[:n_k], qc)
                if diagonal:
                    sc = jnp.where(lax.broadcasted_iota(jnp.int32, (n_k, sub), 0)
                                   <= lax.broadcasted_iota(jnp.int32, (n_k, sub), 1) + r * sub, sc, NEG_INF)
                p = jnp.exp2(sc)
                dp = _dot_nt(vc[:n_k], jnp.concatenate([do_ref[cols], doa_ref[cols]], axis=1))
                dv_s[0:n_k] += _dot(p.astype(BF16), do_ref[cols])
                dk_s[0:n_k] += _dot((p * dp).astype(BF16), qc)

        @pl.when(qi > ki)
        def _():
            step(False)

        @pl.when(qi == ki)
        def _():
            step(True)

        @pl.when(qi == n_t - 1)
        def _():
            dk_ref[...] = dk_s[:, :HEAD_DIM] * (1.0 / LOG2E)
            dka_ref[...] = dk_s[:, HEAD_DIM:]
            dv_ref[...] = dv_s[...].astype(BF16)

    qmain, kmain, qside, kside = _att_specs(t)
    return pl.pallas_call(
        body, name=name,
        grid_spec=pltpu.PrefetchScalarGridSpec(
            num_scalar_prefetch=2, grid=(HEADS, qt.shape[0]),
            in_specs=[qmain, qside, kmain, kside, kmain, qmain, qside],
            out_specs=[kmain, pl.BlockSpec((None, t, HEAD_DIM), lambda h, p, qt, kt: (0, kt[p], h)), kside],
            scratch_shapes=[pltpu.VMEM((t, 2 * HEAD_DIM), F32), pltpu.VMEM((t, HEAD_DIM), F32)]),
        out_shape=[jax.ShapeDtypeStruct((s, D_MODEL), F32), jax.ShapeDtypeStruct((1, s, D_MODEL), BF16),
                   jax.ShapeDtypeStruct((HEADS, s, LANES), F32)],
    )(qt, kt, q, qab, k, ka, v, do, doa)


def _fox_bwd_q(q, qab, k, ka, v, do, doa, name):
    s = q.shape[0]
    t = _tile(s, ATT_TILE)
    sub = t // ATT_SPLIT
    qt, kt = _causal_pairs(s // t, key_major=False)

    def body(qt_ref, kt_ref, q_ref, qab_ref, k_ref, ka_ref, v_ref, do_ref, doa_ref, dq_ref, dqa_ref, dq_s):
        pid = pl.program_id(1)
        qi, ki = qt_ref[pid], kt_ref[pid]

        @pl.when(ki == 0)
        def _():
            dq_s[...] = jnp.zeros_like(dq_s)

        def step(diagonal):
            kc = _with_side(k_ref, ka_ref)
            vc = jnp.concatenate([v_ref[...], _lane_const(t, 0, 3, -1.0)], axis=1)
            for r in range(ATT_SPLIT):
                rows = slice(r * sub, (r + 1) * sub)
                n_k = (r + 1) * sub if diagonal else t
                sc = _dot_nt(jnp.concatenate([q_ref[rows], qab_ref[rows]], axis=1), kc[:n_k])
                if diagonal:
                    sc = jnp.where(lax.broadcasted_iota(jnp.int32, (sub, n_k), 1)
                                   <= lax.broadcasted_iota(jnp.int32, (sub, n_k), 0) + r * sub, sc, NEG_INF)
                dp = _dot_nt(jnp.concatenate([do_ref[rows], doa_ref[rows]], axis=1), vc[:n_k])
                dq_s[rows] += _dot((jnp.exp2(sc) * dp).astype(BF16), kc[:n_k])

        @pl.when(ki < qi)
        def _():
            step(False)

        @pl.when(ki == qi)
        def _():
            step(True)
            dq_ref[...] = dq_s[:, :HEAD_DIM]
            dqa_ref[...] = dq_s[:, HEAD_DIM:]

    qmain, kmain, qside, kside = _att_specs(t)
    return pl.pallas_call(
        body, name=name,
        grid_spec=pltpu.PrefetchScalarGridSpec(
            num_scalar_prefetch=2, grid=(HEADS, qt.shape[0]),
            in_specs=[qmain, qside, kmain, kside, kmain, qmain, qside],
            out_specs=[qmain, qside],
            scratch_shapes=[pltpu.VMEM((t, 2 * HEAD_DIM), F32)]),
        out_shape=[jax.ShapeDtypeStruct((s, D_MODEL), F32), jax.ShapeDtypeStruct((HEADS, s, LANES), F32)],
    )(qt, kt, q, qab, k, ka, v, do, doa)


def _ffn_forward(x_in, branch, gate, shift, scale, w_up, conv_w, conv_b, w_down, tag):
    x_mid, h = _premix(x_in, shift, scale, tag + "_premix", branch=branch, gate=gate)
    u = _mm_nn(h, w_up, 2, BF16, tag + "_up")
    a = _convglu_fwd(u, conv_w, conv_b, tag + "_convglu")
    ffn = _mm_nn(a, w_down, 1, F32, tag + "_down")[0]
    return x_mid, ffn, (h, u, a)


def _ffn_backward(dx_out, x_mid, ffn, gate, scale, saved, w_up, conv_w, conv_b, w_down, tag):
    h, u, a = saved
    dffn, dgate = _branch_bwd(dx_out, ffn, gate, tag + "_gate_bwd")
    da = _mm_nt(dffn, w_down, BF16, tag + "_down_dx")
    dw_down = _mm_tn(a, dffn, 1, tag + "_down_dw")
    du, dconv = _convglu_bwd(u, da, conv_w, conv_b, tag + "_convglu_bwd")
    dh = _mm_nt(du, w_up, F32, tag + "_up_dx")
    dw_up = _mm_tn(h, du, N_CHIPS, tag + "_up_dw")
    dx_mid, dshift, dscale = _premix_bwd(x_mid, dh, scale, dx_out, tag + "_premix_bwd")
    return dx_mid, dw_up, dw_down, dict(gate=dgate, shift=dshift, scale=dscale, conv=dconv)


def _local_step(x, target, mods, lb, vecs, wts):
    m0, m1, mk = mods["l0"], mods["l1"], mods["kv"]
    h0 = _premix(x, m0[0], m0[1], "l0_premix")
    proj = _mm_nn(h0, wts["a_w_in"], 1, F32, "l0_in")[0]
    o_a, yp, states = _hgrn_fwd(proj, lb, vecs["a_norm_g"], "l0_hgrn")
    y0 = _mm_nn(yp, wts["a_w_out"], 1, F32, "l0_out")[0]
    x1, ffn0, saved0 = _ffn_forward(x, y0, m0[2], m0[3], m0[4], wts["up0"], vecs["conv_w0"], vecs["conv_b0"],
                                    wts["down0"], "l0_ffn")
    x2, hk = _premix(x1, mk[0], mk[1], "kv_premix", branch=ffn0, gate=m0[5])
    k_raw = _mm_nn(hk, wts["kv_k"], 1, F32, "kv_k")[0]
    v_sh = _mm_nn(hk, wts["kv_v"], 1, BF16, "kv_v")[0]
    f_raw = _mm_nn(hk, wts["kv_f"], 1, F32, "kv_f")[0]
    k_sh = _headnorm(k_raw, vecs["k_norm_g"], 1.0, "kv_knorm")
    qa, ka = _fcum_fwd(f_raw, vecs["kv_b_f"], "kv_fcum")
    h1 = _premix(x2, m1[0], m1[1], "l1_premix")
    qo = _mm_nn(h1, wts["b_w_q"], 1, F32, "l1_q")[0]
    q_scale = HEAD_DIM ** -0.5
    q = _headnorm(qo, vecs["q_norm_g"], q_scale * LOG2E, "l1_qnorm")
    o_b, og, qab = _fox_fwd(q, qa, k_sh, ka, v_sh, qo, "l1_fox")
    y1 = _mm_nn(og, wts["b_w_out"], 1, F32, "l1_out")[0]
    x3, ffn1, saved1 = _ffn_forward(x2, y1, m1[2], m1[3], m1[4], wts["up1"], vecs["conv_w1"], vecs["conv_b1"],
                                    wts["down1"], "l1_ffn")
    sq, dx4 = _loss_head(x3, ffn1, m1[5], target, "loss_head")

    big, small = {}, {}
    dx3, big["up1"], big["down1"], s_ffn1 = _ffn_backward(dx4, x3, ffn1, m1[5], m1[4], saved1, wts["up1"],
                                                          vecs["conv_w1"], vecs["conv_b1"], wts["down1"], "l1_ffn")
    dy1, dg1_1 = _branch_bwd(dx3, y1, m1[2], "l1_mix_gate_bwd")
    d_og = _mm_nt(dy1, wts["b_w_out"], F32, "l1_out_dx")
    big["b_w_out"] = _mm_tn(og, dy1, 1, "l1_out_dw")
    do_b, dgate_b, doa = _fox_gate_bwd(o_b, qo, d_og, "l1_fox_gate_bwd")
    dk, dv, dka = _fox_bwd_kv(q, qab, k_sh, ka, v_sh, do_b, doa, "l1_fox_bwd_kv")
    dq, dqa = _fox_bwd_q(q, qab, k_sh, ka, v_sh, do_b, doa, "l1_fox_bwd_q")
    dqo, dqg = _headnorm_bwd(qo, vecs["q_norm_g"], q_scale, dq, "l1_qnorm_bwd", extra=dgate_b)
    dh1 = _mm_nt(dqo, wts["b_w_q"], F32, "l1_q_dx")
    big["b_w_q"] = _mm_tn(h1, dqo, N_CHIPS, "l1_q_dw")
    dx2, dsh1_1, dsc1_1 = _premix_bwd(x2, dh1, m1[1], dx3, "l1_premix_bwd")
    dk_raw, dkg = _headnorm_bwd(k_raw, vecs["k_norm_g"], 1.0, dk, "kv_knorm_bwd")
    dz, dbf = _fcum_bwd(f_raw, vecs["kv_b_f"], dka, dqa, "kv_fcum_bwd")
    dhk = _mm_nt(dk_raw, wts["kv_k"], F32, "kv_k_dx")
    dhk = _mm_nt(dv, wts["kv_v"], F32, "kv_v_dx", add=dhk)
    dhk = _mm_nt(dz, wts["kv_f"], F32, "kv_f_dx", add=dhk)
    big["kv_k"] = _mm_tn(hk, dk_raw, 1, "kv_k_dw")
    big["kv_v"] = _mm_tn(hk, dv, 1, "kv_v_dw")
    big["kv_f"] = _mm_tn(hk, dz, 1, "kv_f_dw")
    dx2, dshk, dsck = _premix_bwd(x2, dhk, mk[1], dx2, "kv_premix_bwd")
    dx1, big["up0"], big["down0"], s_ffn0 = _ffn_backward(dx2, x1, ffn0, m0[5], m0[4], saved0, wts["up0"],
                                                          vecs["conv_w0"], vecs["conv_b0"], wts["down0"], "l0_ffn")
    dy0, dg1_0 = _branch_bwd(dx1, y0, m0[2], "l0_mix_gate_bwd")
    dyp = _mm_nt(dy0, wts["a_w_out"], F32, "l0_out_dx")
    big["a_w_out"] = _mm_tn(yp, dy0, 1, "l0_out_dw")
    dproj, dlb, dng = _hgrn_bwd(proj, lb, vecs["a_norm_g"], o_a, states, dyp, "l0_hgrn_bwd")
    dh0 = _mm_nt(dproj, wts["a_w_in"], F32, "l0_in_dx")
    big["a_w_in"] = _mm_tn(h0, dproj, N_CHIPS, "l0_in_dw")
    grad_x, dsh1_0, dsc1_0 = _premix_bwd(x, dh0, m0[1], dx1, "l0_premix_bwd")

    small["mod_l0"] = [dsh1_0, dsc1_0, dg1_0, s_ffn0["shift"], s_ffn0["scale"], s_ffn0["gate"]]
    small["mod_l1"] = [dsh1_1, dsc1_1, dg1_1, s_ffn1["shift"], s_ffn1["scale"], s_ffn1["gate"]]
    small["mod_kv"] = [dshk, dsck]
    small["conv0"], small["conv1"] = s_ffn0["conv"], s_ffn1["conv"]
    small["a_norm_g"], small["k_norm_g"], small["q_norm_g"] = dng, dkg, dqg
    small["kv_b_f"], small["lb"] = dbf, dlb
    return sq, grad_x, big, small


HBM = pl.BlockSpec(memory_space=pltpu.HBM)
COMM_CHUNK_ELEMS = 256 * 1024


def _place():
    x, y, c = lax.axis_index("x"), lax.axis_index("y"), lax.axis_index("c")
    chips = [(1 - x, y), (x, 1 - y), (1 - x, 1 - y)]
    return x, y, c, (x, y, 1 - c), chips


def _chunk_rows(rows, cols):
    best = BF16_ROWS
    for r in range(BF16_ROWS, rows + 1, BF16_ROWS):
        if rows % r == 0 and r * cols <= COMM_CHUNK_ELEMS:
            best = r
    assert rows % best == 0, (rows, cols)
    return best


def _allgather8(block, name):
    m_per, n = block.shape

    def body(x_ref, out_ref, send_sems, recv_sems, local_sem):
        x, y, c, sibling, chips = _place()
        me = (x, y, c)

        def rows(px, py, pc):
            return out_ref.at[pl.ds((4 * px + 2 * py + pc) * m_per, m_per), :]

        def copy(k, blk, to, src=None):
            return pltpu.make_async_remote_copy(
                src_ref=rows(*blk) if src is None else src, dst_ref=rows(*blk),
                send_sem=send_sems.at[k], recv_sem=recv_sems.at[k], device_id=to, device_id_type=MESH)

        mine = pltpu.make_async_copy(x_ref, rows(*me), local_sem)
        mine.start()
        first = [copy(0, me, sibling, src=x_ref)]
        first += [copy(1 + j, me, (*chip, c), src=x_ref) for j, chip in enumerate(chips)]
        for cp in first:
            cp.start()
        passed = [copy(4 + j, (*chip, c), sibling) for j, chip in enumerate(chips)]
        for j, chip in enumerate(chips):
            copy(1 + j, (*chip, c), me).wait_recv()
            passed[j].start()
        copy(0, sibling, me).wait_recv()
        for j, chip in enumerate(chips):
            copy(4 + j, (*chip, 1 - c), me).wait_recv()
        for cp in first + passed:
            cp.wait_send()
        mine.wait()

    return pl.pallas_call(
        body, name=name, out_shape=jax.ShapeDtypeStruct((N_DEV * m_per, n), block.dtype),
        in_specs=[pl.BlockSpec(memory_space=pltpu.VMEM)], out_specs=pl.BlockSpec(memory_space=pltpu.VMEM),
        scratch_shapes=[pltpu.SemaphoreType.DMA((7,)), pltpu.SemaphoreType.DMA((7,)), pltpu.SemaphoreType.DMA],
    )(block)


def _gather_weights(shards, name):
    n_t = len(shards)
    dims = [s.shape for s in shards]

    def body(*refs):
        ins, outs = refs[:n_t], refs[n_t:2 * n_t]
        send_ici, recv_ici, send_d2d, recv_d2d = refs[2 * n_t:]
        x, y, c, sibling, chips = _place()
        p_me = 2 * x + y

        def halves(t, count):
            return outs[t].at[pl.ds(0, count), pl.ds(0, dims[t][0] // 2), :]

        def waiter(t, sem_s, sem_r):
            win = halves(t, 3)
            return pltpu.make_async_remote_copy(src_ref=win, dst_ref=win, send_sem=sem_s.at[t], recv_sem=sem_r.at[t],
                                                device_id=sibling, device_id_type=MESH)

        def half_copy(t, chip_idx, to, sem_s, sem_r):
            r2 = dims[t][0] // 2
            win = outs[t].at[chip_idx, pl.ds(c * r2, r2), :]
            return pltpu.make_async_remote_copy(src_ref=win, dst_ref=win, send_sem=sem_s.at[t], recv_sem=sem_r.at[t],
                                                device_id=to, device_id_type=MESH)

        for t in range(n_t):
            r, cols = dims[t]
            rows = _chunk_rows(r, cols)

            def cast(fbuf, bbuf, t=t, r=r, rows=rows):
                for k in range(r // rows):
                    pltpu.sync_copy(ins[t].at[pl.ds(k * rows, rows), :], fbuf)
                    bbuf[...] = fbuf[...].astype(BF16)
                    pltpu.sync_copy(bbuf, outs[t].at[p_me, pl.ds(k * rows, rows), :])

            pl.run_scoped(cast, pltpu.VMEM((rows, cols), F32), pltpu.VMEM((rows, cols), BF16))
            for chip in chips:
                half_copy(t, p_me, (*chip, c), send_ici, recv_ici).start()
        for t in range(n_t):
            waiter(t, send_ici, recv_ici).wait_recv()
            for cx, cy in chips:
                half_copy(t, 2 * cx + cy, sibling, send_d2d, recv_d2d).start()
        for t in range(n_t):
            waiter(t, send_d2d, recv_d2d).wait_recv()
            waiter(t, send_ici, recv_ici).wait_send()
            waiter(t, send_d2d, recv_d2d).wait_send()

    return pl.pallas_call(
        body, name=name, in_specs=[HBM] * n_t, out_specs=[HBM] * n_t,
        out_shape=[jax.ShapeDtypeStruct((N_CHIPS, r, cols), BF16) for r, cols in dims],
        scratch_shapes=[pltpu.SemaphoreType.DMA((n_t,))] * 4,
    )(*shards)


def _cast_weights(shards, name):
    n_t = len(shards)
    dims = [s.shape for s in shards]

    def body(*refs):
        ins, outs = refs[:n_t], refs[n_t:]
        p_me = 2 * lax.axis_index("x") + lax.axis_index("y")
        for t in range(n_t):
            r, cols = dims[t]
            rows = _chunk_rows(r, cols)

            def cast(fbuf, bbuf, t=t, r=r, rows=rows):
                for k in range(r // rows):
                    pltpu.sync_copy(ins[t].at[pl.ds(k * rows, rows), :], fbuf)
                    bbuf[...] = fbuf[...].astype(BF16)
                    pltpu.sync_copy(bbuf, outs[t].at[p_me, pl.ds(k * rows, rows), :])

            pl.run_scoped(cast, pltpu.VMEM((rows, cols), F32), pltpu.VMEM((rows, cols), BF16))

    return pl.pallas_call(
        body, name=name, in_specs=[HBM] * n_t, out_specs=[HBM] * n_t,
        out_shape=[jax.ShapeDtypeStruct((N_CHIPS, r, cols), BF16) for r, cols in dims],
    )(*shards)


def _sequencer_gather(bufs, name, collective_id):
    n_t = len(bufs)
    dims = [b.shape[1:] for b in bufs]
    refs = [jax.new_ref(b, memory_space=pltpu.MemorySpace.HBM) for b in bufs]

    @pl.kernel(mesh=plsc.ScalarSubcoreMesh(axis_name="sequencer", num_cores=1), name=name,
               scratch_types=[pltpu.SemaphoreType.DMA((n_t,))] * 4,
               compiler_params=pltpu.CompilerParams(collective_id=collective_id))
    def launch(send_ici, recv_ici, send_d2d, recv_d2d):
        x, y, c, sibling, chips = _place()
        p_me = 2 * x + y
        peers = [sibling] + [(cx, cy, c) for cx, cy in chips]
        barrier = pltpu.get_barrier_semaphore()
        for peer in peers:
            pl.semaphore_signal(barrier, inc=1, device_id=peer, device_id_type=MESH)
        pl.semaphore_wait(barrier, len(peers))

        def waiter(t, sem_s, sem_r):
            win = refs[t].at[pl.ds(0, 3), pl.ds(0, dims[t][0] // 2), :]
            return pltpu.make_async_remote_copy(src_ref=win, dst_ref=win, send_sem=sem_s.at[t], recv_sem=sem_r.at[t],
                                                device_id=sibling, device_id_type=MESH)

        def half_copy(t, chip_idx, to, sem_s, sem_r):
            r2 = dims[t][0] // 2
            win = refs[t].at[chip_idx, pl.ds(c * r2, r2), :]
            return pltpu.make_async_remote_copy(src_ref=win, dst_ref=win, send_sem=sem_s.at[t], recv_sem=sem_r.at[t],
                                                device_id=to, device_id_type=MESH)

        for t in range(n_t):
            for cx, cy in chips:
                half_copy(t, p_me, (cx, cy, c), send_ici, recv_ici).start()
        for t in range(n_t):
            waiter(t, send_ici, recv_ici).wait_recv()
            for cx, cy in chips:
                half_copy(t, 2 * cx + cy, sibling, send_d2d, recv_d2d).start()
        for t in range(n_t):
            waiter(t, send_d2d, recv_d2d).wait_recv()
            waiter(t, send_ici, recv_ici).wait_send()
            waiter(t, send_d2d, recv_d2d).wait_send()

    launch()
    return [r[...] for r in refs]


def _reduce_scatter(parts, name):
    n_t = len(parts)
    dims = [p.shape[1:] for p in parts]

    def body(*refs):
        ins = refs[:n_t]
        outs, from_sib, chip_sum, from_chips = (refs[(1 + k) * n_t:(2 + k) * n_t] for k in range(4))
        s1, r1, s2, r2, s3, r3 = refs[5 * n_t:]
        x, y, c, sibling, chips = _place()
        p_me = 2 * x + y

        def remote(src, dst, sem_s, sem_r, t, to):
            return pltpu.make_async_remote_copy(src_ref=src, dst_ref=dst, send_sem=sem_s.at[t], recv_sem=sem_r.at[t],
                                                device_id=to, device_id_type=MESH)

        def swap1(t):
            h = dims[t][0] // 2
            return remote(ins[t].at[:, pl.ds((1 - c) * h, h), :], from_sib[t], s1, r1, t, sibling)

        def to_chips(t):
            return remote(from_chips[t], from_chips[t], s2, r2, t, sibling)

        def swap3(t):
            h = dims[t][0] // 2
            win = outs[t].at[pl.ds(c * h, h), :]
            return remote(win, win, s3, r3, t, sibling)

        for t in range(n_t):
            swap1(t).start()
        for t in range(n_t):
            r, cols = dims[t]
            h = r // 2
            rows = _chunk_rows(h, cols)
            swap1(t).wait_recv()

            def pair_sum(a, b, o, t=t, h=h, rows=rows):
                for p in range(N_CHIPS):
                    for k in range(h // rows):
                        pltpu.sync_copy(ins[t].at[p, pl.ds(c * h + k * rows, rows), :], a)
                        pltpu.sync_copy(from_sib[t].at[p, pl.ds(k * rows, rows), :], b)
                        o[...] = (a[...].astype(F32) + b[...].astype(F32)).astype(BF16)
                        pltpu.sync_copy(o, chip_sum[t].at[p, pl.ds(k * rows, rows), :])

            pl.run_scoped(pair_sum, *[pltpu.VMEM((rows, cols), BF16)] * 3)
            for j, (cx, cy) in enumerate(chips):
                remote(chip_sum[t].at[2 * cx + cy], from_chips[t].at[j], s2, r2, t, (cx, cy, c)).start()
        for t in range(n_t):
            r, cols = dims[t]
            h = r // 2
            rows = _chunk_rows(h, cols)
            to_chips(t).wait_recv()

            def total(a, b0, b1, b2, o, t=t, h=h, rows=rows):
                for k in range(h // rows):
                    pltpu.sync_copy(chip_sum[t].at[p_me, pl.ds(k * rows, rows), :], a)
                    for j, b in enumerate((b0, b1, b2)):
                        pltpu.sync_copy(from_chips[t].at[j, pl.ds(k * rows, rows), :], b)
                    o[...] = ((a[...].astype(F32) + b0[...].astype(F32)) + b1[...].astype(F32)) + b2[...].astype(F32)
                    pltpu.sync_copy(o, outs[t].at[pl.ds(c * h + k * rows, rows), :])

            pl.run_scoped(total, *([pltpu.VMEM((rows, cols), BF16)] * 4 + [pltpu.VMEM((rows, cols), F32)]))
            swap3(t).start()
        for t in range(n_t):
            swap3(t).wait_recv()
            swap1(t).wait_send()
            to_chips(t).wait_send()
            swap3(t).wait_send()

    half = lambda n, rc: jax.ShapeDtypeStruct((n, rc[0] // 2, rc[1]), BF16)
    out_shape = ([jax.ShapeDtypeStruct(rc, F32) for rc in dims] + [half(N_CHIPS, rc) for rc in dims]
                 + [half(N_CHIPS, rc) for rc in dims] + [half(N_CHIPS - 1, rc) for rc in dims])
    outs = pl.pallas_call(
        body, name=name, in_specs=[HBM] * n_t, out_specs=[HBM] * (4 * n_t), out_shape=out_shape,
        scratch_shapes=[pltpu.SemaphoreType.DMA((n_t,))] * 6,
    )(*parts)
    return outs[:n_t]


def _cond_rows(c16, w, act, name):
    n_l, dm, wid = w.shape

    def body(c_ref, w_ref, o_ref, a_ref):
        cv = c_ref[...]
        if act:
            cv = cv * _sig(cv)
        a_ref[...] = cv
        o_ref[...] = _dot_f32(cv, w_ref[...])

    return pl.pallas_call(
        body, name=name, grid=(n_l,),
        in_specs=[_full((16, dm)), pl.BlockSpec((None, dm, wid), lambda l: (l, 0, 0))],
        out_specs=[pl.BlockSpec((None, 16, wid), lambda l: (l, 0, 0)), _full((16, dm))],
        out_shape=[jax.ShapeDtypeStruct((n_l, 16, wid), F32), jax.ShapeDtypeStruct((16, dm), F32)],
    )(c16, w)


def _outer_grad(ct, dm, name):
    n_l, kk, wid = dm.shape
    d_rows = ct.shape[0]

    def body(c_ref, d_ref, o_ref):
        o_ref[...] = _dot_f32(c_ref[...], d_ref[...])

    return pl.pallas_call(
        body, name=name, grid=(n_l,),
        in_specs=[_full((d_rows, kk)), pl.BlockSpec((None, kk, wid), lambda l: (l, 0, 0))],
        out_specs=pl.BlockSpec((None, d_rows, wid), lambda l: (l, 0, 0)),
        out_shape=jax.ShapeDtypeStruct((n_l, d_rows, wid), F32),
    )(ct, dm)


def _sum_devices(g, name):
    rows, n = g.shape

    def body(g_ref, o_ref):
        acc = g_ref[0:SUBLANES, :]
        for dev in range(1, N_DEV):
            acc = acc + g_ref[dev * SUBLANES:(dev + 1) * SUBLANES, :]
        o_ref[...] = acc

    return pl.pallas_call(body, name=name, out_shape=jax.ShapeDtypeStruct((SUBLANES, n), F32))(g)


def _adamw(w, g, m, v, name):
    shape = w.shape
    cols = shape[-1]
    rows = w.size // cols
    tr = rows
    for cand in range(SUBLANES, min(rows, 256) + 1, SUBLANES):
        if rows % cand == 0:
            tr = cand
    if rows * cols <= COMM_CHUNK_ELEMS:
        tr = rows
    c1 = 1.0 / (1.0 - ADAM_B1 ** ADAM_STEP)
    c2 = 1.0 / (1.0 - ADAM_B2 ** ADAM_STEP)

    def body(w_ref, g_ref, m_ref, v_ref, d_ref, mo_ref, vo_ref):
        gv = g_ref[...]
        m_new = ADAM_B1 * m_ref[...] + (1.0 - ADAM_B1) * gv
        v_new = ADAM_B2 * v_ref[...] + (1.0 - ADAM_B2) * (gv * gv)
        mo_ref[...] = m_new
        vo_ref[...] = v_new
        d_ref[...] = -ADAM_LR * ((m_new * c1) / (jnp.sqrt(v_new * c2) + ADAM_EPS) + ADAM_WD * w_ref[...])

    spec = pl.BlockSpec((tr, cols), lambda i: (i, 0))
    outs = pl.pallas_call(
        body, name=name, grid=(rows // tr,), in_specs=[spec] * 4, out_specs=[spec] * 3,
        out_shape=[jax.ShapeDtypeStruct((rows, cols), F32)] * 3,
    )(*[a.reshape(rows, cols) for a in (w, g, m, v)])
    return tuple(o.reshape(shape) for o in outs)


def _pad_cols(a, cols):
    return jnp.pad(a, [(0, 0)] * (a.ndim - 1) + [(0, cols - a.shape[-1])])


def _flat8(parts, width):
    v = jnp.concatenate([p.reshape(-1) for p in parts])
    return jnp.pad(v, (0, width - v.shape[0])).reshape(SUBLANES, width // SUBLANES)


KV_SHARD = 514
KV_SHARD_PAD = 640
BIG = ("a_w_in", "a_w_out", "kv_w", "b_w_q", "b_w_out", "up0", "up1", "down0", "down1")


def kernel(x, c, ada_w, ada_b, a_w_in, a_lb_logits, a_norm_g, a_w_out, kv_ada_w, kv_ada_b, kv_w, kv_b_f, k_norm_g, b_w_q, q_norm_g, b_w_out, ffn_w_up, ffn_conv_w, ffn_conv_b, ffn_w_down, loss_target, m_ada_w, m_ada_b, m_a_w_in, m_a_lb_logits, m_a_norm_g, m_a_w_out, m_kv_ada_w, m_kv_ada_b, m_kv_w, m_kv_b_f, m_k_norm_g, m_b_w_q, m_q_norm_g, m_b_w_out, m_ffn_w_up, m_ffn_conv_w, m_ffn_conv_b, m_ffn_w_down, v_ada_w, v_ada_b, v_a_w_in, v_a_lb_logits, v_a_norm_g, v_a_w_out, v_kv_ada_w, v_kv_ada_b, v_kv_w, v_kv_b_f, v_k_norm_g, v_b_w_q, v_q_norm_g, v_b_w_out, v_ffn_w_up, v_ffn_conv_w, v_ffn_conv_b, v_ffn_w_down):
    dm, ff = D_MODEL, D_FF
    ix, iy, ic = lax.axis_index("x"), lax.axis_index("y"), lax.axis_index("c")
    chip = 2 * ix + iy
    dev = 2 * chip + ic

    w1 = 10240
    g1 = _allgather8(_flat8([c, a_lb_logits, ffn_conv_w], w1), "gather_cond").reshape(N_DEV, w1)
    c_all = g1[:, :dm]
    per_chip = g1[0::2]
    lb_logits = per_chip[:, dm:dm + 512].reshape(N_CHIPS, 2, 256).transpose(1, 0, 2).reshape(2, dm)
    conv_w = per_chip[:, dm + 512:dm + 512 + 2 * CONV_W * FFN_COLS].reshape(N_CHIPS, 2, CONV_W, FFN_COLS)
    conv_w = conv_w.transpose(1, 2, 0, 3).reshape(2, CONV_W, 2, ff).transpose(0, 2, 1, 3)
    conv_b = ffn_conv_b.reshape(2, 2, 1, ff)
    lb = jax.nn.softmax(lb_logits, axis=0)[0:1]

    c16 = jnp.pad(c_all, ((0, 8), (0, 0)))
    mod_ada, c_act16 = _cond_rows(c16, ada_w, True, "mod_ada")
    mod_kv, _ = _cond_rows(c16, kv_ada_w[None], True, "mod_kv")
    mine = jnp.concatenate([mod_ada[0, :8], mod_ada[1, :8], mod_kv[0, :8]], axis=1)
    w2 = mine.shape[1]
    g2 = _allgather8(mine, "gather_mod").reshape(N_DEV, 8, w2)[0::2]
    my_rows = lax.dynamic_index_in_dim(g2, dev, axis=1, keepdims=False)
    mod0 = my_rows[:, 0:1536].reshape(6 * dm) + ada_b[0]
    mod1 = my_rows[:, 1536:3072].reshape(6 * dm) + ada_b[1]
    modk = my_rows[:, 3072:3584].reshape(2 * dm) + kv_ada_b
    mods = {"l0": [v.reshape(1, dm) for v in jnp.split(mod0, 6)],
            "l1": [v.reshape(1, dm) for v in jnp.split(mod1, 6)],
            "kv": [v.reshape(1, dm) for v in jnp.split(modk, 2)]}

    local = [a_w_in[0], a_w_out[0], _pad_cols(kv_w, KV_SHARD_PAD), b_w_q[0], b_w_out[0], ffn_w_up[0], ffn_w_up[1],
             ffn_w_down[0], ffn_w_down[1]]
    own = dict(zip(BIG, _cast_weights(local, "cast_weights")))
    first, later = ("a_w_in", "a_w_out", "up0", "down0"), ("kv_w", "b_w_q", "b_w_out", "up1", "down1")
    gathered = dict(zip(first, _sequencer_gather([own[n] for n in first], "gather_layer0", 1)))
    gathered.update(zip(later, _sequencer_gather([own[n] for n in later], "gather_layer1", 2)))
    kv_full = gathered["kv_w"][:, :, :KV_SHARD].transpose(1, 0, 2).reshape(dm, N_CHIPS * KV_SHARD)
    rowwise = lambda g: g.reshape(1, -1, dm)
    wts = {"a_w_in": gathered["a_w_in"], "a_w_out": rowwise(gathered["a_w_out"]),
           "kv_k": kv_full[None, :, :dm], "kv_v": kv_full[None, :, dm:2 * dm],
           "kv_f": _pad_cols(kv_full[None, :, 2 * dm:], LANES),
           "b_w_q": gathered["b_w_q"], "b_w_out": rowwise(gathered["b_w_out"]),
           "up0": gathered["up0"], "up1": gathered["up1"],
           "down0": rowwise(gathered["down0"]), "down1": rowwise(gathered["down1"])}
    vecs = {"a_norm_g": jnp.tile(a_norm_g, (1, HEADS)), "k_norm_g": jnp.tile(k_norm_g[None], (1, HEADS)),
            "q_norm_g": jnp.tile(q_norm_g, (1, HEADS)), "kv_b_f": _pad_cols(kv_b_f[None], LANES),
            "conv_w0": conv_w[0], "conv_b0": conv_b[0], "conv_w1": conv_w[1], "conv_b1": conv_b[1]}

    sq, grad_x, big, small = _local_step(x[0], loss_target[0], mods, lb, vecs, wts)
    loss = lax.psum(0.5 * jnp.sum(sq) / dm, ("x", "y", "c"))

    kv_grad = jnp.concatenate([big["kv_k"][0], big["kv_v"][0], big["kv_f"][0][:, :HEADS]], axis=1)
    kv_grad = _pad_cols(kv_grad.reshape(dm, N_CHIPS, KV_SHARD).transpose(1, 0, 2), KV_SHARD_PAD)
    chipwise = lambda g: g.reshape(N_CHIPS, -1, dm)
    parts = [big["a_w_in"], chipwise(big["a_w_out"]), kv_grad, big["b_w_q"], chipwise(big["b_w_out"]),
             big["up0"], big["up1"], chipwise(big["down0"]), chipwise(big["down1"])]
    rs = dict(zip(BIG, _reduce_scatter(parts, "reduce_grads")))

    fold = lambda a: a.sum(axis=0)
    heads = lambda a: fold(a).reshape(HEADS, HEAD_DIM).sum(axis=0)
    conv_flat = lambda a: a.sum(axis=2).transpose(1, 0, 2)
    pieces = ([fold(a) for a in small["mod_l0"]] + [fold(a) for a in small["mod_l1"]] + [fold(a) for a in small["mod_kv"]]
              + [conv_flat(small["conv0"]), conv_flat(small["conv1"]), heads(small["a_norm_g"]), heads(small["k_norm_g"]),
                 heads(small["q_norm_g"]), fold(small["kv_b_f"]), fold(small["lb"])])
    w3 = 61440
    g3 = _allgather8(_flat8(pieces, w3), "gather_small")
    tot = _sum_devices(g3, "sum_small").reshape(w3)
    n_mod = 14 * dm
    dmod_all = g3.reshape(N_DEV, w3)[:, :n_mod]
    o = n_mod
    conv_tot = [tot[o + l * 8 * ff: o + (l + 1) * 8 * ff].reshape(4, 2 * ff) for l in range(2)]
    o += 16 * ff
    g_a_norm, g_k_norm, g_q_norm = (tot[o + i * HEAD_DIM: o + (i + 1) * HEAD_DIM] for i in range(3))
    o += 3 * HEAD_DIM
    g_kv_b_f = tot[o:o + HEADS]
    dlb = tot[o + LANES:o + LANES + dm]

    ct = _pad_cols(c_act16[:8].T, LANES)
    dmod_pad = jnp.pad(dmod_all, ((0, LANES - N_DEV), (0, 0)))
    cols_ada = jnp.stack([lax.dynamic_slice_in_dim(dmod_pad, l * 6 * dm + chip * 1536, 1536, axis=1) for l in range(2)])
    cols_kv = lax.dynamic_slice_in_dim(dmod_pad, 12 * dm + chip * 512, 512, axis=1)[None]
    g_ada_w = _outer_grad(ct, cols_ada, "grad_ada_w")
    g_kv_ada_w = _outer_grad(ct, cols_kv, "grad_kv_ada_w")[0]

    my_lb = lax.dynamic_slice_in_dim(lb[0], chip * 256, 256)
    l0 = lax.dynamic_slice_in_dim(dlb, chip * 256, 256) * my_lb * (1.0 - my_lb)
    grads = {
        "ada_w": g_ada_w, "ada_b": jnp.stack([tot[:6 * dm], tot[6 * dm:12 * dm]]),
        "a_w_in": rs["a_w_in"][None], "a_lb_logits": jnp.stack([l0, -l0]), "a_norm_g": g_a_norm[None],
        "a_w_out": rs["a_w_out"][None], "kv_ada_w": g_kv_ada_w, "kv_ada_b": tot[12 * dm:14 * dm],
        "kv_w": rs["kv_w"][:, :KV_SHARD], "kv_b_f": g_kv_b_f, "k_norm_g": g_k_norm,
        "b_w_q": rs["b_w_q"][None], "q_norm_g": g_q_norm[None], "b_w_out": rs["b_w_out"][None],
        "ffn_w_up": jnp.stack([rs["up0"], rs["up1"]]),
        "ffn_conv_w": jnp.stack([lax.dynamic_slice_in_dim(ct_l[:CONV_W], chip * FFN_COLS, FFN_COLS, axis=1) for ct_l in conv_tot]),
        "ffn_conv_b": jnp.stack([ct_l[CONV_W] for ct_l in conv_tot]),
        "ffn_w_down": jnp.stack([rs["down0"], rs["down1"]]),
    }
    weights = dict(ada_w=ada_w, ada_b=ada_b, a_w_in=a_w_in, a_lb_logits=a_lb_logits, a_norm_g=a_norm_g, a_w_out=a_w_out,
                   kv_ada_w=kv_ada_w, kv_ada_b=kv_ada_b, kv_w=kv_w, kv_b_f=kv_b_f, k_norm_g=k_norm_g, b_w_q=b_w_q,
                   q_norm_g=q_norm_g, b_w_out=b_w_out, ffn_w_up=ffn_w_up, ffn_conv_w=ffn_conv_w, ffn_conv_b=ffn_conv_b,
                   ffn_w_down=ffn_w_down)
    m_in = dict(ada_w=m_ada_w, ada_b=m_ada_b, a_w_in=m_a_w_in, a_lb_logits=m_a_lb_logits, a_norm_g=m_a_norm_g,
                a_w_out=m_a_w_out, kv_ada_w=m_kv_ada_w, kv_ada_b=m_kv_ada_b, kv_w=m_kv_w, kv_b_f=m_kv_b_f,
                k_norm_g=m_k_norm_g, b_w_q=m_b_w_q, q_norm_g=m_q_norm_g, b_w_out=m_b_w_out, ffn_w_up=m_ffn_w_up,
                ffn_conv_w=m_ffn_conv_w, ffn_conv_b=m_ffn_conv_b, ffn_w_down=m_ffn_w_down)
    v_in = dict(ada_w=v_ada_w, ada_b=v_ada_b, a_w_in=v_a_w_in, a_lb_logits=v_a_lb_logits, a_norm_g=v_a_norm_g,
                a_w_out=v_a_w_out, kv_ada_w=v_kv_ada_w, kv_ada_b=v_kv_ada_b, kv_w=v_kv_w, kv_b_f=v_kv_b_f,
                k_norm_g=v_k_norm_g, b_w_q=v_b_w_q, q_norm_g=v_q_norm_g, b_w_out=v_b_w_out, ffn_w_up=v_ffn_w_up,
                ffn_conv_w=v_ffn_conv_w, ffn_conv_b=v_ffn_conv_b, ffn_w_down=v_ffn_w_down)

    names = list(weights)
    grads = {n: grads[n].reshape(weights[n].shape) for n in names}
    upd = {n: _adamw(weights[n], grads[n], m_in[n], v_in[n], "adamw_" + n) for n in names}
    return (loss, grad_x[None], *[grads[n] for n in names], *[upd[n][0] for n in names],
            *[upd[n][1] for n in names], *[upd[n][2] for n in names])
```

```python
import jax
import jax.numpy as jnp
from jax import lax
from jax.experimental import pallas as pl
from jax.experimental.pallas import tpu as pltpu
from jax.experimental.pallas import tpu_sc as plsc

F32 = jnp.float32
BF16 = jnp.bfloat16

D_MODEL = 1024
HEADS = 8
HEAD_DIM = 128
A_CHUNK = 64
D_FF = 2816
CONV_W = 3
EPS = 1e-6
NEG_INF = -1e30
N_CHIPS = 4
N_DEV = 8

ADAM_LR = 0.001
ADAM_B1 = 0.9
ADAM_B2 = 0.999
ADAM_EPS = 1e-08
ADAM_WD = 0.01
ADAM_STEP = 10

SUBLANES = 8
BF16_ROWS = 16
LANES = 128
HALO = BF16_ROWS
ROW_TILE = 512
FFN_COLS = 1408
HGRN_ROWS = 256
ATT_TILE = 512
ATT_SPLIT = 2
MESH = pl.DeviceIdType.MESH


def _sig(x):
    return jax.nn.sigmoid(x)


def _dot(a, b):
    return jnp.dot(a, b, preferred_element_type=F32)


def _dot_nt(a, b):
    return lax.dot_general(a, b, (((1,), (1,)), ((), ())), preferred_element_type=F32)


def _dot_tn(a, b):
    return lax.dot_general(a, b, (((0,), (0,)), ((), ())), preferred_element_type=F32)


def _split2(x):
    hi = x.astype(BF16)
    lo = (x - hi.astype(F32)).astype(BF16)
    return hi, lo


def _dot_f32(a, b):
    ah, al = _split2(a)
    bh, bl = _split2(b)
    return _dot(ah, bh) + _dot(ah, bl) + _dot(al, bh)


def _tri_dot(tri, x):
    hi = x.astype(BF16)
    r = x - hi.astype(F32)
    mid = r.astype(BF16)
    lo = (r - mid.astype(F32)).astype(BF16)
    return _dot(tri, hi) + _dot(tri, mid) + _dot(tri, lo)


def _tri(n, upper=False):
    r = lax.broadcasted_iota(jnp.int32, (n, n), 0)
    c = lax.broadcasted_iota(jnp.int32, (n, n), 1)
    keep = (c >= r) if upper else (c <= r)
    return jnp.where(keep, 1.0, 0.0).astype(BF16)


def _colsum8(v):
    rows, n = v.shape
    return v.reshape(rows // SUBLANES, SUBLANES, n).sum(axis=0)


def _full(shape):
    nd = len(shape)
    return pl.BlockSpec(shape, lambda *_: (0,) * nd)


def _tile(n, want):
    t = min(n, want)
    assert n % t == 0, (n, t)
    return t


def _mm_nn(a, w, groups, out_dtype, name):
    m_rows, k = a.shape
    p_n, _, n = w.shape
    per = p_n // groups
    tm = _tile(m_rows, ROW_TILE)

    def body(a_ref, w_ref, o_ref):
        av = a_ref[...]
        for p in range(p_n):
            o_ref[p // per, :, (p % per) * n:(p % per + 1) * n] = _dot(av, w_ref[p]).astype(out_dtype)

    return pl.pallas_call(
        body, name=name, grid=(m_rows // tm,),
        in_specs=[pl.BlockSpec((tm, k), lambda i: (i, 0)), _full((p_n, k, n))],
        out_specs=pl.BlockSpec((groups, tm, per * n), lambda i: (0, i, 0)),
        out_shape=jax.ShapeDtypeStruct((groups, m_rows, per * n), out_dtype),
    )(a, w)


def _mm_nt(d, w, out_dtype, name, add=None):
    g_n, m_rows, _ = d.shape
    p_n, k, n = w.shape
    per = p_n // g_n
    tm = _tile(m_rows, ROW_TILE)

    def body(*refs):
        d_ref, w_ref = refs[0], refs[1]
        o_ref = refs[-1]
        acc = refs[2][...] if add is not None else None
        for p in range(p_n):
            t = _dot_nt(d_ref[p // per, :, (p % per) * n:(p % per + 1) * n], w_ref[p])
            acc = t if acc is None else acc + t
        o_ref[...] = acc.astype(out_dtype)

    ins = [d, w] + ([add] if add is not None else [])
    specs = [pl.BlockSpec((g_n, tm, per * n), lambda i: (0, i, 0)), _full((p_n, k, n))]
    if add is not None:
        specs.append(pl.BlockSpec((tm, k), lambda i: (i, 0)))
    return pl.pallas_call(
        body, name=name, grid=(m_rows // tm,), in_specs=specs,
        out_specs=pl.BlockSpec((tm, k), lambda i: (i, 0)),
        out_shape=jax.ShapeDtypeStruct((m_rows, k), out_dtype),
    )(*ins)


def _mm_tn(a, d, p_n, name):
    m_rows, k = a.shape
    g_n, _, w_cols = d.shape
    per = p_n // g_n
    n = w_cols // per
    tm = _tile(m_rows, ROW_TILE)
    steps = m_rows // tm

    def body(a_ref, d_ref, o_ref, acc):
        m = pl.program_id(1)

        @pl.when(m == 0)
        def _():
            acc[...] = jnp.zeros_like(acc)

        acc[...] += _dot_tn(a_ref[...], d_ref[...])

        @pl.when(m == steps - 1)
        def _():
            o_ref[...] = acc[...].astype(BF16)

    return pl.pallas_call(
        body, name=name, grid=(p_n, steps),
        in_specs=[pl.BlockSpec((tm, k), lambda p, m: (m, 0)),
                  pl.BlockSpec((None, tm, n), lambda p, m: (p // per, m, p % per))],
        out_specs=pl.BlockSpec((None, k, n), lambda p, m: (p, 0, 0)),
        out_shape=jax.ShapeDtypeStruct((p_n, k, n), BF16),
        scratch_shapes=[pltpu.VMEM((k, n), F32)],
    )(a, d)


def _premix(x, shift, scale, name, branch=None, gate=None):
    s, dm = x.shape
    tm = _tile(s, ROW_TILE)
    with_branch = branch is not None

    def body(*refs):
        x_ref, sh_ref, sc_ref = refs[:3]
        xv = x_ref[...]
        if with_branch:
            xv = xv + refs[4][...] * refs[3][...]
            refs[-2][...] = xv
        inv = lax.rsqrt(jnp.mean(xv * xv, axis=-1, keepdims=True) + EPS)
        refs[-1][...] = (xv * inv * (1.0 + sc_ref[...]) + sh_ref[...]).astype(BF16)

    row = pl.BlockSpec((tm, dm), lambda i: (i, 0))
    vec = _full((1, dm))
    ins, specs = [x, shift, scale], [row, vec, vec]
    out_shape, out_specs = [jax.ShapeDtypeStruct((s, dm), BF16)], [row]
    if with_branch:
        ins += [branch, gate]
        specs += [row, vec]
        out_shape.insert(0, jax.ShapeDtypeStruct((s, dm), F32))
        out_specs.insert(0, row)
    outs = pl.pallas_call(body, name=name, grid=(s // tm,), in_specs=specs, out_specs=out_specs,
                          out_shape=out_shape)(*ins)
    return tuple(outs) if with_branch else outs[0]


def _premix_bwd(x, dh, scale, dres, name):
    s, dm = x.shape
    tm = _tile(s, ROW_TILE)

    def body(x_ref, dh_ref, sc_ref, dres_ref, dx_ref, dsh_ref, dsc_ref):
        i = pl.program_id(0)

        @pl.when(i == 0)
        def _():
            dsh_ref[...] = jnp.zeros_like(dsh_ref)
            dsc_ref[...] = jnp.zeros_like(dsc_ref)

        xv = x_ref[...]
        dhv = dh_ref[...]
        inv = lax.rsqrt(jnp.mean(xv * xv, axis=-1, keepdims=True) + EPS)
        r = xv * inv
        dr = dhv * (1.0 + sc_ref[...])
        dx_ref[...] = dres_ref[...] + inv * (dr - r * jnp.mean(dr * r, axis=-1, keepdims=True))
        dsh_ref[...] += _colsum8(dhv)
        dsc_ref[...] += _colsum8(dhv * r)

    row = pl.BlockSpec((tm, dm), lambda i: (i, 0))
    acc = _full((SUBLANES, dm))
    return pl.pallas_call(
        body, name=name, grid=(s // tm,), in_specs=[row, row, _full((1, dm)), row],
        out_specs=[row, acc, acc],
        out_shape=[jax.ShapeDtypeStruct((s, dm), F32), jax.ShapeDtypeStruct((SUBLANES, dm), F32),
                   jax.ShapeDtypeStruct((SUBLANES, dm), F32)],
    )(x, dh, scale, dres)


def _branch_bwd(dx, y, gate, name):
    s, dm = dx.shape
    tm = _tile(s, ROW_TILE)

    def body(dx_ref, y_ref, g_ref, dy_ref, dg_ref):
        @pl.when(pl.program_id(0) == 0)
        def _():
            dg_ref[...] = jnp.zeros_like(dg_ref)

        dxv = dx_ref[...]
        dy_ref[0] = (dxv * g_ref[...]).astype(BF16)
        dg_ref[...] += _colsum8(dxv * y_ref[...])

    row = pl.BlockSpec((tm, dm), lambda i: (i, 0))
    return pl.pallas_call(
        body, name=name, grid=(s // tm,), in_specs=[row, row, _full((1, dm))],
        out_specs=[pl.BlockSpec((1, tm, dm), lambda i: (0, i, 0)), _full((SUBLANES, dm))],
        out_shape=[jax.ShapeDtypeStruct((1, s, dm), BF16), jax.ShapeDtypeStruct((SUBLANES, dm), F32)],
    )(dx, y, gate)


def _loss_head(x, branch, gate, target, name):
    s, dm = x.shape
    tm = _tile(s, ROW_TILE)

    def body(x_ref, b_ref, g_ref, t_ref, sq_ref, dy_ref):
        @pl.when(pl.program_id(0) == 0)
        def _():
            sq_ref[...] = jnp.zeros_like(sq_ref)

        err = x_ref[...] + g_ref[...] * b_ref[...] - t_ref[...]
        sq_ref[...] += _colsum8(err * err)
        dy_ref[...] = err * (1.0 / dm)

    row = pl.BlockSpec((tm, dm), lambda i: (i, 0))
    return pl.pallas_call(
        body, name=name, grid=(s // tm,), in_specs=[row, row, _full((1, dm)), row],
        out_specs=[_full((SUBLANES, dm)), row],
        out_shape=[jax.ShapeDtypeStruct((SUBLANES, dm), F32), jax.ShapeDtypeStruct((s, dm), F32)],
    )(x, branch, gate, target)


def _conv_taps(e, w, b):
    return w[2:3] * e + w[1:2] * pltpu.roll(e, 1, 0) + w[0:1] * pltpu.roll(e, 2, 0) + b


def _ffn_specs(s, tm, cb):
    hb = tm // HALO
    last = s // HALO - 1
    main = pl.BlockSpec((2, tm, cb), lambda j, i: (0, i, j))
    prev = pl.BlockSpec((2, HALO, cb), lambda j, i: (0, jnp.maximum(i * hb - 1, 0), j))
    nxt = pl.BlockSpec((2, HALO, cb), lambda j, i: (0, jnp.minimum((i + 1) * hb, last), j))
    wspec = pl.BlockSpec((2, CONV_W, cb), lambda j, i: (0, 0, j))
    bspec = pl.BlockSpec((2, 1, cb), lambda j, i: (0, 0, j))
    return main, prev, nxt, wspec, bspec


def _convglu_fwd(u, w, b, name):
    _, s, f = u.shape
    tm = _tile(s, 256)
    cb = _tile(f, FFN_COLS)
    main, prev, _, wspec, bspec = _ffn_specs(s, tm, cb)

    def body(u_ref, up_ref, w_ref, b_ref, a_ref):
        first = jnp.where(pl.program_id(1) > 0, 1.0, 0.0)

        def conv(g):
            e = jnp.concatenate([up_ref[g].astype(F32) * first, u_ref[g].astype(F32)], axis=0)
            return _conv_taps(e, w_ref[g], b_ref[g])[HALO:]

        gate = conv(0)
        a_ref[...] = (gate * _sig(gate) * conv(1)).astype(BF16)

    return pl.pallas_call(
        body, name=name, grid=(f // cb, s // tm), in_specs=[main, prev, wspec, bspec],
        out_specs=pl.BlockSpec((tm, cb), lambda j, i: (i, j)),
        out_shape=jax.ShapeDtypeStruct((s, f), BF16),
    )(u, u, w, b)


def _convglu_bwd(u, da, w, b, name):
    _, s, f = u.shape
    tm = _tile(s, 256)
    cb = _tile(f, FFN_COLS)
    steps = s // tm
    n_ext = tm + 2 * HALO
    main, prev, nxt, wspec, bspec = _ffn_specs(s, tm, cb)
    hb = tm // HALO
    last = s // HALO - 1
    da_main = pl.BlockSpec((tm, cb), lambda j, i: (i, j))
    da_next = pl.BlockSpec((HALO, cb), lambda j, i: (jnp.minimum((i + 1) * hb, last), j))

    def body(u_ref, up_ref, un_ref, da_ref, dan_ref, w_ref, b_ref, du_ref, acc_ref):
        i = pl.program_id(1)
        first = jnp.where(i > 0, 1.0, 0.0)
        notlast = jnp.where(i < steps - 1, 1.0, 0.0)

        @pl.when(i == 0)
        def _():
            acc_ref[...] = jnp.zeros_like(acc_ref)

        def ext(g):
            return jnp.concatenate([up_ref[g].astype(F32) * first, u_ref[g].astype(F32), un_ref[g].astype(F32)], axis=0)

        ug, uv = ext(0), ext(1)
        gate = _conv_taps(ug, w_ref[0], b_ref[0])
        val = _conv_taps(uv, w_ref[1], b_ref[1])
        da_e = jnp.concatenate([jnp.zeros((HALO, cb), F32), da_ref[...].astype(F32),
                                dan_ref[...].astype(F32) * notlast], axis=0)
        sg = _sig(gate)
        d_val = da_e * gate * sg
        d_gate = da_e * val * (sg * (1.0 + gate * (1.0 - sg)))

        def finish(g, d, e):
            wv = w_ref[g]
            du = wv[2:3] * d + wv[1:2] * pltpu.roll(d, n_ext - 1, 0) + wv[0:1] * pltpu.roll(d, n_ext - 2, 0)
            du_ref[g] = du[HALO:HALO + tm].astype(BF16)
            dm = d[HALO:HALO + tm]
            acc_ref[g, 2] += _colsum8(dm * e[HALO:HALO + tm])
            acc_ref[g, 1] += _colsum8(dm * pltpu.roll(e, 1, 0)[HALO:HALO + tm])
            acc_ref[g, 0] += _colsum8(dm * pltpu.roll(e, 2, 0)[HALO:HALO + tm])
            acc_ref[g, 3] += _colsum8(dm)

        finish(0, d_gate, ug)
        finish(1, d_val, uv)

    return pl.pallas_call(
        body, name=name, grid=(f // cb, steps),
        in_specs=[main, prev, nxt, da_main, da_next, wspec, bspec],
        out_specs=[main, pl.BlockSpec((2, 4, SUBLANES, cb), lambda j, i: (0, 0, 0, j))],
        out_shape=[jax.ShapeDtypeStruct((2, s, f), BF16), jax.ShapeDtypeStruct((2, 4, SUBLANES, f), F32)],
    )(u, u, u, da, da, w, b)


def _hgrn_gates(q_raw, f_raw, lb, tri):
    sf = _sig(f_raw)
    fg = lb + (1.0 - lb) * sf
    b = _tri_dot(tri, jnp.log(fg))
    return q_raw * _sig(q_raw), 1.0 - fg, b, fg, sf


def _hgrn_fwd(proj, lb, norm_g, name):
    s = proj.shape[0]
    tb = _tile(s, HGRN_ROWS)
    n_c = tb // A_CHUNK
    half = A_CHUNK // 2

    def body(q_ref, f_ref, v_ref, g_ref, lb_ref, ng_ref, o_ref, yp_ref, st_ref, state):
        @pl.when(pl.program_id(0) == 0)
        def _():
            state[...] = jnp.zeros_like(state)

        tri = _tri(A_CHUNK)
        causal = lax.broadcasted_iota(jnp.int32, (A_CHUNK, A_CHUNK), 1) <= lax.broadcasted_iota(
            jnp.int32, (A_CHUNK, A_CHUNK), 0)

        def chunk(ci, carry):
            rows = pl.ds(pl.multiple_of(ci * A_CHUNK, A_CHUNK), A_CHUNK)
            for h in range(HEADS):
                cs = slice(h * HEAD_DIM, (h + 1) * HEAD_DIM)
                qs, k, b, _, _ = _hgrn_gates(q_ref[rows, cs], f_ref[rows, cs], lb_ref[:, cs], tri)
                b_mid, b_last = b[half:half + 1], b[A_CHUNK - 1:A_CHUNK]
                vb = v_ref[rows, cs].astype(BF16)
                scores = _dot_nt((qs * jnp.exp(b - b_mid)).astype(BF16), (k * jnp.exp(b_mid - b)).astype(BF16))
                scores = jnp.where(causal, scores, 0.0)
                st = state[h]
                st_ref[ci, h] = st
                o = _dot(scores.astype(BF16), vb) + _dot_nt((qs * jnp.exp(b)).astype(BF16), st.astype(BF16))
                state[h] = st * jnp.exp(b_last) + _dot_tn(vb, (k * jnp.exp(b_last - b)).astype(BF16))
                o_ref[rows, cs] = o
                inv = lax.rsqrt(jnp.mean(o * o, axis=-1, keepdims=True) + EPS)
                g_raw = g_ref[rows, cs]
                yp_ref[rows, cs] = (o * inv * ng_ref[:, cs] * (g_raw * _sig(g_raw))).astype(BF16)
            return carry

        lax.fori_loop(0, n_c, chunk, 0)

    col = lambda j: pl.BlockSpec((tb, D_MODEL), lambda i: (i, j))
    vec = _full((1, D_MODEL))
    return pl.pallas_call(
        body, name=name, grid=(s // tb,), in_specs=[col(0), col(1), col(2), col(3), vec, vec],
        out_specs=[col(0), col(0), pl.BlockSpec((n_c, HEADS, HEAD_DIM, HEAD_DIM), lambda i: (i, 0, 0, 0))],
        out_shape=[jax.ShapeDtypeStruct((s, D_MODEL), F32), jax.ShapeDtypeStruct((s, D_MODEL), BF16),
                   jax.ShapeDtypeStruct((s // A_CHUNK, HEADS, HEAD_DIM, HEAD_DIM), F32)],
        scratch_shapes=[pltpu.VMEM((HEADS, HEAD_DIM, HEAD_DIM), F32)],
    )(proj, proj, proj, proj, lb, norm_g)


def _hgrn_bwd(proj, lb, norm_g, o, states, dyp, name):
    s = proj.shape[0]
    tb = _tile(s, HGRN_ROWS)
    n_c = tb // A_CHUNK
    n_b = s // tb
    half = A_CHUNK // 2

    def body(q_ref, f_ref, v_ref, g_ref, lb_ref, ng_ref, o_ref, st_ref, dyp_ref, dp_ref, dlb_ref, dng_ref, dstate):
        @pl.when(pl.program_id(0) == 0)
        def _():
            dstate[...] = jnp.zeros_like(dstate)
            dlb_ref[...] = jnp.zeros_like(dlb_ref)
            dng_ref[...] = jnp.zeros_like(dng_ref)

        tri = _tri(A_CHUNK)
        tri_up = _tri(A_CHUNK, upper=True)
        row_id = lax.broadcasted_iota(jnp.int32, (A_CHUNK, HEAD_DIM), 0)
        causal = lax.broadcasted_iota(jnp.int32, (A_CHUNK, A_CHUNK), 1) <= lax.broadcasted_iota(
            jnp.int32, (A_CHUNK, A_CHUNK), 0)

        def chunk(cj, carry):
            ci = n_c - 1 - cj
            rows = pl.ds(pl.multiple_of(ci * A_CHUNK, A_CHUNK), A_CHUNK)
            for h in range(HEADS):
                cs = slice(h * HEAD_DIM, (h + 1) * HEAD_DIM)
                q_raw, lbh = q_ref[rows, cs], lb_ref[:, cs]
                qs, k, b, fg, sf = _hgrn_gates(q_raw, f_ref[rows, cs], lbh, tri)
                b_mid, b_last = b[half:half + 1], b[A_CHUNK - 1:A_CHUNK]
                e_qi, e_ki, e_q, e_ks = jnp.exp(b - b_mid), jnp.exp(b_mid - b), jnp.exp(b), jnp.exp(b_last - b)
                q_i, k_i, q_e, k_s = qs * e_qi, k * e_ki, qs * e_q, k * e_ks
                vb = v_ref[rows, cs].astype(BF16)
                scores = jnp.where(causal, _dot_nt(q_i.astype(BF16), k_i.astype(BF16)), 0.0)
                ov, g_raw, dy, ng = o_ref[rows, cs], g_ref[rows, cs], dyp_ref[rows, cs], ng_ref[:, cs]
                inv = lax.rsqrt(jnp.mean(ov * ov, axis=-1, keepdims=True) + EPS)
                nrm = ov * inv
                sg = _sig(g_raw)
                gs = g_raw * sg
                dn = dy * ng * gs
                dng_ref[0:1, cs] += jnp.sum(dy * nrm * gs, axis=0, keepdims=True)
                dg_raw = dy * nrm * ng * (sg * (1.0 + g_raw * (1.0 - sg)))
                do = (inv * (dn - nrm * jnp.mean(dn * nrm, axis=-1, keepdims=True))).astype(BF16)
                st_prev = st_ref[ci, h]
                dst = dstate[h]
                dstb = dst.astype(BF16)
                d_scores = jnp.where(causal, _dot_nt(do, vb), 0.0).astype(BF16)
                dv = _dot_tn(scores.astype(BF16), do) + _dot_nt(k_s.astype(BF16), dstb)
                dq_i = _dot(d_scores, k_i.astype(BF16))
                dk_i = _dot_tn(d_scores, q_i.astype(BF16))
                dq_e = _dot(do, st_prev.astype(BF16))
                dk_s = _dot(vb, dstb)
                d_decay = jnp.sum(st_prev * dst, axis=0, keepdims=True)
                dstate[h] = dst * jnp.exp(b_last) + _dot_tn(do, q_e.astype(BF16))
                dq = dq_i * e_qi + dq_e * e_q
                dk = dk_i * e_ki + dk_s * e_ks
                t_qi, t_ki, t_ks = dq_i * q_i, dk_i * k_i, dk_s * k_s
                db = t_qi - t_ki + dq_e * q_e - t_ks
                db_mid = jnp.sum(t_ki - t_qi, axis=0, keepdims=True)
                db_last = jnp.sum(t_ks, axis=0, keepdims=True) + d_decay * jnp.exp(b_last)
                db = db + jnp.where(row_id == half, db_mid, 0.0) + jnp.where(row_id == A_CHUNK - 1, db_last, 0.0)
                dfg = _tri_dot(tri_up, db) / fg - dk
                dlb_ref[0:1, cs] += jnp.sum(dfg * (1.0 - sf), axis=0, keepdims=True)
                sq = _sig(q_raw)
                dp_ref[0, rows, cs] = (dq * (sq * (1.0 + q_raw * (1.0 - sq)))).astype(BF16)
                dp_ref[1, rows, cs] = (dfg * (1.0 - lbh) * sf * (1.0 - sf)).astype(BF16)
                dp_ref[2, rows, cs] = dv.astype(BF16)
                dp_ref[3, rows, cs] = dg_raw.astype(BF16)
            return carry

        lax.fori_loop(0, n_c, chunk, 0)

    col = lambda j: pl.BlockSpec((tb, D_MODEL), lambda i: (n_b - 1 - i, j))
    vec = _full((1, D_MODEL))
    acc = _full((SUBLANES, D_MODEL))
    return pl.pallas_call(
        body, name=name, grid=(n_b,),
        in_specs=[col(0), col(1), col(2), col(3), vec, vec, col(0),
                  pl.BlockSpec((n_c, HEADS, HEAD_DIM, HEAD_DIM), lambda i: (n_b - 1 - i, 0, 0, 0)), col(0)],
        out_specs=[pl.BlockSpec((4, tb, D_MODEL), lambda i: (0, n_b - 1 - i, 0)), acc, acc],
        out_shape=[jax.ShapeDtypeStruct((4, s, D_MODEL), BF16), jax.ShapeDtypeStruct((SUBLANES, D_MODEL), F32),
                   jax.ShapeDtypeStruct((SUBLANES, D_MODEL), F32)],
        scratch_shapes=[pltpu.VMEM((HEADS, HEAD_DIM, HEAD_DIM), F32)],
    )(proj, proj, proj, proj, lb, norm_g, o, states, dyp)


def _headnorm(x, g, mult, name, col0=0):
    s = x.shape[0]
    tm = _tile(s, ROW_TILE)

    def body(x_ref, g_ref, y_ref):
        for h in range(HEADS):
            cs = slice(h * HEAD_DIM, (h + 1) * HEAD_DIM)
            xv = x_ref[:, cs]
            inv = lax.rsqrt(jnp.mean(xv * xv, axis=-1, keepdims=True) + EPS)
            y_ref[:, cs] = (xv * inv * g_ref[:, cs] * mult).astype(BF16)

    return pl.pallas_call(
        body, name=name, grid=(s // tm,),
        in_specs=[pl.BlockSpec((tm, D_MODEL), lambda i: (i, col0)), _full((1, D_MODEL))],
        out_specs=pl.BlockSpec((tm, D_MODEL), lambda i: (i, 0)),
        out_shape=jax.ShapeDtypeStruct((s, D_MODEL), BF16),
    )(x, g)


def _headnorm_bwd(x, g, mult, dy, name, col0=0, extra=None):
    s = x.shape[0]
    tm = _tile(s, ROW_TILE)
    groups = 2 if extra is not None else 1

    def body(*refs):
        x_ref, g_ref, dy_ref = refs[:3]
        dx_ref, dg_ref = refs[-2:]

        @pl.when(pl.program_id(0) == 0)
        def _():
            dg_ref[...] = jnp.zeros_like(dg_ref)

        for h in range(HEADS):
            cs = slice(h * HEAD_DIM, (h + 1) * HEAD_DIM)
            xv, dyv, gv = x_ref[:, cs], dy_ref[:, cs], g_ref[:, cs]
            inv = lax.rsqrt(jnp.mean(xv * xv, axis=-1, keepdims=True) + EPS)
            nrm = xv * inv
            dn = dyv * gv * mult
            dg_ref[:, cs] += _colsum8(dyv * nrm * mult)
            dx_ref[0, :, cs] = (inv * (dn - nrm * jnp.mean(dn * nrm, axis=-1, keepdims=True))).astype(BF16)
        if extra is not None:
            dx_ref[1] = refs[3][...]

    row = pl.BlockSpec((tm, D_MODEL), lambda i: (i, 0))
    ins = [x, g, dy] + ([extra] if extra is not None else [])
    specs = [pl.BlockSpec((tm, D_MODEL), lambda i: (i, col0)), _full((1, D_MODEL)), row] + ([row] if extra is not None else [])
    return pl.pallas_call(
        body, name=name, grid=(s // tm,), in_specs=specs,
        out_specs=[pl.BlockSpec((groups, tm, D_MODEL), lambda i: (0, i, 0)), _full((SUBLANES, D_MODEL))],
        out_shape=[jax.ShapeDtypeStruct((groups, s, D_MODEL), BF16), jax.ShapeDtypeStruct((SUBLANES, D_MODEL), F32)],
    )(*ins)


def _log_sigmoid(z):
    return jnp.minimum(z, 0.0) - jnp.log(1.0 + jnp.exp(-jnp.abs(z)))


Q_CUM, Q_ONE, Q_LSE = 0, 3, 6
LOG2E = 1.4426950408889634


def _pieces(v):
    hi = v.astype(BF16).astype(F32)
    mid = (v - hi).astype(BF16).astype(F32)
    lo = ((v - hi) - mid).astype(BF16).astype(F32)
    return hi, mid, lo


def _side(lane, at, v):
    hi, mid, lo = _pieces(v)
    return jnp.where(lane == at, hi, jnp.where(lane == at + 1, mid, jnp.where(lane == at + 2, lo, 0.0)))


def _fcum_fwd(f, bias, name):
    s = f.shape[0]
    tm = _tile(s, ROW_TILE)

    def body(f_ref, b_ref, qa_ref, ka_ref, carry):
        @pl.when(pl.program_id(0) == 0)
        def _():
            carry[...] = jnp.zeros_like(carry)

        cum = _tri_dot(_tri(tm), _log_sigmoid(f_ref[...] + b_ref[...])) + carry[...]
        carry[...] = cum[tm - 1:tm]
        lane = lax.broadcasted_iota(jnp.int32, (tm, LANES), 1)
        ones_q = jnp.where((lane >= Q_ONE) & (lane < Q_LSE), 1.0, 0.0)
        ones_k = jnp.where((lane < Q_ONE) | ((lane >= Q_LSE) & (lane < Q_LSE + 3)), 1.0, 0.0)
        for h in range(HEADS):
            c2 = cum[:, h:h + 1] * LOG2E
            qa_ref[h] = (_side(lane, Q_CUM, c2) + ones_q).astype(BF16)
            ka_ref[h] = (_side(lane, Q_ONE, -c2) + ones_k).astype(BF16)

    side = pl.BlockSpec((HEADS, tm, LANES), lambda i: (0, i, 0))
    return pl.pallas_call(
        body, name=name, grid=(s // tm,),
        in_specs=[pl.BlockSpec((tm, LANES), lambda i: (i, 0)), _full((1, LANES))],
        out_specs=[side, side],
        out_shape=[jax.ShapeDtypeStruct((HEADS, s, LANES), BF16)] * 2,
        scratch_shapes=[pltpu.VMEM((1, LANES), F32)],
    )(f, bias)


def _fcum_bwd(f, bias, dka, dqa, name):
    s = f.shape[0]
    tm = _tile(s, ROW_TILE)
    n_b = s // tm

    def body(f_ref, b_ref, dka_ref, dqa_ref, dz_ref, db_ref, carry):
        @pl.when(pl.program_id(0) == 0)
        def _():
            carry[...] = jnp.zeros_like(carry)
            db_ref[...] = jnp.zeros_like(db_ref)

        lane = lax.broadcasted_iota(jnp.int32, (tm, LANES), 1)
        dcum = jnp.zeros((tm, LANES), F32)
        for h in range(HEADS):
            dcum = dcum + jnp.where(lane == h, dqa_ref[h][:, 0:1] - dka_ref[h][:, Q_ONE:Q_ONE + 1], 0.0)
        dlf = _tri_dot(_tri(tm, upper=True), dcum) + carry[...]
        carry[...] = dlf[0:1]
        dz = dlf * _sig(-(f_ref[...] + b_ref[...]))
        dz_ref[0] = dz.astype(BF16)
        db_ref[...] += _colsum8(dz)

    return pl.pallas_call(
        body, name=name, grid=(n_b,),
        in_specs=[pl.BlockSpec((tm, LANES), lambda i: (n_b - 1 - i, 0)), _full((1, LANES)),
                  pl.BlockSpec((HEADS, tm, LANES), lambda i: (0, n_b - 1 - i, 0)),
                  pl.BlockSpec((HEADS, tm, LANES), lambda i: (0, n_b - 1 - i, 0))],
        out_specs=[pl.BlockSpec((1, tm, LANES), lambda i: (0, n_b - 1 - i, 0)), _full((SUBLANES, LANES))],
        out_shape=[jax.ShapeDtypeStruct((1, s, LANES), BF16), jax.ShapeDtypeStruct((SUBLANES, LANES), F32)],
        scratch_shapes=[pltpu.VMEM((1, LANES), F32)],
    )(f, bias, dka, dqa)


def _causal_pairs(n_t, key_major):
    if key_major:
        pairs = [(qi, ki) for ki in range(n_t) for qi in range(ki, n_t)]
    else:
        pairs = [(qi, ki) for qi in range(n_t) for ki in range(qi + 1)]
    return (jnp.array([p[0] for p in pairs], jnp.int32), jnp.array([p[1] for p in pairs], jnp.int32))


def _with_side(main_ref, side_ref):
    return jnp.concatenate([main_ref[...], side_ref[...]], axis=1)


def _lane_const(t, lo, hi, value):
    lane = lax.broadcasted_iota(jnp.int32, (t, LANES), 1)
    return jnp.where((lane >= lo) & (lane < hi), value, 0.0).astype(BF16)


def _att_specs(t):
    qmain = pl.BlockSpec((t, HEAD_DIM), lambda h, p, qt, kt: (qt[p], h))
    kmain = pl.BlockSpec((t, HEAD_DIM), lambda h, p, qt, kt: (kt[p], h))
    qside = pl.BlockSpec((None, t, LANES), lambda h, p, qt, kt: (h, qt[p], 0))
    kside = pl.BlockSpec((None, t, LANES), lambda h, p, qt, kt: (h, kt[p], 0))
    return qmain, kmain, qside, kside


def _fox_fwd(q, qa, k, ka, v, qo, name):
    s = q.shape[0]
    t = _tile(s, ATT_TILE)
    sub = t // ATT_SPLIT
    qt, kt = _causal_pairs(s // t, key_major=False)

    def body(qt_ref, kt_ref, q_ref, qa_ref, k_ref, ka_ref, v_ref, og_ref, o_ref, y_ref, qab_ref, m_s, l_s, acc_s):
        pid = pl.program_id(1)
        qi, ki = qt_ref[pid], kt_ref[pid]

        @pl.when(ki == 0)
        def _():
            m_s[...] = jnp.full_like(m_s, NEG_INF)
            l_s[...] = jnp.zeros_like(l_s)
            acc_s[...] = jnp.zeros_like(acc_s)

        def step(diagonal):
            kc = _with_side(k_ref, ka_ref)
            vc = jnp.concatenate([v_ref[...], _lane_const(t, 0, 1, 1.0)], axis=1)
            for r in range(ATT_SPLIT):
                rows = slice(r * sub, (r + 1) * sub)
                n_k = (r + 1) * sub if diagonal else t
                sc = _dot_nt(jnp.concatenate([q_ref[rows], qa_ref[rows]], axis=1), kc[:n_k])
                if diagonal:
                    sc = jnp.where(lax.broadcasted_iota(jnp.int32, (sub, n_k), 1)
                                   <= lax.broadcasted_iota(jnp.int32, (sub, n_k), 0) + r * sub, sc, NEG_INF)
                m_old = m_s[rows]
                m_new = jnp.maximum(m_old, jnp.max(sc, axis=-1, keepdims=True))
                alpha = jnp.exp2(m_old - m_new)
                pv = _dot(jnp.exp2(sc - m_new[:, 0:1]).astype(BF16), vc[:n_k])
                acc_s[rows] = alpha * acc_s[rows] + pv[:, :HEAD_DIM]
                l_s[rows] = alpha * l_s[rows] + pv[:, HEAD_DIM:]
                m_s[rows] = m_new

        @pl.when(ki < qi)
        def _():
            step(False)

        @pl.when(ki == qi)
        def _():
            step(True)
            l = l_s[:, 0:1]
            o = acc_s[...] / l
            o_ref[...] = o
            y_ref[...] = (o * _sig(og_ref[...])).astype(BF16)
            lane = lax.broadcasted_iota(jnp.int32, (t, LANES), 1)
            qab_ref[...] = qa_ref[...] + _side(lane, Q_LSE, -(m_s[:, 0:1] + jnp.log2(l))).astype(BF16)

    qmain, kmain, qside, kside = _att_specs(t)
    return pl.pallas_call(
        body, name=name,
        grid_spec=pltpu.PrefetchScalarGridSpec(
            num_scalar_prefetch=2, grid=(HEADS, qt.shape[0]),
            in_specs=[qmain, qside, kmain, kside, kmain,
                      pl.BlockSpec((t, HEAD_DIM), lambda h, p, qt, kt: (qt[p], HEADS + h))],
            out_specs=[qmain, qmain, qside],
            scratch_shapes=[pltpu.VMEM((t, LANES), F32), pltpu.VMEM((t, LANES), F32), pltpu.VMEM((t, HEAD_DIM), F32)]),
        out_shape=[jax.ShapeDtypeStruct((s, D_MODEL), F32), jax.ShapeDtypeStruct((s, D_MODEL), BF16),
                   jax.ShapeDtypeStruct((HEADS, s, LANES), BF16)],
    )(qt, kt, q, qa, k, ka, v, qo)


def _fox_gate_bwd(o, qo, dy, name):
    s = o.shape[0]
    tm = _tile(s, ROW_TILE)

    def body(o_ref, og_ref, dy_ref, do_ref, dg_ref, dl_ref):
        ov, dyv = o_ref[...], dy_ref[...]
        sg = _sig(og_ref[...])
        do = (dyv * sg).astype(BF16)
        do_ref[...] = do
        dg_ref[...] = (dyv * ov * sg * (1.0 - sg)).astype(BF16)
        prod = do.astype(F32) * ov
        lane = lax.broadcasted_iota(jnp.int32, (tm, LANES), 1)
        for h in range(HEADS):
            delta = jnp.sum(prod[:, h * HEAD_DIM:(h + 1) * HEAD_DIM], axis=-1, keepdims=True)
            dl_ref[h] = _side(lane, 0, delta).astype(BF16)

    row = pl.BlockSpec((tm, D_MODEL), lambda i: (i, 0))
    return pl.pallas_call(
        body, name=name, grid=(s // tm,),
        in_specs=[row, pl.BlockSpec((tm, D_MODEL), lambda i: (i, 1)), row],
        out_specs=[row, row, pl.BlockSpec((HEADS, tm, LANES), lambda i: (0, i, 0))],
        out_shape=[jax.ShapeDtypeStruct((s, D_MODEL), BF16), jax.ShapeDtypeStruct((s, D_MODEL), BF16),
                   jax.ShapeDtypeStruct((HEADS, s, LANES), BF16)],
    )(o, qo, dy)


def _fox_bwd_kv(q, qab, k, ka, v, do, doa, name):
    s = q.shape[0]
    t = _tile(s, ATT_TILE)
    n_t = s // t
    sub = t // ATT_SPLIT
    qt, kt = _causal_pairs(n_t, key_major=True)

    def body(qt_ref, kt_ref, q_ref, qab_ref, k_ref, ka_ref, v_ref, do_ref, doa_ref, dk_ref, dv_ref, dka_ref, dk_s, dv_s):
        pid = pl.program_id(1)
        qi, ki = qt_ref[pid], kt_ref[pid]

        @pl.when(qi == ki)
        def _():
            dk_s[...] = jnp.zeros_like(dk_s)
            dv_s[...] = jnp.zeros_like(dv_s)

        def step(diagonal):
            kc = _with_side(k_ref, ka_ref)
            vc = jnp.concatenate([v_ref[...], _lane_const(t, 0, 3, -1.0)], axis=1)
            for r in range(ATT_SPLIT):
                cols = slice(r * sub, (r + 1) * sub)
                n_k = (r + 1) * sub if diagonal else t
                qc = jnp.concatenate([q_ref[cols], qab_ref[cols]], axis=1)
                sc = _dot_nt(kc[:n_k], qc)
                if diagonal:
                    sc = jnp.where(lax.broadcasted_iota(jnp.int32, (n_k, sub), 0)
                                   <= lax.broadcasted_iota(jnp.int32, (n_k, sub), 1) + r * sub, sc, NEG_INF)
                p = jnp.exp2(sc)
                dp = _dot_nt(vc[:n_k], jnp.concatenate([do_ref[cols], doa_ref[cols]], axis=1))
                dv_s[0:n_k] += _dot(p.astype(BF16), do_ref[cols])
                dk_s[0:n_k] += _dot((p * dp).astype(BF16), qc)

        @pl.when(qi > ki)
        def _():
            step(False)

        @pl.when(qi == ki)
        def _():
            step(True)

        @pl.when(qi == n_t - 1)
        def _():
            dk_ref[...] = dk_s[:, :HEAD_DIM] * (1.0 / LOG2E)
            dka_ref[...] = dk_s[:, HEAD_DIM:]
            dv_ref[...] = dv_s[...].astype(BF16)

    qmain, kmain, qside, kside = _att_specs(t)
    return pl.pallas_call(
        body, name=name,
        grid_spec=pltpu.PrefetchScalarGridSpec(
            num_scalar_prefetch=2, grid=(HEADS, qt.shape[0]),
            in_specs=[qmain, qside, kmain, kside, kmain, qmain, qside],
            out_specs=[kmain, pl.BlockSpec((None, t, HEAD_DIM), lambda h, p, qt, kt: (0, kt[p], h)), kside],
            scratch_shapes=[pltpu.VMEM((t, 2 * HEAD_DIM), F32), pltpu.VMEM((t, HEAD_DIM), F32)]),
        out_shape=[jax.ShapeDtypeStruct((s, D_MODEL), F32), jax.ShapeDtypeStruct((1, s, D_MODEL), BF16),
                   jax.ShapeDtypeStruct((HEADS, s, LANES), F32)],
    )(qt, kt, q, qab, k, ka, v, do, doa)


def _fox_bwd_q(q, qab, k, ka, v, do, doa, name):
    s = q.shape[0]
    t = _tile(s, ATT_TILE)
    sub = t // ATT_SPLIT
    qt, kt = _causal_pairs(s // t, key_major=False)

    def body(qt_ref, kt_ref, q_ref, qab_ref, k_ref, ka_ref, v_ref, do_ref, doa_ref, dq_ref, dqa_ref, dq_s):
        pid = pl.program_id(1)
        qi, ki = qt_ref[pid], kt_ref[pid]

        @pl.when(ki == 0)
        def _():
            dq_s[...] = jnp.zeros_like(dq_s)

        def step(diagonal):
            kc = _with_side(k_ref, ka_ref)
            vc = jnp.concatenate([v_ref[...], _lane_const(t, 0, 3, -1.0)], axis=1)
            for r in range(ATT_SPLIT):
                rows = slice(r * sub, (r + 1) * sub)
                n_k = (r + 1) * sub if diagonal else t
                sc = _dot_nt(jnp.concatenate([q_ref[rows], qab_ref[rows]], axis=1), kc[:n_k])
                if diagonal:
                    sc = jnp.where(lax.broadcasted_iota(jnp.int32, (sub, n_k), 1)
                                   <= lax.broadcasted_iota(jnp.int32, (sub, n_k), 0) + r * sub, sc, NEG_INF)
                dp = _dot_nt(jnp.concatenate([do_ref[rows], doa_ref[rows]], axis=1), vc[:n_k])
                dq_s[rows] += _dot((jnp.exp2(sc) * dp).astype(BF16), kc[:n_k])

        @pl.when(ki < qi)
        def _():
            step(False)

        @pl.when(ki == qi)
        def _():
            step(True)
            dq_ref[...] = dq_s[:, :HEAD_DIM]
            dqa_ref[...] = dq_s[:, HEAD_DIM:]

    qmain, kmain, qside, kside = _att_specs(t)
    return pl.pallas_call(
        body, name=name,
        grid_spec=pltpu.PrefetchScalarGridSpec(
            num_scalar_prefetch=2, grid=(HEADS, qt.shape[0]),
            in_specs=[qmain, qside, kmain, kside, kmain, qmain, qside],
            out_specs=[qmain, qside],
            scratch_shapes=[pltpu.VMEM((t, 2 * HEAD_DIM), F32)]),
        out_shape=[jax.ShapeDtypeStruct((s, D_MODEL), F32), jax.ShapeDtypeStruct((HEADS, s, LANES), F32)],
    )(qt, kt, q, qab, k, ka, v, do, doa)


def _ffn_forward(x_in, branch, gate, shift, scale, w_up, conv_w, conv_b, w_down, tag):
    x_mid, h = _premix(x_in, shift, scale, tag + "_premix", branch=branch, gate=gate)
    u = _mm_nn(h, w_up, 2, BF16, tag + "_up")
    a = _convglu_fwd(u, conv_w, conv_b, tag + "_convglu")
    ffn = _mm_nn(a, w_down, 1, F32, tag + "_down")[0]
    return x_mid, ffn, (h, u, a)


def _ffn_backward(dx_out, x_mid, ffn, gate, scale, saved, w_up, conv_w, conv_b, w_down, tag):
    h, u, a = saved
    dffn, dgate = _branch_bwd(dx_out, ffn, gate, tag + "_gate_bwd")
    da = _mm_nt(dffn, w_down, BF16, tag + "_down_dx")
    dw_down = _mm_tn(a, dffn, 1, tag + "_down_dw")
    du, dconv = _convglu_bwd(u, da, conv_w, conv_b, tag + "_convglu_bwd")
    dh = _mm_nt(du, w_up, F32, tag + "_up_dx")
    dw_up = _mm_tn(h, du, N_CHIPS, tag + "_up_dw")
    dx_mid, dshift, dscale = _premix_bwd(x_mid, dh, scale, dx_out, tag + "_premix_bwd")
    return dx_mid, dw_up, dw_down, dict(gate=dgate, shift=dshift, scale=dscale, conv=dconv)


def _local_step(x, target, mods, lb, vecs, weights_at):
    m0, m1, mk = mods["l0"], mods["l1"], mods["kv"]
    h0 = _premix(x, m0[0], m0[1], "l0_premix")
    wts, h0 = weights_at("mixer0", h0)
    proj = _mm_nn(h0, wts["a_w_in"], 1, F32, "l0_in")[0]
    o_a, yp, states = _hgrn_fwd(proj, lb, vecs["a_norm_g"], "l0_hgrn")
    more, yp = weights_at("ffn0", yp)
    wts.update(more)
    y0 = _mm_nn(yp, wts["a_w_out"], 1, F32, "l0_out")[0]
    x1, ffn0, saved0 = _ffn_forward(x, y0, m0[2], m0[3], m0[4], wts["up0"], vecs["conv_w0"], vecs["conv_b0"],
                                    wts["down0"], "l0_ffn")
    more, ffn0 = weights_at("layer1", ffn0)
    wts.update(more)
    x2, hk = _premix(x1, mk[0], mk[1], "kv_premix", branch=ffn0, gate=m0[5])
    k_raw = _mm_nn(hk, wts["kv_k"], 1, F32, "kv_k")[0]
    v_sh = _mm_nn(hk, wts["kv_v"], 1, BF16, "kv_v")[0]
    f_raw = _mm_nn(hk, wts["kv_f"], 1, F32, "kv_f")[0]
    k_sh = _headnorm(k_raw, vecs["k_norm_g"], 1.0, "kv_knorm")
    qa, ka = _fcum_fwd(f_raw, vecs["kv_b_f"], "kv_fcum")
    h1 = _premix(x2, m1[0], m1[1], "l1_premix")
    qo = _mm_nn(h1, wts["b_w_q"], 1, F32, "l1_q")[0]
    q_scale = HEAD_DIM ** -0.5
    q = _headnorm(qo, vecs["q_norm_g"], q_scale * LOG2E, "l1_qnorm")
    o_b, og, qab = _fox_fwd(q, qa, k_sh, ka, v_sh, qo, "l1_fox")
    y1 = _mm_nn(og, wts["b_w_out"], 1, F32, "l1_out")[0]
    x3, ffn1, saved1 = _ffn_forward(x2, y1, m1[2], m1[3], m1[4], wts["up1"], vecs["conv_w1"], vecs["conv_b1"],
                                    wts["down1"], "l1_ffn")
    sq, dx4 = _loss_head(x3, ffn1, m1[5], target, "loss_head")

    big, small = {}, {}
    dx3, big["up1"], big["down1"], s_ffn1 = _ffn_backward(dx4, x3, ffn1, m1[5], m1[4], saved1, wts["up1"],
                                                          vecs["conv_w1"], vecs["conv_b1"], wts["down1"], "l1_ffn")
    dy1, dg1_1 = _branch_bwd(dx3, y1, m1[2], "l1_mix_gate_bwd")
    d_og = _mm_nt(dy1, wts["b_w_out"], F32, "l1_out_dx")
    big["b_w_out"] = _mm_tn(og, dy1, 1, "l1_out_dw")
    do_b, dgate_b, doa = _fox_gate_bwd(o_b, qo, d_og, "l1_fox_gate_bwd")
    dk, dv, dka = _fox_bwd_kv(q, qab, k_sh, ka, v_sh, do_b, doa, "l1_fox_bwd_kv")
    dq, dqa = _fox_bwd_q(q, qab, k_sh, ka, v_sh, do_b, doa, "l1_fox_bwd_q")
    dqo, dqg = _headnorm_bwd(qo, vecs["q_norm_g"], q_scale, dq, "l1_qnorm_bwd", extra=dgate_b)
    dh1 = _mm_nt(dqo, wts["b_w_q"], F32, "l1_q_dx")
    big["b_w_q"] = _mm_tn(h1, dqo, N_CHIPS, "l1_q_dw")
    dx2, dsh1_1, dsc1_1 = _premix_bwd(x2, dh1, m1[1], dx3, "l1_premix_bwd")
    dk_raw, dkg = _headnorm_bwd(k_raw, vecs["k_norm_g"], 1.0, dk, "kv_knorm_bwd")
    dz, dbf = _fcum_bwd(f_raw, vecs["kv_b_f"], dka, dqa, "kv_fcum_bwd")
    dhk = _mm_nt(dk_raw, wts["kv_k"], F32, "kv_k_dx")
    dhk = _mm_nt(dv, wts["kv_v"], F32, "kv_v_dx", add=dhk)
    dhk = _mm_nt(dz, wts["kv_f"], F32, "kv_f_dx", add=dhk)
    big["kv_k"] = _mm_tn(hk, dk_raw, 1, "kv_k_dw")
    big["kv_v"] = _mm_tn(hk, dv, 1, "kv_v_dw")
    big["kv_f"] = _mm_tn(hk, dz, 1, "kv_f_dw")
    dx2, dshk, dsck = _premix_bwd(x2, dhk, mk[1], dx2, "kv_premix_bwd")
    dx1, big["up0"], big["down0"], s_ffn0 = _ffn_backward(dx2, x1, ffn0, m0[5], m0[4], saved0, wts["up0"],
                                                          vecs["conv_w0"], vecs["conv_b0"], wts["down0"], "l0_ffn")
    dy0, dg1_0 = _branch_bwd(dx1, y0, m0[2], "l0_mix_gate_bwd")
    dyp = _mm_nt(dy0, wts["a_w_out"], F32, "l0_out_dx")
    big["a_w_out"] = _mm_tn(yp, dy0, 1, "l0_out_dw")
    dproj, dlb, dng = _hgrn_bwd(proj, lb, vecs["a_norm_g"], o_a, states, dyp, "l0_hgrn_bwd")
    dh0 = _mm_nt(dproj, wts["a_w_in"], F32, "l0_in_dx")
    big["a_w_in"] = _mm_tn(h0, dproj, N_CHIPS, "l0_in_dw")
    grad_x, dsh1_0, dsc1_0 = _premix_bwd(x, dh0, m0[1], dx1, "l0_premix_bwd")

    small["mod_l0"] = [dsh1_0, dsc1_0, dg1_0, s_ffn0["shift"], s_ffn0["scale"], s_ffn0["gate"]]
    small["mod_l1"] = [dsh1_1, dsc1_1, dg1_1, s_ffn1["shift"], s_ffn1["scale"], s_ffn1["gate"]]
    small["mod_kv"] = [dshk, dsck]
    small["conv0"], small["conv1"] = s_ffn0["conv"], s_ffn1["conv"]
    small["a_norm_g"], small["k_norm_g"], small["q_norm_g"] = dng, dkg, dqg
    small["kv_b_f"], small["lb"] = dbf, dlb
    return sq, grad_x, big, small


HBM = pl.BlockSpec(memory_space=pltpu.HBM)
COMM_CHUNK_ELEMS = 256 * 1024


def _place():
    x, y, c = lax.axis_index("x"), lax.axis_index("y"), lax.axis_index("c")
    chips = [(1 - x, y), (x, 1 - y), (1 - x, 1 - y)]
    return x, y, c, (x, y, 1 - c), chips


def _chunk_rows(rows, cols):
    best = BF16_ROWS
    for r in range(BF16_ROWS, rows + 1, BF16_ROWS):
        if rows % r == 0 and r * cols <= COMM_CHUNK_ELEMS:
            best = r
    assert rows % best == 0, (rows, cols)
    return best


def _allgather8(block, name):
    m_per, n = block.shape

    def body(x_ref, out_ref, send_sems, recv_sems, local_sem):
        x, y, c, sibling, chips = _place()
        me = (x, y, c)

        def rows(px, py, pc):
            return out_ref.at[pl.ds((4 * px + 2 * py + pc) * m_per, m_per), :]

        def copy(k, blk, to, src=None):
            return pltpu.make_async_remote_copy(
                src_ref=rows(*blk) if src is None else src, dst_ref=rows(*blk),
                send_sem=send_sems.at[k], recv_sem=recv_sems.at[k], device_id=to, device_id_type=MESH)

        mine = pltpu.make_async_copy(x_ref, rows(*me), local_sem)
        mine.start()
        first = [copy(0, me, sibling, src=x_ref)]
        first += [copy(1 + j, me, (*chip, c), src=x_ref) for j, chip in enumerate(chips)]
        for cp in first:
            cp.start()
        passed = [copy(4 + j, (*chip, c), sibling) for j, chip in enumerate(chips)]
        for j, chip in enumerate(chips):
            copy(1 + j, (*chip, c), me).wait_recv()
            passed[j].start()
        copy(0, sibling, me).wait_recv()
        for j, chip in enumerate(chips):
            copy(4 + j, (*chip, 1 - c), me).wait_recv()
        for cp in first + passed:
            cp.wait_send()
        mine.wait()

    return pl.pallas_call(
        body, name=name, out_shape=jax.ShapeDtypeStruct((N_DEV * m_per, n), block.dtype),
        in_specs=[pl.BlockSpec(memory_space=pltpu.VMEM)], out_specs=pl.BlockSpec(memory_space=pltpu.VMEM),
        scratch_shapes=[pltpu.SemaphoreType.DMA((7,)), pltpu.SemaphoreType.DMA((7,)), pltpu.SemaphoreType.DMA],
    )(block)


def _gather_weights(shards, name):
    n_t = len(shards)
    dims = [s.shape for s in shards]

    def body(*refs):
        ins, outs = refs[:n_t], refs[n_t:2 * n_t]
        send_ici, recv_ici, send_d2d, recv_d2d = refs[2 * n_t:]
        x, y, c, sibling, chips = _place()
        p_me = 2 * x + y

        def halves(t, count):
            return outs[t].at[pl.ds(0, count), pl.ds(0, dims[t][0] // 2), :]

        def waiter(t, sem_s, sem_r):
            win = halves(t, 3)
            return pltpu.make_async_remote_copy(src_ref=win, dst_ref=win, send_sem=sem_s.at[t], recv_sem=sem_r.at[t],
                                                device_id=sibling, device_id_type=MESH)

        def half_copy(t, chip_idx, to, sem_s, sem_r):
            r2 = dims[t][0] // 2
            win = outs[t].at[chip_idx, pl.ds(c * r2, r2), :]
            return pltpu.make_async_remote_copy(src_ref=win, dst_ref=win, send_sem=sem_s.at[t], recv_sem=sem_r.at[t],
                                                device_id=to, device_id_type=MESH)

        for t in range(n_t):
            r, cols = dims[t]
            rows = _chunk_rows(r, cols)

            def cast(fbuf, bbuf, t=t, r=r, rows=rows):
                for k in range(r // rows):
                    pltpu.sync_copy(ins[t].at[pl.ds(k * rows, rows), :], fbuf)
                    bbuf[...] = fbuf[...].astype(BF16)
                    pltpu.sync_copy(bbuf, outs[t].at[p_me, pl.ds(k * rows, rows), :])

            pl.run_scoped(cast, pltpu.VMEM((rows, cols), F32), pltpu.VMEM((rows, cols), BF16))
            for chip in chips:
                half_copy(t, p_me, (*chip, c), send_ici, recv_ici).start()
        for t in range(n_t):
            waiter(t, send_ici, recv_ici).wait_recv()
            for cx, cy in chips:
                half_copy(t, 2 * cx + cy, sibling, send_d2d, recv_d2d).start()
        for t in range(n_t):
            waiter(t, send_d2d, recv_d2d).wait_recv()
            waiter(t, send_ici, recv_ici).wait_send()
            waiter(t, send_d2d, recv_d2d).wait_send()

    return pl.pallas_call(
        body, name=name, in_specs=[HBM] * n_t, out_specs=[HBM] * n_t,
        out_shape=[jax.ShapeDtypeStruct((N_CHIPS, r, cols), BF16) for r, cols in dims],
        scratch_shapes=[pltpu.SemaphoreType.DMA((n_t,))] * 4,
    )(*shards)


def _cast_own_block(shard, chip, name):
    r, cols = shard.shape
    rows = _chunk_rows(r, cols)

    def body(chip_ref, w_ref, o_ref):
        o_ref[...] = w_ref[...].astype(BF16)

    return pl.pallas_call(
        body, name=name,
        grid_spec=pltpu.PrefetchScalarGridSpec(
            num_scalar_prefetch=1, grid=(r // rows,),
            in_specs=[pl.BlockSpec((rows, cols), lambda i, chip_ref: (i, 0))],
            out_specs=pl.BlockSpec((None, rows, cols), lambda i, chip_ref: (chip_ref[0], i, 0))),
        out_shape=jax.ShapeDtypeStruct((N_CHIPS, r, cols), BF16),
    )(chip, shard)


def _sequencer_gather(bufs, name, collective_id):
    n_t = len(bufs)
    dims = [b.shape[1:] for b in bufs]
    refs = [jax.new_ref(b, memory_space=pltpu.MemorySpace.HBM) for b in bufs]

    @pl.kernel(mesh=plsc.ScalarSubcoreMesh(axis_name="sequencer", num_cores=1), name=name,
               scratch_types=[pltpu.SemaphoreType.DMA((n_t,))] * 4,
               compiler_params=pltpu.CompilerParams(collective_id=collective_id))
    def launch(send_ici, recv_ici, send_d2d, recv_d2d):
        x, y, c, sibling, chips = _place()
        p_me = 2 * x + y
        peers = [sibling] + [(cx, cy, c) for cx, cy in chips]
        barrier = pltpu.get_barrier_semaphore()
        for peer in peers:
            pl.semaphore_signal(barrier, inc=1, device_id=peer, device_id_type=MESH)
        pl.semaphore_wait(barrier, len(peers))

        def waiter(t, sem_s, sem_r):
            win = refs[t].at[pl.ds(0, 3), pl.ds(0, dims[t][0] // 2), :]
            return pltpu.make_async_remote_copy(src_ref=win, dst_ref=win, send_sem=sem_s.at[t], recv_sem=sem_r.at[t],
                                                device_id=sibling, device_id_type=MESH)

        def half_copy(t, chip_idx, to, sem_s, sem_r):
            r2 = dims[t][0] // 2
            win = refs[t].at[chip_idx, pl.ds(c * r2, r2), :]
            return pltpu.make_async_remote_copy(src_ref=win, dst_ref=win, send_sem=sem_s.at[t], recv_sem=sem_r.at[t],
                                                device_id=to, device_id_type=MESH)

        for t in range(n_t):
            for cx, cy in chips:
                half_copy(t, p_me, (cx, cy, c), send_ici, recv_ici).start()
        for t in range(n_t):
            waiter(t, send_ici, recv_ici).wait_recv()
            for cx, cy in chips:
                half_copy(t, 2 * cx + cy, sibling, send_d2d, recv_d2d).start()
        for t in range(n_t):
            waiter(t, send_d2d, recv_d2d).wait_recv()
            waiter(t, send_ici, recv_ici).wait_send()
            waiter(t, send_d2d, recv_d2d).wait_send()

    launch()
    return [r[...] for r in refs]


def _reduce_scatter(parts, name):
    n_t = len(parts)
    dims = [p.shape[1:] for p in parts]

    def body(*refs):
        ins = refs[:n_t]
        outs, from_sib, chip_sum, from_chips = (refs[(1 + k) * n_t:(2 + k) * n_t] for k in range(4))
        s1, r1, s2, r2, s3, r3 = refs[5 * n_t:]
        x, y, c, sibling, chips = _place()
        p_me = 2 * x + y

        def remote(src, dst, sem_s, sem_r, t, to):
            return pltpu.make_async_remote_copy(src_ref=src, dst_ref=dst, send_sem=sem_s.at[t], recv_sem=sem_r.at[t],
                                                device_id=to, device_id_type=MESH)

        def swap1(t):
            h = dims[t][0] // 2
            return remote(ins[t].at[:, pl.ds((1 - c) * h, h), :], from_sib[t], s1, r1, t, sibling)

        def to_chips(t):
            return remote(from_chips[t], from_chips[t], s2, r2, t, sibling)

        def swap3(t):
            h = dims[t][0] // 2
            win = outs[t].at[pl.ds(c * h, h), :]
            return remote(win, win, s3, r3, t, sibling)

        for t in range(n_t):
            swap1(t).start()
        for t in range(n_t):
            r, cols = dims[t]
            h = r // 2
            rows = _chunk_rows(h, cols)
            swap1(t).wait_recv()

            def pair_sum(a, b, o, t=t, h=h, rows=rows):
                for p in range(N_CHIPS):
                    for k in range(h // rows):
                        pltpu.sync_copy(ins[t].at[p, pl.ds(c * h + k * rows, rows), :], a)
                        pltpu.sync_copy(from_sib[t].at[p, pl.ds(k * rows, rows), :], b)
                        o[...] = (a[...].astype(F32) + b[...].astype(F32)).astype(BF16)
                        pltpu.sync_copy(o, chip_sum[t].at[p, pl.ds(k * rows, rows), :])

            pl.run_scoped(pair_sum, *[pltpu.VMEM((rows, cols), BF16)] * 3)
            for j, (cx, cy) in enumerate(chips):
                remote(chip_sum[t].at[2 * cx + cy], from_chips[t].at[j], s2, r2, t, (cx, cy, c)).start()
        for t in range(n_t):
            r, cols = dims[t]
            h = r // 2
            rows = _chunk_rows(h, cols)
            to_chips(t).wait_recv()

            def total(a, b0, b1, b2, o, t=t, h=h, rows=rows):
                for k in range(h // rows):
                    pltpu.sync_copy(chip_sum[t].at[p_me, pl.ds(k * rows, rows), :], a)
                    for j, b in enumerate((b0, b1, b2)):
                        pltpu.sync_copy(from_chips[t].at[j, pl.ds(k * rows, rows), :], b)
                    o[...] = ((a[...].astype(F32) + b0[...].astype(F32)) + b1[...].astype(F32)) + b2[...].astype(F32)
                    pltpu.sync_copy(o, outs[t].at[pl.ds(c * h + k * rows, rows), :])

            pl.run_scoped(total, *([pltpu.VMEM((rows, cols), BF16)] * 4 + [pltpu.VMEM((rows, cols), F32)]))
            swap3(t).start()
        for t in range(n_t):
            swap3(t).wait_recv()
            swap1(t).wait_send()
            to_chips(t).wait_send()
            swap3(t).wait_send()

    half = lambda n, rc: jax.ShapeDtypeStruct((n, rc[0] // 2, rc[1]), BF16)
    out_shape = ([jax.ShapeDtypeStruct(rc, F32) for rc in dims] + [half(N_CHIPS, rc) for rc in dims]
                 + [half(N_CHIPS, rc) for rc in dims] + [half(N_CHIPS - 1, rc) for rc in dims])
    outs = pl.pallas_call(
        body, name=name, in_specs=[HBM] * n_t, out_specs=[HBM] * (4 * n_t), out_shape=out_shape,
        scratch_shapes=[pltpu.SemaphoreType.DMA((n_t,))] * 6,
    )(*parts)
    return outs[:n_t]


def _cond_rows(c16, w, act, name):
    n_l, dm, wid = w.shape

    def body(c_ref, w_ref, o_ref, a_ref):
        cv = c_ref[...]
        if act:
            cv = cv * _sig(cv)
        a_ref[...] = cv
        o_ref[...] = _dot_f32(cv, w_ref[...])

    return pl.pallas_call(
        body, name=name, grid=(n_l,),
        in_specs=[_full((16, dm)), pl.BlockSpec((None, dm, wid), lambda l: (l, 0, 0))],
        out_specs=[pl.BlockSpec((None, 16, wid), lambda l: (l, 0, 0)), _full((16, dm))],
        out_shape=[jax.ShapeDtypeStruct((n_l, 16, wid), F32), jax.ShapeDtypeStruct((16, dm), F32)],
    )(c16, w)


def _outer_grad(ct, dm, name):
    n_l, kk, wid = dm.shape
    d_rows = ct.shape[0]

    def body(c_ref, d_ref, o_ref):
        o_ref[...] = _dot_f32(c_ref[...], d_ref[...])

    return pl.pallas_call(
        body, name=name, grid=(n_l,),
        in_specs=[_full((d_rows, kk)), pl.BlockSpec((None, kk, wid), lambda l: (l, 0, 0))],
        out_specs=pl.BlockSpec((None, d_rows, wid), lambda l: (l, 0, 0)),
        out_shape=jax.ShapeDtypeStruct((n_l, d_rows, wid), F32),
    )(ct, dm)


def _sum_devices(g, name):
    rows, n = g.shape

    def body(g_ref, o_ref):
        acc = g_ref[0:SUBLANES, :]
        for dev in range(1, N_DEV):
            acc = acc + g_ref[dev * SUBLANES:(dev + 1) * SUBLANES, :]
        o_ref[...] = acc

    return pl.pallas_call(body, name=name, out_shape=jax.ShapeDtypeStruct((SUBLANES, n), F32))(g)


def _adamw(w, g, m, v, name):
    shape = w.shape
    cols = shape[-1]
    rows = w.size // cols
    tr = rows
    for cand in range(SUBLANES, min(rows, 256) + 1, SUBLANES):
        if rows % cand == 0:
            tr = cand
    if rows * cols <= COMM_CHUNK_ELEMS:
        tr = rows
    c1 = 1.0 / (1.0 - ADAM_B1 ** ADAM_STEP)
    c2 = 1.0 / (1.0 - ADAM_B2 ** ADAM_STEP)

    def body(w_ref, g_ref, m_ref, v_ref, d_ref, mo_ref, vo_ref):
        gv = g_ref[...]
        m_new = ADAM_B1 * m_ref[...] + (1.0 - ADAM_B1) * gv
        v_new = ADAM_B2 * v_ref[...] + (1.0 - ADAM_B2) * (gv * gv)
        mo_ref[...] = m_new
        vo_ref[...] = v_new
        d_ref[...] = -ADAM_LR * ((m_new * c1) / (jnp.sqrt(v_new * c2) + ADAM_EPS) + ADAM_WD * w_ref[...])

    spec = pl.BlockSpec((tr, cols), lambda i: (i, 0))
    outs = pl.pallas_call(
        body, name=name, grid=(rows // tr,), in_specs=[spec] * 4, out_specs=[spec] * 3,
        out_shape=[jax.ShapeDtypeStruct((rows, cols), F32)] * 3,
    )(*[a.reshape(rows, cols) for a in (w, g, m, v)])
    return tuple(o.reshape(shape) for o in outs)


def _pad_cols(a, cols):
    return jnp.pad(a, [(0, 0)] * (a.ndim - 1) + [(0, cols - a.shape[-1])])


def _flat8(parts, width):
    v = jnp.concatenate([p.reshape(-1) for p in parts])
    return jnp.pad(v, (0, width - v.shape[0])).reshape(SUBLANES, width // SUBLANES)


KV_SHARD = 514
KV_SHARD_PAD = 640
BIG = ("a_w_in", "a_w_out", "kv_w", "b_w_q", "b_w_out", "up0", "up1", "down0", "down1")


def kernel(x, c, ada_w, ada_b, a_w_in, a_lb_logits, a_norm_g, a_w_out, kv_ada_w, kv_ada_b, kv_w, kv_b_f, k_norm_g, b_w_q, q_norm_g, b_w_out, ffn_w_up, ffn_conv_w, ffn_conv_b, ffn_w_down, loss_target, m_ada_w, m_ada_b, m_a_w_in, m_a_lb_logits, m_a_norm_g, m_a_w_out, m_kv_ada_w, m_kv_ada_b, m_kv_w, m_kv_b_f, m_k_norm_g, m_b_w_q, m_q_norm_g, m_b_w_out, m_ffn_w_up, m_ffn_conv_w, m_ffn_conv_b, m_ffn_w_down, v_ada_w, v_ada_b, v_a_w_in, v_a_lb_logits, v_a_norm_g, v_a_w_out, v_kv_ada_w, v_kv_ada_b, v_kv_w, v_kv_b_f, v_k_norm_g, v_b_w_q, v_q_norm_g, v_b_w_out, v_ffn_w_up, v_ffn_conv_w, v_ffn_conv_b, v_ffn_w_down):
    dm, ff = D_MODEL, D_FF
    ix, iy, ic = lax.axis_index("x"), lax.axis_index("y"), lax.axis_index("c")
    chip = 2 * ix + iy
    dev = 2 * chip + ic

    w1 = 10240
    g1 = _allgather8(_flat8([c, a_lb_logits, ffn_conv_w], w1), "gather_cond").reshape(N_DEV, w1)
    c_all = g1[:, :dm]
    per_chip = g1[0::2]
    lb_logits = per_chip[:, dm:dm + 512].reshape(N_CHIPS, 2, 256).transpose(1, 0, 2).reshape(2, dm)
    conv_w = per_chip[:, dm + 512:dm + 512 + 2 * CONV_W * FFN_COLS].reshape(N_CHIPS, 2, CONV_W, FFN_COLS)
    conv_w = conv_w.transpose(1, 2, 0, 3).reshape(2, CONV_W, 2, ff).transpose(0, 2, 1, 3)
    conv_b = ffn_conv_b.reshape(2, 2, 1, ff)
    lb = jax.nn.softmax(lb_logits, axis=0)[0:1]

    c16 = jnp.pad(c_all, ((0, 8), (0, 0)))
    mod_ada, c_act16 = _cond_rows(c16, ada_w, True, "mod_ada")
    mod_kv, _ = _cond_rows(c16, kv_ada_w[None], True, "mod_kv")
    mine = jnp.concatenate([mod_ada[0, :8], mod_ada[1, :8], mod_kv[0, :8]], axis=1)
    w2 = mine.shape[1]
    g2 = _allgather8(mine, "gather_mod").reshape(N_DEV, 8, w2)[0::2]
    my_rows = lax.dynamic_index_in_dim(g2, dev, axis=1, keepdims=False)
    mod0 = my_rows[:, 0:1536].reshape(6 * dm) + ada_b[0]
    mod1 = my_rows[:, 1536:3072].reshape(6 * dm) + ada_b[1]
    modk = my_rows[:, 3072:3584].reshape(2 * dm) + kv_ada_b
    mods = {"l0": [v.reshape(1, dm) for v in jnp.split(mod0, 6)],
            "l1": [v.reshape(1, dm) for v in jnp.split(mod1, 6)],
            "kv": [v.reshape(1, dm) for v in jnp.split(modk, 2)]}

    local = [a_w_in[0], a_w_out[0], _pad_cols(kv_w, KV_SHARD_PAD), b_w_q[0], b_w_out[0], ffn_w_up[0], ffn_w_up[1],
             ffn_w_down[0], ffn_w_down[1]]
    chip_arr = chip.reshape(1).astype(jnp.int32)
    own = {n: _cast_own_block(w, chip_arr, "cast_" + n) for n, w in zip(BIG, local)}
    stages = {"mixer0": ("a_w_in",), "ffn0": ("a_w_out", "up0", "down0"),
              "layer1": ("kv_w", "b_w_q", "b_w_out", "up1", "down1")}
    arriving = {st: _sequencer_gather([own[n] for n in names], "gather_" + st, cid)
                for cid, (st, names) in enumerate(stages.items(), start=1)}
    rowwise = lambda g: g.reshape(1, -1, dm)

    def weights_at(stage, token):
        got, token = lax.optimization_barrier((arriving[stage], token))
        g = dict(zip(stages[stage], got))
        if stage == "mixer0":
            return {"a_w_in": g["a_w_in"]}, token
        if stage == "ffn0":
            return {"a_w_out": rowwise(g["a_w_out"]), "up0": g["up0"], "down0": rowwise(g["down0"])}, token
        kv_full = g["kv_w"][:, :, :KV_SHARD].transpose(1, 0, 2).reshape(dm, N_CHIPS * KV_SHARD)
        return {"kv_k": kv_full[None, :, :dm], "kv_v": kv_full[None, :, dm:2 * dm],
                "kv_f": _pad_cols(kv_full[None, :, 2 * dm:], LANES), "b_w_q": g["b_w_q"],
                "b_w_out": rowwise(g["b_w_out"]), "up1": g["up1"], "down1": rowwise(g["down1"])}, token

    vecs = {"a_norm_g": jnp.tile(a_norm_g, (1, HEADS)), "k_norm_g": jnp.tile(k_norm_g[None], (1, HEADS)),
            "q_norm_g": jnp.tile(q_norm_g, (1, HEADS)), "kv_b_f": _pad_cols(kv_b_f[None], LANES),
            "conv_w0": conv_w[0], "conv_b0": conv_b[0], "conv_w1": conv_w[1], "conv_b1": conv_b[1]}

    sq, grad_x, big, small = _local_step(x[0], loss_target[0], mods, lb, vecs, weights_at)
    loss = lax.psum(0.5 * jnp.sum(sq) / dm, ("x", "y", "c"))

    kv_grad = jnp.concatenate([big["kv_k"][0], big["kv_v"][0], big["kv_f"][0][:, :HEADS]], axis=1)
    kv_grad = _pad_cols(kv_grad.reshape(dm, N_CHIPS, KV_SHARD).transpose(1, 0, 2), KV_SHARD_PAD)
    chipwise = lambda g: g.reshape(N_CHIPS, -1, dm)
    parts = [big["a_w_in"], chipwise(big["a_w_out"]), kv_grad, big["b_w_q"], chipwise(big["b_w_out"]),
             big["up0"], big["up1"], chipwise(big["down0"]), chipwise(big["down1"])]
    rs = dict(zip(BIG, _reduce_scatter(parts, "reduce_grads")))

    fold = lambda a: a.sum(axis=0)
    heads = lambda a: fold(a).reshape(HEADS, HEAD_DIM).sum(axis=0)
    conv_flat = lambda a: a.sum(axis=2).transpose(1, 0, 2)
    pieces = ([fold(a) for a in small["mod_l0"]] + [fold(a) for a in small["mod_l1"]] + [fold(a) for a in small["mod_kv"]]
              + [conv_flat(small["conv0"]), conv_flat(small["conv1"]), heads(small["a_norm_g"]), heads(small["k_norm_g"]),
                 heads(small["q_norm_g"]), fold(small["kv_b_f"]), fold(small["lb"])])
    w3 = 61440
    g3 = _allgather8(_flat8(pieces, w3), "gather_small")
    tot = _sum_devices(g3, "sum_small").reshape(w3)
    n_mod = 14 * dm
    dmod_all = g3.reshape(N_DEV, w3)[:, :n_mod]
    o = n_mod
    conv_tot = [tot[o + l * 8 * ff: o + (l + 1) * 8 * ff].reshape(4, 2 * ff) for l in range(2)]
    o += 16 * ff
    g_a_norm, g_k_norm, g_q_norm = (tot[o + i * HEAD_DIM: o + (i + 1) * HEAD_DIM] for i in range(3))
    o += 3 * HEAD_DIM
    g_kv_b_f = tot[o:o + HEADS]
    dlb = tot[o + LANES:o + LANES + dm]

    ct = _pad_cols(c_act16[:8].T, LANES)
    dmod_pad = jnp.pad(dmod_all, ((0, LANES - N_DEV), (0, 0)))
    cols_ada = jnp.stack([lax.dynamic_slice_in_dim(dmod_pad, l * 6 * dm + chip * 1536, 1536, axis=1) for l in range(2)])
    cols_kv = lax.dynamic_slice_in_dim(dmod_pad, 12 * dm + chip * 512, 512, axis=1)[None]
    g_ada_w = _outer_grad(ct, cols_ada, "grad_ada_w")
    g_kv_ada_w = _outer_grad(ct, cols_kv, "grad_kv_ada_w")[0]

    my_lb = lax.dynamic_slice_in_dim(lb[0], chip * 256, 256)
    l0 = lax.dynamic_slice_in_dim(dlb, chip * 256, 256) * my_lb * (1.0 - my_lb)
    grads = {
        "ada_w": g_ada_w, "ada_b": jnp.stack([tot[:6 * dm], tot[6 * dm:12 * dm]]),
        "a_w_in": rs["a_w_in"][None], "a_lb_logits": jnp.stack([l0, -l0]), "a_norm_g": g_a_norm[None],
        "a_w_out": rs["a_w_out"][None], "kv_ada_w": g_kv_ada_w, "kv_ada_b": tot[12 * dm:14 * dm],
        "kv_w": rs["kv_w"][:, :KV_SHARD], "kv_b_f": g_kv_b_f, "k_norm_g": g_k_norm,
        "b_w_q": rs["b_w_q"][None], "q_norm_g": g_q_norm[None], "b_w_out": rs["b_w_out"][None],
        "ffn_w_up": jnp.stack([rs["up0"], rs["up1"]]),
        "ffn_conv_w": jnp.stack([lax.dynamic_slice_in_dim(ct_l[:CONV_W], chip * FFN_COLS, FFN_COLS, axis=1) for ct_l in conv_tot]),
        "ffn_conv_b": jnp.stack([ct_l[CONV_W] for ct_l in conv_tot]),
        "ffn_w_down": jnp.stack([rs["down0"], rs["down1"]]),
    }
    weights = dict(ada_w=ada_w, ada_b=ada_b, a_w_in=a_w_in, a_lb_logits=a_lb_logits, a_norm_g=a_norm_g, a_w_out=a_w_out,
                   kv_ada_w=kv_ada_w, kv_ada_b=kv_ada_b, kv_w=kv_w, kv_b_f=kv_b_f, k_norm_g=k_norm_g, b_w_q=b_w_q,
                   q_norm_g=q_norm_g, b_w_out=b_w_out, ffn_w_up=ffn_w_up, ffn_conv_w=ffn_conv_w, ffn_conv_b=ffn_conv_b,
                   ffn_w_down=ffn_w_down)
    m_in = dict(ada_w=m_ada_w, ada_b=m_ada_b, a_w_in=m_a_w_in, a_lb_logits=m_a_lb_logits, a_norm_g=m_a_norm_g,
                a_w_out=m_a_w_out, kv_ada_w=m_kv_ada_w, kv_ada_b=m_kv_ada_b, kv_w=m_kv_w, kv_b_f=m_kv_b_f,
                k_norm_g=m_k_norm_g, b_w_q=m_b_w_q, q_norm_g=m_q_norm_g, b_w_out=m_b_w_out, ffn_w_up=m_ffn_w_up,
                ffn_conv_w=m_ffn_conv_w, ffn_conv_b=m_ffn_conv_b, ffn_w_down=m_ffn_w_down)
    v_in = dict(ada_w=v_ada_w, ada_b=v_ada_b, a_w_in=v_a_w_in, a_lb_logits=v_a_lb_logits, a_norm_g=v_a_norm_g,
                a_w_out=v_a_w_out, kv_ada_w=v_kv_ada_w, kv_ada_b=v_kv_ada_b, kv_w=v_kv_w, kv_b_f=v_kv_b_f,
                k_norm_g=v_k_norm_g, b_w_q=v_b_w_q, q_norm_g=v_q_norm_g, b_w_out=v_b_w_out, ffn_w_up=v_ffn_w_up,
                ffn_conv_w=v_ffn_conv_w, ffn_conv_b=v_ffn_conv_b, ffn_w_down=v_ffn_w_down)

    names = list(weights)
    grads = {n: grads[n].reshape(weights[n].shape) for n in names}
    upd = {n: _adamw(weights[n], grads[n], m_in[n], v_in[n], "adamw_" + n) for n in names}
    return (loss, grad_x[None], *[grads[n] for n in names], *[upd[n][0] for n in names],
            *[upd[n][1] for n in names], *[upd[n][2] for n in names])
```

```python
import jax
import jax.numpy as jnp
from jax import lax
from jax.experimental import pallas as pl
from jax.experimental.pallas import tpu as pltpu
from jax.experimental.pallas import tpu_sc as plsc

F32 = jnp.float32
BF16 = jnp.bfloat16

D_MODEL = 1024
HEADS = 8
HEAD_DIM = 128
A_CHUNK = 64
D_FF = 2816
CONV_W = 3
EPS = 1e-6
NEG_INF = -1e30
N_CHIPS = 4
N_DEV = 8

ADAM_LR = 0.001
ADAM_B1 = 0.9
ADAM_B2 = 0.999
ADAM_EPS = 1e-08
ADAM_WD = 0.01
ADAM_STEP = 10

SUBLANES = 8
BF16_ROWS = 16
LANES = 128
HALO = BF16_ROWS
ROW_TILE = 512
FFN_COLS = 1408
HGRN_ROWS = 256
ATT_TILE = 512
ATT_SPLIT = 2
MESH = pl.DeviceIdType.MESH


def _sig(x):
    return jax.nn.sigmoid(x)


def _dot(a, b):
    return jnp.dot(a, b, preferred_element_type=F32)


def _dot_nt(a, b):
    return lax.dot_general(a, b, (((1,), (1,)), ((), ())), preferred_element_type=F32)


def _dot_tn(a, b):
    return lax.dot_general(a, b, (((0,), (0,)), ((), ())), preferred_element_type=F32)


def _split2(x):
    hi = x.astype(BF16)
    lo = (x - hi.astype(F32)).astype(BF16)
    return hi, lo


def _dot_f32(a, b):
    ah, al = _split2(a)
    bh, bl = _split2(b)
    return _dot(ah, bh) + _dot(ah, bl) + _dot(al, bh)


def _tri_dot(tri, x):
    hi = x.astype(BF16)
    r = x - hi.astype(F32)
    mid = r.astype(BF16)
    lo = (r - mid.astype(F32)).astype(BF16)
    return _dot(tri, hi) + _dot(tri, mid) + _dot(tri, lo)


def _tri(n, upper=False):
    r = lax.broadcasted_iota(jnp.int32, (n, n), 0)
    c = lax.broadcasted_iota(jnp.int32, (n, n), 1)
    keep = (c >= r) if upper else (c <= r)
    return jnp.where(keep, 1.0, 0.0).astype(BF16)


def _colsum8(v):
    rows, n = v.shape
    return v.reshape(rows // SUBLANES, SUBLANES, n).sum(axis=0)


def _full(shape):
    nd = len(shape)
    return pl.BlockSpec(shape, lambda *_: (0,) * nd)


def _tile(n, want):
    t = min(n, want)
    assert n % t == 0, (n, t)
    return t


def _mm_nn(a, w, groups, out_dtype, name):
    m_rows, k = a.shape
    p_n, _, n = w.shape
    per = p_n // groups
    tm = _tile(m_rows, ROW_TILE)

    def body(a_ref, w_ref, o_ref):
        av = a_ref[...]
        for p in range(p_n):
            o_ref[p // per, :, (p % per) * n:(p % per + 1) * n] = _dot(av, w_ref[p]).astype(out_dtype)

    return pl.pallas_call(
        body, name=name, grid=(m_rows // tm,),
        in_specs=[pl.BlockSpec((tm, k), lambda i: (i, 0)), _full((p_n, k, n))],
        out_specs=pl.BlockSpec((groups, tm, per * n), lambda i: (0, i, 0)),
        out_shape=jax.ShapeDtypeStruct((groups, m_rows, per * n), out_dtype),
    )(a, w)


def _mm_nt(d, w, out_dtype, name, add=None):
    g_n, m_rows, _ = d.shape
    p_n, k, n = w.shape
    per = p_n // g_n
    tm = _tile(m_rows, ROW_TILE)

    def body(*refs):
        d_ref, w_ref = refs[0], refs[1]
        o_ref = refs[-1]
        acc = refs[2][...] if add is not None else None
        for p in range(p_n):
            t = _dot_nt(d_ref[p // per, :, (p % per) * n:(p % per + 1) * n], w_ref[p])
            acc = t if acc is None else acc + t
        o_ref[...] = acc.astype(out_dtype)

    ins = [d, w] + ([add] if add is not None else [])
    specs = [pl.BlockSpec((g_n, tm, per * n), lambda i: (0, i, 0)), _full((p_n, k, n))]
    if add is not None:
        specs.append(pl.BlockSpec((tm, k), lambda i: (i, 0)))
    return pl.pallas_call(
        body, name=name, grid=(m_rows // tm,), in_specs=specs,
        out_specs=pl.BlockSpec((tm, k), lambda i: (i, 0)),
        out_shape=jax.ShapeDtypeStruct((m_rows, k), out_dtype),
    )(*ins)


def _mm_tn(a, d, p_n, name):
    m_rows, k = a.shape
    g_n, _, w_cols = d.shape
    per = p_n // g_n
    n = w_cols // per
    tm = _tile(m_rows, ROW_TILE)
    steps = m_rows // tm

    def body(a_ref, d_ref, o_ref, acc):
        m = pl.program_id(1)

        @pl.when(m == 0)
        def _():
            acc[...] = jnp.zeros_like(acc)

        acc[...] += _dot_tn(a_ref[...], d_ref[...])

        @pl.when(m == steps - 1)
        def _():
            o_ref[...] = acc[...].astype(BF16)

    return pl.pallas_call(
        body, name=name, grid=(p_n, steps),
        in_specs=[pl.BlockSpec((tm, k), lambda p, m: (m, 0)),
                  pl.BlockSpec((None, tm, n), lambda p, m: (p // per, m, p % per))],
        out_specs=pl.BlockSpec((None, k, n), lambda p, m: (p, 0, 0)),
        out_shape=jax.ShapeDtypeStruct((p_n, k, n), BF16),
        scratch_shapes=[pltpu.VMEM((k, n), F32)],
    )(a, d)


def _premix(x, shift, scale, name, branch=None, gate=None):
    s, dm = x.shape
    tm = _tile(s, ROW_TILE)
    with_branch = branch is not None

    def body(*refs):
        x_ref, sh_ref, sc_ref = refs[:3]
        xv = x_ref[...]
        if with_branch:
            xv = xv + refs[4][...] * refs[3][...]
            refs[-2][...] = xv
        inv = lax.rsqrt(jnp.mean(xv * xv, axis=-1, keepdims=True) + EPS)
        refs[-1][...] = (xv * inv * (1.0 + sc_ref[...]) + sh_ref[...]).astype(BF16)

    row = pl.BlockSpec((tm, dm), lambda i: (i, 0))
    vec = _full((1, dm))
    ins, specs = [x, shift, scale], [row, vec, vec]
    out_shape, out_specs = [jax.ShapeDtypeStruct((s, dm), BF16)], [row]
    if with_branch:
        ins += [branch, gate]
        specs += [row, vec]
        out_shape.insert(0, jax.ShapeDtypeStruct((s, dm), F32))
        out_specs.insert(0, row)
    outs = pl.pallas_call(body, name=name, grid=(s // tm,), in_specs=specs, out_specs=out_specs,
                          out_shape=out_shape)(*ins)
    return tuple(outs) if with_branch else outs[0]


def _premix_bwd(x, dh, scale, dres, name):
    s, dm = x.shape
    tm = _tile(s, ROW_TILE)

    def body(x_ref, dh_ref, sc_ref, dres_ref, dx_ref, dsh_ref, dsc_ref):
        i = pl.program_id(0)

        @pl.when(i == 0)
        def _():
            dsh_ref[...] = jnp.zeros_like(dsh_ref)
            dsc_ref[...] = jnp.zeros_like(dsc_ref)

        xv = x_ref[...]
        dhv = dh_ref[...]
        inv = lax.rsqrt(jnp.mean(xv * xv, axis=-1, keepdims=True) + EPS)
        r = xv * inv
        dr = dhv * (1.0 + sc_ref[...])
        dx_ref[...] = dres_ref[...] + inv * (dr - r * jnp.mean(dr * r, axis=-1, keepdims=True))
        dsh_ref[...] += _colsum8(dhv)
        dsc_ref[...] += _colsum8(dhv * r)

    row = pl.BlockSpec((tm, dm), lambda i: (i, 0))
    acc = _full((SUBLANES, dm))
    return pl.pallas_call(
        body, name=name, grid=(s // tm,), in_specs=[row, row, _full((1, dm)), row],
        out_specs=[row, acc, acc],
        out_shape=[jax.ShapeDtypeStruct((s, dm), F32), jax.ShapeDtypeStruct((SUBLANES, dm), F32),
                   jax.ShapeDtypeStruct((SUBLANES, dm), F32)],
    )(x, dh, scale, dres)


def _branch_bwd(dx, y, gate, name):
    s, dm = dx.shape
    tm = _tile(s, ROW_TILE)

    def body(dx_ref, y_ref, g_ref, dy_ref, dg_ref):
        @pl.when(pl.program_id(0) == 0)
        def _():
            dg_ref[...] = jnp.zeros_like(dg_ref)

        dxv = dx_ref[...]
        dy_ref[0] = (dxv * g_ref[...]).astype(BF16)
        dg_ref[...] += _colsum8(dxv * y_ref[...])

    row = pl.BlockSpec((tm, dm), lambda i: (i, 0))
    return pl.pallas_call(
        body, name=name, grid=(s // tm,), in_specs=[row, row, _full((1, dm))],
        out_specs=[pl.BlockSpec((1, tm, dm), lambda i: (0, i, 0)), _full((SUBLANES, dm))],
        out_shape=[jax.ShapeDtypeStruct((1, s, dm), BF16), jax.ShapeDtypeStruct((SUBLANES, dm), F32)],
    )(dx, y, gate)


def _loss_head(x, branch, gate, target, name):
    s, dm = x.shape
    tm = _tile(s, ROW_TILE)

    def body(x_ref, b_ref, g_ref, t_ref, sq_ref, dy_ref):
        @pl.when(pl.program_id(0) == 0)
        def _():
            sq_ref[...] = jnp.zeros_like(sq_ref)

        err = x_ref[...] + g_ref[...] * b_ref[...] - t_ref[...]
        sq_ref[...] += _colsum8(err * err)
        dy_ref[...] = err * (1.0 / dm)

    row = pl.BlockSpec((tm, dm), lambda i: (i, 0))
    return pl.pallas_call(
        body, name=name, grid=(s // tm,), in_specs=[row, row, _full((1, dm)), row],
        out_specs=[_full((SUBLANES, dm)), row],
        out_shape=[jax.ShapeDtypeStruct((SUBLANES, dm), F32), jax.ShapeDtypeStruct((s, dm), F32)],
    )(x, branch, gate, target)


def _conv_taps(e, w, b):
    return w[2:3] * e + w[1:2] * pltpu.roll(e, 1, 0) + w[0:1] * pltpu.roll(e, 2, 0) + b


def _ffn_specs(s, tm, cb):
    hb = tm // HALO
    last = s // HALO - 1
    main = pl.BlockSpec((2, tm, cb), lambda j, i: (0, i, j))
    prev = pl.BlockSpec((2, HALO, cb), lambda j, i: (0, jnp.maximum(i * hb - 1, 0), j))
    nxt = pl.BlockSpec((2, HALO, cb), lambda j, i: (0, jnp.minimum((i + 1) * hb, last), j))
    wspec = pl.BlockSpec((2, CONV_W, cb), lambda j, i: (0, 0, j))
    bspec = pl.BlockSpec((2, 1, cb), lambda j, i: (0, 0, j))
    return main, prev, nxt, wspec, bspec


def _convglu_fwd(u, w, b, name):
    _, s, f = u.shape
    tm = _tile(s, 256)
    cb = _tile(f, FFN_COLS)
    main, prev, _, wspec, bspec = _ffn_specs(s, tm, cb)

    def body(u_ref, up_ref, w_ref, b_ref, a_ref):
        first = jnp.where(pl.program_id(1) > 0, 1.0, 0.0)

        def conv(g):
            e = jnp.concatenate([up_ref[g].astype(F32) * first, u_ref[g].astype(F32)], axis=0)
            return _conv_taps(e, w_ref[g], b_ref[g])[HALO:]

        gate = conv(0)
        a_ref[...] = (gate * _sig(gate) * conv(1)).astype(BF16)

    return pl.pallas_call(
        body, name=name, grid=(f // cb, s // tm), in_specs=[main, prev, wspec, bspec],
        out_specs=pl.BlockSpec((tm, cb), lambda j, i: (i, j)),
        out_shape=jax.ShapeDtypeStruct((s, f), BF16),
    )(u, u, w, b)


def _convglu_bwd(u, da, w, b, name):
    _, s, f = u.shape
    tm = _tile(s, 256)
    cb = _tile(f, FFN_COLS)
    steps = s // tm
    n_ext = tm + 2 * HALO
    main, prev, nxt, wspec, bspec = _ffn_specs(s, tm, cb)
    hb = tm // HALO
    last = s // HALO - 1
    da_main = pl.BlockSpec((tm, cb), lambda j, i: (i, j))
    da_next = pl.BlockSpec((HALO, cb), lambda j, i: (jnp.minimum((i + 1) * hb, last), j))

    def body(u_ref, up_ref, un_ref, da_ref, dan_ref, w_ref, b_ref, du_ref, acc_ref):
        i = pl.program_id(1)
        first = jnp.where(i > 0, 1.0, 0.0)
        notlast = jnp.where(i < steps - 1, 1.0, 0.0)

        @pl.when(i == 0)
        def _():
            acc_ref[...] = jnp.zeros_like(acc_ref)

        def ext(g):
            return jnp.concatenate([up_ref[g].astype(F32) * first, u_ref[g].astype(F32), un_ref[g].astype(F32)], axis=0)

        ug, uv = ext(0), ext(1)
        gate = _conv_taps(ug, w_ref[0], b_ref[0])
        val = _conv_taps(uv, w_ref[1], b_ref[1])
        da_e = jnp.concatenate([jnp.zeros((HALO, cb), F32), da_ref[...].astype(F32),
                                dan_ref[...].astype(F32) * notlast], axis=0)
        sg = _sig(gate)
        d_val = da_e * gate * sg
        d_gate = da_e * val * (sg * (1.0 + gate * (1.0 - sg)))

        def finish(g, d, e):
            wv = w_ref[g]
            du = wv[2:3] * d + wv[1:2] * pltpu.roll(d, n_ext - 1, 0) + wv[0:1] * pltpu.roll(d, n_ext - 2, 0)
            du_ref[g] = du[HALO:HALO + tm].astype(BF16)
            dm = d[HALO:HALO + tm]
            acc_ref[g, 2] += _colsum8(dm * e[HALO:HALO + tm])
            acc_ref[g, 1] += _colsum8(dm * pltpu.roll(e, 1, 0)[HALO:HALO + tm])
            acc_ref[g, 0] += _colsum8(dm * pltpu.roll(e, 2, 0)[HALO:HALO + tm])
            acc_ref[g, 3] += _colsum8(dm)

        finish(0, d_gate, ug)
        finish(1, d_val, uv)

    return pl.pallas_call(
        body, name=name, grid=(f // cb, steps),
        in_specs=[main, prev, nxt, da_main, da_next, wspec, bspec],
        out_specs=[main, pl.BlockSpec((2, 4, SUBLANES, cb), lambda j, i: (0, 0, 0, j))],
        out_shape=[jax.ShapeDtypeStruct((2, s, f), BF16), jax.ShapeDtypeStruct((2, 4, SUBLANES, f), F32)],
    )(u, u, u, da, da, w, b)


def _hgrn_gates(q_raw, f_raw, lb, tri):
    sf = _sig(f_raw)
    fg = lb + (1.0 - lb) * sf
    b = _tri_dot(tri, jnp.log(fg))
    return q_raw * _sig(q_raw), 1.0 - fg, b, fg, sf


def _hgrn_fwd(proj, lb, norm_g, name):
    s = proj.shape[0]
    tb = _tile(s, HGRN_ROWS)
    n_c = tb // A_CHUNK
    half = A_CHUNK // 2

    def body(q_ref, f_ref, v_ref, g_ref, lb_ref, ng_ref, o_ref, yp_ref, st_ref, state):
        @pl.when(pl.program_id(0) == 0)
        def _():
            state[...] = jnp.zeros_like(state)

        tri = _tri(A_CHUNK)
        causal = lax.broadcasted_iota(jnp.int32, (A_CHUNK, A_CHUNK), 1) <= lax.broadcasted_iota(
            jnp.int32, (A_CHUNK, A_CHUNK), 0)

        def chunk(ci, carry):
            rows = pl.ds(pl.multiple_of(ci * A_CHUNK, A_CHUNK), A_CHUNK)
            for h in range(HEADS):
                cs = slice(h * HEAD_DIM, (h + 1) * HEAD_DIM)
                qs, k, b, _, _ = _hgrn_gates(q_ref[rows, cs], f_ref[rows, cs], lb_ref[:, cs], tri)
                b_mid, b_last = b[half:half + 1], b[A_CHUNK - 1:A_CHUNK]
                vb = v_ref[rows, cs].astype(BF16)
                scores = _dot_nt((qs * jnp.exp(b - b_mid)).astype(BF16), (k * jnp.exp(b_mid - b)).astype(BF16))
                scores = jnp.where(causal, scores, 0.0)
                st = state[h]
                st_ref[ci, h] = st
                o = _dot(scores.astype(BF16), vb) + _dot_nt((qs * jnp.exp(b)).astype(BF16), st.astype(BF16))
                state[h] = st * jnp.exp(b_last) + _dot_tn(vb, (k * jnp.exp(b_last - b)).astype(BF16))
                o_ref[rows, cs] = o
                inv = lax.rsqrt(jnp.mean(o * o, axis=-1, keepdims=True) + EPS)
                g_raw = g_ref[rows, cs]
                yp_ref[rows, cs] = (o * inv * ng_ref[:, cs] * (g_raw * _sig(g_raw))).astype(BF16)
            return carry

        lax.fori_loop(0, n_c, chunk, 0)

    col = lambda j: pl.BlockSpec((tb, D_MODEL), lambda i: (i, j))
    vec = _full((1, D_MODEL))
    return pl.pallas_call(
        body, name=name, grid=(s // tb,), in_specs=[col(0), col(1), col(2), col(3), vec, vec],
        out_specs=[col(0), col(0), pl.BlockSpec((n_c, HEADS, HEAD_DIM, HEAD_DIM), lambda i: (i, 0, 0, 0))],
        out_shape=[jax.ShapeDtypeStruct((s, D_MODEL), F32), jax.ShapeDtypeStruct((s, D_MODEL), BF16),
                   jax.ShapeDtypeStruct((s // A_CHUNK, HEADS, HEAD_DIM, HEAD_DIM), F32)],
        scratch_shapes=[pltpu.VMEM((HEADS, HEAD_DIM, HEAD_DIM), F32)],
    )(proj, proj, proj, proj, lb, norm_g)


def _hgrn_bwd(proj, lb, norm_g, o, states, dyp, name):
    s = proj.shape[0]
    tb = _tile(s, HGRN_ROWS)
    n_c = tb // A_CHUNK
    n_b = s // tb
    half = A_CHUNK // 2

    def body(q_ref, f_ref, v_ref, g_ref, lb_ref, ng_ref, o_ref, st_ref, dyp_ref, dp_ref, dlb_ref, dng_ref, dstate):
        @pl.when(pl.program_id(0) == 0)
        def _():
            dstate[...] = jnp.zeros_like(dstate)
            dlb_ref[...] = jnp.zeros_like(dlb_ref)
            dng_ref[...] = jnp.zeros_like(dng_ref)

        tri = _tri(A_CHUNK)
        tri_up = _tri(A_CHUNK, upper=True)
        row_id = lax.broadcasted_iota(jnp.int32, (A_CHUNK, HEAD_DIM), 0)
        causal = lax.broadcasted_iota(jnp.int32, (A_CHUNK, A_CHUNK), 1) <= lax.broadcasted_iota(
            jnp.int32, (A_CHUNK, A_CHUNK), 0)

        def chunk(cj, carry):
            ci = n_c - 1 - cj
            rows = pl.ds(pl.multiple_of(ci * A_CHUNK, A_CHUNK), A_CHUNK)
            for h in range(HEADS):
                cs = slice(h * HEAD_DIM, (h + 1) * HEAD_DIM)
                q_raw, lbh = q_ref[rows, cs], lb_ref[:, cs]
                qs, k, b, fg, sf = _hgrn_gates(q_raw, f_ref[rows, cs], lbh, tri)
                b_mid, b_last = b[half:half + 1], b[A_CHUNK - 1:A_CHUNK]
                e_qi, e_ki, e_q, e_ks = jnp.exp(b - b_mid), jnp.exp(b_mid - b), jnp.exp(b), jnp.exp(b_last - b)
                q_i, k_i, q_e, k_s = qs * e_qi, k * e_ki, qs * e_q, k * e_ks
                vb = v_ref[rows, cs].astype(BF16)
                scores = jnp.where(causal, _dot_nt(q_i.astype(BF16), k_i.astype(BF16)), 0.0)
                ov, g_raw, dy, ng = o_ref[rows, cs], g_ref[rows, cs], dyp_ref[rows, cs], ng_ref[:, cs]
                inv = lax.rsqrt(jnp.mean(ov * ov, axis=-1, keepdims=True) + EPS)
                nrm = ov * inv
                sg = _sig(g_raw)
                gs = g_raw * sg
                dn = dy * ng * gs
                dng_ref[0:1, cs] += jnp.sum(dy * nrm * gs, axis=0, keepdims=True)
                dg_raw = dy * nrm * ng * (sg * (1.0 + g_raw * (1.0 - sg)))
                do = (inv * (dn - nrm * jnp.mean(dn * nrm, axis=-1, keepdims=True))).astype(BF16)
                st_prev = st_ref[ci, h]
                dst = dstate[h]
                dstb = dst.astype(BF16)
                d_scores = jnp.where(causal, _dot_nt(do, vb), 0.0).astype(BF16)
                dv = _dot_tn(scores.astype(BF16), do) + _dot_nt(k_s.astype(BF16), dstb)
                dq_i = _dot(d_scores, k_i.astype(BF16))
                dk_i = _dot_tn(d_scores, q_i.astype(BF16))
                dq_e = _dot(do, st_prev.astype(BF16))
                dk_s = _dot(vb, dstb)
                d_decay = jnp.sum(st_prev * dst, axis=0, keepdims=True)
                dstate[h] = dst * jnp.exp(b_last) + _dot_tn(do, q_e.astype(BF16))
                dq = dq_i * e_qi + dq_e * e_q
                dk = dk_i * e_ki + dk_s * e_ks
                t_qi, t_ki, t_ks = dq_i * q_i, dk_i * k_i, dk_s * k_s
                db = t_qi - t_ki + dq_e * q_e - t_ks
                db_mid = jnp.sum(t_ki - t_qi, axis=0, keepdims=True)
                db_last = jnp.sum(t_ks, axis=0, keepdims=True) + d_decay * jnp.exp(b_last)
                db = db + jnp.where(row_id == half, db_mid, 0.0) + jnp.where(row_id == A_CHUNK - 1, db_last, 0.0)
                dfg = _tri_dot(tri_up, db) / fg - dk
                dlb_ref[0:1, cs] += jnp.sum(dfg * (1.0 - sf), axis=0, keepdims=True)
                sq = _sig(q_raw)
                dp_ref[0, rows, cs] = (dq * (sq * (1.0 + q_raw * (1.0 - sq)))).astype(BF16)
                dp_ref[1, rows, cs] = (dfg * (1.0 - lbh) * sf * (1.0 - sf)).astype(BF16)
                dp_ref[2, rows, cs] = dv.astype(BF16)
                dp_ref[3, rows, cs] = dg_raw.astype(BF16)
            return carry

        lax.fori_loop(0, n_c, chunk, 0)

    col = lambda j: pl.BlockSpec((tb, D_MODEL), lambda i: (n_b - 1 - i, j))
    vec = _full((1, D_MODEL))
    acc = _full((SUBLANES, D_MODEL))
    return pl.pallas_call(
        body, name=name, grid=(n_b,),
        in_specs=[col(0), col(1), col(2), col(3), vec, vec, col(0),
                  pl.BlockSpec((n_c, HEADS, HEAD_DIM, HEAD_DIM), lambda i: (n_b - 1 - i, 0, 0, 0)), col(0)],
        out_specs=[pl.BlockSpec((4, tb, D_MODEL), lambda i: (0, n_b - 1 - i, 0)), acc, acc],
        out_shape=[jax.ShapeDtypeStruct((4, s, D_MODEL), BF16), jax.ShapeDtypeStruct((SUBLANES, D_MODEL), F32),
                   jax.ShapeDtypeStruct((SUBLANES, D_MODEL), F32)],
        scratch_shapes=[pltpu.VMEM((HEADS, HEAD_DIM, HEAD_DIM), F32)],
    )(proj, proj, proj, proj, lb, norm_g, o, states, dyp)


def _headnorm(x, g, mult, name, col0=0):
    s = x.shape[0]
    tm = _tile(s, ROW_TILE)

    def body(x_ref, g_ref, y_ref):
        for h in range(HEADS):
            cs = slice(h * HEAD_DIM, (h + 1) * HEAD_DIM)
            xv = x_ref[:, cs]
            inv = lax.rsqrt(jnp.mean(xv * xv, axis=-1, keepdims=True) + EPS)
            y_ref[:, cs] = (xv * inv * g_ref[:, cs] * mult).astype(BF16)

    return pl.pallas_call(
        body, name=name, grid=(s // tm,),
        in_specs=[pl.BlockSpec((tm, D_MODEL), lambda i: (i, col0)), _full((1, D_MODEL))],
        out_specs=pl.BlockSpec((tm, D_MODEL), lambda i: (i, 0)),
        out_shape=jax.ShapeDtypeStruct((s, D_MODEL), BF16),
    )(x, g)


def _headnorm_bwd(x, g, mult, dy, name, col0=0, extra=None):
    s = x.shape[0]
    tm = _tile(s, ROW_TILE)
    groups = 2 if extra is not None else 1

    def body(*refs):
        x_ref, g_ref, dy_ref = refs[:3]
        dx_ref, dg_ref = refs[-2:]

        @pl.when(pl.program_id(0) == 0)
        def _():
            dg_ref[...] = jnp.zeros_like(dg_ref)

        for h in range(HEADS):
            cs = slice(h * HEAD_DIM, (h + 1) * HEAD_DIM)
            xv, dyv, gv = x_ref[:, cs], dy_ref[:, cs], g_ref[:, cs]
            inv = lax.rsqrt(jnp.mean(xv * xv, axis=-1, keepdims=True) + EPS)
            nrm = xv * inv
            dn = dyv * gv * mult
            dg_ref[:, cs] += _colsum8(dyv * nrm * mult)
            dx_ref[0, :, cs] = (inv * (dn - nrm * jnp.mean(dn * nrm, axis=-1, keepdims=True))).astype(BF16)
        if extra is not None:
            dx_ref[1] = refs[3][...]

    row = pl.BlockSpec((tm, D_MODEL), lambda i: (i, 0))
    ins = [x, g, dy] + ([extra] if extra is not None else [])
    specs = [pl.BlockSpec((tm, D_MODEL), lambda i: (i, col0)), _full((1, D_MODEL)), row] + ([row] if extra is not None else [])
    return pl.pallas_call(
        body, name=name, grid=(s // tm,), in_specs=specs,
        out_specs=[pl.BlockSpec((groups, tm, D_MODEL), lambda i: (0, i, 0)), _full((SUBLANES, D_MODEL))],
        out_shape=[jax.ShapeDtypeStruct((groups, s, D_MODEL), BF16), jax.ShapeDtypeStruct((SUBLANES, D_MODEL), F32)],
    )(*ins)


def _log_sigmoid(z):
    return jnp.minimum(z, 0.0) - jnp.log(1.0 + jnp.exp(-jnp.abs(z)))


Q_CUM, Q_ONE, Q_LSE = 0, 3, 6
LOG2E = 1.4426950408889634


def _pieces(v):
    hi = v.astype(BF16).astype(F32)
    mid = (v - hi).astype(BF16).astype(F32)
    lo = ((v - hi) - mid).astype(BF16).astype(F32)
    return hi, mid, lo


def _side(lane, at, v):
    hi, mid, lo = _pieces(v)
    return jnp.where(lane == at, hi, jnp.where(lane == at + 1, mid, jnp.where(lane == at + 2, lo, 0.0)))


def _fcum_fwd(f, bias, name):
    s = f.shape[0]
    tm = _tile(s, ROW_TILE)

    def body(f_ref, b_ref, qa_ref, ka_ref, carry):
        @pl.when(pl.program_id(0) == 0)
        def _():
            carry[...] = jnp.zeros_like(carry)

        cum = _tri_dot(_tri(tm), _log_sigmoid(f_ref[...] + b_ref[...])) + carry[...]
        carry[...] = cum[tm - 1:tm]
        lane = lax.broadcasted_iota(jnp.int32, (tm, LANES), 1)
        ones_q = jnp.where((lane >= Q_ONE) & (lane < Q_LSE), 1.0, 0.0)
        ones_k = jnp.where((lane < Q_ONE) | ((lane >= Q_LSE) & (lane < Q_LSE + 3)), 1.0, 0.0)
        for h in range(HEADS):
            c2 = cum[:, h:h + 1] * LOG2E
            qa_ref[h] = (_side(lane, Q_CUM, c2) + ones_q).astype(BF16)
            ka_ref[h] = (_side(lane, Q_ONE, -c2) + ones_k).astype(BF16)

    side = pl.BlockSpec((HEADS, tm, LANES), lambda i: (0, i, 0))
    return pl.pallas_call(
        body, name=name, grid=(s // tm,),
        in_specs=[pl.BlockSpec((tm, LANES), lambda i: (i, 0)), _full((1, LANES))],
        out_specs=[side, side],
        out_shape=[jax.ShapeDtypeStruct((HEADS, s, LANES), BF16)] * 2,
        scratch_shapes=[pltpu.VMEM((1, LANES), F32)],
    )(f, bias)


def _fcum_bwd(f, bias, dka, dqa, name):
    s = f.shape[0]
    tm = _tile(s, ROW_TILE)
    n_b = s // tm

    def body(f_ref, b_ref, dka_ref, dqa_ref, dz_ref, db_ref, carry):
        @pl.when(pl.program_id(0) == 0)
        def _():
            carry[...] = jnp.zeros_like(carry)
            db_ref[...] = jnp.zeros_like(db_ref)

        lane = lax.broadcasted_iota(jnp.int32, (tm, LANES), 1)
        dcum = jnp.zeros((tm, LANES), F32)
        for h in range(HEADS):
            dcum = dcum + jnp.where(lane == h, dqa_ref[h][:, 0:1] - dka_ref[h][:, Q_ONE:Q_ONE + 1], 0.0)
        dlf = _tri_dot(_tri(tm, upper=True), dcum) + carry[...]
        carry[...] = dlf[0:1]
        dz = dlf * _sig(-(f_ref[...] + b_ref[...]))
        dz_ref[0] = dz.astype(BF16)
        db_ref[...] += _colsum8(dz)

    return pl.pallas_call(
        body, name=name, grid=(n_b,),
        in_specs=[pl.BlockSpec((tm, LANES), lambda i: (n_b - 1 - i, 0)), _full((1, LANES)),
                  pl.BlockSpec((HEADS, tm, LANES), lambda i: (0, n_b - 1 - i, 0)),
                  pl.BlockSpec((HEADS, tm, LANES), lambda i: (0, n_b - 1 - i, 0))],
        out_specs=[pl.BlockSpec((1, tm, LANES), lambda i: (0, n_b - 1 - i, 0)), _full((SUBLANES, LANES))],
        out_shape=[jax.ShapeDtypeStruct((1, s, LANES), BF16), jax.ShapeDtypeStruct((SUBLANES, LANES), F32)],
        scratch_shapes=[pltpu.VMEM((1, LANES), F32)],
    )(f, bias, dka, dqa)


def _causal_pairs(n_t, key_major):
    if key_major:
        pairs = [(qi, ki) for ki in range(n_t) for qi in range(ki, n_t)]
    else:
        pairs = [(qi, ki) for qi in range(n_t) for ki in range(qi + 1)]
    return (jnp.array([p[0] for p in pairs], jnp.int32), jnp.array([p[1] for p in pairs], jnp.int32))


def _with_side(main_ref, side_ref):
    return jnp.concatenate([main_ref[...], side_ref[...]], axis=1)


def _lane_const(t, lo, hi, value):
    lane = lax.broadcasted_iota(jnp.int32, (t, LANES), 1)
    return jnp.where((lane >= lo) & (lane < hi), value, 0.0).astype(BF16)


def _att_specs(t):
    qmain = pl.BlockSpec((t, HEAD_DIM), lambda h, p, qt, kt: (qt[p], h))
    kmain = pl.BlockSpec((t, HEAD_DIM), lambda h, p, qt, kt: (kt[p], h))
    qside = pl.BlockSpec((None, t, LANES), lambda h, p, qt, kt: (h, qt[p], 0))
    kside = pl.BlockSpec((None, t, LANES), lambda h, p, qt, kt: (h, kt[p], 0))
    return qmain, kmain, qside, kside


def _fox_fwd(q, qa, k, ka, v, qo, name):
    s = q.shape[0]
    t = _tile(s, ATT_TILE)
    sub = t // ATT_SPLIT
    qt, kt = _causal_pairs(s // t, key_major=False)

    def body(qt_ref, kt_ref, q_ref, qa_ref, k_ref, ka_ref, v_ref, og_ref, o_ref, y_ref, qab_ref, m_s, l_s, acc_s):
        pid = pl.program_id(1)
        qi, ki = qt_ref[pid], kt_ref[pid]

        @pl.when(ki == 0)
        def _():
            m_s[...] = jnp.full_like(m_s, NEG_INF)
            l_s[...] = jnp.zeros_like(l_s)
            acc_s[...] = jnp.zeros_like(acc_s)

        def step(diagonal):
            kc = _with_side(k_ref, ka_ref)
            vc = jnp.concatenate([v_ref[...], _lane_const(t, 0, 1, 1.0)], axis=1)
            for r in range(ATT_SPLIT):
                rows = slice(r * sub, (r + 1) * sub)
                n_k = (r + 1) * sub if diagonal else t
                sc = _dot_nt(jnp.concatenate([q_ref[rows], qa_ref[rows]], axis=1), kc[:n_k])
                if diagonal:
                    sc = jnp.where(lax.broadcasted_iota(jnp.int32, (sub, n_k), 1)
                                   <= lax.broadcasted_iota(jnp.int32, (sub, n_k), 0) + r * sub, sc, NEG_INF)
                m_old = m_s[rows]
                m_new = jnp.maximum(m_old, jnp.max(sc, axis=-1, keepdims=True))
                alpha = jnp.exp2(m_old - m_new)
                pv = _dot(jnp.exp2(sc - m_new[:, 0:1]).astype(BF16), vc[:n_k])
                acc_s[rows] = alpha * acc_s[rows] + pv[:, :HEAD_DIM]
                l_s[rows] = alpha * l_s[rows] + pv[:, HEAD_DIM:]
                m_s[rows] = m_new

        @pl.when(ki < qi)
        def _():
            step(False)

        @pl.when(ki == qi)
        def _():
            step(True)
            l = l_s[:, 0:1]
            o = acc_s[...] / l
            o_ref[...] = o
            y_ref[...] = (o * _sig(og_ref[...])).astype(BF16)
            lane = lax.broadcasted_iota(jnp.int32, (t, LANES), 1)
            qab_ref[...] = qa_ref[...] + _side(lane, Q_LSE, -(m_s[:, 0:1] + jnp.log2(l))).astype(BF16)

    qmain, kmain, qside, kside = _att_specs(t)
    return pl.pallas_call(
        body, name=name,
        grid_spec=pltpu.PrefetchScalarGridSpec(
            num_scalar_prefetch=2, grid=(HEADS, qt.shape[0]),
            in_specs=[qmain, qside, kmain, kside, kmain,
                      pl.BlockSpec((t, HEAD_DIM), lambda h, p, qt, kt: (qt[p], HEADS + h))],
            out_specs=[qmain, qmain, qside],
            scratch_shapes=[pltpu.VMEM((t, LANES), F32), pltpu.VMEM((t, LANES), F32), pltpu.VMEM((t, HEAD_DIM), F32)]),
        out_shape=[jax.ShapeDtypeStruct((s, D_MODEL), F32), jax.ShapeDtypeStruct((s, D_MODEL), BF16),
                   jax.ShapeDtypeStruct((HEADS, s, LANES), BF16)],
    )(qt, kt, q, qa, k, ka, v, qo)


def _fox_gate_bwd(o, qo, dy, name):
    s = o.shape[0]
    tm = _tile(s, ROW_TILE)

    def body(o_ref, og_ref, dy_ref, do_ref, dg_ref, dl_ref):
        ov, dyv = o_ref[...], dy_ref[...]
        sg = _sig(og_ref[...])
        do = (dyv * sg).astype(BF16)
        do_ref[...] = do
        dg_ref[...] = (dyv * ov * sg * (1.0 - sg)).astype(BF16)
        prod = do.astype(F32) * ov
        lane = lax.broadcasted_iota(jnp.int32, (tm, LANES), 1)
        for h in range(HEADS):
            delta = jnp.sum(prod[:, h * HEAD_DIM:(h + 1) * HEAD_DIM], axis=-1, keepdims=True)
            dl_ref[h] = _side(lane, 0, delta).astype(BF16)

    row = pl.BlockSpec((tm, D_MODEL), lambda i: (i, 0))
    return pl.pallas_call(
        body, name=name, grid=(s // tm,),
        in_specs=[row, pl.BlockSpec((tm, D_MODEL), lambda i: (i, 1)), row],
        out_specs=[row, row, pl.BlockSpec((HEADS, tm, LANES), lambda i: (0, i, 0))],
        out_shape=[jax.ShapeDtypeStruct((s, D_MODEL), BF16), jax.ShapeDtypeStruct((s, D_MODEL), BF16),
                   jax.ShapeDtypeStruct((HEADS, s, LANES), BF16)],
    )(o, qo, dy)


def _fox_bwd_kv(q, qab, k, ka, v, do, doa, name):
    s = q.shape[0]
    t = _tile(s, ATT_TILE)
    n_t = s // t
    sub = t // ATT_SPLIT
    qt, kt = _causal_pairs(n_t, key_major=True)

    def body(qt_ref, kt_ref, q_ref, qab_ref, k_ref, ka_ref, v_ref, do_ref, doa_ref, dk_ref, dv_ref, dka_ref, dk_s, dv_s):
        pid = pl.program_id(1)
        qi, ki = qt_ref[pid], kt_ref[pid]

        @pl.when(qi == ki)
        def _():
            dk_s[...] = jnp.zeros_like(dk_s)
            dv_s[...] = jnp.zeros_like(dv_s)

        def step(diagonal):
            kc = _with_side(k_ref, ka_ref)
            vc = jnp.concatenate([v_ref[...], _lane_const(t, 0, 3, -1.0)], axis=1)
            for r in range(ATT_SPLIT):
                cols = slice(r * sub, (r + 1) * sub)
                n_k = (r + 1) * sub if diagonal else t
                qc = jnp.concatenate([q_ref[cols], qab_ref[cols]], axis=1)
                sc = _dot_nt(kc[:n_k], qc)
                if diagonal:
                    sc = jnp.where(lax.broadcasted_iota(jnp.int32, (n_k, sub), 0)
                                   <= lax.broadcasted_iota(jnp.int32, (n_k, sub), 1) + r * sub, sc, NEG_INF)
                p = jnp.exp2(sc)
                dp = _dot_nt(vc[:n_k], jnp.concatenate([do_ref[cols], doa_ref[cols]], axis=1))
                dv_s[0:n_k] += _dot(p.astype(BF16), do_ref[cols])
                dk_s[0:n_k] += _dot((p * dp).astype(BF16), qc)

        @pl.when(qi > ki)
        def _():
            step(False)

        @pl.when(qi == ki)
        def _():
            step(True)

        @pl.when(qi == n_t - 1)
        def _():
            dk_ref[...] = dk_s[:, :HEAD_DIM] * (1.0 / LOG2E)
            dka_ref[...] = dk_s[:, HEAD_DIM:]
            dv_ref[...] = dv_s[...].astype(BF16)

    qmain, kmain, qside, kside = _att_specs(t)
    return pl.pallas_call(
        body, name=name,
        grid_spec=pltpu.PrefetchScalarGridSpec(
            num_scalar_prefetch=2, grid=(HEADS, qt.shape[0]),
            in_specs=[qmain, qside, kmain, kside, kmain, qmain, qside],
            out_specs=[kmain, pl.BlockSpec((None, t, HEAD_DIM), lambda h, p, qt, kt: (0, kt[p], h)), kside],
            scratch_shapes=[pltpu.VMEM((t, 2 * HEAD_DIM), F32), pltpu.VMEM((t, HEAD_DIM), F32)]),
        out_shape=[jax.ShapeDtypeStruct((s, D_MODEL), F32), jax.ShapeDtypeStruct((1, s, D_MODEL), BF16),
                   jax.ShapeDtypeStruct((HEADS, s, LANES), F32)],
    )(qt, kt, q, qab, k, ka, v, do, doa)


def _fox_bwd_q(q, qab, k, ka, v, do, doa, name):
    s = q.shape[0]
    t = _tile(s, ATT_TILE)
    sub = t // ATT_SPLIT
    qt, kt = _causal_pairs(s // t, key_major=False)

    def body(qt_ref, kt_ref, q_ref, qab_ref, k_ref, ka_ref, v_ref, do_ref, doa_ref, dq_ref, dqa_ref, dq_s):
        pid = pl.program_id(1)
        qi, ki = qt_ref[pid], kt_ref[pid]

        @pl.when(ki == 0)
        def _():
            dq_s[...] = jnp.zeros_like(dq_s)

        def step(diagonal):
            kc = _with_side(k_ref, ka_ref)
            vc = jnp.concatenate([v_ref[...], _lane_const(t, 0, 3, -1.0)], axis=1)
            for r in range(ATT_SPLIT):
                rows = slice(r * sub, (r + 1) * sub)
                n_k = (r + 1) * sub if diagonal else t
                sc = _dot_nt(jnp.concatenate([q_ref[rows], qab_ref[rows]], axis=1), kc[:n_k])
                if diagonal:
                    sc = jnp.where(lax.broadcasted_iota(jnp.int32, (sub, n_k), 1)
                                   <= lax.broadcasted_iota(jnp.int32, (sub, n_k), 0) + r * sub, sc, NEG_INF)
                dp = _dot_nt(jnp.concatenate([do_ref[rows], doa_ref[rows]], axis=1), vc[:n_k])
                dq_s[rows] += _dot((jnp.exp2(sc) * dp).astype(BF16), kc[:n_k])

        @pl.when(ki < qi)
        def _():
            step(False)

        @pl.when(ki == qi)
        def _():
            step(True)
            dq_ref[...] = dq_s[:, :HEAD_DIM]
            dqa_ref[...] = dq_s[:, HEAD_DIM:]

    qmain, kmain, qside, kside = _att_specs(t)
    return pl.pallas_call(
        body, name=name,
        grid_spec=pltpu.PrefetchScalarGridSpec(
            num_scalar_prefetch=2, grid=(HEADS, qt.shape[0]),
            in_specs=[qmain, qside, kmain, kside, kmain, qmain, qside],
            out_specs=[qmain, qside],
            scratch_shapes=[pltpu.VMEM((t, 2 * HEAD_DIM), F32)]),
        out_shape=[jax.ShapeDtypeStruct((s, D_MODEL), F32), jax.ShapeDtypeStruct((HEADS, s, LANES), F32)],
    )(qt, kt, q, qab, k, ka, v, do, doa)


def _ffn_forward(x_in, branch, gate, shift, scale, w_up, conv_w, conv_b, w_down, tag):
    x_mid, h = _premix(x_in, shift, scale, tag + "_premix", branch=branch, gate=gate)
    u = _mm_nn(h, w_up, 2, BF16, tag + "_up")
    a = _convglu_fwd(u, conv_w, conv_b, tag + "_convglu")
    ffn = _mm_nn(a, w_down, 1, F32, tag + "_down")[0]
    return x_mid, ffn, (h, u, a)


def _ffn_backward(dx_out, x_mid, ffn, gate, scale, saved, w_up, conv_w, conv_b, w_down, tag):
    h, u, a = saved
    dffn, dgate = _branch_bwd(dx_out, ffn, gate, tag + "_gate_bwd")
    da = _mm_nt(dffn, w_down, BF16, tag + "_down_dx")
    dw_down = _mm_tn(a, dffn, 1, tag + "_down_dw")
    du, dconv = _convglu_bwd(u, da, conv_w, conv_b, tag + "_convglu_bwd")
    dh = _mm_nt(du, w_up, F32, tag + "_up_dx")
    dw_up = _mm_tn(h, du, N_CHIPS, tag + "_up_dw")
    dx_mid, dshift, dscale = _premix_bwd(x_mid, dh, scale, dx_out, tag + "_premix_bwd")
    return dx_mid, dw_up, dw_down, dict(gate=dgate, shift=dshift, scale=dscale, conv=dconv)


def _local_step(x, target, mods, lb, vecs, weights_at):
    m0, m1, mk = mods["l0"], mods["l1"], mods["kv"]
    h0 = _premix(x, m0[0], m0[1], "l0_premix")
    wts, h0 = weights_at("mixer0", h0)
    proj = _mm_nn(h0, wts["a_w_in"], 1, F32, "l0_in")[0]
    o_a, yp, states = _hgrn_fwd(proj, lb, vecs["a_norm_g"], "l0_hgrn")
    more, yp = weights_at("ffn0", yp)
    wts.update(more)
    y0 = _mm_nn(yp, wts["a_w_out"], 1, F32, "l0_out")[0]
    x1, ffn0, saved0 = _ffn_forward(x, y0, m0[2], m0[3], m0[4], wts["up0"], vecs["conv_w0"], vecs["conv_b0"],
                                    wts["down0"], "l0_ffn")
    more, ffn0 = weights_at("layer1", ffn0)
    wts.update(more)
    x2, hk = _premix(x1, mk[0], mk[1], "kv_premix", branch=ffn0, gate=m0[5])
    k_raw = _mm_nn(hk, wts["kv_k"], 1, F32, "kv_k")[0]
    v_sh = _mm_nn(hk, wts["kv_v"], 1, BF16, "kv_v")[0]
    f_raw = _mm_nn(hk, wts["kv_f"], 1, F32, "kv_f")[0]
    k_sh = _headnorm(k_raw, vecs["k_norm_g"], 1.0, "kv_knorm")
    qa, ka = _fcum_fwd(f_raw, vecs["kv_b_f"], "kv_fcum")
    h1 = _premix(x2, m1[0], m1[1], "l1_premix")
    qo = _mm_nn(h1, wts["b_w_q"], 1, F32, "l1_q")[0]
    q_scale = HEAD_DIM ** -0.5
    q = _headnorm(qo, vecs["q_norm_g"], q_scale * LOG2E, "l1_qnorm")
    o_b, og, qab = _fox_fwd(q, qa, k_sh, ka, v_sh, qo, "l1_fox")
    y1 = _mm_nn(og, wts["b_w_out"], 1, F32, "l1_out")[0]
    x3, ffn1, saved1 = _ffn_forward(x2, y1, m1[2], m1[3], m1[4], wts["up1"], vecs["conv_w1"], vecs["conv_b1"],
                                    wts["down1"], "l1_ffn")
    sq, dx4 = _loss_head(x3, ffn1, m1[5], target, "loss_head")

    big, small = {}, {}
    dx3, big["up1"], big["down1"], s_ffn1 = _ffn_backward(dx4, x3, ffn1, m1[5], m1[4], saved1, wts["up1"],
                                                          vecs["conv_w1"], vecs["conv_b1"], wts["down1"], "l1_ffn")
    dy1, dg1_1 = _branch_bwd(dx3, y1, m1[2], "l1_mix_gate_bwd")
    d_og = _mm_nt(dy1, wts["b_w_out"], F32, "l1_out_dx")
    big["b_w_out"] = _mm_tn(og, dy1, 1, "l1_out_dw")
    do_b, dgate_b, doa = _fox_gate_bwd(o_b, qo, d_og, "l1_fox_gate_bwd")
    dk, dv, dka = _fox_bwd_kv(q, qab, k_sh, ka, v_sh, do_b, doa, "l1_fox_bwd_kv")
    dq, dqa = _fox_bwd_q(q, qab, k_sh, ka, v_sh, do_b, doa, "l1_fox_bwd_q")
    dqo, dqg = _headnorm_bwd(qo, vecs["q_norm_g"], q_scale, dq, "l1_qnorm_bwd", extra=dgate_b)
    dh1 = _mm_nt(dqo, wts["b_w_q"], F32, "l1_q_dx")
    big["b_w_q"] = _mm_tn(h1, dqo, N_CHIPS, "l1_q_dw")
    dx2, dsh1_1, dsc1_1 = _premix_bwd(x2, dh1, m1[1], dx3, "l1_premix_bwd")
    dk_raw, dkg = _headnorm_bwd(k_raw, vecs["k_norm_g"], 1.0, dk, "kv_knorm_bwd")
    dz, dbf = _fcum_bwd(f_raw, vecs["kv_b_f"], dka, dqa, "kv_fcum_bwd")
    dhk = _mm_nt(dk_raw, wts["kv_k"], F32, "kv_k_dx")
    dhk = _mm_nt(dv, wts["kv_v"], F32, "kv_v_dx", add=dhk)
    dhk = _mm_nt(dz, wts["kv_f"], F32, "kv_f_dx", add=dhk)
    big["kv_k"] = _mm_tn(hk, dk_raw, 1, "kv_k_dw")
    big["kv_v"] = _mm_tn(hk, dv, 1, "kv_v_dw")
    big["kv_f"] = _mm_tn(hk, dz, 1, "kv_f_dw")
    dx2, dshk, dsck = _premix_bwd(x2, dhk, mk[1], dx2, "kv_premix_bwd")
    dx1, big["up0"], big["down0"], s_ffn0 = _ffn_backward(dx2, x1, ffn0, m0[5], m0[4], saved0, wts["up0"],
                                                          vecs["conv_w0"], vecs["conv_b0"], wts["down0"], "l0_ffn")
    dy0, dg1_0 = _branch_bwd(dx1, y0, m0[2], "l0_mix_gate_bwd")
    dyp = _mm_nt(dy0, wts["a_w_out"], F32, "l0_out_dx")
    big["a_w_out"] = _mm_tn(yp, dy0, 1, "l0_out_dw")
    dproj, dlb, dng = _hgrn_bwd(proj, lb, vecs["a_norm_g"], o_a, states, dyp, "l0_hgrn_bwd")
    dh0 = _mm_nt(dproj, wts["a_w_in"], F32, "l0_in_dx")
    big["a_w_in"] = _mm_tn(h0, dproj, N_CHIPS, "l0_in_dw")
    grad_x, dsh1_0, dsc1_0 = _premix_bwd(x, dh0, m0[1], dx1, "l0_premix_bwd")

    small["mod_l0"] = [dsh1_0, dsc1_0, dg1_0, s_ffn0["shift"], s_ffn0["scale"], s_ffn0["gate"]]
    small["mod_l1"] = [dsh1_1, dsc1_1, dg1_1, s_ffn1["shift"], s_ffn1["scale"], s_ffn1["gate"]]
    small["mod_kv"] = [dshk, dsck]
    small["conv0"], small["conv1"] = s_ffn0["conv"], s_ffn1["conv"]
    small["a_norm_g"], small["k_norm_g"], small["q_norm_g"] = dng, dkg, dqg
    small["kv_b_f"], small["lb"] = dbf, dlb
    marks = {"attention_bwd": dk, "ffn0_bwd": dx1, "mixer0_bwd": grad_x}
    return sq, grad_x, big, small, marks


HBM = pl.BlockSpec(memory_space=pltpu.HBM)
COMM_CHUNK_ELEMS = 256 * 1024


def _place():
    x, y, c = lax.axis_index("x"), lax.axis_index("y"), lax.axis_index("c")
    chips = [(1 - x, y), (x, 1 - y), (1 - x, 1 - y)]
    return x, y, c, (x, y, 1 - c), chips


def _chunk_rows(rows, cols):
    best = BF16_ROWS
    for r in range(BF16_ROWS, rows + 1, BF16_ROWS):
        if rows % r == 0 and r * cols <= COMM_CHUNK_ELEMS:
            best = r
    assert rows % best == 0, (rows, cols)
    return best


def _allgather8(block, name):
    m_per, n = block.shape

    def body(x_ref, out_ref, send_sems, recv_sems, local_sem):
        x, y, c, sibling, chips = _place()
        me = (x, y, c)

        def rows(px, py, pc):
            return out_ref.at[pl.ds((4 * px + 2 * py + pc) * m_per, m_per), :]

        def copy(k, blk, to, src=None):
            return pltpu.make_async_remote_copy(
                src_ref=rows(*blk) if src is None else src, dst_ref=rows(*blk),
                send_sem=send_sems.at[k], recv_sem=recv_sems.at[k], device_id=to, device_id_type=MESH)

        mine = pltpu.make_async_copy(x_ref, rows(*me), local_sem)
        mine.start()
        first = [copy(0, me, sibling, src=x_ref)]
        first += [copy(1 + j, me, (*chip, c), src=x_ref) for j, chip in enumerate(chips)]
        for cp in first:
            cp.start()
        passed = [copy(4 + j, (*chip, c), sibling) for j, chip in enumerate(chips)]
        for j, chip in enumerate(chips):
            copy(1 + j, (*chip, c), me).wait_recv()
            passed[j].start()
        copy(0, sibling, me).wait_recv()
        for j, chip in enumerate(chips):
            copy(4 + j, (*chip, 1 - c), me).wait_recv()
        for cp in first + passed:
            cp.wait_send()
        mine.wait()

    return pl.pallas_call(
        body, name=name, out_shape=jax.ShapeDtypeStruct((N_DEV * m_per, n), block.dtype),
        in_specs=[pl.BlockSpec(memory_space=pltpu.VMEM)], out_specs=pl.BlockSpec(memory_space=pltpu.VMEM),
        scratch_shapes=[pltpu.SemaphoreType.DMA((7,)), pltpu.SemaphoreType.DMA((7,)), pltpu.SemaphoreType.DMA],
    )(block)


def _gather_weights(shards, name):
    n_t = len(shards)
    dims = [s.shape for s in shards]

    def body(*refs):
        ins, outs = refs[:n_t], refs[n_t:2 * n_t]
        send_ici, recv_ici, send_d2d, recv_d2d = refs[2 * n_t:]
        x, y, c, sibling, chips = _place()
        p_me = 2 * x + y

        def halves(t, count):
            return outs[t].at[pl.ds(0, count), pl.ds(0, dims[t][0] // 2), :]

        def waiter(t, sem_s, sem_r):
            win = halves(t, 3)
            return pltpu.make_async_remote_copy(src_ref=win, dst_ref=win, send_sem=sem_s.at[t], recv_sem=sem_r.at[t],
                                                device_id=sibling, device_id_type=MESH)

        def half_copy(t, chip_idx, to, sem_s, sem_r):
            r2 = dims[t][0] // 2
            win = outs[t].at[chip_idx, pl.ds(c * r2, r2), :]
            return pltpu.make_async_remote_copy(src_ref=win, dst_ref=win, send_sem=sem_s.at[t], recv_sem=sem_r.at[t],
                                                device_id=to, device_id_type=MESH)

        for t in range(n_t):
            r, cols = dims[t]
            rows = _chunk_rows(r, cols)

            def cast(fbuf, bbuf, t=t, r=r, rows=rows):
                for k in range(r // rows):
                    pltpu.sync_copy(ins[t].at[pl.ds(k * rows, rows), :], fbuf)
                    bbuf[...] = fbuf[...].astype(BF16)
                    pltpu.sync_copy(bbuf, outs[t].at[p_me, pl.ds(k * rows, rows), :])

            pl.run_scoped(cast, pltpu.VMEM((rows, cols), F32), pltpu.VMEM((rows, cols), BF16))
            for chip in chips:
                half_copy(t, p_me, (*chip, c), send_ici, recv_ici).start()
        for t in range(n_t):
            waiter(t, send_ici, recv_ici).wait_recv()
            for cx, cy in chips:
                half_copy(t, 2 * cx + cy, sibling, send_d2d, recv_d2d).start()
        for t in range(n_t):
            waiter(t, send_d2d, recv_d2d).wait_recv()
            waiter(t, send_ici, recv_ici).wait_send()
            waiter(t, send_d2d, recv_d2d).wait_send()

    return pl.pallas_call(
        body, name=name, in_specs=[HBM] * n_t, out_specs=[HBM] * n_t,
        out_shape=[jax.ShapeDtypeStruct((N_CHIPS, r, cols), BF16) for r, cols in dims],
        scratch_shapes=[pltpu.SemaphoreType.DMA((n_t,))] * 4,
    )(*shards)


def _cast_own_block(shard, chip, name):
    r, cols = shard.shape
    rows = _chunk_rows(r, cols)

    def body(chip_ref, w_ref, o_ref):
        o_ref[...] = w_ref[...].astype(BF16)

    return pl.pallas_call(
        body, name=name,
        grid_spec=pltpu.PrefetchScalarGridSpec(
            num_scalar_prefetch=1, grid=(r // rows,),
            in_specs=[pl.BlockSpec((rows, cols), lambda i, chip_ref: (i, 0))],
            out_specs=pl.BlockSpec((None, rows, cols), lambda i, chip_ref: (chip_ref[0], i, 0))),
        out_shape=jax.ShapeDtypeStruct((N_CHIPS, r, cols), BF16),
    )(chip, shard)


def _sequencer_gather(bufs, name, collective_id):
    n_t = len(bufs)
    dims = [b.shape[1:] for b in bufs]
    refs = [jax.new_ref(b, memory_space=pltpu.MemorySpace.HBM) for b in bufs]

    @pl.kernel(mesh=plsc.ScalarSubcoreMesh(axis_name="sequencer", num_cores=1), name=name,
               scratch_types=[pltpu.SemaphoreType.DMA((n_t,))] * 4,
               compiler_params=pltpu.CompilerParams(collective_id=collective_id))
    def launch(send_ici, recv_ici, send_d2d, recv_d2d):
        x, y, c, sibling, chips = _place()
        p_me = 2 * x + y
        peers = [sibling] + [(cx, cy, c) for cx, cy in chips]
        barrier = pltpu.get_barrier_semaphore()
        for peer in peers:
            pl.semaphore_signal(barrier, inc=1, device_id=peer, device_id_type=MESH)
        pl.semaphore_wait(barrier, len(peers))

        def waiter(t, sem_s, sem_r):
            win = refs[t].at[pl.ds(0, 3), pl.ds(0, dims[t][0] // 2), :]
            return pltpu.make_async_remote_copy(src_ref=win, dst_ref=win, send_sem=sem_s.at[t], recv_sem=sem_r.at[t],
                                                device_id=sibling, device_id_type=MESH)

        def half_copy(t, chip_idx, to, sem_s, sem_r):
            r2 = dims[t][0] // 2
            win = refs[t].at[chip_idx, pl.ds(c * r2, r2), :]
            return pltpu.make_async_remote_copy(src_ref=win, dst_ref=win, send_sem=sem_s.at[t], recv_sem=sem_r.at[t],
                                                device_id=to, device_id_type=MESH)

        for t in range(n_t):
            for cx, cy in chips:
                half_copy(t, p_me, (cx, cy, c), send_ici, recv_ici).start()
        for t in range(n_t):
            waiter(t, send_ici, recv_ici).wait_recv()
            for cx, cy in chips:
                half_copy(t, 2 * cx + cy, sibling, send_d2d, recv_d2d).start()
        for t in range(n_t):
            waiter(t, send_d2d, recv_d2d).wait_recv()
            waiter(t, send_ici, recv_ici).wait_send()
            waiter(t, send_d2d, recv_d2d).wait_send()

    launch()
    return [r[...] for r in refs]


def _others():
    x, y, c = lax.axis_index("x"), lax.axis_index("y"), lax.axis_index("c")
    flip = lambda v, f: 1 - v if f else v
    return [(flip(x, fx), flip(y, fy), flip(c, fc))
            for fx in (0, 1) for fy in (0, 1) for fc in (0, 1) if (fx, fy, fc) != (0, 0, 0)]


def _handshake(peers):
    barrier = pltpu.get_barrier_semaphore()
    for peer in peers:
        pl.semaphore_signal(barrier, inc=1, device_id=peer, device_id_type=MESH)
    pl.semaphore_wait(barrier, len(peers))


def _sequencer_scatter(parts, name, collective_id):
    n_t = len(parts)
    dims = [p.shape[1:] for p in parts]
    srcs = [jax.new_ref(p, memory_space=pltpu.MemorySpace.HBM) for p in parts]
    inboxes = [jax.empty_ref(jax.ShapeDtypeStruct((N_DEV, r // 2, cols), BF16), memory_space=pltpu.MemorySpace.HBM)
               for r, cols in dims]

    @pl.kernel(mesh=plsc.ScalarSubcoreMesh(axis_name="sequencer", num_cores=1), name=name,
               scratch_types=[pltpu.SemaphoreType.DMA((n_t,))] * 2,
               compiler_params=pltpu.CompilerParams(collective_id=collective_id))
    def launch(send_sem, recv_sem):
        x, y, c = lax.axis_index("x"), lax.axis_index("y"), lax.axis_index("c")
        me = 4 * x + 2 * y + c
        peers = _others()
        _handshake(peers)
        for t in range(n_t):
            h = dims[t][0] // 2
            for qx, qy, qc in peers:
                pltpu.make_async_remote_copy(
                    src_ref=srcs[t].at[2 * qx + qy, pl.ds(qc * h, h), :], dst_ref=inboxes[t].at[me],
                    send_sem=send_sem.at[t], recv_sem=recv_sem.at[t], device_id=(qx, qy, qc), device_id_type=MESH).start()
        for t in range(n_t):
            win = inboxes[t].at[pl.ds(0, N_DEV - 1)]
            both = pltpu.make_async_remote_copy(src_ref=win, dst_ref=win, send_sem=send_sem.at[t],
                                                recv_sem=recv_sem.at[t], device_id=peers[0], device_id_type=MESH)
            both.wait_recv()
            both.wait_send()

    launch()
    return [b[...] for b in inboxes]


def _sum_pieces(part, inbox, place, name):
    _, r, cols = part.shape
    h = r // 2
    rows = _chunk_rows(h, cols)
    steps = h // rows

    def body(place_ref, own_ref, in_ref, o_ref):
        dev = place_ref[2]
        own = own_ref[...].astype(F32)
        acc = jnp.zeros((rows, cols), F32)
        for d in range(N_DEV):
            acc = acc + jnp.where(dev == d, own, in_ref[d].astype(F32))
        o_ref[...] = acc

    return pl.pallas_call(
        body, name=name,
        grid_spec=pltpu.PrefetchScalarGridSpec(
            num_scalar_prefetch=1, grid=(steps,),
            in_specs=[pl.BlockSpec((None, rows, cols), lambda i, pr: (pr[0], pr[1] * steps + i, 0)),
                      pl.BlockSpec((N_DEV, rows, cols), lambda i, pr: (0, i, 0))],
            out_specs=pl.BlockSpec((rows, cols), lambda i, pr: (pr[1] * steps + i, 0))),
        out_shape=jax.ShapeDtypeStruct((r, cols), F32),
    )(place, part, inbox)


def _sequencer_swap_halves(halves, name, collective_id):
    n_t = len(halves)
    refs = [jax.new_ref(a, memory_space=pltpu.MemorySpace.HBM) for a in halves]

    @pl.kernel(mesh=plsc.ScalarSubcoreMesh(axis_name="sequencer", num_cores=1), name=name,
               scratch_types=[pltpu.SemaphoreType.DMA((n_t,))] * 2,
               compiler_params=pltpu.CompilerParams(collective_id=collective_id))
    def launch(send_sem, recv_sem):
        x, y, c = lax.axis_index("x"), lax.axis_index("y"), lax.axis_index("c")
        sibling = (x, y, 1 - c)
        _handshake([sibling])
        copies = []
        for t in range(n_t):
            h = halves[t].shape[0] // 2
            win = refs[t].at[pl.ds(c * h, h), :]
            copies.append(pltpu.make_async_remote_copy(src_ref=win, dst_ref=win, send_sem=send_sem.at[t],
                                                       recv_sem=recv_sem.at[t], device_id=sibling, device_id_type=MESH))
            copies[-1].start()
        for cp in copies:
            cp.wait()

    launch()
    return [r[...] for r in refs]


def _reduce_scatter(parts, name):
    n_t = len(parts)
    dims = [p.shape[1:] for p in parts]

    def body(*refs):
        ins = refs[:n_t]
        outs, from_sib, chip_sum, from_chips = (refs[(1 + k) * n_t:(2 + k) * n_t] for k in range(4))
        s1, r1, s2, r2, s3, r3 = refs[5 * n_t:]
        x, y, c, sibling, chips = _place()
        p_me = 2 * x + y

        def remote(src, dst, sem_s, sem_r, t, to):
            return pltpu.make_async_remote_copy(src_ref=src, dst_ref=dst, send_sem=sem_s.at[t], recv_sem=sem_r.at[t],
                                                device_id=to, device_id_type=MESH)

        def swap1(t):
            h = dims[t][0] // 2
            return remote(ins[t].at[:, pl.ds((1 - c) * h, h), :], from_sib[t], s1, r1, t, sibling)

        def to_chips(t):
            return remote(from_chips[t], from_chips[t], s2, r2, t, sibling)

        def swap3(t):
            h = dims[t][0] // 2
            win = outs[t].at[pl.ds(c * h, h), :]
            return remote(win, win, s3, r3, t, sibling)

        for t in range(n_t):
            swap1(t).start()
        for t in range(n_t):
            r, cols = dims[t]
            h = r // 2
            rows = _chunk_rows(h, cols)
            swap1(t).wait_recv()

            def pair_sum(a, b, o, t=t, h=h, rows=rows):
                for p in range(N_CHIPS):
                    for k in range(h // rows):
                        pltpu.sync_copy(ins[t].at[p, pl.ds(c * h + k * rows, rows), :], a)
                        pltpu.sync_copy(from_sib[t].at[p, pl.ds(k * rows, rows), :], b)
                        o[...] = (a[...].astype(F32) + b[...].astype(F32)).astype(BF16)
                        pltpu.sync_copy(o, chip_sum[t].at[p, pl.ds(k * rows, rows), :])

            pl.run_scoped(pair_sum, *[pltpu.VMEM((rows, cols), BF16)] * 3)
            for j, (cx, cy) in enumerate(chips):
                remote(chip_sum[t].at[2 * cx + cy], from_chips[t].at[j], s2, r2, t, (cx, cy, c)).start()
        for t in range(n_t):
            r, cols = dims[t]
            h = r // 2
            rows = _chunk_rows(h, cols)
            to_chips(t).wait_recv()

            def total(a, b0, b1, b2, o, t=t, h=h, rows=rows):
                for k in range(h // rows):
                    pltpu.sync_copy(chip_sum[t].at[p_me, pl.ds(k * rows, rows), :], a)
                    for j, b in enumerate((b0, b1, b2)):
                        pltpu.sync_copy(from_chips[t].at[j, pl.ds(k * rows, rows), :], b)
                    o[...] = ((a[...].astype(F32) + b0[...].astype(F32)) + b1[...].astype(F32)) + b2[...].astype(F32)
                    pltpu.sync_copy(o, outs[t].at[pl.ds(c * h + k * rows, rows), :])

            pl.run_scoped(total, *([pltpu.VMEM((rows, cols), BF16)] * 4 + [pltpu.VMEM((rows, cols), F32)]))
            swap3(t).start()
        for t in range(n_t):
            swap3(t).wait_recv()
            swap1(t).wait_send()
            to_chips(t).wait_send()
            swap3(t).wait_send()

    half = lambda n, rc: jax.ShapeDtypeStruct((n, rc[0] // 2, rc[1]), BF16)
    out_shape = ([jax.ShapeDtypeStruct(rc, F32) for rc in dims] + [half(N_CHIPS, rc) for rc in dims]
                 + [half(N_CHIPS, rc) for rc in dims] + [half(N_CHIPS - 1, rc) for rc in dims])
    outs = pl.pallas_call(
        body, name=name, in_specs=[HBM] * n_t, out_specs=[HBM] * (4 * n_t), out_shape=out_shape,
        scratch_shapes=[pltpu.SemaphoreType.DMA((n_t,))] * 6,
    )(*parts)
    return outs[:n_t]


def _cond_rows(c16, w, act, name):
    n_l, dm, wid = w.shape

    def body(c_ref, w_ref, o_ref, a_ref):
        cv = c_ref[...]
        if act:
            cv = cv * _sig(cv)
        a_ref[...] = cv
        o_ref[...] = _dot_f32(cv, w_ref[...])

    return pl.pallas_call(
        body, name=name, grid=(n_l,),
        in_specs=[_full((16, dm)), pl.BlockSpec((None, dm, wid), lambda l: (l, 0, 0))],
        out_specs=[pl.BlockSpec((None, 16, wid), lambda l: (l, 0, 0)), _full((16, dm))],
        out_shape=[jax.ShapeDtypeStruct((n_l, 16, wid), F32), jax.ShapeDtypeStruct((16, dm), F32)],
    )(c16, w)


def _outer_grad(ct, dm, name):
    n_l, kk, wid = dm.shape
    d_rows = ct.shape[0]

    def body(c_ref, d_ref, o_ref):
        o_ref[...] = _dot_f32(c_ref[...], d_ref[...])

    return pl.pallas_call(
        body, name=name, grid=(n_l,),
        in_specs=[_full((d_rows, kk)), pl.BlockSpec((None, kk, wid), lambda l: (l, 0, 0))],
        out_specs=pl.BlockSpec((None, d_rows, wid), lambda l: (l, 0, 0)),
        out_shape=jax.ShapeDtypeStruct((n_l, d_rows, wid), F32),
    )(ct, dm)


def _sum_devices(g, name):
    rows, n = g.shape

    def body(g_ref, o_ref):
        acc = g_ref[0:SUBLANES, :]
        for dev in range(1, N_DEV):
            acc = acc + g_ref[dev * SUBLANES:(dev + 1) * SUBLANES, :]
        o_ref[...] = acc

    return pl.pallas_call(body, name=name, out_shape=jax.ShapeDtypeStruct((SUBLANES, n), F32))(g)


def _adamw(w, g, m, v, name):
    shape = w.shape
    cols = shape[-1]
    rows = w.size // cols
    tr = rows
    for cand in range(SUBLANES, min(rows, 256) + 1, SUBLANES):
        if rows % cand == 0:
            tr = cand
    if rows * cols <= COMM_CHUNK_ELEMS:
        tr = rows
    c1 = 1.0 / (1.0 - ADAM_B1 ** ADAM_STEP)
    c2 = 1.0 / (1.0 - ADAM_B2 ** ADAM_STEP)

    def body(w_ref, g_ref, m_ref, v_ref, d_ref, mo_ref, vo_ref):
        gv = g_ref[...]
        m_new = ADAM_B1 * m_ref[...] + (1.0 - ADAM_B1) * gv
        v_new = ADAM_B2 * v_ref[...] + (1.0 - ADAM_B2) * (gv * gv)
        mo_ref[...] = m_new
        vo_ref[...] = v_new
        d_ref[...] = -ADAM_LR * ((m_new * c1) / (jnp.sqrt(v_new * c2) + ADAM_EPS) + ADAM_WD * w_ref[...])

    spec = pl.BlockSpec((tr, cols), lambda i: (i, 0))
    outs = pl.pallas_call(
        body, name=name, grid=(rows // tr,), in_specs=[spec] * 4, out_specs=[spec] * 3,
        out_shape=[jax.ShapeDtypeStruct((rows, cols), F32)] * 3,
    )(*[a.reshape(rows, cols) for a in (w, g, m, v)])
    return tuple(o.reshape(shape) for o in outs)


def _pad_cols(a, cols):
    return jnp.pad(a, [(0, 0)] * (a.ndim - 1) + [(0, cols - a.shape[-1])])


def _flat8(parts, width):
    v = jnp.concatenate([p.reshape(-1) for p in parts])
    return jnp.pad(v, (0, width - v.shape[0])).reshape(SUBLANES, width // SUBLANES)


KV_SHARD = 514
KV_SHARD_PAD = 640
BIG = ("a_w_in", "a_w_out", "kv_w", "b_w_q", "b_w_out", "up0", "up1", "down0", "down1")


def kernel(x, c, ada_w, ada_b, a_w_in, a_lb_logits, a_norm_g, a_w_out, kv_ada_w, kv_ada_b, kv_w, kv_b_f, k_norm_g, b_w_q, q_norm_g, b_w_out, ffn_w_up, ffn_conv_w, ffn_conv_b, ffn_w_down, loss_target, m_ada_w, m_ada_b, m_a_w_in, m_a_lb_logits, m_a_norm_g, m_a_w_out, m_kv_ada_w, m_kv_ada_b, m_kv_w, m_kv_b_f, m_k_norm_g, m_b_w_q, m_q_norm_g, m_b_w_out, m_ffn_w_up, m_ffn_conv_w, m_ffn_conv_b, m_ffn_w_down, v_ada_w, v_ada_b, v_a_w_in, v_a_lb_logits, v_a_norm_g, v_a_w_out, v_kv_ada_w, v_kv_ada_b, v_kv_w, v_kv_b_f, v_k_norm_g, v_b_w_q, v_q_norm_g, v_b_w_out, v_ffn_w_up, v_ffn_conv_w, v_ffn_conv_b, v_ffn_w_down):
    dm, ff = D_MODEL, D_FF
    ix, iy, ic = lax.axis_index("x"), lax.axis_index("y"), lax.axis_index("c")
    chip = 2 * ix + iy
    dev = 2 * chip + ic

    w1 = 10240
    g1 = _allgather8(_flat8([c, a_lb_logits, ffn_conv_w], w1), "gather_cond").reshape(N_DEV, w1)
    c_all = g1[:, :dm]
    per_chip = g1[0::2]
    lb_logits = per_chip[:, dm:dm + 512].reshape(N_CHIPS, 2, 256).transpose(1, 0, 2).reshape(2, dm)
    conv_w = per_chip[:, dm + 512:dm + 512 + 2 * CONV_W * FFN_COLS].reshape(N_CHIPS, 2, CONV_W, FFN_COLS)
    conv_w = conv_w.transpose(1, 2, 0, 3).reshape(2, CONV_W, 2, ff).transpose(0, 2, 1, 3)
    conv_b = ffn_conv_b.reshape(2, 2, 1, ff)
    lb = jax.nn.softmax(lb_logits, axis=0)[0:1]

    c16 = jnp.pad(c_all, ((0, 8), (0, 0)))
    mod_ada, c_act16 = _cond_rows(c16, ada_w, True, "mod_ada")
    mod_kv, _ = _cond_rows(c16, kv_ada_w[None], True, "mod_kv")
    mine = jnp.concatenate([mod_ada[0, :8], mod_ada[1, :8], mod_kv[0, :8]], axis=1)
    w2 = mine.shape[1]
    g2 = _allgather8(mine, "gather_mod").reshape(N_DEV, 8, w2)[0::2]
    my_rows = lax.dynamic_index_in_dim(g2, dev, axis=1, keepdims=False)
    mod0 = my_rows[:, 0:1536].reshape(6 * dm) + ada_b[0]
    mod1 = my_rows[:, 1536:3072].reshape(6 * dm) + ada_b[1]
    modk = my_rows[:, 3072:3584].reshape(2 * dm) + kv_ada_b
    mods = {"l0": [v.reshape(1, dm) for v in jnp.split(mod0, 6)],
            "l1": [v.reshape(1, dm) for v in jnp.split(mod1, 6)],
            "kv": [v.reshape(1, dm) for v in jnp.split(modk, 2)]}

    local = [a_w_in[0], a_w_out[0], _pad_cols(kv_w, KV_SHARD_PAD), b_w_q[0], b_w_out[0], ffn_w_up[0], ffn_w_up[1],
             ffn_w_down[0], ffn_w_down[1]]
    chip_arr = chip.reshape(1).astype(jnp.int32)
    own = {n: _cast_own_block(w, chip_arr, "cast_" + n) for n, w in zip(BIG, local)}
    stages = {"mixer0": ("a_w_in",), "ffn0": ("a_w_out", "up0", "down0"),
              "layer1": ("kv_w", "b_w_q", "b_w_out", "up1", "down1")}
    arriving = {st: _sequencer_gather([own[n] for n in names], "gather_" + st, cid)
                for cid, (st, names) in enumerate(stages.items(), start=1)}
    rowwise = lambda g: g.reshape(1, -1, dm)

    def weights_at(stage, token):
        got, token = lax.optimization_barrier((arriving[stage], token))
        g = dict(zip(stages[stage], got))
        if stage == "mixer0":
            return {"a_w_in": g["a_w_in"]}, token
        if stage == "ffn0":
            return {"a_w_out": rowwise(g["a_w_out"]), "up0": g["up0"], "down0": rowwise(g["down0"])}, token
        kv_full = g["kv_w"][:, :, :KV_SHARD].transpose(1, 0, 2).reshape(dm, N_CHIPS * KV_SHARD)
        return {"kv_k": kv_full[None, :, :dm], "kv_v": kv_full[None, :, dm:2 * dm],
                "kv_f": _pad_cols(kv_full[None, :, 2 * dm:], LANES), "b_w_q": g["b_w_q"],
                "b_w_out": rowwise(g["b_w_out"]), "up1": g["up1"], "down1": rowwise(g["down1"])}, token

    vecs = {"a_norm_g": jnp.tile(a_norm_g, (1, HEADS)), "k_norm_g": jnp.tile(k_norm_g[None], (1, HEADS)),
            "q_norm_g": jnp.tile(q_norm_g, (1, HEADS)), "kv_b_f": _pad_cols(kv_b_f[None], LANES),
            "conv_w0": conv_w[0], "conv_b0": conv_b[0], "conv_w1": conv_w[1], "conv_b1": conv_b[1]}

    sq, grad_x, big, small, marks = _local_step(x[0], loss_target[0], mods, lb, vecs, weights_at)
    loss = lax.psum(0.5 * jnp.sum(sq) / dm, ("x", "y", "c"))

    kv_grad = jnp.concatenate([big["kv_k"][0], big["kv_v"][0], big["kv_f"][0][:, :HEADS]], axis=1)
    kv_grad = _pad_cols(kv_grad.reshape(dm, N_CHIPS, KV_SHARD).transpose(1, 0, 2), KV_SHARD_PAD)
    chipwise = lambda g: g.reshape(N_CHIPS, -1, dm)
    parts = dict(zip(BIG, [big["a_w_in"], chipwise(big["a_w_out"]), kv_grad, big["b_w_q"], chipwise(big["b_w_out"]),
                           big["up0"], big["up1"], chipwise(big["down0"]), chipwise(big["down1"])]))
    place = jnp.stack([chip, ic, dev]).astype(jnp.int32)

    def reduce_group(k, names, token):
        inboxes = _sequencer_scatter([parts[n] for n in names], "scatter_grads_%d" % k, 4 + k)
        inboxes, _ = lax.optimization_barrier((inboxes, token))
        halves = [_sum_pieces(parts[n], box, place, "sum_" + n) for n, box in zip(names, inboxes)]
        return dict(zip(names, _sequencer_swap_halves(halves, "swap_grads_%d" % k, 8 + k)))

    rs = reduce_group(0, ("up1", "down1"), marks["attention_bwd"])
    rs.update(reduce_group(1, ("b_w_out", "b_w_q", "kv_w"), marks["ffn0_bwd"]))
    rs.update(reduce_group(2, ("up0", "down0", "a_w_out"), marks["mixer0_bwd"]))

    fold = lambda a: a.sum(axis=0)
    heads = lambda a: fold(a).reshape(HEADS, HEAD_DIM).sum(axis=0)
    conv_flat = lambda a: a.sum(axis=2).transpose(1, 0, 2)
    pieces = ([fold(a) for a in small["mod_l0"]] + [fold(a) for a in small["mod_l1"]] + [fold(a) for a in small["mod_kv"]]
              + [conv_flat(small["conv0"]), conv_flat(small["conv1"]), heads(small["a_norm_g"]), heads(small["k_norm_g"]),
                 heads(small["q_norm_g"]), fold(small["kv_b_f"]), fold(small["lb"])])
    w3 = 61440
    g3 = _allgather8(_flat8(pieces, w3), "gather_small")
    tot = _sum_devices(g3, "sum_small").reshape(w3)
    n_mod = 14 * dm
    dmod_all = g3.reshape(N_DEV, w3)[:, :n_mod]
    o = n_mod
    conv_tot = [tot[o + l * 8 * ff: o + (l + 1) * 8 * ff].reshape(4, 2 * ff) for l in range(2)]
    o += 16 * ff
    g_a_norm, g_k_norm, g_q_norm = (tot[o + i * HEAD_DIM: o + (i + 1) * HEAD_DIM] for i in range(3))
    o += 3 * HEAD_DIM
    g_kv_b_f = tot[o:o + HEADS]
    dlb = tot[o + LANES:o + LANES + dm]

    ct = _pad_cols(c_act16[:8].T, LANES)
    dmod_pad = jnp.pad(dmod_all, ((0, LANES - N_DEV), (0, 0)))
    cols_ada = jnp.stack([lax.dynamic_slice_in_dim(dmod_pad, l * 6 * dm + chip * 1536, 1536, axis=1) for l in range(2)])
    cols_kv = lax.dynamic_slice_in_dim(dmod_pad, 12 * dm + chip * 512, 512, axis=1)[None]
    g_ada_w = _outer_grad(ct, cols_ada, "grad_ada_w")
    g_kv_ada_w = _outer_grad(ct, cols_kv, "grad_kv_ada_w")[0]

    my_lb = lax.dynamic_slice_in_dim(lb[0], chip * 256, 256)
    l0 = lax.dynamic_slice_in_dim(dlb, chip * 256, 256) * my_lb * (1.0 - my_lb)
    grads = {
        "ada_w": g_ada_w, "ada_b": jnp.stack([tot[:6 * dm], tot[6 * dm:12 * dm]]),
        "a_lb_logits": jnp.stack([l0, -l0]), "a_norm_g": g_a_norm[None],
        "a_w_out": rs["a_w_out"][None], "kv_ada_w": g_kv_ada_w, "kv_ada_b": tot[12 * dm:14 * dm],
        "kv_w": rs["kv_w"][:, :KV_SHARD], "kv_b_f": g_kv_b_f, "k_norm_g": g_k_norm,
        "b_w_q": rs["b_w_q"][None], "q_norm_g": g_q_norm[None], "b_w_out": rs["b_w_out"][None],
        "ffn_w_up": jnp.stack([rs["up0"], rs["up1"]]),
        "ffn_conv_w": jnp.stack([lax.dynamic_slice_in_dim(ct_l[:CONV_W], chip * FFN_COLS, FFN_COLS, axis=1) for ct_l in conv_tot]),
        "ffn_conv_b": jnp.stack([ct_l[CONV_W] for ct_l in conv_tot]),
        "ffn_w_down": jnp.stack([rs["down0"], rs["down1"]]),
    }
    weights = dict(ada_w=ada_w, ada_b=ada_b, a_w_in=a_w_in, a_lb_logits=a_lb_logits, a_norm_g=a_norm_g, a_w_out=a_w_out,
                   kv_ada_w=kv_ada_w, kv_ada_b=kv_ada_b, kv_w=kv_w, kv_b_f=kv_b_f, k_norm_g=k_norm_g, b_w_q=b_w_q,
                   q_norm_g=q_norm_g, b_w_out=b_w_out, ffn_w_up=ffn_w_up, ffn_conv_w=ffn_conv_w, ffn_conv_b=ffn_conv_b,
                   ffn_w_down=ffn_w_down)
    m_in = dict(ada_w=m_ada_w, ada_b=m_ada_b, a_w_in=m_a_w_in, a_lb_logits=m_a_lb_logits, a_norm_g=m_a_norm_g,
                a_w_out=m_a_w_out, kv_ada_w=m_kv_ada_w, kv_ada_b=m_kv_ada_b, kv_w=m_kv_w, kv_b_f=m_kv_b_f,
                k_norm_g=m_k_norm_g, b_w_q=m_b_w_q, q_norm_g=m_q_norm_g, b_w_out=m_b_w_out, ffn_w_up=m_ffn_w_up,
                ffn_conv_w=m_ffn_conv_w, ffn_conv_b=m_ffn_conv_b, ffn_w_down=m_ffn_w_down)
    v_in = dict(ada_w=v_ada_w, ada_b=v_ada_b, a_w_in=v_a_w_in, a_lb_logits=v_a_lb_logits, a_norm_g=v_a_norm_g,
                a_w_out=v_a_w_out, kv_ada_w=v_kv_ada_w, kv_ada_b=v_kv_ada_b, kv_w=v_kv_w, kv_b_f=v_kv_b_f,
                k_norm_g=v_k_norm_g, b_w_q=v_b_w_q, q_norm_g=v_q_norm_g, b_w_out=v_b_w_out, ffn_w_up=v_ffn_w_up,
                ffn_conv_w=v_ffn_conv_w, ffn_conv_b=v_ffn_conv_b, ffn_w_down=v_ffn_w_down)

    names = list(weights)
    step = lambda n: _adamw(weights[n], grads[n], m_in[n], v_in[n], "adamw_" + n)
    grads = {n: g.reshape(weights[n].shape) for n, g in grads.items()}
    upd = {n: step(n) for n in names if n != "a_w_in"}
    grads["a_w_in"] = reduce_group(3, ("a_w_in",), upd["ffn_w_up"][0])["a_w_in"][None]
    upd["a_w_in"] = step("a_w_in")
    return (loss, grad_x[None], *[grads[n] for n in names], *[upd[n][0] for n in names],
            *[upd[n][1] for n in names], *[upd[n][2] for n in names])
```

```python
import jax
import jax.numpy as jnp
from jax import lax
from jax.experimental import pallas as pl
from jax.experimental.pallas import tpu as pltpu
from jax.experimental.pallas import tpu_sc as plsc

F32 = jnp.float32
BF16 = jnp.bfloat16

D_MODEL = 1024
HEADS = 8
HEAD_DIM = 128
A_CHUNK = 64
D_FF = 2816
CONV_W = 3
EPS = 1e-6
NEG_INF = -1e30
N_CHIPS = 4
N_DEV = 8

ADAM_LR = 0.001
ADAM_B1 = 0.9
ADAM_B2 = 0.999
ADAM_EPS = 1e-08
ADAM_WD = 0.01
ADAM_STEP = 10

SUBLANES = 8
BF16_ROWS = 16
LANES = 128
HALO = BF16_ROWS
ROW_TILE = 512
FFN_COLS = 1408
HGRN_ROWS = 256
ATT_TILE = 512
ATT_SPLIT = 2
MESH = pl.DeviceIdType.MESH


def _sig(x):
    return jax.nn.sigmoid(x)


def _dot(a, b):
    return jnp.dot(a, b, preferred_element_type=F32)


def _dot_nt(a, b):
    return lax.dot_general(a, b, (((1,), (1,)), ((), ())), preferred_element_type=F32)


def _dot_tn(a, b):
    return lax.dot_general(a, b, (((0,), (0,)), ((), ())), preferred_element_type=F32)


def _split2(x):
    hi = x.astype(BF16)
    lo = (x - hi.astype(F32)).astype(BF16)
    return hi, lo


def _dot_f32(a, b):
    ah, al = _split2(a)
    bh, bl = _split2(b)
    return _dot(ah, bh) + _dot(ah, bl) + _dot(al, bh)


def _tri_dot(tri, x):
    hi = x.astype(BF16)
    r = x - hi.astype(F32)
    mid = r.astype(BF16)
    lo = (r - mid.astype(F32)).astype(BF16)
    return _dot(tri, hi) + _dot(tri, mid) + _dot(tri, lo)


def _tri(n, upper=False):
    r = lax.broadcasted_iota(jnp.int32, (n, n), 0)
    c = lax.broadcasted_iota(jnp.int32, (n, n), 1)
    keep = (c >= r) if upper else (c <= r)
    return jnp.where(keep, 1.0, 0.0).astype(BF16)


def _colsum8(v):
    rows, n = v.shape
    return v.reshape(rows // SUBLANES, SUBLANES, n).sum(axis=0)


def _full(shape):
    nd = len(shape)
    return pl.BlockSpec(shape, lambda *_: (0,) * nd)


def _tile(n, want):
    t = min(n, want)
    assert n % t == 0, (n, t)
    return t


def _mm_nn(a, w, groups, out_dtype, name):
    m_rows, k = a.shape
    p_n, _, n = w.shape
    per = p_n // groups
    tm = _tile(m_rows, ROW_TILE)

    def body(a_ref, w_ref, o_ref):
        av = a_ref[...]
        for p in range(p_n):
            o_ref[p // per, :, (p % per) * n:(p % per + 1) * n] = _dot(av, w_ref[p]).astype(out_dtype)

    return pl.pallas_call(
        body, name=name, grid=(m_rows // tm,),
        in_specs=[pl.BlockSpec((tm, k), lambda i: (i, 0)), _full((p_n, k, n))],
        out_specs=pl.BlockSpec((groups, tm, per * n), lambda i: (0, i, 0)),
        out_shape=jax.ShapeDtypeStruct((groups, m_rows, per * n), out_dtype),
    )(a, w)


def _mm_nt(d, w, out_dtype, name, add=None):
    g_n, m_rows, _ = d.shape
    p_n, k, n = w.shape
    per = p_n // g_n
    tm = _tile(m_rows, ROW_TILE)

    def body(*refs):
        d_ref, w_ref = refs[0], refs[1]
        o_ref = refs[-1]
        acc = refs[2][...] if add is not None else None
        for p in range(p_n):
            t = _dot_nt(d_ref[p // per, :, (p % per) * n:(p % per + 1) * n], w_ref[p])
            acc = t if acc is None else acc + t
        o_ref[...] = acc.astype(out_dtype)

    ins = [d, w] + ([add] if add is not None else [])
    specs = [pl.BlockSpec((g_n, tm, per * n), lambda i: (0, i, 0)), _full((p_n, k, n))]
    if add is not None:
        specs.append(pl.BlockSpec((tm, k), lambda i: (i, 0)))
    return pl.pallas_call(
        body, name=name, grid=(m_rows // tm,), in_specs=specs,
        out_specs=pl.BlockSpec((tm, k), lambda i: (i, 0)),
        out_shape=jax.ShapeDtypeStruct((m_rows, k), out_dtype),
    )(*ins)


def _mm_tn(a, d, p_n, name):
    m_rows, k = a.shape
    g_n, _, w_cols = d.shape
    per = p_n // g_n
    n = w_cols // per
    tm = _tile(m_rows, ROW_TILE)
    steps = m_rows // tm

    def body(a_ref, d_ref, o_ref, acc):
        m = pl.program_id(1)

        @pl.when(m == 0)
        def _():
            acc[...] = jnp.zeros_like(acc)

        acc[...] += _dot_tn(a_ref[...], d_ref[...])

        @pl.when(m == steps - 1)
        def _():
            o_ref[...] = acc[...].astype(BF16)

    return pl.pallas_call(
        body, name=name, grid=(p_n, steps),
        in_specs=[pl.BlockSpec((tm, k), lambda p, m: (m, 0)),
                  pl.BlockSpec((None, tm, n), lambda p, m: (p // per, m, p % per))],
        out_specs=pl.BlockSpec((None, k, n), lambda p, m: (p, 0, 0)),
        out_shape=jax.ShapeDtypeStruct((p_n, k, n), BF16),
        scratch_shapes=[pltpu.VMEM((k, n), F32)],
    )(a, d)


def _premix(x, shift, scale, name, branch=None, gate=None):
    s, dm = x.shape
    tm = _tile(s, ROW_TILE)
    with_branch = branch is not None

    def body(*refs):
        x_ref, sh_ref, sc_ref = refs[:3]
        xv = x_ref[...]
        if with_branch:
            xv = xv + refs[4][...] * refs[3][...]
            refs[-2][...] = xv
        inv = lax.rsqrt(jnp.mean(xv * xv, axis=-1, keepdims=True) + EPS)
        refs[-1][...] = (xv * inv * (1.0 + sc_ref[...]) + sh_ref[...]).astype(BF16)

    row = pl.BlockSpec((tm, dm), lambda i: (i, 0))
    vec = _full((1, dm))
    ins, specs = [x, shift, scale], [row, vec, vec]
    out_shape, out_specs = [jax.ShapeDtypeStruct((s, dm), BF16)], [row]
    if with_branch:
        ins += [branch, gate]
        specs += [row, vec]
        out_shape.insert(0, jax.ShapeDtypeStruct((s, dm), F32))
        out_specs.insert(0, row)
    outs = pl.pallas_call(body, name=name, grid=(s // tm,), in_specs=specs, out_specs=out_specs,
                          out_shape=out_shape)(*ins)
    return tuple(outs) if with_branch else outs[0]


def _premix_bwd(x, dh, scale, dres, name):
    s, dm = x.shape
    tm = _tile(s, ROW_TILE)

    def body(x_ref, dh_ref, sc_ref, dres_ref, dx_ref, dsh_ref, dsc_ref):
        i = pl.program_id(0)

        @pl.when(i == 0)
        def _():
            dsh_ref[...] = jnp.zeros_like(dsh_ref)
            dsc_ref[...] = jnp.zeros_like(dsc_ref)

        xv = x_ref[...]
        dhv = dh_ref[...]
        inv = lax.rsqrt(jnp.mean(xv * xv, axis=-1, keepdims=True) + EPS)
        r = xv * inv
        dr = dhv * (1.0 + sc_ref[...])
        dx_ref[...] = dres_ref[...] + inv * (dr - r * jnp.mean(dr * r, axis=-1, keepdims=True))
        dsh_ref[...] += _colsum8(dhv)
        dsc_ref[...] += _colsum8(dhv * r)

    row = pl.BlockSpec((tm, dm), lambda i: (i, 0))
    acc = _full((SUBLANES, dm))
    return pl.pallas_call(
        body, name=name, grid=(s // tm,), in_specs=[row, row, _full((1, dm)), row],
        out_specs=[row, acc, acc],
        out_shape=[jax.ShapeDtypeStruct((s, dm), F32), jax.ShapeDtypeStruct((SUBLANES, dm), F32),
                   jax.ShapeDtypeStruct((SUBLANES, dm), F32)],
    )(x, dh, scale, dres)


def _branch_bwd(dx, y, gate, name):
    s, dm = dx.shape
    tm = _tile(s, ROW_TILE)

    def body(dx_ref, y_ref, g_ref, dy_ref, dg_ref):
        @pl.when(pl.program_id(0) == 0)
        def _():
            dg_ref[...] = jnp.zeros_like(dg_ref)

        dxv = dx_ref[...]
        dy_ref[0] = (dxv * g_ref[...]).astype(BF16)
        dg_ref[...] += _colsum8(dxv * y_ref[...])

    row = pl.BlockSpec((tm, dm), lambda i: (i, 0))
    return pl.pallas_call(
        body, name=name, grid=(s // tm,), in_specs=[row, row, _full((1, dm))],
        out_specs=[pl.BlockSpec((1, tm, dm), lambda i: (0, i, 0)), _full((SUBLANES, dm))],
        out_shape=[jax.ShapeDtypeStruct((1, s, dm), BF16), jax.ShapeDtypeStruct((SUBLANES, dm), F32)],
    )(dx, y, gate)


def _loss_head(x, branch, gate, target, name):
    s, dm = x.shape
    tm = _tile(s, ROW_TILE)

    def body(x_ref, b_ref, g_ref, t_ref, sq_ref, dy_ref):
        @pl.when(pl.program_id(0) == 0)
        def _():
            sq_ref[...] = jnp.zeros_like(sq_ref)

        err = x_ref[...] + g_ref[...] * b_ref[...] - t_ref[...]
        sq_ref[...] += _colsum8(err * err)
        dy_ref[...] = err * (1.0 / dm)

    row = pl.BlockSpec((tm, dm), lambda i: (i, 0))
    return pl.pallas_call(
        body, name=name, grid=(s // tm,), in_specs=[row, row, _full((1, dm)), row],
        out_specs=[_full((SUBLANES, dm)), row],
        out_shape=[jax.ShapeDtypeStruct((SUBLANES, dm), F32), jax.ShapeDtypeStruct((s, dm), F32)],
    )(x, branch, gate, target)


def _conv_taps(e, w, b):
    return w[2:3] * e + w[1:2] * pltpu.roll(e, 1, 0) + w[0:1] * pltpu.roll(e, 2, 0) + b


def _ffn_specs(s, tm, cb):
    hb = tm // HALO
    last = s // HALO - 1
    main = pl.BlockSpec((2, tm, cb), lambda j, i: (0, i, j))
    prev = pl.BlockSpec((2, HALO, cb), lambda j, i: (0, jnp.maximum(i * hb - 1, 0), j))
    nxt = pl.BlockSpec((2, HALO, cb), lambda j, i: (0, jnp.minimum((i + 1) * hb, last), j))
    wspec = pl.BlockSpec((2, CONV_W, cb), lambda j, i: (0, 0, j))
    bspec = pl.BlockSpec((2, 1, cb), lambda j, i: (0, 0, j))
    return main, prev, nxt, wspec, bspec


def _convglu_fwd(u, w, b, name):
    _, s, f = u.shape
    tm = _tile(s, 256)
    cb = _tile(f, FFN_COLS)
    main, prev, _, wspec, bspec = _ffn_specs(s, tm, cb)

    def body(u_ref, up_ref, w_ref, b_ref, a_ref):
        first = jnp.where(pl.program_id(1) > 0, 1.0, 0.0)

        def conv(g):
            e = jnp.concatenate([up_ref[g].astype(F32) * first, u_ref[g].astype(F32)], axis=0)
            return _conv_taps(e, w_ref[g], b_ref[g])[HALO:]

        gate = conv(0)
        a_ref[...] = (gate * _sig(gate) * conv(1)).astype(BF16)

    return pl.pallas_call(
        body, name=name, grid=(f // cb, s // tm), in_specs=[main, prev, wspec, bspec],
        out_specs=pl.BlockSpec((tm, cb), lambda j, i: (i, j)),
        out_shape=jax.ShapeDtypeStruct((s, f), BF16),
    )(u, u, w, b)


def _convglu_bwd(u, da, w, b, name):
    _, s, f = u.shape
    tm = _tile(s, 256)
    cb = _tile(f, FFN_COLS)
    steps = s // tm
    n_ext = tm + 2 * HALO
    main, prev, nxt, wspec, bspec = _ffn_specs(s, tm, cb)
    hb = tm // HALO
    last = s // HALO - 1
    da_main = pl.BlockSpec((tm, cb), lambda j, i: (i, j))
    da_next = pl.BlockSpec((HALO, cb), lambda j, i: (jnp.minimum((i + 1) * hb, last), j))

    def body(u_ref, up_ref, un_ref, da_ref, dan_ref, w_ref, b_ref, du_ref, acc_ref):
        i = pl.program_id(1)
        first = jnp.where(i > 0, 1.0, 0.0)
        notlast = jnp.where(i < steps - 1, 1.0, 0.0)

        @pl.when(i == 0)
        def _():
            acc_ref[...] = jnp.zeros_like(acc_ref)

        def ext(g):
            return jnp.concatenate([up_ref[g].astype(F32) * first, u_ref[g].astype(F32), un_ref[g].astype(F32)], axis=0)

        ug, uv = ext(0), ext(1)
        gate = _conv_taps(ug, w_ref[0], b_ref[0])
        val = _conv_taps(uv, w_ref[1], b_ref[1])
        da_e = jnp.concatenate([jnp.zeros((HALO, cb), F32), da_ref[...].astype(F32),
                                dan_ref[...].astype(F32) * notlast], axis=0)
        sg = _sig(gate)
        d_val = da_e * gate * sg
        d_gate = da_e * val * (sg * (1.0 + gate * (1.0 - sg)))

        def finish(g, d, e):
            wv = w_ref[g]
            du = wv[2:3] * d + wv[1:2] * pltpu.roll(d, n_ext - 1, 0) + wv[0:1] * pltpu.roll(d, n_ext - 2, 0)
            du_ref[g] = du[HALO:HALO + tm].astype(BF16)
            dm = d[HALO:HALO + tm]
            acc_ref[g, 2] += _colsum8(dm * e[HALO:HALO + tm])
            acc_ref[g, 1] += _colsum8(dm * pltpu.roll(e, 1, 0)[HALO:HALO + tm])
            acc_ref[g, 0] += _colsum8(dm * pltpu.roll(e, 2, 0)[HALO:HALO + tm])
            acc_ref[g, 3] += _colsum8(dm)

        finish(0, d_gate, ug)
        finish(1, d_val, uv)

    return pl.pallas_call(
        body, name=name, grid=(f // cb, steps),
        in_specs=[main, prev, nxt, da_main, da_next, wspec, bspec],
        out_specs=[main, pl.BlockSpec((2, 4, SUBLANES, cb), lambda j, i: (0, 0, 0, j))],
        out_shape=[jax.ShapeDtypeStruct((2, s, f), BF16), jax.ShapeDtypeStruct((2, 4, SUBLANES, f), F32)],
    )(u, u, u, da, da, w, b)


def _hgrn_gates(q_raw, f_raw, lb, tri):
    sf = _sig(f_raw)
    fg = lb + (1.0 - lb) * sf
    b = _tri_dot(tri, jnp.log(fg))
    return q_raw * _sig(q_raw), 1.0 - fg, b, fg, sf


def _hgrn_fwd(proj, lb, norm_g, name):
    s = proj.shape[0]
    tb = _tile(s, HGRN_ROWS)
    n_c = tb // A_CHUNK
    half = A_CHUNK // 2

    def body(q_ref, f_ref, v_ref, g_ref, lb_ref, ng_ref, o_ref, yp_ref, st_ref, state):
        @pl.when(pl.program_id(0) == 0)
        def _():
            state[...] = jnp.zeros_like(state)

        tri = _tri(A_CHUNK)
        causal = lax.broadcasted_iota(jnp.int32, (A_CHUNK, A_CHUNK), 1) <= lax.broadcasted_iota(
            jnp.int32, (A_CHUNK, A_CHUNK), 0)

        def chunk(ci, carry):
            rows = pl.ds(pl.multiple_of(ci * A_CHUNK, A_CHUNK), A_CHUNK)
            for h in range(HEADS):
                cs = slice(h * HEAD_DIM, (h + 1) * HEAD_DIM)
                qs, k, b, _, _ = _hgrn_gates(q_ref[rows, cs], f_ref[rows, cs], lb_ref[:, cs], tri)
                b_mid, b_last = b[half:half + 1], b[A_CHUNK - 1:A_CHUNK]
                vb = v_ref[rows, cs].astype(BF16)
                scores = _dot_nt((qs * jnp.exp(b - b_mid)).astype(BF16), (k * jnp.exp(b_mid - b)).astype(BF16))
                scores = jnp.where(causal, scores, 0.0)
                st = state[h]
                st_ref[ci, h] = st
                o = _dot(scores.astype(BF16), vb) + _dot_nt((qs * jnp.exp(b)).astype(BF16), st.astype(BF16))
                state[h] = st * jnp.exp(b_last) + _dot_tn(vb, (k * jnp.exp(b_last - b)).astype(BF16))
                o_ref[rows, cs] = o
                inv = lax.rsqrt(jnp.mean(o * o, axis=-1, keepdims=True) + EPS)
                g_raw = g_ref[rows, cs]
                yp_ref[rows, cs] = (o * inv * ng_ref[:, cs] * (g_raw * _sig(g_raw))).astype(BF16)
            return carry

        lax.fori_loop(0, n_c, chunk, 0)

    col = lambda j: pl.BlockSpec((tb, D_MODEL), lambda i: (i, j))
    vec = _full((1, D_MODEL))
    return pl.pallas_call(
        body, name=name, grid=(s // tb,), in_specs=[col(0), col(1), col(2), col(3), vec, vec],
        out_specs=[col(0), col(0), pl.BlockSpec((n_c, HEADS, HEAD_DIM, HEAD_DIM), lambda i: (i, 0, 0, 0))],
        out_shape=[jax.ShapeDtypeStruct((s, D_MODEL), F32), jax.ShapeDtypeStruct((s, D_MODEL), BF16),
                   jax.ShapeDtypeStruct((s // A_CHUNK, HEADS, HEAD_DIM, HEAD_DIM), F32)],
        scratch_shapes=[pltpu.VMEM((HEADS, HEAD_DIM, HEAD_DIM), F32)],
    )(proj, proj, proj, proj, lb, norm_g)


def _hgrn_bwd(proj, lb, norm_g, o, states, dyp, name):
    s = proj.shape[0]
    tb = _tile(s, HGRN_ROWS)
    n_c = tb // A_CHUNK
    n_b = s // tb
    half = A_CHUNK // 2

    def body(q_ref, f_ref, v_ref, g_ref, lb_ref, ng_ref, o_ref, st_ref, dyp_ref, dp_ref, dlb_ref, dng_ref, dstate):
        @pl.when(pl.program_id(0) == 0)
        def _():
            dstate[...] = jnp.zeros_like(dstate)
            dlb_ref[...] = jnp.zeros_like(dlb_ref)
            dng_ref[...] = jnp.zeros_like(dng_ref)

        tri = _tri(A_CHUNK)
        tri_up = _tri(A_CHUNK, upper=True)
        row_id = lax.broadcasted_iota(jnp.int32, (A_CHUNK, HEAD_DIM), 0)
        causal = lax.broadcasted_iota(jnp.int32, (A_CHUNK, A_CHUNK), 1) <= lax.broadcasted_iota(
            jnp.int32, (A_CHUNK, A_CHUNK), 0)

        def chunk(cj, carry):
            ci = n_c - 1 - cj
            rows = pl.ds(pl.multiple_of(ci * A_CHUNK, A_CHUNK), A_CHUNK)
            for h in range(HEADS):
                cs = slice(h * HEAD_DIM, (h + 1) * HEAD_DIM)
                q_raw, lbh = q_ref[rows, cs], lb_ref[:, cs]
                qs, k, b, fg, sf = _hgrn_gates(q_raw, f_ref[rows, cs], lbh, tri)
                b_mid, b_last = b[half:half + 1], b[A_CHUNK - 1:A_CHUNK]
                e_qi, e_ki, e_q, e_ks = jnp.exp(b - b_mid), jnp.exp(b_mid - b), jnp.exp(b), jnp.exp(b_last - b)
                q_i, k_i, q_e, k_s = qs * e_qi, k * e_ki, qs * e_q, k * e_ks
                vb = v_ref[rows, cs].astype(BF16)
                scores = jnp.where(causal, _dot_nt(q_i.astype(BF16), k_i.astype(BF16)), 0.0)
                ov, g_raw, dy, ng = o_ref[rows, cs], g_ref[rows, cs], dyp_ref[rows, cs], ng_ref[:, cs]
                inv = lax.rsqrt(jnp.mean(ov * ov, axis=-1, keepdims=True) + EPS)
                nrm = ov * inv
                sg = _sig(g_raw)
                gs = g_raw * sg
                dn = dy * ng * gs
                dng_ref[0:1, cs] += jnp.sum(dy * nrm * gs, axis=0, keepdims=True)
                dg_raw = dy * nrm * ng * (sg * (1.0 + g_raw * (1.0 - sg)))
                do = (inv * (dn - nrm * jnp.mean(dn * nrm, axis=-1, keepdims=True))).astype(BF16)
                st_prev = st_ref[ci, h]
                dst = dstate[h]
                dstb = dst.astype(BF16)
                d_scores = jnp.where(causal, _dot_nt(do, vb), 0.0).astype(BF16)
                dv = _dot_tn(scores.astype(BF16), do) + _dot_nt(k_s.astype(BF16), dstb)
                dq_i = _dot(d_scores, k_i.astype(BF16))
                dk_i = _dot_tn(d_scores, q_i.astype(BF16))
                dq_e = _dot(do, st_prev.astype(BF16))
                dk_s = _dot(vb, dstb)
                d_decay = jnp.sum(st_prev * dst, axis=0, keepdims=True)
                dstate[h] = dst * jnp.exp(b_last) + _dot_tn(do, q_e.astype(BF16))
                dq = dq_i * e_qi + dq_e * e_q
                dk = dk_i * e_ki + dk_s * e_ks
                t_qi, t_ki, t_ks = dq_i * q_i, dk_i * k_i, dk_s * k_s
                db = t_qi - t_ki + dq_e * q_e - t_ks
                db_mid = jnp.sum(t_ki - t_qi, axis=0, keepdims=True)
                db_last = jnp.sum(t_ks, axis=0, keepdims=True) + d_decay * jnp.exp(b_last)
                db = db + jnp.where(row_id == half, db_mid, 0.0) + jnp.where(row_id == A_CHUNK - 1, db_last, 0.0)
                dfg = _tri_dot(tri_up, db) / fg - dk
                dlb_ref[0:1, cs] += jnp.sum(dfg * (1.0 - sf), axis=0, keepdims=True)
                sq = _sig(q_raw)
                dp_ref[0, rows, cs] = (dq * (sq * (1.0 + q_raw * (1.0 - sq)))).astype(BF16)
                dp_ref[1, rows, cs] = (dfg * (1.0 - lbh) * sf * (1.0 - sf)).astype(BF16)
                dp_ref[2, rows, cs] = dv.astype(BF16)
                dp_ref[3, rows, cs] = dg_raw.astype(BF16)
            return carry

        lax.fori_loop(0, n_c, chunk, 0)

    col = lambda j: pl.BlockSpec((tb, D_MODEL), lambda i: (n_b - 1 - i, j))
    vec = _full((1, D_MODEL))
    acc = _full((SUBLANES, D_MODEL))
    return pl.pallas_call(
        body, name=name, grid=(n_b,),
        in_specs=[col(0), col(1), col(2), col(3), vec, vec, col(0),
                  pl.BlockSpec((n_c, HEADS, HEAD_DIM, HEAD_DIM), lambda i: (n_b - 1 - i, 0, 0, 0)), col(0)],
        out_specs=[pl.BlockSpec((4, tb, D_MODEL), lambda i: (0, n_b - 1 - i, 0)), acc, acc],
        out_shape=[jax.ShapeDtypeStruct((4, s, D_MODEL), BF16), jax.ShapeDtypeStruct((SUBLANES, D_MODEL), F32),
                   jax.ShapeDtypeStruct((SUBLANES, D_MODEL), F32)],
        scratch_shapes=[pltpu.VMEM((HEADS, HEAD_DIM, HEAD_DIM), F32)],
    )(proj, proj, proj, proj, lb, norm_g, o, states, dyp)


def _headnorm(x, g, mult, name, col0=0):
    s = x.shape[0]
    tm = _tile(s, ROW_TILE)

    def body(x_ref, g_ref, y_ref):
        for h in range(HEADS):
            cs = slice(h * HEAD_DIM, (h + 1) * HEAD_DIM)
            xv = x_ref[:, cs]
            inv = lax.rsqrt(jnp.mean(xv * xv, axis=-1, keepdims=True) + EPS)
            y_ref[:, cs] = (xv * inv * g_ref[:, cs] * mult).astype(BF16)

    return pl.pallas_call(
        body, name=name, grid=(s // tm,),
        in_specs=[pl.BlockSpec((tm, D_MODEL), lambda i: (i, col0)), _full((1, D_MODEL))],
        out_specs=pl.BlockSpec((tm, D_MODEL), lambda i: (i, 0)),
        out_shape=jax.ShapeDtypeStruct((s, D_MODEL), BF16),
    )(x, g)


def _headnorm_bwd(x, g, mult, dy, name, col0=0, extra=None):
    s = x.shape[0]
    tm = _tile(s, ROW_TILE)
    groups = 2 if extra is not None else 1
    head_major = dy.ndim == 3

    def body(*refs):
        x_ref, g_ref, dy_ref = refs[:3]
        dx_ref, dg_ref = refs[-2:]

        @pl.when(pl.program_id(0) == 0)
        def _():
            dg_ref[...] = jnp.zeros_like(dg_ref)

        for h in range(HEADS):
            cs = slice(h * HEAD_DIM, (h + 1) * HEAD_DIM)
            xv, gv = x_ref[:, cs], g_ref[:, cs]
            dyv = dy_ref[h, :, 0:HEAD_DIM] if head_major else dy_ref[:, cs]
            inv = lax.rsqrt(jnp.mean(xv * xv, axis=-1, keepdims=True) + EPS)
            nrm = xv * inv
            dn = dyv * gv * mult
            dg_ref[:, cs] += _colsum8(dyv * nrm * mult)
            dx_ref[0, :, cs] = (inv * (dn - nrm * jnp.mean(dn * nrm, axis=-1, keepdims=True))).astype(BF16)
        if extra is not None:
            dx_ref[1] = refs[3][...]

    row = pl.BlockSpec((tm, D_MODEL), lambda i: (i, 0))
    dy_spec = pl.BlockSpec((HEADS, tm, dy.shape[-1]), lambda i: (0, i, 0)) if head_major else row
    ins = [x, g, dy] + ([extra] if extra is not None else [])
    specs = ([pl.BlockSpec((tm, D_MODEL), lambda i: (i, col0)), _full((1, D_MODEL)), dy_spec]
             + ([row] if extra is not None else []))
    return pl.pallas_call(
        body, name=name, grid=(s // tm,), in_specs=specs,
        out_specs=[pl.BlockSpec((groups, tm, D_MODEL), lambda i: (0, i, 0)), _full((SUBLANES, D_MODEL))],
        out_shape=[jax.ShapeDtypeStruct((groups, s, D_MODEL), BF16), jax.ShapeDtypeStruct((SUBLANES, D_MODEL), F32)],
    )(*ins)


def _log_sigmoid(z):
    return jnp.minimum(z, 0.0) - jnp.log(1.0 + jnp.exp(-jnp.abs(z)))


Q_CUM, Q_ONE, Q_LSE = 0, 3, 6
LOG2E = 1.4426950408889634


def _pieces(v):
    hi = v.astype(BF16).astype(F32)
    mid = (v - hi).astype(BF16).astype(F32)
    lo = ((v - hi) - mid).astype(BF16).astype(F32)
    return hi, mid, lo


def _side(lane, at, v):
    hi, mid, lo = _pieces(v)
    return jnp.where(lane == at, hi, jnp.where(lane == at + 1, mid, jnp.where(lane == at + 2, lo, 0.0)))


def _fcum_fwd(f, bias, name):
    s = f.shape[0]
    tm = _tile(s, ROW_TILE)

    def body(f_ref, b_ref, qa_ref, ka_ref, carry):
        @pl.when(pl.program_id(0) == 0)
        def _():
            carry[...] = jnp.zeros_like(carry)

        cum = _tri_dot(_tri(tm), _log_sigmoid(f_ref[...] + b_ref[...])) + carry[...]
        carry[...] = cum[tm - 1:tm]
        lane = lax.broadcasted_iota(jnp.int32, (tm, LANES), 1)
        ones_q = jnp.where((lane >= Q_ONE) & (lane < Q_LSE), 1.0, 0.0)
        ones_k = jnp.where((lane < Q_ONE) | ((lane >= Q_LSE) & (lane < Q_LSE + 3)), 1.0, 0.0)
        for h in range(HEADS):
            c2 = cum[:, h:h + 1] * LOG2E
            qa_ref[h] = (_side(lane, Q_CUM, c2) + ones_q).astype(BF16)
            ka_ref[h] = (_side(lane, Q_ONE, -c2) + ones_k).astype(BF16)

    side = pl.BlockSpec((HEADS, tm, LANES), lambda i: (0, i, 0))
    return pl.pallas_call(
        body, name=name, grid=(s // tm,),
        in_specs=[pl.BlockSpec((tm, LANES), lambda i: (i, 0)), _full((1, LANES))],
        out_specs=[side, side],
        out_shape=[jax.ShapeDtypeStruct((HEADS, s, LANES), BF16)] * 2,
        scratch_shapes=[pltpu.VMEM((1, LANES), F32)],
    )(f, bias)


def _fcum_bwd(f, bias, dka, dq, name):
    s = f.shape[0]
    tm = _tile(s, ROW_TILE)
    n_b = s // tm
    q_lane = HEAD_DIM + Q_CUM

    def body(f_ref, b_ref, dka_ref, dqa_ref, dz_ref, db_ref, carry):
        @pl.when(pl.program_id(0) == 0)
        def _():
            carry[...] = jnp.zeros_like(carry)
            db_ref[...] = jnp.zeros_like(db_ref)

        lane = lax.broadcasted_iota(jnp.int32, (tm, LANES), 1)
        dcum = jnp.zeros((tm, LANES), F32)
        for h in range(HEADS):
            dcum = dcum + jnp.where(lane == h, dqa_ref[h, :, q_lane:q_lane + 1] - dka_ref[h, :, Q_ONE:Q_ONE + 1], 0.0)
        dlf = _tri_dot(_tri(tm, upper=True), dcum) + carry[...]
        carry[...] = dlf[0:1]
        dz = dlf * _sig(-(f_ref[...] + b_ref[...]))
        dz_ref[0] = dz.astype(BF16)
        db_ref[...] += _colsum8(dz)

    return pl.pallas_call(
        body, name=name, grid=(n_b,),
        in_specs=[pl.BlockSpec((tm, LANES), lambda i: (n_b - 1 - i, 0)), _full((1, LANES)),
                  pl.BlockSpec((HEADS, tm, LANES), lambda i: (0, n_b - 1 - i, 0)),
                  pl.BlockSpec((HEADS, tm, 2 * HEAD_DIM), lambda i: (0, n_b - 1 - i, 0))],
        out_specs=[pl.BlockSpec((1, tm, LANES), lambda i: (0, n_b - 1 - i, 0)), _full((SUBLANES, LANES))],
        out_shape=[jax.ShapeDtypeStruct((1, s, LANES), BF16), jax.ShapeDtypeStruct((SUBLANES, LANES), F32)],
        scratch_shapes=[pltpu.VMEM((1, LANES), F32)],
    )(f, bias, dka, dq)


def _causal_pairs(n_t, key_major):
    if key_major:
        pairs = [(qi, ki) for ki in range(n_t) for qi in range(ki, n_t)]
    else:
        pairs = [(qi, ki) for qi in range(n_t) for ki in range(qi + 1)]
    return (jnp.array([p[0] for p in pairs], jnp.int32), jnp.array([p[1] for p in pairs], jnp.int32))


def _with_side(main_ref, side_ref):
    return jnp.concatenate([main_ref[...], side_ref[...]], axis=1)


def _lane_const(t, lo, hi, value):
    lane = lax.broadcasted_iota(jnp.int32, (t, LANES), 1)
    return jnp.where((lane >= lo) & (lane < hi), value, 0.0).astype(BF16)


def _att_specs(t):
    qmain = pl.BlockSpec((t, HEAD_DIM), lambda h, p, qt, kt: (qt[p], h))
    kmain = pl.BlockSpec((t, HEAD_DIM), lambda h, p, qt, kt: (kt[p], h))
    qside = pl.BlockSpec((None, t, LANES), lambda h, p, qt, kt: (h, qt[p], 0))
    kside = pl.BlockSpec((None, t, LANES), lambda h, p, qt, kt: (h, kt[p], 0))
    return qmain, kmain, qside, kside


def _fox_fwd(q, qa, k, ka, v, qo, name):
    s = q.shape[0]
    t = _tile(s, ATT_TILE)
    sub = t // ATT_SPLIT
    qt, kt = _causal_pairs(s // t, key_major=False)

    def body(qt_ref, kt_ref, q_ref, qa_ref, k_ref, ka_ref, v_ref, og_ref, o_ref, y_ref, qab_ref, m_s, l_s, acc_s):
        pid = pl.program_id(1)
        qi, ki = qt_ref[pid], kt_ref[pid]

        @pl.when(ki == 0)
        def _():
            m_s[...] = jnp.full_like(m_s, NEG_INF)
            l_s[...] = jnp.zeros_like(l_s)
            acc_s[...] = jnp.zeros_like(acc_s)

        def step(diagonal):
            kc = _with_side(k_ref, ka_ref)
            vc = jnp.concatenate([v_ref[...], _lane_const(t, 0, 1, 1.0)], axis=1)
            for r in range(ATT_SPLIT):
                rows = slice(r * sub, (r + 1) * sub)
                n_k = (r + 1) * sub if diagonal else t
                sc = _dot_nt(jnp.concatenate([q_ref[rows], qa_ref[rows]], axis=1), kc[:n_k])
                if diagonal:
                    sc = jnp.where(lax.broadcasted_iota(jnp.int32, (sub, n_k), 1)
                                   <= lax.broadcasted_iota(jnp.int32, (sub, n_k), 0) + r * sub, sc, NEG_INF)
                m_old = m_s[rows]
                m_new = jnp.maximum(m_old, jnp.max(sc, axis=-1, keepdims=True))
                alpha = jnp.exp2(m_old - m_new)
                pv = _dot(jnp.exp2(sc - m_new[:, 0:1]).astype(BF16), vc[:n_k])
                acc_s[rows] = alpha * acc_s[rows] + pv[:, :HEAD_DIM]
                l_s[rows] = alpha * l_s[rows] + pv[:, HEAD_DIM:]
                m_s[rows] = m_new

        @pl.when(ki < qi)
        def _():
            step(False)

        @pl.when(ki == qi)
        def _():
            step(True)
            l = l_s[:, 0:1]
            o = acc_s[...] / l
            o_ref[...] = o
            y_ref[...] = (o * _sig(og_ref[...])).astype(BF16)
            lane = lax.broadcasted_iota(jnp.int32, (t, LANES), 1)
            qab_ref[...] = qa_ref[...] + _side(lane, Q_LSE, -(m_s[:, 0:1] + jnp.log2(l))).astype(BF16)

    qmain, kmain, qside, kside = _att_specs(t)
    return pl.pallas_call(
        body, name=name,
        grid_spec=pltpu.PrefetchScalarGridSpec(
            num_scalar_prefetch=2, grid=(HEADS, qt.shape[0]),
            in_specs=[qmain, qside, kmain, kside, kmain,
                      pl.BlockSpec((t, HEAD_DIM), lambda h, p, qt, kt: (qt[p], HEADS + h))],
            out_specs=[qmain, qmain, qside],
            scratch_shapes=[pltpu.VMEM((t, LANES), F32), pltpu.VMEM((t, LANES), F32), pltpu.VMEM((t, HEAD_DIM), F32)]),
        out_shape=[jax.ShapeDtypeStruct((s, D_MODEL), F32), jax.ShapeDtypeStruct((s, D_MODEL), BF16),
                   jax.ShapeDtypeStruct((HEADS, s, LANES), BF16)],
    )(qt, kt, q, qa, k, ka, v, qo)


def _fox_gate_bwd(o, qo, dy, name):
    s = o.shape[0]
    tm = _tile(s, ROW_TILE)

    def body(o_ref, og_ref, dy_ref, do_ref, dg_ref, dl_ref):
        ov, dyv = o_ref[...], dy_ref[...]
        sg = _sig(og_ref[...])
        do = (dyv * sg).astype(BF16)
        do_ref[...] = do
        dg_ref[...] = (dyv * ov * sg * (1.0 - sg)).astype(BF16)
        prod = do.astype(F32) * ov
        lane = lax.broadcasted_iota(jnp.int32, (tm, LANES), 1)
        for h in range(HEADS):
            delta = jnp.sum(prod[:, h * HEAD_DIM:(h + 1) * HEAD_DIM], axis=-1, keepdims=True)
            dl_ref[h] = _side(lane, 0, delta).astype(BF16)

    row = pl.BlockSpec((tm, D_MODEL), lambda i: (i, 0))
    return pl.pallas_call(
        body, name=name, grid=(s // tm,),
        in_specs=[row, pl.BlockSpec((tm, D_MODEL), lambda i: (i, 1)), row],
        out_specs=[row, row, pl.BlockSpec((HEADS, tm, LANES), lambda i: (0, i, 0))],
        out_shape=[jax.ShapeDtypeStruct((s, D_MODEL), BF16), jax.ShapeDtypeStruct((s, D_MODEL), BF16),
                   jax.ShapeDtypeStruct((HEADS, s, LANES), BF16)],
    )(o, qo, dy)


def _fox_bwd(q, qab, k, ka, v, do, doa, name):
    s = q.shape[0]
    t = _tile(s, ATT_TILE)
    n_t = s // t
    sub = t // ATT_SPLIT
    qt, kt = _causal_pairs(n_t, key_major=True)

    def body(qt_ref, kt_ref, q_ref, qab_ref, k_ref, ka_ref, v_ref, do_ref, doa_ref, dk_ref, dv_ref, dka_ref, dq_ref,
             dk_s, dv_s):
        pid = pl.program_id(1)
        qi, ki = qt_ref[pid], kt_ref[pid]

        @pl.when(pid == 0)
        def _():
            dq_ref[...] = jnp.zeros_like(dq_ref)

        @pl.when(qi == ki)
        def _():
            dk_s[...] = jnp.zeros_like(dk_s)
            dv_s[...] = jnp.zeros_like(dv_s)

        def step(diagonal):
            kc = _with_side(k_ref, ka_ref)
            vc = jnp.concatenate([v_ref[...], _lane_const(t, 0, 3, -1.0)], axis=1)
            for r in range(ATT_SPLIT):
                cols = slice(r * sub, (r + 1) * sub)
                n_k = (r + 1) * sub if diagonal else t
                qc = jnp.concatenate([q_ref[cols], qab_ref[cols]], axis=1)
                sc = _dot_nt(kc[:n_k], qc)
                if diagonal:
                    sc = jnp.where(lax.broadcasted_iota(jnp.int32, (n_k, sub), 0)
                                   <= lax.broadcasted_iota(jnp.int32, (n_k, sub), 1) + r * sub, sc, NEG_INF)
                p = jnp.exp2(sc)
                dp = _dot_nt(vc[:n_k], jnp.concatenate([do_ref[cols], doa_ref[cols]], axis=1))
                ds = (p * dp).astype(BF16)
                dv_s[0:n_k] += _dot(p.astype(BF16), do_ref[cols])
                dk_s[0:n_k] += _dot(ds, qc)
                q_rows = pl.ds(pl.multiple_of(qi * t + r * sub, sub), sub)
                dq_ref[q_rows, :] += _dot_tn(ds, kc[:n_k])

        @pl.when(qi > ki)
        def _():
            step(False)

        @pl.when(qi == ki)
        def _():
            step(True)

        @pl.when(qi == n_t - 1)
        def _():
            dk_ref[...] = dk_s[:, :HEAD_DIM] * (1.0 / LOG2E)
            dka_ref[...] = dk_s[:, HEAD_DIM:]
            dv_ref[...] = dv_s[...].astype(BF16)

    qmain, kmain, qside, kside = _att_specs(t)
    return pl.pallas_call(
        body, name=name,
        grid_spec=pltpu.PrefetchScalarGridSpec(
            num_scalar_prefetch=2, grid=(HEADS, qt.shape[0]),
            in_specs=[qmain, qside, kmain, kside, kmain, qmain, qside],
            out_specs=[kmain, pl.BlockSpec((None, t, HEAD_DIM), lambda h, p, qt, kt: (0, kt[p], h)), kside,
                       pl.BlockSpec((None, s, 2 * HEAD_DIM), lambda h, p, qt, kt: (h, 0, 0))],
            scratch_shapes=[pltpu.VMEM((t, 2 * HEAD_DIM), F32), pltpu.VMEM((t, HEAD_DIM), F32)]),
        out_shape=[jax.ShapeDtypeStruct((s, D_MODEL), F32), jax.ShapeDtypeStruct((1, s, D_MODEL), BF16),
                   jax.ShapeDtypeStruct((HEADS, s, LANES), F32), jax.ShapeDtypeStruct((HEADS, s, 2 * HEAD_DIM), F32)],
    )(qt, kt, q, qab, k, ka, v, do, doa)


def _ffn_forward(x_in, branch, gate, shift, scale, w_up, conv_w, conv_b, w_down, tag):
    x_mid, h = _premix(x_in, shift, scale, tag + "_premix", branch=branch, gate=gate)
    u = _mm_nn(h, w_up, 2, BF16, tag + "_up")
    a = _convglu_fwd(u, conv_w, conv_b, tag + "_convglu")
    ffn = _mm_nn(a, w_down, 1, F32, tag + "_down")[0]
    return x_mid, ffn, (h, u, a)


def _weight_grad_first(a, d, p_n, name):
    return lax.optimization_barrier((_mm_tn(a, d, p_n, name), d))


def _ffn_backward(dx_out, x_mid, ffn, gate, scale, saved, w_up, conv_w, conv_b, w_down, tag):
    h, u, a = saved
    dffn, dgate = _branch_bwd(dx_out, ffn, gate, tag + "_gate_bwd")
    dw_down, dffn = _weight_grad_first(a, dffn, 1, tag + "_down_dw")
    da = _mm_nt(dffn, w_down, BF16, tag + "_down_dx")
    du, dconv = _convglu_bwd(u, da, conv_w, conv_b, tag + "_convglu_bwd")
    dw_up, du = _weight_grad_first(h, du, N_CHIPS, tag + "_up_dw")
    dh = _mm_nt(du, w_up, F32, tag + "_up_dx")
    dx_mid, dshift, dscale = _premix_bwd(x_mid, dh, scale, dx_out, tag + "_premix_bwd")
    return dx_mid, dw_up, dw_down, dict(gate=dgate, shift=dshift, scale=dscale, conv=dconv)


def _local_step(x, target, mods, lb, vecs, weights_at):
    m0, m1, mk = mods["l0"], mods["l1"], mods["kv"]
    h0 = _premix(x, m0[0], m0[1], "l0_premix")
    wts, h0 = weights_at("mixer0", h0)
    proj = _mm_nn(h0, wts["a_w_in"], 1, F32, "l0_in")[0]
    o_a, yp, states = _hgrn_fwd(proj, lb, vecs["a_norm_g"], "l0_hgrn")
    more, yp = weights_at("ffn0", yp)
    wts.update(more)
    y0 = _mm_nn(yp, wts["a_w_out"], 1, F32, "l0_out")[0]
    x1, ffn0, saved0 = _ffn_forward(x, y0, m0[2], m0[3], m0[4], wts["up0"], vecs["conv_w0"], vecs["conv_b0"],
                                    wts["down0"], "l0_ffn")
    more, ffn0 = weights_at("layer1", ffn0)
    wts.update(more)
    x2, hk = _premix(x1, mk[0], mk[1], "kv_premix", branch=ffn0, gate=m0[5])
    k_raw = _mm_nn(hk, wts["kv_k"], 1, F32, "kv_k")[0]
    v_sh = _mm_nn(hk, wts["kv_v"], 1, BF16, "kv_v")[0]
    f_raw = _mm_nn(hk, wts["kv_f"], 1, F32, "kv_f")[0]
    k_sh = _headnorm(k_raw, vecs["k_norm_g"], 1.0, "kv_knorm")
    qa, ka = _fcum_fwd(f_raw, vecs["kv_b_f"], "kv_fcum")
    h1 = _premix(x2, m1[0], m1[1], "l1_premix")
    qo = _mm_nn(h1, wts["b_w_q"], 1, F32, "l1_q")[0]
    q_scale = HEAD_DIM ** -0.5
    q = _headnorm(qo, vecs["q_norm_g"], q_scale * LOG2E, "l1_qnorm")
    o_b, og, qab = _fox_fwd(q, qa, k_sh, ka, v_sh, qo, "l1_fox")
    y1 = _mm_nn(og, wts["b_w_out"], 1, F32, "l1_out")[0]
    x3, ffn1, saved1 = _ffn_forward(x2, y1, m1[2], m1[3], m1[4], wts["up1"], vecs["conv_w1"], vecs["conv_b1"],
                                    wts["down1"], "l1_ffn")
    sq, dx4 = _loss_head(x3, ffn1, m1[5], target, "loss_head")

    big, small = {}, {}
    dx3, big["up1"], big["down1"], s_ffn1 = _ffn_backward(dx4, x3, ffn1, m1[5], m1[4], saved1, wts["up1"],
                                                          vecs["conv_w1"], vecs["conv_b1"], wts["down1"], "l1_ffn")
    dy1, dg1_1 = _branch_bwd(dx3, y1, m1[2], "l1_mix_gate_bwd")
    big["b_w_out"], dy1 = _weight_grad_first(og, dy1, 1, "l1_out_dw")
    d_og = _mm_nt(dy1, wts["b_w_out"], F32, "l1_out_dx")
    do_b, dgate_b, doa = _fox_gate_bwd(o_b, qo, d_og, "l1_fox_gate_bwd")
    dk, dv, dka, dq = _fox_bwd(q, qab, k_sh, ka, v_sh, do_b, doa, "l1_fox_bwd")
    dqo, dqg = _headnorm_bwd(qo, vecs["q_norm_g"], q_scale, dq, "l1_qnorm_bwd", extra=dgate_b)
    big["b_w_q"], dqo = _weight_grad_first(h1, dqo, N_CHIPS, "l1_q_dw")
    dh1 = _mm_nt(dqo, wts["b_w_q"], F32, "l1_q_dx")
    dx2, dsh1_1, dsc1_1 = _premix_bwd(x2, dh1, m1[1], dx3, "l1_premix_bwd")
    dk_raw, dkg = _headnorm_bwd(k_raw, vecs["k_norm_g"], 1.0, dk, "kv_knorm_bwd")
    dz, dbf = _fcum_bwd(f_raw, vecs["kv_b_f"], dka, dq, "kv_fcum_bwd")
    big["kv_k"], dk_raw = _weight_grad_first(hk, dk_raw, 1, "kv_k_dw")
    big["kv_v"], dv = _weight_grad_first(hk, dv, 1, "kv_v_dw")
    big["kv_f"], dz = _weight_grad_first(hk, dz, 1, "kv_f_dw")
    dhk = _mm_nt(dk_raw, wts["kv_k"], F32, "kv_k_dx")
    dhk = _mm_nt(dv, wts["kv_v"], F32, "kv_v_dx", add=dhk)
    dhk = _mm_nt(dz, wts["kv_f"], F32, "kv_f_dx", add=dhk)
    dx2, dshk, dsck = _premix_bwd(x2, dhk, mk[1], dx2, "kv_premix_bwd")
    dx1, big["up0"], big["down0"], s_ffn0 = _ffn_backward(dx2, x1, ffn0, m0[5], m0[4], saved0, wts["up0"],
                                                          vecs["conv_w0"], vecs["conv_b0"], wts["down0"], "l0_ffn")
    dy0, dg1_0 = _branch_bwd(dx1, y0, m0[2], "l0_mix_gate_bwd")
    big["a_w_out"], dy0 = _weight_grad_first(yp, dy0, 1, "l0_out_dw")
    dyp = _mm_nt(dy0, wts["a_w_out"], F32, "l0_out_dx")
    dproj, dlb, dng = _hgrn_bwd(proj, lb, vecs["a_norm_g"], o_a, states, dyp, "l0_hgrn_bwd")
    big["a_w_in"], dproj = _weight_grad_first(h0, dproj, N_CHIPS, "l0_in_dw")
    dh0 = _mm_nt(dproj, wts["a_w_in"], F32, "l0_in_dx")
    grad_x, dsh1_0, dsc1_0 = _premix_bwd(x, dh0, m0[1], dx1, "l0_premix_bwd")

    small["mod_l0"] = [dsh1_0, dsc1_0, dg1_0, s_ffn0["shift"], s_ffn0["scale"], s_ffn0["gate"]]
    small["mod_l1"] = [dsh1_1, dsc1_1, dg1_1, s_ffn1["shift"], s_ffn1["scale"], s_ffn1["gate"]]
    small["mod_kv"] = [dshk, dsck]
    small["conv0"], small["conv1"] = s_ffn0["conv"], s_ffn1["conv"]
    small["a_norm_g"], small["k_norm_g"], small["q_norm_g"] = dng, dkg, dqg
    small["kv_b_f"], small["lb"] = dbf, dlb
    marks = {"attention_bwd": dk, "ffn0_bwd": dx1, "mixer0_bwd": grad_x}
    return sq, grad_x, big, small, marks


HBM = pl.BlockSpec(memory_space=pltpu.HBM)
COMM_CHUNK_ELEMS = 256 * 1024


def _place():
    x, y, c = lax.axis_index("x"), lax.axis_index("y"), lax.axis_index("c")
    chips = [(1 - x, y), (x, 1 - y), (1 - x, 1 - y)]
    return x, y, c, (x, y, 1 - c), chips


def _chunk_rows(rows, cols):
    best = BF16_ROWS
    for r in range(BF16_ROWS, rows + 1, BF16_ROWS):
        if rows % r == 0 and r * cols <= COMM_CHUNK_ELEMS:
            best = r
    assert rows % best == 0, (rows, cols)
    return best


def _allgather8(block, name):
    m_per, n = block.shape

    def body(x_ref, out_ref, send_sems, recv_sems, local_sem):
        x, y, c, sibling, chips = _place()
        me = (x, y, c)

        def rows(px, py, pc):
            return out_ref.at[pl.ds((4 * px + 2 * py + pc) * m_per, m_per), :]

        def copy(k, blk, to, src=None):
            return pltpu.make_async_remote_copy(
                src_ref=rows(*blk) if src is None else src, dst_ref=rows(*blk),
                send_sem=send_sems.at[k], recv_sem=recv_sems.at[k], device_id=to, device_id_type=MESH)

        mine = pltpu.make_async_copy(x_ref, rows(*me), local_sem)
        mine.start()
        first = [copy(0, me, sibling, src=x_ref)]
        first += [copy(1 + j, me, (*chip, c), src=x_ref) for j, chip in enumerate(chips)]
        for cp in first:
            cp.start()
        passed = [copy(4 + j, (*chip, c), sibling) for j, chip in enumerate(chips)]
        for j, chip in enumerate(chips):
            copy(1 + j, (*chip, c), me).wait_recv()
            passed[j].start()
        copy(0, sibling, me).wait_recv()
        for j, chip in enumerate(chips):
            copy(4 + j, (*chip, 1 - c), me).wait_recv()
        for cp in first + passed:
            cp.wait_send()
        mine.wait()

    return pl.pallas_call(
        body, name=name, out_shape=jax.ShapeDtypeStruct((N_DEV * m_per, n), block.dtype),
        in_specs=[pl.BlockSpec(memory_space=pltpu.VMEM)], out_specs=pl.BlockSpec(memory_space=pltpu.VMEM),
        scratch_shapes=[pltpu.SemaphoreType.DMA((7,)), pltpu.SemaphoreType.DMA((7,)), pltpu.SemaphoreType.DMA],
    )(block)


def _cast_own_block(shard, chip, name):
    r, cols = shard.shape
    rows = _chunk_rows(r, cols)

    def body(chip_ref, w_ref, o_ref):
        o_ref[...] = w_ref[...].astype(BF16)

    return pl.pallas_call(
        body, name=name,
        grid_spec=pltpu.PrefetchScalarGridSpec(
            num_scalar_prefetch=1, grid=(r // rows,),
            in_specs=[pl.BlockSpec((rows, cols), lambda i, chip_ref: (i, 0))],
            out_specs=pl.BlockSpec((None, rows, cols), lambda i, chip_ref: (chip_ref[0], i, 0))),
        out_shape=jax.ShapeDtypeStruct((N_CHIPS, r, cols), BF16),
    )(chip, shard)


def _sequencer_gather(bufs, name, collective_id):
    n_t = len(bufs)
    dims = [b.shape[1:] for b in bufs]
    refs = [jax.new_ref(b, memory_space=pltpu.MemorySpace.HBM) for b in bufs]

    @pl.kernel(mesh=plsc.ScalarSubcoreMesh(axis_name="sequencer", num_cores=1), name=name,
               scratch_types=[pltpu.SemaphoreType.DMA((n_t,))] * 4,
               compiler_params=pltpu.CompilerParams(collective_id=collective_id))
    def launch(send_ici, recv_ici, send_d2d, recv_d2d):
        x, y, c, sibling, chips = _place()
        p_me = 2 * x + y
        peers = [sibling] + [(cx, cy, c) for cx, cy in chips]
        barrier = pltpu.get_barrier_semaphore()
        for peer in peers:
            pl.semaphore_signal(barrier, inc=1, device_id=peer, device_id_type=MESH)
        pl.semaphore_wait(barrier, len(peers))

        def waiter(t, sem_s, sem_r):
            win = refs[t].at[pl.ds(0, 3), pl.ds(0, dims[t][0] // 2), :]
            return pltpu.make_async_remote_copy(src_ref=win, dst_ref=win, send_sem=sem_s.at[t], recv_sem=sem_r.at[t],
                                                device_id=sibling, device_id_type=MESH)

        def half_copy(t, chip_idx, to, sem_s, sem_r):
            r2 = dims[t][0] // 2
            win = refs[t].at[chip_idx, pl.ds(c * r2, r2), :]
            return pltpu.make_async_remote_copy(src_ref=win, dst_ref=win, send_sem=sem_s.at[t], recv_sem=sem_r.at[t],
                                                device_id=to, device_id_type=MESH)

        for t in range(n_t):
            for cx, cy in chips:
                half_copy(t, p_me, (cx, cy, c), send_ici, recv_ici).start()
        for t in range(n_t):
            waiter(t, send_ici, recv_ici).wait_recv()
            for cx, cy in chips:
                half_copy(t, 2 * cx + cy, sibling, send_d2d, recv_d2d).start()
        for t in range(n_t):
            waiter(t, send_d2d, recv_d2d).wait_recv()
            waiter(t, send_ici, recv_ici).wait_send()
            waiter(t, send_d2d, recv_d2d).wait_send()

    launch()
    return [r[...] for r in refs]


def _others():
    x, y, c = lax.axis_index("x"), lax.axis_index("y"), lax.axis_index("c")
    flip = lambda v, f: 1 - v if f else v
    return [(flip(x, fx), flip(y, fy), flip(c, fc))
            for fx in (0, 1) for fy in (0, 1) for fc in (0, 1) if (fx, fy, fc) != (0, 0, 0)]


def _handshake(peers):
    barrier = pltpu.get_barrier_semaphore()
    for peer in peers:
        pl.semaphore_signal(barrier, inc=1, device_id=peer, device_id_type=MESH)
    pl.semaphore_wait(barrier, len(peers))


def _sequencer_scatter(parts, name, collective_id):
    n_t = len(parts)
    dims = [p.shape[1:] for p in parts]
    srcs = [jax.new_ref(p, memory_space=pltpu.MemorySpace.HBM) for p in parts]
    inboxes = [jax.empty_ref(jax.ShapeDtypeStruct((N_DEV, r // 2, cols), BF16), memory_space=pltpu.MemorySpace.HBM)
               for r, cols in dims]

    @pl.kernel(mesh=plsc.ScalarSubcoreMesh(axis_name="sequencer", num_cores=1), name=name,
               scratch_types=[pltpu.SemaphoreType.DMA((n_t,))] * 2,
               compiler_params=pltpu.CompilerParams(collective_id=collective_id))
    def launch(send_sem, recv_sem):
        x, y, c = lax.axis_index("x"), lax.axis_index("y"), lax.axis_index("c")
        me = 4 * x + 2 * y + c
        peers = _others()
        _handshake(peers)
        for t in range(n_t):
            h = dims[t][0] // 2
            for qx, qy, qc in peers:
                pltpu.make_async_remote_copy(
                    src_ref=srcs[t].at[2 * qx + qy, pl.ds(qc * h, h), :], dst_ref=inboxes[t].at[me],
                    send_sem=send_sem.at[t], recv_sem=recv_sem.at[t], device_id=(qx, qy, qc), device_id_type=MESH).start()
        for t in range(n_t):
            win = inboxes[t].at[pl.ds(0, N_DEV - 1)]
            both = pltpu.make_async_remote_copy(src_ref=win, dst_ref=win, send_sem=send_sem.at[t],
                                                recv_sem=recv_sem.at[t], device_id=peers[0], device_id_type=MESH)
            both.wait_recv()
            both.wait_send()

    launch()
    return [b[...] for b in inboxes]


def _sum_pieces(part, inbox, place, name):
    _, r, cols = part.shape
    h = r // 2
    rows = _chunk_rows(h, cols)
    steps = h // rows

    def body(place_ref, own_ref, in_ref, o_ref):
        dev = place_ref[2]
        own = own_ref[...].astype(F32)
        acc = jnp.zeros((rows, cols), F32)
        for d in range(N_DEV):
            acc = acc + jnp.where(dev == d, own, in_ref[d].astype(F32))
        o_ref[...] = acc

    return pl.pallas_call(
        body, name=name,
        grid_spec=pltpu.PrefetchScalarGridSpec(
            num_scalar_prefetch=1, grid=(steps,),
            in_specs=[pl.BlockSpec((None, rows, cols), lambda i, pr: (pr[0], pr[1] * steps + i, 0)),
                      pl.BlockSpec((N_DEV, rows, cols), lambda i, pr: (0, i, 0))],
            out_specs=pl.BlockSpec((rows, cols), lambda i, pr: (pr[1] * steps + i, 0))),
        out_shape=jax.ShapeDtypeStruct((r, cols), F32),
    )(place, part, inbox)


def _sequencer_swap_halves(halves, name, collective_id):
    n_t = len(halves)
    refs = [jax.new_ref(a, memory_space=pltpu.MemorySpace.HBM) for a in halves]

    @pl.kernel(mesh=plsc.ScalarSubcoreMesh(axis_name="sequencer", num_cores=1), name=name,
               scratch_types=[pltpu.SemaphoreType.DMA((n_t,))] * 2,
               compiler_params=pltpu.CompilerParams(collective_id=collective_id))
    def launch(send_sem, recv_sem):
        x, y, c = lax.axis_index("x"), lax.axis_index("y"), lax.axis_index("c")
        sibling = (x, y, 1 - c)
        _handshake([sibling])
        copies = []
        for t in range(n_t):
            h = halves[t].shape[0] // 2
            win = refs[t].at[pl.ds(c * h, h), :]
            copies.append(pltpu.make_async_remote_copy(src_ref=win, dst_ref=win, send_sem=send_sem.at[t],
                                                       recv_sem=recv_sem.at[t], device_id=sibling, device_id_type=MESH))
            copies[-1].start()
        for cp in copies:
            cp.wait()

    launch()
    return [r[...] for r in refs]


def _cond_rows(c16, w, act, name):
    n_l, dm, wid = w.shape

    def body(c_ref, w_ref, o_ref, a_ref):
        cv = c_ref[...]
        if act:
            cv = cv * _sig(cv)
        a_ref[...] = cv
        o_ref[...] = _dot_f32(cv, w_ref[...])

    return pl.pallas_call(
        body, name=name, grid=(n_l,),
        in_specs=[_full((16, dm)), pl.BlockSpec((None, dm, wid), lambda l: (l, 0, 0))],
        out_specs=[pl.BlockSpec((None, 16, wid), lambda l: (l, 0, 0)), _full((16, dm))],
        out_shape=[jax.ShapeDtypeStruct((n_l, 16, wid), F32), jax.ShapeDtypeStruct((16, dm), F32)],
    )(c16, w)


def _outer_grad(ct, dm, name):
    n_l, kk, wid = dm.shape
    d_rows = ct.shape[0]

    def body(c_ref, d_ref, o_ref):
        o_ref[...] = _dot_f32(c_ref[...], d_ref[...])

    return pl.pallas_call(
        body, name=name, grid=(n_l,),
        in_specs=[_full((d_rows, kk)), pl.BlockSpec((None, kk, wid), lambda l: (l, 0, 0))],
        out_specs=pl.BlockSpec((None, d_rows, wid), lambda l: (l, 0, 0)),
        out_shape=jax.ShapeDtypeStruct((n_l, d_rows, wid), F32),
    )(ct, dm)


def _sum_devices(g, name):
    rows, n = g.shape

    def body(g_ref, o_ref):
        acc = g_ref[0:SUBLANES, :]
        for dev in range(1, N_DEV):
            acc = acc + g_ref[dev * SUBLANES:(dev + 1) * SUBLANES, :]
        o_ref[...] = acc

    return pl.pallas_call(body, name=name, out_shape=jax.ShapeDtypeStruct((SUBLANES, n), F32))(g)


def _adamw(w, g, m, v, name):
    shape = w.shape
    cols = shape[-1]
    rows = w.size // cols
    tr = rows
    for cand in range(SUBLANES, min(rows, 256) + 1, SUBLANES):
        if rows % cand == 0:
            tr = cand
    if rows * cols <= COMM_CHUNK_ELEMS:
        tr = rows
    c1 = 1.0 / (1.0 - ADAM_B1 ** ADAM_STEP)
    c2 = 1.0 / (1.0 - ADAM_B2 ** ADAM_STEP)

    def body(w_ref, g_ref, m_ref, v_ref, d_ref, mo_ref, vo_ref):
        gv = g_ref[...]
        m_new = ADAM_B1 * m_ref[...] + (1.0 - ADAM_B1) * gv
        v_new = ADAM_B2 * v_ref[...] + (1.0 - ADAM_B2) * (gv * gv)
        mo_ref[...] = m_new
        vo_ref[...] = v_new
        d_ref[...] = -ADAM_LR * ((m_new * c1) / (jnp.sqrt(v_new * c2) + ADAM_EPS) + ADAM_WD * w_ref[...])

    spec = pl.BlockSpec((tr, cols), lambda i: (i, 0))
    outs = pl.pallas_call(
        body, name=name, grid=(rows // tr,), in_specs=[spec] * 4, out_specs=[spec] * 3,
        out_shape=[jax.ShapeDtypeStruct((rows, cols), F32)] * 3,
    )(*[a.reshape(rows, cols) for a in (w, g, m, v)])
    return tuple(o.reshape(shape) for o in outs)


def _pad_cols(a, cols):
    return jnp.pad(a, [(0, 0)] * (a.ndim - 1) + [(0, cols - a.shape[-1])])


def _flat8(parts, width):
    v = jnp.concatenate([p.reshape(-1) for p in parts])
    return jnp.pad(v, (0, width - v.shape[0])).reshape(SUBLANES, width // SUBLANES)


KV_SHARD = 514
KV_SHARD_PAD = 640
BIG = ("a_w_in", "a_w_out", "kv_w", "b_w_q", "b_w_out", "up0", "up1", "down0", "down1")


def kernel(x, c, ada_w, ada_b, a_w_in, a_lb_logits, a_norm_g, a_w_out, kv_ada_w, kv_ada_b, kv_w, kv_b_f, k_norm_g, b_w_q, q_norm_g, b_w_out, ffn_w_up, ffn_conv_w, ffn_conv_b, ffn_w_down, loss_target, m_ada_w, m_ada_b, m_a_w_in, m_a_lb_logits, m_a_norm_g, m_a_w_out, m_kv_ada_w, m_kv_ada_b, m_kv_w, m_kv_b_f, m_k_norm_g, m_b_w_q, m_q_norm_g, m_b_w_out, m_ffn_w_up, m_ffn_conv_w, m_ffn_conv_b, m_ffn_w_down, v_ada_w, v_ada_b, v_a_w_in, v_a_lb_logits, v_a_norm_g, v_a_w_out, v_kv_ada_w, v_kv_ada_b, v_kv_w, v_kv_b_f, v_k_norm_g, v_b_w_q, v_q_norm_g, v_b_w_out, v_ffn_w_up, v_ffn_conv_w, v_ffn_conv_b, v_ffn_w_down):
    dm, ff = D_MODEL, D_FF
    ix, iy, ic = lax.axis_index("x"), lax.axis_index("y"), lax.axis_index("c")
    chip = 2 * ix + iy
    dev = 2 * chip + ic

    w1 = 10240
    g1 = _allgather8(_flat8([c, a_lb_logits, ffn_conv_w], w1), "gather_cond").reshape(N_DEV, w1)
    c_all = g1[:, :dm]
    per_chip = g1[0::2]
    lb_logits = per_chip[:, dm:dm + 512].reshape(N_CHIPS, 2, 256).transpose(1, 0, 2).reshape(2, dm)
    conv_w = per_chip[:, dm + 512:dm + 512 + 2 * CONV_W * FFN_COLS].reshape(N_CHIPS, 2, CONV_W, FFN_COLS)
    conv_w = conv_w.transpose(1, 2, 0, 3).reshape(2, CONV_W, 2, ff).transpose(0, 2, 1, 3)
    conv_b = ffn_conv_b.reshape(2, 2, 1, ff)
    lb = jax.nn.softmax(lb_logits, axis=0)[0:1]

    c16 = jnp.pad(c_all, ((0, 8), (0, 0)))
    mod_ada, c_act16 = _cond_rows(c16, ada_w, True, "mod_ada")
    mod_kv, _ = _cond_rows(c16, kv_ada_w[None], True, "mod_kv")
    mine = jnp.concatenate([mod_ada[0, :8], mod_ada[1, :8], mod_kv[0, :8]], axis=1)
    w2 = mine.shape[1]
    g2 = _allgather8(mine, "gather_mod").reshape(N_DEV, 8, w2)[0::2]
    my_rows = lax.dynamic_index_in_dim(g2, dev, axis=1, keepdims=False)
    mod0 = my_rows[:, 0:1536].reshape(6 * dm) + ada_b[0]
    mod1 = my_rows[:, 1536:3072].reshape(6 * dm) + ada_b[1]
    modk = my_rows[:, 3072:3584].reshape(2 * dm) + kv_ada_b
    mods = {"l0": [v.reshape(1, dm) for v in jnp.split(mod0, 6)],
            "l1": [v.reshape(1, dm) for v in jnp.split(mod1, 6)],
            "kv": [v.reshape(1, dm) for v in jnp.split(modk, 2)]}

    local = [a_w_in[0], a_w_out[0], _pad_cols(kv_w, KV_SHARD_PAD), b_w_q[0], b_w_out[0], ffn_w_up[0], ffn_w_up[1],
             ffn_w_down[0], ffn_w_down[1]]
    chip_arr = chip.reshape(1).astype(jnp.int32)
    own = {n: _cast_own_block(w, chip_arr, "cast_" + n) for n, w in zip(BIG, local)}
    stages = {"mixer0": ("a_w_in",), "ffn0": ("a_w_out", "up0", "down0"),
              "layer1": ("kv_w", "b_w_q", "b_w_out", "up1", "down1")}
    arriving = {st: _sequencer_gather([own[n] for n in names], "gather_" + st, cid)
                for cid, (st, names) in enumerate(stages.items(), start=1)}
    rowwise = lambda g: g.reshape(1, -1, dm)

    def weights_at(stage, token):
        got, token = lax.optimization_barrier((arriving[stage], token))
        g = dict(zip(stages[stage], got))
        if stage == "mixer0":
            return {"a_w_in": g["a_w_in"]}, token
        if stage == "ffn0":
            return {"a_w_out": rowwise(g["a_w_out"]), "up0": g["up0"], "down0": rowwise(g["down0"])}, token
        kv_full = g["kv_w"][:, :, :KV_SHARD].transpose(1, 0, 2).reshape(dm, N_CHIPS * KV_SHARD)
        return {"kv_k": kv_full[None, :, :dm], "kv_v": kv_full[None, :, dm:2 * dm],
                "kv_f": _pad_cols(kv_full[None, :, 2 * dm:], LANES), "b_w_q": g["b_w_q"],
                "b_w_out": rowwise(g["b_w_out"]), "up1": g["up1"], "down1": rowwise(g["down1"])}, token

    vecs = {"a_norm_g": jnp.tile(a_norm_g, (1, HEADS)), "k_norm_g": jnp.tile(k_norm_g[None], (1, HEADS)),
            "q_norm_g": jnp.tile(q_norm_g, (1, HEADS)), "kv_b_f": _pad_cols(kv_b_f[None], LANES),
            "conv_w0": conv_w[0], "conv_b0": conv_b[0], "conv_w1": conv_w[1], "conv_b1": conv_b[1]}

    sq, grad_x, big, small, marks = _local_step(x[0], loss_target[0], mods, lb, vecs, weights_at)
    loss = lax.psum(0.5 * jnp.sum(sq) / dm, ("x", "y", "c"))

    kv_grad = jnp.concatenate([big["kv_k"][0], big["kv_v"][0], big["kv_f"][0][:, :HEADS]], axis=1)
    kv_grad = _pad_cols(kv_grad.reshape(dm, N_CHIPS, KV_SHARD).transpose(1, 0, 2), KV_SHARD_PAD)
    chipwise = lambda g: g.reshape(N_CHIPS, -1, dm)
    parts = dict(zip(BIG, [big["a_w_in"], chipwise(big["a_w_out"]), kv_grad, big["b_w_q"], chipwise(big["b_w_out"]),
                           big["up0"], big["up1"], chipwise(big["down0"]), chipwise(big["down1"])]))
    place = jnp.stack([chip, ic, dev]).astype(jnp.int32)

    def reduce_group(k, names, token):
        inboxes = _sequencer_scatter([parts[n] for n in names], "scatter_grads_%d" % k, 4 + k)
        inboxes, _ = lax.optimization_barrier((inboxes, token))
        halves = [_sum_pieces(parts[n], box, place, "sum_" + n) for n, box in zip(names, inboxes)]
        return dict(zip(names, _sequencer_swap_halves(halves, "swap_grads_%d" % k, 8 + k)))

    rs = reduce_group(0, ("up1", "down1"), marks["attention_bwd"])
    rs.update(reduce_group(1, ("b_w_out", "b_w_q", "kv_w"), marks["ffn0_bwd"]))
    rs.update(reduce_group(2, ("up0", "down0", "a_w_out"), marks["mixer0_bwd"]))

    fold = lambda a: a.sum(axis=0)
    heads = lambda a: fold(a).reshape(HEADS, HEAD_DIM).sum(axis=0)
    conv_flat = lambda a: a.sum(axis=2).transpose(1, 0, 2)
    pieces = ([fold(a) for a in small["mod_l0"]] + [fold(a) for a in small["mod_l1"]] + [fold(a) for a in small["mod_kv"]]
              + [conv_flat(small["conv0"]), conv_flat(small["conv1"]), heads(small["a_norm_g"]), heads(small["k_norm_g"]),
                 heads(small["q_norm_g"]), fold(small["kv_b_f"]), fold(small["lb"])])
    w3 = 61440
    g3 = _allgather8(_flat8(pieces, w3), "gather_small")
    tot = _sum_devices(g3, "sum_small").reshape(w3)
    n_mod = 14 * dm
    dmod_all = g3.reshape(N_DEV, w3)[:, :n_mod]
    o = n_mod
    conv_tot = [tot[o + l * 8 * ff: o + (l + 1) * 8 * ff].reshape(4, 2 * ff) for l in range(2)]
    o += 16 * ff
    g_a_norm, g_k_norm, g_q_norm = (tot[o + i * HEAD_DIM: o + (i + 1) * HEAD_DIM] for i in range(3))
    o += 3 * HEAD_DIM
    g_kv_b_f = tot[o:o + HEADS]
    dlb = tot[o + LANES:o + LANES + dm]

    ct = _pad_cols(c_act16[:8].T, LANES)
    dmod_pad = jnp.pad(dmod_all, ((0, LANES - N_DEV), (0, 0)))
    cols_ada = jnp.stack([lax.dynamic_slice_in_dim(dmod_pad, l * 6 * dm + chip * 1536, 1536, axis=1) for l in range(2)])
    cols_kv = lax.dynamic_slice_in_dim(dmod_pad, 12 * dm + chip * 512, 512, axis=1)[None]
    g_ada_w = _outer_grad(ct, cols_ada, "grad_ada_w")
    g_kv_ada_w = _outer_grad(ct, cols_kv, "grad_kv_ada_w")[0]

    my_lb = lax.dynamic_slice_in_dim(lb[0], chip * 256, 256)
    l0 = lax.dynamic_slice_in_dim(dlb, chip * 256, 256) * my_lb * (1.0 - my_lb)
    grads = {
        "ada_w": g_ada_w, "ada_b": jnp.stack([tot[:6 * dm], tot[6 * dm:12 * dm]]),
        "a_lb_logits": jnp.stack([l0, -l0]), "a_norm_g": g_a_norm[None],
        "a_w_out": rs["a_w_out"][None], "kv_ada_w": g_kv_ada_w, "kv_ada_b": tot[12 * dm:14 * dm],
        "kv_w": rs["kv_w"][:, :KV_SHARD], "kv_b_f": g_kv_b_f, "k_norm_g": g_k_norm,
        "b_w_q": rs["b_w_q"][None], "q_norm_g": g_q_norm[None], "b_w_out": rs["b_w_out"][None],
        "ffn_w_up": jnp.stack([rs["up0"], rs["up1"]]),
        "ffn_conv_w": jnp.stack([lax.dynamic_slice_in_dim(ct_l[:CONV_W], chip * FFN_COLS, FFN_COLS, axis=1) for ct_l in conv_tot]),
        "ffn_conv_b": jnp.stack([ct_l[CONV_W] for ct_l in conv_tot]),
        "ffn_w_down": jnp.stack([rs["down0"], rs["down1"]]),
    }
    weights = dict(ada_w=ada_w, ada_b=ada_b, a_w_in=a_w_in, a_lb_logits=a_lb_logits, a_norm_g=a_norm_g, a_w_out=a_w_out,
                   kv_ada_w=kv_ada_w, kv_ada_b=kv_ada_b, kv_w=kv_w, kv_b_f=kv_b_f, k_norm_g=k_norm_g, b_w_q=b_w_q,
                   q_norm_g=q_norm_g, b_w_out=b_w_out, ffn_w_up=ffn_w_up, ffn_conv_w=ffn_conv_w, ffn_conv_b=ffn_conv_b,
                   ffn_w_down=ffn_w_down)
    m_in = dict(ada_w=m_ada_w, ada_b=m_ada_b, a_w_in=m_a_w_in, a_lb_logits=m_a_lb_logits, a_norm_g=m_a_norm_g,
                a_w_out=m_a_w_out, kv_ada_w=m_kv_ada_w, kv_ada_b=m_kv_ada_b, kv_w=m_kv_w, kv_b_f=m_kv_b_f,
                k_norm_g=m_k_norm_g, b_w_q=m_b_w_q, q_norm_g=m_q_norm_g, b_w_out=m_b_w_out, ffn_w_up=m_ffn_w_up,
                ffn_conv_w=m_ffn_conv_w, ffn_conv_b=m_ffn_conv_b, ffn_w_down=m_ffn_w_down)
    v_in = dict(ada_w=v_ada_w, ada_b=v_ada_b, a_w_in=v_a_w_in, a_lb_logits=v_a_lb_logits, a_norm_g=v_a_norm_g,
                a_w_out=v_a_w_out, kv_ada_w=v_kv_ada_w, kv_ada_b=v_kv_ada_b, kv_w=v_kv_w, kv_b_f=v_kv_b_f,
                k_norm_g=v_k_norm_g, b_w_q=v_b_w_q, q_norm_g=v_q_norm_g, b_w_out=v_b_w_out, ffn_w_up=v_ffn_w_up,
                ffn_conv_w=v_ffn_conv_w, ffn_conv_b=v_ffn_conv_b, ffn_w_down=v_ffn_w_down)

    names = list(weights)
    step = lambda n: _adamw(weights[n], grads[n], m_in[n], v_in[n], "adamw_" + n)
    grads = {n: g.reshape(weights[n].shape) for n, g in grads.items()}
    upd = {n: step(n) for n in names if n != "a_w_in"}
    grads["a_w_in"] = reduce_group(3, ("a_w_in",), [u[0] for u in upd.values()])["a_w_in"][None]
    upd["a_w_in"] = step("a_w_in")
    return (loss, grad_x[None], *[grads[n] for n in names], *[upd[n][0] for n in names],
            *[upd[n][1] for n in names], *[upd[n][2] for n in names])
```

```python
import jax
import jax.numpy as jnp
from jax import lax
from jax.experimental import pallas as pl
from jax.experimental.pallas import tpu as pltpu
from jax.experimental.pallas import tpu_sc as plsc

F32 = jnp.float32
BF16 = jnp.bfloat16

D_MODEL = 1024
HEADS = 8
HEAD_DIM = 128
A_CHUNK = 64
D_FF = 2816
CONV_W = 3
EPS = 1e-6
NEG_INF = -1e30
N_CHIPS = 4
N_DEV = 8

ADAM_LR = 0.001
ADAM_B1 = 0.9
ADAM_B2 = 0.999
ADAM_EPS = 1e-08
ADAM_WD = 0.01
ADAM_STEP = 10

SUBLANES = 8
BF16_ROWS = 16
LANES = 128
HALO = BF16_ROWS
ROW_TILE = 512
FFN_COLS = 1408
HGRN_ROWS = 256
ATT_TILE = 512
ATT_SPLIT = 2
MESH = pl.DeviceIdType.MESH


def _sig(x):
    return jax.nn.sigmoid(x)


def _dot(a, b):
    return jnp.dot(a, b, preferred_element_type=F32)


def _dot_nt(a, b):
    return lax.dot_general(a, b, (((1,), (1,)), ((), ())), preferred_element_type=F32)


def _dot_tn(a, b):
    return lax.dot_general(a, b, (((0,), (0,)), ((), ())), preferred_element_type=F32)


def _split2(x):
    hi = x.astype(BF16)
    lo = (x - hi.astype(F32)).astype(BF16)
    return hi, lo


def _dot_f32(a, b):
    ah, al = _split2(a)
    bh, bl = _split2(b)
    return _dot(ah, bh) + _dot(ah, bl) + _dot(al, bh)


def _tri_dot(tri, x):
    hi = x.astype(BF16)
    r = x - hi.astype(F32)
    mid = r.astype(BF16)
    lo = (r - mid.astype(F32)).astype(BF16)
    return _dot(tri, hi) + _dot(tri, mid) + _dot(tri, lo)


def _tri(n, upper=False):
    r = lax.broadcasted_iota(jnp.int32, (n, n), 0)
    c = lax.broadcasted_iota(jnp.int32, (n, n), 1)
    keep = (c >= r) if upper else (c <= r)
    return jnp.where(keep, 1.0, 0.0).astype(BF16)


def _colsum8(v):
    rows, n = v.shape
    return v.reshape(rows // SUBLANES, SUBLANES, n).sum(axis=0)


def _full(shape):
    nd = len(shape)
    return pl.BlockSpec(shape, lambda *_: (0,) * nd)


def _tile(n, want):
    t = min(n, want)
    assert n % t == 0, (n, t)
    return t


def _mm_nn(a, w, groups, out_dtype, name):
    m_rows, k = a.shape
    p_n, _, n = w.shape
    per = p_n // groups
    tm = _tile(m_rows, ROW_TILE)

    def body(a_ref, w_ref, o_ref):
        av = a_ref[...]
        for p in range(p_n):
            o_ref[p // per, :, (p % per) * n:(p % per + 1) * n] = _dot(av, w_ref[p]).astype(out_dtype)

    return pl.pallas_call(
        body, name=name, grid=(m_rows // tm,),
        in_specs=[pl.BlockSpec((tm, k), lambda i: (i, 0)), _full((p_n, k, n))],
        out_specs=pl.BlockSpec((groups, tm, per * n), lambda i: (0, i, 0)),
        out_shape=jax.ShapeDtypeStruct((groups, m_rows, per * n), out_dtype),
    )(a, w)


def _mm_nt(d, w, out_dtype, name, add=None):
    g_n, m_rows, _ = d.shape
    p_n, k, n = w.shape
    per = p_n // g_n
    tm = _tile(m_rows, ROW_TILE)

    def body(*refs):
        d_ref, w_ref = refs[0], refs[1]
        o_ref = refs[-1]
        acc = refs[2][...] if add is not None else None
        for p in range(p_n):
            t = _dot_nt(d_ref[p // per, :, (p % per) * n:(p % per + 1) * n], w_ref[p])
            acc = t if acc is None else acc + t
        o_ref[...] = acc.astype(out_dtype)

    ins = [d, w] + ([add] if add is not None else [])
    specs = [pl.BlockSpec((g_n, tm, per * n), lambda i: (0, i, 0)), _full((p_n, k, n))]
    if add is not None:
        specs.append(pl.BlockSpec((tm, k), lambda i: (i, 0)))
    return pl.pallas_call(
        body, name=name, grid=(m_rows // tm,), in_specs=specs,
        out_specs=pl.BlockSpec((tm, k), lambda i: (i, 0)),
        out_shape=jax.ShapeDtypeStruct((m_rows, k), out_dtype),
    )(*ins)


def _mm_tn(a, d, p_n, name):
    m_rows, k = a.shape
    g_n, _, w_cols = d.shape
    per = p_n // g_n
    n = w_cols // per
    tm = _tile(m_rows, ROW_TILE)
    steps = m_rows // tm

    def body(a_ref, d_ref, o_ref, acc):
        m = pl.program_id(1)

        @pl.when(m == 0)
        def _():
            acc[...] = jnp.zeros_like(acc)

        acc[...] += _dot_tn(a_ref[...], d_ref[...])

        @pl.when(m == steps - 1)
        def _():
            o_ref[...] = acc[...].astype(BF16)

    return pl.pallas_call(
        body, name=name, grid=(p_n, steps),
        in_specs=[pl.BlockSpec((tm, k), lambda p, m: (m, 0)),
                  pl.BlockSpec((None, tm, n), lambda p, m: (p // per, m, p % per))],
        out_specs=pl.BlockSpec((None, k, n), lambda p, m: (p, 0, 0)),
        out_shape=jax.ShapeDtypeStruct((p_n, k, n), BF16),
        scratch_shapes=[pltpu.VMEM((k, n), F32)],
    )(a, d)


def _premix(x, shift, scale, name, branch=None, gate=None):
    s, dm = x.shape
    tm = _tile(s, ROW_TILE)
    with_branch = branch is not None

    def body(*refs):
        x_ref, sh_ref, sc_ref = refs[:3]
        xv = x_ref[...]
        if with_branch:
            xv = xv + refs[4][...] * refs[3][...]
            refs[-2][...] = xv
        inv = lax.rsqrt(jnp.mean(xv * xv, axis=-1, keepdims=True) + EPS)
        refs[-1][...] = (xv * inv * (1.0 + sc_ref[...]) + sh_ref[...]).astype(BF16)

    row = pl.BlockSpec((tm, dm), lambda i: (i, 0))
    vec = _full((1, dm))
    ins, specs = [x, shift, scale], [row, vec, vec]
    out_shape, out_specs = [jax.ShapeDtypeStruct((s, dm), BF16)], [row]
    if with_branch:
        ins += [branch, gate]
        specs += [row, vec]
        out_shape.insert(0, jax.ShapeDtypeStruct((s, dm), F32))
        out_specs.insert(0, row)
    outs = pl.pallas_call(body, name=name, grid=(s // tm,), in_specs=specs, out_specs=out_specs,
                          out_shape=out_shape)(*ins)
    return tuple(outs) if with_branch else outs[0]


def _premix_bwd(x, dh, scale, dres, name):
    s, dm = x.shape
    tm = _tile(s, ROW_TILE)

    def body(x_ref, dh_ref, sc_ref, dres_ref, dx_ref, dsh_ref, dsc_ref):
        i = pl.program_id(0)

        @pl.when(i == 0)
        def _():
            dsh_ref[...] = jnp.zeros_like(dsh_ref)
            dsc_ref[...] = jnp.zeros_like(dsc_ref)

        xv = x_ref[...]
        dhv = dh_ref[...]
        inv = lax.rsqrt(jnp.mean(xv * xv, axis=-1, keepdims=True) + EPS)
        r = xv * inv
        dr = dhv * (1.0 + sc_ref[...])
        dx_ref[...] = dres_ref[...] + inv * (dr - r * jnp.mean(dr * r, axis=-1, keepdims=True))
        dsh_ref[...] += _colsum8(dhv)
        dsc_ref[...] += _colsum8(dhv * r)

    row = pl.BlockSpec((tm, dm), lambda i: (i, 0))
    acc = _full((SUBLANES, dm))
    return pl.pallas_call(
        body, name=name, grid=(s // tm,), in_specs=[row, row, _full((1, dm)), row],
        out_specs=[row, acc, acc],
        out_shape=[jax.ShapeDtypeStruct((s, dm), F32), jax.ShapeDtypeStruct((SUBLANES, dm), F32),
                   jax.ShapeDtypeStruct((SUBLANES, dm), F32)],
    )(x, dh, scale, dres)


def _branch_bwd(dx, y, gate, name):
    s, dm = dx.shape
    tm = _tile(s, ROW_TILE)

    def body(dx_ref, y_ref, g_ref, dy_ref, dg_ref):
        @pl.when(pl.program_id(0) == 0)
        def _():
            dg_ref[...] = jnp.zeros_like(dg_ref)

        dxv = dx_ref[...]
        dy_ref[0] = (dxv * g_ref[...]).astype(BF16)
        dg_ref[...] += _colsum8(dxv * y_ref[...])

    row = pl.BlockSpec((tm, dm), lambda i: (i, 0))
    return pl.pallas_call(
        body, name=name, grid=(s // tm,), in_specs=[row, row, _full((1, dm))],
        out_specs=[pl.BlockSpec((1, tm, dm), lambda i: (0, i, 0)), _full((SUBLANES, dm))],
        out_shape=[jax.ShapeDtypeStruct((1, s, dm), BF16), jax.ShapeDtypeStruct((SUBLANES, dm), F32)],
    )(dx, y, gate)


def _loss_head(x, branch, gate, target, name):
    s, dm = x.shape
    tm = _tile(s, ROW_TILE)

    def body(x_ref, b_ref, g_ref, t_ref, sq_ref, dy_ref):
        @pl.when(pl.program_id(0) == 0)
        def _():
            sq_ref[...] = jnp.zeros_like(sq_ref)

        err = x_ref[...] + g_ref[...] * b_ref[...] - t_ref[...]
        sq_ref[...] += _colsum8(err * err)
        dy_ref[...] = err * (1.0 / dm)

    row = pl.BlockSpec((tm, dm), lambda i: (i, 0))
    return pl.pallas_call(
        body, name=name, grid=(s // tm,), in_specs=[row, row, _full((1, dm)), row],
        out_specs=[_full((SUBLANES, dm)), row],
        out_shape=[jax.ShapeDtypeStruct((SUBLANES, dm), F32), jax.ShapeDtypeStruct((s, dm), F32)],
    )(x, branch, gate, target)


def _conv_taps(e, w, b):
    return w[2:3] * e + w[1:2] * pltpu.roll(e, 1, 0) + w[0:1] * pltpu.roll(e, 2, 0) + b


def _ffn_specs(s, tm, cb):
    hb = tm // HALO
    last = s // HALO - 1
    main = pl.BlockSpec((2, tm, cb), lambda j, i: (0, i, j))
    prev = pl.BlockSpec((2, HALO, cb), lambda j, i: (0, jnp.maximum(i * hb - 1, 0), j))
    nxt = pl.BlockSpec((2, HALO, cb), lambda j, i: (0, jnp.minimum((i + 1) * hb, last), j))
    wspec = pl.BlockSpec((2, CONV_W, cb), lambda j, i: (0, 0, j))
    bspec = pl.BlockSpec((2, 1, cb), lambda j, i: (0, 0, j))
    return main, prev, nxt, wspec, bspec


def _convglu_fwd(u, w, b, name):
    _, s, f = u.shape
    tm = _tile(s, 256)
    cb = _tile(f, FFN_COLS)
    main, prev, _, wspec, bspec = _ffn_specs(s, tm, cb)

    def body(u_ref, up_ref, w_ref, b_ref, a_ref):
        first = jnp.where(pl.program_id(1) > 0, 1.0, 0.0)

        def conv(g):
            e = jnp.concatenate([up_ref[g].astype(F32) * first, u_ref[g].astype(F32)], axis=0)
            return _conv_taps(e, w_ref[g], b_ref[g])[HALO:]

        gate = conv(0)
        a_ref[...] = (gate * _sig(gate) * conv(1)).astype(BF16)

    return pl.pallas_call(
        body, name=name, grid=(f // cb, s // tm), in_specs=[main, prev, wspec, bspec],
        out_specs=pl.BlockSpec((tm, cb), lambda j, i: (i, j)),
        out_shape=jax.ShapeDtypeStruct((s, f), BF16),
    )(u, u, w, b)


def _convglu_bwd(u, da, w, b, name):
    _, s, f = u.shape
    tm = _tile(s, 256)
    cb = _tile(f, FFN_COLS)
    steps = s // tm
    n_ext = tm + 2 * HALO
    main, prev, nxt, wspec, bspec = _ffn_specs(s, tm, cb)
    hb = tm // HALO
    last = s // HALO - 1
    da_main = pl.BlockSpec((tm, cb), lambda j, i: (i, j))
    da_next = pl.BlockSpec((HALO, cb), lambda j, i: (jnp.minimum((i + 1) * hb, last), j))

    def body(u_ref, up_ref, un_ref, da_ref, dan_ref, w_ref, b_ref, du_ref, acc_ref):
        i = pl.program_id(1)
        first = jnp.where(i > 0, 1.0, 0.0)
        notlast = jnp.where(i < steps - 1, 1.0, 0.0)

        @pl.when(i == 0)
        def _():
            acc_ref[...] = jnp.zeros_like(acc_ref)

        def ext(g):
            return jnp.concatenate([up_ref[g].astype(F32) * first, u_ref[g].astype(F32), un_ref[g].astype(F32)], axis=0)

        ug, uv = ext(0), ext(1)
        gate = _conv_taps(ug, w_ref[0], b_ref[0])
        val = _conv_taps(uv, w_ref[1], b_ref[1])
        da_e = jnp.concatenate([jnp.zeros((HALO, cb), F32), da_ref[...].astype(F32),
                                dan_ref[...].astype(F32) * notlast], axis=0)
        sg = _sig(gate)
        d_val = da_e * gate * sg
        d_gate = da_e * val * (sg * (1.0 + gate * (1.0 - sg)))

        def finish(g, d, e):
            wv = w_ref[g]
            du = wv[2:3] * d + wv[1:2] * pltpu.roll(d, n_ext - 1, 0) + wv[0:1] * pltpu.roll(d, n_ext - 2, 0)
            du_ref[g] = du[HALO:HALO + tm].astype(BF16)
            dm = d[HALO:HALO + tm]
            acc_ref[g, 2] += _colsum8(dm * e[HALO:HALO + tm])
            acc_ref[g, 1] += _colsum8(dm * pltpu.roll(e, 1, 0)[HALO:HALO + tm])
            acc_ref[g, 0] += _colsum8(dm * pltpu.roll(e, 2, 0)[HALO:HALO + tm])
            acc_ref[g, 3] += _colsum8(dm)

        finish(0, d_gate, ug)
        finish(1, d_val, uv)

    return pl.pallas_call(
        body, name=name, grid=(f // cb, steps),
        in_specs=[main, prev, nxt, da_main, da_next, wspec, bspec],
        out_specs=[main, pl.BlockSpec((2, 4, SUBLANES, cb), lambda j, i: (0, 0, 0, j))],
        out_shape=[jax.ShapeDtypeStruct((2, s, f), BF16), jax.ShapeDtypeStruct((2, 4, SUBLANES, f), F32)],
    )(u, u, u, da, da, w, b)


def _hgrn_gates(q_raw, f_raw, lb, tri):
    sf = _sig(f_raw)
    fg = lb + (1.0 - lb) * sf
    b = _tri_dot(tri, jnp.log(fg))
    return q_raw * _sig(q_raw), 1.0 - fg, b, fg, sf


def _hgrn_fwd(proj, lb, norm_g, name):
    s = proj.shape[0]
    tb = _tile(s, HGRN_ROWS)
    n_c = tb // A_CHUNK
    half = A_CHUNK // 2

    def body(q_ref, f_ref, v_ref, g_ref, lb_ref, ng_ref, o_ref, yp_ref, st_ref, state):
        @pl.when(pl.program_id(0) == 0)
        def _():
            state[...] = jnp.zeros_like(state)

        tri = _tri(A_CHUNK)
        causal = lax.broadcasted_iota(jnp.int32, (A_CHUNK, A_CHUNK), 1) <= lax.broadcasted_iota(
            jnp.int32, (A_CHUNK, A_CHUNK), 0)

        def chunk(ci, carry):
            rows = pl.ds(pl.multiple_of(ci * A_CHUNK, A_CHUNK), A_CHUNK)
            for h in range(HEADS):
                cs = slice(h * HEAD_DIM, (h + 1) * HEAD_DIM)
                qs, k, b, _, _ = _hgrn_gates(q_ref[rows, cs], f_ref[rows, cs], lb_ref[:, cs], tri)
                b_mid, b_last = b[half:half + 1], b[A_CHUNK - 1:A_CHUNK]
                vb = v_ref[rows, cs].astype(BF16)
                scores = _dot_nt((qs * jnp.exp(b - b_mid)).astype(BF16), (k * jnp.exp(b_mid - b)).astype(BF16))
                scores = jnp.where(causal, scores, 0.0)
                st = state[h]
                st_ref[ci, h] = st
                o = _dot(scores.astype(BF16), vb) + _dot_nt((qs * jnp.exp(b)).astype(BF16), st.astype(BF16))
                state[h] = st * jnp.exp(b_last) + _dot_tn(vb, (k * jnp.exp(b_last - b)).astype(BF16))
                o_ref[rows, cs] = o
                inv = lax.rsqrt(jnp.mean(o * o, axis=-1, keepdims=True) + EPS)
                g_raw = g_ref[rows, cs]
                yp_ref[rows, cs] = (o * inv * ng_ref[:, cs] * (g_raw * _sig(g_raw))).astype(BF16)
            return carry

        lax.fori_loop(0, n_c, chunk, 0)

    col = lambda j: pl.BlockSpec((tb, D_MODEL), lambda i: (i, j))
    vec = _full((1, D_MODEL))
    return pl.pallas_call(
        body, name=name, grid=(s // tb,), in_specs=[col(0), col(1), col(2), col(3), vec, vec],
        out_specs=[col(0), col(0), pl.BlockSpec((n_c, HEADS, HEAD_DIM, HEAD_DIM), lambda i: (i, 0, 0, 0))],
        out_shape=[jax.ShapeDtypeStruct((s, D_MODEL), F32), jax.ShapeDtypeStruct((s, D_MODEL), BF16),
                   jax.ShapeDtypeStruct((s // A_CHUNK, HEADS, HEAD_DIM, HEAD_DIM), F32)],
        scratch_shapes=[pltpu.VMEM((HEADS, HEAD_DIM, HEAD_DIM), F32)],
    )(proj, proj, proj, proj, lb, norm_g)


def _hgrn_bwd(proj, lb, norm_g, o, states, dyp, name):
    s = proj.shape[0]
    tb = _tile(s, HGRN_ROWS)
    n_c = tb // A_CHUNK
    n_b = s // tb
    half = A_CHUNK // 2

    def body(q_ref, f_ref, v_ref, g_ref, lb_ref, ng_ref, o_ref, st_ref, dyp_ref, dp_ref, dlb_ref, dng_ref, dstate):
        @pl.when(pl.program_id(0) == 0)
        def _():
            dstate[...] = jnp.zeros_like(dstate)
            dlb_ref[...] = jnp.zeros_like(dlb_ref)
            dng_ref[...] = jnp.zeros_like(dng_ref)

        tri = _tri(A_CHUNK)
        tri_up = _tri(A_CHUNK, upper=True)
        row_id = lax.broadcasted_iota(jnp.int32, (A_CHUNK, HEAD_DIM), 0)
        causal = lax.broadcasted_iota(jnp.int32, (A_CHUNK, A_CHUNK), 1) <= lax.broadcasted_iota(
            jnp.int32, (A_CHUNK, A_CHUNK), 0)

        def chunk(cj, carry):
            ci = n_c - 1 - cj
            rows = pl.ds(pl.multiple_of(ci * A_CHUNK, A_CHUNK), A_CHUNK)
            for h in range(HEADS):
                cs = slice(h * HEAD_DIM, (h + 1) * HEAD_DIM)
                q_raw, lbh = q_ref[rows, cs], lb_ref[:, cs]
                qs, k, b, fg, sf = _hgrn_gates(q_raw, f_ref[rows, cs], lbh, tri)
                b_mid, b_last = b[half:half + 1], b[A_CHUNK - 1:A_CHUNK]
                e_qi, e_ki, e_q, e_ks = jnp.exp(b - b_mid), jnp.exp(b_mid - b), jnp.exp(b), jnp.exp(b_last - b)
                q_i, k_i, q_e, k_s = qs * e_qi, k * e_ki, qs * e_q, k * e_ks
                vb = v_ref[rows, cs].astype(BF16)
                scores = jnp.where(causal, _dot_nt(q_i.astype(BF16), k_i.astype(BF16)), 0.0)
                ov, g_raw, dy, ng = o_ref[rows, cs], g_ref[rows, cs], dyp_ref[rows, cs], ng_ref[:, cs]
                inv = lax.rsqrt(jnp.mean(ov * ov, axis=-1, keepdims=True) + EPS)
                nrm = ov * inv
                sg = _sig(g_raw)
                gs = g_raw * sg
                dn = dy * ng * gs
                dng_ref[0:1, cs] += jnp.sum(dy * nrm * gs, axis=0, keepdims=True)
                dg_raw = dy * nrm * ng * (sg * (1.0 + g_raw * (1.0 - sg)))
                do = (inv * (dn - nrm * jnp.mean(dn * nrm, axis=-1, keepdims=True))).astype(BF16)
                st_prev = st_ref[ci, h]
                dst = dstate[h]
                dstb = dst.astype(BF16)
                d_scores = jnp.where(causal, _dot_nt(do, vb), 0.0).astype(BF16)
                dv = _dot_tn(scores.astype(BF16), do) + _dot_nt(k_s.astype(BF16), dstb)
                dq_i = _dot(d_scores, k_i.astype(BF16))
                dk_i = _dot_tn(d_scores, q_i.astype(BF16))
                dq_e = _dot(do, st_prev.astype(BF16))
                dk_s = _dot(vb, dstb)
                d_decay = jnp.sum(st_prev * dst, axis=0, keepdims=True)
                dstate[h] = dst * jnp.exp(b_last) + _dot_tn(do, q_e.astype(BF16))
                dq = dq_i * e_qi + dq_e * e_q
                dk = dk_i * e_ki + dk_s * e_ks
                t_qi, t_ki, t_ks = dq_i * q_i, dk_i * k_i, dk_s * k_s
                db = t_qi - t_ki + dq_e * q_e - t_ks
                db_mid = jnp.sum(t_ki - t_qi, axis=0, keepdims=True)
                db_last = jnp.sum(t_ks, axis=0, keepdims=True) + d_decay * jnp.exp(b_last)
                db = db + jnp.where(row_id == half, db_mid, 0.0) + jnp.where(row_id == A_CHUNK - 1, db_last, 0.0)
                dfg = _tri_dot(tri_up, db) / fg - dk
                dlb_ref[0:1, cs] += jnp.sum(dfg * (1.0 - sf), axis=0, keepdims=True)
                sq = _sig(q_raw)
                dp_ref[0, rows, cs] = (dq * (sq * (1.0 + q_raw * (1.0 - sq)))).astype(BF16)
                dp_ref[1, rows, cs] = (dfg * (1.0 - lbh) * sf * (1.0 - sf)).astype(BF16)
                dp_ref[2, rows, cs] = dv.astype(BF16)
                dp_ref[3, rows, cs] = dg_raw.astype(BF16)
            return carry

        lax.fori_loop(0, n_c, chunk, 0)

    col = lambda j: pl.BlockSpec((tb, D_MODEL), lambda i: (n_b - 1 - i, j))
    vec = _full((1, D_MODEL))
    acc = _full((SUBLANES, D_MODEL))
    return pl.pallas_call(
        body, name=name, grid=(n_b,),
        in_specs=[col(0), col(1), col(2), col(3), vec, vec, col(0),
                  pl.BlockSpec((n_c, HEADS, HEAD_DIM, HEAD_DIM), lambda i: (n_b - 1 - i, 0, 0, 0)), col(0)],
        out_specs=[pl.BlockSpec((4, tb, D_MODEL), lambda i: (0, n_b - 1 - i, 0)), acc, acc],
        out_shape=[jax.ShapeDtypeStruct((4, s, D_MODEL), BF16), jax.ShapeDtypeStruct((SUBLANES, D_MODEL), F32),
                   jax.ShapeDtypeStruct((SUBLANES, D_MODEL), F32)],
        scratch_shapes=[pltpu.VMEM((HEADS, HEAD_DIM, HEAD_DIM), F32)],
    )(proj, proj, proj, proj, lb, norm_g, o, states, dyp)


def _headnorm(x, g, mult, name, col0=0):
    s = x.shape[0]
    tm = _tile(s, ROW_TILE)

    def body(x_ref, g_ref, y_ref):
        for h in range(HEADS):
            cs = slice(h * HEAD_DIM, (h + 1) * HEAD_DIM)
            xv = x_ref[:, cs]
            inv = lax.rsqrt(jnp.mean(xv * xv, axis=-1, keepdims=True) + EPS)
            y_ref[:, cs] = (xv * inv * g_ref[:, cs] * mult).astype(BF16)

    return pl.pallas_call(
        body, name=name, grid=(s // tm,),
        in_specs=[pl.BlockSpec((tm, D_MODEL), lambda i: (i, col0)), _full((1, D_MODEL))],
        out_specs=pl.BlockSpec((tm, D_MODEL), lambda i: (i, 0)),
        out_shape=jax.ShapeDtypeStruct((s, D_MODEL), BF16),
    )(x, g)


def _headnorm_bwd(x, g, mult, dy, name, col0=0, extra=None):
    s = x.shape[0]
    tm = _tile(s, ROW_TILE)
    groups = 2 if extra is not None else 1
    head_major = dy.ndim == 3

    def body(*refs):
        x_ref, g_ref, dy_ref = refs[:3]
        dx_ref, dg_ref = refs[-2:]

        @pl.when(pl.program_id(0) == 0)
        def _():
            dg_ref[...] = jnp.zeros_like(dg_ref)

        for h in range(HEADS):
            cs = slice(h * HEAD_DIM, (h + 1) * HEAD_DIM)
            xv, gv = x_ref[:, cs], g_ref[:, cs]
            dyv = dy_ref[h, :, 0:HEAD_DIM] if head_major else dy_ref[:, cs]
            inv = lax.rsqrt(jnp.mean(xv * xv, axis=-1, keepdims=True) + EPS)
            nrm = xv * inv
            dn = dyv * gv * mult
            dg_ref[:, cs] += _colsum8(dyv * nrm * mult)
            dx_ref[0, :, cs] = (inv * (dn - nrm * jnp.mean(dn * nrm, axis=-1, keepdims=True))).astype(BF16)
        if extra is not None:
            dx_ref[1] = refs[3][...]

    row = pl.BlockSpec((tm, D_MODEL), lambda i: (i, 0))
    dy_spec = pl.BlockSpec((HEADS, tm, dy.shape[-1]), lambda i: (0, i, 0)) if head_major else row
    ins = [x, g, dy] + ([extra] if extra is not None else [])
    specs = ([pl.BlockSpec((tm, D_MODEL), lambda i: (i, col0)), _full((1, D_MODEL)), dy_spec]
             + ([row] if extra is not None else []))
    return pl.pallas_call(
        body, name=name, grid=(s // tm,), in_specs=specs,
        out_specs=[pl.BlockSpec((groups, tm, D_MODEL), lambda i: (0, i, 0)), _full((SUBLANES, D_MODEL))],
        out_shape=[jax.ShapeDtypeStruct((groups, s, D_MODEL), BF16), jax.ShapeDtypeStruct((SUBLANES, D_MODEL), F32)],
    )(*ins)


def _log_sigmoid(z):
    return jnp.minimum(z, 0.0) - jnp.log(1.0 + jnp.exp(-jnp.abs(z)))


Q_CUM, Q_ONE, Q_LSE = 0, 3, 6
LOG2E = 1.4426950408889634


def _pieces(v):
    hi = v.astype(BF16).astype(F32)
    mid = (v - hi).astype(BF16).astype(F32)
    lo = ((v - hi) - mid).astype(BF16).astype(F32)
    return hi, mid, lo


def _side(lane, at, v):
    hi, mid, lo = _pieces(v)
    return jnp.where(lane == at, hi, jnp.where(lane == at + 1, mid, jnp.where(lane == at + 2, lo, 0.0)))


def _fcum_fwd(f, bias, name):
    s = f.shape[0]
    tm = _tile(s, ROW_TILE)

    def body(f_ref, b_ref, qa_ref, ka_ref, carry):
        @pl.when(pl.program_id(0) == 0)
        def _():
            carry[...] = jnp.zeros_like(carry)

        cum = _tri_dot(_tri(tm), _log_sigmoid(f_ref[...] + b_ref[...])) + carry[...]
        carry[...] = cum[tm - 1:tm]
        lane = lax.broadcasted_iota(jnp.int32, (tm, LANES), 1)
        ones_q = jnp.where((lane >= Q_ONE) & (lane < Q_LSE), 1.0, 0.0)
        ones_k = jnp.where((lane < Q_ONE) | ((lane >= Q_LSE) & (lane < Q_LSE + 3)), 1.0, 0.0)
        for h in range(HEADS):
            c2 = cum[:, h:h + 1] * LOG2E
            qa_ref[h] = (_side(lane, Q_CUM, c2) + ones_q).astype(BF16)
            ka_ref[h] = (_side(lane, Q_ONE, -c2) + ones_k).astype(BF16)

    side = pl.BlockSpec((HEADS, tm, LANES), lambda i: (0, i, 0))
    return pl.pallas_call(
        body, name=name, grid=(s // tm,),
        in_specs=[pl.BlockSpec((tm, LANES), lambda i: (i, 0)), _full((1, LANES))],
        out_specs=[side, side],
        out_shape=[jax.ShapeDtypeStruct((HEADS, s, LANES), BF16)] * 2,
        scratch_shapes=[pltpu.VMEM((1, LANES), F32)],
    )(f, bias)


def _fcum_bwd(f, bias, dka, dq, name):
    s = f.shape[0]
    tm = _tile(s, ROW_TILE)
    n_b = s // tm
    q_lane = HEAD_DIM + Q_CUM

    def body(f_ref, b_ref, dka_ref, dqa_ref, dz_ref, db_ref, carry):
        @pl.when(pl.program_id(0) == 0)
        def _():
            carry[...] = jnp.zeros_like(carry)
            db_ref[...] = jnp.zeros_like(db_ref)

        lane = lax.broadcasted_iota(jnp.int32, (tm, LANES), 1)
        dcum = jnp.zeros((tm, LANES), F32)
        for h in range(HEADS):
            dcum = dcum + jnp.where(lane == h, dqa_ref[h, :, q_lane:q_lane + 1] - dka_ref[h, :, Q_ONE:Q_ONE + 1], 0.0)
        dlf = _tri_dot(_tri(tm, upper=True), dcum) + carry[...]
        carry[...] = dlf[0:1]
        dz = dlf * _sig(-(f_ref[...] + b_ref[...]))
        dz_ref[0] = dz.astype(BF16)
        db_ref[...] += _colsum8(dz)

    return pl.pallas_call(
        body, name=name, grid=(n_b,),
        in_specs=[pl.BlockSpec((tm, LANES), lambda i: (n_b - 1 - i, 0)), _full((1, LANES)),
                  pl.BlockSpec((HEADS, tm, LANES), lambda i: (0, n_b - 1 - i, 0)),
                  pl.BlockSpec((HEADS, tm, 2 * HEAD_DIM), lambda i: (0, n_b - 1 - i, 0))],
        out_specs=[pl.BlockSpec((1, tm, LANES), lambda i: (0, n_b - 1 - i, 0)), _full((SUBLANES, LANES))],
        out_shape=[jax.ShapeDtypeStruct((1, s, LANES), BF16), jax.ShapeDtypeStruct((SUBLANES, LANES), F32)],
        scratch_shapes=[pltpu.VMEM((1, LANES), F32)],
    )(f, bias, dka, dq)


def _causal_pairs(n_t, key_major):
    if key_major:
        pairs = [(qi, ki) for ki in range(n_t) for qi in range(ki, n_t)]
    else:
        pairs = [(qi, ki) for qi in range(n_t) for ki in range(qi + 1)]
    return (jnp.array([p[0] for p in pairs], jnp.int32), jnp.array([p[1] for p in pairs], jnp.int32))


def _with_side(main_ref, side_ref):
    return jnp.concatenate([main_ref[...], side_ref[...]], axis=1)


def _lane_const(t, lo, hi, value):
    lane = lax.broadcasted_iota(jnp.int32, (t, LANES), 1)
    return jnp.where((lane >= lo) & (lane < hi), value, 0.0).astype(BF16)


def _att_specs(t):
    qmain = pl.BlockSpec((t, HEAD_DIM), lambda h, p, qt, kt: (qt[p], h))
    kmain = pl.BlockSpec((t, HEAD_DIM), lambda h, p, qt, kt: (kt[p], h))
    qside = pl.BlockSpec((None, t, LANES), lambda h, p, qt, kt: (h, qt[p], 0))
    kside = pl.BlockSpec((None, t, LANES), lambda h, p, qt, kt: (h, kt[p], 0))
    return qmain, kmain, qside, kside


def _fox_fwd(q, qa, k, ka, v, qo, name):
    s = q.shape[0]
    t = _tile(s, ATT_TILE)
    sub = t // ATT_SPLIT
    qt, kt = _causal_pairs(s // t, key_major=False)

    def body(qt_ref, kt_ref, q_ref, qa_ref, k_ref, ka_ref, v_ref, og_ref, o_ref, y_ref, qab_ref, m_s, l_s, acc_s):
        pid = pl.program_id(1)
        qi, ki = qt_ref[pid], kt_ref[pid]

        @pl.when(ki == 0)
        def _():
            m_s[...] = jnp.full_like(m_s, NEG_INF)
            l_s[...] = jnp.zeros_like(l_s)
            acc_s[...] = jnp.zeros_like(acc_s)

        def step(diagonal):
            kc = _with_side(k_ref, ka_ref)
            vc = jnp.concatenate([v_ref[...], _lane_const(t, 0, 1, 1.0)], axis=1)
            for r in range(ATT_SPLIT):
                rows = slice(r * sub, (r + 1) * sub)
                n_k = (r + 1) * sub if diagonal else t
                sc = _dot_nt(jnp.concatenate([q_ref[rows], qa_ref[rows]], axis=1), kc[:n_k])
                if diagonal:
                    sc = jnp.where(lax.broadcasted_iota(jnp.int32, (sub, n_k), 1)
                                   <= lax.broadcasted_iota(jnp.int32, (sub, n_k), 0) + r * sub, sc, NEG_INF)
                m_old = m_s[rows]
                m_new = jnp.maximum(m_old, jnp.max(sc, axis=-1, keepdims=True))
                alpha = jnp.exp2(m_old - m_new)
                pv = _dot(jnp.exp2(sc - m_new[:, 0:1]).astype(BF16), vc[:n_k])
                acc_s[rows] = alpha * acc_s[rows] + pv[:, :HEAD_DIM]
                l_s[rows] = alpha * l_s[rows] + pv[:, HEAD_DIM:]
                m_s[rows] = m_new

        @pl.when(ki < qi)
        def _():
            step(False)

        @pl.when(ki == qi)
        def _():
            step(True)
            l = l_s[:, 0:1]
            o = acc_s[...] / l
            o_ref[...] = o
            y_ref[...] = (o * _sig(og_ref[...])).astype(BF16)
            lane = lax.broadcasted_iota(jnp.int32, (t, LANES), 1)
            qab_ref[...] = qa_ref[...] + _side(lane, Q_LSE, -(m_s[:, 0:1] + jnp.log2(l))).astype(BF16)

    qmain, kmain, qside, kside = _att_specs(t)
    return pl.pallas_call(
        body, name=name,
        grid_spec=pltpu.PrefetchScalarGridSpec(
            num_scalar_prefetch=2, grid=(HEADS, qt.shape[0]),
            in_specs=[qmain, qside, kmain, kside, kmain,
                      pl.BlockSpec((t, HEAD_DIM), lambda h, p, qt, kt: (qt[p], HEADS + h))],
            out_specs=[qmain, qmain, qside],
            scratch_shapes=[pltpu.VMEM((t, LANES), F32), pltpu.VMEM((t, LANES), F32), pltpu.VMEM((t, HEAD_DIM), F32)]),
        out_shape=[jax.ShapeDtypeStruct((s, D_MODEL), F32), jax.ShapeDtypeStruct((s, D_MODEL), BF16),
                   jax.ShapeDtypeStruct((HEADS, s, LANES), BF16)],
    )(qt, kt, q, qa, k, ka, v, qo)


def _fox_gate_bwd(o, qo, dy, name):
    s = o.shape[0]
    tm = _tile(s, ROW_TILE)

    def body(o_ref, og_ref, dy_ref, do_ref, dg_ref, dl_ref):
        ov, dyv = o_ref[...], dy_ref[...]
        sg = _sig(og_ref[...])
        do = (dyv * sg).astype(BF16)
        do_ref[...] = do
        dg_ref[...] = (dyv * ov * sg * (1.0 - sg)).astype(BF16)
        prod = do.astype(F32) * ov
        lane = lax.broadcasted_iota(jnp.int32, (tm, LANES), 1)
        for h in range(HEADS):
            delta = jnp.sum(prod[:, h * HEAD_DIM:(h + 1) * HEAD_DIM], axis=-1, keepdims=True)
            dl_ref[h] = _side(lane, 0, delta).astype(BF16)

    row = pl.BlockSpec((tm, D_MODEL), lambda i: (i, 0))
    return pl.pallas_call(
        body, name=name, grid=(s // tm,),
        in_specs=[row, pl.BlockSpec((tm, D_MODEL), lambda i: (i, 1)), row],
        out_specs=[row, row, pl.BlockSpec((HEADS, tm, LANES), lambda i: (0, i, 0))],
        out_shape=[jax.ShapeDtypeStruct((s, D_MODEL), BF16), jax.ShapeDtypeStruct((s, D_MODEL), BF16),
                   jax.ShapeDtypeStruct((HEADS, s, LANES), BF16)],
    )(o, qo, dy)


def _fox_bwd(q, qab, k, ka, v, do, doa, name):
    s = q.shape[0]
    t = _tile(s, ATT_TILE)
    n_t = s // t
    sub = t // ATT_SPLIT
    qt, kt = _causal_pairs(n_t, key_major=True)

    def body(qt_ref, kt_ref, q_ref, qab_ref, k_ref, ka_ref, v_ref, do_ref, doa_ref, dk_ref, dv_ref, dka_ref, dq_ref,
             dk_s, dv_s):
        pid = pl.program_id(1)
        qi, ki = qt_ref[pid], kt_ref[pid]

        @pl.when(pid == 0)
        def _():
            dq_ref[...] = jnp.zeros_like(dq_ref)

        @pl.when(qi == ki)
        def _():
            dk_s[...] = jnp.zeros_like(dk_s)
            dv_s[...] = jnp.zeros_like(dv_s)

        def step(diagonal):
            kc = _with_side(k_ref, ka_ref)
            vc = jnp.concatenate([v_ref[...], _lane_const(t, 0, 3, -1.0)], axis=1)
            for r in range(ATT_SPLIT):
                cols = slice(r * sub, (r + 1) * sub)
                n_k = (r + 1) * sub if diagonal else t
                qc = jnp.concatenate([q_ref[cols], qab_ref[cols]], axis=1)
                sc = _dot_nt(kc[:n_k], qc)
                if diagonal:
                    sc = jnp.where(lax.broadcasted_iota(jnp.int32, (n_k, sub), 0)
                                   <= lax.broadcasted_iota(jnp.int32, (n_k, sub), 1) + r * sub, sc, NEG_INF)
                p = jnp.exp2(sc)
                dp = _dot_nt(vc[:n_k], jnp.concatenate([do_ref[cols], doa_ref[cols]], axis=1))
                ds = (p * dp).astype(BF16)
                dv_s[0:n_k] += _dot(p.astype(BF16), do_ref[cols])
                dk_s[0:n_k] += _dot(ds, qc)
                q_rows = pl.ds(pl.multiple_of(qi * t + r * sub, sub), sub)
                dq_ref[q_rows, :] += _dot_tn(ds, kc[:n_k])

        @pl.when(qi > ki)
        def _():
            step(False)

        @pl.when(qi == ki)
        def _():
            step(True)

        @pl.when(qi == n_t - 1)
        def _():
            dk_ref[...] = dk_s[:, :HEAD_DIM] * (1.0 / LOG2E)
            dka_ref[...] = dk_s[:, HEAD_DIM:]
            dv_ref[...] = dv_s[...].astype(BF16)

    qmain, kmain, qside, kside = _att_specs(t)
    return pl.pallas_call(
        body, name=name,
        grid_spec=pltpu.PrefetchScalarGridSpec(
            num_scalar_prefetch=2, grid=(HEADS, qt.shape[0]),
            in_specs=[qmain, qside, kmain, kside, kmain, qmain, qside],
            out_specs=[kmain, pl.BlockSpec((None, t, HEAD_DIM), lambda h, p, qt, kt: (0, kt[p], h)), kside,
                       pl.BlockSpec((None, s, 2 * HEAD_DIM), lambda h, p, qt, kt: (h, 0, 0))],
            scratch_shapes=[pltpu.VMEM((t, 2 * HEAD_DIM), F32), pltpu.VMEM((t, HEAD_DIM), F32)]),
        out_shape=[jax.ShapeDtypeStruct((s, D_MODEL), F32), jax.ShapeDtypeStruct((1, s, D_MODEL), BF16),
                   jax.ShapeDtypeStruct((HEADS, s, LANES), F32), jax.ShapeDtypeStruct((HEADS, s, 2 * HEAD_DIM), F32)],
    )(qt, kt, q, qab, k, ka, v, do, doa)


def _ffn_forward(x_in, branch, gate, shift, scale, w_up, conv_w, conv_b, w_down, tag):
    x_mid, h = _premix(x_in, shift, scale, tag + "_premix", branch=branch, gate=gate)
    u = _mm_nn(h, w_up, 2, BF16, tag + "_up")
    a = _convglu_fwd(u, conv_w, conv_b, tag + "_convglu")
    ffn = _mm_nn(a, w_down, 1, F32, tag + "_down")[0]
    return x_mid, ffn, (h, u, a)


def _weight_grad_first(a, d, p_n, name):
    return lax.optimization_barrier((_mm_tn(a, d, p_n, name), d))


def _ffn_backward(dx_out, x_mid, ffn, gate, scale, saved, w_up, conv_w, conv_b, w_down, tag):
    h, u, a = saved
    dffn, dgate = _branch_bwd(dx_out, ffn, gate, tag + "_gate_bwd")
    dw_down, dffn = _weight_grad_first(a, dffn, 1, tag + "_down_dw")
    da = _mm_nt(dffn, w_down, BF16, tag + "_down_dx")
    du, dconv = _convglu_bwd(u, da, conv_w, conv_b, tag + "_convglu_bwd")
    dw_up, du = _weight_grad_first(h, du, N_CHIPS, tag + "_up_dw")
    dh = _mm_nt(du, w_up, F32, tag + "_up_dx")
    dx_mid, dshift, dscale = _premix_bwd(x_mid, dh, scale, dx_out, tag + "_premix_bwd")
    return dx_mid, dw_up, dw_down, dict(gate=dgate, shift=dshift, scale=dscale, conv=dconv)


def _local_step(x, target, mods, lb, vecs, weights_at):
    m0, m1, mk = mods["l0"], mods["l1"], mods["kv"]
    h0 = _premix(x, m0[0], m0[1], "l0_premix")
    wts, h0 = weights_at("mixer0", h0)
    proj = _mm_nn(h0, wts["a_w_in"], 1, F32, "l0_in")[0]
    o_a, yp, states = _hgrn_fwd(proj, lb, vecs["a_norm_g"], "l0_hgrn")
    more, yp = weights_at("ffn0", yp)
    wts.update(more)
    y0 = _mm_nn(yp, wts["a_w_out"], 1, F32, "l0_out")[0]
    x1, ffn0, saved0 = _ffn_forward(x, y0, m0[2], m0[3], m0[4], wts["up0"], vecs["conv_w0"], vecs["conv_b0"],
                                    wts["down0"], "l0_ffn")
    more, ffn0 = weights_at("layer1", ffn0)
    wts.update(more)
    x2, hk = _premix(x1, mk[0], mk[1], "kv_premix", branch=ffn0, gate=m0[5])
    k_raw = _mm_nn(hk, wts["kv_k"], 1, F32, "kv_k")[0]
    v_sh = _mm_nn(hk, wts["kv_v"], 1, BF16, "kv_v")[0]
    f_raw = _mm_nn(hk, wts["kv_f"], 1, F32, "kv_f")[0]
    k_sh = _headnorm(k_raw, vecs["k_norm_g"], 1.0, "kv_knorm")
    qa, ka = _fcum_fwd(f_raw, vecs["kv_b_f"], "kv_fcum")
    h1 = _premix(x2, m1[0], m1[1], "l1_premix")
    qo = _mm_nn(h1, wts["b_w_q"], 1, F32, "l1_q")[0]
    q_scale = HEAD_DIM ** -0.5
    q = _headnorm(qo, vecs["q_norm_g"], q_scale * LOG2E, "l1_qnorm")
    o_b, og, qab = _fox_fwd(q, qa, k_sh, ka, v_sh, qo, "l1_fox")
    y1 = _mm_nn(og, wts["b_w_out"], 1, F32, "l1_out")[0]
    x3, ffn1, saved1 = _ffn_forward(x2, y1, m1[2], m1[3], m1[4], wts["up1"], vecs["conv_w1"], vecs["conv_b1"],
                                    wts["down1"], "l1_ffn")
    sq, dx4 = _loss_head(x3, ffn1, m1[5], target, "loss_head")

    big, small = {}, {}
    dx3, big["up1"], big["down1"], s_ffn1 = _ffn_backward(dx4, x3, ffn1, m1[5], m1[4], saved1, wts["up1"],
                                                          vecs["conv_w1"], vecs["conv_b1"], wts["down1"], "l1_ffn")
    dy1, dg1_1 = _branch_bwd(dx3, y1, m1[2], "l1_mix_gate_bwd")
    big["b_w_out"], dy1 = _weight_grad_first(og, dy1, 1, "l1_out_dw")
    d_og = _mm_nt(dy1, wts["b_w_out"], F32, "l1_out_dx")
    do_b, dgate_b, doa = _fox_gate_bwd(o_b, qo, d_og, "l1_fox_gate_bwd")
    dk, dv, dka, dq = _fox_bwd(q, qab, k_sh, ka, v_sh, do_b, doa, "l1_fox_bwd")
    dqo, dqg = _headnorm_bwd(qo, vecs["q_norm_g"], q_scale, dq, "l1_qnorm_bwd", extra=dgate_b)
    big["b_w_q"], dqo = _weight_grad_first(h1, dqo, N_CHIPS, "l1_q_dw")
    dh1 = _mm_nt(dqo, wts["b_w_q"], F32, "l1_q_dx")
    dx2, dsh1_1, dsc1_1 = _premix_bwd(x2, dh1, m1[1], dx3, "l1_premix_bwd")
    dk_raw, dkg = _headnorm_bwd(k_raw, vecs["k_norm_g"], 1.0, dk, "kv_knorm_bwd")
    dz, dbf = _fcum_bwd(f_raw, vecs["kv_b_f"], dka, dq, "kv_fcum_bwd")
    big["kv_k"], dk_raw = _weight_grad_first(hk, dk_raw, 1, "kv_k_dw")
    big["kv_v"], dv = _weight_grad_first(hk, dv, 1, "kv_v_dw")
    big["kv_f"], dz = _weight_grad_first(hk, dz, 1, "kv_f_dw")
    dhk = _mm_nt(dk_raw, wts["kv_k"], F32, "kv_k_dx")
    dhk = _mm_nt(dv, wts["kv_v"], F32, "kv_v_dx", add=dhk)
    dhk = _mm_nt(dz, wts["kv_f"], F32, "kv_f_dx", add=dhk)
    dx2, dshk, dsck = _premix_bwd(x2, dhk, mk[1], dx2, "kv_premix_bwd")
    dx1, big["up0"], big["down0"], s_ffn0 = _ffn_backward(dx2, x1, ffn0, m0[5], m0[4], saved0, wts["up0"],
                                                          vecs["conv_w0"], vecs["conv_b0"], wts["down0"], "l0_ffn")
    dy0, dg1_0 = _branch_bwd(dx1, y0, m0[2], "l0_mix_gate_bwd")
    big["a_w_out"], dy0 = _weight_grad_first(yp, dy0, 1, "l0_out_dw")
    dyp = _mm_nt(dy0, wts["a_w_out"], F32, "l0_out_dx")
    dproj, dlb, dng = _hgrn_bwd(proj, lb, vecs["a_norm_g"], o_a, states, dyp, "l0_hgrn_bwd")
    big["a_w_in"], dproj = _weight_grad_first(h0, dproj, N_CHIPS, "l0_in_dw")
    dh0 = _mm_nt(dproj, wts["a_w_in"], F32, "l0_in_dx")
    grad_x, dsh1_0, dsc1_0 = _premix_bwd(x, dh0, m0[1], dx1, "l0_premix_bwd")

    small["mod_l0"] = [dsh1_0, dsc1_0, dg1_0, s_ffn0["shift"], s_ffn0["scale"], s_ffn0["gate"]]
    small["mod_l1"] = [dsh1_1, dsc1_1, dg1_1, s_ffn1["shift"], s_ffn1["scale"], s_ffn1["gate"]]
    small["mod_kv"] = [dshk, dsck]
    small["conv0"], small["conv1"] = s_ffn0["conv"], s_ffn1["conv"]
    small["a_norm_g"], small["k_norm_g"], small["q_norm_g"] = dng, dkg, dqg
    small["kv_b_f"], small["lb"] = dbf, dlb
    marks = {"attention_bwd": dk, "ffn0_bwd": dx1, "mixer0_bwd": grad_x}
    return sq, grad_x, big, small, marks


HBM = pl.BlockSpec(memory_space=pltpu.HBM)
COMM_CHUNK_ELEMS = 256 * 1024


def _place():
    x, y, c = lax.axis_index("x"), lax.axis_index("y"), lax.axis_index("c")
    chips = [(1 - x, y), (x, 1 - y), (1 - x, 1 - y)]
    return x, y, c, (x, y, 1 - c), chips


def _chunk_rows(rows, cols):
    best = BF16_ROWS
    for r in range(BF16_ROWS, rows + 1, BF16_ROWS):
        if rows % r == 0 and r * cols <= COMM_CHUNK_ELEMS:
            best = r
    assert rows % best == 0, (rows, cols)
    return best


def _allgather8(block, name):
    m_per, n = block.shape

    def body(x_ref, out_ref, send_sems, recv_sems, local_sem):
        x, y, c, sibling, chips = _place()
        me = (x, y, c)

        def rows(px, py, pc):
            return out_ref.at[pl.ds((4 * px + 2 * py + pc) * m_per, m_per), :]

        def copy(k, blk, to, src=None):
            return pltpu.make_async_remote_copy(
                src_ref=rows(*blk) if src is None else src, dst_ref=rows(*blk),
                send_sem=send_sems.at[k], recv_sem=recv_sems.at[k], device_id=to, device_id_type=MESH)

        mine = pltpu.make_async_copy(x_ref, rows(*me), local_sem)
        mine.start()
        first = [copy(0, me, sibling, src=x_ref)]
        first += [copy(1 + j, me, (*chip, c), src=x_ref) for j, chip in enumerate(chips)]
        for cp in first:
            cp.start()
        passed = [copy(4 + j, (*chip, c), sibling) for j, chip in enumerate(chips)]
        for j, chip in enumerate(chips):
            copy(1 + j, (*chip, c), me).wait_recv()
            passed[j].start()
        copy(0, sibling, me).wait_recv()
        for j, chip in enumerate(chips):
            copy(4 + j, (*chip, 1 - c), me).wait_recv()
        for cp in first + passed:
            cp.wait_send()
        mine.wait()

    return pl.pallas_call(
        body, name=name, out_shape=jax.ShapeDtypeStruct((N_DEV * m_per, n), block.dtype),
        in_specs=[pl.BlockSpec(memory_space=pltpu.VMEM)], out_specs=pl.BlockSpec(memory_space=pltpu.VMEM),
        scratch_shapes=[pltpu.SemaphoreType.DMA((7,)), pltpu.SemaphoreType.DMA((7,)), pltpu.SemaphoreType.DMA],
    )(block)


def _cast_own_block(shards, layer, chip, name):
    _, r, cols = shards.shape
    rows = _chunk_rows(r, cols)

    def body(chip_ref, w_ref, o_ref):
        o_ref[...] = w_ref[...].astype(BF16)

    return pl.pallas_call(
        body, name=name,
        grid_spec=pltpu.PrefetchScalarGridSpec(
            num_scalar_prefetch=1, grid=(r // rows,),
            in_specs=[pl.BlockSpec((None, rows, cols), lambda i, chip_ref: (layer, i, 0))],
            out_specs=pl.BlockSpec((None, rows, cols), lambda i, chip_ref: (chip_ref[0], i, 0))),
        out_shape=jax.ShapeDtypeStruct((N_CHIPS, r, cols), BF16),
    )(chip, shards)


def _sequencer_gather(bufs, name, collective_id):
    n_t = len(bufs)
    dims = [b.shape[1:] for b in bufs]
    refs = [jax.new_ref(b, memory_space=pltpu.MemorySpace.HBM) for b in bufs]

    @pl.kernel(mesh=plsc.ScalarSubcoreMesh(axis_name="sequencer", num_cores=1), name=name,
               scratch_types=[pltpu.SemaphoreType.DMA((n_t,))] * 4,
               compiler_params=pltpu.CompilerParams(collective_id=collective_id))
    def launch(send_ici, recv_ici, send_d2d, recv_d2d):
        x, y, c, sibling, chips = _place()
        p_me = 2 * x + y
        peers = [sibling] + [(cx, cy, c) for cx, cy in chips]
        barrier = pltpu.get_barrier_semaphore()
        for peer in peers:
            pl.semaphore_signal(barrier, inc=1, device_id=peer, device_id_type=MESH)
        pl.semaphore_wait(barrier, len(peers))

        def waiter(t, sem_s, sem_r):
            win = refs[t].at[pl.ds(0, 3), pl.ds(0, dims[t][0] // 2), :]
            return pltpu.make_async_remote_copy(src_ref=win, dst_ref=win, send_sem=sem_s.at[t], recv_sem=sem_r.at[t],
                                                device_id=sibling, device_id_type=MESH)

        def half_copy(t, chip_idx, to, sem_s, sem_r):
            r2 = dims[t][0] // 2
            win = refs[t].at[chip_idx, pl.ds(c * r2, r2), :]
            return pltpu.make_async_remote_copy(src_ref=win, dst_ref=win, send_sem=sem_s.at[t], recv_sem=sem_r.at[t],
                                                device_id=to, device_id_type=MESH)

        for t in range(n_t):
            for cx, cy in chips:
                half_copy(t, p_me, (cx, cy, c), send_ici, recv_ici).start()
        for t in range(n_t):
            waiter(t, send_ici, recv_ici).wait_recv()
            for cx, cy in chips:
                half_copy(t, 2 * cx + cy, sibling, send_d2d, recv_d2d).start()
        for t in range(n_t):
            waiter(t, send_d2d, recv_d2d).wait_recv()
            waiter(t, send_ici, recv_ici).wait_send()
            waiter(t, send_d2d, recv_d2d).wait_send()

    launch()
    return [r[...] for r in refs]


def _others():
    x, y, c = lax.axis_index("x"), lax.axis_index("y"), lax.axis_index("c")
    flip = lambda v, f: 1 - v if f else v
    return [(flip(x, fx), flip(y, fy), flip(c, fc))
            for fx in (0, 1) for fy in (0, 1) for fc in (0, 1) if (fx, fy, fc) != (0, 0, 0)]


def _handshake(peers):
    barrier = pltpu.get_barrier_semaphore()
    for peer in peers:
        pl.semaphore_signal(barrier, inc=1, device_id=peer, device_id_type=MESH)
    pl.semaphore_wait(barrier, len(peers))


def _sequencer_scatter(parts, name, collective_id):
    n_t = len(parts)
    dims = [p.shape[1:] for p in parts]
    srcs = [jax.new_ref(p, memory_space=pltpu.MemorySpace.HBM) for p in parts]
    inboxes = [jax.empty_ref(jax.ShapeDtypeStruct((N_DEV, r // 2, cols), BF16), memory_space=pltpu.MemorySpace.HBM)
               for r, cols in dims]

    @pl.kernel(mesh=plsc.ScalarSubcoreMesh(axis_name="sequencer", num_cores=1), name=name,
               scratch_types=[pltpu.SemaphoreType.DMA((n_t,))] * 2,
               compiler_params=pltpu.CompilerParams(collective_id=collective_id))
    def launch(send_sem, recv_sem):
        x, y, c = lax.axis_index("x"), lax.axis_index("y"), lax.axis_index("c")
        me = 4 * x + 2 * y + c
        peers = _others()
        _handshake(peers)
        for t in range(n_t):
            h = dims[t][0] // 2
            for qx, qy, qc in peers:
                pltpu.make_async_remote_copy(
                    src_ref=srcs[t].at[2 * qx + qy, pl.ds(qc * h, h), :], dst_ref=inboxes[t].at[me],
                    send_sem=send_sem.at[t], recv_sem=recv_sem.at[t], device_id=(qx, qy, qc), device_id_type=MESH).start()
        for t in range(n_t):
            win = inboxes[t].at[pl.ds(0, N_DEV - 1)]
            both = pltpu.make_async_remote_copy(src_ref=win, dst_ref=win, send_sem=send_sem.at[t],
                                                recv_sem=recv_sem.at[t], device_id=peers[0], device_id_type=MESH)
            both.wait_recv()
            both.wait_send()

    launch()
    return [b[...] for b in inboxes]


def _sum_pieces(part, inbox, place, name):
    _, r, cols = part.shape
    h = r // 2
    rows = _chunk_rows(h, cols)
    steps = h // rows

    def body(place_ref, own_ref, in_ref, o_ref):
        dev = place_ref[2]
        own = own_ref[...].astype(F32)
        acc = jnp.zeros((rows, cols), F32)
        for d in range(N_DEV):
            acc = acc + jnp.where(dev == d, own, in_ref[d].astype(F32))
        o_ref[...] = acc

    return pl.pallas_call(
        body, name=name,
        grid_spec=pltpu.PrefetchScalarGridSpec(
            num_scalar_prefetch=1, grid=(steps,),
            in_specs=[pl.BlockSpec((None, rows, cols), lambda i, pr: (pr[0], pr[1] * steps + i, 0)),
                      pl.BlockSpec((N_DEV, rows, cols), lambda i, pr: (0, i, 0))],
            out_specs=pl.BlockSpec((rows, cols), lambda i, pr: (pr[1] * steps + i, 0))),
        out_shape=jax.ShapeDtypeStruct((r, cols), F32),
    )(place, part, inbox)


def _sequencer_swap_halves(halves, name, collective_id):
    n_t = len(halves)
    refs = [jax.new_ref(a, memory_space=pltpu.MemorySpace.HBM) for a in halves]

    @pl.kernel(mesh=plsc.ScalarSubcoreMesh(axis_name="sequencer", num_cores=1), name=name,
               scratch_types=[pltpu.SemaphoreType.DMA((n_t,))] * 2,
               compiler_params=pltpu.CompilerParams(collective_id=collective_id))
    def launch(send_sem, recv_sem):
        x, y, c = lax.axis_index("x"), lax.axis_index("y"), lax.axis_index("c")
        sibling = (x, y, 1 - c)
        _handshake([sibling])
        copies = []
        for t in range(n_t):
            h = halves[t].shape[0] // 2
            win = refs[t].at[pl.ds(c * h, h), :]
            copies.append(pltpu.make_async_remote_copy(src_ref=win, dst_ref=win, send_sem=send_sem.at[t],
                                                       recv_sem=recv_sem.at[t], device_id=sibling, device_id_type=MESH))
            copies[-1].start()
        for cp in copies:
            cp.wait()

    launch()
    return [r[...] for r in refs]


def _cond_rows(c16, w, act, name):
    n_l, dm, wid = w.shape

    def body(c_ref, w_ref, o_ref, a_ref):
        cv = c_ref[...]
        if act:
            cv = cv * _sig(cv)
        a_ref[...] = cv
        o_ref[...] = _dot_f32(cv, w_ref[...])

    return pl.pallas_call(
        body, name=name, grid=(n_l,),
        in_specs=[_full((16, dm)), pl.BlockSpec((None, dm, wid), lambda l: (l, 0, 0))],
        out_specs=[pl.BlockSpec((None, 16, wid), lambda l: (l, 0, 0)), _full((16, dm))],
        out_shape=[jax.ShapeDtypeStruct((n_l, 16, wid), F32), jax.ShapeDtypeStruct((16, dm), F32)],
    )(c16, w)


def _outer_grad(ct, dm, name):
    n_l, kk, wid = dm.shape
    d_rows = ct.shape[0]

    def body(c_ref, d_ref, o_ref):
        o_ref[...] = _dot_f32(c_ref[...], d_ref[...])

    return pl.pallas_call(
        body, name=name, grid=(n_l,),
        in_specs=[_full((d_rows, kk)), pl.BlockSpec((None, kk, wid), lambda l: (l, 0, 0))],
        out_specs=pl.BlockSpec((None, d_rows, wid), lambda l: (l, 0, 0)),
        out_shape=jax.ShapeDtypeStruct((n_l, d_rows, wid), F32),
    )(ct, dm)


def _sum_devices(g, name):
    rows, n = g.shape

    def body(g_ref, o_ref):
        acc = g_ref[0:SUBLANES, :]
        for dev in range(1, N_DEV):
            acc = acc + g_ref[dev * SUBLANES:(dev + 1) * SUBLANES, :]
        o_ref[...] = acc

    return pl.pallas_call(body, name=name, out_shape=jax.ShapeDtypeStruct((SUBLANES, n), F32))(g)


def _adamw(w, g, m, v, name):
    shape = w.shape
    cols = shape[-1]
    rows = w.size // cols
    tr = rows
    for cand in range(SUBLANES, min(rows, 256) + 1, SUBLANES):
        if rows % cand == 0:
            tr = cand
    if rows * cols <= COMM_CHUNK_ELEMS:
        tr = rows
    c1 = 1.0 / (1.0 - ADAM_B1 ** ADAM_STEP)
    c2 = 1.0 / (1.0 - ADAM_B2 ** ADAM_STEP)

    def body(w_ref, g_ref, m_ref, v_ref, d_ref, mo_ref, vo_ref):
        gv = g_ref[...]
        m_new = ADAM_B1 * m_ref[...] + (1.0 - ADAM_B1) * gv
        v_new = ADAM_B2 * v_ref[...] + (1.0 - ADAM_B2) * (gv * gv)
        mo_ref[...] = m_new
        vo_ref[...] = v_new
        d_ref[...] = -ADAM_LR * ((m_new * c1) / (jnp.sqrt(v_new * c2) + ADAM_EPS) + ADAM_WD * w_ref[...])

    spec = pl.BlockSpec((tr, cols), lambda i: (i, 0))
    outs = pl.pallas_call(
        body, name=name, grid=(rows // tr,), in_specs=[spec] * 4, out_specs=[spec] * 3,
        out_shape=[jax.ShapeDtypeStruct((rows, cols), F32)] * 3,
    )(*[a.reshape(rows, cols) for a in (w, g, m, v)])
    return tuple(o.reshape(shape) for o in outs)


def _pad_cols(a, cols):
    return jnp.pad(a, [(0, 0)] * (a.ndim - 1) + [(0, cols - a.shape[-1])])


def _flat8(parts, width):
    v = jnp.concatenate([p.reshape(-1) for p in parts])
    return jnp.pad(v, (0, width - v.shape[0])).reshape(SUBLANES, width // SUBLANES)


KV_SHARD = 514
KV_SHARD_PAD = 640
BIG = ("a_w_in", "a_w_out", "kv_w", "b_w_q", "b_w_out", "up0", "up1", "down0", "down1")


def kernel(x, c, ada_w, ada_b, a_w_in, a_lb_logits, a_norm_g, a_w_out, kv_ada_w, kv_ada_b, kv_w, kv_b_f, k_norm_g, b_w_q, q_norm_g, b_w_out, ffn_w_up, ffn_conv_w, ffn_conv_b, ffn_w_down, loss_target, m_ada_w, m_ada_b, m_a_w_in, m_a_lb_logits, m_a_norm_g, m_a_w_out, m_kv_ada_w, m_kv_ada_b, m_kv_w, m_kv_b_f, m_k_norm_g, m_b_w_q, m_q_norm_g, m_b_w_out, m_ffn_w_up, m_ffn_conv_w, m_ffn_conv_b, m_ffn_w_down, v_ada_w, v_ada_b, v_a_w_in, v_a_lb_logits, v_a_norm_g, v_a_w_out, v_kv_ada_w, v_kv_ada_b, v_kv_w, v_kv_b_f, v_k_norm_g, v_b_w_q, v_q_norm_g, v_b_w_out, v_ffn_w_up, v_ffn_conv_w, v_ffn_conv_b, v_ffn_w_down):
    dm, ff = D_MODEL, D_FF
    ix, iy, ic = lax.axis_index("x"), lax.axis_index("y"), lax.axis_index("c")
    chip = 2 * ix + iy
    dev = 2 * chip + ic

    w1 = 10240
    g1 = _allgather8(_flat8([c, a_lb_logits, ffn_conv_w], w1), "gather_cond").reshape(N_DEV, w1)
    c_all = g1[:, :dm]
    per_chip = g1[0::2]
    lb_logits = per_chip[:, dm:dm + 512].reshape(N_CHIPS, 2, 256).transpose(1, 0, 2).reshape(2, dm)
    conv_w = per_chip[:, dm + 512:dm + 512 + 2 * CONV_W * FFN_COLS].reshape(N_CHIPS, 2, CONV_W, FFN_COLS)
    conv_w = conv_w.transpose(1, 2, 0, 3).reshape(2, CONV_W, 2, ff).transpose(0, 2, 1, 3)
    conv_b = ffn_conv_b.reshape(2, 2, 1, ff)
    lb = jax.nn.softmax(lb_logits, axis=0)[0:1]

    c16 = jnp.pad(c_all, ((0, 8), (0, 0)))
    mod_ada, c_act16 = _cond_rows(c16, ada_w, True, "mod_ada")
    mod_kv, _ = _cond_rows(c16, kv_ada_w[None], True, "mod_kv")
    mine = jnp.concatenate([mod_ada[0, :8], mod_ada[1, :8], mod_kv[0, :8]], axis=1)
    w2 = mine.shape[1]
    g2 = _allgather8(mine, "gather_mod").reshape(N_DEV, 8, w2)[0::2]
    my_rows = lax.dynamic_index_in_dim(g2, dev, axis=1, keepdims=False)
    mod0 = my_rows[:, 0:1536].reshape(6 * dm) + ada_b[0]
    mod1 = my_rows[:, 1536:3072].reshape(6 * dm) + ada_b[1]
    modk = my_rows[:, 3072:3584].reshape(2 * dm) + kv_ada_b
    mods = {"l0": [v.reshape(1, dm) for v in jnp.split(mod0, 6)],
            "l1": [v.reshape(1, dm) for v in jnp.split(mod1, 6)],
            "kv": [v.reshape(1, dm) for v in jnp.split(modk, 2)]}

    local = [(a_w_in, 0), (a_w_out, 0), (_pad_cols(kv_w, KV_SHARD_PAD)[None], 0), (b_w_q, 0), (b_w_out, 0),
             (ffn_w_up, 0), (ffn_w_up, 1), (ffn_w_down, 0), (ffn_w_down, 1)]
    chip_arr = chip.reshape(1).astype(jnp.int32)
    own = {n: _cast_own_block(w, layer, chip_arr, "cast_" + n) for n, (w, layer) in zip(BIG, local)}
    stages = {"mixer0": ("a_w_in",), "ffn0": ("a_w_out", "up0", "down0"),
              "layer1": ("kv_w", "b_w_q", "b_w_out", "up1", "down1")}
    arriving = {st: _sequencer_gather([own[n] for n in names], "gather_" + st, cid)
                for cid, (st, names) in enumerate(stages.items(), start=1)}
    rowwise = lambda g: g.reshape(1, -1, dm)

    def weights_at(stage, token):
        got, token = lax.optimization_barrier((arriving[stage], token))
        g = dict(zip(stages[stage], got))
        if stage == "mixer0":
            return {"a_w_in": g["a_w_in"]}, token
        if stage == "ffn0":
            return {"a_w_out": rowwise(g["a_w_out"]), "up0": g["up0"], "down0": rowwise(g["down0"])}, token
        kv_full = g["kv_w"][:, :, :KV_SHARD].transpose(1, 0, 2).reshape(dm, N_CHIPS * KV_SHARD)
        return {"kv_k": kv_full[None, :, :dm], "kv_v": kv_full[None, :, dm:2 * dm],
                "kv_f": _pad_cols(kv_full[None, :, 2 * dm:], LANES), "b_w_q": g["b_w_q"],
                "b_w_out": rowwise(g["b_w_out"]), "up1": g["up1"], "down1": rowwise(g["down1"])}, token

    vecs = {"a_norm_g": jnp.tile(a_norm_g, (1, HEADS)), "k_norm_g": jnp.tile(k_norm_g[None], (1, HEADS)),
            "q_norm_g": jnp.tile(q_norm_g, (1, HEADS)), "kv_b_f": _pad_cols(kv_b_f[None], LANES),
            "conv_w0": conv_w[0], "conv_b0": conv_b[0], "conv_w1": conv_w[1], "conv_b1": conv_b[1]}

    sq, grad_x, big, small, marks = _local_step(x[0], loss_target[0], mods, lb, vecs, weights_at)
    loss = lax.psum(0.5 * jnp.sum(sq) / dm, ("x", "y", "c"))

    kv_grad = jnp.concatenate([big["kv_k"][0], big["kv_v"][0], big["kv_f"][0][:, :HEADS]], axis=1)
    kv_grad = _pad_cols(kv_grad.reshape(dm, N_CHIPS, KV_SHARD).transpose(1, 0, 2), KV_SHARD_PAD)
    chipwise = lambda g: g.reshape(N_CHIPS, -1, dm)
    parts = dict(zip(BIG, [big["a_w_in"], chipwise(big["a_w_out"]), kv_grad, big["b_w_q"], chipwise(big["b_w_out"]),
                           big["up0"], big["up1"], chipwise(big["down0"]), chipwise(big["down1"])]))
    place = jnp.stack([chip, ic, dev]).astype(jnp.int32)

    served = []

    def reduce_group(k, names, token):
        mine = [parts[n] for n in names]
        if served:
            mine, _ = lax.optimization_barrier((mine, served[-1]))
        inboxes = _sequencer_scatter(mine, "scatter_grads_%d" % k, 4 + k)
        served.append(inboxes)
        inboxes, _ = lax.optimization_barrier((inboxes, token))
        halves = [_sum_pieces(parts[n], box, place, "sum_" + n) for n, box in zip(names, inboxes)]
        return dict(zip(names, _sequencer_swap_halves(halves, "swap_grads_%d" % k, 8 + k)))

    rs = reduce_group(0, ("up1", "down1"), marks["attention_bwd"])
    rs.update(reduce_group(1, ("b_w_out", "b_w_q", "kv_w"), marks["ffn0_bwd"]))
    rs.update(reduce_group(2, ("up0", "down0", "a_w_out"), marks["mixer0_bwd"]))

    fold = lambda a: a.sum(axis=0)
    heads = lambda a: fold(a).reshape(HEADS, HEAD_DIM).sum(axis=0)
    conv_flat = lambda a: a.sum(axis=2).transpose(1, 0, 2)
    pieces = ([fold(a) for a in small["mod_l0"]] + [fold(a) for a in small["mod_l1"]] + [fold(a) for a in small["mod_kv"]]
              + [conv_flat(small["conv0"]), conv_flat(small["conv1"]), heads(small["a_norm_g"]), heads(small["k_norm_g"]),
                 heads(small["q_norm_g"]), fold(small["kv_b_f"]), fold(small["lb"])])
    w3 = 61440
    g3 = _allgather8(_flat8(pieces, w3), "gather_small")
    tot = _sum_devices(g3, "sum_small").reshape(w3)
    n_mod = 14 * dm
    dmod_all = g3.reshape(N_DEV, w3)[:, :n_mod]
    o = n_mod
    conv_tot = [tot[o + l * 8 * ff: o + (l + 1) * 8 * ff].reshape(4, 2 * ff) for l in range(2)]
    o += 16 * ff
    g_a_norm, g_k_norm, g_q_norm = (tot[o + i * HEAD_DIM: o + (i + 1) * HEAD_DIM] for i in range(3))
    o += 3 * HEAD_DIM
    g_kv_b_f = tot[o:o + HEADS]
    dlb = tot[o + LANES:o + LANES + dm]

    ct = _pad_cols(c_act16[:8].T, LANES)
    dmod_pad = jnp.pad(dmod_all, ((0, LANES - N_DEV), (0, 0)))
    cols_ada = jnp.stack([lax.dynamic_slice_in_dim(dmod_pad, l * 6 * dm + chip * 1536, 1536, axis=1) for l in range(2)])
    cols_kv = lax.dynamic_slice_in_dim(dmod_pad, 12 * dm + chip * 512, 512, axis=1)[None]
    g_ada_w = _outer_grad(ct, cols_ada, "grad_ada_w")
    g_kv_ada_w = _outer_grad(ct, cols_kv, "grad_kv_ada_w")[0]

    my_lb = lax.dynamic_slice_in_dim(lb[0], chip * 256, 256)
    l0 = lax.dynamic_slice_in_dim(dlb, chip * 256, 256) * my_lb * (1.0 - my_lb)
    grads = {
        "ada_w": g_ada_w, "ada_b": jnp.stack([tot[:6 * dm], tot[6 * dm:12 * dm]]),
        "a_lb_logits": jnp.stack([l0, -l0]), "a_norm_g": g_a_norm[None],
        "a_w_out": rs["a_w_out"][None], "kv_ada_w": g_kv_ada_w, "kv_ada_b": tot[12 * dm:14 * dm],
        "kv_w": rs["kv_w"][:, :KV_SHARD], "kv_b_f": g_kv_b_f, "k_norm_g": g_k_norm,
        "b_w_q": rs["b_w_q"][None], "q_norm_g": g_q_norm[None], "b_w_out": rs["b_w_out"][None],
        "ffn_w_up": jnp.stack([rs["up0"], rs["up1"]]),
        "ffn_conv_w": jnp.stack([lax.dynamic_slice_in_dim(ct_l[:CONV_W], chip * FFN_COLS, FFN_COLS, axis=1) for ct_l in conv_tot]),
        "ffn_conv_b": jnp.stack([ct_l[CONV_W] for ct_l in conv_tot]),
        "ffn_w_down": jnp.stack([rs["down0"], rs["down1"]]),
    }
    weights = dict(ada_w=ada_w, ada_b=ada_b, a_w_in=a_w_in, a_lb_logits=a_lb_logits, a_norm_g=a_norm_g, a_w_out=a_w_out,
                   kv_ada_w=kv_ada_w, kv_ada_b=kv_ada_b, kv_w=kv_w, kv_b_f=kv_b_f, k_norm_g=k_norm_g, b_w_q=b_w_q,
                   q_norm_g=q_norm_g, b_w_out=b_w_out, ffn_w_up=ffn_w_up, ffn_conv_w=ffn_conv_w, ffn_conv_b=ffn_conv_b,
                   ffn_w_down=ffn_w_down)
    m_in = dict(ada_w=m_ada_w, ada_b=m_ada_b, a_w_in=m_a_w_in, a_lb_logits=m_a_lb_logits, a_norm_g=m_a_norm_g,
                a_w_out=m_a_w_out, kv_ada_w=m_kv_ada_w, kv_ada_b=m_kv_ada_b, kv_w=m_kv_w, kv_b_f=m_kv_b_f,
                k_norm_g=m_k_norm_g, b_w_q=m_b_w_q, q_norm_g=m_q_norm_g, b_w_out=m_b_w_out, ffn_w_up=m_ffn_w_up,
                ffn_conv_w=m_ffn_conv_w, ffn_conv_b=m_ffn_conv_b, ffn_w_down=m_ffn_w_down)
    v_in = dict(ada_w=v_ada_w, ada_b=v_ada_b, a_w_in=v_a_w_in, a_lb_logits=v_a_lb_logits, a_norm_g=v_a_norm_g,
                a_w_out=v_a_w_out, kv_ada_w=v_kv_ada_w, kv_ada_b=v_kv_ada_b, kv_w=v_kv_w, kv_b_f=v_kv_b_f,
                k_norm_g=v_k_norm_g, b_w_q=v_b_w_q, q_norm_g=v_q_norm_g, b_w_out=v_b_w_out, ffn_w_up=v_ffn_w_up,
                ffn_conv_w=v_ffn_conv_w, ffn_conv_b=v_ffn_conv_b, ffn_w_down=v_ffn_w_down)

    names = list(weights)
    step = lambda n: _adamw(weights[n], grads[n], m_in[n], v_in[n], "adamw_" + n)
    grads = {n: g.reshape(weights[n].shape) for n, g in grads.items()}
    upd = {n: step(n) for n in names if n != "a_w_in"}
    grads["a_w_in"] = reduce_group(3, ("a_w_in",), [u[0] for u in upd.values()])["a_w_in"][None]
    upd["a_w_in"] = step("a_w_in")
    return (loss, grad_x[None], *[grads[n] for n in names], *[upd[n][0] for n in names],
            *[upd[n][1] for n in names], *[upd[n][2] for n in names])
```

```python
import jax
import jax.numpy as jnp
from jax import lax
from jax.experimental import pallas as pl
from jax.experimental.pallas import tpu as pltpu
from jax.experimental.pallas import tpu_sc as plsc

F32 = jnp.float32
BF16 = jnp.bfloat16

D_MODEL = 1024
HEADS = 8
HEAD_DIM = 128
A_CHUNK = 64
D_FF = 2816
CONV_W = 3
EPS = 1e-6
NEG_INF = -1e30
N_CHIPS = 4
N_DEV = 8

ADAM_LR = 0.001
ADAM_B1 = 0.9
ADAM_B2 = 0.999
ADAM_EPS = 1e-08
ADAM_WD = 0.01
ADAM_STEP = 10

SUBLANES = 8
BF16_ROWS = 16
LANES = 128
HALO = BF16_ROWS
ROW_TILE = 512
TOKEN_TILE_TN = 2048
FFN_COLS = 1408
HGRN_ROWS = 256
ATT_TILE = 512
ATT_SPLIT = 2
MESH = pl.DeviceIdType.MESH


def _sig(x):
    return jax.nn.sigmoid(x)


def _dot(a, b):
    return jnp.dot(a, b, preferred_element_type=F32)


def _dot_nt(a, b):
    return lax.dot_general(a, b, (((1,), (1,)), ((), ())), preferred_element_type=F32)


def _dot_tn(a, b):
    return lax.dot_general(a, b, (((0,), (0,)), ((), ())), preferred_element_type=F32)


def _split2(x):
    hi = x.astype(BF16)
    lo = (x - hi.astype(F32)).astype(BF16)
    return hi, lo


def _dot_f32(a, b):
    ah, al = _split2(a)
    bh, bl = _split2(b)
    return _dot(ah, bh) + _dot(ah, bl) + _dot(al, bh)


def _tri_dot(tri, x):
    hi = x.astype(BF16)
    r = x - hi.astype(F32)
    mid = r.astype(BF16)
    lo = (r - mid.astype(F32)).astype(BF16)
    return _dot(tri, hi) + _dot(tri, mid) + _dot(tri, lo)


def _tri(n, upper=False):
    r = lax.broadcasted_iota(jnp.int32, (n, n), 0)
    c = lax.broadcasted_iota(jnp.int32, (n, n), 1)
    keep = (c >= r) if upper else (c <= r)
    return jnp.where(keep, 1.0, 0.0).astype(BF16)


def _colsum8(v):
    rows, n = v.shape
    return v.reshape(rows // SUBLANES, SUBLANES, n).sum(axis=0)


def _full(shape):
    nd = len(shape)
    return pl.BlockSpec(shape, lambda *_: (0,) * nd)


def _tile(n, want):
    t = min(n, want)
    assert n % t == 0, (n, t)
    return t


def _mm_nn(a, w, groups, out_dtype, name):
    m_rows, k = a.shape
    p_n, _, n = w.shape
    per = p_n // groups
    tm = _tile(m_rows, ROW_TILE)

    def body(a_ref, w_ref, o_ref):
        av = a_ref[...]
        for p in range(p_n):
            o_ref[p // per, :, (p % per) * n:(p % per + 1) * n] = _dot(av, w_ref[p]).astype(out_dtype)

    return pl.pallas_call(
        body, name=name, grid=(m_rows // tm,),
        in_specs=[pl.BlockSpec((tm, k), lambda i: (i, 0)), _full((p_n, k, n))],
        out_specs=pl.BlockSpec((groups, tm, per * n), lambda i: (0, i, 0)),
        out_shape=jax.ShapeDtypeStruct((groups, m_rows, per * n), out_dtype),
    )(a, w)


def _mm_nt(d, w, out_dtype, name, add=None):
    g_n, m_rows, _ = d.shape
    p_n, k, n = w.shape
    per = p_n // g_n
    tm = _tile(m_rows, ROW_TILE)

    def body(*refs):
        d_ref, w_ref = refs[0], refs[1]
        o_ref = refs[-1]
        acc = refs[2][...] if add is not None else None
        for p in range(p_n):
            t = _dot_nt(d_ref[p // per, :, (p % per) * n:(p % per + 1) * n], w_ref[p])
            acc = t if acc is None else acc + t
        o_ref[...] = acc.astype(out_dtype)

    ins = [d, w] + ([add] if add is not None else [])
    specs = [pl.BlockSpec((g_n, tm, per * n), lambda i: (0, i, 0)), _full((p_n, k, n))]
    if add is not None:
        specs.append(pl.BlockSpec((tm, k), lambda i: (i, 0)))
    return pl.pallas_call(
        body, name=name, grid=(m_rows // tm,), in_specs=specs,
        out_specs=pl.BlockSpec((tm, k), lambda i: (i, 0)),
        out_shape=jax.ShapeDtypeStruct((m_rows, k), out_dtype),
    )(*ins)


def _mm_tn(a, d, p_n, name):
    m_rows, k = a.shape
    g_n, _, w_cols = d.shape
    per = p_n // g_n
    n = w_cols // per
    tm = _tile(m_rows, TOKEN_TILE_TN if k <= D_MODEL else ROW_TILE)
    steps = m_rows // tm

    def body(a_ref, d_ref, o_ref, acc):
        m = pl.program_id(1)

        @pl.when(m == 0)
        def _():
            acc[...] = jnp.zeros_like(acc)

        acc[...] += _dot_tn(a_ref[...], d_ref[...])

        @pl.when(m == steps - 1)
        def _():
            o_ref[...] = acc[...].astype(BF16)

    return pl.pallas_call(
        body, name=name, grid=(p_n, steps),
        in_specs=[pl.BlockSpec((tm, k), lambda p, m: (m, 0)),
                  pl.BlockSpec((None, tm, n), lambda p, m: (p // per, m, p % per))],
        out_specs=pl.BlockSpec((None, k, n), lambda p, m: (p, 0, 0)),
        out_shape=jax.ShapeDtypeStruct((p_n, k, n), BF16),
        scratch_shapes=[pltpu.VMEM((k, n), F32)],
    )(a, d)


def _premix(x, shift, scale, name, branch=None, gate=None):
    s, dm = x.shape
    tm = _tile(s, ROW_TILE)
    with_branch = branch is not None

    def body(*refs):
        x_ref, sh_ref, sc_ref = refs[:3]
        xv = x_ref[...]
        if with_branch:
            xv = xv + refs[4][...] * refs[3][...]
            refs[-2][...] = xv
        inv = lax.rsqrt(jnp.mean(xv * xv, axis=-1, keepdims=True) + EPS)
        refs[-1][...] = (xv * inv * (1.0 + sc_ref[...]) + sh_ref[...]).astype(BF16)

    row = pl.BlockSpec((tm, dm), lambda i: (i, 0))
    vec = _full((1, dm))
    ins, specs = [x, shift, scale], [row, vec, vec]
    out_shape, out_specs = [jax.ShapeDtypeStruct((s, dm), BF16)], [row]
    if with_branch:
        ins += [branch, gate]
        specs += [row, vec]
        out_shape.insert(0, jax.ShapeDtypeStruct((s, dm), F32))
        out_specs.insert(0, row)
    outs = pl.pallas_call(body, name=name, grid=(s // tm,), in_specs=specs, out_specs=out_specs,
                          out_shape=out_shape)(*ins)
    return tuple(outs) if with_branch else outs[0]


def _premix_bwd(x, dh, scale, dres, name):
    s, dm = x.shape
    tm = _tile(s, ROW_TILE)

    def body(x_ref, dh_ref, sc_ref, dres_ref, dx_ref, dsh_ref, dsc_ref):
        i = pl.program_id(0)

        @pl.when(i == 0)
        def _():
            dsh_ref[...] = jnp.zeros_like(dsh_ref)
            dsc_ref[...] = jnp.zeros_like(dsc_ref)

        xv = x_ref[...]
        dhv = dh_ref[...]
        inv = lax.rsqrt(jnp.mean(xv * xv, axis=-1, keepdims=True) + EPS)
        r = xv * inv
        dr = dhv * (1.0 + sc_ref[...])
        dx_ref[...] = dres_ref[...] + inv * (dr - r * jnp.mean(dr * r, axis=-1, keepdims=True))
        dsh_ref[...] += _colsum8(dhv)
        dsc_ref[...] += _colsum8(dhv * r)

    row = pl.BlockSpec((tm, dm), lambda i: (i, 0))
    acc = _full((SUBLANES, dm))
    return pl.pallas_call(
        body, name=name, grid=(s // tm,), in_specs=[row, row, _full((1, dm)), row],
        out_specs=[row, acc, acc],
        out_shape=[jax.ShapeDtypeStruct((s, dm), F32), jax.ShapeDtypeStruct((SUBLANES, dm), F32),
                   jax.ShapeDtypeStruct((SUBLANES, dm), F32)],
    )(x, dh, scale, dres)


def _branch_bwd(dx, y, gate, name):
    s, dm = dx.shape
    tm = _tile(s, ROW_TILE)

    def body(dx_ref, y_ref, g_ref, dy_ref, dg_ref):
        @pl.when(pl.program_id(0) == 0)
        def _():
            dg_ref[...] = jnp.zeros_like(dg_ref)

        dxv = dx_ref[...]
        dy_ref[0] = (dxv * g_ref[...]).astype(BF16)
        dg_ref[...] += _colsum8(dxv * y_ref[...])

    row = pl.BlockSpec((tm, dm), lambda i: (i, 0))
    return pl.pallas_call(
        body, name=name, grid=(s // tm,), in_specs=[row, row, _full((1, dm))],
        out_specs=[pl.BlockSpec((1, tm, dm), lambda i: (0, i, 0)), _full((SUBLANES, dm))],
        out_shape=[jax.ShapeDtypeStruct((1, s, dm), BF16), jax.ShapeDtypeStruct((SUBLANES, dm), F32)],
    )(dx, y, gate)


def _loss_head(x, branch, gate, target, name):
    s, dm = x.shape
    tm = _tile(s, ROW_TILE)

    def body(x_ref, b_ref, g_ref, t_ref, sq_ref, dy_ref):
        @pl.when(pl.program_id(0) == 0)
        def _():
            sq_ref[...] = jnp.zeros_like(sq_ref)

        err = x_ref[...] + g_ref[...] * b_ref[...] - t_ref[...]
        sq_ref[...] += _colsum8(err * err)
        dy_ref[...] = err * (1.0 / dm)

    row = pl.BlockSpec((tm, dm), lambda i: (i, 0))
    return pl.pallas_call(
        body, name=name, grid=(s // tm,), in_specs=[row, row, _full((1, dm)), row],
        out_specs=[_full((SUBLANES, dm)), row],
        out_shape=[jax.ShapeDtypeStruct((SUBLANES, dm), F32), jax.ShapeDtypeStruct((s, dm), F32)],
    )(x, branch, gate, target)


def _conv_taps(e, w, b):
    return w[2:3] * e + w[1:2] * pltpu.roll(e, 1, 0) + w[0:1] * pltpu.roll(e, 2, 0) + b


def _ffn_specs(s, tm, cb):
    hb = tm // HALO
    last = s // HALO - 1
    main = pl.BlockSpec((2, tm, cb), lambda j, i: (0, i, j))
    prev = pl.BlockSpec((2, HALO, cb), lambda j, i: (0, jnp.maximum(i * hb - 1, 0), j))
    nxt = pl.BlockSpec((2, HALO, cb), lambda j, i: (0, jnp.minimum((i + 1) * hb, last), j))
    wspec = pl.BlockSpec((2, CONV_W, cb), lambda j, i: (0, 0, j))
    bspec = pl.BlockSpec((2, 1, cb), lambda j, i: (0, 0, j))
    return main, prev, nxt, wspec, bspec


def _convglu_fwd(u, w, b, name):
    _, s, f = u.shape
    tm = _tile(s, 256)
    cb = _tile(f, FFN_COLS)
    main, prev, _, wspec, bspec = _ffn_specs(s, tm, cb)

    def body(u_ref, up_ref, w_ref, b_ref, a_ref):
        first = jnp.where(pl.program_id(1) > 0, 1.0, 0.0)

        def conv(g):
            e = jnp.concatenate([up_ref[g].astype(F32) * first, u_ref[g].astype(F32)], axis=0)
            return _conv_taps(e, w_ref[g], b_ref[g])[HALO:]

        gate = conv(0)
        a_ref[...] = (gate * _sig(gate) * conv(1)).astype(BF16)

    return pl.pallas_call(
        body, name=name, grid=(f // cb, s // tm), in_specs=[main, prev, wspec, bspec],
        out_specs=pl.BlockSpec((tm, cb), lambda j, i: (i, j)),
        out_shape=jax.ShapeDtypeStruct((s, f), BF16),
    )(u, u, w, b)


def _convglu_bwd(u, da, w, b, name):
    _, s, f = u.shape
    tm = _tile(s, 256)
    cb = _tile(f, FFN_COLS)
    steps = s // tm
    n_ext = tm + 2 * HALO
    main, prev, nxt, wspec, bspec = _ffn_specs(s, tm, cb)
    hb = tm // HALO
    last = s // HALO - 1
    da_main = pl.BlockSpec((tm, cb), lambda j, i: (i, j))
    da_next = pl.BlockSpec((HALO, cb), lambda j, i: (jnp.minimum((i + 1) * hb, last), j))

    def body(u_ref, up_ref, un_ref, da_ref, dan_ref, w_ref, b_ref, du_ref, acc_ref):
        i = pl.program_id(1)
        first = jnp.where(i > 0, 1.0, 0.0)
        notlast = jnp.where(i < steps - 1, 1.0, 0.0)

        @pl.when(i == 0)
        def _():
            acc_ref[...] = jnp.zeros_like(acc_ref)

        def ext(g):
            return jnp.concatenate([up_ref[g].astype(F32) * first, u_ref[g].astype(F32), un_ref[g].astype(F32)], axis=0)

        ug, uv = ext(0), ext(1)
        gate = _conv_taps(ug, w_ref[0], b_ref[0])
        val = _conv_taps(uv, w_ref[1], b_ref[1])
        da_e = jnp.concatenate([jnp.zeros((HALO, cb), F32), da_ref[...].astype(F32),
                                dan_ref[...].astype(F32) * notlast], axis=0)
        sg = _sig(gate)
        d_val = da_e * gate * sg
        d_gate = da_e * val * (sg * (1.0 + gate * (1.0 - sg)))

        def finish(g, d, e):
            wv = w_ref[g]
            rows = slice(HALO, HALO + tm)
            d1, d2 = pltpu.roll(d, n_ext - 1, 0), pltpu.roll(d, n_ext - 2, 0)
            du_ref[g] = (wv[2:3] * d + wv[1:2] * d1 + wv[0:1] * d2)[rows].astype(BF16)
            em = e[rows]
            acc_ref[g, 2] += _colsum8(d[rows] * em)
            acc_ref[g, 1] += _colsum8(d1[rows] * em)
            acc_ref[g, 0] += _colsum8(d2[rows] * em)
            acc_ref[g, 3] += _colsum8(d[rows])

        finish(0, d_gate, ug)
        finish(1, d_val, uv)

    return pl.pallas_call(
        body, name=name, grid=(f // cb, steps),
        in_specs=[main, prev, nxt, da_main, da_next, wspec, bspec],
        out_specs=[main, pl.BlockSpec((2, 4, SUBLANES, cb), lambda j, i: (0, 0, 0, j))],
        out_shape=[jax.ShapeDtypeStruct((2, s, f), BF16), jax.ShapeDtypeStruct((2, 4, SUBLANES, f), F32)],
    )(u, u, u, da, da, w, b)


def _hgrn_gates(q_raw, f_raw, lb, tri):
    sf = _sig(f_raw)
    fg = lb + (1.0 - lb) * sf
    b = _tri_dot(tri, jnp.log(fg))
    return q_raw * _sig(q_raw), 1.0 - fg, b, fg, sf


def _hgrn_fwd(proj, lb, norm_g, name):
    s = proj.shape[0]
    tb = _tile(s, HGRN_ROWS)
    n_c = tb // A_CHUNK
    half = A_CHUNK // 2

    def body(q_ref, f_ref, v_ref, g_ref, lb_ref, ng_ref, o_ref, yp_ref, st_ref, state):
        @pl.when(pl.program_id(0) == 0)
        def _():
            state[...] = jnp.zeros_like(state)

        tri = _tri(A_CHUNK)
        causal = lax.broadcasted_iota(jnp.int32, (A_CHUNK, A_CHUNK), 1) <= lax.broadcasted_iota(
            jnp.int32, (A_CHUNK, A_CHUNK), 0)

        def chunk(ci, carry):
            rows = pl.ds(pl.multiple_of(ci * A_CHUNK, A_CHUNK), A_CHUNK)
            for h in range(HEADS):
                cs = slice(h * HEAD_DIM, (h + 1) * HEAD_DIM)
                qs, k, b, _, _ = _hgrn_gates(q_ref[rows, cs], f_ref[rows, cs], lb_ref[:, cs], tri)
                b_mid, b_last = b[half:half + 1], b[A_CHUNK - 1:A_CHUNK]
                vb = v_ref[rows, cs].astype(BF16)
                scores = _dot_nt((qs * jnp.exp(b - b_mid)).astype(BF16), (k * jnp.exp(b_mid - b)).astype(BF16))
                scores = jnp.where(causal, scores, 0.0)
                st = state[h]
                st_ref[ci, h] = st
                o = _dot(scores.astype(BF16), vb) + _dot_nt((qs * jnp.exp(b)).astype(BF16), st.astype(BF16))
                state[h] = st * jnp.exp(b_last) + _dot_tn(vb, (k * jnp.exp(b_last - b)).astype(BF16))
                o_ref[rows, cs] = o
                inv = lax.rsqrt(jnp.mean(o * o, axis=-1, keepdims=True) + EPS)
                g_raw = g_ref[rows, cs]
                yp_ref[rows, cs] = (o * inv * ng_ref[:, cs] * (g_raw * _sig(g_raw))).astype(BF16)
            return carry

        lax.fori_loop(0, n_c, chunk, 0)

    col = lambda j: pl.BlockSpec((tb, D_MODEL), lambda i: (i, j))
    vec = _full((1, D_MODEL))
    return pl.pallas_call(
        body, name=name, grid=(s // tb,), in_specs=[col(0), col(1), col(2), col(3), vec, vec],
        out_specs=[col(0), col(0), pl.BlockSpec((n_c, HEADS, HEAD_DIM, HEAD_DIM), lambda i: (i, 0, 0, 0))],
        out_shape=[jax.ShapeDtypeStruct((s, D_MODEL), F32), jax.ShapeDtypeStruct((s, D_MODEL), BF16),
                   jax.ShapeDtypeStruct((s // A_CHUNK, HEADS, HEAD_DIM, HEAD_DIM), F32)],
        scratch_shapes=[pltpu.VMEM((HEADS, HEAD_DIM, HEAD_DIM), F32)],
    )(proj, proj, proj, proj, lb, norm_g)


def _hgrn_bwd(proj, lb, norm_g, o, states, dyp, name):
    s = proj.shape[0]
    tb = _tile(s, HGRN_ROWS)
    n_c = tb // A_CHUNK
    n_b = s // tb
    half = A_CHUNK // 2

    def body(q_ref, f_ref, v_ref, g_ref, lb_ref, ng_ref, o_ref, st_ref, dyp_ref, dp_ref, dlb_ref, dng_ref, dstate):
        @pl.when(pl.program_id(0) == 0)
        def _():
            dstate[...] = jnp.zeros_like(dstate)
            dlb_ref[...] = jnp.zeros_like(dlb_ref)
            dng_ref[...] = jnp.zeros_like(dng_ref)

        tri = _tri(A_CHUNK)
        tri_up = _tri(A_CHUNK, upper=True)
        row_id = lax.broadcasted_iota(jnp.int32, (A_CHUNK, HEAD_DIM), 0)
        causal = lax.broadcasted_iota(jnp.int32, (A_CHUNK, A_CHUNK), 1) <= lax.broadcasted_iota(
            jnp.int32, (A_CHUNK, A_CHUNK), 0)

        def chunk(cj, carry):
            ci = n_c - 1 - cj
            rows = pl.ds(pl.multiple_of(ci * A_CHUNK, A_CHUNK), A_CHUNK)
            for h in range(HEADS):
                cs = slice(h * HEAD_DIM, (h + 1) * HEAD_DIM)
                q_raw, lbh = q_ref[rows, cs], lb_ref[:, cs]
                qs, k, b, fg, sf = _hgrn_gates(q_raw, f_ref[rows, cs], lbh, tri)
                b_mid, b_last = b[half:half + 1], b[A_CHUNK - 1:A_CHUNK]
                e_qi, e_ki, e_q, e_ks = jnp.exp(b - b_mid), jnp.exp(b_mid - b), jnp.exp(b), jnp.exp(b_last - b)
                q_i, k_i, q_e, k_s = qs * e_qi, k * e_ki, qs * e_q, k * e_ks
                vb = v_ref[rows, cs].astype(BF16)
                scores = jnp.where(causal, _dot_nt(q_i.astype(BF16), k_i.astype(BF16)), 0.0)
                ov, g_raw, dy, ng = o_ref[rows, cs], g_ref[rows, cs], dyp_ref[rows, cs], ng_ref[:, cs]
                inv = lax.rsqrt(jnp.mean(ov * ov, axis=-1, keepdims=True) + EPS)
                nrm = ov * inv
                sg = _sig(g_raw)
                gs = g_raw * sg
                dn = dy * ng * gs
                dng_ref[0:1, cs] += jnp.sum(dy * nrm * gs, axis=0, keepdims=True)
                dg_raw = dy * nrm * ng * (sg * (1.0 + g_raw * (1.0 - sg)))
                do = (inv * (dn - nrm * jnp.mean(dn * nrm, axis=-1, keepdims=True))).astype(BF16)
                st_prev = st_ref[ci, h]
                dst = dstate[h]
                dstb = dst.astype(BF16)
                d_scores = jnp.where(causal, _dot_nt(do, vb), 0.0).astype(BF16)
                dv = _dot_tn(scores.astype(BF16), do) + _dot_nt(k_s.astype(BF16), dstb)
                dq_i = _dot(d_scores, k_i.astype(BF16))
                dk_i = _dot_tn(d_scores, q_i.astype(BF16))
                dq_e = _dot(do, st_prev.astype(BF16))
                dk_s = _dot(vb, dstb)
                d_decay = jnp.sum(st_prev * dst, axis=0, keepdims=True)
                dstate[h] = dst * jnp.exp(b_last) + _dot_tn(do, q_e.astype(BF16))
                dq = dq_i * e_qi + dq_e * e_q
                dk = dk_i * e_ki + dk_s * e_ks
                t_qi, t_ki, t_ks = dq_i * q_i, dk_i * k_i, dk_s * k_s
                db = t_qi - t_ki + dq_e * q_e - t_ks
                db_mid = jnp.sum(t_ki - t_qi, axis=0, keepdims=True)
                db_last = jnp.sum(t_ks, axis=0, keepdims=True) + d_decay * jnp.exp(b_last)
                db = db + jnp.where(row_id == half, db_mid, 0.0) + jnp.where(row_id == A_CHUNK - 1, db_last, 0.0)
                dfg = _tri_dot(tri_up, db) / fg - dk
                dlb_ref[0:1, cs] += jnp.sum(dfg * (1.0 - sf), axis=0, keepdims=True)
                sq = _sig(q_raw)
                dp_ref[0, rows, cs] = (dq * (sq * (1.0 + q_raw * (1.0 - sq)))).astype(BF16)
                dp_ref[1, rows, cs] = (dfg * (1.0 - lbh) * sf * (1.0 - sf)).astype(BF16)
                dp_ref[2, rows, cs] = dv.astype(BF16)
                dp_ref[3, rows, cs] = dg_raw.astype(BF16)
            return carry

        lax.fori_loop(0, n_c, chunk, 0)

    col = lambda j: pl.BlockSpec((tb, D_MODEL), lambda i: (n_b - 1 - i, j))
    vec = _full((1, D_MODEL))
    acc = _full((SUBLANES, D_MODEL))
    return pl.pallas_call(
        body, name=name, grid=(n_b,),
        in_specs=[col(0), col(1), col(2), col(3), vec, vec, col(0),
                  pl.BlockSpec((n_c, HEADS, HEAD_DIM, HEAD_DIM), lambda i: (n_b - 1 - i, 0, 0, 0)), col(0)],
        out_specs=[pl.BlockSpec((4, tb, D_MODEL), lambda i: (0, n_b - 1 - i, 0)), acc, acc],
        out_shape=[jax.ShapeDtypeStruct((4, s, D_MODEL), BF16), jax.ShapeDtypeStruct((SUBLANES, D_MODEL), F32),
                   jax.ShapeDtypeStruct((SUBLANES, D_MODEL), F32)],
        scratch_shapes=[pltpu.VMEM((HEADS, HEAD_DIM, HEAD_DIM), F32)],
    )(proj, proj, proj, proj, lb, norm_g, o, states, dyp)


def _headnorm(x, g, mult, name, col0=0):
    s = x.shape[0]
    tm = _tile(s, ROW_TILE)

    def body(x_ref, g_ref, y_ref):
        for h in range(HEADS):
            cs = slice(h * HEAD_DIM, (h + 1) * HEAD_DIM)
            xv = x_ref[:, cs]
            inv = lax.rsqrt(jnp.mean(xv * xv, axis=-1, keepdims=True) + EPS)
            y_ref[:, cs] = (xv * inv * g_ref[:, cs] * mult).astype(BF16)

    return pl.pallas_call(
        body, name=name, grid=(s // tm,),
        in_specs=[pl.BlockSpec((tm, D_MODEL), lambda i: (i, col0)), _full((1, D_MODEL))],
        out_specs=pl.BlockSpec((tm, D_MODEL), lambda i: (i, 0)),
        out_shape=jax.ShapeDtypeStruct((s, D_MODEL), BF16),
    )(x, g)


def _headnorm_bwd(x, g, mult, dy, name, col0=0, extra=None):
    s = x.shape[0]
    tm = _tile(s, ROW_TILE)
    groups = 2 if extra is not None else 1
    head_major = dy.ndim == 3

    def body(*refs):
        x_ref, g_ref, dy_ref = refs[:3]
        dx_ref, dg_ref = refs[-2:]

        @pl.when(pl.program_id(0) == 0)
        def _():
            dg_ref[...] = jnp.zeros_like(dg_ref)

        for h in range(HEADS):
            cs = slice(h * HEAD_DIM, (h + 1) * HEAD_DIM)
            xv, gv = x_ref[:, cs], g_ref[:, cs]
            dyv = dy_ref[h, :, 0:HEAD_DIM] if head_major else dy_ref[:, cs]
            inv = lax.rsqrt(jnp.mean(xv * xv, axis=-1, keepdims=True) + EPS)
            nrm = xv * inv
            dn = dyv * gv * mult
            dg_ref[:, cs] += _colsum8(dyv * nrm * mult)
            dx_ref[0, :, cs] = (inv * (dn - nrm * jnp.mean(dn * nrm, axis=-1, keepdims=True))).astype(BF16)
        if extra is not None:
            dx_ref[1] = refs[3][...]

    row = pl.BlockSpec((tm, D_MODEL), lambda i: (i, 0))
    dy_spec = pl.BlockSpec((HEADS, tm, dy.shape[-1]), lambda i: (0, i, 0)) if head_major else row
    ins = [x, g, dy] + ([extra] if extra is not None else [])
    specs = ([pl.BlockSpec((tm, D_MODEL), lambda i: (i, col0)), _full((1, D_MODEL)), dy_spec]
             + ([row] if extra is not None else []))
    return pl.pallas_call(
        body, name=name, grid=(s // tm,), in_specs=specs,
        out_specs=[pl.BlockSpec((groups, tm, D_MODEL), lambda i: (0, i, 0)), _full((SUBLANES, D_MODEL))],
        out_shape=[jax.ShapeDtypeStruct((groups, s, D_MODEL), BF16), jax.ShapeDtypeStruct((SUBLANES, D_MODEL), F32)],
    )(*ins)


def _log_sigmoid(z):
    return jnp.minimum(z, 0.0) - jnp.log(1.0 + jnp.exp(-jnp.abs(z)))


Q_CUM, Q_ONE, Q_LSE = 0, 3, 6
LOG2E = 1.4426950408889634


def _pieces(v):
    hi = v.astype(BF16).astype(F32)
    mid = (v - hi).astype(BF16).astype(F32)
    lo = ((v - hi) - mid).astype(BF16).astype(F32)
    return hi, mid, lo


def _side(lane, at, v):
    hi, mid, lo = _pieces(v)
    return jnp.where(lane == at, hi, jnp.where(lane == at + 1, mid, jnp.where(lane == at + 2, lo, 0.0)))


def _fcum_fwd(f, bias, name):
    s = f.shape[0]
    tm = _tile(s, ROW_TILE)

    def body(f_ref, b_ref, qa_ref, ka_ref, carry):
        @pl.when(pl.program_id(0) == 0)
        def _():
            carry[...] = jnp.zeros_like(carry)

        cum = _tri_dot(_tri(tm), _log_sigmoid(f_ref[...] + b_ref[...])) + carry[...]
        carry[...] = cum[tm - 1:tm]
        lane = lax.broadcasted_iota(jnp.int32, (tm, LANES), 1)
        ones_q = jnp.where((lane >= Q_ONE) & (lane < Q_LSE), 1.0, 0.0)
        ones_k = jnp.where((lane < Q_ONE) | ((lane >= Q_LSE) & (lane < Q_LSE + 3)), 1.0, 0.0)
        for h in range(HEADS):
            c2 = cum[:, h:h + 1] * LOG2E
            qa_ref[h] = (_side(lane, Q_CUM, c2) + ones_q).astype(BF16)
            ka_ref[h] = (_side(lane, Q_ONE, -c2) + ones_k).astype(BF16)

    side = pl.BlockSpec((HEADS, tm, LANES), lambda i: (0, i, 0))
    return pl.pallas_call(
        body, name=name, grid=(s // tm,),
        in_specs=[pl.BlockSpec((tm, LANES), lambda i: (i, 0)), _full((1, LANES))],
        out_specs=[side, side],
        out_shape=[jax.ShapeDtypeStruct((HEADS, s, LANES), BF16)] * 2,
        scratch_shapes=[pltpu.VMEM((1, LANES), F32)],
    )(f, bias)


def _fcum_bwd(f, bias, dka, dq, name):
    s = f.shape[0]
    tm = _tile(s, ROW_TILE)
    n_b = s // tm
    q_lane = HEAD_DIM + Q_CUM

    def body(f_ref, b_ref, dka_ref, dqa_ref, dz_ref, db_ref, carry):
        @pl.when(pl.program_id(0) == 0)
        def _():
            carry[...] = jnp.zeros_like(carry)
            db_ref[...] = jnp.zeros_like(db_ref)

        lane = lax.broadcasted_iota(jnp.int32, (tm, LANES), 1)
        dcum = jnp.zeros((tm, LANES), F32)
        for h in range(HEADS):
            dcum = dcum + jnp.where(lane == h, dqa_ref[h, :, q_lane:q_lane + 1] - dka_ref[h, :, Q_ONE:Q_ONE + 1], 0.0)
        dlf = _tri_dot(_tri(tm, upper=True), dcum) + carry[...]
        carry[...] = dlf[0:1]
        dz = dlf * _sig(-(f_ref[...] + b_ref[...]))
        dz_ref[0] = dz.astype(BF16)
        db_ref[...] += _colsum8(dz)

    return pl.pallas_call(
        body, name=name, grid=(n_b,),
        in_specs=[pl.BlockSpec((tm, LANES), lambda i: (n_b - 1 - i, 0)), _full((1, LANES)),
                  pl.BlockSpec((HEADS, tm, LANES), lambda i: (0, n_b - 1 - i, 0)),
                  pl.BlockSpec((HEADS, tm, 2 * HEAD_DIM), lambda i: (0, n_b - 1 - i, 0))],
        out_specs=[pl.BlockSpec((1, tm, LANES), lambda i: (0, n_b - 1 - i, 0)), _full((SUBLANES, LANES))],
        out_shape=[jax.ShapeDtypeStruct((1, s, LANES), BF16), jax.ShapeDtypeStruct((SUBLANES, LANES), F32)],
        scratch_shapes=[pltpu.VMEM((1, LANES), F32)],
    )(f, bias, dka, dq)


def _causal_pairs(n_t, key_major):
    if key_major:
        pairs = [(qi, ki) for ki in range(n_t) for qi in range(ki, n_t)]
    else:
        pairs = [(qi, ki) for qi in range(n_t) for ki in range(qi + 1)]
    return (jnp.array([p[0] for p in pairs], jnp.int32), jnp.array([p[1] for p in pairs], jnp.int32))


def _with_side(main_ref, side_ref):
    return jnp.concatenate([main_ref[...], side_ref[...]], axis=1)


def _lane_const(t, lo, hi, value):
    lane = lax.broadcasted_iota(jnp.int32, (t, LANES), 1)
    return jnp.where((lane >= lo) & (lane < hi), value, 0.0).astype(BF16)


def _att_specs(t):
    qmain = pl.BlockSpec((t, HEAD_DIM), lambda h, p, qt, kt: (qt[p], h))
    kmain = pl.BlockSpec((t, HEAD_DIM), lambda h, p, qt, kt: (kt[p], h))
    qside = pl.BlockSpec((None, t, LANES), lambda h, p, qt, kt: (h, qt[p], 0))
    kside = pl.BlockSpec((None, t, LANES), lambda h, p, qt, kt: (h, kt[p], 0))
    return qmain, kmain, qside, kside


def _fox_fwd(q, qa, k, ka, v, qo, name):
    s = q.shape[0]
    t = _tile(s, ATT_TILE)
    sub = t // ATT_SPLIT
    qt, kt = _causal_pairs(s // t, key_major=False)

    def body(qt_ref, kt_ref, q_ref, qa_ref, k_ref, ka_ref, v_ref, og_ref, o_ref, y_ref, qab_ref, m_s, l_s, acc_s):
        pid = pl.program_id(1)
        qi, ki = qt_ref[pid], kt_ref[pid]

        @pl.when(ki == 0)
        def _():
            m_s[...] = jnp.full_like(m_s, NEG_INF)
            l_s[...] = jnp.zeros_like(l_s)
            acc_s[...] = jnp.zeros_like(acc_s)

        def step(diagonal):
            kc = _with_side(k_ref, ka_ref)
            vc = jnp.concatenate([v_ref[...], _lane_const(t, 0, 1, 1.0)], axis=1)
            for r in range(ATT_SPLIT):
                rows = slice(r * sub, (r + 1) * sub)
                n_k = (r + 1) * sub if diagonal else t
                sc = _dot_nt(jnp.concatenate([q_ref[rows], qa_ref[rows]], axis=1), kc[:n_k])
                if diagonal:
                    sc = jnp.where(lax.broadcasted_iota(jnp.int32, (sub, n_k), 1)
                                   <= lax.broadcasted_iota(jnp.int32, (sub, n_k), 0) + r * sub, sc, NEG_INF)
                m_old = m_s[rows]
                m_new = jnp.maximum(m_old, jnp.max(sc, axis=-1, keepdims=True))
                alpha = jnp.exp2(m_old - m_new)
                pv = _dot(jnp.exp2(sc - m_new[:, 0:1]).astype(BF16), vc[:n_k])
                acc_s[rows] = alpha * acc_s[rows] + pv[:, :HEAD_DIM]
                l_s[rows] = alpha * l_s[rows] + pv[:, HEAD_DIM:]
                m_s[rows] = m_new

        @pl.when(ki < qi)
        def _():
            step(False)

        @pl.when(ki == qi)
        def _():
            step(True)
            l = l_s[:, 0:1]
            o = acc_s[...] / l
            o_ref[...] = o
            y_ref[...] = (o * _sig(og_ref[...])).astype(BF16)
            lane = lax.broadcasted_iota(jnp.int32, (t, LANES), 1)
            qab_ref[...] = qa_ref[...] + _side(lane, Q_LSE, -(m_s[:, 0:1] + jnp.log2(l))).astype(BF16)

    qmain, kmain, qside, kside = _att_specs(t)
    return pl.pallas_call(
        body, name=name,
        grid_spec=pltpu.PrefetchScalarGridSpec(
            num_scalar_prefetch=2, grid=(HEADS, qt.shape[0]),
            in_specs=[qmain, qside, kmain, kside, kmain,
                      pl.BlockSpec((t, HEAD_DIM), lambda h, p, qt, kt: (qt[p], HEADS + h))],
            out_specs=[qmain, qmain, qside],
            scratch_shapes=[pltpu.VMEM((t, LANES), F32), pltpu.VMEM((t, LANES), F32), pltpu.VMEM((t, HEAD_DIM), F32)]),
        out_shape=[jax.ShapeDtypeStruct((s, D_MODEL), F32), jax.ShapeDtypeStruct((s, D_MODEL), BF16),
                   jax.ShapeDtypeStruct((HEADS, s, LANES), BF16)],
    )(qt, kt, q, qa, k, ka, v, qo)


def _fox_gate_bwd(o, qo, dy, name):
    s = o.shape[0]
    tm = _tile(s, ROW_TILE)

    def body(o_ref, og_ref, dy_ref, do_ref, dg_ref, dl_ref):
        ov, dyv = o_ref[...], dy_ref[...]
        sg = _sig(og_ref[...])
        do = (dyv * sg).astype(BF16)
        do_ref[...] = do
        dg_ref[...] = (dyv * ov * sg * (1.0 - sg)).astype(BF16)
        prod = do.astype(F32) * ov
        lane = lax.broadcasted_iota(jnp.int32, (tm, LANES), 1)
        for h in range(HEADS):
            delta = jnp.sum(prod[:, h * HEAD_DIM:(h + 1) * HEAD_DIM], axis=-1, keepdims=True)
            dl_ref[h] = _side(lane, 0, delta).astype(BF16)

    row = pl.BlockSpec((tm, D_MODEL), lambda i: (i, 0))
    return pl.pallas_call(
        body, name=name, grid=(s // tm,),
        in_specs=[row, pl.BlockSpec((tm, D_MODEL), lambda i: (i, 1)), row],
        out_specs=[row, row, pl.BlockSpec((HEADS, tm, LANES), lambda i: (0, i, 0))],
        out_shape=[jax.ShapeDtypeStruct((s, D_MODEL), BF16), jax.ShapeDtypeStruct((s, D_MODEL), BF16),
                   jax.ShapeDtypeStruct((HEADS, s, LANES), BF16)],
    )(o, qo, dy)


def _fox_bwd(q, qab, k, ka, v, do, doa, name):
    s = q.shape[0]
    t = _tile(s, ATT_TILE)
    n_t = s // t
    sub = t // ATT_SPLIT
    qt, kt = _causal_pairs(n_t, key_major=True)

    def body(qt_ref, kt_ref, q_ref, qab_ref, k_ref, ka_ref, v_ref, do_ref, doa_ref, dk_ref, dv_ref, dka_ref, dq_ref,
             dk_s, dv_s):
        pid = pl.program_id(1)
        qi, ki = qt_ref[pid], kt_ref[pid]

        @pl.when(pid == 0)
        def _():
            dq_ref[...] = jnp.zeros_like(dq_ref)

        @pl.when(qi == ki)
        def _():
            dk_s[...] = jnp.zeros_like(dk_s)
            dv_s[...] = jnp.zeros_like(dv_s)

        def step(diagonal):
            kc = _with_side(k_ref, ka_ref)
            vc = jnp.concatenate([v_ref[...], _lane_const(t, 0, 3, -1.0)], axis=1)
            for r in range(ATT_SPLIT):
                cols = slice(r * sub, (r + 1) * sub)
                n_k = (r + 1) * sub if diagonal else t
                qc = jnp.concatenate([q_ref[cols], qab_ref[cols]], axis=1)
                sc = _dot_nt(kc[:n_k], qc)
                if diagonal:
                    sc = jnp.where(lax.broadcasted_iota(jnp.int32, (n_k, sub), 0)
                                   <= lax.broadcasted_iota(jnp.int32, (n_k, sub), 1) + r * sub, sc, NEG_INF)
                p = jnp.exp2(sc)
                dp = _dot_nt(vc[:n_k], jnp.concatenate([do_ref[cols], doa_ref[cols]], axis=1))
                ds = (p * dp).astype(BF16)
                dv_s[0:n_k] += _dot(p.astype(BF16), do_ref[cols])
                dk_s[0:n_k] += _dot(ds, qc)
                q_rows = pl.ds(pl.multiple_of(qi * t + r * sub, sub), sub)
                dq_ref[q_rows, :] += _dot_tn(ds, kc[:n_k])

        @pl.when(qi > ki)
        def _():
            step(False)

        @pl.when(qi == ki)
        def _():
            step(True)

        @pl.when(qi == n_t - 1)
        def _():
            dk_ref[...] = dk_s[:, :HEAD_DIM] * (1.0 / LOG2E)
            dka_ref[...] = dk_s[:, HEAD_DIM:]
            dv_ref[...] = dv_s[...].astype(BF16)

    qmain, kmain, qside, kside = _att_specs(t)
    return pl.pallas_call(
        body, name=name,
        grid_spec=pltpu.PrefetchScalarGridSpec(
            num_scalar_prefetch=2, grid=(HEADS, qt.shape[0]),
            in_specs=[qmain, qside, kmain, kside, kmain, qmain, qside],
            out_specs=[kmain, pl.BlockSpec((None, t, HEAD_DIM), lambda h, p, qt, kt: (0, kt[p], h)), kside,
                       pl.BlockSpec((None, s, 2 * HEAD_DIM), lambda h, p, qt, kt: (h, 0, 0))],
            scratch_shapes=[pltpu.VMEM((t, 2 * HEAD_DIM), F32), pltpu.VMEM((t, HEAD_DIM), F32)]),
        out_shape=[jax.ShapeDtypeStruct((s, D_MODEL), F32), jax.ShapeDtypeStruct((1, s, D_MODEL), BF16),
                   jax.ShapeDtypeStruct((HEADS, s, LANES), F32), jax.ShapeDtypeStruct((HEADS, s, 2 * HEAD_DIM), F32)],
    )(qt, kt, q, qab, k, ka, v, do, doa)


def _ffn_forward(x_in, branch, gate, shift, scale, w_up, conv_w, conv_b, w_down, tag):
    x_mid, h = _premix(x_in, shift, scale, tag + "_premix", branch=branch, gate=gate)
    u = _mm_nn(h, w_up, 2, BF16, tag + "_up")
    a = _convglu_fwd(u, conv_w, conv_b, tag + "_convglu")
    ffn = _mm_nn(a, w_down, 1, F32, tag + "_down")[0]
    return x_mid, ffn, (h, u, a)


def _weight_grad_first(a, d, p_n, name):
    return lax.optimization_barrier((_mm_tn(a, d, p_n, name), d))


def _ffn_backward(dx_out, x_mid, ffn, gate, scale, saved, w_up, conv_w, conv_b, w_down, tag):
    h, u, a = saved
    dffn, dgate = _branch_bwd(dx_out, ffn, gate, tag + "_gate_bwd")
    dw_down, dffn = _weight_grad_first(a, dffn, 1, tag + "_down_dw")
    da = _mm_nt(dffn, w_down, BF16, tag + "_down_dx")
    du, dconv = _convglu_bwd(u, da, conv_w, conv_b, tag + "_convglu_bwd")
    dw_up, du = _weight_grad_first(h, du, N_CHIPS, tag + "_up_dw")
    dh = _mm_nt(du, w_up, F32, tag + "_up_dx")
    dx_mid, dshift, dscale = _premix_bwd(x_mid, dh, scale, dx_out, tag + "_premix_bwd")
    return dx_mid, dw_up, dw_down, dict(gate=dgate, shift=dshift, scale=dscale, conv=dconv)


def _local_step(x, target, mods, lb, vecs, weights_at):
    m0, m1, mk = mods["l0"], mods["l1"], mods["kv"]
    h0 = _premix(x, m0[0], m0[1], "l0_premix")
    wts, h0 = weights_at("mixer0", h0)
    proj = _mm_nn(h0, wts["a_w_in"], 1, F32, "l0_in")[0]
    o_a, yp, states = _hgrn_fwd(proj, lb, vecs["a_norm_g"], "l0_hgrn")
    more, yp = weights_at("ffn0", yp)
    wts.update(more)
    y0 = _mm_nn(yp, wts["a_w_out"], 1, F32, "l0_out")[0]
    x1, ffn0, saved0 = _ffn_forward(x, y0, m0[2], m0[3], m0[4], wts["up0"], vecs["conv_w0"], vecs["conv_b0"],
                                    wts["down0"], "l0_ffn")
    more, ffn0 = weights_at("layer1", ffn0)
    wts.update(more)
    x2, hk = _premix(x1, mk[0], mk[1], "kv_premix", branch=ffn0, gate=m0[5])
    k_raw = _mm_nn(hk, wts["kv_k"], 1, F32, "kv_k")[0]
    v_sh = _mm_nn(hk, wts["kv_v"], 1, BF16, "kv_v")[0]
    f_raw = _mm_nn(hk, wts["kv_f"], 1, F32, "kv_f")[0]
    k_sh = _headnorm(k_raw, vecs["k_norm_g"], 1.0, "kv_knorm")
    qa, ka = _fcum_fwd(f_raw, vecs["kv_b_f"], "kv_fcum")
    h1 = _premix(x2, m1[0], m1[1], "l1_premix")
    qo = _mm_nn(h1, wts["b_w_q"], 1, F32, "l1_q")[0]
    q_scale = HEAD_DIM ** -0.5
    q = _headnorm(qo, vecs["q_norm_g"], q_scale * LOG2E, "l1_qnorm")
    o_b, og, qab = _fox_fwd(q, qa, k_sh, ka, v_sh, qo, "l1_fox")
    y1 = _mm_nn(og, wts["b_w_out"], 1, F32, "l1_out")[0]
    x3, ffn1, saved1 = _ffn_forward(x2, y1, m1[2], m1[3], m1[4], wts["up1"], vecs["conv_w1"], vecs["conv_b1"],
                                    wts["down1"], "l1_ffn")
    sq, dx4 = _loss_head(x3, ffn1, m1[5], target, "loss_head")

    big, small = {}, {}
    dx3, big["up1"], big["down1"], s_ffn1 = _ffn_backward(dx4, x3, ffn1, m1[5], m1[4], saved1, wts["up1"],
                                                          vecs["conv_w1"], vecs["conv_b1"], wts["down1"], "l1_ffn")
    dy1, dg1_1 = _branch_bwd(dx3, y1, m1[2], "l1_mix_gate_bwd")
    big["b_w_out"], dy1 = _weight_grad_first(og, dy1, 1, "l1_out_dw")
    d_og = _mm_nt(dy1, wts["b_w_out"], F32, "l1_out_dx")
    do_b, dgate_b, doa = _fox_gate_bwd(o_b, qo, d_og, "l1_fox_gate_bwd")
    dk, dv, dka, dq = _fox_bwd(q, qab, k_sh, ka, v_sh, do_b, doa, "l1_fox_bwd")
    dqo, dqg = _headnorm_bwd(qo, vecs["q_norm_g"], q_scale, dq, "l1_qnorm_bwd", extra=dgate_b)
    big["b_w_q"], dqo = _weight_grad_first(h1, dqo, N_CHIPS, "l1_q_dw")
    dh1 = _mm_nt(dqo, wts["b_w_q"], F32, "l1_q_dx")
    dx2, dsh1_1, dsc1_1 = _premix_bwd(x2, dh1, m1[1], dx3, "l1_premix_bwd")
    dk_raw, dkg = _headnorm_bwd(k_raw, vecs["k_norm_g"], 1.0, dk, "kv_knorm_bwd")
    dz, dbf = _fcum_bwd(f_raw, vecs["kv_b_f"], dka, dq, "kv_fcum_bwd")
    big["kv_k"], dk_raw = _weight_grad_first(hk, dk_raw, 1, "kv_k_dw")
    big["kv_v"], dv = _weight_grad_first(hk, dv, 1, "kv_v_dw")
    big["kv_f"], dz = _weight_grad_first(hk, dz, 1, "kv_f_dw")
    dhk = _mm_nt(dk_raw, wts["kv_k"], F32, "kv_k_dx")
    dhk = _mm_nt(dv, wts["kv_v"], F32, "kv_v_dx", add=dhk)
    dhk = _mm_nt(dz, wts["kv_f"], F32, "kv_f_dx", add=dhk)
    dx2, dshk, dsck = _premix_bwd(x2, dhk, mk[1], dx2, "kv_premix_bwd")
    dx1, big["up0"], big["down0"], s_ffn0 = _ffn_backward(dx2, x1, ffn0, m0[5], m0[4], saved0, wts["up0"],
                                                          vecs["conv_w0"], vecs["conv_b0"], wts["down0"], "l0_ffn")
    dy0, dg1_0 = _branch_bwd(dx1, y0, m0[2], "l0_mix_gate_bwd")
    big["a_w_out"], dy0 = _weight_grad_first(yp, dy0, 1, "l0_out_dw")
    dyp = _mm_nt(dy0, wts["a_w_out"], F32, "l0_out_dx")
    dproj, dlb, dng = _hgrn_bwd(proj, lb, vecs["a_norm_g"], o_a, states, dyp, "l0_hgrn_bwd")
    big["a_w_in"], dproj = _weight_grad_first(h0, dproj, N_CHIPS, "l0_in_dw")
    dh0 = _mm_nt(dproj, wts["a_w_in"], F32, "l0_in_dx")
    grad_x, dsh1_0, dsc1_0 = _premix_bwd(x, dh0, m0[1], dx1, "l0_premix_bwd")

    small["mod_l0"] = [dsh1_0, dsc1_0, dg1_0, s_ffn0["shift"], s_ffn0["scale"], s_ffn0["gate"]]
    small["mod_l1"] = [dsh1_1, dsc1_1, dg1_1, s_ffn1["shift"], s_ffn1["scale"], s_ffn1["gate"]]
    small["mod_kv"] = [dshk, dsck]
    small["conv0"], small["conv1"] = s_ffn0["conv"], s_ffn1["conv"]
    small["a_norm_g"], small["k_norm_g"], small["q_norm_g"] = dng, dkg, dqg
    small["kv_b_f"], small["lb"] = dbf, dlb
    marks = {"attention_bwd": dk, "ffn0_bwd": dx1, "mixer0_bwd": grad_x}
    return sq, grad_x, big, small, marks


HBM = pl.BlockSpec(memory_space=pltpu.HBM)
COMM_CHUNK_ELEMS = 256 * 1024


def _place():
    x, y, c = lax.axis_index("x"), lax.axis_index("y"), lax.axis_index("c")
    chips = [(1 - x, y), (x, 1 - y), (1 - x, 1 - y)]
    return x, y, c, (x, y, 1 - c), chips


def _chunk_rows(rows, cols):
    best = BF16_ROWS
    for r in range(BF16_ROWS, rows + 1, BF16_ROWS):
        if rows % r == 0 and r * cols <= COMM_CHUNK_ELEMS:
            best = r
    assert rows % best == 0, (rows, cols)
    return best


def _allgather8(block, name):
    m_per, n = block.shape

    def body(x_ref, out_ref, send_sems, recv_sems, local_sem):
        x, y, c, sibling, chips = _place()
        me = (x, y, c)

        def rows(px, py, pc):
            return out_ref.at[pl.ds((4 * px + 2 * py + pc) * m_per, m_per), :]

        def copy(k, blk, to, src=None):
            return pltpu.make_async_remote_copy(
                src_ref=rows(*blk) if src is None else src, dst_ref=rows(*blk),
                send_sem=send_sems.at[k], recv_sem=recv_sems.at[k], device_id=to, device_id_type=MESH)

        mine = pltpu.make_async_copy(x_ref, rows(*me), local_sem)
        mine.start()
        first = [copy(0, me, sibling, src=x_ref)]
        first += [copy(1 + j, me, (*chip, c), src=x_ref) for j, chip in enumerate(chips)]
        for cp in first:
            cp.start()
        passed = [copy(4 + j, (*chip, c), sibling) for j, chip in enumerate(chips)]
        for j, chip in enumerate(chips):
            copy(1 + j, (*chip, c), me).wait_recv()
            passed[j].start()
        copy(0, sibling, me).wait_recv()
        for j, chip in enumerate(chips):
            copy(4 + j, (*chip, 1 - c), me).wait_recv()
        for cp in first + passed:
            cp.wait_send()
        mine.wait()

    return pl.pallas_call(
        body, name=name, out_shape=jax.ShapeDtypeStruct((N_DEV * m_per, n), block.dtype),
        in_specs=[pl.BlockSpec(memory_space=pltpu.VMEM)], out_specs=pl.BlockSpec(memory_space=pltpu.VMEM),
        scratch_shapes=[pltpu.SemaphoreType.DMA((7,)), pltpu.SemaphoreType.DMA((7,)), pltpu.SemaphoreType.DMA],
    )(block)


def _cast_own_block(shards, layer, chip, name):
    _, r, cols = shards.shape
    rows = _chunk_rows(r, cols)

    def body(chip_ref, w_ref, o_ref):
        o_ref[...] = w_ref[...].astype(BF16)

    return pl.pallas_call(
        body, name=name,
        grid_spec=pltpu.PrefetchScalarGridSpec(
            num_scalar_prefetch=1, grid=(r // rows,),
            in_specs=[pl.BlockSpec((None, rows, cols), lambda i, chip_ref: (layer, i, 0))],
            out_specs=pl.BlockSpec((None, rows, cols), lambda i, chip_ref: (chip_ref[0], i, 0))),
        out_shape=jax.ShapeDtypeStruct((N_CHIPS, r, cols), BF16),
    )(chip, shards)


def _sequencer_gather(bufs, name, collective_id):
    n_t = len(bufs)
    dims = [b.shape[1:] for b in bufs]
    refs = [jax.new_ref(b, memory_space=pltpu.MemorySpace.HBM) for b in bufs]

    @pl.kernel(mesh=plsc.ScalarSubcoreMesh(axis_name="sequencer", num_cores=1), name=name,
               scratch_types=[pltpu.SemaphoreType.DMA((n_t,))] * 4,
               compiler_params=pltpu.CompilerParams(collective_id=collective_id))
    def launch(send_ici, recv_ici, send_d2d, recv_d2d):
        x, y, c, sibling, chips = _place()
        p_me = 2 * x + y
        peers = [sibling] + [(cx, cy, c) for cx, cy in chips]
        barrier = pltpu.get_barrier_semaphore()
        for peer in peers:
            pl.semaphore_signal(barrier, inc=1, device_id=peer, device_id_type=MESH)
        pl.semaphore_wait(barrier, len(peers))

        def waiter(t, sem_s, sem_r):
            win = refs[t].at[pl.ds(0, 3), pl.ds(0, dims[t][0] // 2), :]
            return pltpu.make_async_remote_copy(src_ref=win, dst_ref=win, send_sem=sem_s.at[t], recv_sem=sem_r.at[t],
                                                device_id=sibling, device_id_type=MESH)

        def half_copy(t, chip_idx, to, sem_s, sem_r):
            r2 = dims[t][0] // 2
            win = refs[t].at[chip_idx, pl.ds(c * r2, r2), :]
            return pltpu.make_async_remote_copy(src_ref=win, dst_ref=win, send_sem=sem_s.at[t], recv_sem=sem_r.at[t],
                                                device_id=to, device_id_type=MESH)

        for t in range(n_t):
            for cx, cy in chips:
                half_copy(t, p_me, (cx, cy, c), send_ici, recv_ici).start()
        for t in range(n_t):
            waiter(t, send_ici, recv_ici).wait_recv()
            for cx, cy in chips:
                half_copy(t, 2 * cx + cy, sibling, send_d2d, recv_d2d).start()
        for t in range(n_t):
            waiter(t, send_d2d, recv_d2d).wait_recv()
            waiter(t, send_ici, recv_ici).wait_send()
            waiter(t, send_d2d, recv_d2d).wait_send()

    launch()
    return [r[...] for r in refs]


def _sequencer_allgather8(block, dev, name, collective_id):
    m_per, n = block.shape
    src = jax.new_ref(block, memory_space=pltpu.MemorySpace.HBM)
    out = jax.empty_ref(jax.ShapeDtypeStruct((N_DEV * m_per, n), block.dtype), memory_space=pltpu.MemorySpace.HBM)

    @pl.kernel(mesh=plsc.ScalarSubcoreMesh(axis_name="sequencer", num_cores=1), name=name,
               scratch_types=[pltpu.SemaphoreType.DMA((7,))] * 2,
               compiler_params=pltpu.CompilerParams(collective_id=collective_id))
    def launch(send_sems, recv_sems):
        x, y, c, sibling, chips = _place()
        me = (x, y, c)
        _handshake([sibling] + [(cx, cy, c) for cx, cy in chips])

        def rows(px, py, pc):
            return out.at[pl.ds((4 * px + 2 * py + pc) * m_per, m_per), :]

        def copy(k, blk, to, from_src=False):
            return pltpu.make_async_remote_copy(
                src_ref=src if from_src else rows(*blk), dst_ref=rows(*blk),
                send_sem=send_sems.at[k], recv_sem=recv_sems.at[k], device_id=to, device_id_type=MESH)

        first = [copy(0, me, sibling, True)] + [copy(1 + j, me, (*chip, c), True) for j, chip in enumerate(chips)]
        for cp in first:
            cp.start()
        passed = [copy(4 + j, (*chip, c), sibling) for j, chip in enumerate(chips)]
        for j, chip in enumerate(chips):
            copy(1 + j, (*chip, c), me).wait_recv()
            passed[j].start()
        copy(0, sibling, me).wait_recv()
        for j, chip in enumerate(chips):
            copy(4 + j, (*chip, 1 - c), me).wait_recv()
        for cp in first + passed:
            cp.wait_send()

    launch()
    return lax.dynamic_update_slice(out[...], block, (dev * m_per, 0))


def _others():
    x, y, c = lax.axis_index("x"), lax.axis_index("y"), lax.axis_index("c")
    flip = lambda v, f: 1 - v if f else v
    return [(flip(x, fx), flip(y, fy), flip(c, fc))
            for fx in (0, 1) for fy in (0, 1) for fc in (0, 1) if (fx, fy, fc) != (0, 0, 0)]


def _handshake(peers):
    barrier = pltpu.get_barrier_semaphore()
    for peer in peers:
        pl.semaphore_signal(barrier, inc=1, device_id=peer, device_id_type=MESH)
    pl.semaphore_wait(barrier, len(peers))


def _sequencer_scatter(parts, name, collective_id):
    n_t = len(parts)
    dims = [p.shape[1:] for p in parts]
    srcs = [jax.new_ref(p, memory_space=pltpu.MemorySpace.HBM) for p in parts]
    inboxes = [jax.empty_ref(jax.ShapeDtypeStruct((N_DEV, r // 2, cols), BF16), memory_space=pltpu.MemorySpace.HBM)
               for r, cols in dims]

    @pl.kernel(mesh=plsc.ScalarSubcoreMesh(axis_name="sequencer", num_cores=1), name=name,
               scratch_types=[pltpu.SemaphoreType.DMA((n_t,))] * 2,
               compiler_params=pltpu.CompilerParams(collective_id=collective_id))
    def launch(send_sem, recv_sem):
        x, y, c = lax.axis_index("x"), lax.axis_index("y"), lax.axis_index("c")
        me = 4 * x + 2 * y + c
        peers = _others()
        _handshake(peers)
        for t in range(n_t):
            h = dims[t][0] // 2
            for qx, qy, qc in peers:
                pltpu.make_async_remote_copy(
                    src_ref=srcs[t].at[2 * qx + qy, pl.ds(qc * h, h), :], dst_ref=inboxes[t].at[me],
                    send_sem=send_sem.at[t], recv_sem=recv_sem.at[t], device_id=(qx, qy, qc), device_id_type=MESH).start()
        for t in range(n_t):
            win = inboxes[t].at[pl.ds(0, N_DEV - 1)]
            both = pltpu.make_async_remote_copy(src_ref=win, dst_ref=win, send_sem=send_sem.at[t],
                                                recv_sem=recv_sem.at[t], device_id=peers[0], device_id_type=MESH)
            both.wait_recv()
            both.wait_send()

    launch()
    return [b[...] for b in inboxes]


def _sum_pieces(part, inbox, place, name):
    _, r, cols = part.shape
    h = r // 2
    rows = _chunk_rows(h, cols)
    steps = h // rows

    def body(place_ref, own_ref, in_ref, o_ref):
        dev = place_ref[2]
        own = own_ref[...].astype(F32)
        acc = jnp.zeros((rows, cols), F32)
        for d in range(N_DEV):
            acc = acc + jnp.where(dev == d, own, in_ref[d].astype(F32))
        o_ref[...] = acc

    return pl.pallas_call(
        body, name=name,
        grid_spec=pltpu.PrefetchScalarGridSpec(
            num_scalar_prefetch=1, grid=(steps,),
            in_specs=[pl.BlockSpec((None, rows, cols), lambda i, pr: (pr[0], pr[1] * steps + i, 0)),
                      pl.BlockSpec((N_DEV, rows, cols), lambda i, pr: (0, i, 0))],
            out_specs=pl.BlockSpec((rows, cols), lambda i, pr: (pr[1] * steps + i, 0))),
        out_shape=jax.ShapeDtypeStruct((r, cols), F32),
    )(place, part, inbox)


def _sequencer_swap_halves(halves, name, collective_id):
    n_t = len(halves)
    refs = [jax.new_ref(a, memory_space=pltpu.MemorySpace.HBM) for a in halves]

    @pl.kernel(mesh=plsc.ScalarSubcoreMesh(axis_name="sequencer", num_cores=1), name=name,
               scratch_types=[pltpu.SemaphoreType.DMA((n_t,))] * 2,
               compiler_params=pltpu.CompilerParams(collective_id=collective_id))
    def launch(send_sem, recv_sem):
        x, y, c = lax.axis_index("x"), lax.axis_index("y"), lax.axis_index("c")
        sibling = (x, y, 1 - c)
        _handshake([sibling])
        copies = []
        for t in range(n_t):
            h = halves[t].shape[0] // 2
            win = refs[t].at[pl.ds(c * h, h), :]
            copies.append(pltpu.make_async_remote_copy(src_ref=win, dst_ref=win, send_sem=send_sem.at[t],
                                                       recv_sem=recv_sem.at[t], device_id=sibling, device_id_type=MESH))
            copies[-1].start()
        for cp in copies:
            cp.wait()

    launch()
    return [r[...] for r in refs]


def _cond_rows(c16, w, act, name):
    n_l, dm, wid = w.shape

    def body(c_ref, w_ref, o_ref, a_ref):
        cv = c_ref[...]
        if act:
            cv = cv * _sig(cv)
        a_ref[...] = cv
        o_ref[...] = _dot_f32(cv, w_ref[...])

    return pl.pallas_call(
        body, name=name, grid=(n_l,),
        in_specs=[_full((16, dm)), pl.BlockSpec((None, dm, wid), lambda l: (l, 0, 0))],
        out_specs=[pl.BlockSpec((None, 16, wid), lambda l: (l, 0, 0)), _full((16, dm))],
        out_shape=[jax.ShapeDtypeStruct((n_l, 16, wid), F32), jax.ShapeDtypeStruct((16, dm), F32)],
    )(c16, w)


def _outer_grad(ct, dm, name):
    n_l, kk, wid = dm.shape
    d_rows = ct.shape[0]

    def body(c_ref, d_ref, o_ref):
        o_ref[...] = _dot_f32(c_ref[...], d_ref[...])

    return pl.pallas_call(
        body, name=name, grid=(n_l,),
        in_specs=[_full((d_rows, kk)), pl.BlockSpec((None, kk, wid), lambda l: (l, 0, 0))],
        out_specs=pl.BlockSpec((None, d_rows, wid), lambda l: (l, 0, 0)),
        out_shape=jax.ShapeDtypeStruct((n_l, d_rows, wid), F32),
    )(ct, dm)


def _sum_devices(g, name):
    rows, n = g.shape

    def body(g_ref, o_ref):
        acc = g_ref[0:SUBLANES, :]
        for dev in range(1, N_DEV):
            acc = acc + g_ref[dev * SUBLANES:(dev + 1) * SUBLANES, :]
        o_ref[...] = acc

    return pl.pallas_call(body, name=name, out_shape=jax.ShapeDtypeStruct((SUBLANES, n), F32))(g)


def _adamw(w, g, m, v, name):
    shape = w.shape
    cols = shape[-1]
    rows = w.size // cols
    tr = rows
    for cand in range(SUBLANES, min(rows, 256) + 1, SUBLANES):
        if rows % cand == 0:
            tr = cand
    if rows * cols <= COMM_CHUNK_ELEMS:
        tr = rows
    c1 = 1.0 / (1.0 - ADAM_B1 ** ADAM_STEP)
    c2 = 1.0 / (1.0 - ADAM_B2 ** ADAM_STEP)

    def body(w_ref, g_ref, m_ref, v_ref, d_ref, mo_ref, vo_ref):
        gv = g_ref[...]
        m_new = ADAM_B1 * m_ref[...] + (1.0 - ADAM_B1) * gv
        v_new = ADAM_B2 * v_ref[...] + (1.0 - ADAM_B2) * (gv * gv)
        mo_ref[...] = m_new
        vo_ref[...] = v_new
        d_ref[...] = -ADAM_LR * ((m_new * c1) / (jnp.sqrt(v_new * c2) + ADAM_EPS) + ADAM_WD * w_ref[...])

    spec = pl.BlockSpec((tr, cols), lambda i: (i, 0))
    outs = pl.pallas_call(
        body, name=name, grid=(rows // tr,), in_specs=[spec] * 4, out_specs=[spec] * 3,
        out_shape=[jax.ShapeDtypeStruct((rows, cols), F32)] * 3,
    )(*[a.reshape(rows, cols) for a in (w, g, m, v)])
    return tuple(o.reshape(shape) for o in outs)


def _pad_cols(a, cols):
    return jnp.pad(a, [(0, 0)] * (a.ndim - 1) + [(0, cols - a.shape[-1])])


def _flat8(parts, width):
    v = jnp.concatenate([p.reshape(-1) for p in parts])
    return jnp.pad(v, (0, width - v.shape[0])).reshape(SUBLANES, width // SUBLANES)


KV_SHARD = 514
KV_SHARD_PAD = 640
BIG = ("a_w_in", "a_w_out", "kv_w", "b_w_q", "b_w_out", "up0", "up1", "down0", "down1")


def kernel(x, c, ada_w, ada_b, a_w_in, a_lb_logits, a_norm_g, a_w_out, kv_ada_w, kv_ada_b, kv_w, kv_b_f, k_norm_g, b_w_q, q_norm_g, b_w_out, ffn_w_up, ffn_conv_w, ffn_conv_b, ffn_w_down, loss_target, m_ada_w, m_ada_b, m_a_w_in, m_a_lb_logits, m_a_norm_g, m_a_w_out, m_kv_ada_w, m_kv_ada_b, m_kv_w, m_kv_b_f, m_k_norm_g, m_b_w_q, m_q_norm_g, m_b_w_out, m_ffn_w_up, m_ffn_conv_w, m_ffn_conv_b, m_ffn_w_down, v_ada_w, v_ada_b, v_a_w_in, v_a_lb_logits, v_a_norm_g, v_a_w_out, v_kv_ada_w, v_kv_ada_b, v_kv_w, v_kv_b_f, v_k_norm_g, v_b_w_q, v_q_norm_g, v_b_w_out, v_ffn_w_up, v_ffn_conv_w, v_ffn_conv_b, v_ffn_w_down):
    dm, ff = D_MODEL, D_FF
    ix, iy, ic = lax.axis_index("x"), lax.axis_index("y"), lax.axis_index("c")
    chip = 2 * ix + iy
    dev = 2 * chip + ic

    w1 = 10240
    g1 = _allgather8(_flat8([c, a_lb_logits, ffn_conv_w], w1), "gather_cond").reshape(N_DEV, w1)
    c_all = g1[:, :dm]
    per_chip = g1[0::2]
    lb_logits = per_chip[:, dm:dm + 512].reshape(N_CHIPS, 2, 256).transpose(1, 0, 2).reshape(2, dm)
    conv_w = per_chip[:, dm + 512:dm + 512 + 2 * CONV_W * FFN_COLS].reshape(N_CHIPS, 2, CONV_W, FFN_COLS)
    conv_w = conv_w.transpose(1, 2, 0, 3).reshape(2, CONV_W, 2, ff).transpose(0, 2, 1, 3)
    conv_b = ffn_conv_b.reshape(2, 2, 1, ff)
    lb = jax.nn.softmax(lb_logits, axis=0)[0:1]

    c16 = jnp.pad(c_all, ((0, 8), (0, 0)))
    mod_ada, c_act16 = _cond_rows(c16, ada_w, True, "mod_ada")
    mod_kv, _ = _cond_rows(c16, kv_ada_w[None], True, "mod_kv")
    mine = jnp.concatenate([mod_ada[0, :8], mod_ada[1, :8], mod_kv[0, :8]], axis=1)
    w2 = mine.shape[1]
    g2 = _allgather8(mine, "gather_mod").reshape(N_DEV, 8, w2)[0::2]
    my_rows = lax.dynamic_index_in_dim(g2, dev, axis=1, keepdims=False)
    mod0 = my_rows[:, 0:1536].reshape(6 * dm) + ada_b[0]
    mod1 = my_rows[:, 1536:3072].reshape(6 * dm) + ada_b[1]
    modk = my_rows[:, 3072:3584].reshape(2 * dm) + kv_ada_b
    mods = {"l0": [v.reshape(1, dm) for v in jnp.split(mod0, 6)],
            "l1": [v.reshape(1, dm) for v in jnp.split(mod1, 6)],
            "kv": [v.reshape(1, dm) for v in jnp.split(modk, 2)]}

    local = [(a_w_in, 0), (a_w_out, 0), (_pad_cols(kv_w, KV_SHARD_PAD)[None], 0), (b_w_q, 0), (b_w_out, 0),
             (ffn_w_up, 0), (ffn_w_up, 1), (ffn_w_down, 0), (ffn_w_down, 1)]
    chip_arr = chip.reshape(1).astype(jnp.int32)
    own = {n: _cast_own_block(w, layer, chip_arr, "cast_" + n) for n, (w, layer) in zip(BIG, local)}
    stages = {"mixer0": ("a_w_in",), "ffn0": ("a_w_out", "up0", "down0"),
              "layer1": ("kv_w", "b_w_q", "b_w_out", "up1", "down1")}
    arriving = {st: _sequencer_gather([own[n] for n in names], "gather_" + st, cid)
                for cid, (st, names) in enumerate(stages.items(), start=1)}
    rowwise = lambda g: g.reshape(1, -1, dm)

    def weights_at(stage, token):
        got, token = lax.optimization_barrier((arriving[stage], token))
        g = dict(zip(stages[stage], got))
        if stage == "mixer0":
            return {"a_w_in": g["a_w_in"]}, token
        if stage == "ffn0":
            return {"a_w_out": rowwise(g["a_w_out"]), "up0": g["up0"], "down0": rowwise(g["down0"])}, token
        kv_full = g["kv_w"][:, :, :KV_SHARD].transpose(1, 0, 2).reshape(dm, N_CHIPS * KV_SHARD)
        return {"kv_k": kv_full[None, :, :dm], "kv_v": kv_full[None, :, dm:2 * dm],
                "kv_f": _pad_cols(kv_full[None, :, 2 * dm:], LANES), "b_w_q": g["b_w_q"],
                "b_w_out": rowwise(g["b_w_out"]), "up1": g["up1"], "down1": rowwise(g["down1"])}, token

    vecs = {"a_norm_g": jnp.tile(a_norm_g, (1, HEADS)), "k_norm_g": jnp.tile(k_norm_g[None], (1, HEADS)),
            "q_norm_g": jnp.tile(q_norm_g, (1, HEADS)), "kv_b_f": _pad_cols(kv_b_f[None], LANES),
            "conv_w0": conv_w[0], "conv_b0": conv_b[0], "conv_w1": conv_w[1], "conv_b1": conv_b[1]}

    sq, grad_x, big, small, marks = _local_step(x[0], loss_target[0], mods, lb, vecs, weights_at)
    loss = lax.psum(0.5 * jnp.sum(sq) / dm, ("x", "y", "c"))

    kv_grad = jnp.concatenate([big["kv_k"][0], big["kv_v"][0], big["kv_f"][0][:, :HEADS]], axis=1)
    kv_grad = _pad_cols(kv_grad.reshape(dm, N_CHIPS, KV_SHARD).transpose(1, 0, 2), KV_SHARD_PAD)
    chipwise = lambda g: g.reshape(N_CHIPS, -1, dm)
    parts = dict(zip(BIG, [big["a_w_in"], chipwise(big["a_w_out"]), kv_grad, big["b_w_q"], chipwise(big["b_w_out"]),
                           big["up0"], big["up1"], chipwise(big["down0"]), chipwise(big["down1"])]))
    place = jnp.stack([chip, ic, dev]).astype(jnp.int32)

    served = []

    groups = (("up1", "down1"), ("b_w_out", "b_w_q", "kv_w"), ("up0", "down0", "a_w_out"), ("a_w_in",))

    def scatter_group(k):
        mine = [parts[n] for n in groups[k]]
        if served:
            mine, _ = lax.optimization_barrier((mine, served[-1]))
        served.append(_sequencer_scatter(mine, "scatter_grads_%d" % k, 4 + k))

    def sum_group(k, token):
        inboxes, _ = lax.optimization_barrier((served[k], token))
        return [_sum_pieces(parts[n], box, place, "sum_" + n) for n, box in zip(groups[k], inboxes)]

    def swap_group(k, halves, behind):
        halves, _ = lax.optimization_barrier((halves, behind))
        return dict(zip(groups[k], _sequencer_swap_halves(halves, "swap_grads_%d" % k, 8 + k)))

    for k in range(4):
        scatter_group(k)
    halves = [sum_group(0, marks["attention_bwd"]), sum_group(1, marks["ffn0_bwd"]), sum_group(2, marks["mixer0_bwd"])]

    fold = lambda a: a.sum(axis=0)
    heads = lambda a: fold(a).reshape(HEADS, HEAD_DIM).sum(axis=0)
    conv_flat = lambda a: a.sum(axis=2).transpose(1, 0, 2)
    pieces = ([fold(a) for a in small["mod_l0"]] + [fold(a) for a in small["mod_l1"]] + [fold(a) for a in small["mod_kv"]]
              + [conv_flat(small["conv0"]), conv_flat(small["conv1"]), heads(small["a_norm_g"]), heads(small["k_norm_g"]),
                 heads(small["q_norm_g"]), fold(small["kv_b_f"]), fold(small["lb"])])
    w3 = 61440
    small_vec, _ = lax.optimization_barrier((_flat8(pieces, w3), served[3]))
    g3 = _sequencer_allgather8(small_vec, dev, "gather_small", 12)
    rs = {}
    for k in range(3):
        rs.update(swap_group(k, halves[k], g3))
    tot = _sum_devices(g3, "sum_small").reshape(w3)
    n_mod = 14 * dm
    dmod_all = g3.reshape(N_DEV, w3)[:, :n_mod]
    o = n_mod
    conv_tot = [tot[o + l * 8 * ff: o + (l + 1) * 8 * ff].reshape(4, 2 * ff) for l in range(2)]
    o += 16 * ff
    g_a_norm, g_k_norm, g_q_norm = (tot[o + i * HEAD_DIM: o + (i + 1) * HEAD_DIM] for i in range(3))
    o += 3 * HEAD_DIM
    g_kv_b_f = tot[o:o + HEADS]
    dlb = tot[o + LANES:o + LANES + dm]

    ct = _pad_cols(c_act16[:8].T, LANES)
    dmod_pad = jnp.pad(dmod_all, ((0, LANES - N_DEV), (0, 0)))
    cols_ada = jnp.stack([lax.dynamic_slice_in_dim(dmod_pad, l * 6 * dm + chip * 1536, 1536, axis=1) for l in range(2)])
    cols_kv = lax.dynamic_slice_in_dim(dmod_pad, 12 * dm + chip * 512, 512, axis=1)[None]
    g_ada_w = _outer_grad(ct, cols_ada, "grad_ada_w")
    g_kv_ada_w = _outer_grad(ct, cols_kv, "grad_kv_ada_w")[0]

    my_lb = lax.dynamic_slice_in_dim(lb[0], chip * 256, 256)
    l0 = lax.dynamic_slice_in_dim(dlb, chip * 256, 256) * my_lb * (1.0 - my_lb)
    grads = {
        "ada_w": g_ada_w, "ada_b": jnp.stack([tot[:6 * dm], tot[6 * dm:12 * dm]]),
        "a_lb_logits": jnp.stack([l0, -l0]), "a_norm_g": g_a_norm[None],
        "a_w_out": rs["a_w_out"][None], "kv_ada_w": g_kv_ada_w, "kv_ada_b": tot[12 * dm:14 * dm],
        "kv_w": rs["kv_w"][:, :KV_SHARD], "kv_b_f": g_kv_b_f, "k_norm_g": g_k_norm,
        "b_w_q": rs["b_w_q"][None], "q_norm_g": g_q_norm[None], "b_w_out": rs["b_w_out"][None],
        "ffn_w_up": jnp.stack([rs["up0"], rs["up1"]]),
        "ffn_conv_w": jnp.stack([lax.dynamic_slice_in_dim(ct_l[:CONV_W], chip * FFN_COLS, FFN_COLS, axis=1) for ct_l in conv_tot]),
        "ffn_conv_b": jnp.stack([ct_l[CONV_W] for ct_l in conv_tot]),
        "ffn_w_down": jnp.stack([rs["down0"], rs["down1"]]),
    }
    weights = dict(ada_w=ada_w, ada_b=ada_b, a_w_in=a_w_in, a_lb_logits=a_lb_logits, a_norm_g=a_norm_g, a_w_out=a_w_out,
                   kv_ada_w=kv_ada_w, kv_ada_b=kv_ada_b, kv_w=kv_w, kv_b_f=kv_b_f, k_norm_g=k_norm_g, b_w_q=b_w_q,
                   q_norm_g=q_norm_g, b_w_out=b_w_out, ffn_w_up=ffn_w_up, ffn_conv_w=ffn_conv_w, ffn_conv_b=ffn_conv_b,
                   ffn_w_down=ffn_w_down)
    m_in = dict(ada_w=m_ada_w, ada_b=m_ada_b, a_w_in=m_a_w_in, a_lb_logits=m_a_lb_logits, a_norm_g=m_a_norm_g,
                a_w_out=m_a_w_out, kv_ada_w=m_kv_ada_w, kv_ada_b=m_kv_ada_b, kv_w=m_kv_w, kv_b_f=m_kv_b_f,
                k_norm_g=m_k_norm_g, b_w_q=m_b_w_q, q_norm_g=m_q_norm_g, b_w_out=m_b_w_out, ffn_w_up=m_ffn_w_up,
                ffn_conv_w=m_ffn_conv_w, ffn_conv_b=m_ffn_conv_b, ffn_w_down=m_ffn_w_down)
    v_in = dict(ada_w=v_ada_w, ada_b=v_ada_b, a_w_in=v_a_w_in, a_lb_logits=v_a_lb_logits, a_norm_g=v_a_norm_g,
                a_w_out=v_a_w_out, kv_ada_w=v_kv_ada_w, kv_ada_b=v_kv_ada_b, kv_w=v_kv_w, kv_b_f=v_kv_b_f,
                k_norm_g=v_k_norm_g, b_w_q=v_b_w_q, q_norm_g=v_q_norm_g, b_w_out=v_b_w_out, ffn_w_up=v_ffn_w_up,
                ffn_conv_w=v_ffn_conv_w, ffn_conv_b=v_ffn_conv_b, ffn_w_down=v_ffn_w_down)

    names = list(weights)
    step = lambda n: _adamw(weights[n], grads[n], m_in[n], v_in[n], "adamw_" + n)
    grads = {n: g.reshape(weights[n].shape) for n, g in grads.items()}
    upd = {n: step(n) for n in names if n != "a_w_in"}
    last = sum_group(3, [u[0] for u in upd.values()])
    grads["a_w_in"] = swap_group(3, last, last)["a_w_in"][None]
    upd["a_w_in"] = step("a_w_in")
    return (loss, grad_x[None], *[grads[n] for n in names], *[upd[n][0] for n in names],
            *[upd[n][1] for n in names], *[upd[n][2] for n in names])
```

```python
import jax
import jax.numpy as jnp
from jax import lax
from jax.experimental import pallas as pl
from jax.experimental.pallas import tpu as pltpu
from jax.experimental.pallas import tpu_sc as plsc

F32 = jnp.float32
BF16 = jnp.bfloat16

D_MODEL = 1024
HEADS = 8
HEAD_DIM = 128
A_CHUNK = 64
D_FF = 2816
CONV_W = 3
EPS = 1e-6
NEG_INF = -1e30
N_CHIPS = 4
N_DEV = 8

ADAM_LR = 0.001
ADAM_B1 = 0.9
ADAM_B2 = 0.999
ADAM_EPS = 1e-08
ADAM_WD = 0.01
ADAM_STEP = 10

SUBLANES = 8
BF16_ROWS = 16
LANES = 128
HALO = BF16_ROWS
ROW_TILE = 512
TOKEN_TILE_TN = 2048
FFN_COLS = 1408
HGRN_ROWS = 256
ATT_TILE = 512
ATT_SPLIT = 2
MESH = pl.DeviceIdType.MESH


def _sig(x):
    return jax.nn.sigmoid(x)


def _dot(a, b):
    return jnp.dot(a, b, preferred_element_type=F32)


def _dot_nt(a, b):
    return lax.dot_general(a, b, (((1,), (1,)), ((), ())), preferred_element_type=F32)


def _dot_tn(a, b):
    return lax.dot_general(a, b, (((0,), (0,)), ((), ())), preferred_element_type=F32)


def _split2(x):
    hi = x.astype(BF16)
    lo = (x - hi.astype(F32)).astype(BF16)
    return hi, lo


def _dot_f32(a, b):
    ah, al = _split2(a)
    bh, bl = _split2(b)
    return _dot(ah, bh) + _dot(ah, bl) + _dot(al, bh)


def _tri_dot(tri, x):
    hi = x.astype(BF16)
    r = x - hi.astype(F32)
    mid = r.astype(BF16)
    lo = (r - mid.astype(F32)).astype(BF16)
    return _dot(tri, hi) + _dot(tri, mid) + _dot(tri, lo)


def _tri(n, upper=False):
    r = lax.broadcasted_iota(jnp.int32, (n, n), 0)
    c = lax.broadcasted_iota(jnp.int32, (n, n), 1)
    keep = (c >= r) if upper else (c <= r)
    return jnp.where(keep, 1.0, 0.0).astype(BF16)


def _colsum8(v):
    rows, n = v.shape
    return v.reshape(rows // SUBLANES, SUBLANES, n).sum(axis=0)


def _full(shape):
    nd = len(shape)
    return pl.BlockSpec(shape, lambda *_: (0,) * nd)


def _tile(n, want):
    t = min(n, want)
    assert n % t == 0, (n, t)
    return t


def _mm_nn(a, w, groups, out_dtype, name):
    m_rows, k = a.shape
    p_n, _, n = w.shape
    per = p_n // groups
    tm = _tile(m_rows, ROW_TILE)

    def body(a_ref, w_ref, o_ref):
        av = a_ref[...]
        for p in range(p_n):
            o_ref[p // per, :, (p % per) * n:(p % per + 1) * n] = _dot(av, w_ref[p]).astype(out_dtype)

    return pl.pallas_call(
        body, name=name, grid=(m_rows // tm,),
        in_specs=[pl.BlockSpec((tm, k), lambda i: (i, 0)), _full((p_n, k, n))],
        out_specs=pl.BlockSpec((groups, tm, per * n), lambda i: (0, i, 0)),
        out_shape=jax.ShapeDtypeStruct((groups, m_rows, per * n), out_dtype),
    )(a, w)


def _mm_nt(d, w, out_dtype, name):
    g_n, m_rows, _ = d.shape
    p_n, k, n = w.shape
    per = p_n // g_n
    tm = _tile(m_rows, ROW_TILE)

    def body(d_ref, w_ref, o_ref):
        acc = None
        for p in range(p_n):
            t = _dot_nt(d_ref[p // per, :, (p % per) * n:(p % per + 1) * n], w_ref[p])
            acc = t if acc is None else acc + t
        o_ref[...] = acc.astype(out_dtype)

    return pl.pallas_call(
        body, name=name, grid=(m_rows // tm,),
        in_specs=[pl.BlockSpec((g_n, tm, per * n), lambda i: (0, i, 0)), _full((p_n, k, n))],
        out_specs=pl.BlockSpec((tm, k), lambda i: (i, 0)),
        out_shape=jax.ShapeDtypeStruct((m_rows, k), out_dtype),
    )(d, w)


def _mm_tn(a, d, p_n, name):
    m_rows, k = a.shape
    g_n, _, w_cols = d.shape
    per = p_n // g_n
    n = w_cols // per
    tm = _tile(m_rows, TOKEN_TILE_TN if k <= D_MODEL else ROW_TILE)
    steps = m_rows // tm

    def body(a_ref, d_ref, o_ref, acc):
        m = pl.program_id(1)

        @pl.when(m == 0)
        def _():
            acc[...] = jnp.zeros_like(acc)

        acc[...] += _dot_tn(a_ref[...], d_ref[...])

        @pl.when(m == steps - 1)
        def _():
            o_ref[...] = acc[...].astype(BF16)

    return pl.pallas_call(
        body, name=name, grid=(p_n, steps),
        in_specs=[pl.BlockSpec((tm, k), lambda p, m: (m, 0)),
                  pl.BlockSpec((None, tm, n), lambda p, m: (p // per, m, p % per))],
        out_specs=pl.BlockSpec((None, k, n), lambda p, m: (p, 0, 0)),
        out_shape=jax.ShapeDtypeStruct((p_n, k, n), BF16),
        scratch_shapes=[pltpu.VMEM((k, n), F32)],
    )(a, d)


def _premix(x, shift, scale, name, branch=None, gate=None):
    s, dm = x.shape
    tm = _tile(s, ROW_TILE)
    with_branch = branch is not None

    def body(*refs):
        x_ref, sh_ref, sc_ref = refs[:3]
        xv = x_ref[...]
        if with_branch:
            xv = xv + refs[4][...] * refs[3][...]
            refs[-2][...] = xv
        inv = lax.rsqrt(jnp.mean(xv * xv, axis=-1, keepdims=True) + EPS)
        refs[-1][...] = (xv * inv * (1.0 + sc_ref[...]) + sh_ref[...]).astype(BF16)

    row = pl.BlockSpec((tm, dm), lambda i: (i, 0))
    vec = _full((1, dm))
    ins, specs = [x, shift, scale], [row, vec, vec]
    out_shape, out_specs = [jax.ShapeDtypeStruct((s, dm), BF16)], [row]
    if with_branch:
        ins += [branch, gate]
        specs += [row, vec]
        out_shape.insert(0, jax.ShapeDtypeStruct((s, dm), F32))
        out_specs.insert(0, row)
    outs = pl.pallas_call(body, name=name, grid=(s // tm,), in_specs=specs, out_specs=out_specs,
                          out_shape=out_shape)(*ins)
    return tuple(outs) if with_branch else outs[0]


def _premix_bwd(x, terms, dres, name, branch=None):
    s, dm = x.shape
    tm = _tile(s, ROW_TILE)
    pairs = [pr for _, prs in terms for pr in prs]
    n_in = 2 + len(terms) + 2 * len(pairs) + (2 if branch else 0)

    def body(*refs):
        x_ref, dres_ref = refs[:2]
        sc_refs = refs[2:2 + len(terms)]
        mm_refs = refs[2 + len(terms):2 + len(terms) + 2 * len(pairs)]
        outs = refs[n_in:]

        @pl.when(pl.program_id(0) == 0)
        def _():
            for o in outs[1:1 + 2 * len(terms)]:
                o[...] = jnp.zeros_like(o)
            if branch:
                outs[-1][...] = jnp.zeros_like(outs[-1])

        xv = x_ref[...]
        inv = lax.rsqrt(jnp.mean(xv * xv, axis=-1, keepdims=True) + EPS)
        r = xv * inv
        dx = dres_ref[...]
        k = 0
        for t, (_, prs) in enumerate(terms):
            dh = None
            for d, w in prs:
                d_ref, w_ref = mm_refs[2 * k], mm_refs[2 * k + 1]
                k += 1
                p_n, _, n = w.shape
                per = p_n // d.shape[0]
                for p in range(p_n):
                    part = _dot_nt(d_ref[p // per, :, (p % per) * n:(p % per + 1) * n], w_ref[p])
                    dh = part if dh is None else dh + part
            dr = dh * (1.0 + sc_refs[t][...])
            dx = dx + inv * (dr - r * jnp.mean(dr * r, axis=-1, keepdims=True))
            outs[1 + 2 * t][...] += _colsum8(dh)
            outs[2 + 2 * t][...] += _colsum8(dh * r)
        outs[0][...] = dx
        if branch:
            y_ref, g_ref = refs[n_in - 2:n_in]
            outs[-2][0] = (dx * g_ref[...]).astype(BF16)
            outs[-1][...] += _colsum8(dx * y_ref[...])

    row = pl.BlockSpec((tm, dm), lambda i: (i, 0))
    vec, acc = _full((1, dm)), _full((SUBLANES, dm))
    ins, specs = [x, dres] + [sc for sc, _ in terms], [row, row] + [vec] * len(terms)
    for d, w in pairs:
        ins += [d, w]
        specs += [pl.BlockSpec((d.shape[0], tm, d.shape[2]), lambda i: (0, i, 0)), _full(w.shape)]
    out_shape = [jax.ShapeDtypeStruct((s, dm), F32)] + [jax.ShapeDtypeStruct((SUBLANES, dm), F32)] * (2 * len(terms))
    out_specs = [row] + [acc] * (2 * len(terms))
    if branch:
        ins += list(branch)
        specs += [row, vec]
        out_shape += [jax.ShapeDtypeStruct((1, s, dm), BF16), jax.ShapeDtypeStruct((SUBLANES, dm), F32)]
        out_specs += [pl.BlockSpec((1, tm, dm), lambda i: (0, i, 0)), acc]
    outs = pl.pallas_call(body, name=name, grid=(s // tm,), in_specs=specs, out_specs=out_specs,
                          out_shape=out_shape)(*ins)
    partials = [(outs[1 + 2 * t], outs[2 + 2 * t]) for t in range(len(terms))]
    return (outs[0], partials) + ((outs[-2], outs[-1]) if branch else ())


def _loss_head(x, branch, gate, target, name):
    s, dm = x.shape
    tm = _tile(s, ROW_TILE)

    def body(x_ref, b_ref, g_ref, t_ref, sq_ref, dy_ref, db_ref, dg_ref):
        @pl.when(pl.program_id(0) == 0)
        def _():
            sq_ref[...] = jnp.zeros_like(sq_ref)
            dg_ref[...] = jnp.zeros_like(dg_ref)

        bv, gv = b_ref[...], g_ref[...]
        err = x_ref[...] + gv * bv - t_ref[...]
        sq_ref[...] += _colsum8(err * err)
        dy = err * (1.0 / dm)
        dy_ref[...] = dy
        db_ref[0] = (dy * gv).astype(BF16)
        dg_ref[...] += _colsum8(dy * bv)

    row = pl.BlockSpec((tm, dm), lambda i: (i, 0))
    acc = _full((SUBLANES, dm))
    return pl.pallas_call(
        body, name=name, grid=(s // tm,), in_specs=[row, row, _full((1, dm)), row],
        out_specs=[acc, row, pl.BlockSpec((1, tm, dm), lambda i: (0, i, 0)), acc],
        out_shape=[jax.ShapeDtypeStruct((SUBLANES, dm), F32), jax.ShapeDtypeStruct((s, dm), F32),
                   jax.ShapeDtypeStruct((1, s, dm), BF16), jax.ShapeDtypeStruct((SUBLANES, dm), F32)],
    )(x, branch, gate, target)


def _conv_taps(e, w, b):
    return w[2:3] * e + w[1:2] * pltpu.roll(e, 1, 0) + w[0:1] * pltpu.roll(e, 2, 0) + b


def _ffn_specs(s, tm, cb):
    hb = tm // HALO
    last = s // HALO - 1
    main = pl.BlockSpec((2, tm, cb), lambda j, i: (0, i, j))
    prev = pl.BlockSpec((2, HALO, cb), lambda j, i: (0, jnp.maximum(i * hb - 1, 0), j))
    nxt = pl.BlockSpec((2, HALO, cb), lambda j, i: (0, jnp.minimum((i + 1) * hb, last), j))
    wspec = pl.BlockSpec((2, CONV_W, cb), lambda j, i: (0, 0, j))
    bspec = pl.BlockSpec((2, 1, cb), lambda j, i: (0, 0, j))
    return main, prev, nxt, wspec, bspec


def _convglu_fwd(u, w, b, name):
    _, s, f = u.shape
    tm = _tile(s, 256)
    cb = _tile(f, FFN_COLS)
    main, prev, _, wspec, bspec = _ffn_specs(s, tm, cb)

    def body(u_ref, up_ref, w_ref, b_ref, a_ref):
        first = jnp.where(pl.program_id(1) > 0, 1.0, 0.0)

        def conv(g):
            e = jnp.concatenate([up_ref[g].astype(F32) * first, u_ref[g].astype(F32)], axis=0)
            return _conv_taps(e, w_ref[g], b_ref[g])[HALO:]

        gate = conv(0)
        a_ref[...] = (gate * _sig(gate) * conv(1)).astype(BF16)

    return pl.pallas_call(
        body, name=name, grid=(f // cb, s // tm), in_specs=[main, prev, wspec, bspec],
        out_specs=pl.BlockSpec((tm, cb), lambda j, i: (i, j)),
        out_shape=jax.ShapeDtypeStruct((s, f), BF16),
    )(u, u, w, b)


def _convglu_bwd(u, da, w, b, name):
    _, s, f = u.shape
    tm = _tile(s, 256)
    cb = _tile(f, FFN_COLS)
    steps = s // tm
    n_ext = tm + 2 * HALO
    main, prev, nxt, wspec, bspec = _ffn_specs(s, tm, cb)
    hb = tm // HALO
    last = s // HALO - 1
    da_main = pl.BlockSpec((tm, cb), lambda j, i: (i, j))
    da_next = pl.BlockSpec((HALO, cb), lambda j, i: (jnp.minimum((i + 1) * hb, last), j))

    def body(u_ref, up_ref, un_ref, da_ref, dan_ref, w_ref, b_ref, du_ref, acc_ref):
        i = pl.program_id(1)
        first = jnp.where(i > 0, 1.0, 0.0)
        notlast = jnp.where(i < steps - 1, 1.0, 0.0)

        @pl.when(i == 0)
        def _():
            acc_ref[...] = jnp.zeros_like(acc_ref)

        def ext(g):
            return jnp.concatenate([up_ref[g].astype(F32) * first, u_ref[g].astype(F32), un_ref[g].astype(F32)], axis=0)

        ug, uv = ext(0), ext(1)
        gate = _conv_taps(ug, w_ref[0], b_ref[0])
        val = _conv_taps(uv, w_ref[1], b_ref[1])
        da_e = jnp.concatenate([jnp.zeros((HALO, cb), F32), da_ref[...].astype(F32),
                                dan_ref[...].astype(F32) * notlast], axis=0)
        sg = _sig(gate)
        d_val = da_e * gate * sg
        d_gate = da_e * val * (sg * (1.0 + gate * (1.0 - sg)))

        def finish(g, d, e):
            wv = w_ref[g]
            rows = slice(HALO, HALO + tm)
            d1, d2 = pltpu.roll(d, n_ext - 1, 0), pltpu.roll(d, n_ext - 2, 0)
            du_ref[g] = (wv[2:3] * d + wv[1:2] * d1 + wv[0:1] * d2)[rows].astype(BF16)
            em = e[rows]
            acc_ref[g, 2] += _colsum8(d[rows] * em)
            acc_ref[g, 1] += _colsum8(d1[rows] * em)
            acc_ref[g, 0] += _colsum8(d2[rows] * em)
            acc_ref[g, 3] += _colsum8(d[rows])

        finish(0, d_gate, ug)
        finish(1, d_val, uv)

    return pl.pallas_call(
        body, name=name, grid=(f // cb, steps),
        in_specs=[main, prev, nxt, da_main, da_next, wspec, bspec],
        out_specs=[main, pl.BlockSpec((2, 4, SUBLANES, cb), lambda j, i: (0, 0, 0, j))],
        out_shape=[jax.ShapeDtypeStruct((2, s, f), BF16), jax.ShapeDtypeStruct((2, 4, SUBLANES, f), F32)],
    )(u, u, u, da, da, w, b)


def _hgrn_gates(q_raw, f_raw, lb, tri):
    sf = _sig(f_raw)
    fg = lb + (1.0 - lb) * sf
    b = _tri_dot(tri, jnp.log(fg))
    return q_raw * _sig(q_raw), 1.0 - fg, b, fg, sf


def _hgrn_fwd(proj, lb, norm_g, name):
    s = proj.shape[0]
    tb = _tile(s, HGRN_ROWS)
    n_c = tb // A_CHUNK
    half = A_CHUNK // 2

    def body(q_ref, f_ref, v_ref, g_ref, lb_ref, ng_ref, o_ref, yp_ref, st_ref, state):
        @pl.when(pl.program_id(0) == 0)
        def _():
            state[...] = jnp.zeros_like(state)

        tri = _tri(A_CHUNK)
        causal = lax.broadcasted_iota(jnp.int32, (A_CHUNK, A_CHUNK), 1) <= lax.broadcasted_iota(
            jnp.int32, (A_CHUNK, A_CHUNK), 0)

        def chunk(ci, carry):
            rows = pl.ds(pl.multiple_of(ci * A_CHUNK, A_CHUNK), A_CHUNK)
            for h in range(HEADS):
                cs = slice(h * HEAD_DIM, (h + 1) * HEAD_DIM)
                qs, k, b, _, _ = _hgrn_gates(q_ref[rows, cs], f_ref[rows, cs], lb_ref[:, cs], tri)
                b_mid, b_last = b[half:half + 1], b[A_CHUNK - 1:A_CHUNK]
                vb = v_ref[rows, cs].astype(BF16)
                scores = _dot_nt((qs * jnp.exp(b - b_mid)).astype(BF16), (k * jnp.exp(b_mid - b)).astype(BF16))
                scores = jnp.where(causal, scores, 0.0)
                st = state[h]
                st_ref[ci, h] = st
                o = _dot(scores.astype(BF16), vb) + _dot_nt((qs * jnp.exp(b)).astype(BF16), st.astype(BF16))
                state[h] = st * jnp.exp(b_last) + _dot_tn(vb, (k * jnp.exp(b_last - b)).astype(BF16))
                o_ref[rows, cs] = o
                inv = lax.rsqrt(jnp.mean(o * o, axis=-1, keepdims=True) + EPS)
                g_raw = g_ref[rows, cs]
                yp_ref[rows, cs] = (o * inv * ng_ref[:, cs] * (g_raw * _sig(g_raw))).astype(BF16)
            return carry

        lax.fori_loop(0, n_c, chunk, 0)

    col = lambda j: pl.BlockSpec((tb, D_MODEL), lambda i: (i, j))
    vec = _full((1, D_MODEL))
    return pl.pallas_call(
        body, name=name, grid=(s // tb,), in_specs=[col(0), col(1), col(2), col(3), vec, vec],
        out_specs=[col(0), col(0), pl.BlockSpec((n_c, HEADS, HEAD_DIM, HEAD_DIM), lambda i: (i, 0, 0, 0))],
        out_shape=[jax.ShapeDtypeStruct((s, D_MODEL), F32), jax.ShapeDtypeStruct((s, D_MODEL), BF16),
                   jax.ShapeDtypeStruct((s // A_CHUNK, HEADS, HEAD_DIM, HEAD_DIM), F32)],
        scratch_shapes=[pltpu.VMEM((HEADS, HEAD_DIM, HEAD_DIM), F32)],
    )(proj, proj, proj, proj, lb, norm_g)


def _hgrn_bwd(proj, lb, norm_g, o, states, dyp, name):
    s = proj.shape[0]
    tb = _tile(s, HGRN_ROWS)
    n_c = tb // A_CHUNK
    n_b = s // tb
    half = A_CHUNK // 2

    def body(q_ref, f_ref, v_ref, g_ref, lb_ref, ng_ref, o_ref, st_ref, dyp_ref, dp_ref, dlb_ref, dng_ref, dstate):
        @pl.when(pl.program_id(0) == 0)
        def _():
            dstate[...] = jnp.zeros_like(dstate)
            dlb_ref[...] = jnp.zeros_like(dlb_ref)
            dng_ref[...] = jnp.zeros_like(dng_ref)

        tri = _tri(A_CHUNK)
        tri_up = _tri(A_CHUNK, upper=True)
        row_id = lax.broadcasted_iota(jnp.int32, (A_CHUNK, HEAD_DIM), 0)
        causal = lax.broadcasted_iota(jnp.int32, (A_CHUNK, A_CHUNK), 1) <= lax.broadcasted_iota(
            jnp.int32, (A_CHUNK, A_CHUNK), 0)

        def chunk(cj, carry):
            ci = n_c - 1 - cj
            rows = pl.ds(pl.multiple_of(ci * A_CHUNK, A_CHUNK), A_CHUNK)
            for h in range(HEADS):
                cs = slice(h * HEAD_DIM, (h + 1) * HEAD_DIM)
                q_raw, lbh = q_ref[rows, cs], lb_ref[:, cs]
                qs, k, b, fg, sf = _hgrn_gates(q_raw, f_ref[rows, cs], lbh, tri)
                b_mid, b_last = b[half:half + 1], b[A_CHUNK - 1:A_CHUNK]
                e_qi, e_ki, e_q, e_ks = jnp.exp(b - b_mid), jnp.exp(b_mid - b), jnp.exp(b), jnp.exp(b_last - b)
                q_i, k_i, q_e, k_s = qs * e_qi, k * e_ki, qs * e_q, k * e_ks
                vb = v_ref[rows, cs].astype(BF16)
                scores = jnp.where(causal, _dot_nt(q_i.astype(BF16), k_i.astype(BF16)), 0.0)
                ov, g_raw, dy, ng = o_ref[rows, cs], g_ref[rows, cs], dyp_ref[rows, cs], ng_ref[:, cs]
                inv = lax.rsqrt(jnp.mean(ov * ov, axis=-1, keepdims=True) + EPS)
                nrm = ov * inv
                sg = _sig(g_raw)
                gs = g_raw * sg
                dn = dy * ng * gs
                dng_ref[0:1, cs] += jnp.sum(dy * nrm * gs, axis=0, keepdims=True)
                dg_raw = dy * nrm * ng * (sg * (1.0 + g_raw * (1.0 - sg)))
                do = (inv * (dn - nrm * jnp.mean(dn * nrm, axis=-1, keepdims=True))).astype(BF16)
                st_prev = st_ref[ci, h]
                dst = dstate[h]
                dstb = dst.astype(BF16)
                d_scores = jnp.where(causal, _dot_nt(do, vb), 0.0).astype(BF16)
                dv = _dot_tn(scores.astype(BF16), do) + _dot_nt(k_s.astype(BF16), dstb)
                dq_i = _dot(d_scores, k_i.astype(BF16))
                dk_i = _dot_tn(d_scores, q_i.astype(BF16))
                dq_e = _dot(do, st_prev.astype(BF16))
                dk_s = _dot(vb, dstb)
                d_decay = jnp.sum(st_prev * dst, axis=0, keepdims=True)
                dstate[h] = dst * jnp.exp(b_last) + _dot_tn(do, q_e.astype(BF16))
                dq = dq_i * e_qi + dq_e * e_q
                dk = dk_i * e_ki + dk_s * e_ks
                t_qi, t_ki, t_ks = dq_i * q_i, dk_i * k_i, dk_s * k_s
                db = t_qi - t_ki + dq_e * q_e - t_ks
                db_mid = jnp.sum(t_ki - t_qi, axis=0, keepdims=True)
                db_last = jnp.sum(t_ks, axis=0, keepdims=True) + d_decay * jnp.exp(b_last)
                db = db + jnp.where(row_id == half, db_mid, 0.0) + jnp.where(row_id == A_CHUNK - 1, db_last, 0.0)
                dfg = _tri_dot(tri_up, db) / fg - dk
                dlb_ref[0:1, cs] += jnp.sum(dfg * (1.0 - sf), axis=0, keepdims=True)
                sq = _sig(q_raw)
                dp_ref[0, rows, cs] = (dq * (sq * (1.0 + q_raw * (1.0 - sq)))).astype(BF16)
                dp_ref[1, rows, cs] = (dfg * (1.0 - lbh) * sf * (1.0 - sf)).astype(BF16)
                dp_ref[2, rows, cs] = dv.astype(BF16)
                dp_ref[3, rows, cs] = dg_raw.astype(BF16)
            return carry

        lax.fori_loop(0, n_c, chunk, 0)

    col = lambda j: pl.BlockSpec((tb, D_MODEL), lambda i: (n_b - 1 - i, j))
    vec = _full((1, D_MODEL))
    acc = _full((SUBLANES, D_MODEL))
    return pl.pallas_call(
        body, name=name, grid=(n_b,),
        in_specs=[col(0), col(1), col(2), col(3), vec, vec, col(0),
                  pl.BlockSpec((n_c, HEADS, HEAD_DIM, HEAD_DIM), lambda i: (n_b - 1 - i, 0, 0, 0)), col(0)],
        out_specs=[pl.BlockSpec((4, tb, D_MODEL), lambda i: (0, n_b - 1 - i, 0)), acc, acc],
        out_shape=[jax.ShapeDtypeStruct((4, s, D_MODEL), BF16), jax.ShapeDtypeStruct((SUBLANES, D_MODEL), F32),
                   jax.ShapeDtypeStruct((SUBLANES, D_MODEL), F32)],
        scratch_shapes=[pltpu.VMEM((HEADS, HEAD_DIM, HEAD_DIM), F32)],
    )(proj, proj, proj, proj, lb, norm_g, o, states, dyp)


def _headnorm(x, g, mult, name, col0=0):
    s = x.shape[0]
    tm = _tile(s, ROW_TILE)

    def body(x_ref, g_ref, y_ref):
        for h in range(HEADS):
            cs = slice(h * HEAD_DIM, (h + 1) * HEAD_DIM)
            xv = x_ref[:, cs]
            inv = lax.rsqrt(jnp.mean(xv * xv, axis=-1, keepdims=True) + EPS)
            y_ref[:, cs] = (xv * inv * g_ref[:, cs] * mult).astype(BF16)

    return pl.pallas_call(
        body, name=name, grid=(s // tm,),
        in_specs=[pl.BlockSpec((tm, D_MODEL), lambda i: (i, col0)), _full((1, D_MODEL))],
        out_specs=pl.BlockSpec((tm, D_MODEL), lambda i: (i, 0)),
        out_shape=jax.ShapeDtypeStruct((s, D_MODEL), BF16),
    )(x, g)


def _headnorm_bwd(x, g, mult, dy, name, col0=0, extra=None):
    s = x.shape[0]
    tm = _tile(s, ROW_TILE)
    groups = 2 if extra is not None else 1
    head_major = dy.ndim == 3

    def body(*refs):
        x_ref, g_ref, dy_ref = refs[:3]
        dx_ref, dg_ref = refs[-2:]

        @pl.when(pl.program_id(0) == 0)
        def _():
            dg_ref[...] = jnp.zeros_like(dg_ref)

        for h in range(HEADS):
            cs = slice(h * HEAD_DIM, (h + 1) * HEAD_DIM)
            xv, gv = x_ref[:, cs], g_ref[:, cs]
            dyv = dy_ref[h, :, 0:HEAD_DIM] if head_major else dy_ref[:, cs]
            inv = lax.rsqrt(jnp.mean(xv * xv, axis=-1, keepdims=True) + EPS)
            nrm = xv * inv
            dn = dyv * gv * mult
            dg_ref[:, cs] += _colsum8(dyv * nrm * mult)
            dx_ref[0, :, cs] = (inv * (dn - nrm * jnp.mean(dn * nrm, axis=-1, keepdims=True))).astype(BF16)
        if extra is not None:
            dx_ref[1] = refs[3][...]

    row = pl.BlockSpec((tm, D_MODEL), lambda i: (i, 0))
    dy_spec = pl.BlockSpec((HEADS, tm, dy.shape[-1]), lambda i: (0, i, 0)) if head_major else row
    ins = [x, g, dy] + ([extra] if extra is not None else [])
    specs = ([pl.BlockSpec((tm, D_MODEL), lambda i: (i, col0)), _full((1, D_MODEL)), dy_spec]
             + ([row] if extra is not None else []))
    return pl.pallas_call(
        body, name=name, grid=(s // tm,), in_specs=specs,
        out_specs=[pl.BlockSpec((groups, tm, D_MODEL), lambda i: (0, i, 0)), _full((SUBLANES, D_MODEL))],
        out_shape=[jax.ShapeDtypeStruct((groups, s, D_MODEL), BF16), jax.ShapeDtypeStruct((SUBLANES, D_MODEL), F32)],
    )(*ins)


def _log_sigmoid(z):
    return jnp.minimum(z, 0.0) - jnp.log(1.0 + jnp.exp(-jnp.abs(z)))


Q_CUM, Q_ONE, Q_LSE = 0, 3, 6
LOG2E = 1.4426950408889634


def _pieces(v):
    hi = v.astype(BF16).astype(F32)
    mid = (v - hi).astype(BF16).astype(F32)
    lo = ((v - hi) - mid).astype(BF16).astype(F32)
    return hi, mid, lo


def _side(lane, at, v):
    hi, mid, lo = _pieces(v)
    return jnp.where(lane == at, hi, jnp.where(lane == at + 1, mid, jnp.where(lane == at + 2, lo, 0.0)))


def _fcum_fwd(f, bias, name):
    s = f.shape[0]
    tm = _tile(s, ROW_TILE)

    def body(f_ref, b_ref, qa_ref, ka_ref, carry):
        @pl.when(pl.program_id(0) == 0)
        def _():
            carry[...] = jnp.zeros_like(carry)

        cum = _tri_dot(_tri(tm), _log_sigmoid(f_ref[...] + b_ref[...])) + carry[...]
        carry[...] = cum[tm - 1:tm]
        lane = lax.broadcasted_iota(jnp.int32, (tm, LANES), 1)
        ones_q = jnp.where((lane >= Q_ONE) & (lane < Q_LSE), 1.0, 0.0)
        ones_k = jnp.where((lane < Q_ONE) | ((lane >= Q_LSE) & (lane < Q_LSE + 3)), 1.0, 0.0)
        for h in range(HEADS):
            c2 = cum[:, h:h + 1] * LOG2E
            qa_ref[h] = (_side(lane, Q_CUM, c2) + ones_q).astype(BF16)
            ka_ref[h] = (_side(lane, Q_ONE, -c2) + ones_k).astype(BF16)

    side = pl.BlockSpec((HEADS, tm, LANES), lambda i: (0, i, 0))
    return pl.pallas_call(
        body, name=name, grid=(s // tm,),
        in_specs=[pl.BlockSpec((tm, LANES), lambda i: (i, 0)), _full((1, LANES))],
        out_specs=[side, side],
        out_shape=[jax.ShapeDtypeStruct((HEADS, s, LANES), BF16)] * 2,
        scratch_shapes=[pltpu.VMEM((1, LANES), F32)],
    )(f, bias)


def _fcum_bwd(f, bias, dka, dq, name):
    s = f.shape[0]
    tm = _tile(s, ROW_TILE)
    n_b = s // tm
    q_lane = HEAD_DIM + Q_CUM

    def body(f_ref, b_ref, dka_ref, dqa_ref, dz_ref, db_ref, carry):
        @pl.when(pl.program_id(0) == 0)
        def _():
            carry[...] = jnp.zeros_like(carry)
            db_ref[...] = jnp.zeros_like(db_ref)

        lane = lax.broadcasted_iota(jnp.int32, (tm, LANES), 1)
        dcum = jnp.zeros((tm, LANES), F32)
        for h in range(HEADS):
            dcum = dcum + jnp.where(lane == h, dqa_ref[h, :, q_lane:q_lane + 1] - dka_ref[h, :, Q_ONE:Q_ONE + 1], 0.0)
        dlf = _tri_dot(_tri(tm, upper=True), dcum) + carry[...]
        carry[...] = dlf[0:1]
        dz = dlf * _sig(-(f_ref[...] + b_ref[...]))
        dz_ref[0] = dz.astype(BF16)
        db_ref[...] += _colsum8(dz)

    return pl.pallas_call(
        body, name=name, grid=(n_b,),
        in_specs=[pl.BlockSpec((tm, LANES), lambda i: (n_b - 1 - i, 0)), _full((1, LANES)),
                  pl.BlockSpec((HEADS, tm, LANES), lambda i: (0, n_b - 1 - i, 0)),
                  pl.BlockSpec((HEADS, tm, 2 * HEAD_DIM), lambda i: (0, n_b - 1 - i, 0))],
        out_specs=[pl.BlockSpec((1, tm, LANES), lambda i: (0, n_b - 1 - i, 0)), _full((SUBLANES, LANES))],
        out_shape=[jax.ShapeDtypeStruct((1, s, LANES), BF16), jax.ShapeDtypeStruct((SUBLANES, LANES), F32)],
        scratch_shapes=[pltpu.VMEM((1, LANES), F32)],
    )(f, bias, dka, dq)


def _causal_pairs(n_t, key_major):
    if key_major:
        pairs = [(qi, ki) for ki in range(n_t) for qi in range(ki, n_t)]
    else:
        pairs = [(qi, ki) for qi in range(n_t) for ki in range(qi + 1)]
    return (jnp.array([p[0] for p in pairs], jnp.int32), jnp.array([p[1] for p in pairs], jnp.int32))


def _with_side(main_ref, side_ref):
    return jnp.concatenate([main_ref[...], side_ref[...]], axis=1)


def _lane_const(t, lo, hi, value):
    lane = lax.broadcasted_iota(jnp.int32, (t, LANES), 1)
    return jnp.where((lane >= lo) & (lane < hi), value, 0.0).astype(BF16)


def _att_specs(t):
    qmain = pl.BlockSpec((t, HEAD_DIM), lambda h, p, qt, kt: (qt[p], h))
    kmain = pl.BlockSpec((t, HEAD_DIM), lambda h, p, qt, kt: (kt[p], h))
    qside = pl.BlockSpec((None, t, LANES), lambda h, p, qt, kt: (h, qt[p], 0))
    kside = pl.BlockSpec((None, t, LANES), lambda h, p, qt, kt: (h, kt[p], 0))
    return qmain, kmain, qside, kside


def _fox_fwd(q, qa, k, ka, v, qo, name):
    s = q.shape[0]
    t = _tile(s, ATT_TILE)
    sub = t // ATT_SPLIT
    qt, kt = _causal_pairs(s // t, key_major=False)

    def body(qt_ref, kt_ref, q_ref, qa_ref, k_ref, ka_ref, v_ref, og_ref, o_ref, y_ref, qab_ref, m_s, l_s, acc_s):
        pid = pl.program_id(1)
        qi, ki = qt_ref[pid], kt_ref[pid]

        @pl.when(ki == 0)
        def _():
            m_s[...] = jnp.full_like(m_s, NEG_INF)
            l_s[...] = jnp.zeros_like(l_s)
            acc_s[...] = jnp.zeros_like(acc_s)

        def step(diagonal):
            kc = _with_side(k_ref, ka_ref)
            vc = jnp.concatenate([v_ref[...], _lane_const(t, 0, 1, 1.0)], axis=1)
            for r in range(ATT_SPLIT):
                rows = slice(r * sub, (r + 1) * sub)
                n_k = (r + 1) * sub if diagonal else t
                sc = _dot_nt(jnp.concatenate([q_ref[rows], qa_ref[rows]], axis=1), kc[:n_k])
                if diagonal:
                    sc = jnp.where(lax.broadcasted_iota(jnp.int32, (sub, n_k), 1)
                                   <= lax.broadcasted_iota(jnp.int32, (sub, n_k), 0) + r * sub, sc, NEG_INF)
                m_old = m_s[rows]
                m_new = jnp.maximum(m_old, jnp.max(sc, axis=-1, keepdims=True))
                alpha = jnp.exp2(m_old - m_new)
                pv = _dot(jnp.exp2(sc - m_new[:, 0:1]).astype(BF16), vc[:n_k])
                acc_s[rows] = alpha * acc_s[rows] + pv[:, :HEAD_DIM]
                l_s[rows] = alpha * l_s[rows] + pv[:, HEAD_DIM:]
                m_s[rows] = m_new

        @pl.when(ki < qi)
        def _():
            step(False)

        @pl.when(ki == qi)
        def _():
            step(True)
            l = l_s[:, 0:1]
            o = acc_s[...] / l
            o_ref[...] = o
            y_ref[...] = (o * _sig(og_ref[...])).astype(BF16)
            lane = lax.broadcasted_iota(jnp.int32, (t, LANES), 1)
            qab_ref[...] = qa_ref[...] + _side(lane, Q_LSE, -(m_s[:, 0:1] + jnp.log2(l))).astype(BF16)

    qmain, kmain, qside, kside = _att_specs(t)
    return pl.pallas_call(
        body, name=name,
        grid_spec=pltpu.PrefetchScalarGridSpec(
            num_scalar_prefetch=2, grid=(HEADS, qt.shape[0]),
            in_specs=[qmain, qside, kmain, kside, kmain,
                      pl.BlockSpec((t, HEAD_DIM), lambda h, p, qt, kt: (qt[p], HEADS + h))],
            out_specs=[qmain, qmain, qside],
            scratch_shapes=[pltpu.VMEM((t, LANES), F32), pltpu.VMEM((t, LANES), F32), pltpu.VMEM((t, HEAD_DIM), F32)]),
        out_shape=[jax.ShapeDtypeStruct((s, D_MODEL), F32), jax.ShapeDtypeStruct((s, D_MODEL), BF16),
                   jax.ShapeDtypeStruct((HEADS, s, LANES), BF16)],
    )(qt, kt, q, qa, k, ka, v, qo)


def _fox_gate_bwd(o, qo, dy, name):
    s = o.shape[0]
    tm = _tile(s, ROW_TILE)

    def body(o_ref, og_ref, dy_ref, do_ref, dg_ref, dl_ref):
        ov, dyv = o_ref[...], dy_ref[...]
        sg = _sig(og_ref[...])
        do = (dyv * sg).astype(BF16)
        do_ref[...] = do
        dg_ref[...] = (dyv * ov * sg * (1.0 - sg)).astype(BF16)
        prod = do.astype(F32) * ov
        lane = lax.broadcasted_iota(jnp.int32, (tm, LANES), 1)
        for h in range(HEADS):
            delta = jnp.sum(prod[:, h * HEAD_DIM:(h + 1) * HEAD_DIM], axis=-1, keepdims=True)
            dl_ref[h] = _side(lane, 0, delta).astype(BF16)

    row = pl.BlockSpec((tm, D_MODEL), lambda i: (i, 0))
    return pl.pallas_call(
        body, name=name, grid=(s // tm,),
        in_specs=[row, pl.BlockSpec((tm, D_MODEL), lambda i: (i, 1)), row],
        out_specs=[row, row, pl.BlockSpec((HEADS, tm, LANES), lambda i: (0, i, 0))],
        out_shape=[jax.ShapeDtypeStruct((s, D_MODEL), BF16), jax.ShapeDtypeStruct((s, D_MODEL), BF16),
                   jax.ShapeDtypeStruct((HEADS, s, LANES), BF16)],
    )(o, qo, dy)


def _fox_bwd(q, qab, k, ka, v, do, doa, name):
    s = q.shape[0]
    t = _tile(s, ATT_TILE)
    n_t = s // t
    sub = t // ATT_SPLIT
    qt, kt = _causal_pairs(n_t, key_major=True)

    def body(qt_ref, kt_ref, q_ref, qab_ref, k_ref, ka_ref, v_ref, do_ref, doa_ref, dk_ref, dv_ref, dka_ref, dq_ref,
             dk_s, dv_s):
        pid = pl.program_id(1)
        qi, ki = qt_ref[pid], kt_ref[pid]

        @pl.when(pid == 0)
        def _():
            dq_ref[...] = jnp.zeros_like(dq_ref)

        @pl.when(qi == ki)
        def _():
            dk_s[...] = jnp.zeros_like(dk_s)
            dv_s[...] = jnp.zeros_like(dv_s)

        def step(diagonal):
            kc = _with_side(k_ref, ka_ref)
            vc = jnp.concatenate([v_ref[...], _lane_const(t, 0, 3, -1.0)], axis=1)
            for r in range(ATT_SPLIT):
                cols = slice(r * sub, (r + 1) * sub)
                n_k = (r + 1) * sub if diagonal else t
                qc = jnp.concatenate([q_ref[cols], qab_ref[cols]], axis=1)
                sc = _dot_nt(kc[:n_k], qc)
                if diagonal:
                    sc = jnp.where(lax.broadcasted_iota(jnp.int32, (n_k, sub), 0)
                                   <= lax.broadcasted_iota(jnp.int32, (n_k, sub), 1) + r * sub, sc, NEG_INF)
                p = jnp.exp2(sc)
                dp = _dot_nt(vc[:n_k], jnp.concatenate([do_ref[cols], doa_ref[cols]], axis=1))
                ds = (p * dp).astype(BF16)
                dv_s[0:n_k] += _dot(p.astype(BF16), do_ref[cols])
                dk_s[0:n_k] += _dot(ds, qc)
                q_rows = pl.ds(pl.multiple_of(qi * t + r * sub, sub), sub)
                dq_ref[q_rows, :] += _dot_tn(ds, kc[:n_k])

        @pl.when(qi > ki)
        def _():
            step(False)

        @pl.when(qi == ki)
        def _():
            step(True)

        @pl.when(qi == n_t - 1)
        def _():
            dk_ref[...] = dk_s[:, :HEAD_DIM] * (1.0 / LOG2E)
            dka_ref[...] = dk_s[:, HEAD_DIM:]
            dv_ref[...] = dv_s[...].astype(BF16)

    qmain, kmain, qside, kside = _att_specs(t)
    return pl.pallas_call(
        body, name=name,
        grid_spec=pltpu.PrefetchScalarGridSpec(
            num_scalar_prefetch=2, grid=(HEADS, qt.shape[0]),
            in_specs=[qmain, qside, kmain, kside, kmain, qmain, qside],
            out_specs=[kmain, pl.BlockSpec((None, t, HEAD_DIM), lambda h, p, qt, kt: (0, kt[p], h)), kside,
                       pl.BlockSpec((None, s, 2 * HEAD_DIM), lambda h, p, qt, kt: (h, 0, 0))],
            scratch_shapes=[pltpu.VMEM((t, 2 * HEAD_DIM), F32), pltpu.VMEM((t, HEAD_DIM), F32)]),
        out_shape=[jax.ShapeDtypeStruct((s, D_MODEL), F32), jax.ShapeDtypeStruct((1, s, D_MODEL), BF16),
                   jax.ShapeDtypeStruct((HEADS, s, LANES), F32), jax.ShapeDtypeStruct((HEADS, s, 2 * HEAD_DIM), F32)],
    )(qt, kt, q, qab, k, ka, v, do, doa)


def _ffn_forward(x_in, branch, gate, shift, scale, w_up, conv_w, conv_b, w_down, tag):
    x_mid, h = _premix(x_in, shift, scale, tag + "_premix", branch=branch, gate=gate)
    u = _mm_nn(h, w_up, 2, BF16, tag + "_up")
    a = _convglu_fwd(u, conv_w, conv_b, tag + "_convglu")
    ffn = _mm_nn(a, w_down, 1, F32, tag + "_down")[0]
    return x_mid, ffn, (h, u, a)


def _weight_grad_first(a, d, p_n, name):
    return lax.optimization_barrier((_mm_tn(a, d, p_n, name), d))


def _ffn_backward(dx_out, dffn, x_mid, scale, saved, w_up, conv_w, conv_b, w_down, mixer, tag):
    h, u, a = saved
    dw_down, dffn = _weight_grad_first(a, dffn, 1, tag + "_down_dw")
    da = _mm_nt(dffn, w_down, BF16, tag + "_down_dx")
    du, dconv = _convglu_bwd(u, da, conv_w, conv_b, tag + "_convglu_bwd")
    dw_up, du = _weight_grad_first(h, du, N_CHIPS, tag + "_up_dw")
    dx_mid, [(dshift, dscale)], dy, dgate_mixer = _premix_bwd(x_mid, [(scale, [(du, w_up)])], dx_out,
                                                              tag + "_premix_bwd", branch=mixer)
    return dx_mid, dy, dgate_mixer, dw_up, dw_down, dict(shift=dshift, scale=dscale, conv=dconv)


def _local_step(x, target, mods, lb, vecs, weights_at):
    m0, m1, mk = mods["l0"], mods["l1"], mods["kv"]
    h0 = _premix(x, m0[0], m0[1], "l0_premix")
    wts, h0 = weights_at("mixer0", h0)
    proj = _mm_nn(h0, wts["a_w_in"], 1, F32, "l0_in")[0]
    o_a, yp, states = _hgrn_fwd(proj, lb, vecs["a_norm_g"], "l0_hgrn")
    more, yp = weights_at("ffn0", yp)
    wts.update(more)
    y0 = _mm_nn(yp, wts["a_w_out"], 1, F32, "l0_out")[0]
    x1, ffn0, saved0 = _ffn_forward(x, y0, m0[2], m0[3], m0[4], wts["up0"], vecs["conv_w0"], vecs["conv_b0"],
                                    wts["down0"], "l0_ffn")
    more, ffn0 = weights_at("layer1", ffn0)
    wts.update(more)
    x2, hk = _premix(x1, mk[0], mk[1], "kv_premix", branch=ffn0, gate=m0[5])
    k_raw = _mm_nn(hk, wts["kv_k"], 1, F32, "kv_k")[0]
    v_sh = _mm_nn(hk, wts["kv_v"], 1, BF16, "kv_v")[0]
    f_raw = _mm_nn(hk, wts["kv_f"], 1, F32, "kv_f")[0]
    k_sh = _headnorm(k_raw, vecs["k_norm_g"], 1.0, "kv_knorm")
    qa, ka = _fcum_fwd(f_raw, vecs["kv_b_f"], "kv_fcum")
    h1 = _premix(x2, m1[0], m1[1], "l1_premix")
    qo = _mm_nn(h1, wts["b_w_q"], 1, F32, "l1_q")[0]
    q_scale = HEAD_DIM ** -0.5
    q = _headnorm(qo, vecs["q_norm_g"], q_scale * LOG2E, "l1_qnorm")
    o_b, og, qab = _fox_fwd(q, qa, k_sh, ka, v_sh, qo, "l1_fox")
    y1 = _mm_nn(og, wts["b_w_out"], 1, F32, "l1_out")[0]
    x3, ffn1, saved1 = _ffn_forward(x2, y1, m1[2], m1[3], m1[4], wts["up1"], vecs["conv_w1"], vecs["conv_b1"],
                                    wts["down1"], "l1_ffn")
    sq, dx4, dffn1, dg2_1 = _loss_head(x3, ffn1, m1[5], target, "loss_head")

    big, small = {}, {}
    dx3, dy1, dg1_1, big["up1"], big["down1"], s_ffn1 = _ffn_backward(
        dx4, dffn1, x3, m1[4], saved1, wts["up1"], vecs["conv_w1"], vecs["conv_b1"], wts["down1"], (y1, m1[2]), "l1_ffn")
    big["b_w_out"], dy1 = _weight_grad_first(og, dy1, 1, "l1_out_dw")
    d_og = _mm_nt(dy1, wts["b_w_out"], F32, "l1_out_dx")
    do_b, dgate_b, doa = _fox_gate_bwd(o_b, qo, d_og, "l1_fox_gate_bwd")
    dk, dv, dka, dq = _fox_bwd(q, qab, k_sh, ka, v_sh, do_b, doa, "l1_fox_bwd")
    dqo, dqg = _headnorm_bwd(qo, vecs["q_norm_g"], q_scale, dq, "l1_qnorm_bwd", extra=dgate_b)
    big["b_w_q"], dqo = _weight_grad_first(h1, dqo, N_CHIPS, "l1_q_dw")
    dk_raw, dkg = _headnorm_bwd(k_raw, vecs["k_norm_g"], 1.0, dk, "kv_knorm_bwd")
    dz, dbf = _fcum_bwd(f_raw, vecs["kv_b_f"], dka, dq, "kv_fcum_bwd")
    big["kv_k"], dk_raw = _weight_grad_first(hk, dk_raw, 1, "kv_k_dw")
    big["kv_v"], dv = _weight_grad_first(hk, dv, 1, "kv_v_dw")
    big["kv_f"], dz = _weight_grad_first(hk, dz, 1, "kv_f_dw")
    kv_pairs = [(dk_raw, wts["kv_k"]), (dv, wts["kv_v"]), (dz, wts["kv_f"])]
    dx2, [(dsh1_1, dsc1_1), (dshk, dsck)], dffn0, dg2_0 = _premix_bwd(
        x2, [(m1[1], [(dqo, wts["b_w_q"])]), (mk[1], kv_pairs)], dx3, "l1_kv_premix_bwd", branch=(ffn0, m0[5]))
    dx1, dy0, dg1_0, big["up0"], big["down0"], s_ffn0 = _ffn_backward(
        dx2, dffn0, x1, m0[4], saved0, wts["up0"], vecs["conv_w0"], vecs["conv_b0"], wts["down0"], (y0, m0[2]), "l0_ffn")
    big["a_w_out"], dy0 = _weight_grad_first(yp, dy0, 1, "l0_out_dw")
    dyp = _mm_nt(dy0, wts["a_w_out"], F32, "l0_out_dx")
    dproj, dlb, dng = _hgrn_bwd(proj, lb, vecs["a_norm_g"], o_a, states, dyp, "l0_hgrn_bwd")
    big["a_w_in"], dproj = _weight_grad_first(h0, dproj, N_CHIPS, "l0_in_dw")
    grad_x, [(dsh1_0, dsc1_0)] = _premix_bwd(x, [(m0[1], [(dproj, wts["a_w_in"])])], dx1, "l0_premix_bwd")

    small["mod_l0"] = [dsh1_0, dsc1_0, dg1_0, s_ffn0["shift"], s_ffn0["scale"], dg2_0]
    small["mod_l1"] = [dsh1_1, dsc1_1, dg1_1, s_ffn1["shift"], s_ffn1["scale"], dg2_1]
    small["mod_kv"] = [dshk, dsck]
    small["conv0"], small["conv1"] = s_ffn0["conv"], s_ffn1["conv"]
    small["a_norm_g"], small["k_norm_g"], small["q_norm_g"] = dng, dkg, dqg
    small["kv_b_f"], small["lb"] = dbf, dlb
    marks = {"attention_bwd": dk, "ffn0_bwd": dx1, "mixer0_bwd": grad_x}
    return sq, grad_x, big, small, marks


HBM = pl.BlockSpec(memory_space=pltpu.HBM)
COMM_CHUNK_ELEMS = 256 * 1024


def _place():
    x, y, c = lax.axis_index("x"), lax.axis_index("y"), lax.axis_index("c")
    chips = [(1 - x, y), (x, 1 - y), (1 - x, 1 - y)]
    return x, y, c, (x, y, 1 - c), chips


def _chunk_rows(rows, cols):
    best = BF16_ROWS
    for r in range(BF16_ROWS, rows + 1, BF16_ROWS):
        if rows % r == 0 and r * cols <= COMM_CHUNK_ELEMS:
            best = r
    assert rows % best == 0, (rows, cols)
    return best


def _allgather8(block, name):
    m_per, n = block.shape

    def body(x_ref, out_ref, send_sems, recv_sems, local_sem):
        x, y, c, sibling, chips = _place()
        me = (x, y, c)

        def rows(px, py, pc):
            return out_ref.at[pl.ds((4 * px + 2 * py + pc) * m_per, m_per), :]

        def copy(k, blk, to, src=None):
            return pltpu.make_async_remote_copy(
                src_ref=rows(*blk) if src is None else src, dst_ref=rows(*blk),
                send_sem=send_sems.at[k], recv_sem=recv_sems.at[k], device_id=to, device_id_type=MESH)

        mine = pltpu.make_async_copy(x_ref, rows(*me), local_sem)
        mine.start()
        first = [copy(0, me, sibling, src=x_ref)]
        first += [copy(1 + j, me, (*chip, c), src=x_ref) for j, chip in enumerate(chips)]
        for cp in first:
            cp.start()
        passed = [copy(4 + j, (*chip, c), sibling) for j, chip in enumerate(chips)]
        for j, chip in enumerate(chips):
            copy(1 + j, (*chip, c), me).wait_recv()
            passed[j].start()
        copy(0, sibling, me).wait_recv()
        for j, chip in enumerate(chips):
            copy(4 + j, (*chip, 1 - c), me).wait_recv()
        for cp in first + passed:
            cp.wait_send()
        mine.wait()

    return pl.pallas_call(
        body, name=name, out_shape=jax.ShapeDtypeStruct((N_DEV * m_per, n), block.dtype),
        in_specs=[pl.BlockSpec(memory_space=pltpu.VMEM)], out_specs=pl.BlockSpec(memory_space=pltpu.VMEM),
        scratch_shapes=[pltpu.SemaphoreType.DMA((7,)), pltpu.SemaphoreType.DMA((7,)), pltpu.SemaphoreType.DMA],
    )(block)


def _cast_own_block(shards, layer, chip, name):
    _, r, cols = shards.shape
    rows = _chunk_rows(r, cols)

    def body(chip_ref, w_ref, o_ref):
        o_ref[...] = w_ref[...].astype(BF16)

    return pl.pallas_call(
        body, name=name,
        grid_spec=pltpu.PrefetchScalarGridSpec(
            num_scalar_prefetch=1, grid=(r // rows,),
            in_specs=[pl.BlockSpec((None, rows, cols), lambda i, chip_ref: (layer, i, 0))],
            out_specs=pl.BlockSpec((None, rows, cols), lambda i, chip_ref: (chip_ref[0], i, 0))),
        out_shape=jax.ShapeDtypeStruct((N_CHIPS, r, cols), BF16),
    )(chip, shards)


def _sequencer_gather(bufs, name, collective_id):
    n_t = len(bufs)
    dims = [b.shape[1:] for b in bufs]
    refs = [jax.new_ref(b, memory_space=pltpu.MemorySpace.HBM) for b in bufs]

    @pl.kernel(mesh=plsc.ScalarSubcoreMesh(axis_name="sequencer", num_cores=1), name=name,
               scratch_types=[pltpu.SemaphoreType.DMA((n_t,))] * 4,
               compiler_params=pltpu.CompilerParams(collective_id=collective_id))
    def launch(send_ici, recv_ici, send_d2d, recv_d2d):
        x, y, c, sibling, chips = _place()
        p_me = 2 * x + y
        peers = [sibling] + [(cx, cy, c) for cx, cy in chips]
        barrier = pltpu.get_barrier_semaphore()
        for peer in peers:
            pl.semaphore_signal(barrier, inc=1, device_id=peer, device_id_type=MESH)
        pl.semaphore_wait(barrier, len(peers))

        def waiter(t, sem_s, sem_r):
            win = refs[t].at[pl.ds(0, 3), pl.ds(0, dims[t][0] // 2), :]
            return pltpu.make_async_remote_copy(src_ref=win, dst_ref=win, send_sem=sem_s.at[t], recv_sem=sem_r.at[t],
                                                device_id=sibling, device_id_type=MESH)

        def half_copy(t, chip_idx, to, sem_s, sem_r):
            r2 = dims[t][0] // 2
            win = refs[t].at[chip_idx, pl.ds(c * r2, r2), :]
            return pltpu.make_async_remote_copy(src_ref=win, dst_ref=win, send_sem=sem_s.at[t], recv_sem=sem_r.at[t],
                                                device_id=to, device_id_type=MESH)

        for t in range(n_t):
            for cx, cy in chips:
                half_copy(t, p_me, (cx, cy, c), send_ici, recv_ici).start()
        for t in range(n_t):
            waiter(t, send_ici, recv_ici).wait_recv()
            for cx, cy in chips:
                half_copy(t, 2 * cx + cy, sibling, send_d2d, recv_d2d).start()
        for t in range(n_t):
            waiter(t, send_d2d, recv_d2d).wait_recv()
            waiter(t, send_ici, recv_ici).wait_send()
            waiter(t, send_d2d, recv_d2d).wait_send()

    launch()
    return [r[...] for r in refs]


def _sequencer_allgather8(block, dev, name, collective_id):
    m_per, n = block.shape
    src = jax.new_ref(block, memory_space=pltpu.MemorySpace.HBM)
    out = jax.empty_ref(jax.ShapeDtypeStruct((N_DEV * m_per, n), block.dtype), memory_space=pltpu.MemorySpace.HBM)

    @pl.kernel(mesh=plsc.ScalarSubcoreMesh(axis_name="sequencer", num_cores=1), name=name,
               scratch_types=[pltpu.SemaphoreType.DMA((7,))] * 2,
               compiler_params=pltpu.CompilerParams(collective_id=collective_id))
    def launch(send_sems, recv_sems):
        x, y, c, sibling, chips = _place()
        me = (x, y, c)
        _handshake([sibling] + [(cx, cy, c) for cx, cy in chips])

        def rows(px, py, pc):
            return out.at[pl.ds((4 * px + 2 * py + pc) * m_per, m_per), :]

        def copy(k, blk, to, from_src=False):
            return pltpu.make_async_remote_copy(
                src_ref=src if from_src else rows(*blk), dst_ref=rows(*blk),
                send_sem=send_sems.at[k], recv_sem=recv_sems.at[k], device_id=to, device_id_type=MESH)

        first = [copy(0, me, sibling, True)] + [copy(1 + j, me, (*chip, c), True) for j, chip in enumerate(chips)]
        for cp in first:
            cp.start()
        passed = [copy(4 + j, (*chip, c), sibling) for j, chip in enumerate(chips)]
        for j, chip in enumerate(chips):
            copy(1 + j, (*chip, c), me).wait_recv()
            passed[j].start()
        copy(0, sibling, me).wait_recv()
        for j, chip in enumerate(chips):
            copy(4 + j, (*chip, 1 - c), me).wait_recv()
        for cp in first + passed:
            cp.wait_send()

    launch()
    return lax.dynamic_update_slice(out[...], block, (dev * m_per, 0))


def _others():
    x, y, c = lax.axis_index("x"), lax.axis_index("y"), lax.axis_index("c")
    flip = lambda v, f: 1 - v if f else v
    return [(flip(x, fx), flip(y, fy), flip(c, fc))
            for fx in (0, 1) for fy in (0, 1) for fc in (0, 1) if (fx, fy, fc) != (0, 0, 0)]


def _handshake(peers):
    barrier = pltpu.get_barrier_semaphore()
    for peer in peers:
        pl.semaphore_signal(barrier, inc=1, device_id=peer, device_id_type=MESH)
    pl.semaphore_wait(barrier, len(peers))


def _sequencer_scatter(parts, name, collective_id):
    n_t = len(parts)
    dims = [p.shape[1:] for p in parts]
    srcs = [jax.new_ref(p, memory_space=pltpu.MemorySpace.HBM) for p in parts]
    inboxes = [jax.empty_ref(jax.ShapeDtypeStruct((N_DEV, r // 2, cols), BF16), memory_space=pltpu.MemorySpace.HBM)
               for r, cols in dims]

    @pl.kernel(mesh=plsc.ScalarSubcoreMesh(axis_name="sequencer", num_cores=1), name=name,
               scratch_types=[pltpu.SemaphoreType.DMA((n_t,))] * 2,
               compiler_params=pltpu.CompilerParams(collective_id=collective_id))
    def launch(send_sem, recv_sem):
        x, y, c = lax.axis_index("x"), lax.axis_index("y"), lax.axis_index("c")
        me = 4 * x + 2 * y + c
        peers = _others()
        _handshake(peers)
        for t in range(n_t):
            h = dims[t][0] // 2
            for qx, qy, qc in peers:
                pltpu.make_async_remote_copy(
                    src_ref=srcs[t].at[2 * qx + qy, pl.ds(qc * h, h), :], dst_ref=inboxes[t].at[me],
                    send_sem=send_sem.at[t], recv_sem=recv_sem.at[t], device_id=(qx, qy, qc), device_id_type=MESH).start()
        for t in range(n_t):
            win = inboxes[t].at[pl.ds(0, N_DEV - 1)]
            both = pltpu.make_async_remote_copy(src_ref=win, dst_ref=win, send_sem=send_sem.at[t],
                                                recv_sem=recv_sem.at[t], device_id=peers[0], device_id_type=MESH)
            both.wait_recv()
            both.wait_send()

    launch()
    return [b[...] for b in inboxes]


def _sum_pieces(part, inbox, place, name):
    _, r, cols = part.shape
    h = r // 2
    rows = _chunk_rows(h, cols)
    steps = h // rows

    def body(place_ref, own_ref, in_ref, o_ref):
        dev = place_ref[2]
        own = own_ref[...].astype(F32)
        acc = jnp.zeros((rows, cols), F32)
        for d in range(N_DEV):
            acc = acc + jnp.where(dev == d, own, in_ref[d].astype(F32))
        o_ref[...] = acc

    return pl.pallas_call(
        body, name=name,
        grid_spec=pltpu.PrefetchScalarGridSpec(
            num_scalar_prefetch=1, grid=(steps,),
            in_specs=[pl.BlockSpec((None, rows, cols), lambda i, pr: (pr[0], pr[1] * steps + i, 0)),
                      pl.BlockSpec((N_DEV, rows, cols), lambda i, pr: (0, i, 0))],
            out_specs=pl.BlockSpec((rows, cols), lambda i, pr: (pr[1] * steps + i, 0))),
        out_shape=jax.ShapeDtypeStruct((r, cols), F32),
    )(place, part, inbox)


def _sequencer_swap_halves(halves, name, collective_id):
    n_t = len(halves)
    refs = [jax.new_ref(a, memory_space=pltpu.MemorySpace.HBM) for a in halves]

    @pl.kernel(mesh=plsc.ScalarSubcoreMesh(axis_name="sequencer", num_cores=1), name=name,
               scratch_types=[pltpu.SemaphoreType.DMA((n_t,))] * 2,
               compiler_params=pltpu.CompilerParams(collective_id=collective_id))
    def launch(send_sem, recv_sem):
        x, y, c = lax.axis_index("x"), lax.axis_index("y"), lax.axis_index("c")
        sibling = (x, y, 1 - c)
        _handshake([sibling])
        copies = []
        for t in range(n_t):
            h = halves[t].shape[0] // 2
            win = refs[t].at[pl.ds(c * h, h), :]
            copies.append(pltpu.make_async_remote_copy(src_ref=win, dst_ref=win, send_sem=send_sem.at[t],
                                                       recv_sem=recv_sem.at[t], device_id=sibling, device_id_type=MESH))
            copies[-1].start()
        for cp in copies:
            cp.wait()

    launch()
    return [r[...] for r in refs]


def _cond_rows(c16, w, act, name):
    n_l, dm, wid = w.shape

    def body(c_ref, w_ref, o_ref, a_ref):
        cv = c_ref[...]
        if act:
            cv = cv * _sig(cv)
        a_ref[...] = cv
        o_ref[...] = _dot_f32(cv, w_ref[...])

    return pl.pallas_call(
        body, name=name, grid=(n_l,),
        in_specs=[_full((16, dm)), pl.BlockSpec((None, dm, wid), lambda l: (l, 0, 0))],
        out_specs=[pl.BlockSpec((None, 16, wid), lambda l: (l, 0, 0)), _full((16, dm))],
        out_shape=[jax.ShapeDtypeStruct((n_l, 16, wid), F32), jax.ShapeDtypeStruct((16, dm), F32)],
    )(c16, w)


def _outer_grad(ct, dm, name):
    n_l, kk, wid = dm.shape
    d_rows = ct.shape[0]

    def body(c_ref, d_ref, o_ref):
        o_ref[...] = _dot_f32(c_ref[...], d_ref[...])

    return pl.pallas_call(
        body, name=name, grid=(n_l,),
        in_specs=[_full((d_rows, kk)), pl.BlockSpec((None, kk, wid), lambda l: (l, 0, 0))],
        out_specs=pl.BlockSpec((None, d_rows, wid), lambda l: (l, 0, 0)),
        out_shape=jax.ShapeDtypeStruct((n_l, d_rows, wid), F32),
    )(ct, dm)


def _sum_devices(g, name):
    rows, n = g.shape

    def body(g_ref, o_ref):
        acc = g_ref[0:SUBLANES, :]
        for dev in range(1, N_DEV):
            acc = acc + g_ref[dev * SUBLANES:(dev + 1) * SUBLANES, :]
        o_ref[...] = acc

    return pl.pallas_call(body, name=name, out_shape=jax.ShapeDtypeStruct((SUBLANES, n), F32))(g)


def _adamw(w, g, m, v, name):
    shape = w.shape
    cols = shape[-1]
    rows = w.size // cols
    tr = rows
    for cand in range(SUBLANES, min(rows, 256) + 1, SUBLANES):
        if rows % cand == 0:
            tr = cand
    if rows * cols <= COMM_CHUNK_ELEMS:
        tr = rows
    c1 = 1.0 / (1.0 - ADAM_B1 ** ADAM_STEP)
    c2 = 1.0 / (1.0 - ADAM_B2 ** ADAM_STEP)

    def body(w_ref, g_ref, m_ref, v_ref, d_ref, mo_ref, vo_ref):
        gv = g_ref[...]
        m_new = ADAM_B1 * m_ref[...] + (1.0 - ADAM_B1) * gv
        v_new = ADAM_B2 * v_ref[...] + (1.0 - ADAM_B2) * (gv * gv)
        mo_ref[...] = m_new
        vo_ref[...] = v_new
        d_ref[...] = -ADAM_LR * ((m_new * c1) / (jnp.sqrt(v_new * c2) + ADAM_EPS) + ADAM_WD * w_ref[...])

    spec = pl.BlockSpec((tr, cols), lambda i: (i, 0))
    outs = pl.pallas_call(
        body, name=name, grid=(rows // tr,), in_specs=[spec] * 4, out_specs=[spec] * 3,
        out_shape=[jax.ShapeDtypeStruct((rows, cols), F32)] * 3,
    )(*[a.reshape(rows, cols) for a in (w, g, m, v)])
    return tuple(o.reshape(shape) for o in outs)


def _pad_cols(a, cols):
    return jnp.pad(a, [(0, 0)] * (a.ndim - 1) + [(0, cols - a.shape[-1])])


def _flat8(parts, width):
    v = jnp.concatenate([p.reshape(-1) for p in parts])
    return jnp.pad(v, (0, width - v.shape[0])).reshape(SUBLANES, width // SUBLANES)


KV_SHARD = 514
KV_SHARD_PAD = 640
BIG = ("a_w_in", "a_w_out", "kv_w", "b_w_q", "b_w_out", "up0", "up1", "down0", "down1")


def kernel(x, c, ada_w, ada_b, a_w_in, a_lb_logits, a_norm_g, a_w_out, kv_ada_w, kv_ada_b, kv_w, kv_b_f, k_norm_g, b_w_q, q_norm_g, b_w_out, ffn_w_up, ffn_conv_w, ffn_conv_b, ffn_w_down, loss_target, m_ada_w, m_ada_b, m_a_w_in, m_a_lb_logits, m_a_norm_g, m_a_w_out, m_kv_ada_w, m_kv_ada_b, m_kv_w, m_kv_b_f, m_k_norm_g, m_b_w_q, m_q_norm_g, m_b_w_out, m_ffn_w_up, m_ffn_conv_w, m_ffn_conv_b, m_ffn_w_down, v_ada_w, v_ada_b, v_a_w_in, v_a_lb_logits, v_a_norm_g, v_a_w_out, v_kv_ada_w, v_kv_ada_b, v_kv_w, v_kv_b_f, v_k_norm_g, v_b_w_q, v_q_norm_g, v_b_w_out, v_ffn_w_up, v_ffn_conv_w, v_ffn_conv_b, v_ffn_w_down):
    dm, ff = D_MODEL, D_FF
    ix, iy, ic = lax.axis_index("x"), lax.axis_index("y"), lax.axis_index("c")
    chip = 2 * ix + iy
    dev = 2 * chip + ic

    w1 = 10240
    g1 = _allgather8(_flat8([c, a_lb_logits, ffn_conv_w], w1), "gather_cond").reshape(N_DEV, w1)
    c_all = g1[:, :dm]
    per_chip = g1[0::2]
    lb_logits = per_chip[:, dm:dm + 512].reshape(N_CHIPS, 2, 256).transpose(1, 0, 2).reshape(2, dm)
    conv_w = per_chip[:, dm + 512:dm + 512 + 2 * CONV_W * FFN_COLS].reshape(N_CHIPS, 2, CONV_W, FFN_COLS)
    conv_w = conv_w.transpose(1, 2, 0, 3).reshape(2, CONV_W, 2, ff).transpose(0, 2, 1, 3)
    conv_b = ffn_conv_b.reshape(2, 2, 1, ff)
    lb = jax.nn.softmax(lb_logits, axis=0)[0:1]

    c16 = jnp.pad(c_all, ((0, 8), (0, 0)))
    mod_ada, c_act16 = _cond_rows(c16, ada_w, True, "mod_ada")
    mod_kv, _ = _cond_rows(c16, kv_ada_w[None], True, "mod_kv")
    mine = jnp.concatenate([mod_ada[0, :8], mod_ada[1, :8], mod_kv[0, :8]], axis=1)
    w2 = mine.shape[1]
    g2 = _allgather8(mine, "gather_mod").reshape(N_DEV, 8, w2)[0::2]
    my_rows = lax.dynamic_index_in_dim(g2, dev, axis=1, keepdims=False)
    mod0 = my_rows[:, 0:1536].reshape(6 * dm) + ada_b[0]
    mod1 = my_rows[:, 1536:3072].reshape(6 * dm) + ada_b[1]
    modk = my_rows[:, 3072:3584].reshape(2 * dm) + kv_ada_b
    mods = {"l0": [v.reshape(1, dm) for v in jnp.split(mod0, 6)],
            "l1": [v.reshape(1, dm) for v in jnp.split(mod1, 6)],
            "kv": [v.reshape(1, dm) for v in jnp.split(modk, 2)]}

    local = [(a_w_in, 0), (a_w_out, 0), (_pad_cols(kv_w, KV_SHARD_PAD)[None], 0), (b_w_q, 0), (b_w_out, 0),
             (ffn_w_up, 0), (ffn_w_up, 1), (ffn_w_down, 0), (ffn_w_down, 1)]
    chip_arr = chip.reshape(1).astype(jnp.int32)
    own = {n: _cast_own_block(w, layer, chip_arr, "cast_" + n) for n, (w, layer) in zip(BIG, local)}
    stages = {"mixer0": ("a_w_in",), "ffn0": ("a_w_out", "up0", "down0"),
              "layer1": ("kv_w", "b_w_q", "b_w_out", "up1", "down1")}
    arriving = {st: _sequencer_gather([own[n] for n in names], "gather_" + st, cid)
                for cid, (st, names) in enumerate(stages.items(), start=1)}
    rowwise = lambda g: g.reshape(1, -1, dm)

    def weights_at(stage, token):
        got, token = lax.optimization_barrier((arriving[stage], token))
        g = dict(zip(stages[stage], got))
        if stage == "mixer0":
            return {"a_w_in": g["a_w_in"]}, token
        if stage == "ffn0":
            return {"a_w_out": rowwise(g["a_w_out"]), "up0": g["up0"], "down0": rowwise(g["down0"])}, token
        kv_full = g["kv_w"][:, :, :KV_SHARD].transpose(1, 0, 2).reshape(dm, N_CHIPS * KV_SHARD)
        return {"kv_k": kv_full[None, :, :dm], "kv_v": kv_full[None, :, dm:2 * dm],
                "kv_f": _pad_cols(kv_full[None, :, 2 * dm:], LANES), "b_w_q": g["b_w_q"],
                "b_w_out": rowwise(g["b_w_out"]), "up1": g["up1"], "down1": rowwise(g["down1"])}, token

    vecs = {"a_norm_g": jnp.tile(a_norm_g, (1, HEADS)), "k_norm_g": jnp.tile(k_norm_g[None], (1, HEADS)),
            "q_norm_g": jnp.tile(q_norm_g, (1, HEADS)), "kv_b_f": _pad_cols(kv_b_f[None], LANES),
            "conv_w0": conv_w[0], "conv_b0": conv_b[0], "conv_w1": conv_w[1], "conv_b1": conv_b[1]}

    sq, grad_x, big, small, marks = _local_step(x[0], loss_target[0], mods, lb, vecs, weights_at)
    loss = lax.psum(0.5 * jnp.sum(sq) / dm, ("x", "y", "c"))

    kv_grad = jnp.concatenate([big["kv_k"][0], big["kv_v"][0], big["kv_f"][0][:, :HEADS]], axis=1)
    kv_grad = _pad_cols(kv_grad.reshape(dm, N_CHIPS, KV_SHARD).transpose(1, 0, 2), KV_SHARD_PAD)
    chipwise = lambda g: g.reshape(N_CHIPS, -1, dm)
    parts = dict(zip(BIG, [big["a_w_in"], chipwise(big["a_w_out"]), kv_grad, big["b_w_q"], chipwise(big["b_w_out"]),
                           big["up0"], big["up1"], chipwise(big["down0"]), chipwise(big["down1"])]))
    place = jnp.stack([chip, ic, dev]).astype(jnp.int32)

    served = []

    groups = (("up1", "down1"), ("b_w_out", "b_w_q", "kv_w"), ("up0", "down0", "a_w_out"), ("a_w_in",))

    def scatter_group(k):
        mine = [parts[n] for n in groups[k]]
        if served:
            mine, _ = lax.optimization_barrier((mine, served[-1]))
        served.append(_sequencer_scatter(mine, "scatter_grads_%d" % k, 4 + k))

    def sum_group(k, token):
        inboxes, _ = lax.optimization_barrier((served[k], token))
        return [_sum_pieces(parts[n], box, place, "sum_" + n) for n, box in zip(groups[k], inboxes)]

    def swap_group(k, halves, behind):
        halves, _ = lax.optimization_barrier((halves, behind))
        return dict(zip(groups[k], _sequencer_swap_halves(halves, "swap_grads_%d" % k, 8 + k)))

    for k in range(4):
        scatter_group(k)
    halves = [sum_group(0, marks["attention_bwd"]), sum_group(1, marks["ffn0_bwd"]), sum_group(2, marks["mixer0_bwd"])]

    fold = lambda a: a.sum(axis=0)
    heads = lambda a: fold(a).reshape(HEADS, HEAD_DIM).sum(axis=0)
    conv_flat = lambda a: a.sum(axis=2).transpose(1, 0, 2)
    pieces = ([fold(a) for a in small["mod_l0"]] + [fold(a) for a in small["mod_l1"]] + [fold(a) for a in small["mod_kv"]]
              + [conv_flat(small["conv0"]), conv_flat(small["conv1"]), heads(small["a_norm_g"]), heads(small["k_norm_g"]),
                 heads(small["q_norm_g"]), fold(small["kv_b_f"]), fold(small["lb"])])
    w3 = 61440
    small_vec, _ = lax.optimization_barrier((_flat8(pieces, w3), served[3]))
    g3 = _sequencer_allgather8(small_vec, dev, "gather_small", 12)
    rs = {}
    for k in range(3):
        rs.update(swap_group(k, halves[k], g3))
    tot = _sum_devices(g3, "sum_small").reshape(w3)
    n_mod = 14 * dm
    dmod_all = g3.reshape(N_DEV, w3)[:, :n_mod]
    o = n_mod
    conv_tot = [tot[o + l * 8 * ff: o + (l + 1) * 8 * ff].reshape(4, 2 * ff) for l in range(2)]
    o += 16 * ff
    g_a_norm, g_k_norm, g_q_norm = (tot[o + i * HEAD_DIM: o + (i + 1) * HEAD_DIM] for i in range(3))
    o += 3 * HEAD_DIM
    g_kv_b_f = tot[o:o + HEADS]
    dlb = tot[o + LANES:o + LANES + dm]

    ct = _pad_cols(c_act16[:8].T, LANES)
    dmod_pad = jnp.pad(dmod_all, ((0, LANES - N_DEV), (0, 0)))
    cols_ada = jnp.stack([lax.dynamic_slice_in_dim(dmod_pad, l * 6 * dm + chip * 1536, 1536, axis=1) for l in range(2)])
    cols_kv = lax.dynamic_slice_in_dim(dmod_pad, 12 * dm + chip * 512, 512, axis=1)[None]
    g_ada_w = _outer_grad(ct, cols_ada, "grad_ada_w")
    g_kv_ada_w = _outer_grad(ct, cols_kv, "grad_kv_ada_w")[0]

    my_lb = lax.dynamic_slice_in_dim(lb[0], chip * 256, 256)
    l0 = lax.dynamic_slice_in_dim(dlb, chip * 256, 256) * my_lb * (1.0 - my_lb)
    grads = {
        "ada_w": g_ada_w, "ada_b": jnp.stack([tot[:6 * dm], tot[6 * dm:12 * dm]]),
        "a_lb_logits": jnp.stack([l0, -l0]), "a_norm_g": g_a_norm[None],
        "a_w_out": rs["a_w_out"][None], "kv_ada_w": g_kv_ada_w, "kv_ada_b": tot[12 * dm:14 * dm],
        "kv_w": rs["kv_w"][:, :KV_SHARD], "kv_b_f": g_kv_b_f, "k_norm_g": g_k_norm,
        "b_w_q": rs["b_w_q"][None], "q_norm_g": g_q_norm[None], "b_w_out": rs["b_w_out"][None],
        "ffn_w_up": jnp.stack([rs["up0"], rs["up1"]]),
        "ffn_conv_w": jnp.stack([lax.dynamic_slice_in_dim(ct_l[:CONV_W], chip * FFN_COLS, FFN_COLS, axis=1) for ct_l in conv_tot]),
        "ffn_conv_b": jnp.stack([ct_l[CONV_W] for ct_l in conv_tot]),
        "ffn_w_down": jnp.stack([rs["down0"], rs["down1"]]),
    }
    weights = dict(ada_w=ada_w, ada_b=ada_b, a_w_in=a_w_in, a_lb_logits=a_lb_logits, a_norm_g=a_norm_g, a_w_out=a_w_out,
                   kv_ada_w=kv_ada_w, kv_ada_b=kv_ada_b, kv_w=kv_w, kv_b_f=kv_b_f, k_norm_g=k_norm_g, b_w_q=b_w_q,
                   q_norm_g=q_norm_g, b_w_out=b_w_out, ffn_w_up=ffn_w_up, ffn_conv_w=ffn_conv_w, ffn_conv_b=ffn_conv_b,
                   ffn_w_down=ffn_w_down)
    m_in = dict(ada_w=m_ada_w, ada_b=m_ada_b, a_w_in=m_a_w_in, a_lb_logits=m_a_lb_logits, a_norm_g=m_a_norm_g,
                a_w_out=m_a_w_out, kv_ada_w=m_kv_ada_w, kv_ada_b=m_kv_ada_b, kv_w=m_kv_w, kv_b_f=m_kv_b_f,
                k_norm_g=m_k_norm_g, b_w_q=m_b_w_q, q_norm_g=m_q_norm_g, b_w_out=m_b_w_out, ffn_w_up=m_ffn_w_up,
                ffn_conv_w=m_ffn_conv_w, ffn_conv_b=m_ffn_conv_b, ffn_w_down=m_ffn_w_down)
    v_in = dict(ada_w=v_ada_w, ada_b=v_ada_b, a_w_in=v_a_w_in, a_lb_logits=v_a_lb_logits, a_norm_g=v_a_norm_g,
                a_w_out=v_a_w_out, kv_ada_w=v_kv_ada_w, kv_ada_b=v_kv_ada_b, kv_w=v_kv_w, kv_b_f=v_kv_b_f,
                k_norm_g=v_k_norm_g, b_w_q=v_b_w_q, q_norm_g=v_q_norm_g, b_w_out=v_b_w_out, ffn_w_up=v_ffn_w_up,
                ffn_conv_w=v_ffn_conv_w, ffn_conv_b=v_ffn_conv_b, ffn_w_down=v_ffn_w_down)

    names = list(weights)
    step = lambda n: _adamw(weights[n], grads[n], m_in[n], v_in[n], "adamw_" + n)
    grads = {n: g.reshape(weights[n].shape) for n, g in grads.items()}
    upd = {n: step(n) for n in names if n != "a_w_in"}
    last = sum_group(3, [u[0] for u in upd.values()])
    grads["a_w_in"] = swap_group(3, last, last)["a_w_in"][None]
    upd["a_w_in"] = step("a_w_in")
    return (loss, grad_x[None], *[grads[n] for n in names], *[upd[n][0] for n in names],
            *[upd[n][1] for n in names], *[upd[n][2] for n in names])
```

```python
import jax
import jax.numpy as jnp
from jax import lax
from jax.experimental import pallas as pl
from jax.experimental.pallas import tpu as pltpu
from jax.experimental.pallas import tpu_sc as plsc

F32 = jnp.float32
BF16 = jnp.bfloat16

D_MODEL = 1024
HEADS = 8
HEAD_DIM = 128
A_CHUNK = 64
D_FF = 2816
CONV_W = 3
EPS = 1e-6
NEG_INF = -1e30
N_CHIPS = 4
N_DEV = 8

ADAM_LR = 0.001
ADAM_B1 = 0.9
ADAM_B2 = 0.999
ADAM_EPS = 1e-08
ADAM_WD = 0.01
ADAM_STEP = 10

SUBLANES = 8
BF16_ROWS = 16
LANES = 128
HALO = BF16_ROWS
ROW_TILE = 512
TOKEN_TILE_TN = 2048
FFN_COLS = 1408
HGRN_ROWS = 256
ATT_TILE = 512
ATT_SPLIT = 2
MESH = pl.DeviceIdType.MESH


def _sig(x):
    return jax.nn.sigmoid(x)


def _dot(a, b):
    return jnp.dot(a, b, preferred_element_type=F32)


def _dot_nt(a, b):
    return lax.dot_general(a, b, (((1,), (1,)), ((), ())), preferred_element_type=F32)


def _dot_tn(a, b):
    return lax.dot_general(a, b, (((0,), (0,)), ((), ())), preferred_element_type=F32)


def _split2(x):
    hi = x.astype(BF16)
    lo = (x - hi.astype(F32)).astype(BF16)
    return hi, lo


def _dot_f32(a, b):
    ah, al = _split2(a)
    bh, bl = _split2(b)
    return _dot(ah, bh) + _dot(ah, bl) + _dot(al, bh)


def _tri_dot(tri, x):
    hi = x.astype(BF16)
    r = x - hi.astype(F32)
    mid = r.astype(BF16)
    lo = (r - mid.astype(F32)).astype(BF16)
    return _dot(tri, hi) + _dot(tri, mid) + _dot(tri, lo)


def _tri(n, upper=False):
    r = lax.broadcasted_iota(jnp.int32, (n, n), 0)
    c = lax.broadcasted_iota(jnp.int32, (n, n), 1)
    keep = (c >= r) if upper else (c <= r)
    return jnp.where(keep, 1.0, 0.0).astype(BF16)


def _colsum8(v):
    rows, n = v.shape
    return v.reshape(rows // SUBLANES, SUBLANES, n).sum(axis=0)


def _full(shape):
    nd = len(shape)
    return pl.BlockSpec(shape, lambda *_: (0,) * nd)


def _tile(n, want):
    t = min(n, want)
    assert n % t == 0, (n, t)
    return t


def _mm_nn(a, w, groups, out_dtype, name):
    m_rows, k = a.shape
    p_n, _, n = w.shape
    per = p_n // groups
    tm = _tile(m_rows, ROW_TILE)

    def body(a_ref, w_ref, o_ref):
        av = a_ref[...]
        for p in range(p_n):
            o_ref[p // per, :, (p % per) * n:(p % per + 1) * n] = _dot(av, w_ref[p]).astype(out_dtype)

    return pl.pallas_call(
        body, name=name, grid=(m_rows // tm,),
        in_specs=[pl.BlockSpec((tm, k), lambda i: (i, 0)), _full((p_n, k, n))],
        out_specs=pl.BlockSpec((groups, tm, per * n), lambda i: (0, i, 0)),
        out_shape=jax.ShapeDtypeStruct((groups, m_rows, per * n), out_dtype),
    )(a, w)


def _mm_nt(d, w, out_dtype, name):
    g_n, m_rows, _ = d.shape
    p_n, k, n = w.shape
    per = p_n // g_n
    tm = _tile(m_rows, ROW_TILE)

    def body(d_ref, w_ref, o_ref):
        acc = None
        for p in range(p_n):
            t = _dot_nt(d_ref[p // per, :, (p % per) * n:(p % per + 1) * n], w_ref[p])
            acc = t if acc is None else acc + t
        o_ref[...] = acc.astype(out_dtype)

    return pl.pallas_call(
        body, name=name, grid=(m_rows // tm,),
        in_specs=[pl.BlockSpec((g_n, tm, per * n), lambda i: (0, i, 0)), _full((p_n, k, n))],
        out_specs=pl.BlockSpec((tm, k), lambda i: (i, 0)),
        out_shape=jax.ShapeDtypeStruct((m_rows, k), out_dtype),
    )(d, w)


def _mm_tn(a, d, p_n, name):
    m_rows, k = a.shape
    g_n, _, w_cols = d.shape
    per = p_n // g_n
    n = w_cols // per
    tm = _tile(m_rows, TOKEN_TILE_TN if k <= D_MODEL else ROW_TILE)
    steps = m_rows // tm

    def body(a_ref, d_ref, o_ref, acc):
        m = pl.program_id(1)

        @pl.when(m == 0)
        def _():
            acc[...] = jnp.zeros_like(acc)

        acc[...] += _dot_tn(a_ref[...], d_ref[...])

        @pl.when(m == steps - 1)
        def _():
            o_ref[...] = acc[...].astype(BF16)

    return pl.pallas_call(
        body, name=name, grid=(p_n, steps),
        in_specs=[pl.BlockSpec((tm, k), lambda p, m: (m, 0)),
                  pl.BlockSpec((None, tm, n), lambda p, m: (p // per, m, p % per))],
        out_specs=pl.BlockSpec((None, k, n), lambda p, m: (p, 0, 0)),
        out_shape=jax.ShapeDtypeStruct((p_n, k, n), BF16),
        scratch_shapes=[pltpu.VMEM((k, n), F32)],
    )(a, d)


def _premix(x, shift, scale, name):
    s, dm = x.shape
    tm = _tile(s, ROW_TILE)

    def body(x_ref, sh_ref, sc_ref, h_ref):
        xv = x_ref[...]
        inv = lax.rsqrt(jnp.mean(xv * xv, axis=-1, keepdims=True) + EPS)
        h_ref[...] = (xv * inv * (1.0 + sc_ref[...]) + sh_ref[...]).astype(BF16)

    row = pl.BlockSpec((tm, dm), lambda i: (i, 0))
    vec = _full((1, dm))
    return pl.pallas_call(body, name=name, grid=(s // tm,), in_specs=[row, vec, vec], out_specs=row,
                          out_shape=jax.ShapeDtypeStruct((s, dm), BF16))(x, shift, scale)


def _premix_bwd(x, terms, dres, name, branch=None):
    s, dm = x.shape
    tm = _tile(s, ROW_TILE)
    pairs = [pr for _, prs in terms for pr in prs]
    n_in = 2 + len(terms) + 2 * len(pairs) + (2 if branch else 0)

    def body(*refs):
        x_ref, dres_ref = refs[:2]
        sc_refs = refs[2:2 + len(terms)]
        mm_refs = refs[2 + len(terms):2 + len(terms) + 2 * len(pairs)]
        outs = refs[n_in:]

        @pl.when(pl.program_id(0) == 0)
        def _():
            for o in outs[1:1 + 2 * len(terms)]:
                o[...] = jnp.zeros_like(o)
            if branch:
                outs[-1][...] = jnp.zeros_like(outs[-1])

        xv = x_ref[...]
        inv = lax.rsqrt(jnp.mean(xv * xv, axis=-1, keepdims=True) + EPS)
        r = xv * inv
        dx = dres_ref[...]
        k = 0
        for t, (_, prs) in enumerate(terms):
            dh = None
            for d, w in prs:
                d_ref, w_ref = mm_refs[2 * k], mm_refs[2 * k + 1]
                k += 1
                p_n, _, n = w.shape
                per = p_n // d.shape[0]
                for p in range(p_n):
                    part = _dot_nt(d_ref[p // per, :, (p % per) * n:(p % per + 1) * n], w_ref[p])
                    dh = part if dh is None else dh + part
            dr = dh * (1.0 + sc_refs[t][...])
            dx = dx + inv * (dr - r * jnp.mean(dr * r, axis=-1, keepdims=True))
            outs[1 + 2 * t][...] += _colsum8(dh)
            outs[2 + 2 * t][...] += _colsum8(dh * r)
        outs[0][...] = dx
        if branch:
            y_ref, g_ref = refs[n_in - 2:n_in]
            outs[-2][0] = (dx * g_ref[...]).astype(BF16)
            outs[-1][...] += _colsum8(dx * y_ref[...])

    row = pl.BlockSpec((tm, dm), lambda i: (i, 0))
    vec, acc = _full((1, dm)), _full((SUBLANES, dm))
    ins, specs = [x, dres] + [sc for sc, _ in terms], [row, row] + [vec] * len(terms)
    for d, w in pairs:
        ins += [d, w]
        specs += [pl.BlockSpec((d.shape[0], tm, d.shape[2]), lambda i: (0, i, 0)), _full(w.shape)]
    out_shape = [jax.ShapeDtypeStruct((s, dm), F32)] + [jax.ShapeDtypeStruct((SUBLANES, dm), F32)] * (2 * len(terms))
    out_specs = [row] + [acc] * (2 * len(terms))
    if branch:
        ins += list(branch)
        specs += [row, vec]
        out_shape += [jax.ShapeDtypeStruct((1, s, dm), BF16), jax.ShapeDtypeStruct((SUBLANES, dm), F32)]
        out_specs += [pl.BlockSpec((1, tm, dm), lambda i: (0, i, 0)), acc]
    outs = pl.pallas_call(body, name=name, grid=(s // tm,), in_specs=specs, out_specs=out_specs,
                          out_shape=out_shape)(*ins)
    partials = [(outs[1 + 2 * t], outs[2 + 2 * t]) for t in range(len(terms))]
    return (outs[0], partials) + ((outs[-2], outs[-1]) if branch else ())


def _conv_taps(e, w, b):
    return w[2:3] * e + w[1:2] * pltpu.roll(e, 1, 0) + w[0:1] * pltpu.roll(e, 2, 0) + b


def _ffn_specs(s, tm, cb):
    hb = tm // HALO
    last = s // HALO - 1
    main = pl.BlockSpec((2, tm, cb), lambda j, i: (0, i, j))
    prev = pl.BlockSpec((2, HALO, cb), lambda j, i: (0, jnp.maximum(i * hb - 1, 0), j))
    nxt = pl.BlockSpec((2, HALO, cb), lambda j, i: (0, jnp.minimum((i + 1) * hb, last), j))
    wspec = pl.BlockSpec((2, CONV_W, cb), lambda j, i: (0, 0, j))
    bspec = pl.BlockSpec((2, 1, cb), lambda j, i: (0, 0, j))
    return main, prev, nxt, wspec, bspec


def _convglu_fwd(u, w, b, name):
    _, s, f = u.shape
    tm = _tile(s, 256)
    cb = _tile(f, FFN_COLS)
    main, prev, _, wspec, bspec = _ffn_specs(s, tm, cb)

    def body(u_ref, up_ref, w_ref, b_ref, a_ref):
        first = jnp.where(pl.program_id(1) > 0, 1.0, 0.0)

        def conv(g):
            e = jnp.concatenate([up_ref[g].astype(F32) * first, u_ref[g].astype(F32)], axis=0)
            return _conv_taps(e, w_ref[g], b_ref[g])[HALO:]

        gate = conv(0)
        a_ref[...] = (gate * _sig(gate) * conv(1)).astype(BF16)

    return pl.pallas_call(
        body, name=name, grid=(f // cb, s // tm), in_specs=[main, prev, wspec, bspec],
        out_specs=pl.BlockSpec((tm, cb), lambda j, i: (i, j)),
        out_shape=jax.ShapeDtypeStruct((s, f), BF16),
    )(u, u, w, b)


def _convglu_bwd(u, da, w, b, name):
    _, s, f = u.shape
    tm = _tile(s, 256)
    cb = _tile(f, FFN_COLS)
    steps = s // tm
    n_ext = tm + 2 * HALO
    main, prev, nxt, wspec, bspec = _ffn_specs(s, tm, cb)
    hb = tm // HALO
    last = s // HALO - 1
    da_main = pl.BlockSpec((tm, cb), lambda j, i: (i, j))
    da_next = pl.BlockSpec((HALO, cb), lambda j, i: (jnp.minimum((i + 1) * hb, last), j))

    def body(u_ref, up_ref, un_ref, da_ref, dan_ref, w_ref, b_ref, du_ref, acc_ref):
        i = pl.program_id(1)
        first = jnp.where(i > 0, 1.0, 0.0)
        notlast = jnp.where(i < steps - 1, 1.0, 0.0)

        @pl.when(i == 0)
        def _():
            acc_ref[...] = jnp.zeros_like(acc_ref)

        def ext(g):
            return jnp.concatenate([up_ref[g].astype(F32) * first, u_ref[g].astype(F32), un_ref[g].astype(F32)], axis=0)

        ug, uv = ext(0), ext(1)
        gate = _conv_taps(ug, w_ref[0], b_ref[0])
        val = _conv_taps(uv, w_ref[1], b_ref[1])
        da_e = jnp.concatenate([jnp.zeros((HALO, cb), F32), da_ref[...].astype(F32),
                                dan_ref[...].astype(F32) * notlast], axis=0)
        sg = _sig(gate)
        d_val = da_e * gate * sg
        d_gate = da_e * val * (sg * (1.0 + gate * (1.0 - sg)))

        def finish(g, d, e):
            wv = w_ref[g]
            rows = slice(HALO, HALO + tm)
            d1, d2 = pltpu.roll(d, n_ext - 1, 0), pltpu.roll(d, n_ext - 2, 0)
            du_ref[g] = (wv[2:3] * d + wv[1:2] * d1 + wv[0:1] * d2)[rows].astype(BF16)
            em = e[rows]
            acc_ref[g, 2] += _colsum8(d[rows] * em)
            acc_ref[g, 1] += _colsum8(d1[rows] * em)
            acc_ref[g, 0] += _colsum8(d2[rows] * em)
            acc_ref[g, 3] += _colsum8(d[rows])

        finish(0, d_gate, ug)
        finish(1, d_val, uv)

    return pl.pallas_call(
        body, name=name, grid=(f // cb, steps),
        in_specs=[main, prev, nxt, da_main, da_next, wspec, bspec],
        out_specs=[main, pl.BlockSpec((2, 4, SUBLANES, cb), lambda j, i: (0, 0, 0, j))],
        out_shape=[jax.ShapeDtypeStruct((2, s, f), BF16), jax.ShapeDtypeStruct((2, 4, SUBLANES, f), F32)],
    )(u, u, u, da, da, w, b)


def _hgrn_gates(q_raw, f_raw, lb, tri):
    sf = _sig(f_raw)
    fg = lb + (1.0 - lb) * sf
    b = _tri_dot(tri, jnp.log(fg))
    return q_raw * _sig(q_raw), 1.0 - fg, b, fg, sf


def _hgrn_fwd(proj, lb, norm_g, name):
    s = proj.shape[0]
    tb = _tile(s, HGRN_ROWS)
    n_c = tb // A_CHUNK
    half = A_CHUNK // 2

    def body(q_ref, f_ref, v_ref, g_ref, lb_ref, ng_ref, o_ref, yp_ref, st_ref, state):
        @pl.when(pl.program_id(0) == 0)
        def _():
            state[...] = jnp.zeros_like(state)

        tri = _tri(A_CHUNK)
        causal = lax.broadcasted_iota(jnp.int32, (A_CHUNK, A_CHUNK), 1) <= lax.broadcasted_iota(
            jnp.int32, (A_CHUNK, A_CHUNK), 0)

        def chunk(ci, carry):
            rows = pl.ds(pl.multiple_of(ci * A_CHUNK, A_CHUNK), A_CHUNK)
            for h in range(HEADS):
                cs = slice(h * HEAD_DIM, (h + 1) * HEAD_DIM)
                qs, k, b, _, _ = _hgrn_gates(q_ref[rows, cs], f_ref[rows, cs], lb_ref[:, cs], tri)
                b_mid, b_last = b[half:half + 1], b[A_CHUNK - 1:A_CHUNK]
                vb = v_ref[rows, cs].astype(BF16)
                scores = _dot_nt((qs * jnp.exp(b - b_mid)).astype(BF16), (k * jnp.exp(b_mid - b)).astype(BF16))
                scores = jnp.where(causal, scores, 0.0)
                st = state[h]
                st_ref[ci, h] = st
                o = _dot(scores.astype(BF16), vb) + _dot_nt((qs * jnp.exp(b)).astype(BF16), st.astype(BF16))
                state[h] = st * jnp.exp(b_last) + _dot_tn(vb, (k * jnp.exp(b_last - b)).astype(BF16))
                o_ref[rows, cs] = o
                inv = lax.rsqrt(jnp.mean(o * o, axis=-1, keepdims=True) + EPS)
                g_raw = g_ref[rows, cs]
                yp_ref[rows, cs] = (o * inv * ng_ref[:, cs] * (g_raw * _sig(g_raw))).astype(BF16)
            return carry

        lax.fori_loop(0, n_c, chunk, 0)

    col = lambda j: pl.BlockSpec((tb, D_MODEL), lambda i: (i, j))
    vec = _full((1, D_MODEL))
    return pl.pallas_call(
        body, name=name, grid=(s // tb,), in_specs=[col(0), col(1), col(2), col(3), vec, vec],
        out_specs=[col(0), col(0), pl.BlockSpec((n_c, HEADS, HEAD_DIM, HEAD_DIM), lambda i: (i, 0, 0, 0))],
        out_shape=[jax.ShapeDtypeStruct((s, D_MODEL), F32), jax.ShapeDtypeStruct((s, D_MODEL), BF16),
                   jax.ShapeDtypeStruct((s // A_CHUNK, HEADS, HEAD_DIM, HEAD_DIM), F32)],
        scratch_shapes=[pltpu.VMEM((HEADS, HEAD_DIM, HEAD_DIM), F32)],
    )(proj, proj, proj, proj, lb, norm_g)


def _hgrn_bwd(proj, lb, norm_g, o, states, dyp, name):
    s = proj.shape[0]
    tb = _tile(s, HGRN_ROWS)
    n_c = tb // A_CHUNK
    n_b = s // tb
    half = A_CHUNK // 2

    def body(q_ref, f_ref, v_ref, g_ref, lb_ref, ng_ref, o_ref, st_ref, dyp_ref, dp_ref, dlb_ref, dng_ref, dstate):
        @pl.when(pl.program_id(0) == 0)
        def _():
            dstate[...] = jnp.zeros_like(dstate)
            dlb_ref[...] = jnp.zeros_like(dlb_ref)
            dng_ref[...] = jnp.zeros_like(dng_ref)

        tri = _tri(A_CHUNK)
        tri_up = _tri(A_CHUNK, upper=True)
        row_id = lax.broadcasted_iota(jnp.int32, (A_CHUNK, HEAD_DIM), 0)
        causal = lax.broadcasted_iota(jnp.int32, (A_CHUNK, A_CHUNK), 1) <= lax.broadcasted_iota(
            jnp.int32, (A_CHUNK, A_CHUNK), 0)

        def chunk(cj, carry):
            ci = n_c - 1 - cj
            rows = pl.ds(pl.multiple_of(ci * A_CHUNK, A_CHUNK), A_CHUNK)
            for h in range(HEADS):
                cs = slice(h * HEAD_DIM, (h + 1) * HEAD_DIM)
                q_raw, lbh = q_ref[rows, cs], lb_ref[:, cs]
                qs, k, b, fg, sf = _hgrn_gates(q_raw, f_ref[rows, cs], lbh, tri)
                b_mid, b_last = b[half:half + 1], b[A_CHUNK - 1:A_CHUNK]
                e_qi, e_ki, e_q, e_ks = jnp.exp(b - b_mid), jnp.exp(b_mid - b), jnp.exp(b), jnp.exp(b_last - b)
                q_i, k_i, q_e, k_s = qs * e_qi, k * e_ki, qs * e_q, k * e_ks
                vb = v_ref[rows, cs].astype(BF16)
                scores = jnp.where(causal, _dot_nt(q_i.astype(BF16), k_i.astype(BF16)), 0.0)
                ov, g_raw, dy, ng = o_ref[rows, cs], g_ref[rows, cs], dyp_ref[rows, cs], ng_ref[:, cs]
                inv = lax.rsqrt(jnp.mean(ov * ov, axis=-1, keepdims=True) + EPS)
                nrm = ov * inv
                sg = _sig(g_raw)
                gs = g_raw * sg
                dn = dy * ng * gs
                dng_ref[0:1, cs] += jnp.sum(dy * nrm * gs, axis=0, keepdims=True)
                dg_raw = dy * nrm * ng * (sg * (1.0 + g_raw * (1.0 - sg)))
                do = (inv * (dn - nrm * jnp.mean(dn * nrm, axis=-1, keepdims=True))).astype(BF16)
                st_prev = st_ref[ci, h]
                dst = dstate[h]
                dstb = dst.astype(BF16)
                d_scores = jnp.where(causal, _dot_nt(do, vb), 0.0).astype(BF16)
                dv = _dot_tn(scores.astype(BF16), do) + _dot_nt(k_s.astype(BF16), dstb)
                dq_i = _dot(d_scores, k_i.astype(BF16))
                dk_i = _dot_tn(d_scores, q_i.astype(BF16))
                dq_e = _dot(do, st_prev.astype(BF16))
                dk_s = _dot(vb, dstb)
                d_decay = jnp.sum(st_prev * dst, axis=0, keepdims=True)
                dstate[h] = dst * jnp.exp(b_last) + _dot_tn(do, q_e.astype(BF16))
                dq = dq_i * e_qi + dq_e * e_q
                dk = dk_i * e_ki + dk_s * e_ks
                t_qi, t_ki, t_ks = dq_i * q_i, dk_i * k_i, dk_s * k_s
                db = t_qi - t_ki + dq_e * q_e - t_ks
                db_mid = jnp.sum(t_ki - t_qi, axis=0, keepdims=True)
                db_last = jnp.sum(t_ks, axis=0, keepdims=True) + d_decay * jnp.exp(b_last)
                db = db + jnp.where(row_id == half, db_mid, 0.0) + jnp.where(row_id == A_CHUNK - 1, db_last, 0.0)
                dfg = _tri_dot(tri_up, db) / fg - dk
                dlb_ref[0:1, cs] += jnp.sum(dfg * (1.0 - sf), axis=0, keepdims=True)
                sq = _sig(q_raw)
                dp_ref[0, rows, cs] = (dq * (sq * (1.0 + q_raw * (1.0 - sq)))).astype(BF16)
                dp_ref[1, rows, cs] = (dfg * (1.0 - lbh) * sf * (1.0 - sf)).astype(BF16)
                dp_ref[2, rows, cs] = dv.astype(BF16)
                dp_ref[3, rows, cs] = dg_raw.astype(BF16)
            return carry

        lax.fori_loop(0, n_c, chunk, 0)

    col = lambda j: pl.BlockSpec((tb, D_MODEL), lambda i: (n_b - 1 - i, j))
    vec = _full((1, D_MODEL))
    acc = _full((SUBLANES, D_MODEL))
    return pl.pallas_call(
        body, name=name, grid=(n_b,),
        in_specs=[col(0), col(1), col(2), col(3), vec, vec, col(0),
                  pl.BlockSpec((n_c, HEADS, HEAD_DIM, HEAD_DIM), lambda i: (n_b - 1 - i, 0, 0, 0)), col(0)],
        out_specs=[pl.BlockSpec((4, tb, D_MODEL), lambda i: (0, n_b - 1 - i, 0)), acc, acc],
        out_shape=[jax.ShapeDtypeStruct((4, s, D_MODEL), BF16), jax.ShapeDtypeStruct((SUBLANES, D_MODEL), F32),
                   jax.ShapeDtypeStruct((SUBLANES, D_MODEL), F32)],
        scratch_shapes=[pltpu.VMEM((HEADS, HEAD_DIM, HEAD_DIM), F32)],
    )(proj, proj, proj, proj, lb, norm_g, o, states, dyp)


def _headnorm(x, g, mult, name, col0=0):
    s = x.shape[0]
    tm = _tile(s, ROW_TILE)

    def body(x_ref, g_ref, y_ref):
        for h in range(HEADS):
            cs = slice(h * HEAD_DIM, (h + 1) * HEAD_DIM)
            xv = x_ref[:, cs]
            inv = lax.rsqrt(jnp.mean(xv * xv, axis=-1, keepdims=True) + EPS)
            y_ref[:, cs] = (xv * inv * g_ref[:, cs] * mult).astype(BF16)

    return pl.pallas_call(
        body, name=name, grid=(s // tm,),
        in_specs=[pl.BlockSpec((tm, D_MODEL), lambda i: (i, col0)), _full((1, D_MODEL))],
        out_specs=pl.BlockSpec((tm, D_MODEL), lambda i: (i, 0)),
        out_shape=jax.ShapeDtypeStruct((s, D_MODEL), BF16),
    )(x, g)


def _headnorm_bwd(x, g, mult, dy, name, col0=0, extra=None):
    s = x.shape[0]
    tm = _tile(s, ROW_TILE)
    groups = 2 if extra is not None else 1
    head_major = dy.ndim == 3

    def body(*refs):
        x_ref, g_ref, dy_ref = refs[:3]
        dx_ref, dg_ref = refs[-2:]

        @pl.when(pl.program_id(0) == 0)
        def _():
            dg_ref[...] = jnp.zeros_like(dg_ref)

        for h in range(HEADS):
            cs = slice(h * HEAD_DIM, (h + 1) * HEAD_DIM)
            xv, gv = x_ref[:, cs], g_ref[:, cs]
            dyv = dy_ref[h, :, 0:HEAD_DIM] if head_major else dy_ref[:, cs]
            inv = lax.rsqrt(jnp.mean(xv * xv, axis=-1, keepdims=True) + EPS)
            nrm = xv * inv
            dn = dyv * gv * mult
            dg_ref[:, cs] += _colsum8(dyv * nrm * mult)
            dx_ref[0, :, cs] = (inv * (dn - nrm * jnp.mean(dn * nrm, axis=-1, keepdims=True))).astype(BF16)
        if extra is not None:
            dx_ref[1] = refs[3][...]

    row = pl.BlockSpec((tm, D_MODEL), lambda i: (i, 0))
    dy_spec = pl.BlockSpec((HEADS, tm, dy.shape[-1]), lambda i: (0, i, 0)) if head_major else row
    ins = [x, g, dy] + ([extra] if extra is not None else [])
    specs = ([pl.BlockSpec((tm, D_MODEL), lambda i: (i, col0)), _full((1, D_MODEL)), dy_spec]
             + ([row] if extra is not None else []))
    return pl.pallas_call(
        body, name=name, grid=(s // tm,), in_specs=specs,
        out_specs=[pl.BlockSpec((groups, tm, D_MODEL), lambda i: (0, i, 0)), _full((SUBLANES, D_MODEL))],
        out_shape=[jax.ShapeDtypeStruct((groups, s, D_MODEL), BF16), jax.ShapeDtypeStruct((SUBLANES, D_MODEL), F32)],
    )(*ins)


def _log_sigmoid(z):
    return jnp.minimum(z, 0.0) - jnp.log(1.0 + jnp.exp(-jnp.abs(z)))


Q_CUM, Q_ONE, Q_LSE = 0, 3, 6
LOG2E = 1.4426950408889634


def _pieces(v):
    hi = v.astype(BF16).astype(F32)
    mid = (v - hi).astype(BF16).astype(F32)
    lo = ((v - hi) - mid).astype(BF16).astype(F32)
    return hi, mid, lo


def _side(lane, at, v):
    hi, mid, lo = _pieces(v)
    return jnp.where(lane == at, hi, jnp.where(lane == at + 1, mid, jnp.where(lane == at + 2, lo, 0.0)))


def _fcum_fwd(f, bias, name):
    s = f.shape[0]
    tm = _tile(s, ROW_TILE)

    def body(f_ref, b_ref, qa_ref, ka_ref, carry):
        @pl.when(pl.program_id(0) == 0)
        def _():
            carry[...] = jnp.zeros_like(carry)

        cum = _tri_dot(_tri(tm), _log_sigmoid(f_ref[...] + b_ref[...])) + carry[...]
        carry[...] = cum[tm - 1:tm]
        lane = lax.broadcasted_iota(jnp.int32, (tm, LANES), 1)
        ones_q = jnp.where((lane >= Q_ONE) & (lane < Q_LSE), 1.0, 0.0)
        ones_k = jnp.where((lane < Q_ONE) | ((lane >= Q_LSE) & (lane < Q_LSE + 3)), 1.0, 0.0)
        for h in range(HEADS):
            c2 = cum[:, h:h + 1] * LOG2E
            qa_ref[h] = (_side(lane, Q_CUM, c2) + ones_q).astype(BF16)
            ka_ref[h] = (_side(lane, Q_ONE, -c2) + ones_k).astype(BF16)

    side = pl.BlockSpec((HEADS, tm, LANES), lambda i: (0, i, 0))
    return pl.pallas_call(
        body, name=name, grid=(s // tm,),
        in_specs=[pl.BlockSpec((tm, LANES), lambda i: (i, 0)), _full((1, LANES))],
        out_specs=[side, side],
        out_shape=[jax.ShapeDtypeStruct((HEADS, s, LANES), BF16)] * 2,
        scratch_shapes=[pltpu.VMEM((1, LANES), F32)],
    )(f, bias)


def _fcum_bwd(f, bias, dka, dq, name):
    s = f.shape[0]
    tm = _tile(s, ROW_TILE)
    n_b = s // tm
    q_lane = HEAD_DIM + Q_CUM

    def body(f_ref, b_ref, dka_ref, dqa_ref, dz_ref, db_ref, carry):
        @pl.when(pl.program_id(0) == 0)
        def _():
            carry[...] = jnp.zeros_like(carry)
            db_ref[...] = jnp.zeros_like(db_ref)

        lane = lax.broadcasted_iota(jnp.int32, (tm, LANES), 1)
        dcum = jnp.zeros((tm, LANES), F32)
        for h in range(HEADS):
            dcum = dcum + jnp.where(lane == h, dqa_ref[h, :, q_lane:q_lane + 1] - dka_ref[h, :, Q_ONE:Q_ONE + 1], 0.0)
        dlf = _tri_dot(_tri(tm, upper=True), dcum) + carry[...]
        carry[...] = dlf[0:1]
        dz = dlf * _sig(-(f_ref[...] + b_ref[...]))
        dz_ref[0] = dz.astype(BF16)
        db_ref[...] += _colsum8(dz)

    return pl.pallas_call(
        body, name=name, grid=(n_b,),
        in_specs=[pl.BlockSpec((tm, LANES), lambda i: (n_b - 1 - i, 0)), _full((1, LANES)),
                  pl.BlockSpec((HEADS, tm, LANES), lambda i: (0, n_b - 1 - i, 0)),
                  pl.BlockSpec((HEADS, tm, 2 * HEAD_DIM), lambda i: (0, n_b - 1 - i, 0))],
        out_specs=[pl.BlockSpec((1, tm, LANES), lambda i: (0, n_b - 1 - i, 0)), _full((SUBLANES, LANES))],
        out_shape=[jax.ShapeDtypeStruct((1, s, LANES), BF16), jax.ShapeDtypeStruct((SUBLANES, LANES), F32)],
        scratch_shapes=[pltpu.VMEM((1, LANES), F32)],
    )(f, bias, dka, dq)


def _causal_pairs(n_t, key_major):
    if key_major:
        pairs = [(qi, ki) for ki in range(n_t) for qi in range(ki, n_t)]
    else:
        pairs = [(qi, ki) for qi in range(n_t) for ki in range(qi + 1)]
    return (jnp.array([p[0] for p in pairs], jnp.int32), jnp.array([p[1] for p in pairs], jnp.int32))


def _with_side(main_ref, side_ref):
    return jnp.concatenate([main_ref[...], side_ref[...]], axis=1)


def _lane_const(t, lo, hi, value):
    lane = lax.broadcasted_iota(jnp.int32, (t, LANES), 1)
    return jnp.where((lane >= lo) & (lane < hi), value, 0.0).astype(BF16)


def _att_specs(t):
    qmain = pl.BlockSpec((t, HEAD_DIM), lambda h, p, qt, kt: (qt[p], h))
    kmain = pl.BlockSpec((t, HEAD_DIM), lambda h, p, qt, kt: (kt[p], h))
    qside = pl.BlockSpec((None, t, LANES), lambda h, p, qt, kt: (h, qt[p], 0))
    kside = pl.BlockSpec((None, t, LANES), lambda h, p, qt, kt: (h, kt[p], 0))
    return qmain, kmain, qside, kside


def _fox_fwd(q, qa, k, ka, v, qo, name):
    s = q.shape[0]
    t = _tile(s, ATT_TILE)
    sub = t // ATT_SPLIT
    qt, kt = _causal_pairs(s // t, key_major=False)

    def body(qt_ref, kt_ref, q_ref, qa_ref, k_ref, ka_ref, v_ref, og_ref, o_ref, y_ref, qab_ref, m_s, l_s, acc_s):
        pid = pl.program_id(1)
        qi, ki = qt_ref[pid], kt_ref[pid]

        @pl.when(ki == 0)
        def _():
            m_s[...] = jnp.full_like(m_s, NEG_INF)
            l_s[...] = jnp.zeros_like(l_s)
            acc_s[...] = jnp.zeros_like(acc_s)

        def step(diagonal):
            kc = _with_side(k_ref, ka_ref)
            vc = jnp.concatenate([v_ref[...], _lane_const(t, 0, 1, 1.0)], axis=1)
            for r in range(ATT_SPLIT):
                rows = slice(r * sub, (r + 1) * sub)
                n_k = (r + 1) * sub if diagonal else t
                sc = _dot_nt(jnp.concatenate([q_ref[rows], qa_ref[rows]], axis=1), kc[:n_k])
                if diagonal:
                    sc = jnp.where(lax.broadcasted_iota(jnp.int32, (sub, n_k), 1)
                                   <= lax.broadcasted_iota(jnp.int32, (sub, n_k), 0) + r * sub, sc, NEG_INF)
                m_old = m_s[rows]
                m_new = jnp.maximum(m_old, jnp.max(sc, axis=-1, keepdims=True))
                alpha = jnp.exp2(m_old - m_new)
                pv = _dot(jnp.exp2(sc - m_new[:, 0:1]).astype(BF16), vc[:n_k])
                acc_s[rows] = alpha * acc_s[rows] + pv[:, :HEAD_DIM]
                l_s[rows] = alpha * l_s[rows] + pv[:, HEAD_DIM:]
                m_s[rows] = m_new

        @pl.when(ki < qi)
        def _():
            step(False)

        @pl.when(ki == qi)
        def _():
            step(True)
            l = l_s[:, 0:1]
            o = acc_s[...] / l
            o_ref[...] = o
            y_ref[...] = (o * _sig(og_ref[...])).astype(BF16)
            lane = lax.broadcasted_iota(jnp.int32, (t, LANES), 1)
            qab_ref[...] = qa_ref[...] + _side(lane, Q_LSE, -(m_s[:, 0:1] + jnp.log2(l))).astype(BF16)

    qmain, kmain, qside, kside = _att_specs(t)
    return pl.pallas_call(
        body, name=name,
        grid_spec=pltpu.PrefetchScalarGridSpec(
            num_scalar_prefetch=2, grid=(HEADS, qt.shape[0]),
            in_specs=[qmain, qside, kmain, kside, kmain,
                      pl.BlockSpec((t, HEAD_DIM), lambda h, p, qt, kt: (qt[p], HEADS + h))],
            out_specs=[qmain, qmain, qside],
            scratch_shapes=[pltpu.VMEM((t, LANES), F32), pltpu.VMEM((t, LANES), F32), pltpu.VMEM((t, HEAD_DIM), F32)]),
        out_shape=[jax.ShapeDtypeStruct((s, D_MODEL), F32), jax.ShapeDtypeStruct((s, D_MODEL), BF16),
                   jax.ShapeDtypeStruct((HEADS, s, LANES), BF16)],
    )(qt, kt, q, qa, k, ka, v, qo)


def _fox_gate_bwd(o, qo, dy, name):
    s = o.shape[0]
    tm = _tile(s, ROW_TILE)

    def body(o_ref, og_ref, dy_ref, do_ref, dg_ref, dl_ref):
        ov, dyv = o_ref[...], dy_ref[...]
        sg = _sig(og_ref[...])
        do = (dyv * sg).astype(BF16)
        do_ref[...] = do
        dg_ref[...] = (dyv * ov * sg * (1.0 - sg)).astype(BF16)
        prod = do.astype(F32) * ov
        lane = lax.broadcasted_iota(jnp.int32, (tm, LANES), 1)
        for h in range(HEADS):
            delta = jnp.sum(prod[:, h * HEAD_DIM:(h + 1) * HEAD_DIM], axis=-1, keepdims=True)
            dl_ref[h] = _side(lane, 0, delta).astype(BF16)

    row = pl.BlockSpec((tm, D_MODEL), lambda i: (i, 0))
    return pl.pallas_call(
        body, name=name, grid=(s // tm,),
        in_specs=[row, pl.BlockSpec((tm, D_MODEL), lambda i: (i, 1)), row],
        out_specs=[row, row, pl.BlockSpec((HEADS, tm, LANES), lambda i: (0, i, 0))],
        out_shape=[jax.ShapeDtypeStruct((s, D_MODEL), BF16), jax.ShapeDtypeStruct((s, D_MODEL), BF16),
                   jax.ShapeDtypeStruct((HEADS, s, LANES), BF16)],
    )(o, qo, dy)


def _fox_bwd(q, qab, k, ka, v, do, doa, name):
    s = q.shape[0]
    t = _tile(s, ATT_TILE)
    n_t = s // t
    sub = t // ATT_SPLIT
    qt, kt = _causal_pairs(n_t, key_major=True)

    def body(qt_ref, kt_ref, q_ref, qab_ref, k_ref, ka_ref, v_ref, do_ref, doa_ref, dk_ref, dv_ref, dka_ref, dq_ref,
             dk_s, dv_s):
        pid = pl.program_id(1)
        qi, ki = qt_ref[pid], kt_ref[pid]

        @pl.when(pid == 0)
        def _():
            dq_ref[...] = jnp.zeros_like(dq_ref)

        @pl.when(qi == ki)
        def _():
            dk_s[...] = jnp.zeros_like(dk_s)
            dv_s[...] = jnp.zeros_like(dv_s)

        def step(diagonal):
            kc = _with_side(k_ref, ka_ref)
            vc = jnp.concatenate([v_ref[...], _lane_const(t, 0, 3, -1.0)], axis=1)
            for r in range(ATT_SPLIT):
                cols = slice(r * sub, (r + 1) * sub)
                n_k = (r + 1) * sub if diagonal else t
                qc = jnp.concatenate([q_ref[cols], qab_ref[cols]], axis=1)
                sc = _dot_nt(kc[:n_k], qc)
                if diagonal:
                    sc = jnp.where(lax.broadcasted_iota(jnp.int32, (n_k, sub), 0)
                                   <= lax.broadcasted_iota(jnp.int32, (n_k, sub), 1) + r * sub, sc, NEG_INF)
                p = jnp.exp2(sc)
                dp = _dot_nt(vc[:n_k], jnp.concatenate([do_ref[cols], doa_ref[cols]], axis=1))
                ds = (p * dp).astype(BF16)
                dv_s[0:n_k] += _dot(p.astype(BF16), do_ref[cols])
                dk_s[0:n_k] += _dot(ds, qc)
                q_rows = pl.ds(pl.multiple_of(qi * t + r * sub, sub), sub)
                dq_ref[q_rows, :] += _dot_tn(ds, kc[:n_k])

        @pl.when(qi > ki)
        def _():
            step(False)

        @pl.when(qi == ki)
        def _():
            step(True)

        @pl.when(qi == n_t - 1)
        def _():
            dk_ref[...] = dk_s[:, :HEAD_DIM] * (1.0 / LOG2E)
            dka_ref[...] = dk_s[:, HEAD_DIM:]
            dv_ref[...] = dv_s[...].astype(BF16)

    qmain, kmain, qside, kside = _att_specs(t)
    return pl.pallas_call(
        body, name=name,
        grid_spec=pltpu.PrefetchScalarGridSpec(
            num_scalar_prefetch=2, grid=(HEADS, qt.shape[0]),
            in_specs=[qmain, qside, kmain, kside, kmain, qmain, qside],
            out_specs=[kmain, pl.BlockSpec((None, t, HEAD_DIM), lambda h, p, qt, kt: (0, kt[p], h)), kside,
                       pl.BlockSpec((None, s, 2 * HEAD_DIM), lambda h, p, qt, kt: (h, 0, 0))],
            scratch_shapes=[pltpu.VMEM((t, 2 * HEAD_DIM), F32), pltpu.VMEM((t, HEAD_DIM), F32)]),
        out_shape=[jax.ShapeDtypeStruct((s, D_MODEL), F32), jax.ShapeDtypeStruct((1, s, D_MODEL), BF16),
                   jax.ShapeDtypeStruct((HEADS, s, LANES), F32), jax.ShapeDtypeStruct((HEADS, s, 2 * HEAD_DIM), F32)],
    )(qt, kt, q, qab, k, ka, v, do, doa)


def _mm_residual_premix(a, w, x, gate, mods, name):
    s, k = a.shape
    dm = x.shape[1]
    tm = _tile(s, ROW_TILE)

    def body(*refs):
        a_ref, w_ref, x_ref, g_ref = refs[:4]
        mod_refs = refs[4:4 + 2 * len(mods)]
        y_ref, xn_ref = refs[4 + 2 * len(mods):6 + 2 * len(mods)]
        h_refs = refs[6 + 2 * len(mods):]
        y = _dot(a_ref[...], w_ref[0])
        y_ref[...] = y
        xv = x_ref[...] + g_ref[...] * y
        xn_ref[...] = xv
        nrm = xv * lax.rsqrt(jnp.mean(xv * xv, axis=-1, keepdims=True) + EPS)
        for t, h_ref in enumerate(h_refs):
            h_ref[...] = (nrm * (1.0 + mod_refs[2 * t + 1][...]) + mod_refs[2 * t][...]).astype(BF16)

    row = pl.BlockSpec((tm, dm), lambda i: (i, 0))
    vec = _full((1, dm))
    outs = pl.pallas_call(
        body, name=name, grid=(s // tm,),
        in_specs=[pl.BlockSpec((tm, k), lambda i: (i, 0)), _full(w.shape), row, vec] + [vec] * (2 * len(mods)),
        out_specs=[row] * (2 + len(mods)),
        out_shape=[jax.ShapeDtypeStruct((s, dm), F32)] * 2 + [jax.ShapeDtypeStruct((s, dm), BF16)] * len(mods),
    )(a, w, x, gate, *[v for m in mods for v in m])
    return outs[0], outs[1], list(outs[2:])


def _mm_loss_head(a, w, x, gate, target, name):
    s, k = a.shape
    dm = x.shape[1]
    tm = _tile(s, ROW_TILE)

    def body(a_ref, w_ref, x_ref, g_ref, t_ref, sq_ref, do_ref, dy_ref, dg_ref):
        @pl.when(pl.program_id(0) == 0)
        def _():
            sq_ref[...] = jnp.zeros_like(sq_ref)
            dg_ref[...] = jnp.zeros_like(dg_ref)

        y, gv = _dot(a_ref[...], w_ref[0]), g_ref[...]
        err = x_ref[...] + gv * y - t_ref[...]
        sq_ref[...] += _colsum8(err * err)
        dout = err * (1.0 / dm)
        do_ref[...] = dout
        dy_ref[0] = (dout * gv).astype(BF16)
        dg_ref[...] += _colsum8(dout * y)

    row = pl.BlockSpec((tm, dm), lambda i: (i, 0))
    acc = _full((SUBLANES, dm))
    return pl.pallas_call(
        body, name=name, grid=(s // tm,),
        in_specs=[pl.BlockSpec((tm, k), lambda i: (i, 0)), _full(w.shape), row, _full((1, dm)), row],
        out_specs=[acc, row, pl.BlockSpec((1, tm, dm), lambda i: (0, i, 0)), acc],
        out_shape=[jax.ShapeDtypeStruct((SUBLANES, dm), F32), jax.ShapeDtypeStruct((s, dm), F32),
                   jax.ShapeDtypeStruct((1, s, dm), BF16), jax.ShapeDtypeStruct((SUBLANES, dm), F32)],
    )(a, w, x, gate, target)


def _ffn_inner(h, w_up, conv_w, conv_b, tag):
    u = _mm_nn(h, w_up, 2, BF16, tag + "_up")
    return u, _convglu_fwd(u, conv_w, conv_b, tag + "_convglu")


def _weight_grad_first(a, d, p_n, name):
    return lax.optimization_barrier((_mm_tn(a, d, p_n, name), d))


def _ffn_backward(dx_out, dffn, x_mid, scale, saved, w_up, conv_w, conv_b, w_down, mixer, tag):
    h, u, a = saved
    dw_down, dffn = _weight_grad_first(a, dffn, 1, tag + "_down_dw")
    da = _mm_nt(dffn, w_down, BF16, tag + "_down_dx")
    du, dconv = _convglu_bwd(u, da, conv_w, conv_b, tag + "_convglu_bwd")
    dw_up, du = _weight_grad_first(h, du, N_CHIPS, tag + "_up_dw")
    dx_mid, [(dshift, dscale)], dy, dgate_mixer = _premix_bwd(x_mid, [(scale, [(du, w_up)])], dx_out,
                                                              tag + "_premix_bwd", branch=mixer)
    return dx_mid, dy, dgate_mixer, dw_up, dw_down, dict(shift=dshift, scale=dscale, conv=dconv)


def _local_step(x, target, mods, lb, vecs, weights_at):
    m0, m1, mk = mods["l0"], mods["l1"], mods["kv"]
    h0 = _premix(x, m0[0], m0[1], "l0_premix")
    wts, h0 = weights_at("mixer0", h0)
    proj = _mm_nn(h0, wts["a_w_in"], 1, F32, "l0_in")[0]
    o_a, yp, states = _hgrn_fwd(proj, lb, vecs["a_norm_g"], "l0_hgrn")
    more, yp = weights_at("ffn0", yp)
    wts.update(more)
    y0, x1, [hf0] = _mm_residual_premix(yp, wts["a_w_out"], x, m0[2], [(m0[3], m0[4])], "l0_out")
    u0, a0 = _ffn_inner(hf0, wts["up0"], vecs["conv_w0"], vecs["conv_b0"], "l0_ffn")
    saved0 = (hf0, u0, a0)
    ffn0, x2, [hk, h1] = _mm_residual_premix(a0, wts["down0"], x1, m0[5], [(mk[0], mk[1]), (m1[0], m1[1])],
                                             "l0_ffn_down")
    more, hk = weights_at("layer1", hk)
    wts.update(more)
    k_raw = _mm_nn(hk, wts["kv_k"], 1, F32, "kv_k")[0]
    v_sh = _mm_nn(hk, wts["kv_v"], 1, BF16, "kv_v")[0]
    f_raw = _mm_nn(hk, wts["kv_f"], 1, F32, "kv_f")[0]
    k_sh = _headnorm(k_raw, vecs["k_norm_g"], 1.0, "kv_knorm")
    qa, ka = _fcum_fwd(f_raw, vecs["kv_b_f"], "kv_fcum")
    qo = _mm_nn(h1, wts["b_w_q"], 1, F32, "l1_q")[0]
    q_scale = HEAD_DIM ** -0.5
    q = _headnorm(qo, vecs["q_norm_g"], q_scale * LOG2E, "l1_qnorm")
    o_b, og, qab = _fox_fwd(q, qa, k_sh, ka, v_sh, qo, "l1_fox")
    y1, x3, [hf1] = _mm_residual_premix(og, wts["b_w_out"], x2, m1[2], [(m1[3], m1[4])], "l1_out")
    u1, a1 = _ffn_inner(hf1, wts["up1"], vecs["conv_w1"], vecs["conv_b1"], "l1_ffn")
    saved1 = (hf1, u1, a1)
    sq, dx4, dffn1, dg2_1 = _mm_loss_head(a1, wts["down1"], x3, m1[5], target, "l1_ffn_down")

    big, small = {}, {}
    dx3, dy1, dg1_1, big["up1"], big["down1"], s_ffn1 = _ffn_backward(
        dx4, dffn1, x3, m1[4], saved1, wts["up1"], vecs["conv_w1"], vecs["conv_b1"], wts["down1"], (y1, m1[2]), "l1_ffn")
    big["b_w_out"], dy1 = _weight_grad_first(og, dy1, 1, "l1_out_dw")
    d_og = _mm_nt(dy1, wts["b_w_out"], F32, "l1_out_dx")
    do_b, dgate_b, doa = _fox_gate_bwd(o_b, qo, d_og, "l1_fox_gate_bwd")
    dk, dv, dka, dq = _fox_bwd(q, qab, k_sh, ka, v_sh, do_b, doa, "l1_fox_bwd")
    dqo, dqg = _headnorm_bwd(qo, vecs["q_norm_g"], q_scale, dq, "l1_qnorm_bwd", extra=dgate_b)
    big["b_w_q"], dqo = _weight_grad_first(h1, dqo, N_CHIPS, "l1_q_dw")
    dk_raw, dkg = _headnorm_bwd(k_raw, vecs["k_norm_g"], 1.0, dk, "kv_knorm_bwd")
    dz, dbf = _fcum_bwd(f_raw, vecs["kv_b_f"], dka, dq, "kv_fcum_bwd")
    big["kv_k"], dk_raw = _weight_grad_first(hk, dk_raw, 1, "kv_k_dw")
    big["kv_v"], dv = _weight_grad_first(hk, dv, 1, "kv_v_dw")
    big["kv_f"], dz = _weight_grad_first(hk, dz, 1, "kv_f_dw")
    kv_pairs = [(dk_raw, wts["kv_k"]), (dv, wts["kv_v"]), (dz, wts["kv_f"])]
    dx2, [(dsh1_1, dsc1_1), (dshk, dsck)], dffn0, dg2_0 = _premix_bwd(
        x2, [(m1[1], [(dqo, wts["b_w_q"])]), (mk[1], kv_pairs)], dx3, "l1_kv_premix_bwd", branch=(ffn0, m0[5]))
    dx1, dy0, dg1_0, big["up0"], big["down0"], s_ffn0 = _ffn_backward(
        dx2, dffn0, x1, m0[4], saved0, wts["up0"], vecs["conv_w0"], vecs["conv_b0"], wts["down0"], (y0, m0[2]), "l0_ffn")
    big["a_w_out"], dy0 = _weight_grad_first(yp, dy0, 1, "l0_out_dw")
    dyp = _mm_nt(dy0, wts["a_w_out"], F32, "l0_out_dx")
    dproj, dlb, dng = _hgrn_bwd(proj, lb, vecs["a_norm_g"], o_a, states, dyp, "l0_hgrn_bwd")
    big["a_w_in"], dproj = _weight_grad_first(h0, dproj, N_CHIPS, "l0_in_dw")
    grad_x, [(dsh1_0, dsc1_0)] = _premix_bwd(x, [(m0[1], [(dproj, wts["a_w_in"])])], dx1, "l0_premix_bwd")

    small["mod_l0"] = [dsh1_0, dsc1_0, dg1_0, s_ffn0["shift"], s_ffn0["scale"], dg2_0]
    small["mod_l1"] = [dsh1_1, dsc1_1, dg1_1, s_ffn1["shift"], s_ffn1["scale"], dg2_1]
    small["mod_kv"] = [dshk, dsck]
    small["conv0"], small["conv1"] = s_ffn0["conv"], s_ffn1["conv"]
    small["a_norm_g"], small["k_norm_g"], small["q_norm_g"] = dng, dkg, dqg
    small["kv_b_f"], small["lb"] = dbf, dlb
    marks = {"attention_bwd": dk, "ffn0_bwd": dx1, "mixer0_bwd": grad_x}
    return sq, grad_x, big, small, marks


HBM = pl.BlockSpec(memory_space=pltpu.HBM)
COMM_CHUNK_ELEMS = 256 * 1024


def _place():
    x, y, c = lax.axis_index("x"), lax.axis_index("y"), lax.axis_index("c")
    chips = [(1 - x, y), (x, 1 - y), (1 - x, 1 - y)]
    return x, y, c, (x, y, 1 - c), chips


def _chunk_rows(rows, cols):
    best = BF16_ROWS
    for r in range(BF16_ROWS, rows + 1, BF16_ROWS):
        if rows % r == 0 and r * cols <= COMM_CHUNK_ELEMS:
            best = r
    assert rows % best == 0, (rows, cols)
    return best


def _allgather8(block, name):
    m_per, n = block.shape

    def body(x_ref, out_ref, send_sems, recv_sems, local_sem):
        x, y, c, sibling, chips = _place()
        me = (x, y, c)

        def rows(px, py, pc):
            return out_ref.at[pl.ds((4 * px + 2 * py + pc) * m_per, m_per), :]

        def copy(k, blk, to, src=None):
            return pltpu.make_async_remote_copy(
                src_ref=rows(*blk) if src is None else src, dst_ref=rows(*blk),
                send_sem=send_sems.at[k], recv_sem=recv_sems.at[k], device_id=to, device_id_type=MESH)

        mine = pltpu.make_async_copy(x_ref, rows(*me), local_sem)
        mine.start()
        first = [copy(0, me, sibling, src=x_ref)]
        first += [copy(1 + j, me, (*chip, c), src=x_ref) for j, chip in enumerate(chips)]
        for cp in first:
            cp.start()
        passed = [copy(4 + j, (*chip, c), sibling) for j, chip in enumerate(chips)]
        for j, chip in enumerate(chips):
            copy(1 + j, (*chip, c), me).wait_recv()
            passed[j].start()
        copy(0, sibling, me).wait_recv()
        for j, chip in enumerate(chips):
            copy(4 + j, (*chip, 1 - c), me).wait_recv()
        for cp in first + passed:
            cp.wait_send()
        mine.wait()

    return pl.pallas_call(
        body, name=name, out_shape=jax.ShapeDtypeStruct((N_DEV * m_per, n), block.dtype),
        in_specs=[pl.BlockSpec(memory_space=pltpu.VMEM)], out_specs=pl.BlockSpec(memory_space=pltpu.VMEM),
        scratch_shapes=[pltpu.SemaphoreType.DMA((7,)), pltpu.SemaphoreType.DMA((7,)), pltpu.SemaphoreType.DMA],
    )(block)


def _cast_own_block(shards, layer, chip, name):
    _, r, cols = shards.shape
    rows = _chunk_rows(r, cols)

    def body(chip_ref, w_ref, o_ref):
        o_ref[...] = w_ref[...].astype(BF16)

    return pl.pallas_call(
        body, name=name,
        grid_spec=pltpu.PrefetchScalarGridSpec(
            num_scalar_prefetch=1, grid=(r // rows,),
            in_specs=[pl.BlockSpec((None, rows, cols), lambda i, chip_ref: (layer, i, 0))],
            out_specs=pl.BlockSpec((None, rows, cols), lambda i, chip_ref: (chip_ref[0], i, 0))),
        out_shape=jax.ShapeDtypeStruct((N_CHIPS, r, cols), BF16),
    )(chip, shards)


def _sequencer_gather(bufs, name, collective_id):
    n_t = len(bufs)
    dims = [b.shape[1:] for b in bufs]
    refs = [jax.new_ref(b, memory_space=pltpu.MemorySpace.HBM) for b in bufs]

    @pl.kernel(mesh=plsc.ScalarSubcoreMesh(axis_name="sequencer", num_cores=1), name=name,
               scratch_types=[pltpu.SemaphoreType.DMA((n_t,))] * 4,
               compiler_params=pltpu.CompilerParams(collective_id=collective_id))
    def launch(send_ici, recv_ici, send_d2d, recv_d2d):
        x, y, c, sibling, chips = _place()
        p_me = 2 * x + y
        peers = [sibling] + [(cx, cy, c) for cx, cy in chips]
        barrier = pltpu.get_barrier_semaphore()
        for peer in peers:
            pl.semaphore_signal(barrier, inc=1, device_id=peer, device_id_type=MESH)
        pl.semaphore_wait(barrier, len(peers))

        def waiter(t, sem_s, sem_r):
            win = refs[t].at[pl.ds(0, 3), pl.ds(0, dims[t][0] // 2), :]
            return pltpu.make_async_remote_copy(src_ref=win, dst_ref=win, send_sem=sem_s.at[t], recv_sem=sem_r.at[t],
                                                device_id=sibling, device_id_type=MESH)

        def half_copy(t, chip_idx, to, sem_s, sem_r):
            r2 = dims[t][0] // 2
            win = refs[t].at[chip_idx, pl.ds(c * r2, r2), :]
            return pltpu.make_async_remote_copy(src_ref=win, dst_ref=win, send_sem=sem_s.at[t], recv_sem=sem_r.at[t],
                                                device_id=to, device_id_type=MESH)

        for t in range(n_t):
            for cx, cy in chips:
                half_copy(t, p_me, (cx, cy, c), send_ici, recv_ici).start()
        for t in range(n_t):
            waiter(t, send_ici, recv_ici).wait_recv()
            for cx, cy in chips:
                half_copy(t, 2 * cx + cy, sibling, send_d2d, recv_d2d).start()
        for t in range(n_t):
            waiter(t, send_d2d, recv_d2d).wait_recv()
            waiter(t, send_ici, recv_ici).wait_send()
            waiter(t, send_d2d, recv_d2d).wait_send()

    launch()
    return [r[...] for r in refs]


def _sequencer_allgather8(block, dev, name, collective_id):
    m_per, n = block.shape
    src = jax.new_ref(block, memory_space=pltpu.MemorySpace.HBM)
    out = jax.empty_ref(jax.ShapeDtypeStruct((N_DEV * m_per, n), block.dtype), memory_space=pltpu.MemorySpace.HBM)

    @pl.kernel(mesh=plsc.ScalarSubcoreMesh(axis_name="sequencer", num_cores=1), name=name,
               scratch_types=[pltpu.SemaphoreType.DMA((7,))] * 2,
               compiler_params=pltpu.CompilerParams(collective_id=collective_id))
    def launch(send_sems, recv_sems):
        x, y, c, sibling, chips = _place()
        me = (x, y, c)
        _handshake([sibling] + [(cx, cy, c) for cx, cy in chips])

        def rows(px, py, pc):
            return out.at[pl.ds((4 * px + 2 * py + pc) * m_per, m_per), :]

        def copy(k, blk, to, from_src=False):
            return pltpu.make_async_remote_copy(
                src_ref=src if from_src else rows(*blk), dst_ref=rows(*blk),
                send_sem=send_sems.at[k], recv_sem=recv_sems.at[k], device_id=to, device_id_type=MESH)

        first = [copy(0, me, sibling, True)] + [copy(1 + j, me, (*chip, c), True) for j, chip in enumerate(chips)]
        for cp in first:
            cp.start()
        passed = [copy(4 + j, (*chip, c), sibling) for j, chip in enumerate(chips)]
        for j, chip in enumerate(chips):
            copy(1 + j, (*chip, c), me).wait_recv()
            passed[j].start()
        copy(0, sibling, me).wait_recv()
        for j, chip in enumerate(chips):
            copy(4 + j, (*chip, 1 - c), me).wait_recv()
        for cp in first + passed:
            cp.wait_send()

    launch()
    return lax.dynamic_update_slice(out[...], block, (dev * m_per, 0))


def _others():
    x, y, c = lax.axis_index("x"), lax.axis_index("y"), lax.axis_index("c")
    flip = lambda v, f: 1 - v if f else v
    return [(flip(x, fx), flip(y, fy), flip(c, fc))
            for fx in (0, 1) for fy in (0, 1) for fc in (0, 1) if (fx, fy, fc) != (0, 0, 0)]


def _handshake(peers):
    barrier = pltpu.get_barrier_semaphore()
    for peer in peers:
        pl.semaphore_signal(barrier, inc=1, device_id=peer, device_id_type=MESH)
    pl.semaphore_wait(barrier, len(peers))


def _sequencer_scatter(parts, name, collective_id):
    n_t = len(parts)
    dims = [p.shape[1:] for p in parts]
    srcs = [jax.new_ref(p, memory_space=pltpu.MemorySpace.HBM) for p in parts]
    inboxes = [jax.empty_ref(jax.ShapeDtypeStruct((N_DEV, r // 2, cols), BF16), memory_space=pltpu.MemorySpace.HBM)
               for r, cols in dims]

    @pl.kernel(mesh=plsc.ScalarSubcoreMesh(axis_name="sequencer", num_cores=1), name=name,
               scratch_types=[pltpu.SemaphoreType.DMA((n_t,))] * 2,
               compiler_params=pltpu.CompilerParams(collective_id=collective_id))
    def launch(send_sem, recv_sem):
        x, y, c = lax.axis_index("x"), lax.axis_index("y"), lax.axis_index("c")
        me = 4 * x + 2 * y + c
        peers = _others()
        _handshake(peers)
        for t in range(n_t):
            h = dims[t][0] // 2
            for qx, qy, qc in peers:
                pltpu.make_async_remote_copy(
                    src_ref=srcs[t].at[2 * qx + qy, pl.ds(qc * h, h), :], dst_ref=inboxes[t].at[me],
                    send_sem=send_sem.at[t], recv_sem=recv_sem.at[t], device_id=(qx, qy, qc), device_id_type=MESH).start()
        for t in range(n_t):
            win = inboxes[t].at[pl.ds(0, N_DEV - 1)]
            both = pltpu.make_async_remote_copy(src_ref=win, dst_ref=win, send_sem=send_sem.at[t],
                                                recv_sem=recv_sem.at[t], device_id=peers[0], device_id_type=MESH)
            both.wait_recv()
            both.wait_send()

    launch()
    return [b[...] for b in inboxes]


def _sum_pieces(part, inbox, place, name):
    _, r, cols = part.shape
    h = r // 2
    rows = _chunk_rows(h, cols)
    steps = h // rows

    def body(place_ref, own_ref, in_ref, o_ref):
        dev = place_ref[2]
        own = own_ref[...].astype(F32)
        acc = jnp.zeros((rows, cols), F32)
        for d in range(N_DEV):
            acc = acc + jnp.where(dev == d, own, in_ref[d].astype(F32))
        o_ref[...] = acc

    return pl.pallas_call(
        body, name=name,
        grid_spec=pltpu.PrefetchScalarGridSpec(
            num_scalar_prefetch=1, grid=(steps,),
            in_specs=[pl.BlockSpec((None, rows, cols), lambda i, pr: (pr[0], pr[1] * steps + i, 0)),
                      pl.BlockSpec((N_DEV, rows, cols), lambda i, pr: (0, i, 0))],
            out_specs=pl.BlockSpec((rows, cols), lambda i, pr: (pr[1] * steps + i, 0))),
        out_shape=jax.ShapeDtypeStruct((r, cols), F32),
    )(place, part, inbox)


def _sequencer_swap_halves(halves, name, collective_id):
    n_t = len(halves)
    refs = [jax.new_ref(a, memory_space=pltpu.MemorySpace.HBM) for a in halves]

    @pl.kernel(mesh=plsc.ScalarSubcoreMesh(axis_name="sequencer", num_cores=1), name=name,
               scratch_types=[pltpu.SemaphoreType.DMA((n_t,))] * 2,
               compiler_params=pltpu.CompilerParams(collective_id=collective_id))
    def launch(send_sem, recv_sem):
        x, y, c = lax.axis_index("x"), lax.axis_index("y"), lax.axis_index("c")
        sibling = (x, y, 1 - c)
        _handshake([sibling])
        copies = []
        for t in range(n_t):
            h = halves[t].shape[0] // 2
            win = refs[t].at[pl.ds(c * h, h), :]
            copies.append(pltpu.make_async_remote_copy(src_ref=win, dst_ref=win, send_sem=send_sem.at[t],
                                                       recv_sem=recv_sem.at[t], device_id=sibling, device_id_type=MESH))
            copies[-1].start()
        for cp in copies:
            cp.wait()

    launch()
    return [r[...] for r in refs]


def _cond_rows(c16, w, act, name):
    n_l, dm, wid = w.shape

    def body(c_ref, w_ref, o_ref, a_ref):
        cv = c_ref[...]
        if act:
            cv = cv * _sig(cv)
        a_ref[...] = cv
        o_ref[...] = _dot_f32(cv, w_ref[...])

    return pl.pallas_call(
        body, name=name, grid=(n_l,),
        in_specs=[_full((16, dm)), pl.BlockSpec((None, dm, wid), lambda l: (l, 0, 0))],
        out_specs=[pl.BlockSpec((None, 16, wid), lambda l: (l, 0, 0)), _full((16, dm))],
        out_shape=[jax.ShapeDtypeStruct((n_l, 16, wid), F32), jax.ShapeDtypeStruct((16, dm), F32)],
    )(c16, w)


def _outer_grad(ct, dm, name):
    n_l, kk, wid = dm.shape
    d_rows = ct.shape[0]

    def body(c_ref, d_ref, o_ref):
        o_ref[...] = _dot_f32(c_ref[...], d_ref[...])

    return pl.pallas_call(
        body, name=name, grid=(n_l,),
        in_specs=[_full((d_rows, kk)), pl.BlockSpec((None, kk, wid), lambda l: (l, 0, 0))],
        out_specs=pl.BlockSpec((None, d_rows, wid), lambda l: (l, 0, 0)),
        out_shape=jax.ShapeDtypeStruct((n_l, d_rows, wid), F32),
    )(ct, dm)


def _sum_devices(g, name):
    rows, n = g.shape

    def body(g_ref, o_ref):
        acc = g_ref[0:SUBLANES, :]
        for dev in range(1, N_DEV):
            acc = acc + g_ref[dev * SUBLANES:(dev + 1) * SUBLANES, :]
        o_ref[...] = acc

    return pl.pallas_call(body, name=name, out_shape=jax.ShapeDtypeStruct((SUBLANES, n), F32))(g)


def _adamw(w, g, m, v, name):
    shape = w.shape
    cols = shape[-1]
    rows = w.size // cols
    tr = rows
    for cand in range(SUBLANES, min(rows, 256) + 1, SUBLANES):
        if rows % cand == 0:
            tr = cand
    if rows * cols <= COMM_CHUNK_ELEMS:
        tr = rows
    c1 = 1.0 / (1.0 - ADAM_B1 ** ADAM_STEP)
    c2 = 1.0 / (1.0 - ADAM_B2 ** ADAM_STEP)

    def body(w_ref, g_ref, m_ref, v_ref, d_ref, mo_ref, vo_ref):
        gv = g_ref[...]
        m_new = ADAM_B1 * m_ref[...] + (1.0 - ADAM_B1) * gv
        v_new = ADAM_B2 * v_ref[...] + (1.0 - ADAM_B2) * (gv * gv)
        mo_ref[...] = m_new
        vo_ref[...] = v_new
        d_ref[...] = -ADAM_LR * ((m_new * c1) / (jnp.sqrt(v_new * c2) + ADAM_EPS) + ADAM_WD * w_ref[...])

    spec = pl.BlockSpec((tr, cols), lambda i: (i, 0))
    outs = pl.pallas_call(
        body, name=name, grid=(rows // tr,), in_specs=[spec] * 4, out_specs=[spec] * 3,
        out_shape=[jax.ShapeDtypeStruct((rows, cols), F32)] * 3,
    )(*[a.reshape(rows, cols) for a in (w, g, m, v)])
    return tuple(o.reshape(shape) for o in outs)


def _pad_cols(a, cols):
    return jnp.pad(a, [(0, 0)] * (a.ndim - 1) + [(0, cols - a.shape[-1])])


def _flat8(parts, width):
    v = jnp.concatenate([p.reshape(-1) for p in parts])
    return jnp.pad(v, (0, width - v.shape[0])).reshape(SUBLANES, width // SUBLANES)


KV_SHARD = 514
KV_SHARD_PAD = 640
BIG = ("a_w_in", "a_w_out", "kv_w", "b_w_q", "b_w_out", "up0", "up1", "down0", "down1")


def kernel(x, c, ada_w, ada_b, a_w_in, a_lb_logits, a_norm_g, a_w_out, kv_ada_w, kv_ada_b, kv_w, kv_b_f, k_norm_g, b_w_q, q_norm_g, b_w_out, ffn_w_up, ffn_conv_w, ffn_conv_b, ffn_w_down, loss_target, m_ada_w, m_ada_b, m_a_w_in, m_a_lb_logits, m_a_norm_g, m_a_w_out, m_kv_ada_w, m_kv_ada_b, m_kv_w, m_kv_b_f, m_k_norm_g, m_b_w_q, m_q_norm_g, m_b_w_out, m_ffn_w_up, m_ffn_conv_w, m_ffn_conv_b, m_ffn_w_down, v_ada_w, v_ada_b, v_a_w_in, v_a_lb_logits, v_a_norm_g, v_a_w_out, v_kv_ada_w, v_kv_ada_b, v_kv_w, v_kv_b_f, v_k_norm_g, v_b_w_q, v_q_norm_g, v_b_w_out, v_ffn_w_up, v_ffn_conv_w, v_ffn_conv_b, v_ffn_w_down):
    dm, ff = D_MODEL, D_FF
    ix, iy, ic = lax.axis_index("x"), lax.axis_index("y"), lax.axis_index("c")
    chip = 2 * ix + iy
    dev = 2 * chip + ic

    w1 = 10240
    g1 = _allgather8(_flat8([c, a_lb_logits, ffn_conv_w], w1), "gather_cond").reshape(N_DEV, w1)
    c_all = g1[:, :dm]
    per_chip = g1[0::2]
    lb_logits = per_chip[:, dm:dm + 512].reshape(N_CHIPS, 2, 256).transpose(1, 0, 2).reshape(2, dm)
    conv_w = per_chip[:, dm + 512:dm + 512 + 2 * CONV_W * FFN_COLS].reshape(N_CHIPS, 2, CONV_W, FFN_COLS)
    conv_w = conv_w.transpose(1, 2, 0, 3).reshape(2, CONV_W, 2, ff).transpose(0, 2, 1, 3)
    conv_b = ffn_conv_b.reshape(2, 2, 1, ff)
    lb = jax.nn.softmax(lb_logits, axis=0)[0:1]

    c16 = jnp.pad(c_all, ((0, 8), (0, 0)))
    mod_ada, c_act16 = _cond_rows(c16, ada_w, True, "mod_ada")
    mod_kv, _ = _cond_rows(c16, kv_ada_w[None], True, "mod_kv")
    mine = jnp.concatenate([mod_ada[0, :8], mod_ada[1, :8], mod_kv[0, :8]], axis=1)
    w2 = mine.shape[1]
    g2 = _allgather8(mine, "gather_mod").reshape(N_DEV, 8, w2)[0::2]
    my_rows = lax.dynamic_index_in_dim(g2, dev, axis=1, keepdims=False)
    mod0 = my_rows[:, 0:1536].reshape(6 * dm) + ada_b[0]
    mod1 = my_rows[:, 1536:3072].reshape(6 * dm) + ada_b[1]
    modk = my_rows[:, 3072:3584].reshape(2 * dm) + kv_ada_b
    mods = {"l0": [v.reshape(1, dm) for v in jnp.split(mod0, 6)],
            "l1": [v.reshape(1, dm) for v in jnp.split(mod1, 6)],
            "kv": [v.reshape(1, dm) for v in jnp.split(modk, 2)]}

    local = [(a_w_in, 0), (a_w_out, 0), (_pad_cols(kv_w, KV_SHARD_PAD)[None], 0), (b_w_q, 0), (b_w_out, 0),
             (ffn_w_up, 0), (ffn_w_up, 1), (ffn_w_down, 0), (ffn_w_down, 1)]
    chip_arr = chip.reshape(1).astype(jnp.int32)
    own = {n: _cast_own_block(w, layer, chip_arr, "cast_" + n) for n, (w, layer) in zip(BIG, local)}
    stages = {"mixer0": ("a_w_in",), "ffn0": ("a_w_out", "up0", "down0"),
              "layer1": ("kv_w", "b_w_q", "b_w_out", "up1", "down1")}
    arriving = {st: _sequencer_gather([own[n] for n in names], "gather_" + st, cid)
                for cid, (st, names) in enumerate(stages.items(), start=1)}
    rowwise = lambda g: g.reshape(1, -1, dm)

    def weights_at(stage, token):
        got, token = lax.optimization_barrier((arriving[stage], token))
        g = dict(zip(stages[stage], got))
        if stage == "mixer0":
            return {"a_w_in": g["a_w_in"]}, token
        if stage == "ffn0":
            return {"a_w_out": rowwise(g["a_w_out"]), "up0": g["up0"], "down0": rowwise(g["down0"])}, token
        kv_full = g["kv_w"][:, :, :KV_SHARD].transpose(1, 0, 2).reshape(dm, N_CHIPS * KV_SHARD)
        return {"kv_k": kv_full[None, :, :dm], "kv_v": kv_full[None, :, dm:2 * dm],
                "kv_f": _pad_cols(kv_full[None, :, 2 * dm:], LANES), "b_w_q": g["b_w_q"],
                "b_w_out": rowwise(g["b_w_out"]), "up1": g["up1"], "down1": rowwise(g["down1"])}, token

    vecs = {"a_norm_g": jnp.tile(a_norm_g, (1, HEADS)), "k_norm_g": jnp.tile(k_norm_g[None], (1, HEADS)),
            "q_norm_g": jnp.tile(q_norm_g, (1, HEADS)), "kv_b_f": _pad_cols(kv_b_f[None], LANES),
            "conv_w0": conv_w[0], "conv_b0": conv_b[0], "conv_w1": conv_w[1], "conv_b1": conv_b[1]}

    sq, grad_x, big, small, marks = _local_step(x[0], loss_target[0], mods, lb, vecs, weights_at)
    loss = lax.psum(0.5 * jnp.sum(sq) / dm, ("x", "y", "c"))

    kv_grad = jnp.concatenate([big["kv_k"][0], big["kv_v"][0], big["kv_f"][0][:, :HEADS]], axis=1)
    kv_grad = _pad_cols(kv_grad.reshape(dm, N_CHIPS, KV_SHARD).transpose(1, 0, 2), KV_SHARD_PAD)
    chipwise = lambda g: g.reshape(N_CHIPS, -1, dm)
    parts = dict(zip(BIG, [big["a_w_in"], chipwise(big["a_w_out"]), kv_grad, big["b_w_q"], chipwise(big["b_w_out"]),
                           big["up0"], big["up1"], chipwise(big["down0"]), chipwise(big["down1"])]))
    place = jnp.stack([chip, ic, dev]).astype(jnp.int32)

    served = []

    groups = (("up1", "down1"), ("b_w_out", "b_w_q", "kv_w"), ("up0", "down0", "a_w_out"), ("a_w_in",))

    def scatter_group(k):
        mine = [parts[n] for n in groups[k]]
        if served:
            mine, _ = lax.optimization_barrier((mine, served[-1]))
        served.append(_sequencer_scatter(mine, "scatter_grads_%d" % k, 4 + k))

    def sum_group(k, token):
        inboxes, _ = lax.optimization_barrier((served[k], token))
        return [_sum_pieces(parts[n], box, place, "sum_" + n) for n, box in zip(groups[k], inboxes)]

    def swap_group(k, halves, behind):
        halves, _ = lax.optimization_barrier((halves, behind))
        return dict(zip(groups[k], _sequencer_swap_halves(halves, "swap_grads_%d" % k, 8 + k)))

    for k in range(4):
        scatter_group(k)
    halves = [sum_group(0, marks["attention_bwd"]), sum_group(1, marks["ffn0_bwd"]), sum_group(2, marks["mixer0_bwd"])]

    fold = lambda a: a.sum(axis=0)
    heads = lambda a: fold(a).reshape(HEADS, HEAD_DIM).sum(axis=0)
    conv_flat = lambda a: a.sum(axis=2).transpose(1, 0, 2)
    pieces = ([fold(a) for a in small["mod_l0"]] + [fold(a) for a in small["mod_l1"]] + [fold(a) for a in small["mod_kv"]]
              + [conv_flat(small["conv0"]), conv_flat(small["conv1"]), heads(small["a_norm_g"]), heads(small["k_norm_g"]),
                 heads(small["q_norm_g"]), fold(small["kv_b_f"]), fold(small["lb"])])
    w3 = 61440
    small_vec, _ = lax.optimization_barrier((_flat8(pieces, w3), served[3]))
    g3 = _sequencer_allgather8(small_vec, dev, "gather_small", 12)
    rs = {}
    for k in range(3):
        rs.update(swap_group(k, halves[k], g3))
    tot = _sum_devices(g3, "sum_small").reshape(w3)
    n_mod = 14 * dm
    dmod_all = g3.reshape(N_DEV, w3)[:, :n_mod]
    o = n_mod
    conv_tot = [tot[o + l * 8 * ff: o + (l + 1) * 8 * ff].reshape(4, 2 * ff) for l in range(2)]
    o += 16 * ff
    g_a_norm, g_k_norm, g_q_norm = (tot[o + i * HEAD_DIM: o + (i + 1) * HEAD_DIM] for i in range(3))
    o += 3 * HEAD_DIM
    g_kv_b_f = tot[o:o + HEADS]
    dlb = tot[o + LANES:o + LANES + dm]

    ct = _pad_cols(c_act16[:8].T, LANES)
    dmod_pad = jnp.pad(dmod_all, ((0, LANES - N_DEV), (0, 0)))
    cols_ada = jnp.stack([lax.dynamic_slice_in_dim(dmod_pad, l * 6 * dm + chip * 1536, 1536, axis=1) for l in range(2)])
    cols_kv = lax.dynamic_slice_in_dim(dmod_pad, 12 * dm + chip * 512, 512, axis=1)[None]
    g_ada_w = _outer_grad(ct, cols_ada, "grad_ada_w")
    g_kv_ada_w = _outer_grad(ct, cols_kv, "grad_kv_ada_w")[0]

    my_lb = lax.dynamic_slice_in_dim(lb[0], chip * 256, 256)
    l0 = lax.dynamic_slice_in_dim(dlb, chip * 256, 256) * my_lb * (1.0 - my_lb)
    grads = {
        "ada_w": g_ada_w, "ada_b": jnp.stack([tot[:6 * dm], tot[6 * dm:12 * dm]]),
        "a_lb_logits": jnp.stack([l0, -l0]), "a_norm_g": g_a_norm[None],
        "a_w_out": rs["a_w_out"][None], "kv_ada_w": g_kv_ada_w, "kv_ada_b": tot[12 * dm:14 * dm],
        "kv_w": rs["kv_w"][:, :KV_SHARD], "kv_b_f": g_kv_b_f, "k_norm_g": g_k_norm,
        "b_w_q": rs["b_w_q"][None], "q_norm_g": g_q_norm[None], "b_w_out": rs["b_w_out"][None],
        "ffn_w_up": jnp.stack([rs["up0"], rs["up1"]]),
        "ffn_conv_w": jnp.stack([lax.dynamic_slice_in_dim(ct_l[:CONV_W], chip * FFN_COLS, FFN_COLS, axis=1) for ct_l in conv_tot]),
        "ffn_conv_b": jnp.stack([ct_l[CONV_W] for ct_l in conv_tot]),
        "ffn_w_down": jnp.stack([rs["down0"], rs["down1"]]),
    }
    weights = dict(ada_w=ada_w, ada_b=ada_b, a_w_in=a_w_in, a_lb_logits=a_lb_logits, a_norm_g=a_norm_g, a_w_out=a_w_out,
                   kv_ada_w=kv_ada_w, kv_ada_b=kv_ada_b, kv_w=kv_w, kv_b_f=kv_b_f, k_norm_g=k_norm_g, b_w_q=b_w_q,
                   q_norm_g=q_norm_g, b_w_out=b_w_out, ffn_w_up=ffn_w_up, ffn_conv_w=ffn_conv_w, ffn_conv_b=ffn_conv_b,
                   ffn_w_down=ffn_w_down)
    m_in = dict(ada_w=m_ada_w, ada_b=m_ada_b, a_w_in=m_a_w_in, a_lb_logits=m_a_lb_logits, a_norm_g=m_a_norm_g,
                a_w_out=m_a_w_out, kv_ada_w=m_kv_ada_w, kv_ada_b=m_kv_ada_b, kv_w=m_kv_w, kv_b_f=m_kv_b_f,
                k_norm_g=m_k_norm_g, b_w_q=m_b_w_q, q_norm_g=m_q_norm_g, b_w_out=m_b_w_out, ffn_w_up=m_ffn_w_up,
                ffn_conv_w=m_ffn_conv_w, ffn_conv_b=m_ffn_conv_b, ffn_w_down=m_ffn_w_down)
    v_in = dict(ada_w=v_ada_w, ada_b=v_ada_b, a_w_in=v_a_w_in, a_lb_logits=v_a_lb_logits, a_norm_g=v_a_norm_g,
                a_w_out=v_a_w_out, kv_ada_w=v_kv_ada_w, kv_ada_b=v_kv_ada_b, kv_w=v_kv_w, kv_b_f=v_kv_b_f,
                k_norm_g=v_k_norm_g, b_w_q=v_b_w_q, q_norm_g=v_q_norm_g, b_w_out=v_b_w_out, ffn_w_up=v_ffn_w_up,
                ffn_conv_w=v_ffn_conv_w, ffn_conv_b=v_ffn_conv_b, ffn_w_down=v_ffn_w_down)

    names = list(weights)
    step = lambda n: _adamw(weights[n], grads[n], m_in[n], v_in[n], "adamw_" + n)
    grads = {n: g.reshape(weights[n].shape) for n, g in grads.items()}
    upd = {n: step(n) for n in names if n != "a_w_in"}
    last = sum_group(3, [u[0] for u in upd.values()])
    grads["a_w_in"] = swap_group(3, last, last)["a_w_in"][None]
    upd["a_w_in"] = step("a_w_in")
    return (loss, grad_x[None], *[grads[n] for n in names], *[upd[n][0] for n in names],
            *[upd[n][1] for n in names], *[upd[n][2] for n in names])
```

```python
import jax
import jax.numpy as jnp
from jax import lax
from jax.experimental import pallas as pl
from jax.experimental.pallas import tpu as pltpu
from jax.experimental.pallas import tpu_sc as plsc

F32 = jnp.float32
BF16 = jnp.bfloat16

D_MODEL = 1024
HEADS = 8
HEAD_DIM = 128
A_CHUNK = 64
D_FF = 2816
CONV_W = 3
EPS = 1e-6
NEG_INF = -1e30
N_CHIPS = 4
N_DEV = 8

ADAM_LR = 0.001
ADAM_B1 = 0.9
ADAM_B2 = 0.999
ADAM_EPS = 1e-08
ADAM_WD = 0.01
ADAM_STEP = 10

SUBLANES = 8
BF16_ROWS = 16
LANES = 128
HALO = BF16_ROWS
ROW_TILE = 512
TOKEN_TILE_TN = 2048
FFN_COLS = 1408
FFN_ROWS = 256
HGRN_ROWS = 256
ATT_TILE = 512
ATT_SPLIT = 2
MESH = pl.DeviceIdType.MESH


def _sig(x):
    return jax.nn.sigmoid(x)


def _dot(a, b):
    return jnp.dot(a, b, preferred_element_type=F32)


def _dot_nt(a, b):
    return lax.dot_general(a, b, (((1,), (1,)), ((), ())), preferred_element_type=F32)


def _dot_tn(a, b):
    return lax.dot_general(a, b, (((0,), (0,)), ((), ())), preferred_element_type=F32)


def _split2(x):
    hi = x.astype(BF16)
    lo = (x - hi.astype(F32)).astype(BF16)
    return hi, lo


def _dot_f32(a, b):
    ah, al = _split2(a)
    bh, bl = _split2(b)
    return _dot(ah, bh) + _dot(ah, bl) + _dot(al, bh)


def _tri_dot(tri, x):
    hi = x.astype(BF16)
    r = x - hi.astype(F32)
    mid = r.astype(BF16)
    lo = (r - mid.astype(F32)).astype(BF16)
    return _dot(tri, hi) + _dot(tri, mid) + _dot(tri, lo)


def _tri(n, upper=False):
    r = lax.broadcasted_iota(jnp.int32, (n, n), 0)
    c = lax.broadcasted_iota(jnp.int32, (n, n), 1)
    keep = (c >= r) if upper else (c <= r)
    return jnp.where(keep, 1.0, 0.0).astype(BF16)


def _colsum8(v):
    rows, n = v.shape
    return v.reshape(rows // SUBLANES, SUBLANES, n).sum(axis=0)


def _full(shape):
    nd = len(shape)
    return pl.BlockSpec(shape, lambda *_: (0,) * nd)


def _tile(n, want):
    t = min(n, want)
    assert n % t == 0, (n, t)
    return t


def _mm_nn(a, w, groups, out_dtype, name):
    m_rows, k = a.shape
    p_n, _, n = w.shape
    per = p_n // groups
    tm = _tile(m_rows, ROW_TILE)

    def body(a_ref, w_ref, o_ref):
        av = a_ref[...]
        for p in range(p_n):
            o_ref[p // per, :, (p % per) * n:(p % per + 1) * n] = _dot(av, w_ref[p]).astype(out_dtype)

    return pl.pallas_call(
        body, name=name, grid=(m_rows // tm,),
        in_specs=[pl.BlockSpec((tm, k), lambda i: (i, 0)), _full((p_n, k, n))],
        out_specs=pl.BlockSpec((groups, tm, per * n), lambda i: (0, i, 0)),
        out_shape=jax.ShapeDtypeStruct((groups, m_rows, per * n), out_dtype),
    )(a, w)


def _mm_nt(d, w, out_dtype, name):
    g_n, m_rows, _ = d.shape
    p_n, k, n = w.shape
    per = p_n // g_n
    tm = _tile(m_rows, ROW_TILE)

    def body(d_ref, w_ref, o_ref):
        acc = None
        for p in range(p_n):
            t = _dot_nt(d_ref[p // per, :, (p % per) * n:(p % per + 1) * n], w_ref[p])
            acc = t if acc is None else acc + t
        o_ref[...] = acc.astype(out_dtype)

    return pl.pallas_call(
        body, name=name, grid=(m_rows // tm,),
        in_specs=[pl.BlockSpec((g_n, tm, per * n), lambda i: (0, i, 0)), _full((p_n, k, n))],
        out_specs=pl.BlockSpec((tm, k), lambda i: (i, 0)),
        out_shape=jax.ShapeDtypeStruct((m_rows, k), out_dtype),
    )(d, w)


def _mm_tn(a, d, p_n, name):
    m_rows, k = a.shape
    g_n, _, w_cols = d.shape
    per = p_n // g_n
    n = w_cols // per
    tm = _tile(m_rows, TOKEN_TILE_TN if k <= D_MODEL else ROW_TILE)
    steps = m_rows // tm

    def body(a_ref, d_ref, o_ref, acc):
        m = pl.program_id(1)

        @pl.when(m == 0)
        def _():
            acc[...] = jnp.zeros_like(acc)

        acc[...] += _dot_tn(a_ref[...], d_ref[...])

        @pl.when(m == steps - 1)
        def _():
            o_ref[...] = acc[...].astype(BF16)

    return pl.pallas_call(
        body, name=name, grid=(p_n, steps),
        in_specs=[pl.BlockSpec((tm, k), lambda p, m: (m, 0)),
                  pl.BlockSpec((None, tm, n), lambda p, m: (p // per, m, p % per))],
        out_specs=pl.BlockSpec((None, k, n), lambda p, m: (p, 0, 0)),
        out_shape=jax.ShapeDtypeStruct((p_n, k, n), BF16),
        scratch_shapes=[pltpu.VMEM((k, n), F32)],
    )(a, d)


def _premix(x, shift, scale, name):
    s, dm = x.shape
    tm = _tile(s, ROW_TILE)

    def body(x_ref, sh_ref, sc_ref, h_ref):
        xv = x_ref[...]
        inv = lax.rsqrt(jnp.mean(xv * xv, axis=-1, keepdims=True) + EPS)
        h_ref[...] = (xv * inv * (1.0 + sc_ref[...]) + sh_ref[...]).astype(BF16)

    row = pl.BlockSpec((tm, dm), lambda i: (i, 0))
    vec = _full((1, dm))
    return pl.pallas_call(body, name=name, grid=(s // tm,), in_specs=[row, vec, vec], out_specs=row,
                          out_shape=jax.ShapeDtypeStruct((s, dm), BF16))(x, shift, scale)


def _premix_bwd(x, terms, dres, name, branch=None):
    s, dm = x.shape
    tm = _tile(s, ROW_TILE)
    pairs = [pr for _, prs in terms for pr in prs]
    n_in = 2 + len(terms) + 2 * len(pairs) + (2 if branch else 0)

    def body(*refs):
        x_ref, dres_ref = refs[:2]
        sc_refs = refs[2:2 + len(terms)]
        mm_refs = refs[2 + len(terms):2 + len(terms) + 2 * len(pairs)]
        outs = refs[n_in:]

        @pl.when(pl.program_id(0) == 0)
        def _():
            for o in outs[1:1 + 2 * len(terms)]:
                o[...] = jnp.zeros_like(o)
            if branch:
                outs[-1][...] = jnp.zeros_like(outs[-1])

        xv = x_ref[...]
        inv = lax.rsqrt(jnp.mean(xv * xv, axis=-1, keepdims=True) + EPS)
        r = xv * inv
        dx = dres_ref[...]
        k = 0
        for t, (_, prs) in enumerate(terms):
            dh = None
            for d, w in prs:
                d_ref, w_ref = mm_refs[2 * k], mm_refs[2 * k + 1]
                k += 1
                p_n, _, n = w.shape
                per = p_n // d.shape[0]
                for p in range(p_n):
                    part = _dot_nt(d_ref[p // per, :, (p % per) * n:(p % per + 1) * n], w_ref[p])
                    dh = part if dh is None else dh + part
            dr = dh * (1.0 + sc_refs[t][...])
            dx = dx + inv * (dr - r * jnp.mean(dr * r, axis=-1, keepdims=True))
            outs[1 + 2 * t][...] += _colsum8(dh)
            outs[2 + 2 * t][...] += _colsum8(dh * r)
        outs[0][...] = dx
        if branch:
            y_ref, g_ref = refs[n_in - 2:n_in]
            outs[-2][0] = (dx * g_ref[...]).astype(BF16)
            outs[-1][...] += _colsum8(dx * y_ref[...])

    row = pl.BlockSpec((tm, dm), lambda i: (i, 0))
    vec, acc = _full((1, dm)), _full((SUBLANES, dm))
    ins, specs = [x, dres] + [sc for sc, _ in terms], [row, row] + [vec] * len(terms)
    for d, w in pairs:
        ins += [d, w]
        specs += [pl.BlockSpec((d.shape[0], tm, d.shape[2]), lambda i: (0, i, 0)), _full(w.shape)]
    out_shape = [jax.ShapeDtypeStruct((s, dm), F32)] + [jax.ShapeDtypeStruct((SUBLANES, dm), F32)] * (2 * len(terms))
    out_specs = [row] + [acc] * (2 * len(terms))
    if branch:
        ins += list(branch)
        specs += [row, vec]
        out_shape += [jax.ShapeDtypeStruct((1, s, dm), BF16), jax.ShapeDtypeStruct((SUBLANES, dm), F32)]
        out_specs += [pl.BlockSpec((1, tm, dm), lambda i: (0, i, 0)), acc]
    outs = pl.pallas_call(body, name=name, grid=(s // tm,), in_specs=specs, out_specs=out_specs,
                          out_shape=out_shape)(*ins)
    partials = [(outs[1 + 2 * t], outs[2 + 2 * t]) for t in range(len(terms))]
    return (outs[0], partials) + ((outs[-2], outs[-1]) if branch else ())


def _conv_taps(e, w, b):
    return w[2:3] * e + w[1:2] * pltpu.roll(e, 1, 0) + w[0:1] * pltpu.roll(e, 2, 0) + b


def _ffn_specs(s, tm, cb):
    hb = tm // HALO
    last = s // HALO - 1
    main = pl.BlockSpec((2, tm, cb), lambda j, i: (0, i, j))
    prev = pl.BlockSpec((2, HALO, cb), lambda j, i: (0, jnp.maximum(i * hb - 1, 0), j))
    nxt = pl.BlockSpec((2, HALO, cb), lambda j, i: (0, jnp.minimum((i + 1) * hb, last), j))
    wspec = pl.BlockSpec((2, CONV_W, cb), lambda j, i: (0, 0, j))
    bspec = pl.BlockSpec((2, 1, cb), lambda j, i: (0, 0, j))
    return main, prev, nxt, wspec, bspec


def _convglu_bwd(u, dffn, w_down, w, b, name):
    _, s, f = u.shape
    dm = dffn.shape[2]
    tm = _tile(s, 256)
    cb = _tile(f, FFN_COLS)
    steps = s // tm
    n_ext = tm + 2 * HALO
    main, prev, nxt, wspec, bspec = _ffn_specs(s, tm, cb)
    hb = tm // HALO
    last = s // HALO - 1
    d_main = pl.BlockSpec((None, tm, dm), lambda j, i: (0, i, 0))
    d_next = pl.BlockSpec((None, HALO, dm), lambda j, i: (0, jnp.minimum((i + 1) * hb, last), 0))
    wd_spec = pl.BlockSpec((None, cb, dm), lambda j, i: (0, j, 0))

    def body(u_ref, up_ref, un_ref, d_ref, dn_ref, wd_ref, w_ref, b_ref, du_ref, acc_ref):
        i = pl.program_id(1)
        first = jnp.where(i > 0, 1.0, 0.0)
        notlast = jnp.where(i < steps - 1, 1.0, 0.0)

        @pl.when(i == 0)
        def _():
            acc_ref[...] = jnp.zeros_like(acc_ref)

        def ext(g):
            return jnp.concatenate([up_ref[g].astype(F32) * first, u_ref[g].astype(F32), un_ref[g].astype(F32)], axis=0)

        ug, uv = ext(0), ext(1)
        gate = _conv_taps(ug, w_ref[0], b_ref[0])
        val = _conv_taps(uv, w_ref[1], b_ref[1])
        wd = wd_ref[...]
        da = _dot_nt(d_ref[...], wd).astype(BF16).astype(F32)
        da_next = _dot_nt(dn_ref[...], wd).astype(BF16).astype(F32) * notlast
        da_e = jnp.concatenate([jnp.zeros((HALO, cb), F32), da, da_next], axis=0)
        sg = _sig(gate)
        d_val = da_e * gate * sg
        d_gate = da_e * val * (sg * (1.0 + gate * (1.0 - sg)))

        def finish(g, d, e):
            wv = w_ref[g]
            rows = slice(HALO, HALO + tm)
            d1, d2 = pltpu.roll(d, n_ext - 1, 0), pltpu.roll(d, n_ext - 2, 0)
            du_ref[g] = (wv[2:3] * d + wv[1:2] * d1 + wv[0:1] * d2)[rows].astype(BF16)
            em = e[rows]
            acc_ref[g, 2] += _colsum8(d[rows] * em)
            acc_ref[g, 1] += _colsum8(d1[rows] * em)
            acc_ref[g, 0] += _colsum8(d2[rows] * em)
            acc_ref[g, 3] += _colsum8(d[rows])

        finish(0, d_gate, ug)
        finish(1, d_val, uv)

    return pl.pallas_call(
        body, name=name, grid=(f // cb, steps),
        in_specs=[main, prev, nxt, d_main, d_next, wd_spec, wspec, bspec],
        out_specs=[main, pl.BlockSpec((2, 4, SUBLANES, cb), lambda j, i: (0, 0, 0, j))],
        out_shape=[jax.ShapeDtypeStruct((2, s, f), BF16), jax.ShapeDtypeStruct((2, 4, SUBLANES, f), F32)],
    )(u, u, u, dffn, dffn, w_down, w, b)


def _hgrn_gates(q_raw, f_raw, lb, tri):
    sf = _sig(f_raw)
    fg = lb + (1.0 - lb) * sf
    b = _tri_dot(tri, jnp.log(fg))
    return q_raw * _sig(q_raw), 1.0 - fg, b, fg, sf


def _hgrn_fwd(proj, lb, norm_g, name):
    s = proj.shape[0]
    tb = _tile(s, HGRN_ROWS)
    n_c = tb // A_CHUNK
    half = A_CHUNK // 2

    def body(q_ref, f_ref, v_ref, g_ref, lb_ref, ng_ref, o_ref, yp_ref, st_ref, state):
        @pl.when(pl.program_id(0) == 0)
        def _():
            state[...] = jnp.zeros_like(state)

        tri = _tri(A_CHUNK)
        causal = lax.broadcasted_iota(jnp.int32, (A_CHUNK, A_CHUNK), 1) <= lax.broadcasted_iota(
            jnp.int32, (A_CHUNK, A_CHUNK), 0)

        def chunk(ci, carry):
            rows = pl.ds(pl.multiple_of(ci * A_CHUNK, A_CHUNK), A_CHUNK)
            for h in range(HEADS):
                cs = slice(h * HEAD_DIM, (h + 1) * HEAD_DIM)
                qs, k, b, _, _ = _hgrn_gates(q_ref[rows, cs], f_ref[rows, cs], lb_ref[:, cs], tri)
                b_mid, b_last = b[half:half + 1], b[A_CHUNK - 1:A_CHUNK]
                vb = v_ref[rows, cs].astype(BF16)
                scores = _dot_nt((qs * jnp.exp(b - b_mid)).astype(BF16), (k * jnp.exp(b_mid - b)).astype(BF16))
                scores = jnp.where(causal, scores, 0.0)
                st = state[h]
                st_ref[ci, h] = st
                o = _dot(scores.astype(BF16), vb) + _dot_nt((qs * jnp.exp(b)).astype(BF16), st.astype(BF16))
                state[h] = st * jnp.exp(b_last) + _dot_tn(vb, (k * jnp.exp(b_last - b)).astype(BF16))
                o_ref[rows, cs] = o
                inv = lax.rsqrt(jnp.mean(o * o, axis=-1, keepdims=True) + EPS)
                g_raw = g_ref[rows, cs]
                yp_ref[rows, cs] = (o * inv * ng_ref[:, cs] * (g_raw * _sig(g_raw))).astype(BF16)
            return carry

        lax.fori_loop(0, n_c, chunk, 0)

    col = lambda j: pl.BlockSpec((tb, D_MODEL), lambda i: (i, j))
    vec = _full((1, D_MODEL))
    return pl.pallas_call(
        body, name=name, grid=(s // tb,), in_specs=[col(0), col(1), col(2), col(3), vec, vec],
        out_specs=[col(0), col(0), pl.BlockSpec((n_c, HEADS, HEAD_DIM, HEAD_DIM), lambda i: (i, 0, 0, 0))],
        out_shape=[jax.ShapeDtypeStruct((s, D_MODEL), F32), jax.ShapeDtypeStruct((s, D_MODEL), BF16),
                   jax.ShapeDtypeStruct((s // A_CHUNK, HEADS, HEAD_DIM, HEAD_DIM), F32)],
        scratch_shapes=[pltpu.VMEM((HEADS, HEAD_DIM, HEAD_DIM), F32)],
    )(proj, proj, proj, proj, lb, norm_g)


def _hgrn_bwd(proj, lb, norm_g, o, states, dyp, name):
    s = proj.shape[0]
    tb = _tile(s, HGRN_ROWS)
    n_c = tb // A_CHUNK
    n_b = s // tb
    half = A_CHUNK // 2

    def body(q_ref, f_ref, v_ref, g_ref, lb_ref, ng_ref, o_ref, st_ref, dyp_ref, dp_ref, dlb_ref, dng_ref, dstate):
        @pl.when(pl.program_id(0) == 0)
        def _():
            dstate[...] = jnp.zeros_like(dstate)
            dlb_ref[...] = jnp.zeros_like(dlb_ref)
            dng_ref[...] = jnp.zeros_like(dng_ref)

        tri = _tri(A_CHUNK)
        tri_up = _tri(A_CHUNK, upper=True)
        row_id = lax.broadcasted_iota(jnp.int32, (A_CHUNK, HEAD_DIM), 0)
        causal = lax.broadcasted_iota(jnp.int32, (A_CHUNK, A_CHUNK), 1) <= lax.broadcasted_iota(
            jnp.int32, (A_CHUNK, A_CHUNK), 0)

        def chunk(cj, carry):
            ci = n_c - 1 - cj
            rows = pl.ds(pl.multiple_of(ci * A_CHUNK, A_CHUNK), A_CHUNK)
            for h in range(HEADS):
                cs = slice(h * HEAD_DIM, (h + 1) * HEAD_DIM)
                q_raw, lbh = q_ref[rows, cs], lb_ref[:, cs]
                qs, k, b, fg, sf = _hgrn_gates(q_raw, f_ref[rows, cs], lbh, tri)
                b_mid, b_last = b[half:half + 1], b[A_CHUNK - 1:A_CHUNK]
                e_qi, e_ki, e_q, e_ks = jnp.exp(b - b_mid), jnp.exp(b_mid - b), jnp.exp(b), jnp.exp(b_last - b)
                q_i, k_i, q_e, k_s = qs * e_qi, k * e_ki, qs * e_q, k * e_ks
                vb = v_ref[rows, cs].astype(BF16)
                scores = jnp.where(causal, _dot_nt(q_i.astype(BF16), k_i.astype(BF16)), 0.0)
                ov, g_raw, dy, ng = o_ref[rows, cs], g_ref[rows, cs], dyp_ref[rows, cs], ng_ref[:, cs]
                inv = lax.rsqrt(jnp.mean(ov * ov, axis=-1, keepdims=True) + EPS)
                nrm = ov * inv
                sg = _sig(g_raw)
                gs = g_raw * sg
                dn = dy * ng * gs
                dng_ref[0:1, cs] += jnp.sum(dy * nrm * gs, axis=0, keepdims=True)
                dg_raw = dy * nrm * ng * (sg * (1.0 + g_raw * (1.0 - sg)))
                do = (inv * (dn - nrm * jnp.mean(dn * nrm, axis=-1, keepdims=True))).astype(BF16)
                st_prev = st_ref[ci, h]
                dst = dstate[h]
                dstb = dst.astype(BF16)
                d_scores = jnp.where(causal, _dot_nt(do, vb), 0.0).astype(BF16)
                dv = _dot_tn(scores.astype(BF16), do) + _dot_nt(k_s.astype(BF16), dstb)
                dq_i = _dot(d_scores, k_i.astype(BF16))
                dk_i = _dot_tn(d_scores, q_i.astype(BF16))
                dq_e = _dot(do, st_prev.astype(BF16))
                dk_s = _dot(vb, dstb)
                d_decay = jnp.sum(st_prev * dst, axis=0, keepdims=True)
                dstate[h] = dst * jnp.exp(b_last) + _dot_tn(do, q_e.astype(BF16))
                dq = dq_i * e_qi + dq_e * e_q
                dk = dk_i * e_ki + dk_s * e_ks
                t_qi, t_ki, t_ks = dq_i * q_i, dk_i * k_i, dk_s * k_s
                db = t_qi - t_ki + dq_e * q_e - t_ks
                db_mid = jnp.sum(t_ki - t_qi, axis=0, keepdims=True)
                db_last = jnp.sum(t_ks, axis=0, keepdims=True) + d_decay * jnp.exp(b_last)
                db = db + jnp.where(row_id == half, db_mid, 0.0) + jnp.where(row_id == A_CHUNK - 1, db_last, 0.0)
                dfg = _tri_dot(tri_up, db) / fg - dk
                dlb_ref[0:1, cs] += jnp.sum(dfg * (1.0 - sf), axis=0, keepdims=True)
                sq = _sig(q_raw)
                dp_ref[0, rows, cs] = (dq * (sq * (1.0 + q_raw * (1.0 - sq)))).astype(BF16)
                dp_ref[1, rows, cs] = (dfg * (1.0 - lbh) * sf * (1.0 - sf)).astype(BF16)
                dp_ref[2, rows, cs] = dv.astype(BF16)
                dp_ref[3, rows, cs] = dg_raw.astype(BF16)
            return carry

        lax.fori_loop(0, n_c, chunk, 0)

    col = lambda j: pl.BlockSpec((tb, D_MODEL), lambda i: (n_b - 1 - i, j))
    vec = _full((1, D_MODEL))
    acc = _full((SUBLANES, D_MODEL))
    return pl.pallas_call(
        body, name=name, grid=(n_b,),
        in_specs=[col(0), col(1), col(2), col(3), vec, vec, col(0),
                  pl.BlockSpec((n_c, HEADS, HEAD_DIM, HEAD_DIM), lambda i: (n_b - 1 - i, 0, 0, 0)), col(0)],
        out_specs=[pl.BlockSpec((4, tb, D_MODEL), lambda i: (0, n_b - 1 - i, 0)), acc, acc],
        out_shape=[jax.ShapeDtypeStruct((4, s, D_MODEL), BF16), jax.ShapeDtypeStruct((SUBLANES, D_MODEL), F32),
                   jax.ShapeDtypeStruct((SUBLANES, D_MODEL), F32)],
        scratch_shapes=[pltpu.VMEM((HEADS, HEAD_DIM, HEAD_DIM), F32)],
    )(proj, proj, proj, proj, lb, norm_g, o, states, dyp)


def _headnorm(x, g, mult, name, col0=0):
    s = x.shape[0]
    tm = _tile(s, ROW_TILE)

    def body(x_ref, g_ref, y_ref):
        for h in range(HEADS):
            cs = slice(h * HEAD_DIM, (h + 1) * HEAD_DIM)
            xv = x_ref[:, cs]
            inv = lax.rsqrt(jnp.mean(xv * xv, axis=-1, keepdims=True) + EPS)
            y_ref[:, cs] = (xv * inv * g_ref[:, cs] * mult).astype(BF16)

    return pl.pallas_call(
        body, name=name, grid=(s // tm,),
        in_specs=[pl.BlockSpec((tm, D_MODEL), lambda i: (i, col0)), _full((1, D_MODEL))],
        out_specs=pl.BlockSpec((tm, D_MODEL), lambda i: (i, 0)),
        out_shape=jax.ShapeDtypeStruct((s, D_MODEL), BF16),
    )(x, g)


def _headnorm_bwd(x, g, mult, dy, name, col0=0, extra=None):
    s = x.shape[0]
    tm = _tile(s, ROW_TILE)
    groups = 2 if extra is not None else 1
    head_major = dy.ndim == 3

    def body(*refs):
        x_ref, g_ref, dy_ref = refs[:3]
        dx_ref, dg_ref = refs[-2:]

        @pl.when(pl.program_id(0) == 0)
        def _():
            dg_ref[...] = jnp.zeros_like(dg_ref)

        for h in range(HEADS):
            cs = slice(h * HEAD_DIM, (h + 1) * HEAD_DIM)
            xv, gv = x_ref[:, cs], g_ref[:, cs]
            dyv = dy_ref[h, :, 0:HEAD_DIM] if head_major else dy_ref[:, cs]
            inv = lax.rsqrt(jnp.mean(xv * xv, axis=-1, keepdims=True) + EPS)
            nrm = xv * inv
            dn = dyv * gv * mult
            dg_ref[:, cs] += _colsum8(dyv * nrm * mult)
            dx_ref[0, :, cs] = (inv * (dn - nrm * jnp.mean(dn * nrm, axis=-1, keepdims=True))).astype(BF16)
        if extra is not None:
            dx_ref[1] = refs[3][...]

    row = pl.BlockSpec((tm, D_MODEL), lambda i: (i, 0))
    dy_spec = pl.BlockSpec((HEADS, tm, dy.shape[-1]), lambda i: (0, i, 0)) if head_major else row
    ins = [x, g, dy] + ([extra] if extra is not None else [])
    specs = ([pl.BlockSpec((tm, D_MODEL), lambda i: (i, col0)), _full((1, D_MODEL)), dy_spec]
             + ([row] if extra is not None else []))
    return pl.pallas_call(
        body, name=name, grid=(s // tm,), in_specs=specs,
        out_specs=[pl.BlockSpec((groups, tm, D_MODEL), lambda i: (0, i, 0)), _full((SUBLANES, D_MODEL))],
        out_shape=[jax.ShapeDtypeStruct((groups, s, D_MODEL), BF16), jax.ShapeDtypeStruct((SUBLANES, D_MODEL), F32)],
    )(*ins)


def _log_sigmoid(z):
    return jnp.minimum(z, 0.0) - jnp.log(1.0 + jnp.exp(-jnp.abs(z)))


Q_CUM, Q_ONE, Q_LSE = 0, 3, 6
LOG2E = 1.4426950408889634


def _pieces(v):
    hi = v.astype(BF16).astype(F32)
    mid = (v - hi).astype(BF16).astype(F32)
    lo = ((v - hi) - mid).astype(BF16).astype(F32)
    return hi, mid, lo


def _side(lane, at, v):
    hi, mid, lo = _pieces(v)
    return jnp.where(lane == at, hi, jnp.where(lane == at + 1, mid, jnp.where(lane == at + 2, lo, 0.0)))


def _fcum_fwd(f, bias, name):
    s = f.shape[0]
    tm = _tile(s, ROW_TILE)

    def body(f_ref, b_ref, qa_ref, ka_ref, carry):
        @pl.when(pl.program_id(0) == 0)
        def _():
            carry[...] = jnp.zeros_like(carry)

        cum = _tri_dot(_tri(tm), _log_sigmoid(f_ref[...] + b_ref[...])) + carry[...]
        carry[...] = cum[tm - 1:tm]
        lane = lax.broadcasted_iota(jnp.int32, (tm, LANES), 1)
        ones_q = jnp.where((lane >= Q_ONE) & (lane < Q_LSE), 1.0, 0.0)
        ones_k = jnp.where((lane < Q_ONE) | ((lane >= Q_LSE) & (lane < Q_LSE + 3)), 1.0, 0.0)
        for h in range(HEADS):
            c2 = cum[:, h:h + 1] * LOG2E
            qa_ref[h] = (_side(lane, Q_CUM, c2) + ones_q).astype(BF16)
            ka_ref[h] = (_side(lane, Q_ONE, -c2) + ones_k).astype(BF16)

    side = pl.BlockSpec((HEADS, tm, LANES), lambda i: (0, i, 0))
    return pl.pallas_call(
        body, name=name, grid=(s // tm,),
        in_specs=[pl.BlockSpec((tm, LANES), lambda i: (i, 0)), _full((1, LANES))],
        out_specs=[side, side],
        out_shape=[jax.ShapeDtypeStruct((HEADS, s, LANES), BF16)] * 2,
        scratch_shapes=[pltpu.VMEM((1, LANES), F32)],
    )(f, bias)


def _fcum_bwd(f, bias, dka, dq, name):
    s = f.shape[0]
    tm = _tile(s, ROW_TILE)
    n_b = s // tm
    q_lane = HEAD_DIM + Q_CUM

    def body(f_ref, b_ref, dka_ref, dqa_ref, dz_ref, db_ref, carry):
        @pl.when(pl.program_id(0) == 0)
        def _():
            carry[...] = jnp.zeros_like(carry)
            db_ref[...] = jnp.zeros_like(db_ref)

        lane = lax.broadcasted_iota(jnp.int32, (tm, LANES), 1)
        dcum = jnp.zeros((tm, LANES), F32)
        for h in range(HEADS):
            dcum = dcum + jnp.where(lane == h, dqa_ref[h, :, q_lane:q_lane + 1] - dka_ref[h, :, Q_ONE:Q_ONE + 1], 0.0)
        dlf = _tri_dot(_tri(tm, upper=True), dcum) + carry[...]
        carry[...] = dlf[0:1]
        dz = dlf * _sig(-(f_ref[...] + b_ref[...]))
        dz_ref[0] = dz.astype(BF16)
        db_ref[...] += _colsum8(dz)

    return pl.pallas_call(
        body, name=name, grid=(n_b,),
        in_specs=[pl.BlockSpec((tm, LANES), lambda i: (n_b - 1 - i, 0)), _full((1, LANES)),
                  pl.BlockSpec((HEADS, tm, LANES), lambda i: (0, n_b - 1 - i, 0)),
                  pl.BlockSpec((HEADS, tm, 2 * HEAD_DIM), lambda i: (0, n_b - 1 - i, 0))],
        out_specs=[pl.BlockSpec((1, tm, LANES), lambda i: (0, n_b - 1 - i, 0)), _full((SUBLANES, LANES))],
        out_shape=[jax.ShapeDtypeStruct((1, s, LANES), BF16), jax.ShapeDtypeStruct((SUBLANES, LANES), F32)],
        scratch_shapes=[pltpu.VMEM((1, LANES), F32)],
    )(f, bias, dka, dq)


def _causal_pairs(n_t, key_major):
    if key_major:
        pairs = [(qi, ki) for ki in range(n_t) for qi in range(ki, n_t)]
    else:
        pairs = [(qi, ki) for qi in range(n_t) for ki in range(qi + 1)]
    return (jnp.array([p[0] for p in pairs], jnp.int32), jnp.array([p[1] for p in pairs], jnp.int32))


def _with_side(main_ref, side_ref):
    return jnp.concatenate([main_ref[...], side_ref[...]], axis=1)


def _lane_const(t, lo, hi, value):
    lane = lax.broadcasted_iota(jnp.int32, (t, LANES), 1)
    return jnp.where((lane >= lo) & (lane < hi), value, 0.0).astype(BF16)


def _att_specs(t):
    qmain = pl.BlockSpec((t, HEAD_DIM), lambda h, p, qt, kt: (qt[p], h))
    kmain = pl.BlockSpec((t, HEAD_DIM), lambda h, p, qt, kt: (kt[p], h))
    qside = pl.BlockSpec((None, t, LANES), lambda h, p, qt, kt: (h, qt[p], 0))
    kside = pl.BlockSpec((None, t, LANES), lambda h, p, qt, kt: (h, kt[p], 0))
    return qmain, kmain, qside, kside


def _fox_fwd(q, qa, k, ka, v, qo, name):
    s = q.shape[0]
    t = _tile(s, ATT_TILE)
    sub = t // ATT_SPLIT
    qt, kt = _causal_pairs(s // t, key_major=False)

    def body(qt_ref, kt_ref, q_ref, qa_ref, k_ref, ka_ref, v_ref, og_ref, o_ref, y_ref, qab_ref, m_s, l_s, acc_s):
        pid = pl.program_id(1)
        qi, ki = qt_ref[pid], kt_ref[pid]

        @pl.when(ki == 0)
        def _():
            m_s[...] = jnp.full_like(m_s, NEG_INF)
            l_s[...] = jnp.zeros_like(l_s)
            acc_s[...] = jnp.zeros_like(acc_s)

        def step(diagonal):
            kc = _with_side(k_ref, ka_ref)
            vc = jnp.concatenate([v_ref[...], _lane_const(t, 0, 1, 1.0)], axis=1)
            for r in range(ATT_SPLIT):
                rows = slice(r * sub, (r + 1) * sub)
                n_k = (r + 1) * sub if diagonal else t
                sc = _dot_nt(jnp.concatenate([q_ref[rows], qa_ref[rows]], axis=1), kc[:n_k])
                if diagonal:
                    sc = jnp.where(lax.broadcasted_iota(jnp.int32, (sub, n_k), 1)
                                   <= lax.broadcasted_iota(jnp.int32, (sub, n_k), 0) + r * sub, sc, NEG_INF)
                m_old = m_s[rows]
                m_new = jnp.maximum(m_old, jnp.max(sc, axis=-1, keepdims=True))
                alpha = jnp.exp2(m_old - m_new)
                pv = _dot(jnp.exp2(sc - m_new[:, 0:1]).astype(BF16), vc[:n_k])
                acc_s[rows] = alpha * acc_s[rows] + pv[:, :HEAD_DIM]
                l_s[rows] = alpha * l_s[rows] + pv[:, HEAD_DIM:]
                m_s[rows] = m_new

        @pl.when(ki < qi)
        def _():
            step(False)

        @pl.when(ki == qi)
        def _():
            step(True)
            l = l_s[:, 0:1]
            o = acc_s[...] / l
            o_ref[...] = o
            y_ref[...] = (o * _sig(og_ref[...])).astype(BF16)
            lane = lax.broadcasted_iota(jnp.int32, (t, LANES), 1)
            qab_ref[...] = qa_ref[...] + _side(lane, Q_LSE, -(m_s[:, 0:1] + jnp.log2(l))).astype(BF16)

    qmain, kmain, qside, kside = _att_specs(t)
    return pl.pallas_call(
        body, name=name,
        grid_spec=pltpu.PrefetchScalarGridSpec(
            num_scalar_prefetch=2, grid=(HEADS, qt.shape[0]),
            in_specs=[qmain, qside, kmain, kside, kmain,
                      pl.BlockSpec((t, HEAD_DIM), lambda h, p, qt, kt: (qt[p], HEADS + h))],
            out_specs=[qmain, qmain, qside],
            scratch_shapes=[pltpu.VMEM((t, LANES), F32), pltpu.VMEM((t, LANES), F32), pltpu.VMEM((t, HEAD_DIM), F32)]),
        out_shape=[jax.ShapeDtypeStruct((s, D_MODEL), F32), jax.ShapeDtypeStruct((s, D_MODEL), BF16),
                   jax.ShapeDtypeStruct((HEADS, s, LANES), BF16)],
    )(qt, kt, q, qa, k, ka, v, qo)


def _fox_gate_bwd(o, qo, dy, name):
    s = o.shape[0]
    tm = _tile(s, ROW_TILE)

    def body(o_ref, og_ref, dy_ref, do_ref, dg_ref, dl_ref):
        ov, dyv = o_ref[...], dy_ref[...]
        sg = _sig(og_ref[...])
        do = (dyv * sg).astype(BF16)
        do_ref[...] = do
        dg_ref[...] = (dyv * ov * sg * (1.0 - sg)).astype(BF16)
        prod = do.astype(F32) * ov
        lane = lax.broadcasted_iota(jnp.int32, (tm, LANES), 1)
        for h in range(HEADS):
            delta = jnp.sum(prod[:, h * HEAD_DIM:(h + 1) * HEAD_DIM], axis=-1, keepdims=True)
            dl_ref[h] = _side(lane, 0, delta).astype(BF16)

    row = pl.BlockSpec((tm, D_MODEL), lambda i: (i, 0))
    return pl.pallas_call(
        body, name=name, grid=(s // tm,),
        in_specs=[row, pl.BlockSpec((tm, D_MODEL), lambda i: (i, 1)), row],
        out_specs=[row, row, pl.BlockSpec((HEADS, tm, LANES), lambda i: (0, i, 0))],
        out_shape=[jax.ShapeDtypeStruct((s, D_MODEL), BF16), jax.ShapeDtypeStruct((s, D_MODEL), BF16),
                   jax.ShapeDtypeStruct((HEADS, s, LANES), BF16)],
    )(o, qo, dy)


def _fox_bwd(q, qab, k, ka, v, do, doa, name):
    s = q.shape[0]
    t = _tile(s, ATT_TILE)
    n_t = s // t
    sub = t // ATT_SPLIT
    qt, kt = _causal_pairs(n_t, key_major=True)

    def body(qt_ref, kt_ref, q_ref, qab_ref, k_ref, ka_ref, v_ref, do_ref, doa_ref, dk_ref, dv_ref, dka_ref, dq_ref,
             dk_s, dv_s):
        pid = pl.program_id(1)
        qi, ki = qt_ref[pid], kt_ref[pid]

        @pl.when(pid == 0)
        def _():
            dq_ref[...] = jnp.zeros_like(dq_ref)

        @pl.when(qi == ki)
        def _():
            dk_s[...] = jnp.zeros_like(dk_s)
            dv_s[...] = jnp.zeros_like(dv_s)

        def step(diagonal):
            kc = _with_side(k_ref, ka_ref)
            vc = jnp.concatenate([v_ref[...], _lane_const(t, 0, 3, -1.0)], axis=1)
            for r in range(ATT_SPLIT):
                cols = slice(r * sub, (r + 1) * sub)
                n_k = (r + 1) * sub if diagonal else t
                qc = jnp.concatenate([q_ref[cols], qab_ref[cols]], axis=1)
                sc = _dot_nt(kc[:n_k], qc)
                if diagonal:
                    sc = jnp.where(lax.broadcasted_iota(jnp.int32, (n_k, sub), 0)
                                   <= lax.broadcasted_iota(jnp.int32, (n_k, sub), 1) + r * sub, sc, NEG_INF)
                p = jnp.exp2(sc)
                dp = _dot_nt(vc[:n_k], jnp.concatenate([do_ref[cols], doa_ref[cols]], axis=1))
                ds = (p * dp).astype(BF16)
                dv_s[0:n_k] += _dot(p.astype(BF16), do_ref[cols])
                dk_s[0:n_k] += _dot(ds, qc)
                q_rows = pl.ds(pl.multiple_of(qi * t + r * sub, sub), sub)
                dq_ref[q_rows, :] += _dot_tn(ds, kc[:n_k])

        @pl.when(qi > ki)
        def _():
            step(False)

        @pl.when(qi == ki)
        def _():
            step(True)

        @pl.when(qi == n_t - 1)
        def _():
            dk_ref[...] = dk_s[:, :HEAD_DIM] * (1.0 / LOG2E)
            dka_ref[...] = dk_s[:, HEAD_DIM:]
            dv_ref[...] = dv_s[...].astype(BF16)

    qmain, kmain, qside, kside = _att_specs(t)
    return pl.pallas_call(
        body, name=name,
        grid_spec=pltpu.PrefetchScalarGridSpec(
            num_scalar_prefetch=2, grid=(HEADS, qt.shape[0]),
            in_specs=[qmain, qside, kmain, kside, kmain, qmain, qside],
            out_specs=[kmain, pl.BlockSpec((None, t, HEAD_DIM), lambda h, p, qt, kt: (0, kt[p], h)), kside,
                       pl.BlockSpec((None, s, 2 * HEAD_DIM), lambda h, p, qt, kt: (h, 0, 0))],
            scratch_shapes=[pltpu.VMEM((t, 2 * HEAD_DIM), F32), pltpu.VMEM((t, HEAD_DIM), F32)]),
        out_shape=[jax.ShapeDtypeStruct((s, D_MODEL), F32), jax.ShapeDtypeStruct((1, s, D_MODEL), BF16),
                   jax.ShapeDtypeStruct((HEADS, s, LANES), F32), jax.ShapeDtypeStruct((HEADS, s, 2 * HEAD_DIM), F32)],
    )(qt, kt, q, qab, k, ka, v, do, doa)


def _mm_residual_premix(a, w, x, gate, mods, name):
    s, k = a.shape
    dm = x.shape[1]
    tm = _tile(s, ROW_TILE)

    def body(*refs):
        a_ref, w_ref, x_ref, g_ref = refs[:4]
        mod_refs = refs[4:4 + 2 * len(mods)]
        y_ref, xn_ref = refs[4 + 2 * len(mods):6 + 2 * len(mods)]
        h_refs = refs[6 + 2 * len(mods):]
        y = _dot(a_ref[...], w_ref[0])
        y_ref[...] = y
        xv = x_ref[...] + g_ref[...] * y
        xn_ref[...] = xv
        nrm = xv * lax.rsqrt(jnp.mean(xv * xv, axis=-1, keepdims=True) + EPS)
        for t, h_ref in enumerate(h_refs):
            h_ref[...] = (nrm * (1.0 + mod_refs[2 * t + 1][...]) + mod_refs[2 * t][...]).astype(BF16)

    row = pl.BlockSpec((tm, dm), lambda i: (i, 0))
    vec = _full((1, dm))
    outs = pl.pallas_call(
        body, name=name, grid=(s // tm,),
        in_specs=[pl.BlockSpec((tm, k), lambda i: (i, 0)), _full(w.shape), row, vec] + [vec] * (2 * len(mods)),
        out_specs=[row] * (2 + len(mods)),
        out_shape=[jax.ShapeDtypeStruct((s, dm), F32)] * 2 + [jax.ShapeDtypeStruct((s, dm), BF16)] * len(mods),
    )(a, w, x, gate, *[v for m in mods for v in m])
    return outs[0], outs[1], list(outs[2:])


def _mm_loss_head(a, w, x, gate, target, name):
    s, k = a.shape
    dm = x.shape[1]
    tm = _tile(s, ROW_TILE)

    def body(a_ref, w_ref, x_ref, g_ref, t_ref, sq_ref, do_ref, dy_ref, dg_ref):
        @pl.when(pl.program_id(0) == 0)
        def _():
            sq_ref[...] = jnp.zeros_like(sq_ref)
            dg_ref[...] = jnp.zeros_like(dg_ref)

        y, gv = _dot(a_ref[...], w_ref[0]), g_ref[...]
        err = x_ref[...] + gv * y - t_ref[...]
        sq_ref[...] += _colsum8(err * err)
        dout = err * (1.0 / dm)
        do_ref[...] = dout
        dy_ref[0] = (dout * gv).astype(BF16)
        dg_ref[...] += _colsum8(dout * y)

    row = pl.BlockSpec((tm, dm), lambda i: (i, 0))
    acc = _full((SUBLANES, dm))
    return pl.pallas_call(
        body, name=name, grid=(s // tm,),
        in_specs=[pl.BlockSpec((tm, k), lambda i: (i, 0)), _full(w.shape), row, _full((1, dm)), row],
        out_specs=[acc, row, pl.BlockSpec((1, tm, dm), lambda i: (0, i, 0)), acc],
        out_shape=[jax.ShapeDtypeStruct((SUBLANES, dm), F32), jax.ShapeDtypeStruct((s, dm), F32),
                   jax.ShapeDtypeStruct((1, s, dm), BF16), jax.ShapeDtypeStruct((SUBLANES, dm), F32)],
    )(a, w, x, gate, target)


def _ffn_inner(h, w_up, conv_w, conv_b, tag):
    s, dm = h.shape
    half = w_up.shape[2]
    f = 2 * half
    tm = _tile(s, FFN_ROWS)

    def body(h_ref, w_ref, cw_ref, cb_ref, u_ref, a_ref, carry):
        @pl.when(pl.program_id(0) == 0)
        def _():
            carry[...] = jnp.zeros_like(carry)

        hv = h_ref[...]
        for j in range(2):
            cols = slice(j * half, (j + 1) * half)
            conv = []
            for g in range(2):
                ub = _dot(hv, w_ref[2 * g + j]).astype(BF16)
                u_ref[g, :, cols] = ub
                uf = ub.astype(F32)
                e = jnp.concatenate([carry[g, j], uf], axis=0)
                carry[g, j] = uf[tm - SUBLANES:tm]
                conv.append(_conv_taps(e, cw_ref[g][:, cols], cb_ref[g][:, cols])[SUBLANES:])
            a_ref[:, cols] = (conv[0] * _sig(conv[0]) * conv[1]).astype(BF16)

    return pl.pallas_call(
        body, name=tag + "_up_convglu", grid=(s // tm,),
        in_specs=[pl.BlockSpec((tm, dm), lambda i: (i, 0)), _full(w_up.shape), _full(conv_w.shape), _full(conv_b.shape)],
        out_specs=[pl.BlockSpec((2, tm, f), lambda i: (0, i, 0)), pl.BlockSpec((tm, f), lambda i: (i, 0))],
        out_shape=[jax.ShapeDtypeStruct((2, s, f), BF16), jax.ShapeDtypeStruct((s, f), BF16)],
        scratch_shapes=[pltpu.VMEM((2, 2, SUBLANES, half), F32)],
    )(h, w_up, conv_w, conv_b)


def _weight_grad_first(a, d, p_n, name):
    return lax.optimization_barrier((_mm_tn(a, d, p_n, name), d))


def _ffn_backward(dx_out, dffn, x_mid, scale, saved, w_up, conv_w, conv_b, w_down, mixer, tag):
    h, u, a = saved
    dw_down, dffn = _weight_grad_first(a, dffn, 1, tag + "_down_dw")
    du, dconv = _convglu_bwd(u, dffn, w_down, conv_w, conv_b, tag + "_convglu_bwd")
    dw_up, du = _weight_grad_first(h, du, N_CHIPS, tag + "_up_dw")
    dx_mid, [(dshift, dscale)], dy, dgate_mixer = _premix_bwd(x_mid, [(scale, [(du, w_up)])], dx_out,
                                                              tag + "_premix_bwd", branch=mixer)
    return dx_mid, dy, dgate_mixer, dw_up, dw_down, dict(shift=dshift, scale=dscale, conv=dconv)


def _local_step(x, target, mods, lb, vecs, weights_at):
    m0, m1, mk = mods["l0"], mods["l1"], mods["kv"]
    h0 = _premix(x, m0[0], m0[1], "l0_premix")
    wts, h0 = weights_at("mixer0", h0)
    proj = _mm_nn(h0, wts["a_w_in"], 1, F32, "l0_in")[0]
    o_a, yp, states = _hgrn_fwd(proj, lb, vecs["a_norm_g"], "l0_hgrn")
    more, yp = weights_at("ffn0", yp)
    wts.update(more)
    y0, x1, [hf0] = _mm_residual_premix(yp, wts["a_w_out"], x, m0[2], [(m0[3], m0[4])], "l0_out")
    u0, a0 = _ffn_inner(hf0, wts["up0"], vecs["conv_w0"], vecs["conv_b0"], "l0_ffn")
    saved0 = (hf0, u0, a0)
    ffn0, x2, [hk, h1] = _mm_residual_premix(a0, wts["down0"], x1, m0[5], [(mk[0], mk[1]), (m1[0], m1[1])],
                                             "l0_ffn_down")
    more, hk = weights_at("layer1", hk)
    wts.update(more)
    k_raw = _mm_nn(hk, wts["kv_k"], 1, F32, "kv_k")[0]
    v_sh = _mm_nn(hk, wts["kv_v"], 1, BF16, "kv_v")[0]
    f_raw = _mm_nn(hk, wts["kv_f"], 1, F32, "kv_f")[0]
    k_sh = _headnorm(k_raw, vecs["k_norm_g"], 1.0, "kv_knorm")
    qa, ka = _fcum_fwd(f_raw, vecs["kv_b_f"], "kv_fcum")
    qo = _mm_nn(h1, wts["b_w_q"], 1, F32, "l1_q")[0]
    q_scale = HEAD_DIM ** -0.5
    q = _headnorm(qo, vecs["q_norm_g"], q_scale * LOG2E, "l1_qnorm")
    o_b, og, qab = _fox_fwd(q, qa, k_sh, ka, v_sh, qo, "l1_fox")
    y1, x3, [hf1] = _mm_residual_premix(og, wts["b_w_out"], x2, m1[2], [(m1[3], m1[4])], "l1_out")
    u1, a1 = _ffn_inner(hf1, wts["up1"], vecs["conv_w1"], vecs["conv_b1"], "l1_ffn")
    saved1 = (hf1, u1, a1)
    sq, dx4, dffn1, dg2_1 = _mm_loss_head(a1, wts["down1"], x3, m1[5], target, "l1_ffn_down")

    big, small = {}, {}
    dx3, dy1, dg1_1, big["up1"], big["down1"], s_ffn1 = _ffn_backward(
        dx4, dffn1, x3, m1[4], saved1, wts["up1"], vecs["conv_w1"], vecs["conv_b1"], wts["down1"], (y1, m1[2]), "l1_ffn")
    big["b_w_out"], dy1 = _weight_grad_first(og, dy1, 1, "l1_out_dw")
    d_og = _mm_nt(dy1, wts["b_w_out"], F32, "l1_out_dx")
    do_b, dgate_b, doa = _fox_gate_bwd(o_b, qo, d_og, "l1_fox_gate_bwd")
    dk, dv, dka, dq = _fox_bwd(q, qab, k_sh, ka, v_sh, do_b, doa, "l1_fox_bwd")
    dqo, dqg = _headnorm_bwd(qo, vecs["q_norm_g"], q_scale, dq, "l1_qnorm_bwd", extra=dgate_b)
    big["b_w_q"], dqo = _weight_grad_first(h1, dqo, N_CHIPS, "l1_q_dw")
    dk_raw, dkg = _headnorm_bwd(k_raw, vecs["k_norm_g"], 1.0, dk, "kv_knorm_bwd")
    dz, dbf = _fcum_bwd(f_raw, vecs["kv_b_f"], dka, dq, "kv_fcum_bwd")
    big["kv_k"], dk_raw = _weight_grad_first(hk, dk_raw, 1, "kv_k_dw")
    big["kv_v"], dv = _weight_grad_first(hk, dv, 1, "kv_v_dw")
    big["kv_f"], dz = _weight_grad_first(hk, dz, 1, "kv_f_dw")
    kv_pairs = [(dk_raw, wts["kv_k"]), (dv, wts["kv_v"]), (dz, wts["kv_f"])]
    dx2, [(dsh1_1, dsc1_1), (dshk, dsck)], dffn0, dg2_0 = _premix_bwd(
        x2, [(m1[1], [(dqo, wts["b_w_q"])]), (mk[1], kv_pairs)], dx3, "l1_kv_premix_bwd", branch=(ffn0, m0[5]))
    dx1, dy0, dg1_0, big["up0"], big["down0"], s_ffn0 = _ffn_backward(
        dx2, dffn0, x1, m0[4], saved0, wts["up0"], vecs["conv_w0"], vecs["conv_b0"], wts["down0"], (y0, m0[2]), "l0_ffn")
    big["a_w_out"], dy0 = _weight_grad_first(yp, dy0, 1, "l0_out_dw")
    dyp = _mm_nt(dy0, wts["a_w_out"], F32, "l0_out_dx")
    dproj, dlb, dng = _hgrn_bwd(proj, lb, vecs["a_norm_g"], o_a, states, dyp, "l0_hgrn_bwd")
    big["a_w_in"], dproj = _weight_grad_first(h0, dproj, N_CHIPS, "l0_in_dw")
    grad_x, [(dsh1_0, dsc1_0)] = _premix_bwd(x, [(m0[1], [(dproj, wts["a_w_in"])])], dx1, "l0_premix_bwd")

    small["mod_l0"] = [dsh1_0, dsc1_0, dg1_0, s_ffn0["shift"], s_ffn0["scale"], dg2_0]
    small["mod_l1"] = [dsh1_1, dsc1_1, dg1_1, s_ffn1["shift"], s_ffn1["scale"], dg2_1]
    small["mod_kv"] = [dshk, dsck]
    small["conv0"], small["conv1"] = s_ffn0["conv"], s_ffn1["conv"]
    small["a_norm_g"], small["k_norm_g"], small["q_norm_g"] = dng, dkg, dqg
    small["kv_b_f"], small["lb"] = dbf, dlb
    marks = {"attention_bwd": dk, "ffn0_bwd": dx1, "mixer0_bwd": grad_x}
    return sq, grad_x, big, small, marks


HBM = pl.BlockSpec(memory_space=pltpu.HBM)
COMM_CHUNK_ELEMS = 256 * 1024


def _place():
    x, y, c = lax.axis_index("x"), lax.axis_index("y"), lax.axis_index("c")
    chips = [(1 - x, y), (x, 1 - y), (1 - x, 1 - y)]
    return x, y, c, (x, y, 1 - c), chips


def _chunk_rows(rows, cols):
    best = BF16_ROWS
    for r in range(BF16_ROWS, rows + 1, BF16_ROWS):
        if rows % r == 0 and r * cols <= COMM_CHUNK_ELEMS:
            best = r
    assert rows % best == 0, (rows, cols)
    return best


def _allgather8(block, name):
    m_per, n = block.shape

    def body(x_ref, out_ref, send_sems, recv_sems, local_sem):
        x, y, c, sibling, chips = _place()
        me = (x, y, c)

        def rows(px, py, pc):
            return out_ref.at[pl.ds((4 * px + 2 * py + pc) * m_per, m_per), :]

        def copy(k, blk, to, src=None):
            return pltpu.make_async_remote_copy(
                src_ref=rows(*blk) if src is None else src, dst_ref=rows(*blk),
                send_sem=send_sems.at[k], recv_sem=recv_sems.at[k], device_id=to, device_id_type=MESH)

        mine = pltpu.make_async_copy(x_ref, rows(*me), local_sem)
        mine.start()
        first = [copy(0, me, sibling, src=x_ref)]
        first += [copy(1 + j, me, (*chip, c), src=x_ref) for j, chip in enumerate(chips)]
        for cp in first:
            cp.start()
        passed = [copy(4 + j, (*chip, c), sibling) for j, chip in enumerate(chips)]
        for j, chip in enumerate(chips):
            copy(1 + j, (*chip, c), me).wait_recv()
            passed[j].start()
        copy(0, sibling, me).wait_recv()
        for j, chip in enumerate(chips):
            copy(4 + j, (*chip, 1 - c), me).wait_recv()
        for cp in first + passed:
            cp.wait_send()
        mine.wait()

    return pl.pallas_call(
        body, name=name, out_shape=jax.ShapeDtypeStruct((N_DEV * m_per, n), block.dtype),
        in_specs=[pl.BlockSpec(memory_space=pltpu.VMEM)], out_specs=pl.BlockSpec(memory_space=pltpu.VMEM),
        scratch_shapes=[pltpu.SemaphoreType.DMA((7,)), pltpu.SemaphoreType.DMA((7,)), pltpu.SemaphoreType.DMA],
    )(block)


def _cast_own_block(shards, layer, chip, name):
    _, r, cols = shards.shape
    rows = _chunk_rows(r, cols)

    def body(chip_ref, w_ref, o_ref):
        o_ref[...] = w_ref[...].astype(BF16)

    return pl.pallas_call(
        body, name=name,
        grid_spec=pltpu.PrefetchScalarGridSpec(
            num_scalar_prefetch=1, grid=(r // rows,),
            in_specs=[pl.BlockSpec((None, rows, cols), lambda i, chip_ref: (layer, i, 0))],
            out_specs=pl.BlockSpec((None, rows, cols), lambda i, chip_ref: (chip_ref[0], i, 0))),
        out_shape=jax.ShapeDtypeStruct((N_CHIPS, r, cols), BF16),
    )(chip, shards)


def _sequencer_gather(bufs, name, collective_id):
    n_t = len(bufs)
    dims = [b.shape[1:] for b in bufs]
    refs = [jax.new_ref(b, memory_space=pltpu.MemorySpace.HBM) for b in bufs]

    @pl.kernel(mesh=plsc.ScalarSubcoreMesh(axis_name="sequencer", num_cores=1), name=name,
               scratch_types=[pltpu.SemaphoreType.DMA((n_t,))] * 4,
               compiler_params=pltpu.CompilerParams(collective_id=collective_id))
    def launch(send_ici, recv_ici, send_d2d, recv_d2d):
        x, y, c, sibling, chips = _place()
        p_me = 2 * x + y
        peers = [sibling] + [(cx, cy, c) for cx, cy in chips]
        barrier = pltpu.get_barrier_semaphore()
        for peer in peers:
            pl.semaphore_signal(barrier, inc=1, device_id=peer, device_id_type=MESH)
        pl.semaphore_wait(barrier, len(peers))

        def waiter(t, sem_s, sem_r):
            win = refs[t].at[pl.ds(0, 3), pl.ds(0, dims[t][0] // 2), :]
            return pltpu.make_async_remote_copy(src_ref=win, dst_ref=win, send_sem=sem_s.at[t], recv_sem=sem_r.at[t],
                                                device_id=sibling, device_id_type=MESH)

        def half_copy(t, chip_idx, to, sem_s, sem_r):
            r2 = dims[t][0] // 2
            win = refs[t].at[chip_idx, pl.ds(c * r2, r2), :]
            return pltpu.make_async_remote_copy(src_ref=win, dst_ref=win, send_sem=sem_s.at[t], recv_sem=sem_r.at[t],
                                                device_id=to, device_id_type=MESH)

        for t in range(n_t):
            for cx, cy in chips:
                half_copy(t, p_me, (cx, cy, c), send_ici, recv_ici).start()
        for t in range(n_t):
            waiter(t, send_ici, recv_ici).wait_recv()
            for cx, cy in chips:
                half_copy(t, 2 * cx + cy, sibling, send_d2d, recv_d2d).start()
        for t in range(n_t):
            waiter(t, send_d2d, recv_d2d).wait_recv()
            waiter(t, send_ici, recv_ici).wait_send()
            waiter(t, send_d2d, recv_d2d).wait_send()

    launch()
    return [r[...] for r in refs]


def _sequencer_allgather8(block, dev, name, collective_id):
    m_per, n = block.shape
    src = jax.new_ref(block, memory_space=pltpu.MemorySpace.HBM)
    out = jax.empty_ref(jax.ShapeDtypeStruct((N_DEV * m_per, n), block.dtype), memory_space=pltpu.MemorySpace.HBM)

    @pl.kernel(mesh=plsc.ScalarSubcoreMesh(axis_name="sequencer", num_cores=1), name=name,
               scratch_types=[pltpu.SemaphoreType.DMA((7,))] * 2,
               compiler_params=pltpu.CompilerParams(collective_id=collective_id))
    def launch(send_sems, recv_sems):
        x, y, c, sibling, chips = _place()
        me = (x, y, c)
        _handshake([sibling] + [(cx, cy, c) for cx, cy in chips])

        def rows(px, py, pc):
            return out.at[pl.ds((4 * px + 2 * py + pc) * m_per, m_per), :]

        def copy(k, blk, to, from_src=False):
            return pltpu.make_async_remote_copy(
                src_ref=src if from_src else rows(*blk), dst_ref=rows(*blk),
                send_sem=send_sems.at[k], recv_sem=recv_sems.at[k], device_id=to, device_id_type=MESH)

        first = [copy(0, me, sibling, True)] + [copy(1 + j, me, (*chip, c), True) for j, chip in enumerate(chips)]
        for cp in first:
            cp.start()
        passed = [copy(4 + j, (*chip, c), sibling) for j, chip in enumerate(chips)]
        for j, chip in enumerate(chips):
            copy(1 + j, (*chip, c), me).wait_recv()
            passed[j].start()
        copy(0, sibling, me).wait_recv()
        for j, chip in enumerate(chips):
            copy(4 + j, (*chip, 1 - c), me).wait_recv()
        for cp in first + passed:
            cp.wait_send()

    launch()
    return lax.dynamic_update_slice(out[...], block, (dev * m_per, 0))


def _others():
    x, y, c = lax.axis_index("x"), lax.axis_index("y"), lax.axis_index("c")
    flip = lambda v, f: 1 - v if f else v
    return [(flip(x, fx), flip(y, fy), flip(c, fc))
            for fx in (0, 1) for fy in (0, 1) for fc in (0, 1) if (fx, fy, fc) != (0, 0, 0)]


def _handshake(peers):
    barrier = pltpu.get_barrier_semaphore()
    for peer in peers:
        pl.semaphore_signal(barrier, inc=1, device_id=peer, device_id_type=MESH)
    pl.semaphore_wait(barrier, len(peers))


def _sequencer_scatter(parts, name, collective_id):
    n_t = len(parts)
    dims = [p.shape[1:] for p in parts]
    srcs = [jax.new_ref(p, memory_space=pltpu.MemorySpace.HBM) for p in parts]
    inboxes = [jax.empty_ref(jax.ShapeDtypeStruct((N_DEV, r // 2, cols), BF16), memory_space=pltpu.MemorySpace.HBM)
               for r, cols in dims]

    @pl.kernel(mesh=plsc.ScalarSubcoreMesh(axis_name="sequencer", num_cores=1), name=name,
               scratch_types=[pltpu.SemaphoreType.DMA((n_t,))] * 2,
               compiler_params=pltpu.CompilerParams(collective_id=collective_id))
    def launch(send_sem, recv_sem):
        x, y, c = lax.axis_index("x"), lax.axis_index("y"), lax.axis_index("c")
        me = 4 * x + 2 * y + c
        peers = _others()
        _handshake(peers)
        for t in range(n_t):
            h = dims[t][0] // 2
            for qx, qy, qc in peers:
                pltpu.make_async_remote_copy(
                    src_ref=srcs[t].at[2 * qx + qy, pl.ds(qc * h, h), :], dst_ref=inboxes[t].at[me],
                    send_sem=send_sem.at[t], recv_sem=recv_sem.at[t], device_id=(qx, qy, qc), device_id_type=MESH).start()
        for t in range(n_t):
            win = inboxes[t].at[pl.ds(0, N_DEV - 1)]
            both = pltpu.make_async_remote_copy(src_ref=win, dst_ref=win, send_sem=send_sem.at[t],
                                                recv_sem=recv_sem.at[t], device_id=peers[0], device_id_type=MESH)
            both.wait_recv()
            both.wait_send()

    launch()
    return [b[...] for b in inboxes]


def _sum_pieces(part, inbox, place, name):
    _, r, cols = part.shape
    h = r // 2
    rows = _chunk_rows(h, cols)
    steps = h // rows

    def body(place_ref, own_ref, in_ref, o_ref):
        dev = place_ref[2]
        own = own_ref[...].astype(F32)
        acc = jnp.zeros((rows, cols), F32)
        for d in range(N_DEV):
            acc = acc + jnp.where(dev == d, own, in_ref[d].astype(F32))
        o_ref[...] = acc

    return pl.pallas_call(
        body, name=name,
        grid_spec=pltpu.PrefetchScalarGridSpec(
            num_scalar_prefetch=1, grid=(steps,),
            in_specs=[pl.BlockSpec((None, rows, cols), lambda i, pr: (pr[0], pr[1] * steps + i, 0)),
                      pl.BlockSpec((N_DEV, rows, cols), lambda i, pr: (0, i, 0))],
            out_specs=pl.BlockSpec((rows, cols), lambda i, pr: (pr[1] * steps + i, 0))),
        out_shape=jax.ShapeDtypeStruct((r, cols), F32),
    )(place, part, inbox)


def _sequencer_swap_halves(halves, name, collective_id):
    n_t = len(halves)
    refs = [jax.new_ref(a, memory_space=pltpu.MemorySpace.HBM) for a in halves]

    @pl.kernel(mesh=plsc.ScalarSubcoreMesh(axis_name="sequencer", num_cores=1), name=name,
               scratch_types=[pltpu.SemaphoreType.DMA((n_t,))] * 2,
               compiler_params=pltpu.CompilerParams(collective_id=collective_id))
    def launch(send_sem, recv_sem):
        x, y, c = lax.axis_index("x"), lax.axis_index("y"), lax.axis_index("c")
        sibling = (x, y, 1 - c)
        _handshake([sibling])
        copies = []
        for t in range(n_t):
            h = halves[t].shape[0] // 2
            win = refs[t].at[pl.ds(c * h, h), :]
            copies.append(pltpu.make_async_remote_copy(src_ref=win, dst_ref=win, send_sem=send_sem.at[t],
                                                       recv_sem=recv_sem.at[t], device_id=sibling, device_id_type=MESH))
            copies[-1].start()
        for cp in copies:
            cp.wait()

    launch()
    return [r[...] for r in refs]


def _cond_rows(c16, w, act, name):
    n_l, dm, wid = w.shape

    def body(c_ref, w_ref, o_ref, a_ref):
        cv = c_ref[...]
        if act:
            cv = cv * _sig(cv)
        a_ref[...] = cv
        o_ref[...] = _dot_f32(cv, w_ref[...])

    return pl.pallas_call(
        body, name=name, grid=(n_l,),
        in_specs=[_full((16, dm)), pl.BlockSpec((None, dm, wid), lambda l: (l, 0, 0))],
        out_specs=[pl.BlockSpec((None, 16, wid), lambda l: (l, 0, 0)), _full((16, dm))],
        out_shape=[jax.ShapeDtypeStruct((n_l, 16, wid), F32), jax.ShapeDtypeStruct((16, dm), F32)],
    )(c16, w)


def _outer_grad(ct, dm, name):
    n_l, kk, wid = dm.shape
    d_rows = ct.shape[0]

    def body(c_ref, d_ref, o_ref):
        o_ref[...] = _dot_f32(c_ref[...], d_ref[...])

    return pl.pallas_call(
        body, name=name, grid=(n_l,),
        in_specs=[_full((d_rows, kk)), pl.BlockSpec((None, kk, wid), lambda l: (l, 0, 0))],
        out_specs=pl.BlockSpec((None, d_rows, wid), lambda l: (l, 0, 0)),
        out_shape=jax.ShapeDtypeStruct((n_l, d_rows, wid), F32),
    )(ct, dm)


def _sum_devices(g, name):
    rows, n = g.shape

    def body(g_ref, o_ref):
        acc = g_ref[0:SUBLANES, :]
        for dev in range(1, N_DEV):
            acc = acc + g_ref[dev * SUBLANES:(dev + 1) * SUBLANES, :]
        o_ref[...] = acc

    return pl.pallas_call(body, name=name, out_shape=jax.ShapeDtypeStruct((SUBLANES, n), F32))(g)


def _adamw(w, g, m, v, name):
    shape = w.shape
    cols = shape[-1]
    rows = w.size // cols
    tr = rows
    for cand in range(SUBLANES, min(rows, 256) + 1, SUBLANES):
        if rows % cand == 0:
            tr = cand
    if rows * cols <= COMM_CHUNK_ELEMS:
        tr = rows
    c1 = 1.0 / (1.0 - ADAM_B1 ** ADAM_STEP)
    c2 = 1.0 / (1.0 - ADAM_B2 ** ADAM_STEP)

    def body(w_ref, g_ref, m_ref, v_ref, d_ref, mo_ref, vo_ref):
        gv = g_ref[...]
        m_new = ADAM_B1 * m_ref[...] + (1.0 - ADAM_B1) * gv
        v_new = ADAM_B2 * v_ref[...] + (1.0 - ADAM_B2) * (gv * gv)
        mo_ref[...] = m_new
        vo_ref[...] = v_new
        d_ref[...] = -ADAM_LR * ((m_new * c1) / (jnp.sqrt(v_new * c2) + ADAM_EPS) + ADAM_WD * w_ref[...])

    spec = pl.BlockSpec((tr, cols), lambda i: (i, 0))
    outs = pl.pallas_call(
        body, name=name, grid=(rows // tr,), in_specs=[spec] * 4, out_specs=[spec] * 3,
        out_shape=[jax.ShapeDtypeStruct((rows, cols), F32)] * 3,
    )(*[a.reshape(rows, cols) for a in (w, g, m, v)])
    return tuple(o.reshape(shape) for o in outs)


def _pad_cols(a, cols):
    return jnp.pad(a, [(0, 0)] * (a.ndim - 1) + [(0, cols - a.shape[-1])])


def _flat8(parts, width):
    v = jnp.concatenate([p.reshape(-1) for p in parts])
    return jnp.pad(v, (0, width - v.shape[0])).reshape(SUBLANES, width // SUBLANES)


KV_SHARD = 514
KV_SHARD_PAD = 640
BIG = ("a_w_in", "a_w_out", "kv_w", "b_w_q", "b_w_out", "up0", "up1", "down0", "down1")


def kernel(x, c, ada_w, ada_b, a_w_in, a_lb_logits, a_norm_g, a_w_out, kv_ada_w, kv_ada_b, kv_w, kv_b_f, k_norm_g, b_w_q, q_norm_g, b_w_out, ffn_w_up, ffn_conv_w, ffn_conv_b, ffn_w_down, loss_target, m_ada_w, m_ada_b, m_a_w_in, m_a_lb_logits, m_a_norm_g, m_a_w_out, m_kv_ada_w, m_kv_ada_b, m_kv_w, m_kv_b_f, m_k_norm_g, m_b_w_q, m_q_norm_g, m_b_w_out, m_ffn_w_up, m_ffn_conv_w, m_ffn_conv_b, m_ffn_w_down, v_ada_w, v_ada_b, v_a_w_in, v_a_lb_logits, v_a_norm_g, v_a_w_out, v_kv_ada_w, v_kv_ada_b, v_kv_w, v_kv_b_f, v_k_norm_g, v_b_w_q, v_q_norm_g, v_b_w_out, v_ffn_w_up, v_ffn_conv_w, v_ffn_conv_b, v_ffn_w_down):
    dm, ff = D_MODEL, D_FF
    ix, iy, ic = lax.axis_index("x"), lax.axis_index("y"), lax.axis_index("c")
    chip = 2 * ix + iy
    dev = 2 * chip + ic

    w1 = 10240
    g1 = _allgather8(_flat8([c, a_lb_logits, ffn_conv_w], w1), "gather_cond").reshape(N_DEV, w1)
    c_all = g1[:, :dm]
    per_chip = g1[0::2]
    lb_logits = per_chip[:, dm:dm + 512].reshape(N_CHIPS, 2, 256).transpose(1, 0, 2).reshape(2, dm)
    conv_w = per_chip[:, dm + 512:dm + 512 + 2 * CONV_W * FFN_COLS].reshape(N_CHIPS, 2, CONV_W, FFN_COLS)
    conv_w = conv_w.transpose(1, 2, 0, 3).reshape(2, CONV_W, 2, ff).transpose(0, 2, 1, 3)
    conv_b = ffn_conv_b.reshape(2, 2, 1, ff)
    lb = jax.nn.softmax(lb_logits, axis=0)[0:1]

    c16 = jnp.pad(c_all, ((0, 8), (0, 0)))
    mod_ada, c_act16 = _cond_rows(c16, ada_w, True, "mod_ada")
    mod_kv, _ = _cond_rows(c16, kv_ada_w[None], True, "mod_kv")
    mine = jnp.concatenate([mod_ada[0, :8], mod_ada[1, :8], mod_kv[0, :8]], axis=1)
    w2 = mine.shape[1]
    g2 = _allgather8(mine, "gather_mod").reshape(N_DEV, 8, w2)[0::2]
    my_rows = lax.dynamic_index_in_dim(g2, dev, axis=1, keepdims=False)
    mod0 = my_rows[:, 0:1536].reshape(6 * dm) + ada_b[0]
    mod1 = my_rows[:, 1536:3072].reshape(6 * dm) + ada_b[1]
    modk = my_rows[:, 3072:3584].reshape(2 * dm) + kv_ada_b
    mods = {"l0": [v.reshape(1, dm) for v in jnp.split(mod0, 6)],
            "l1": [v.reshape(1, dm) for v in jnp.split(mod1, 6)],
            "kv": [v.reshape(1, dm) for v in jnp.split(modk, 2)]}

    local = [(a_w_in, 0), (a_w_out, 0), (_pad_cols(kv_w, KV_SHARD_PAD)[None], 0), (b_w_q, 0), (b_w_out, 0),
             (ffn_w_up, 0), (ffn_w_up, 1), (ffn_w_down, 0), (ffn_w_down, 1)]
    chip_arr = chip.reshape(1).astype(jnp.int32)
    own = {n: _cast_own_block(w, layer, chip_arr, "cast_" + n) for n, (w, layer) in zip(BIG, local)}
    stages = {"mixer0": ("a_w_in",), "ffn0": ("a_w_out", "up0", "down0"),
              "layer1": ("kv_w", "b_w_q", "b_w_out", "up1", "down1")}
    arriving = {st: _sequencer_gather([own[n] for n in names], "gather_" + st, cid)
                for cid, (st, names) in enumerate(stages.items(), start=1)}
    rowwise = lambda g: g.reshape(1, -1, dm)

    def weights_at(stage, token):
        got, token = lax.optimization_barrier((arriving[stage], token))
        g = dict(zip(stages[stage], got))
        if stage == "mixer0":
            return {"a_w_in": g["a_w_in"]}, token
        if stage == "ffn0":
            return {"a_w_out": rowwise(g["a_w_out"]), "up0": g["up0"], "down0": rowwise(g["down0"])}, token
        kv_full = g["kv_w"][:, :, :KV_SHARD].transpose(1, 0, 2).reshape(dm, N_CHIPS * KV_SHARD)
        return {"kv_k": kv_full[None, :, :dm], "kv_v": kv_full[None, :, dm:2 * dm],
                "kv_f": _pad_cols(kv_full[None, :, 2 * dm:], LANES), "b_w_q": g["b_w_q"],
                "b_w_out": rowwise(g["b_w_out"]), "up1": g["up1"], "down1": rowwise(g["down1"])}, token

    vecs = {"a_norm_g": jnp.tile(a_norm_g, (1, HEADS)), "k_norm_g": jnp.tile(k_norm_g[None], (1, HEADS)),
            "q_norm_g": jnp.tile(q_norm_g, (1, HEADS)), "kv_b_f": _pad_cols(kv_b_f[None], LANES),
            "conv_w0": conv_w[0], "conv_b0": conv_b[0], "conv_w1": conv_w[1], "conv_b1": conv_b[1]}

    sq, grad_x, big, small, marks = _local_step(x[0], loss_target[0], mods, lb, vecs, weights_at)
    loss = lax.psum(0.5 * jnp.sum(sq) / dm, ("x", "y", "c"))

    kv_grad = jnp.concatenate([big["kv_k"][0], big["kv_v"][0], big["kv_f"][0][:, :HEADS]], axis=1)
    kv_grad = _pad_cols(kv_grad.reshape(dm, N_CHIPS, KV_SHARD).transpose(1, 0, 2), KV_SHARD_PAD)
    chipwise = lambda g: g.reshape(N_CHIPS, -1, dm)
    parts = dict(zip(BIG, [big["a_w_in"], chipwise(big["a_w_out"]), kv_grad, big["b_w_q"], chipwise(big["b_w_out"]),
                           big["up0"], big["up1"], chipwise(big["down0"]), chipwise(big["down1"])]))
    place = jnp.stack([chip, ic, dev]).astype(jnp.int32)

    served = []

    groups = (("up1", "down1"), ("b_w_out", "b_w_q", "kv_w"), ("up0", "down0", "a_w_out"), ("a_w_in",))

    def scatter_group(k):
        mine = [parts[n] for n in groups[k]]
        if served:
            mine, _ = lax.optimization_barrier((mine, served[-1]))
        served.append(_sequencer_scatter(mine, "scatter_grads_%d" % k, 4 + k))

    def sum_group(k, token):
        inboxes, _ = lax.optimization_barrier((served[k], token))
        return [_sum_pieces(parts[n], box, place, "sum_" + n) for n, box in zip(groups[k], inboxes)]

    def swap_group(k, halves, behind):
        halves, _ = lax.optimization_barrier((halves, behind))
        return dict(zip(groups[k], _sequencer_swap_halves(halves, "swap_grads_%d" % k, 8 + k)))

    for k in range(4):
        scatter_group(k)
    halves = [sum_group(0, marks["attention_bwd"]), sum_group(1, marks["ffn0_bwd"]), sum_group(2, marks["mixer0_bwd"])]

    fold = lambda a: a.sum(axis=0)
    heads = lambda a: fold(a).reshape(HEADS, HEAD_DIM).sum(axis=0)
    conv_flat = lambda a: a.sum(axis=2).transpose(1, 0, 2)
    pieces = ([fold(a) for a in small["mod_l0"]] + [fold(a) for a in small["mod_l1"]] + [fold(a) for a in small["mod_kv"]]
              + [conv_flat(small["conv0"]), conv_flat(small["conv1"]), heads(small["a_norm_g"]), heads(small["k_norm_g"]),
                 heads(small["q_norm_g"]), fold(small["kv_b_f"]), fold(small["lb"])])
    w3 = 61440
    small_vec, _ = lax.optimization_barrier((_flat8(pieces, w3), served[3]))
    g3 = _sequencer_allgather8(small_vec, dev, "gather_small", 12)
    rs = {}
    for k in range(3):
        rs.update(swap_group(k, halves[k], g3))
    tot = _sum_devices(g3, "sum_small").reshape(w3)
    n_mod = 14 * dm
    dmod_all = g3.reshape(N_DEV, w3)[:, :n_mod]
    o = n_mod
    conv_tot = [tot[o + l * 8 * ff: o + (l + 1) * 8 * ff].reshape(4, 2 * ff) for l in range(2)]
    o += 16 * ff
    g_a_norm, g_k_norm, g_q_norm = (tot[o + i * HEAD_DIM: o + (i + 1) * HEAD_DIM] for i in range(3))
    o += 3 * HEAD_DIM
    g_kv_b_f = tot[o:o + HEADS]
    dlb = tot[o + LANES:o + LANES + dm]

    ct = _pad_cols(c_act16[:8].T, LANES)
    dmod_pad = jnp.pad(dmod_all, ((0, LANES - N_DEV), (0, 0)))
    cols_ada = jnp.stack([lax.dynamic_slice_in_dim(dmod_pad, l * 6 * dm + chip * 1536, 1536, axis=1) for l in range(2)])
    cols_kv = lax.dynamic_slice_in_dim(dmod_pad, 12 * dm + chip * 512, 512, axis=1)[None]
    g_ada_w = _outer_grad(ct, cols_ada, "grad_ada_w")
    g_kv_ada_w = _outer_grad(ct, cols_kv, "grad_kv_ada_w")[0]

    my_lb = lax.dynamic_slice_in_dim(lb[0], chip * 256, 256)
    l0 = lax.dynamic_slice_in_dim(dlb, chip * 256, 256) * my_lb * (1.0 - my_lb)
    grads = {
        "ada_w": g_ada_w, "ada_b": jnp.stack([tot[:6 * dm], tot[6 * dm:12 * dm]]),
        "a_lb_logits": jnp.stack([l0, -l0]), "a_norm_g": g_a_norm[None],
        "a_w_out": rs["a_w_out"][None], "kv_ada_w": g_kv_ada_w, "kv_ada_b": tot[12 * dm:14 * dm],
        "kv_w": rs["kv_w"][:, :KV_SHARD], "kv_b_f": g_kv_b_f, "k_norm_g": g_k_norm,
        "b_w_q": rs["b_w_q"][None], "q_norm_g": g_q_norm[None], "b_w_out": rs["b_w_out"][None],
        "ffn_w_up": jnp.stack([rs["up0"], rs["up1"]]),
        "ffn_conv_w": jnp.stack([lax.dynamic_slice_in_dim(ct_l[:CONV_W], chip * FFN_COLS, FFN_COLS, axis=1) for ct_l in conv_tot]),
        "ffn_conv_b": jnp.stack([ct_l[CONV_W] for ct_l in conv_tot]),
        "ffn_w_down": jnp.stack([rs["down0"], rs["down1"]]),
    }
    weights = dict(ada_w=ada_w, ada_b=ada_b, a_w_in=a_w_in, a_lb_logits=a_lb_logits, a_norm_g=a_norm_g, a_w_out=a_w_out,
                   kv_ada_w=kv_ada_w, kv_ada_b=kv_ada_b, kv_w=kv_w, kv_b_f=kv_b_f, k_norm_g=k_norm_g, b_w_q=b_w_q,
                   q_norm_g=q_norm_g, b_w_out=b_w_out, ffn_w_up=ffn_w_up, ffn_conv_w=ffn_conv_w, ffn_conv_b=ffn_conv_b,
                   ffn_w_down=ffn_w_down)
    m_in = dict(ada_w=m_ada_w, ada_b=m_ada_b, a_w_in=m_a_w_in, a_lb_logits=m_a_lb_logits, a_norm_g=m_a_norm_g,
                a_w_out=m_a_w_out, kv_ada_w=m_kv_ada_w, kv_ada_b=m_kv_ada_b, kv_w=m_kv_w, kv_b_f=m_kv_b_f,
                k_norm_g=m_k_norm_g, b_w_q=m_b_w_q, q_norm_g=m_q_norm_g, b_w_out=m_b_w_out, ffn_w_up=m_ffn_w_up,
                ffn_conv_w=m_ffn_conv_w, ffn_conv_b=m_ffn_conv_b, ffn_w_down=m_ffn_w_down)
    v_in = dict(ada_w=v_ada_w, ada_b=v_ada_b, a_w_in=v_a_w_in, a_lb_logits=v_a_lb_logits, a_norm_g=v_a_norm_g,
                a_w_out=v_a_w_out, kv_ada_w=v_kv_ada_w, kv_ada_b=v_kv_ada_b, kv_w=v_kv_w, kv_b_f=v_kv_b_f,
                k_norm_g=v_k_norm_g, b_w_q=v_b_w_q, q_norm_g=v_q_norm_g, b_w_out=v_b_w_out, ffn_w_up=v_ffn_w_up,
                ffn_conv_w=v_ffn_conv_w, ffn_conv_b=v_ffn_conv_b, ffn_w_down=v_ffn_w_down)

    names = list(weights)
    step = lambda n: _adamw(weights[n], grads[n], m_in[n], v_in[n], "adamw_" + n)
    grads = {n: g.reshape(weights[n].shape) for n, g in grads.items()}
    upd = {n: step(n) for n in names if n != "a_w_in"}
    last = sum_group(3, [u[0] for u in upd.values()])
    grads["a_w_in"] = swap_group(3, last, last)["a_w_in"][None]
    upd["a_w_in"] = step("a_w_in")
    return (loss, grad_x[None], *[grads[n] for n in names], *[upd[n][0] for n in names],
            *[upd[n][1] for n in names], *[upd[n][2] for n in names])
```

```python
import jax
import jax.numpy as jnp
from jax import lax
from jax.experimental import pallas as pl
from jax.experimental.pallas import tpu as pltpu
from jax.experimental.pallas import tpu_sc as plsc

F32 = jnp.float32
BF16 = jnp.bfloat16

D_MODEL = 1024
HEADS = 8
HEAD_DIM = 128
A_CHUNK = 64
D_FF = 2816
CONV_W = 3
EPS = 1e-6
NEG_INF = -1e30
N_CHIPS = 4
N_DEV = 8

ADAM_LR = 0.001
ADAM_B1 = 0.9
ADAM_B2 = 0.999
ADAM_EPS = 1e-08
ADAM_WD = 0.01
ADAM_STEP = 10

SUBLANES = 8
BF16_ROWS = 16
LANES = 128
HALO = BF16_ROWS
ROW_TILE = 512
TOKEN_TILE_TN = 2048
FFN_COLS = 1408
FFN_ROWS = 256
HGRN_ROWS = 256
ATT_TILE = 512
ATT_SPLIT = 2
ATT_HEADS = 2
MESH = pl.DeviceIdType.MESH


def _sig(x):
    return jax.nn.sigmoid(x)


def _dot(a, b):
    return jnp.dot(a, b, preferred_element_type=F32)


def _dot_nt(a, b):
    return lax.dot_general(a, b, (((1,), (1,)), ((), ())), preferred_element_type=F32)


def _dot_tn(a, b):
    return lax.dot_general(a, b, (((0,), (0,)), ((), ())), preferred_element_type=F32)


def _split2(x):
    hi = x.astype(BF16)
    lo = (x - hi.astype(F32)).astype(BF16)
    return hi, lo


def _dot_f32(a, b):
    ah, al = _split2(a)
    bh, bl = _split2(b)
    return _dot(ah, bh) + _dot(ah, bl) + _dot(al, bh)


def _tri_dot(tri, x):
    hi = x.astype(BF16)
    r = x - hi.astype(F32)
    mid = r.astype(BF16)
    lo = (r - mid.astype(F32)).astype(BF16)
    return _dot(tri, hi) + _dot(tri, mid) + _dot(tri, lo)


def _tri(n, upper=False):
    r = lax.broadcasted_iota(jnp.int32, (n, n), 0)
    c = lax.broadcasted_iota(jnp.int32, (n, n), 1)
    keep = (c >= r) if upper else (c <= r)
    return jnp.where(keep, 1.0, 0.0).astype(BF16)


def _colsum8(v):
    rows, n = v.shape
    return v.reshape(rows // SUBLANES, SUBLANES, n).sum(axis=0)


def _full(shape):
    nd = len(shape)
    return pl.BlockSpec(shape, lambda *_: (0,) * nd)


def _tile(n, want):
    t = min(n, want)
    assert n % t == 0, (n, t)
    return t


def _mm_nn(a, w, groups, out_dtype, name):
    m_rows, k = a.shape
    p_n, _, n = w.shape
    per = p_n // groups
    tm = _tile(m_rows, ROW_TILE)

    def body(a_ref, w_ref, o_ref):
        av = a_ref[...]
        for p in range(p_n):
            o_ref[p // per, :, (p % per) * n:(p % per + 1) * n] = _dot(av, w_ref[p]).astype(out_dtype)

    return pl.pallas_call(
        body, name=name, grid=(m_rows // tm,),
        in_specs=[pl.BlockSpec((tm, k), lambda i: (i, 0)), _full((p_n, k, n))],
        out_specs=pl.BlockSpec((groups, tm, per * n), lambda i: (0, i, 0)),
        out_shape=jax.ShapeDtypeStruct((groups, m_rows, per * n), out_dtype),
    )(a, w)


def _mm_nt(d, w, out_dtype, name):
    g_n, m_rows, _ = d.shape
    p_n, k, n = w.shape
    per = p_n // g_n
    tm = _tile(m_rows, ROW_TILE)

    def body(d_ref, w_ref, o_ref):
        acc = None
        for p in range(p_n):
            t = _dot_nt(d_ref[p // per, :, (p % per) * n:(p % per + 1) * n], w_ref[p])
            acc = t if acc is None else acc + t
        o_ref[...] = acc.astype(out_dtype)

    return pl.pallas_call(
        body, name=name, grid=(m_rows // tm,),
        in_specs=[pl.BlockSpec((g_n, tm, per * n), lambda i: (0, i, 0)), _full((p_n, k, n))],
        out_specs=pl.BlockSpec((tm, k), lambda i: (i, 0)),
        out_shape=jax.ShapeDtypeStruct((m_rows, k), out_dtype),
    )(d, w)


def _mm_tn(a, d, p_n, name):
    m_rows, k = a.shape
    g_n, _, w_cols = d.shape
    per = p_n // g_n
    n = w_cols // per
    tm = _tile(m_rows, TOKEN_TILE_TN if k <= D_MODEL else ROW_TILE)
    steps = m_rows // tm

    def body(a_ref, d_ref, o_ref, acc):
        m = pl.program_id(1)

        @pl.when(m == 0)
        def _():
            acc[...] = jnp.zeros_like(acc)

        acc[...] += _dot_tn(a_ref[...], d_ref[...])

        @pl.when(m == steps - 1)
        def _():
            o_ref[...] = acc[...].astype(BF16)

    return pl.pallas_call(
        body, name=name, grid=(p_n, steps),
        in_specs=[pl.BlockSpec((tm, k), lambda p, m: (m, 0)),
                  pl.BlockSpec((None, tm, n), lambda p, m: (p // per, m, p % per))],
        out_specs=pl.BlockSpec((None, k, n), lambda p, m: (p, 0, 0)),
        out_shape=jax.ShapeDtypeStruct((p_n, k, n), BF16),
        scratch_shapes=[pltpu.VMEM((k, n), F32)],
    )(a, d)


def _premix(x, shift, scale, name):
    s, dm = x.shape
    tm = _tile(s, ROW_TILE)

    def body(x_ref, sh_ref, sc_ref, h_ref):
        xv = x_ref[...]
        inv = lax.rsqrt(jnp.mean(xv * xv, axis=-1, keepdims=True) + EPS)
        h_ref[...] = (xv * inv * (1.0 + sc_ref[...]) + sh_ref[...]).astype(BF16)

    row = pl.BlockSpec((tm, dm), lambda i: (i, 0))
    vec = _full((1, dm))
    return pl.pallas_call(body, name=name, grid=(s // tm,), in_specs=[row, vec, vec], out_specs=row,
                          out_shape=jax.ShapeDtypeStruct((s, dm), BF16))(x, shift, scale)


def _premix_bwd(x, terms, dres, name, branch=None):
    s, dm = x.shape
    tm = _tile(s, ROW_TILE)
    pairs = [pr for _, prs in terms for pr in prs]
    n_in = 2 + len(terms) + 2 * len(pairs) + (2 if branch else 0)

    def body(*refs):
        x_ref, dres_ref = refs[:2]
        sc_refs = refs[2:2 + len(terms)]
        mm_refs = refs[2 + len(terms):2 + len(terms) + 2 * len(pairs)]
        outs = refs[n_in:]

        @pl.when(pl.program_id(0) == 0)
        def _():
            for o in outs[1:1 + 2 * len(terms)]:
                o[...] = jnp.zeros_like(o)
            if branch:
                outs[-1][...] = jnp.zeros_like(outs[-1])

        xv = x_ref[...]
        inv = lax.rsqrt(jnp.mean(xv * xv, axis=-1, keepdims=True) + EPS)
        r = xv * inv
        dx = dres_ref[...]
        k = 0
        for t, (_, prs) in enumerate(terms):
            dh = None
            for d, w in prs:
                d_ref, w_ref = mm_refs[2 * k], mm_refs[2 * k + 1]
                k += 1
                p_n, _, n = w.shape
                per = p_n // d.shape[0]
                for p in range(p_n):
                    part = _dot_nt(d_ref[p // per, :, (p % per) * n:(p % per + 1) * n], w_ref[p])
                    dh = part if dh is None else dh + part
            dr = dh * (1.0 + sc_refs[t][...])
            dx = dx + inv * (dr - r * jnp.mean(dr * r, axis=-1, keepdims=True))
            outs[1 + 2 * t][...] += _colsum8(dh)
            outs[2 + 2 * t][...] += _colsum8(dh * r)
        outs[0][...] = dx
        if branch:
            y_ref, g_ref = refs[n_in - 2:n_in]
            outs[-2][0] = (dx * g_ref[...]).astype(BF16)
            outs[-1][...] += _colsum8(dx * y_ref[...])

    row = pl.BlockSpec((tm, dm), lambda i: (i, 0))
    vec, acc = _full((1, dm)), _full((SUBLANES, dm))
    ins, specs = [x, dres] + [sc for sc, _ in terms], [row, row] + [vec] * len(terms)
    for d, w in pairs:
        ins += [d, w]
        specs += [pl.BlockSpec((d.shape[0], tm, d.shape[2]), lambda i: (0, i, 0)), _full(w.shape)]
    out_shape = [jax.ShapeDtypeStruct((s, dm), F32)] + [jax.ShapeDtypeStruct((SUBLANES, dm), F32)] * (2 * len(terms))
    out_specs = [row] + [acc] * (2 * len(terms))
    if branch:
        ins += list(branch)
        specs += [row, vec]
        out_shape += [jax.ShapeDtypeStruct((1, s, dm), BF16), jax.ShapeDtypeStruct((SUBLANES, dm), F32)]
        out_specs += [pl.BlockSpec((1, tm, dm), lambda i: (0, i, 0)), acc]
    outs = pl.pallas_call(body, name=name, grid=(s // tm,), in_specs=specs, out_specs=out_specs,
                          out_shape=out_shape)(*ins)
    partials = [(outs[1 + 2 * t], outs[2 + 2 * t]) for t in range(len(terms))]
    return (outs[0], partials) + ((outs[-2], outs[-1]) if branch else ())


def _conv_taps(e, w, b):
    return w[2:3] * e + w[1:2] * pltpu.roll(e, 1, 0) + w[0:1] * pltpu.roll(e, 2, 0) + b


def _ffn_specs(s, tm, cb):
    hb = tm // HALO
    last = s // HALO - 1
    main = pl.BlockSpec((2, tm, cb), lambda j, i: (0, i, j))
    prev = pl.BlockSpec((2, HALO, cb), lambda j, i: (0, jnp.maximum(i * hb - 1, 0), j))
    nxt = pl.BlockSpec((2, HALO, cb), lambda j, i: (0, jnp.minimum((i + 1) * hb, last), j))
    wspec = pl.BlockSpec((2, CONV_W, cb), lambda j, i: (0, 0, j))
    bspec = pl.BlockSpec((2, 1, cb), lambda j, i: (0, 0, j))
    return main, prev, nxt, wspec, bspec


def _convglu_bwd(u, dffn, w_down, w, b, name):
    _, s, f = u.shape
    dm = dffn.shape[2]
    tm = _tile(s, 256)
    cb = _tile(f, FFN_COLS)
    steps = s // tm
    n_ext = tm + 2 * HALO
    main, prev, nxt, wspec, bspec = _ffn_specs(s, tm, cb)
    hb = tm // HALO
    last = s // HALO - 1
    d_main = pl.BlockSpec((None, tm, dm), lambda j, i: (0, i, 0))
    d_next = pl.BlockSpec((None, HALO, dm), lambda j, i: (0, jnp.minimum((i + 1) * hb, last), 0))
    wd_spec = pl.BlockSpec((None, cb, dm), lambda j, i: (0, j, 0))

    def body(u_ref, up_ref, un_ref, d_ref, dn_ref, wd_ref, w_ref, b_ref, du_ref, acc_ref):
        i = pl.program_id(1)
        first = jnp.where(i > 0, 1.0, 0.0)
        notlast = jnp.where(i < steps - 1, 1.0, 0.0)

        @pl.when(i == 0)
        def _():
            acc_ref[...] = jnp.zeros_like(acc_ref)

        def ext(g):
            return jnp.concatenate([up_ref[g].astype(F32) * first, u_ref[g].astype(F32), un_ref[g].astype(F32)], axis=0)

        ug, uv = ext(0), ext(1)
        gate = _conv_taps(ug, w_ref[0], b_ref[0])
        val = _conv_taps(uv, w_ref[1], b_ref[1])
        wd = wd_ref[...]
        da = _dot_nt(d_ref[...], wd).astype(BF16).astype(F32)
        da_next = _dot_nt(dn_ref[...], wd).astype(BF16).astype(F32) * notlast
        da_e = jnp.concatenate([jnp.zeros((HALO, cb), F32), da, da_next], axis=0)
        sg = _sig(gate)
        d_val = da_e * gate * sg
        d_gate = da_e * val * (sg * (1.0 + gate * (1.0 - sg)))

        def finish(g, d, e):
            wv = w_ref[g]
            rows = slice(HALO, HALO + tm)
            d1, d2 = pltpu.roll(d, n_ext - 1, 0), pltpu.roll(d, n_ext - 2, 0)
            du_ref[g] = (wv[2:3] * d + wv[1:2] * d1 + wv[0:1] * d2)[rows].astype(BF16)
            em = e[rows]
            acc_ref[g, 2] += _colsum8(d[rows] * em)
            acc_ref[g, 1] += _colsum8(d1[rows] * em)
            acc_ref[g, 0] += _colsum8(d2[rows] * em)
            acc_ref[g, 3] += _colsum8(d[rows])

        finish(0, d_gate, ug)
        finish(1, d_val, uv)

    return pl.pallas_call(
        body, name=name, grid=(f // cb, steps),
        in_specs=[main, prev, nxt, d_main, d_next, wd_spec, wspec, bspec],
        out_specs=[main, pl.BlockSpec((2, 4, SUBLANES, cb), lambda j, i: (0, 0, 0, j))],
        out_shape=[jax.ShapeDtypeStruct((2, s, f), BF16), jax.ShapeDtypeStruct((2, 4, SUBLANES, f), F32)],
    )(u, u, u, dffn, dffn, w_down, w, b)


def _hgrn_gates(q_raw, f_raw, lb, tri):
    sf = _sig(f_raw)
    fg = lb + (1.0 - lb) * sf
    b = _tri_dot(tri, jnp.log(fg))
    return q_raw * _sig(q_raw), 1.0 - fg, b, fg, sf


def _hgrn_fwd(proj, lb, norm_g, name):
    s = proj.shape[0]
    tb = _tile(s, HGRN_ROWS)
    n_c = tb // A_CHUNK
    half = A_CHUNK // 2

    def body(q_ref, f_ref, v_ref, g_ref, lb_ref, ng_ref, o_ref, yp_ref, st_ref, state):
        @pl.when(pl.program_id(0) == 0)
        def _():
            state[...] = jnp.zeros_like(state)

        tri = _tri(A_CHUNK)
        causal = lax.broadcasted_iota(jnp.int32, (A_CHUNK, A_CHUNK), 1) <= lax.broadcasted_iota(
            jnp.int32, (A_CHUNK, A_CHUNK), 0)

        def chunk(ci, carry):
            rows = pl.ds(pl.multiple_of(ci * A_CHUNK, A_CHUNK), A_CHUNK)
            for h in range(HEADS):
                cs = slice(h * HEAD_DIM, (h + 1) * HEAD_DIM)
                qs, k, b, _, _ = _hgrn_gates(q_ref[rows, cs], f_ref[rows, cs], lb_ref[:, cs], tri)
                b_mid, b_last = b[half:half + 1], b[A_CHUNK - 1:A_CHUNK]
                vb = v_ref[rows, cs].astype(BF16)
                scores = _dot_nt((qs * jnp.exp(b - b_mid)).astype(BF16), (k * jnp.exp(b_mid - b)).astype(BF16))
                scores = jnp.where(causal, scores, 0.0)
                st = state[h]
                st_ref[ci, h] = st
                o = _dot(scores.astype(BF16), vb) + _dot_nt((qs * jnp.exp(b)).astype(BF16), st.astype(BF16))
                state[h] = st * jnp.exp(b_last) + _dot_tn(vb, (k * jnp.exp(b_last - b)).astype(BF16))
                o_ref[rows, cs] = o
                inv = lax.rsqrt(jnp.mean(o * o, axis=-1, keepdims=True) + EPS)
                g_raw = g_ref[rows, cs]
                yp_ref[rows, cs] = (o * inv * ng_ref[:, cs] * (g_raw * _sig(g_raw))).astype(BF16)
            return carry

        lax.fori_loop(0, n_c, chunk, 0)

    col = lambda j: pl.BlockSpec((tb, D_MODEL), lambda i: (i, j))
    vec = _full((1, D_MODEL))
    return pl.pallas_call(
        body, name=name, grid=(s // tb,), in_specs=[col(0), col(1), col(2), col(3), vec, vec],
        out_specs=[col(0), col(0), pl.BlockSpec((n_c, HEADS, HEAD_DIM, HEAD_DIM), lambda i: (i, 0, 0, 0))],
        out_shape=[jax.ShapeDtypeStruct((s, D_MODEL), F32), jax.ShapeDtypeStruct((s, D_MODEL), BF16),
                   jax.ShapeDtypeStruct((s // A_CHUNK, HEADS, HEAD_DIM, HEAD_DIM), F32)],
        scratch_shapes=[pltpu.VMEM((HEADS, HEAD_DIM, HEAD_DIM), F32)],
    )(proj, proj, proj, proj, lb, norm_g)


def _hgrn_bwd(proj, lb, norm_g, o, states, dyp, name):
    s = proj.shape[0]
    tb = _tile(s, HGRN_ROWS)
    n_c = tb // A_CHUNK
    n_b = s // tb
    half = A_CHUNK // 2

    def body(q_ref, f_ref, v_ref, g_ref, lb_ref, ng_ref, o_ref, st_ref, dyp_ref, dp_ref, dlb_ref, dng_ref, dstate):
        @pl.when(pl.program_id(0) == 0)
        def _():
            dstate[...] = jnp.zeros_like(dstate)
            dlb_ref[...] = jnp.zeros_like(dlb_ref)
            dng_ref[...] = jnp.zeros_like(dng_ref)

        tri = _tri(A_CHUNK)
        tri_up = _tri(A_CHUNK, upper=True)
        row_id = lax.broadcasted_iota(jnp.int32, (A_CHUNK, HEAD_DIM), 0)
        causal = lax.broadcasted_iota(jnp.int32, (A_CHUNK, A_CHUNK), 1) <= lax.broadcasted_iota(
            jnp.int32, (A_CHUNK, A_CHUNK), 0)

        def chunk(cj, carry):
            ci = n_c - 1 - cj
            rows = pl.ds(pl.multiple_of(ci * A_CHUNK, A_CHUNK), A_CHUNK)
            for h in range(HEADS):
                cs = slice(h * HEAD_DIM, (h + 1) * HEAD_DIM)
                q_raw, lbh = q_ref[rows, cs], lb_ref[:, cs]
                qs, k, b, fg, sf = _hgrn_gates(q_raw, f_ref[rows, cs], lbh, tri)
                b_mid, b_last = b[half:half + 1], b[A_CHUNK - 1:A_CHUNK]
                e_qi, e_ki, e_q, e_ks = jnp.exp(b - b_mid), jnp.exp(b_mid - b), jnp.exp(b), jnp.exp(b_last - b)
                q_i, k_i, q_e, k_s = qs * e_qi, k * e_ki, qs * e_q, k * e_ks
                vb = v_ref[rows, cs].astype(BF16)
                scores = jnp.where(causal, _dot_nt(q_i.astype(BF16), k_i.astype(BF16)), 0.0)
                ov, g_raw, dy, ng = o_ref[rows, cs], g_ref[rows, cs], dyp_ref[rows, cs], ng_ref[:, cs]
                inv = lax.rsqrt(jnp.mean(ov * ov, axis=-1, keepdims=True) + EPS)
                nrm = ov * inv
                sg = _sig(g_raw)
                gs = g_raw * sg
                dn = dy * ng * gs
                dng_ref[0:1, cs] += jnp.sum(dy * nrm * gs, axis=0, keepdims=True)
                dg_raw = dy * nrm * ng * (sg * (1.0 + g_raw * (1.0 - sg)))
                do = (inv * (dn - nrm * jnp.mean(dn * nrm, axis=-1, keepdims=True))).astype(BF16)
                st_prev = st_ref[ci, h]
                dst = dstate[h]
                dstb = dst.astype(BF16)
                d_scores = jnp.where(causal, _dot_nt(do, vb), 0.0).astype(BF16)
                dv = _dot_tn(scores.astype(BF16), do) + _dot_nt(k_s.astype(BF16), dstb)
                dq_i = _dot(d_scores, k_i.astype(BF16))
                dk_i = _dot_tn(d_scores, q_i.astype(BF16))
                dq_e = _dot(do, st_prev.astype(BF16))
                dk_s = _dot(vb, dstb)
                d_decay = jnp.sum(st_prev * dst, axis=0, keepdims=True)
                dstate[h] = dst * jnp.exp(b_last) + _dot_tn(do, q_e.astype(BF16))
                dq = dq_i * e_qi + dq_e * e_q
                dk = dk_i * e_ki + dk_s * e_ks
                t_qi, t_ki, t_ks = dq_i * q_i, dk_i * k_i, dk_s * k_s
                db = t_qi - t_ki + dq_e * q_e - t_ks
                db_mid = jnp.sum(t_ki - t_qi, axis=0, keepdims=True)
                db_last = jnp.sum(t_ks, axis=0, keepdims=True) + d_decay * jnp.exp(b_last)
                db = db + jnp.where(row_id == half, db_mid, 0.0) + jnp.where(row_id == A_CHUNK - 1, db_last, 0.0)
                dfg = _tri_dot(tri_up, db) / fg - dk
                dlb_ref[0:1, cs] += jnp.sum(dfg * (1.0 - sf), axis=0, keepdims=True)
                sq = _sig(q_raw)
                dp_ref[0, rows, cs] = (dq * (sq * (1.0 + q_raw * (1.0 - sq)))).astype(BF16)
                dp_ref[1, rows, cs] = (dfg * (1.0 - lbh) * sf * (1.0 - sf)).astype(BF16)
                dp_ref[2, rows, cs] = dv.astype(BF16)
                dp_ref[3, rows, cs] = dg_raw.astype(BF16)
            return carry

        lax.fori_loop(0, n_c, chunk, 0)

    col = lambda j: pl.BlockSpec((tb, D_MODEL), lambda i: (n_b - 1 - i, j))
    vec = _full((1, D_MODEL))
    acc = _full((SUBLANES, D_MODEL))
    return pl.pallas_call(
        body, name=name, grid=(n_b,),
        in_specs=[col(0), col(1), col(2), col(3), vec, vec, col(0),
                  pl.BlockSpec((n_c, HEADS, HEAD_DIM, HEAD_DIM), lambda i: (n_b - 1 - i, 0, 0, 0)), col(0)],
        out_specs=[pl.BlockSpec((4, tb, D_MODEL), lambda i: (0, n_b - 1 - i, 0)), acc, acc],
        out_shape=[jax.ShapeDtypeStruct((4, s, D_MODEL), BF16), jax.ShapeDtypeStruct((SUBLANES, D_MODEL), F32),
                   jax.ShapeDtypeStruct((SUBLANES, D_MODEL), F32)],
        scratch_shapes=[pltpu.VMEM((HEADS, HEAD_DIM, HEAD_DIM), F32)],
    )(proj, proj, proj, proj, lb, norm_g, o, states, dyp)


def _headnorm(x, g, mult, name, col0=0):
    s = x.shape[0]
    tm = _tile(s, ROW_TILE)

    def body(x_ref, g_ref, y_ref):
        for h in range(HEADS):
            cs = slice(h * HEAD_DIM, (h + 1) * HEAD_DIM)
            xv = x_ref[:, cs]
            inv = lax.rsqrt(jnp.mean(xv * xv, axis=-1, keepdims=True) + EPS)
            y_ref[:, cs] = (xv * inv * g_ref[:, cs] * mult).astype(BF16)

    return pl.pallas_call(
        body, name=name, grid=(s // tm,),
        in_specs=[pl.BlockSpec((tm, D_MODEL), lambda i: (i, col0)), _full((1, D_MODEL))],
        out_specs=pl.BlockSpec((tm, D_MODEL), lambda i: (i, 0)),
        out_shape=jax.ShapeDtypeStruct((s, D_MODEL), BF16),
    )(x, g)


def _headnorm_bwd(x, g, mult, dy, name, col0=0, extra=None):
    s = x.shape[0]
    tm = _tile(s, ROW_TILE)
    groups = 2 if extra is not None else 1
    head_major = dy.ndim == 3

    def body(*refs):
        x_ref, g_ref, dy_ref = refs[:3]
        dx_ref, dg_ref = refs[-2:]

        @pl.when(pl.program_id(0) == 0)
        def _():
            dg_ref[...] = jnp.zeros_like(dg_ref)

        for h in range(HEADS):
            cs = slice(h * HEAD_DIM, (h + 1) * HEAD_DIM)
            xv, gv = x_ref[:, cs], g_ref[:, cs]
            dyv = dy_ref[h, :, 0:HEAD_DIM] if head_major else dy_ref[:, cs]
            inv = lax.rsqrt(jnp.mean(xv * xv, axis=-1, keepdims=True) + EPS)
            nrm = xv * inv
            dn = dyv * gv * mult
            dg_ref[:, cs] += _colsum8(dyv * nrm * mult)
            dx_ref[0, :, cs] = (inv * (dn - nrm * jnp.mean(dn * nrm, axis=-1, keepdims=True))).astype(BF16)
        if extra is not None:
            dx_ref[1] = refs[3][...]

    row = pl.BlockSpec((tm, D_MODEL), lambda i: (i, 0))
    dy_spec = pl.BlockSpec((HEADS, tm, dy.shape[-1]), lambda i: (0, i, 0)) if head_major else row
    ins = [x, g, dy] + ([extra] if extra is not None else [])
    specs = ([pl.BlockSpec((tm, D_MODEL), lambda i: (i, col0)), _full((1, D_MODEL)), dy_spec]
             + ([row] if extra is not None else []))
    return pl.pallas_call(
        body, name=name, grid=(s // tm,), in_specs=specs,
        out_specs=[pl.BlockSpec((groups, tm, D_MODEL), lambda i: (0, i, 0)), _full((SUBLANES, D_MODEL))],
        out_shape=[jax.ShapeDtypeStruct((groups, s, D_MODEL), BF16), jax.ShapeDtypeStruct((SUBLANES, D_MODEL), F32)],
    )(*ins)


def _log_sigmoid(z):
    return jnp.minimum(z, 0.0) - jnp.log(1.0 + jnp.exp(-jnp.abs(z)))


Q_CUM, Q_ONE, Q_LSE = 0, 3, 6
LOG2E = 1.4426950408889634


def _pieces(v):
    hi = v.astype(BF16).astype(F32)
    mid = (v - hi).astype(BF16).astype(F32)
    lo = ((v - hi) - mid).astype(BF16).astype(F32)
    return hi, mid, lo


def _side(lane, at, v):
    hi, mid, lo = _pieces(v)
    return jnp.where(lane == at, hi, jnp.where(lane == at + 1, mid, jnp.where(lane == at + 2, lo, 0.0)))


def _fcum_fwd(f, bias, name):
    s = f.shape[0]
    tm = _tile(s, ROW_TILE)

    def body(f_ref, b_ref, qa_ref, ka_ref, carry):
        @pl.when(pl.program_id(0) == 0)
        def _():
            carry[...] = jnp.zeros_like(carry)

        cum = _tri_dot(_tri(tm), _log_sigmoid(f_ref[...] + b_ref[...])) + carry[...]
        carry[...] = cum[tm - 1:tm]
        lane = lax.broadcasted_iota(jnp.int32, (tm, LANES), 1)
        ones_q = jnp.where((lane >= Q_ONE) & (lane < Q_LSE), 1.0, 0.0)
        ones_k = jnp.where((lane < Q_ONE) | ((lane >= Q_LSE) & (lane < Q_LSE + 3)), 1.0, 0.0)
        for h in range(HEADS):
            c2 = cum[:, h:h + 1] * LOG2E
            qa_ref[h] = (_side(lane, Q_CUM, c2) + ones_q).astype(BF16)
            ka_ref[h] = (_side(lane, Q_ONE, -c2) + ones_k).astype(BF16)

    side = pl.BlockSpec((HEADS, tm, LANES), lambda i: (0, i, 0))
    return pl.pallas_call(
        body, name=name, grid=(s // tm,),
        in_specs=[pl.BlockSpec((tm, LANES), lambda i: (i, 0)), _full((1, LANES))],
        out_specs=[side, side],
        out_shape=[jax.ShapeDtypeStruct((HEADS, s, LANES), BF16)] * 2,
        scratch_shapes=[pltpu.VMEM((1, LANES), F32)],
    )(f, bias)


def _fcum_bwd(f, bias, dka, dq, name):
    s = f.shape[0]
    tm = _tile(s, ROW_TILE)
    n_b = s // tm
    q_lane = HEAD_DIM + Q_CUM

    def body(f_ref, b_ref, dka_ref, dqa_ref, dz_ref, db_ref, carry):
        @pl.when(pl.program_id(0) == 0)
        def _():
            carry[...] = jnp.zeros_like(carry)
            db_ref[...] = jnp.zeros_like(db_ref)

        lane = lax.broadcasted_iota(jnp.int32, (tm, LANES), 1)
        dcum = jnp.zeros((tm, LANES), F32)
        for h in range(HEADS):
            dcum = dcum + jnp.where(lane == h, dqa_ref[h, :, q_lane:q_lane + 1] - dka_ref[h, :, Q_ONE:Q_ONE + 1], 0.0)
        dlf = _tri_dot(_tri(tm, upper=True), dcum) + carry[...]
        carry[...] = dlf[0:1]
        dz = dlf * _sig(-(f_ref[...] + b_ref[...]))
        dz_ref[0] = dz.astype(BF16)
        db_ref[...] += _colsum8(dz)

    return pl.pallas_call(
        body, name=name, grid=(n_b,),
        in_specs=[pl.BlockSpec((tm, LANES), lambda i: (n_b - 1 - i, 0)), _full((1, LANES)),
                  pl.BlockSpec((HEADS, tm, LANES), lambda i: (0, n_b - 1 - i, 0)),
                  pl.BlockSpec((HEADS, tm, 2 * HEAD_DIM), lambda i: (0, n_b - 1 - i, 0))],
        out_specs=[pl.BlockSpec((1, tm, LANES), lambda i: (0, n_b - 1 - i, 0)), _full((SUBLANES, LANES))],
        out_shape=[jax.ShapeDtypeStruct((1, s, LANES), BF16), jax.ShapeDtypeStruct((SUBLANES, LANES), F32)],
        scratch_shapes=[pltpu.VMEM((1, LANES), F32)],
    )(f, bias, dka, dq)


def _causal_pairs(n_t, key_major):
    if key_major:
        pairs = [(qi, ki) for ki in range(n_t) for qi in range(ki, n_t)]
    else:
        pairs = [(qi, ki) for qi in range(n_t) for ki in range(qi + 1)]
    return (jnp.array([p[0] for p in pairs], jnp.int32), jnp.array([p[1] for p in pairs], jnp.int32))


def _with_side(main_ref, side_ref):
    return jnp.concatenate([main_ref[...], side_ref[...]], axis=1)


def _lane_const(t, lo, hi, value):
    lane = lax.broadcasted_iota(jnp.int32, (t, LANES), 1)
    return jnp.where((lane >= lo) & (lane < hi), value, 0.0).astype(BF16)


def _att_specs(t):
    qmain = pl.BlockSpec((t, ATT_HEADS * HEAD_DIM), lambda h, p, qt, kt: (qt[p], h))
    kmain = pl.BlockSpec((t, ATT_HEADS * HEAD_DIM), lambda h, p, qt, kt: (kt[p], h))
    qside = pl.BlockSpec((ATT_HEADS, t, LANES), lambda h, p, qt, kt: (h, qt[p], 0))
    kside = pl.BlockSpec((ATT_HEADS, t, LANES), lambda h, p, qt, kt: (h, kt[p], 0))
    return qmain, kmain, qside, kside


def _fox_fwd(q, qa, k, ka, v, qo, name):
    s = q.shape[0]
    t = _tile(s, ATT_TILE)
    sub = t // ATT_SPLIT
    qt, kt = _causal_pairs(s // t, key_major=False)

    def body(qt_ref, kt_ref, q_ref, qa_ref, k_ref, ka_ref, v_ref, og_ref, o_ref, y_ref, qab_ref, m_s, l_s, acc_s):
        pid = pl.program_id(1)
        qi, ki = qt_ref[pid], kt_ref[pid]

        @pl.when(ki == 0)
        def _():
            m_s[...] = jnp.full_like(m_s, NEG_INF)
            l_s[...] = jnp.zeros_like(l_s)
            acc_s[...] = jnp.zeros_like(acc_s)

        def step(diagonal):
            for hh in range(ATT_HEADS):
                hc = slice(hh * HEAD_DIM, (hh + 1) * HEAD_DIM)
                kc = jnp.concatenate([k_ref[:, hc], ka_ref[hh]], axis=1)
                vc = jnp.concatenate([v_ref[:, hc], _lane_const(t, 0, 1, 1.0)], axis=1)
                for r in range(ATT_SPLIT):
                    rows = slice(r * sub, (r + 1) * sub)
                    n_k = (r + 1) * sub if diagonal else t
                    sc = _dot_nt(jnp.concatenate([q_ref[rows, hc], qa_ref[hh, rows]], axis=1), kc[:n_k])
                    if diagonal:
                        sc = jnp.where(lax.broadcasted_iota(jnp.int32, (sub, n_k), 1)
                                       <= lax.broadcasted_iota(jnp.int32, (sub, n_k), 0) + r * sub, sc, NEG_INF)
                    m_old = m_s[hh, rows]
                    m_new = jnp.maximum(m_old, jnp.max(sc, axis=-1, keepdims=True))
                    alpha = jnp.exp2(m_old - m_new)
                    pv = _dot(jnp.exp2(sc - m_new[:, 0:1]).astype(BF16), vc[:n_k])
                    acc_s[hh, rows] = alpha * acc_s[hh, rows] + pv[:, :HEAD_DIM]
                    l_s[hh, rows] = alpha * l_s[hh, rows] + pv[:, HEAD_DIM:]
                    m_s[hh, rows] = m_new

        @pl.when(ki < qi)
        def _():
            step(False)

        @pl.when(ki == qi)
        def _():
            step(True)
            lane = lax.broadcasted_iota(jnp.int32, (t, LANES), 1)
            for hh in range(ATT_HEADS):
                hc = slice(hh * HEAD_DIM, (hh + 1) * HEAD_DIM)
                l = l_s[hh, :, 0:1]
                o = acc_s[hh] / l
                o_ref[:, hc] = o
                y_ref[:, hc] = (o * _sig(og_ref[:, hc])).astype(BF16)
                qab_ref[hh] = qa_ref[hh] + _side(lane, Q_LSE, -(m_s[hh, :, 0:1] + jnp.log2(l))).astype(BF16)

    qmain, kmain, qside, kside = _att_specs(t)
    return pl.pallas_call(
        body, name=name,
        grid_spec=pltpu.PrefetchScalarGridSpec(
            num_scalar_prefetch=2, grid=(HEADS // ATT_HEADS, qt.shape[0]),
            in_specs=[qmain, qside, kmain, kside, kmain,
                      pl.BlockSpec((t, ATT_HEADS * HEAD_DIM), lambda h, p, qt, kt: (qt[p], HEADS // ATT_HEADS + h))],
            out_specs=[qmain, qmain, qside],
            scratch_shapes=[pltpu.VMEM((ATT_HEADS, t, LANES), F32), pltpu.VMEM((ATT_HEADS, t, LANES), F32),
                            pltpu.VMEM((ATT_HEADS, t, HEAD_DIM), F32)]),
        out_shape=[jax.ShapeDtypeStruct((s, D_MODEL), F32), jax.ShapeDtypeStruct((s, D_MODEL), BF16),
                   jax.ShapeDtypeStruct((HEADS, s, LANES), BF16)],
    )(qt, kt, q, qa, k, ka, v, qo)


def _fox_gate_bwd(o, qo, dy, name):
    s = o.shape[0]
    tm = _tile(s, ROW_TILE)

    def body(o_ref, og_ref, dy_ref, do_ref, dg_ref, dl_ref):
        ov, dyv = o_ref[...], dy_ref[...]
        sg = _sig(og_ref[...])
        do = (dyv * sg).astype(BF16)
        do_ref[...] = do
        dg_ref[...] = (dyv * ov * sg * (1.0 - sg)).astype(BF16)
        prod = do.astype(F32) * ov
        lane = lax.broadcasted_iota(jnp.int32, (tm, LANES), 1)
        for h in range(HEADS):
            delta = jnp.sum(prod[:, h * HEAD_DIM:(h + 1) * HEAD_DIM], axis=-1, keepdims=True)
            dl_ref[h] = _side(lane, 0, delta).astype(BF16)

    row = pl.BlockSpec((tm, D_MODEL), lambda i: (i, 0))
    return pl.pallas_call(
        body, name=name, grid=(s // tm,),
        in_specs=[row, pl.BlockSpec((tm, D_MODEL), lambda i: (i, 1)), row],
        out_specs=[row, row, pl.BlockSpec((HEADS, tm, LANES), lambda i: (0, i, 0))],
        out_shape=[jax.ShapeDtypeStruct((s, D_MODEL), BF16), jax.ShapeDtypeStruct((s, D_MODEL), BF16),
                   jax.ShapeDtypeStruct((HEADS, s, LANES), BF16)],
    )(o, qo, dy)


def _fox_bwd(q, qab, k, ka, v, do, doa, name):
    s = q.shape[0]
    t = _tile(s, ATT_TILE)
    n_t = s // t
    sub = t // ATT_SPLIT
    qt, kt = _causal_pairs(n_t, key_major=True)

    def body(qt_ref, kt_ref, q_ref, qab_ref, k_ref, ka_ref, v_ref, do_ref, doa_ref, dk_ref, dv_ref, dka_ref, dq_ref,
             dk_s, dv_s):
        pid = pl.program_id(1)
        qi, ki = qt_ref[pid], kt_ref[pid]

        @pl.when(pid == 0)
        def _():
            dq_ref[...] = jnp.zeros_like(dq_ref)

        @pl.when(qi == ki)
        def _():
            dk_s[...] = jnp.zeros_like(dk_s)
            dv_s[...] = jnp.zeros_like(dv_s)

        def step(diagonal):
            for hh in range(ATT_HEADS):
                hc = slice(hh * HEAD_DIM, (hh + 1) * HEAD_DIM)
                kc = jnp.concatenate([k_ref[:, hc], ka_ref[hh]], axis=1)
                vc = jnp.concatenate([v_ref[:, hc], _lane_const(t, 0, 3, -1.0)], axis=1)
                for r in range(ATT_SPLIT):
                    cols = slice(r * sub, (r + 1) * sub)
                    n_k = (r + 1) * sub if diagonal else t
                    qc = jnp.concatenate([q_ref[cols, hc], qab_ref[hh, cols]], axis=1)
                    sc = _dot_nt(kc[:n_k], qc)
                    if diagonal:
                        sc = jnp.where(lax.broadcasted_iota(jnp.int32, (n_k, sub), 0)
                                       <= lax.broadcasted_iota(jnp.int32, (n_k, sub), 1) + r * sub, sc, NEG_INF)
                    p = jnp.exp2(sc)
                    dov = do_ref[cols, hc]
                    dp = _dot_nt(vc[:n_k], jnp.concatenate([dov, doa_ref[hh, cols]], axis=1))
                    ds = (p * dp).astype(BF16)
                    dv_s[hh, 0:n_k] += _dot(p.astype(BF16), dov)
                    dk_s[hh, 0:n_k] += _dot(ds, qc)
                    q_rows = pl.ds(pl.multiple_of(qi * t + r * sub, sub), sub)
                    dq_ref[hh, q_rows, :] += _dot_tn(ds, kc[:n_k])

        @pl.when(qi > ki)
        def _():
            step(False)

        @pl.when(qi == ki)
        def _():
            step(True)

        @pl.when(qi == n_t - 1)
        def _():
            for hh in range(ATT_HEADS):
                hc = slice(hh * HEAD_DIM, (hh + 1) * HEAD_DIM)
                dk_ref[:, hc] = dk_s[hh, :, :HEAD_DIM] * (1.0 / LOG2E)
                dka_ref[hh] = dk_s[hh, :, HEAD_DIM:]
                dv_ref[:, hc] = dv_s[hh].astype(BF16)

    qmain, kmain, qside, kside = _att_specs(t)
    return pl.pallas_call(
        body, name=name,
        grid_spec=pltpu.PrefetchScalarGridSpec(
            num_scalar_prefetch=2, grid=(HEADS // ATT_HEADS, qt.shape[0]),
            in_specs=[qmain, qside, kmain, kside, kmain, qmain, qside],
            out_specs=[kmain, pl.BlockSpec((None, t, ATT_HEADS * HEAD_DIM), lambda h, p, qt, kt: (0, kt[p], h)), kside,
                       pl.BlockSpec((ATT_HEADS, s, 2 * HEAD_DIM), lambda h, p, qt, kt: (h, 0, 0))],
            scratch_shapes=[pltpu.VMEM((ATT_HEADS, t, 2 * HEAD_DIM), F32), pltpu.VMEM((ATT_HEADS, t, HEAD_DIM), F32)]),
        out_shape=[jax.ShapeDtypeStruct((s, D_MODEL), F32), jax.ShapeDtypeStruct((1, s, D_MODEL), BF16),
                   jax.ShapeDtypeStruct((HEADS, s, LANES), F32), jax.ShapeDtypeStruct((HEADS, s, 2 * HEAD_DIM), F32)],
    )(qt, kt, q, qab, k, ka, v, do, doa)


def _mm_residual_premix(a, w, x, gate, mods, name):
    s, k = a.shape
    dm = x.shape[1]
    tm = _tile(s, ROW_TILE)

    def body(*refs):
        a_ref, w_ref, x_ref, g_ref = refs[:4]
        mod_refs = refs[4:4 + 2 * len(mods)]
        y_ref, xn_ref = refs[4 + 2 * len(mods):6 + 2 * len(mods)]
        h_refs = refs[6 + 2 * len(mods):]
        y = _dot(a_ref[...], w_ref[0])
        y_ref[...] = y
        xv = x_ref[...] + g_ref[...] * y
        xn_ref[...] = xv
        nrm = xv * lax.rsqrt(jnp.mean(xv * xv, axis=-1, keepdims=True) + EPS)
        for t, h_ref in enumerate(h_refs):
            h_ref[...] = (nrm * (1.0 + mod_refs[2 * t + 1][...]) + mod_refs[2 * t][...]).astype(BF16)

    row = pl.BlockSpec((tm, dm), lambda i: (i, 0))
    vec = _full((1, dm))
    outs = pl.pallas_call(
        body, name=name, grid=(s // tm,),
        in_specs=[pl.BlockSpec((tm, k), lambda i: (i, 0)), _full(w.shape), row, vec] + [vec] * (2 * len(mods)),
        out_specs=[row] * (2 + len(mods)),
        out_shape=[jax.ShapeDtypeStruct((s, dm), F32)] * 2 + [jax.ShapeDtypeStruct((s, dm), BF16)] * len(mods),
    )(a, w, x, gate, *[v for m in mods for v in m])
    return outs[0], outs[1], list(outs[2:])


def _mm_loss_head(a, w, x, gate, target, name):
    s, k = a.shape
    dm = x.shape[1]
    tm = _tile(s, ROW_TILE)

    def body(a_ref, w_ref, x_ref, g_ref, t_ref, sq_ref, do_ref, dy_ref, dg_ref):
        @pl.when(pl.program_id(0) == 0)
        def _():
            sq_ref[...] = jnp.zeros_like(sq_ref)
            dg_ref[...] = jnp.zeros_like(dg_ref)

        y, gv = _dot(a_ref[...], w_ref[0]), g_ref[...]
        err = x_ref[...] + gv * y - t_ref[...]
        sq_ref[...] += _colsum8(err * err)
        dout = err * (1.0 / dm)
        do_ref[...] = dout
        dy_ref[0] = (dout * gv).astype(BF16)
        dg_ref[...] += _colsum8(dout * y)

    row = pl.BlockSpec((tm, dm), lambda i: (i, 0))
    acc = _full((SUBLANES, dm))
    return pl.pallas_call(
        body, name=name, grid=(s // tm,),
        in_specs=[pl.BlockSpec((tm, k), lambda i: (i, 0)), _full(w.shape), row, _full((1, dm)), row],
        out_specs=[acc, row, pl.BlockSpec((1, tm, dm), lambda i: (0, i, 0)), acc],
        out_shape=[jax.ShapeDtypeStruct((SUBLANES, dm), F32), jax.ShapeDtypeStruct((s, dm), F32),
                   jax.ShapeDtypeStruct((1, s, dm), BF16), jax.ShapeDtypeStruct((SUBLANES, dm), F32)],
    )(a, w, x, gate, target)


def _ffn_inner(h, w_up, conv_w, conv_b, tag):
    s, dm = h.shape
    half = w_up.shape[2]
    f = 2 * half
    tm = _tile(s, FFN_ROWS)

    def body(h_ref, w_ref, cw_ref, cb_ref, u_ref, a_ref, carry):
        @pl.when(pl.program_id(0) == 0)
        def _():
            carry[...] = jnp.zeros_like(carry)

        hv = h_ref[...]
        for j in range(2):
            cols = slice(j * half, (j + 1) * half)
            conv = []
            for g in range(2):
                ub = _dot(hv, w_ref[2 * g + j]).astype(BF16)
                u_ref[g, :, cols] = ub
                uf = ub.astype(F32)
                e = jnp.concatenate([carry[g, j], uf], axis=0)
                carry[g, j] = uf[tm - SUBLANES:tm]
                conv.append(_conv_taps(e, cw_ref[g][:, cols], cb_ref[g][:, cols])[SUBLANES:])
            a_ref[:, cols] = (conv[0] * _sig(conv[0]) * conv[1]).astype(BF16)

    return pl.pallas_call(
        body, name=tag + "_up_convglu", grid=(s // tm,),
        in_specs=[pl.BlockSpec((tm, dm), lambda i: (i, 0)), _full(w_up.shape), _full(conv_w.shape), _full(conv_b.shape)],
        out_specs=[pl.BlockSpec((2, tm, f), lambda i: (0, i, 0)), pl.BlockSpec((tm, f), lambda i: (i, 0))],
        out_shape=[jax.ShapeDtypeStruct((2, s, f), BF16), jax.ShapeDtypeStruct((s, f), BF16)],
        scratch_shapes=[pltpu.VMEM((2, 2, SUBLANES, half), F32)],
    )(h, w_up, conv_w, conv_b)


def _weight_grad_first(a, d, p_n, name):
    return lax.optimization_barrier((_mm_tn(a, d, p_n, name), d))


def _ffn_backward(dx_out, dffn, x_mid, scale, saved, w_up, conv_w, conv_b, w_down, mixer, tag):
    h, u, a = saved
    dw_down, dffn = _weight_grad_first(a, dffn, 1, tag + "_down_dw")
    du, dconv = _convglu_bwd(u, dffn, w_down, conv_w, conv_b, tag + "_convglu_bwd")
    dw_up, du = _weight_grad_first(h, du, N_CHIPS, tag + "_up_dw")
    dx_mid, [(dshift, dscale)], dy, dgate_mixer = _premix_bwd(x_mid, [(scale, [(du, w_up)])], dx_out,
                                                              tag + "_premix_bwd", branch=mixer)
    return dx_mid, dy, dgate_mixer, dw_up, dw_down, dict(shift=dshift, scale=dscale, conv=dconv)


def _local_step(x, target, mods, lb, vecs, weights_at):
    m0, m1, mk = mods["l0"], mods["l1"], mods["kv"]
    h0 = _premix(x, m0[0], m0[1], "l0_premix")
    wts, h0 = weights_at("mixer0", h0)
    proj = _mm_nn(h0, wts["a_w_in"], 1, F32, "l0_in")[0]
    o_a, yp, states = _hgrn_fwd(proj, lb, vecs["a_norm_g"], "l0_hgrn")
    more, yp = weights_at("ffn0", yp)
    wts.update(more)
    y0, x1, [hf0] = _mm_residual_premix(yp, wts["a_w_out"], x, m0[2], [(m0[3], m0[4])], "l0_out")
    u0, a0 = _ffn_inner(hf0, wts["up0"], vecs["conv_w0"], vecs["conv_b0"], "l0_ffn")
    saved0 = (hf0, u0, a0)
    ffn0, x2, [hk, h1] = _mm_residual_premix(a0, wts["down0"], x1, m0[5], [(mk[0], mk[1]), (m1[0], m1[1])],
                                             "l0_ffn_down")
    more, hk = weights_at("layer1", hk)
    wts.update(more)
    k_raw = _mm_nn(hk, wts["kv_k"], 1, F32, "kv_k")[0]
    v_sh = _mm_nn(hk, wts["kv_v"], 1, BF16, "kv_v")[0]
    f_raw = _mm_nn(hk, wts["kv_f"], 1, F32, "kv_f")[0]
    k_sh = _headnorm(k_raw, vecs["k_norm_g"], 1.0, "kv_knorm")
    qa, ka = _fcum_fwd(f_raw, vecs["kv_b_f"], "kv_fcum")
    qo = _mm_nn(h1, wts["b_w_q"], 1, F32, "l1_q")[0]
    q_scale = HEAD_DIM ** -0.5
    q = _headnorm(qo, vecs["q_norm_g"], q_scale * LOG2E, "l1_qnorm")
    o_b, og, qab = _fox_fwd(q, qa, k_sh, ka, v_sh, qo, "l1_fox")
    y1, x3, [hf1] = _mm_residual_premix(og, wts["b_w_out"], x2, m1[2], [(m1[3], m1[4])], "l1_out")
    u1, a1 = _ffn_inner(hf1, wts["up1"], vecs["conv_w1"], vecs["conv_b1"], "l1_ffn")
    saved1 = (hf1, u1, a1)
    sq, dx4, dffn1, dg2_1 = _mm_loss_head(a1, wts["down1"], x3, m1[5], target, "l1_ffn_down")

    big, small = {}, {}
    dx3, dy1, dg1_1, big["up1"], big["down1"], s_ffn1 = _ffn_backward(
        dx4, dffn1, x3, m1[4], saved1, wts["up1"], vecs["conv_w1"], vecs["conv_b1"], wts["down1"], (y1, m1[2]), "l1_ffn")
    big["b_w_out"], dy1 = _weight_grad_first(og, dy1, 1, "l1_out_dw")
    d_og = _mm_nt(dy1, wts["b_w_out"], F32, "l1_out_dx")
    do_b, dgate_b, doa = _fox_gate_bwd(o_b, qo, d_og, "l1_fox_gate_bwd")
    dk, dv, dka, dq = _fox_bwd(q, qab, k_sh, ka, v_sh, do_b, doa, "l1_fox_bwd")
    dqo, dqg = _headnorm_bwd(qo, vecs["q_norm_g"], q_scale, dq, "l1_qnorm_bwd", extra=dgate_b)
    big["b_w_q"], dqo = _weight_grad_first(h1, dqo, N_CHIPS, "l1_q_dw")
    dk_raw, dkg = _headnorm_bwd(k_raw, vecs["k_norm_g"], 1.0, dk, "kv_knorm_bwd")
    dz, dbf = _fcum_bwd(f_raw, vecs["kv_b_f"], dka, dq, "kv_fcum_bwd")
    big["kv_k"], dk_raw = _weight_grad_first(hk, dk_raw, 1, "kv_k_dw")
    big["kv_v"], dv = _weight_grad_first(hk, dv, 1, "kv_v_dw")
    big["kv_f"], dz = _weight_grad_first(hk, dz, 1, "kv_f_dw")
    kv_pairs = [(dk_raw, wts["kv_k"]), (dv, wts["kv_v"]), (dz, wts["kv_f"])]
    dx2, [(dsh1_1, dsc1_1), (dshk, dsck)], dffn0, dg2_0 = _premix_bwd(
        x2, [(m1[1], [(dqo, wts["b_w_q"])]), (mk[1], kv_pairs)], dx3, "l1_kv_premix_bwd", branch=(ffn0, m0[5]))
    dx1, dy0, dg1_0, big["up0"], big["down0"], s_ffn0 = _ffn_backward(
        dx2, dffn0, x1, m0[4], saved0, wts["up0"], vecs["conv_w0"], vecs["conv_b0"], wts["down0"], (y0, m0[2]), "l0_ffn")
    big["a_w_out"], dy0 = _weight_grad_first(yp, dy0, 1, "l0_out_dw")
    dyp = _mm_nt(dy0, wts["a_w_out"], F32, "l0_out_dx")
    dproj, dlb, dng = _hgrn_bwd(proj, lb, vecs["a_norm_g"], o_a, states, dyp, "l0_hgrn_bwd")
    grad_x, [(dsh1_0, dsc1_0)] = _premix_bwd(x, [(m0[1], [(dproj, wts["a_w_in"])])], dx1, "l0_premix_bwd")
    dproj, _ = lax.optimization_barrier((dproj, (dsh1_0, dsc1_0)))
    big["a_w_in"] = _mm_tn(h0, dproj, N_CHIPS, "l0_in_dw")

    small["mod_l0"] = [dsh1_0, dsc1_0, dg1_0, s_ffn0["shift"], s_ffn0["scale"], dg2_0]
    small["mod_l1"] = [dsh1_1, dsc1_1, dg1_1, s_ffn1["shift"], s_ffn1["scale"], dg2_1]
    small["mod_kv"] = [dshk, dsck]
    small["conv0"], small["conv1"] = s_ffn0["conv"], s_ffn1["conv"]
    small["a_norm_g"], small["k_norm_g"], small["q_norm_g"] = dng, dkg, dqg
    small["kv_b_f"], small["lb"] = dbf, dlb
    marks = {"attention_bwd": dk, "ffn0_bwd": dx1, "mixer0_bwd": grad_x}
    return sq, grad_x, big, small, marks


HBM = pl.BlockSpec(memory_space=pltpu.HBM)
COMM_CHUNK_ELEMS = 256 * 1024


def _place():
    x, y, c = lax.axis_index("x"), lax.axis_index("y"), lax.axis_index("c")
    chips = [(1 - x, y), (x, 1 - y), (1 - x, 1 - y)]
    return x, y, c, (x, y, 1 - c), chips


def _chunk_rows(rows, cols):
    best = BF16_ROWS
    for r in range(BF16_ROWS, rows + 1, BF16_ROWS):
        if rows % r == 0 and r * cols <= COMM_CHUNK_ELEMS:
            best = r
    assert rows % best == 0, (rows, cols)
    return best


def _allgather8(block, name):
    m_per, n = block.shape

    def body(x_ref, out_ref, send_sems, recv_sems, local_sem):
        x, y, c, sibling, chips = _place()
        me = (x, y, c)

        def rows(px, py, pc):
            return out_ref.at[pl.ds((4 * px + 2 * py + pc) * m_per, m_per), :]

        def copy(k, blk, to, src=None):
            return pltpu.make_async_remote_copy(
                src_ref=rows(*blk) if src is None else src, dst_ref=rows(*blk),
                send_sem=send_sems.at[k], recv_sem=recv_sems.at[k], device_id=to, device_id_type=MESH)

        mine = pltpu.make_async_copy(x_ref, rows(*me), local_sem)
        mine.start()
        first = [copy(0, me, sibling, src=x_ref)]
        first += [copy(1 + j, me, (*chip, c), src=x_ref) for j, chip in enumerate(chips)]
        for cp in first:
            cp.start()
        passed = [copy(4 + j, (*chip, c), sibling) for j, chip in enumerate(chips)]
        for j, chip in enumerate(chips):
            copy(1 + j, (*chip, c), me).wait_recv()
            passed[j].start()
        copy(0, sibling, me).wait_recv()
        for j, chip in enumerate(chips):
            copy(4 + j, (*chip, 1 - c), me).wait_recv()
        for cp in first + passed:
            cp.wait_send()
        mine.wait()

    return pl.pallas_call(
        body, name=name, out_shape=jax.ShapeDtypeStruct((N_DEV * m_per, n), block.dtype),
        in_specs=[pl.BlockSpec(memory_space=pltpu.VMEM)], out_specs=pl.BlockSpec(memory_space=pltpu.VMEM),
        scratch_shapes=[pltpu.SemaphoreType.DMA((7,)), pltpu.SemaphoreType.DMA((7,)), pltpu.SemaphoreType.DMA],
    )(block)


def _cast_own_block(shards, layer, chip, name):
    _, r, cols = shards.shape
    rows = _chunk_rows(r, cols)

    def body(chip_ref, w_ref, o_ref):
        o_ref[...] = w_ref[...].astype(BF16)

    return pl.pallas_call(
        body, name=name,
        grid_spec=pltpu.PrefetchScalarGridSpec(
            num_scalar_prefetch=1, grid=(r // rows,),
            in_specs=[pl.BlockSpec((None, rows, cols), lambda i, chip_ref: (layer, i, 0))],
            out_specs=pl.BlockSpec((None, rows, cols), lambda i, chip_ref: (chip_ref[0], i, 0))),
        out_shape=jax.ShapeDtypeStruct((N_CHIPS, r, cols), BF16),
    )(chip, shards)


def _sequencer_gather(bufs, name, collective_id):
    n_t = len(bufs)
    dims = [b.shape[1:] for b in bufs]
    refs = [jax.new_ref(b, memory_space=pltpu.MemorySpace.HBM) for b in bufs]

    @pl.kernel(mesh=plsc.ScalarSubcoreMesh(axis_name="sequencer", num_cores=1), name=name,
               scratch_types=[pltpu.SemaphoreType.DMA((n_t,))] * 4,
               compiler_params=pltpu.CompilerParams(collective_id=collective_id))
    def launch(send_ici, recv_ici, send_d2d, recv_d2d):
        x, y, c, sibling, chips = _place()
        p_me = 2 * x + y
        peers = [sibling] + [(cx, cy, c) for cx, cy in chips]
        barrier = pltpu.get_barrier_semaphore()
        for peer in peers:
            pl.semaphore_signal(barrier, inc=1, device_id=peer, device_id_type=MESH)
        pl.semaphore_wait(barrier, len(peers))

        def waiter(t, sem_s, sem_r):
            win = refs[t].at[pl.ds(0, 3), pl.ds(0, dims[t][0] // 2), :]
            return pltpu.make_async_remote_copy(src_ref=win, dst_ref=win, send_sem=sem_s.at[t], recv_sem=sem_r.at[t],
                                                device_id=sibling, device_id_type=MESH)

        def half_copy(t, chip_idx, to, sem_s, sem_r):
            r2 = dims[t][0] // 2
            win = refs[t].at[chip_idx, pl.ds(c * r2, r2), :]
            return pltpu.make_async_remote_copy(src_ref=win, dst_ref=win, send_sem=sem_s.at[t], recv_sem=sem_r.at[t],
                                                device_id=to, device_id_type=MESH)

        for t in range(n_t):
            for cx, cy in chips:
                half_copy(t, p_me, (cx, cy, c), send_ici, recv_ici).start()
        for t in range(n_t):
            waiter(t, send_ici, recv_ici).wait_recv()
            for cx, cy in chips:
                half_copy(t, 2 * cx + cy, sibling, send_d2d, recv_d2d).start()
        for t in range(n_t):
            waiter(t, send_d2d, recv_d2d).wait_recv()
            waiter(t, send_ici, recv_ici).wait_send()
            waiter(t, send_d2d, recv_d2d).wait_send()

    launch()
    return [r[...] for r in refs]


def _sequencer_allgather8(block, dev, name, collective_id):
    m_per, n = block.shape
    src = jax.new_ref(block, memory_space=pltpu.MemorySpace.HBM)
    out = jax.empty_ref(jax.ShapeDtypeStruct((N_DEV * m_per, n), block.dtype), memory_space=pltpu.MemorySpace.HBM)

    @pl.kernel(mesh=plsc.ScalarSubcoreMesh(axis_name="sequencer", num_cores=1), name=name,
               scratch_types=[pltpu.SemaphoreType.DMA((7,))] * 2,
               compiler_params=pltpu.CompilerParams(collective_id=collective_id))
    def launch(send_sems, recv_sems):
        x, y, c, sibling, chips = _place()
        me = (x, y, c)
        _handshake([sibling] + [(cx, cy, c) for cx, cy in chips])

        def rows(px, py, pc):
            return out.at[pl.ds((4 * px + 2 * py + pc) * m_per, m_per), :]

        def copy(k, blk, to, from_src=False):
            return pltpu.make_async_remote_copy(
                src_ref=src if from_src else rows(*blk), dst_ref=rows(*blk),
                send_sem=send_sems.at[k], recv_sem=recv_sems.at[k], device_id=to, device_id_type=MESH)

        first = [copy(0, me, sibling, True)] + [copy(1 + j, me, (*chip, c), True) for j, chip in enumerate(chips)]
        for cp in first:
            cp.start()
        passed = [copy(4 + j, (*chip, c), sibling) for j, chip in enumerate(chips)]
        for j, chip in enumerate(chips):
            copy(1 + j, (*chip, c), me).wait_recv()
            passed[j].start()
        copy(0, sibling, me).wait_recv()
        for j, chip in enumerate(chips):
            copy(4 + j, (*chip, 1 - c), me).wait_recv()
        for cp in first + passed:
            cp.wait_send()

    launch()
    return lax.dynamic_update_slice(out[...], block, (dev * m_per, 0))


def _others():
    x, y, c = lax.axis_index("x"), lax.axis_index("y"), lax.axis_index("c")
    flip = lambda v, f: 1 - v if f else v
    return [(flip(x, fx), flip(y, fy), flip(c, fc))
            for fx in (0, 1) for fy in (0, 1) for fc in (0, 1) if (fx, fy, fc) != (0, 0, 0)]


def _handshake(peers):
    barrier = pltpu.get_barrier_semaphore()
    for peer in peers:
        pl.semaphore_signal(barrier, inc=1, device_id=peer, device_id_type=MESH)
    pl.semaphore_wait(barrier, len(peers))


def _sequencer_scatter(parts, name, collective_id):
    n_t = len(parts)
    dims = [p.shape[1:] for p in parts]
    srcs = [jax.new_ref(p, memory_space=pltpu.MemorySpace.HBM) for p in parts]
    inboxes = [jax.empty_ref(jax.ShapeDtypeStruct((N_DEV, r // 2, cols), BF16), memory_space=pltpu.MemorySpace.HBM)
               for r, cols in dims]

    @pl.kernel(mesh=plsc.ScalarSubcoreMesh(axis_name="sequencer", num_cores=1), name=name,
               scratch_types=[pltpu.SemaphoreType.DMA((n_t,))] * 2,
               compiler_params=pltpu.CompilerParams(collective_id=collective_id))
    def launch(send_sem, recv_sem):
        x, y, c = lax.axis_index("x"), lax.axis_index("y"), lax.axis_index("c")
        me = 4 * x + 2 * y + c
        peers = _others()
        _handshake(peers)
        for t in range(n_t):
            h = dims[t][0] // 2
            for qx, qy, qc in peers:
                pltpu.make_async_remote_copy(
                    src_ref=srcs[t].at[2 * qx + qy, pl.ds(qc * h, h), :], dst_ref=inboxes[t].at[me],
                    send_sem=send_sem.at[t], recv_sem=recv_sem.at[t], device_id=(qx, qy, qc), device_id_type=MESH).start()
        for t in range(n_t):
            win = inboxes[t].at[pl.ds(0, N_DEV - 1)]
            both = pltpu.make_async_remote_copy(src_ref=win, dst_ref=win, send_sem=send_sem.at[t],
                                                recv_sem=recv_sem.at[t], device_id=peers[0], device_id_type=MESH)
            both.wait_recv()
            both.wait_send()

    launch()
    return [b[...] for b in inboxes]


def _sum_pieces(part, inbox, place, name):
    _, r, cols = part.shape
    h = r // 2
    rows = _chunk_rows(h, cols)
    steps = h // rows

    def body(place_ref, own_ref, in_ref, o_ref):
        dev = place_ref[2]
        own = own_ref[...].astype(F32)
        acc = jnp.zeros((rows, cols), F32)
        for d in range(N_DEV):
            acc = acc + jnp.where(dev == d, own, in_ref[d].astype(F32))
        o_ref[...] = acc

    return pl.pallas_call(
        body, name=name,
        grid_spec=pltpu.PrefetchScalarGridSpec(
            num_scalar_prefetch=1, grid=(steps,),
            in_specs=[pl.BlockSpec((None, rows, cols), lambda i, pr: (pr[0], pr[1] * steps + i, 0)),
                      pl.BlockSpec((N_DEV, rows, cols), lambda i, pr: (0, i, 0))],
            out_specs=pl.BlockSpec((rows, cols), lambda i, pr: (pr[1] * steps + i, 0))),
        out_shape=jax.ShapeDtypeStruct((r, cols), F32),
    )(place, part, inbox)


def _sequencer_swap_halves(halves, name, collective_id):
    n_t = len(halves)
    refs = [jax.new_ref(a, memory_space=pltpu.MemorySpace.HBM) for a in halves]

    @pl.kernel(mesh=plsc.ScalarSubcoreMesh(axis_name="sequencer", num_cores=1), name=name,
               scratch_types=[pltpu.SemaphoreType.DMA((n_t,))] * 2,
               compiler_params=pltpu.CompilerParams(collective_id=collective_id))
    def launch(send_sem, recv_sem):
        x, y, c = lax.axis_index("x"), lax.axis_index("y"), lax.axis_index("c")
        sibling = (x, y, 1 - c)
        _handshake([sibling])
        copies = []
        for t in range(n_t):
            h = halves[t].shape[0] // 2
            win = refs[t].at[pl.ds(c * h, h), :]
            copies.append(pltpu.make_async_remote_copy(src_ref=win, dst_ref=win, send_sem=send_sem.at[t],
                                                       recv_sem=recv_sem.at[t], device_id=sibling, device_id_type=MESH))
            copies[-1].start()
        for cp in copies:
            cp.wait()

    launch()
    return [r[...] for r in refs]


def _cond_rows(c16, w, act, name):
    n_l, dm, wid = w.shape

    def body(c_ref, w_ref, o_ref, a_ref):
        cv = c_ref[...]
        if act:
            cv = cv * _sig(cv)
        a_ref[...] = cv
        o_ref[...] = _dot_f32(cv, w_ref[...])

    return pl.pallas_call(
        body, name=name, grid=(n_l,),
        in_specs=[_full((16, dm)), pl.BlockSpec((None, dm, wid), lambda l: (l, 0, 0))],
        out_specs=[pl.BlockSpec((None, 16, wid), lambda l: (l, 0, 0)), _full((16, dm))],
        out_shape=[jax.ShapeDtypeStruct((n_l, 16, wid), F32), jax.ShapeDtypeStruct((16, dm), F32)],
    )(c16, w)


def _outer_grad(ct, dm, name):
    n_l, kk, wid = dm.shape
    d_rows = ct.shape[0]

    def body(c_ref, d_ref, o_ref):
        o_ref[...] = _dot_f32(c_ref[...], d_ref[...])

    return pl.pallas_call(
        body, name=name, grid=(n_l,),
        in_specs=[_full((d_rows, kk)), pl.BlockSpec((None, kk, wid), lambda l: (l, 0, 0))],
        out_specs=pl.BlockSpec((None, d_rows, wid), lambda l: (l, 0, 0)),
        out_shape=jax.ShapeDtypeStruct((n_l, d_rows, wid), F32),
    )(ct, dm)


def _sum_devices(g, name):
    rows, n = g.shape

    def body(g_ref, o_ref):
        acc = g_ref[0:SUBLANES, :]
        for dev in range(1, N_DEV):
            acc = acc + g_ref[dev * SUBLANES:(dev + 1) * SUBLANES, :]
        o_ref[...] = acc

    return pl.pallas_call(body, name=name, out_shape=jax.ShapeDtypeStruct((SUBLANES, n), F32))(g)


def _adamw(w, g, m, v, name):
    shape = w.shape
    cols = shape[-1]
    rows = w.size // cols
    tr = rows
    for cand in range(SUBLANES, min(rows, 256) + 1, SUBLANES):
        if rows % cand == 0:
            tr = cand
    if rows * cols <= COMM_CHUNK_ELEMS:
        tr = rows
    c1 = 1.0 / (1.0 - ADAM_B1 ** ADAM_STEP)
    c2 = 1.0 / (1.0 - ADAM_B2 ** ADAM_STEP)

    def body(w_ref, g_ref, m_ref, v_ref, d_ref, mo_ref, vo_ref):
        gv = g_ref[...]
        m_new = ADAM_B1 * m_ref[...] + (1.0 - ADAM_B1) * gv
        v_new = ADAM_B2 * v_ref[...] + (1.0 - ADAM_B2) * (gv * gv)
        mo_ref[...] = m_new
        vo_ref[...] = v_new
        d_ref[...] = -ADAM_LR * ((m_new * c1) / (jnp.sqrt(v_new * c2) + ADAM_EPS) + ADAM_WD * w_ref[...])

    spec = pl.BlockSpec((tr, cols), lambda i: (i, 0))
    outs = pl.pallas_call(
        body, name=name, grid=(rows // tr,), in_specs=[spec] * 4, out_specs=[spec] * 3,
        out_shape=[jax.ShapeDtypeStruct((rows, cols), F32)] * 3,
    )(*[a.reshape(rows, cols) for a in (w, g, m, v)])
    return tuple(o.reshape(shape) for o in outs)


def _pad_cols(a, cols):
    return jnp.pad(a, [(0, 0)] * (a.ndim - 1) + [(0, cols - a.shape[-1])])


def _flat8(parts, width):
    v = jnp.concatenate([p.reshape(-1) for p in parts])
    return jnp.pad(v, (0, width - v.shape[0])).reshape(SUBLANES, width // SUBLANES)


KV_SHARD = 514
KV_SHARD_PAD = 640
BIG = ("a_w_in", "a_w_out", "kv_w", "b_w_q", "b_w_out", "up0", "up1", "down0", "down1")


def kernel(x, c, ada_w, ada_b, a_w_in, a_lb_logits, a_norm_g, a_w_out, kv_ada_w, kv_ada_b, kv_w, kv_b_f, k_norm_g, b_w_q, q_norm_g, b_w_out, ffn_w_up, ffn_conv_w, ffn_conv_b, ffn_w_down, loss_target, m_ada_w, m_ada_b, m_a_w_in, m_a_lb_logits, m_a_norm_g, m_a_w_out, m_kv_ada_w, m_kv_ada_b, m_kv_w, m_kv_b_f, m_k_norm_g, m_b_w_q, m_q_norm_g, m_b_w_out, m_ffn_w_up, m_ffn_conv_w, m_ffn_conv_b, m_ffn_w_down, v_ada_w, v_ada_b, v_a_w_in, v_a_lb_logits, v_a_norm_g, v_a_w_out, v_kv_ada_w, v_kv_ada_b, v_kv_w, v_kv_b_f, v_k_norm_g, v_b_w_q, v_q_norm_g, v_b_w_out, v_ffn_w_up, v_ffn_conv_w, v_ffn_conv_b, v_ffn_w_down):
    dm, ff = D_MODEL, D_FF
    ix, iy, ic = lax.axis_index("x"), lax.axis_index("y"), lax.axis_index("c")
    chip = 2 * ix + iy
    dev = 2 * chip + ic

    w1 = 10240
    g1 = _allgather8(_flat8([c, a_lb_logits, ffn_conv_w], w1), "gather_cond").reshape(N_DEV, w1)
    c_all = g1[:, :dm]
    per_chip = g1[0::2]
    lb_logits = per_chip[:, dm:dm + 512].reshape(N_CHIPS, 2, 256).transpose(1, 0, 2).reshape(2, dm)
    conv_w = per_chip[:, dm + 512:dm + 512 + 2 * CONV_W * FFN_COLS].reshape(N_CHIPS, 2, CONV_W, FFN_COLS)
    conv_w = conv_w.transpose(1, 2, 0, 3).reshape(2, CONV_W, 2, ff).transpose(0, 2, 1, 3)
    conv_b = ffn_conv_b.reshape(2, 2, 1, ff)
    lb = jax.nn.softmax(lb_logits, axis=0)[0:1]

    c16 = jnp.pad(c_all, ((0, 8), (0, 0)))
    mod_ada, c_act16 = _cond_rows(c16, ada_w, True, "mod_ada")
    mod_kv, _ = _cond_rows(c16, kv_ada_w[None], True, "mod_kv")
    mine = jnp.concatenate([mod_ada[0, :8], mod_ada[1, :8], mod_kv[0, :8]], axis=1)
    w2 = mine.shape[1]
    g2 = _allgather8(mine, "gather_mod").reshape(N_DEV, 8, w2)[0::2]
    my_rows = lax.dynamic_index_in_dim(g2, dev, axis=1, keepdims=False)
    mod0 = my_rows[:, 0:1536].reshape(6 * dm) + ada_b[0]
    mod1 = my_rows[:, 1536:3072].reshape(6 * dm) + ada_b[1]
    modk = my_rows[:, 3072:3584].reshape(2 * dm) + kv_ada_b
    mods = {"l0": [v.reshape(1, dm) for v in jnp.split(mod0, 6)],
            "l1": [v.reshape(1, dm) for v in jnp.split(mod1, 6)],
            "kv": [v.reshape(1, dm) for v in jnp.split(modk, 2)]}

    local = [(a_w_in, 0), (a_w_out, 0), (_pad_cols(kv_w, KV_SHARD_PAD)[None], 0), (b_w_q, 0), (b_w_out, 0),
             (ffn_w_up, 0), (ffn_w_up, 1), (ffn_w_down, 0), (ffn_w_down, 1)]
    chip_arr = chip.reshape(1).astype(jnp.int32)
    own = {n: _cast_own_block(w, layer, chip_arr, "cast_" + n) for n, (w, layer) in zip(BIG, local)}
    stages = {"mixer0": ("a_w_in",), "ffn0": ("a_w_out", "up0", "down0"),
              "layer1": ("kv_w", "b_w_q", "b_w_out", "up1", "down1")}
    arriving = {st: _sequencer_gather([own[n] for n in names], "gather_" + st, cid)
                for cid, (st, names) in enumerate(stages.items(), start=1)}
    rowwise = lambda g: g.reshape(1, -1, dm)

    def weights_at(stage, token):
        got, token = lax.optimization_barrier((arriving[stage], token))
        g = dict(zip(stages[stage], got))
        if stage == "mixer0":
            return {"a_w_in": g["a_w_in"]}, token
        if stage == "ffn0":
            return {"a_w_out": rowwise(g["a_w_out"]), "up0": g["up0"], "down0": rowwise(g["down0"])}, token
        kv_full = g["kv_w"][:, :, :KV_SHARD].transpose(1, 0, 2).reshape(dm, N_CHIPS * KV_SHARD)
        return {"kv_k": kv_full[None, :, :dm], "kv_v": kv_full[None, :, dm:2 * dm],
                "kv_f": _pad_cols(kv_full[None, :, 2 * dm:], LANES), "b_w_q": g["b_w_q"],
                "b_w_out": rowwise(g["b_w_out"]), "up1": g["up1"], "down1": rowwise(g["down1"])}, token

    vecs = {"a_norm_g": jnp.tile(a_norm_g, (1, HEADS)), "k_norm_g": jnp.tile(k_norm_g[None], (1, HEADS)),
            "q_norm_g": jnp.tile(q_norm_g, (1, HEADS)), "kv_b_f": _pad_cols(kv_b_f[None], LANES),
            "conv_w0": conv_w[0], "conv_b0": conv_b[0], "conv_w1": conv_w[1], "conv_b1": conv_b[1]}

    sq, grad_x, big, small, marks = _local_step(x[0], loss_target[0], mods, lb, vecs, weights_at)
    loss = lax.psum(0.5 * jnp.sum(sq) / dm, ("x", "y", "c"))

    kv_grad = jnp.concatenate([big["kv_k"][0], big["kv_v"][0], big["kv_f"][0][:, :HEADS]], axis=1)
    kv_grad = _pad_cols(kv_grad.reshape(dm, N_CHIPS, KV_SHARD).transpose(1, 0, 2), KV_SHARD_PAD)
    chipwise = lambda g: g.reshape(N_CHIPS, -1, dm)
    parts = dict(zip(BIG, [big["a_w_in"], chipwise(big["a_w_out"]), kv_grad, big["b_w_q"], chipwise(big["b_w_out"]),
                           big["up0"], big["up1"], chipwise(big["down0"]), chipwise(big["down1"])]))
    place = jnp.stack([chip, ic, dev]).astype(jnp.int32)

    served = []
    boxes = {}

    groups = (("up1", "down1"), ("b_w_out", "b_w_q", "kv_w"), ("up0", "down0", "a_w_out"), ("a_w_in",))

    def scatter_group(k):
        mine = [parts[n] for n in groups[k]]
        if served:
            mine, _ = lax.optimization_barrier((mine, served[-1]))
        boxes[k] = _sequencer_scatter(mine, "scatter_grads_%d" % k, 4 + k)
        served.append(boxes[k])

    def sum_group(k, token):
        inboxes, _ = lax.optimization_barrier((boxes[k], token))
        return [_sum_pieces(parts[n], box, place, "sum_" + n) for n, box in zip(groups[k], inboxes)]

    def swap_group(k, halves, behind):
        halves, _ = lax.optimization_barrier((halves, behind))
        return dict(zip(groups[k], _sequencer_swap_halves(halves, "swap_grads_%d" % k, 8 + k)))

    for k in range(3):
        scatter_group(k)
    halves = [sum_group(0, marks["attention_bwd"]), sum_group(1, marks["ffn0_bwd"]), sum_group(2, marks["mixer0_bwd"])]

    fold = lambda a: a.sum(axis=0)
    heads = lambda a: fold(a).reshape(HEADS, HEAD_DIM).sum(axis=0)
    conv_flat = lambda a: a.sum(axis=2).transpose(1, 0, 2)
    pieces = ([fold(a) for a in small["mod_l0"]] + [fold(a) for a in small["mod_l1"]] + [fold(a) for a in small["mod_kv"]]
              + [conv_flat(small["conv0"]), conv_flat(small["conv1"]), heads(small["a_norm_g"]), heads(small["k_norm_g"]),
                 heads(small["q_norm_g"]), fold(small["kv_b_f"]), fold(small["lb"])])
    w3 = 61440
    small_vec, _ = lax.optimization_barrier((_flat8(pieces, w3), served[2]))
    g3 = _sequencer_allgather8(small_vec, dev, "gather_small", 12)
    served.append(g3)
    scatter_group(3)
    rs = {}
    for k in range(3):
        rs.update(swap_group(k, halves[k], served[-1]))
    tot = _sum_devices(g3, "sum_small").reshape(w3)
    n_mod = 14 * dm
    dmod_all = g3.reshape(N_DEV, w3)[:, :n_mod]
    o = n_mod
    conv_tot = [tot[o + l * 8 * ff: o + (l + 1) * 8 * ff].reshape(4, 2 * ff) for l in range(2)]
    o += 16 * ff
    g_a_norm, g_k_norm, g_q_norm = (tot[o + i * HEAD_DIM: o + (i + 1) * HEAD_DIM] for i in range(3))
    o += 3 * HEAD_DIM
    g_kv_b_f = tot[o:o + HEADS]
    dlb = tot[o + LANES:o + LANES + dm]

    ct = _pad_cols(c_act16[:8].T, LANES)
    dmod_pad = jnp.pad(dmod_all, ((0, LANES - N_DEV), (0, 0)))
    cols_ada = jnp.stack([lax.dynamic_slice_in_dim(dmod_pad, l * 6 * dm + chip * 1536, 1536, axis=1) for l in range(2)])
    cols_kv = lax.dynamic_slice_in_dim(dmod_pad, 12 * dm + chip * 512, 512, axis=1)[None]
    g_ada_w = _outer_grad(ct, cols_ada, "grad_ada_w")
    g_kv_ada_w = _outer_grad(ct, cols_kv, "grad_kv_ada_w")[0]

    my_lb = lax.dynamic_slice_in_dim(lb[0], chip * 256, 256)
    l0 = lax.dynamic_slice_in_dim(dlb, chip * 256, 256) * my_lb * (1.0 - my_lb)
    grads = {
        "ada_w": g_ada_w, "ada_b": jnp.stack([tot[:6 * dm], tot[6 * dm:12 * dm]]),
        "a_lb_logits": jnp.stack([l0, -l0]), "a_norm_g": g_a_norm[None],
        "a_w_out": rs["a_w_out"][None], "kv_ada_w": g_kv_ada_w, "kv_ada_b": tot[12 * dm:14 * dm],
        "kv_w": rs["kv_w"][:, :KV_SHARD], "kv_b_f": g_kv_b_f, "k_norm_g": g_k_norm,
        "b_w_q": rs["b_w_q"][None], "q_norm_g": g_q_norm[None], "b_w_out": rs["b_w_out"][None],
        "ffn_w_up": jnp.stack([rs["up0"], rs["up1"]]),
        "ffn_conv_w": jnp.stack([lax.dynamic_slice_in_dim(ct_l[:CONV_W], chip * FFN_COLS, FFN_COLS, axis=1) for ct_l in conv_tot]),
        "ffn_conv_b": jnp.stack([ct_l[CONV_W] for ct_l in conv_tot]),
        "ffn_w_down": jnp.stack([rs["down0"], rs["down1"]]),
    }
    weights = dict(ada_w=ada_w, ada_b=ada_b, a_w_in=a_w_in, a_lb_logits=a_lb_logits, a_norm_g=a_norm_g, a_w_out=a_w_out,
                   kv_ada_w=kv_ada_w, kv_ada_b=kv_ada_b, kv_w=kv_w, kv_b_f=kv_b_f, k_norm_g=k_norm_g, b_w_q=b_w_q,
                   q_norm_g=q_norm_g, b_w_out=b_w_out, ffn_w_up=ffn_w_up, ffn_conv_w=ffn_conv_w, ffn_conv_b=ffn_conv_b,
                   ffn_w_down=ffn_w_down)
    m_in = dict(ada_w=m_ada_w, ada_b=m_ada_b, a_w_in=m_a_w_in, a_lb_logits=m_a_lb_logits, a_norm_g=m_a_norm_g,
                a_w_out=m_a_w_out, kv_ada_w=m_kv_ada_w, kv_ada_b=m_kv_ada_b, kv_w=m_kv_w, kv_b_f=m_kv_b_f,
                k_norm_g=m_k_norm_g, b_w_q=m_b_w_q, q_norm_g=m_q_norm_g, b_w_out=m_b_w_out, ffn_w_up=m_ffn_w_up,
                ffn_conv_w=m_ffn_conv_w, ffn_conv_b=m_ffn_conv_b, ffn_w_down=m_ffn_w_down)
    v_in = dict(ada_w=v_ada_w, ada_b=v_ada_b, a_w_in=v_a_w_in, a_lb_logits=v_a_lb_logits, a_norm_g=v_a_norm_g,
                a_w_out=v_a_w_out, kv_ada_w=v_kv_ada_w, kv_ada_b=v_kv_ada_b, kv_w=v_kv_w, kv_b_f=v_kv_b_f,
                k_norm_g=v_k_norm_g, b_w_q=v_b_w_q, q_norm_g=v_q_norm_g, b_w_out=v_b_w_out, ffn_w_up=v_ffn_w_up,
                ffn_conv_w=v_ffn_conv_w, ffn_conv_b=v_ffn_conv_b, ffn_w_down=v_ffn_w_down)

    names = list(weights)
    step = lambda n: _adamw(weights[n], grads[n], m_in[n], v_in[n], "adamw_" + n)
    grads = {n: g.reshape(weights[n].shape) for n, g in grads.items()}
    upd = {n: step(n) for n in names if n != "a_w_in"}
    last = sum_group(3, [u[0] for u in upd.values()])
    grads["a_w_in"] = swap_group(3, last, last)["a_w_in"][None]
    upd["a_w_in"] = step("a_w_in")
    return (loss, grad_x[None], *[grads[n] for n in names], *[upd[n][0] for n in names],
            *[upd[n][1] for n in names], *[upd[n][2] for n in names])
```

```python
import jax
import jax.numpy as jnp
from jax import lax
from jax.experimental import pallas as pl
from jax.experimental.pallas import tpu as pltpu
from jax.experimental.pallas import tpu_sc as plsc

F32 = jnp.float32
BF16 = jnp.bfloat16

D_MODEL = 1024
HEADS = 8
HEAD_DIM = 128
A_CHUNK = 64
D_FF = 2816
CONV_W = 3
EPS = 1e-6
NEG_INF = -1e30
N_CHIPS = 4
N_DEV = 8

ADAM_LR = 0.001
ADAM_B1 = 0.9
ADAM_B2 = 0.999
ADAM_EPS = 1e-08
ADAM_WD = 0.01
ADAM_STEP = 10

SUBLANES = 8
BF16_ROWS = 16
LANES = 128
HALO = BF16_ROWS
ROW_TILE = 512
TOKEN_TILE_TN = 2048
FFN_COLS = 1408
FFN_ROWS = 256
HGRN_ROWS = 256
ATT_TILE = 512
ATT_SPLIT = 2
ATT_FWD_HEADS = 8
ATT_BWD_HEADS = 4
MESH = pl.DeviceIdType.MESH


def _sig(x):
    return jax.nn.sigmoid(x)


def _dot(a, b):
    return jnp.dot(a, b, preferred_element_type=F32)


def _dot_nt(a, b):
    return lax.dot_general(a, b, (((1,), (1,)), ((), ())), preferred_element_type=F32)


def _dot_tn(a, b):
    return lax.dot_general(a, b, (((0,), (0,)), ((), ())), preferred_element_type=F32)


def _split2(x):
    hi = x.astype(BF16)
    lo = (x - hi.astype(F32)).astype(BF16)
    return hi, lo


def _dot_f32(a, b):
    ah, al = _split2(a)
    bh, bl = _split2(b)
    return _dot(ah, bh) + _dot(ah, bl) + _dot(al, bh)


def _tri_dot(tri, x):
    hi = x.astype(BF16)
    r = x - hi.astype(F32)
    mid = r.astype(BF16)
    lo = (r - mid.astype(F32)).astype(BF16)
    return _dot(tri, hi) + _dot(tri, mid) + _dot(tri, lo)


def _tri(n, upper=False):
    r = lax.broadcasted_iota(jnp.int32, (n, n), 0)
    c = lax.broadcasted_iota(jnp.int32, (n, n), 1)
    keep = (c >= r) if upper else (c <= r)
    return jnp.where(keep, 1.0, 0.0).astype(BF16)


def _colsum8(v):
    rows, n = v.shape
    return v.reshape(rows // SUBLANES, SUBLANES, n).sum(axis=0)


def _full(shape):
    nd = len(shape)
    return pl.BlockSpec(shape, lambda *_: (0,) * nd)


def _tile(n, want):
    t = min(n, want)
    assert n % t == 0, (n, t)
    return t


def _mm_nn(a, w, groups, out_dtype, name):
    m_rows, k = a.shape
    p_n, _, n = w.shape
    per = p_n // groups
    tm = _tile(m_rows, ROW_TILE)

    def body(a_ref, w_ref, o_ref):
        av = a_ref[...]
        for p in range(p_n):
            o_ref[p // per, :, (p % per) * n:(p % per + 1) * n] = _dot(av, w_ref[p]).astype(out_dtype)

    return pl.pallas_call(
        body, name=name, grid=(m_rows // tm,),
        in_specs=[pl.BlockSpec((tm, k), lambda i: (i, 0)), _full((p_n, k, n))],
        out_specs=pl.BlockSpec((groups, tm, per * n), lambda i: (0, i, 0)),
        out_shape=jax.ShapeDtypeStruct((groups, m_rows, per * n), out_dtype),
    )(a, w)


def _mm_nt(d, w, out_dtype, name):
    g_n, m_rows, _ = d.shape
    p_n, k, n = w.shape
    per = p_n // g_n
    tm = _tile(m_rows, ROW_TILE)

    def body(d_ref, w_ref, o_ref):
        acc = None
        for p in range(p_n):
            t = _dot_nt(d_ref[p // per, :, (p % per) * n:(p % per + 1) * n], w_ref[p])
            acc = t if acc is None else acc + t
        o_ref[...] = acc.astype(out_dtype)

    return pl.pallas_call(
        body, name=name, grid=(m_rows // tm,),
        in_specs=[pl.BlockSpec((g_n, tm, per * n), lambda i: (0, i, 0)), _full((p_n, k, n))],
        out_specs=pl.BlockSpec((tm, k), lambda i: (i, 0)),
        out_shape=jax.ShapeDtypeStruct((m_rows, k), out_dtype),
    )(d, w)


def _mm_tn(a, d, p_n, name):
    m_rows, k = a.shape
    g_n, _, w_cols = d.shape
    per = p_n // g_n
    n = w_cols // per
    tm = _tile(m_rows, TOKEN_TILE_TN if k <= D_MODEL else ROW_TILE)
    steps = m_rows // tm

    def body(a_ref, d_ref, o_ref, acc):
        m = pl.program_id(1)

        @pl.when(m == 0)
        def _():
            acc[...] = jnp.zeros_like(acc)

        acc[...] += _dot_tn(a_ref[...], d_ref[...])

        @pl.when(m == steps - 1)
        def _():
            o_ref[...] = acc[...].astype(BF16)

    return pl.pallas_call(
        body, name=name, grid=(p_n, steps),
        in_specs=[pl.BlockSpec((tm, k), lambda p, m: (m, 0)),
                  pl.BlockSpec((None, tm, n), lambda p, m: (p // per, m, p % per))],
        out_specs=pl.BlockSpec((None, k, n), lambda p, m: (p, 0, 0)),
        out_shape=jax.ShapeDtypeStruct((p_n, k, n), BF16),
        scratch_shapes=[pltpu.VMEM((k, n), F32)],
    )(a, d)


def _premix(x, shift, scale, name):
    s, dm = x.shape
    tm = _tile(s, ROW_TILE)

    def body(x_ref, sh_ref, sc_ref, h_ref):
        xv = x_ref[...]
        inv = lax.rsqrt(jnp.mean(xv * xv, axis=-1, keepdims=True) + EPS)
        h_ref[...] = (xv * inv * (1.0 + sc_ref[...]) + sh_ref[...]).astype(BF16)

    row = pl.BlockSpec((tm, dm), lambda i: (i, 0))
    vec = _full((1, dm))
    return pl.pallas_call(body, name=name, grid=(s // tm,), in_specs=[row, vec, vec], out_specs=row,
                          out_shape=jax.ShapeDtypeStruct((s, dm), BF16))(x, shift, scale)


def _premix_bwd(x, terms, dres, name, branch=None):
    s, dm = x.shape
    tm = _tile(s, ROW_TILE)
    pairs = [pr for _, prs in terms for pr in prs]
    n_in = 2 + len(terms) + 2 * len(pairs) + (2 if branch else 0)

    def body(*refs):
        x_ref, dres_ref = refs[:2]
        sc_refs = refs[2:2 + len(terms)]
        mm_refs = refs[2 + len(terms):2 + len(terms) + 2 * len(pairs)]
        outs = refs[n_in:]

        @pl.when(pl.program_id(0) == 0)
        def _():
            for o in outs[1:1 + 2 * len(terms)]:
                o[...] = jnp.zeros_like(o)
            if branch:
                outs[-1][...] = jnp.zeros_like(outs[-1])

        xv = x_ref[...]
        inv = lax.rsqrt(jnp.mean(xv * xv, axis=-1, keepdims=True) + EPS)
        r = xv * inv
        dx = dres_ref[...]
        k = 0
        for t, (_, prs) in enumerate(terms):
            dh = None
            for d, w in prs:
                d_ref, w_ref = mm_refs[2 * k], mm_refs[2 * k + 1]
                k += 1
                p_n, _, n = w.shape
                per = p_n // d.shape[0]
                for p in range(p_n):
                    part = _dot_nt(d_ref[p // per, :, (p % per) * n:(p % per + 1) * n], w_ref[p])
                    dh = part if dh is None else dh + part
            dr = dh * (1.0 + sc_refs[t][...])
            dx = dx + inv * (dr - r * jnp.mean(dr * r, axis=-1, keepdims=True))
            outs[1 + 2 * t][...] += _colsum8(dh)
            outs[2 + 2 * t][...] += _colsum8(dh * r)
        outs[0][...] = dx
        if branch:
            y_ref, g_ref = refs[n_in - 2:n_in]
            outs[-2][0] = (dx * g_ref[...]).astype(BF16)
            outs[-1][...] += _colsum8(dx * y_ref[...])

    row = pl.BlockSpec((tm, dm), lambda i: (i, 0))
    vec, acc = _full((1, dm)), _full((SUBLANES, dm))
    ins, specs = [x, dres] + [sc for sc, _ in terms], [row, row] + [vec] * len(terms)
    for d, w in pairs:
        ins += [d, w]
        specs += [pl.BlockSpec((d.shape[0], tm, d.shape[2]), lambda i: (0, i, 0)), _full(w.shape)]
    out_shape = [jax.ShapeDtypeStruct((s, dm), F32)] + [jax.ShapeDtypeStruct((SUBLANES, dm), F32)] * (2 * len(terms))
    out_specs = [row] + [acc] * (2 * len(terms))
    if branch:
        ins += list(branch)
        specs += [row, vec]
        out_shape += [jax.ShapeDtypeStruct((1, s, dm), BF16), jax.ShapeDtypeStruct((SUBLANES, dm), F32)]
        out_specs += [pl.BlockSpec((1, tm, dm), lambda i: (0, i, 0)), acc]
    outs = pl.pallas_call(body, name=name, grid=(s // tm,), in_specs=specs, out_specs=out_specs,
                          out_shape=out_shape)(*ins)
    partials = [(outs[1 + 2 * t], outs[2 + 2 * t]) for t in range(len(terms))]
    return (outs[0], partials) + ((outs[-2], outs[-1]) if branch else ())


def _conv_taps(e, w, b):
    return w[2:3] * e + w[1:2] * pltpu.roll(e, 1, 0) + w[0:1] * pltpu.roll(e, 2, 0) + b


def _ffn_specs(s, tm, cb):
    hb = tm // HALO
    last = s // HALO - 1
    main = pl.BlockSpec((2, tm, cb), lambda j, i: (0, i, j))
    prev = pl.BlockSpec((2, HALO, cb), lambda j, i: (0, jnp.maximum(i * hb - 1, 0), j))
    nxt = pl.BlockSpec((2, HALO, cb), lambda j, i: (0, jnp.minimum((i + 1) * hb, last), j))
    wspec = pl.BlockSpec((2, CONV_W, cb), lambda j, i: (0, 0, j))
    bspec = pl.BlockSpec((2, 1, cb), lambda j, i: (0, 0, j))
    return main, prev, nxt, wspec, bspec


def _convglu_bwd(u, dffn, w_down, w, b, name):
    _, s, f = u.shape
    dm = dffn.shape[2]
    tm = _tile(s, 256)
    cb = _tile(f, FFN_COLS)
    steps = s // tm
    n_ext = tm + 2 * HALO
    main, prev, nxt, wspec, bspec = _ffn_specs(s, tm, cb)
    hb = tm // HALO
    last = s // HALO - 1
    d_main = pl.BlockSpec((None, tm, dm), lambda j, i: (0, i, 0))
    d_next = pl.BlockSpec((None, HALO, dm), lambda j, i: (0, jnp.minimum((i + 1) * hb, last), 0))
    wd_spec = pl.BlockSpec((None, cb, dm), lambda j, i: (0, j, 0))

    def body(u_ref, up_ref, un_ref, d_ref, dn_ref, wd_ref, w_ref, b_ref, du_ref, acc_ref):
        i = pl.program_id(1)
        first = jnp.where(i > 0, 1.0, 0.0)
        notlast = jnp.where(i < steps - 1, 1.0, 0.0)

        @pl.when(i == 0)
        def _():
            acc_ref[...] = jnp.zeros_like(acc_ref)

        def ext(g):
            return jnp.concatenate([up_ref[g].astype(F32) * first, u_ref[g].astype(F32), un_ref[g].astype(F32)], axis=0)

        ug, uv = ext(0), ext(1)
        gate = _conv_taps(ug, w_ref[0], b_ref[0])
        val = _conv_taps(uv, w_ref[1], b_ref[1])
        wd = wd_ref[...]
        da = _dot_nt(d_ref[...], wd).astype(BF16).astype(F32)
        da_next = _dot_nt(dn_ref[...], wd).astype(BF16).astype(F32) * notlast
        da_e = jnp.concatenate([jnp.zeros((HALO, cb), F32), da, da_next], axis=0)
        sg = _sig(gate)
        d_val = da_e * gate * sg
        d_gate = da_e * val * (sg * (1.0 + gate * (1.0 - sg)))

        def finish(g, d, e):
            wv = w_ref[g]
            rows = slice(HALO, HALO + tm)
            d1, d2 = pltpu.roll(d, n_ext - 1, 0), pltpu.roll(d, n_ext - 2, 0)
            du_ref[g] = (wv[2:3] * d + wv[1:2] * d1 + wv[0:1] * d2)[rows].astype(BF16)
            em = e[rows]
            acc_ref[g, 2] += _colsum8(d[rows] * em)
            acc_ref[g, 1] += _colsum8(d1[rows] * em)
            acc_ref[g, 0] += _colsum8(d2[rows] * em)
            acc_ref[g, 3] += _colsum8(d[rows])

        finish(0, d_gate, ug)
        finish(1, d_val, uv)

    return pl.pallas_call(
        body, name=name, grid=(f // cb, steps),
        in_specs=[main, prev, nxt, d_main, d_next, wd_spec, wspec, bspec],
        out_specs=[main, pl.BlockSpec((2, 4, SUBLANES, cb), lambda j, i: (0, 0, 0, j))],
        out_shape=[jax.ShapeDtypeStruct((2, s, f), BF16), jax.ShapeDtypeStruct((2, 4, SUBLANES, f), F32)],
    )(u, u, u, dffn, dffn, w_down, w, b)


def _hgrn_gates(q_raw, f_raw, lb, tri):
    sf = _sig(f_raw)
    fg = lb + (1.0 - lb) * sf
    b = _tri_dot(tri, jnp.log(fg))
    return q_raw * _sig(q_raw), 1.0 - fg, b, fg, sf


def _hgrn_fwd(proj, lb, norm_g, name):
    s = proj.shape[0]
    tb = _tile(s, HGRN_ROWS)
    n_c = tb // A_CHUNK
    half = A_CHUNK // 2

    def body(q_ref, f_ref, v_ref, g_ref, lb_ref, ng_ref, o_ref, yp_ref, st_ref, state):
        @pl.when(pl.program_id(0) == 0)
        def _():
            state[...] = jnp.zeros_like(state)

        tri = _tri(A_CHUNK)
        causal = lax.broadcasted_iota(jnp.int32, (A_CHUNK, A_CHUNK), 1) <= lax.broadcasted_iota(
            jnp.int32, (A_CHUNK, A_CHUNK), 0)

        def chunk(ci, carry):
            rows = pl.ds(pl.multiple_of(ci * A_CHUNK, A_CHUNK), A_CHUNK)
            for h in range(HEADS):
                cs = slice(h * HEAD_DIM, (h + 1) * HEAD_DIM)
                qs, k, b, _, _ = _hgrn_gates(q_ref[rows, cs], f_ref[rows, cs], lb_ref[:, cs], tri)
                b_mid, b_last = b[half:half + 1], b[A_CHUNK - 1:A_CHUNK]
                vb = v_ref[rows, cs].astype(BF16)
                scores = _dot_nt((qs * jnp.exp(b - b_mid)).astype(BF16), (k * jnp.exp(b_mid - b)).astype(BF16))
                scores = jnp.where(causal, scores, 0.0)
                st = state[h]
                st_ref[ci, h] = st
                o = _dot(scores.astype(BF16), vb) + _dot_nt((qs * jnp.exp(b)).astype(BF16), st.astype(BF16))
                state[h] = st * jnp.exp(b_last) + _dot_tn(vb, (k * jnp.exp(b_last - b)).astype(BF16))
                o_ref[rows, cs] = o
                inv = lax.rsqrt(jnp.mean(o * o, axis=-1, keepdims=True) + EPS)
                g_raw = g_ref[rows, cs]
                yp_ref[rows, cs] = (o * inv * ng_ref[:, cs] * (g_raw * _sig(g_raw))).astype(BF16)
            return carry

        lax.fori_loop(0, n_c, chunk, 0)

    col = lambda j: pl.BlockSpec((tb, D_MODEL), lambda i: (i, j))
    vec = _full((1, D_MODEL))
    return pl.pallas_call(
        body, name=name, grid=(s // tb,), in_specs=[col(0), col(1), col(2), col(3), vec, vec],
        out_specs=[col(0), col(0), pl.BlockSpec((n_c, HEADS, HEAD_DIM, HEAD_DIM), lambda i: (i, 0, 0, 0))],
        out_shape=[jax.ShapeDtypeStruct((s, D_MODEL), F32), jax.ShapeDtypeStruct((s, D_MODEL), BF16),
                   jax.ShapeDtypeStruct((s // A_CHUNK, HEADS, HEAD_DIM, HEAD_DIM), F32)],
        scratch_shapes=[pltpu.VMEM((HEADS, HEAD_DIM, HEAD_DIM), F32)],
    )(proj, proj, proj, proj, lb, norm_g)


def _hgrn_bwd(proj, lb, norm_g, o, states, dyp, name):
    s = proj.shape[0]
    tb = _tile(s, HGRN_ROWS)
    n_c = tb // A_CHUNK
    n_b = s // tb
    half = A_CHUNK // 2

    def body(q_ref, f_ref, v_ref, g_ref, lb_ref, ng_ref, o_ref, st_ref, dyp_ref, dp_ref, dlb_ref, dng_ref, dstate):
        @pl.when(pl.program_id(0) == 0)
        def _():
            dstate[...] = jnp.zeros_like(dstate)
            dlb_ref[...] = jnp.zeros_like(dlb_ref)
            dng_ref[...] = jnp.zeros_like(dng_ref)

        tri = _tri(A_CHUNK)
        tri_up = _tri(A_CHUNK, upper=True)
        row_id = lax.broadcasted_iota(jnp.int32, (A_CHUNK, HEAD_DIM), 0)
        causal = lax.broadcasted_iota(jnp.int32, (A_CHUNK, A_CHUNK), 1) <= lax.broadcasted_iota(
            jnp.int32, (A_CHUNK, A_CHUNK), 0)

        def chunk(cj, carry):
            ci = n_c - 1 - cj
            rows = pl.ds(pl.multiple_of(ci * A_CHUNK, A_CHUNK), A_CHUNK)
            for h in range(HEADS):
                cs = slice(h * HEAD_DIM, (h + 1) * HEAD_DIM)
                q_raw, lbh = q_ref[rows, cs], lb_ref[:, cs]
                qs, k, b, fg, sf = _hgrn_gates(q_raw, f_ref[rows, cs], lbh, tri)
                b_mid, b_last = b[half:half + 1], b[A_CHUNK - 1:A_CHUNK]
                e_qi, e_ki, e_q, e_ks = jnp.exp(b - b_mid), jnp.exp(b_mid - b), jnp.exp(b), jnp.exp(b_last - b)
                q_i, k_i, q_e, k_s = qs * e_qi, k * e_ki, qs * e_q, k * e_ks
                vb = v_ref[rows, cs].astype(BF16)
                scores = jnp.where(causal, _dot_nt(q_i.astype(BF16), k_i.astype(BF16)), 0.0)
                ov, g_raw, dy, ng = o_ref[rows, cs], g_ref[rows, cs], dyp_ref[rows, cs], ng_ref[:, cs]
                inv = lax.rsqrt(jnp.mean(ov * ov, axis=-1, keepdims=True) + EPS)
                nrm = ov * inv
                sg = _sig(g_raw)
                gs = g_raw * sg
                dn = dy * ng * gs
                dng_ref[0:1, cs] += jnp.sum(dy * nrm * gs, axis=0, keepdims=True)
                dg_raw = dy * nrm * ng * (sg * (1.0 + g_raw * (1.0 - sg)))
                do = (inv * (dn - nrm * jnp.mean(dn * nrm, axis=-1, keepdims=True))).astype(BF16)
                st_prev = st_ref[ci, h]
                dst = dstate[h]
                dstb = dst.astype(BF16)
                d_scores = jnp.where(causal, _dot_nt(do, vb), 0.0).astype(BF16)
                dv = _dot_tn(scores.astype(BF16), do) + _dot_nt(k_s.astype(BF16), dstb)
                dq_i = _dot(d_scores, k_i.astype(BF16))
                dk_i = _dot_tn(d_scores, q_i.astype(BF16))
                dq_e = _dot(do, st_prev.astype(BF16))
                dk_s = _dot(vb, dstb)
                d_decay = jnp.sum(st_prev * dst, axis=0, keepdims=True)
                dstate[h] = dst * jnp.exp(b_last) + _dot_tn(do, q_e.astype(BF16))
                dq = dq_i * e_qi + dq_e * e_q
                dk = dk_i * e_ki + dk_s * e_ks
                t_qi, t_ki, t_ks = dq_i * q_i, dk_i * k_i, dk_s * k_s
                db = t_qi - t_ki + dq_e * q_e - t_ks
                db_mid = jnp.sum(t_ki - t_qi, axis=0, keepdims=True)
                db_last = jnp.sum(t_ks, axis=0, keepdims=True) + d_decay * jnp.exp(b_last)
                db = db + jnp.where(row_id == half, db_mid, 0.0) + jnp.where(row_id == A_CHUNK - 1, db_last, 0.0)
                dfg = _tri_dot(tri_up, db) / fg - dk
                dlb_ref[0:1, cs] += jnp.sum(dfg * (1.0 - sf), axis=0, keepdims=True)
                sq = _sig(q_raw)
                dp_ref[0, rows, cs] = (dq * (sq * (1.0 + q_raw * (1.0 - sq)))).astype(BF16)
                dp_ref[1, rows, cs] = (dfg * (1.0 - lbh) * sf * (1.0 - sf)).astype(BF16)
                dp_ref[2, rows, cs] = dv.astype(BF16)
                dp_ref[3, rows, cs] = dg_raw.astype(BF16)
            return carry

        lax.fori_loop(0, n_c, chunk, 0)

    col = lambda j: pl.BlockSpec((tb, D_MODEL), lambda i: (n_b - 1 - i, j))
    vec = _full((1, D_MODEL))
    acc = _full((SUBLANES, D_MODEL))
    return pl.pallas_call(
        body, name=name, grid=(n_b,),
        in_specs=[col(0), col(1), col(2), col(3), vec, vec, col(0),
                  pl.BlockSpec((n_c, HEADS, HEAD_DIM, HEAD_DIM), lambda i: (n_b - 1 - i, 0, 0, 0)), col(0)],
        out_specs=[pl.BlockSpec((4, tb, D_MODEL), lambda i: (0, n_b - 1 - i, 0)), acc, acc],
        out_shape=[jax.ShapeDtypeStruct((4, s, D_MODEL), BF16), jax.ShapeDtypeStruct((SUBLANES, D_MODEL), F32),
                   jax.ShapeDtypeStruct((SUBLANES, D_MODEL), F32)],
        scratch_shapes=[pltpu.VMEM((HEADS, HEAD_DIM, HEAD_DIM), F32)],
    )(proj, proj, proj, proj, lb, norm_g, o, states, dyp)


def _headnorm(x, g, mult, name, col0=0):
    s = x.shape[0]
    tm = _tile(s, ROW_TILE)

    def body(x_ref, g_ref, y_ref):
        for h in range(HEADS):
            cs = slice(h * HEAD_DIM, (h + 1) * HEAD_DIM)
            xv = x_ref[:, cs]
            inv = lax.rsqrt(jnp.mean(xv * xv, axis=-1, keepdims=True) + EPS)
            y_ref[:, cs] = (xv * inv * g_ref[:, cs] * mult).astype(BF16)

    return pl.pallas_call(
        body, name=name, grid=(s // tm,),
        in_specs=[pl.BlockSpec((tm, D_MODEL), lambda i: (i, col0)), _full((1, D_MODEL))],
        out_specs=pl.BlockSpec((tm, D_MODEL), lambda i: (i, 0)),
        out_shape=jax.ShapeDtypeStruct((s, D_MODEL), BF16),
    )(x, g)


def _headnorm_bwd(x, g, mult, dy, name, col0=0, extra=None):
    s = x.shape[0]
    tm = _tile(s, ROW_TILE)
    groups = 2 if extra is not None else 1
    head_major = dy.ndim == 3

    def body(*refs):
        x_ref, g_ref, dy_ref = refs[:3]
        dx_ref, dg_ref = refs[-2:]

        @pl.when(pl.program_id(0) == 0)
        def _():
            dg_ref[...] = jnp.zeros_like(dg_ref)

        for h in range(HEADS):
            cs = slice(h * HEAD_DIM, (h + 1) * HEAD_DIM)
            xv, gv = x_ref[:, cs], g_ref[:, cs]
            dyv = dy_ref[h, :, 0:HEAD_DIM] if head_major else dy_ref[:, cs]
            inv = lax.rsqrt(jnp.mean(xv * xv, axis=-1, keepdims=True) + EPS)
            nrm = xv * inv
            dn = dyv * gv * mult
            dg_ref[:, cs] += _colsum8(dyv * nrm * mult)
            dx_ref[0, :, cs] = (inv * (dn - nrm * jnp.mean(dn * nrm, axis=-1, keepdims=True))).astype(BF16)
        if extra is not None:
            dx_ref[1] = refs[3][...]

    row = pl.BlockSpec((tm, D_MODEL), lambda i: (i, 0))
    dy_spec = pl.BlockSpec((HEADS, tm, dy.shape[-1]), lambda i: (0, i, 0)) if head_major else row
    ins = [x, g, dy] + ([extra] if extra is not None else [])
    specs = ([pl.BlockSpec((tm, D_MODEL), lambda i: (i, col0)), _full((1, D_MODEL)), dy_spec]
             + ([row] if extra is not None else []))
    return pl.pallas_call(
        body, name=name, grid=(s // tm,), in_specs=specs,
        out_specs=[pl.BlockSpec((groups, tm, D_MODEL), lambda i: (0, i, 0)), _full((SUBLANES, D_MODEL))],
        out_shape=[jax.ShapeDtypeStruct((groups, s, D_MODEL), BF16), jax.ShapeDtypeStruct((SUBLANES, D_MODEL), F32)],
    )(*ins)


def _log_sigmoid(z):
    return jnp.minimum(z, 0.0) - jnp.log(1.0 + jnp.exp(-jnp.abs(z)))


Q_CUM, Q_ONE, Q_LSE = 0, 3, 6
LOG2E = 1.4426950408889634


def _pieces(v):
    hi = v.astype(BF16).astype(F32)
    mid = (v - hi).astype(BF16).astype(F32)
    lo = ((v - hi) - mid).astype(BF16).astype(F32)
    return hi, mid, lo


def _side(lane, at, v):
    hi, mid, lo = _pieces(v)
    return jnp.where(lane == at, hi, jnp.where(lane == at + 1, mid, jnp.where(lane == at + 2, lo, 0.0)))


def _fcum_fwd(f, bias, name):
    s = f.shape[0]
    tm = _tile(s, ROW_TILE)

    def body(f_ref, b_ref, qa_ref, ka_ref, carry):
        @pl.when(pl.program_id(0) == 0)
        def _():
            carry[...] = jnp.zeros_like(carry)

        cum = _tri_dot(_tri(tm), _log_sigmoid(f_ref[...] + b_ref[...])) + carry[...]
        carry[...] = cum[tm - 1:tm]
        lane = lax.broadcasted_iota(jnp.int32, (tm, LANES), 1)
        ones_q = jnp.where((lane >= Q_ONE) & (lane < Q_LSE), 1.0, 0.0)
        ones_k = jnp.where((lane < Q_ONE) | ((lane >= Q_LSE) & (lane < Q_LSE + 3)), 1.0, 0.0)
        for h in range(HEADS):
            c2 = cum[:, h:h + 1] * LOG2E
            qa_ref[h] = (_side(lane, Q_CUM, c2) + ones_q).astype(BF16)
            ka_ref[h] = (_side(lane, Q_ONE, -c2) + ones_k).astype(BF16)

    side = pl.BlockSpec((HEADS, tm, LANES), lambda i: (0, i, 0))
    return pl.pallas_call(
        body, name=name, grid=(s // tm,),
        in_specs=[pl.BlockSpec((tm, LANES), lambda i: (i, 0)), _full((1, LANES))],
        out_specs=[side, side],
        out_shape=[jax.ShapeDtypeStruct((HEADS, s, LANES), BF16)] * 2,
        scratch_shapes=[pltpu.VMEM((1, LANES), F32)],
    )(f, bias)


def _fcum_bwd(f, bias, dka, dq, name):
    s = f.shape[0]
    tm = _tile(s, ROW_TILE)
    n_b = s // tm
    q_lane = HEAD_DIM + Q_CUM

    def body(f_ref, b_ref, dka_ref, dqa_ref, dz_ref, db_ref, carry):
        @pl.when(pl.program_id(0) == 0)
        def _():
            carry[...] = jnp.zeros_like(carry)
            db_ref[...] = jnp.zeros_like(db_ref)

        lane = lax.broadcasted_iota(jnp.int32, (tm, LANES), 1)
        dcum = jnp.zeros((tm, LANES), F32)
        for h in range(HEADS):
            dcum = dcum + jnp.where(lane == h, dqa_ref[h, :, q_lane:q_lane + 1] - dka_ref[h, :, Q_ONE:Q_ONE + 1], 0.0)
        dlf = _tri_dot(_tri(tm, upper=True), dcum) + carry[...]
        carry[...] = dlf[0:1]
        dz = dlf * _sig(-(f_ref[...] + b_ref[...]))
        dz_ref[0] = dz.astype(BF16)
        db_ref[...] += _colsum8(dz)

    return pl.pallas_call(
        body, name=name, grid=(n_b,),
        in_specs=[pl.BlockSpec((tm, LANES), lambda i: (n_b - 1 - i, 0)), _full((1, LANES)),
                  pl.BlockSpec((HEADS, tm, LANES), lambda i: (0, n_b - 1 - i, 0)),
                  pl.BlockSpec((HEADS, tm, 2 * HEAD_DIM), lambda i: (0, n_b - 1 - i, 0))],
        out_specs=[pl.BlockSpec((1, tm, LANES), lambda i: (0, n_b - 1 - i, 0)), _full((SUBLANES, LANES))],
        out_shape=[jax.ShapeDtypeStruct((1, s, LANES), BF16), jax.ShapeDtypeStruct((SUBLANES, LANES), F32)],
        scratch_shapes=[pltpu.VMEM((1, LANES), F32)],
    )(f, bias, dka, dq)


def _causal_pairs(n_t, key_major):
    if key_major:
        pairs = [(qi, ki) for ki in range(n_t) for qi in range(ki, n_t)]
    else:
        pairs = [(qi, ki) for qi in range(n_t) for ki in range(qi + 1)]
    return (jnp.array([p[0] for p in pairs], jnp.int32), jnp.array([p[1] for p in pairs], jnp.int32))


def _with_side(main_ref, side_ref):
    return jnp.concatenate([main_ref[...], side_ref[...]], axis=1)


def _lane_const(t, lo, hi, value):
    lane = lax.broadcasted_iota(jnp.int32, (t, LANES), 1)
    return jnp.where((lane >= lo) & (lane < hi), value, 0.0).astype(BF16)


def _att_specs(t, nh):
    qmain = pl.BlockSpec((t, nh * HEAD_DIM), lambda h, p, qt, kt: (qt[p], h))
    kmain = pl.BlockSpec((t, nh * HEAD_DIM), lambda h, p, qt, kt: (kt[p], h))
    qside = pl.BlockSpec((nh, t, LANES), lambda h, p, qt, kt: (h, qt[p], 0))
    kside = pl.BlockSpec((nh, t, LANES), lambda h, p, qt, kt: (h, kt[p], 0))
    return qmain, kmain, qside, kside


def _fox_fwd(q, qa, k, ka, v, qo, name):
    s = q.shape[0]
    t = _tile(s, ATT_TILE)
    sub = t // ATT_SPLIT
    nh = ATT_FWD_HEADS
    qt, kt = _causal_pairs(s // t, key_major=False)

    def body(qt_ref, kt_ref, q_ref, qa_ref, k_ref, ka_ref, v_ref, og_ref, o_ref, y_ref, qab_ref, m_s, l_s, acc_s):
        pid = pl.program_id(1)
        qi, ki = qt_ref[pid], kt_ref[pid]

        @pl.when(ki == 0)
        def _():
            m_s[...] = jnp.full_like(m_s, NEG_INF)
            l_s[...] = jnp.zeros_like(l_s)
            acc_s[...] = jnp.zeros_like(acc_s)

        def step(diagonal):
            for hh in range(nh):
                hc = slice(hh * HEAD_DIM, (hh + 1) * HEAD_DIM)
                kc = jnp.concatenate([k_ref[:, hc], ka_ref[hh]], axis=1)
                vc = jnp.concatenate([v_ref[:, hc], _lane_const(t, 0, 1, 1.0)], axis=1)
                for r in range(ATT_SPLIT):
                    rows = slice(r * sub, (r + 1) * sub)
                    n_k = (r + 1) * sub if diagonal else t
                    sc = _dot_nt(jnp.concatenate([q_ref[rows, hc], qa_ref[hh, rows]], axis=1), kc[:n_k])
                    if diagonal:
                        sc = jnp.where(lax.broadcasted_iota(jnp.int32, (sub, n_k), 1)
                                       <= lax.broadcasted_iota(jnp.int32, (sub, n_k), 0) + r * sub, sc, NEG_INF)
                    m_old = m_s[hh, rows]
                    m_new = jnp.maximum(m_old, jnp.max(sc, axis=-1, keepdims=True))
                    alpha = jnp.exp2(m_old - m_new)
                    pv = _dot(jnp.exp2(sc - m_new[:, 0:1]).astype(BF16), vc[:n_k])
                    acc_s[hh, rows] = alpha * acc_s[hh, rows] + pv[:, :HEAD_DIM]
                    l_s[hh, rows] = alpha * l_s[hh, rows] + pv[:, HEAD_DIM:]
                    m_s[hh, rows] = m_new

        @pl.when(ki < qi)
        def _():
            step(False)

        @pl.when(ki == qi)
        def _():
            step(True)
            lane = lax.broadcasted_iota(jnp.int32, (t, LANES), 1)
            for hh in range(nh):
                hc = slice(hh * HEAD_DIM, (hh + 1) * HEAD_DIM)
                l = l_s[hh, :, 0:1]
                o = acc_s[hh] / l
                o_ref[:, hc] = o
                y_ref[:, hc] = (o * _sig(og_ref[:, hc])).astype(BF16)
                qab_ref[hh] = qa_ref[hh] + _side(lane, Q_LSE, -(m_s[hh, :, 0:1] + jnp.log2(l))).astype(BF16)

    qmain, kmain, qside, kside = _att_specs(t, nh)
    return pl.pallas_call(
        body, name=name,
        grid_spec=pltpu.PrefetchScalarGridSpec(
            num_scalar_prefetch=2, grid=(HEADS // nh, qt.shape[0]),
            in_specs=[qmain, qside, kmain, kside, kmain,
                      pl.BlockSpec((t, nh * HEAD_DIM), lambda h, p, qt, kt: (qt[p], HEADS // nh + h))],
            out_specs=[qmain, qmain, qside],
            scratch_shapes=[pltpu.VMEM((nh, t, LANES), F32), pltpu.VMEM((nh, t, LANES), F32),
                            pltpu.VMEM((nh, t, HEAD_DIM), F32)]),
        out_shape=[jax.ShapeDtypeStruct((s, D_MODEL), F32), jax.ShapeDtypeStruct((s, D_MODEL), BF16),
                   jax.ShapeDtypeStruct((HEADS, s, LANES), BF16)],
    )(qt, kt, q, qa, k, ka, v, qo)


def _fox_gate_bwd(o, qo, dy, name):
    s = o.shape[0]
    tm = _tile(s, ROW_TILE)

    def body(o_ref, og_ref, dy_ref, do_ref, dg_ref, dl_ref):
        ov, dyv = o_ref[...], dy_ref[...]
        sg = _sig(og_ref[...])
        do = (dyv * sg).astype(BF16)
        do_ref[...] = do
        dg_ref[...] = (dyv * ov * sg * (1.0 - sg)).astype(BF16)
        prod = do.astype(F32) * ov
        lane = lax.broadcasted_iota(jnp.int32, (tm, LANES), 1)
        for h in range(HEADS):
            delta = jnp.sum(prod[:, h * HEAD_DIM:(h + 1) * HEAD_DIM], axis=-1, keepdims=True)
            dl_ref[h] = _side(lane, 0, delta).astype(BF16)

    row = pl.BlockSpec((tm, D_MODEL), lambda i: (i, 0))
    return pl.pallas_call(
        body, name=name, grid=(s // tm,),
        in_specs=[row, pl.BlockSpec((tm, D_MODEL), lambda i: (i, 1)), row],
        out_specs=[row, row, pl.BlockSpec((HEADS, tm, LANES), lambda i: (0, i, 0))],
        out_shape=[jax.ShapeDtypeStruct((s, D_MODEL), BF16), jax.ShapeDtypeStruct((s, D_MODEL), BF16),
                   jax.ShapeDtypeStruct((HEADS, s, LANES), BF16)],
    )(o, qo, dy)


def _fox_bwd(q, qab, k, ka, v, do, doa, name):
    s = q.shape[0]
    t = _tile(s, ATT_TILE)
    n_t = s // t
    sub = t // ATT_SPLIT
    nh = ATT_BWD_HEADS
    qt, kt = _causal_pairs(n_t, key_major=True)

    def body(qt_ref, kt_ref, q_ref, qab_ref, k_ref, ka_ref, v_ref, do_ref, doa_ref, dk_ref, dv_ref, dka_ref, dq_hbm,
             dk_s, dv_s, dq_ref):
        group, pid = pl.program_id(0), pl.program_id(1)
        qi, ki = qt_ref[pid], kt_ref[pid]

        @pl.when(pid == 0)
        def _():
            dq_ref[...] = jnp.zeros_like(dq_ref)

        @pl.when(qi == ki)
        def _():
            dk_s[...] = jnp.zeros_like(dk_s)
            dv_s[...] = jnp.zeros_like(dv_s)

        def step(diagonal):
            for hh in range(nh):
                hc = slice(hh * HEAD_DIM, (hh + 1) * HEAD_DIM)
                kc = jnp.concatenate([k_ref[:, hc], ka_ref[hh]], axis=1)
                vc = jnp.concatenate([v_ref[:, hc], _lane_const(t, 0, 3, -1.0)], axis=1)
                for r in range(ATT_SPLIT):
                    cols = slice(r * sub, (r + 1) * sub)
                    n_k = (r + 1) * sub if diagonal else t
                    qc = jnp.concatenate([q_ref[cols, hc], qab_ref[hh, cols]], axis=1)
                    sc = _dot_nt(kc[:n_k], qc)
                    if diagonal:
                        sc = jnp.where(lax.broadcasted_iota(jnp.int32, (n_k, sub), 0)
                                       <= lax.broadcasted_iota(jnp.int32, (n_k, sub), 1) + r * sub, sc, NEG_INF)
                    p = jnp.exp2(sc)
                    dov = do_ref[cols, hc]
                    dp = _dot_nt(vc[:n_k], jnp.concatenate([dov, doa_ref[hh, cols]], axis=1))
                    ds = (p * dp).astype(BF16)
                    dv_s[hh, 0:n_k] += _dot(p.astype(BF16), dov)
                    dk_s[hh, 0:n_k] += _dot(ds, qc)
                    q_rows = pl.ds(pl.multiple_of(qi * t + r * sub, sub), sub)
                    dq_ref[hh, q_rows, :] += _dot_tn(ds, kc[:n_k])

        @pl.when(qi > ki)
        def _():
            step(False)

        @pl.when(qi == ki)
        def _():
            step(True)

        @pl.when(qi == n_t - 1)
        def _():
            for hh in range(nh):
                hc = slice(hh * HEAD_DIM, (hh + 1) * HEAD_DIM)
                dk_ref[:, hc] = dk_s[hh, :, :HEAD_DIM] * (1.0 / LOG2E)
                dka_ref[hh] = dk_s[hh, :, HEAD_DIM:]
                dv_ref[:, hc] = dv_s[hh].astype(BF16)

        @pl.when(pid == qt.shape[0] - 1)
        def _():
            pltpu.sync_copy(dq_ref, dq_hbm.at[pl.ds(group * nh, nh)])

    qmain, kmain, qside, kside = _att_specs(t, nh)
    return pl.pallas_call(
        body, name=name,
        grid_spec=pltpu.PrefetchScalarGridSpec(
            num_scalar_prefetch=2, grid=(HEADS // nh, qt.shape[0]),
            in_specs=[qmain, qside, kmain, kside, kmain, qmain, qside],
            out_specs=[kmain, pl.BlockSpec((None, t, nh * HEAD_DIM), lambda h, p, qt, kt: (0, kt[p], h)), kside,
                       pl.BlockSpec(memory_space=pltpu.HBM)],
            scratch_shapes=[pltpu.VMEM((nh, t, 2 * HEAD_DIM), F32), pltpu.VMEM((nh, t, HEAD_DIM), F32),
                            pltpu.VMEM((nh, s, 2 * HEAD_DIM), F32)]),
        out_shape=[jax.ShapeDtypeStruct((s, D_MODEL), F32), jax.ShapeDtypeStruct((1, s, D_MODEL), BF16),
                   jax.ShapeDtypeStruct((HEADS, s, LANES), F32), jax.ShapeDtypeStruct((HEADS, s, 2 * HEAD_DIM), F32)],
    )(qt, kt, q, qab, k, ka, v, do, doa)


def _mm_residual_premix(a, w, x, gate, mods, name):
    s, k = a.shape
    dm = x.shape[1]
    tm = _tile(s, ROW_TILE)

    def body(*refs):
        a_ref, w_ref, x_ref, g_ref = refs[:4]
        mod_refs = refs[4:4 + 2 * len(mods)]
        y_ref, xn_ref = refs[4 + 2 * len(mods):6 + 2 * len(mods)]
        h_refs = refs[6 + 2 * len(mods):]
        y = _dot(a_ref[...], w_ref[0])
        y_ref[...] = y
        xv = x_ref[...] + g_ref[...] * y
        xn_ref[...] = xv
        nrm = xv * lax.rsqrt(jnp.mean(xv * xv, axis=-1, keepdims=True) + EPS)
        for t, h_ref in enumerate(h_refs):
            h_ref[...] = (nrm * (1.0 + mod_refs[2 * t + 1][...]) + mod_refs[2 * t][...]).astype(BF16)

    row = pl.BlockSpec((tm, dm), lambda i: (i, 0))
    vec = _full((1, dm))
    outs = pl.pallas_call(
        body, name=name, grid=(s // tm,),
        in_specs=[pl.BlockSpec((tm, k), lambda i: (i, 0)), _full(w.shape), row, vec] + [vec] * (2 * len(mods)),
        out_specs=[row] * (2 + len(mods)),
        out_shape=[jax.ShapeDtypeStruct((s, dm), F32)] * 2 + [jax.ShapeDtypeStruct((s, dm), BF16)] * len(mods),
    )(a, w, x, gate, *[v for m in mods for v in m])
    return outs[0], outs[1], list(outs[2:])


def _mm_loss_head(a, w, x, gate, target, name):
    s, k = a.shape
    dm = x.shape[1]
    tm = _tile(s, ROW_TILE)

    def body(a_ref, w_ref, x_ref, g_ref, t_ref, sq_ref, do_ref, dy_ref, dg_ref):
        @pl.when(pl.program_id(0) == 0)
        def _():
            sq_ref[...] = jnp.zeros_like(sq_ref)
            dg_ref[...] = jnp.zeros_like(dg_ref)

        y, gv = _dot(a_ref[...], w_ref[0]), g_ref[...]
        err = x_ref[...] + gv * y - t_ref[...]
        sq_ref[...] += _colsum8(err * err)
        dout = err * (1.0 / dm)
        do_ref[...] = dout
        dy_ref[0] = (dout * gv).astype(BF16)
        dg_ref[...] += _colsum8(dout * y)

    row = pl.BlockSpec((tm, dm), lambda i: (i, 0))
    acc = _full((SUBLANES, dm))
    return pl.pallas_call(
        body, name=name, grid=(s // tm,),
        in_specs=[pl.BlockSpec((tm, k), lambda i: (i, 0)), _full(w.shape), row, _full((1, dm)), row],
        out_specs=[acc, row, pl.BlockSpec((1, tm, dm), lambda i: (0, i, 0)), acc],
        out_shape=[jax.ShapeDtypeStruct((SUBLANES, dm), F32), jax.ShapeDtypeStruct((s, dm), F32),
                   jax.ShapeDtypeStruct((1, s, dm), BF16), jax.ShapeDtypeStruct((SUBLANES, dm), F32)],
    )(a, w, x, gate, target)


def _ffn_inner(h, w_up, conv_w, conv_b, tag):
    s, dm = h.shape
    half = w_up.shape[2]
    f = 2 * half
    tm = _tile(s, FFN_ROWS)

    def body(h_ref, w_ref, cw_ref, cb_ref, u_ref, a_ref, carry):
        @pl.when(pl.program_id(0) == 0)
        def _():
            carry[...] = jnp.zeros_like(carry)

        hv = h_ref[...]
        for j in range(2):
            cols = slice(j * half, (j + 1) * half)
            conv = []
            for g in range(2):
                ub = _dot(hv, w_ref[2 * g + j]).astype(BF16)
                u_ref[g, :, cols] = ub
                uf = ub.astype(F32)
                e = jnp.concatenate([carry[g, j], uf], axis=0)
                carry[g, j] = uf[tm - SUBLANES:tm]
                conv.append(_conv_taps(e, cw_ref[g][:, cols], cb_ref[g][:, cols])[SUBLANES:])
            a_ref[:, cols] = (conv[0] * _sig(conv[0]) * conv[1]).astype(BF16)

    return pl.pallas_call(
        body, name=tag + "_up_convglu", grid=(s // tm,),
        in_specs=[pl.BlockSpec((tm, dm), lambda i: (i, 0)), _full(w_up.shape), _full(conv_w.shape), _full(conv_b.shape)],
        out_specs=[pl.BlockSpec((2, tm, f), lambda i: (0, i, 0)), pl.BlockSpec((tm, f), lambda i: (i, 0))],
        out_shape=[jax.ShapeDtypeStruct((2, s, f), BF16), jax.ShapeDtypeStruct((s, f), BF16)],
        scratch_shapes=[pltpu.VMEM((2, 2, SUBLANES, half), F32)],
    )(h, w_up, conv_w, conv_b)


def _weight_grad_first(a, d, p_n, name):
    return lax.optimization_barrier((_mm_tn(a, d, p_n, name), d))


def _ffn_backward(dx_out, dffn, x_mid, scale, saved, w_up, conv_w, conv_b, w_down, mixer, tag):
    h, u, a = saved
    dw_down, dffn = _weight_grad_first(a, dffn, 1, tag + "_down_dw")
    du, dconv = _convglu_bwd(u, dffn, w_down, conv_w, conv_b, tag + "_convglu_bwd")
    dw_up, du = _weight_grad_first(h, du, N_CHIPS, tag + "_up_dw")
    dx_mid, [(dshift, dscale)], dy, dgate_mixer = _premix_bwd(x_mid, [(scale, [(du, w_up)])], dx_out,
                                                              tag + "_premix_bwd", branch=mixer)
    return dx_mid, dy, dgate_mixer, dw_up, dw_down, dict(shift=dshift, scale=dscale, conv=dconv)


def _local_step(x, target, mods, lb, vecs, weights_at):
    m0, m1, mk = mods["l0"], mods["l1"], mods["kv"]
    h0 = _premix(x, m0[0], m0[1], "l0_premix")
    wts, h0 = weights_at("mixer0", h0)
    proj = _mm_nn(h0, wts["a_w_in"], 1, F32, "l0_in")[0]
    o_a, yp, states = _hgrn_fwd(proj, lb, vecs["a_norm_g"], "l0_hgrn")
    more, yp = weights_at("ffn0", yp)
    wts.update(more)
    y0, x1, [hf0] = _mm_residual_premix(yp, wts["a_w_out"], x, m0[2], [(m0[3], m0[4])], "l0_out")
    u0, a0 = _ffn_inner(hf0, wts["up0"], vecs["conv_w0"], vecs["conv_b0"], "l0_ffn")
    saved0 = (hf0, u0, a0)
    ffn0, x2, [hk, h1] = _mm_residual_premix(a0, wts["down0"], x1, m0[5], [(mk[0], mk[1]), (m1[0], m1[1])],
                                             "l0_ffn_down")
    more, hk = weights_at("layer1", hk)
    wts.update(more)
    k_raw = _mm_nn(hk, wts["kv_k"], 1, F32, "kv_k")[0]
    v_sh = _mm_nn(hk, wts["kv_v"], 1, BF16, "kv_v")[0]
    f_raw = _mm_nn(hk, wts["kv_f"], 1, F32, "kv_f")[0]
    k_sh = _headnorm(k_raw, vecs["k_norm_g"], 1.0, "kv_knorm")
    qa, ka = _fcum_fwd(f_raw, vecs["kv_b_f"], "kv_fcum")
    qo = _mm_nn(h1, wts["b_w_q"], 1, F32, "l1_q")[0]
    q_scale = HEAD_DIM ** -0.5
    q = _headnorm(qo, vecs["q_norm_g"], q_scale * LOG2E, "l1_qnorm")
    o_b, og, qab = _fox_fwd(q, qa, k_sh, ka, v_sh, qo, "l1_fox")
    y1, x3, [hf1] = _mm_residual_premix(og, wts["b_w_out"], x2, m1[2], [(m1[3], m1[4])], "l1_out")
    u1, a1 = _ffn_inner(hf1, wts["up1"], vecs["conv_w1"], vecs["conv_b1"], "l1_ffn")
    saved1 = (hf1, u1, a1)
    sq, dx4, dffn1, dg2_1 = _mm_loss_head(a1, wts["down1"], x3, m1[5], target, "l1_ffn_down")

    big, small = {}, {}
    dx3, dy1, dg1_1, big["up1"], big["down1"], s_ffn1 = _ffn_backward(
        dx4, dffn1, x3, m1[4], saved1, wts["up1"], vecs["conv_w1"], vecs["conv_b1"], wts["down1"], (y1, m1[2]), "l1_ffn")
    big["b_w_out"], dy1 = _weight_grad_first(og, dy1, 1, "l1_out_dw")
    d_og = _mm_nt(dy1, wts["b_w_out"], F32, "l1_out_dx")
    do_b, dgate_b, doa = _fox_gate_bwd(o_b, qo, d_og, "l1_fox_gate_bwd")
    dk, dv, dka, dq = _fox_bwd(q, qab, k_sh, ka, v_sh, do_b, doa, "l1_fox_bwd")
    dqo, dqg = _headnorm_bwd(qo, vecs["q_norm_g"], q_scale, dq, "l1_qnorm_bwd", extra=dgate_b)
    big["b_w_q"], dqo = _weight_grad_first(h1, dqo, N_CHIPS, "l1_q_dw")
    dk_raw, dkg = _headnorm_bwd(k_raw, vecs["k_norm_g"], 1.0, dk, "kv_knorm_bwd")
    dz, dbf = _fcum_bwd(f_raw, vecs["kv_b_f"], dka, dq, "kv_fcum_bwd")
    big["kv_k"], dk_raw = _weight_grad_first(hk, dk_raw, 1, "kv_k_dw")
    big["kv_v"], dv = _weight_grad_first(hk, dv, 1, "kv_v_dw")
    big["kv_f"], dz = _weight_grad_first(hk, dz, 1, "kv_f_dw")
    kv_pairs = [(dk_raw, wts["kv_k"]), (dv, wts["kv_v"]), (dz, wts["kv_f"])]
    dx2, [(dsh1_1, dsc1_1), (dshk, dsck)], dffn0, dg2_0 = _premix_bwd(
        x2, [(m1[1], [(dqo, wts["b_w_q"])]), (mk[1], kv_pairs)], dx3, "l1_kv_premix_bwd", branch=(ffn0, m0[5]))
    dx1, dy0, dg1_0, big["up0"], big["down0"], s_ffn0 = _ffn_backward(
        dx2, dffn0, x1, m0[4], saved0, wts["up0"], vecs["conv_w0"], vecs["conv_b0"], wts["down0"], (y0, m0[2]), "l0_ffn")
    big["a_w_out"], dy0 = _weight_grad_first(yp, dy0, 1, "l0_out_dw")
    dyp = _mm_nt(dy0, wts["a_w_out"], F32, "l0_out_dx")
    dproj, dlb, dng = _hgrn_bwd(proj, lb, vecs["a_norm_g"], o_a, states, dyp, "l0_hgrn_bwd")
    grad_x, [(dsh1_0, dsc1_0)] = _premix_bwd(x, [(m0[1], [(dproj, wts["a_w_in"])])], dx1, "l0_premix_bwd")
    dproj, _ = lax.optimization_barrier((dproj, (dsh1_0, dsc1_0)))
    big["a_w_in"] = _mm_tn(h0, dproj, N_CHIPS, "l0_in_dw")

    small["mod_l0"] = [dsh1_0, dsc1_0, dg1_0, s_ffn0["shift"], s_ffn0["scale"], dg2_0]
    small["mod_l1"] = [dsh1_1, dsc1_1, dg1_1, s_ffn1["shift"], s_ffn1["scale"], dg2_1]
    small["mod_kv"] = [dshk, dsck]
    small["conv0"], small["conv1"] = s_ffn0["conv"], s_ffn1["conv"]
    small["a_norm_g"], small["k_norm_g"], small["q_norm_g"] = dng, dkg, dqg
    small["kv_b_f"], small["lb"] = dbf, dlb
    marks = {"attention_bwd": dk, "ffn0_bwd": dx1, "mixer0_bwd": grad_x}
    return sq, grad_x, big, small, marks


HBM = pl.BlockSpec(memory_space=pltpu.HBM)
COMM_CHUNK_ELEMS = 256 * 1024


def _place():
    x, y, c = lax.axis_index("x"), lax.axis_index("y"), lax.axis_index("c")
    chips = [(1 - x, y), (x, 1 - y), (1 - x, 1 - y)]
    return x, y, c, (x, y, 1 - c), chips


def _chunk_rows(rows, cols):
    best = BF16_ROWS
    for r in range(BF16_ROWS, rows + 1, BF16_ROWS):
        if rows % r == 0 and r * cols <= COMM_CHUNK_ELEMS:
            best = r
    assert rows % best == 0, (rows, cols)
    return best


def _allgather8(block, name):
    m_per, n = block.shape

    def body(x_ref, out_ref, send_sems, recv_sems, local_sem):
        x, y, c, sibling, chips = _place()
        me = (x, y, c)

        def rows(px, py, pc):
            return out_ref.at[pl.ds((4 * px + 2 * py + pc) * m_per, m_per), :]

        def copy(k, blk, to, src=None):
            return pltpu.make_async_remote_copy(
                src_ref=rows(*blk) if src is None else src, dst_ref=rows(*blk),
                send_sem=send_sems.at[k], recv_sem=recv_sems.at[k], device_id=to, device_id_type=MESH)

        mine = pltpu.make_async_copy(x_ref, rows(*me), local_sem)
        mine.start()
        first = [copy(0, me, sibling, src=x_ref)]
        first += [copy(1 + j, me, (*chip, c), src=x_ref) for j, chip in enumerate(chips)]
        for cp in first:
            cp.start()
        passed = [copy(4 + j, (*chip, c), sibling) for j, chip in enumerate(chips)]
        for j, chip in enumerate(chips):
            copy(1 + j, (*chip, c), me).wait_recv()
            passed[j].start()
        copy(0, sibling, me).wait_recv()
        for j, chip in enumerate(chips):
            copy(4 + j, (*chip, 1 - c), me).wait_recv()
        for cp in first + passed:
            cp.wait_send()
        mine.wait()

    return pl.pallas_call(
        body, name=name, out_shape=jax.ShapeDtypeStruct((N_DEV * m_per, n), block.dtype),
        in_specs=[pl.BlockSpec(memory_space=pltpu.VMEM)], out_specs=pl.BlockSpec(memory_space=pltpu.VMEM),
        scratch_shapes=[pltpu.SemaphoreType.DMA((7,)), pltpu.SemaphoreType.DMA((7,)), pltpu.SemaphoreType.DMA],
    )(block)


def _cast_own_block(shards, layer, chip, name):
    _, r, cols = shards.shape
    rows = _chunk_rows(r, cols)

    def body(chip_ref, w_ref, o_ref):
        o_ref[...] = w_ref[...].astype(BF16)

    return pl.pallas_call(
        body, name=name,
        grid_spec=pltpu.PrefetchScalarGridSpec(
            num_scalar_prefetch=1, grid=(r // rows,),
            in_specs=[pl.BlockSpec((None, rows, cols), lambda i, chip_ref: (layer, i, 0))],
            out_specs=pl.BlockSpec((None, rows, cols), lambda i, chip_ref: (chip_ref[0], i, 0))),
        out_shape=jax.ShapeDtypeStruct((N_CHIPS, r, cols), BF16),
    )(chip, shards)


def _sequencer_gather(bufs, name, collective_id):
    n_t = len(bufs)
    dims = [b.shape[1:] for b in bufs]
    refs = [jax.new_ref(b, memory_space=pltpu.MemorySpace.HBM) for b in bufs]

    @pl.kernel(mesh=plsc.ScalarSubcoreMesh(axis_name="sequencer", num_cores=1), name=name,
               scratch_types=[pltpu.SemaphoreType.DMA((n_t,))] * 4,
               compiler_params=pltpu.CompilerParams(collective_id=collective_id))
    def launch(send_ici, recv_ici, send_d2d, recv_d2d):
        x, y, c, sibling, chips = _place()
        p_me = 2 * x + y
        peers = [sibling] + [(cx, cy, c) for cx, cy in chips]
        barrier = pltpu.get_barrier_semaphore()
        for peer in peers:
            pl.semaphore_signal(barrier, inc=1, device_id=peer, device_id_type=MESH)
        pl.semaphore_wait(barrier, len(peers))

        def waiter(t, sem_s, sem_r):
            win = refs[t].at[pl.ds(0, 3), pl.ds(0, dims[t][0] // 2), :]
            return pltpu.make_async_remote_copy(src_ref=win, dst_ref=win, send_sem=sem_s.at[t], recv_sem=sem_r.at[t],
                                                device_id=sibling, device_id_type=MESH)

        def half_copy(t, chip_idx, to, sem_s, sem_r):
            r2 = dims[t][0] // 2
            win = refs[t].at[chip_idx, pl.ds(c * r2, r2), :]
            return pltpu.make_async_remote_copy(src_ref=win, dst_ref=win, send_sem=sem_s.at[t], recv_sem=sem_r.at[t],
                                                device_id=to, device_id_type=MESH)

        for t in range(n_t):
            for cx, cy in chips:
                half_copy(t, p_me, (cx, cy, c), send_ici, recv_ici).start()
        for t in range(n_t):
            waiter(t, send_ici, recv_ici).wait_recv()
            for cx, cy in chips:
                half_copy(t, 2 * cx + cy, sibling, send_d2d, recv_d2d).start()
        for t in range(n_t):
            waiter(t, send_d2d, recv_d2d).wait_recv()
            waiter(t, send_ici, recv_ici).wait_send()
            waiter(t, send_d2d, recv_d2d).wait_send()

    launch()
    return [r[...] for r in refs]


def _sequencer_allgather8(block, dev, name, collective_id):
    m_per, n = block.shape
    src = jax.new_ref(block, memory_space=pltpu.MemorySpace.HBM)
    out = jax.empty_ref(jax.ShapeDtypeStruct((N_DEV * m_per, n), block.dtype), memory_space=pltpu.MemorySpace.HBM)

    @pl.kernel(mesh=plsc.ScalarSubcoreMesh(axis_name="sequencer", num_cores=1), name=name,
               scratch_types=[pltpu.SemaphoreType.DMA((7,))] * 2,
               compiler_params=pltpu.CompilerParams(collective_id=collective_id))
    def launch(send_sems, recv_sems):
        x, y, c, sibling, chips = _place()
        me = (x, y, c)
        _handshake([sibling] + [(cx, cy, c) for cx, cy in chips])

        def rows(px, py, pc):
            return out.at[pl.ds((4 * px + 2 * py + pc) * m_per, m_per), :]

        def copy(k, blk, to, from_src=False):
            return pltpu.make_async_remote_copy(
                src_ref=src if from_src else rows(*blk), dst_ref=rows(*blk),
                send_sem=send_sems.at[k], recv_sem=recv_sems.at[k], device_id=to, device_id_type=MESH)

        first = [copy(0, me, sibling, True)] + [copy(1 + j, me, (*chip, c), True) for j, chip in enumerate(chips)]
        for cp in first:
            cp.start()
        passed = [copy(4 + j, (*chip, c), sibling) for j, chip in enumerate(chips)]
        for j, chip in enumerate(chips):
            copy(1 + j, (*chip, c), me).wait_recv()
            passed[j].start()
        copy(0, sibling, me).wait_recv()
        for j, chip in enumerate(chips):
            copy(4 + j, (*chip, 1 - c), me).wait_recv()
        for cp in first + passed:
            cp.wait_send()

    launch()
    return lax.dynamic_update_slice(out[...], block, (dev * m_per, 0))


def _others():
    x, y, c = lax.axis_index("x"), lax.axis_index("y"), lax.axis_index("c")
    flip = lambda v, f: 1 - v if f else v
    return [(flip(x, fx), flip(y, fy), flip(c, fc))
            for fx in (0, 1) for fy in (0, 1) for fc in (0, 1) if (fx, fy, fc) != (0, 0, 0)]


def _handshake(peers):
    barrier = pltpu.get_barrier_semaphore()
    for peer in peers:
        pl.semaphore_signal(barrier, inc=1, device_id=peer, device_id_type=MESH)
    pl.semaphore_wait(barrier, len(peers))


def _sequencer_scatter(parts, name, collective_id):
    n_t = len(parts)
    dims = [p.shape[1:] for p in parts]
    srcs = [jax.new_ref(p, memory_space=pltpu.MemorySpace.HBM) for p in parts]
    inboxes = [jax.empty_ref(jax.ShapeDtypeStruct((N_DEV, r // 2, cols), BF16), memory_space=pltpu.MemorySpace.HBM)
               for r, cols in dims]

    @pl.kernel(mesh=plsc.ScalarSubcoreMesh(axis_name="sequencer", num_cores=1), name=name,
               scratch_types=[pltpu.SemaphoreType.DMA((n_t,))] * 2,
               compiler_params=pltpu.CompilerParams(collective_id=collective_id))
    def launch(send_sem, recv_sem):
        x, y, c = lax.axis_index("x"), lax.axis_index("y"), lax.axis_index("c")
        me = 4 * x + 2 * y + c
        peers = _others()
        _handshake(peers)
        for t in range(n_t):
            h = dims[t][0] // 2
            for qx, qy, qc in peers:
                pltpu.make_async_remote_copy(
                    src_ref=srcs[t].at[2 * qx + qy, pl.ds(qc * h, h), :], dst_ref=inboxes[t].at[me],
                    send_sem=send_sem.at[t], recv_sem=recv_sem.at[t], device_id=(qx, qy, qc), device_id_type=MESH).start()
        for t in range(n_t):
            win = inboxes[t].at[pl.ds(0, N_DEV - 1)]
            both = pltpu.make_async_remote_copy(src_ref=win, dst_ref=win, send_sem=send_sem.at[t],
                                                recv_sem=recv_sem.at[t], device_id=peers[0], device_id_type=MESH)
            both.wait_recv()
            both.wait_send()

    launch()
    return [b[...] for b in inboxes]


def _sum_pieces(part, inbox, place, name):
    _, r, cols = part.shape
    h = r // 2
    rows = _chunk_rows(h, cols)
    steps = h // rows

    def body(place_ref, own_ref, in_ref, o_ref):
        dev = place_ref[2]
        own = own_ref[...].astype(F32)
        acc = jnp.zeros((rows, cols), F32)
        for d in range(N_DEV):
            acc = acc + jnp.where(dev == d, own, in_ref[d].astype(F32))
        o_ref[...] = acc

    return pl.pallas_call(
        body, name=name,
        grid_spec=pltpu.PrefetchScalarGridSpec(
            num_scalar_prefetch=1, grid=(steps,),
            in_specs=[pl.BlockSpec((None, rows, cols), lambda i, pr: (pr[0], pr[1] * steps + i, 0)),
                      pl.BlockSpec((N_DEV, rows, cols), lambda i, pr: (0, i, 0))],
            out_specs=pl.BlockSpec((rows, cols), lambda i, pr: (pr[1] * steps + i, 0))),
        out_shape=jax.ShapeDtypeStruct((r, cols), F32),
    )(place, part, inbox)


def _sequencer_swap_halves(halves, name, collective_id):
    n_t = len(halves)
    refs = [jax.new_ref(a, memory_space=pltpu.MemorySpace.HBM) for a in halves]

    @pl.kernel(mesh=plsc.ScalarSubcoreMesh(axis_name="sequencer", num_cores=1), name=name,
               scratch_types=[pltpu.SemaphoreType.DMA((n_t,))] * 2,
               compiler_params=pltpu.CompilerParams(collective_id=collective_id))
    def launch(send_sem, recv_sem):
        x, y, c = lax.axis_index("x"), lax.axis_index("y"), lax.axis_index("c")
        sibling = (x, y, 1 - c)
        _handshake([sibling])
        copies = []
        for t in range(n_t):
            h = halves[t].shape[0] // 2
            win = refs[t].at[pl.ds(c * h, h), :]
            copies.append(pltpu.make_async_remote_copy(src_ref=win, dst_ref=win, send_sem=send_sem.at[t],
                                                       recv_sem=recv_sem.at[t], device_id=sibling, device_id_type=MESH))
            copies[-1].start()
        for cp in copies:
            cp.wait()

    launch()
    return [r[...] for r in refs]


def _cond_rows(c16, w, act, name):
    n_l, dm, wid = w.shape

    def body(c_ref, w_ref, o_ref, a_ref):
        cv = c_ref[...]
        if act:
            cv = cv * _sig(cv)
        a_ref[...] = cv
        o_ref[...] = _dot_f32(cv, w_ref[...])

    return pl.pallas_call(
        body, name=name, grid=(n_l,),
        in_specs=[_full((16, dm)), pl.BlockSpec((None, dm, wid), lambda l: (l, 0, 0))],
        out_specs=[pl.BlockSpec((None, 16, wid), lambda l: (l, 0, 0)), _full((16, dm))],
        out_shape=[jax.ShapeDtypeStruct((n_l, 16, wid), F32), jax.ShapeDtypeStruct((16, dm), F32)],
    )(c16, w)


def _outer_grad(ct, dm, name):
    n_l, kk, wid = dm.shape
    d_rows = ct.shape[0]

    def body(c_ref, d_ref, o_ref):
        o_ref[...] = _dot_f32(c_ref[...], d_ref[...])

    return pl.pallas_call(
        body, name=name, grid=(n_l,),
        in_specs=[_full((d_rows, kk)), pl.BlockSpec((None, kk, wid), lambda l: (l, 0, 0))],
        out_specs=pl.BlockSpec((None, d_rows, wid), lambda l: (l, 0, 0)),
        out_shape=jax.ShapeDtypeStruct((n_l, d_rows, wid), F32),
    )(ct, dm)


def _sum_devices(g, name):
    rows, n = g.shape

    def body(g_ref, o_ref):
        acc = g_ref[0:SUBLANES, :]
        for dev in range(1, N_DEV):
            acc = acc + g_ref[dev * SUBLANES:(dev + 1) * SUBLANES, :]
        o_ref[...] = acc

    return pl.pallas_call(body, name=name, out_shape=jax.ShapeDtypeStruct((SUBLANES, n), F32))(g)


def _adamw(w, g, m, v, name):
    shape = w.shape
    cols = shape[-1]
    rows = w.size // cols
    tr = rows
    for cand in range(SUBLANES, min(rows, 256) + 1, SUBLANES):
        if rows % cand == 0:
            tr = cand
    if rows * cols <= COMM_CHUNK_ELEMS:
        tr = rows
    c1 = 1.0 / (1.0 - ADAM_B1 ** ADAM_STEP)
    c2 = 1.0 / (1.0 - ADAM_B2 ** ADAM_STEP)

    def body(w_ref, g_ref, m_ref, v_ref, d_ref, mo_ref, vo_ref):
        gv = g_ref[...]
        m_new = ADAM_B1 * m_ref[...] + (1.0 - ADAM_B1) * gv
        v_new = ADAM_B2 * v_ref[...] + (1.0 - ADAM_B2) * (gv * gv)
        mo_ref[...] = m_new
        vo_ref[...] = v_new
        d_ref[...] = -ADAM_LR * ((m_new * c1) / (jnp.sqrt(v_new * c2) + ADAM_EPS) + ADAM_WD * w_ref[...])

    spec = pl.BlockSpec((tr, cols), lambda i: (i, 0))
    outs = pl.pallas_call(
        body, name=name, grid=(rows // tr,), in_specs=[spec] * 4, out_specs=[spec] * 3,
        out_shape=[jax.ShapeDtypeStruct((rows, cols), F32)] * 3,
    )(*[a.reshape(rows, cols) for a in (w, g, m, v)])
    return tuple(o.reshape(shape) for o in outs)


def _pad_cols(a, cols):
    return jnp.pad(a, [(0, 0)] * (a.ndim - 1) + [(0, cols - a.shape[-1])])


def _flat8(parts, width):
    v = jnp.concatenate([p.reshape(-1) for p in parts])
    return jnp.pad(v, (0, width - v.shape[0])).reshape(SUBLANES, width // SUBLANES)


KV_SHARD = 514
KV_SHARD_PAD = 640
BIG = ("a_w_in", "a_w_out", "kv_w", "b_w_q", "b_w_out", "up0", "up1", "down0", "down1")


def kernel(x, c, ada_w, ada_b, a_w_in, a_lb_logits, a_norm_g, a_w_out, kv_ada_w, kv_ada_b, kv_w, kv_b_f, k_norm_g, b_w_q, q_norm_g, b_w_out, ffn_w_up, ffn_conv_w, ffn_conv_b, ffn_w_down, loss_target, m_ada_w, m_ada_b, m_a_w_in, m_a_lb_logits, m_a_norm_g, m_a_w_out, m_kv_ada_w, m_kv_ada_b, m_kv_w, m_kv_b_f, m_k_norm_g, m_b_w_q, m_q_norm_g, m_b_w_out, m_ffn_w_up, m_ffn_conv_w, m_ffn_conv_b, m_ffn_w_down, v_ada_w, v_ada_b, v_a_w_in, v_a_lb_logits, v_a_norm_g, v_a_w_out, v_kv_ada_w, v_kv_ada_b, v_kv_w, v_kv_b_f, v_k_norm_g, v_b_w_q, v_q_norm_g, v_b_w_out, v_ffn_w_up, v_ffn_conv_w, v_ffn_conv_b, v_ffn_w_down):
    dm, ff = D_MODEL, D_FF
    ix, iy, ic = lax.axis_index("x"), lax.axis_index("y"), lax.axis_index("c")
    chip = 2 * ix + iy
    dev = 2 * chip + ic

    w1 = 10240
    g1 = _allgather8(_flat8([c, a_lb_logits, ffn_conv_w], w1), "gather_cond").reshape(N_DEV, w1)
    c_all = g1[:, :dm]
    per_chip = g1[0::2]
    lb_logits = per_chip[:, dm:dm + 512].reshape(N_CHIPS, 2, 256).transpose(1, 0, 2).reshape(2, dm)
    conv_w = per_chip[:, dm + 512:dm + 512 + 2 * CONV_W * FFN_COLS].reshape(N_CHIPS, 2, CONV_W, FFN_COLS)
    conv_w = conv_w.transpose(1, 2, 0, 3).reshape(2, CONV_W, 2, ff).transpose(0, 2, 1, 3)
    conv_b = ffn_conv_b.reshape(2, 2, 1, ff)
    lb = jax.nn.softmax(lb_logits, axis=0)[0:1]

    c16 = jnp.pad(c_all, ((0, 8), (0, 0)))
    mod_ada, c_act16 = _cond_rows(c16, ada_w, True, "mod_ada")
    mod_kv, _ = _cond_rows(c16, kv_ada_w[None], True, "mod_kv")
    mine = jnp.concatenate([mod_ada[0, :8], mod_ada[1, :8], mod_kv[0, :8]], axis=1)
    w2 = mine.shape[1]
    g2 = _allgather8(mine, "gather_mod").reshape(N_DEV, 8, w2)[0::2]
    my_rows = lax.dynamic_index_in_dim(g2, dev, axis=1, keepdims=False)
    mod0 = my_rows[:, 0:1536].reshape(6 * dm) + ada_b[0]
    mod1 = my_rows[:, 1536:3072].reshape(6 * dm) + ada_b[1]
    modk = my_rows[:, 3072:3584].reshape(2 * dm) + kv_ada_b
    mods = {"l0": [v.reshape(1, dm) for v in jnp.split(mod0, 6)],
            "l1": [v.reshape(1, dm) for v in jnp.split(mod1, 6)],
            "kv": [v.reshape(1, dm) for v in jnp.split(modk, 2)]}

    local = [(a_w_in, 0), (a_w_out, 0), (_pad_cols(kv_w, KV_SHARD_PAD)[None], 0), (b_w_q, 0), (b_w_out, 0),
             (ffn_w_up, 0), (ffn_w_up, 1), (ffn_w_down, 0), (ffn_w_down, 1)]
    chip_arr = chip.reshape(1).astype(jnp.int32)
    own = {n: _cast_own_block(w, layer, chip_arr, "cast_" + n) for n, (w, layer) in zip(BIG, local)}
    stages = {"mixer0": ("a_w_in",), "ffn0": ("a_w_out", "up0", "down0"),
              "layer1": ("kv_w", "b_w_q", "b_w_out", "up1", "down1")}
    arriving = {st: _sequencer_gather([own[n] for n in names], "gather_" + st, cid)
                for cid, (st, names) in enumerate(stages.items(), start=1)}
    rowwise = lambda g: g.reshape(1, -1, dm)

    def weights_at(stage, token):
        got, token = lax.optimization_barrier((arriving[stage], token))
        g = dict(zip(stages[stage], got))
        if stage == "mixer0":
            return {"a_w_in": g["a_w_in"]}, token
        if stage == "ffn0":
            return {"a_w_out": rowwise(g["a_w_out"]), "up0": g["up0"], "down0": rowwise(g["down0"])}, token
        kv_full = g["kv_w"][:, :, :KV_SHARD].transpose(1, 0, 2).reshape(dm, N_CHIPS * KV_SHARD)
        return {"kv_k": kv_full[None, :, :dm], "kv_v": kv_full[None, :, dm:2 * dm],
                "kv_f": _pad_cols(kv_full[None, :, 2 * dm:], LANES), "b_w_q": g["b_w_q"],
                "b_w_out": rowwise(g["b_w_out"]), "up1": g["up1"], "down1": rowwise(g["down1"])}, token

    vecs = {"a_norm_g": jnp.tile(a_norm_g, (1, HEADS)), "k_norm_g": jnp.tile(k_norm_g[None], (1, HEADS)),
            "q_norm_g": jnp.tile(q_norm_g, (1, HEADS)), "kv_b_f": _pad_cols(kv_b_f[None], LANES),
            "conv_w0": conv_w[0], "conv_b0": conv_b[0], "conv_w1": conv_w[1], "conv_b1": conv_b[1]}

    sq, grad_x, big, small, marks = _local_step(x[0], loss_target[0], mods, lb, vecs, weights_at)
    loss = lax.psum(0.5 * jnp.sum(sq) / dm, ("x", "y", "c"))

    kv_grad = jnp.concatenate([big["kv_k"][0], big["kv_v"][0], big["kv_f"][0][:, :HEADS]], axis=1)
    kv_grad = _pad_cols(kv_grad.reshape(dm, N_CHIPS, KV_SHARD).transpose(1, 0, 2), KV_SHARD_PAD)
    chipwise = lambda g: g.reshape(N_CHIPS, -1, dm)
    parts = dict(zip(BIG, [big["a_w_in"], chipwise(big["a_w_out"]), kv_grad, big["b_w_q"], chipwise(big["b_w_out"]),
                           big["up0"], big["up1"], chipwise(big["down0"]), chipwise(big["down1"])]))
    place = jnp.stack([chip, ic, dev]).astype(jnp.int32)

    served = []
    boxes = {}

    groups = (("up1", "down1"), ("b_w_out", "b_w_q", "kv_w"), ("up0", "down0", "a_w_out"), ("a_w_in",))

    def scatter_group(k):
        mine = [parts[n] for n in groups[k]]
        if served:
            mine, _ = lax.optimization_barrier((mine, served[-1]))
        boxes[k] = _sequencer_scatter(mine, "scatter_grads_%d" % k, 4 + k)
        served.append(boxes[k])

    def sum_group(k, token):
        inboxes, _ = lax.optimization_barrier((boxes[k], token))
        return [_sum_pieces(parts[n], box, place, "sum_" + n) for n, box in zip(groups[k], inboxes)]

    def swap_group(k, halves, behind):
        halves, _ = lax.optimization_barrier((halves, behind))
        return dict(zip(groups[k], _sequencer_swap_halves(halves, "swap_grads_%d" % k, 8 + k)))

    for k in range(3):
        scatter_group(k)
    halves = [sum_group(0, marks["attention_bwd"]), sum_group(1, marks["ffn0_bwd"]), sum_group(2, marks["mixer0_bwd"])]

    fold = lambda a: a.sum(axis=0)
    heads = lambda a: fold(a).reshape(HEADS, HEAD_DIM).sum(axis=0)
    conv_flat = lambda a: a.sum(axis=2).transpose(1, 0, 2)
    pieces = ([fold(a) for a in small["mod_l0"]] + [fold(a) for a in small["mod_l1"]] + [fold(a) for a in small["mod_kv"]]
              + [conv_flat(small["conv0"]), conv_flat(small["conv1"]), heads(small["a_norm_g"]), heads(small["k_norm_g"]),
                 heads(small["q_norm_g"]), fold(small["kv_b_f"]), fold(small["lb"])])
    w3 = 61440
    small_vec, _ = lax.optimization_barrier((_flat8(pieces, w3), served[2]))
    g3 = _sequencer_allgather8(small_vec, dev, "gather_small", 12)
    served.append(g3)
    scatter_group(3)
    rs = {}
    for k in range(3):
        rs.update(swap_group(k, halves[k], g3))
    tot = _sum_devices(g3, "sum_small").reshape(w3)
    n_mod = 14 * dm
    dmod_all = g3.reshape(N_DEV, w3)[:, :n_mod]
    o = n_mod
    conv_tot = [tot[o + l * 8 * ff: o + (l + 1) * 8 * ff].reshape(4, 2 * ff) for l in range(2)]
    o += 16 * ff
    g_a_norm, g_k_norm, g_q_norm = (tot[o + i * HEAD_DIM: o + (i + 1) * HEAD_DIM] for i in range(3))
    o += 3 * HEAD_DIM
    g_kv_b_f = tot[o:o + HEADS]
    dlb = tot[o + LANES:o + LANES + dm]

    ct = _pad_cols(c_act16[:8].T, LANES)
    dmod_pad = jnp.pad(dmod_all, ((0, LANES - N_DEV), (0, 0)))
    cols_ada = jnp.stack([lax.dynamic_slice_in_dim(dmod_pad, l * 6 * dm + chip * 1536, 1536, axis=1) for l in range(2)])
    cols_kv = lax.dynamic_slice_in_dim(dmod_pad, 12 * dm + chip * 512, 512, axis=1)[None]
    g_ada_w = _outer_grad(ct, cols_ada, "grad_ada_w")
    g_kv_ada_w = _outer_grad(ct, cols_kv, "grad_kv_ada_w")[0]

    my_lb = lax.dynamic_slice_in_dim(lb[0], chip * 256, 256)
    l0 = lax.dynamic_slice_in_dim(dlb, chip * 256, 256) * my_lb * (1.0 - my_lb)
    grads = {
        "ada_w": g_ada_w, "ada_b": jnp.stack([tot[:6 * dm], tot[6 * dm:12 * dm]]),
        "a_lb_logits": jnp.stack([l0, -l0]), "a_norm_g": g_a_norm[None],
        "a_w_out": rs["a_w_out"][None], "kv_ada_w": g_kv_ada_w, "kv_ada_b": tot[12 * dm:14 * dm],
        "kv_w": rs["kv_w"][:, :KV_SHARD], "kv_b_f": g_kv_b_f, "k_norm_g": g_k_norm,
        "b_w_q": rs["b_w_q"][None], "q_norm_g": g_q_norm[None], "b_w_out": rs["b_w_out"][None],
        "ffn_w_up": jnp.stack([rs["up0"], rs["up1"]]),
        "ffn_conv_w": jnp.stack([lax.dynamic_slice_in_dim(ct_l[:CONV_W], chip * FFN_COLS, FFN_COLS, axis=1) for ct_l in conv_tot]),
        "ffn_conv_b": jnp.stack([ct_l[CONV_W] for ct_l in conv_tot]),
        "ffn_w_down": jnp.stack([rs["down0"], rs["down1"]]),
    }
    weights = dict(ada_w=ada_w, ada_b=ada_b, a_w_in=a_w_in, a_lb_logits=a_lb_logits, a_norm_g=a_norm_g, a_w_out=a_w_out,
                   kv_ada_w=kv_ada_w, kv_ada_b=kv_ada_b, kv_w=kv_w, kv_b_f=kv_b_f, k_norm_g=k_norm_g, b_w_q=b_w_q,
                   q_norm_g=q_norm_g, b_w_out=b_w_out, ffn_w_up=ffn_w_up, ffn_conv_w=ffn_conv_w, ffn_conv_b=ffn_conv_b,
                   ffn_w_down=ffn_w_down)
    m_in = dict(ada_w=m_ada_w, ada_b=m_ada_b, a_w_in=m_a_w_in, a_lb_logits=m_a_lb_logits, a_norm_g=m_a_norm_g,
                a_w_out=m_a_w_out, kv_ada_w=m_kv_ada_w, kv_ada_b=m_kv_ada_b, kv_w=m_kv_w, kv_b_f=m_kv_b_f,
                k_norm_g=m_k_norm_g, b_w_q=m_b_w_q, q_norm_g=m_q_norm_g, b_w_out=m_b_w_out, ffn_w_up=m_ffn_w_up,
                ffn_conv_w=m_ffn_conv_w, ffn_conv_b=m_ffn_conv_b, ffn_w_down=m_ffn_w_down)
    v_in = dict(ada_w=v_ada_w, ada_b=v_ada_b, a_w_in=v_a_w_in, a_lb_logits=v_a_lb_logits, a_norm_g=v_a_norm_g,
                a_w_out=v_a_w_out, kv_ada_w=v_kv_ada_w, kv_ada_b=v_kv_ada_b, kv_w=v_kv_w, kv_b_f=v_kv_b_f,
                k_norm_g=v_k_norm_g, b_w_q=v_b_w_q, q_norm_g=v_q_norm_g, b_w_out=v_b_w_out, ffn_w_up=v_ffn_w_up,
                ffn_conv_w=v_ffn_conv_w, ffn_conv_b=v_ffn_conv_b, ffn_w_down=v_ffn_w_down)

    names = list(weights)
    step = lambda n: _adamw(weights[n], grads[n], m_in[n], v_in[n], "adamw_" + n)
    grads = {n: g.reshape(weights[n].shape) for n, g in grads.items()}
    upd = {n: step(n) for n in names if n != "a_w_in"}
    last = sum_group(3, [u[0] for u in upd.values()])
    grads["a_w_in"] = swap_group(3, last, last)["a_w_in"][None]
    upd["a_w_in"] = step("a_w_in")
    return (loss, grad_x[None], *[grads[n] for n in names], *[upd[n][0] for n in names],
            *[upd[n][1] for n in names], *[upd[n][2] for n in names])
```

```python
import jax
import jax.numpy as jnp
from jax import lax
from jax.experimental import pallas as pl
from jax.experimental.pallas import tpu as pltpu
from jax.experimental.pallas import tpu_sc as plsc

F32 = jnp.float32
BF16 = jnp.bfloat16

D_MODEL = 1024
HEADS = 8
HEAD_DIM = 128
A_CHUNK = 64
D_FF = 2816
CONV_W = 3
EPS = 1e-6
NEG_INF = -1e30
N_CHIPS = 4
N_DEV = 8

ADAM_LR = 0.001
ADAM_B1 = 0.9
ADAM_B2 = 0.999
ADAM_EPS = 1e-08
ADAM_WD = 0.01
ADAM_STEP = 10

SUBLANES = 8
BF16_ROWS = 16
LANES = 128
HALO = BF16_ROWS
ROW_TILE = 512
TOKEN_TILE_TN = 2048
FFN_COLS = 1408
FFN_ROWS = 256
HGRN_ROWS = 256
ATT_TILE = 512
ATT_SPLIT = 2
ATT_FWD_HEADS = 8
ATT_BWD_HEADS = 4
MESH = pl.DeviceIdType.MESH


def _sig(x):
    return jax.nn.sigmoid(x)


def _dot(a, b):
    return jnp.dot(a, b, preferred_element_type=F32)


def _dot_nt(a, b):
    return lax.dot_general(a, b, (((1,), (1,)), ((), ())), preferred_element_type=F32)


def _dot_tn(a, b):
    return lax.dot_general(a, b, (((0,), (0,)), ((), ())), preferred_element_type=F32)


def _split2(x):
    hi = x.astype(BF16)
    lo = (x - hi.astype(F32)).astype(BF16)
    return hi, lo


def _dot_f32(a, b):
    ah, al = _split2(a)
    bh, bl = _split2(b)
    return _dot(ah, bh) + _dot(ah, bl) + _dot(al, bh)


def _tri_dot(tri, x):
    hi = x.astype(BF16)
    r = x - hi.astype(F32)
    mid = r.astype(BF16)
    lo = (r - mid.astype(F32)).astype(BF16)
    return _dot(tri, hi) + _dot(tri, mid) + _dot(tri, lo)


def _tri(n, upper=False):
    r = lax.broadcasted_iota(jnp.int32, (n, n), 0)
    c = lax.broadcasted_iota(jnp.int32, (n, n), 1)
    keep = (c >= r) if upper else (c <= r)
    return jnp.where(keep, 1.0, 0.0).astype(BF16)


def _colsum8(v):
    rows, n = v.shape
    return v.reshape(rows // SUBLANES, SUBLANES, n).sum(axis=0)


def _full(shape):
    nd = len(shape)
    return pl.BlockSpec(shape, lambda *_: (0,) * nd)


def _tile(n, want):
    t = min(n, want)
    assert n % t == 0, (n, t)
    return t


def _mm_nn(a, w, groups, out_dtype, name):
    m_rows, k = a.shape
    p_n, _, n = w.shape
    per = p_n // groups
    tm = _tile(m_rows, ROW_TILE)

    def body(a_ref, w_ref, o_ref):
        av = a_ref[...]
        for p in range(p_n):
            o_ref[p // per, :, (p % per) * n:(p % per + 1) * n] = _dot(av, w_ref[p]).astype(out_dtype)

    return pl.pallas_call(
        body, name=name, grid=(m_rows // tm,),
        in_specs=[pl.BlockSpec((tm, k), lambda i: (i, 0)), _full((p_n, k, n))],
        out_specs=pl.BlockSpec((groups, tm, per * n), lambda i: (0, i, 0)),
        out_shape=jax.ShapeDtypeStruct((groups, m_rows, per * n), out_dtype),
    )(a, w)


def _mm_nt(d, w, out_dtype, name):
    g_n, m_rows, _ = d.shape
    p_n, k, n = w.shape
    per = p_n // g_n
    tm = _tile(m_rows, ROW_TILE)

    def body(d_ref, w_ref, o_ref):
        acc = None
        for p in range(p_n):
            t = _dot_nt(d_ref[p // per, :, (p % per) * n:(p % per + 1) * n], w_ref[p])
            acc = t if acc is None else acc + t
        o_ref[...] = acc.astype(out_dtype)

    return pl.pallas_call(
        body, name=name, grid=(m_rows // tm,),
        in_specs=[pl.BlockSpec((g_n, tm, per * n), lambda i: (0, i, 0)), _full((p_n, k, n))],
        out_specs=pl.BlockSpec((tm, k), lambda i: (i, 0)),
        out_shape=jax.ShapeDtypeStruct((m_rows, k), out_dtype),
    )(d, w)


def _mm_tn(a, d, p_n, name):
    m_rows, k = a.shape
    g_n, _, w_cols = d.shape
    per = p_n // g_n
    n = w_cols // per
    tm = _tile(m_rows, TOKEN_TILE_TN if k <= D_MODEL else ROW_TILE)
    steps = m_rows // tm

    def body(a_ref, d_ref, o_ref, acc):
        m = pl.program_id(1)

        @pl.when(m == 0)
        def _():
            acc[...] = jnp.zeros_like(acc)

        acc[...] += _dot_tn(a_ref[...], d_ref[...])

        @pl.when(m == steps - 1)
        def _():
            o_ref[...] = acc[...].astype(BF16)

    return pl.pallas_call(
        body, name=name, grid=(p_n, steps),
        in_specs=[pl.BlockSpec((tm, k), lambda p, m: (m, 0)),
                  pl.BlockSpec((None, tm, n), lambda p, m: (p // per, m, p % per))],
        out_specs=pl.BlockSpec((None, k, n), lambda p, m: (p, 0, 0)),
        out_shape=jax.ShapeDtypeStruct((p_n, k, n), BF16),
        scratch_shapes=[pltpu.VMEM((k, n), F32)],
    )(a, d)


def _premix(x, shift, scale, name):
    s, dm = x.shape
    tm = _tile(s, ROW_TILE)

    def body(x_ref, sh_ref, sc_ref, h_ref):
        xv = x_ref[...]
        inv = lax.rsqrt(jnp.mean(xv * xv, axis=-1, keepdims=True) + EPS)
        h_ref[...] = (xv * inv * (1.0 + sc_ref[...]) + sh_ref[...]).astype(BF16)

    row = pl.BlockSpec((tm, dm), lambda i: (i, 0))
    vec = _full((1, dm))
    return pl.pallas_call(body, name=name, grid=(s // tm,), in_specs=[row, vec, vec], out_specs=row,
                          out_shape=jax.ShapeDtypeStruct((s, dm), BF16))(x, shift, scale)


def _premix_bwd(x, terms, dres, name, branch=None):
    s, dm = x.shape
    tm = _tile(s, ROW_TILE)
    pairs = [pr for _, prs in terms for pr in prs]
    n_in = 2 + len(terms) + 2 * len(pairs) + (2 if branch else 0)

    def body(*refs):
        x_ref, dres_ref = refs[:2]
        sc_refs = refs[2:2 + len(terms)]
        mm_refs = refs[2 + len(terms):2 + len(terms) + 2 * len(pairs)]
        outs = refs[n_in:]

        @pl.when(pl.program_id(0) == 0)
        def _():
            for o in outs[1:1 + 2 * len(terms)]:
                o[...] = jnp.zeros_like(o)
            if branch:
                outs[-1][...] = jnp.zeros_like(outs[-1])

        xv = x_ref[...]
        inv = lax.rsqrt(jnp.mean(xv * xv, axis=-1, keepdims=True) + EPS)
        r = xv * inv
        dx = dres_ref[...]
        k = 0
        for t, (_, prs) in enumerate(terms):
            dh = None
            for d, w in prs:
                d_ref, w_ref = mm_refs[2 * k], mm_refs[2 * k + 1]
                k += 1
                p_n, _, n = w.shape
                per = p_n // d.shape[0]
                for p in range(p_n):
                    part = _dot_nt(d_ref[p // per, :, (p % per) * n:(p % per + 1) * n], w_ref[p])
                    dh = part if dh is None else dh + part
            dr = dh * (1.0 + sc_refs[t][...])
            dx = dx + inv * (dr - r * jnp.mean(dr * r, axis=-1, keepdims=True))
            outs[1 + 2 * t][...] += _colsum8(dh)
            outs[2 + 2 * t][...] += _colsum8(dh * r)
        outs[0][...] = dx
        if branch:
            y_ref, g_ref = refs[n_in - 2:n_in]
            outs[-2][0] = (dx * g_ref[...]).astype(BF16)
            outs[-1][...] += _colsum8(dx * y_ref[...])

    row = pl.BlockSpec((tm, dm), lambda i: (i, 0))
    vec, acc = _full((1, dm)), _full((SUBLANES, dm))
    ins, specs = [x, dres] + [sc for sc, _ in terms], [row, row] + [vec] * len(terms)
    for d, w in pairs:
        ins += [d, w]
        specs += [pl.BlockSpec((d.shape[0], tm, d.shape[2]), lambda i: (0, i, 0)), _full(w.shape)]
    out_shape = [jax.ShapeDtypeStruct((s, dm), F32)] + [jax.ShapeDtypeStruct((SUBLANES, dm), F32)] * (2 * len(terms))
    out_specs = [row] + [acc] * (2 * len(terms))
    if branch:
        ins += list(branch)
        specs += [row, vec]
        out_shape += [jax.ShapeDtypeStruct((1, s, dm), BF16), jax.ShapeDtypeStruct((SUBLANES, dm), F32)]
        out_specs += [pl.BlockSpec((1, tm, dm), lambda i: (0, i, 0)), acc]
    outs = pl.pallas_call(body, name=name, grid=(s // tm,), in_specs=specs, out_specs=out_specs,
                          out_shape=out_shape)(*ins)
    partials = [(outs[1 + 2 * t], outs[2 + 2 * t]) for t in range(len(terms))]
    return (outs[0], partials) + ((outs[-2], outs[-1]) if branch else ())


def _conv_taps(e, w, b):
    return w[2:3] * e + w[1:2] * pltpu.roll(e, 1, 0) + w[0:1] * pltpu.roll(e, 2, 0) + b


def _ffn_specs(s, tm, cb):
    hb = tm // HALO
    last = s // HALO - 1
    main = pl.BlockSpec((2, tm, cb), lambda j, i: (0, i, j))
    prev = pl.BlockSpec((2, HALO, cb), lambda j, i: (0, jnp.maximum(i * hb - 1, 0), j))
    nxt = pl.BlockSpec((2, HALO, cb), lambda j, i: (0, jnp.minimum((i + 1) * hb, last), j))
    wspec = pl.BlockSpec((2, CONV_W, cb), lambda j, i: (0, 0, j))
    bspec = pl.BlockSpec((2, 1, cb), lambda j, i: (0, 0, j))
    return main, prev, nxt, wspec, bspec


def _convglu_bwd(u, dffn, w_down, w, b, name):
    _, s, f = u.shape
    dm = dffn.shape[2]
    tm = _tile(s, 256)
    cb = _tile(f, FFN_COLS)
    steps = s // tm
    n_ext = tm + 2 * HALO
    main, prev, nxt, wspec, bspec = _ffn_specs(s, tm, cb)
    hb = tm // HALO
    last = s // HALO - 1
    d_main = pl.BlockSpec((None, tm, dm), lambda j, i: (0, i, 0))
    d_next = pl.BlockSpec((None, HALO, dm), lambda j, i: (0, jnp.minimum((i + 1) * hb, last), 0))
    wd_spec = pl.BlockSpec((None, cb, dm), lambda j, i: (0, j, 0))

    def body(u_ref, up_ref, un_ref, d_ref, dn_ref, wd_ref, w_ref, b_ref, du_ref, acc_ref):
        i = pl.program_id(1)
        first = jnp.where(i > 0, 1.0, 0.0)
        notlast = jnp.where(i < steps - 1, 1.0, 0.0)

        @pl.when(i == 0)
        def _():
            acc_ref[...] = jnp.zeros_like(acc_ref)

        def ext(g):
            return jnp.concatenate([up_ref[g].astype(F32) * first, u_ref[g].astype(F32), un_ref[g].astype(F32)], axis=0)

        ug, uv = ext(0), ext(1)
        gate = _conv_taps(ug, w_ref[0], b_ref[0])
        val = _conv_taps(uv, w_ref[1], b_ref[1])
        wd = wd_ref[...]
        da = _dot_nt(d_ref[...], wd).astype(BF16).astype(F32)
        da_next = _dot_nt(dn_ref[...], wd).astype(BF16).astype(F32) * notlast
        da_e = jnp.concatenate([jnp.zeros((HALO, cb), F32), da, da_next], axis=0)
        sg = _sig(gate)
        d_val = da_e * gate * sg
        d_gate = da_e * val * (sg * (1.0 + gate * (1.0 - sg)))

        def finish(g, d, e):
            wv = w_ref[g]
            rows = slice(HALO, HALO + tm)
            d1, d2 = pltpu.roll(d, n_ext - 1, 0), pltpu.roll(d, n_ext - 2, 0)
            du_ref[g] = (wv[2:3] * d + wv[1:2] * d1 + wv[0:1] * d2)[rows].astype(BF16)
            em = e[rows]
            acc_ref[g, 2] += _colsum8(d[rows] * em)
            acc_ref[g, 1] += _colsum8(d1[rows] * em)
            acc_ref[g, 0] += _colsum8(d2[rows] * em)
            acc_ref[g, 3] += _colsum8(d[rows])

        finish(0, d_gate, ug)
        finish(1, d_val, uv)

    return pl.pallas_call(
        body, name=name, grid=(f // cb, steps),
        in_specs=[main, prev, nxt, d_main, d_next, wd_spec, wspec, bspec],
        out_specs=[main, pl.BlockSpec((2, 4, SUBLANES, cb), lambda j, i: (0, 0, 0, j))],
        out_shape=[jax.ShapeDtypeStruct((2, s, f), BF16), jax.ShapeDtypeStruct((2, 4, SUBLANES, f), F32)],
    )(u, u, u, dffn, dffn, w_down, w, b)


def _hgrn_gates(q_raw, f_raw, lb, tri):
    sf = _sig(f_raw)
    fg = lb + (1.0 - lb) * sf
    b = _tri_dot(tri, jnp.log(fg))
    return q_raw * _sig(q_raw), 1.0 - fg, b, fg, sf


def _hgrn_fwd(proj, lb, norm_g, name):
    s = proj.shape[0]
    tb = _tile(s, HGRN_ROWS)
    n_c = tb // A_CHUNK
    half = A_CHUNK // 2

    def body(q_ref, f_ref, v_ref, g_ref, lb_ref, ng_ref, o_ref, yp_ref, st_ref, state):
        @pl.when(pl.program_id(0) == 0)
        def _():
            state[...] = jnp.zeros_like(state)

        tri = _tri(A_CHUNK)
        causal = lax.broadcasted_iota(jnp.int32, (A_CHUNK, A_CHUNK), 1) <= lax.broadcasted_iota(
            jnp.int32, (A_CHUNK, A_CHUNK), 0)

        def chunk(ci, carry):
            rows = pl.ds(pl.multiple_of(ci * A_CHUNK, A_CHUNK), A_CHUNK)
            heads = [slice(h * HEAD_DIM, (h + 1) * HEAD_DIM) for h in range(HEADS)]
            qs, k, b, _, _ = _hgrn_gates(q_ref[rows, :], f_ref[rows, :], lb_ref[...], tri)
            b_mid, b_last = b[half:half + 1], b[A_CHUNK - 1:A_CHUNK]
            q_i = (qs * jnp.exp(b - b_mid)).astype(BF16)
            k_i = (k * jnp.exp(b_mid - b)).astype(BF16)
            q_e = (qs * jnp.exp(b)).astype(BF16)
            k_s = (k * jnp.exp(b_last - b)).astype(BF16)
            decay = jnp.exp(b_last)
            vb = v_ref[rows, :].astype(BF16)
            scores = [jnp.where(causal, _dot_nt(q_i[:, cs], k_i[:, cs]), 0.0).astype(BF16) for cs in heads]
            st = [state[h] for h in range(HEADS)]
            outs = [_dot(scores[h], vb[:, cs]) + _dot_nt(q_e[:, cs], st[h].astype(BF16)) for h, cs in enumerate(heads)]
            for h, cs in enumerate(heads):
                st_ref[ci, h] = st[h]
                state[h] = st[h] * decay[:, cs] + _dot_tn(vb[:, cs], k_s[:, cs])
            o = jnp.concatenate(outs, axis=1)
            o_ref[rows, :] = o
            sq = o * o
            inv = jnp.concatenate([jnp.broadcast_to(lax.rsqrt(jnp.mean(sq[:, cs], axis=-1, keepdims=True) + EPS),
                                                    (A_CHUNK, HEAD_DIM)) for cs in heads], axis=1)
            g_raw = g_ref[rows, :]
            yp_ref[rows, :] = (o * inv * ng_ref[...] * (g_raw * _sig(g_raw))).astype(BF16)
            return carry

        lax.fori_loop(0, n_c, chunk, 0)

    col = lambda j: pl.BlockSpec((tb, D_MODEL), lambda i: (i, j))
    vec = _full((1, D_MODEL))
    return pl.pallas_call(
        body, name=name, grid=(s // tb,), in_specs=[col(0), col(1), col(2), col(3), vec, vec],
        out_specs=[col(0), col(0), pl.BlockSpec((n_c, HEADS, HEAD_DIM, HEAD_DIM), lambda i: (i, 0, 0, 0))],
        out_shape=[jax.ShapeDtypeStruct((s, D_MODEL), F32), jax.ShapeDtypeStruct((s, D_MODEL), BF16),
                   jax.ShapeDtypeStruct((s // A_CHUNK, HEADS, HEAD_DIM, HEAD_DIM), F32)],
        scratch_shapes=[pltpu.VMEM((HEADS, HEAD_DIM, HEAD_DIM), F32)],
    )(proj, proj, proj, proj, lb, norm_g)


def _hgrn_bwd(proj, lb, norm_g, o, states, dyp, name):
    s = proj.shape[0]
    tb = _tile(s, HGRN_ROWS)
    n_c = tb // A_CHUNK
    n_b = s // tb
    half = A_CHUNK // 2

    def body(q_ref, f_ref, v_ref, g_ref, lb_ref, ng_ref, o_ref, st_ref, dyp_ref, dp_ref, dlb_ref, dng_ref, dstate):
        @pl.when(pl.program_id(0) == 0)
        def _():
            dstate[...] = jnp.zeros_like(dstate)
            dlb_ref[...] = jnp.zeros_like(dlb_ref)
            dng_ref[...] = jnp.zeros_like(dng_ref)

        tri = _tri(A_CHUNK)
        tri_up = _tri(A_CHUNK, upper=True)
        row_id = lax.broadcasted_iota(jnp.int32, (A_CHUNK, D_MODEL), 0)
        causal = lax.broadcasted_iota(jnp.int32, (A_CHUNK, A_CHUNK), 1) <= lax.broadcasted_iota(
            jnp.int32, (A_CHUNK, A_CHUNK), 0)

        def chunk(cj, carry):
            ci = n_c - 1 - cj
            rows = pl.ds(pl.multiple_of(ci * A_CHUNK, A_CHUNK), A_CHUNK)
            heads = [slice(h * HEAD_DIM, (h + 1) * HEAD_DIM) for h in range(HEADS)]
            cat = lambda parts: jnp.concatenate(parts, axis=1)
            per_head_mean = lambda a: cat([jnp.broadcast_to(jnp.mean(a[:, cs], axis=-1, keepdims=True),
                                                            (A_CHUNK, HEAD_DIM)) for cs in heads])
            q_raw, lbv = q_ref[rows, :], lb_ref[...]
            qs, k, b, fg, sf = _hgrn_gates(q_raw, f_ref[rows, :], lbv, tri)
            b_mid, b_last = b[half:half + 1], b[A_CHUNK - 1:A_CHUNK]
            e_qi, e_ki, e_q, e_ks = jnp.exp(b - b_mid), jnp.exp(b_mid - b), jnp.exp(b), jnp.exp(b_last - b)
            decay = jnp.exp(b_last)
            q_i, k_i, q_e, k_s = qs * e_qi, k * e_ki, qs * e_q, k * e_ks
            qib, kib, qeb, ksb = q_i.astype(BF16), k_i.astype(BF16), q_e.astype(BF16), k_s.astype(BF16)
            vb = v_ref[rows, :].astype(BF16)
            ov, g_raw, dy, ng = o_ref[rows, :], g_ref[rows, :], dyp_ref[rows, :], ng_ref[...]
            inv = lax.rsqrt(per_head_mean(ov * ov) + EPS)
            nrm = ov * inv
            sg = _sig(g_raw)
            gs = g_raw * sg
            dn = dy * ng * gs
            dng_ref[0:1, :] += jnp.sum(dy * nrm * gs, axis=0, keepdims=True)
            dg_raw = dy * nrm * ng * (sg * (1.0 + g_raw * (1.0 - sg)))
            do = (inv * (dn - nrm * per_head_mean(dn * nrm))).astype(BF16)
            st_prev = [st_ref[ci, h] for h in range(HEADS)]
            dst = [dstate[h] for h in range(HEADS)]
            dstb = [d.astype(BF16) for d in dst]
            scores = [jnp.where(causal, _dot_nt(qib[:, cs], kib[:, cs]), 0.0).astype(BF16) for cs in heads]
            d_scores = [jnp.where(causal, _dot_nt(do[:, cs], vb[:, cs]), 0.0).astype(BF16) for cs in heads]
            dv = cat([_dot_tn(scores[h], do[:, cs]) + _dot_nt(ksb[:, cs], dstb[h]) for h, cs in enumerate(heads)])
            dq_i = cat([_dot(d_scores[h], kib[:, cs]) for h, cs in enumerate(heads)])
            dk_i = cat([_dot_tn(d_scores[h], qib[:, cs]) for h, cs in enumerate(heads)])
            dq_e = cat([_dot(do[:, cs], st_prev[h].astype(BF16)) for h, cs in enumerate(heads)])
            dk_s = cat([_dot(vb[:, cs], dstb[h]) for h, cs in enumerate(heads)])
            d_decay = cat([jnp.sum(st_prev[h] * dst[h], axis=0, keepdims=True) for h in range(HEADS)])
            for h, cs in enumerate(heads):
                dstate[h] = dst[h] * decay[:, cs] + _dot_tn(do[:, cs], qeb[:, cs])
            dq = dq_i * e_qi + dq_e * e_q
            dk = dk_i * e_ki + dk_s * e_ks
            t_qi, t_ki, t_ks = dq_i * q_i, dk_i * k_i, dk_s * k_s
            db = t_qi - t_ki + dq_e * q_e - t_ks
            db_mid = jnp.sum(t_ki - t_qi, axis=0, keepdims=True)
            db_last = jnp.sum(t_ks, axis=0, keepdims=True) + d_decay * decay
            db = db + jnp.where(row_id == half, db_mid, 0.0) + jnp.where(row_id == A_CHUNK - 1, db_last, 0.0)
            dfg = _tri_dot(tri_up, db) / fg - dk
            dlb_ref[0:1, :] += jnp.sum(dfg * (1.0 - sf), axis=0, keepdims=True)
            sq = _sig(q_raw)
            dp_ref[0, rows, :] = (dq * (sq * (1.0 + q_raw * (1.0 - sq)))).astype(BF16)
            dp_ref[1, rows, :] = (dfg * (1.0 - lbv) * sf * (1.0 - sf)).astype(BF16)
            dp_ref[2, rows, :] = dv.astype(BF16)
            dp_ref[3, rows, :] = dg_raw.astype(BF16)
            return carry

        lax.fori_loop(0, n_c, chunk, 0)

    col = lambda j: pl.BlockSpec((tb, D_MODEL), lambda i: (n_b - 1 - i, j))
    vec = _full((1, D_MODEL))
    acc = _full((SUBLANES, D_MODEL))
    return pl.pallas_call(
        body, name=name, grid=(n_b,),
        in_specs=[col(0), col(1), col(2), col(3), vec, vec, col(0),
                  pl.BlockSpec((n_c, HEADS, HEAD_DIM, HEAD_DIM), lambda i: (n_b - 1 - i, 0, 0, 0)), col(0)],
        out_specs=[pl.BlockSpec((4, tb, D_MODEL), lambda i: (0, n_b - 1 - i, 0)), acc, acc],
        out_shape=[jax.ShapeDtypeStruct((4, s, D_MODEL), BF16), jax.ShapeDtypeStruct((SUBLANES, D_MODEL), F32),
                   jax.ShapeDtypeStruct((SUBLANES, D_MODEL), F32)],
        scratch_shapes=[pltpu.VMEM((HEADS, HEAD_DIM, HEAD_DIM), F32)],
    )(proj, proj, proj, proj, lb, norm_g, o, states, dyp)


def _headnorm(x, g, mult, name, col0=0):
    s = x.shape[0]
    tm = _tile(s, ROW_TILE)

    def body(x_ref, g_ref, y_ref):
        for h in range(HEADS):
            cs = slice(h * HEAD_DIM, (h + 1) * HEAD_DIM)
            xv = x_ref[:, cs]
            inv = lax.rsqrt(jnp.mean(xv * xv, axis=-1, keepdims=True) + EPS)
            y_ref[:, cs] = (xv * inv * g_ref[:, cs] * mult).astype(BF16)

    return pl.pallas_call(
        body, name=name, grid=(s // tm,),
        in_specs=[pl.BlockSpec((tm, D_MODEL), lambda i: (i, col0)), _full((1, D_MODEL))],
        out_specs=pl.BlockSpec((tm, D_MODEL), lambda i: (i, 0)),
        out_shape=jax.ShapeDtypeStruct((s, D_MODEL), BF16),
    )(x, g)


def _headnorm_bwd(x, g, mult, dy, name, col0=0, extra=None):
    s = x.shape[0]
    tm = _tile(s, ROW_TILE)
    groups = 2 if extra is not None else 1
    head_major = dy.ndim == 3

    def body(*refs):
        x_ref, g_ref, dy_ref = refs[:3]
        dx_ref, dg_ref = refs[-2:]

        @pl.when(pl.program_id(0) == 0)
        def _():
            dg_ref[...] = jnp.zeros_like(dg_ref)

        for h in range(HEADS):
            cs = slice(h * HEAD_DIM, (h + 1) * HEAD_DIM)
            xv, gv = x_ref[:, cs], g_ref[:, cs]
            dyv = dy_ref[h, :, 0:HEAD_DIM] if head_major else dy_ref[:, cs]
            inv = lax.rsqrt(jnp.mean(xv * xv, axis=-1, keepdims=True) + EPS)
            nrm = xv * inv
            dn = dyv * gv * mult
            dg_ref[:, cs] += _colsum8(dyv * nrm * mult)
            dx_ref[0, :, cs] = (inv * (dn - nrm * jnp.mean(dn * nrm, axis=-1, keepdims=True))).astype(BF16)
        if extra is not None:
            dx_ref[1] = refs[3][...]

    row = pl.BlockSpec((tm, D_MODEL), lambda i: (i, 0))
    dy_spec = pl.BlockSpec((HEADS, tm, dy.shape[-1]), lambda i: (0, i, 0)) if head_major else row
    ins = [x, g, dy] + ([extra] if extra is not None else [])
    specs = ([pl.BlockSpec((tm, D_MODEL), lambda i: (i, col0)), _full((1, D_MODEL)), dy_spec]
             + ([row] if extra is not None else []))
    return pl.pallas_call(
        body, name=name, grid=(s // tm,), in_specs=specs,
        out_specs=[pl.BlockSpec((groups, tm, D_MODEL), lambda i: (0, i, 0)), _full((SUBLANES, D_MODEL))],
        out_shape=[jax.ShapeDtypeStruct((groups, s, D_MODEL), BF16), jax.ShapeDtypeStruct((SUBLANES, D_MODEL), F32)],
    )(*ins)


def _log_sigmoid(z):
    return jnp.minimum(z, 0.0) - jnp.log(1.0 + jnp.exp(-jnp.abs(z)))


Q_CUM, Q_ONE, Q_LSE = 0, 3, 6
LOG2E = 1.4426950408889634


def _pieces(v):
    hi = v.astype(BF16).astype(F32)
    mid = (v - hi).astype(BF16).astype(F32)
    lo = ((v - hi) - mid).astype(BF16).astype(F32)
    return hi, mid, lo


def _side(lane, at, v):
    hi, mid, lo = _pieces(v)
    return jnp.where(lane == at, hi, jnp.where(lane == at + 1, mid, jnp.where(lane == at + 2, lo, 0.0)))


def _fcum_fwd(f, bias, name):
    s = f.shape[0]
    tm = _tile(s, ROW_TILE)

    def body(f_ref, b_ref, qa_ref, ka_ref, carry):
        @pl.when(pl.program_id(0) == 0)
        def _():
            carry[...] = jnp.zeros_like(carry)

        cum = _tri_dot(_tri(tm), _log_sigmoid(f_ref[...] + b_ref[...])) + carry[...]
        carry[...] = cum[tm - 1:tm]
        lane = lax.broadcasted_iota(jnp.int32, (tm, LANES), 1)
        ones_q = jnp.where((lane >= Q_ONE) & (lane < Q_LSE), 1.0, 0.0)
        ones_k = jnp.where((lane < Q_ONE) | ((lane >= Q_LSE) & (lane < Q_LSE + 3)), 1.0, 0.0)
        for h in range(HEADS):
            c2 = cum[:, h:h + 1] * LOG2E
            qa_ref[h] = (_side(lane, Q_CUM, c2) + ones_q).astype(BF16)
            ka_ref[h] = (_side(lane, Q_ONE, -c2) + ones_k).astype(BF16)

    side = pl.BlockSpec((HEADS, tm, LANES), lambda i: (0, i, 0))
    return pl.pallas_call(
        body, name=name, grid=(s // tm,),
        in_specs=[pl.BlockSpec((tm, LANES), lambda i: (i, 0)), _full((1, LANES))],
        out_specs=[side, side],
        out_shape=[jax.ShapeDtypeStruct((HEADS, s, LANES), BF16)] * 2,
        scratch_shapes=[pltpu.VMEM((1, LANES), F32)],
    )(f, bias)


def _fcum_bwd(f, bias, dka, dq, name):
    s = f.shape[0]
    tm = _tile(s, ROW_TILE)
    n_b = s // tm
    q_lane = HEAD_DIM + Q_CUM

    def body(f_ref, b_ref, dka_ref, dqa_ref, dz_ref, db_ref, carry):
        @pl.when(pl.program_id(0) == 0)
        def _():
            carry[...] = jnp.zeros_like(carry)
            db_ref[...] = jnp.zeros_like(db_ref)

        lane = lax.broadcasted_iota(jnp.int32, (tm, LANES), 1)
        dcum = jnp.zeros((tm, LANES), F32)
        for h in range(HEADS):
            dcum = dcum + jnp.where(lane == h, dqa_ref[h, :, q_lane:q_lane + 1] - dka_ref[h, :, Q_ONE:Q_ONE + 1], 0.0)
        dlf = _tri_dot(_tri(tm, upper=True), dcum) + carry[...]
        carry[...] = dlf[0:1]
        dz = dlf * _sig(-(f_ref[...] + b_ref[...]))
        dz_ref[0] = dz.astype(BF16)
        db_ref[...] += _colsum8(dz)

    return pl.pallas_call(
        body, name=name, grid=(n_b,),
        in_specs=[pl.BlockSpec((tm, LANES), lambda i: (n_b - 1 - i, 0)), _full((1, LANES)),
                  pl.BlockSpec((HEADS, tm, LANES), lambda i: (0, n_b - 1 - i, 0)),
                  pl.BlockSpec((HEADS, tm, 2 * HEAD_DIM), lambda i: (0, n_b - 1 - i, 0))],
        out_specs=[pl.BlockSpec((1, tm, LANES), lambda i: (0, n_b - 1 - i, 0)), _full((SUBLANES, LANES))],
        out_shape=[jax.ShapeDtypeStruct((1, s, LANES), BF16), jax.ShapeDtypeStruct((SUBLANES, LANES), F32)],
        scratch_shapes=[pltpu.VMEM((1, LANES), F32)],
    )(f, bias, dka, dq)


def _causal_pairs(n_t, key_major):
    if key_major:
        pairs = [(qi, ki) for ki in range(n_t) for qi in range(ki, n_t)]
    else:
        pairs = [(qi, ki) for qi in range(n_t) for ki in range(qi + 1)]
    return (jnp.array([p[0] for p in pairs], jnp.int32), jnp.array([p[1] for p in pairs], jnp.int32))


def _with_side(main_ref, side_ref):
    return jnp.concatenate([main_ref[...], side_ref[...]], axis=1)


def _lane_const(t, lo, hi, value):
    lane = lax.broadcasted_iota(jnp.int32, (t, LANES), 1)
    return jnp.where((lane >= lo) & (lane < hi), value, 0.0).astype(BF16)


def _att_specs(t, nh):
    qmain = pl.BlockSpec((t, nh * HEAD_DIM), lambda h, p, qt, kt: (qt[p], h))
    kmain = pl.BlockSpec((t, nh * HEAD_DIM), lambda h, p, qt, kt: (kt[p], h))
    qside = pl.BlockSpec((nh, t, LANES), lambda h, p, qt, kt: (h, qt[p], 0))
    kside = pl.BlockSpec((nh, t, LANES), lambda h, p, qt, kt: (h, kt[p], 0))
    return qmain, kmain, qside, kside


def _fox_fwd(q, qa, k, ka, v, qo, name):
    s = q.shape[0]
    t = _tile(s, ATT_TILE)
    sub = t // ATT_SPLIT
    nh = ATT_FWD_HEADS
    qt, kt = _causal_pairs(s // t, key_major=False)

    def body(qt_ref, kt_ref, q_ref, qa_ref, k_ref, ka_ref, v_ref, og_ref, o_ref, y_ref, qab_ref, m_s, l_s, acc_s):
        pid = pl.program_id(1)
        qi, ki = qt_ref[pid], kt_ref[pid]

        @pl.when(ki == 0)
        def _():
            m_s[...] = jnp.full_like(m_s, NEG_INF)
            l_s[...] = jnp.zeros_like(l_s)
            acc_s[...] = jnp.zeros_like(acc_s)

        def step(diagonal):
            for hh in range(nh):
                hc = slice(hh * HEAD_DIM, (hh + 1) * HEAD_DIM)
                kc = jnp.concatenate([k_ref[:, hc], ka_ref[hh]], axis=1)
                vc = jnp.concatenate([v_ref[:, hc], _lane_const(t, 0, 1, 1.0)], axis=1)
                for r in range(ATT_SPLIT):
                    rows = slice(r * sub, (r + 1) * sub)
                    n_k = (r + 1) * sub if diagonal else t
                    sc = _dot_nt(jnp.concatenate([q_ref[rows, hc], qa_ref[hh, rows]], axis=1), kc[:n_k])
                    if diagonal:
                        sc = jnp.where(lax.broadcasted_iota(jnp.int32, (sub, n_k), 1)
                                       <= lax.broadcasted_iota(jnp.int32, (sub, n_k), 0) + r * sub, sc, NEG_INF)
                    m_old = m_s[hh, rows]
                    m_new = jnp.maximum(m_old, jnp.max(sc, axis=-1, keepdims=True))
                    alpha = jnp.exp2(m_old - m_new)
                    pv = _dot(jnp.exp2(sc - m_new[:, 0:1]).astype(BF16), vc[:n_k])
                    acc_s[hh, rows] = alpha * acc_s[hh, rows] + pv[:, :HEAD_DIM]
                    l_s[hh, rows] = alpha * l_s[hh, rows] + pv[:, HEAD_DIM:]
                    m_s[hh, rows] = m_new

        @pl.when(ki < qi)
        def _():
            step(False)

        @pl.when(ki == qi)
        def _():
            step(True)
            lane = lax.broadcasted_iota(jnp.int32, (t, LANES), 1)
            for hh in range(nh):
                hc = slice(hh * HEAD_DIM, (hh + 1) * HEAD_DIM)
                l = l_s[hh, :, 0:1]
                o = acc_s[hh] / l
                o_ref[:, hc] = o
                y_ref[:, hc] = (o * _sig(og_ref[:, hc])).astype(BF16)
                qab_ref[hh] = qa_ref[hh] + _side(lane, Q_LSE, -(m_s[hh, :, 0:1] + jnp.log2(l))).astype(BF16)

    qmain, kmain, qside, kside = _att_specs(t, nh)
    return pl.pallas_call(
        body, name=name,
        grid_spec=pltpu.PrefetchScalarGridSpec(
            num_scalar_prefetch=2, grid=(HEADS // nh, qt.shape[0]),
            in_specs=[qmain, qside, kmain, kside, kmain,
                      pl.BlockSpec((t, nh * HEAD_DIM), lambda h, p, qt, kt: (qt[p], HEADS // nh + h))],
            out_specs=[qmain, qmain, qside],
            scratch_shapes=[pltpu.VMEM((nh, t, LANES), F32), pltpu.VMEM((nh, t, LANES), F32),
                            pltpu.VMEM((nh, t, HEAD_DIM), F32)]),
        out_shape=[jax.ShapeDtypeStruct((s, D_MODEL), F32), jax.ShapeDtypeStruct((s, D_MODEL), BF16),
                   jax.ShapeDtypeStruct((HEADS, s, LANES), BF16)],
    )(qt, kt, q, qa, k, ka, v, qo)


def _fox_gate_bwd(o, qo, dy, name):
    s = o.shape[0]
    tm = _tile(s, ROW_TILE)

    def body(o_ref, og_ref, dy_ref, do_ref, dg_ref, dl_ref):
        ov, dyv = o_ref[...], dy_ref[...]
        sg = _sig(og_ref[...])
        do = (dyv * sg).astype(BF16)
        do_ref[...] = do
        dg_ref[...] = (dyv * ov * sg * (1.0 - sg)).astype(BF16)
        prod = do.astype(F32) * ov
        lane = lax.broadcasted_iota(jnp.int32, (tm, LANES), 1)
        for h in range(HEADS):
            delta = jnp.sum(prod[:, h * HEAD_DIM:(h + 1) * HEAD_DIM], axis=-1, keepdims=True)
            dl_ref[h] = _side(lane, 0, delta).astype(BF16)

    row = pl.BlockSpec((tm, D_MODEL), lambda i: (i, 0))
    return pl.pallas_call(
        body, name=name, grid=(s // tm,),
        in_specs=[row, pl.BlockSpec((tm, D_MODEL), lambda i: (i, 1)), row],
        out_specs=[row, row, pl.BlockSpec((HEADS, tm, LANES), lambda i: (0, i, 0))],
        out_shape=[jax.ShapeDtypeStruct((s, D_MODEL), BF16), jax.ShapeDtypeStruct((s, D_MODEL), BF16),
                   jax.ShapeDtypeStruct((HEADS, s, LANES), BF16)],
    )(o, qo, dy)


def _fox_bwd(q, qab, k, ka, v, do, doa, name):
    s = q.shape[0]
    t = _tile(s, ATT_TILE)
    n_t = s // t
    sub = t // ATT_SPLIT
    nh = ATT_BWD_HEADS
    qt, kt = _causal_pairs(n_t, key_major=True)

    def body(qt_ref, kt_ref, q_ref, qab_ref, k_ref, ka_ref, v_ref, do_ref, doa_ref, dk_ref, dv_ref, dka_ref, dq_hbm,
             dk_s, dv_s, dq_ref):
        group, pid = pl.program_id(0), pl.program_id(1)
        qi, ki = qt_ref[pid], kt_ref[pid]

        @pl.when(pid == 0)
        def _():
            dq_ref[...] = jnp.zeros_like(dq_ref)

        @pl.when(qi == ki)
        def _():
            dk_s[...] = jnp.zeros_like(dk_s)
            dv_s[...] = jnp.zeros_like(dv_s)

        def step(diagonal):
            for hh in range(nh):
                hc = slice(hh * HEAD_DIM, (hh + 1) * HEAD_DIM)
                kc = jnp.concatenate([k_ref[:, hc], ka_ref[hh]], axis=1)
                vc = jnp.concatenate([v_ref[:, hc], _lane_const(t, 0, 3, -1.0)], axis=1)
                for r in range(ATT_SPLIT):
                    cols = slice(r * sub, (r + 1) * sub)
                    n_k = (r + 1) * sub if diagonal else t
                    qc = jnp.concatenate([q_ref[cols, hc], qab_ref[hh, cols]], axis=1)
                    sc = _dot_nt(kc[:n_k], qc)
                    if diagonal:
                        sc = jnp.where(lax.broadcasted_iota(jnp.int32, (n_k, sub), 0)
                                       <= lax.broadcasted_iota(jnp.int32, (n_k, sub), 1) + r * sub, sc, NEG_INF)
                    p = jnp.exp2(sc)
                    dov = do_ref[cols, hc]
                    dp = _dot_nt(vc[:n_k], jnp.concatenate([dov, doa_ref[hh, cols]], axis=1))
                    ds = (p * dp).astype(BF16)
                    dv_s[hh, 0:n_k] += _dot(p.astype(BF16), dov)
                    dk_s[hh, 0:n_k] += _dot(ds, qc)
                    q_rows = pl.ds(pl.multiple_of(qi * t + r * sub, sub), sub)
                    dq_ref[hh, q_rows, :] += _dot_tn(ds, kc[:n_k])

        @pl.when(qi > ki)
        def _():
            step(False)

        @pl.when(qi == ki)
        def _():
            step(True)

        @pl.when(qi == n_t - 1)
        def _():
            for hh in range(nh):
                hc = slice(hh * HEAD_DIM, (hh + 1) * HEAD_DIM)
                dk_ref[:, hc] = dk_s[hh, :, :HEAD_DIM] * (1.0 / LOG2E)
                dka_ref[hh] = dk_s[hh, :, HEAD_DIM:]
                dv_ref[:, hc] = dv_s[hh].astype(BF16)

        @pl.when(pid == qt.shape[0] - 1)
        def _():
            pltpu.sync_copy(dq_ref, dq_hbm.at[pl.ds(group * nh, nh)])

    qmain, kmain, qside, kside = _att_specs(t, nh)
    return pl.pallas_call(
        body, name=name,
        grid_spec=pltpu.PrefetchScalarGridSpec(
            num_scalar_prefetch=2, grid=(HEADS // nh, qt.shape[0]),
            in_specs=[qmain, qside, kmain, kside, kmain, qmain, qside],
            out_specs=[kmain, pl.BlockSpec((None, t, nh * HEAD_DIM), lambda h, p, qt, kt: (0, kt[p], h)), kside,
                       pl.BlockSpec(memory_space=pltpu.HBM)],
            scratch_shapes=[pltpu.VMEM((nh, t, 2 * HEAD_DIM), F32), pltpu.VMEM((nh, t, HEAD_DIM), F32),
                            pltpu.VMEM((nh, s, 2 * HEAD_DIM), F32)]),
        out_shape=[jax.ShapeDtypeStruct((s, D_MODEL), F32), jax.ShapeDtypeStruct((1, s, D_MODEL), BF16),
                   jax.ShapeDtypeStruct((HEADS, s, LANES), F32), jax.ShapeDtypeStruct((HEADS, s, 2 * HEAD_DIM), F32)],
    )(qt, kt, q, qab, k, ka, v, do, doa)


def _mm_residual_premix(a, w, x, gate, mods, name):
    s, k = a.shape
    dm = x.shape[1]
    tm = _tile(s, ROW_TILE)

    def body(*refs):
        a_ref, w_ref, x_ref, g_ref = refs[:4]
        mod_refs = refs[4:4 + 2 * len(mods)]
        y_ref, xn_ref = refs[4 + 2 * len(mods):6 + 2 * len(mods)]
        h_refs = refs[6 + 2 * len(mods):]
        y = _dot(a_ref[...], w_ref[0])
        y_ref[...] = y
        xv = x_ref[...] + g_ref[...] * y
        xn_ref[...] = xv
        nrm = xv * lax.rsqrt(jnp.mean(xv * xv, axis=-1, keepdims=True) + EPS)
        for t, h_ref in enumerate(h_refs):
            h_ref[...] = (nrm * (1.0 + mod_refs[2 * t + 1][...]) + mod_refs[2 * t][...]).astype(BF16)

    row = pl.BlockSpec((tm, dm), lambda i: (i, 0))
    vec = _full((1, dm))
    outs = pl.pallas_call(
        body, name=name, grid=(s // tm,),
        in_specs=[pl.BlockSpec((tm, k), lambda i: (i, 0)), _full(w.shape), row, vec] + [vec] * (2 * len(mods)),
        out_specs=[row] * (2 + len(mods)),
        out_shape=[jax.ShapeDtypeStruct((s, dm), F32)] * 2 + [jax.ShapeDtypeStruct((s, dm), BF16)] * len(mods),
    )(a, w, x, gate, *[v for m in mods for v in m])
    return outs[0], outs[1], list(outs[2:])


def _mm_loss_head(a, w, x, gate, target, name):
    s, k = a.shape
    dm = x.shape[1]
    tm = _tile(s, ROW_TILE)

    def body(a_ref, w_ref, x_ref, g_ref, t_ref, sq_ref, do_ref, dy_ref, dg_ref):
        @pl.when(pl.program_id(0) == 0)
        def _():
            sq_ref[...] = jnp.zeros_like(sq_ref)
            dg_ref[...] = jnp.zeros_like(dg_ref)

        y, gv = _dot(a_ref[...], w_ref[0]), g_ref[...]
        err = x_ref[...] + gv * y - t_ref[...]
        sq_ref[...] += _colsum8(err * err)
        dout = err * (1.0 / dm)
        do_ref[...] = dout
        dy_ref[0] = (dout * gv).astype(BF16)
        dg_ref[...] += _colsum8(dout * y)

    row = pl.BlockSpec((tm, dm), lambda i: (i, 0))
    acc = _full((SUBLANES, dm))
    return pl.pallas_call(
        body, name=name, grid=(s // tm,),
        in_specs=[pl.BlockSpec((tm, k), lambda i: (i, 0)), _full(w.shape), row, _full((1, dm)), row],
        out_specs=[acc, row, pl.BlockSpec((1, tm, dm), lambda i: (0, i, 0)), acc],
        out_shape=[jax.ShapeDtypeStruct((SUBLANES, dm), F32), jax.ShapeDtypeStruct((s, dm), F32),
                   jax.ShapeDtypeStruct((1, s, dm), BF16), jax.ShapeDtypeStruct((SUBLANES, dm), F32)],
    )(a, w, x, gate, target)


def _ffn_inner(h, w_up, conv_w, conv_b, tag):
    s, dm = h.shape
    half = w_up.shape[2]
    f = 2 * half
    tm = _tile(s, FFN_ROWS)

    def body(h_ref, w_ref, cw_ref, cb_ref, u_ref, a_ref, carry):
        @pl.when(pl.program_id(0) == 0)
        def _():
            carry[...] = jnp.zeros_like(carry)

        hv = h_ref[...]
        for j in range(2):
            cols = slice(j * half, (j + 1) * half)
            conv = []
            for g in range(2):
                ub = _dot(hv, w_ref[2 * g + j]).astype(BF16)
                u_ref[g, :, cols] = ub
                uf = ub.astype(F32)
                e = jnp.concatenate([carry[g, j], uf], axis=0)
                carry[g, j] = uf[tm - SUBLANES:tm]
                conv.append(_conv_taps(e, cw_ref[g][:, cols], cb_ref[g][:, cols])[SUBLANES:])
            a_ref[:, cols] = (conv[0] * _sig(conv[0]) * conv[1]).astype(BF16)

    return pl.pallas_call(
        body, name=tag + "_up_convglu", grid=(s // tm,),
        in_specs=[pl.BlockSpec((tm, dm), lambda i: (i, 0)), _full(w_up.shape), _full(conv_w.shape), _full(conv_b.shape)],
        out_specs=[pl.BlockSpec((2, tm, f), lambda i: (0, i, 0)), pl.BlockSpec((tm, f), lambda i: (i, 0))],
        out_shape=[jax.ShapeDtypeStruct((2, s, f), BF16), jax.ShapeDtypeStruct((s, f), BF16)],
        scratch_shapes=[pltpu.VMEM((2, 2, SUBLANES, half), F32)],
    )(h, w_up, conv_w, conv_b)


def _weight_grad_first(a, d, p_n, name):
    return lax.optimization_barrier((_mm_tn(a, d, p_n, name), d))


def _ffn_backward(dx_out, dffn, x_mid, scale, saved, w_up, conv_w, conv_b, w_down, mixer, tag):
    h, u, a = saved
    dw_down, dffn = _weight_grad_first(a, dffn, 1, tag + "_down_dw")
    du, dconv = _convglu_bwd(u, dffn, w_down, conv_w, conv_b, tag + "_convglu_bwd")
    dw_up, du = _weight_grad_first(h, du, N_CHIPS, tag + "_up_dw")
    dx_mid, [(dshift, dscale)], dy, dgate_mixer = _premix_bwd(x_mid, [(scale, [(du, w_up)])], dx_out,
                                                              tag + "_premix_bwd", branch=mixer)
    return dx_mid, dy, dgate_mixer, dw_up, dw_down, dict(shift=dshift, scale=dscale, conv=dconv)


def _local_step(x, target, mods, lb, vecs, weights_at):
    m0, m1, mk = mods["l0"], mods["l1"], mods["kv"]
    h0 = _premix(x, m0[0], m0[1], "l0_premix")
    wts, h0 = weights_at("mixer0", h0)
    proj = _mm_nn(h0, wts["a_w_in"], 1, F32, "l0_in")[0]
    o_a, yp, states = _hgrn_fwd(proj, lb, vecs["a_norm_g"], "l0_hgrn")
    more, yp = weights_at("ffn0", yp)
    wts.update(more)
    y0, x1, [hf0] = _mm_residual_premix(yp, wts["a_w_out"], x, m0[2], [(m0[3], m0[4])], "l0_out")
    u0, a0 = _ffn_inner(hf0, wts["up0"], vecs["conv_w0"], vecs["conv_b0"], "l0_ffn")
    saved0 = (hf0, u0, a0)
    ffn0, x2, [hk, h1] = _mm_residual_premix(a0, wts["down0"], x1, m0[5], [(mk[0], mk[1]), (m1[0], m1[1])],
                                             "l0_ffn_down")
    more, hk = weights_at("layer1", hk)
    wts.update(more)
    k_raw = _mm_nn(hk, wts["kv_k"], 1, F32, "kv_k")[0]
    v_sh = _mm_nn(hk, wts["kv_v"], 1, BF16, "kv_v")[0]
    f_raw = _mm_nn(hk, wts["kv_f"], 1, F32, "kv_f")[0]
    k_sh = _headnorm(k_raw, vecs["k_norm_g"], 1.0, "kv_knorm")
    qa, ka = _fcum_fwd(f_raw, vecs["kv_b_f"], "kv_fcum")
    qo = _mm_nn(h1, wts["b_w_q"], 1, F32, "l1_q")[0]
    q_scale = HEAD_DIM ** -0.5
    q = _headnorm(qo, vecs["q_norm_g"], q_scale * LOG2E, "l1_qnorm")
    o_b, og, qab = _fox_fwd(q, qa, k_sh, ka, v_sh, qo, "l1_fox")
    y1, x3, [hf1] = _mm_residual_premix(og, wts["b_w_out"], x2, m1[2], [(m1[3], m1[4])], "l1_out")
    u1, a1 = _ffn_inner(hf1, wts["up1"], vecs["conv_w1"], vecs["conv_b1"], "l1_ffn")
    saved1 = (hf1, u1, a1)
    sq, dx4, dffn1, dg2_1 = _mm_loss_head(a1, wts["down1"], x3, m1[5], target, "l1_ffn_down")

    big, small = {}, {}
    dx3, dy1, dg1_1, big["up1"], big["down1"], s_ffn1 = _ffn_backward(
        dx4, dffn1, x3, m1[4], saved1, wts["up1"], vecs["conv_w1"], vecs["conv_b1"], wts["down1"], (y1, m1[2]), "l1_ffn")
    big["b_w_out"], dy1 = _weight_grad_first(og, dy1, 1, "l1_out_dw")
    d_og = _mm_nt(dy1, wts["b_w_out"], F32, "l1_out_dx")
    do_b, dgate_b, doa = _fox_gate_bwd(o_b, qo, d_og, "l1_fox_gate_bwd")
    dk, dv, dka, dq = _fox_bwd(q, qab, k_sh, ka, v_sh, do_b, doa, "l1_fox_bwd")
    dqo, dqg = _headnorm_bwd(qo, vecs["q_norm_g"], q_scale, dq, "l1_qnorm_bwd", extra=dgate_b)
    big["b_w_q"], dqo = _weight_grad_first(h1, dqo, N_CHIPS, "l1_q_dw")
    dk_raw, dkg = _headnorm_bwd(k_raw, vecs["k_norm_g"], 1.0, dk, "kv_knorm_bwd")
    dz, dbf = _fcum_bwd(f_raw, vecs["kv_b_f"], dka, dq, "kv_fcum_bwd")
    big["kv_k"], dk_raw = _weight_grad_first(hk, dk_raw, 1, "kv_k_dw")
    big["kv_v"], dv = _weight_grad_first(hk, dv, 1, "kv_v_dw")
    big["kv_f"], dz = _weight_grad_first(hk, dz, 1, "kv_f_dw")
    kv_pairs = [(dk_raw, wts["kv_k"]), (dv, wts["kv_v"]), (dz, wts["kv_f"])]
    dx2, [(dsh1_1, dsc1_1), (dshk, dsck)], dffn0, dg2_0 = _premix_bwd(
        x2, [(m1[1], [(dqo, wts["b_w_q"])]), (mk[1], kv_pairs)], dx3, "l1_kv_premix_bwd", branch=(ffn0, m0[5]))
    dx1, dy0, dg1_0, big["up0"], big["down0"], s_ffn0 = _ffn_backward(
        dx2, dffn0, x1, m0[4], saved0, wts["up0"], vecs["conv_w0"], vecs["conv_b0"], wts["down0"], (y0, m0[2]), "l0_ffn")
    big["a_w_out"], dy0 = _weight_grad_first(yp, dy0, 1, "l0_out_dw")
    dyp = _mm_nt(dy0, wts["a_w_out"], F32, "l0_out_dx")
    dproj, dlb, dng = _hgrn_bwd(proj, lb, vecs["a_norm_g"], o_a, states, dyp, "l0_hgrn_bwd")
    grad_x, [(dsh1_0, dsc1_0)] = _premix_bwd(x, [(m0[1], [(dproj, wts["a_w_in"])])], dx1, "l0_premix_bwd")
    dproj, _ = lax.optimization_barrier((dproj, (dsh1_0, dsc1_0)))
    big["a_w_in"] = _mm_tn(h0, dproj, N_CHIPS, "l0_in_dw")

    small["mod_l0"] = [dsh1_0, dsc1_0, dg1_0, s_ffn0["shift"], s_ffn0["scale"], dg2_0]
    small["mod_l1"] = [dsh1_1, dsc1_1, dg1_1, s_ffn1["shift"], s_ffn1["scale"], dg2_1]
    small["mod_kv"] = [dshk, dsck]
    small["conv0"], small["conv1"] = s_ffn0["conv"], s_ffn1["conv"]
    small["a_norm_g"], small["k_norm_g"], small["q_norm_g"] = dng, dkg, dqg
    small["kv_b_f"], small["lb"] = dbf, dlb
    marks = {"attention_bwd": dk, "ffn0_bwd": dx1, "mixer0_bwd": grad_x}
    return sq, grad_x, big, small, marks


HBM = pl.BlockSpec(memory_space=pltpu.HBM)
COMM_CHUNK_ELEMS = 256 * 1024


def _place():
    x, y, c = lax.axis_index("x"), lax.axis_index("y"), lax.axis_index("c")
    chips = [(1 - x, y), (x, 1 - y), (1 - x, 1 - y)]
    return x, y, c, (x, y, 1 - c), chips


def _chunk_rows(rows, cols):
    best = BF16_ROWS
    for r in range(BF16_ROWS, rows + 1, BF16_ROWS):
        if rows % r == 0 and r * cols <= COMM_CHUNK_ELEMS:
            best = r
    assert rows % best == 0, (rows, cols)
    return best


def _allgather8(block, name):
    m_per, n = block.shape

    def body(x_ref, out_ref, send_sems, recv_sems, local_sem):
        x, y, c, sibling, chips = _place()
        me = (x, y, c)

        def rows(px, py, pc):
            return out_ref.at[pl.ds((4 * px + 2 * py + pc) * m_per, m_per), :]

        def copy(k, blk, to, src=None):
            return pltpu.make_async_remote_copy(
                src_ref=rows(*blk) if src is None else src, dst_ref=rows(*blk),
                send_sem=send_sems.at[k], recv_sem=recv_sems.at[k], device_id=to, device_id_type=MESH)

        mine = pltpu.make_async_copy(x_ref, rows(*me), local_sem)
        mine.start()
        first = [copy(0, me, sibling, src=x_ref)]
        first += [copy(1 + j, me, (*chip, c), src=x_ref) for j, chip in enumerate(chips)]
        for cp in first:
            cp.start()
        passed = [copy(4 + j, (*chip, c), sibling) for j, chip in enumerate(chips)]
        for j, chip in enumerate(chips):
            copy(1 + j, (*chip, c), me).wait_recv()
            passed[j].start()
        copy(0, sibling, me).wait_recv()
        for j, chip in enumerate(chips):
            copy(4 + j, (*chip, 1 - c), me).wait_recv()
        for cp in first + passed:
            cp.wait_send()
        mine.wait()

    return pl.pallas_call(
        body, name=name, out_shape=jax.ShapeDtypeStruct((N_DEV * m_per, n), block.dtype),
        in_specs=[pl.BlockSpec(memory_space=pltpu.VMEM)], out_specs=pl.BlockSpec(memory_space=pltpu.VMEM),
        scratch_shapes=[pltpu.SemaphoreType.DMA((7,)), pltpu.SemaphoreType.DMA((7,)), pltpu.SemaphoreType.DMA],
    )(block)


def _cast_own_block(shards, layer, chip, name):
    _, r, cols = shards.shape
    rows = _chunk_rows(r, cols)

    def body(chip_ref, w_ref, o_ref):
        o_ref[...] = w_ref[...].astype(BF16)

    return pl.pallas_call(
        body, name=name,
        grid_spec=pltpu.PrefetchScalarGridSpec(
            num_scalar_prefetch=1, grid=(r // rows,),
            in_specs=[pl.BlockSpec((None, rows, cols), lambda i, chip_ref: (layer, i, 0))],
            out_specs=pl.BlockSpec((None, rows, cols), lambda i, chip_ref: (chip_ref[0], i, 0))),
        out_shape=jax.ShapeDtypeStruct((N_CHIPS, r, cols), BF16),
    )(chip, shards)


def _sequencer_gather(bufs, name, collective_id):
    n_t = len(bufs)
    dims = [b.shape[1:] for b in bufs]
    refs = [jax.new_ref(b, memory_space=pltpu.MemorySpace.HBM) for b in bufs]

    @pl.kernel(mesh=plsc.ScalarSubcoreMesh(axis_name="sequencer", num_cores=1), name=name,
               scratch_types=[pltpu.SemaphoreType.DMA((n_t,))] * 4,
               compiler_params=pltpu.CompilerParams(collective_id=collective_id))
    def launch(send_ici, recv_ici, send_d2d, recv_d2d):
        x, y, c, sibling, chips = _place()
        p_me = 2 * x + y
        peers = [sibling] + [(cx, cy, c) for cx, cy in chips]
        barrier = pltpu.get_barrier_semaphore()
        for peer in peers:
            pl.semaphore_signal(barrier, inc=1, device_id=peer, device_id_type=MESH)
        pl.semaphore_wait(barrier, len(peers))

        def waiter(t, sem_s, sem_r):
            win = refs[t].at[pl.ds(0, 3), pl.ds(0, dims[t][0] // 2), :]
            return pltpu.make_async_remote_copy(src_ref=win, dst_ref=win, send_sem=sem_s.at[t], recv_sem=sem_r.at[t],
                                                device_id=sibling, device_id_type=MESH)

        def half_copy(t, chip_idx, to, sem_s, sem_r):
            r2 = dims[t][0] // 2
            win = refs[t].at[chip_idx, pl.ds(c * r2, r2), :]
            return pltpu.make_async_remote_copy(src_ref=win, dst_ref=win, send_sem=sem_s.at[t], recv_sem=sem_r.at[t],
                                                device_id=to, device_id_type=MESH)

        for t in range(n_t):
            for cx, cy in chips:
                half_copy(t, p_me, (cx, cy, c), send_ici, recv_ici).start()
        for t in range(n_t):
            waiter(t, send_ici, recv_ici).wait_recv()
            for cx, cy in chips:
                half_copy(t, 2 * cx + cy, sibling, send_d2d, recv_d2d).start()
        for t in range(n_t):
            waiter(t, send_d2d, recv_d2d).wait_recv()
            waiter(t, send_ici, recv_ici).wait_send()
            waiter(t, send_d2d, recv_d2d).wait_send()

    launch()
    return [r[...] for r in refs]


def _sequencer_allgather8(block, dev, name, collective_id):
    m_per, n = block.shape
    src = jax.new_ref(block, memory_space=pltpu.MemorySpace.HBM)
    out = jax.empty_ref(jax.ShapeDtypeStruct((N_DEV * m_per, n), block.dtype), memory_space=pltpu.MemorySpace.HBM)

    @pl.kernel(mesh=plsc.ScalarSubcoreMesh(axis_name="sequencer", num_cores=1), name=name,
               scratch_types=[pltpu.SemaphoreType.DMA((7,))] * 2,
               compiler_params=pltpu.CompilerParams(collective_id=collective_id))
    def launch(send_sems, recv_sems):
        x, y, c, sibling, chips = _place()
        me = (x, y, c)
        _handshake([sibling] + [(cx, cy, c) for cx, cy in chips])

        def rows(px, py, pc):
            return out.at[pl.ds((4 * px + 2 * py + pc) * m_per, m_per), :]

        def copy(k, blk, to, from_src=False):
            return pltpu.make_async_remote_copy(
                src_ref=src if from_src else rows(*blk), dst_ref=rows(*blk),
                send_sem=send_sems.at[k], recv_sem=recv_sems.at[k], device_id=to, device_id_type=MESH)

        first = [copy(0, me, sibling, True)] + [copy(1 + j, me, (*chip, c), True) for j, chip in enumerate(chips)]
        for cp in first:
            cp.start()
        passed = [copy(4 + j, (*chip, c), sibling) for j, chip in enumerate(chips)]
        for j, chip in enumerate(chips):
            copy(1 + j, (*chip, c), me).wait_recv()
            passed[j].start()
        copy(0, sibling, me).wait_recv()
        for j, chip in enumerate(chips):
            copy(4 + j, (*chip, 1 - c), me).wait_recv()
        for cp in first + passed:
            cp.wait_send()

    launch()
    return lax.dynamic_update_slice(out[...], block, (dev * m_per, 0))


def _others():
    x, y, c = lax.axis_index("x"), lax.axis_index("y"), lax.axis_index("c")
    flip = lambda v, f: 1 - v if f else v
    return [(flip(x, fx), flip(y, fy), flip(c, fc))
            for fx in (0, 1) for fy in (0, 1) for fc in (0, 1) if (fx, fy, fc) != (0, 0, 0)]


def _handshake(peers):
    barrier = pltpu.get_barrier_semaphore()
    for peer in peers:
        pl.semaphore_signal(barrier, inc=1, device_id=peer, device_id_type=MESH)
    pl.semaphore_wait(barrier, len(peers))


def _sequencer_scatter(parts, name, collective_id):
    n_t = len(parts)
    dims = [p.shape[1:] for p in parts]
    srcs = [jax.new_ref(p, memory_space=pltpu.MemorySpace.HBM) for p in parts]
    inboxes = [jax.empty_ref(jax.ShapeDtypeStruct((N_DEV, r // 2, cols), BF16), memory_space=pltpu.MemorySpace.HBM)
               for r, cols in dims]

    @pl.kernel(mesh=plsc.ScalarSubcoreMesh(axis_name="sequencer", num_cores=1), name=name,
               scratch_types=[pltpu.SemaphoreType.DMA((n_t,))] * 2,
               compiler_params=pltpu.CompilerParams(collective_id=collective_id))
    def launch(send_sem, recv_sem):
        x, y, c = lax.axis_index("x"), lax.axis_index("y"), lax.axis_index("c")
        me = 4 * x + 2 * y + c
        peers = _others()
        _handshake(peers)
        for t in range(n_t):
            h = dims[t][0] // 2
            for qx, qy, qc in peers:
                pltpu.make_async_remote_copy(
                    src_ref=srcs[t].at[2 * qx + qy, pl.ds(qc * h, h), :], dst_ref=inboxes[t].at[me],
                    send_sem=send_sem.at[t], recv_sem=recv_sem.at[t], device_id=(qx, qy, qc), device_id_type=MESH).start()
        for t in range(n_t):
            win = inboxes[t].at[pl.ds(0, N_DEV - 1)]
            both = pltpu.make_async_remote_copy(src_ref=win, dst_ref=win, send_sem=send_sem.at[t],
                                                recv_sem=recv_sem.at[t], device_id=peers[0], device_id_type=MESH)
            both.wait_recv()
            both.wait_send()

    launch()
    return [b[...] for b in inboxes]


def _sum_pieces(part, inbox, place, name):
    _, r, cols = part.shape
    h = r // 2
    rows = _chunk_rows(h, cols)
    steps = h // rows

    def body(place_ref, own_ref, in_ref, o_ref):
        dev = place_ref[2]
        own = own_ref[...].astype(F32)
        acc = jnp.zeros((rows, cols), F32)
        for d in range(N_DEV):
            acc = acc + jnp.where(dev == d, own, in_ref[d].astype(F32))
        o_ref[...] = acc

    return pl.pallas_call(
        body, name=name,
        grid_spec=pltpu.PrefetchScalarGridSpec(
            num_scalar_prefetch=1, grid=(steps,),
            in_specs=[pl.BlockSpec((None, rows, cols), lambda i, pr: (pr[0], pr[1] * steps + i, 0)),
                      pl.BlockSpec((N_DEV, rows, cols), lambda i, pr: (0, i, 0))],
            out_specs=pl.BlockSpec((rows, cols), lambda i, pr: (pr[1] * steps + i, 0))),
        out_shape=jax.ShapeDtypeStruct((r, cols), F32),
    )(place, part, inbox)


def _sequencer_swap_halves(halves, name, collective_id):
    n_t = len(halves)
    refs = [jax.new_ref(a, memory_space=pltpu.MemorySpace.HBM) for a in halves]

    @pl.kernel(mesh=plsc.ScalarSubcoreMesh(axis_name="sequencer", num_cores=1), name=name,
               scratch_types=[pltpu.SemaphoreType.DMA((n_t,))] * 2,
               compiler_params=pltpu.CompilerParams(collective_id=collective_id))
    def launch(send_sem, recv_sem):
        x, y, c = lax.axis_index("x"), lax.axis_index("y"), lax.axis_index("c")
        sibling = (x, y, 1 - c)
        _handshake([sibling])
        copies = []
        for t in range(n_t):
            h = halves[t].shape[0] // 2
            win = refs[t].at[pl.ds(c * h, h), :]
            copies.append(pltpu.make_async_remote_copy(src_ref=win, dst_ref=win, send_sem=send_sem.at[t],
                                                       recv_sem=recv_sem.at[t], device_id=sibling, device_id_type=MESH))
            copies[-1].start()
        for cp in copies:
            cp.wait()

    launch()
    return [r[...] for r in refs]


def _cond_rows(c16, w, act, name):
    n_l, dm, wid = w.shape

    def body(c_ref, w_ref, o_ref, a_ref):
        cv = c_ref[...]
        if act:
            cv = cv * _sig(cv)
        a_ref[...] = cv
        o_ref[...] = _dot_f32(cv, w_ref[...])

    return pl.pallas_call(
        body, name=name, grid=(n_l,),
        in_specs=[_full((16, dm)), pl.BlockSpec((None, dm, wid), lambda l: (l, 0, 0))],
        out_specs=[pl.BlockSpec((None, 16, wid), lambda l: (l, 0, 0)), _full((16, dm))],
        out_shape=[jax.ShapeDtypeStruct((n_l, 16, wid), F32), jax.ShapeDtypeStruct((16, dm), F32)],
    )(c16, w)


def _outer_grad(ct, dm, name):
    n_l, kk, wid = dm.shape
    d_rows = ct.shape[0]

    def body(c_ref, d_ref, o_ref):
        o_ref[...] = _dot_f32(c_ref[...], d_ref[...])

    return pl.pallas_call(
        body, name=name, grid=(n_l,),
        in_specs=[_full((d_rows, kk)), pl.BlockSpec((None, kk, wid), lambda l: (l, 0, 0))],
        out_specs=pl.BlockSpec((None, d_rows, wid), lambda l: (l, 0, 0)),
        out_shape=jax.ShapeDtypeStruct((n_l, d_rows, wid), F32),
    )(ct, dm)


def _sum_devices(g, name):
    rows, n = g.shape

    def body(g_ref, o_ref):
        acc = g_ref[0:SUBLANES, :]
        for dev in range(1, N_DEV):
            acc = acc + g_ref[dev * SUBLANES:(dev + 1) * SUBLANES, :]
        o_ref[...] = acc

    return pl.pallas_call(body, name=name, out_shape=jax.ShapeDtypeStruct((SUBLANES, n), F32))(g)


def _adamw(w, g, m, v, name):
    shape = w.shape
    cols = shape[-1]
    rows = w.size // cols
    tr = rows
    for cand in range(SUBLANES, min(rows, 256) + 1, SUBLANES):
        if rows % cand == 0:
            tr = cand
    if rows * cols <= COMM_CHUNK_ELEMS:
        tr = rows
    c1 = 1.0 / (1.0 - ADAM_B1 ** ADAM_STEP)
    c2 = 1.0 / (1.0 - ADAM_B2 ** ADAM_STEP)

    def body(w_ref, g_ref, m_ref, v_ref, d_ref, mo_ref, vo_ref):
        gv = g_ref[...]
        m_new = ADAM_B1 * m_ref[...] + (1.0 - ADAM_B1) * gv
        v_new = ADAM_B2 * v_ref[...] + (1.0 - ADAM_B2) * (gv * gv)
        mo_ref[...] = m_new
        vo_ref[...] = v_new
        d_ref[...] = -ADAM_LR * ((m_new * c1) / (jnp.sqrt(v_new * c2) + ADAM_EPS) + ADAM_WD * w_ref[...])

    spec = pl.BlockSpec((tr, cols), lambda i: (i, 0))
    outs = pl.pallas_call(
        body, name=name, grid=(rows // tr,), in_specs=[spec] * 4, out_specs=[spec] * 3,
        out_shape=[jax.ShapeDtypeStruct((rows, cols), F32)] * 3,
    )(*[a.reshape(rows, cols) for a in (w, g, m, v)])
    return tuple(o.reshape(shape) for o in outs)


def _pad_cols(a, cols):
    return jnp.pad(a, [(0, 0)] * (a.ndim - 1) + [(0, cols - a.shape[-1])])


def _flat8(parts, width):
    v = jnp.concatenate([p.reshape(-1) for p in parts])
    return jnp.pad(v, (0, width - v.shape[0])).reshape(SUBLANES, width // SUBLANES)


KV_SHARD = 514
KV_SHARD_PAD = 640
BIG = ("a_w_in", "a_w_out", "kv_w", "b_w_q", "b_w_out", "up0", "up1", "down0", "down1")


def kernel(x, c, ada_w, ada_b, a_w_in, a_lb_logits, a_norm_g, a_w_out, kv_ada_w, kv_ada_b, kv_w, kv_b_f, k_norm_g, b_w_q, q_norm_g, b_w_out, ffn_w_up, ffn_conv_w, ffn_conv_b, ffn_w_down, loss_target, m_ada_w, m_ada_b, m_a_w_in, m_a_lb_logits, m_a_norm_g, m_a_w_out, m_kv_ada_w, m_kv_ada_b, m_kv_w, m_kv_b_f, m_k_norm_g, m_b_w_q, m_q_norm_g, m_b_w_out, m_ffn_w_up, m_ffn_conv_w, m_ffn_conv_b, m_ffn_w_down, v_ada_w, v_ada_b, v_a_w_in, v_a_lb_logits, v_a_norm_g, v_a_w_out, v_kv_ada_w, v_kv_ada_b, v_kv_w, v_kv_b_f, v_k_norm_g, v_b_w_q, v_q_norm_g, v_b_w_out, v_ffn_w_up, v_ffn_conv_w, v_ffn_conv_b, v_ffn_w_down):
    dm, ff = D_MODEL, D_FF
    ix, iy, ic = lax.axis_index("x"), lax.axis_index("y"), lax.axis_index("c")
    chip = 2 * ix + iy
    dev = 2 * chip + ic

    w1 = 10240
    g1 = _allgather8(_flat8([c, a_lb_logits, ffn_conv_w], w1), "gather_cond").reshape(N_DEV, w1)
    c_all = g1[:, :dm]
    per_chip = g1[0::2]
    lb_logits = per_chip[:, dm:dm + 512].reshape(N_CHIPS, 2, 256).transpose(1, 0, 2).reshape(2, dm)
    conv_w = per_chip[:, dm + 512:dm + 512 + 2 * CONV_W * FFN_COLS].reshape(N_CHIPS, 2, CONV_W, FFN_COLS)
    conv_w = conv_w.transpose(1, 2, 0, 3).reshape(2, CONV_W, 2, ff).transpose(0, 2, 1, 3)
    conv_b = ffn_conv_b.reshape(2, 2, 1, ff)
    lb = jax.nn.softmax(lb_logits, axis=0)[0:1]

    c16 = jnp.pad(c_all, ((0, 8), (0, 0)))
    mod_ada, c_act16 = _cond_rows(c16, ada_w, True, "mod_ada")
    mod_kv, _ = _cond_rows(c16, kv_ada_w[None], True, "mod_kv")
    mine = jnp.concatenate([mod_ada[0, :8], mod_ada[1, :8], mod_kv[0, :8]], axis=1)
    w2 = mine.shape[1]
    g2 = _allgather8(mine, "gather_mod").reshape(N_DEV, 8, w2)[0::2]
    my_rows = lax.dynamic_index_in_dim(g2, dev, axis=1, keepdims=False)
    mod0 = my_rows[:, 0:1536].reshape(6 * dm) + ada_b[0]
    mod1 = my_rows[:, 1536:3072].reshape(6 * dm) + ada_b[1]
    modk = my_rows[:, 3072:3584].reshape(2 * dm) + kv_ada_b
    mods = {"l0": [v.reshape(1, dm) for v in jnp.split(mod0, 6)],
            "l1": [v.reshape(1, dm) for v in jnp.split(mod1, 6)],
            "kv": [v.reshape(1, dm) for v in jnp.split(modk, 2)]}

    local = [(a_w_in, 0), (a_w_out, 0), (_pad_cols(kv_w, KV_SHARD_PAD)[None], 0), (b_w_q, 0), (b_w_out, 0),
             (ffn_w_up, 0), (ffn_w_up, 1), (ffn_w_down, 0), (ffn_w_down, 1)]
    chip_arr = chip.reshape(1).astype(jnp.int32)
    own = {n: _cast_own_block(w, layer, chip_arr, "cast_" + n) for n, (w, layer) in zip(BIG, local)}
    stages = {"mixer0": ("a_w_in",), "ffn0": ("a_w_out", "up0", "down0"),
              "layer1": ("kv_w", "b_w_q", "b_w_out", "up1", "down1")}
    arriving = {st: _sequencer_gather([own[n] for n in names], "gather_" + st, cid)
                for cid, (st, names) in enumerate(stages.items(), start=1)}
    rowwise = lambda g: g.reshape(1, -1, dm)

    def weights_at(stage, token):
        got, token = lax.optimization_barrier((arriving[stage], token))
        g = dict(zip(stages[stage], got))
        if stage == "mixer0":
            return {"a_w_in": g["a_w_in"]}, token
        if stage == "ffn0":
            return {"a_w_out": rowwise(g["a_w_out"]), "up0": g["up0"], "down0": rowwise(g["down0"])}, token
        kv_full = g["kv_w"][:, :, :KV_SHARD].transpose(1, 0, 2).reshape(dm, N_CHIPS * KV_SHARD)
        return {"kv_k": kv_full[None, :, :dm], "kv_v": kv_full[None, :, dm:2 * dm],
                "kv_f": _pad_cols(kv_full[None, :, 2 * dm:], LANES), "b_w_q": g["b_w_q"],
                "b_w_out": rowwise(g["b_w_out"]), "up1": g["up1"], "down1": rowwise(g["down1"])}, token

    vecs = {"a_norm_g": jnp.tile(a_norm_g, (1, HEADS)), "k_norm_g": jnp.tile(k_norm_g[None], (1, HEADS)),
            "q_norm_g": jnp.tile(q_norm_g, (1, HEADS)), "kv_b_f": _pad_cols(kv_b_f[None], LANES),
            "conv_w0": conv_w[0], "conv_b0": conv_b[0], "conv_w1": conv_w[1], "conv_b1": conv_b[1]}

    sq, grad_x, big, small, marks = _local_step(x[0], loss_target[0], mods, lb, vecs, weights_at)
    loss = lax.psum(0.5 * jnp.sum(sq) / dm, ("x", "y", "c"))

    kv_grad = jnp.concatenate([big["kv_k"][0], big["kv_v"][0], big["kv_f"][0][:, :HEADS]], axis=1)
    kv_grad = _pad_cols(kv_grad.reshape(dm, N_CHIPS, KV_SHARD).transpose(1, 0, 2), KV_SHARD_PAD)
    chipwise = lambda g: g.reshape(N_CHIPS, -1, dm)
    parts = dict(zip(BIG, [big["a_w_in"], chipwise(big["a_w_out"]), kv_grad, big["b_w_q"], chipwise(big["b_w_out"]),
                           big["up0"], big["up1"], chipwise(big["down0"]), chipwise(big["down1"])]))
    place = jnp.stack([chip, ic, dev]).astype(jnp.int32)

    served = []
    boxes = {}

    groups = (("up1", "down1"), ("b_w_out", "b_w_q", "kv_w"), ("up0", "down0", "a_w_out"), ("a_w_in",))

    def scatter_group(k):
        mine = [parts[n] for n in groups[k]]
        if served:
            mine, _ = lax.optimization_barrier((mine, served[-1]))
        boxes[k] = _sequencer_scatter(mine, "scatter_grads_%d" % k, 4 + k)
        served.append(boxes[k])

    def sum_group(k, token):
        inboxes, _ = lax.optimization_barrier((boxes[k], token))
        return [_sum_pieces(parts[n], box, place, "sum_" + n) for n, box in zip(groups[k], inboxes)]

    def swap_group(k, halves, behind):
        halves, _ = lax.optimization_barrier((halves, behind))
        return dict(zip(groups[k], _sequencer_swap_halves(halves, "swap_grads_%d" % k, 8 + k)))

    for k in range(3):
        scatter_group(k)
    halves = [sum_group(0, marks["attention_bwd"]), sum_group(1, marks["ffn0_bwd"]), sum_group(2, marks["mixer0_bwd"])]

    fold = lambda a: a.sum(axis=0)
    heads = lambda a: fold(a).reshape(HEADS, HEAD_DIM).sum(axis=0)
    conv_flat = lambda a: a.sum(axis=2).transpose(1, 0, 2)
    pieces = ([fold(a) for a in small["mod_l0"]] + [fold(a) for a in small["mod_l1"]] + [fold(a) for a in small["mod_kv"]]
              + [conv_flat(small["conv0"]), conv_flat(small["conv1"]), heads(small["a_norm_g"]), heads(small["k_norm_g"]),
                 heads(small["q_norm_g"]), fold(small["kv_b_f"]), fold(small["lb"])])
    w3 = 61440
    small_vec, _ = lax.optimization_barrier((_flat8(pieces, w3), served[2]))
    g3 = _sequencer_allgather8(small_vec, dev, "gather_small", 12)
    served.append(g3)
    scatter_group(3)
    rs = {}
    for k in range(3):
        rs.update(swap_group(k, halves[k], g3))
    tot = _sum_devices(g3, "sum_small").reshape(w3)
    n_mod = 14 * dm
    dmod_all = g3.reshape(N_DEV, w3)[:, :n_mod]
    o = n_mod
    conv_tot = [tot[o + l * 8 * ff: o + (l + 1) * 8 * ff].reshape(4, 2 * ff) for l in range(2)]
    o += 16 * ff
    g_a_norm, g_k_norm, g_q_norm = (tot[o + i * HEAD_DIM: o + (i + 1) * HEAD_DIM] for i in range(3))
    o += 3 * HEAD_DIM
    g_kv_b_f = tot[o:o + HEADS]
    dlb = tot[o + LANES:o + LANES + dm]

    ct = _pad_cols(c_act16[:8].T, LANES)
    dmod_pad = jnp.pad(dmod_all, ((0, LANES - N_DEV), (0, 0)))
    cols_ada = jnp.stack([lax.dynamic_slice_in_dim(dmod_pad, l * 6 * dm + chip * 1536, 1536, axis=1) for l in range(2)])
    cols_kv = lax.dynamic_slice_in_dim(dmod_pad, 12 * dm + chip * 512, 512, axis=1)[None]
    g_ada_w = _outer_grad(ct, cols_ada, "grad_ada_w")
    g_kv_ada_w = _outer_grad(ct, cols_kv, "grad_kv_ada_w")[0]

    my_lb = lax.dynamic_slice_in_dim(lb[0], chip * 256, 256)
    l0 = lax.dynamic_slice_in_dim(dlb, chip * 256, 256) * my_lb * (1.0 - my_lb)
    grads = {
        "ada_w": g_ada_w, "ada_b": jnp.stack([tot[:6 * dm], tot[6 * dm:12 * dm]]),
        "a_lb_logits": jnp.stack([l0, -l0]), "a_norm_g": g_a_norm[None],
        "a_w_out": rs["a_w_out"][None], "kv_ada_w": g_kv_ada_w, "kv_ada_b": tot[12 * dm:14 * dm],
        "kv_w": rs["kv_w"][:, :KV_SHARD], "kv_b_f": g_kv_b_f, "k_norm_g": g_k_norm,
        "b_w_q": rs["b_w_q"][None], "q_norm_g": g_q_norm[None], "b_w_out": rs["b_w_out"][None],
        "ffn_w_up": jnp.stack([rs["up0"], rs["up1"]]),
        "ffn_conv_w": jnp.stack([lax.dynamic_slice_in_dim(ct_l[:CONV_W], chip * FFN_COLS, FFN_COLS, axis=1) for ct_l in conv_tot]),
        "ffn_conv_b": jnp.stack([ct_l[CONV_W] for ct_l in conv_tot]),
        "ffn_w_down": jnp.stack([rs["down0"], rs["down1"]]),
    }
    weights = dict(ada_w=ada_w, ada_b=ada_b, a_w_in=a_w_in, a_lb_logits=a_lb_logits, a_norm_g=a_norm_g, a_w_out=a_w_out,
                   kv_ada_w=kv_ada_w, kv_ada_b=kv_ada_b, kv_w=kv_w, kv_b_f=kv_b_f, k_norm_g=k_norm_g, b_w_q=b_w_q,
                   q_norm_g=q_norm_g, b_w_out=b_w_out, ffn_w_up=ffn_w_up, ffn_conv_w=ffn_conv_w, ffn_conv_b=ffn_conv_b,
                   ffn_w_down=ffn_w_down)
    m_in = dict(ada_w=m_ada_w, ada_b=m_ada_b, a_w_in=m_a_w_in, a_lb_logits=m_a_lb_logits, a_norm_g=m_a_norm_g,
                a_w_out=m_a_w_out, kv_ada_w=m_kv_ada_w, kv_ada_b=m_kv_ada_b, kv_w=m_kv_w, kv_b_f=m_kv_b_f,
                k_norm_g=m_k_norm_g, b_w_q=m_b_w_q, q_norm_g=m_q_norm_g, b_w_out=m_b_w_out, ffn_w_up=m_ffn_w_up,
                ffn_conv_w=m_ffn_conv_w, ffn_conv_b=m_ffn_conv_b, ffn_w_down=m_ffn_w_down)
    v_in = dict(ada_w=v_ada_w, ada_b=v_ada_b, a_w_in=v_a_w_in, a_lb_logits=v_a_lb_logits, a_norm_g=v_a_norm_g,
                a_w_out=v_a_w_out, kv_ada_w=v_kv_ada_w, kv_ada_b=v_kv_ada_b, kv_w=v_kv_w, kv_b_f=v_kv_b_f,
                k_norm_g=v_k_norm_g, b_w_q=v_b_w_q, q_norm_g=v_q_norm_g, b_w_out=v_b_w_out, ffn_w_up=v_ffn_w_up,
                ffn_conv_w=v_ffn_conv_w, ffn_conv_b=v_ffn_conv_b, ffn_w_down=v_ffn_w_down)

    names = list(weights)
    step = lambda n: _adamw(weights[n], grads[n], m_in[n], v_in[n], "adamw_" + n)
    grads = {n: g.reshape(weights[n].shape) for n, g in grads.items()}
    upd = {n: step(n) for n in names if n != "a_w_in"}
    last = sum_group(3, [u[0] for u in upd.values()])
    grads["a_w_in"] = swap_group(3, last, last)["a_w_in"][None]
    upd["a_w_in"] = step("a_w_in")
    return (loss, grad_x[None], *[grads[n] for n in names], *[upd[n][0] for n in names],
            *[upd[n][1] for n in names], *[upd[n][2] for n in names])
```

```python
import jax
import jax.numpy as jnp
from jax import lax
from jax.experimental import pallas as pl
from jax.experimental.pallas import tpu as pltpu
from jax.experimental.pallas import tpu_sc as plsc

F32 = jnp.float32
BF16 = jnp.bfloat16

D_MODEL = 1024
HEADS = 8
HEAD_DIM = 128
A_CHUNK = 64
D_FF = 2816
CONV_W = 3
EPS = 1e-6
NEG_INF = -1e30
N_CHIPS = 4
N_DEV = 8

ADAM_LR = 0.001
ADAM_B1 = 0.9
ADAM_B2 = 0.999
ADAM_EPS = 1e-08
ADAM_WD = 0.01
ADAM_STEP = 10

SUBLANES = 8
BF16_ROWS = 16
LANES = 128
HALO = BF16_ROWS
ROW_TILE = 512
TOKEN_TILE_TN = 2048
FFN_COLS = 1408
FFN_ROWS = 256
HGRN_ROWS = 256
ATT_TILE = 512
ATT_SPLIT = 2
ATT_FWD_HEADS = 8
ATT_BWD_HEADS = 4
MESH = pl.DeviceIdType.MESH


def _sig(x):
    return jax.nn.sigmoid(x)


def _dot(a, b):
    return jnp.dot(a, b, preferred_element_type=F32)


def _dot_nt(a, b):
    return lax.dot_general(a, b, (((1,), (1,)), ((), ())), preferred_element_type=F32)


def _dot_tn(a, b):
    return lax.dot_general(a, b, (((0,), (0,)), ((), ())), preferred_element_type=F32)


def _split2(x):
    hi = x.astype(BF16)
    lo = (x - hi.astype(F32)).astype(BF16)
    return hi, lo


def _dot_f32(a, b):
    ah, al = _split2(a)
    bh, bl = _split2(b)
    return _dot(ah, bh) + _dot(ah, bl) + _dot(al, bh)


def _tri_dot(tri, x):
    hi = x.astype(BF16)
    r = x - hi.astype(F32)
    mid = r.astype(BF16)
    lo = (r - mid.astype(F32)).astype(BF16)
    return _dot(tri, hi) + _dot(tri, mid) + _dot(tri, lo)


def _tri(n, upper=False):
    r = lax.broadcasted_iota(jnp.int32, (n, n), 0)
    c = lax.broadcasted_iota(jnp.int32, (n, n), 1)
    keep = (c >= r) if upper else (c <= r)
    return jnp.where(keep, 1.0, 0.0).astype(BF16)


def _colsum8(v):
    rows, n = v.shape
    return v.reshape(rows // SUBLANES, SUBLANES, n).sum(axis=0)


def _full(shape):
    nd = len(shape)
    return pl.BlockSpec(shape, lambda *_: (0,) * nd)


def _tile(n, want):
    t = min(n, want)
    assert n % t == 0, (n, t)
    return t


def _mm_nn(a, w, groups, out_dtype, name):
    m_rows, k = a.shape
    p_n, _, n = w.shape
    per = p_n // groups
    tm = _tile(m_rows, ROW_TILE)

    def body(a_ref, w_ref, o_ref):
        av = a_ref[...]
        for p in range(p_n):
            o_ref[p // per, :, (p % per) * n:(p % per + 1) * n] = _dot(av, w_ref[p]).astype(out_dtype)

    return pl.pallas_call(
        body, name=name, grid=(m_rows // tm,),
        in_specs=[pl.BlockSpec((tm, k), lambda i: (i, 0)), _full((p_n, k, n))],
        out_specs=pl.BlockSpec((groups, tm, per * n), lambda i: (0, i, 0)),
        out_shape=jax.ShapeDtypeStruct((groups, m_rows, per * n), out_dtype),
    )(a, w)


def _mm_nt(d, w, out_dtype, name):
    g_n, m_rows, _ = d.shape
    p_n, k, n = w.shape
    per = p_n // g_n
    tm = _tile(m_rows, ROW_TILE)

    def body(d_ref, w_ref, o_ref):
        acc = None
        for p in range(p_n):
            t = _dot_nt(d_ref[p // per, :, (p % per) * n:(p % per + 1) * n], w_ref[p])
            acc = t if acc is None else acc + t
        o_ref[...] = acc.astype(out_dtype)

    return pl.pallas_call(
        body, name=name, grid=(m_rows // tm,),
        in_specs=[pl.BlockSpec((g_n, tm, per * n), lambda i: (0, i, 0)), _full((p_n, k, n))],
        out_specs=pl.BlockSpec((tm, k), lambda i: (i, 0)),
        out_shape=jax.ShapeDtypeStruct((m_rows, k), out_dtype),
    )(d, w)


def _mm_tn(a, d, p_n, name):
    m_rows, k = a.shape
    g_n, _, w_cols = d.shape
    per = p_n // g_n
    n = w_cols // per
    tm = _tile(m_rows, TOKEN_TILE_TN if k <= D_MODEL else ROW_TILE)
    steps = m_rows // tm

    def body(a_ref, d_ref, o_ref, acc):
        m = pl.program_id(1)

        @pl.when(m == 0)
        def _():
            acc[...] = jnp.zeros_like(acc)

        acc[...] += _dot_tn(a_ref[...], d_ref[...])

        @pl.when(m == steps - 1)
        def _():
            o_ref[...] = acc[...].astype(BF16)

    return pl.pallas_call(
        body, name=name, grid=(p_n, steps),
        in_specs=[pl.BlockSpec((tm, k), lambda p, m: (m, 0)),
                  pl.BlockSpec((None, tm, n), lambda p, m: (p // per, m, p % per))],
        out_specs=pl.BlockSpec((None, k, n), lambda p, m: (p, 0, 0)),
        out_shape=jax.ShapeDtypeStruct((p_n, k, n), BF16),
        scratch_shapes=[pltpu.VMEM((k, n), F32)],
    )(a, d)


def _premix(x, shift, scale, name):
    s, dm = x.shape
    tm = _tile(s, ROW_TILE)

    def body(x_ref, sh_ref, sc_ref, h_ref):
        xv = x_ref[...]
        inv = lax.rsqrt(jnp.mean(xv * xv, axis=-1, keepdims=True) + EPS)
        h_ref[...] = (xv * inv * (1.0 + sc_ref[...]) + sh_ref[...]).astype(BF16)

    row = pl.BlockSpec((tm, dm), lambda i: (i, 0))
    vec = _full((1, dm))
    return pl.pallas_call(body, name=name, grid=(s // tm,), in_specs=[row, vec, vec], out_specs=row,
                          out_shape=jax.ShapeDtypeStruct((s, dm), BF16))(x, shift, scale)


def _premix_bwd(x, terms, dres, name, branch=None):
    s, dm = x.shape
    tm = _tile(s, ROW_TILE)
    pairs = [pr for _, prs in terms for pr in prs]
    n_in = 2 + len(terms) + 2 * len(pairs) + (2 if branch else 0)

    def body(*refs):
        x_ref, dres_ref = refs[:2]
        sc_refs = refs[2:2 + len(terms)]
        mm_refs = refs[2 + len(terms):2 + len(terms) + 2 * len(pairs)]
        outs = refs[n_in:]

        @pl.when(pl.program_id(0) == 0)
        def _():
            for o in outs[1:1 + 2 * len(terms)]:
                o[...] = jnp.zeros_like(o)
            if branch:
                outs[-1][...] = jnp.zeros_like(outs[-1])

        xv = x_ref[...]
        inv = lax.rsqrt(jnp.mean(xv * xv, axis=-1, keepdims=True) + EPS)
        r = xv * inv
        dx = dres_ref[...]
        k = 0
        for t, (_, prs) in enumerate(terms):
            dh = None
            for d, w in prs:
                d_ref, w_ref = mm_refs[2 * k], mm_refs[2 * k + 1]
                k += 1
                p_n, _, n = w.shape
                per = p_n // d.shape[0]
                for p in range(p_n):
                    part = _dot_nt(d_ref[p // per, :, (p % per) * n:(p % per + 1) * n], w_ref[p])
                    dh = part if dh is None else dh + part
            dr = dh * (1.0 + sc_refs[t][...])
            dx = dx + inv * (dr - r * jnp.mean(dr * r, axis=-1, keepdims=True))
            outs[1 + 2 * t][...] += _colsum8(dh)
            outs[2 + 2 * t][...] += _colsum8(dh * r)
        outs[0][...] = dx
        if branch:
            y_ref, g_ref = refs[n_in - 2:n_in]
            outs[-2][0] = (dx * g_ref[...]).astype(BF16)
            outs[-1][...] += _colsum8(dx * y_ref[...])

    row = pl.BlockSpec((tm, dm), lambda i: (i, 0))
    vec, acc = _full((1, dm)), _full((SUBLANES, dm))
    ins, specs = [x, dres] + [sc for sc, _ in terms], [row, row] + [vec] * len(terms)
    for d, w in pairs:
        ins += [d, w]
        specs += [pl.BlockSpec((d.shape[0], tm, d.shape[2]), lambda i: (0, i, 0)), _full(w.shape)]
    out_shape = [jax.ShapeDtypeStruct((s, dm), F32)] + [jax.ShapeDtypeStruct((SUBLANES, dm), F32)] * (2 * len(terms))
    out_specs = [row] + [acc] * (2 * len(terms))
    if branch:
        ins += list(branch)
        specs += [row, vec]
        out_shape += [jax.ShapeDtypeStruct((1, s, dm), BF16), jax.ShapeDtypeStruct((SUBLANES, dm), F32)]
        out_specs += [pl.BlockSpec((1, tm, dm), lambda i: (0, i, 0)), acc]
    outs = pl.pallas_call(body, name=name, grid=(s // tm,), in_specs=specs, out_specs=out_specs,
                          out_shape=out_shape)(*ins)
    partials = [(outs[1 + 2 * t], outs[2 + 2 * t]) for t in range(len(terms))]
    return (outs[0], partials) + ((outs[-2], outs[-1]) if branch else ())


def _conv_taps(e, w, b):
    return w[2:3] * e + w[1:2] * pltpu.roll(e, 1, 0) + w[0:1] * pltpu.roll(e, 2, 0) + b


def _ffn_specs(s, tm, cb):
    hb = tm // HALO
    last = s // HALO - 1
    main = pl.BlockSpec((2, tm, cb), lambda j, i: (0, i, j))
    prev = pl.BlockSpec((2, HALO, cb), lambda j, i: (0, jnp.maximum(i * hb - 1, 0), j))
    nxt = pl.BlockSpec((2, HALO, cb), lambda j, i: (0, jnp.minimum((i + 1) * hb, last), j))
    wspec = pl.BlockSpec((2, CONV_W, cb), lambda j, i: (0, 0, j))
    bspec = pl.BlockSpec((2, 1, cb), lambda j, i: (0, 0, j))
    return main, prev, nxt, wspec, bspec


def _convglu_bwd(u, c, dffn, w_down, w, name):
    _, s, f = u.shape
    dm = dffn.shape[2]
    tm = _tile(s, 256)
    cb = _tile(f, FFN_COLS)
    steps = s // tm
    n_ext = tm + HALO
    main, _, nxt, wspec, _ = _ffn_specs(s, tm, cb)
    hb = tm // HALO
    last = s // HALO - 1
    d_main = pl.BlockSpec((None, tm, dm), lambda j, i: (0, i, 0))
    d_next = pl.BlockSpec((None, HALO, dm), lambda j, i: (0, jnp.minimum((i + 1) * hb, last), 0))
    wd_spec = pl.BlockSpec((None, cb, dm), lambda j, i: (0, j, 0))

    def body(u_ref, c_ref, cn_ref, d_ref, dn_ref, wd_ref, w_ref, du_ref, acc_ref):
        i = pl.program_id(1)
        notlast = jnp.where(i < steps - 1, 1.0, 0.0)

        @pl.when(i == 0)
        def _():
            acc_ref[...] = jnp.zeros_like(acc_ref)

        gate, val = (jnp.concatenate([c_ref[g].astype(F32), cn_ref[g].astype(F32)], axis=0) for g in range(2))
        wd = wd_ref[...]
        da = jnp.concatenate([_dot_nt(d_ref[...], wd).astype(BF16).astype(F32),
                              _dot_nt(dn_ref[...], wd).astype(BF16).astype(F32) * notlast], axis=0)
        sg = _sig(gate)
        d_val = da * gate * sg
        d_gate = da * val * (sg * (1.0 + gate * (1.0 - sg)))

        def finish(g, d):
            wv = w_ref[g]
            d1, d2 = pltpu.roll(d, n_ext - 1, 0), pltpu.roll(d, n_ext - 2, 0)
            du_ref[g] = (wv[2:3] * d + wv[1:2] * d1 + wv[0:1] * d2)[0:tm].astype(BF16)
            uv = u_ref[g].astype(F32)
            acc_ref[g, 2] += _colsum8(d[0:tm] * uv)
            acc_ref[g, 1] += _colsum8(d1[0:tm] * uv)
            acc_ref[g, 0] += _colsum8(d2[0:tm] * uv)
            acc_ref[g, 3] += _colsum8(d[0:tm])

        finish(0, d_gate)
        finish(1, d_val)

    return pl.pallas_call(
        body, name=name, grid=(f // cb, steps),
        in_specs=[main, main, nxt, d_main, d_next, wd_spec, wspec],
        out_specs=[main, pl.BlockSpec((2, 4, SUBLANES, cb), lambda j, i: (0, 0, 0, j))],
        out_shape=[jax.ShapeDtypeStruct((2, s, f), BF16), jax.ShapeDtypeStruct((2, 4, SUBLANES, f), F32)],
    )(u, c, c, dffn, dffn, w_down, w)


def _hgrn_gates(q_raw, f_raw, lb, tri):
    sf = _sig(f_raw)
    fg = lb + (1.0 - lb) * sf
    b = _tri_dot(tri, jnp.log(fg))
    return q_raw * _sig(q_raw), 1.0 - fg, b, fg, sf


def _hgrn_fwd(proj, lb, norm_g, name):
    s = proj.shape[0]
    tb = _tile(s, HGRN_ROWS)
    n_c = tb // A_CHUNK
    half = A_CHUNK // 2

    def body(q_ref, f_ref, v_ref, g_ref, lb_ref, ng_ref, o_ref, yp_ref, st_ref, state):
        @pl.when(pl.program_id(0) == 0)
        def _():
            state[...] = jnp.zeros_like(state)

        tri = _tri(A_CHUNK)
        causal = lax.broadcasted_iota(jnp.int32, (A_CHUNK, A_CHUNK), 1) <= lax.broadcasted_iota(
            jnp.int32, (A_CHUNK, A_CHUNK), 0)

        def chunk(ci, carry):
            rows = pl.ds(pl.multiple_of(ci * A_CHUNK, A_CHUNK), A_CHUNK)
            heads = [slice(h * HEAD_DIM, (h + 1) * HEAD_DIM) for h in range(HEADS)]
            qs, k, b, _, _ = _hgrn_gates(q_ref[rows, :], f_ref[rows, :], lb_ref[...], tri)
            b_mid, b_last = b[half:half + 1], b[A_CHUNK - 1:A_CHUNK]
            q_i = (qs * jnp.exp(b - b_mid)).astype(BF16)
            k_i = (k * jnp.exp(b_mid - b)).astype(BF16)
            q_e = (qs * jnp.exp(b)).astype(BF16)
            k_s = (k * jnp.exp(b_last - b)).astype(BF16)
            decay = jnp.exp(b_last)
            vb = v_ref[rows, :].astype(BF16)
            scores = [jnp.where(causal, _dot_nt(q_i[:, cs], k_i[:, cs]), 0.0).astype(BF16) for cs in heads]
            st = [state[h] for h in range(HEADS)]
            outs = [_dot(scores[h], vb[:, cs]) + _dot_nt(q_e[:, cs], st[h].astype(BF16)) for h, cs in enumerate(heads)]
            for h, cs in enumerate(heads):
                st_ref[ci, h] = st[h]
                state[h] = st[h] * decay[:, cs] + _dot_tn(vb[:, cs], k_s[:, cs])
            o = jnp.concatenate(outs, axis=1)
            o_ref[rows, :] = o
            sq = o * o
            inv = jnp.concatenate([jnp.broadcast_to(lax.rsqrt(jnp.mean(sq[:, cs], axis=-1, keepdims=True) + EPS),
                                                    (A_CHUNK, HEAD_DIM)) for cs in heads], axis=1)
            g_raw = g_ref[rows, :]
            yp_ref[rows, :] = (o * inv * ng_ref[...] * (g_raw * _sig(g_raw))).astype(BF16)
            return carry

        lax.fori_loop(0, n_c, chunk, 0)

    col = lambda j: pl.BlockSpec((tb, D_MODEL), lambda i: (i, j))
    vec = _full((1, D_MODEL))
    return pl.pallas_call(
        body, name=name, grid=(s // tb,), in_specs=[col(0), col(1), col(2), col(3), vec, vec],
        out_specs=[col(0), col(0), pl.BlockSpec((n_c, HEADS, HEAD_DIM, HEAD_DIM), lambda i: (i, 0, 0, 0))],
        out_shape=[jax.ShapeDtypeStruct((s, D_MODEL), F32), jax.ShapeDtypeStruct((s, D_MODEL), BF16),
                   jax.ShapeDtypeStruct((s // A_CHUNK, HEADS, HEAD_DIM, HEAD_DIM), F32)],
        scratch_shapes=[pltpu.VMEM((HEADS, HEAD_DIM, HEAD_DIM), F32)],
    )(proj, proj, proj, proj, lb, norm_g)


def _hgrn_bwd(proj, lb, norm_g, o, states, dyp, name):
    s = proj.shape[0]
    tb = _tile(s, HGRN_ROWS)
    n_c = tb // A_CHUNK
    n_b = s // tb
    half = A_CHUNK // 2

    def body(q_ref, f_ref, v_ref, g_ref, lb_ref, ng_ref, o_ref, st_ref, dyp_ref, dp_ref, dlb_ref, dng_ref, dstate):
        @pl.when(pl.program_id(0) == 0)
        def _():
            dstate[...] = jnp.zeros_like(dstate)
            dlb_ref[...] = jnp.zeros_like(dlb_ref)
            dng_ref[...] = jnp.zeros_like(dng_ref)

        tri = _tri(A_CHUNK)
        tri_up = _tri(A_CHUNK, upper=True)
        row_id = lax.broadcasted_iota(jnp.int32, (A_CHUNK, D_MODEL), 0)
        causal = lax.broadcasted_iota(jnp.int32, (A_CHUNK, A_CHUNK), 1) <= lax.broadcasted_iota(
            jnp.int32, (A_CHUNK, A_CHUNK), 0)

        def chunk(cj, carry):
            ci = n_c - 1 - cj
            rows = pl.ds(pl.multiple_of(ci * A_CHUNK, A_CHUNK), A_CHUNK)
            heads = [slice(h * HEAD_DIM, (h + 1) * HEAD_DIM) for h in range(HEADS)]
            cat = lambda parts: jnp.concatenate(parts, axis=1)
            per_head_mean = lambda a: cat([jnp.broadcast_to(jnp.mean(a[:, cs], axis=-1, keepdims=True),
                                                            (A_CHUNK, HEAD_DIM)) for cs in heads])
            q_raw, lbv = q_ref[rows, :], lb_ref[...]
            qs, k, b, fg, sf = _hgrn_gates(q_raw, f_ref[rows, :], lbv, tri)
            b_mid, b_last = b[half:half + 1], b[A_CHUNK - 1:A_CHUNK]
            e_qi, e_ki, e_q, e_ks = jnp.exp(b - b_mid), jnp.exp(b_mid - b), jnp.exp(b), jnp.exp(b_last - b)
            decay = jnp.exp(b_last)
            q_i, k_i, q_e, k_s = qs * e_qi, k * e_ki, qs * e_q, k * e_ks
            qib, kib, qeb, ksb = q_i.astype(BF16), k_i.astype(BF16), q_e.astype(BF16), k_s.astype(BF16)
            vb = v_ref[rows, :].astype(BF16)
            ov, g_raw, dy, ng = o_ref[rows, :], g_ref[rows, :], dyp_ref[rows, :], ng_ref[...]
            inv = lax.rsqrt(per_head_mean(ov * ov) + EPS)
            nrm = ov * inv
            sg = _sig(g_raw)
            gs = g_raw * sg
            dn = dy * ng * gs
            dng_ref[0:1, :] += jnp.sum(dy * nrm * gs, axis=0, keepdims=True)
            dg_raw = dy * nrm * ng * (sg * (1.0 + g_raw * (1.0 - sg)))
            do = (inv * (dn - nrm * per_head_mean(dn * nrm))).astype(BF16)
            st_prev = [st_ref[ci, h] for h in range(HEADS)]
            dst = [dstate[h] for h in range(HEADS)]
            dstb = [d.astype(BF16) for d in dst]
            scores = [jnp.where(causal, _dot_nt(qib[:, cs], kib[:, cs]), 0.0).astype(BF16) for cs in heads]
            d_scores = [jnp.where(causal, _dot_nt(do[:, cs], vb[:, cs]), 0.0).astype(BF16) for cs in heads]
            dv = cat([_dot_tn(scores[h], do[:, cs]) + _dot_nt(ksb[:, cs], dstb[h]) for h, cs in enumerate(heads)])
            dq_i = cat([_dot(d_scores[h], kib[:, cs]) for h, cs in enumerate(heads)])
            dk_i = cat([_dot_tn(d_scores[h], qib[:, cs]) for h, cs in enumerate(heads)])
            dq_e = cat([_dot(do[:, cs], st_prev[h].astype(BF16)) for h, cs in enumerate(heads)])
            dk_s = cat([_dot(vb[:, cs], dstb[h]) for h, cs in enumerate(heads)])
            d_decay = cat([jnp.sum(st_prev[h] * dst[h], axis=0, keepdims=True) for h in range(HEADS)])
            for h, cs in enumerate(heads):
                dstate[h] = dst[h] * decay[:, cs] + _dot_tn(do[:, cs], qeb[:, cs])
            dq = dq_i * e_qi + dq_e * e_q
            dk = dk_i * e_ki + dk_s * e_ks
            t_qi, t_ki, t_ks = dq_i * q_i, dk_i * k_i, dk_s * k_s
            db = t_qi - t_ki + dq_e * q_e - t_ks
            db_mid = jnp.sum(t_ki - t_qi, axis=0, keepdims=True)
            db_last = jnp.sum(t_ks, axis=0, keepdims=True) + d_decay * decay
            db = db + jnp.where(row_id == half, db_mid, 0.0) + jnp.where(row_id == A_CHUNK - 1, db_last, 0.0)
            dfg = _tri_dot(tri_up, db) / fg - dk
            dlb_ref[0:1, :] += jnp.sum(dfg * (1.0 - sf), axis=0, keepdims=True)
            sq = _sig(q_raw)
            dp_ref[0, rows, :] = (dq * (sq * (1.0 + q_raw * (1.0 - sq)))).astype(BF16)
            dp_ref[1, rows, :] = (dfg * (1.0 - lbv) * sf * (1.0 - sf)).astype(BF16)
            dp_ref[2, rows, :] = dv.astype(BF16)
            dp_ref[3, rows, :] = dg_raw.astype(BF16)
            return carry

        lax.fori_loop(0, n_c, chunk, 0)

    col = lambda j: pl.BlockSpec((tb, D_MODEL), lambda i: (n_b - 1 - i, j))
    vec = _full((1, D_MODEL))
    acc = _full((SUBLANES, D_MODEL))
    return pl.pallas_call(
        body, name=name, grid=(n_b,),
        in_specs=[col(0), col(1), col(2), col(3), vec, vec, col(0),
                  pl.BlockSpec((n_c, HEADS, HEAD_DIM, HEAD_DIM), lambda i: (n_b - 1 - i, 0, 0, 0)), col(0)],
        out_specs=[pl.BlockSpec((4, tb, D_MODEL), lambda i: (0, n_b - 1 - i, 0)), acc, acc],
        out_shape=[jax.ShapeDtypeStruct((4, s, D_MODEL), BF16), jax.ShapeDtypeStruct((SUBLANES, D_MODEL), F32),
                   jax.ShapeDtypeStruct((SUBLANES, D_MODEL), F32)],
        scratch_shapes=[pltpu.VMEM((HEADS, HEAD_DIM, HEAD_DIM), F32)],
    )(proj, proj, proj, proj, lb, norm_g, o, states, dyp)


def _headnorm(x, g, mult, name, col0=0):
    s = x.shape[0]
    tm = _tile(s, ROW_TILE)

    def body(x_ref, g_ref, y_ref):
        for h in range(HEADS):
            cs = slice(h * HEAD_DIM, (h + 1) * HEAD_DIM)
            xv = x_ref[:, cs]
            inv = lax.rsqrt(jnp.mean(xv * xv, axis=-1, keepdims=True) + EPS)
            y_ref[:, cs] = (xv * inv * g_ref[:, cs] * mult).astype(BF16)

    return pl.pallas_call(
        body, name=name, grid=(s // tm,),
        in_specs=[pl.BlockSpec((tm, D_MODEL), lambda i: (i, col0)), _full((1, D_MODEL))],
        out_specs=pl.BlockSpec((tm, D_MODEL), lambda i: (i, 0)),
        out_shape=jax.ShapeDtypeStruct((s, D_MODEL), BF16),
    )(x, g)


def _headnorm_bwd(x, g, mult, dy, name, col0=0, extra=None):
    s = x.shape[0]
    tm = _tile(s, ROW_TILE)
    groups = 2 if extra is not None else 1
    head_major = dy.ndim == 3

    def body(*refs):
        x_ref, g_ref, dy_ref = refs[:3]
        dx_ref, dg_ref = refs[-2:]

        @pl.when(pl.program_id(0) == 0)
        def _():
            dg_ref[...] = jnp.zeros_like(dg_ref)

        for h in range(HEADS):
            cs = slice(h * HEAD_DIM, (h + 1) * HEAD_DIM)
            xv, gv = x_ref[:, cs], g_ref[:, cs]
            dyv = dy_ref[h, :, 0:HEAD_DIM] if head_major else dy_ref[:, cs]
            inv = lax.rsqrt(jnp.mean(xv * xv, axis=-1, keepdims=True) + EPS)
            nrm = xv * inv
            dn = dyv * gv * mult
            dg_ref[:, cs] += _colsum8(dyv * nrm * mult)
            dx_ref[0, :, cs] = (inv * (dn - nrm * jnp.mean(dn * nrm, axis=-1, keepdims=True))).astype(BF16)
        if extra is not None:
            dx_ref[1] = refs[3][...]

    row = pl.BlockSpec((tm, D_MODEL), lambda i: (i, 0))
    dy_spec = pl.BlockSpec((HEADS, tm, dy.shape[-1]), lambda i: (0, i, 0)) if head_major else row
    ins = [x, g, dy] + ([extra] if extra is not None else [])
    specs = ([pl.BlockSpec((tm, D_MODEL), lambda i: (i, col0)), _full((1, D_MODEL)), dy_spec]
             + ([row] if extra is not None else []))
    return pl.pallas_call(
        body, name=name, grid=(s // tm,), in_specs=specs,
        out_specs=[pl.BlockSpec((groups, tm, D_MODEL), lambda i: (0, i, 0)), _full((SUBLANES, D_MODEL))],
        out_shape=[jax.ShapeDtypeStruct((groups, s, D_MODEL), BF16), jax.ShapeDtypeStruct((SUBLANES, D_MODEL), F32)],
    )(*ins)


def _log_sigmoid(z):
    return jnp.minimum(z, 0.0) - jnp.log(1.0 + jnp.exp(-jnp.abs(z)))


Q_CUM, Q_ONE, Q_LSE = 0, 3, 6
LOG2E = 1.4426950408889634


def _pieces(v):
    hi = v.astype(BF16).astype(F32)
    mid = (v - hi).astype(BF16).astype(F32)
    lo = ((v - hi) - mid).astype(BF16).astype(F32)
    return hi, mid, lo


def _side(lane, at, v):
    hi, mid, lo = _pieces(v)
    return jnp.where(lane == at, hi, jnp.where(lane == at + 1, mid, jnp.where(lane == at + 2, lo, 0.0)))


def _fcum_fwd(f, bias, name):
    s = f.shape[0]
    tm = _tile(s, ROW_TILE)

    def body(f_ref, b_ref, qa_ref, ka_ref, carry):
        @pl.when(pl.program_id(0) == 0)
        def _():
            carry[...] = jnp.zeros_like(carry)

        cum = _tri_dot(_tri(tm), _log_sigmoid(f_ref[...] + b_ref[...])) + carry[...]
        carry[...] = cum[tm - 1:tm]
        lane = lax.broadcasted_iota(jnp.int32, (tm, LANES), 1)
        ones_q = jnp.where((lane >= Q_ONE) & (lane < Q_LSE), 1.0, 0.0)
        ones_k = jnp.where((lane < Q_ONE) | ((lane >= Q_LSE) & (lane < Q_LSE + 3)), 1.0, 0.0)
        for h in range(HEADS):
            c2 = cum[:, h:h + 1] * LOG2E
            qa_ref[h] = (_side(lane, Q_CUM, c2) + ones_q).astype(BF16)
            ka_ref[h] = (_side(lane, Q_ONE, -c2) + ones_k).astype(BF16)

    side = pl.BlockSpec((HEADS, tm, LANES), lambda i: (0, i, 0))
    return pl.pallas_call(
        body, name=name, grid=(s // tm,),
        in_specs=[pl.BlockSpec((tm, LANES), lambda i: (i, 0)), _full((1, LANES))],
        out_specs=[side, side],
        out_shape=[jax.ShapeDtypeStruct((HEADS, s, LANES), BF16)] * 2,
        scratch_shapes=[pltpu.VMEM((1, LANES), F32)],
    )(f, bias)


def _fcum_bwd(f, bias, dka, dq, name):
    s = f.shape[0]
    tm = _tile(s, ROW_TILE)
    n_b = s // tm
    q_lane = HEAD_DIM + Q_CUM

    def body(f_ref, b_ref, dka_ref, dqa_ref, dz_ref, db_ref, carry):
        @pl.when(pl.program_id(0) == 0)
        def _():
            carry[...] = jnp.zeros_like(carry)
            db_ref[...] = jnp.zeros_like(db_ref)

        lane = lax.broadcasted_iota(jnp.int32, (tm, LANES), 1)
        dcum = jnp.zeros((tm, LANES), F32)
        for h in range(HEADS):
            dcum = dcum + jnp.where(lane == h, dqa_ref[h, :, q_lane:q_lane + 1] - dka_ref[h, :, Q_ONE:Q_ONE + 1], 0.0)
        dlf = _tri_dot(_tri(tm, upper=True), dcum) + carry[...]
        carry[...] = dlf[0:1]
        dz = dlf * _sig(-(f_ref[...] + b_ref[...]))
        dz_ref[0] = dz.astype(BF16)
        db_ref[...] += _colsum8(dz)

    return pl.pallas_call(
        body, name=name, grid=(n_b,),
        in_specs=[pl.BlockSpec((tm, LANES), lambda i: (n_b - 1 - i, 0)), _full((1, LANES)),
                  pl.BlockSpec((HEADS, tm, LANES), lambda i: (0, n_b - 1 - i, 0)),
                  pl.BlockSpec((HEADS, tm, 2 * HEAD_DIM), lambda i: (0, n_b - 1 - i, 0))],
        out_specs=[pl.BlockSpec((1, tm, LANES), lambda i: (0, n_b - 1 - i, 0)), _full((SUBLANES, LANES))],
        out_shape=[jax.ShapeDtypeStruct((1, s, LANES), BF16), jax.ShapeDtypeStruct((SUBLANES, LANES), F32)],
        scratch_shapes=[pltpu.VMEM((1, LANES), F32)],
    )(f, bias, dka, dq)


def _causal_pairs(n_t, key_major):
    if key_major:
        pairs = [(qi, ki) for ki in range(n_t) for qi in range(ki, n_t)]
    else:
        pairs = [(qi, ki) for qi in range(n_t) for ki in range(qi + 1)]
    return (jnp.array([p[0] for p in pairs], jnp.int32), jnp.array([p[1] for p in pairs], jnp.int32))


def _with_side(main_ref, side_ref):
    return jnp.concatenate([main_ref[...], side_ref[...]], axis=1)


def _lane_const(t, lo, hi, value):
    lane = lax.broadcasted_iota(jnp.int32, (t, LANES), 1)
    return jnp.where((lane >= lo) & (lane < hi), value, 0.0).astype(BF16)


def _att_specs(t, nh):
    qmain = pl.BlockSpec((t, nh * HEAD_DIM), lambda h, p, qt, kt: (qt[p], h))
    kmain = pl.BlockSpec((t, nh * HEAD_DIM), lambda h, p, qt, kt: (kt[p], h))
    qside = pl.BlockSpec((nh, t, LANES), lambda h, p, qt, kt: (h, qt[p], 0))
    kside = pl.BlockSpec((nh, t, LANES), lambda h, p, qt, kt: (h, kt[p], 0))
    return qmain, kmain, qside, kside


def _fox_fwd(q, qa, k, ka, v, qo, name):
    s = q.shape[0]
    t = _tile(s, ATT_TILE)
    sub = t // ATT_SPLIT
    nh = ATT_FWD_HEADS
    qt, kt = _causal_pairs(s // t, key_major=False)

    def body(qt_ref, kt_ref, q_ref, qa_ref, k_ref, ka_ref, v_ref, og_ref, o_ref, y_ref, qab_ref, m_s, l_s, acc_s):
        pid = pl.program_id(1)
        qi, ki = qt_ref[pid], kt_ref[pid]

        @pl.when(ki == 0)
        def _():
            m_s[...] = jnp.full_like(m_s, NEG_INF)
            l_s[...] = jnp.zeros_like(l_s)
            acc_s[...] = jnp.zeros_like(acc_s)

        def step(diagonal):
            for hh in range(nh):
                hc = slice(hh * HEAD_DIM, (hh + 1) * HEAD_DIM)
                kc = jnp.concatenate([k_ref[:, hc], ka_ref[hh]], axis=1)
                vc = jnp.concatenate([v_ref[:, hc], _lane_const(t, 0, 1, 1.0)], axis=1)
                for r in range(ATT_SPLIT):
                    rows = slice(r * sub, (r + 1) * sub)
                    n_k = (r + 1) * sub if diagonal else t
                    sc = _dot_nt(jnp.concatenate([q_ref[rows, hc], qa_ref[hh, rows]], axis=1), kc[:n_k])
                    if diagonal:
                        sc = jnp.where(lax.broadcasted_iota(jnp.int32, (sub, n_k), 1)
                                       <= lax.broadcasted_iota(jnp.int32, (sub, n_k), 0) + r * sub, sc, NEG_INF)
                    m_old = m_s[hh, rows]
                    m_new = jnp.maximum(m_old, jnp.max(sc, axis=-1, keepdims=True))
                    alpha = jnp.exp2(m_old - m_new)
                    pv = _dot(jnp.exp2(sc - m_new[:, 0:1]).astype(BF16), vc[:n_k])
                    acc_s[hh, rows] = alpha * acc_s[hh, rows] + pv[:, :HEAD_DIM]
                    l_s[hh, rows] = alpha * l_s[hh, rows] + pv[:, HEAD_DIM:]
                    m_s[hh, rows] = m_new

        @pl.when(ki < qi)
        def _():
            step(False)

        @pl.when(ki == qi)
        def _():
            step(True)
            lane = lax.broadcasted_iota(jnp.int32, (t, LANES), 1)
            for hh in range(nh):
                hc = slice(hh * HEAD_DIM, (hh + 1) * HEAD_DIM)
                l = l_s[hh, :, 0:1]
                o = acc_s[hh] / l
                o_ref[:, hc] = o
                y_ref[:, hc] = (o * _sig(og_ref[:, hc])).astype(BF16)
                qab_ref[hh] = qa_ref[hh] + _side(lane, Q_LSE, -(m_s[hh, :, 0:1] + jnp.log2(l))).astype(BF16)

    qmain, kmain, qside, kside = _att_specs(t, nh)
    return pl.pallas_call(
        body, name=name,
        grid_spec=pltpu.PrefetchScalarGridSpec(
            num_scalar_prefetch=2, grid=(HEADS // nh, qt.shape[0]),
            in_specs=[qmain, qside, kmain, kside, kmain,
                      pl.BlockSpec((t, nh * HEAD_DIM), lambda h, p, qt, kt: (qt[p], HEADS // nh + h))],
            out_specs=[qmain, qmain, qside],
            scratch_shapes=[pltpu.VMEM((nh, t, LANES), F32), pltpu.VMEM((nh, t, LANES), F32),
                            pltpu.VMEM((nh, t, HEAD_DIM), F32)]),
        out_shape=[jax.ShapeDtypeStruct((s, D_MODEL), F32), jax.ShapeDtypeStruct((s, D_MODEL), BF16),
                   jax.ShapeDtypeStruct((HEADS, s, LANES), BF16)],
    )(qt, kt, q, qa, k, ka, v, qo)


def _fox_gate_bwd(o, qo, dy, name):
    s = o.shape[0]
    tm = _tile(s, ROW_TILE)

    def body(o_ref, og_ref, dy_ref, do_ref, dg_ref, dl_ref):
        ov, dyv = o_ref[...], dy_ref[...]
        sg = _sig(og_ref[...])
        do = (dyv * sg).astype(BF16)
        do_ref[...] = do
        dg_ref[...] = (dyv * ov * sg * (1.0 - sg)).astype(BF16)
        prod = do.astype(F32) * ov
        lane = lax.broadcasted_iota(jnp.int32, (tm, LANES), 1)
        for h in range(HEADS):
            delta = jnp.sum(prod[:, h * HEAD_DIM:(h + 1) * HEAD_DIM], axis=-1, keepdims=True)
            dl_ref[h] = _side(lane, 0, delta).astype(BF16)

    row = pl.BlockSpec((tm, D_MODEL), lambda i: (i, 0))
    return pl.pallas_call(
        body, name=name, grid=(s // tm,),
        in_specs=[row, pl.BlockSpec((tm, D_MODEL), lambda i: (i, 1)), row],
        out_specs=[row, row, pl.BlockSpec((HEADS, tm, LANES), lambda i: (0, i, 0))],
        out_shape=[jax.ShapeDtypeStruct((s, D_MODEL), BF16), jax.ShapeDtypeStruct((s, D_MODEL), BF16),
                   jax.ShapeDtypeStruct((HEADS, s, LANES), BF16)],
    )(o, qo, dy)


def _fox_bwd(q, qab, k, ka, v, do, doa, name):
    s = q.shape[0]
    t = _tile(s, ATT_TILE)
    n_t = s // t
    sub = t // ATT_SPLIT
    nh = ATT_BWD_HEADS
    qt, kt = _causal_pairs(n_t, key_major=True)

    def body(qt_ref, kt_ref, q_ref, qab_ref, k_ref, ka_ref, v_ref, do_ref, doa_ref, dk_ref, dv_ref, dka_ref, dq_hbm,
             dk_s, dv_s, dq_ref):
        group, pid = pl.program_id(0), pl.program_id(1)
        qi, ki = qt_ref[pid], kt_ref[pid]

        @pl.when(pid == 0)
        def _():
            dq_ref[...] = jnp.zeros_like(dq_ref)

        @pl.when(qi == ki)
        def _():
            dk_s[...] = jnp.zeros_like(dk_s)
            dv_s[...] = jnp.zeros_like(dv_s)

        def step(diagonal):
            for hh in range(nh):
                hc = slice(hh * HEAD_DIM, (hh + 1) * HEAD_DIM)
                kc = jnp.concatenate([k_ref[:, hc], ka_ref[hh]], axis=1)
                vc = jnp.concatenate([v_ref[:, hc], _lane_const(t, 0, 3, -1.0)], axis=1)
                for r in range(ATT_SPLIT):
                    cols = slice(r * sub, (r + 1) * sub)
                    n_k = (r + 1) * sub if diagonal else t
                    qc = jnp.concatenate([q_ref[cols, hc], qab_ref[hh, cols]], axis=1)
                    sc = _dot_nt(kc[:n_k], qc)
                    if diagonal:
                        sc = jnp.where(lax.broadcasted_iota(jnp.int32, (n_k, sub), 0)
                                       <= lax.broadcasted_iota(jnp.int32, (n_k, sub), 1) + r * sub, sc, NEG_INF)
                    p = jnp.exp2(sc)
                    dov = do_ref[cols, hc]
                    dp = _dot_nt(vc[:n_k], jnp.concatenate([dov, doa_ref[hh, cols]], axis=1))
                    ds = (p * dp).astype(BF16)
                    dv_s[hh, 0:n_k] += _dot(p.astype(BF16), dov)
                    dk_s[hh, 0:n_k] += _dot(ds, qc)
                    q_rows = pl.ds(pl.multiple_of(qi * t + r * sub, sub), sub)
                    dq_ref[hh, q_rows, :] += _dot_tn(ds, kc[:n_k])

        @pl.when(qi > ki)
        def _():
            step(False)

        @pl.when(qi == ki)
        def _():
            step(True)

        @pl.when(qi == n_t - 1)
        def _():
            for hh in range(nh):
                hc = slice(hh * HEAD_DIM, (hh + 1) * HEAD_DIM)
                dk_ref[:, hc] = dk_s[hh, :, :HEAD_DIM] * (1.0 / LOG2E)
                dka_ref[hh] = dk_s[hh, :, HEAD_DIM:]
                dv_ref[:, hc] = dv_s[hh].astype(BF16)

        @pl.when(pid == qt.shape[0] - 1)
        def _():
            pltpu.sync_copy(dq_ref, dq_hbm.at[pl.ds(group * nh, nh)])

    qmain, kmain, qside, kside = _att_specs(t, nh)
    return pl.pallas_call(
        body, name=name,
        grid_spec=pltpu.PrefetchScalarGridSpec(
            num_scalar_prefetch=2, grid=(HEADS // nh, qt.shape[0]),
            in_specs=[qmain, qside, kmain, kside, kmain, qmain, qside],
            out_specs=[kmain, pl.BlockSpec((None, t, nh * HEAD_DIM), lambda h, p, qt, kt: (0, kt[p], h)), kside,
                       pl.BlockSpec(memory_space=pltpu.HBM)],
            scratch_shapes=[pltpu.VMEM((nh, t, 2 * HEAD_DIM), F32), pltpu.VMEM((nh, t, HEAD_DIM), F32),
                            pltpu.VMEM((nh, s, 2 * HEAD_DIM), F32)]),
        out_shape=[jax.ShapeDtypeStruct((s, D_MODEL), F32), jax.ShapeDtypeStruct((1, s, D_MODEL), BF16),
                   jax.ShapeDtypeStruct((HEADS, s, LANES), F32), jax.ShapeDtypeStruct((HEADS, s, 2 * HEAD_DIM), F32)],
    )(qt, kt, q, qab, k, ka, v, do, doa)


def _mm_residual_premix(a, w, x, gate, mods, name):
    s, k = a.shape
    dm = x.shape[1]
    tm = _tile(s, ROW_TILE)

    def body(*refs):
        a_ref, w_ref, x_ref, g_ref = refs[:4]
        mod_refs = refs[4:4 + 2 * len(mods)]
        y_ref, xn_ref = refs[4 + 2 * len(mods):6 + 2 * len(mods)]
        h_refs = refs[6 + 2 * len(mods):]
        y = _dot(a_ref[...], w_ref[0])
        y_ref[...] = y
        xv = x_ref[...] + g_ref[...] * y
        xn_ref[...] = xv
        nrm = xv * lax.rsqrt(jnp.mean(xv * xv, axis=-1, keepdims=True) + EPS)
        for t, h_ref in enumerate(h_refs):
            h_ref[...] = (nrm * (1.0 + mod_refs[2 * t + 1][...]) + mod_refs[2 * t][...]).astype(BF16)

    row = pl.BlockSpec((tm, dm), lambda i: (i, 0))
    vec = _full((1, dm))
    outs = pl.pallas_call(
        body, name=name, grid=(s // tm,),
        in_specs=[pl.BlockSpec((tm, k), lambda i: (i, 0)), _full(w.shape), row, vec] + [vec] * (2 * len(mods)),
        out_specs=[row] * (2 + len(mods)),
        out_shape=[jax.ShapeDtypeStruct((s, dm), F32)] * 2 + [jax.ShapeDtypeStruct((s, dm), BF16)] * len(mods),
    )(a, w, x, gate, *[v for m in mods for v in m])
    return outs[0], outs[1], list(outs[2:])


def _mm_loss_head(a, w, x, gate, target, name):
    s, k = a.shape
    dm = x.shape[1]
    tm = _tile(s, ROW_TILE)

    def body(a_ref, w_ref, x_ref, g_ref, t_ref, sq_ref, do_ref, dy_ref, dg_ref):
        @pl.when(pl.program_id(0) == 0)
        def _():
            sq_ref[...] = jnp.zeros_like(sq_ref)
            dg_ref[...] = jnp.zeros_like(dg_ref)

        y, gv = _dot(a_ref[...], w_ref[0]), g_ref[...]
        err = x_ref[...] + gv * y - t_ref[...]
        sq_ref[...] += _colsum8(err * err)
        dout = err * (1.0 / dm)
        do_ref[...] = dout
        dy_ref[0] = (dout * gv).astype(BF16)
        dg_ref[...] += _colsum8(dout * y)

    row = pl.BlockSpec((tm, dm), lambda i: (i, 0))
    acc = _full((SUBLANES, dm))
    return pl.pallas_call(
        body, name=name, grid=(s // tm,),
        in_specs=[pl.BlockSpec((tm, k), lambda i: (i, 0)), _full(w.shape), row, _full((1, dm)), row],
        out_specs=[acc, row, pl.BlockSpec((1, tm, dm), lambda i: (0, i, 0)), acc],
        out_shape=[jax.ShapeDtypeStruct((SUBLANES, dm), F32), jax.ShapeDtypeStruct((s, dm), F32),
                   jax.ShapeDtypeStruct((1, s, dm), BF16), jax.ShapeDtypeStruct((SUBLANES, dm), F32)],
    )(a, w, x, gate, target)


def _ffn_inner(h, w_up, conv_w, conv_b, tag):
    s, dm = h.shape
    half = w_up.shape[2]
    f = 2 * half
    tm = _tile(s, FFN_ROWS)

    def body(h_ref, w_ref, cw_ref, cb_ref, u_ref, c_ref, a_ref, carry):
        @pl.when(pl.program_id(0) == 0)
        def _():
            carry[...] = jnp.zeros_like(carry)

        hv = h_ref[...]
        for j in range(2):
            cols = slice(j * half, (j + 1) * half)
            conv = []
            for g in range(2):
                ub = _dot(hv, w_ref[2 * g + j]).astype(BF16)
                u_ref[g, :, cols] = ub
                uf = ub.astype(F32)
                e = jnp.concatenate([carry[g, j], uf], axis=0)
                carry[g, j] = uf[tm - SUBLANES:tm]
                conv.append(_conv_taps(e, cw_ref[g][:, cols], cb_ref[g][:, cols])[SUBLANES:])
                c_ref[g, :, cols] = conv[g].astype(BF16)
            a_ref[:, cols] = (conv[0] * _sig(conv[0]) * conv[1]).astype(BF16)

    pair = pl.BlockSpec((2, tm, f), lambda i: (0, i, 0))
    return pl.pallas_call(
        body, name=tag + "_up_convglu", grid=(s // tm,),
        in_specs=[pl.BlockSpec((tm, dm), lambda i: (i, 0)), _full(w_up.shape), _full(conv_w.shape), _full(conv_b.shape)],
        out_specs=[pair, pair, pl.BlockSpec((tm, f), lambda i: (i, 0))],
        out_shape=[jax.ShapeDtypeStruct((2, s, f), BF16)] * 2 + [jax.ShapeDtypeStruct((s, f), BF16)],
        scratch_shapes=[pltpu.VMEM((2, 2, SUBLANES, half), F32)],
    )(h, w_up, conv_w, conv_b)


def _weight_grad_first(a, d, p_n, name):
    return lax.optimization_barrier((_mm_tn(a, d, p_n, name), d))


def _ffn_backward(dx_out, dffn, x_mid, scale, saved, w_up, conv_w, conv_b, w_down, mixer, tag):
    h, u, c, a = saved
    dw_down, dffn = _weight_grad_first(a, dffn, 1, tag + "_down_dw")
    du, dconv = _convglu_bwd(u, c, dffn, w_down, conv_w, tag + "_convglu_bwd")
    dw_up, du = _weight_grad_first(h, du, N_CHIPS, tag + "_up_dw")
    dx_mid, [(dshift, dscale)], dy, dgate_mixer = _premix_bwd(x_mid, [(scale, [(du, w_up)])], dx_out,
                                                              tag + "_premix_bwd", branch=mixer)
    return dx_mid, dy, dgate_mixer, dw_up, dw_down, dict(shift=dshift, scale=dscale, conv=dconv)


def _local_step(x, target, mods, lb, vecs, weights_at):
    m0, m1, mk = mods["l0"], mods["l1"], mods["kv"]
    h0 = _premix(x, m0[0], m0[1], "l0_premix")
    wts, h0 = weights_at("mixer0", h0)
    proj = _mm_nn(h0, wts["a_w_in"], 1, F32, "l0_in")[0]
    o_a, yp, states = _hgrn_fwd(proj, lb, vecs["a_norm_g"], "l0_hgrn")
    more, yp = weights_at("ffn0", yp)
    wts.update(more)
    y0, x1, [hf0] = _mm_residual_premix(yp, wts["a_w_out"], x, m0[2], [(m0[3], m0[4])], "l0_out")
    u0, c0, a0 = _ffn_inner(hf0, wts["up0"], vecs["conv_w0"], vecs["conv_b0"], "l0_ffn")
    saved0 = (hf0, u0, c0, a0)
    ffn0, x2, [hk, h1] = _mm_residual_premix(a0, wts["down0"], x1, m0[5], [(mk[0], mk[1]), (m1[0], m1[1])],
                                             "l0_ffn_down")
    more, hk = weights_at("layer1", hk)
    wts.update(more)
    k_raw = _mm_nn(hk, wts["kv_k"], 1, F32, "kv_k")[0]
    v_sh = _mm_nn(hk, wts["kv_v"], 1, BF16, "kv_v")[0]
    f_raw = _mm_nn(hk, wts["kv_f"], 1, F32, "kv_f")[0]
    k_sh = _headnorm(k_raw, vecs["k_norm_g"], 1.0, "kv_knorm")
    qa, ka = _fcum_fwd(f_raw, vecs["kv_b_f"], "kv_fcum")
    qo = _mm_nn(h1, wts["b_w_q"], 1, F32, "l1_q")[0]
    q_scale = HEAD_DIM ** -0.5
    q = _headnorm(qo, vecs["q_norm_g"], q_scale * LOG2E, "l1_qnorm")
    o_b, og, qab = _fox_fwd(q, qa, k_sh, ka, v_sh, qo, "l1_fox")
    y1, x3, [hf1] = _mm_residual_premix(og, wts["b_w_out"], x2, m1[2], [(m1[3], m1[4])], "l1_out")
    u1, c1, a1 = _ffn_inner(hf1, wts["up1"], vecs["conv_w1"], vecs["conv_b1"], "l1_ffn")
    saved1 = (hf1, u1, c1, a1)
    sq, dx4, dffn1, dg2_1 = _mm_loss_head(a1, wts["down1"], x3, m1[5], target, "l1_ffn_down")

    big, small = {}, {}
    dx3, dy1, dg1_1, big["up1"], big["down1"], s_ffn1 = _ffn_backward(
        dx4, dffn1, x3, m1[4], saved1, wts["up1"], vecs["conv_w1"], vecs["conv_b1"], wts["down1"], (y1, m1[2]), "l1_ffn")
    big["b_w_out"], dy1 = _weight_grad_first(og, dy1, 1, "l1_out_dw")
    d_og = _mm_nt(dy1, wts["b_w_out"], F32, "l1_out_dx")
    do_b, dgate_b, doa = _fox_gate_bwd(o_b, qo, d_og, "l1_fox_gate_bwd")
    dk, dv, dka, dq = _fox_bwd(q, qab, k_sh, ka, v_sh, do_b, doa, "l1_fox_bwd")
    dqo, dqg = _headnorm_bwd(qo, vecs["q_norm_g"], q_scale, dq, "l1_qnorm_bwd", extra=dgate_b)
    big["b_w_q"], dqo = _weight_grad_first(h1, dqo, N_CHIPS, "l1_q_dw")
    dk_raw, dkg = _headnorm_bwd(k_raw, vecs["k_norm_g"], 1.0, dk, "kv_knorm_bwd")
    dz, dbf = _fcum_bwd(f_raw, vecs["kv_b_f"], dka, dq, "kv_fcum_bwd")
    big["kv_k"], dk_raw = _weight_grad_first(hk, dk_raw, 1, "kv_k_dw")
    big["kv_v"], dv = _weight_grad_first(hk, dv, 1, "kv_v_dw")
    big["kv_f"], dz = _weight_grad_first(hk, dz, 1, "kv_f_dw")
    kv_pairs = [(dk_raw, wts["kv_k"]), (dv, wts["kv_v"]), (dz, wts["kv_f"])]
    dx2, [(dsh1_1, dsc1_1), (dshk, dsck)], dffn0, dg2_0 = _premix_bwd(
        x2, [(m1[1], [(dqo, wts["b_w_q"])]), (mk[1], kv_pairs)], dx3, "l1_kv_premix_bwd", branch=(ffn0, m0[5]))
    dx1, dy0, dg1_0, big["up0"], big["down0"], s_ffn0 = _ffn_backward(
        dx2, dffn0, x1, m0[4], saved0, wts["up0"], vecs["conv_w0"], vecs["conv_b0"], wts["down0"], (y0, m0[2]), "l0_ffn")
    big["a_w_out"], dy0 = _weight_grad_first(yp, dy0, 1, "l0_out_dw")
    dyp = _mm_nt(dy0, wts["a_w_out"], F32, "l0_out_dx")
    dproj, dlb, dng = _hgrn_bwd(proj, lb, vecs["a_norm_g"], o_a, states, dyp, "l0_hgrn_bwd")
    grad_x, [(dsh1_0, dsc1_0)] = _premix_bwd(x, [(m0[1], [(dproj, wts["a_w_in"])])], dx1, "l0_premix_bwd")
    dproj, _ = lax.optimization_barrier((dproj, (dsh1_0, dsc1_0)))
    big["a_w_in"] = _mm_tn(h0, dproj, N_CHIPS, "l0_in_dw")

    small["mod_l0"] = [dsh1_0, dsc1_0, dg1_0, s_ffn0["shift"], s_ffn0["scale"], dg2_0]
    small["mod_l1"] = [dsh1_1, dsc1_1, dg1_1, s_ffn1["shift"], s_ffn1["scale"], dg2_1]
    small["mod_kv"] = [dshk, dsck]
    small["conv0"], small["conv1"] = s_ffn0["conv"], s_ffn1["conv"]
    small["a_norm_g"], small["k_norm_g"], small["q_norm_g"] = dng, dkg, dqg
    small["kv_b_f"], small["lb"] = dbf, dlb
    marks = {"attention_bwd": dk, "ffn0_bwd": dx1, "mixer0_bwd": grad_x}
    return sq, grad_x, big, small, marks


HBM = pl.BlockSpec(memory_space=pltpu.HBM)
COMM_CHUNK_ELEMS = 256 * 1024


def _place():
    x, y, c = lax.axis_index("x"), lax.axis_index("y"), lax.axis_index("c")
    chips = [(1 - x, y), (x, 1 - y), (1 - x, 1 - y)]
    return x, y, c, (x, y, 1 - c), chips


def _chunk_rows(rows, cols):
    best = BF16_ROWS
    for r in range(BF16_ROWS, rows + 1, BF16_ROWS):
        if rows % r == 0 and r * cols <= COMM_CHUNK_ELEMS:
            best = r
    assert rows % best == 0, (rows, cols)
    return best


def _allgather8(block, name):
    m_per, n = block.shape

    def body(x_ref, out_ref, send_sems, recv_sems, local_sem):
        x, y, c, sibling, chips = _place()
        me = (x, y, c)

        def rows(px, py, pc):
            return out_ref.at[pl.ds((4 * px + 2 * py + pc) * m_per, m_per), :]

        def copy(k, blk, to, src=None):
            return pltpu.make_async_remote_copy(
                src_ref=rows(*blk) if src is None else src, dst_ref=rows(*blk),
                send_sem=send_sems.at[k], recv_sem=recv_sems.at[k], device_id=to, device_id_type=MESH)

        mine = pltpu.make_async_copy(x_ref, rows(*me), local_sem)
        mine.start()
        first = [copy(0, me, sibling, src=x_ref)]
        first += [copy(1 + j, me, (*chip, c), src=x_ref) for j, chip in enumerate(chips)]
        for cp in first:
            cp.start()
        passed = [copy(4 + j, (*chip, c), sibling) for j, chip in enumerate(chips)]
        for j, chip in enumerate(chips):
            copy(1 + j, (*chip, c), me).wait_recv()
            passed[j].start()
        copy(0, sibling, me).wait_recv()
        for j, chip in enumerate(chips):
            copy(4 + j, (*chip, 1 - c), me).wait_recv()
        for cp in first + passed:
            cp.wait_send()
        mine.wait()

    return pl.pallas_call(
        body, name=name, out_shape=jax.ShapeDtypeStruct((N_DEV * m_per, n), block.dtype),
        in_specs=[pl.BlockSpec(memory_space=pltpu.VMEM)], out_specs=pl.BlockSpec(memory_space=pltpu.VMEM),
        scratch_shapes=[pltpu.SemaphoreType.DMA((7,)), pltpu.SemaphoreType.DMA((7,)), pltpu.SemaphoreType.DMA],
    )(block)


def _cast_own_block(shards, layer, chip, name):
    _, r, cols = shards.shape
    rows = _chunk_rows(r, cols)

    def body(chip_ref, w_ref, o_ref):
        o_ref[...] = w_ref[...].astype(BF16)

    return pl.pallas_call(
        body, name=name,
        grid_spec=pltpu.PrefetchScalarGridSpec(
            num_scalar_prefetch=1, grid=(r // rows,),
            in_specs=[pl.BlockSpec((None, rows, cols), lambda i, chip_ref: (layer, i, 0))],
            out_specs=pl.BlockSpec((None, rows, cols), lambda i, chip_ref: (chip_ref[0], i, 0))),
        out_shape=jax.ShapeDtypeStruct((N_CHIPS, r, cols), BF16),
    )(chip, shards)


def _sequencer_gather(bufs, name, collective_id):
    n_t = len(bufs)
    dims = [b.shape[1:] for b in bufs]
    refs = [jax.new_ref(b, memory_space=pltpu.MemorySpace.HBM) for b in bufs]

    @pl.kernel(mesh=plsc.ScalarSubcoreMesh(axis_name="sequencer", num_cores=1), name=name,
               scratch_types=[pltpu.SemaphoreType.DMA((n_t,))] * 4,
               compiler_params=pltpu.CompilerParams(collective_id=collective_id))
    def launch(send_ici, recv_ici, send_d2d, recv_d2d):
        x, y, c, sibling, chips = _place()
        p_me = 2 * x + y
        peers = [sibling] + [(cx, cy, c) for cx, cy in chips]
        barrier = pltpu.get_barrier_semaphore()
        for peer in peers:
            pl.semaphore_signal(barrier, inc=1, device_id=peer, device_id_type=MESH)
        pl.semaphore_wait(barrier, len(peers))

        def waiter(t, sem_s, sem_r):
            win = refs[t].at[pl.ds(0, 3), pl.ds(0, dims[t][0] // 2), :]
            return pltpu.make_async_remote_copy(src_ref=win, dst_ref=win, send_sem=sem_s.at[t], recv_sem=sem_r.at[t],
                                                device_id=sibling, device_id_type=MESH)

        def half_copy(t, chip_idx, to, sem_s, sem_r):
            r2 = dims[t][0] // 2
            win = refs[t].at[chip_idx, pl.ds(c * r2, r2), :]
            return pltpu.make_async_remote_copy(src_ref=win, dst_ref=win, send_sem=sem_s.at[t], recv_sem=sem_r.at[t],
                                                device_id=to, device_id_type=MESH)

        for t in range(n_t):
            for cx, cy in chips:
                half_copy(t, p_me, (cx, cy, c), send_ici, recv_ici).start()
        for t in range(n_t):
            waiter(t, send_ici, recv_ici).wait_recv()
            for cx, cy in chips:
                half_copy(t, 2 * cx + cy, sibling, send_d2d, recv_d2d).start()
        for t in range(n_t):
            waiter(t, send_d2d, recv_d2d).wait_recv()
            waiter(t, send_ici, recv_ici).wait_send()
            waiter(t, send_d2d, recv_d2d).wait_send()

    launch()
    return [r[...] for r in refs]


def _sequencer_allgather8(block, dev, name, collective_id):
    m_per, n = block.shape
    src = jax.new_ref(block, memory_space=pltpu.MemorySpace.HBM)
    out = jax.empty_ref(jax.ShapeDtypeStruct((N_DEV * m_per, n), block.dtype), memory_space=pltpu.MemorySpace.HBM)

    @pl.kernel(mesh=plsc.ScalarSubcoreMesh(axis_name="sequencer", num_cores=1), name=name,
               scratch_types=[pltpu.SemaphoreType.DMA((7,))] * 2,
               compiler_params=pltpu.CompilerParams(collective_id=collective_id))
    def launch(send_sems, recv_sems):
        x, y, c, sibling, chips = _place()
        me = (x, y, c)
        _handshake([sibling] + [(cx, cy, c) for cx, cy in chips])

        def rows(px, py, pc):
            return out.at[pl.ds((4 * px + 2 * py + pc) * m_per, m_per), :]

        def copy(k, blk, to, from_src=False):
            return pltpu.make_async_remote_copy(
                src_ref=src if from_src else rows(*blk), dst_ref=rows(*blk),
                send_sem=send_sems.at[k], recv_sem=recv_sems.at[k], device_id=to, device_id_type=MESH)

        first = [copy(0, me, sibling, True)] + [copy(1 + j, me, (*chip, c), True) for j, chip in enumerate(chips)]
        for cp in first:
            cp.start()
        passed = [copy(4 + j, (*chip, c), sibling) for j, chip in enumerate(chips)]
        for j, chip in enumerate(chips):
            copy(1 + j, (*chip, c), me).wait_recv()
            passed[j].start()
        copy(0, sibling, me).wait_recv()
        for j, chip in enumerate(chips):
            copy(4 + j, (*chip, 1 - c), me).wait_recv()
        for cp in first + passed:
            cp.wait_send()

    launch()
    return lax.dynamic_update_slice(out[...], block, (dev * m_per, 0))


def _others():
    x, y, c = lax.axis_index("x"), lax.axis_index("y"), lax.axis_index("c")
    flip = lambda v, f: 1 - v if f else v
    return [(flip(x, fx), flip(y, fy), flip(c, fc))
            for fx in (0, 1) for fy in (0, 1) for fc in (0, 1) if (fx, fy, fc) != (0, 0, 0)]


def _handshake(peers):
    barrier = pltpu.get_barrier_semaphore()
    for peer in peers:
        pl.semaphore_signal(barrier, inc=1, device_id=peer, device_id_type=MESH)
    pl.semaphore_wait(barrier, len(peers))


def _sequencer_scatter(parts, name, collective_id):
    n_t = len(parts)
    dims = [p.shape[1:] for p in parts]
    srcs = [jax.new_ref(p, memory_space=pltpu.MemorySpace.HBM) for p in parts]
    inboxes = [jax.empty_ref(jax.ShapeDtypeStruct((N_DEV, r // 2, cols), BF16), memory_space=pltpu.MemorySpace.HBM)
               for r, cols in dims]

    @pl.kernel(mesh=plsc.ScalarSubcoreMesh(axis_name="sequencer", num_cores=1), name=name,
               scratch_types=[pltpu.SemaphoreType.DMA((n_t,))] * 2,
               compiler_params=pltpu.CompilerParams(collective_id=collective_id))
    def launch(send_sem, recv_sem):
        x, y, c = lax.axis_index("x"), lax.axis_index("y"), lax.axis_index("c")
        me = 4 * x + 2 * y + c
        peers = _others()
        _handshake(peers)
        for t in range(n_t):
            h = dims[t][0] // 2
            for qx, qy, qc in peers:
                pltpu.make_async_remote_copy(
                    src_ref=srcs[t].at[2 * qx + qy, pl.ds(qc * h, h), :], dst_ref=inboxes[t].at[me],
                    send_sem=send_sem.at[t], recv_sem=recv_sem.at[t], device_id=(qx, qy, qc), device_id_type=MESH).start()
        for t in range(n_t):
            win = inboxes[t].at[pl.ds(0, N_DEV - 1)]
            both = pltpu.make_async_remote_copy(src_ref=win, dst_ref=win, send_sem=send_sem.at[t],
                                                recv_sem=recv_sem.at[t], device_id=peers[0], device_id_type=MESH)
            both.wait_recv()
            both.wait_send()

    launch()
    return [b[...] for b in inboxes]


def _sum_pieces(part, inbox, place, name):
    _, r, cols = part.shape
    h = r // 2
    rows = _chunk_rows(h, cols)
    steps = h // rows

    def body(place_ref, own_ref, in_ref, o_ref):
        dev = place_ref[2]
        own = own_ref[...].astype(F32)
        acc = jnp.zeros((rows, cols), F32)
        for d in range(N_DEV):
            acc = acc + jnp.where(dev == d, own, in_ref[d].astype(F32))
        o_ref[...] = acc

    return pl.pallas_call(
        body, name=name,
        grid_spec=pltpu.PrefetchScalarGridSpec(
            num_scalar_prefetch=1, grid=(steps,),
            in_specs=[pl.BlockSpec((None, rows, cols), lambda i, pr: (pr[0], pr[1] * steps + i, 0)),
                      pl.BlockSpec((N_DEV, rows, cols), lambda i, pr: (0, i, 0))],
            out_specs=pl.BlockSpec((rows, cols), lambda i, pr: (pr[1] * steps + i, 0))),
        out_shape=jax.ShapeDtypeStruct((r, cols), F32),
    )(place, part, inbox)


def _sequencer_swap_halves(halves, name, collective_id):
    n_t = len(halves)
    refs = [jax.new_ref(a, memory_space=pltpu.MemorySpace.HBM) for a in halves]

    @pl.kernel(mesh=plsc.ScalarSubcoreMesh(axis_name="sequencer", num_cores=1), name=name,
               scratch_types=[pltpu.SemaphoreType.DMA((n_t,))] * 2,
               compiler_params=pltpu.CompilerParams(collective_id=collective_id))
    def launch(send_sem, recv_sem):
        x, y, c = lax.axis_index("x"), lax.axis_index("y"), lax.axis_index("c")
        sibling = (x, y, 1 - c)
        _handshake([sibling])
        copies = []
        for t in range(n_t):
            h = halves[t].shape[0] // 2
            win = refs[t].at[pl.ds(c * h, h), :]
            copies.append(pltpu.make_async_remote_copy(src_ref=win, dst_ref=win, send_sem=send_sem.at[t],
                                                       recv_sem=recv_sem.at[t], device_id=sibling, device_id_type=MESH))
            copies[-1].start()
        for cp in copies:
            cp.wait()

    launch()
    return [r[...] for r in refs]


def _cond_rows(c16, w, act, name):
    n_l, dm, wid = w.shape

    def body(c_ref, w_ref, o_ref, a_ref):
        cv = c_ref[...]
        if act:
            cv = cv * _sig(cv)
        a_ref[...] = cv
        o_ref[...] = _dot_f32(cv, w_ref[...])

    return pl.pallas_call(
        body, name=name, grid=(n_l,),
        in_specs=[_full((16, dm)), pl.BlockSpec((None, dm, wid), lambda l: (l, 0, 0))],
        out_specs=[pl.BlockSpec((None, 16, wid), lambda l: (l, 0, 0)), _full((16, dm))],
        out_shape=[jax.ShapeDtypeStruct((n_l, 16, wid), F32), jax.ShapeDtypeStruct((16, dm), F32)],
    )(c16, w)


def _outer_grad(ct, dm, name):
    n_l, kk, wid = dm.shape
    d_rows = ct.shape[0]

    def body(c_ref, d_ref, o_ref):
        o_ref[...] = _dot_f32(c_ref[...], d_ref[...])

    return pl.pallas_call(
        body, name=name, grid=(n_l,),
        in_specs=[_full((d_rows, kk)), pl.BlockSpec((None, kk, wid), lambda l: (l, 0, 0))],
        out_specs=pl.BlockSpec((None, d_rows, wid), lambda l: (l, 0, 0)),
        out_shape=jax.ShapeDtypeStruct((n_l, d_rows, wid), F32),
    )(ct, dm)


def _sum_devices(g, name):
    rows, n = g.shape

    def body(g_ref, o_ref):
        acc = g_ref[0:SUBLANES, :]
        for dev in range(1, N_DEV):
            acc = acc + g_ref[dev * SUBLANES:(dev + 1) * SUBLANES, :]
        o_ref[...] = acc

    return pl.pallas_call(body, name=name, out_shape=jax.ShapeDtypeStruct((SUBLANES, n), F32))(g)


def _adamw(w, g, m, v, name):
    shape = w.shape
    cols = shape[-1]
    rows = w.size // cols
    tr = rows
    for cand in range(SUBLANES, min(rows, 256) + 1, SUBLANES):
        if rows % cand == 0:
            tr = cand
    if rows * cols <= COMM_CHUNK_ELEMS:
        tr = rows
    c1 = 1.0 / (1.0 - ADAM_B1 ** ADAM_STEP)
    c2 = 1.0 / (1.0 - ADAM_B2 ** ADAM_STEP)

    def body(w_ref, g_ref, m_ref, v_ref, d_ref, mo_ref, vo_ref):
        gv = g_ref[...]
        m_new = ADAM_B1 * m_ref[...] + (1.0 - ADAM_B1) * gv
        v_new = ADAM_B2 * v_ref[...] + (1.0 - ADAM_B2) * (gv * gv)
        mo_ref[...] = m_new
        vo_ref[...] = v_new
        d_ref[...] = -ADAM_LR * ((m_new * c1) / (jnp.sqrt(v_new * c2) + ADAM_EPS) + ADAM_WD * w_ref[...])

    spec = pl.BlockSpec((tr, cols), lambda i: (i, 0))
    outs = pl.pallas_call(
        body, name=name, grid=(rows // tr,), in_specs=[spec] * 4, out_specs=[spec] * 3,
        out_shape=[jax.ShapeDtypeStruct((rows, cols), F32)] * 3,
    )(*[a.reshape(rows, cols) for a in (w, g, m, v)])
    return tuple(o.reshape(shape) for o in outs)


def _pad_cols(a, cols):
    return jnp.pad(a, [(0, 0)] * (a.ndim - 1) + [(0, cols - a.shape[-1])])


def _flat8(parts, width):
    v = jnp.concatenate([p.reshape(-1) for p in parts])
    return jnp.pad(v, (0, width - v.shape[0])).reshape(SUBLANES, width // SUBLANES)


KV_SHARD = 514
KV_SHARD_PAD = 640
BIG = ("a_w_in", "a_w_out", "kv_w", "b_w_q", "b_w_out", "up0", "up1", "down0", "down1")


def kernel(x, c, ada_w, ada_b, a_w_in, a_lb_logits, a_norm_g, a_w_out, kv_ada_w, kv_ada_b, kv_w, kv_b_f, k_norm_g, b_w_q, q_norm_g, b_w_out, ffn_w_up, ffn_conv_w, ffn_conv_b, ffn_w_down, loss_target, m_ada_w, m_ada_b, m_a_w_in, m_a_lb_logits, m_a_norm_g, m_a_w_out, m_kv_ada_w, m_kv_ada_b, m_kv_w, m_kv_b_f, m_k_norm_g, m_b_w_q, m_q_norm_g, m_b_w_out, m_ffn_w_up, m_ffn_conv_w, m_ffn_conv_b, m_ffn_w_down, v_ada_w, v_ada_b, v_a_w_in, v_a_lb_logits, v_a_norm_g, v_a_w_out, v_kv_ada_w, v_kv_ada_b, v_kv_w, v_kv_b_f, v_k_norm_g, v_b_w_q, v_q_norm_g, v_b_w_out, v_ffn_w_up, v_ffn_conv_w, v_ffn_conv_b, v_ffn_w_down):
    dm, ff = D_MODEL, D_FF
    ix, iy, ic = lax.axis_index("x"), lax.axis_index("y"), lax.axis_index("c")
    chip = 2 * ix + iy
    dev = 2 * chip + ic

    w1 = 10240
    g1 = _allgather8(_flat8([c, a_lb_logits, ffn_conv_w], w1), "gather_cond").reshape(N_DEV, w1)
    c_all = g1[:, :dm]
    per_chip = g1[0::2]
    lb_logits = per_chip[:, dm:dm + 512].reshape(N_CHIPS, 2, 256).transpose(1, 0, 2).reshape(2, dm)
    conv_w = per_chip[:, dm + 512:dm + 512 + 2 * CONV_W * FFN_COLS].reshape(N_CHIPS, 2, CONV_W, FFN_COLS)
    conv_w = conv_w.transpose(1, 2, 0, 3).reshape(2, CONV_W, 2, ff).transpose(0, 2, 1, 3)
    conv_b = ffn_conv_b.reshape(2, 2, 1, ff)
    lb = jax.nn.softmax(lb_logits, axis=0)[0:1]

    c16 = jnp.pad(c_all, ((0, 8), (0, 0)))
    mod_ada, c_act16 = _cond_rows(c16, ada_w, True, "mod_ada")
    mod_kv, _ = _cond_rows(c16, kv_ada_w[None], True, "mod_kv")
    mine = jnp.concatenate([mod_ada[0, :8], mod_ada[1, :8], mod_kv[0, :8]], axis=1)
    w2 = mine.shape[1]
    g2 = _allgather8(mine, "gather_mod").reshape(N_DEV, 8, w2)[0::2]
    my_rows = lax.dynamic_index_in_dim(g2, dev, axis=1, keepdims=False)
    mod0 = my_rows[:, 0:1536].reshape(6 * dm) + ada_b[0]
    mod1 = my_rows[:, 1536:3072].reshape(6 * dm) + ada_b[1]
    modk = my_rows[:, 3072:3584].reshape(2 * dm) + kv_ada_b
    mods = {"l0": [v.reshape(1, dm) for v in jnp.split(mod0, 6)],
            "l1": [v.reshape(1, dm) for v in jnp.split(mod1, 6)],
            "kv": [v.reshape(1, dm) for v in jnp.split(modk, 2)]}

    local = [(a_w_in, 0), (a_w_out, 0), (_pad_cols(kv_w, KV_SHARD_PAD)[None], 0), (b_w_q, 0), (b_w_out, 0),
             (ffn_w_up, 0), (ffn_w_up, 1), (ffn_w_down, 0), (ffn_w_down, 1)]
    chip_arr = chip.reshape(1).astype(jnp.int32)
    own = {n: _cast_own_block(w, layer, chip_arr, "cast_" + n) for n, (w, layer) in zip(BIG, local)}
    stages = {"mixer0": ("a_w_in",), "ffn0": ("a_w_out", "up0", "down0"),
              "layer1": ("kv_w", "b_w_q", "b_w_out", "up1", "down1")}
    arriving = {st: _sequencer_gather([own[n] for n in names], "gather_" + st, cid)
                for cid, (st, names) in enumerate(stages.items(), start=1)}
    rowwise = lambda g: g.reshape(1, -1, dm)

    def weights_at(stage, token):
        got, token = lax.optimization_barrier((arriving[stage], token))
        g = dict(zip(stages[stage], got))
        if stage == "mixer0":
            return {"a_w_in": g["a_w_in"]}, token
        if stage == "ffn0":
            return {"a_w_out": rowwise(g["a_w_out"]), "up0": g["up0"], "down0": rowwise(g["down0"])}, token
        kv_full = g["kv_w"][:, :, :KV_SHARD].transpose(1, 0, 2).reshape(dm, N_CHIPS * KV_SHARD)
        return {"kv_k": kv_full[None, :, :dm], "kv_v": kv_full[None, :, dm:2 * dm],
                "kv_f": _pad_cols(kv_full[None, :, 2 * dm:], LANES), "b_w_q": g["b_w_q"],
                "b_w_out": rowwise(g["b_w_out"]), "up1": g["up1"], "down1": rowwise(g["down1"])}, token

    vecs = {"a_norm_g": jnp.tile(a_norm_g, (1, HEADS)), "k_norm_g": jnp.tile(k_norm_g[None], (1, HEADS)),
            "q_norm_g": jnp.tile(q_norm_g, (1, HEADS)), "kv_b_f": _pad_cols(kv_b_f[None], LANES),
            "conv_w0": conv_w[0], "conv_b0": conv_b[0], "conv_w1": conv_w[1], "conv_b1": conv_b[1]}

    sq, grad_x, big, small, marks = _local_step(x[0], loss_target[0], mods, lb, vecs, weights_at)
    loss = lax.psum(0.5 * jnp.sum(sq) / dm, ("x", "y", "c"))

    kv_grad = jnp.concatenate([big["kv_k"][0], big["kv_v"][0], big["kv_f"][0][:, :HEADS]], axis=1)
    kv_grad = _pad_cols(kv_grad.reshape(dm, N_CHIPS, KV_SHARD).transpose(1, 0, 2), KV_SHARD_PAD)
    chipwise = lambda g: g.reshape(N_CHIPS, -1, dm)
    parts = dict(zip(BIG, [big["a_w_in"], chipwise(big["a_w_out"]), kv_grad, big["b_w_q"], chipwise(big["b_w_out"]),
                           big["up0"], big["up1"], chipwise(big["down0"]), chipwise(big["down1"])]))
    place = jnp.stack([chip, ic, dev]).astype(jnp.int32)

    served = []
    boxes = {}

    groups = (("up1", "down1"), ("b_w_out", "b_w_q", "kv_w"), ("up0", "down0", "a_w_out"), ("a_w_in",))

    def scatter_group(k):
        mine = [parts[n] for n in groups[k]]
        if served:
            mine, _ = lax.optimization_barrier((mine, served[-1]))
        boxes[k] = _sequencer_scatter(mine, "scatter_grads_%d" % k, 4 + k)
        served.append(boxes[k])

    def sum_group(k, token):
        inboxes, _ = lax.optimization_barrier((boxes[k], token))
        return [_sum_pieces(parts[n], box, place, "sum_" + n) for n, box in zip(groups[k], inboxes)]

    def swap_group(k, halves, behind):
        halves, _ = lax.optimization_barrier((halves, behind))
        return dict(zip(groups[k], _sequencer_swap_halves(halves, "swap_grads_%d" % k, 8 + k)))

    for k in range(3):
        scatter_group(k)
    halves = [sum_group(0, marks["attention_bwd"]), sum_group(1, marks["ffn0_bwd"]), sum_group(2, marks["mixer0_bwd"])]

    fold = lambda a: a.sum(axis=0)
    heads = lambda a: fold(a).reshape(HEADS, HEAD_DIM).sum(axis=0)
    conv_flat = lambda a: a.sum(axis=2).transpose(1, 0, 2)
    pieces = ([fold(a) for a in small["mod_l0"]] + [fold(a) for a in small["mod_l1"]] + [fold(a) for a in small["mod_kv"]]
              + [conv_flat(small["conv0"]), conv_flat(small["conv1"]), heads(small["a_norm_g"]), heads(small["k_norm_g"]),
                 heads(small["q_norm_g"]), fold(small["kv_b_f"]), fold(small["lb"])])
    w3 = 61440
    small_vec, _ = lax.optimization_barrier((_flat8(pieces, w3), served[2]))
    g3 = _sequencer_allgather8(small_vec, dev, "gather_small", 12)
    served.append(g3)
    scatter_group(3)
    rs = {}
    for k in range(3):
        rs.update(swap_group(k, halves[k], g3))
    tot = _sum_devices(g3, "sum_small").reshape(w3)
    n_mod = 14 * dm
    dmod_all = g3.reshape(N_DEV, w3)[:, :n_mod]
    o = n_mod
    conv_tot = [tot[o + l * 8 * ff: o + (l + 1) * 8 * ff].reshape(4, 2 * ff) for l in range(2)]
    o += 16 * ff
    g_a_norm, g_k_norm, g_q_norm = (tot[o + i * HEAD_DIM: o + (i + 1) * HEAD_DIM] for i in range(3))
    o += 3 * HEAD_DIM
    g_kv_b_f = tot[o:o + HEADS]
    dlb = tot[o + LANES:o + LANES + dm]

    ct = _pad_cols(c_act16[:8].T, LANES)
    dmod_pad = jnp.pad(dmod_all, ((0, LANES - N_DEV), (0, 0)))
    cols_ada = jnp.stack([lax.dynamic_slice_in_dim(dmod_pad, l * 6 * dm + chip * 1536, 1536, axis=1) for l in range(2)])
    cols_kv = lax.dynamic_slice_in_dim(dmod_pad, 12 * dm + chip * 512, 512, axis=1)[None]
    g_ada_w = _outer_grad(ct, cols_ada, "grad_ada_w")
    g_kv_ada_w = _outer_grad(ct, cols_kv, "grad_kv_ada_w")[0]

    my_lb = lax.dynamic_slice_in_dim(lb[0], chip * 256, 256)
    l0 = lax.dynamic_slice_in_dim(dlb, chip * 256, 256) * my_lb * (1.0 - my_lb)
    grads = {
        "ada_w": g_ada_w, "ada_b": jnp.stack([tot[:6 * dm], tot[6 * dm:12 * dm]]),
        "a_lb_logits": jnp.stack([l0, -l0]), "a_norm_g": g_a_norm[None],
        "a_w_out": rs["a_w_out"][None], "kv_ada_w": g_kv_ada_w, "kv_ada_b": tot[12 * dm:14 * dm],
        "kv_w": rs["kv_w"][:, :KV_SHARD], "kv_b_f": g_kv_b_f, "k_norm_g": g_k_norm,
        "b_w_q": rs["b_w_q"][None], "q_norm_g": g_q_norm[None], "b_w_out": rs["b_w_out"][None],
        "ffn_w_up": jnp.stack([rs["up0"], rs["up1"]]),
        "ffn_conv_w": jnp.stack([lax.dynamic_slice_in_dim(ct_l[:CONV_W], chip * FFN_COLS, FFN_COLS, axis=1) for ct_l in conv_tot]),
        "ffn_conv_b": jnp.stack([ct_l[CONV_W] for ct_l in conv_tot]),
        "ffn_w_down": jnp.stack([rs["down0"], rs["down1"]]),
    }
    weights = dict(ada_w=ada_w, ada_b=ada_b, a_w_in=a_w_in, a_lb_logits=a_lb_logits, a_norm_g=a_norm_g, a_w_out=a_w_out,
                   kv_ada_w=kv_ada_w, kv_ada_b=kv_ada_b, kv_w=kv_w, kv_b_f=kv_b_f, k_norm_g=k_norm_g, b_w_q=b_w_q,
                   q_norm_g=q_norm_g, b_w_out=b_w_out, ffn_w_up=ffn_w_up, ffn_conv_w=ffn_conv_w, ffn_conv_b=ffn_conv_b,
                   ffn_w_down=ffn_w_down)
    m_in = dict(ada_w=m_ada_w, ada_b=m_ada_b, a_w_in=m_a_w_in, a_lb_logits=m_a_lb_logits, a_norm_g=m_a_norm_g,
                a_w_out=m_a_w_out, kv_ada_w=m_kv_ada_w, kv_ada_b=m_kv_ada_b, kv_w=m_kv_w, kv_b_f=m_kv_b_f,
                k_norm_g=m_k_norm_g, b_w_q=m_b_w_q, q_norm_g=m_q_norm_g, b_w_out=m_b_w_out, ffn_w_up=m_ffn_w_up,
                ffn_conv_w=m_ffn_conv_w, ffn_conv_b=m_ffn_conv_b, ffn_w_down=m_ffn_w_down)
    v_in = dict(ada_w=v_ada_w, ada_b=v_ada_b, a_w_in=v_a_w_in, a_lb_logits=v_a_lb_logits, a_norm_g=v_a_norm_g,
                a_w_out=v_a_w_out, kv_ada_w=v_kv_ada_w, kv_ada_b=v_kv_ada_b, kv_w=v_kv_w, kv_b_f=v_kv_b_f,
                k_norm_g=v_k_norm_g, b_w_q=v_b_w_q, q_norm_g=v_q_norm_g, b_w_out=v_b_w_out, ffn_w_up=v_ffn_w_up,
                ffn_conv_w=v_ffn_conv_w, ffn_conv_b=v_ffn_conv_b, ffn_w_down=v_ffn_w_down)

    names = list(weights)
    step = lambda n: _adamw(weights[n], grads[n], m_in[n], v_in[n], "adamw_" + n)
    grads = {n: g.reshape(weights[n].shape) for n, g in grads.items()}
    upd = {n: step(n) for n in names if n != "a_w_in"}
    last = sum_group(3, [u[0] for u in upd.values()])
    grads["a_w_in"] = swap_group(3, last, last)["a_w_in"][None]
    upd["a_w_in"] = step("a_w_in")
    return (loss, grad_x[None], *[grads[n] for n in names], *[upd[n][0] for n in names],
            *[upd[n][1] for n in names], *[upd[n][2] for n in names])
```

```python
import jax
import jax.numpy as jnp
from jax import lax
from jax.experimental import pallas as pl
from jax.experimental.pallas import tpu as pltpu
from jax.experimental.pallas import tpu_sc as plsc

F32 = jnp.float32
BF16 = jnp.bfloat16

D_MODEL = 1024
HEADS = 8
HEAD_DIM = 128
A_CHUNK = 64
D_FF = 2816
CONV_W = 3
EPS = 1e-6
NEG_INF = -1e30
N_CHIPS = 4
N_DEV = 8

ADAM_LR = 0.001
ADAM_B1 = 0.9
ADAM_B2 = 0.999
ADAM_EPS = 1e-08
ADAM_WD = 0.01
ADAM_STEP = 10

SUBLANES = 8
BF16_ROWS = 16
LANES = 128
HALO = BF16_ROWS
ROW_TILE = 512
TOKEN_TILE_TN = 2048
FFN_COLS = 1408
FFN_ROWS = 256
HGRN_ROWS = 256
ATT_TILE = 512
ATT_SPLIT = 2
ATT_FWD_HEADS = 8
ATT_BWD_HEADS = 8
MESH = pl.DeviceIdType.MESH


def _sig(x):
    return jax.nn.sigmoid(x)


def _dot(a, b):
    return jnp.dot(a, b, preferred_element_type=F32)


def _dot_nt(a, b):
    return lax.dot_general(a, b, (((1,), (1,)), ((), ())), preferred_element_type=F32)


def _dot_tn(a, b):
    return lax.dot_general(a, b, (((0,), (0,)), ((), ())), preferred_element_type=F32)


def _split2(x):
    hi = x.astype(BF16)
    lo = (x - hi.astype(F32)).astype(BF16)
    return hi, lo


def _dot_f32(a, b):
    ah, al = _split2(a)
    bh, bl = _split2(b)
    return _dot(ah, bh) + _dot(ah, bl) + _dot(al, bh)


def _tri_dot(tri, x):
    hi = x.astype(BF16)
    r = x - hi.astype(F32)
    mid = r.astype(BF16)
    lo = (r - mid.astype(F32)).astype(BF16)
    return _dot(tri, hi) + _dot(tri, mid) + _dot(tri, lo)


def _tri(n, upper=False):
    r = lax.broadcasted_iota(jnp.int32, (n, n), 0)
    c = lax.broadcasted_iota(jnp.int32, (n, n), 1)
    keep = (c >= r) if upper else (c <= r)
    return jnp.where(keep, 1.0, 0.0).astype(BF16)


def _colsum8(v):
    rows, n = v.shape
    return v.reshape(rows // SUBLANES, SUBLANES, n).sum(axis=0)


def _full(shape):
    nd = len(shape)
    return pl.BlockSpec(shape, lambda *_: (0,) * nd)


def _tile(n, want):
    t = min(n, want)
    assert n % t == 0, (n, t)
    return t


def _mm_nn(a, w, groups, out_dtype, name):
    m_rows, k = a.shape
    p_n, _, n = w.shape
    per = p_n // groups
    tm = _tile(m_rows, ROW_TILE)

    def body(a_ref, w_ref, o_ref):
        av = a_ref[...]
        for p in range(p_n):
            o_ref[p // per, :, (p % per) * n:(p % per + 1) * n] = _dot(av, w_ref[p]).astype(out_dtype)

    return pl.pallas_call(
        body, name=name, grid=(m_rows // tm,),
        in_specs=[pl.BlockSpec((tm, k), lambda i: (i, 0)), _full((p_n, k, n))],
        out_specs=pl.BlockSpec((groups, tm, per * n), lambda i: (0, i, 0)),
        out_shape=jax.ShapeDtypeStruct((groups, m_rows, per * n), out_dtype),
    )(a, w)


def _mm_tn(a, d, p_n, name):
    m_rows, k = a.shape
    g_n, _, w_cols = d.shape
    per = p_n // g_n
    n = w_cols // per
    tm = _tile(m_rows, TOKEN_TILE_TN if k <= D_MODEL else ROW_TILE)
    steps = m_rows // tm

    def body(a_ref, d_ref, o_ref, acc):
        m = pl.program_id(1)

        @pl.when(m == 0)
        def _():
            acc[...] = jnp.zeros_like(acc)

        acc[...] += _dot_tn(a_ref[...], d_ref[...])

        @pl.when(m == steps - 1)
        def _():
            o_ref[...] = acc[...].astype(BF16)

    return pl.pallas_call(
        body, name=name, grid=(p_n, steps),
        in_specs=[pl.BlockSpec((tm, k), lambda p, m: (m, 0)),
                  pl.BlockSpec((None, tm, n), lambda p, m: (p // per, m, p % per))],
        out_specs=pl.BlockSpec((None, k, n), lambda p, m: (p, 0, 0)),
        out_shape=jax.ShapeDtypeStruct((p_n, k, n), BF16),
        scratch_shapes=[pltpu.VMEM((k, n), F32)],
    )(a, d)


def _premix(x, shift, scale, name):
    s, dm = x.shape
    tm = _tile(s, ROW_TILE)

    def body(x_ref, sh_ref, sc_ref, h_ref):
        xv = x_ref[...]
        inv = lax.rsqrt(jnp.mean(xv * xv, axis=-1, keepdims=True) + EPS)
        h_ref[...] = (xv * inv * (1.0 + sc_ref[...]) + sh_ref[...]).astype(BF16)

    row = pl.BlockSpec((tm, dm), lambda i: (i, 0))
    vec = _full((1, dm))
    return pl.pallas_call(body, name=name, grid=(s // tm,), in_specs=[row, vec, vec], out_specs=row,
                          out_shape=jax.ShapeDtypeStruct((s, dm), BF16))(x, shift, scale)


def _premix_bwd(x, terms, dres, name, branch=None):
    s, dm = x.shape
    tm = _tile(s, ROW_TILE)
    pairs = [pr for _, prs in terms for pr in prs]
    n_in = 2 + len(terms) + 2 * len(pairs) + (2 if branch else 0)

    def body(*refs):
        x_ref, dres_ref = refs[:2]
        sc_refs = refs[2:2 + len(terms)]
        mm_refs = refs[2 + len(terms):2 + len(terms) + 2 * len(pairs)]
        outs = refs[n_in:]

        @pl.when(pl.program_id(0) == 0)
        def _():
            for o in outs[1:1 + 2 * len(terms)]:
                o[...] = jnp.zeros_like(o)
            if branch:
                outs[-1][...] = jnp.zeros_like(outs[-1])

        xv = x_ref[...]
        inv = lax.rsqrt(jnp.mean(xv * xv, axis=-1, keepdims=True) + EPS)
        r = xv * inv
        dx = dres_ref[...]
        k = 0
        for t, (_, prs) in enumerate(terms):
            dh = None
            for d, w in prs:
                d_ref, w_ref = mm_refs[2 * k], mm_refs[2 * k + 1]
                k += 1
                p_n, _, n = w.shape
                per = p_n // d.shape[0]
                for p in range(p_n):
                    part = _dot_nt(d_ref[p // per, :, (p % per) * n:(p % per + 1) * n], w_ref[p])
                    dh = part if dh is None else dh + part
            dr = dh * (1.0 + sc_refs[t][...])
            dx = dx + inv * (dr - r * jnp.mean(dr * r, axis=-1, keepdims=True))
            outs[1 + 2 * t][...] += _colsum8(dh)
            outs[2 + 2 * t][...] += _colsum8(dh * r)
        outs[0][...] = dx
        if branch:
            y_ref, g_ref = refs[n_in - 2:n_in]
            outs[-2][0] = (dx * g_ref[...]).astype(BF16)
            outs[-1][...] += _colsum8(dx * y_ref[...])

    row = pl.BlockSpec((tm, dm), lambda i: (i, 0))
    vec, acc = _full((1, dm)), _full((SUBLANES, dm))
    ins, specs = [x, dres] + [sc for sc, _ in terms], [row, row] + [vec] * len(terms)
    for d, w in pairs:
        ins += [d, w]
        specs += [pl.BlockSpec((d.shape[0], tm, d.shape[2]), lambda i: (0, i, 0)), _full(w.shape)]
    out_shape = [jax.ShapeDtypeStruct((s, dm), F32)] + [jax.ShapeDtypeStruct((SUBLANES, dm), F32)] * (2 * len(terms))
    out_specs = [row] + [acc] * (2 * len(terms))
    if branch:
        ins += list(branch)
        specs += [row, vec]
        out_shape += [jax.ShapeDtypeStruct((1, s, dm), BF16), jax.ShapeDtypeStruct((SUBLANES, dm), F32)]
        out_specs += [pl.BlockSpec((1, tm, dm), lambda i: (0, i, 0)), acc]
    outs = pl.pallas_call(body, name=name, grid=(s // tm,), in_specs=specs, out_specs=out_specs,
                          out_shape=out_shape)(*ins)
    partials = [(outs[1 + 2 * t], outs[2 + 2 * t]) for t in range(len(terms))]
    return (outs[0], partials) + ((outs[-2], outs[-1]) if branch else ())


def _conv_taps(e, w, b):
    return w[2:3] * e + w[1:2] * pltpu.roll(e, 1, 0) + w[0:1] * pltpu.roll(e, 2, 0) + b


def _ffn_specs(s, tm, cb):
    hb = tm // HALO
    last = s // HALO - 1
    main = pl.BlockSpec((2, tm, cb), lambda j, i: (0, i, j))
    prev = pl.BlockSpec((2, HALO, cb), lambda j, i: (0, jnp.maximum(i * hb - 1, 0), j))
    nxt = pl.BlockSpec((2, HALO, cb), lambda j, i: (0, jnp.minimum((i + 1) * hb, last), j))
    wspec = pl.BlockSpec((2, CONV_W, cb), lambda j, i: (0, 0, j))
    bspec = pl.BlockSpec((2, 1, cb), lambda j, i: (0, 0, j))
    return main, prev, nxt, wspec, bspec


def _convglu_bwd(u, c, dffn, w_down, w, name):
    _, s, f = u.shape
    dm = dffn.shape[2]
    tm = _tile(s, 256)
    cb = _tile(f, FFN_COLS)
    steps = s // tm
    n_ext = tm + HALO
    main, _, nxt, wspec, _ = _ffn_specs(s, tm, cb)
    hb = tm // HALO
    last = s // HALO - 1
    d_main = pl.BlockSpec((None, tm, dm), lambda j, i: (0, i, 0))
    d_next = pl.BlockSpec((None, HALO, dm), lambda j, i: (0, jnp.minimum((i + 1) * hb, last), 0))
    wd_spec = pl.BlockSpec((None, cb, dm), lambda j, i: (0, j, 0))

    def body(u_ref, c_ref, cn_ref, d_ref, dn_ref, wd_ref, w_ref, du_ref, acc_ref):
        i = pl.program_id(1)
        notlast = jnp.where(i < steps - 1, 1.0, 0.0)

        @pl.when(i == 0)
        def _():
            acc_ref[...] = jnp.zeros_like(acc_ref)

        gate, val = (jnp.concatenate([c_ref[g].astype(F32), cn_ref[g].astype(F32)], axis=0) for g in range(2))
        wd = wd_ref[...]
        da = jnp.concatenate([_dot_nt(d_ref[...], wd).astype(BF16).astype(F32),
                              _dot_nt(dn_ref[...], wd).astype(BF16).astype(F32) * notlast], axis=0)
        sg = _sig(gate)
        d_val = da * gate * sg
        d_gate = da * val * (sg * (1.0 + gate * (1.0 - sg)))

        def finish(g, d):
            wv = w_ref[g]
            d1, d2 = pltpu.roll(d, n_ext - 1, 0), pltpu.roll(d, n_ext - 2, 0)
            du_ref[g] = (wv[2:3] * d + wv[1:2] * d1 + wv[0:1] * d2)[0:tm].astype(BF16)
            uv = u_ref[g].astype(F32)
            acc_ref[g, 2] += _colsum8(d[0:tm] * uv)
            acc_ref[g, 1] += _colsum8(d1[0:tm] * uv)
            acc_ref[g, 0] += _colsum8(d2[0:tm] * uv)
            acc_ref[g, 3] += _colsum8(d[0:tm])

        finish(0, d_gate)
        finish(1, d_val)

    return pl.pallas_call(
        body, name=name, grid=(f // cb, steps),
        in_specs=[main, main, nxt, d_main, d_next, wd_spec, wspec],
        out_specs=[main, pl.BlockSpec((2, 4, SUBLANES, cb), lambda j, i: (0, 0, 0, j))],
        out_shape=[jax.ShapeDtypeStruct((2, s, f), BF16), jax.ShapeDtypeStruct((2, 4, SUBLANES, f), F32)],
    )(u, c, c, dffn, dffn, w_down, w)


def _hgrn_gates(q_raw, f_raw, lb, tri):
    sf = _sig(f_raw)
    fg = lb + (1.0 - lb) * sf
    b = _tri_dot(tri, jnp.log(fg))
    return q_raw * _sig(q_raw), 1.0 - fg, b, fg, sf


def _hgrn_fwd(proj, lb, norm_g, name):
    s = proj.shape[0]
    tb = _tile(s, HGRN_ROWS)
    n_c = tb // A_CHUNK
    half = A_CHUNK // 2

    def body(q_ref, f_ref, v_ref, g_ref, lb_ref, ng_ref, o_ref, yp_ref, st_ref, state):
        @pl.when(pl.program_id(0) == 0)
        def _():
            state[...] = jnp.zeros_like(state)

        tri = _tri(A_CHUNK)
        causal = lax.broadcasted_iota(jnp.int32, (A_CHUNK, A_CHUNK), 1) <= lax.broadcasted_iota(
            jnp.int32, (A_CHUNK, A_CHUNK), 0)

        def chunk(ci, carry):
            rows = pl.ds(pl.multiple_of(ci * A_CHUNK, A_CHUNK), A_CHUNK)
            heads = [slice(h * HEAD_DIM, (h + 1) * HEAD_DIM) for h in range(HEADS)]
            qs, k, b, _, _ = _hgrn_gates(q_ref[rows, :], f_ref[rows, :], lb_ref[...], tri)
            b_mid, b_last = b[half:half + 1], b[A_CHUNK - 1:A_CHUNK]
            q_i = (qs * jnp.exp(b - b_mid)).astype(BF16)
            k_i = (k * jnp.exp(b_mid - b)).astype(BF16)
            q_e = (qs * jnp.exp(b)).astype(BF16)
            k_s = (k * jnp.exp(b_last - b)).astype(BF16)
            decay = jnp.exp(b_last)
            vb = v_ref[rows, :].astype(BF16)
            scores = [jnp.where(causal, _dot_nt(q_i[:, cs], k_i[:, cs]), 0.0).astype(BF16) for cs in heads]
            st = [state[h] for h in range(HEADS)]
            outs = [_dot(scores[h], vb[:, cs]) + _dot_nt(q_e[:, cs], st[h].astype(BF16)) for h, cs in enumerate(heads)]
            for h, cs in enumerate(heads):
                st_ref[ci, h] = st[h]
                state[h] = st[h] * decay[:, cs] + _dot_tn(vb[:, cs], k_s[:, cs])
            o = jnp.concatenate(outs, axis=1)
            o_ref[rows, :] = o
            sq = o * o
            inv = jnp.concatenate([jnp.broadcast_to(lax.rsqrt(jnp.mean(sq[:, cs], axis=-1, keepdims=True) + EPS),
                                                    (A_CHUNK, HEAD_DIM)) for cs in heads], axis=1)
            g_raw = g_ref[rows, :]
            yp_ref[rows, :] = (o * inv * ng_ref[...] * (g_raw * _sig(g_raw))).astype(BF16)
            return carry

        lax.fori_loop(0, n_c, chunk, 0)

    col = lambda j: pl.BlockSpec((tb, D_MODEL), lambda i: (i, j))
    vec = _full((1, D_MODEL))
    return pl.pallas_call(
        body, name=name, grid=(s // tb,), in_specs=[col(0), col(1), col(2), col(3), vec, vec],
        out_specs=[col(0), col(0), pl.BlockSpec((n_c, HEADS, HEAD_DIM, HEAD_DIM), lambda i: (i, 0, 0, 0))],
        out_shape=[jax.ShapeDtypeStruct((s, D_MODEL), F32), jax.ShapeDtypeStruct((s, D_MODEL), BF16),
                   jax.ShapeDtypeStruct((s // A_CHUNK, HEADS, HEAD_DIM, HEAD_DIM), F32)],
        scratch_shapes=[pltpu.VMEM((HEADS, HEAD_DIM, HEAD_DIM), F32)],
    )(proj, proj, proj, proj, lb, norm_g)


def _hgrn_bwd(proj, lb, norm_g, o, states, dout, w_out, name):
    s = proj.shape[0]
    tb = _tile(s, HGRN_ROWS)
    n_c = tb // A_CHUNK
    n_b = s // tb
    half = A_CHUNK // 2

    def body(q_ref, f_ref, v_ref, g_ref, lb_ref, ng_ref, o_ref, st_ref, dout_ref, w_ref, dp_ref, dlb_ref, dng_ref,
             dstate, dyp_ref):
        @pl.when(pl.program_id(0) == 0)
        def _():
            dstate[...] = jnp.zeros_like(dstate)
            dlb_ref[...] = jnp.zeros_like(dlb_ref)
            dng_ref[...] = jnp.zeros_like(dng_ref)

        dyp_ref[...] = _dot_nt(dout_ref[0], w_ref[0])

        tri = _tri(A_CHUNK)
        tri_up = _tri(A_CHUNK, upper=True)
        row_id = lax.broadcasted_iota(jnp.int32, (A_CHUNK, D_MODEL), 0)
        causal = lax.broadcasted_iota(jnp.int32, (A_CHUNK, A_CHUNK), 1) <= lax.broadcasted_iota(
            jnp.int32, (A_CHUNK, A_CHUNK), 0)

        def chunk(cj, carry):
            ci = n_c - 1 - cj
            rows = pl.ds(pl.multiple_of(ci * A_CHUNK, A_CHUNK), A_CHUNK)
            heads = [slice(h * HEAD_DIM, (h + 1) * HEAD_DIM) for h in range(HEADS)]
            cat = lambda parts: jnp.concatenate(parts, axis=1)
            per_head_mean = lambda a: cat([jnp.broadcast_to(jnp.mean(a[:, cs], axis=-1, keepdims=True),
                                                            (A_CHUNK, HEAD_DIM)) for cs in heads])
            q_raw, lbv = q_ref[rows, :], lb_ref[...]
            qs, k, b, fg, sf = _hgrn_gates(q_raw, f_ref[rows, :], lbv, tri)
            b_mid, b_last = b[half:half + 1], b[A_CHUNK - 1:A_CHUNK]
            e_qi, e_ki, e_q, e_ks = jnp.exp(b - b_mid), jnp.exp(b_mid - b), jnp.exp(b), jnp.exp(b_last - b)
            decay = jnp.exp(b_last)
            q_i, k_i, q_e, k_s = qs * e_qi, k * e_ki, qs * e_q, k * e_ks
            qib, kib, qeb, ksb = q_i.astype(BF16), k_i.astype(BF16), q_e.astype(BF16), k_s.astype(BF16)
            vb = v_ref[rows, :].astype(BF16)
            ov, g_raw, dy, ng = o_ref[rows, :], g_ref[rows, :], dyp_ref[rows, :], ng_ref[...]
            inv = lax.rsqrt(per_head_mean(ov * ov) + EPS)
            nrm = ov * inv
            sg = _sig(g_raw)
            gs = g_raw * sg
            dn = dy * ng * gs
            dng_ref[0:1, :] += jnp.sum(dy * nrm * gs, axis=0, keepdims=True)
            dg_raw = dy * nrm * ng * (sg * (1.0 + g_raw * (1.0 - sg)))
            do = (inv * (dn - nrm * per_head_mean(dn * nrm))).astype(BF16)
            st_prev = [st_ref[ci, h] for h in range(HEADS)]
            dst = [dstate[h] for h in range(HEADS)]
            dstb = [d.astype(BF16) for d in dst]
            scores = [jnp.where(causal, _dot_nt(qib[:, cs], kib[:, cs]), 0.0).astype(BF16) for cs in heads]
            d_scores = [jnp.where(causal, _dot_nt(do[:, cs], vb[:, cs]), 0.0).astype(BF16) for cs in heads]
            dv = cat([_dot_tn(scores[h], do[:, cs]) + _dot_nt(ksb[:, cs], dstb[h]) for h, cs in enumerate(heads)])
            dq_i = cat([_dot(d_scores[h], kib[:, cs]) for h, cs in enumerate(heads)])
            dk_i = cat([_dot_tn(d_scores[h], qib[:, cs]) for h, cs in enumerate(heads)])
            dq_e = cat([_dot(do[:, cs], st_prev[h].astype(BF16)) for h, cs in enumerate(heads)])
            dk_s = cat([_dot(vb[:, cs], dstb[h]) for h, cs in enumerate(heads)])
            d_decay = cat([jnp.sum(st_prev[h] * dst[h], axis=0, keepdims=True) for h in range(HEADS)])
            for h, cs in enumerate(heads):
                dstate[h] = dst[h] * decay[:, cs] + _dot_tn(do[:, cs], qeb[:, cs])
            dq = dq_i * e_qi + dq_e * e_q
            dk = dk_i * e_ki + dk_s * e_ks
            t_qi, t_ki, t_ks = dq_i * q_i, dk_i * k_i, dk_s * k_s
            db = t_qi - t_ki + dq_e * q_e - t_ks
            db_mid = jnp.sum(t_ki - t_qi, axis=0, keepdims=True)
            db_last = jnp.sum(t_ks, axis=0, keepdims=True) + d_decay * decay
            db = db + jnp.where(row_id == half, db_mid, 0.0) + jnp.where(row_id == A_CHUNK - 1, db_last, 0.0)
            dfg = _tri_dot(tri_up, db) / fg - dk
            dlb_ref[0:1, :] += jnp.sum(dfg * (1.0 - sf), axis=0, keepdims=True)
            sq = _sig(q_raw)
            dp_ref[0, rows, :] = (dq * (sq * (1.0 + q_raw * (1.0 - sq)))).astype(BF16)
            dp_ref[1, rows, :] = (dfg * (1.0 - lbv) * sf * (1.0 - sf)).astype(BF16)
            dp_ref[2, rows, :] = dv.astype(BF16)
            dp_ref[3, rows, :] = dg_raw.astype(BF16)
            return carry

        lax.fori_loop(0, n_c, chunk, 0)

    col = lambda j: pl.BlockSpec((tb, D_MODEL), lambda i: (n_b - 1 - i, j))
    vec = _full((1, D_MODEL))
    acc = _full((SUBLANES, D_MODEL))
    return pl.pallas_call(
        body, name=name, grid=(n_b,),
        in_specs=[col(0), col(1), col(2), col(3), vec, vec, col(0),
                  pl.BlockSpec((n_c, HEADS, HEAD_DIM, HEAD_DIM), lambda i: (n_b - 1 - i, 0, 0, 0)),
                  pl.BlockSpec((1, tb, D_MODEL), lambda i: (0, n_b - 1 - i, 0)), _full(w_out.shape)],
        out_specs=[pl.BlockSpec((4, tb, D_MODEL), lambda i: (0, n_b - 1 - i, 0)), acc, acc],
        out_shape=[jax.ShapeDtypeStruct((4, s, D_MODEL), BF16), jax.ShapeDtypeStruct((SUBLANES, D_MODEL), F32),
                   jax.ShapeDtypeStruct((SUBLANES, D_MODEL), F32)],
        scratch_shapes=[pltpu.VMEM((HEADS, HEAD_DIM, HEAD_DIM), F32), pltpu.VMEM((tb, D_MODEL), F32)],
    )(proj, proj, proj, proj, lb, norm_g, o, states, dout, w_out)


def _headnorm(x, g, mult, name, col0=0):
    s = x.shape[0]
    tm = _tile(s, ROW_TILE)

    def body(x_ref, g_ref, y_ref):
        for h in range(HEADS):
            cs = slice(h * HEAD_DIM, (h + 1) * HEAD_DIM)
            xv = x_ref[:, cs]
            inv = lax.rsqrt(jnp.mean(xv * xv, axis=-1, keepdims=True) + EPS)
            y_ref[:, cs] = (xv * inv * g_ref[:, cs] * mult).astype(BF16)

    return pl.pallas_call(
        body, name=name, grid=(s // tm,),
        in_specs=[pl.BlockSpec((tm, D_MODEL), lambda i: (i, col0)), _full((1, D_MODEL))],
        out_specs=pl.BlockSpec((tm, D_MODEL), lambda i: (i, 0)),
        out_shape=jax.ShapeDtypeStruct((s, D_MODEL), BF16),
    )(x, g)


def _headnorm_bwd(x, g, mult, dy, name, col0=0, extra=None):
    s = x.shape[0]
    tm = _tile(s, ROW_TILE)
    groups = 2 if extra is not None else 1
    head_major = dy.ndim == 3

    def body(*refs):
        x_ref, g_ref, dy_ref = refs[:3]
        dx_ref, dg_ref = refs[-2:]

        @pl.when(pl.program_id(0) == 0)
        def _():
            dg_ref[...] = jnp.zeros_like(dg_ref)

        for h in range(HEADS):
            cs = slice(h * HEAD_DIM, (h + 1) * HEAD_DIM)
            xv, gv = x_ref[:, cs], g_ref[:, cs]
            dyv = dy_ref[h, :, 0:HEAD_DIM] if head_major else dy_ref[:, cs]
            inv = lax.rsqrt(jnp.mean(xv * xv, axis=-1, keepdims=True) + EPS)
            nrm = xv * inv
            dn = dyv * gv * mult
            dg_ref[:, cs] += _colsum8(dyv * nrm * mult)
            dx_ref[0, :, cs] = (inv * (dn - nrm * jnp.mean(dn * nrm, axis=-1, keepdims=True))).astype(BF16)
        if extra is not None:
            dx_ref[1] = refs[3][...]

    row = pl.BlockSpec((tm, D_MODEL), lambda i: (i, 0))
    dy_spec = pl.BlockSpec((HEADS, tm, dy.shape[-1]), lambda i: (0, i, 0)) if head_major else row
    ins = [x, g, dy] + ([extra] if extra is not None else [])
    specs = ([pl.BlockSpec((tm, D_MODEL), lambda i: (i, col0)), _full((1, D_MODEL)), dy_spec]
             + ([row] if extra is not None else []))
    return pl.pallas_call(
        body, name=name, grid=(s // tm,), in_specs=specs,
        out_specs=[pl.BlockSpec((groups, tm, D_MODEL), lambda i: (0, i, 0)), _full((SUBLANES, D_MODEL))],
        out_shape=[jax.ShapeDtypeStruct((groups, s, D_MODEL), BF16), jax.ShapeDtypeStruct((SUBLANES, D_MODEL), F32)],
    )(*ins)


def _log_sigmoid(z):
    return jnp.minimum(z, 0.0) - jnp.log(1.0 + jnp.exp(-jnp.abs(z)))


Q_CUM, Q_ONE, Q_LSE = 0, 3, 6
LOG2E = 1.4426950408889634


def _pieces(v):
    hi = v.astype(BF16).astype(F32)
    mid = (v - hi).astype(BF16).astype(F32)
    lo = ((v - hi) - mid).astype(BF16).astype(F32)
    return hi, mid, lo


def _side(lane, at, v):
    hi, mid, lo = _pieces(v)
    return jnp.where(lane == at, hi, jnp.where(lane == at + 1, mid, jnp.where(lane == at + 2, lo, 0.0)))


def _fcum_fwd(f, bias, name):
    s = f.shape[0]
    tm = _tile(s, ROW_TILE)

    def body(f_ref, b_ref, qa_ref, ka_ref, carry):
        @pl.when(pl.program_id(0) == 0)
        def _():
            carry[...] = jnp.zeros_like(carry)

        cum = _tri_dot(_tri(tm), _log_sigmoid(f_ref[...] + b_ref[...])) + carry[...]
        carry[...] = cum[tm - 1:tm]
        lane = lax.broadcasted_iota(jnp.int32, (tm, LANES), 1)
        ones_q = jnp.where((lane >= Q_ONE) & (lane < Q_LSE), 1.0, 0.0)
        ones_k = jnp.where((lane < Q_ONE) | ((lane >= Q_LSE) & (lane < Q_LSE + 3)), 1.0, 0.0)
        for h in range(HEADS):
            c2 = cum[:, h:h + 1] * LOG2E
            qa_ref[h] = (_side(lane, Q_CUM, c2) + ones_q).astype(BF16)
            ka_ref[h] = (_side(lane, Q_ONE, -c2) + ones_k).astype(BF16)

    side = pl.BlockSpec((HEADS, tm, LANES), lambda i: (0, i, 0))
    return pl.pallas_call(
        body, name=name, grid=(s // tm,),
        in_specs=[pl.BlockSpec((tm, LANES), lambda i: (i, 0)), _full((1, LANES))],
        out_specs=[side, side],
        out_shape=[jax.ShapeDtypeStruct((HEADS, s, LANES), BF16)] * 2,
        scratch_shapes=[pltpu.VMEM((1, LANES), F32)],
    )(f, bias)


def _fcum_bwd(f, bias, dka, dcq, name):
    s = f.shape[0]
    tm = _tile(s, ROW_TILE)
    n_b = s // tm

    def body(f_ref, b_ref, dka_ref, dcq_ref, dz_ref, db_ref, carry):
        @pl.when(pl.program_id(0) == 0)
        def _():
            carry[...] = jnp.zeros_like(carry)
            db_ref[...] = jnp.zeros_like(db_ref)

        lane = lax.broadcasted_iota(jnp.int32, (tm, LANES), 1)
        rows = jnp.concatenate([dcq_ref[h] for h in range(HEADS)] + [jnp.zeros((LANES - HEADS, tm), F32)], axis=0)
        dcum = rows.T
        for h in range(HEADS):
            dcum = dcum - jnp.where(lane == h, dka_ref[h, :, Q_ONE:Q_ONE + 1], 0.0)
        dlf = _tri_dot(_tri(tm, upper=True), dcum) + carry[...]
        carry[...] = dlf[0:1]
        dz = dlf * _sig(-(f_ref[...] + b_ref[...]))
        dz_ref[0] = dz.astype(BF16)
        db_ref[...] += _colsum8(dz)

    return pl.pallas_call(
        body, name=name, grid=(n_b,),
        in_specs=[pl.BlockSpec((tm, LANES), lambda i: (n_b - 1 - i, 0)), _full((1, LANES)),
                  pl.BlockSpec((HEADS, tm, LANES), lambda i: (0, n_b - 1 - i, 0)),
                  pl.BlockSpec((HEADS, 1, tm), lambda i: (0, 0, n_b - 1 - i))],
        out_specs=[pl.BlockSpec((1, tm, LANES), lambda i: (0, n_b - 1 - i, 0)), _full((SUBLANES, LANES))],
        out_shape=[jax.ShapeDtypeStruct((1, s, LANES), BF16), jax.ShapeDtypeStruct((SUBLANES, LANES), F32)],
        scratch_shapes=[pltpu.VMEM((1, LANES), F32)],
    )(f, bias, dka, dcq)


def _causal_pairs(n_t, key_major):
    if key_major:
        pairs = [(qi, ki) for ki in range(n_t) for qi in range(ki, n_t)]
    else:
        pairs = [(qi, ki) for qi in range(n_t) for ki in range(qi + 1)]
    return (jnp.array([p[0] for p in pairs], jnp.int32), jnp.array([p[1] for p in pairs], jnp.int32))


def _with_side(main_ref, side_ref):
    return jnp.concatenate([main_ref[...], side_ref[...]], axis=1)


def _lane_const(t, lo, hi, value):
    lane = lax.broadcasted_iota(jnp.int32, (t, LANES), 1)
    return jnp.where((lane >= lo) & (lane < hi), value, 0.0).astype(BF16)


def _att_specs(t, nh):
    qmain = pl.BlockSpec((t, nh * HEAD_DIM), lambda h, p, qt, kt: (qt[p], h))
    kmain = pl.BlockSpec((t, nh * HEAD_DIM), lambda h, p, qt, kt: (kt[p], h))
    qside = pl.BlockSpec((nh, t, LANES), lambda h, p, qt, kt: (h, qt[p], 0))
    kside = pl.BlockSpec((nh, t, LANES), lambda h, p, qt, kt: (h, kt[p], 0))
    return qmain, kmain, qside, kside


def _fox_fwd(q, qa, k, ka, v, qo, name):
    s = q.shape[0]
    t = _tile(s, ATT_TILE)
    sub = t // ATT_SPLIT
    nh = ATT_FWD_HEADS
    qt, kt = _causal_pairs(s // t, key_major=False)

    def body(qt_ref, kt_ref, q_ref, qa_ref, k_ref, ka_ref, v_ref, og_ref, o_ref, y_ref, qab_ref, m_s, l_s, acc_s):
        pid = pl.program_id(1)
        qi, ki = qt_ref[pid], kt_ref[pid]

        @pl.when(ki == 0)
        def _():
            m_s[...] = jnp.full_like(m_s, NEG_INF)
            l_s[...] = jnp.zeros_like(l_s)
            acc_s[...] = jnp.zeros_like(acc_s)

        def step(diagonal):
            for hh in range(nh):
                hc = slice(hh * HEAD_DIM, (hh + 1) * HEAD_DIM)
                kc = jnp.concatenate([k_ref[:, hc], ka_ref[hh]], axis=1)
                vc = jnp.concatenate([v_ref[:, hc], _lane_const(t, 0, 1, 1.0)], axis=1)
                for r in range(ATT_SPLIT):
                    rows = slice(r * sub, (r + 1) * sub)
                    n_k = (r + 1) * sub if diagonal else t
                    sc = _dot_nt(jnp.concatenate([q_ref[rows, hc], qa_ref[hh, rows]], axis=1), kc[:n_k])
                    if diagonal:
                        sc = jnp.where(lax.broadcasted_iota(jnp.int32, (sub, n_k), 1)
                                       <= lax.broadcasted_iota(jnp.int32, (sub, n_k), 0) + r * sub, sc, NEG_INF)
                    m_old = m_s[hh, rows]
                    m_new = jnp.maximum(m_old, jnp.max(sc, axis=-1, keepdims=True))
                    alpha = jnp.exp2(m_old - m_new)
                    pv = _dot(jnp.exp2(sc - m_new[:, 0:1]).astype(BF16), vc[:n_k])
                    acc_s[hh, rows] = alpha * acc_s[hh, rows] + pv[:, :HEAD_DIM]
                    l_s[hh, rows] = alpha * l_s[hh, rows] + pv[:, HEAD_DIM:]
                    m_s[hh, rows] = m_new

        @pl.when(ki < qi)
        def _():
            step(False)

        @pl.when(ki == qi)
        def _():
            step(True)
            lane = lax.broadcasted_iota(jnp.int32, (t, LANES), 1)
            for hh in range(nh):
                hc = slice(hh * HEAD_DIM, (hh + 1) * HEAD_DIM)
                l = l_s[hh, :, 0:1]
                o = acc_s[hh] / l
                o_ref[:, hc] = o
                y_ref[:, hc] = (o * _sig(og_ref[:, hc])).astype(BF16)
                qab_ref[hh] = qa_ref[hh] + _side(lane, Q_LSE, -(m_s[hh, :, 0:1] + jnp.log2(l))).astype(BF16)

    qmain, kmain, qside, kside = _att_specs(t, nh)
    return pl.pallas_call(
        body, name=name,
        grid_spec=pltpu.PrefetchScalarGridSpec(
            num_scalar_prefetch=2, grid=(HEADS // nh, qt.shape[0]),
            in_specs=[qmain, qside, kmain, kside, kmain,
                      pl.BlockSpec((t, nh * HEAD_DIM), lambda h, p, qt, kt: (qt[p], HEADS // nh + h))],
            out_specs=[qmain, qmain, qside],
            scratch_shapes=[pltpu.VMEM((nh, t, LANES), F32), pltpu.VMEM((nh, t, LANES), F32),
                            pltpu.VMEM((nh, t, HEAD_DIM), F32)]),
        out_shape=[jax.ShapeDtypeStruct((s, D_MODEL), F32), jax.ShapeDtypeStruct((s, D_MODEL), BF16),
                   jax.ShapeDtypeStruct((HEADS, s, LANES), BF16)],
    )(qt, kt, q, qa, k, ka, v, qo)


def _fox_gate_bwd(o, qo, dout, w_out, name):
    s = o.shape[0]
    tm = _tile(s, ROW_TILE)

    def body(o_ref, og_ref, dout_ref, w_ref, do_ref, dg_ref, dl_ref):
        ov, dyv = o_ref[...], _dot_nt(dout_ref[0], w_ref[0])
        sg = _sig(og_ref[...])
        do = (dyv * sg).astype(BF16)
        do_ref[...] = do
        dg_ref[...] = (dyv * ov * sg * (1.0 - sg)).astype(BF16)
        prod = do.astype(F32) * ov
        lane = lax.broadcasted_iota(jnp.int32, (tm, LANES), 1)
        for h in range(HEADS):
            delta = jnp.sum(prod[:, h * HEAD_DIM:(h + 1) * HEAD_DIM], axis=-1, keepdims=True)
            dl_ref[h] = _side(lane, 0, delta).astype(BF16)

    row = pl.BlockSpec((tm, D_MODEL), lambda i: (i, 0))
    return pl.pallas_call(
        body, name=name, grid=(s // tm,),
        in_specs=[row, pl.BlockSpec((tm, D_MODEL), lambda i: (i, 1)),
                  pl.BlockSpec((1, tm, D_MODEL), lambda i: (0, i, 0)), _full(w_out.shape)],
        out_specs=[row, row, pl.BlockSpec((HEADS, tm, LANES), lambda i: (0, i, 0))],
        out_shape=[jax.ShapeDtypeStruct((s, D_MODEL), BF16), jax.ShapeDtypeStruct((s, D_MODEL), BF16),
                   jax.ShapeDtypeStruct((HEADS, s, LANES), BF16)],
    )(o, qo, dout, w_out)


def _fox_bwd(q, qab, k, ka, v, do, doa, name):
    s = q.shape[0]
    t = _tile(s, ATT_TILE)
    n_t = s // t
    sub = t // ATT_SPLIT
    nh = ATT_BWD_HEADS
    qt, kt = _causal_pairs(n_t, key_major=True)

    def body(qt_ref, kt_ref, q_ref, qab_ref, k_ref, ka_ref, v_ref, do_ref, doa_ref, dk_ref, dv_ref, dka_ref, dq_hbm,
             dcq_hbm, dk_s, dv_s, dq_ref, dcq_ref):
        group, pid = pl.program_id(0), pl.program_id(1)
        qi, ki = qt_ref[pid], kt_ref[pid]

        @pl.when(pid == 0)
        def _():
            dq_ref[...] = jnp.zeros_like(dq_ref)
            dcq_ref[...] = jnp.zeros_like(dcq_ref)

        @pl.when(qi == ki)
        def _():
            dk_s[...] = jnp.zeros_like(dk_s)
            dv_s[...] = jnp.zeros_like(dv_s)

        def step(diagonal):
            for hh in range(nh):
                hc = slice(hh * HEAD_DIM, (hh + 1) * HEAD_DIM)
                kc = jnp.concatenate([k_ref[:, hc], ka_ref[hh]], axis=1)
                vc = jnp.concatenate([v_ref[:, hc], _lane_const(t, 0, 3, -1.0)], axis=1)
                for r in range(ATT_SPLIT):
                    cols = slice(r * sub, (r + 1) * sub)
                    n_k = (r + 1) * sub if diagonal else t
                    qc = jnp.concatenate([q_ref[cols, hc], qab_ref[hh, cols]], axis=1)
                    sc = _dot_nt(kc[:n_k], qc)
                    if diagonal:
                        sc = jnp.where(lax.broadcasted_iota(jnp.int32, (n_k, sub), 0)
                                       <= lax.broadcasted_iota(jnp.int32, (n_k, sub), 1) + r * sub, sc, NEG_INF)
                    p = jnp.exp2(sc)
                    dov = do_ref[cols, hc]
                    dp = _dot_nt(vc[:n_k], jnp.concatenate([dov, doa_ref[hh, cols]], axis=1))
                    ds = (p * dp).astype(BF16)
                    dv_s[hh, 0:n_k] += _dot(p.astype(BF16), dov)
                    dk_s[hh, 0:n_k] += _dot(ds, qc)
                    q_rows = pl.ds(pl.multiple_of(qi * t + r * sub, sub), sub)
                    dq_ref[hh, q_rows, :] += _dot_tn(ds, k_ref[0:n_k, hc])
                    dcq_ref[hh, qi * ATT_SPLIT + r] += jnp.sum(ds.astype(F32), axis=0, keepdims=True)

        @pl.when(qi > ki)
        def _():
            step(False)

        @pl.when(qi == ki)
        def _():
            step(True)

        @pl.when(qi == n_t - 1)
        def _():
            for hh in range(nh):
                hc = slice(hh * HEAD_DIM, (hh + 1) * HEAD_DIM)
                dk_ref[:, hc] = dk_s[hh, :, :HEAD_DIM] * (1.0 / LOG2E)
                dka_ref[hh] = dk_s[hh, :, HEAD_DIM:]
                dv_ref[:, hc] = dv_s[hh].astype(BF16)

        @pl.when(pid == qt.shape[0] - 1)
        def _():
            pltpu.sync_copy(dq_ref, dq_hbm.at[pl.ds(group * nh, nh)])
            pltpu.sync_copy(dcq_ref, dcq_hbm.at[pl.ds(group * nh, nh)])

    qmain, kmain, qside, kside = _att_specs(t, nh)
    in_hbm = pl.BlockSpec(memory_space=pltpu.HBM)
    return pl.pallas_call(
        body, name=name,
        grid_spec=pltpu.PrefetchScalarGridSpec(
            num_scalar_prefetch=2, grid=(HEADS // nh, qt.shape[0]),
            in_specs=[qmain, qside, kmain, kside, kmain, qmain, qside],
            out_specs=[kmain, pl.BlockSpec((None, t, nh * HEAD_DIM), lambda h, p, qt, kt: (0, kt[p], h)), kside,
                       in_hbm, in_hbm],
            scratch_shapes=[pltpu.VMEM((nh, t, 2 * HEAD_DIM), F32), pltpu.VMEM((nh, t, HEAD_DIM), F32),
                            pltpu.VMEM((nh, s, HEAD_DIM), F32), pltpu.VMEM((nh, s // sub, 1, sub), F32)]),
        out_shape=[jax.ShapeDtypeStruct((s, D_MODEL), F32), jax.ShapeDtypeStruct((1, s, D_MODEL), BF16),
                   jax.ShapeDtypeStruct((HEADS, s, LANES), F32), jax.ShapeDtypeStruct((HEADS, s, HEAD_DIM), F32),
                   jax.ShapeDtypeStruct((HEADS, s // sub, 1, sub), F32)],
    )(qt, kt, q, qab, k, ka, v, do, doa)


def _mm_residual_premix(a, w, x, gate, mods, name):
    s, k = a.shape
    dm = x.shape[1]
    tm = _tile(s, ROW_TILE)

    def body(*refs):
        a_ref, w_ref, x_ref, g_ref = refs[:4]
        mod_refs = refs[4:4 + 2 * len(mods)]
        y_ref, xn_ref = refs[4 + 2 * len(mods):6 + 2 * len(mods)]
        h_refs = refs[6 + 2 * len(mods):]
        y = _dot(a_ref[...], w_ref[0])
        y_ref[...] = y
        xv = x_ref[...] + g_ref[...] * y
        xn_ref[...] = xv
        nrm = xv * lax.rsqrt(jnp.mean(xv * xv, axis=-1, keepdims=True) + EPS)
        for t, h_ref in enumerate(h_refs):
            h_ref[...] = (nrm * (1.0 + mod_refs[2 * t + 1][...]) + mod_refs[2 * t][...]).astype(BF16)

    row = pl.BlockSpec((tm, dm), lambda i: (i, 0))
    vec = _full((1, dm))
    outs = pl.pallas_call(
        body, name=name, grid=(s // tm,),
        in_specs=[pl.BlockSpec((tm, k), lambda i: (i, 0)), _full(w.shape), row, vec] + [vec] * (2 * len(mods)),
        out_specs=[row] * (2 + len(mods)),
        out_shape=[jax.ShapeDtypeStruct((s, dm), F32)] * 2 + [jax.ShapeDtypeStruct((s, dm), BF16)] * len(mods),
    )(a, w, x, gate, *[v for m in mods for v in m])
    return outs[0], outs[1], list(outs[2:])


def _mm_loss_head(a, w, x, gate, target, name):
    s, k = a.shape
    dm = x.shape[1]
    tm = _tile(s, ROW_TILE)

    def body(a_ref, w_ref, x_ref, g_ref, t_ref, sq_ref, do_ref, dy_ref, dg_ref):
        @pl.when(pl.program_id(0) == 0)
        def _():
            sq_ref[...] = jnp.zeros_like(sq_ref)
            dg_ref[...] = jnp.zeros_like(dg_ref)

        y, gv = _dot(a_ref[...], w_ref[0]), g_ref[...]
        err = x_ref[...] + gv * y - t_ref[...]
        sq_ref[...] += _colsum8(err * err)
        dout = err * (1.0 / dm)
        do_ref[...] = dout
        dy_ref[0] = (dout * gv).astype(BF16)
        dg_ref[...] += _colsum8(dout * y)

    row = pl.BlockSpec((tm, dm), lambda i: (i, 0))
    acc = _full((SUBLANES, dm))
    return pl.pallas_call(
        body, name=name, grid=(s // tm,),
        in_specs=[pl.BlockSpec((tm, k), lambda i: (i, 0)), _full(w.shape), row, _full((1, dm)), row],
        out_specs=[acc, row, pl.BlockSpec((1, tm, dm), lambda i: (0, i, 0)), acc],
        out_shape=[jax.ShapeDtypeStruct((SUBLANES, dm), F32), jax.ShapeDtypeStruct((s, dm), F32),
                   jax.ShapeDtypeStruct((1, s, dm), BF16), jax.ShapeDtypeStruct((SUBLANES, dm), F32)],
    )(a, w, x, gate, target)


def _ffn_inner(h, w_up, conv_w, conv_b, tag):
    s, dm = h.shape
    half = w_up.shape[2]
    f = 2 * half
    tm = _tile(s, FFN_ROWS)

    def body(h_ref, w_ref, cw_ref, cb_ref, u_ref, c_ref, a_ref, carry):
        @pl.when(pl.program_id(0) == 0)
        def _():
            carry[...] = jnp.zeros_like(carry)

        hv = h_ref[...]
        for j in range(2):
            cols = slice(j * half, (j + 1) * half)
            conv = []
            for g in range(2):
                ub = _dot(hv, w_ref[2 * g + j]).astype(BF16)
                u_ref[g, :, cols] = ub
                uf = ub.astype(F32)
                e = jnp.concatenate([carry[g, j], uf], axis=0)
                carry[g, j] = uf[tm - SUBLANES:tm]
                conv.append(_conv_taps(e, cw_ref[g][:, cols], cb_ref[g][:, cols])[SUBLANES:])
                c_ref[g, :, cols] = conv[g].astype(BF16)
            a_ref[:, cols] = (conv[0] * _sig(conv[0]) * conv[1]).astype(BF16)

    pair = pl.BlockSpec((2, tm, f), lambda i: (0, i, 0))
    return pl.pallas_call(
        body, name=tag + "_up_convglu", grid=(s // tm,),
        in_specs=[pl.BlockSpec((tm, dm), lambda i: (i, 0)), _full(w_up.shape), _full(conv_w.shape), _full(conv_b.shape)],
        out_specs=[pair, pair, pl.BlockSpec((tm, f), lambda i: (i, 0))],
        out_shape=[jax.ShapeDtypeStruct((2, s, f), BF16)] * 2 + [jax.ShapeDtypeStruct((s, f), BF16)],
        scratch_shapes=[pltpu.VMEM((2, 2, SUBLANES, half), F32)],
    )(h, w_up, conv_w, conv_b)


def _weight_grad_first(a, d, p_n, name):
    return lax.optimization_barrier((_mm_tn(a, d, p_n, name), d))


def _ffn_backward(dx_out, dffn, x_mid, scale, saved, w_up, conv_w, conv_b, w_down, mixer, tag):
    h, u, c, a = saved
    dw_down, dffn = _weight_grad_first(a, dffn, 1, tag + "_down_dw")
    du, dconv = _convglu_bwd(u, c, dffn, w_down, conv_w, tag + "_convglu_bwd")
    dw_up, du = _weight_grad_first(h, du, N_CHIPS, tag + "_up_dw")
    dx_mid, [(dshift, dscale)], dy, dgate_mixer = _premix_bwd(x_mid, [(scale, [(du, w_up)])], dx_out,
                                                              tag + "_premix_bwd", branch=mixer)
    return dx_mid, dy, dgate_mixer, dw_up, dw_down, dict(shift=dshift, scale=dscale, conv=dconv)


def _local_step(x, target, mods, lb, vecs, weights_at):
    m0, m1, mk = mods["l0"], mods["l1"], mods["kv"]
    h0 = _premix(x, m0[0], m0[1], "l0_premix")
    wts, h0 = weights_at("mixer0", h0)
    proj = _mm_nn(h0, wts["a_w_in"], 1, F32, "l0_in")[0]
    o_a, yp, states = _hgrn_fwd(proj, lb, vecs["a_norm_g"], "l0_hgrn")
    more, yp = weights_at("ffn0", yp)
    wts.update(more)
    y0, x1, [hf0] = _mm_residual_premix(yp, wts["a_w_out"], x, m0[2], [(m0[3], m0[4])], "l0_out")
    u0, c0, a0 = _ffn_inner(hf0, wts["up0"], vecs["conv_w0"], vecs["conv_b0"], "l0_ffn")
    saved0 = (hf0, u0, c0, a0)
    ffn0, x2, [hk, h1] = _mm_residual_premix(a0, wts["down0"], x1, m0[5], [(mk[0], mk[1]), (m1[0], m1[1])],
                                             "l0_ffn_down")
    more, hk = weights_at("layer1", hk)
    wts.update(more)
    k_raw = _mm_nn(hk, wts["kv_k"], 1, F32, "kv_k")[0]
    v_sh = _mm_nn(hk, wts["kv_v"], 1, BF16, "kv_v")[0]
    f_raw = _mm_nn(hk, wts["kv_f"], 1, F32, "kv_f")[0]
    k_sh = _headnorm(k_raw, vecs["k_norm_g"], 1.0, "kv_knorm")
    qa, ka = _fcum_fwd(f_raw, vecs["kv_b_f"], "kv_fcum")
    qo = _mm_nn(h1, wts["b_w_q"], 1, F32, "l1_q")[0]
    q_scale = HEAD_DIM ** -0.5
    q = _headnorm(qo, vecs["q_norm_g"], q_scale * LOG2E, "l1_qnorm")
    o_b, og, qab = _fox_fwd(q, qa, k_sh, ka, v_sh, qo, "l1_fox")
    y1, x3, [hf1] = _mm_residual_premix(og, wts["b_w_out"], x2, m1[2], [(m1[3], m1[4])], "l1_out")
    u1, c1, a1 = _ffn_inner(hf1, wts["up1"], vecs["conv_w1"], vecs["conv_b1"], "l1_ffn")
    saved1 = (hf1, u1, c1, a1)
    sq, dx4, dffn1, dg2_1 = _mm_loss_head(a1, wts["down1"], x3, m1[5], target, "l1_ffn_down")

    big, small = {}, {}
    dx3, dy1, dg1_1, big["up1"], big["down1"], s_ffn1 = _ffn_backward(
        dx4, dffn1, x3, m1[4], saved1, wts["up1"], vecs["conv_w1"], vecs["conv_b1"], wts["down1"], (y1, m1[2]), "l1_ffn")
    big["b_w_out"], dy1 = _weight_grad_first(og, dy1, 1, "l1_out_dw")
    do_b, dgate_b, doa = _fox_gate_bwd(o_b, qo, dy1, wts["b_w_out"], "l1_out_dx_gate_bwd")
    dk, dv, dka, dq, dcq = _fox_bwd(q, qab, k_sh, ka, v_sh, do_b, doa, "l1_fox_bwd")
    dqo, dqg = _headnorm_bwd(qo, vecs["q_norm_g"], q_scale, dq, "l1_qnorm_bwd", extra=dgate_b)
    big["b_w_q"], dqo = _weight_grad_first(h1, dqo, N_CHIPS, "l1_q_dw")
    dk_raw, dkg = _headnorm_bwd(k_raw, vecs["k_norm_g"], 1.0, dk, "kv_knorm_bwd")
    dz, dbf = _fcum_bwd(f_raw, vecs["kv_b_f"], dka, dcq.reshape(HEADS, 1, -1), "kv_fcum_bwd")
    big["kv_k"], dk_raw = _weight_grad_first(hk, dk_raw, 1, "kv_k_dw")
    big["kv_v"], dv = _weight_grad_first(hk, dv, 1, "kv_v_dw")
    big["kv_f"], dz = _weight_grad_first(hk, dz, 1, "kv_f_dw")
    kv_pairs = [(dk_raw, wts["kv_k"]), (dv, wts["kv_v"]), (dz, wts["kv_f"])]
    dx2, [(dsh1_1, dsc1_1), (dshk, dsck)], dffn0, dg2_0 = _premix_bwd(
        x2, [(m1[1], [(dqo, wts["b_w_q"])]), (mk[1], kv_pairs)], dx3, "l1_kv_premix_bwd", branch=(ffn0, m0[5]))
    dx1, dy0, dg1_0, big["up0"], big["down0"], s_ffn0 = _ffn_backward(
        dx2, dffn0, x1, m0[4], saved0, wts["up0"], vecs["conv_w0"], vecs["conv_b0"], wts["down0"], (y0, m0[2]), "l0_ffn")
    big["a_w_out"], dy0 = _weight_grad_first(yp, dy0, 1, "l0_out_dw")
    dproj, dlb, dng = _hgrn_bwd(proj, lb, vecs["a_norm_g"], o_a, states, dy0, wts["a_w_out"], "l0_out_dx_hgrn_bwd")
    grad_x, [(dsh1_0, dsc1_0)] = _premix_bwd(x, [(m0[1], [(dproj, wts["a_w_in"])])], dx1, "l0_premix_bwd")
    dproj, _ = lax.optimization_barrier((dproj, (dsh1_0, dsc1_0)))
    big["a_w_in"] = _mm_tn(h0, dproj, N_CHIPS, "l0_in_dw")

    small["mod_l0"] = [dsh1_0, dsc1_0, dg1_0, s_ffn0["shift"], s_ffn0["scale"], dg2_0]
    small["mod_l1"] = [dsh1_1, dsc1_1, dg1_1, s_ffn1["shift"], s_ffn1["scale"], dg2_1]
    small["mod_kv"] = [dshk, dsck]
    small["conv0"], small["conv1"] = s_ffn0["conv"], s_ffn1["conv"]
    small["a_norm_g"], small["k_norm_g"], small["q_norm_g"] = dng, dkg, dqg
    small["kv_b_f"], small["lb"] = dbf, dlb
    marks = {"attention_bwd": dk, "ffn0_bwd": dx1, "mixer0_bwd": grad_x}
    return sq, grad_x, big, small, marks


HBM = pl.BlockSpec(memory_space=pltpu.HBM)
COMM_CHUNK_ELEMS = 256 * 1024


def _place():
    x, y, c = lax.axis_index("x"), lax.axis_index("y"), lax.axis_index("c")
    chips = [(1 - x, y), (x, 1 - y), (1 - x, 1 - y)]
    return x, y, c, (x, y, 1 - c), chips


def _chunk_rows(rows, cols):
    best = BF16_ROWS
    for r in range(BF16_ROWS, rows + 1, BF16_ROWS):
        if rows % r == 0 and r * cols <= COMM_CHUNK_ELEMS:
            best = r
    assert rows % best == 0, (rows, cols)
    return best


def _allgather8(block, name):
    m_per, n = block.shape

    def body(x_ref, out_ref, send_sems, recv_sems, local_sem):
        x, y, c, sibling, chips = _place()
        me = (x, y, c)

        def rows(px, py, pc):
            return out_ref.at[pl.ds((4 * px + 2 * py + pc) * m_per, m_per), :]

        def copy(k, blk, to, src=None):
            return pltpu.make_async_remote_copy(
                src_ref=rows(*blk) if src is None else src, dst_ref=rows(*blk),
                send_sem=send_sems.at[k], recv_sem=recv_sems.at[k], device_id=to, device_id_type=MESH)

        mine = pltpu.make_async_copy(x_ref, rows(*me), local_sem)
        mine.start()
        first = [copy(0, me, sibling, src=x_ref)]
        first += [copy(1 + j, me, (*chip, c), src=x_ref) for j, chip in enumerate(chips)]
        for cp in first:
            cp.start()
        passed = [copy(4 + j, (*chip, c), sibling) for j, chip in enumerate(chips)]
        for j, chip in enumerate(chips):
            copy(1 + j, (*chip, c), me).wait_recv()
            passed[j].start()
        copy(0, sibling, me).wait_recv()
        for j, chip in enumerate(chips):
            copy(4 + j, (*chip, 1 - c), me).wait_recv()
        for cp in first + passed:
            cp.wait_send()
        mine.wait()

    return pl.pallas_call(
        body, name=name, out_shape=jax.ShapeDtypeStruct((N_DEV * m_per, n), block.dtype),
        in_specs=[pl.BlockSpec(memory_space=pltpu.VMEM)], out_specs=pl.BlockSpec(memory_space=pltpu.VMEM),
        scratch_shapes=[pltpu.SemaphoreType.DMA((7,)), pltpu.SemaphoreType.DMA((7,)), pltpu.SemaphoreType.DMA],
    )(block)


def _cast_own_block(shards, layer, chip, name):
    _, r, cols = shards.shape
    rows = _chunk_rows(r, cols)

    def body(chip_ref, w_ref, o_ref):
        o_ref[...] = w_ref[...].astype(BF16)

    return pl.pallas_call(
        body, name=name,
        grid_spec=pltpu.PrefetchScalarGridSpec(
            num_scalar_prefetch=1, grid=(r // rows,),
            in_specs=[pl.BlockSpec((None, rows, cols), lambda i, chip_ref: (layer, i, 0))],
            out_specs=pl.BlockSpec((None, rows, cols), lambda i, chip_ref: (chip_ref[0], i, 0))),
        out_shape=jax.ShapeDtypeStruct((N_CHIPS, r, cols), BF16),
    )(chip, shards)


def _sequencer_gather(bufs, name, collective_id):
    n_t = len(bufs)
    dims = [b.shape[1:] for b in bufs]
    refs = [jax.new_ref(b, memory_space=pltpu.MemorySpace.HBM) for b in bufs]

    @pl.kernel(mesh=plsc.ScalarSubcoreMesh(axis_name="sequencer", num_cores=1), name=name,
               scratch_types=[pltpu.SemaphoreType.DMA((n_t,))] * 4,
               compiler_params=pltpu.CompilerParams(collective_id=collective_id))
    def launch(send_ici, recv_ici, send_d2d, recv_d2d):
        x, y, c, sibling, chips = _place()
        p_me = 2 * x + y
        peers = [sibling] + [(cx, cy, c) for cx, cy in chips]
        barrier = pltpu.get_barrier_semaphore()
        for peer in peers:
            pl.semaphore_signal(barrier, inc=1, device_id=peer, device_id_type=MESH)
        pl.semaphore_wait(barrier, len(peers))

        def waiter(t, sem_s, sem_r):
            win = refs[t].at[pl.ds(0, 3), pl.ds(0, dims[t][0] // 2), :]
            return pltpu.make_async_remote_copy(src_ref=win, dst_ref=win, send_sem=sem_s.at[t], recv_sem=sem_r.at[t],
                                                device_id=sibling, device_id_type=MESH)

        def half_copy(t, chip_idx, to, sem_s, sem_r):
            r2 = dims[t][0] // 2
            win = refs[t].at[chip_idx, pl.ds(c * r2, r2), :]
            return pltpu.make_async_remote_copy(src_ref=win, dst_ref=win, send_sem=sem_s.at[t], recv_sem=sem_r.at[t],
                                                device_id=to, device_id_type=MESH)

        for t in range(n_t):
            for cx, cy in chips:
                half_copy(t, p_me, (cx, cy, c), send_ici, recv_ici).start()
        for t in range(n_t):
            waiter(t, send_ici, recv_ici).wait_recv()
            for cx, cy in chips:
                half_copy(t, 2 * cx + cy, sibling, send_d2d, recv_d2d).start()
        for t in range(n_t):
            waiter(t, send_d2d, recv_d2d).wait_recv()
            waiter(t, send_ici, recv_ici).wait_send()
            waiter(t, send_d2d, recv_d2d).wait_send()

    launch()
    return [r[...] for r in refs]


def _sequencer_allgather8(block, dev, name, collective_id):
    m_per, n = block.shape
    src = jax.new_ref(block, memory_space=pltpu.MemorySpace.HBM)
    out = jax.empty_ref(jax.ShapeDtypeStruct((N_DEV * m_per, n), block.dtype), memory_space=pltpu.MemorySpace.HBM)

    @pl.kernel(mesh=plsc.ScalarSubcoreMesh(axis_name="sequencer", num_cores=1), name=name,
               scratch_types=[pltpu.SemaphoreType.DMA((7,))] * 2,
               compiler_params=pltpu.CompilerParams(collective_id=collective_id))
    def launch(send_sems, recv_sems):
        x, y, c, sibling, chips = _place()
        me = (x, y, c)
        _handshake([sibling] + [(cx, cy, c) for cx, cy in chips])

        def rows(px, py, pc):
            return out.at[pl.ds((4 * px + 2 * py + pc) * m_per, m_per), :]

        def copy(k, blk, to, from_src=False):
            return pltpu.make_async_remote_copy(
                src_ref=src if from_src else rows(*blk), dst_ref=rows(*blk),
                send_sem=send_sems.at[k], recv_sem=recv_sems.at[k], device_id=to, device_id_type=MESH)

        first = [copy(0, me, sibling, True)] + [copy(1 + j, me, (*chip, c), True) for j, chip in enumerate(chips)]
        for cp in first:
            cp.start()
        passed = [copy(4 + j, (*chip, c), sibling) for j, chip in enumerate(chips)]
        for j, chip in enumerate(chips):
            copy(1 + j, (*chip, c), me).wait_recv()
            passed[j].start()
        copy(0, sibling, me).wait_recv()
        for j, chip in enumerate(chips):
            copy(4 + j, (*chip, 1 - c), me).wait_recv()
        for cp in first + passed:
            cp.wait_send()

    launch()
    return lax.dynamic_update_slice(out[...], block, (dev * m_per, 0))


def _others():
    x, y, c = lax.axis_index("x"), lax.axis_index("y"), lax.axis_index("c")
    flip = lambda v, f: 1 - v if f else v
    return [(flip(x, fx), flip(y, fy), flip(c, fc))
            for fx in (0, 1) for fy in (0, 1) for fc in (0, 1) if (fx, fy, fc) != (0, 0, 0)]


def _handshake(peers):
    barrier = pltpu.get_barrier_semaphore()
    for peer in peers:
        pl.semaphore_signal(barrier, inc=1, device_id=peer, device_id_type=MESH)
    pl.semaphore_wait(barrier, len(peers))


def _sequencer_scatter(parts, name, collective_id):
    n_t = len(parts)
    dims = [p.shape[1:] for p in parts]
    srcs = [jax.new_ref(p, memory_space=pltpu.MemorySpace.HBM) for p in parts]
    inboxes = [jax.empty_ref(jax.ShapeDtypeStruct((N_DEV, r // 2, cols), BF16), memory_space=pltpu.MemorySpace.HBM)
               for r, cols in dims]

    @pl.kernel(mesh=plsc.ScalarSubcoreMesh(axis_name="sequencer", num_cores=1), name=name,
               scratch_types=[pltpu.SemaphoreType.DMA((n_t,))] * 2,
               compiler_params=pltpu.CompilerParams(collective_id=collective_id))
    def launch(send_sem, recv_sem):
        x, y, c = lax.axis_index("x"), lax.axis_index("y"), lax.axis_index("c")
        me = 4 * x + 2 * y + c
        peers = _others()
        _handshake(peers)
        for t in range(n_t):
            h = dims[t][0] // 2
            for qx, qy, qc in peers:
                pltpu.make_async_remote_copy(
                    src_ref=srcs[t].at[2 * qx + qy, pl.ds(qc * h, h), :], dst_ref=inboxes[t].at[me],
                    send_sem=send_sem.at[t], recv_sem=recv_sem.at[t], device_id=(qx, qy, qc), device_id_type=MESH).start()
        for t in range(n_t):
            win = inboxes[t].at[pl.ds(0, N_DEV - 1)]
            both = pltpu.make_async_remote_copy(src_ref=win, dst_ref=win, send_sem=send_sem.at[t],
                                                recv_sem=recv_sem.at[t], device_id=peers[0], device_id_type=MESH)
            both.wait_recv()
            both.wait_send()

    launch()
    return [b[...] for b in inboxes]


def _sum_pieces(part, inbox, place, name):
    _, r, cols = part.shape
    h = r // 2
    rows = _chunk_rows(h, cols)
    steps = h // rows

    def body(place_ref, own_ref, in_ref, o_ref):
        dev = place_ref[2]
        own = own_ref[...].astype(F32)
        acc = jnp.zeros((rows, cols), F32)
        for d in range(N_DEV):
            acc = acc + jnp.where(dev == d, own, in_ref[d].astype(F32))
        o_ref[...] = acc

    return pl.pallas_call(
        body, name=name,
        grid_spec=pltpu.PrefetchScalarGridSpec(
            num_scalar_prefetch=1, grid=(steps,),
            in_specs=[pl.BlockSpec((None, rows, cols), lambda i, pr: (pr[0], pr[1] * steps + i, 0)),
                      pl.BlockSpec((N_DEV, rows, cols), lambda i, pr: (0, i, 0))],
            out_specs=pl.BlockSpec((rows, cols), lambda i, pr: (pr[1] * steps + i, 0))),
        out_shape=jax.ShapeDtypeStruct((r, cols), F32),
    )(place, part, inbox)


def _sequencer_swap_halves(halves, name, collective_id):
    n_t = len(halves)
    refs = [jax.new_ref(a, memory_space=pltpu.MemorySpace.HBM) for a in halves]

    @pl.kernel(mesh=plsc.ScalarSubcoreMesh(axis_name="sequencer", num_cores=1), name=name,
               scratch_types=[pltpu.SemaphoreType.DMA((n_t,))] * 2,
               compiler_params=pltpu.CompilerParams(collective_id=collective_id))
    def launch(send_sem, recv_sem):
        x, y, c = lax.axis_index("x"), lax.axis_index("y"), lax.axis_index("c")
        sibling = (x, y, 1 - c)
        _handshake([sibling])
        copies = []
        for t in range(n_t):
            h = halves[t].shape[0] // 2
            win = refs[t].at[pl.ds(c * h, h), :]
            copies.append(pltpu.make_async_remote_copy(src_ref=win, dst_ref=win, send_sem=send_sem.at[t],
                                                       recv_sem=recv_sem.at[t], device_id=sibling, device_id_type=MESH))
            copies[-1].start()
        for cp in copies:
            cp.wait()

    launch()
    return [r[...] for r in refs]


def _cond_rows(c16, w, act, name):
    n_l, dm, wid = w.shape

    def body(c_ref, w_ref, o_ref, a_ref):
        cv = c_ref[...]
        if act:
            cv = cv * _sig(cv)
        a_ref[...] = cv
        o_ref[...] = _dot_f32(cv, w_ref[...])

    return pl.pallas_call(
        body, name=name, grid=(n_l,),
        in_specs=[_full((16, dm)), pl.BlockSpec((None, dm, wid), lambda l: (l, 0, 0))],
        out_specs=[pl.BlockSpec((None, 16, wid), lambda l: (l, 0, 0)), _full((16, dm))],
        out_shape=[jax.ShapeDtypeStruct((n_l, 16, wid), F32), jax.ShapeDtypeStruct((16, dm), F32)],
    )(c16, w)


def _outer_grad(ct, dm, name):
    n_l, kk, wid = dm.shape
    d_rows = ct.shape[0]

    def body(c_ref, d_ref, o_ref):
        o_ref[...] = _dot_f32(c_ref[...], d_ref[...])

    return pl.pallas_call(
        body, name=name, grid=(n_l,),
        in_specs=[_full((d_rows, kk)), pl.BlockSpec((None, kk, wid), lambda l: (l, 0, 0))],
        out_specs=pl.BlockSpec((None, d_rows, wid), lambda l: (l, 0, 0)),
        out_shape=jax.ShapeDtypeStruct((n_l, d_rows, wid), F32),
    )(ct, dm)


def _sum_devices(g, name):
    rows, n = g.shape

    def body(g_ref, o_ref):
        acc = g_ref[0:SUBLANES, :]
        for dev in range(1, N_DEV):
            acc = acc + g_ref[dev * SUBLANES:(dev + 1) * SUBLANES, :]
        o_ref[...] = acc

    return pl.pallas_call(body, name=name, out_shape=jax.ShapeDtypeStruct((SUBLANES, n), F32))(g)


def _adamw(w, g, m, v, name):
    shape = w.shape
    cols = shape[-1]
    rows = w.size // cols
    tr = rows
    for cand in range(SUBLANES, min(rows, 256) + 1, SUBLANES):
        if rows % cand == 0:
            tr = cand
    if rows * cols <= COMM_CHUNK_ELEMS:
        tr = rows
    c1 = 1.0 / (1.0 - ADAM_B1 ** ADAM_STEP)
    c2 = 1.0 / (1.0 - ADAM_B2 ** ADAM_STEP)

    def body(w_ref, g_ref, m_ref, v_ref, d_ref, mo_ref, vo_ref):
        gv = g_ref[...]
        m_new = ADAM_B1 * m_ref[...] + (1.0 - ADAM_B1) * gv
        v_new = ADAM_B2 * v_ref[...] + (1.0 - ADAM_B2) * (gv * gv)
        mo_ref[...] = m_new
        vo_ref[...] = v_new
        d_ref[...] = -ADAM_LR * ((m_new * c1) / (jnp.sqrt(v_new * c2) + ADAM_EPS) + ADAM_WD * w_ref[...])

    spec = pl.BlockSpec((tr, cols), lambda i: (i, 0))
    outs = pl.pallas_call(
        body, name=name, grid=(rows // tr,), in_specs=[spec] * 4, out_specs=[spec] * 3,
        out_shape=[jax.ShapeDtypeStruct((rows, cols), F32)] * 3,
    )(*[a.reshape(rows, cols) for a in (w, g, m, v)])
    return tuple(o.reshape(shape) for o in outs)


def _pad_cols(a, cols):
    return jnp.pad(a, [(0, 0)] * (a.ndim - 1) + [(0, cols - a.shape[-1])])


def _flat8(parts, width):
    v = jnp.concatenate([p.reshape(-1) for p in parts])
    return jnp.pad(v, (0, width - v.shape[0])).reshape(SUBLANES, width // SUBLANES)


KV_SHARD = 514
KV_SHARD_PAD = 640
BIG = ("a_w_in", "a_w_out", "kv_w", "b_w_q", "b_w_out", "up0", "up1", "down0", "down1")


def kernel(x, c, ada_w, ada_b, a_w_in, a_lb_logits, a_norm_g, a_w_out, kv_ada_w, kv_ada_b, kv_w, kv_b_f, k_norm_g, b_w_q, q_norm_g, b_w_out, ffn_w_up, ffn_conv_w, ffn_conv_b, ffn_w_down, loss_target, m_ada_w, m_ada_b, m_a_w_in, m_a_lb_logits, m_a_norm_g, m_a_w_out, m_kv_ada_w, m_kv_ada_b, m_kv_w, m_kv_b_f, m_k_norm_g, m_b_w_q, m_q_norm_g, m_b_w_out, m_ffn_w_up, m_ffn_conv_w, m_ffn_conv_b, m_ffn_w_down, v_ada_w, v_ada_b, v_a_w_in, v_a_lb_logits, v_a_norm_g, v_a_w_out, v_kv_ada_w, v_kv_ada_b, v_kv_w, v_kv_b_f, v_k_norm_g, v_b_w_q, v_q_norm_g, v_b_w_out, v_ffn_w_up, v_ffn_conv_w, v_ffn_conv_b, v_ffn_w_down):
    dm, ff = D_MODEL, D_FF
    ix, iy, ic = lax.axis_index("x"), lax.axis_index("y"), lax.axis_index("c")
    chip = 2 * ix + iy
    dev = 2 * chip + ic

    w1 = 10240
    g1 = _allgather8(_flat8([c, a_lb_logits, ffn_conv_w], w1), "gather_cond").reshape(N_DEV, w1)
    c_all = g1[:, :dm]
    per_chip = g1[0::2]
    lb_logits = per_chip[:, dm:dm + 512].reshape(N_CHIPS, 2, 256).transpose(1, 0, 2).reshape(2, dm)
    conv_w = per_chip[:, dm + 512:dm + 512 + 2 * CONV_W * FFN_COLS].reshape(N_CHIPS, 2, CONV_W, FFN_COLS)
    conv_w = conv_w.transpose(1, 2, 0, 3).reshape(2, CONV_W, 2, ff).transpose(0, 2, 1, 3)
    conv_b = ffn_conv_b.reshape(2, 2, 1, ff)
    lb = jax.nn.softmax(lb_logits, axis=0)[0:1]

    c16 = jnp.pad(c_all, ((0, 8), (0, 0)))
    mod_ada, c_act16 = _cond_rows(c16, ada_w, True, "mod_ada")
    mod_kv, _ = _cond_rows(c16, kv_ada_w[None], True, "mod_kv")
    mine = jnp.concatenate([mod_ada[0, :8], mod_ada[1, :8], mod_kv[0, :8]], axis=1)
    w2 = mine.shape[1]
    g2 = _allgather8(mine, "gather_mod").reshape(N_DEV, 8, w2)[0::2]
    my_rows = lax.dynamic_index_in_dim(g2, dev, axis=1, keepdims=False)
    mod0 = my_rows[:, 0:1536].reshape(6 * dm) + ada_b[0]
    mod1 = my_rows[:, 1536:3072].reshape(6 * dm) + ada_b[1]
    modk = my_rows[:, 3072:3584].reshape(2 * dm) + kv_ada_b
    mods = {"l0": [v.reshape(1, dm) for v in jnp.split(mod0, 6)],
            "l1": [v.reshape(1, dm) for v in jnp.split(mod1, 6)],
            "kv": [v.reshape(1, dm) for v in jnp.split(modk, 2)]}

    local = [(a_w_in, 0), (a_w_out, 0), (_pad_cols(kv_w, KV_SHARD_PAD)[None], 0), (b_w_q, 0), (b_w_out, 0),
             (ffn_w_up, 0), (ffn_w_up, 1), (ffn_w_down, 0), (ffn_w_down, 1)]
    chip_arr = chip.reshape(1).astype(jnp.int32)
    own = {n: _cast_own_block(w, layer, chip_arr, "cast_" + n) for n, (w, layer) in zip(BIG, local)}
    stages = {"mixer0": ("a_w_in",), "ffn0": ("a_w_out", "up0", "down0"),
              "layer1": ("kv_w", "b_w_q", "b_w_out", "up1", "down1")}
    arriving = {st: _sequencer_gather([own[n] for n in names], "gather_" + st, cid)
                for cid, (st, names) in enumerate(stages.items(), start=1)}
    rowwise = lambda g: g.reshape(1, -1, dm)

    def weights_at(stage, token):
        got, token = lax.optimization_barrier((arriving[stage], token))
        g = dict(zip(stages[stage], got))
        if stage == "mixer0":
            return {"a_w_in": g["a_w_in"]}, token
        if stage == "ffn0":
            return {"a_w_out": rowwise(g["a_w_out"]), "up0": g["up0"], "down0": rowwise(g["down0"])}, token
        kv_full = g["kv_w"][:, :, :KV_SHARD].transpose(1, 0, 2).reshape(dm, N_CHIPS * KV_SHARD)
        return {"kv_k": kv_full[None, :, :dm], "kv_v": kv_full[None, :, dm:2 * dm],
                "kv_f": _pad_cols(kv_full[None, :, 2 * dm:], LANES), "b_w_q": g["b_w_q"],
                "b_w_out": rowwise(g["b_w_out"]), "up1": g["up1"], "down1": rowwise(g["down1"])}, token

    vecs = {"a_norm_g": jnp.tile(a_norm_g, (1, HEADS)), "k_norm_g": jnp.tile(k_norm_g[None], (1, HEADS)),
            "q_norm_g": jnp.tile(q_norm_g, (1, HEADS)), "kv_b_f": _pad_cols(kv_b_f[None], LANES),
            "conv_w0": conv_w[0], "conv_b0": conv_b[0], "conv_w1": conv_w[1], "conv_b1": conv_b[1]}

    sq, grad_x, big, small, marks = _local_step(x[0], loss_target[0], mods, lb, vecs, weights_at)
    loss = lax.psum(0.5 * jnp.sum(sq) / dm, ("x", "y", "c"))

    kv_grad = jnp.concatenate([big["kv_k"][0], big["kv_v"][0], big["kv_f"][0][:, :HEADS]], axis=1)
    kv_grad = _pad_cols(kv_grad.reshape(dm, N_CHIPS, KV_SHARD).transpose(1, 0, 2), KV_SHARD_PAD)
    chipwise = lambda g: g.reshape(N_CHIPS, -1, dm)
    parts = dict(zip(BIG, [big["a_w_in"], chipwise(big["a_w_out"]), kv_grad, big["b_w_q"], chipwise(big["b_w_out"]),
                           big["up0"], big["up1"], chipwise(big["down0"]), chipwise(big["down1"])]))
    place = jnp.stack([chip, ic, dev]).astype(jnp.int32)

    served = []
    boxes = {}

    groups = (("up1", "down1"), ("b_w_out", "b_w_q", "kv_w"), ("up0", "down0", "a_w_out"), ("a_w_in",))

    def scatter_group(k):
        mine = [parts[n] for n in groups[k]]
        if served:
            mine, _ = lax.optimization_barrier((mine, served[-1]))
        boxes[k] = _sequencer_scatter(mine, "scatter_grads_%d" % k, 4 + k)
        served.append(boxes[k])

    def sum_group(k, token):
        inboxes, _ = lax.optimization_barrier((boxes[k], token))
        return [_sum_pieces(parts[n], box, place, "sum_" + n) for n, box in zip(groups[k], inboxes)]

    def swap_group(k, halves, behind):
        halves, _ = lax.optimization_barrier((halves, behind))
        return dict(zip(groups[k], _sequencer_swap_halves(halves, "swap_grads_%d" % k, 8 + k)))

    for k in range(3):
        scatter_group(k)
    halves = [sum_group(0, marks["attention_bwd"]), sum_group(1, marks["ffn0_bwd"]), sum_group(2, marks["mixer0_bwd"])]

    fold = lambda a: a.sum(axis=0)
    heads = lambda a: fold(a).reshape(HEADS, HEAD_DIM).sum(axis=0)
    conv_flat = lambda a: a.sum(axis=2).transpose(1, 0, 2)
    pieces = ([fold(a) for a in small["mod_l0"]] + [fold(a) for a in small["mod_l1"]] + [fold(a) for a in small["mod_kv"]]
              + [conv_flat(small["conv0"]), conv_flat(small["conv1"]), heads(small["a_norm_g"]), heads(small["k_norm_g"]),
                 heads(small["q_norm_g"]), fold(small["kv_b_f"]), fold(small["lb"])])
    w3 = 61440
    small_vec, _ = lax.optimization_barrier((_flat8(pieces, w3), served[2]))
    g3 = _sequencer_allgather8(small_vec, dev, "gather_small", 12)
    served.append(g3)
    scatter_group(3)
    rs = {}
    for k in range(3):
        rs.update(swap_group(k, halves[k], g3))
    tot = _sum_devices(g3, "sum_small").reshape(w3)
    n_mod = 14 * dm
    dmod_all = g3.reshape(N_DEV, w3)[:, :n_mod]
    o = n_mod
    conv_tot = [tot[o + l * 8 * ff: o + (l + 1) * 8 * ff].reshape(4, 2 * ff) for l in range(2)]
    o += 16 * ff
    g_a_norm, g_k_norm, g_q_norm = (tot[o + i * HEAD_DIM: o + (i + 1) * HEAD_DIM] for i in range(3))
    o += 3 * HEAD_DIM
    g_kv_b_f = tot[o:o + HEADS]
    dlb = tot[o + LANES:o + LANES + dm]

    ct = _pad_cols(c_act16[:8].T, LANES)
    dmod_pad = jnp.pad(dmod_all, ((0, LANES - N_DEV), (0, 0)))
    cols_ada = jnp.stack([lax.dynamic_slice_in_dim(dmod_pad, l * 6 * dm + chip * 1536, 1536, axis=1) for l in range(2)])
    cols_kv = lax.dynamic_slice_in_dim(dmod_pad, 12 * dm + chip * 512, 512, axis=1)[None]
    g_ada_w = _outer_grad(ct, cols_ada, "grad_ada_w")
    g_kv_ada_w = _outer_grad(ct, cols_kv, "grad_kv_ada_w")[0]

    my_lb = lax.dynamic_slice_in_dim(lb[0], chip * 256, 256)
    l0 = lax.dynamic_slice_in_dim(dlb, chip * 256, 256) * my_lb * (1.0 - my_lb)
    grads = {
        "ada_w": g_ada_w, "ada_b": jnp.stack([tot[:6 * dm], tot[6 * dm:12 * dm]]),
        "a_lb_logits": jnp.stack([l0, -l0]), "a_norm_g": g_a_norm[None],
        "a_w_out": rs["a_w_out"][None], "kv_ada_w": g_kv_ada_w, "kv_ada_b": tot[12 * dm:14 * dm],
        "kv_w": rs["kv_w"][:, :KV_SHARD], "kv_b_f": g_kv_b_f, "k_norm_g": g_k_norm,
        "b_w_q": rs["b_w_q"][None], "q_norm_g": g_q_norm[None], "b_w_out": rs["b_w_out"][None],
        "ffn_w_up": jnp.stack([rs["up0"], rs["up1"]]),
        "ffn_conv_w": jnp.stack([lax.dynamic_slice_in_dim(ct_l[:CONV_W], chip * FFN_COLS, FFN_COLS, axis=1) for ct_l in conv_tot]),
        "ffn_conv_b": jnp.stack([ct_l[CONV_W] for ct_l in conv_tot]),
        "ffn_w_down": jnp.stack([rs["down0"], rs["down1"]]),
    }
    weights = dict(ada_w=ada_w, ada_b=ada_b, a_w_in=a_w_in, a_lb_logits=a_lb_logits, a_norm_g=a_norm_g, a_w_out=a_w_out,
                   kv_ada_w=kv_ada_w, kv_ada_b=kv_ada_b, kv_w=kv_w, kv_b_f=kv_b_f, k_norm_g=k_norm_g, b_w_q=b_w_q,
                   q_norm_g=q_norm_g, b_w_out=b_w_out, ffn_w_up=ffn_w_up, ffn_conv_w=ffn_conv_w, ffn_conv_b=ffn_conv_b,
                   ffn_w_down=ffn_w_down)
    m_in = dict(ada_w=m_ada_w, ada_b=m_ada_b, a_w_in=m_a_w_in, a_lb_logits=m_a_lb_logits, a_norm_g=m_a_norm_g,
                a_w_out=m_a_w_out, kv_ada_w=m_kv_ada_w, kv_ada_b=m_kv_ada_b, kv_w=m_kv_w, kv_b_f=m_kv_b_f,
                k_norm_g=m_k_norm_g, b_w_q=m_b_w_q, q_norm_g=m_q_norm_g, b_w_out=m_b_w_out, ffn_w_up=m_ffn_w_up,
                ffn_conv_w=m_ffn_conv_w, ffn_conv_b=m_ffn_conv_b, ffn_w_down=m_ffn_w_down)
    v_in = dict(ada_w=v_ada_w, ada_b=v_ada_b, a_w_in=v_a_w_in, a_lb_logits=v_a_lb_logits, a_norm_g=v_a_norm_g,
                a_w_out=v_a_w_out, kv_ada_w=v_kv_ada_w, kv_ada_b=v_kv_ada_b, kv_w=v_kv_w, kv_b_f=v_kv_b_f,
                k_norm_g=v_k_norm_g, b_w_q=v_b_w_q, q_norm_g=v_q_norm_g, b_w_out=v_b_w_out, ffn_w_up=v_ffn_w_up,
                ffn_conv_w=v_ffn_conv_w, ffn_conv_b=v_ffn_conv_b, ffn_w_down=v_ffn_w_down)

    names = list(weights)
    step = lambda n: _adamw(weights[n], grads[n], m_in[n], v_in[n], "adamw_" + n)
    grads = {n: g.reshape(weights[n].shape) for n, g in grads.items()}
    upd = {n: step(n) for n in names if n != "a_w_in"}
    last = sum_group(3, [u[0] for u in upd.values()])
    grads["a_w_in"] = swap_group(3, last, last)["a_w_in"][None]
    upd["a_w_in"] = step("a_w_in")
    return (loss, grad_x[None], *[grads[n] for n in names], *[upd[n][0] for n in names],
            *[upd[n][1] for n in names], *[upd[n][2] for n in names])
```

```python
import jax
import jax.numpy as jnp
from jax import lax
from jax.experimental import pallas as pl
from jax.experimental.pallas import tpu as pltpu
from jax.experimental.pallas import tpu_sc as plsc

F32 = jnp.float32
BF16 = jnp.bfloat16

D_MODEL = 1024
HEADS = 8
HEAD_DIM = 128
A_CHUNK = 64
D_FF = 2816
CONV_W = 3
EPS = 1e-6
NEG_INF = -1e30
N_CHIPS = 4
N_DEV = 8

ADAM_LR = 0.001
ADAM_B1 = 0.9
ADAM_B2 = 0.999
ADAM_EPS = 1e-08
ADAM_WD = 0.01
ADAM_STEP = 10

SUBLANES = 8
BF16_ROWS = 16
LANES = 128
HALO = BF16_ROWS
ROW_TILE = 512
TOKEN_TILE_TN = 2048
FFN_COLS = 1408
FFN_ROWS = 256
HGRN_ROWS = 256
ATT_TILE = 512
ATT_SPLIT = 2
ATT_FWD_HEADS = 8
ATT_BWD_HEADS = 8
MESH = pl.DeviceIdType.MESH


def _sig(x):
    return jax.nn.sigmoid(x)


def _dot(a, b):
    return jnp.dot(a, b, preferred_element_type=F32)


def _dot_nt(a, b):
    return lax.dot_general(a, b, (((1,), (1,)), ((), ())), preferred_element_type=F32)


def _dot_tn(a, b):
    return lax.dot_general(a, b, (((0,), (0,)), ((), ())), preferred_element_type=F32)


def _split2(x):
    hi = x.astype(BF16)
    lo = (x - hi.astype(F32)).astype(BF16)
    return hi, lo


def _dot_f32(a, b):
    ah, al = _split2(a)
    bh, bl = _split2(b)
    return _dot(ah, bh) + _dot(ah, bl) + _dot(al, bh)


def _tri_dot(tri, x):
    hi = x.astype(BF16)
    r = x - hi.astype(F32)
    mid = r.astype(BF16)
    lo = (r - mid.astype(F32)).astype(BF16)
    return _dot(tri, hi) + _dot(tri, mid) + _dot(tri, lo)


def _tri(n, upper=False):
    r = lax.broadcasted_iota(jnp.int32, (n, n), 0)
    c = lax.broadcasted_iota(jnp.int32, (n, n), 1)
    keep = (c >= r) if upper else (c <= r)
    return jnp.where(keep, 1.0, 0.0).astype(BF16)


def _colsum8(v):
    rows, n = v.shape
    return v.reshape(rows // SUBLANES, SUBLANES, n).sum(axis=0)


def _full(shape):
    nd = len(shape)
    return pl.BlockSpec(shape, lambda *_: (0,) * nd)


def _tile(n, want):
    t = min(n, want)
    assert n % t == 0, (n, t)
    return t


def _mm_tn(a, d, p_n, name):
    m_rows, k = a.shape
    g_n, _, w_cols = d.shape
    per = p_n // g_n
    n = w_cols // per
    tm = _tile(m_rows, TOKEN_TILE_TN if k <= D_MODEL else ROW_TILE)
    steps = m_rows // tm

    def body(a_ref, d_ref, o_ref, acc):
        m = pl.program_id(1)

        @pl.when(m == 0)
        def _():
            acc[...] = jnp.zeros_like(acc)

        acc[...] += _dot_tn(a_ref[...], d_ref[...])

        @pl.when(m == steps - 1)
        def _():
            o_ref[...] = acc[...].astype(BF16)

    return pl.pallas_call(
        body, name=name, grid=(p_n, steps),
        in_specs=[pl.BlockSpec((tm, k), lambda p, m: (m, 0)),
                  pl.BlockSpec((None, tm, n), lambda p, m: (p // per, m, p % per))],
        out_specs=pl.BlockSpec((None, k, n), lambda p, m: (p, 0, 0)),
        out_shape=jax.ShapeDtypeStruct((p_n, k, n), BF16),
        scratch_shapes=[pltpu.VMEM((k, n), F32)],
    )(a, d)


def _premix_proj(x, shift, scale, w, name):
    s, dm = x.shape
    p_n, _, n = w.shape
    tm = _tile(s, ROW_TILE)

    def body(x_ref, sh_ref, sc_ref, w_ref, h_ref, o_ref):
        xv = x_ref[...]
        inv = lax.rsqrt(jnp.mean(xv * xv, axis=-1, keepdims=True) + EPS)
        h = (xv * inv * (1.0 + sc_ref[...]) + sh_ref[...]).astype(BF16)
        h_ref[...] = h
        for p in range(p_n):
            o_ref[:, p * n:(p + 1) * n] = _dot(h, w_ref[p])

    row = pl.BlockSpec((tm, dm), lambda i: (i, 0))
    vec = _full((1, dm))
    return pl.pallas_call(
        body, name=name, grid=(s // tm,), in_specs=[row, vec, vec, _full(w.shape)],
        out_specs=[row, pl.BlockSpec((tm, p_n * n), lambda i: (i, 0))],
        out_shape=[jax.ShapeDtypeStruct((s, dm), BF16), jax.ShapeDtypeStruct((s, p_n * n), F32)],
    )(x, shift, scale, w)


def _premix_bwd(x, terms, dres, name, branch=None):
    s, dm = x.shape
    tm = _tile(s, ROW_TILE)
    pairs = [pr for _, prs in terms for pr in prs]
    n_in = 2 + len(terms) + 2 * len(pairs) + (2 if branch else 0)

    def body(*refs):
        x_ref, dres_ref = refs[:2]
        sc_refs = refs[2:2 + len(terms)]
        mm_refs = refs[2 + len(terms):2 + len(terms) + 2 * len(pairs)]
        outs = refs[n_in:]

        @pl.when(pl.program_id(0) == 0)
        def _():
            for o in outs[1:1 + 2 * len(terms)]:
                o[...] = jnp.zeros_like(o)
            if branch:
                outs[-1][...] = jnp.zeros_like(outs[-1])

        xv = x_ref[...]
        inv = lax.rsqrt(jnp.mean(xv * xv, axis=-1, keepdims=True) + EPS)
        r = xv * inv
        dx = dres_ref[...]
        k = 0
        for t, (_, prs) in enumerate(terms):
            dh = None
            for d, w in prs:
                d_ref, w_ref = mm_refs[2 * k], mm_refs[2 * k + 1]
                k += 1
                p_n, _, n = w.shape
                per = p_n // d.shape[0]
                for p in range(p_n):
                    part = _dot_nt(d_ref[p // per, :, (p % per) * n:(p % per + 1) * n], w_ref[p])
                    dh = part if dh is None else dh + part
            dr = dh * (1.0 + sc_refs[t][...])
            dx = dx + inv * (dr - r * jnp.mean(dr * r, axis=-1, keepdims=True))
            outs[1 + 2 * t][...] += _colsum8(dh)
            outs[2 + 2 * t][...] += _colsum8(dh * r)
        outs[0][...] = dx
        if branch:
            y_ref, g_ref = refs[n_in - 2:n_in]
            outs[-2][0] = (dx * g_ref[...]).astype(BF16)
            outs[-1][...] += _colsum8(dx * y_ref[...])

    row = pl.BlockSpec((tm, dm), lambda i: (i, 0))
    vec, acc = _full((1, dm)), _full((SUBLANES, dm))
    ins, specs = [x, dres] + [sc for sc, _ in terms], [row, row] + [vec] * len(terms)
    for d, w in pairs:
        ins += [d, w]
        specs += [pl.BlockSpec((d.shape[0], tm, d.shape[2]), lambda i: (0, i, 0)), _full(w.shape)]
    out_shape = [jax.ShapeDtypeStruct((s, dm), F32)] + [jax.ShapeDtypeStruct((SUBLANES, dm), F32)] * (2 * len(terms))
    out_specs = [row] + [acc] * (2 * len(terms))
    if branch:
        ins += list(branch)
        specs += [row, vec]
        out_shape += [jax.ShapeDtypeStruct((1, s, dm), BF16), jax.ShapeDtypeStruct((SUBLANES, dm), F32)]
        out_specs += [pl.BlockSpec((1, tm, dm), lambda i: (0, i, 0)), acc]
    outs = pl.pallas_call(body, name=name, grid=(s // tm,), in_specs=specs, out_specs=out_specs,
                          out_shape=out_shape)(*ins)
    partials = [(outs[1 + 2 * t], outs[2 + 2 * t]) for t in range(len(terms))]
    return (outs[0], partials) + ((outs[-2], outs[-1]) if branch else ())


def _conv_taps(e, w, b):
    return w[2:3] * e + w[1:2] * pltpu.roll(e, 1, 0) + w[0:1] * pltpu.roll(e, 2, 0) + b


def _ffn_specs(s, tm, cb):
    hb = tm // HALO
    last = s // HALO - 1
    main = pl.BlockSpec((2, tm, cb), lambda j, i: (0, i, j))
    prev = pl.BlockSpec((2, HALO, cb), lambda j, i: (0, jnp.maximum(i * hb - 1, 0), j))
    nxt = pl.BlockSpec((2, HALO, cb), lambda j, i: (0, jnp.minimum((i + 1) * hb, last), j))
    wspec = pl.BlockSpec((2, CONV_W, cb), lambda j, i: (0, 0, j))
    bspec = pl.BlockSpec((2, 1, cb), lambda j, i: (0, 0, j))
    return main, prev, nxt, wspec, bspec


def _convglu_bwd(u, c, dffn, w_down, w, name):
    _, s, f = u.shape
    dm = dffn.shape[2]
    tm = _tile(s, 256)
    cb = _tile(f, FFN_COLS)
    steps = s // tm
    n_ext = tm + HALO
    main, _, nxt, wspec, _ = _ffn_specs(s, tm, cb)
    hb = tm // HALO
    last = s // HALO - 1
    d_main = pl.BlockSpec((None, tm, dm), lambda j, i: (0, i, 0))
    d_next = pl.BlockSpec((None, HALO, dm), lambda j, i: (0, jnp.minimum((i + 1) * hb, last), 0))
    wd_spec = pl.BlockSpec((None, cb, dm), lambda j, i: (0, j, 0))

    def body(u_ref, c_ref, cn_ref, d_ref, dn_ref, wd_ref, w_ref, du_ref, acc_ref):
        i = pl.program_id(1)
        notlast = jnp.where(i < steps - 1, 1.0, 0.0)

        @pl.when(i == 0)
        def _():
            acc_ref[...] = jnp.zeros_like(acc_ref)

        gate, val = (jnp.concatenate([c_ref[g].astype(F32), cn_ref[g].astype(F32)], axis=0) for g in range(2))
        wd = wd_ref[...]
        da = jnp.concatenate([_dot_nt(d_ref[...], wd).astype(BF16).astype(F32),
                              _dot_nt(dn_ref[...], wd).astype(BF16).astype(F32) * notlast], axis=0)
        sg = _sig(gate)
        d_val = da * gate * sg
        d_gate = da * val * (sg * (1.0 + gate * (1.0 - sg)))

        def finish(g, d):
            wv = w_ref[g]
            d1, d2 = pltpu.roll(d, n_ext - 1, 0), pltpu.roll(d, n_ext - 2, 0)
            du_ref[g] = (wv[2:3] * d + wv[1:2] * d1 + wv[0:1] * d2)[0:tm].astype(BF16)
            uv = u_ref[g].astype(F32)
            acc_ref[g, 2] += _colsum8(d[0:tm] * uv)
            acc_ref[g, 1] += _colsum8(d1[0:tm] * uv)
            acc_ref[g, 0] += _colsum8(d2[0:tm] * uv)
            acc_ref[g, 3] += _colsum8(d[0:tm])

        finish(0, d_gate)
        finish(1, d_val)

    return pl.pallas_call(
        body, name=name, grid=(f // cb, steps),
        in_specs=[main, main, nxt, d_main, d_next, wd_spec, wspec],
        out_specs=[main, pl.BlockSpec((2, 4, SUBLANES, cb), lambda j, i: (0, 0, 0, j))],
        out_shape=[jax.ShapeDtypeStruct((2, s, f), BF16), jax.ShapeDtypeStruct((2, 4, SUBLANES, f), F32)],
    )(u, c, c, dffn, dffn, w_down, w)


def _hgrn_gates(q_raw, f_raw, lb, tri):
    sf = _sig(f_raw)
    fg = lb + (1.0 - lb) * sf
    b = _tri_dot(tri, jnp.log(fg))
    return q_raw * _sig(q_raw), 1.0 - fg, b, fg, sf


def _hgrn_fwd(proj, lb, norm_g, name):
    s = proj.shape[0]
    tb = _tile(s, HGRN_ROWS)
    n_c = tb // A_CHUNK
    half = A_CHUNK // 2

    def body(q_ref, f_ref, v_ref, g_ref, lb_ref, ng_ref, o_ref, yp_ref, st_ref, state):
        @pl.when(pl.program_id(0) == 0)
        def _():
            state[...] = jnp.zeros_like(state)

        tri = _tri(A_CHUNK)
        causal = lax.broadcasted_iota(jnp.int32, (A_CHUNK, A_CHUNK), 1) <= lax.broadcasted_iota(
            jnp.int32, (A_CHUNK, A_CHUNK), 0)

        def chunk(ci, carry):
            rows = pl.ds(pl.multiple_of(ci * A_CHUNK, A_CHUNK), A_CHUNK)
            heads = [slice(h * HEAD_DIM, (h + 1) * HEAD_DIM) for h in range(HEADS)]
            qs, k, b, _, _ = _hgrn_gates(q_ref[rows, :], f_ref[rows, :], lb_ref[...], tri)
            b_mid, b_last = b[half:half + 1], b[A_CHUNK - 1:A_CHUNK]
            q_i = (qs * jnp.exp(b - b_mid)).astype(BF16)
            k_i = (k * jnp.exp(b_mid - b)).astype(BF16)
            q_e = (qs * jnp.exp(b)).astype(BF16)
            k_s = (k * jnp.exp(b_last - b)).astype(BF16)
            decay = jnp.exp(b_last)
            vb = v_ref[rows, :].astype(BF16)
            scores = [jnp.where(causal, _dot_nt(q_i[:, cs], k_i[:, cs]), 0.0).astype(BF16) for cs in heads]
            st = [state[h] for h in range(HEADS)]
            outs = [_dot(scores[h], vb[:, cs]) + _dot_nt(q_e[:, cs], st[h].astype(BF16)) for h, cs in enumerate(heads)]
            for h, cs in enumerate(heads):
                st_ref[ci, h] = st[h]
                state[h] = st[h] * decay[:, cs] + _dot_tn(vb[:, cs], k_s[:, cs])
            o = jnp.concatenate(outs, axis=1)
            o_ref[rows, :] = o
            sq = o * o
            inv = jnp.concatenate([jnp.broadcast_to(lax.rsqrt(jnp.mean(sq[:, cs], axis=-1, keepdims=True) + EPS),
                                                    (A_CHUNK, HEAD_DIM)) for cs in heads], axis=1)
            g_raw = g_ref[rows, :]
            yp_ref[rows, :] = (o * inv * ng_ref[...] * (g_raw * _sig(g_raw))).astype(BF16)
            return carry

        lax.fori_loop(0, n_c, chunk, 0)

    col = lambda j: pl.BlockSpec((tb, D_MODEL), lambda i: (i, j))
    vec = _full((1, D_MODEL))
    return pl.pallas_call(
        body, name=name, grid=(s // tb,), in_specs=[col(0), col(1), col(2), col(3), vec, vec],
        out_specs=[col(0), col(0), pl.BlockSpec((n_c, HEADS, HEAD_DIM, HEAD_DIM), lambda i: (i, 0, 0, 0))],
        out_shape=[jax.ShapeDtypeStruct((s, D_MODEL), F32), jax.ShapeDtypeStruct((s, D_MODEL), BF16),
                   jax.ShapeDtypeStruct((s // A_CHUNK, HEADS, HEAD_DIM, HEAD_DIM), F32)],
        scratch_shapes=[pltpu.VMEM((HEADS, HEAD_DIM, HEAD_DIM), F32)],
    )(proj, proj, proj, proj, lb, norm_g)


def _hgrn_bwd(proj, lb, norm_g, o, states, dout, w_out, name):
    s = proj.shape[0]
    tb = _tile(s, HGRN_ROWS)
    n_c = tb // A_CHUNK
    n_b = s // tb
    half = A_CHUNK // 2

    def body(q_ref, f_ref, v_ref, g_ref, lb_ref, ng_ref, o_ref, st_ref, dout_ref, w_ref, dp_ref, dlb_ref, dng_ref,
             dstate, dyp_ref):
        @pl.when(pl.program_id(0) == 0)
        def _():
            dstate[...] = jnp.zeros_like(dstate)
            dlb_ref[...] = jnp.zeros_like(dlb_ref)
            dng_ref[...] = jnp.zeros_like(dng_ref)

        dyp_ref[...] = _dot_nt(dout_ref[0], w_ref[0])

        tri = _tri(A_CHUNK)
        tri_up = _tri(A_CHUNK, upper=True)
        row_id = lax.broadcasted_iota(jnp.int32, (A_CHUNK, D_MODEL), 0)
        causal = lax.broadcasted_iota(jnp.int32, (A_CHUNK, A_CHUNK), 1) <= lax.broadcasted_iota(
            jnp.int32, (A_CHUNK, A_CHUNK), 0)

        def chunk(cj, carry):
            ci = n_c - 1 - cj
            rows = pl.ds(pl.multiple_of(ci * A_CHUNK, A_CHUNK), A_CHUNK)
            heads = [slice(h * HEAD_DIM, (h + 1) * HEAD_DIM) for h in range(HEADS)]
            cat = lambda parts: jnp.concatenate(parts, axis=1)
            per_head_mean = lambda a: cat([jnp.broadcast_to(jnp.mean(a[:, cs], axis=-1, keepdims=True),
                                                            (A_CHUNK, HEAD_DIM)) for cs in heads])
            q_raw, lbv = q_ref[rows, :], lb_ref[...]
            qs, k, b, fg, sf = _hgrn_gates(q_raw, f_ref[rows, :], lbv, tri)
            b_mid, b_last = b[half:half + 1], b[A_CHUNK - 1:A_CHUNK]
            e_qi, e_ki, e_q, e_ks = jnp.exp(b - b_mid), jnp.exp(b_mid - b), jnp.exp(b), jnp.exp(b_last - b)
            decay = jnp.exp(b_last)
            q_i, k_i, q_e, k_s = qs * e_qi, k * e_ki, qs * e_q, k * e_ks
            qib, kib, qeb, ksb = q_i.astype(BF16), k_i.astype(BF16), q_e.astype(BF16), k_s.astype(BF16)
            vb = v_ref[rows, :].astype(BF16)
            ov, g_raw, dy, ng = o_ref[rows, :], g_ref[rows, :], dyp_ref[rows, :], ng_ref[...]
            inv = lax.rsqrt(per_head_mean(ov * ov) + EPS)
            nrm = ov * inv
            sg = _sig(g_raw)
            gs = g_raw * sg
            dn = dy * ng * gs
            dng_ref[0:1, :] += jnp.sum(dy * nrm * gs, axis=0, keepdims=True)
            dg_raw = dy * nrm * ng * (sg * (1.0 + g_raw * (1.0 - sg)))
            do = (inv * (dn - nrm * per_head_mean(dn * nrm))).astype(BF16)
            st_prev = [st_ref[ci, h] for h in range(HEADS)]
            dst = [dstate[h] for h in range(HEADS)]
            dstb = [d.astype(BF16) for d in dst]
            scores = [jnp.where(causal, _dot_nt(qib[:, cs], kib[:, cs]), 0.0).astype(BF16) for cs in heads]
            d_scores = [jnp.where(causal, _dot_nt(do[:, cs], vb[:, cs]), 0.0).astype(BF16) for cs in heads]
            dv = cat([_dot_tn(scores[h], do[:, cs]) + _dot_nt(ksb[:, cs], dstb[h]) for h, cs in enumerate(heads)])
            dq_i = cat([_dot(d_scores[h], kib[:, cs]) for h, cs in enumerate(heads)])
            dk_i = cat([_dot_tn(d_scores[h], qib[:, cs]) for h, cs in enumerate(heads)])
            dq_e = cat([_dot(do[:, cs], st_prev[h].astype(BF16)) for h, cs in enumerate(heads)])
            dk_s = cat([_dot(vb[:, cs], dstb[h]) for h, cs in enumerate(heads)])
            d_decay = cat([jnp.sum(st_prev[h] * dst[h], axis=0, keepdims=True) for h in range(HEADS)])
            for h, cs in enumerate(heads):
                dstate[h] = dst[h] * decay[:, cs] + _dot_tn(do[:, cs], qeb[:, cs])
            dq = dq_i * e_qi + dq_e * e_q
            dk = dk_i * e_ki + dk_s * e_ks
            t_qi, t_ki, t_ks = dq_i * q_i, dk_i * k_i, dk_s * k_s
            db = t_qi - t_ki + dq_e * q_e - t_ks
            db_mid = jnp.sum(t_ki - t_qi, axis=0, keepdims=True)
            db_last = jnp.sum(t_ks, axis=0, keepdims=True) + d_decay * decay
            db = db + jnp.where(row_id == half, db_mid, 0.0) + jnp.where(row_id == A_CHUNK - 1, db_last, 0.0)
            dfg = _tri_dot(tri_up, db) / fg - dk
            dlb_ref[0:1, :] += jnp.sum(dfg * (1.0 - sf), axis=0, keepdims=True)
            sq = _sig(q_raw)
            dp_ref[0, rows, :] = (dq * (sq * (1.0 + q_raw * (1.0 - sq)))).astype(BF16)
            dp_ref[1, rows, :] = (dfg * (1.0 - lbv) * sf * (1.0 - sf)).astype(BF16)
            dp_ref[2, rows, :] = dv.astype(BF16)
            dp_ref[3, rows, :] = dg_raw.astype(BF16)
            return carry

        lax.fori_loop(0, n_c, chunk, 0)

    col = lambda j: pl.BlockSpec((tb, D_MODEL), lambda i: (n_b - 1 - i, j))
    vec = _full((1, D_MODEL))
    acc = _full((SUBLANES, D_MODEL))
    return pl.pallas_call(
        body, name=name, grid=(n_b,),
        in_specs=[col(0), col(1), col(2), col(3), vec, vec, col(0),
                  pl.BlockSpec((n_c, HEADS, HEAD_DIM, HEAD_DIM), lambda i: (n_b - 1 - i, 0, 0, 0)),
                  pl.BlockSpec((1, tb, D_MODEL), lambda i: (0, n_b - 1 - i, 0)), _full(w_out.shape)],
        out_specs=[pl.BlockSpec((4, tb, D_MODEL), lambda i: (0, n_b - 1 - i, 0)), acc, acc],
        out_shape=[jax.ShapeDtypeStruct((4, s, D_MODEL), BF16), jax.ShapeDtypeStruct((SUBLANES, D_MODEL), F32),
                   jax.ShapeDtypeStruct((SUBLANES, D_MODEL), F32)],
        scratch_shapes=[pltpu.VMEM((HEADS, HEAD_DIM, HEAD_DIM), F32), pltpu.VMEM((tb, D_MODEL), F32)],
    )(proj, proj, proj, proj, lb, norm_g, o, states, dout, w_out)


def _head_rms(raw_ref, g_ref, mult, y_ref):
    for h in range(HEADS):
        cs = slice(h * HEAD_DIM, (h + 1) * HEAD_DIM)
        xv = raw_ref[:, cs]
        inv = lax.rsqrt(jnp.mean(xv * xv, axis=-1, keepdims=True) + EPS)
        y_ref[:, cs] = (xv * inv * g_ref[:, cs] * mult).astype(BF16)


def _proj_headnorm(a, w, g, mult, name):
    s, k = a.shape
    p_n, _, n = w.shape
    tm = _tile(s, ROW_TILE)

    def body(a_ref, w_ref, g_ref, raw_ref, y_ref):
        av = a_ref[...]
        for p in range(p_n):
            raw_ref[:, p * n:(p + 1) * n] = _dot(av, w_ref[p])
        _head_rms(raw_ref, g_ref, mult, y_ref)

    row = lambda wid: pl.BlockSpec((tm, wid), lambda i: (i, 0))
    return pl.pallas_call(
        body, name=name, grid=(s // tm,), in_specs=[row(k), _full(w.shape), _full((1, D_MODEL))],
        out_specs=[row(p_n * n), row(D_MODEL)],
        out_shape=[jax.ShapeDtypeStruct((s, p_n * n), F32), jax.ShapeDtypeStruct((s, D_MODEL), BF16)],
    )(a, w, g)


def _kv_proj(hk, w_k, w_v, w_f, g, name):
    s, k = hk.shape
    tm = _tile(s, ROW_TILE)

    def body(h_ref, wk_ref, wv_ref, wf_ref, g_ref, kr_ref, k_ref, v_ref, f_ref):
        hv = h_ref[...]
        kr_ref[...] = _dot(hv, wk_ref[0])
        v_ref[...] = _dot(hv, wv_ref[0]).astype(BF16)
        f_ref[...] = _dot(hv, wf_ref[0])
        _head_rms(kr_ref, g_ref, 1.0, k_ref)

    row = lambda wid: pl.BlockSpec((tm, wid), lambda i: (i, 0))
    return pl.pallas_call(
        body, name=name, grid=(s // tm,),
        in_specs=[row(k), _full(w_k.shape), _full(w_v.shape), _full(w_f.shape), _full((1, D_MODEL))],
        out_specs=[row(D_MODEL), row(D_MODEL), row(D_MODEL), row(LANES)],
        out_shape=[jax.ShapeDtypeStruct((s, D_MODEL), F32), jax.ShapeDtypeStruct((s, D_MODEL), BF16),
                   jax.ShapeDtypeStruct((s, D_MODEL), BF16), jax.ShapeDtypeStruct((s, LANES), F32)],
    )(hk, w_k, w_v, w_f, g)


def _headnorm_bwd(x, g, mult, dy, name, col0=0, extra=None):
    s = x.shape[0]
    tm = _tile(s, ROW_TILE)
    groups = 2 if extra is not None else 1
    head_major = dy.ndim == 3

    def body(*refs):
        x_ref, g_ref, dy_ref = refs[:3]
        dx_ref, dg_ref = refs[-2:]

        @pl.when(pl.program_id(0) == 0)
        def _():
            dg_ref[...] = jnp.zeros_like(dg_ref)

        for h in range(HEADS):
            cs = slice(h * HEAD_DIM, (h + 1) * HEAD_DIM)
            xv, gv = x_ref[:, cs], g_ref[:, cs]
            dyv = dy_ref[h, :, 0:HEAD_DIM] if head_major else dy_ref[:, cs]
            inv = lax.rsqrt(jnp.mean(xv * xv, axis=-1, keepdims=True) + EPS)
            nrm = xv * inv
            dn = dyv * gv * mult
            dg_ref[:, cs] += _colsum8(dyv * nrm * mult)
            dx_ref[0, :, cs] = (inv * (dn - nrm * jnp.mean(dn * nrm, axis=-1, keepdims=True))).astype(BF16)
        if extra is not None:
            dx_ref[1] = refs[3][...]

    row = pl.BlockSpec((tm, D_MODEL), lambda i: (i, 0))
    dy_spec = pl.BlockSpec((HEADS, tm, dy.shape[-1]), lambda i: (0, i, 0)) if head_major else row
    ins = [x, g, dy] + ([extra] if extra is not None else [])
    specs = ([pl.BlockSpec((tm, D_MODEL), lambda i: (i, col0)), _full((1, D_MODEL)), dy_spec]
             + ([row] if extra is not None else []))
    return pl.pallas_call(
        body, name=name, grid=(s // tm,), in_specs=specs,
        out_specs=[pl.BlockSpec((groups, tm, D_MODEL), lambda i: (0, i, 0)), _full((SUBLANES, D_MODEL))],
        out_shape=[jax.ShapeDtypeStruct((groups, s, D_MODEL), BF16), jax.ShapeDtypeStruct((SUBLANES, D_MODEL), F32)],
    )(*ins)


def _log_sigmoid(z):
    return jnp.minimum(z, 0.0) - jnp.log(1.0 + jnp.exp(-jnp.abs(z)))


Q_CUM, Q_ONE, Q_LSE = 0, 3, 6
LOG2E = 1.4426950408889634


def _pieces(v):
    hi = v.astype(BF16).astype(F32)
    mid = (v - hi).astype(BF16).astype(F32)
    lo = ((v - hi) - mid).astype(BF16).astype(F32)
    return hi, mid, lo


def _side(lane, at, v):
    hi, mid, lo = _pieces(v)
    return jnp.where(lane == at, hi, jnp.where(lane == at + 1, mid, jnp.where(lane == at + 2, lo, 0.0)))


def _fcum_fwd(f, bias, name):
    s = f.shape[0]
    tm = _tile(s, ROW_TILE)

    def body(f_ref, b_ref, qa_ref, ka_ref, carry):
        @pl.when(pl.program_id(0) == 0)
        def _():
            carry[...] = jnp.zeros_like(carry)

        cum = _tri_dot(_tri(tm), _log_sigmoid(f_ref[...] + b_ref[...])) + carry[...]
        carry[...] = cum[tm - 1:tm]
        lane = lax.broadcasted_iota(jnp.int32, (tm, LANES), 1)
        ones_q = jnp.where((lane >= Q_ONE) & (lane < Q_LSE), 1.0, 0.0)
        ones_k = jnp.where((lane < Q_ONE) | ((lane >= Q_LSE) & (lane < Q_LSE + 3)), 1.0, 0.0)
        for h in range(HEADS):
            c2 = cum[:, h:h + 1] * LOG2E
            qa_ref[h] = (_side(lane, Q_CUM, c2) + ones_q).astype(BF16)
            ka_ref[h] = (_side(lane, Q_ONE, -c2) + ones_k).astype(BF16)

    side = pl.BlockSpec((HEADS, tm, LANES), lambda i: (0, i, 0))
    return pl.pallas_call(
        body, name=name, grid=(s // tm,),
        in_specs=[pl.BlockSpec((tm, LANES), lambda i: (i, 0)), _full((1, LANES))],
        out_specs=[side, side],
        out_shape=[jax.ShapeDtypeStruct((HEADS, s, LANES), BF16)] * 2,
        scratch_shapes=[pltpu.VMEM((1, LANES), F32)],
    )(f, bias)


def _fcum_bwd(f, bias, dka, dcq, name):
    s = f.shape[0]
    tm = _tile(s, ROW_TILE)
    n_b = s // tm

    def body(f_ref, b_ref, dka_ref, dcq_ref, dz_ref, db_ref, carry):
        @pl.when(pl.program_id(0) == 0)
        def _():
            carry[...] = jnp.zeros_like(carry)
            db_ref[...] = jnp.zeros_like(db_ref)

        lane = lax.broadcasted_iota(jnp.int32, (tm, LANES), 1)
        rows = jnp.concatenate([dcq_ref[h] for h in range(HEADS)] + [jnp.zeros((LANES - HEADS, tm), F32)], axis=0)
        dcum = rows.T
        for h in range(HEADS):
            dcum = dcum - jnp.where(lane == h, dka_ref[h, :, Q_ONE:Q_ONE + 1], 0.0)
        dlf = _tri_dot(_tri(tm, upper=True), dcum) + carry[...]
        carry[...] = dlf[0:1]
        dz = dlf * _sig(-(f_ref[...] + b_ref[...]))
        dz_ref[0] = dz.astype(BF16)
        db_ref[...] += _colsum8(dz)

    return pl.pallas_call(
        body, name=name, grid=(n_b,),
        in_specs=[pl.BlockSpec((tm, LANES), lambda i: (n_b - 1 - i, 0)), _full((1, LANES)),
                  pl.BlockSpec((HEADS, tm, LANES), lambda i: (0, n_b - 1 - i, 0)),
                  pl.BlockSpec((HEADS, 1, tm), lambda i: (0, 0, n_b - 1 - i))],
        out_specs=[pl.BlockSpec((1, tm, LANES), lambda i: (0, n_b - 1 - i, 0)), _full((SUBLANES, LANES))],
        out_shape=[jax.ShapeDtypeStruct((1, s, LANES), BF16), jax.ShapeDtypeStruct((SUBLANES, LANES), F32)],
        scratch_shapes=[pltpu.VMEM((1, LANES), F32)],
    )(f, bias, dka, dcq)


def _causal_pairs(n_t, key_major):
    if key_major:
        pairs = [(qi, ki) for ki in range(n_t) for qi in range(ki, n_t)]
    else:
        pairs = [(qi, ki) for qi in range(n_t) for ki in range(qi + 1)]
    return (jnp.array([p[0] for p in pairs], jnp.int32), jnp.array([p[1] for p in pairs], jnp.int32))


def _with_side(main_ref, side_ref):
    return jnp.concatenate([main_ref[...], side_ref[...]], axis=1)


def _lane_const(t, lo, hi, value):
    lane = lax.broadcasted_iota(jnp.int32, (t, LANES), 1)
    return jnp.where((lane >= lo) & (lane < hi), value, 0.0).astype(BF16)


def _att_specs(t, nh):
    qmain = pl.BlockSpec((t, nh * HEAD_DIM), lambda h, p, qt, kt: (qt[p], h))
    kmain = pl.BlockSpec((t, nh * HEAD_DIM), lambda h, p, qt, kt: (kt[p], h))
    qside = pl.BlockSpec((nh, t, LANES), lambda h, p, qt, kt: (h, qt[p], 0))
    kside = pl.BlockSpec((nh, t, LANES), lambda h, p, qt, kt: (h, kt[p], 0))
    return qmain, kmain, qside, kside


def _fox_fwd(q, qa, k, ka, v, qo, name):
    s = q.shape[0]
    t = _tile(s, ATT_TILE)
    sub = t // ATT_SPLIT
    nh = ATT_FWD_HEADS
    qt, kt = _causal_pairs(s // t, key_major=False)

    def body(qt_ref, kt_ref, q_ref, qa_ref, k_ref, ka_ref, v_ref, og_ref, o_ref, y_ref, qab_ref, m_s, l_s, acc_s):
        pid = pl.program_id(1)
        qi, ki = qt_ref[pid], kt_ref[pid]

        @pl.when(ki == 0)
        def _():
            m_s[...] = jnp.full_like(m_s, NEG_INF)
            l_s[...] = jnp.zeros_like(l_s)
            acc_s[...] = jnp.zeros_like(acc_s)

        def step(diagonal):
            for hh in range(nh):
                hc = slice(hh * HEAD_DIM, (hh + 1) * HEAD_DIM)
                kc = jnp.concatenate([k_ref[:, hc], ka_ref[hh]], axis=1)
                vc = jnp.concatenate([v_ref[:, hc], _lane_const(t, 0, 1, 1.0)], axis=1)
                for r in range(ATT_SPLIT):
                    rows = slice(r * sub, (r + 1) * sub)
                    n_k = (r + 1) * sub if diagonal else t
                    sc = _dot_nt(jnp.concatenate([q_ref[rows, hc], qa_ref[hh, rows]], axis=1), kc[:n_k])
                    if diagonal:
                        sc = jnp.where(lax.broadcasted_iota(jnp.int32, (sub, n_k), 1)
                                       <= lax.broadcasted_iota(jnp.int32, (sub, n_k), 0) + r * sub, sc, NEG_INF)
                    m_old = m_s[hh, rows]
                    m_new = jnp.maximum(m_old, jnp.max(sc, axis=-1, keepdims=True))
                    alpha = jnp.exp2(m_old - m_new)
                    pv = _dot(jnp.exp2(sc - m_new[:, 0:1]).astype(BF16), vc[:n_k])
                    acc_s[hh, rows] = alpha * acc_s[hh, rows] + pv[:, :HEAD_DIM]
                    l_s[hh, rows] = alpha * l_s[hh, rows] + pv[:, HEAD_DIM:]
                    m_s[hh, rows] = m_new

        @pl.when(ki < qi)
        def _():
            step(False)

        @pl.when(ki == qi)
        def _():
            step(True)
            lane = lax.broadcasted_iota(jnp.int32, (t, LANES), 1)
            for hh in range(nh):
                hc = slice(hh * HEAD_DIM, (hh + 1) * HEAD_DIM)
                l = l_s[hh, :, 0:1]
                o = acc_s[hh] / l
                o_ref[:, hc] = o
                y_ref[:, hc] = (o * _sig(og_ref[:, hc])).astype(BF16)
                qab_ref[hh] = qa_ref[hh] + _side(lane, Q_LSE, -(m_s[hh, :, 0:1] + jnp.log2(l))).astype(BF16)

    qmain, kmain, qside, kside = _att_specs(t, nh)
    return pl.pallas_call(
        body, name=name,
        grid_spec=pltpu.PrefetchScalarGridSpec(
            num_scalar_prefetch=2, grid=(HEADS // nh, qt.shape[0]),
            in_specs=[qmain, qside, kmain, kside, kmain,
                      pl.BlockSpec((t, nh * HEAD_DIM), lambda h, p, qt, kt: (qt[p], HEADS // nh + h))],
            out_specs=[qmain, qmain, qside],
            scratch_shapes=[pltpu.VMEM((nh, t, LANES), F32), pltpu.VMEM((nh, t, LANES), F32),
                            pltpu.VMEM((nh, t, HEAD_DIM), F32)]),
        out_shape=[jax.ShapeDtypeStruct((s, D_MODEL), F32), jax.ShapeDtypeStruct((s, D_MODEL), BF16),
                   jax.ShapeDtypeStruct((HEADS, s, LANES), BF16)],
    )(qt, kt, q, qa, k, ka, v, qo)


def _fox_gate_bwd(o, qo, dout, w_out, name):
    s = o.shape[0]
    tm = _tile(s, ROW_TILE)

    def body(o_ref, og_ref, dout_ref, w_ref, do_ref, dg_ref, dl_ref):
        ov, dyv = o_ref[...], _dot_nt(dout_ref[0], w_ref[0])
        sg = _sig(og_ref[...])
        do = (dyv * sg).astype(BF16)
        do_ref[...] = do
        dg_ref[...] = (dyv * ov * sg * (1.0 - sg)).astype(BF16)
        prod = do.astype(F32) * ov
        lane = lax.broadcasted_iota(jnp.int32, (tm, LANES), 1)
        for h in range(HEADS):
            delta = jnp.sum(prod[:, h * HEAD_DIM:(h + 1) * HEAD_DIM], axis=-1, keepdims=True)
            dl_ref[h] = _side(lane, 0, delta).astype(BF16)

    row = pl.BlockSpec((tm, D_MODEL), lambda i: (i, 0))
    return pl.pallas_call(
        body, name=name, grid=(s // tm,),
        in_specs=[row, pl.BlockSpec((tm, D_MODEL), lambda i: (i, 1)),
                  pl.BlockSpec((1, tm, D_MODEL), lambda i: (0, i, 0)), _full(w_out.shape)],
        out_specs=[row, row, pl.BlockSpec((HEADS, tm, LANES), lambda i: (0, i, 0))],
        out_shape=[jax.ShapeDtypeStruct((s, D_MODEL), BF16), jax.ShapeDtypeStruct((s, D_MODEL), BF16),
                   jax.ShapeDtypeStruct((HEADS, s, LANES), BF16)],
    )(o, qo, dout, w_out)


def _fox_bwd(q, qab, k, ka, v, do, doa, name):
    s = q.shape[0]
    t = _tile(s, ATT_TILE)
    n_t = s // t
    sub = t // ATT_SPLIT
    nh = ATT_BWD_HEADS
    qt, kt = _causal_pairs(n_t, key_major=True)

    def body(qt_ref, kt_ref, q_ref, qab_ref, k_ref, ka_ref, v_ref, do_ref, doa_ref, dk_ref, dv_ref, dka_ref, dq_hbm,
             dcq_hbm, dk_s, dv_s, dq_ref, dcq_ref):
        group, pid = pl.program_id(0), pl.program_id(1)
        qi, ki = qt_ref[pid], kt_ref[pid]

        @pl.when(pid == 0)
        def _():
            dq_ref[...] = jnp.zeros_like(dq_ref)
            dcq_ref[...] = jnp.zeros_like(dcq_ref)

        @pl.when(qi == ki)
        def _():
            dk_s[...] = jnp.zeros_like(dk_s)
            dv_s[...] = jnp.zeros_like(dv_s)

        def step(diagonal):
            for hh in range(nh):
                hc = slice(hh * HEAD_DIM, (hh + 1) * HEAD_DIM)
                kc = jnp.concatenate([k_ref[:, hc], ka_ref[hh]], axis=1)
                vc = jnp.concatenate([v_ref[:, hc], _lane_const(t, 0, 3, -1.0)], axis=1)
                for r in range(ATT_SPLIT):
                    cols = slice(r * sub, (r + 1) * sub)
                    n_k = (r + 1) * sub if diagonal else t
                    qc = jnp.concatenate([q_ref[cols, hc], qab_ref[hh, cols]], axis=1)
                    sc = _dot_nt(kc[:n_k], qc)
                    if diagonal:
                        sc = jnp.where(lax.broadcasted_iota(jnp.int32, (n_k, sub), 0)
                                       <= lax.broadcasted_iota(jnp.int32, (n_k, sub), 1) + r * sub, sc, NEG_INF)
                    p = jnp.exp2(sc)
                    dov = do_ref[cols, hc]
                    dp = _dot_nt(vc[:n_k], jnp.concatenate([dov, doa_ref[hh, cols]], axis=1))
                    ds = (p * dp).astype(BF16)
                    dv_s[hh, 0:n_k] += _dot(p.astype(BF16), dov)
                    dk_s[hh, 0:n_k] += _dot(ds, qc)
                    q_rows = pl.ds(pl.multiple_of(qi * t + r * sub, sub), sub)
                    dq_ref[hh, q_rows, :] += _dot_tn(ds, k_ref[0:n_k, hc])
                    dcq_ref[hh, qi * ATT_SPLIT + r] += jnp.sum(ds.astype(F32), axis=0, keepdims=True)

        @pl.when(qi > ki)
        def _():
            step(False)

        @pl.when(qi == ki)
        def _():
            step(True)

        @pl.when(qi == n_t - 1)
        def _():
            for hh in range(nh):
                hc = slice(hh * HEAD_DIM, (hh + 1) * HEAD_DIM)
                dk_ref[:, hc] = dk_s[hh, :, :HEAD_DIM] * (1.0 / LOG2E)
                dka_ref[hh] = dk_s[hh, :, HEAD_DIM:]
                dv_ref[:, hc] = dv_s[hh].astype(BF16)

        @pl.when(pid == qt.shape[0] - 1)
        def _():
            pltpu.sync_copy(dq_ref, dq_hbm.at[pl.ds(group * nh, nh)])
            pltpu.sync_copy(dcq_ref, dcq_hbm.at[pl.ds(group * nh, nh)])

    qmain, kmain, qside, kside = _att_specs(t, nh)
    in_hbm = pl.BlockSpec(memory_space=pltpu.HBM)
    return pl.pallas_call(
        body, name=name,
        grid_spec=pltpu.PrefetchScalarGridSpec(
            num_scalar_prefetch=2, grid=(HEADS // nh, qt.shape[0]),
            in_specs=[qmain, qside, kmain, kside, kmain, qmain, qside],
            out_specs=[kmain, pl.BlockSpec((None, t, nh * HEAD_DIM), lambda h, p, qt, kt: (0, kt[p], h)), kside,
                       in_hbm, in_hbm],
            scratch_shapes=[pltpu.VMEM((nh, t, 2 * HEAD_DIM), F32), pltpu.VMEM((nh, t, HEAD_DIM), F32),
                            pltpu.VMEM((nh, s, HEAD_DIM), F32), pltpu.VMEM((nh, s // sub, 1, sub), F32)]),
        out_shape=[jax.ShapeDtypeStruct((s, D_MODEL), F32), jax.ShapeDtypeStruct((1, s, D_MODEL), BF16),
                   jax.ShapeDtypeStruct((HEADS, s, LANES), F32), jax.ShapeDtypeStruct((HEADS, s, HEAD_DIM), F32),
                   jax.ShapeDtypeStruct((HEADS, s // sub, 1, sub), F32)],
    )(qt, kt, q, qab, k, ka, v, do, doa)


def _mm_residual_premix(a, w, x, gate, mods, name):
    s, k = a.shape
    dm = x.shape[1]
    tm = _tile(s, ROW_TILE)

    def body(*refs):
        a_ref, w_ref, x_ref, g_ref = refs[:4]
        mod_refs = refs[4:4 + 2 * len(mods)]
        y_ref, xn_ref = refs[4 + 2 * len(mods):6 + 2 * len(mods)]
        h_refs = refs[6 + 2 * len(mods):]
        y = _dot(a_ref[...], w_ref[0])
        y_ref[...] = y
        xv = x_ref[...] + g_ref[...] * y
        xn_ref[...] = xv
        nrm = xv * lax.rsqrt(jnp.mean(xv * xv, axis=-1, keepdims=True) + EPS)
        for t, h_ref in enumerate(h_refs):
            h_ref[...] = (nrm * (1.0 + mod_refs[2 * t + 1][...]) + mod_refs[2 * t][...]).astype(BF16)

    row = pl.BlockSpec((tm, dm), lambda i: (i, 0))
    vec = _full((1, dm))
    outs = pl.pallas_call(
        body, name=name, grid=(s // tm,),
        in_specs=[pl.BlockSpec((tm, k), lambda i: (i, 0)), _full(w.shape), row, vec] + [vec] * (2 * len(mods)),
        out_specs=[row] * (2 + len(mods)),
        out_shape=[jax.ShapeDtypeStruct((s, dm), F32)] * 2 + [jax.ShapeDtypeStruct((s, dm), BF16)] * len(mods),
    )(a, w, x, gate, *[v for m in mods for v in m])
    return outs[0], outs[1], list(outs[2:])


def _mm_loss_head(a, w, x, gate, target, name):
    s, k = a.shape
    dm = x.shape[1]
    tm = _tile(s, ROW_TILE)

    def body(a_ref, w_ref, x_ref, g_ref, t_ref, sq_ref, do_ref, dy_ref, dg_ref):
        @pl.when(pl.program_id(0) == 0)
        def _():
            sq_ref[...] = jnp.zeros_like(sq_ref)
            dg_ref[...] = jnp.zeros_like(dg_ref)

        y, gv = _dot(a_ref[...], w_ref[0]), g_ref[...]
        err = x_ref[...] + gv * y - t_ref[...]
        sq_ref[...] += _colsum8(err * err)
        dout = err * (1.0 / dm)
        do_ref[...] = dout
        dy_ref[0] = (dout * gv).astype(BF16)
        dg_ref[...] += _colsum8(dout * y)

    row = pl.BlockSpec((tm, dm), lambda i: (i, 0))
    acc = _full((SUBLANES, dm))
    return pl.pallas_call(
        body, name=name, grid=(s // tm,),
        in_specs=[pl.BlockSpec((tm, k), lambda i: (i, 0)), _full(w.shape), row, _full((1, dm)), row],
        out_specs=[acc, row, pl.BlockSpec((1, tm, dm), lambda i: (0, i, 0)), acc],
        out_shape=[jax.ShapeDtypeStruct((SUBLANES, dm), F32), jax.ShapeDtypeStruct((s, dm), F32),
                   jax.ShapeDtypeStruct((1, s, dm), BF16), jax.ShapeDtypeStruct((SUBLANES, dm), F32)],
    )(a, w, x, gate, target)


def _ffn_inner(h, w_up, conv_w, conv_b, tag):
    s, dm = h.shape
    half = w_up.shape[2]
    f = 2 * half
    tm = _tile(s, FFN_ROWS)

    def body(h_ref, w_ref, cw_ref, cb_ref, u_ref, c_ref, a_ref, carry):
        @pl.when(pl.program_id(0) == 0)
        def _():
            carry[...] = jnp.zeros_like(carry)

        hv = h_ref[...]
        for j in range(2):
            cols = slice(j * half, (j + 1) * half)
            conv = []
            for g in range(2):
                ub = _dot(hv, w_ref[2 * g + j]).astype(BF16)
                u_ref[g, :, cols] = ub
                uf = ub.astype(F32)
                e = jnp.concatenate([carry[g, j], uf], axis=0)
                carry[g, j] = uf[tm - SUBLANES:tm]
                conv.append(_conv_taps(e, cw_ref[g][:, cols], cb_ref[g][:, cols])[SUBLANES:])
                c_ref[g, :, cols] = conv[g].astype(BF16)
            a_ref[:, cols] = (conv[0] * _sig(conv[0]) * conv[1]).astype(BF16)

    pair = pl.BlockSpec((2, tm, f), lambda i: (0, i, 0))
    return pl.pallas_call(
        body, name=tag + "_up_convglu", grid=(s // tm,),
        in_specs=[pl.BlockSpec((tm, dm), lambda i: (i, 0)), _full(w_up.shape), _full(conv_w.shape), _full(conv_b.shape)],
        out_specs=[pair, pair, pl.BlockSpec((tm, f), lambda i: (i, 0))],
        out_shape=[jax.ShapeDtypeStruct((2, s, f), BF16)] * 2 + [jax.ShapeDtypeStruct((s, f), BF16)],
        scratch_shapes=[pltpu.VMEM((2, 2, SUBLANES, half), F32)],
    )(h, w_up, conv_w, conv_b)


def _weight_grad_first(a, d, p_n, name):
    return lax.optimization_barrier((_mm_tn(a, d, p_n, name), d))


def _ffn_backward(dx_out, dffn, x_mid, scale, saved, w_up, conv_w, conv_b, w_down, mixer, tag):
    h, u, c, a = saved
    dw_down, dffn = _weight_grad_first(a, dffn, 1, tag + "_down_dw")
    du, dconv = _convglu_bwd(u, c, dffn, w_down, conv_w, tag + "_convglu_bwd")
    dw_up, du = _weight_grad_first(h, du, N_CHIPS, tag + "_up_dw")
    dx_mid, [(dshift, dscale)], dy, dgate_mixer = _premix_bwd(x_mid, [(scale, [(du, w_up)])], dx_out,
                                                              tag + "_premix_bwd", branch=mixer)
    return dx_mid, dy, dgate_mixer, dw_up, dw_down, dict(shift=dshift, scale=dscale, conv=dconv)


def _local_step(x, target, mods, lb, vecs, weights_at):
    m0, m1, mk = mods["l0"], mods["l1"], mods["kv"]
    wts, x = weights_at("mixer0", x)
    h0, proj = _premix_proj(x, m0[0], m0[1], wts["a_w_in"], "l0_premix_in")
    o_a, yp, states = _hgrn_fwd(proj, lb, vecs["a_norm_g"], "l0_hgrn")
    more, yp = weights_at("ffn0", yp)
    wts.update(more)
    y0, x1, [hf0] = _mm_residual_premix(yp, wts["a_w_out"], x, m0[2], [(m0[3], m0[4])], "l0_out")
    u0, c0, a0 = _ffn_inner(hf0, wts["up0"], vecs["conv_w0"], vecs["conv_b0"], "l0_ffn")
    saved0 = (hf0, u0, c0, a0)
    ffn0, x2, [hk, h1] = _mm_residual_premix(a0, wts["down0"], x1, m0[5], [(mk[0], mk[1]), (m1[0], m1[1])],
                                             "l0_ffn_down")
    more, hk = weights_at("layer1", hk)
    wts.update(more)
    k_raw, k_sh, v_sh, f_raw = _kv_proj(hk, wts["kv_k"], wts["kv_v"], wts["kv_f"], vecs["k_norm_g"], "kv_proj")
    qa, ka = _fcum_fwd(f_raw, vecs["kv_b_f"], "kv_fcum")
    q_scale = HEAD_DIM ** -0.5
    qo, q = _proj_headnorm(h1, wts["b_w_q"], vecs["q_norm_g"], q_scale * LOG2E, "l1_q")
    o_b, og, qab = _fox_fwd(q, qa, k_sh, ka, v_sh, qo, "l1_fox")
    y1, x3, [hf1] = _mm_residual_premix(og, wts["b_w_out"], x2, m1[2], [(m1[3], m1[4])], "l1_out")
    u1, c1, a1 = _ffn_inner(hf1, wts["up1"], vecs["conv_w1"], vecs["conv_b1"], "l1_ffn")
    saved1 = (hf1, u1, c1, a1)
    sq, dx4, dffn1, dg2_1 = _mm_loss_head(a1, wts["down1"], x3, m1[5], target, "l1_ffn_down")

    big, small = {}, {}
    dx3, dy1, dg1_1, big["up1"], big["down1"], s_ffn1 = _ffn_backward(
        dx4, dffn1, x3, m1[4], saved1, wts["up1"], vecs["conv_w1"], vecs["conv_b1"], wts["down1"], (y1, m1[2]), "l1_ffn")
    big["b_w_out"], dy1 = _weight_grad_first(og, dy1, 1, "l1_out_dw")
    do_b, dgate_b, doa = _fox_gate_bwd(o_b, qo, dy1, wts["b_w_out"], "l1_out_dx_gate_bwd")
    dk, dv, dka, dq, dcq = _fox_bwd(q, qab, k_sh, ka, v_sh, do_b, doa, "l1_fox_bwd")
    dqo, dqg = _headnorm_bwd(qo, vecs["q_norm_g"], q_scale, dq, "l1_qnorm_bwd", extra=dgate_b)
    big["b_w_q"], dqo = _weight_grad_first(h1, dqo, N_CHIPS, "l1_q_dw")
    dk_raw, dkg = _headnorm_bwd(k_raw, vecs["k_norm_g"], 1.0, dk, "kv_knorm_bwd")
    dz, dbf = _fcum_bwd(f_raw, vecs["kv_b_f"], dka, dcq.reshape(HEADS, 1, -1), "kv_fcum_bwd")
    big["kv_k"], dk_raw = _weight_grad_first(hk, dk_raw, 1, "kv_k_dw")
    big["kv_v"], dv = _weight_grad_first(hk, dv, 1, "kv_v_dw")
    big["kv_f"], dz = _weight_grad_first(hk, dz, 1, "kv_f_dw")
    kv_pairs = [(dk_raw, wts["kv_k"]), (dv, wts["kv_v"]), (dz, wts["kv_f"])]
    dx2, [(dsh1_1, dsc1_1), (dshk, dsck)], dffn0, dg2_0 = _premix_bwd(
        x2, [(m1[1], [(dqo, wts["b_w_q"])]), (mk[1], kv_pairs)], dx3, "l1_kv_premix_bwd", branch=(ffn0, m0[5]))
    dx1, dy0, dg1_0, big["up0"], big["down0"], s_ffn0 = _ffn_backward(
        dx2, dffn0, x1, m0[4], saved0, wts["up0"], vecs["conv_w0"], vecs["conv_b0"], wts["down0"], (y0, m0[2]), "l0_ffn")
    big["a_w_out"], dy0 = _weight_grad_first(yp, dy0, 1, "l0_out_dw")
    dproj, dlb, dng = _hgrn_bwd(proj, lb, vecs["a_norm_g"], o_a, states, dy0, wts["a_w_out"], "l0_out_dx_hgrn_bwd")
    grad_x, [(dsh1_0, dsc1_0)] = _premix_bwd(x, [(m0[1], [(dproj, wts["a_w_in"])])], dx1, "l0_premix_bwd")
    dproj, _ = lax.optimization_barrier((dproj, (dsh1_0, dsc1_0)))
    big["a_w_in"] = _mm_tn(h0, dproj, N_CHIPS, "l0_in_dw")

    small["mod_l0"] = [dsh1_0, dsc1_0, dg1_0, s_ffn0["shift"], s_ffn0["scale"], dg2_0]
    small["mod_l1"] = [dsh1_1, dsc1_1, dg1_1, s_ffn1["shift"], s_ffn1["scale"], dg2_1]
    small["mod_kv"] = [dshk, dsck]
    small["conv0"], small["conv1"] = s_ffn0["conv"], s_ffn1["conv"]
    small["a_norm_g"], small["k_norm_g"], small["q_norm_g"] = dng, dkg, dqg
    small["kv_b_f"], small["lb"] = dbf, dlb
    marks = {"attention_bwd": dk, "ffn0_bwd": dx1, "mixer0_bwd": grad_x}
    return sq, grad_x, big, small, marks


HBM = pl.BlockSpec(memory_space=pltpu.HBM)
COMM_CHUNK_ELEMS = 256 * 1024


def _place():
    x, y, c = lax.axis_index("x"), lax.axis_index("y"), lax.axis_index("c")
    chips = [(1 - x, y), (x, 1 - y), (1 - x, 1 - y)]
    return x, y, c, (x, y, 1 - c), chips


def _chunk_rows(rows, cols):
    best = BF16_ROWS
    for r in range(BF16_ROWS, rows + 1, BF16_ROWS):
        if rows % r == 0 and r * cols <= COMM_CHUNK_ELEMS:
            best = r
    assert rows % best == 0, (rows, cols)
    return best


def _allgather8(block, name):
    m_per, n = block.shape

    def body(x_ref, out_ref, send_sems, recv_sems, local_sem):
        x, y, c, sibling, chips = _place()
        me = (x, y, c)

        def rows(px, py, pc):
            return out_ref.at[pl.ds((4 * px + 2 * py + pc) * m_per, m_per), :]

        def copy(k, blk, to, src=None):
            return pltpu.make_async_remote_copy(
                src_ref=rows(*blk) if src is None else src, dst_ref=rows(*blk),
                send_sem=send_sems.at[k], recv_sem=recv_sems.at[k], device_id=to, device_id_type=MESH)

        mine = pltpu.make_async_copy(x_ref, rows(*me), local_sem)
        mine.start()
        first = [copy(0, me, sibling, src=x_ref)]
        first += [copy(1 + j, me, (*chip, c), src=x_ref) for j, chip in enumerate(chips)]
        for cp in first:
            cp.start()
        passed = [copy(4 + j, (*chip, c), sibling) for j, chip in enumerate(chips)]
        for j, chip in enumerate(chips):
            copy(1 + j, (*chip, c), me).wait_recv()
            passed[j].start()
        copy(0, sibling, me).wait_recv()
        for j, chip in enumerate(chips):
            copy(4 + j, (*chip, 1 - c), me).wait_recv()
        for cp in first + passed:
            cp.wait_send()
        mine.wait()

    return pl.pallas_call(
        body, name=name, out_shape=jax.ShapeDtypeStruct((N_DEV * m_per, n), block.dtype),
        in_specs=[pl.BlockSpec(memory_space=pltpu.VMEM)], out_specs=pl.BlockSpec(memory_space=pltpu.VMEM),
        scratch_shapes=[pltpu.SemaphoreType.DMA((7,)), pltpu.SemaphoreType.DMA((7,)), pltpu.SemaphoreType.DMA],
    )(block)


def _cast_own_block(shards, layer, chip, name):
    _, r, cols = shards.shape
    rows = _chunk_rows(r, cols)

    def body(chip_ref, w_ref, o_ref):
        o_ref[...] = w_ref[...].astype(BF16)

    return pl.pallas_call(
        body, name=name,
        grid_spec=pltpu.PrefetchScalarGridSpec(
            num_scalar_prefetch=1, grid=(r // rows,),
            in_specs=[pl.BlockSpec((None, rows, cols), lambda i, chip_ref: (layer, i, 0))],
            out_specs=pl.BlockSpec((None, rows, cols), lambda i, chip_ref: (chip_ref[0], i, 0))),
        out_shape=jax.ShapeDtypeStruct((N_CHIPS, r, cols), BF16),
    )(chip, shards)


def _sequencer_gather(bufs, name, collective_id):
    n_t = len(bufs)
    dims = [b.shape[1:] for b in bufs]
    refs = [jax.new_ref(b, memory_space=pltpu.MemorySpace.HBM) for b in bufs]

    @pl.kernel(mesh=plsc.ScalarSubcoreMesh(axis_name="sequencer", num_cores=1), name=name,
               scratch_types=[pltpu.SemaphoreType.DMA((n_t,))] * 4,
               compiler_params=pltpu.CompilerParams(collective_id=collective_id))
    def launch(send_ici, recv_ici, send_d2d, recv_d2d):
        x, y, c, sibling, chips = _place()
        p_me = 2 * x + y
        peers = [sibling] + [(cx, cy, c) for cx, cy in chips]
        barrier = pltpu.get_barrier_semaphore()
        for peer in peers:
            pl.semaphore_signal(barrier, inc=1, device_id=peer, device_id_type=MESH)
        pl.semaphore_wait(barrier, len(peers))

        def waiter(t, sem_s, sem_r):
            win = refs[t].at[pl.ds(0, 3), pl.ds(0, dims[t][0] // 2), :]
            return pltpu.make_async_remote_copy(src_ref=win, dst_ref=win, send_sem=sem_s.at[t], recv_sem=sem_r.at[t],
                                                device_id=sibling, device_id_type=MESH)

        def half_copy(t, chip_idx, to, sem_s, sem_r):
            r2 = dims[t][0] // 2
            win = refs[t].at[chip_idx, pl.ds(c * r2, r2), :]
            return pltpu.make_async_remote_copy(src_ref=win, dst_ref=win, send_sem=sem_s.at[t], recv_sem=sem_r.at[t],
                                                device_id=to, device_id_type=MESH)

        for t in range(n_t):
            for cx, cy in chips:
                half_copy(t, p_me, (cx, cy, c), send_ici, recv_ici).start()
        for t in range(n_t):
            waiter(t, send_ici, recv_ici).wait_recv()
            for cx, cy in chips:
                half_copy(t, 2 * cx + cy, sibling, send_d2d, recv_d2d).start()
        for t in range(n_t):
            waiter(t, send_d2d, recv_d2d).wait_recv()
            waiter(t, send_ici, recv_ici).wait_send()
            waiter(t, send_d2d, recv_d2d).wait_send()

    launch()
    return [r[...] for r in refs]


def _sequencer_allgather8(block, dev, name, collective_id):
    m_per, n = block.shape
    src = jax.new_ref(block, memory_space=pltpu.MemorySpace.HBM)
    out = jax.empty_ref(jax.ShapeDtypeStruct((N_DEV * m_per, n), block.dtype), memory_space=pltpu.MemorySpace.HBM)

    @pl.kernel(mesh=plsc.ScalarSubcoreMesh(axis_name="sequencer", num_cores=1), name=name,
               scratch_types=[pltpu.SemaphoreType.DMA((7,))] * 2,
               compiler_params=pltpu.CompilerParams(collective_id=collective_id))
    def launch(send_sems, recv_sems):
        x, y, c, sibling, chips = _place()
        me = (x, y, c)
        _handshake([sibling] + [(cx, cy, c) for cx, cy in chips])

        def rows(px, py, pc):
            return out.at[pl.ds((4 * px + 2 * py + pc) * m_per, m_per), :]

        def copy(k, blk, to, from_src=False):
            return pltpu.make_async_remote_copy(
                src_ref=src if from_src else rows(*blk), dst_ref=rows(*blk),
                send_sem=send_sems.at[k], recv_sem=recv_sems.at[k], device_id=to, device_id_type=MESH)

        first = [copy(0, me, sibling, True)] + [copy(1 + j, me, (*chip, c), True) for j, chip in enumerate(chips)]
        for cp in first:
            cp.start()
        passed = [copy(4 + j, (*chip, c), sibling) for j, chip in enumerate(chips)]
        for j, chip in enumerate(chips):
            copy(1 + j, (*chip, c), me).wait_recv()
            passed[j].start()
        copy(0, sibling, me).wait_recv()
        for j, chip in enumerate(chips):
            copy(4 + j, (*chip, 1 - c), me).wait_recv()
        for cp in first + passed:
            cp.wait_send()

    launch()
    return lax.dynamic_update_slice(out[...], block, (dev * m_per, 0))


def _others():
    x, y, c = lax.axis_index("x"), lax.axis_index("y"), lax.axis_index("c")
    flip = lambda v, f: 1 - v if f else v
    return [(flip(x, fx), flip(y, fy), flip(c, fc))
            for fx in (0, 1) for fy in (0, 1) for fc in (0, 1) if (fx, fy, fc) != (0, 0, 0)]


def _handshake(peers):
    barrier = pltpu.get_barrier_semaphore()
    for peer in peers:
        pl.semaphore_signal(barrier, inc=1, device_id=peer, device_id_type=MESH)
    pl.semaphore_wait(barrier, len(peers))


def _sequencer_scatter(parts, name, collective_id):
    n_t = len(parts)
    dims = [p.shape[1:] for p in parts]
    srcs = [jax.new_ref(p, memory_space=pltpu.MemorySpace.HBM) for p in parts]
    inboxes = [jax.empty_ref(jax.ShapeDtypeStruct((N_DEV, r // 2, cols), BF16), memory_space=pltpu.MemorySpace.HBM)
               for r, cols in dims]

    @pl.kernel(mesh=plsc.ScalarSubcoreMesh(axis_name="sequencer", num_cores=1), name=name,
               scratch_types=[pltpu.SemaphoreType.DMA((n_t,))] * 2,
               compiler_params=pltpu.CompilerParams(collective_id=collective_id))
    def launch(send_sem, recv_sem):
        x, y, c = lax.axis_index("x"), lax.axis_index("y"), lax.axis_index("c")
        me = 4 * x + 2 * y + c
        peers = _others()
        _handshake(peers)
        for t in range(n_t):
            h = dims[t][0] // 2
            for qx, qy, qc in peers:
                pltpu.make_async_remote_copy(
                    src_ref=srcs[t].at[2 * qx + qy, pl.ds(qc * h, h), :], dst_ref=inboxes[t].at[me],
                    send_sem=send_sem.at[t], recv_sem=recv_sem.at[t], device_id=(qx, qy, qc), device_id_type=MESH).start()
        for t in range(n_t):
            win = inboxes[t].at[pl.ds(0, N_DEV - 1)]
            both = pltpu.make_async_remote_copy(src_ref=win, dst_ref=win, send_sem=send_sem.at[t],
                                                recv_sem=recv_sem.at[t], device_id=peers[0], device_id_type=MESH)
            both.wait_recv()
            both.wait_send()

    launch()
    return [b[...] for b in inboxes]


def _sum_pieces(part, inbox, place, name):
    _, r, cols = part.shape
    h = r // 2
    rows = _chunk_rows(h, cols)
    steps = h // rows

    def body(place_ref, own_ref, in_ref, o_ref):
        dev = place_ref[2]
        own = own_ref[...].astype(F32)
        acc = jnp.zeros((rows, cols), F32)
        for d in range(N_DEV):
            acc = acc + jnp.where(dev == d, own, in_ref[d].astype(F32))
        o_ref[...] = acc

    return pl.pallas_call(
        body, name=name,
        grid_spec=pltpu.PrefetchScalarGridSpec(
            num_scalar_prefetch=1, grid=(steps,),
            in_specs=[pl.BlockSpec((None, rows, cols), lambda i, pr: (pr[0], pr[1] * steps + i, 0)),
                      pl.BlockSpec((N_DEV, rows, cols), lambda i, pr: (0, i, 0))],
            out_specs=pl.BlockSpec((rows, cols), lambda i, pr: (pr[1] * steps + i, 0))),
        out_shape=jax.ShapeDtypeStruct((r, cols), F32),
    )(place, part, inbox)


def _sequencer_swap_halves(halves, name, collective_id):
    n_t = len(halves)
    refs = [jax.new_ref(a, memory_space=pltpu.MemorySpace.HBM) for a in halves]

    @pl.kernel(mesh=plsc.ScalarSubcoreMesh(axis_name="sequencer", num_cores=1), name=name,
               scratch_types=[pltpu.SemaphoreType.DMA((n_t,))] * 2,
               compiler_params=pltpu.CompilerParams(collective_id=collective_id))
    def launch(send_sem, recv_sem):
        x, y, c = lax.axis_index("x"), lax.axis_index("y"), lax.axis_index("c")
        sibling = (x, y, 1 - c)
        _handshake([sibling])
        copies = []
        for t in range(n_t):
            h = halves[t].shape[0] // 2
            win = refs[t].at[pl.ds(c * h, h), :]
            copies.append(pltpu.make_async_remote_copy(src_ref=win, dst_ref=win, send_sem=send_sem.at[t],
                                                       recv_sem=recv_sem.at[t], device_id=sibling, device_id_type=MESH))
            copies[-1].start()
        for cp in copies:
            cp.wait()

    launch()
    return [r[...] for r in refs]


def _cond_rows(c16, w, act, name):
    n_l, dm, wid = w.shape

    def body(c_ref, w_ref, o_ref, a_ref):
        cv = c_ref[...]
        if act:
            cv = cv * _sig(cv)
        a_ref[...] = cv
        o_ref[...] = _dot_f32(cv, w_ref[...])

    return pl.pallas_call(
        body, name=name, grid=(n_l,),
        in_specs=[_full((16, dm)), pl.BlockSpec((None, dm, wid), lambda l: (l, 0, 0))],
        out_specs=[pl.BlockSpec((None, 16, wid), lambda l: (l, 0, 0)), _full((16, dm))],
        out_shape=[jax.ShapeDtypeStruct((n_l, 16, wid), F32), jax.ShapeDtypeStruct((16, dm), F32)],
    )(c16, w)


def _outer_grad(ct, dm, name):
    n_l, kk, wid = dm.shape
    d_rows = ct.shape[0]

    def body(c_ref, d_ref, o_ref):
        o_ref[...] = _dot_f32(c_ref[...], d_ref[...])

    return pl.pallas_call(
        body, name=name, grid=(n_l,),
        in_specs=[_full((d_rows, kk)), pl.BlockSpec((None, kk, wid), lambda l: (l, 0, 0))],
        out_specs=pl.BlockSpec((None, d_rows, wid), lambda l: (l, 0, 0)),
        out_shape=jax.ShapeDtypeStruct((n_l, d_rows, wid), F32),
    )(ct, dm)


def _sum_devices(g, name):
    rows, n = g.shape

    def body(g_ref, o_ref):
        acc = g_ref[0:SUBLANES, :]
        for dev in range(1, N_DEV):
            acc = acc + g_ref[dev * SUBLANES:(dev + 1) * SUBLANES, :]
        o_ref[...] = acc

    return pl.pallas_call(body, name=name, out_shape=jax.ShapeDtypeStruct((SUBLANES, n), F32))(g)


def _adamw(w, g, m, v, name):
    shape = w.shape
    cols = shape[-1]
    rows = w.size // cols
    tr = rows
    for cand in range(SUBLANES, min(rows, 256) + 1, SUBLANES):
        if rows % cand == 0:
            tr = cand
    if rows * cols <= COMM_CHUNK_ELEMS:
        tr = rows
    c1 = 1.0 / (1.0 - ADAM_B1 ** ADAM_STEP)
    c2 = 1.0 / (1.0 - ADAM_B2 ** ADAM_STEP)

    def body(w_ref, g_ref, m_ref, v_ref, d_ref, mo_ref, vo_ref):
        gv = g_ref[...]
        m_new = ADAM_B1 * m_ref[...] + (1.0 - ADAM_B1) * gv
        v_new = ADAM_B2 * v_ref[...] + (1.0 - ADAM_B2) * (gv * gv)
        mo_ref[...] = m_new
        vo_ref[...] = v_new
        d_ref[...] = -ADAM_LR * ((m_new * c1) / (jnp.sqrt(v_new * c2) + ADAM_EPS) + ADAM_WD * w_ref[...])

    spec = pl.BlockSpec((tr, cols), lambda i: (i, 0))
    outs = pl.pallas_call(
        body, name=name, grid=(rows // tr,), in_specs=[spec] * 4, out_specs=[spec] * 3,
        out_shape=[jax.ShapeDtypeStruct((rows, cols), F32)] * 3,
    )(*[a.reshape(rows, cols) for a in (w, g, m, v)])
    return tuple(o.reshape(shape) for o in outs)


def _pad_cols(a, cols):
    return jnp.pad(a, [(0, 0)] * (a.ndim - 1) + [(0, cols - a.shape[-1])])


def _flat8(parts, width):
    v = jnp.concatenate([p.reshape(-1) for p in parts])
    return jnp.pad(v, (0, width - v.shape[0])).reshape(SUBLANES, width // SUBLANES)


KV_SHARD = 514
KV_SHARD_PAD = 640
BIG = ("a_w_in", "a_w_out", "kv_w", "b_w_q", "b_w_out", "up0", "up1", "down0", "down1")


def kernel(x, c, ada_w, ada_b, a_w_in, a_lb_logits, a_norm_g, a_w_out, kv_ada_w, kv_ada_b, kv_w, kv_b_f, k_norm_g, b_w_q, q_norm_g, b_w_out, ffn_w_up, ffn_conv_w, ffn_conv_b, ffn_w_down, loss_target, m_ada_w, m_ada_b, m_a_w_in, m_a_lb_logits, m_a_norm_g, m_a_w_out, m_kv_ada_w, m_kv_ada_b, m_kv_w, m_kv_b_f, m_k_norm_g, m_b_w_q, m_q_norm_g, m_b_w_out, m_ffn_w_up, m_ffn_conv_w, m_ffn_conv_b, m_ffn_w_down, v_ada_w, v_ada_b, v_a_w_in, v_a_lb_logits, v_a_norm_g, v_a_w_out, v_kv_ada_w, v_kv_ada_b, v_kv_w, v_kv_b_f, v_k_norm_g, v_b_w_q, v_q_norm_g, v_b_w_out, v_ffn_w_up, v_ffn_conv_w, v_ffn_conv_b, v_ffn_w_down):
    dm, ff = D_MODEL, D_FF
    ix, iy, ic = lax.axis_index("x"), lax.axis_index("y"), lax.axis_index("c")
    chip = 2 * ix + iy
    dev = 2 * chip + ic

    w1 = 10240
    g1 = _allgather8(_flat8([c, a_lb_logits, ffn_conv_w], w1), "gather_cond").reshape(N_DEV, w1)
    c_all = g1[:, :dm]
    per_chip = g1[0::2]
    lb_logits = per_chip[:, dm:dm + 512].reshape(N_CHIPS, 2, 256).transpose(1, 0, 2).reshape(2, dm)
    conv_w = per_chip[:, dm + 512:dm + 512 + 2 * CONV_W * FFN_COLS].reshape(N_CHIPS, 2, CONV_W, FFN_COLS)
    conv_w = conv_w.transpose(1, 2, 0, 3).reshape(2, CONV_W, 2, ff).transpose(0, 2, 1, 3)
    conv_b = ffn_conv_b.reshape(2, 2, 1, ff)
    lb = jax.nn.softmax(lb_logits, axis=0)[0:1]

    c16 = jnp.pad(c_all, ((0, 8), (0, 0)))
    mod_ada, c_act16 = _cond_rows(c16, ada_w, True, "mod_ada")
    mod_kv, _ = _cond_rows(c16, kv_ada_w[None], True, "mod_kv")
    mine = jnp.concatenate([mod_ada[0, :8], mod_ada[1, :8], mod_kv[0, :8]], axis=1)
    w2 = mine.shape[1]
    g2 = _allgather8(mine, "gather_mod").reshape(N_DEV, 8, w2)[0::2]
    my_rows = lax.dynamic_index_in_dim(g2, dev, axis=1, keepdims=False)
    mod0 = my_rows[:, 0:1536].reshape(6 * dm) + ada_b[0]
    mod1 = my_rows[:, 1536:3072].reshape(6 * dm) + ada_b[1]
    modk = my_rows[:, 3072:3584].reshape(2 * dm) + kv_ada_b
    mods = {"l0": [v.reshape(1, dm) for v in jnp.split(mod0, 6)],
            "l1": [v.reshape(1, dm) for v in jnp.split(mod1, 6)],
            "kv": [v.reshape(1, dm) for v in jnp.split(modk, 2)]}

    local = [(a_w_in, 0), (a_w_out, 0), (_pad_cols(kv_w, KV_SHARD_PAD)[None], 0), (b_w_q, 0), (b_w_out, 0),
             (ffn_w_up, 0), (ffn_w_up, 1), (ffn_w_down, 0), (ffn_w_down, 1)]
    chip_arr = chip.reshape(1).astype(jnp.int32)
    own = {n: _cast_own_block(w, layer, chip_arr, "cast_" + n) for n, (w, layer) in zip(BIG, local)}
    stages = {"mixer0": ("a_w_in",), "ffn0": ("a_w_out", "up0", "down0"),
              "layer1": ("kv_w", "b_w_q", "b_w_out", "up1", "down1")}
    arriving = {st: _sequencer_gather([own[n] for n in names], "gather_" + st, cid)
                for cid, (st, names) in enumerate(stages.items(), start=1)}
    rowwise = lambda g: g.reshape(1, -1, dm)

    def weights_at(stage, token):
        got, token = lax.optimization_barrier((arriving[stage], token))
        g = dict(zip(stages[stage], got))
        if stage == "mixer0":
            return {"a_w_in": g["a_w_in"]}, token
        if stage == "ffn0":
            return {"a_w_out": rowwise(g["a_w_out"]), "up0": g["up0"], "down0": rowwise(g["down0"])}, token
        kv_full = g["kv_w"][:, :, :KV_SHARD].transpose(1, 0, 2).reshape(dm, N_CHIPS * KV_SHARD)
        return {"kv_k": kv_full[None, :, :dm], "kv_v": kv_full[None, :, dm:2 * dm],
                "kv_f": _pad_cols(kv_full[None, :, 2 * dm:], LANES), "b_w_q": g["b_w_q"],
                "b_w_out": rowwise(g["b_w_out"]), "up1": g["up1"], "down1": rowwise(g["down1"])}, token

    vecs = {"a_norm_g": jnp.tile(a_norm_g, (1, HEADS)), "k_norm_g": jnp.tile(k_norm_g[None], (1, HEADS)),
            "q_norm_g": jnp.tile(q_norm_g, (1, HEADS)), "kv_b_f": _pad_cols(kv_b_f[None], LANES),
            "conv_w0": conv_w[0], "conv_b0": conv_b[0], "conv_w1": conv_w[1], "conv_b1": conv_b[1]}

    sq, grad_x, big, small, marks = _local_step(x[0], loss_target[0], mods, lb, vecs, weights_at)
    loss = lax.psum(0.5 * jnp.sum(sq) / dm, ("x", "y", "c"))

    kv_grad = jnp.concatenate([big["kv_k"][0], big["kv_v"][0], big["kv_f"][0][:, :HEADS]], axis=1)
    kv_grad = _pad_cols(kv_grad.reshape(dm, N_CHIPS, KV_SHARD).transpose(1, 0, 2), KV_SHARD_PAD)
    chipwise = lambda g: g.reshape(N_CHIPS, -1, dm)
    parts = dict(zip(BIG, [big["a_w_in"], chipwise(big["a_w_out"]), kv_grad, big["b_w_q"], chipwise(big["b_w_out"]),
                           big["up0"], big["up1"], chipwise(big["down0"]), chipwise(big["down1"])]))
    place = jnp.stack([chip, ic, dev]).astype(jnp.int32)

    served = []
    boxes = {}

    groups = (("up1", "down1"), ("b_w_out", "b_w_q", "kv_w"), ("up0", "down0", "a_w_out"), ("a_w_in",))

    def scatter_group(k):
        mine = [parts[n] for n in groups[k]]
        if served:
            mine, _ = lax.optimization_barrier((mine, served[-1]))
        boxes[k] = _sequencer_scatter(mine, "scatter_grads_%d" % k, 4 + k)
        served.append(boxes[k])

    def sum_group(k, token):
        inboxes, _ = lax.optimization_barrier((boxes[k], token))
        return [_sum_pieces(parts[n], box, place, "sum_" + n) for n, box in zip(groups[k], inboxes)]

    def swap_group(k, halves, behind):
        halves, _ = lax.optimization_barrier((halves, behind))
        return dict(zip(groups[k], _sequencer_swap_halves(halves, "swap_grads_%d" % k, 8 + k)))

    for k in range(3):
        scatter_group(k)
    halves = [sum_group(0, marks["attention_bwd"]), sum_group(1, marks["ffn0_bwd"]), sum_group(2, marks["mixer0_bwd"])]

    fold = lambda a: a.sum(axis=0)
    heads = lambda a: fold(a).reshape(HEADS, HEAD_DIM).sum(axis=0)
    conv_flat = lambda a: a.sum(axis=2).transpose(1, 0, 2)
    pieces = ([fold(a) for a in small["mod_l0"]] + [fold(a) for a in small["mod_l1"]] + [fold(a) for a in small["mod_kv"]]
              + [conv_flat(small["conv0"]), conv_flat(small["conv1"]), heads(small["a_norm_g"]), heads(small["k_norm_g"]),
                 heads(small["q_norm_g"]), fold(small["kv_b_f"]), fold(small["lb"])])
    w3 = 61440
    small_vec, _ = lax.optimization_barrier((_flat8(pieces, w3), served[2]))
    g3 = _sequencer_allgather8(small_vec, dev, "gather_small", 12)
    served.append(g3)
    scatter_group(3)
    rs = {}
    for k in range(3):
        rs.update(swap_group(k, halves[k], g3))
    tot = _sum_devices(g3, "sum_small").reshape(w3)
    n_mod = 14 * dm
    dmod_all = g3.reshape(N_DEV, w3)[:, :n_mod]
    o = n_mod
    conv_tot = [tot[o + l * 8 * ff: o + (l + 1) * 8 * ff].reshape(4, 2 * ff) for l in range(2)]
    o += 16 * ff
    g_a_norm, g_k_norm, g_q_norm = (tot[o + i * HEAD_DIM: o + (i + 1) * HEAD_DIM] for i in range(3))
    o += 3 * HEAD_DIM
    g_kv_b_f = tot[o:o + HEADS]
    dlb = tot[o + LANES:o + LANES + dm]

    ct = _pad_cols(c_act16[:8].T, LANES)
    dmod_pad = jnp.pad(dmod_all, ((0, LANES - N_DEV), (0, 0)))
    cols_ada = jnp.stack([lax.dynamic_slice_in_dim(dmod_pad, l * 6 * dm + chip * 1536, 1536, axis=1) for l in range(2)])
    cols_kv = lax.dynamic_slice_in_dim(dmod_pad, 12 * dm + chip * 512, 512, axis=1)[None]
    g_ada_w = _outer_grad(ct, cols_ada, "grad_ada_w")
    g_kv_ada_w = _outer_grad(ct, cols_kv, "grad_kv_ada_w")[0]

    my_lb = lax.dynamic_slice_in_dim(lb[0], chip * 256, 256)
    l0 = lax.dynamic_slice_in_dim(dlb, chip * 256, 256) * my_lb * (1.0 - my_lb)
    grads = {
        "ada_w": g_ada_w, "ada_b": jnp.stack([tot[:6 * dm], tot[6 * dm:12 * dm]]),
        "a_lb_logits": jnp.stack([l0, -l0]), "a_norm_g": g_a_norm[None],
        "a_w_out": rs["a_w_out"][None], "kv_ada_w": g_kv_ada_w, "kv_ada_b": tot[12 * dm:14 * dm],
        "kv_w": rs["kv_w"][:, :KV_SHARD], "kv_b_f": g_kv_b_f, "k_norm_g": g_k_norm,
        "b_w_q": rs["b_w_q"][None], "q_norm_g": g_q_norm[None], "b_w_out": rs["b_w_out"][None],
        "ffn_w_up": jnp.stack([rs["up0"], rs["up1"]]),
        "ffn_conv_w": jnp.stack([lax.dynamic_slice_in_dim(ct_l[:CONV_W], chip * FFN_COLS, FFN_COLS, axis=1) for ct_l in conv_tot]),
        "ffn_conv_b": jnp.stack([ct_l[CONV_W] for ct_l in conv_tot]),
        "ffn_w_down": jnp.stack([rs["down0"], rs["down1"]]),
    }
    weights = dict(ada_w=ada_w, ada_b=ada_b, a_w_in=a_w_in, a_lb_logits=a_lb_logits, a_norm_g=a_norm_g, a_w_out=a_w_out,
                   kv_ada_w=kv_ada_w, kv_ada_b=kv_ada_b, kv_w=kv_w, kv_b_f=kv_b_f, k_norm_g=k_norm_g, b_w_q=b_w_q,
                   q_norm_g=q_norm_g, b_w_out=b_w_out, ffn_w_up=ffn_w_up, ffn_conv_w=ffn_conv_w, ffn_conv_b=ffn_conv_b,
                   ffn_w_down=ffn_w_down)
    m_in = dict(ada_w=m_ada_w, ada_b=m_ada_b, a_w_in=m_a_w_in, a_lb_logits=m_a_lb_logits, a_norm_g=m_a_norm_g,
                a_w_out=m_a_w_out, kv_ada_w=m_kv_ada_w, kv_ada_b=m_kv_ada_b, kv_w=m_kv_w, kv_b_f=m_kv_b_f,
                k_norm_g=m_k_norm_g, b_w_q=m_b_w_q, q_norm_g=m_q_norm_g, b_w_out=m_b_w_out, ffn_w_up=m_ffn_w_up,
                ffn_conv_w=m_ffn_conv_w, ffn_conv_b=m_ffn_conv_b, ffn_w_down=m_ffn_w_down)
    v_in = dict(ada_w=v_ada_w, ada_b=v_ada_b, a_w_in=v_a_w_in, a_lb_logits=v_a_lb_logits, a_norm_g=v_a_norm_g,
                a_w_out=v_a_w_out, kv_ada_w=v_kv_ada_w, kv_ada_b=v_kv_ada_b, kv_w=v_kv_w, kv_b_f=v_kv_b_f,
                k_norm_g=v_k_norm_g, b_w_q=v_b_w_q, q_norm_g=v_q_norm_g, b_w_out=v_b_w_out, ffn_w_up=v_ffn_w_up,
                ffn_conv_w=v_ffn_conv_w, ffn_conv_b=v_ffn_conv_b, ffn_w_down=v_ffn_w_down)

    names = list(weights)
    step = lambda n: _adamw(weights[n], grads[n], m_in[n], v_in[n], "adamw_" + n)
    grads = {n: g.reshape(weights[n].shape) for n, g in grads.items()}
    upd = {n: step(n) for n in names if n != "a_w_in"}
    last = sum_group(3, [u[0] for u in upd.values()])
    grads["a_w_in"] = swap_group(3, last, last)["a_w_in"][None]
    upd["a_w_in"] = step("a_w_in")
    return (loss, grad_x[None], *[grads[n] for n in names], *[upd[n][0] for n in names],
            *[upd[n][1] for n in names], *[upd[n][2] for n in names])
```

```python
import jax
import jax.numpy as jnp
from jax import lax
from jax.experimental import pallas as pl
from jax.experimental.pallas import tpu as pltpu
from jax.experimental.pallas import tpu_sc as plsc

F32 = jnp.float32
BF16 = jnp.bfloat16

D_MODEL = 1024
HEADS = 8
HEAD_DIM = 128
A_CHUNK = 64
D_FF = 2816
CONV_W = 3
EPS = 1e-6
NEG_INF = -1e30
N_CHIPS = 4
N_DEV = 8

ADAM_LR = 0.001
ADAM_B1 = 0.9
ADAM_B2 = 0.999
ADAM_EPS = 1e-08
ADAM_WD = 0.01
ADAM_STEP = 10

SUBLANES = 8
BF16_ROWS = 16
LANES = 128
HALO = BF16_ROWS
ROW_TILE = 512
TOKEN_TILE_TN = 2048
FFN_COLS = 1408
FFN_ROWS = 256
HGRN_ROWS = 256
ATT_TILE = 512
ATT_SPLIT = 2
ATT_FWD_HEADS = 8
ATT_BWD_HEADS = 8
MESH = pl.DeviceIdType.MESH


def _sig(x):
    return jax.nn.sigmoid(x)


def _dot(a, b):
    return jnp.dot(a, b, preferred_element_type=F32)


def _dot_nt(a, b):
    return lax.dot_general(a, b, (((1,), (1,)), ((), ())), preferred_element_type=F32)


def _dot_tn(a, b):
    return lax.dot_general(a, b, (((0,), (0,)), ((), ())), preferred_element_type=F32)


def _split2(x):
    hi = x.astype(BF16)
    lo = (x - hi.astype(F32)).astype(BF16)
    return hi, lo


def _dot_f32(a, b):
    ah, al = _split2(a)
    bh, bl = _split2(b)
    return _dot(ah, bh) + _dot(ah, bl) + _dot(al, bh)


def _tri_dot(tri, x):
    hi = x.astype(BF16)
    r = x - hi.astype(F32)
    mid = r.astype(BF16)
    lo = (r - mid.astype(F32)).astype(BF16)
    return _dot(tri, hi) + _dot(tri, mid) + _dot(tri, lo)


def _tri(n, upper=False):
    r = lax.broadcasted_iota(jnp.int32, (n, n), 0)
    c = lax.broadcasted_iota(jnp.int32, (n, n), 1)
    keep = (c >= r) if upper else (c <= r)
    return jnp.where(keep, 1.0, 0.0).astype(BF16)


def _colsum8(v):
    rows, n = v.shape
    return v.reshape(rows // SUBLANES, SUBLANES, n).sum(axis=0)


def _full(shape):
    nd = len(shape)
    return pl.BlockSpec(shape, lambda *_: (0,) * nd)


def _tile(n, want):
    t = min(n, want)
    assert n % t == 0, (n, t)
    return t


def _mm_tn(a, d, p_n, name):
    m_rows, k = a.shape
    g_n, _, w_cols = d.shape
    per = p_n // g_n
    n = w_cols // per
    tm = _tile(m_rows, TOKEN_TILE_TN if k <= D_MODEL else ROW_TILE)
    steps = m_rows // tm

    def body(a_ref, d_ref, o_ref, acc):
        m = pl.program_id(1)

        @pl.when(m == 0)
        def _():
            acc[...] = jnp.zeros_like(acc)

        acc[...] += _dot_tn(a_ref[...], d_ref[...])

        @pl.when(m == steps - 1)
        def _():
            o_ref[...] = acc[...].astype(BF16)

    return pl.pallas_call(
        body, name=name, grid=(p_n, steps),
        in_specs=[pl.BlockSpec((tm, k), lambda p, m: (m, 0)),
                  pl.BlockSpec((None, tm, n), lambda p, m: (p // per, m, p % per))],
        out_specs=pl.BlockSpec((None, k, n), lambda p, m: (p, 0, 0)),
        out_shape=jax.ShapeDtypeStruct((p_n, k, n), BF16),
        scratch_shapes=[pltpu.VMEM((k, n), F32)],
    )(a, d)


def _premix_proj(x, shift, scale, w, name):
    s, dm = x.shape
    p_n, _, n = w.shape
    tm = _tile(s, ROW_TILE)

    def body(x_ref, sh_ref, sc_ref, w_ref, h_ref, o_ref):
        xv = x_ref[...]
        inv = lax.rsqrt(jnp.mean(xv * xv, axis=-1, keepdims=True) + EPS)
        h = (xv * inv * (1.0 + sc_ref[...]) + sh_ref[...]).astype(BF16)
        h_ref[...] = h
        for p in range(p_n):
            o_ref[:, p * n:(p + 1) * n] = _dot(h, w_ref[p])

    row = pl.BlockSpec((tm, dm), lambda i: (i, 0))
    vec = _full((1, dm))
    return pl.pallas_call(
        body, name=name, grid=(s // tm,), in_specs=[row, vec, vec, _full(w.shape)],
        out_specs=[row, pl.BlockSpec((tm, p_n * n), lambda i: (i, 0))],
        out_shape=[jax.ShapeDtypeStruct((s, dm), BF16), jax.ShapeDtypeStruct((s, p_n * n), F32)],
    )(x, shift, scale, w)


def _premix_bwd(x, terms, dres, name, branch=None):
    s, dm = x.shape
    tm = _tile(s, ROW_TILE)
    pairs = [pr for _, prs in terms for pr in prs]
    n_in = 2 + len(terms) + 2 * len(pairs) + (2 if branch else 0)

    def body(*refs):
        x_ref, dres_ref = refs[:2]
        sc_refs = refs[2:2 + len(terms)]
        mm_refs = refs[2 + len(terms):2 + len(terms) + 2 * len(pairs)]
        outs = refs[n_in:]

        @pl.when(pl.program_id(0) == 0)
        def _():
            for o in outs[1:1 + 2 * len(terms)]:
                o[...] = jnp.zeros_like(o)
            if branch:
                outs[-1][...] = jnp.zeros_like(outs[-1])

        xv = x_ref[...]
        inv = lax.rsqrt(jnp.mean(xv * xv, axis=-1, keepdims=True) + EPS)
        r = xv * inv
        dx = dres_ref[...]
        k = 0
        for t, (_, prs) in enumerate(terms):
            dh = None
            for d, w in prs:
                d_ref, w_ref = mm_refs[2 * k], mm_refs[2 * k + 1]
                k += 1
                p_n, _, n = w.shape
                per = p_n // d.shape[0]
                for p in range(p_n):
                    part = _dot_nt(d_ref[p // per, :, (p % per) * n:(p % per + 1) * n], w_ref[p])
                    dh = part if dh is None else dh + part
            dr = dh * (1.0 + sc_refs[t][...])
            dx = dx + inv * (dr - r * jnp.mean(dr * r, axis=-1, keepdims=True))
            outs[1 + 2 * t][...] += _colsum8(dh)
            outs[2 + 2 * t][...] += _colsum8(dh * r)
        outs[0][...] = dx
        if branch:
            y_ref, g_ref = refs[n_in - 2:n_in]
            outs[-2][0] = (dx * g_ref[...]).astype(BF16)
            outs[-1][...] += _colsum8(dx * y_ref[...])

    row = pl.BlockSpec((tm, dm), lambda i: (i, 0))
    vec, acc = _full((1, dm)), _full((SUBLANES, dm))
    ins, specs = [x, dres] + [sc for sc, _ in terms], [row, row] + [vec] * len(terms)
    for d, w in pairs:
        ins += [d, w]
        specs += [pl.BlockSpec((d.shape[0], tm, d.shape[2]), lambda i: (0, i, 0)), _full(w.shape)]
    out_shape = [jax.ShapeDtypeStruct((s, dm), F32)] + [jax.ShapeDtypeStruct((SUBLANES, dm), F32)] * (2 * len(terms))
    out_specs = [row] + [acc] * (2 * len(terms))
    if branch:
        ins += list(branch)
        specs += [row, vec]
        out_shape += [jax.ShapeDtypeStruct((1, s, dm), BF16), jax.ShapeDtypeStruct((SUBLANES, dm), F32)]
        out_specs += [pl.BlockSpec((1, tm, dm), lambda i: (0, i, 0)), acc]
    outs = pl.pallas_call(body, name=name, grid=(s // tm,), in_specs=specs, out_specs=out_specs,
                          out_shape=out_shape)(*ins)
    partials = [(outs[1 + 2 * t], outs[2 + 2 * t]) for t in range(len(terms))]
    return (outs[0], partials) + ((outs[-2], outs[-1]) if branch else ())


def _conv_taps(e, w, b):
    return w[2:3] * e + w[1:2] * pltpu.roll(e, 1, 0) + w[0:1] * pltpu.roll(e, 2, 0) + b


def _ffn_specs(s, tm, cb):
    hb = tm // HALO
    last = s // HALO - 1
    main = pl.BlockSpec((2, tm, cb), lambda j, i: (0, i, j))
    prev = pl.BlockSpec((2, HALO, cb), lambda j, i: (0, jnp.maximum(i * hb - 1, 0), j))
    nxt = pl.BlockSpec((2, HALO, cb), lambda j, i: (0, jnp.minimum((i + 1) * hb, last), j))
    wspec = pl.BlockSpec((2, CONV_W, cb), lambda j, i: (0, 0, j))
    bspec = pl.BlockSpec((2, 1, cb), lambda j, i: (0, 0, j))
    return main, prev, nxt, wspec, bspec


def _convglu_bwd(u, c, dffn, w_down, w, name):
    _, s, f = u.shape
    dm = dffn.shape[2]
    tm = _tile(s, 256)
    cb = _tile(f, FFN_COLS)
    steps = s // tm
    n_ext = tm + HALO
    main, _, nxt, wspec, _ = _ffn_specs(s, tm, cb)
    hb = tm // HALO
    last = s // HALO - 1
    d_main = pl.BlockSpec((None, tm, dm), lambda j, i: (0, i, 0))
    d_next = pl.BlockSpec((None, HALO, dm), lambda j, i: (0, jnp.minimum((i + 1) * hb, last), 0))
    wd_spec = pl.BlockSpec((None, cb, dm), lambda j, i: (0, j, 0))

    def body(u_ref, c_ref, cn_ref, d_ref, dn_ref, wd_ref, w_ref, du_ref, acc_ref):
        i = pl.program_id(1)
        notlast = jnp.where(i < steps - 1, 1.0, 0.0)

        @pl.when(i == 0)
        def _():
            acc_ref[...] = jnp.zeros_like(acc_ref)

        gate, val = (jnp.concatenate([c_ref[g].astype(F32), cn_ref[g].astype(F32)], axis=0) for g in range(2))
        wd = wd_ref[...]
        da = jnp.concatenate([_dot_nt(d_ref[...], wd).astype(BF16).astype(F32),
                              _dot_nt(dn_ref[...], wd).astype(BF16).astype(F32) * notlast], axis=0)
        sg = _sig(gate)
        d_val = da * gate * sg
        d_gate = da * val * (sg * (1.0 + gate * (1.0 - sg)))

        def finish(g, d):
            wv = w_ref[g]
            d1, d2 = pltpu.roll(d, n_ext - 1, 0), pltpu.roll(d, n_ext - 2, 0)
            du_ref[g] = (wv[2:3] * d + wv[1:2] * d1 + wv[0:1] * d2)[0:tm].astype(BF16)
            uv = u_ref[g].astype(F32)
            acc_ref[g, 2] += _colsum8(d[0:tm] * uv)
            acc_ref[g, 1] += _colsum8(d1[0:tm] * uv)
            acc_ref[g, 0] += _colsum8(d2[0:tm] * uv)
            acc_ref[g, 3] += _colsum8(d[0:tm])

        finish(0, d_gate)
        finish(1, d_val)

    return pl.pallas_call(
        body, name=name, grid=(f // cb, steps),
        in_specs=[main, main, nxt, d_main, d_next, wd_spec, wspec],
        out_specs=[main, pl.BlockSpec((2, 4, SUBLANES, cb), lambda j, i: (0, 0, 0, j))],
        out_shape=[jax.ShapeDtypeStruct((2, s, f), BF16), jax.ShapeDtypeStruct((2, 4, SUBLANES, f), F32)],
    )(u, c, c, dffn, dffn, w_down, w)


def _hgrn_gates(q_raw, f_raw, lb, tri):
    sf = _sig(f_raw)
    fg = lb + (1.0 - lb) * sf
    b = _tri_dot(tri, jnp.log(fg))
    return q_raw * _sig(q_raw), 1.0 - fg, b, fg, sf


def _hgrn_fwd(proj, lb, norm_g, name):
    s = proj.shape[0]
    tb = _tile(s, HGRN_ROWS)
    n_c = tb // A_CHUNK
    half = A_CHUNK // 2

    def body(q_ref, f_ref, v_ref, g_ref, lb_ref, ng_ref, o_ref, yp_ref, st_ref, state):
        @pl.when(pl.program_id(0) == 0)
        def _():
            state[...] = jnp.zeros_like(state)

        tri = _tri(A_CHUNK)
        causal = lax.broadcasted_iota(jnp.int32, (A_CHUNK, A_CHUNK), 1) <= lax.broadcasted_iota(
            jnp.int32, (A_CHUNK, A_CHUNK), 0)

        def chunk(ci, carry):
            rows = pl.ds(pl.multiple_of(ci * A_CHUNK, A_CHUNK), A_CHUNK)
            heads = [slice(h * HEAD_DIM, (h + 1) * HEAD_DIM) for h in range(HEADS)]
            qs, k, b, _, _ = _hgrn_gates(q_ref[rows, :], f_ref[rows, :], lb_ref[...], tri)
            b_mid, b_last = b[half:half + 1], b[A_CHUNK - 1:A_CHUNK]
            q_i = (qs * jnp.exp(b - b_mid)).astype(BF16)
            k_i = (k * jnp.exp(b_mid - b)).astype(BF16)
            q_e = (qs * jnp.exp(b)).astype(BF16)
            k_s = (k * jnp.exp(b_last - b)).astype(BF16)
            decay = jnp.exp(b_last)
            vb = v_ref[rows, :].astype(BF16)
            scores = [jnp.where(causal, _dot_nt(q_i[:, cs], k_i[:, cs]), 0.0).astype(BF16) for cs in heads]
            st = [state[h] for h in range(HEADS)]
            outs = [_dot(scores[h], vb[:, cs]) + _dot_nt(q_e[:, cs], st[h].astype(BF16)) for h, cs in enumerate(heads)]
            for h, cs in enumerate(heads):
                st_ref[ci, h] = st[h]
                state[h] = st[h] * decay[:, cs] + _dot_tn(vb[:, cs], k_s[:, cs])
            o = jnp.concatenate(outs, axis=1)
            o_ref[rows, :] = o
            sq = o * o
            inv = jnp.concatenate([jnp.broadcast_to(lax.rsqrt(jnp.mean(sq[:, cs], axis=-1, keepdims=True) + EPS),
                                                    (A_CHUNK, HEAD_DIM)) for cs in heads], axis=1)
            g_raw = g_ref[rows, :]
            yp_ref[rows, :] = (o * inv * ng_ref[...] * (g_raw * _sig(g_raw))).astype(BF16)
            return carry

        lax.fori_loop(0, n_c, chunk, 0)

    col = lambda j: pl.BlockSpec((tb, D_MODEL), lambda i: (i, j))
    vec = _full((1, D_MODEL))
    return pl.pallas_call(
        body, name=name, grid=(s // tb,), in_specs=[col(0), col(1), col(2), col(3), vec, vec],
        out_specs=[col(0), col(0), pl.BlockSpec((n_c, HEADS, HEAD_DIM, HEAD_DIM), lambda i: (i, 0, 0, 0))],
        out_shape=[jax.ShapeDtypeStruct((s, D_MODEL), F32), jax.ShapeDtypeStruct((s, D_MODEL), BF16),
                   jax.ShapeDtypeStruct((s // A_CHUNK, HEADS, HEAD_DIM, HEAD_DIM), F32)],
        scratch_shapes=[pltpu.VMEM((HEADS, HEAD_DIM, HEAD_DIM), F32)],
    )(proj, proj, proj, proj, lb, norm_g)


def _hgrn_bwd(proj, lb, norm_g, o, states, dout, w_out, name):
    s = proj.shape[0]
    tb = _tile(s, HGRN_ROWS)
    n_c = tb // A_CHUNK
    n_b = s // tb
    half = A_CHUNK // 2

    def body(q_ref, f_ref, v_ref, g_ref, lb_ref, ng_ref, o_ref, st_ref, dout_ref, w_ref, dp_ref, dlb_ref, dng_ref,
             dstate, dyp_ref):
        @pl.when(pl.program_id(0) == 0)
        def _():
            dstate[...] = jnp.zeros_like(dstate)
            dlb_ref[...] = jnp.zeros_like(dlb_ref)
            dng_ref[...] = jnp.zeros_like(dng_ref)

        dyp_ref[...] = _dot_nt(dout_ref[0], w_ref[0])

        tri = _tri(A_CHUNK)
        tri_up = _tri(A_CHUNK, upper=True)
        row_id = lax.broadcasted_iota(jnp.int32, (A_CHUNK, D_MODEL), 0)
        causal = lax.broadcasted_iota(jnp.int32, (A_CHUNK, A_CHUNK), 1) <= lax.broadcasted_iota(
            jnp.int32, (A_CHUNK, A_CHUNK), 0)

        def chunk(cj, carry):
            ci = n_c - 1 - cj
            rows = pl.ds(pl.multiple_of(ci * A_CHUNK, A_CHUNK), A_CHUNK)
            heads = [slice(h * HEAD_DIM, (h + 1) * HEAD_DIM) for h in range(HEADS)]
            cat = lambda parts: jnp.concatenate(parts, axis=1)
            per_head_mean = lambda a: cat([jnp.broadcast_to(jnp.mean(a[:, cs], axis=-1, keepdims=True),
                                                            (A_CHUNK, HEAD_DIM)) for cs in heads])
            q_raw, lbv = q_ref[rows, :], lb_ref[...]
            qs, k, b, fg, sf = _hgrn_gates(q_raw, f_ref[rows, :], lbv, tri)
            b_mid, b_last = b[half:half + 1], b[A_CHUNK - 1:A_CHUNK]
            e_qi, e_ki, e_q, e_ks = jnp.exp(b - b_mid), jnp.exp(b_mid - b), jnp.exp(b), jnp.exp(b_last - b)
            decay = jnp.exp(b_last)
            q_i, k_i, q_e, k_s = qs * e_qi, k * e_ki, qs * e_q, k * e_ks
            qib, kib, qeb, ksb = q_i.astype(BF16), k_i.astype(BF16), q_e.astype(BF16), k_s.astype(BF16)
            vb = v_ref[rows, :].astype(BF16)
            ov, g_raw, dy, ng = o_ref[rows, :], g_ref[rows, :], dyp_ref[rows, :], ng_ref[...]
            inv = lax.rsqrt(per_head_mean(ov * ov) + EPS)
            nrm = ov * inv
            sg = _sig(g_raw)
            gs = g_raw * sg
            dn = dy * ng * gs
            dng_ref[0:1, :] += jnp.sum(dy * nrm * gs, axis=0, keepdims=True)
            dg_raw = dy * nrm * ng * (sg * (1.0 + g_raw * (1.0 - sg)))
            do = (inv * (dn - nrm * per_head_mean(dn * nrm))).astype(BF16)
            st_prev = [st_ref[ci, h] for h in range(HEADS)]
            dst = [dstate[h] for h in range(HEADS)]
            dstb = [d.astype(BF16) for d in dst]
            scores = [jnp.where(causal, _dot_nt(qib[:, cs], kib[:, cs]), 0.0).astype(BF16) for cs in heads]
            d_scores = [jnp.where(causal, _dot_nt(do[:, cs], vb[:, cs]), 0.0).astype(BF16) for cs in heads]
            dv = cat([_dot_tn(scores[h], do[:, cs]) + _dot_nt(ksb[:, cs], dstb[h]) for h, cs in enumerate(heads)])
            dq_i = cat([_dot(d_scores[h], kib[:, cs]) for h, cs in enumerate(heads)])
            dk_i = cat([_dot_tn(d_scores[h], qib[:, cs]) for h, cs in enumerate(heads)])
            dq_e = cat([_dot(do[:, cs], st_prev[h].astype(BF16)) for h, cs in enumerate(heads)])
            dk_s = cat([_dot(vb[:, cs], dstb[h]) for h, cs in enumerate(heads)])
            d_decay = cat([jnp.sum(st_prev[h] * dst[h], axis=0, keepdims=True) for h in range(HEADS)])
            for h, cs in enumerate(heads):
                dstate[h] = dst[h] * decay[:, cs] + _dot_tn(do[:, cs], qeb[:, cs])
            dq = dq_i * e_qi + dq_e * e_q
            dk = dk_i * e_ki + dk_s * e_ks
            t_qi, t_ki, t_ks = dq_i * q_i, dk_i * k_i, dk_s * k_s
            db = t_qi - t_ki + dq_e * q_e - t_ks
            db_mid = jnp.sum(t_ki - t_qi, axis=0, keepdims=True)
            db_last = jnp.sum(t_ks, axis=0, keepdims=True) + d_decay * decay
            db = db + jnp.where(row_id == half, db_mid, 0.0) + jnp.where(row_id == A_CHUNK - 1, db_last, 0.0)
            dfg = _tri_dot(tri_up, db) / fg - dk
            dlb_ref[0:1, :] += jnp.sum(dfg * (1.0 - sf), axis=0, keepdims=True)
            sq = _sig(q_raw)
            dp_ref[0, rows, :] = (dq * (sq * (1.0 + q_raw * (1.0 - sq)))).astype(BF16)
            dp_ref[1, rows, :] = (dfg * (1.0 - lbv) * sf * (1.0 - sf)).astype(BF16)
            dp_ref[2, rows, :] = dv.astype(BF16)
            dp_ref[3, rows, :] = dg_raw.astype(BF16)
            return carry

        lax.fori_loop(0, n_c, chunk, 0)

    col = lambda j: pl.BlockSpec((tb, D_MODEL), lambda i: (n_b - 1 - i, j))
    vec = _full((1, D_MODEL))
    acc = _full((SUBLANES, D_MODEL))
    return pl.pallas_call(
        body, name=name, grid=(n_b,),
        in_specs=[col(0), col(1), col(2), col(3), vec, vec, col(0),
                  pl.BlockSpec((n_c, HEADS, HEAD_DIM, HEAD_DIM), lambda i: (n_b - 1 - i, 0, 0, 0)),
                  pl.BlockSpec((1, tb, D_MODEL), lambda i: (0, n_b - 1 - i, 0)), _full(w_out.shape)],
        out_specs=[pl.BlockSpec((4, tb, D_MODEL), lambda i: (0, n_b - 1 - i, 0)), acc, acc],
        out_shape=[jax.ShapeDtypeStruct((4, s, D_MODEL), BF16), jax.ShapeDtypeStruct((SUBLANES, D_MODEL), F32),
                   jax.ShapeDtypeStruct((SUBLANES, D_MODEL), F32)],
        scratch_shapes=[pltpu.VMEM((HEADS, HEAD_DIM, HEAD_DIM), F32), pltpu.VMEM((tb, D_MODEL), F32)],
    )(proj, proj, proj, proj, lb, norm_g, o, states, dout, w_out)


def _head_rms(raw_ref, g_ref, mult, y_ref):
    for h in range(HEADS):
        cs = slice(h * HEAD_DIM, (h + 1) * HEAD_DIM)
        xv = raw_ref[:, cs]
        inv = lax.rsqrt(jnp.mean(xv * xv, axis=-1, keepdims=True) + EPS)
        y_ref[:, cs] = (xv * inv * g_ref[:, cs] * mult).astype(BF16)


def _proj_headnorm(a, w, g, mult, name):
    s, k = a.shape
    p_n, _, n = w.shape
    tm = _tile(s, ROW_TILE)

    def body(a_ref, w_ref, g_ref, raw_ref, y_ref):
        av = a_ref[...]
        for p in range(p_n):
            raw_ref[:, p * n:(p + 1) * n] = _dot(av, w_ref[p])
        _head_rms(raw_ref, g_ref, mult, y_ref)

    row = lambda wid: pl.BlockSpec((tm, wid), lambda i: (i, 0))
    return pl.pallas_call(
        body, name=name, grid=(s // tm,), in_specs=[row(k), _full(w.shape), _full((1, D_MODEL))],
        out_specs=[row(p_n * n), row(D_MODEL)],
        out_shape=[jax.ShapeDtypeStruct((s, p_n * n), F32), jax.ShapeDtypeStruct((s, D_MODEL), BF16)],
    )(a, w, g)


def _kv_proj(hk, w_k, w_v, w_f, g, name):
    s, k = hk.shape
    tm = _tile(s, ROW_TILE)

    def body(h_ref, wk_ref, wv_ref, wf_ref, g_ref, kr_ref, k_ref, v_ref, f_ref):
        hv = h_ref[...]
        kr_ref[...] = _dot(hv, wk_ref[0])
        v_ref[...] = _dot(hv, wv_ref[0]).astype(BF16)
        f_ref[...] = _dot(hv, wf_ref[0])
        _head_rms(kr_ref, g_ref, 1.0, k_ref)

    row = lambda wid: pl.BlockSpec((tm, wid), lambda i: (i, 0))
    return pl.pallas_call(
        body, name=name, grid=(s // tm,),
        in_specs=[row(k), _full(w_k.shape), _full(w_v.shape), _full(w_f.shape), _full((1, D_MODEL))],
        out_specs=[row(D_MODEL), row(D_MODEL), row(D_MODEL), row(LANES)],
        out_shape=[jax.ShapeDtypeStruct((s, D_MODEL), F32), jax.ShapeDtypeStruct((s, D_MODEL), BF16),
                   jax.ShapeDtypeStruct((s, D_MODEL), BF16), jax.ShapeDtypeStruct((s, LANES), F32)],
    )(hk, w_k, w_v, w_f, g)


def _headnorm_bwd(x, g, mult, dy, name, col0=0, extra=None):
    s = x.shape[0]
    tm = _tile(s, ROW_TILE)
    groups = 2 if extra is not None else 1
    head_major = dy.ndim == 3

    def body(*refs):
        x_ref, g_ref, dy_ref = refs[:3]
        dx_ref, dg_ref = refs[-2:]

        @pl.when(pl.program_id(0) == 0)
        def _():
            dg_ref[...] = jnp.zeros_like(dg_ref)

        for h in range(HEADS):
            cs = slice(h * HEAD_DIM, (h + 1) * HEAD_DIM)
            xv, gv = x_ref[:, cs], g_ref[:, cs]
            dyv = dy_ref[h, :, 0:HEAD_DIM] if head_major else dy_ref[:, cs]
            inv = lax.rsqrt(jnp.mean(xv * xv, axis=-1, keepdims=True) + EPS)
            nrm = xv * inv
            dn = dyv * gv * mult
            dg_ref[:, cs] += _colsum8(dyv * nrm * mult)
            dx_ref[0, :, cs] = (inv * (dn - nrm * jnp.mean(dn * nrm, axis=-1, keepdims=True))).astype(BF16)
        if extra is not None:
            dx_ref[1] = refs[3][...]

    row = pl.BlockSpec((tm, D_MODEL), lambda i: (i, 0))
    dy_spec = pl.BlockSpec((HEADS, tm, dy.shape[-1]), lambda i: (0, i, 0)) if head_major else row
    ins = [x, g, dy] + ([extra] if extra is not None else [])
    specs = ([pl.BlockSpec((tm, D_MODEL), lambda i: (i, col0)), _full((1, D_MODEL)), dy_spec]
             + ([row] if extra is not None else []))
    return pl.pallas_call(
        body, name=name, grid=(s // tm,), in_specs=specs,
        out_specs=[pl.BlockSpec((groups, tm, D_MODEL), lambda i: (0, i, 0)), _full((SUBLANES, D_MODEL))],
        out_shape=[jax.ShapeDtypeStruct((groups, s, D_MODEL), BF16), jax.ShapeDtypeStruct((SUBLANES, D_MODEL), F32)],
    )(*ins)


def _log_sigmoid(z):
    return jnp.minimum(z, 0.0) - jnp.log(1.0 + jnp.exp(-jnp.abs(z)))


Q_CUM, Q_ONE, Q_LSE = 0, 3, 6
LOG2E = 1.4426950408889634


def _pieces(v):
    hi = v.astype(BF16).astype(F32)
    mid = (v - hi).astype(BF16).astype(F32)
    lo = ((v - hi) - mid).astype(BF16).astype(F32)
    return hi, mid, lo


def _side(lane, at, v):
    hi, mid, lo = _pieces(v)
    return jnp.where(lane == at, hi, jnp.where(lane == at + 1, mid, jnp.where(lane == at + 2, lo, 0.0)))


def _fcum_fwd(f, bias, name):
    s = f.shape[0]
    tm = _tile(s, ROW_TILE)

    def body(f_ref, b_ref, qa_ref, ka_ref, carry):
        @pl.when(pl.program_id(0) == 0)
        def _():
            carry[...] = jnp.zeros_like(carry)

        cum = _tri_dot(_tri(tm), _log_sigmoid(f_ref[...] + b_ref[...])) + carry[...]
        carry[...] = cum[tm - 1:tm]
        lane = lax.broadcasted_iota(jnp.int32, (tm, LANES), 1)
        ones_q = jnp.where((lane >= Q_ONE) & (lane < Q_LSE), 1.0, 0.0)
        ones_k = jnp.where((lane < Q_ONE) | ((lane >= Q_LSE) & (lane < Q_LSE + 3)), 1.0, 0.0)
        for h in range(HEADS):
            c2 = cum[:, h:h + 1] * LOG2E
            qa_ref[h] = (_side(lane, Q_CUM, c2) + ones_q).astype(BF16)
            ka_ref[h] = (_side(lane, Q_ONE, -c2) + ones_k).astype(BF16)

    side = pl.BlockSpec((HEADS, tm, LANES), lambda i: (0, i, 0))
    return pl.pallas_call(
        body, name=name, grid=(s // tm,),
        in_specs=[pl.BlockSpec((tm, LANES), lambda i: (i, 0)), _full((1, LANES))],
        out_specs=[side, side],
        out_shape=[jax.ShapeDtypeStruct((HEADS, s, LANES), BF16)] * 2,
        scratch_shapes=[pltpu.VMEM((1, LANES), F32)],
    )(f, bias)


def _fcum_bwd(f, bias, dka, dcq, name):
    s = f.shape[0]
    tm = _tile(s, ROW_TILE)
    n_b = s // tm

    def body(f_ref, b_ref, dka_ref, dcq_ref, dz_ref, db_ref, carry):
        @pl.when(pl.program_id(0) == 0)
        def _():
            carry[...] = jnp.zeros_like(carry)
            db_ref[...] = jnp.zeros_like(db_ref)

        lane = lax.broadcasted_iota(jnp.int32, (tm, LANES), 1)
        rows = jnp.concatenate([dcq_ref[h] for h in range(HEADS)] + [jnp.zeros((LANES - HEADS, tm), F32)], axis=0)
        dcum = rows.T
        for h in range(HEADS):
            dcum = dcum - jnp.where(lane == h, dka_ref[h, :, Q_ONE:Q_ONE + 1], 0.0)
        dlf = _tri_dot(_tri(tm, upper=True), dcum) + carry[...]
        carry[...] = dlf[0:1]
        dz = dlf * _sig(-(f_ref[...] + b_ref[...]))
        dz_ref[0] = dz.astype(BF16)
        db_ref[...] += _colsum8(dz)

    return pl.pallas_call(
        body, name=name, grid=(n_b,),
        in_specs=[pl.BlockSpec((tm, LANES), lambda i: (n_b - 1 - i, 0)), _full((1, LANES)),
                  pl.BlockSpec((HEADS, tm, LANES), lambda i: (0, n_b - 1 - i, 0)),
                  pl.BlockSpec((HEADS, 1, tm), lambda i: (0, 0, n_b - 1 - i))],
        out_specs=[pl.BlockSpec((1, tm, LANES), lambda i: (0, n_b - 1 - i, 0)), _full((SUBLANES, LANES))],
        out_shape=[jax.ShapeDtypeStruct((1, s, LANES), BF16), jax.ShapeDtypeStruct((SUBLANES, LANES), F32)],
        scratch_shapes=[pltpu.VMEM((1, LANES), F32)],
    )(f, bias, dka, dcq)


def _causal_pairs(n_t, key_major):
    if key_major:
        pairs = [(qi, ki) for ki in range(n_t) for qi in range(ki, n_t)]
    else:
        pairs = [(qi, ki) for qi in range(n_t) for ki in range(qi + 1)]
    return (jnp.array([p[0] for p in pairs], jnp.int32), jnp.array([p[1] for p in pairs], jnp.int32))


def _with_side(main_ref, side_ref):
    return jnp.concatenate([main_ref[...], side_ref[...]], axis=1)


def _lane_const(t, lo, hi, value):
    lane = lax.broadcasted_iota(jnp.int32, (t, LANES), 1)
    return jnp.where((lane >= lo) & (lane < hi), value, 0.0).astype(BF16)


def _att_specs(t, nh):
    qmain = pl.BlockSpec((t, nh * HEAD_DIM), lambda h, p, qt, kt: (qt[p], h))
    kmain = pl.BlockSpec((t, nh * HEAD_DIM), lambda h, p, qt, kt: (kt[p], h))
    qside = pl.BlockSpec((nh, t, LANES), lambda h, p, qt, kt: (h, qt[p], 0))
    kside = pl.BlockSpec((nh, t, LANES), lambda h, p, qt, kt: (h, kt[p], 0))
    return qmain, kmain, qside, kside


def _fox_fwd(q, qa, k, ka, v, qo, name):
    s = q.shape[0]
    t = _tile(s, ATT_TILE)
    sub = t // ATT_SPLIT
    nh = ATT_FWD_HEADS
    qt, kt = _causal_pairs(s // t, key_major=False)

    def body(qt_ref, kt_ref, q_ref, qa_ref, k_ref, ka_ref, v_ref, og_ref, o_ref, y_ref, qab_ref, m_s, l_s, acc_s):
        pid = pl.program_id(1)
        qi, ki = qt_ref[pid], kt_ref[pid]

        @pl.when(ki == 0)
        def _():
            m_s[...] = jnp.full_like(m_s, NEG_INF)
            l_s[...] = jnp.zeros_like(l_s)
            acc_s[...] = jnp.zeros_like(acc_s)

        def step(diagonal):
            for hh in range(nh):
                hc = slice(hh * HEAD_DIM, (hh + 1) * HEAD_DIM)
                kc = jnp.concatenate([k_ref[:, hc], ka_ref[hh]], axis=1)
                vc = jnp.concatenate([v_ref[:, hc], _lane_const(t, 0, 1, 1.0)], axis=1)
                for r in range(ATT_SPLIT):
                    rows = slice(r * sub, (r + 1) * sub)
                    n_k = (r + 1) * sub if diagonal else t
                    sc = _dot_nt(jnp.concatenate([q_ref[rows, hc], qa_ref[hh, rows]], axis=1), kc[:n_k])
                    if diagonal:
                        sc = jnp.where(lax.broadcasted_iota(jnp.int32, (sub, n_k), 1)
                                       <= lax.broadcasted_iota(jnp.int32, (sub, n_k), 0) + r * sub, sc, NEG_INF)
                    m_old = m_s[hh, rows]
                    m_new = jnp.maximum(m_old, jnp.max(sc, axis=-1, keepdims=True))
                    alpha = jnp.exp2(m_old - m_new)
                    pv = _dot(jnp.exp2(sc - m_new[:, 0:1]).astype(BF16), vc[:n_k])
                    acc_s[hh, rows] = alpha * acc_s[hh, rows] + pv[:, :HEAD_DIM]
                    l_s[hh, rows] = alpha * l_s[hh, rows] + pv[:, HEAD_DIM:]
                    m_s[hh, rows] = m_new

        @pl.when(ki < qi)
        def _():
            step(False)

        @pl.when(ki == qi)
        def _():
            step(True)
            lane = lax.broadcasted_iota(jnp.int32, (t, LANES), 1)
            for hh in range(nh):
                hc = slice(hh * HEAD_DIM, (hh + 1) * HEAD_DIM)
                l = l_s[hh, :, 0:1]
                o = acc_s[hh] / l
                o_ref[:, hc] = o
                y_ref[:, hc] = (o * _sig(og_ref[:, hc])).astype(BF16)
                qab_ref[hh] = qa_ref[hh] + _side(lane, Q_LSE, -(m_s[hh, :, 0:1] + jnp.log2(l))).astype(BF16)

    qmain, kmain, qside, kside = _att_specs(t, nh)
    return pl.pallas_call(
        body, name=name,
        grid_spec=pltpu.PrefetchScalarGridSpec(
            num_scalar_prefetch=2, grid=(HEADS // nh, qt.shape[0]),
            in_specs=[qmain, qside, kmain, kside, kmain,
                      pl.BlockSpec((t, nh * HEAD_DIM), lambda h, p, qt, kt: (qt[p], HEADS // nh + h))],
            out_specs=[qmain, qmain, qside],
            scratch_shapes=[pltpu.VMEM((nh, t, LANES), F32), pltpu.VMEM((nh, t, LANES), F32),
                            pltpu.VMEM((nh, t, HEAD_DIM), F32)]),
        out_shape=[jax.ShapeDtypeStruct((s, D_MODEL), F32), jax.ShapeDtypeStruct((s, D_MODEL), BF16),
                   jax.ShapeDtypeStruct((HEADS, s, LANES), BF16)],
    )(qt, kt, q, qa, k, ka, v, qo)


def _fox_gate_bwd(o, qo, dout, w_out, name):
    s = o.shape[0]
    tm = _tile(s, ROW_TILE)

    def body(o_ref, og_ref, dout_ref, w_ref, do_ref, dg_ref, dl_ref):
        ov, dyv = o_ref[...], _dot_nt(dout_ref[0], w_ref[0])
        sg = _sig(og_ref[...])
        do = (dyv * sg).astype(BF16)
        do_ref[...] = do
        dg_ref[...] = (dyv * ov * sg * (1.0 - sg)).astype(BF16)
        prod = do.astype(F32) * ov
        lane = lax.broadcasted_iota(jnp.int32, (tm, LANES), 1)
        for h in range(HEADS):
            delta = jnp.sum(prod[:, h * HEAD_DIM:(h + 1) * HEAD_DIM], axis=-1, keepdims=True)
            dl_ref[h] = _side(lane, 0, delta).astype(BF16)

    row = pl.BlockSpec((tm, D_MODEL), lambda i: (i, 0))
    return pl.pallas_call(
        body, name=name, grid=(s // tm,),
        in_specs=[row, pl.BlockSpec((tm, D_MODEL), lambda i: (i, 1)),
                  pl.BlockSpec((1, tm, D_MODEL), lambda i: (0, i, 0)), _full(w_out.shape)],
        out_specs=[row, row, pl.BlockSpec((HEADS, tm, LANES), lambda i: (0, i, 0))],
        out_shape=[jax.ShapeDtypeStruct((s, D_MODEL), BF16), jax.ShapeDtypeStruct((s, D_MODEL), BF16),
                   jax.ShapeDtypeStruct((HEADS, s, LANES), BF16)],
    )(o, qo, dout, w_out)


def _fox_bwd(q, qab, k, ka, v, do, doa, name):
    s = q.shape[0]
    t = _tile(s, ATT_TILE)
    n_t = s // t
    sub = t // ATT_SPLIT
    nh = ATT_BWD_HEADS
    qt, kt = _causal_pairs(n_t, key_major=True)

    def body(qt_ref, kt_ref, q_ref, qab_ref, k_ref, ka_ref, v_ref, do_ref, doa_ref, dk_ref, dv_ref, dka_ref, dq_hbm,
             dcq_hbm, dk_s, dv_s, dq_ref, dcq_ref):
        group, pid = pl.program_id(0), pl.program_id(1)
        qi, ki = qt_ref[pid], kt_ref[pid]

        @pl.when(pid == 0)
        def _():
            dq_ref[...] = jnp.zeros_like(dq_ref)
            dcq_ref[...] = jnp.zeros_like(dcq_ref)

        @pl.when(qi == ki)
        def _():
            dk_s[...] = jnp.zeros_like(dk_s)
            dv_s[...] = jnp.zeros_like(dv_s)

        def step(diagonal):
            for hh in range(nh):
                hc = slice(hh * HEAD_DIM, (hh + 1) * HEAD_DIM)
                kc = jnp.concatenate([k_ref[:, hc], ka_ref[hh]], axis=1)
                vc = jnp.concatenate([v_ref[:, hc], _lane_const(t, 0, 3, -1.0)], axis=1)
                for r in range(ATT_SPLIT):
                    cols = slice(r * sub, (r + 1) * sub)
                    n_k = (r + 1) * sub if diagonal else t
                    qc = jnp.concatenate([q_ref[cols, hc], qab_ref[hh, cols]], axis=1)
                    sc = _dot_nt(kc[:n_k], qc)
                    if diagonal:
                        sc = jnp.where(lax.broadcasted_iota(jnp.int32, (n_k, sub), 0)
                                       <= lax.broadcasted_iota(jnp.int32, (n_k, sub), 1) + r * sub, sc, NEG_INF)
                    p = jnp.exp2(sc)
                    dov = do_ref[cols, hc]
                    dp = _dot_nt(vc[:n_k], jnp.concatenate([dov, doa_ref[hh, cols]], axis=1))
                    ds = (p * dp).astype(BF16)
                    dv_s[hh, 0:n_k] += _dot(p.astype(BF16), dov)
                    dk_s[hh, 0:n_k] += _dot(ds, qc)
                    q_rows = pl.ds(pl.multiple_of(qi * t + r * sub, sub), sub)
                    dq_ref[hh, q_rows, :] += _dot_tn(ds, k_ref[0:n_k, hc])
                    dcq_ref[hh, qi * ATT_SPLIT + r] += jnp.sum(ds.astype(F32), axis=0, keepdims=True)

        @pl.when(qi > ki)
        def _():
            step(False)

        @pl.when(qi == ki)
        def _():
            step(True)

        @pl.when(qi == n_t - 1)
        def _():
            for hh in range(nh):
                hc = slice(hh * HEAD_DIM, (hh + 1) * HEAD_DIM)
                dk_ref[:, hc] = dk_s[hh, :, :HEAD_DIM] * (1.0 / LOG2E)
                dka_ref[hh] = dk_s[hh, :, HEAD_DIM:]
                dv_ref[:, hc] = dv_s[hh].astype(BF16)

        @pl.when(pid == qt.shape[0] - 1)
        def _():
            pltpu.sync_copy(dq_ref, dq_hbm.at[pl.ds(group * nh, nh)])
            pltpu.sync_copy(dcq_ref, dcq_hbm.at[pl.ds(group * nh, nh)])

    qmain, kmain, qside, kside = _att_specs(t, nh)
    in_hbm = pl.BlockSpec(memory_space=pltpu.HBM)
    return pl.pallas_call(
        body, name=name,
        grid_spec=pltpu.PrefetchScalarGridSpec(
            num_scalar_prefetch=2, grid=(HEADS // nh, qt.shape[0]),
            in_specs=[qmain, qside, kmain, kside, kmain, qmain, qside],
            out_specs=[kmain, pl.BlockSpec((None, t, nh * HEAD_DIM), lambda h, p, qt, kt: (0, kt[p], h)), kside,
                       in_hbm, in_hbm],
            scratch_shapes=[pltpu.VMEM((nh, t, 2 * HEAD_DIM), F32), pltpu.VMEM((nh, t, HEAD_DIM), F32),
                            pltpu.VMEM((nh, s, HEAD_DIM), F32), pltpu.VMEM((nh, s // sub, 1, sub), F32)]),
        out_shape=[jax.ShapeDtypeStruct((s, D_MODEL), F32), jax.ShapeDtypeStruct((1, s, D_MODEL), BF16),
                   jax.ShapeDtypeStruct((HEADS, s, LANES), F32), jax.ShapeDtypeStruct((HEADS, s, HEAD_DIM), F32),
                   jax.ShapeDtypeStruct((HEADS, s // sub, 1, sub), F32)],
    )(qt, kt, q, qab, k, ka, v, do, doa)


def _mm_residual_premix(a, w, x, gate, mods, name):
    s, k = a.shape
    dm = x.shape[1]
    tm = _tile(s, ROW_TILE)

    def body(*refs):
        a_ref, w_ref, x_ref, g_ref = refs[:4]
        mod_refs = refs[4:4 + 2 * len(mods)]
        y_ref, xn_ref = refs[4 + 2 * len(mods):6 + 2 * len(mods)]
        h_refs = refs[6 + 2 * len(mods):]
        y = _dot(a_ref[...], w_ref[0])
        y_ref[...] = y
        xv = x_ref[...] + g_ref[...] * y
        xn_ref[...] = xv
        nrm = xv * lax.rsqrt(jnp.mean(xv * xv, axis=-1, keepdims=True) + EPS)
        for t, h_ref in enumerate(h_refs):
            h_ref[...] = (nrm * (1.0 + mod_refs[2 * t + 1][...]) + mod_refs[2 * t][...]).astype(BF16)

    row = pl.BlockSpec((tm, dm), lambda i: (i, 0))
    vec = _full((1, dm))
    outs = pl.pallas_call(
        body, name=name, grid=(s // tm,),
        in_specs=[pl.BlockSpec((tm, k), lambda i: (i, 0)), _full(w.shape), row, vec] + [vec] * (2 * len(mods)),
        out_specs=[row] * (2 + len(mods)),
        out_shape=[jax.ShapeDtypeStruct((s, dm), F32)] * 2 + [jax.ShapeDtypeStruct((s, dm), BF16)] * len(mods),
    )(a, w, x, gate, *[v for m in mods for v in m])
    return outs[0], outs[1], list(outs[2:])


def _mm_loss_head(a, w, x, gate, target, name):
    s, k = a.shape
    dm = x.shape[1]
    tm = _tile(s, ROW_TILE)

    def body(a_ref, w_ref, x_ref, g_ref, t_ref, sq_ref, do_ref, dy_ref, dg_ref):
        @pl.when(pl.program_id(0) == 0)
        def _():
            sq_ref[...] = jnp.zeros_like(sq_ref)
            dg_ref[...] = jnp.zeros_like(dg_ref)

        y, gv = _dot(a_ref[...], w_ref[0]), g_ref[...]
        err = x_ref[...] + gv * y - t_ref[...]
        sq_ref[...] += _colsum8(err * err)
        dout = err * (1.0 / dm)
        do_ref[...] = dout
        dy_ref[0] = (dout * gv).astype(BF16)
        dg_ref[...] += _colsum8(dout * y)

    row = pl.BlockSpec((tm, dm), lambda i: (i, 0))
    acc = _full((SUBLANES, dm))
    return pl.pallas_call(
        body, name=name, grid=(s // tm,),
        in_specs=[pl.BlockSpec((tm, k), lambda i: (i, 0)), _full(w.shape), row, _full((1, dm)), row],
        out_specs=[acc, row, pl.BlockSpec((1, tm, dm), lambda i: (0, i, 0)), acc],
        out_shape=[jax.ShapeDtypeStruct((SUBLANES, dm), F32), jax.ShapeDtypeStruct((s, dm), F32),
                   jax.ShapeDtypeStruct((1, s, dm), BF16), jax.ShapeDtypeStruct((SUBLANES, dm), F32)],
    )(a, w, x, gate, target)


def _ffn_inner(h, w_up, conv_w, conv_b, tag):
    s, dm = h.shape
    half = w_up.shape[2]
    f = 2 * half
    tm = _tile(s, FFN_ROWS)

    def body(h_ref, w_ref, cw_ref, cb_ref, u_ref, c_ref, a_ref, carry):
        @pl.when(pl.program_id(0) == 0)
        def _():
            carry[...] = jnp.zeros_like(carry)

        hv = h_ref[...]
        for j in range(2):
            cols = slice(j * half, (j + 1) * half)
            conv = []
            for g in range(2):
                ub = _dot(hv, w_ref[2 * g + j]).astype(BF16)
                u_ref[g, :, cols] = ub
                uf = ub.astype(F32)
                e = jnp.concatenate([carry[g, j], uf], axis=0)
                carry[g, j] = uf[tm - SUBLANES:tm]
                conv.append(_conv_taps(e, cw_ref[g][:, cols], cb_ref[g][:, cols])[SUBLANES:])
                c_ref[g, :, cols] = conv[g].astype(BF16)
            a_ref[:, cols] = (conv[0] * _sig(conv[0]) * conv[1]).astype(BF16)

    pair = pl.BlockSpec((2, tm, f), lambda i: (0, i, 0))
    return pl.pallas_call(
        body, name=tag + "_up_convglu", grid=(s // tm,),
        in_specs=[pl.BlockSpec((tm, dm), lambda i: (i, 0)), _full(w_up.shape), _full(conv_w.shape), _full(conv_b.shape)],
        out_specs=[pair, pair, pl.BlockSpec((tm, f), lambda i: (i, 0))],
        out_shape=[jax.ShapeDtypeStruct((2, s, f), BF16)] * 2 + [jax.ShapeDtypeStruct((s, f), BF16)],
        scratch_shapes=[pltpu.VMEM((2, 2, SUBLANES, half), F32)],
    )(h, w_up, conv_w, conv_b)


def _weight_grad_first(a, d, p_n, name):
    return lax.optimization_barrier((_mm_tn(a, d, p_n, name), d))


def _ffn_backward(dx_out, dffn, x_mid, scale, saved, w_up, conv_w, conv_b, w_down, mixer, tag):
    h, u, c, a = saved
    dw_down, dffn = _weight_grad_first(a, dffn, 1, tag + "_down_dw")
    du, dconv = _convglu_bwd(u, c, dffn, w_down, conv_w, tag + "_convglu_bwd")
    dw_up, du = _weight_grad_first(h, du, N_CHIPS, tag + "_up_dw")
    dx_mid, [(dshift, dscale)], dy, dgate_mixer = _premix_bwd(x_mid, [(scale, [(du, w_up)])], dx_out,
                                                              tag + "_premix_bwd", branch=mixer)
    return dx_mid, dy, dgate_mixer, dw_up, dw_down, dict(shift=dshift, scale=dscale, conv=dconv)


def _local_step(x, target, mods, lb, vecs, weights_at):
    m0, m1, mk = mods["l0"], mods["l1"], mods["kv"]
    wts, x = weights_at("mixer0", x)
    h0, proj = _premix_proj(x, m0[0], m0[1], wts["a_w_in"], "l0_premix_in")
    o_a, yp, states = _hgrn_fwd(proj, lb, vecs["a_norm_g"], "l0_hgrn")
    more, yp = weights_at("ffn0", yp)
    wts.update(more)
    y0, x1, [hf0] = _mm_residual_premix(yp, wts["a_w_out"], x, m0[2], [(m0[3], m0[4])], "l0_out")
    u0, c0, a0 = _ffn_inner(hf0, wts["up0"], vecs["conv_w0"], vecs["conv_b0"], "l0_ffn")
    saved0 = (hf0, u0, c0, a0)
    ffn0, x2, [hk, h1] = _mm_residual_premix(a0, wts["down0"], x1, m0[5], [(mk[0], mk[1]), (m1[0], m1[1])],
                                             "l0_ffn_down")
    more, hk = weights_at("layer1", hk)
    wts.update(more)
    k_raw, k_sh, v_sh, f_raw = _kv_proj(hk, wts["kv_k"], wts["kv_v"], wts["kv_f"], vecs["k_norm_g"], "kv_proj")
    qa, ka = _fcum_fwd(f_raw, vecs["kv_b_f"], "kv_fcum")
    q_scale = HEAD_DIM ** -0.5
    qo, q = _proj_headnorm(h1, wts["b_w_q"], vecs["q_norm_g"], q_scale * LOG2E, "l1_q")
    o_b, og, qab = _fox_fwd(q, qa, k_sh, ka, v_sh, qo, "l1_fox")
    y1, x3, [hf1] = _mm_residual_premix(og, wts["b_w_out"], x2, m1[2], [(m1[3], m1[4])], "l1_out")
    u1, c1, a1 = _ffn_inner(hf1, wts["up1"], vecs["conv_w1"], vecs["conv_b1"], "l1_ffn")
    saved1 = (hf1, u1, c1, a1)
    sq, dx4, dffn1, dg2_1 = _mm_loss_head(a1, wts["down1"], x3, m1[5], target, "l1_ffn_down")

    big, small = {}, {}
    dx3, dy1, dg1_1, big["up1"], big["down1"], s_ffn1 = _ffn_backward(
        dx4, dffn1, x3, m1[4], saved1, wts["up1"], vecs["conv_w1"], vecs["conv_b1"], wts["down1"], (y1, m1[2]), "l1_ffn")
    big["b_w_out"], dy1 = _weight_grad_first(og, dy1, 1, "l1_out_dw")
    do_b, dgate_b, doa = _fox_gate_bwd(o_b, qo, dy1, wts["b_w_out"], "l1_out_dx_gate_bwd")
    dk, dv, dka, dq, dcq = _fox_bwd(q, qab, k_sh, ka, v_sh, do_b, doa, "l1_fox_bwd")
    dqo, dqg = _headnorm_bwd(qo, vecs["q_norm_g"], q_scale, dq, "l1_qnorm_bwd", extra=dgate_b)
    big["b_w_q"], dqo = _weight_grad_first(h1, dqo, N_CHIPS, "l1_q_dw")
    dk_raw, dkg = _headnorm_bwd(k_raw, vecs["k_norm_g"], 1.0, dk, "kv_knorm_bwd")
    dz, dbf = _fcum_bwd(f_raw, vecs["kv_b_f"], dka, dcq.reshape(HEADS, 1, -1), "kv_fcum_bwd")
    big["kv_k"], dk_raw = _weight_grad_first(hk, dk_raw, 1, "kv_k_dw")
    big["kv_v"], dv = _weight_grad_first(hk, dv, 1, "kv_v_dw")
    big["kv_f"], dz = _weight_grad_first(hk, dz, 1, "kv_f_dw")
    kv_pairs = [(dk_raw, wts["kv_k"]), (dv, wts["kv_v"]), (dz, wts["kv_f"])]
    dx2, [(dsh1_1, dsc1_1), (dshk, dsck)], dffn0, dg2_0 = _premix_bwd(
        x2, [(m1[1], [(dqo, wts["b_w_q"])]), (mk[1], kv_pairs)], dx3, "l1_kv_premix_bwd", branch=(ffn0, m0[5]))
    dx1, dy0, dg1_0, big["up0"], big["down0"], s_ffn0 = _ffn_backward(
        dx2, dffn0, x1, m0[4], saved0, wts["up0"], vecs["conv_w0"], vecs["conv_b0"], wts["down0"], (y0, m0[2]), "l0_ffn")
    big["a_w_out"], dy0 = _weight_grad_first(yp, dy0, 1, "l0_out_dw")
    dproj, dlb, dng = _hgrn_bwd(proj, lb, vecs["a_norm_g"], o_a, states, dy0, wts["a_w_out"], "l0_out_dx_hgrn_bwd")
    grad_x, [(dsh1_0, dsc1_0)] = _premix_bwd(x, [(m0[1], [(dproj, wts["a_w_in"])])], dx1, "l0_premix_bwd")
    dproj, _ = lax.optimization_barrier((dproj, (dsh1_0, dsc1_0)))
    big["a_w_in"] = _mm_tn(h0, dproj, N_CHIPS, "l0_in_dw")

    small["mod_l0"] = [dsh1_0, dsc1_0, dg1_0, s_ffn0["shift"], s_ffn0["scale"], dg2_0]
    small["mod_l1"] = [dsh1_1, dsc1_1, dg1_1, s_ffn1["shift"], s_ffn1["scale"], dg2_1]
    small["mod_kv"] = [dshk, dsck]
    small["conv0"], small["conv1"] = s_ffn0["conv"], s_ffn1["conv"]
    small["a_norm_g"], small["k_norm_g"], small["q_norm_g"] = dng, dkg, dqg
    small["kv_b_f"], small["lb"] = dbf, dlb
    marks = {"attention_bwd": dk, "ffn0_bwd": dx1, "mixer0_bwd": grad_x}
    return sq, grad_x, big, small, marks


HBM = pl.BlockSpec(memory_space=pltpu.HBM)
COMM_CHUNK_ELEMS = 256 * 1024


def _place():
    x, y, c = lax.axis_index("x"), lax.axis_index("y"), lax.axis_index("c")
    chips = [(1 - x, y), (x, 1 - y), (1 - x, 1 - y)]
    return x, y, c, (x, y, 1 - c), chips


def _chunk_rows(rows, cols):
    best = BF16_ROWS
    for r in range(BF16_ROWS, rows + 1, BF16_ROWS):
        if rows % r == 0 and r * cols <= COMM_CHUNK_ELEMS:
            best = r
    assert rows % best == 0, (rows, cols)
    return best


def _allgather8(block, name):
    m_per, n = block.shape

    def body(x_ref, out_ref, send_sems, recv_sems, local_sem):
        x, y, c, sibling, chips = _place()
        me = (x, y, c)

        def rows(px, py, pc):
            return out_ref.at[pl.ds((4 * px + 2 * py + pc) * m_per, m_per), :]

        def copy(k, blk, to, src=None):
            return pltpu.make_async_remote_copy(
                src_ref=rows(*blk) if src is None else src, dst_ref=rows(*blk),
                send_sem=send_sems.at[k], recv_sem=recv_sems.at[k], device_id=to, device_id_type=MESH)

        mine = pltpu.make_async_copy(x_ref, rows(*me), local_sem)
        mine.start()
        first = [copy(0, me, sibling, src=x_ref)]
        first += [copy(1 + j, me, (*chip, c), src=x_ref) for j, chip in enumerate(chips)]
        for cp in first:
            cp.start()
        passed = [copy(4 + j, (*chip, c), sibling) for j, chip in enumerate(chips)]
        for j, chip in enumerate(chips):
            copy(1 + j, (*chip, c), me).wait_recv()
            passed[j].start()
        copy(0, sibling, me).wait_recv()
        for j, chip in enumerate(chips):
            copy(4 + j, (*chip, 1 - c), me).wait_recv()
        for cp in first + passed:
            cp.wait_send()
        mine.wait()

    return pl.pallas_call(
        body, name=name, out_shape=jax.ShapeDtypeStruct((N_DEV * m_per, n), block.dtype),
        in_specs=[pl.BlockSpec(memory_space=pltpu.VMEM)], out_specs=pl.BlockSpec(memory_space=pltpu.VMEM),
        scratch_shapes=[pltpu.SemaphoreType.DMA((7,)), pltpu.SemaphoreType.DMA((7,)), pltpu.SemaphoreType.DMA],
    )(block)


def _cast_own_block(shards, layer, chip, name):
    _, r, cols = shards.shape
    rows = _chunk_rows(r, cols)

    def body(chip_ref, w_ref, o_ref):
        o_ref[...] = w_ref[...].astype(BF16)

    return pl.pallas_call(
        body, name=name,
        grid_spec=pltpu.PrefetchScalarGridSpec(
            num_scalar_prefetch=1, grid=(r // rows,),
            in_specs=[pl.BlockSpec((None, rows, cols), lambda i, chip_ref: (layer, i, 0))],
            out_specs=pl.BlockSpec((None, rows, cols), lambda i, chip_ref: (chip_ref[0], i, 0))),
        out_shape=jax.ShapeDtypeStruct((N_CHIPS, r, cols), BF16),
    )(chip, shards)


def _sequencer_gather(bufs, name, collective_id):
    n_t = len(bufs)
    dims = [b.shape[1:] for b in bufs]
    refs = [jax.new_ref(b, memory_space=pltpu.MemorySpace.HBM) for b in bufs]

    @pl.kernel(mesh=plsc.ScalarSubcoreMesh(axis_name="sequencer", num_cores=1), name=name,
               scratch_types=[pltpu.SemaphoreType.DMA((n_t,))] * 4,
               compiler_params=pltpu.CompilerParams(collective_id=collective_id))
    def launch(send_ici, recv_ici, send_d2d, recv_d2d):
        x, y, c, sibling, chips = _place()
        p_me = 2 * x + y
        peers = [sibling] + [(cx, cy, c) for cx, cy in chips]
        barrier = pltpu.get_barrier_semaphore()
        for peer in peers:
            pl.semaphore_signal(barrier, inc=1, device_id=peer, device_id_type=MESH)
        pl.semaphore_wait(barrier, len(peers))

        def waiter(t, sem_s, sem_r):
            win = refs[t].at[pl.ds(0, 3), pl.ds(0, dims[t][0] // 2), :]
            return pltpu.make_async_remote_copy(src_ref=win, dst_ref=win, send_sem=sem_s.at[t], recv_sem=sem_r.at[t],
                                                device_id=sibling, device_id_type=MESH)

        def half_copy(t, chip_idx, to, sem_s, sem_r):
            r2 = dims[t][0] // 2
            win = refs[t].at[chip_idx, pl.ds(c * r2, r2), :]
            return pltpu.make_async_remote_copy(src_ref=win, dst_ref=win, send_sem=sem_s.at[t], recv_sem=sem_r.at[t],
                                                device_id=to, device_id_type=MESH)

        for t in range(n_t):
            for cx, cy in chips:
                half_copy(t, p_me, (cx, cy, c), send_ici, recv_ici).start()
        for t in range(n_t):
            waiter(t, send_ici, recv_ici).wait_recv()
            for cx, cy in chips:
                half_copy(t, 2 * cx + cy, sibling, send_d2d, recv_d2d).start()
        for t in range(n_t):
            waiter(t, send_d2d, recv_d2d).wait_recv()
            waiter(t, send_ici, recv_ici).wait_send()
            waiter(t, send_d2d, recv_d2d).wait_send()

    launch()
    return [r[...] for r in refs]


def _sequencer_allgather8(block, dev, name, collective_id):
    m_per, n = block.shape
    src = jax.new_ref(block, memory_space=pltpu.MemorySpace.HBM)
    out = jax.empty_ref(jax.ShapeDtypeStruct((N_DEV * m_per, n), block.dtype), memory_space=pltpu.MemorySpace.HBM)

    @pl.kernel(mesh=plsc.ScalarSubcoreMesh(axis_name="sequencer", num_cores=1), name=name,
               scratch_types=[pltpu.SemaphoreType.DMA((7,))] * 2,
               compiler_params=pltpu.CompilerParams(collective_id=collective_id))
    def launch(send_sems, recv_sems):
        x, y, c, sibling, chips = _place()
        me = (x, y, c)
        _handshake([sibling] + [(cx, cy, c) for cx, cy in chips])

        def rows(px, py, pc):
            return out.at[pl.ds((4 * px + 2 * py + pc) * m_per, m_per), :]

        def copy(k, blk, to, from_src=False):
            return pltpu.make_async_remote_copy(
                src_ref=src if from_src else rows(*blk), dst_ref=rows(*blk),
                send_sem=send_sems.at[k], recv_sem=recv_sems.at[k], device_id=to, device_id_type=MESH)

        first = [copy(0, me, sibling, True)] + [copy(1 + j, me, (*chip, c), True) for j, chip in enumerate(chips)]
        for cp in first:
            cp.start()
        passed = [copy(4 + j, (*chip, c), sibling) for j, chip in enumerate(chips)]
        for j, chip in enumerate(chips):
            copy(1 + j, (*chip, c), me).wait_recv()
            passed[j].start()
        copy(0, sibling, me).wait_recv()
        for j, chip in enumerate(chips):
            copy(4 + j, (*chip, 1 - c), me).wait_recv()
        for cp in first + passed:
            cp.wait_send()

    launch()
    return lax.dynamic_update_slice(out[...], block, (dev * m_per, 0))


def _others():
    x, y, c = lax.axis_index("x"), lax.axis_index("y"), lax.axis_index("c")
    flip = lambda v, f: 1 - v if f else v
    return [(flip(x, fx), flip(y, fy), flip(c, fc))
            for fx in (0, 1) for fy in (0, 1) for fc in (0, 1) if (fx, fy, fc) != (0, 0, 0)]


def _handshake(peers):
    barrier = pltpu.get_barrier_semaphore()
    for peer in peers:
        pl.semaphore_signal(barrier, inc=1, device_id=peer, device_id_type=MESH)
    pl.semaphore_wait(barrier, len(peers))


def _sequencer_scatter(parts, name, collective_id):
    n_t = len(parts)
    dims = [p.shape[1:] for p in parts]
    srcs = [jax.new_ref(p, memory_space=pltpu.MemorySpace.HBM) for p in parts]
    inboxes = [jax.empty_ref(jax.ShapeDtypeStruct((N_DEV, r // 2, cols), BF16), memory_space=pltpu.MemorySpace.HBM)
               for r, cols in dims]

    @pl.kernel(mesh=plsc.ScalarSubcoreMesh(axis_name="sequencer", num_cores=1), name=name,
               scratch_types=[pltpu.SemaphoreType.DMA((n_t,))] * 2,
               compiler_params=pltpu.CompilerParams(collective_id=collective_id))
    def launch(send_sem, recv_sem):
        x, y, c = lax.axis_index("x"), lax.axis_index("y"), lax.axis_index("c")
        me = 4 * x + 2 * y + c
        peers = _others()
        _handshake(peers)
        for t in range(n_t):
            h = dims[t][0] // 2
            for qx, qy, qc in peers:
                pltpu.make_async_remote_copy(
                    src_ref=srcs[t].at[2 * qx + qy, pl.ds(qc * h, h), :], dst_ref=inboxes[t].at[me],
                    send_sem=send_sem.at[t], recv_sem=recv_sem.at[t], device_id=(qx, qy, qc), device_id_type=MESH).start()
        for t in range(n_t):
            win = inboxes[t].at[pl.ds(0, N_DEV - 1)]
            both = pltpu.make_async_remote_copy(src_ref=win, dst_ref=win, send_sem=send_sem.at[t],
                                                recv_sem=recv_sem.at[t], device_id=peers[0], device_id_type=MESH)
            both.wait_recv()
            both.wait_send()

    launch()
    return [b[...] for b in inboxes]


def _sum_pieces(part, inbox, place, name):
    _, r, cols = part.shape
    h = r // 2
    rows = _chunk_rows(h, cols)
    steps = h // rows

    def body(place_ref, own_ref, in_ref, o_ref):
        dev = place_ref[2]
        own = own_ref[...].astype(F32)
        acc = jnp.zeros((rows, cols), F32)
        for d in range(N_DEV):
            acc = acc + jnp.where(dev == d, own, in_ref[d].astype(F32))
        o_ref[...] = acc

    return pl.pallas_call(
        body, name=name,
        grid_spec=pltpu.PrefetchScalarGridSpec(
            num_scalar_prefetch=1, grid=(steps,),
            in_specs=[pl.BlockSpec((None, rows, cols), lambda i, pr: (pr[0], pr[1] * steps + i, 0)),
                      pl.BlockSpec((N_DEV, rows, cols), lambda i, pr: (0, i, 0))],
            out_specs=pl.BlockSpec((rows, cols), lambda i, pr: (pr[1] * steps + i, 0))),
        out_shape=jax.ShapeDtypeStruct((r, cols), F32),
    )(place, part, inbox)


def _sequencer_swap_halves(halves, name, collective_id):
    n_t = len(halves)
    refs = [jax.new_ref(a, memory_space=pltpu.MemorySpace.HBM) for a in halves]

    @pl.kernel(mesh=plsc.ScalarSubcoreMesh(axis_name="sequencer", num_cores=1), name=name,
               scratch_types=[pltpu.SemaphoreType.DMA((n_t,))] * 2,
               compiler_params=pltpu.CompilerParams(collective_id=collective_id))
    def launch(send_sem, recv_sem):
        x, y, c = lax.axis_index("x"), lax.axis_index("y"), lax.axis_index("c")
        sibling = (x, y, 1 - c)
        _handshake([sibling])
        copies = []
        for t in range(n_t):
            h = halves[t].shape[0] // 2
            win = refs[t].at[pl.ds(c * h, h), :]
            copies.append(pltpu.make_async_remote_copy(src_ref=win, dst_ref=win, send_sem=send_sem.at[t],
                                                       recv_sem=recv_sem.at[t], device_id=sibling, device_id_type=MESH))
            copies[-1].start()
        for cp in copies:
            cp.wait()

    launch()
    return [r[...] for r in refs]


def _cond_rows(c16, w, act, name):
    n_l, dm, wid = w.shape

    def body(c_ref, w_ref, o_ref, a_ref):
        cv = c_ref[...]
        if act:
            cv = cv * _sig(cv)
        a_ref[...] = cv
        o_ref[...] = _dot_f32(cv, w_ref[...])

    return pl.pallas_call(
        body, name=name, grid=(n_l,),
        in_specs=[_full((16, dm)), pl.BlockSpec((None, dm, wid), lambda l: (l, 0, 0))],
        out_specs=[pl.BlockSpec((None, 16, wid), lambda l: (l, 0, 0)), _full((16, dm))],
        out_shape=[jax.ShapeDtypeStruct((n_l, 16, wid), F32), jax.ShapeDtypeStruct((16, dm), F32)],
    )(c16, w)


def _outer_grad(ct, dm, name):
    n_l, kk, wid = dm.shape
    d_rows = ct.shape[0]

    def body(c_ref, d_ref, o_ref):
        o_ref[...] = _dot_f32(c_ref[...], d_ref[...])

    return pl.pallas_call(
        body, name=name, grid=(n_l,),
        in_specs=[_full((d_rows, kk)), pl.BlockSpec((None, kk, wid), lambda l: (l, 0, 0))],
        out_specs=pl.BlockSpec((None, d_rows, wid), lambda l: (l, 0, 0)),
        out_shape=jax.ShapeDtypeStruct((n_l, d_rows, wid), F32),
    )(ct, dm)


def _sum_devices(g, name):
    rows, n = g.shape

    def body(g_ref, o_ref):
        acc = g_ref[0:SUBLANES, :]
        for dev in range(1, N_DEV):
            acc = acc + g_ref[dev * SUBLANES:(dev + 1) * SUBLANES, :]
        o_ref[...] = acc

    return pl.pallas_call(body, name=name, out_shape=jax.ShapeDtypeStruct((SUBLANES, n), F32))(g)


def _adamw(w, g, m, v, name):
    shape = w.shape
    cols = shape[-1]
    rows = w.size // cols
    tr = rows
    for cand in range(SUBLANES, min(rows, 256) + 1, SUBLANES):
        if rows % cand == 0:
            tr = cand
    if rows * cols <= COMM_CHUNK_ELEMS:
        tr = rows
    c1 = 1.0 / (1.0 - ADAM_B1 ** ADAM_STEP)
    c2 = 1.0 / (1.0 - ADAM_B2 ** ADAM_STEP)

    def body(w_ref, g_ref, m_ref, v_ref, d_ref, mo_ref, vo_ref):
        gv = g_ref[...]
        m_new = ADAM_B1 * m_ref[...] + (1.0 - ADAM_B1) * gv
        v_new = ADAM_B2 * v_ref[...] + (1.0 - ADAM_B2) * (gv * gv)
        mo_ref[...] = m_new
        vo_ref[...] = v_new
        d_ref[...] = -ADAM_LR * ((m_new * c1) / (jnp.sqrt(v_new * c2) + ADAM_EPS) + ADAM_WD * w_ref[...])

    spec = pl.BlockSpec((tr, cols), lambda i: (i, 0))
    outs = pl.pallas_call(
        body, name=name, grid=(rows // tr,), in_specs=[spec] * 4, out_specs=[spec] * 3,
        out_shape=[jax.ShapeDtypeStruct((rows, cols), F32)] * 3,
    )(*[a.reshape(rows, cols) for a in (w, g, m, v)])
    return tuple(o.reshape(shape) for o in outs)


def _pad_cols(a, cols):
    return jnp.pad(a, [(0, 0)] * (a.ndim - 1) + [(0, cols - a.shape[-1])])


def _flat8(parts, width):
    v = jnp.concatenate([p.reshape(-1) for p in parts])
    return jnp.pad(v, (0, width - v.shape[0])).reshape(SUBLANES, width // SUBLANES)


KV_SHARD = 514
KV_SHARD_PAD = 640
BIG = ("a_w_in", "a_w_out", "kv_w", "b_w_q", "b_w_out", "up0", "up1", "down0", "down1")


def kernel(x, c, ada_w, ada_b, a_w_in, a_lb_logits, a_norm_g, a_w_out, kv_ada_w, kv_ada_b, kv_w, kv_b_f, k_norm_g, b_w_q, q_norm_g, b_w_out, ffn_w_up, ffn_conv_w, ffn_conv_b, ffn_w_down, loss_target, m_ada_w, m_ada_b, m_a_w_in, m_a_lb_logits, m_a_norm_g, m_a_w_out, m_kv_ada_w, m_kv_ada_b, m_kv_w, m_kv_b_f, m_k_norm_g, m_b_w_q, m_q_norm_g, m_b_w_out, m_ffn_w_up, m_ffn_conv_w, m_ffn_conv_b, m_ffn_w_down, v_ada_w, v_ada_b, v_a_w_in, v_a_lb_logits, v_a_norm_g, v_a_w_out, v_kv_ada_w, v_kv_ada_b, v_kv_w, v_kv_b_f, v_k_norm_g, v_b_w_q, v_q_norm_g, v_b_w_out, v_ffn_w_up, v_ffn_conv_w, v_ffn_conv_b, v_ffn_w_down):
    dm, ff = D_MODEL, D_FF
    ix, iy, ic = lax.axis_index("x"), lax.axis_index("y"), lax.axis_index("c")
    chip = 2 * ix + iy
    dev = 2 * chip + ic

    w1 = 10240
    g1 = _allgather8(_flat8([c, a_lb_logits, ffn_conv_w], w1), "gather_cond").reshape(N_DEV, w1)
    c_all = g1[:, :dm]
    per_chip = g1[0::2]
    lb_logits = per_chip[:, dm:dm + 512].reshape(N_CHIPS, 2, 256).transpose(1, 0, 2).reshape(2, dm)
    conv_w = per_chip[:, dm + 512:dm + 512 + 2 * CONV_W * FFN_COLS].reshape(N_CHIPS, 2, CONV_W, FFN_COLS)
    conv_w = conv_w.transpose(1, 2, 0, 3).reshape(2, CONV_W, 2, ff).transpose(0, 2, 1, 3)
    conv_b = ffn_conv_b.reshape(2, 2, 1, ff)
    lb = jax.nn.softmax(lb_logits, axis=0)[0:1]

    c16 = jnp.pad(c_all, ((0, 8), (0, 0)))
    mod_ada, c_act16 = _cond_rows(c16, ada_w, True, "mod_ada")
    mod_kv, _ = _cond_rows(c16, kv_ada_w[None], True, "mod_kv")
    mine = jnp.concatenate([mod_ada[0, :8], mod_ada[1, :8], mod_kv[0, :8]], axis=1)
    w2 = mine.shape[1]
    g2 = _allgather8(mine, "gather_mod").reshape(N_DEV, 8, w2)[0::2]
    my_rows = lax.dynamic_index_in_dim(g2, dev, axis=1, keepdims=False)
    mod0 = my_rows[:, 0:1536].reshape(6 * dm) + ada_b[0]
    mod1 = my_rows[:, 1536:3072].reshape(6 * dm) + ada_b[1]
    modk = my_rows[:, 3072:3584].reshape(2 * dm) + kv_ada_b
    mods = {"l0": [v.reshape(1, dm) for v in jnp.split(mod0, 6)],
            "l1": [v.reshape(1, dm) for v in jnp.split(mod1, 6)],
            "kv": [v.reshape(1, dm) for v in jnp.split(modk, 2)]}

    local = [(a_w_in, 0), (a_w_out, 0), (_pad_cols(kv_w, KV_SHARD_PAD)[None], 0), (b_w_q, 0), (b_w_out, 0),
             (ffn_w_up, 0), (ffn_w_up, 1), (ffn_w_down, 0), (ffn_w_down, 1)]
    chip_arr = chip.reshape(1).astype(jnp.int32)
    local = dict(zip(BIG, local))
    stages = {"mixer0": ("a_w_in",), "ffn0": ("a_w_out", "up0", "down0"),
              "layer1": ("kv_w", "b_w_q", "b_w_out", "up1", "down1")}
    arriving = {}

    def launch(stage, behind):
        shards = [local[n][0] for n in stages[stage]]
        if behind is not None:
            shards, _ = lax.optimization_barrier((shards, behind))
        own = [_cast_own_block(w, local[n][1], chip_arr, "cast_" + n) for n, w in zip(stages[stage], shards)]
        arriving[stage] = _sequencer_gather(own, "gather_" + stage, 1 + list(stages).index(stage))

    launch("mixer0", None)
    launch("ffn0", mod0)
    rowwise = lambda g: g.reshape(1, -1, dm)

    def weights_at(stage, token):
        if stage == "ffn0":
            launch("layer1", token)
        got, token = lax.optimization_barrier((arriving[stage], token))
        g = dict(zip(stages[stage], got))
        if stage == "mixer0":
            return {"a_w_in": g["a_w_in"]}, token
        if stage == "ffn0":
            return {"a_w_out": rowwise(g["a_w_out"]), "up0": g["up0"], "down0": rowwise(g["down0"])}, token
        s0, s1, s2, s3 = (g["kv_w"][p] for p in range(N_CHIPS))
        second = dm - KV_SHARD
        w_k = jnp.concatenate([s0[:, :KV_SHARD], s1[:, :second]], axis=1)
        w_v = jnp.concatenate([s1[:, second:KV_SHARD], s2[:, :KV_SHARD], s3[:, :KV_SHARD - HEADS]], axis=1)
        w_f = _pad_cols(s3[:, KV_SHARD - HEADS:KV_SHARD], LANES)
        return {"kv_k": w_k[None], "kv_v": w_v[None], "kv_f": w_f[None], "b_w_q": g["b_w_q"],
                "b_w_out": rowwise(g["b_w_out"]), "up1": g["up1"], "down1": rowwise(g["down1"])}, token

    vecs = {"a_norm_g": jnp.tile(a_norm_g, (1, HEADS)), "k_norm_g": jnp.tile(k_norm_g[None], (1, HEADS)),
            "q_norm_g": jnp.tile(q_norm_g, (1, HEADS)), "kv_b_f": _pad_cols(kv_b_f[None], LANES),
            "conv_w0": conv_w[0], "conv_b0": conv_b[0], "conv_w1": conv_w[1], "conv_b1": conv_b[1]}

    sq, grad_x, big, small, marks = _local_step(x[0], loss_target[0], mods, lb, vecs, weights_at)
    loss = lax.psum(0.5 * jnp.sum(sq) / dm, ("x", "y", "c"))

    gk, gv, gf = big["kv_k"][0], big["kv_v"][0], big["kv_f"][0][:, :HEADS]
    second = dm - KV_SHARD
    kv_blocks = [gk[:, :KV_SHARD], jnp.concatenate([gk[:, KV_SHARD:], gv[:, :KV_SHARD - second]], axis=1),
                 gv[:, KV_SHARD - second:2 * KV_SHARD - second], jnp.concatenate([gv[:, 2 * KV_SHARD - second:], gf], axis=1)]
    kv_grad = jnp.stack([_pad_cols(b, KV_SHARD_PAD) for b in kv_blocks])
    chipwise = lambda g: g.reshape(N_CHIPS, -1, dm)
    parts = dict(zip(BIG, [big["a_w_in"], chipwise(big["a_w_out"]), kv_grad, big["b_w_q"], chipwise(big["b_w_out"]),
                           big["up0"], big["up1"], chipwise(big["down0"]), chipwise(big["down1"])]))
    place = jnp.stack([chip, ic, dev]).astype(jnp.int32)

    served = []
    boxes = {}

    groups = (("up1", "down1"), ("b_w_out", "b_w_q", "kv_w"), ("up0", "down0", "a_w_out"), ("a_w_in",))

    def scatter_group(k):
        mine = [parts[n] for n in groups[k]]
        if served:
            mine, _ = lax.optimization_barrier((mine, served[-1]))
        boxes[k] = _sequencer_scatter(mine, "scatter_grads_%d" % k, 4 + k)
        served.append(boxes[k])

    def sum_group(k, token):
        inboxes, _ = lax.optimization_barrier((boxes[k], token))
        return [_sum_pieces(parts[n], box, place, "sum_" + n) for n, box in zip(groups[k], inboxes)]

    def swap_group(k, halves, behind):
        halves, _ = lax.optimization_barrier((halves, behind))
        return dict(zip(groups[k], _sequencer_swap_halves(halves, "swap_grads_%d" % k, 8 + k)))

    for k in range(3):
        scatter_group(k)
    halves = [sum_group(0, marks["attention_bwd"]), sum_group(1, marks["ffn0_bwd"]), sum_group(2, marks["mixer0_bwd"])]

    fold = lambda a: a.sum(axis=0)
    heads = lambda a: fold(a).reshape(HEADS, HEAD_DIM).sum(axis=0)
    conv_flat = lambda a: a.sum(axis=2).transpose(1, 0, 2)
    pieces = ([fold(a) for a in small["mod_l0"]] + [fold(a) for a in small["mod_l1"]] + [fold(a) for a in small["mod_kv"]]
              + [conv_flat(small["conv0"]), conv_flat(small["conv1"]), heads(small["a_norm_g"]), heads(small["k_norm_g"]),
                 heads(small["q_norm_g"]), fold(small["kv_b_f"]), fold(small["lb"])])
    w3 = 61440
    small_vec, _ = lax.optimization_barrier((_flat8(pieces, w3), served[2]))
    g3 = _sequencer_allgather8(small_vec, dev, "gather_small", 12)
    served.append(g3)
    scatter_group(3)
    rs = {}
    for k in range(3):
        rs.update(swap_group(k, halves[k], g3))
    tot = _sum_devices(g3, "sum_small").reshape(w3)
    n_mod = 14 * dm
    dmod_all = g3.reshape(N_DEV, w3)[:, :n_mod]
    o = n_mod
    conv_tot = [tot[o + l * 8 * ff: o + (l + 1) * 8 * ff].reshape(4, 2 * ff) for l in range(2)]
    o += 16 * ff
    g_a_norm, g_k_norm, g_q_norm = (tot[o + i * HEAD_DIM: o + (i + 1) * HEAD_DIM] for i in range(3))
    o += 3 * HEAD_DIM
    g_kv_b_f = tot[o:o + HEADS]
    dlb = tot[o + LANES:o + LANES + dm]

    ct = _pad_cols(c_act16[:8].T, LANES)
    dmod_pad = jnp.pad(dmod_all, ((0, LANES - N_DEV), (0, 0)))
    cols_ada = jnp.stack([lax.dynamic_slice_in_dim(dmod_pad, l * 6 * dm + chip * 1536, 1536, axis=1) for l in range(2)])
    cols_kv = lax.dynamic_slice_in_dim(dmod_pad, 12 * dm + chip * 512, 512, axis=1)[None]
    g_ada_w = _outer_grad(ct, cols_ada, "grad_ada_w")
    g_kv_ada_w = _outer_grad(ct, cols_kv, "grad_kv_ada_w")[0]

    my_lb = lax.dynamic_slice_in_dim(lb[0], chip * 256, 256)
    l0 = lax.dynamic_slice_in_dim(dlb, chip * 256, 256) * my_lb * (1.0 - my_lb)
    grads = {
        "ada_w": g_ada_w, "ada_b": jnp.stack([tot[:6 * dm], tot[6 * dm:12 * dm]]),
        "a_lb_logits": jnp.stack([l0, -l0]), "a_norm_g": g_a_norm[None],
        "a_w_out": rs["a_w_out"][None], "kv_ada_w": g_kv_ada_w, "kv_ada_b": tot[12 * dm:14 * dm],
        "kv_w": rs["kv_w"][:, :KV_SHARD], "kv_b_f": g_kv_b_f, "k_norm_g": g_k_norm,
        "b_w_q": rs["b_w_q"][None], "q_norm_g": g_q_norm[None], "b_w_out": rs["b_w_out"][None],
        "ffn_w_up": jnp.stack([rs["up0"], rs["up1"]]),
        "ffn_conv_w": jnp.stack([lax.dynamic_slice_in_dim(ct_l[:CONV_W], chip * FFN_COLS, FFN_COLS, axis=1) for ct_l in conv_tot]),
        "ffn_conv_b": jnp.stack([ct_l[CONV_W] for ct_l in conv_tot]),
        "ffn_w_down": jnp.stack([rs["down0"], rs["down1"]]),
    }
    weights = dict(ada_w=ada_w, ada_b=ada_b, a_w_in=a_w_in, a_lb_logits=a_lb_logits, a_norm_g=a_norm_g, a_w_out=a_w_out,
                   kv_ada_w=kv_ada_w, kv_ada_b=kv_ada_b, kv_w=kv_w, kv_b_f=kv_b_f, k_norm_g=k_norm_g, b_w_q=b_w_q,
                   q_norm_g=q_norm_g, b_w_out=b_w_out, ffn_w_up=ffn_w_up, ffn_conv_w=ffn_conv_w, ffn_conv_b=ffn_conv_b,
                   ffn_w_down=ffn_w_down)
    m_in = dict(ada_w=m_ada_w, ada_b=m_ada_b, a_w_in=m_a_w_in, a_lb_logits=m_a_lb_logits, a_norm_g=m_a_norm_g,
                a_w_out=m_a_w_out, kv_ada_w=m_kv_ada_w, kv_ada_b=m_kv_ada_b, kv_w=m_kv_w, kv_b_f=m_kv_b_f,
                k_norm_g=m_k_norm_g, b_w_q=m_b_w_q, q_norm_g=m_q_norm_g, b_w_out=m_b_w_out, ffn_w_up=m_ffn_w_up,
                ffn_conv_w=m_ffn_conv_w, ffn_conv_b=m_ffn_conv_b, ffn_w_down=m_ffn_w_down)
    v_in = dict(ada_w=v_ada_w, ada_b=v_ada_b, a_w_in=v_a_w_in, a_lb_logits=v_a_lb_logits, a_norm_g=v_a_norm_g,
                a_w_out=v_a_w_out, kv_ada_w=v_kv_ada_w, kv_ada_b=v_kv_ada_b, kv_w=v_kv_w, kv_b_f=v_kv_b_f,
                k_norm_g=v_k_norm_g, b_w_q=v_b_w_q, q_norm_g=v_q_norm_g, b_w_out=v_b_w_out, ffn_w_up=v_ffn_w_up,
                ffn_conv_w=v_ffn_conv_w, ffn_conv_b=v_ffn_conv_b, ffn_w_down=v_ffn_w_down)

    names = list(weights)
    step = lambda n: _adamw(weights[n], grads[n], m_in[n], v_in[n], "adamw_" + n)
    grads = {n: g.reshape(weights[n].shape) for n, g in grads.items()}
    upd = {n: step(n) for n in names if n != "a_w_in"}
    last = sum_group(3, [u[0] for u in upd.values()])
    grads["a_w_in"] = swap_group(3, last, last)["a_w_in"][None]
    upd["a_w_in"] = step("a_w_in")
    return (loss, grad_x[None], *[grads[n] for n in names], *[upd[n][0] for n in names],
            *[upd[n][1] for n in names], *[upd[n][2] for n in names])
```

```python
import jax
import jax.numpy as jnp
from jax import lax
from jax.experimental import pallas as pl
from jax.experimental.pallas import tpu as pltpu
from jax.experimental.pallas import tpu_sc as plsc

F32 = jnp.float32
BF16 = jnp.bfloat16

D_MODEL = 1024
HEADS = 8
HEAD_DIM = 128
A_CHUNK = 64
D_FF = 2816
CONV_W = 3
EPS = 1e-6
NEG_INF = -1e30
N_CHIPS = 4
N_DEV = 8

ADAM_LR = 0.001
ADAM_B1 = 0.9
ADAM_B2 = 0.999
ADAM_EPS = 1e-08
ADAM_WD = 0.01
ADAM_STEP = 10

SUBLANES = 8
BF16_ROWS = 16
LANES = 128
HALO = BF16_ROWS
ROW_TILE = 512
TOKEN_TILE_TN = 2048
FFN_COLS = 1408
FFN_ROWS = 256
HGRN_ROWS = 256
ATT_TILE = 512
ATT_SPLIT = 2
ATT_FWD_HEADS = 8
ATT_BWD_HEADS = 8
MESH = pl.DeviceIdType.MESH


def _sig(x):
    return jax.nn.sigmoid(x)


def _dot(a, b):
    return jnp.dot(a, b, preferred_element_type=F32)


def _dot_nt(a, b):
    return lax.dot_general(a, b, (((1,), (1,)), ((), ())), preferred_element_type=F32)


def _dot_tn(a, b):
    return lax.dot_general(a, b, (((0,), (0,)), ((), ())), preferred_element_type=F32)


def _split2(x):
    hi = x.astype(BF16)
    lo = (x - hi.astype(F32)).astype(BF16)
    return hi, lo


def _dot_f32(a, b):
    ah, al = _split2(a)
    bh, bl = _split2(b)
    return _dot(ah, bh) + _dot(ah, bl) + _dot(al, bh)


def _tri_dot(tri, x):
    hi = x.astype(BF16)
    r = x - hi.astype(F32)
    mid = r.astype(BF16)
    lo = (r - mid.astype(F32)).astype(BF16)
    return _dot(tri, hi) + _dot(tri, mid) + _dot(tri, lo)


def _tri(n, upper=False):
    r = lax.broadcasted_iota(jnp.int32, (n, n), 0)
    c = lax.broadcasted_iota(jnp.int32, (n, n), 1)
    keep = (c >= r) if upper else (c <= r)
    return jnp.where(keep, 1.0, 0.0).astype(BF16)


def _colsum8(v):
    rows, n = v.shape
    return v.reshape(rows // SUBLANES, SUBLANES, n).sum(axis=0)


def _full(shape):
    nd = len(shape)
    return pl.BlockSpec(shape, lambda *_: (0,) * nd)


def _tile(n, want):
    t = min(n, want)
    assert n % t == 0, (n, t)
    return t


def _mm_tn(a, d, p_n, name):
    m_rows, k = a.shape
    g_n, _, w_cols = d.shape
    per = p_n // g_n
    n = w_cols // per
    tm = _tile(m_rows, TOKEN_TILE_TN if k <= D_MODEL else ROW_TILE)
    steps = m_rows // tm

    def body(a_ref, d_ref, o_ref, acc):
        m = pl.program_id(1)

        @pl.when(m == 0)
        def _():
            acc[...] = jnp.zeros_like(acc)

        acc[...] += _dot_tn(a_ref[...], d_ref[...])

        @pl.when(m == steps - 1)
        def _():
            o_ref[...] = acc[...].astype(BF16)

    return pl.pallas_call(
        body, name=name, grid=(p_n, steps),
        in_specs=[pl.BlockSpec((tm, k), lambda p, m: (m, 0)),
                  pl.BlockSpec((None, tm, n), lambda p, m: (p // per, m, p % per))],
        out_specs=pl.BlockSpec((None, k, n), lambda p, m: (p, 0, 0)),
        out_shape=jax.ShapeDtypeStruct((p_n, k, n), BF16),
        scratch_shapes=[pltpu.VMEM((k, n), F32)],
    )(a, d)


def _premix_proj(x, shift, scale, w, name):
    s, dm = x.shape
    p_n, _, n = w.shape
    tm = _tile(s, ROW_TILE)

    def body(x_ref, sh_ref, sc_ref, w_ref, h_ref, o_ref):
        xv = x_ref[...]
        inv = lax.rsqrt(jnp.mean(xv * xv, axis=-1, keepdims=True) + EPS)
        h = (xv * inv * (1.0 + sc_ref[...]) + sh_ref[...]).astype(BF16)
        h_ref[...] = h
        for p in range(p_n):
            o_ref[:, p * n:(p + 1) * n] = _dot(h, w_ref[p])

    row = pl.BlockSpec((tm, dm), lambda i: (i, 0))
    vec = _full((1, dm))
    return pl.pallas_call(
        body, name=name, grid=(s // tm,), in_specs=[row, vec, vec, _full(w.shape)],
        out_specs=[row, pl.BlockSpec((tm, p_n * n), lambda i: (i, 0))],
        out_shape=[jax.ShapeDtypeStruct((s, dm), BF16), jax.ShapeDtypeStruct((s, p_n * n), F32)],
    )(x, shift, scale, w)


def _premix_bwd(x, terms, dres, name, branch=None):
    s, dm = x.shape
    tm = _tile(s, ROW_TILE)
    pairs = [pr for _, prs in terms for pr in prs]
    n_in = 2 + len(terms) + 2 * len(pairs) + (2 if branch else 0)

    def body(*refs):
        x_ref, dres_ref = refs[:2]
        sc_refs = refs[2:2 + len(terms)]
        mm_refs = refs[2 + len(terms):2 + len(terms) + 2 * len(pairs)]
        outs = refs[n_in:]

        @pl.when(pl.program_id(0) == 0)
        def _():
            for o in outs[1:1 + 2 * len(terms)]:
                o[...] = jnp.zeros_like(o)
            if branch:
                outs[-1][...] = jnp.zeros_like(outs[-1])

        xv = x_ref[...]
        inv = lax.rsqrt(jnp.mean(xv * xv, axis=-1, keepdims=True) + EPS)
        r = xv * inv
        dx = dres_ref[...]
        k = 0
        for t, (_, prs) in enumerate(terms):
            dh = None
            for d, w in prs:
                d_ref, w_ref = mm_refs[2 * k], mm_refs[2 * k + 1]
                k += 1
                p_n, _, n = w.shape
                per = p_n // d.shape[0]
                for p in range(p_n):
                    part = _dot_nt(d_ref[p // per, :, (p % per) * n:(p % per + 1) * n], w_ref[p])
                    dh = part if dh is None else dh + part
            dr = dh * (1.0 + sc_refs[t][...])
            dx = dx + inv * (dr - r * jnp.mean(dr * r, axis=-1, keepdims=True))
            outs[1 + 2 * t][...] += _colsum8(dh)
            outs[2 + 2 * t][...] += _colsum8(dh * r)
        outs[0][...] = dx
        if branch:
            y_ref, g_ref = refs[n_in - 2:n_in]
            outs[-2][0] = (dx * g_ref[...]).astype(BF16)
            outs[-1][...] += _colsum8(dx * y_ref[...])

    row = pl.BlockSpec((tm, dm), lambda i: (i, 0))
    vec, acc = _full((1, dm)), _full((SUBLANES, dm))
    ins, specs = [x, dres] + [sc for sc, _ in terms], [row, row] + [vec] * len(terms)
    for d, w in pairs:
        ins += [d, w]
        specs += [pl.BlockSpec((d.shape[0], tm, d.shape[2]), lambda i: (0, i, 0)), _full(w.shape)]
    out_shape = [jax.ShapeDtypeStruct((s, dm), F32)] + [jax.ShapeDtypeStruct((SUBLANES, dm), F32)] * (2 * len(terms))
    out_specs = [row] + [acc] * (2 * len(terms))
    if branch:
        ins += list(branch)
        specs += [row, vec]
        out_shape += [jax.ShapeDtypeStruct((1, s, dm), BF16), jax.ShapeDtypeStruct((SUBLANES, dm), F32)]
        out_specs += [pl.BlockSpec((1, tm, dm), lambda i: (0, i, 0)), acc]
    outs = pl.pallas_call(body, name=name, grid=(s // tm,), in_specs=specs, out_specs=out_specs,
                          out_shape=out_shape)(*ins)
    partials = [(outs[1 + 2 * t], outs[2 + 2 * t]) for t in range(len(terms))]
    return (outs[0], partials) + ((outs[-2], outs[-1]) if branch else ())


def _conv_taps(e, w, b):
    return w[2:3] * e + w[1:2] * pltpu.roll(e, 1, 0) + w[0:1] * pltpu.roll(e, 2, 0) + b


def _ffn_specs(s, tm, cb):
    hb = tm // HALO
    last = s // HALO - 1
    main = pl.BlockSpec((2, tm, cb), lambda j, i: (0, i, j))
    prev = pl.BlockSpec((2, HALO, cb), lambda j, i: (0, jnp.maximum(i * hb - 1, 0), j))
    nxt = pl.BlockSpec((2, HALO, cb), lambda j, i: (0, jnp.minimum((i + 1) * hb, last), j))
    wspec = pl.BlockSpec((2, CONV_W, cb), lambda j, i: (0, 0, j))
    bspec = pl.BlockSpec((2, 1, cb), lambda j, i: (0, 0, j))
    return main, prev, nxt, wspec, bspec


def _convglu_bwd(u, c, dffn, w_down, w, name):
    _, s, f = u.shape
    dm = dffn.shape[2]
    tm = _tile(s, 256)
    cb = _tile(f, FFN_COLS)
    steps = s // tm
    n_ext = tm + HALO
    main, _, nxt, wspec, _ = _ffn_specs(s, tm, cb)
    hb = tm // HALO
    last = s // HALO - 1
    d_main = pl.BlockSpec((None, tm, dm), lambda j, i: (0, i, 0))
    d_next = pl.BlockSpec((None, HALO, dm), lambda j, i: (0, jnp.minimum((i + 1) * hb, last), 0))
    wd_spec = pl.BlockSpec((None, cb, dm), lambda j, i: (0, j, 0))

    def body(u_ref, c_ref, cn_ref, d_ref, dn_ref, wd_ref, w_ref, du_ref, acc_ref):
        i = pl.program_id(1)
        notlast = jnp.where(i < steps - 1, 1.0, 0.0)

        @pl.when(i == 0)
        def _():
            acc_ref[...] = jnp.zeros_like(acc_ref)

        gate, val = (jnp.concatenate([c_ref[g].astype(F32), cn_ref[g].astype(F32)], axis=0) for g in range(2))
        wd = wd_ref[...]
        da = jnp.concatenate([_dot_nt(d_ref[...], wd).astype(BF16).astype(F32),
                              _dot_nt(dn_ref[...], wd).astype(BF16).astype(F32) * notlast], axis=0)
        sg = _sig(gate)
        d_val = da * gate * sg
        d_gate = da * val * (sg * (1.0 + gate * (1.0 - sg)))

        def finish(g, d):
            wv = w_ref[g]
            d1, d2 = pltpu.roll(d, n_ext - 1, 0), pltpu.roll(d, n_ext - 2, 0)
            du_ref[g] = (wv[2:3] * d + wv[1:2] * d1 + wv[0:1] * d2)[0:tm].astype(BF16)
            uv = u_ref[g].astype(F32)
            acc_ref[g, 2] += _colsum8(d[0:tm] * uv)
            acc_ref[g, 1] += _colsum8(d1[0:tm] * uv)
            acc_ref[g, 0] += _colsum8(d2[0:tm] * uv)
            acc_ref[g, 3] += _colsum8(d[0:tm])

        finish(0, d_gate)
        finish(1, d_val)

    return pl.pallas_call(
        body, name=name, grid=(f // cb, steps),
        in_specs=[main, main, nxt, d_main, d_next, wd_spec, wspec],
        out_specs=[main, pl.BlockSpec((2, 4, SUBLANES, cb), lambda j, i: (0, 0, 0, j))],
        out_shape=[jax.ShapeDtypeStruct((2, s, f), BF16), jax.ShapeDtypeStruct((2, 4, SUBLANES, f), F32)],
    )(u, c, c, dffn, dffn, w_down, w)


def _hgrn_gates(q_raw, f_raw, lb, tri):
    sf = _sig(f_raw)
    fg = lb + (1.0 - lb) * sf
    b = _tri_dot(tri, jnp.log(fg))
    return q_raw * _sig(q_raw), 1.0 - fg, b, fg, sf


def _hgrn_fwd(proj, lb, norm_g, name):
    s = proj.shape[0]
    tb = _tile(s, HGRN_ROWS)
    n_c = tb // A_CHUNK
    half = A_CHUNK // 2

    def body(q_ref, f_ref, v_ref, g_ref, lb_ref, ng_ref, o_ref, yp_ref, st_ref, state):
        @pl.when(pl.program_id(0) == 0)
        def _():
            state[...] = jnp.zeros_like(state)

        tri = _tri(A_CHUNK)
        causal = lax.broadcasted_iota(jnp.int32, (A_CHUNK, A_CHUNK), 1) <= lax.broadcasted_iota(
            jnp.int32, (A_CHUNK, A_CHUNK), 0)

        def chunk(ci, carry):
            rows = pl.ds(pl.multiple_of(ci * A_CHUNK, A_CHUNK), A_CHUNK)
            heads = [slice(h * HEAD_DIM, (h + 1) * HEAD_DIM) for h in range(HEADS)]
            qs, k, b, _, _ = _hgrn_gates(q_ref[rows, :], f_ref[rows, :], lb_ref[...], tri)
            b_mid, b_last = b[half:half + 1], b[A_CHUNK - 1:A_CHUNK]
            q_i = (qs * jnp.exp(b - b_mid)).astype(BF16)
            k_i = (k * jnp.exp(b_mid - b)).astype(BF16)
            q_e = (qs * jnp.exp(b)).astype(BF16)
            k_s = (k * jnp.exp(b_last - b)).astype(BF16)
            decay = jnp.exp(b_last)
            vb = v_ref[rows, :].astype(BF16)
            scores = [jnp.where(causal, _dot_nt(q_i[:, cs], k_i[:, cs]), 0.0).astype(BF16) for cs in heads]
            st = [state[h] for h in range(HEADS)]
            outs = [_dot(scores[h], vb[:, cs]) + _dot_nt(q_e[:, cs], st[h].astype(BF16)) for h, cs in enumerate(heads)]
            for h, cs in enumerate(heads):
                st_ref[ci, h] = st[h]
                state[h] = st[h] * decay[:, cs] + _dot_tn(vb[:, cs], k_s[:, cs])
            o = jnp.concatenate(outs, axis=1)
            o_ref[rows, :] = o
            sq = o * o
            inv = jnp.concatenate([jnp.broadcast_to(lax.rsqrt(jnp.mean(sq[:, cs], axis=-1, keepdims=True) + EPS),
                                                    (A_CHUNK, HEAD_DIM)) for cs in heads], axis=1)
            g_raw = g_ref[rows, :]
            yp_ref[rows, :] = (o * inv * ng_ref[...] * (g_raw * _sig(g_raw))).astype(BF16)
            return carry

        lax.fori_loop(0, n_c, chunk, 0)

    col = lambda j: pl.BlockSpec((tb, D_MODEL), lambda i: (i, j))
    vec = _full((1, D_MODEL))
    return pl.pallas_call(
        body, name=name, grid=(s // tb,), in_specs=[col(0), col(1), col(2), col(3), vec, vec],
        out_specs=[col(0), col(0), pl.BlockSpec((n_c, HEADS, HEAD_DIM, HEAD_DIM), lambda i: (i, 0, 0, 0))],
        out_shape=[jax.ShapeDtypeStruct((s, D_MODEL), F32), jax.ShapeDtypeStruct((s, D_MODEL), BF16),
                   jax.ShapeDtypeStruct((s // A_CHUNK, HEADS, HEAD_DIM, HEAD_DIM), F32)],
        scratch_shapes=[pltpu.VMEM((HEADS, HEAD_DIM, HEAD_DIM), F32)],
    )(proj, proj, proj, proj, lb, norm_g)


def _hgrn_bwd(proj, lb, norm_g, o, states, dout, w_out, name):
    s = proj.shape[0]
    tb = _tile(s, HGRN_ROWS)
    n_c = tb // A_CHUNK
    n_b = s // tb
    half = A_CHUNK // 2

    def body(q_ref, f_ref, v_ref, g_ref, lb_ref, ng_ref, o_ref, st_ref, dout_ref, w_ref, dp_ref, dlb_ref, dng_ref,
             dstate, dyp_ref):
        @pl.when(pl.program_id(0) == 0)
        def _():
            dstate[...] = jnp.zeros_like(dstate)
            dlb_ref[...] = jnp.zeros_like(dlb_ref)
            dng_ref[...] = jnp.zeros_like(dng_ref)

        dyp_ref[...] = _dot_nt(dout_ref[0], w_ref[0])

        tri = _tri(A_CHUNK)
        tri_up = _tri(A_CHUNK, upper=True)
        row_id = lax.broadcasted_iota(jnp.int32, (A_CHUNK, D_MODEL), 0)
        causal = lax.broadcasted_iota(jnp.int32, (A_CHUNK, A_CHUNK), 1) <= lax.broadcasted_iota(
            jnp.int32, (A_CHUNK, A_CHUNK), 0)

        def chunk(cj, carry):
            ci = n_c - 1 - cj
            rows = pl.ds(pl.multiple_of(ci * A_CHUNK, A_CHUNK), A_CHUNK)
            heads = [slice(h * HEAD_DIM, (h + 1) * HEAD_DIM) for h in range(HEADS)]
            cat = lambda parts: jnp.concatenate(parts, axis=1)
            per_head_mean = lambda a: cat([jnp.broadcast_to(jnp.mean(a[:, cs], axis=-1, keepdims=True),
                                                            (A_CHUNK, HEAD_DIM)) for cs in heads])
            q_raw, lbv = q_ref[rows, :], lb_ref[...]
            qs, k, b, fg, sf = _hgrn_gates(q_raw, f_ref[rows, :], lbv, tri)
            b_mid, b_last = b[half:half + 1], b[A_CHUNK - 1:A_CHUNK]
            e_qi, e_ki, e_q, e_ks = jnp.exp(b - b_mid), jnp.exp(b_mid - b), jnp.exp(b), jnp.exp(b_last - b)
            decay = jnp.exp(b_last)
            q_i, k_i, q_e, k_s = qs * e_qi, k * e_ki, qs * e_q, k * e_ks
            qib, kib, qeb, ksb = q_i.astype(BF16), k_i.astype(BF16), q_e.astype(BF16), k_s.astype(BF16)
            vb = v_ref[rows, :].astype(BF16)
            ov, g_raw, dy, ng = o_ref[rows, :], g_ref[rows, :], dyp_ref[rows, :], ng_ref[...]
            inv = lax.rsqrt(per_head_mean(ov * ov) + EPS)
            nrm = ov * inv
            sg = _sig(g_raw)
            gs = g_raw * sg
            dn = dy * ng * gs
            dng_ref[0:1, :] += jnp.sum(dy * nrm * gs, axis=0, keepdims=True)
            dg_raw = dy * nrm * ng * (sg * (1.0 + g_raw * (1.0 - sg)))
            do = (inv * (dn - nrm * per_head_mean(dn * nrm))).astype(BF16)
            st_prev = [st_ref[ci, h] for h in range(HEADS)]
            dst = [dstate[h] for h in range(HEADS)]
            dstb = [d.astype(BF16) for d in dst]
            scores = [jnp.where(causal, _dot_nt(qib[:, cs], kib[:, cs]), 0.0).astype(BF16) for cs in heads]
            d_scores = [jnp.where(causal, _dot_nt(do[:, cs], vb[:, cs]), 0.0).astype(BF16) for cs in heads]
            dv = cat([_dot_tn(scores[h], do[:, cs]) + _dot_nt(ksb[:, cs], dstb[h]) for h, cs in enumerate(heads)])
            dq_i = cat([_dot(d_scores[h], kib[:, cs]) for h, cs in enumerate(heads)])
            dk_i = cat([_dot_tn(d_scores[h], qib[:, cs]) for h, cs in enumerate(heads)])
            dq_e = cat([_dot(do[:, cs], st_prev[h].astype(BF16)) for h, cs in enumerate(heads)])
            dk_s = cat([_dot(vb[:, cs], dstb[h]) for h, cs in enumerate(heads)])
            d_decay = cat([jnp.sum(st_prev[h] * dst[h], axis=0, keepdims=True) for h in range(HEADS)])
            for h, cs in enumerate(heads):
                dstate[h] = dst[h] * decay[:, cs] + _dot_tn(do[:, cs], qeb[:, cs])
            dq = dq_i * e_qi + dq_e * e_q
            dk = dk_i * e_ki + dk_s * e_ks
            t_qi, t_ki, t_ks = dq_i * q_i, dk_i * k_i, dk_s * k_s
            db = t_qi - t_ki + dq_e * q_e - t_ks
            db_mid = jnp.sum(t_ki - t_qi, axis=0, keepdims=True)
            db_last = jnp.sum(t_ks, axis=0, keepdims=True) + d_decay * decay
            db = db + jnp.where(row_id == half, db_mid, 0.0) + jnp.where(row_id == A_CHUNK - 1, db_last, 0.0)
            dfg = _tri_dot(tri_up, db) / fg - dk
            dlb_ref[0:1, :] += jnp.sum(dfg * (1.0 - sf), axis=0, keepdims=True)
            sq = _sig(q_raw)
            dp_ref[0, rows, :] = (dq * (sq * (1.0 + q_raw * (1.0 - sq)))).astype(BF16)
            dp_ref[1, rows, :] = (dfg * (1.0 - lbv) * sf * (1.0 - sf)).astype(BF16)
            dp_ref[2, rows, :] = dv.astype(BF16)
            dp_ref[3, rows, :] = dg_raw.astype(BF16)
            return carry

        lax.fori_loop(0, n_c, chunk, 0)

    col = lambda j: pl.BlockSpec((tb, D_MODEL), lambda i: (n_b - 1 - i, j))
    vec = _full((1, D_MODEL))
    acc = _full((SUBLANES, D_MODEL))
    return pl.pallas_call(
        body, name=name, grid=(n_b,),
        in_specs=[col(0), col(1), col(2), col(3), vec, vec, col(0),
                  pl.BlockSpec((n_c, HEADS, HEAD_DIM, HEAD_DIM), lambda i: (n_b - 1 - i, 0, 0, 0)),
                  pl.BlockSpec((1, tb, D_MODEL), lambda i: (0, n_b - 1 - i, 0)), _full(w_out.shape)],
        out_specs=[pl.BlockSpec((4, tb, D_MODEL), lambda i: (0, n_b - 1 - i, 0)), acc, acc],
        out_shape=[jax.ShapeDtypeStruct((4, s, D_MODEL), BF16), jax.ShapeDtypeStruct((SUBLANES, D_MODEL), F32),
                   jax.ShapeDtypeStruct((SUBLANES, D_MODEL), F32)],
        scratch_shapes=[pltpu.VMEM((HEADS, HEAD_DIM, HEAD_DIM), F32), pltpu.VMEM((tb, D_MODEL), F32)],
    )(proj, proj, proj, proj, lb, norm_g, o, states, dout, w_out)


def _head_rms(raw_ref, g_ref, mult, y_ref):
    for h in range(HEADS):
        cs = slice(h * HEAD_DIM, (h + 1) * HEAD_DIM)
        xv = raw_ref[:, cs]
        inv = lax.rsqrt(jnp.mean(xv * xv, axis=-1, keepdims=True) + EPS)
        y_ref[:, cs] = (xv * inv * g_ref[:, cs] * mult).astype(BF16)


def _proj_headnorm(a, w, g, mult, name):
    s, k = a.shape
    p_n, _, n = w.shape
    tm = _tile(s, ROW_TILE)

    def body(a_ref, w_ref, g_ref, raw_ref, y_ref):
        av = a_ref[...]
        for p in range(p_n):
            raw_ref[:, p * n:(p + 1) * n] = _dot(av, w_ref[p])
        _head_rms(raw_ref, g_ref, mult, y_ref)

    row = lambda wid: pl.BlockSpec((tm, wid), lambda i: (i, 0))
    return pl.pallas_call(
        body, name=name, grid=(s // tm,), in_specs=[row(k), _full(w.shape), _full((1, D_MODEL))],
        out_specs=[row(p_n * n), row(D_MODEL)],
        out_shape=[jax.ShapeDtypeStruct((s, p_n * n), F32), jax.ShapeDtypeStruct((s, D_MODEL), BF16)],
    )(a, w, g)


def _kv_proj(hk, w_k, w_v, w_f, g, name):
    s, k = hk.shape
    tm = _tile(s, ROW_TILE)

    def body(h_ref, wk_ref, wv_ref, wf_ref, g_ref, kr_ref, k_ref, v_ref, f_ref):
        hv = h_ref[...]
        kr_ref[...] = _dot(hv, wk_ref[0])
        v_ref[...] = _dot(hv, wv_ref[0]).astype(BF16)
        f_ref[...] = _dot(hv, wf_ref[0])
        _head_rms(kr_ref, g_ref, 1.0, k_ref)

    row = lambda wid: pl.BlockSpec((tm, wid), lambda i: (i, 0))
    return pl.pallas_call(
        body, name=name, grid=(s // tm,),
        in_specs=[row(k), _full(w_k.shape), _full(w_v.shape), _full(w_f.shape), _full((1, D_MODEL))],
        out_specs=[row(D_MODEL), row(D_MODEL), row(D_MODEL), row(LANES)],
        out_shape=[jax.ShapeDtypeStruct((s, D_MODEL), F32), jax.ShapeDtypeStruct((s, D_MODEL), BF16),
                   jax.ShapeDtypeStruct((s, D_MODEL), BF16), jax.ShapeDtypeStruct((s, LANES), F32)],
    )(hk, w_k, w_v, w_f, g)


def _headnorm_bwd(x, g, mult, dy, name, col0=0, extra=None):
    s = x.shape[0]
    tm = _tile(s, ROW_TILE)
    groups = 2 if extra is not None else 1
    head_major = dy.ndim == 3

    def body(*refs):
        x_ref, g_ref, dy_ref = refs[:3]
        dx_ref, dg_ref = refs[-2:]

        @pl.when(pl.program_id(0) == 0)
        def _():
            dg_ref[...] = jnp.zeros_like(dg_ref)

        for h in range(HEADS):
            cs = slice(h * HEAD_DIM, (h + 1) * HEAD_DIM)
            xv, gv = x_ref[:, cs], g_ref[:, cs]
            dyv = dy_ref[h, :, 0:HEAD_DIM] if head_major else dy_ref[:, cs]
            inv = lax.rsqrt(jnp.mean(xv * xv, axis=-1, keepdims=True) + EPS)
            nrm = xv * inv
            dn = dyv * gv * mult
            dg_ref[:, cs] += _colsum8(dyv * nrm * mult)
            dx_ref[0, :, cs] = (inv * (dn - nrm * jnp.mean(dn * nrm, axis=-1, keepdims=True))).astype(BF16)
        if extra is not None:
            dx_ref[1] = refs[3][...]

    row = pl.BlockSpec((tm, D_MODEL), lambda i: (i, 0))
    dy_spec = pl.BlockSpec((HEADS, tm, dy.shape[-1]), lambda i: (0, i, 0)) if head_major else row
    ins = [x, g, dy] + ([extra] if extra is not None else [])
    specs = ([pl.BlockSpec((tm, D_MODEL), lambda i: (i, col0)), _full((1, D_MODEL)), dy_spec]
             + ([row] if extra is not None else []))
    return pl.pallas_call(
        body, name=name, grid=(s // tm,), in_specs=specs,
        out_specs=[pl.BlockSpec((groups, tm, D_MODEL), lambda i: (0, i, 0)), _full((SUBLANES, D_MODEL))],
        out_shape=[jax.ShapeDtypeStruct((groups, s, D_MODEL), BF16), jax.ShapeDtypeStruct((SUBLANES, D_MODEL), F32)],
    )(*ins)


def _log_sigmoid(z):
    return jnp.minimum(z, 0.0) - jnp.log(1.0 + jnp.exp(-jnp.abs(z)))


Q_CUM, Q_ONE, Q_LSE = 0, 3, 6
LOG2E = 1.4426950408889634


def _pieces(v):
    hi = v.astype(BF16).astype(F32)
    mid = (v - hi).astype(BF16).astype(F32)
    lo = ((v - hi) - mid).astype(BF16).astype(F32)
    return hi, mid, lo


def _side(lane, at, v):
    hi, mid, lo = _pieces(v)
    return jnp.where(lane == at, hi, jnp.where(lane == at + 1, mid, jnp.where(lane == at + 2, lo, 0.0)))


def _fcum_fwd(f, bias, name):
    s = f.shape[0]
    tm = _tile(s, ROW_TILE)

    def body(f_ref, b_ref, qa_ref, ka_ref, carry):
        @pl.when(pl.program_id(0) == 0)
        def _():
            carry[...] = jnp.zeros_like(carry)

        cum = _tri_dot(_tri(tm), _log_sigmoid(f_ref[...] + b_ref[...])) + carry[...]
        carry[...] = cum[tm - 1:tm]
        lane = lax.broadcasted_iota(jnp.int32, (tm, LANES), 1)
        ones_q = jnp.where((lane >= Q_ONE) & (lane < Q_LSE), 1.0, 0.0)
        ones_k = jnp.where((lane < Q_ONE) | ((lane >= Q_LSE) & (lane < Q_LSE + 3)), 1.0, 0.0)
        for h in range(HEADS):
            c2 = cum[:, h:h + 1] * LOG2E
            qa_ref[h] = (_side(lane, Q_CUM, c2) + ones_q).astype(BF16)
            ka_ref[h] = (_side(lane, Q_ONE, -c2) + ones_k).astype(BF16)

    side = pl.BlockSpec((HEADS, tm, LANES), lambda i: (0, i, 0))
    return pl.pallas_call(
        body, name=name, grid=(s // tm,),
        in_specs=[pl.BlockSpec((tm, LANES), lambda i: (i, 0)), _full((1, LANES))],
        out_specs=[side, side],
        out_shape=[jax.ShapeDtypeStruct((HEADS, s, LANES), BF16)] * 2,
        scratch_shapes=[pltpu.VMEM((1, LANES), F32)],
    )(f, bias)


def _fcum_bwd(f, bias, dka, dcq, name):
    s = f.shape[0]
    tm = _tile(s, ROW_TILE)
    n_b = s // tm

    def body(f_ref, b_ref, dka_ref, dcq_ref, dz_ref, db_ref, carry):
        @pl.when(pl.program_id(0) == 0)
        def _():
            carry[...] = jnp.zeros_like(carry)
            db_ref[...] = jnp.zeros_like(db_ref)

        lane = lax.broadcasted_iota(jnp.int32, (tm, LANES), 1)
        rows = jnp.concatenate([dcq_ref[h] for h in range(HEADS)] + [jnp.zeros((LANES - HEADS, tm), F32)], axis=0)
        dcum = rows.T
        for h in range(HEADS):
            dcum = dcum - jnp.where(lane == h, dka_ref[h, :, Q_ONE:Q_ONE + 1], 0.0)
        dlf = _tri_dot(_tri(tm, upper=True), dcum) + carry[...]
        carry[...] = dlf[0:1]
        dz = dlf * _sig(-(f_ref[...] + b_ref[...]))
        dz_ref[0] = dz.astype(BF16)
        db_ref[...] += _colsum8(dz)

    return pl.pallas_call(
        body, name=name, grid=(n_b,),
        in_specs=[pl.BlockSpec((tm, LANES), lambda i: (n_b - 1 - i, 0)), _full((1, LANES)),
                  pl.BlockSpec((HEADS, tm, LANES), lambda i: (0, n_b - 1 - i, 0)),
                  pl.BlockSpec((HEADS, 1, tm), lambda i: (0, 0, n_b - 1 - i))],
        out_specs=[pl.BlockSpec((1, tm, LANES), lambda i: (0, n_b - 1 - i, 0)), _full((SUBLANES, LANES))],
        out_shape=[jax.ShapeDtypeStruct((1, s, LANES), BF16), jax.ShapeDtypeStruct((SUBLANES, LANES), F32)],
        scratch_shapes=[pltpu.VMEM((1, LANES), F32)],
    )(f, bias, dka, dcq)


def _causal_pairs(n_t, key_major):
    if key_major:
        pairs = [(qi, ki) for ki in range(n_t) for qi in range(ki, n_t)]
    else:
        pairs = [(qi, ki) for qi in range(n_t) for ki in range(qi + 1)]
    return (jnp.array([p[0] for p in pairs], jnp.int32), jnp.array([p[1] for p in pairs], jnp.int32))


def _with_side(main_ref, side_ref):
    return jnp.concatenate([main_ref[...], side_ref[...]], axis=1)


def _lane_const(t, lo, hi, value):
    lane = lax.broadcasted_iota(jnp.int32, (t, LANES), 1)
    return jnp.where((lane >= lo) & (lane < hi), value, 0.0).astype(BF16)


def _att_specs(t, nh):
    qmain = pl.BlockSpec((t, nh * HEAD_DIM), lambda h, p, qt, kt: (qt[p], h))
    kmain = pl.BlockSpec((t, nh * HEAD_DIM), lambda h, p, qt, kt: (kt[p], h))
    qside = pl.BlockSpec((nh, t, LANES), lambda h, p, qt, kt: (h, qt[p], 0))
    kside = pl.BlockSpec((nh, t, LANES), lambda h, p, qt, kt: (h, kt[p], 0))
    return qmain, kmain, qside, kside


def _fox_fwd(q, qa, k, ka, v, qo, name):
    s = q.shape[0]
    t = _tile(s, ATT_TILE)
    sub = t // ATT_SPLIT
    nh = ATT_FWD_HEADS
    qt, kt = _causal_pairs(s // t, key_major=False)

    def body(qt_ref, kt_ref, q_ref, qa_ref, k_ref, ka_ref, v_ref, og_ref, o_ref, y_ref, qab_ref, m_s, l_s, acc_s):
        pid = pl.program_id(1)
        qi, ki = qt_ref[pid], kt_ref[pid]

        @pl.when(ki == 0)
        def _():
            m_s[...] = jnp.full_like(m_s, NEG_INF)
            l_s[...] = jnp.zeros_like(l_s)
            acc_s[...] = jnp.zeros_like(acc_s)

        def step(diagonal):
            for hh in range(nh):
                hc = slice(hh * HEAD_DIM, (hh + 1) * HEAD_DIM)
                kc = jnp.concatenate([k_ref[:, hc], ka_ref[hh]], axis=1)
                vc = jnp.concatenate([v_ref[:, hc], _lane_const(t, 0, 1, 1.0)], axis=1)
                for r in range(ATT_SPLIT):
                    rows = slice(r * sub, (r + 1) * sub)
                    n_k = (r + 1) * sub if diagonal else t
                    sc = _dot_nt(jnp.concatenate([q_ref[rows, hc], qa_ref[hh, rows]], axis=1), kc[:n_k])
                    if diagonal:
                        sc = jnp.where(lax.broadcasted_iota(jnp.int32, (sub, n_k), 1)
                                       <= lax.broadcasted_iota(jnp.int32, (sub, n_k), 0) + r * sub, sc, NEG_INF)
                    m_old = m_s[hh, rows]
                    m_new = jnp.maximum(m_old, jnp.max(sc, axis=-1, keepdims=True))
                    alpha = jnp.exp2(m_old - m_new)
                    pv = _dot(jnp.exp2(sc - m_new[:, 0:1]).astype(BF16), vc[:n_k])
                    acc_s[hh, rows] = alpha * acc_s[hh, rows] + pv[:, :HEAD_DIM]
                    l_s[hh, rows] = alpha * l_s[hh, rows] + pv[:, HEAD_DIM:]
                    m_s[hh, rows] = m_new

        @pl.when(ki < qi)
        def _():
            step(False)

        @pl.when(ki == qi)
        def _():
            step(True)
            lane = lax.broadcasted_iota(jnp.int32, (t, LANES), 1)
            for hh in range(nh):
                hc = slice(hh * HEAD_DIM, (hh + 1) * HEAD_DIM)
                l = l_s[hh, :, 0:1]
                o = acc_s[hh] / l
                o_ref[:, hc] = o
                y_ref[:, hc] = (o * _sig(og_ref[:, hc])).astype(BF16)
                qab_ref[hh] = qa_ref[hh] + _side(lane, Q_LSE, -(m_s[hh, :, 0:1] + jnp.log2(l))).astype(BF16)

    qmain, kmain, qside, kside = _att_specs(t, nh)
    return pl.pallas_call(
        body, name=name,
        grid_spec=pltpu.PrefetchScalarGridSpec(
            num_scalar_prefetch=2, grid=(HEADS // nh, qt.shape[0]),
            in_specs=[qmain, qside, kmain, kside, kmain,
                      pl.BlockSpec((t, nh * HEAD_DIM), lambda h, p, qt, kt: (qt[p], HEADS // nh + h))],
            out_specs=[qmain, qmain, qside],
            scratch_shapes=[pltpu.VMEM((nh, t, LANES), F32), pltpu.VMEM((nh, t, LANES), F32),
                            pltpu.VMEM((nh, t, HEAD_DIM), F32)]),
        out_shape=[jax.ShapeDtypeStruct((s, D_MODEL), F32), jax.ShapeDtypeStruct((s, D_MODEL), BF16),
                   jax.ShapeDtypeStruct((HEADS, s, LANES), BF16)],
    )(qt, kt, q, qa, k, ka, v, qo)


def _fox_gate_bwd(o, qo, dout, w_out, name):
    s = o.shape[0]
    tm = _tile(s, ROW_TILE)

    def body(o_ref, og_ref, dout_ref, w_ref, do_ref, dg_ref, dl_ref):
        ov, dyv = o_ref[...], _dot_nt(dout_ref[0], w_ref[0])
        sg = _sig(og_ref[...])
        do = (dyv * sg).astype(BF16)
        do_ref[...] = do
        dg_ref[...] = (dyv * ov * sg * (1.0 - sg)).astype(BF16)
        prod = do.astype(F32) * ov
        lane = lax.broadcasted_iota(jnp.int32, (tm, LANES), 1)
        for h in range(HEADS):
            delta = jnp.sum(prod[:, h * HEAD_DIM:(h + 1) * HEAD_DIM], axis=-1, keepdims=True)
            dl_ref[h] = _side(lane, 0, delta).astype(BF16)

    row = pl.BlockSpec((tm, D_MODEL), lambda i: (i, 0))
    return pl.pallas_call(
        body, name=name, grid=(s // tm,),
        in_specs=[row, pl.BlockSpec((tm, D_MODEL), lambda i: (i, 1)),
                  pl.BlockSpec((1, tm, D_MODEL), lambda i: (0, i, 0)), _full(w_out.shape)],
        out_specs=[row, row, pl.BlockSpec((HEADS, tm, LANES), lambda i: (0, i, 0))],
        out_shape=[jax.ShapeDtypeStruct((s, D_MODEL), BF16), jax.ShapeDtypeStruct((s, D_MODEL), BF16),
                   jax.ShapeDtypeStruct((HEADS, s, LANES), BF16)],
    )(o, qo, dout, w_out)


def _fox_bwd(q, qab, k, ka, v, do, doa, name):
    s = q.shape[0]
    t = _tile(s, ATT_TILE)
    n_t = s // t
    sub = t // ATT_SPLIT
    nh = ATT_BWD_HEADS
    qt, kt = _causal_pairs(n_t, key_major=True)

    def body(qt_ref, kt_ref, q_ref, qab_ref, k_ref, ka_ref, v_ref, do_ref, doa_ref, dk_ref, dv_ref, dka_ref, dq_hbm,
             dcq_hbm, dk_s, dv_s, dq_ref, dcq_ref):
        group, pid = pl.program_id(0), pl.program_id(1)
        qi, ki = qt_ref[pid], kt_ref[pid]

        @pl.when(pid == 0)
        def _():
            dq_ref[...] = jnp.zeros_like(dq_ref)
            dcq_ref[...] = jnp.zeros_like(dcq_ref)

        @pl.when(qi == ki)
        def _():
            dk_s[...] = jnp.zeros_like(dk_s)
            dv_s[...] = jnp.zeros_like(dv_s)

        def step(diagonal):
            for hh in range(nh):
                hc = slice(hh * HEAD_DIM, (hh + 1) * HEAD_DIM)
                kc = jnp.concatenate([k_ref[:, hc], ka_ref[hh]], axis=1)
                vc = jnp.concatenate([v_ref[:, hc], _lane_const(t, 0, 3, -1.0)], axis=1)
                for r in range(ATT_SPLIT):
                    cols = slice(r * sub, (r + 1) * sub)
                    n_k = (r + 1) * sub if diagonal else t
                    qc = jnp.concatenate([q_ref[cols, hc], qab_ref[hh, cols]], axis=1)
                    sc = _dot_nt(kc[:n_k], qc)
                    if diagonal:
                        sc = jnp.where(lax.broadcasted_iota(jnp.int32, (n_k, sub), 0)
                                       <= lax.broadcasted_iota(jnp.int32, (n_k, sub), 1) + r * sub, sc, NEG_INF)
                    p = jnp.exp2(sc)
                    dov = do_ref[cols, hc]
                    dp = _dot_nt(vc[:n_k], jnp.concatenate([dov, doa_ref[hh, cols]], axis=1))
                    ds = (p * dp).astype(BF16)
                    dv_s[hh, 0:n_k] += _dot(p.astype(BF16), dov)
                    dk_s[hh, 0:n_k] += _dot(ds, qc)
                    q_rows = pl.ds(pl.multiple_of(qi * t + r * sub, sub), sub)
                    dq_ref[hh, q_rows, :] += _dot_tn(ds, k_ref[0:n_k, hc])
                    dcq_ref[hh, qi * ATT_SPLIT + r] += jnp.sum(ds.astype(F32), axis=0, keepdims=True)

        @pl.when(qi > ki)
        def _():
            step(False)

        @pl.when(qi == ki)
        def _():
            step(True)

        @pl.when(qi == n_t - 1)
        def _():
            for hh in range(nh):
                hc = slice(hh * HEAD_DIM, (hh + 1) * HEAD_DIM)
                dk_ref[:, hc] = dk_s[hh, :, :HEAD_DIM] * (1.0 / LOG2E)
                dka_ref[hh] = dk_s[hh, :, HEAD_DIM:]
                dv_ref[:, hc] = dv_s[hh].astype(BF16)

        @pl.when(pid == qt.shape[0] - 1)
        def _():
            pltpu.sync_copy(dq_ref, dq_hbm.at[pl.ds(group * nh, nh)])
            pltpu.sync_copy(dcq_ref, dcq_hbm.at[pl.ds(group * nh, nh)])

    qmain, kmain, qside, kside = _att_specs(t, nh)
    in_hbm = pl.BlockSpec(memory_space=pltpu.HBM)
    return pl.pallas_call(
        body, name=name,
        grid_spec=pltpu.PrefetchScalarGridSpec(
            num_scalar_prefetch=2, grid=(HEADS // nh, qt.shape[0]),
            in_specs=[qmain, qside, kmain, kside, kmain, qmain, qside],
            out_specs=[kmain, pl.BlockSpec((None, t, nh * HEAD_DIM), lambda h, p, qt, kt: (0, kt[p], h)), kside,
                       in_hbm, in_hbm],
            scratch_shapes=[pltpu.VMEM((nh, t, 2 * HEAD_DIM), F32), pltpu.VMEM((nh, t, HEAD_DIM), F32),
                            pltpu.VMEM((nh, s, HEAD_DIM), F32), pltpu.VMEM((nh, s // sub, 1, sub), F32)]),
        out_shape=[jax.ShapeDtypeStruct((s, D_MODEL), F32), jax.ShapeDtypeStruct((1, s, D_MODEL), BF16),
                   jax.ShapeDtypeStruct((HEADS, s, LANES), F32), jax.ShapeDtypeStruct((HEADS, s, HEAD_DIM), F32),
                   jax.ShapeDtypeStruct((HEADS, s // sub, 1, sub), F32)],
    )(qt, kt, q, qab, k, ka, v, do, doa)


def _mm_residual_premix(a, w, x, gate, mods, name):
    s, k = a.shape
    dm = x.shape[1]
    tm = _tile(s, ROW_TILE)

    def body(*refs):
        a_ref, w_ref, x_ref, g_ref = refs[:4]
        mod_refs = refs[4:4 + 2 * len(mods)]
        y_ref, xn_ref = refs[4 + 2 * len(mods):6 + 2 * len(mods)]
        h_refs = refs[6 + 2 * len(mods):]
        y = _dot(a_ref[...], w_ref[0])
        y_ref[...] = y
        xv = x_ref[...] + g_ref[...] * y
        xn_ref[...] = xv
        nrm = xv * lax.rsqrt(jnp.mean(xv * xv, axis=-1, keepdims=True) + EPS)
        for t, h_ref in enumerate(h_refs):
            h_ref[...] = (nrm * (1.0 + mod_refs[2 * t + 1][...]) + mod_refs[2 * t][...]).astype(BF16)

    row = pl.BlockSpec((tm, dm), lambda i: (i, 0))
    vec = _full((1, dm))
    outs = pl.pallas_call(
        body, name=name, grid=(s // tm,),
        in_specs=[pl.BlockSpec((tm, k), lambda i: (i, 0)), _full(w.shape), row, vec] + [vec] * (2 * len(mods)),
        out_specs=[row] * (2 + len(mods)),
        out_shape=[jax.ShapeDtypeStruct((s, dm), F32)] * 2 + [jax.ShapeDtypeStruct((s, dm), BF16)] * len(mods),
    )(a, w, x, gate, *[v for m in mods for v in m])
    return outs[0], outs[1], list(outs[2:])


def _mm_loss_head(a, w, x, gate, target, name):
    s, k = a.shape
    dm = x.shape[1]
    tm = _tile(s, ROW_TILE)

    def body(a_ref, w_ref, x_ref, g_ref, t_ref, sq_ref, do_ref, dy_ref, dg_ref):
        @pl.when(pl.program_id(0) == 0)
        def _():
            sq_ref[...] = jnp.zeros_like(sq_ref)
            dg_ref[...] = jnp.zeros_like(dg_ref)

        y, gv = _dot(a_ref[...], w_ref[0]), g_ref[...]
        err = x_ref[...] + gv * y - t_ref[...]
        sq_ref[...] += _colsum8(err * err)
        dout = err * (1.0 / dm)
        do_ref[...] = dout
        dy_ref[0] = (dout * gv).astype(BF16)
        dg_ref[...] += _colsum8(dout * y)

    row = pl.BlockSpec((tm, dm), lambda i: (i, 0))
    acc = _full((SUBLANES, dm))
    return pl.pallas_call(
        body, name=name, grid=(s // tm,),
        in_specs=[pl.BlockSpec((tm, k), lambda i: (i, 0)), _full(w.shape), row, _full((1, dm)), row],
        out_specs=[acc, row, pl.BlockSpec((1, tm, dm), lambda i: (0, i, 0)), acc],
        out_shape=[jax.ShapeDtypeStruct((SUBLANES, dm), F32), jax.ShapeDtypeStruct((s, dm), F32),
                   jax.ShapeDtypeStruct((1, s, dm), BF16), jax.ShapeDtypeStruct((SUBLANES, dm), F32)],
    )(a, w, x, gate, target)


def _ffn_inner(h, w_up, conv_w, conv_b, tag):
    s, dm = h.shape
    half = w_up.shape[2]
    f = 2 * half
    tm = _tile(s, FFN_ROWS)

    def body(h_ref, w_ref, cw_ref, cb_ref, u_ref, c_ref, a_ref, carry):
        @pl.when(pl.program_id(0) == 0)
        def _():
            carry[...] = jnp.zeros_like(carry)

        hv = h_ref[...]
        for j in range(2):
            cols = slice(j * half, (j + 1) * half)
            conv = []
            for g in range(2):
                ub = _dot(hv, w_ref[2 * g + j]).astype(BF16)
                u_ref[g, :, cols] = ub
                uf = ub.astype(F32)
                e = jnp.concatenate([carry[g, j], uf], axis=0)
                carry[g, j] = uf[tm - SUBLANES:tm]
                conv.append(_conv_taps(e, cw_ref[g][:, cols], cb_ref[g][:, cols])[SUBLANES:])
                c_ref[g, :, cols] = conv[g].astype(BF16)
            a_ref[:, cols] = (conv[0] * _sig(conv[0]) * conv[1]).astype(BF16)

    pair = pl.BlockSpec((2, tm, f), lambda i: (0, i, 0))
    return pl.pallas_call(
        body, name=tag + "_up_convglu", grid=(s // tm,),
        in_specs=[pl.BlockSpec((tm, dm), lambda i: (i, 0)), _full(w_up.shape), _full(conv_w.shape), _full(conv_b.shape)],
        out_specs=[pair, pair, pl.BlockSpec((tm, f), lambda i: (i, 0))],
        out_shape=[jax.ShapeDtypeStruct((2, s, f), BF16)] * 2 + [jax.ShapeDtypeStruct((s, f), BF16)],
        scratch_shapes=[pltpu.VMEM((2, 2, SUBLANES, half), F32)],
    )(h, w_up, conv_w, conv_b)


def _weight_grad_first(a, d, p_n, name):
    return lax.optimization_barrier((_mm_tn(a, d, p_n, name), d))


def _ffn_backward(dx_out, dffn, x_mid, scale, saved, w_up, conv_w, conv_b, w_down, mixer, tag):
    h, u, c, a = saved
    dw_down, dffn = _weight_grad_first(a, dffn, 1, tag + "_down_dw")
    du, dconv = _convglu_bwd(u, c, dffn, w_down, conv_w, tag + "_convglu_bwd")
    dw_up, du = _weight_grad_first(h, du, N_CHIPS, tag + "_up_dw")
    dx_mid, [(dshift, dscale)], dy, dgate_mixer = _premix_bwd(x_mid, [(scale, [(du, w_up)])], dx_out,
                                                              tag + "_premix_bwd", branch=mixer)
    return dx_mid, dy, dgate_mixer, dw_up, dw_down, dict(shift=dshift, scale=dscale, conv=dconv)


def _local_step(x, target, mods, lb, vecs, weights_at):
    m0, m1, mk = mods["l0"], mods["l1"], mods["kv"]
    wts, x = weights_at("mixer0", x)
    h0, proj = _premix_proj(x, m0[0], m0[1], wts["a_w_in"], "l0_premix_in")
    o_a, yp, states = _hgrn_fwd(proj, lb, vecs["a_norm_g"], "l0_hgrn")
    more, yp = weights_at("ffn0", yp)
    wts.update(more)
    y0, x1, [hf0] = _mm_residual_premix(yp, wts["a_w_out"], x, m0[2], [(m0[3], m0[4])], "l0_out")
    u0, c0, a0 = _ffn_inner(hf0, wts["up0"], vecs["conv_w0"], vecs["conv_b0"], "l0_ffn")
    saved0 = (hf0, u0, c0, a0)
    ffn0, x2, [hk, h1] = _mm_residual_premix(a0, wts["down0"], x1, m0[5], [(mk[0], mk[1]), (m1[0], m1[1])],
                                             "l0_ffn_down")
    more, hk = weights_at("layer1", hk)
    wts.update(more)
    k_raw, k_sh, v_sh, f_raw = _kv_proj(hk, wts["kv_k"], wts["kv_v"], wts["kv_f"], vecs["k_norm_g"], "kv_proj")
    qa, ka = _fcum_fwd(f_raw, vecs["kv_b_f"], "kv_fcum")
    q_scale = HEAD_DIM ** -0.5
    qo, q = _proj_headnorm(h1, wts["b_w_q"], vecs["q_norm_g"], q_scale * LOG2E, "l1_q")
    o_b, og, qab = _fox_fwd(q, qa, k_sh, ka, v_sh, qo, "l1_fox")
    y1, x3, [hf1] = _mm_residual_premix(og, wts["b_w_out"], x2, m1[2], [(m1[3], m1[4])], "l1_out")
    u1, c1, a1 = _ffn_inner(hf1, wts["up1"], vecs["conv_w1"], vecs["conv_b1"], "l1_ffn")
    saved1 = (hf1, u1, c1, a1)
    sq, dx4, dffn1, dg2_1 = _mm_loss_head(a1, wts["down1"], x3, m1[5], target, "l1_ffn_down")

    big, small = {}, {}
    dx3, dy1, dg1_1, big["up1"], big["down1"], s_ffn1 = _ffn_backward(
        dx4, dffn1, x3, m1[4], saved1, wts["up1"], vecs["conv_w1"], vecs["conv_b1"], wts["down1"], (y1, m1[2]), "l1_ffn")
    big["b_w_out"], dy1 = _weight_grad_first(og, dy1, 1, "l1_out_dw")
    do_b, dgate_b, doa = _fox_gate_bwd(o_b, qo, dy1, wts["b_w_out"], "l1_out_dx_gate_bwd")
    dk, dv, dka, dq, dcq = _fox_bwd(q, qab, k_sh, ka, v_sh, do_b, doa, "l1_fox_bwd")
    dqo, dqg = _headnorm_bwd(qo, vecs["q_norm_g"], q_scale, dq, "l1_qnorm_bwd", extra=dgate_b)
    big["b_w_q"], dqo = _weight_grad_first(h1, dqo, N_CHIPS, "l1_q_dw")
    dk_raw, dkg = _headnorm_bwd(k_raw, vecs["k_norm_g"], 1.0, dk, "kv_knorm_bwd")
    dz, dbf = _fcum_bwd(f_raw, vecs["kv_b_f"], dka, dcq.reshape(HEADS, 1, -1), "kv_fcum_bwd")
    big["kv_k"], dk_raw = _weight_grad_first(hk, dk_raw, 1, "kv_k_dw")
    big["kv_v"], dv = _weight_grad_first(hk, dv, 1, "kv_v_dw")
    big["kv_f"], dz = _weight_grad_first(hk, dz, 1, "kv_f_dw")
    kv_pairs = [(dk_raw, wts["kv_k"]), (dv, wts["kv_v"]), (dz, wts["kv_f"])]
    dx2, [(dsh1_1, dsc1_1), (dshk, dsck)], dffn0, dg2_0 = _premix_bwd(
        x2, [(m1[1], [(dqo, wts["b_w_q"])]), (mk[1], kv_pairs)], dx3, "l1_kv_premix_bwd", branch=(ffn0, m0[5]))
    dx1, dy0, dg1_0, big["up0"], big["down0"], s_ffn0 = _ffn_backward(
        dx2, dffn0, x1, m0[4], saved0, wts["up0"], vecs["conv_w0"], vecs["conv_b0"], wts["down0"], (y0, m0[2]), "l0_ffn")
    big["a_w_out"], dy0 = _weight_grad_first(yp, dy0, 1, "l0_out_dw")
    dproj, dlb, dng = _hgrn_bwd(proj, lb, vecs["a_norm_g"], o_a, states, dy0, wts["a_w_out"], "l0_out_dx_hgrn_bwd")
    grad_x, [(dsh1_0, dsc1_0)] = _premix_bwd(x, [(m0[1], [(dproj, wts["a_w_in"])])], dx1, "l0_premix_bwd")
    dproj, _ = lax.optimization_barrier((dproj, (dsh1_0, dsc1_0)))
    big["a_w_in"] = _mm_tn(h0, dproj, N_CHIPS, "l0_in_dw")

    small["mod_l0"] = [dsh1_0, dsc1_0, dg1_0, s_ffn0["shift"], s_ffn0["scale"], dg2_0]
    small["mod_l1"] = [dsh1_1, dsc1_1, dg1_1, s_ffn1["shift"], s_ffn1["scale"], dg2_1]
    small["mod_kv"] = [dshk, dsck]
    small["conv0"], small["conv1"] = s_ffn0["conv"], s_ffn1["conv"]
    small["a_norm_g"], small["k_norm_g"], small["q_norm_g"] = dng, dkg, dqg
    small["kv_b_f"], small["lb"] = dbf, dlb
    marks = {"attention_bwd": dk, "ffn0_bwd": dx1, "mixer0_bwd": grad_x}
    return sq, grad_x, big, small, marks


HBM = pl.BlockSpec(memory_space=pltpu.HBM)
COMM_CHUNK_ELEMS = 256 * 1024


def _place():
    x, y, c = lax.axis_index("x"), lax.axis_index("y"), lax.axis_index("c")
    chips = [(1 - x, y), (x, 1 - y), (1 - x, 1 - y)]
    return x, y, c, (x, y, 1 - c), chips


def _chunk_rows(rows, cols):
    best = BF16_ROWS
    for r in range(BF16_ROWS, rows + 1, BF16_ROWS):
        if rows % r == 0 and r * cols <= COMM_CHUNK_ELEMS:
            best = r
    assert rows % best == 0, (rows, cols)
    return best


def _allgather8(block, name):
    m_per, n = block.shape

    def body(x_ref, out_ref, send_sems, recv_sems, local_sem):
        x, y, c, sibling, chips = _place()
        me = (x, y, c)

        def rows(px, py, pc):
            return out_ref.at[pl.ds((4 * px + 2 * py + pc) * m_per, m_per), :]

        def copy(k, blk, to, src=None):
            return pltpu.make_async_remote_copy(
                src_ref=rows(*blk) if src is None else src, dst_ref=rows(*blk),
                send_sem=send_sems.at[k], recv_sem=recv_sems.at[k], device_id=to, device_id_type=MESH)

        mine = pltpu.make_async_copy(x_ref, rows(*me), local_sem)
        mine.start()
        first = [copy(0, me, sibling, src=x_ref)]
        first += [copy(1 + j, me, (*chip, c), src=x_ref) for j, chip in enumerate(chips)]
        for cp in first:
            cp.start()
        passed = [copy(4 + j, (*chip, c), sibling) for j, chip in enumerate(chips)]
        for j, chip in enumerate(chips):
            copy(1 + j, (*chip, c), me).wait_recv()
            passed[j].start()
        copy(0, sibling, me).wait_recv()
        for j, chip in enumerate(chips):
            copy(4 + j, (*chip, 1 - c), me).wait_recv()
        for cp in first + passed:
            cp.wait_send()
        mine.wait()

    return pl.pallas_call(
        body, name=name, out_shape=jax.ShapeDtypeStruct((N_DEV * m_per, n), block.dtype),
        in_specs=[pl.BlockSpec(memory_space=pltpu.VMEM)], out_specs=pl.BlockSpec(memory_space=pltpu.VMEM),
        scratch_shapes=[pltpu.SemaphoreType.DMA((7,)), pltpu.SemaphoreType.DMA((7,)), pltpu.SemaphoreType.DMA],
    )(block)


def _cast_own_block(shards, layer, chip, name):
    _, r, cols = shards.shape
    rows = _chunk_rows(r, cols)

    def body(chip_ref, w_ref, o_ref):
        o_ref[...] = w_ref[...].astype(BF16)

    return pl.pallas_call(
        body, name=name,
        grid_spec=pltpu.PrefetchScalarGridSpec(
            num_scalar_prefetch=1, grid=(r // rows,),
            in_specs=[pl.BlockSpec((None, rows, cols), lambda i, chip_ref: (layer, i, 0))],
            out_specs=pl.BlockSpec((None, rows, cols), lambda i, chip_ref: (chip_ref[0], i, 0))),
        out_shape=jax.ShapeDtypeStruct((N_CHIPS, r, cols), BF16),
    )(chip, shards)


def _sequencer_gather(bufs, name, collective_id):
    n_t = len(bufs)
    dims = [b.shape[1:] for b in bufs]
    refs = [jax.new_ref(b, memory_space=pltpu.MemorySpace.HBM) for b in bufs]

    @pl.kernel(mesh=plsc.ScalarSubcoreMesh(axis_name="sequencer", num_cores=1), name=name,
               scratch_types=[pltpu.SemaphoreType.DMA((n_t,)), pltpu.SemaphoreType.DMA((3 * n_t,)),
                              pltpu.SemaphoreType.DMA((n_t,)), pltpu.SemaphoreType.DMA((n_t,))],
               compiler_params=pltpu.CompilerParams(collective_id=collective_id))
    def launch(send_ici, recv_ici, send_d2d, recv_d2d):
        x, y, c, sibling, chips = _place()
        p_me = 2 * x + y
        peers = [sibling] + [(cx, cy, c) for cx, cy in chips]
        barrier = pltpu.get_barrier_semaphore()
        for peer in peers:
            pl.semaphore_signal(barrier, inc=1, device_id=peer, device_id_type=MESH)
        pl.semaphore_wait(barrier, len(peers))

        def waiter(t, sem_s, sem_r):
            win = refs[t].at[pl.ds(0, 3), pl.ds(0, dims[t][0] // 2), :]
            return pltpu.make_async_remote_copy(src_ref=win, dst_ref=win, send_sem=sem_s.at[t], recv_sem=sem_r.at[t],
                                                device_id=sibling, device_id_type=MESH)

        def half_copy(t, chip_idx, to, sem_s, sem_r, k):
            r2 = dims[t][0] // 2
            win = refs[t].at[chip_idx, pl.ds(c * r2, r2), :]
            return pltpu.make_async_remote_copy(src_ref=win, dst_ref=win, send_sem=sem_s.at[t], recv_sem=sem_r.at[k],
                                                device_id=to, device_id_type=MESH)

        for t in range(n_t):
            for j, (cx, cy) in enumerate(chips):
                half_copy(t, p_me, (cx, cy, c), send_ici, recv_ici, 3 * t + j).start()
        for t in range(n_t):
            for j, (cx, cy) in enumerate(chips):
                half_copy(t, 2 * cx + cy, (cx, cy, c), send_ici, recv_ici, 3 * t + j).wait_recv()
                half_copy(t, 2 * cx + cy, sibling, send_d2d, recv_d2d, t).start()
        for t in range(n_t):
            waiter(t, send_d2d, recv_d2d).wait_recv()
            waiter(t, send_ici, recv_ici).wait_send()
            waiter(t, send_d2d, recv_d2d).wait_send()

    launch()
    return [r[...] for r in refs]


def _sequencer_allgather8(block, dev, name, collective_id):
    m_per, n = block.shape
    src = jax.new_ref(block, memory_space=pltpu.MemorySpace.HBM)
    out = jax.empty_ref(jax.ShapeDtypeStruct((N_DEV * m_per, n), block.dtype), memory_space=pltpu.MemorySpace.HBM)

    @pl.kernel(mesh=plsc.ScalarSubcoreMesh(axis_name="sequencer", num_cores=1), name=name,
               scratch_types=[pltpu.SemaphoreType.DMA((7,))] * 2,
               compiler_params=pltpu.CompilerParams(collective_id=collective_id))
    def launch(send_sems, recv_sems):
        x, y, c, sibling, chips = _place()
        me = (x, y, c)
        _handshake([sibling] + [(cx, cy, c) for cx, cy in chips])

        def rows(px, py, pc):
            return out.at[pl.ds((4 * px + 2 * py + pc) * m_per, m_per), :]

        def copy(k, blk, to, from_src=False):
            return pltpu.make_async_remote_copy(
                src_ref=src if from_src else rows(*blk), dst_ref=rows(*blk),
                send_sem=send_sems.at[k], recv_sem=recv_sems.at[k], device_id=to, device_id_type=MESH)

        first = [copy(0, me, sibling, True)] + [copy(1 + j, me, (*chip, c), True) for j, chip in enumerate(chips)]
        for cp in first:
            cp.start()
        passed = [copy(4 + j, (*chip, c), sibling) for j, chip in enumerate(chips)]
        for j, chip in enumerate(chips):
            copy(1 + j, (*chip, c), me).wait_recv()
            passed[j].start()
        copy(0, sibling, me).wait_recv()
        for j, chip in enumerate(chips):
            copy(4 + j, (*chip, 1 - c), me).wait_recv()
        for cp in first + passed:
            cp.wait_send()

    launch()
    return lax.dynamic_update_slice(out[...], block, (dev * m_per, 0))


def _others():
    x, y, c = lax.axis_index("x"), lax.axis_index("y"), lax.axis_index("c")
    flip = lambda v, f: 1 - v if f else v
    return [(flip(x, fx), flip(y, fy), flip(c, fc))
            for fx in (0, 1) for fy in (0, 1) for fc in (0, 1) if (fx, fy, fc) != (0, 0, 0)]


def _handshake(peers):
    barrier = pltpu.get_barrier_semaphore()
    for peer in peers:
        pl.semaphore_signal(barrier, inc=1, device_id=peer, device_id_type=MESH)
    pl.semaphore_wait(barrier, len(peers))


def _sequencer_scatter(parts, name, collective_id):
    n_t = len(parts)
    dims = [p.shape[1:] for p in parts]
    srcs = [jax.new_ref(p, memory_space=pltpu.MemorySpace.HBM) for p in parts]
    inboxes = [jax.empty_ref(jax.ShapeDtypeStruct((N_DEV, r // 2, cols), BF16), memory_space=pltpu.MemorySpace.HBM)
               for r, cols in dims]

    @pl.kernel(mesh=plsc.ScalarSubcoreMesh(axis_name="sequencer", num_cores=1), name=name,
               scratch_types=[pltpu.SemaphoreType.DMA((n_t,))] * 2,
               compiler_params=pltpu.CompilerParams(collective_id=collective_id))
    def launch(send_sem, recv_sem):
        x, y, c = lax.axis_index("x"), lax.axis_index("y"), lax.axis_index("c")
        me = 4 * x + 2 * y + c
        peers = _others()
        _handshake(peers)
        for t in range(n_t):
            h = dims[t][0] // 2
            for qx, qy, qc in peers:
                pltpu.make_async_remote_copy(
                    src_ref=srcs[t].at[2 * qx + qy, pl.ds(qc * h, h), :], dst_ref=inboxes[t].at[me],
                    send_sem=send_sem.at[t], recv_sem=recv_sem.at[t], device_id=(qx, qy, qc), device_id_type=MESH).start()
        for t in range(n_t):
            win = inboxes[t].at[pl.ds(0, N_DEV - 1)]
            both = pltpu.make_async_remote_copy(src_ref=win, dst_ref=win, send_sem=send_sem.at[t],
                                                recv_sem=recv_sem.at[t], device_id=peers[0], device_id_type=MESH)
            both.wait_recv()
            both.wait_send()

    launch()
    return [b[...] for b in inboxes]


def _sum_pieces(part, inbox, place, name):
    _, r, cols = part.shape
    h = r // 2
    rows = _chunk_rows(h, cols)
    steps = h // rows

    def body(place_ref, own_ref, in_ref, o_ref):
        dev = place_ref[2]
        own = own_ref[...].astype(F32)
        acc = jnp.zeros((rows, cols), F32)
        for d in range(N_DEV):
            acc = acc + jnp.where(dev == d, own, in_ref[d].astype(F32))
        o_ref[...] = acc

    return pl.pallas_call(
        body, name=name,
        grid_spec=pltpu.PrefetchScalarGridSpec(
            num_scalar_prefetch=1, grid=(steps,),
            in_specs=[pl.BlockSpec((None, rows, cols), lambda i, pr: (pr[0], pr[1] * steps + i, 0)),
                      pl.BlockSpec((N_DEV, rows, cols), lambda i, pr: (0, i, 0))],
            out_specs=pl.BlockSpec((rows, cols), lambda i, pr: (pr[1] * steps + i, 0))),
        out_shape=jax.ShapeDtypeStruct((r, cols), F32),
    )(place, part, inbox)


def _sequencer_swap_halves(halves, name, collective_id):
    n_t = len(halves)
    refs = [jax.new_ref(a, memory_space=pltpu.MemorySpace.HBM) for a in halves]

    @pl.kernel(mesh=plsc.ScalarSubcoreMesh(axis_name="sequencer", num_cores=1), name=name,
               scratch_types=[pltpu.SemaphoreType.DMA((n_t,))] * 2,
               compiler_params=pltpu.CompilerParams(collective_id=collective_id))
    def launch(send_sem, recv_sem):
        x, y, c = lax.axis_index("x"), lax.axis_index("y"), lax.axis_index("c")
        sibling = (x, y, 1 - c)
        _handshake([sibling])
        copies = []
        for t in range(n_t):
            h = halves[t].shape[0] // 2
            win = refs[t].at[pl.ds(c * h, h), :]
            copies.append(pltpu.make_async_remote_copy(src_ref=win, dst_ref=win, send_sem=send_sem.at[t],
                                                       recv_sem=recv_sem.at[t], device_id=sibling, device_id_type=MESH))
            copies[-1].start()
        for cp in copies:
            cp.wait()

    launch()
    return [r[...] for r in refs]


def _cond_rows(c16, w, act, name):
    n_l, dm, wid = w.shape

    def body(c_ref, w_ref, o_ref, a_ref):
        cv = c_ref[...]
        if act:
            cv = cv * _sig(cv)
        a_ref[...] = cv
        o_ref[...] = _dot_f32(cv, w_ref[...])

    return pl.pallas_call(
        body, name=name, grid=(n_l,),
        in_specs=[_full((16, dm)), pl.BlockSpec((None, dm, wid), lambda l: (l, 0, 0))],
        out_specs=[pl.BlockSpec((None, 16, wid), lambda l: (l, 0, 0)), _full((16, dm))],
        out_shape=[jax.ShapeDtypeStruct((n_l, 16, wid), F32), jax.ShapeDtypeStruct((16, dm), F32)],
    )(c16, w)


def _outer_grad(ct, dm, name):
    n_l, kk, wid = dm.shape
    d_rows = ct.shape[0]

    def body(c_ref, d_ref, o_ref):
        o_ref[...] = _dot_f32(c_ref[...], d_ref[...])

    return pl.pallas_call(
        body, name=name, grid=(n_l,),
        in_specs=[_full((d_rows, kk)), pl.BlockSpec((None, kk, wid), lambda l: (l, 0, 0))],
        out_specs=pl.BlockSpec((None, d_rows, wid), lambda l: (l, 0, 0)),
        out_shape=jax.ShapeDtypeStruct((n_l, d_rows, wid), F32),
    )(ct, dm)


def _sum_devices(g, name):
    rows, n = g.shape

    def body(g_ref, o_ref):
        acc = g_ref[0:SUBLANES, :]
        for dev in range(1, N_DEV):
            acc = acc + g_ref[dev * SUBLANES:(dev + 1) * SUBLANES, :]
        o_ref[...] = acc

    return pl.pallas_call(body, name=name, out_shape=jax.ShapeDtypeStruct((SUBLANES, n), F32))(g)


def _adamw(w, g, m, v, name):
    shape = w.shape
    cols = shape[-1]
    rows = w.size // cols
    tr = rows
    for cand in range(SUBLANES, min(rows, 256) + 1, SUBLANES):
        if rows % cand == 0:
            tr = cand
    if rows * cols <= COMM_CHUNK_ELEMS:
        tr = rows
    c1 = 1.0 / (1.0 - ADAM_B1 ** ADAM_STEP)
    c2 = 1.0 / (1.0 - ADAM_B2 ** ADAM_STEP)

    def body(w_ref, g_ref, m_ref, v_ref, d_ref, mo_ref, vo_ref):
        gv = g_ref[...]
        m_new = ADAM_B1 * m_ref[...] + (1.0 - ADAM_B1) * gv
        v_new = ADAM_B2 * v_ref[...] + (1.0 - ADAM_B2) * (gv * gv)
        mo_ref[...] = m_new
        vo_ref[...] = v_new
        d_ref[...] = -ADAM_LR * ((m_new * c1) / (jnp.sqrt(v_new * c2) + ADAM_EPS) + ADAM_WD * w_ref[...])

    spec = pl.BlockSpec((tr, cols), lambda i: (i, 0))
    outs = pl.pallas_call(
        body, name=name, grid=(rows // tr,), in_specs=[spec] * 4, out_specs=[spec] * 3,
        out_shape=[jax.ShapeDtypeStruct((rows, cols), F32)] * 3,
    )(*[a.reshape(rows, cols) for a in (w, g, m, v)])
    return tuple(o.reshape(shape) for o in outs)


def _pad_cols(a, cols):
    return jnp.pad(a, [(0, 0)] * (a.ndim - 1) + [(0, cols - a.shape[-1])])


def _flat8(parts, width):
    v = jnp.concatenate([p.reshape(-1) for p in parts])
    return jnp.pad(v, (0, width - v.shape[0])).reshape(SUBLANES, width // SUBLANES)


KV_SHARD = 514
KV_SHARD_PAD = 640
BIG = ("a_w_in", "a_w_out", "kv_w", "b_w_q", "b_w_out", "up0", "up1", "down0", "down1")


def kernel(x, c, ada_w, ada_b, a_w_in, a_lb_logits, a_norm_g, a_w_out, kv_ada_w, kv_ada_b, kv_w, kv_b_f, k_norm_g, b_w_q, q_norm_g, b_w_out, ffn_w_up, ffn_conv_w, ffn_conv_b, ffn_w_down, loss_target, m_ada_w, m_ada_b, m_a_w_in, m_a_lb_logits, m_a_norm_g, m_a_w_out, m_kv_ada_w, m_kv_ada_b, m_kv_w, m_kv_b_f, m_k_norm_g, m_b_w_q, m_q_norm_g, m_b_w_out, m_ffn_w_up, m_ffn_conv_w, m_ffn_conv_b, m_ffn_w_down, v_ada_w, v_ada_b, v_a_w_in, v_a_lb_logits, v_a_norm_g, v_a_w_out, v_kv_ada_w, v_kv_ada_b, v_kv_w, v_kv_b_f, v_k_norm_g, v_b_w_q, v_q_norm_g, v_b_w_out, v_ffn_w_up, v_ffn_conv_w, v_ffn_conv_b, v_ffn_w_down):
    dm, ff = D_MODEL, D_FF
    ix, iy, ic = lax.axis_index("x"), lax.axis_index("y"), lax.axis_index("c")
    chip = 2 * ix + iy
    dev = 2 * chip + ic

    w1 = 10240
    g1 = _allgather8(_flat8([c, a_lb_logits, ffn_conv_w], w1), "gather_cond").reshape(N_DEV, w1)
    c_all = g1[:, :dm]
    per_chip = g1[0::2]
    lb_logits = per_chip[:, dm:dm + 512].reshape(N_CHIPS, 2, 256).transpose(1, 0, 2).reshape(2, dm)
    conv_w = per_chip[:, dm + 512:dm + 512 + 2 * CONV_W * FFN_COLS].reshape(N_CHIPS, 2, CONV_W, FFN_COLS)
    conv_w = conv_w.transpose(1, 2, 0, 3).reshape(2, CONV_W, 2, ff).transpose(0, 2, 1, 3)
    conv_b = ffn_conv_b.reshape(2, 2, 1, ff)
    lb = jax.nn.softmax(lb_logits, axis=0)[0:1]

    c16 = jnp.pad(c_all, ((0, 8), (0, 0)))
    mod_ada, c_act16 = _cond_rows(c16, ada_w, True, "mod_ada")
    mod_kv, _ = _cond_rows(c16, kv_ada_w[None], True, "mod_kv")
    mine = jnp.concatenate([mod_ada[0, :8], mod_ada[1, :8], mod_kv[0, :8]], axis=1)
    w2 = mine.shape[1]
    g2 = _allgather8(mine, "gather_mod").reshape(N_DEV, 8, w2)[0::2]
    my_rows = lax.dynamic_index_in_dim(g2, dev, axis=1, keepdims=False)
    mod0 = my_rows[:, 0:1536].reshape(6 * dm) + ada_b[0]
    mod1 = my_rows[:, 1536:3072].reshape(6 * dm) + ada_b[1]
    modk = my_rows[:, 3072:3584].reshape(2 * dm) + kv_ada_b
    mods = {"l0": [v.reshape(1, dm) for v in jnp.split(mod0, 6)],
            "l1": [v.reshape(1, dm) for v in jnp.split(mod1, 6)],
            "kv": [v.reshape(1, dm) for v in jnp.split(modk, 2)]}

    local = [(a_w_in, 0), (a_w_out, 0), (_pad_cols(kv_w, KV_SHARD_PAD)[None], 0), (b_w_q, 0), (b_w_out, 0),
             (ffn_w_up, 0), (ffn_w_up, 1), (ffn_w_down, 0), (ffn_w_down, 1)]
    chip_arr = chip.reshape(1).astype(jnp.int32)
    local = dict(zip(BIG, local))
    stages = {"mixer0": ("a_w_in",), "ffn0": ("a_w_out", "up0", "down0"),
              "layer1": ("kv_w", "b_w_q", "b_w_out", "up1", "down1")}
    arriving = {}

    def launch(stage, behind):
        shards = [local[n][0] for n in stages[stage]]
        if behind is not None:
            shards, _ = lax.optimization_barrier((shards, behind))
        own = [_cast_own_block(w, local[n][1], chip_arr, "cast_" + n) for n, w in zip(stages[stage], shards)]
        arriving[stage] = _sequencer_gather(own, "gather_" + stage, 1 + list(stages).index(stage))

    launch("mixer0", None)
    launch("ffn0", mod0)
    rowwise = lambda g: g.reshape(1, -1, dm)

    def weights_at(stage, token):
        if stage == "ffn0":
            launch("layer1", token)
        got, token = lax.optimization_barrier((arriving[stage], token))
        g = dict(zip(stages[stage], got))
        if stage == "mixer0":
            return {"a_w_in": g["a_w_in"]}, token
        if stage == "ffn0":
            return {"a_w_out": rowwise(g["a_w_out"]), "up0": g["up0"], "down0": rowwise(g["down0"])}, token
        s0, s1, s2, s3 = (g["kv_w"][p] for p in range(N_CHIPS))
        second = dm - KV_SHARD
        w_k = jnp.concatenate([s0[:, :KV_SHARD], s1[:, :second]], axis=1)
        w_v = jnp.concatenate([s1[:, second:KV_SHARD], s2[:, :KV_SHARD], s3[:, :KV_SHARD - HEADS]], axis=1)
        w_f = _pad_cols(s3[:, KV_SHARD - HEADS:KV_SHARD], LANES)
        return {"kv_k": w_k[None], "kv_v": w_v[None], "kv_f": w_f[None], "b_w_q": g["b_w_q"],
                "b_w_out": rowwise(g["b_w_out"]), "up1": g["up1"], "down1": rowwise(g["down1"])}, token

    vecs = {"a_norm_g": jnp.tile(a_norm_g, (1, HEADS)), "k_norm_g": jnp.tile(k_norm_g[None], (1, HEADS)),
            "q_norm_g": jnp.tile(q_norm_g, (1, HEADS)), "kv_b_f": _pad_cols(kv_b_f[None], LANES),
            "conv_w0": conv_w[0], "conv_b0": conv_b[0], "conv_w1": conv_w[1], "conv_b1": conv_b[1]}

    sq, grad_x, big, small, marks = _local_step(x[0], loss_target[0], mods, lb, vecs, weights_at)

    gk, gv, gf = big["kv_k"][0], big["kv_v"][0], big["kv_f"][0][:, :HEADS]
    second = dm - KV_SHARD
    kv_blocks = [gk[:, :KV_SHARD], jnp.concatenate([gk[:, KV_SHARD:], gv[:, :KV_SHARD - second]], axis=1),
                 gv[:, KV_SHARD - second:2 * KV_SHARD - second], jnp.concatenate([gv[:, 2 * KV_SHARD - second:], gf], axis=1)]
    kv_grad = jnp.stack([_pad_cols(b, KV_SHARD_PAD) for b in kv_blocks])
    chipwise = lambda g: g.reshape(N_CHIPS, -1, dm)
    parts = dict(zip(BIG, [big["a_w_in"], chipwise(big["a_w_out"]), kv_grad, big["b_w_q"], chipwise(big["b_w_out"]),
                           big["up0"], big["up1"], chipwise(big["down0"]), chipwise(big["down1"])]))
    place = jnp.stack([chip, ic, dev]).astype(jnp.int32)

    served = []
    boxes = {}

    groups = (("up1", "down1"), ("b_w_out", "b_w_q", "kv_w"), ("up0", "down0", "a_w_out"), ("a_w_in",))

    def scatter_group(k):
        mine = [parts[n] for n in groups[k]]
        if served:
            mine, _ = lax.optimization_barrier((mine, served[-1]))
        boxes[k] = _sequencer_scatter(mine, "scatter_grads_%d" % k, 4 + k)
        served.append(boxes[k])

    def sum_group(k, token):
        inboxes, _ = lax.optimization_barrier((boxes[k], token))
        return [_sum_pieces(parts[n], box, place, "sum_" + n) for n, box in zip(groups[k], inboxes)]

    def swap_group(k, halves, behind):
        halves, _ = lax.optimization_barrier((halves, behind))
        return dict(zip(groups[k], _sequencer_swap_halves(halves, "swap_grads_%d" % k, 8 + k)))

    for k in range(3):
        scatter_group(k)
    halves = [sum_group(0, marks["attention_bwd"]), sum_group(1, marks["ffn0_bwd"]), sum_group(2, marks["mixer0_bwd"])]

    fold = lambda a: a.sum(axis=0)
    heads = lambda a: fold(a).reshape(HEADS, HEAD_DIM).sum(axis=0)
    conv_flat = lambda a: a.sum(axis=2).transpose(1, 0, 2)
    pieces = ([fold(a) for a in small["mod_l0"]] + [fold(a) for a in small["mod_l1"]] + [fold(a) for a in small["mod_kv"]]
              + [conv_flat(small["conv0"]), conv_flat(small["conv1"]), heads(small["a_norm_g"]), heads(small["k_norm_g"]),
                 heads(small["q_norm_g"]), fold(small["kv_b_f"]), fold(small["lb"]),
                 0.5 * jnp.sum(sq).reshape(1) / dm])
    w3 = 61440
    small_vec, _ = lax.optimization_barrier((_flat8(pieces, w3), served[2]))
    g3 = _sequencer_allgather8(small_vec, dev, "gather_small", 12)
    served.append(g3)
    scatter_group(3)
    rs = {}
    for k in range(3):
        rs.update(swap_group(k, halves[k], g3))
    tot = _sum_devices(g3, "sum_small").reshape(w3)
    n_mod = 14 * dm
    dmod_all = g3.reshape(N_DEV, w3)[:, :n_mod]
    o = n_mod
    conv_tot = [tot[o + l * 8 * ff: o + (l + 1) * 8 * ff].reshape(4, 2 * ff) for l in range(2)]
    o += 16 * ff
    g_a_norm, g_k_norm, g_q_norm = (tot[o + i * HEAD_DIM: o + (i + 1) * HEAD_DIM] for i in range(3))
    o += 3 * HEAD_DIM
    g_kv_b_f = tot[o:o + HEADS]
    dlb = tot[o + LANES:o + LANES + dm]
    loss = tot[o + LANES + dm]

    ct = _pad_cols(c_act16[:8].T, LANES)
    dmod_pad = jnp.pad(dmod_all, ((0, LANES - N_DEV), (0, 0)))
    cols_ada = jnp.stack([lax.dynamic_slice_in_dim(dmod_pad, l * 6 * dm + chip * 1536, 1536, axis=1) for l in range(2)])
    cols_kv = lax.dynamic_slice_in_dim(dmod_pad, 12 * dm + chip * 512, 512, axis=1)[None]
    g_ada_w = _outer_grad(ct, cols_ada, "grad_ada_w")
    g_kv_ada_w = _outer_grad(ct, cols_kv, "grad_kv_ada_w")[0]

    my_lb = lax.dynamic_slice_in_dim(lb[0], chip * 256, 256)
    l0 = lax.dynamic_slice_in_dim(dlb, chip * 256, 256) * my_lb * (1.0 - my_lb)
    grads = {
        "ada_w": g_ada_w, "ada_b": jnp.stack([tot[:6 * dm], tot[6 * dm:12 * dm]]),
        "a_lb_logits": jnp.stack([l0, -l0]), "a_norm_g": g_a_norm[None],
        "a_w_out": rs["a_w_out"][None], "kv_ada_w": g_kv_ada_w, "kv_ada_b": tot[12 * dm:14 * dm],
        "kv_w": rs["kv_w"][:, :KV_SHARD], "kv_b_f": g_kv_b_f, "k_norm_g": g_k_norm,
        "b_w_q": rs["b_w_q"][None], "q_norm_g": g_q_norm[None], "b_w_out": rs["b_w_out"][None],
        "ffn_w_up": jnp.stack([rs["up0"], rs["up1"]]),
        "ffn_conv_w": jnp.stack([lax.dynamic_slice_in_dim(ct_l[:CONV_W], chip * FFN_COLS, FFN_COLS, axis=1) for ct_l in conv_tot]),
        "ffn_conv_b": jnp.stack([ct_l[CONV_W] for ct_l in conv_tot]),
        "ffn_w_down": jnp.stack([rs["down0"], rs["down1"]]),
    }
    weights = dict(ada_w=ada_w, ada_b=ada_b, a_w_in=a_w_in, a_lb_logits=a_lb_logits, a_norm_g=a_norm_g, a_w_out=a_w_out,
                   kv_ada_w=kv_ada_w, kv_ada_b=kv_ada_b, kv_w=kv_w, kv_b_f=kv_b_f, k_norm_g=k_norm_g, b_w_q=b_w_q,
                   q_norm_g=q_norm_g, b_w_out=b_w_out, ffn_w_up=ffn_w_up, ffn_conv_w=ffn_conv_w, ffn_conv_b=ffn_conv_b,
                   ffn_w_down=ffn_w_down)
    m_in = dict(ada_w=m_ada_w, ada_b=m_ada_b, a_w_in=m_a_w_in, a_lb_logits=m_a_lb_logits, a_norm_g=m_a_norm_g,
                a_w_out=m_a_w_out, kv_ada_w=m_kv_ada_w, kv_ada_b=m_kv_ada_b, kv_w=m_kv_w, kv_b_f=m_kv_b_f,
                k_norm_g=m_k_norm_g, b_w_q=m_b_w_q, q_norm_g=m_q_norm_g, b_w_out=m_b_w_out, ffn_w_up=m_ffn_w_up,
                ffn_conv_w=m_ffn_conv_w, ffn_conv_b=m_ffn_conv_b, ffn_w_down=m_ffn_w_down)
    v_in = dict(ada_w=v_ada_w, ada_b=v_ada_b, a_w_in=v_a_w_in, a_lb_logits=v_a_lb_logits, a_norm_g=v_a_norm_g,
                a_w_out=v_a_w_out, kv_ada_w=v_kv_ada_w, kv_ada_b=v_kv_ada_b, kv_w=v_kv_w, kv_b_f=v_kv_b_f,
                k_norm_g=v_k_norm_g, b_w_q=v_b_w_q, q_norm_g=v_q_norm_g, b_w_out=v_b_w_out, ffn_w_up=v_ffn_w_up,
                ffn_conv_w=v_ffn_conv_w, ffn_conv_b=v_ffn_conv_b, ffn_w_down=v_ffn_w_down)

    names = list(weights)
    step = lambda n: _adamw(weights[n], grads[n], m_in[n], v_in[n], "adamw_" + n)
    grads = {n: g.reshape(weights[n].shape) for n, g in grads.items()}
    upd = {n: step(n) for n in names if n != "a_w_in"}
    last = sum_group(3, [u[0] for u in upd.values()])
    grads["a_w_in"] = swap_group(3, last, last)["a_w_in"][None]
    upd["a_w_in"] = step("a_w_in")
    return (loss, grad_x[None], *[grads[n] for n in names], *[upd[n][0] for n in names],
            *[upd[n][1] for n in names], *[upd[n][2] for n in names])
```

```python
import jax
import jax.numpy as jnp
from jax import lax
from jax.experimental import pallas as pl
from jax.experimental.pallas import tpu as pltpu
from jax.experimental.pallas import tpu_sc as plsc

F32 = jnp.float32
BF16 = jnp.bfloat16

D_MODEL = 1024
HEADS = 8
HEAD_DIM = 128
A_CHUNK = 64
D_FF = 2816
CONV_W = 3
EPS = 1e-6
NEG_INF = -1e30
N_CHIPS = 4
N_DEV = 8

ADAM_LR = 0.001
ADAM_B1 = 0.9
ADAM_B2 = 0.999
ADAM_EPS = 1e-08
ADAM_WD = 0.01
ADAM_STEP = 10

SUBLANES = 8
BF16_ROWS = 16
LANES = 128
HALO = BF16_ROWS
ROW_TILE = 512
TOKEN_TILE_TN = 2048
FFN_COLS = 1408
FFN_ROWS = 256
HGRN_ROWS = 256
ATT_TILE = 512
ATT_SPLIT = 2
ATT_FWD_HEADS = 8
ATT_BWD_HEADS = 8
MESH = pl.DeviceIdType.MESH


def _sig(x):
    return jax.nn.sigmoid(x)


def _dot(a, b):
    return jnp.dot(a, b, preferred_element_type=F32)


def _dot_nt(a, b):
    return lax.dot_general(a, b, (((1,), (1,)), ((), ())), preferred_element_type=F32)


def _dot_tn(a, b):
    return lax.dot_general(a, b, (((0,), (0,)), ((), ())), preferred_element_type=F32)


def _split2(x):
    hi = x.astype(BF16)
    lo = (x - hi.astype(F32)).astype(BF16)
    return hi, lo


def _dot_f32(a, b):
    ah, al = _split2(a)
    bh, bl = _split2(b)
    return _dot(ah, bh) + _dot(ah, bl) + _dot(al, bh)


def _tri_dot(tri, x):
    hi = x.astype(BF16)
    r = x - hi.astype(F32)
    mid = r.astype(BF16)
    lo = (r - mid.astype(F32)).astype(BF16)
    return _dot(tri, hi) + _dot(tri, mid) + _dot(tri, lo)


def _tri(n, upper=False):
    r = lax.broadcasted_iota(jnp.int32, (n, n), 0)
    c = lax.broadcasted_iota(jnp.int32, (n, n), 1)
    keep = (c >= r) if upper else (c <= r)
    return jnp.where(keep, 1.0, 0.0).astype(BF16)


def _colsum8(v):
    rows, n = v.shape
    return v.reshape(rows // SUBLANES, SUBLANES, n).sum(axis=0)


def _full(shape):
    nd = len(shape)
    return pl.BlockSpec(shape, lambda *_: (0,) * nd)


def _tile(n, want):
    t = min(n, want)
    assert n % t == 0, (n, t)
    return t


def _mm_tn(a, d, p_n, name):
    m_rows, k = a.shape
    g_n, _, w_cols = d.shape
    per = p_n // g_n
    n = w_cols // per
    tm = _tile(m_rows, TOKEN_TILE_TN if k <= D_MODEL else ROW_TILE)
    steps = m_rows // tm

    def body(a_ref, d_ref, o_ref, acc):
        m = pl.program_id(1)

        @pl.when(m == 0)
        def _():
            acc[...] = jnp.zeros_like(acc)

        acc[...] += _dot_tn(a_ref[...], d_ref[...])

        @pl.when(m == steps - 1)
        def _():
            o_ref[...] = acc[...].astype(BF16)

    return pl.pallas_call(
        body, name=name, grid=(p_n, steps),
        in_specs=[pl.BlockSpec((tm, k), lambda p, m: (m, 0)),
                  pl.BlockSpec((None, tm, n), lambda p, m: (p // per, m, p % per))],
        out_specs=pl.BlockSpec((None, k, n), lambda p, m: (p, 0, 0)),
        out_shape=jax.ShapeDtypeStruct((p_n, k, n), BF16),
        scratch_shapes=[pltpu.VMEM((k, n), F32)],
    )(a, d)


def _premix_proj(x, shift, scale, w, name):
    s, dm = x.shape
    p_n, _, n = w.shape
    tm = _tile(s, ROW_TILE)

    def body(x_ref, sh_ref, sc_ref, w_ref, h_ref, o_ref):
        xv = x_ref[...]
        inv = lax.rsqrt(jnp.mean(xv * xv, axis=-1, keepdims=True) + EPS)
        h = (xv * inv * (1.0 + sc_ref[...]) + sh_ref[...]).astype(BF16)
        h_ref[...] = h
        for p in range(p_n):
            o_ref[:, p * n:(p + 1) * n] = _dot(h, w_ref[p])

    row = pl.BlockSpec((tm, dm), lambda i: (i, 0))
    vec = _full((1, dm))
    return pl.pallas_call(
        body, name=name, grid=(s // tm,), in_specs=[row, vec, vec, _full(w.shape)],
        out_specs=[row, pl.BlockSpec((tm, p_n * n), lambda i: (i, 0))],
        out_shape=[jax.ShapeDtypeStruct((s, dm), BF16), jax.ShapeDtypeStruct((s, p_n * n), F32)],
    )(x, shift, scale, w)


def _premix_bwd(x, terms, dres, name, branch=None):
    s, dm = x.shape
    tm = _tile(s, ROW_TILE)
    pairs = [pr for _, prs in terms for pr in prs]
    n_in = 2 + len(terms) + 2 * len(pairs) + (2 if branch else 0)

    def body(*refs):
        x_ref, dres_ref = refs[:2]
        sc_refs = refs[2:2 + len(terms)]
        mm_refs = refs[2 + len(terms):2 + len(terms) + 2 * len(pairs)]
        outs = refs[n_in:]

        @pl.when(pl.program_id(0) == 0)
        def _():
            for o in outs[1:1 + 2 * len(terms)]:
                o[...] = jnp.zeros_like(o)
            if branch:
                outs[-1][...] = jnp.zeros_like(outs[-1])

        xv = x_ref[...]
        inv = lax.rsqrt(jnp.mean(xv * xv, axis=-1, keepdims=True) + EPS)
        r = xv * inv
        dx = dres_ref[...]
        k = 0
        for t, (_, prs) in enumerate(terms):
            dh = None
            for d, w in prs:
                d_ref, w_ref = mm_refs[2 * k], mm_refs[2 * k + 1]
                k += 1
                p_n, _, n = w.shape
                per = p_n // d.shape[0]
                for p in range(p_n):
                    part = _dot_nt(d_ref[p // per, :, (p % per) * n:(p % per + 1) * n], w_ref[p])
                    dh = part if dh is None else dh + part
            dr = dh * (1.0 + sc_refs[t][...])
            dx = dx + inv * (dr - r * jnp.mean(dr * r, axis=-1, keepdims=True))
            outs[1 + 2 * t][...] += _colsum8(dh)
            outs[2 + 2 * t][...] += _colsum8(dh * r)
        outs[0][...] = dx
        if branch:
            y_ref, g_ref = refs[n_in - 2:n_in]
            outs[-2][0] = (dx * g_ref[...]).astype(BF16)
            outs[-1][...] += _colsum8(dx * y_ref[...])

    row = pl.BlockSpec((tm, dm), lambda i: (i, 0))
    vec, acc = _full((1, dm)), _full((SUBLANES, dm))
    ins, specs = [x, dres] + [sc for sc, _ in terms], [row, row] + [vec] * len(terms)
    for d, w in pairs:
        ins += [d, w]
        specs += [pl.BlockSpec((d.shape[0], tm, d.shape[2]), lambda i: (0, i, 0)), _full(w.shape)]
    out_shape = [jax.ShapeDtypeStruct((s, dm), F32)] + [jax.ShapeDtypeStruct((SUBLANES, dm), F32)] * (2 * len(terms))
    out_specs = [row] + [acc] * (2 * len(terms))
    if branch:
        ins += list(branch)
        specs += [row, vec]
        out_shape += [jax.ShapeDtypeStruct((1, s, dm), BF16), jax.ShapeDtypeStruct((SUBLANES, dm), F32)]
        out_specs += [pl.BlockSpec((1, tm, dm), lambda i: (0, i, 0)), acc]
    outs = pl.pallas_call(body, name=name, grid=(s // tm,), in_specs=specs, out_specs=out_specs,
                          out_shape=out_shape)(*ins)
    partials = [(outs[1 + 2 * t], outs[2 + 2 * t]) for t in range(len(terms))]
    return (outs[0], partials) + ((outs[-2], outs[-1]) if branch else ())


def _conv_taps(e, w, b):
    return w[2:3] * e + w[1:2] * pltpu.roll(e, 1, 0) + w[0:1] * pltpu.roll(e, 2, 0) + b


def _ffn_specs(s, tm, cb):
    hb = tm // HALO
    last = s // HALO - 1
    main = pl.BlockSpec((2, tm, cb), lambda j, i: (0, i, j))
    prev = pl.BlockSpec((2, HALO, cb), lambda j, i: (0, jnp.maximum(i * hb - 1, 0), j))
    nxt = pl.BlockSpec((2, HALO, cb), lambda j, i: (0, jnp.minimum((i + 1) * hb, last), j))
    wspec = pl.BlockSpec((2, CONV_W, cb), lambda j, i: (0, 0, j))
    bspec = pl.BlockSpec((2, 1, cb), lambda j, i: (0, 0, j))
    return main, prev, nxt, wspec, bspec


def _convglu_bwd(u, c, dffn, w_down, w, name):
    _, s, f = u.shape
    dm = dffn.shape[2]
    tm = _tile(s, 256)
    cb = _tile(f, FFN_COLS)
    steps = s // tm
    n_ext = tm + HALO
    main, _, nxt, wspec, _ = _ffn_specs(s, tm, cb)
    hb = tm // HALO
    last = s // HALO - 1
    d_main = pl.BlockSpec((None, tm, dm), lambda j, i: (0, i, 0))
    d_next = pl.BlockSpec((None, HALO, dm), lambda j, i: (0, jnp.minimum((i + 1) * hb, last), 0))
    wd_spec = pl.BlockSpec((None, cb, dm), lambda j, i: (0, j, 0))

    def body(u_ref, c_ref, cn_ref, d_ref, dn_ref, wd_ref, w_ref, du_ref, acc_ref):
        i = pl.program_id(1)
        notlast = jnp.where(i < steps - 1, 1.0, 0.0)

        @pl.when(i == 0)
        def _():
            acc_ref[...] = jnp.zeros_like(acc_ref)

        gate, val = (jnp.concatenate([c_ref[g].astype(F32), cn_ref[g].astype(F32)], axis=0) for g in range(2))
        wd = wd_ref[...]
        da = jnp.concatenate([_dot_nt(d_ref[...], wd).astype(BF16).astype(F32),
                              _dot_nt(dn_ref[...], wd).astype(BF16).astype(F32) * notlast], axis=0)
        sg = _sig(gate)
        d_val = da * gate * sg
        d_gate = da * val * (sg * (1.0 + gate * (1.0 - sg)))

        def finish(g, d):
            wv = w_ref[g]
            d1, d2 = pltpu.roll(d, n_ext - 1, 0), pltpu.roll(d, n_ext - 2, 0)
            du_ref[g] = (wv[2:3] * d + wv[1:2] * d1 + wv[0:1] * d2)[0:tm].astype(BF16)
            uv = u_ref[g].astype(F32)
            acc_ref[g, 2] += _colsum8(d[0:tm] * uv)
            acc_ref[g, 1] += _colsum8(d1[0:tm] * uv)
            acc_ref[g, 0] += _colsum8(d2[0:tm] * uv)
            acc_ref[g, 3] += _colsum8(d[0:tm])

        finish(0, d_gate)
        finish(1, d_val)

    return pl.pallas_call(
        body, name=name, grid=(f // cb, steps),
        in_specs=[main, main, nxt, d_main, d_next, wd_spec, wspec],
        out_specs=[main, pl.BlockSpec((2, 4, SUBLANES, cb), lambda j, i: (0, 0, 0, j))],
        out_shape=[jax.ShapeDtypeStruct((2, s, f), BF16), jax.ShapeDtypeStruct((2, 4, SUBLANES, f), F32)],
    )(u, c, c, dffn, dffn, w_down, w)


def _hgrn_gates(q_raw, f_raw, lb, tri):
    sf = _sig(f_raw)
    fg = lb + (1.0 - lb) * sf
    b = _tri_dot(tri, jnp.log(fg))
    return q_raw * _sig(q_raw), 1.0 - fg, b, fg, sf


def _hgrn_fwd(proj, lb, norm_g, name):
    s = proj.shape[0]
    tb = _tile(s, HGRN_ROWS)
    n_c = tb // A_CHUNK
    half = A_CHUNK // 2

    def body(q_ref, f_ref, v_ref, g_ref, lb_ref, ng_ref, o_ref, yp_ref, st_ref, state):
        @pl.when(pl.program_id(0) == 0)
        def _():
            state[...] = jnp.zeros_like(state)

        tri = _tri(A_CHUNK)
        causal = lax.broadcasted_iota(jnp.int32, (A_CHUNK, A_CHUNK), 1) <= lax.broadcasted_iota(
            jnp.int32, (A_CHUNK, A_CHUNK), 0)

        def chunk(ci, carry):
            rows = pl.ds(ci * A_CHUNK, A_CHUNK)
            heads = [slice(h * HEAD_DIM, (h + 1) * HEAD_DIM) for h in range(HEADS)]
            qs, k, b, _, _ = _hgrn_gates(q_ref[rows, :], f_ref[rows, :], lb_ref[...], tri)
            b_mid, b_last = b[half:half + 1], b[A_CHUNK - 1:A_CHUNK]
            q_i = (qs * jnp.exp(b - b_mid)).astype(BF16)
            k_i = (k * jnp.exp(b_mid - b)).astype(BF16)
            q_e = (qs * jnp.exp(b)).astype(BF16)
            k_s = (k * jnp.exp(b_last - b)).astype(BF16)
            decay = jnp.exp(b_last)
            vb = v_ref[rows, :].astype(BF16)
            scores = [jnp.where(causal, _dot_nt(q_i[:, cs], k_i[:, cs]), 0.0).astype(BF16) for cs in heads]
            st = [state[h] for h in range(HEADS)]
            outs = [_dot(scores[h], vb[:, cs]) + _dot_nt(q_e[:, cs], st[h].astype(BF16)) for h, cs in enumerate(heads)]
            for h, cs in enumerate(heads):
                st_ref[ci, h] = st[h]
                state[h] = st[h] * decay[:, cs] + _dot_tn(vb[:, cs], k_s[:, cs])
            o = jnp.concatenate(outs, axis=1)
            o_ref[rows, :] = o
            sq = o * o
            inv = jnp.concatenate([jnp.broadcast_to(lax.rsqrt(jnp.mean(sq[:, cs], axis=-1, keepdims=True) + EPS),
                                                    (A_CHUNK, HEAD_DIM)) for cs in heads], axis=1)
            g_raw = g_ref[rows, :]
            yp_ref[rows, :] = (o * inv * ng_ref[...] * (g_raw * _sig(g_raw))).astype(BF16)
            return carry

        for step in range(n_c):
            chunk(step, 0)

    col = lambda j: pl.BlockSpec((tb, D_MODEL), lambda i: (i, j))
    vec = _full((1, D_MODEL))
    return pl.pallas_call(
        body, name=name, grid=(s // tb,), in_specs=[col(0), col(1), col(2), col(3), vec, vec],
        out_specs=[col(0), col(0), pl.BlockSpec((n_c, HEADS, HEAD_DIM, HEAD_DIM), lambda i: (i, 0, 0, 0))],
        out_shape=[jax.ShapeDtypeStruct((s, D_MODEL), F32), jax.ShapeDtypeStruct((s, D_MODEL), BF16),
                   jax.ShapeDtypeStruct((s // A_CHUNK, HEADS, HEAD_DIM, HEAD_DIM), F32)],
        scratch_shapes=[pltpu.VMEM((HEADS, HEAD_DIM, HEAD_DIM), F32)],
    )(proj, proj, proj, proj, lb, norm_g)


def _hgrn_bwd(proj, lb, norm_g, o, states, dout, w_out, name):
    s = proj.shape[0]
    tb = _tile(s, HGRN_ROWS)
    n_c = tb // A_CHUNK
    n_b = s // tb
    half = A_CHUNK // 2

    def body(q_ref, f_ref, v_ref, g_ref, lb_ref, ng_ref, o_ref, st_ref, dout_ref, w_ref, dp_ref, dlb_ref, dng_ref,
             dstate, dyp_ref):
        @pl.when(pl.program_id(0) == 0)
        def _():
            dstate[...] = jnp.zeros_like(dstate)
            dlb_ref[...] = jnp.zeros_like(dlb_ref)
            dng_ref[...] = jnp.zeros_like(dng_ref)

        dyp_ref[...] = _dot_nt(dout_ref[0], w_ref[0])

        tri = _tri(A_CHUNK)
        tri_up = _tri(A_CHUNK, upper=True)
        row_id = lax.broadcasted_iota(jnp.int32, (A_CHUNK, D_MODEL), 0)
        causal = lax.broadcasted_iota(jnp.int32, (A_CHUNK, A_CHUNK), 1) <= lax.broadcasted_iota(
            jnp.int32, (A_CHUNK, A_CHUNK), 0)

        def chunk(cj, carry):
            ci = n_c - 1 - cj
            rows = pl.ds(ci * A_CHUNK, A_CHUNK)
            heads = [slice(h * HEAD_DIM, (h + 1) * HEAD_DIM) for h in range(HEADS)]
            cat = lambda parts: jnp.concatenate(parts, axis=1)
            per_head_mean = lambda a: cat([jnp.broadcast_to(jnp.mean(a[:, cs], axis=-1, keepdims=True),
                                                            (A_CHUNK, HEAD_DIM)) for cs in heads])
            q_raw, lbv = q_ref[rows, :], lb_ref[...]
            qs, k, b, fg, sf = _hgrn_gates(q_raw, f_ref[rows, :], lbv, tri)
            b_mid, b_last = b[half:half + 1], b[A_CHUNK - 1:A_CHUNK]
            e_qi, e_ki, e_q, e_ks = jnp.exp(b - b_mid), jnp.exp(b_mid - b), jnp.exp(b), jnp.exp(b_last - b)
            decay = jnp.exp(b_last)
            q_i, k_i, q_e, k_s = qs * e_qi, k * e_ki, qs * e_q, k * e_ks
            qib, kib, qeb, ksb = q_i.astype(BF16), k_i.astype(BF16), q_e.astype(BF16), k_s.astype(BF16)
            vb = v_ref[rows, :].astype(BF16)
            ov, g_raw, dy, ng = o_ref[rows, :], g_ref[rows, :], dyp_ref[rows, :], ng_ref[...]
            inv = lax.rsqrt(per_head_mean(ov * ov) + EPS)
            nrm = ov * inv
            sg = _sig(g_raw)
            gs = g_raw * sg
            dn = dy * ng * gs
            dng_ref[0:1, :] += jnp.sum(dy * nrm * gs, axis=0, keepdims=True)
            dg_raw = dy * nrm * ng * (sg * (1.0 + g_raw * (1.0 - sg)))
            do = (inv * (dn - nrm * per_head_mean(dn * nrm))).astype(BF16)
            st_prev = [st_ref[ci, h] for h in range(HEADS)]
            dst = [dstate[h] for h in range(HEADS)]
            dstb = [d.astype(BF16) for d in dst]
            scores = [jnp.where(causal, _dot_nt(qib[:, cs], kib[:, cs]), 0.0).astype(BF16) for cs in heads]
            d_scores = [jnp.where(causal, _dot_nt(do[:, cs], vb[:, cs]), 0.0).astype(BF16) for cs in heads]
            dv = cat([_dot_tn(scores[h], do[:, cs]) + _dot_nt(ksb[:, cs], dstb[h]) for h, cs in enumerate(heads)])
            dq_i = cat([_dot(d_scores[h], kib[:, cs]) for h, cs in enumerate(heads)])
            dk_i = cat([_dot_tn(d_scores[h], qib[:, cs]) for h, cs in enumerate(heads)])
            dq_e = cat([_dot(do[:, cs], st_prev[h].astype(BF16)) for h, cs in enumerate(heads)])
            dk_s = cat([_dot(vb[:, cs], dstb[h]) for h, cs in enumerate(heads)])
            d_decay = cat([jnp.sum(st_prev[h] * dst[h], axis=0, keepdims=True) for h in range(HEADS)])
            for h, cs in enumerate(heads):
                dstate[h] = dst[h] * decay[:, cs] + _dot_tn(do[:, cs], qeb[:, cs])
            dq = dq_i * e_qi + dq_e * e_q
            dk = dk_i * e_ki + dk_s * e_ks
            t_qi, t_ki, t_ks = dq_i * q_i, dk_i * k_i, dk_s * k_s
            db = t_qi - t_ki + dq_e * q_e - t_ks
            db_mid = jnp.sum(t_ki - t_qi, axis=0, keepdims=True)
            db_last = jnp.sum(t_ks, axis=0, keepdims=True) + d_decay * decay
            db = db + jnp.where(row_id == half, db_mid, 0.0) + jnp.where(row_id == A_CHUNK - 1, db_last, 0.0)
            dfg = _tri_dot(tri_up, db) / fg - dk
            dlb_ref[0:1, :] += jnp.sum(dfg * (1.0 - sf), axis=0, keepdims=True)
            sq = _sig(q_raw)
            dp_ref[0, rows, :] = (dq * (sq * (1.0 + q_raw * (1.0 - sq)))).astype(BF16)
            dp_ref[1, rows, :] = (dfg * (1.0 - lbv) * sf * (1.0 - sf)).astype(BF16)
            dp_ref[2, rows, :] = dv.astype(BF16)
            dp_ref[3, rows, :] = dg_raw.astype(BF16)
            return carry

        for step in range(n_c):
            chunk(step, 0)

    col = lambda j: pl.BlockSpec((tb, D_MODEL), lambda i: (n_b - 1 - i, j))
    vec = _full((1, D_MODEL))
    acc = _full((SUBLANES, D_MODEL))
    return pl.pallas_call(
        body, name=name, grid=(n_b,),
        in_specs=[col(0), col(1), col(2), col(3), vec, vec, col(0),
                  pl.BlockSpec((n_c, HEADS, HEAD_DIM, HEAD_DIM), lambda i: (n_b - 1 - i, 0, 0, 0)),
                  pl.BlockSpec((1, tb, D_MODEL), lambda i: (0, n_b - 1 - i, 0)), _full(w_out.shape)],
        out_specs=[pl.BlockSpec((4, tb, D_MODEL), lambda i: (0, n_b - 1 - i, 0)), acc, acc],
        out_shape=[jax.ShapeDtypeStruct((4, s, D_MODEL), BF16), jax.ShapeDtypeStruct((SUBLANES, D_MODEL), F32),
                   jax.ShapeDtypeStruct((SUBLANES, D_MODEL), F32)],
        scratch_shapes=[pltpu.VMEM((HEADS, HEAD_DIM, HEAD_DIM), F32), pltpu.VMEM((tb, D_MODEL), F32)],
    )(proj, proj, proj, proj, lb, norm_g, o, states, dout, w_out)


def _head_rms(raw_ref, g_ref, mult, y_ref):
    for h in range(HEADS):
        cs = slice(h * HEAD_DIM, (h + 1) * HEAD_DIM)
        xv = raw_ref[:, cs]
        inv = lax.rsqrt(jnp.mean(xv * xv, axis=-1, keepdims=True) + EPS)
        y_ref[:, cs] = (xv * inv * g_ref[:, cs] * mult).astype(BF16)


def _proj_headnorm(a, w, g, mult, name):
    s, k = a.shape
    p_n, _, n = w.shape
    tm = _tile(s, ROW_TILE)

    def body(a_ref, w_ref, g_ref, raw_ref, y_ref):
        av = a_ref[...]
        for p in range(p_n):
            raw_ref[:, p * n:(p + 1) * n] = _dot(av, w_ref[p])
        _head_rms(raw_ref, g_ref, mult, y_ref)

    row = lambda wid: pl.BlockSpec((tm, wid), lambda i: (i, 0))
    return pl.pallas_call(
        body, name=name, grid=(s // tm,), in_specs=[row(k), _full(w.shape), _full((1, D_MODEL))],
        out_specs=[row(p_n * n), row(D_MODEL)],
        out_shape=[jax.ShapeDtypeStruct((s, p_n * n), F32), jax.ShapeDtypeStruct((s, D_MODEL), BF16)],
    )(a, w, g)


def _kv_proj(hk, w_k, w_v, w_f, g, name):
    s, k = hk.shape
    tm = _tile(s, ROW_TILE)

    def body(h_ref, wk_ref, wv_ref, wf_ref, g_ref, kr_ref, k_ref, v_ref, f_ref):
        hv = h_ref[...]
        kr_ref[...] = _dot(hv, wk_ref[0])
        v_ref[...] = _dot(hv, wv_ref[0]).astype(BF16)
        f_ref[...] = _dot(hv, wf_ref[0])
        _head_rms(kr_ref, g_ref, 1.0, k_ref)

    row = lambda wid: pl.BlockSpec((tm, wid), lambda i: (i, 0))
    return pl.pallas_call(
        body, name=name, grid=(s // tm,),
        in_specs=[row(k), _full(w_k.shape), _full(w_v.shape), _full(w_f.shape), _full((1, D_MODEL))],
        out_specs=[row(D_MODEL), row(D_MODEL), row(D_MODEL), row(LANES)],
        out_shape=[jax.ShapeDtypeStruct((s, D_MODEL), F32), jax.ShapeDtypeStruct((s, D_MODEL), BF16),
                   jax.ShapeDtypeStruct((s, D_MODEL), BF16), jax.ShapeDtypeStruct((s, LANES), F32)],
    )(hk, w_k, w_v, w_f, g)


def _headnorm_bwd(x, g, mult, dy, name, col0=0, extra=None):
    s = x.shape[0]
    tm = _tile(s, ROW_TILE)
    groups = 2 if extra is not None else 1
    head_major = dy.ndim == 3

    def body(*refs):
        x_ref, g_ref, dy_ref = refs[:3]
        dx_ref, dg_ref = refs[-2:]

        @pl.when(pl.program_id(0) == 0)
        def _():
            dg_ref[...] = jnp.zeros_like(dg_ref)

        for h in range(HEADS):
            cs = slice(h * HEAD_DIM, (h + 1) * HEAD_DIM)
            xv, gv = x_ref[:, cs], g_ref[:, cs]
            dyv = dy_ref[h, :, 0:HEAD_DIM] if head_major else dy_ref[:, cs]
            inv = lax.rsqrt(jnp.mean(xv * xv, axis=-1, keepdims=True) + EPS)
            nrm = xv * inv
            dn = dyv * gv * mult
            dg_ref[:, cs] += _colsum8(dyv * nrm * mult)
            dx_ref[0, :, cs] = (inv * (dn - nrm * jnp.mean(dn * nrm, axis=-1, keepdims=True))).astype(BF16)
        if extra is not None:
            dx_ref[1] = refs[3][...]

    row = pl.BlockSpec((tm, D_MODEL), lambda i: (i, 0))
    dy_spec = pl.BlockSpec((HEADS, tm, dy.shape[-1]), lambda i: (0, i, 0)) if head_major else row
    ins = [x, g, dy] + ([extra] if extra is not None else [])
    specs = ([pl.BlockSpec((tm, D_MODEL), lambda i: (i, col0)), _full((1, D_MODEL)), dy_spec]
             + ([row] if extra is not None else []))
    return pl.pallas_call(
        body, name=name, grid=(s // tm,), in_specs=specs,
        out_specs=[pl.BlockSpec((groups, tm, D_MODEL), lambda i: (0, i, 0)), _full((SUBLANES, D_MODEL))],
        out_shape=[jax.ShapeDtypeStruct((groups, s, D_MODEL), BF16), jax.ShapeDtypeStruct((SUBLANES, D_MODEL), F32)],
    )(*ins)


def _log_sigmoid(z):
    return jnp.minimum(z, 0.0) - jnp.log(1.0 + jnp.exp(-jnp.abs(z)))


Q_CUM, Q_ONE, Q_LSE = 0, 3, 6
LOG2E = 1.4426950408889634


def _pieces(v):
    hi = v.astype(BF16).astype(F32)
    mid = (v - hi).astype(BF16).astype(F32)
    lo = ((v - hi) - mid).astype(BF16).astype(F32)
    return hi, mid, lo


def _side(lane, at, v):
    hi, mid, lo = _pieces(v)
    return jnp.where(lane == at, hi, jnp.where(lane == at + 1, mid, jnp.where(lane == at + 2, lo, 0.0)))


def _fcum_fwd(f, bias, name):
    s = f.shape[0]
    tm = _tile(s, ROW_TILE)

    def body(f_ref, b_ref, qa_ref, ka_ref, carry):
        @pl.when(pl.program_id(0) == 0)
        def _():
            carry[...] = jnp.zeros_like(carry)

        cum = _tri_dot(_tri(tm), _log_sigmoid(f_ref[...] + b_ref[...])) + carry[...]
        carry[...] = cum[tm - 1:tm]
        lane = lax.broadcasted_iota(jnp.int32, (tm, LANES), 1)
        ones_q = jnp.where((lane >= Q_ONE) & (lane < Q_LSE), 1.0, 0.0)
        ones_k = jnp.where((lane < Q_ONE) | ((lane >= Q_LSE) & (lane < Q_LSE + 3)), 1.0, 0.0)
        for h in range(HEADS):
            c2 = cum[:, h:h + 1] * LOG2E
            qa_ref[h] = (_side(lane, Q_CUM, c2) + ones_q).astype(BF16)
            ka_ref[h] = (_side(lane, Q_ONE, -c2) + ones_k).astype(BF16)

    side = pl.BlockSpec((HEADS, tm, LANES), lambda i: (0, i, 0))
    return pl.pallas_call(
        body, name=name, grid=(s // tm,),
        in_specs=[pl.BlockSpec((tm, LANES), lambda i: (i, 0)), _full((1, LANES))],
        out_specs=[side, side],
        out_shape=[jax.ShapeDtypeStruct((HEADS, s, LANES), BF16)] * 2,
        scratch_shapes=[pltpu.VMEM((1, LANES), F32)],
    )(f, bias)


def _fcum_bwd(f, bias, dka, dcq, name):
    s = f.shape[0]
    tm = _tile(s, ROW_TILE)
    n_b = s // tm

    def body(f_ref, b_ref, dka_ref, dcq_ref, dz_ref, db_ref, carry):
        @pl.when(pl.program_id(0) == 0)
        def _():
            carry[...] = jnp.zeros_like(carry)
            db_ref[...] = jnp.zeros_like(db_ref)

        lane = lax.broadcasted_iota(jnp.int32, (tm, LANES), 1)
        rows = jnp.concatenate([dcq_ref[h] for h in range(HEADS)] + [jnp.zeros((LANES - HEADS, tm), F32)], axis=0)
        dcum = rows.T
        for h in range(HEADS):
            dcum = dcum - jnp.where(lane == h, dka_ref[h, :, Q_ONE:Q_ONE + 1], 0.0)
        dlf = _tri_dot(_tri(tm, upper=True), dcum) + carry[...]
        carry[...] = dlf[0:1]
        dz = dlf * _sig(-(f_ref[...] + b_ref[...]))
        dz_ref[0] = dz.astype(BF16)
        db_ref[...] += _colsum8(dz)

    return pl.pallas_call(
        body, name=name, grid=(n_b,),
        in_specs=[pl.BlockSpec((tm, LANES), lambda i: (n_b - 1 - i, 0)), _full((1, LANES)),
                  pl.BlockSpec((HEADS, tm, LANES), lambda i: (0, n_b - 1 - i, 0)),
                  pl.BlockSpec((HEADS, 1, tm), lambda i: (0, 0, n_b - 1 - i))],
        out_specs=[pl.BlockSpec((1, tm, LANES), lambda i: (0, n_b - 1 - i, 0)), _full((SUBLANES, LANES))],
        out_shape=[jax.ShapeDtypeStruct((1, s, LANES), BF16), jax.ShapeDtypeStruct((SUBLANES, LANES), F32)],
        scratch_shapes=[pltpu.VMEM((1, LANES), F32)],
    )(f, bias, dka, dcq)


def _causal_pairs(n_t, key_major):
    if key_major:
        pairs = [(qi, ki) for ki in range(n_t) for qi in range(ki, n_t)]
    else:
        pairs = [(qi, ki) for qi in range(n_t) for ki in range(qi + 1)]
    return (jnp.array([p[0] for p in pairs], jnp.int32), jnp.array([p[1] for p in pairs], jnp.int32))


def _lane_const(t, lo, hi, value):
    lane = lax.broadcasted_iota(jnp.int32, (t, LANES), 1)
    return jnp.where((lane >= lo) & (lane < hi), value, 0.0).astype(BF16)


def _att_specs(t, nh):
    qmain = pl.BlockSpec((t, nh * HEAD_DIM), lambda h, p, qt, kt: (qt[p], h))
    kmain = pl.BlockSpec((t, nh * HEAD_DIM), lambda h, p, qt, kt: (kt[p], h))
    qside = pl.BlockSpec((nh, t, LANES), lambda h, p, qt, kt: (h, qt[p], 0))
    kside = pl.BlockSpec((nh, t, LANES), lambda h, p, qt, kt: (h, kt[p], 0))
    return qmain, kmain, qside, kside


def _fox_fwd(q, qa, k, ka, v, qo, name):
    s = q.shape[0]
    t = _tile(s, ATT_TILE)
    sub = t // ATT_SPLIT
    nh = ATT_FWD_HEADS
    qt, kt = _causal_pairs(s // t, key_major=False)

    def body(qt_ref, kt_ref, q_ref, qa_ref, k_ref, ka_ref, v_ref, og_ref, o_ref, y_ref, qab_ref, m_s, l_s, acc_s):
        pid = pl.program_id(1)
        qi, ki = qt_ref[pid], kt_ref[pid]

        @pl.when(ki == 0)
        def _():
            m_s[...] = jnp.full_like(m_s, NEG_INF)
            l_s[...] = jnp.zeros_like(l_s)
            acc_s[...] = jnp.zeros_like(acc_s)

        def step(diagonal):
            for hh in range(nh):
                hc = slice(hh * HEAD_DIM, (hh + 1) * HEAD_DIM)
                kc = jnp.concatenate([k_ref[:, hc], ka_ref[hh]], axis=1)
                vc = jnp.concatenate([v_ref[:, hc], _lane_const(t, 0, 1, 1.0)], axis=1)
                for r in range(ATT_SPLIT):
                    rows = slice(r * sub, (r + 1) * sub)
                    n_k = (r + 1) * sub if diagonal else t
                    sc = _dot_nt(jnp.concatenate([q_ref[rows, hc], qa_ref[hh, rows]], axis=1), kc[:n_k])
                    if diagonal:
                        sc = jnp.where(lax.broadcasted_iota(jnp.int32, (sub, n_k), 1)
                                       <= lax.broadcasted_iota(jnp.int32, (sub, n_k), 0) + r * sub, sc, NEG_INF)
                    m_old = m_s[hh, rows]
                    m_new = jnp.maximum(m_old, jnp.max(sc, axis=-1, keepdims=True))
                    alpha = jnp.exp2(m_old - m_new)
                    pv = _dot(jnp.exp2(sc - m_new[:, 0:1]).astype(BF16), vc[:n_k])
                    acc_s[hh, rows] = alpha * acc_s[hh, rows] + pv[:, :HEAD_DIM]
                    l_s[hh, rows] = alpha * l_s[hh, rows] + pv[:, HEAD_DIM:]
                    m_s[hh, rows] = m_new

        @pl.when(ki < qi)
        def _():
            step(False)

        @pl.when(ki == qi)
        def _():
            step(True)
            lane = lax.broadcasted_iota(jnp.int32, (t, LANES), 1)
            for hh in range(nh):
                hc = slice(hh * HEAD_DIM, (hh + 1) * HEAD_DIM)
                l = l_s[hh, :, 0:1]
                o = acc_s[hh] / l
                o_ref[:, hc] = o
                y_ref[:, hc] = (o * _sig(og_ref[:, hc])).astype(BF16)
                qab_ref[hh] = qa_ref[hh] + _side(lane, Q_LSE, -(m_s[hh, :, 0:1] + jnp.log2(l))).astype(BF16)

    qmain, kmain, qside, kside = _att_specs(t, nh)
    return pl.pallas_call(
        body, name=name,
        grid_spec=pltpu.PrefetchScalarGridSpec(
            num_scalar_prefetch=2, grid=(HEADS // nh, qt.shape[0]),
            in_specs=[qmain, qside, kmain, kside, kmain,
                      pl.BlockSpec((t, nh * HEAD_DIM), lambda h, p, qt, kt: (qt[p], HEADS // nh + h))],
            out_specs=[qmain, qmain, qside],
            scratch_shapes=[pltpu.VMEM((nh, t, LANES), F32), pltpu.VMEM((nh, t, LANES), F32),
                            pltpu.VMEM((nh, t, HEAD_DIM), F32)]),
        out_shape=[jax.ShapeDtypeStruct((s, D_MODEL), F32), jax.ShapeDtypeStruct((s, D_MODEL), BF16),
                   jax.ShapeDtypeStruct((HEADS, s, LANES), BF16)],
    )(qt, kt, q, qa, k, ka, v, qo)


def _fox_gate_bwd(o, qo, dout, w_out, name):
    s = o.shape[0]
    tm = _tile(s, ROW_TILE)

    def body(o_ref, og_ref, dout_ref, w_ref, do_ref, dg_ref, dl_ref):
        ov, dyv = o_ref[...], _dot_nt(dout_ref[0], w_ref[0])
        sg = _sig(og_ref[...])
        do = (dyv * sg).astype(BF16)
        do_ref[...] = do
        dg_ref[...] = (dyv * ov * sg * (1.0 - sg)).astype(BF16)
        prod = do.astype(F32) * ov
        lane = lax.broadcasted_iota(jnp.int32, (tm, LANES), 1)
        for h in range(HEADS):
            delta = jnp.sum(prod[:, h * HEAD_DIM:(h + 1) * HEAD_DIM], axis=-1, keepdims=True)
            dl_ref[h] = _side(lane, 0, delta).astype(BF16)

    row = pl.BlockSpec((tm, D_MODEL), lambda i: (i, 0))
    return pl.pallas_call(
        body, name=name, grid=(s // tm,),
        in_specs=[row, pl.BlockSpec((tm, D_MODEL), lambda i: (i, 1)),
                  pl.BlockSpec((1, tm, D_MODEL), lambda i: (0, i, 0)), _full(w_out.shape)],
        out_specs=[row, row, pl.BlockSpec((HEADS, tm, LANES), lambda i: (0, i, 0))],
        out_shape=[jax.ShapeDtypeStruct((s, D_MODEL), BF16), jax.ShapeDtypeStruct((s, D_MODEL), BF16),
                   jax.ShapeDtypeStruct((HEADS, s, LANES), BF16)],
    )(o, qo, dout, w_out)


def _fox_bwd(q, qab, k, ka, v, do, doa, name):
    s = q.shape[0]
    t = _tile(s, ATT_TILE)
    n_t = s // t
    sub = t // ATT_SPLIT
    nh = ATT_BWD_HEADS
    qt, kt = _causal_pairs(n_t, key_major=True)

    def body(qt_ref, kt_ref, q_ref, qab_ref, k_ref, ka_ref, v_ref, do_ref, doa_ref, dk_ref, dv_ref, dka_ref, dq_hbm,
             dcq_hbm, dk_s, dv_s, dq_ref, dcq_ref):
        group, pid = pl.program_id(0), pl.program_id(1)
        qi, ki = qt_ref[pid], kt_ref[pid]

        @pl.when(pid == 0)
        def _():
            dq_ref[...] = jnp.zeros_like(dq_ref)
            dcq_ref[...] = jnp.zeros_like(dcq_ref)

        @pl.when(qi == ki)
        def _():
            dk_s[...] = jnp.zeros_like(dk_s)
            dv_s[...] = jnp.zeros_like(dv_s)

        def step(diagonal):
            for hh in range(nh):
                hc = slice(hh * HEAD_DIM, (hh + 1) * HEAD_DIM)
                kc = jnp.concatenate([k_ref[:, hc], ka_ref[hh]], axis=1)
                vc = jnp.concatenate([v_ref[:, hc], _lane_const(t, 0, 3, -1.0)], axis=1)
                for r in range(ATT_SPLIT):
                    cols = slice(r * sub, (r + 1) * sub)
                    n_k = (r + 1) * sub if diagonal else t
                    qc = jnp.concatenate([q_ref[cols, hc], qab_ref[hh, cols]], axis=1)
                    sc = _dot_nt(kc[:n_k], qc)
                    if diagonal:
                        sc = jnp.where(lax.broadcasted_iota(jnp.int32, (n_k, sub), 0)
                                       <= lax.broadcasted_iota(jnp.int32, (n_k, sub), 1) + r * sub, sc, NEG_INF)
                    p = jnp.exp2(sc)
                    dov = do_ref[cols, hc]
                    dp = _dot_nt(vc[:n_k], jnp.concatenate([dov, doa_ref[hh, cols]], axis=1))
                    ds = (p * dp).astype(BF16)
                    dv_s[hh, 0:n_k] += _dot(p.astype(BF16), dov)
                    dk_s[hh, 0:n_k] += _dot(ds, qc)
                    q_rows = pl.ds(pl.multiple_of(qi * t + r * sub, sub), sub)
                    dq_ref[hh, q_rows, :] += _dot_tn(ds, k_ref[0:n_k, hc])
                    dcq_ref[hh, qi * ATT_SPLIT + r] += jnp.sum(ds.astype(F32), axis=0, keepdims=True)

        @pl.when(qi > ki)
        def _():
            step(False)

        @pl.when(qi == ki)
        def _():
            step(True)

        @pl.when(qi == n_t - 1)
        def _():
            for hh in range(nh):
                hc = slice(hh * HEAD_DIM, (hh + 1) * HEAD_DIM)
                dk_ref[:, hc] = dk_s[hh, :, :HEAD_DIM] * (1.0 / LOG2E)
                dka_ref[hh] = dk_s[hh, :, HEAD_DIM:]
                dv_ref[:, hc] = dv_s[hh].astype(BF16)

        @pl.when(pid == qt.shape[0] - 1)
        def _():
            pltpu.sync_copy(dq_ref, dq_hbm.at[pl.ds(group * nh, nh)])
            pltpu.sync_copy(dcq_ref, dcq_hbm.at[pl.ds(group * nh, nh)])

    qmain, kmain, qside, kside = _att_specs(t, nh)
    in_hbm = pl.BlockSpec(memory_space=pltpu.HBM)
    return pl.pallas_call(
        body, name=name,
        grid_spec=pltpu.PrefetchScalarGridSpec(
            num_scalar_prefetch=2, grid=(HEADS // nh, qt.shape[0]),
            in_specs=[qmain, qside, kmain, kside, kmain, qmain, qside],
            out_specs=[kmain, pl.BlockSpec((None, t, nh * HEAD_DIM), lambda h, p, qt, kt: (0, kt[p], h)), kside,
                       in_hbm, in_hbm],
            scratch_shapes=[pltpu.VMEM((nh, t, 2 * HEAD_DIM), F32), pltpu.VMEM((nh, t, HEAD_DIM), F32),
                            pltpu.VMEM((nh, s, HEAD_DIM), F32), pltpu.VMEM((nh, s // sub, 1, sub), F32)]),
        out_shape=[jax.ShapeDtypeStruct((s, D_MODEL), F32), jax.ShapeDtypeStruct((1, s, D_MODEL), BF16),
                   jax.ShapeDtypeStruct((HEADS, s, LANES), F32), jax.ShapeDtypeStruct((HEADS, s, HEAD_DIM), F32),
                   jax.ShapeDtypeStruct((HEADS, s // sub, 1, sub), F32)],
    )(qt, kt, q, qab, k, ka, v, do, doa)


def _mm_residual_premix(a, w, x, gate, mods, name):
    s, k = a.shape
    dm = x.shape[1]
    tm = _tile(s, ROW_TILE)

    def body(*refs):
        a_ref, w_ref, x_ref, g_ref = refs[:4]
        mod_refs = refs[4:4 + 2 * len(mods)]
        y_ref, xn_ref = refs[4 + 2 * len(mods):6 + 2 * len(mods)]
        h_refs = refs[6 + 2 * len(mods):]
        y = _dot(a_ref[...], w_ref[0])
        y_ref[...] = y
        xv = x_ref[...] + g_ref[...] * y
        xn_ref[...] = xv
        nrm = xv * lax.rsqrt(jnp.mean(xv * xv, axis=-1, keepdims=True) + EPS)
        for t, h_ref in enumerate(h_refs):
            h_ref[...] = (nrm * (1.0 + mod_refs[2 * t + 1][...]) + mod_refs[2 * t][...]).astype(BF16)

    row = pl.BlockSpec((tm, dm), lambda i: (i, 0))
    vec = _full((1, dm))
    outs = pl.pallas_call(
        body, name=name, grid=(s // tm,),
        in_specs=[pl.BlockSpec((tm, k), lambda i: (i, 0)), _full(w.shape), row, vec] + [vec] * (2 * len(mods)),
        out_specs=[row] * (2 + len(mods)),
        out_shape=[jax.ShapeDtypeStruct((s, dm), F32)] * 2 + [jax.ShapeDtypeStruct((s, dm), BF16)] * len(mods),
    )(a, w, x, gate, *[v for m in mods for v in m])
    return outs[0], outs[1], list(outs[2:])


def _mm_loss_head(a, w, x, gate, target, name):
    s, k = a.shape
    dm = x.shape[1]
    tm = _tile(s, ROW_TILE)

    def body(a_ref, w_ref, x_ref, g_ref, t_ref, sq_ref, do_ref, dy_ref, dg_ref):
        @pl.when(pl.program_id(0) == 0)
        def _():
            sq_ref[...] = jnp.zeros_like(sq_ref)
            dg_ref[...] = jnp.zeros_like(dg_ref)

        y, gv = _dot(a_ref[...], w_ref[0]), g_ref[...]
        err = x_ref[...] + gv * y - t_ref[...]
        sq_ref[...] += _colsum8(err * err)
        dout = err * (1.0 / dm)
        do_ref[...] = dout
        dy_ref[0] = (dout * gv).astype(BF16)
        dg_ref[...] += _colsum8(dout * y)

    row = pl.BlockSpec((tm, dm), lambda i: (i, 0))
    acc = _full((SUBLANES, dm))
    return pl.pallas_call(
        body, name=name, grid=(s // tm,),
        in_specs=[pl.BlockSpec((tm, k), lambda i: (i, 0)), _full(w.shape), row, _full((1, dm)), row],
        out_specs=[acc, row, pl.BlockSpec((1, tm, dm), lambda i: (0, i, 0)), acc],
        out_shape=[jax.ShapeDtypeStruct((SUBLANES, dm), F32), jax.ShapeDtypeStruct((s, dm), F32),
                   jax.ShapeDtypeStruct((1, s, dm), BF16), jax.ShapeDtypeStruct((SUBLANES, dm), F32)],
    )(a, w, x, gate, target)


def _ffn_inner(h, w_up, conv_w, conv_b, tag):
    s, dm = h.shape
    half = w_up.shape[2]
    f = 2 * half
    tm = _tile(s, FFN_ROWS)

    def body(h_ref, w_ref, cw_ref, cb_ref, u_ref, c_ref, a_ref, carry):
        @pl.when(pl.program_id(0) == 0)
        def _():
            carry[...] = jnp.zeros_like(carry)

        hv = h_ref[...]
        for j in range(2):
            cols = slice(j * half, (j + 1) * half)
            conv = []
            for g in range(2):
                ub = _dot(hv, w_ref[2 * g + j]).astype(BF16)
                u_ref[g, :, cols] = ub
                uf = ub.astype(F32)
                e = jnp.concatenate([carry[g, j], uf], axis=0)
                carry[g, j] = uf[tm - SUBLANES:tm]
                conv.append(_conv_taps(e, cw_ref[g][:, cols], cb_ref[g][:, cols])[SUBLANES:])
                c_ref[g, :, cols] = conv[g].astype(BF16)
            a_ref[:, cols] = (conv[0] * _sig(conv[0]) * conv[1]).astype(BF16)

    pair = pl.BlockSpec((2, tm, f), lambda i: (0, i, 0))
    return pl.pallas_call(
        body, name=tag + "_up_convglu", grid=(s // tm,),
        in_specs=[pl.BlockSpec((tm, dm), lambda i: (i, 0)), _full(w_up.shape), _full(conv_w.shape), _full(conv_b.shape)],
        out_specs=[pair, pair, pl.BlockSpec((tm, f), lambda i: (i, 0))],
        out_shape=[jax.ShapeDtypeStruct((2, s, f), BF16)] * 2 + [jax.ShapeDtypeStruct((s, f), BF16)],
        scratch_shapes=[pltpu.VMEM((2, 2, SUBLANES, half), F32)],
    )(h, w_up, conv_w, conv_b)


def _weight_grad_first(a, d, p_n, name):
    return lax.optimization_barrier((_mm_tn(a, d, p_n, name), d))


def _ffn_backward(dx_out, dffn, x_mid, scale, saved, w_up, conv_w, conv_b, w_down, mixer, tag):
    h, u, c, a = saved
    dw_down, dffn = _weight_grad_first(a, dffn, 1, tag + "_down_dw")
    du, dconv = _convglu_bwd(u, c, dffn, w_down, conv_w, tag + "_convglu_bwd")
    dw_up, du = _weight_grad_first(h, du, N_CHIPS, tag + "_up_dw")
    dx_mid, [(dshift, dscale)], dy, dgate_mixer = _premix_bwd(x_mid, [(scale, [(du, w_up)])], dx_out,
                                                              tag + "_premix_bwd", branch=mixer)
    return dx_mid, dy, dgate_mixer, dw_up, dw_down, dict(shift=dshift, scale=dscale, conv=dconv)


def _local_step(x, target, mods, lb, vecs, weights_at):
    m0, m1, mk = mods["l0"], mods["l1"], mods["kv"]
    wts, x = weights_at("mixer0", x)
    h0, proj = _premix_proj(x, m0[0], m0[1], wts["a_w_in"], "l0_premix_in")
    o_a, yp, states = _hgrn_fwd(proj, lb, vecs["a_norm_g"], "l0_hgrn")
    more, yp = weights_at("ffn0", yp)
    wts.update(more)
    y0, x1, [hf0] = _mm_residual_premix(yp, wts["a_w_out"], x, m0[2], [(m0[3], m0[4])], "l0_out")
    u0, c0, a0 = _ffn_inner(hf0, wts["up0"], vecs["conv_w0"], vecs["conv_b0"], "l0_ffn")
    saved0 = (hf0, u0, c0, a0)
    ffn0, x2, [hk, h1] = _mm_residual_premix(a0, wts["down0"], x1, m0[5], [(mk[0], mk[1]), (m1[0], m1[1])],
                                             "l0_ffn_down")
    more, hk = weights_at("layer1", hk)
    wts.update(more)
    k_raw, k_sh, v_sh, f_raw = _kv_proj(hk, wts["kv_k"], wts["kv_v"], wts["kv_f"], vecs["k_norm_g"], "kv_proj")
    qa, ka = _fcum_fwd(f_raw, vecs["kv_b_f"], "kv_fcum")
    q_scale = HEAD_DIM ** -0.5
    qo, q = _proj_headnorm(h1, wts["b_w_q"], vecs["q_norm_g"], q_scale * LOG2E, "l1_q")
    o_b, og, qab = _fox_fwd(q, qa, k_sh, ka, v_sh, qo, "l1_fox")
    y1, x3, [hf1] = _mm_residual_premix(og, wts["b_w_out"], x2, m1[2], [(m1[3], m1[4])], "l1_out")
    u1, c1, a1 = _ffn_inner(hf1, wts["up1"], vecs["conv_w1"], vecs["conv_b1"], "l1_ffn")
    saved1 = (hf1, u1, c1, a1)
    sq, dx4, dffn1, dg2_1 = _mm_loss_head(a1, wts["down1"], x3, m1[5], target, "l1_ffn_down")

    big, small = {}, {}
    dx3, dy1, dg1_1, big["up1"], big["down1"], s_ffn1 = _ffn_backward(
        dx4, dffn1, x3, m1[4], saved1, wts["up1"], vecs["conv_w1"], vecs["conv_b1"], wts["down1"], (y1, m1[2]), "l1_ffn")
    big["b_w_out"], dy1 = _weight_grad_first(og, dy1, 1, "l1_out_dw")
    do_b, dgate_b, doa = _fox_gate_bwd(o_b, qo, dy1, wts["b_w_out"], "l1_out_dx_gate_bwd")
    dk, dv, dka, dq, dcq = _fox_bwd(q, qab, k_sh, ka, v_sh, do_b, doa, "l1_fox_bwd")
    dqo, dqg = _headnorm_bwd(qo, vecs["q_norm_g"], q_scale, dq, "l1_qnorm_bwd", extra=dgate_b)
    big["b_w_q"], dqo = _weight_grad_first(h1, dqo, N_CHIPS, "l1_q_dw")
    dk_raw, dkg = _headnorm_bwd(k_raw, vecs["k_norm_g"], 1.0, dk, "kv_knorm_bwd")
    dz, dbf = _fcum_bwd(f_raw, vecs["kv_b_f"], dka, dcq.reshape(HEADS, 1, -1), "kv_fcum_bwd")
    big["kv_k"], dk_raw = _weight_grad_first(hk, dk_raw, 1, "kv_k_dw")
    big["kv_v"], dv = _weight_grad_first(hk, dv, 1, "kv_v_dw")
    big["kv_f"], dz = _weight_grad_first(hk, dz, 1, "kv_f_dw")
    kv_pairs = [(dk_raw, wts["kv_k"]), (dv, wts["kv_v"]), (dz, wts["kv_f"])]
    dx2, [(dsh1_1, dsc1_1), (dshk, dsck)], dffn0, dg2_0 = _premix_bwd(
        x2, [(m1[1], [(dqo, wts["b_w_q"])]), (mk[1], kv_pairs)], dx3, "l1_kv_premix_bwd", branch=(ffn0, m0[5]))
    dx1, dy0, dg1_0, big["up0"], big["down0"], s_ffn0 = _ffn_backward(
        dx2, dffn0, x1, m0[4], saved0, wts["up0"], vecs["conv_w0"], vecs["conv_b0"], wts["down0"], (y0, m0[2]), "l0_ffn")
    big["a_w_out"], dy0 = _weight_grad_first(yp, dy0, 1, "l0_out_dw")
    dproj, dlb, dng = _hgrn_bwd(proj, lb, vecs["a_norm_g"], o_a, states, dy0, wts["a_w_out"], "l0_out_dx_hgrn_bwd")
    grad_x, [(dsh1_0, dsc1_0)] = _premix_bwd(x, [(m0[1], [(dproj, wts["a_w_in"])])], dx1, "l0_premix_bwd")
    dproj, _ = lax.optimization_barrier((dproj, (dsh1_0, dsc1_0)))
    big["a_w_in"] = _mm_tn(h0, dproj, N_CHIPS, "l0_in_dw")

    small["mod_l0"] = [dsh1_0, dsc1_0, dg1_0, s_ffn0["shift"], s_ffn0["scale"], dg2_0]
    small["mod_l1"] = [dsh1_1, dsc1_1, dg1_1, s_ffn1["shift"], s_ffn1["scale"], dg2_1]
    small["mod_kv"] = [dshk, dsck]
    small["conv0"], small["conv1"] = s_ffn0["conv"], s_ffn1["conv"]
    small["a_norm_g"], small["k_norm_g"], small["q_norm_g"] = dng, dkg, dqg
    small["kv_b_f"], small["lb"] = dbf, dlb
    marks = {"attention_bwd": dk, "ffn0_bwd": dx1, "mixer0_bwd": grad_x}
    return sq, grad_x, big, small, marks


COMM_CHUNK_ELEMS = 256 * 1024


def _place():
    x, y, c = lax.axis_index("x"), lax.axis_index("y"), lax.axis_index("c")
    chips = [(1 - x, y), (x, 1 - y), (1 - x, 1 - y)]
    return x, y, c, (x, y, 1 - c), chips


def _chunk_rows(rows, cols):
    best = BF16_ROWS
    for r in range(BF16_ROWS, rows + 1, BF16_ROWS):
        if rows % r == 0 and r * cols <= COMM_CHUNK_ELEMS:
            best = r
    assert rows % best == 0, (rows, cols)
    return best


def _allgather8(block, name):
    m_per, n = block.shape

    def body(x_ref, out_ref, send_sems, recv_sems, local_sem):
        x, y, c, sibling, chips = _place()
        me = (x, y, c)

        def rows(px, py, pc):
            return out_ref.at[pl.ds((4 * px + 2 * py + pc) * m_per, m_per), :]

        def copy(k, blk, to, src=None):
            return pltpu.make_async_remote_copy(
                src_ref=rows(*blk) if src is None else src, dst_ref=rows(*blk),
                send_sem=send_sems.at[k], recv_sem=recv_sems.at[k], device_id=to, device_id_type=MESH)

        mine = pltpu.make_async_copy(x_ref, rows(*me), local_sem)
        mine.start()
        first = [copy(0, me, sibling, src=x_ref)]
        first += [copy(1 + j, me, (*chip, c), src=x_ref) for j, chip in enumerate(chips)]
        for cp in first:
            cp.start()
        passed = [copy(4 + j, (*chip, c), sibling) for j, chip in enumerate(chips)]
        for j, chip in enumerate(chips):
            copy(1 + j, (*chip, c), me).wait_recv()
            passed[j].start()
        copy(0, sibling, me).wait_recv()
        for j, chip in enumerate(chips):
            copy(4 + j, (*chip, 1 - c), me).wait_recv()
        for cp in first + passed:
            cp.wait_send()
        mine.wait()

    return pl.pallas_call(
        body, name=name, out_shape=jax.ShapeDtypeStruct((N_DEV * m_per, n), block.dtype),
        in_specs=[pl.BlockSpec(memory_space=pltpu.VMEM)], out_specs=pl.BlockSpec(memory_space=pltpu.VMEM),
        scratch_shapes=[pltpu.SemaphoreType.DMA((7,)), pltpu.SemaphoreType.DMA((7,)), pltpu.SemaphoreType.DMA],
    )(block)


def _cast_own_block(shards, layer, chip, name):
    _, r, cols = shards.shape
    rows = _chunk_rows(r, cols)

    def body(chip_ref, w_ref, o_ref):
        o_ref[...] = w_ref[...].astype(BF16)

    return pl.pallas_call(
        body, name=name,
        grid_spec=pltpu.PrefetchScalarGridSpec(
            num_scalar_prefetch=1, grid=(r // rows,),
            in_specs=[pl.BlockSpec((None, rows, cols), lambda i, chip_ref: (layer, i, 0))],
            out_specs=pl.BlockSpec((None, rows, cols), lambda i, chip_ref: (chip_ref[0], i, 0))),
        out_shape=jax.ShapeDtypeStruct((N_CHIPS, r, cols), BF16),
    )(chip, shards)


def _sequencer_gather(bufs, name, collective_id):
    n_t = len(bufs)
    dims = [b.shape[1:] for b in bufs]
    refs = [jax.new_ref(b, memory_space=pltpu.MemorySpace.HBM) for b in bufs]

    @pl.kernel(mesh=plsc.ScalarSubcoreMesh(axis_name="sequencer", num_cores=1), name=name,
               scratch_types=[pltpu.SemaphoreType.DMA((n_t,)), pltpu.SemaphoreType.DMA((3 * n_t,)),
                              pltpu.SemaphoreType.DMA((n_t,)), pltpu.SemaphoreType.DMA((n_t,))],
               compiler_params=pltpu.CompilerParams(collective_id=collective_id))
    def launch(send_ici, recv_ici, send_d2d, recv_d2d):
        x, y, c, sibling, chips = _place()
        p_me = 2 * x + y
        peers = [sibling] + [(cx, cy, c) for cx, cy in chips]
        barrier = pltpu.get_barrier_semaphore()
        for peer in peers:
            pl.semaphore_signal(barrier, inc=1, device_id=peer, device_id_type=MESH)
        pl.semaphore_wait(barrier, len(peers))

        def waiter(t, sem_s, sem_r):
            win = refs[t].at[pl.ds(0, 3), pl.ds(0, dims[t][0] // 2), :]
            return pltpu.make_async_remote_copy(src_ref=win, dst_ref=win, send_sem=sem_s.at[t], recv_sem=sem_r.at[t],
                                                device_id=sibling, device_id_type=MESH)

        def half_copy(t, chip_idx, to, sem_s, sem_r, k):
            r2 = dims[t][0] // 2
            win = refs[t].at[chip_idx, pl.ds(c * r2, r2), :]
            return pltpu.make_async_remote_copy(src_ref=win, dst_ref=win, send_sem=sem_s.at[t], recv_sem=sem_r.at[k],
                                                device_id=to, device_id_type=MESH)

        for t in range(n_t):
            for j, (cx, cy) in enumerate(chips):
                half_copy(t, p_me, (cx, cy, c), send_ici, recv_ici, 3 * t + j).start()
        for t in range(n_t):
            for j, (cx, cy) in enumerate(chips):
                half_copy(t, 2 * cx + cy, (cx, cy, c), send_ici, recv_ici, 3 * t + j).wait_recv()
                half_copy(t, 2 * cx + cy, sibling, send_d2d, recv_d2d, t).start()
        for t in range(n_t):
            waiter(t, send_d2d, recv_d2d).wait_recv()
            waiter(t, send_ici, recv_ici).wait_send()
            waiter(t, send_d2d, recv_d2d).wait_send()

    launch()
    return [r[...] for r in refs]


def _sequencer_allgather8(block, dev, name, collective_id):
    m_per, n = block.shape
    src = jax.new_ref(block, memory_space=pltpu.MemorySpace.HBM)
    out = jax.empty_ref(jax.ShapeDtypeStruct((N_DEV * m_per, n), block.dtype), memory_space=pltpu.MemorySpace.HBM)

    @pl.kernel(mesh=plsc.ScalarSubcoreMesh(axis_name="sequencer", num_cores=1), name=name,
               scratch_types=[pltpu.SemaphoreType.DMA((7,))] * 2,
               compiler_params=pltpu.CompilerParams(collective_id=collective_id))
    def launch(send_sems, recv_sems):
        x, y, c, sibling, chips = _place()
        me = (x, y, c)
        _handshake([sibling] + [(cx, cy, c) for cx, cy in chips])

        def rows(px, py, pc):
            return out.at[pl.ds((4 * px + 2 * py + pc) * m_per, m_per), :]

        def copy(k, blk, to, from_src=False):
            return pltpu.make_async_remote_copy(
                src_ref=src if from_src else rows(*blk), dst_ref=rows(*blk),
                send_sem=send_sems.at[k], recv_sem=recv_sems.at[k], device_id=to, device_id_type=MESH)

        first = [copy(0, me, sibling, True)] + [copy(1 + j, me, (*chip, c), True) for j, chip in enumerate(chips)]
        for cp in first:
            cp.start()
        passed = [copy(4 + j, (*chip, c), sibling) for j, chip in enumerate(chips)]
        for j, chip in enumerate(chips):
            copy(1 + j, (*chip, c), me).wait_recv()
            passed[j].start()
        copy(0, sibling, me).wait_recv()
        for j, chip in enumerate(chips):
            copy(4 + j, (*chip, 1 - c), me).wait_recv()
        for cp in first + passed:
            cp.wait_send()

    launch()
    return lax.dynamic_update_slice(out[...], block, (dev * m_per, 0))


def _others():
    x, y, c = lax.axis_index("x"), lax.axis_index("y"), lax.axis_index("c")
    flip = lambda v, f: 1 - v if f else v
    return [(flip(x, fx), flip(y, fy), flip(c, fc))
            for fx in (0, 1) for fy in (0, 1) for fc in (0, 1) if (fx, fy, fc) != (0, 0, 0)]


def _handshake(peers):
    barrier = pltpu.get_barrier_semaphore()
    for peer in peers:
        pl.semaphore_signal(barrier, inc=1, device_id=peer, device_id_type=MESH)
    pl.semaphore_wait(barrier, len(peers))


def _sequencer_scatter(parts, name, collective_id):
    n_t = len(parts)
    dims = [p.shape[1:] for p in parts]
    srcs = [jax.new_ref(p, memory_space=pltpu.MemorySpace.HBM) for p in parts]
    inboxes = [jax.empty_ref(jax.ShapeDtypeStruct((N_DEV, r // 2, cols), BF16), memory_space=pltpu.MemorySpace.HBM)
               for r, cols in dims]

    @pl.kernel(mesh=plsc.ScalarSubcoreMesh(axis_name="sequencer", num_cores=1), name=name,
               scratch_types=[pltpu.SemaphoreType.DMA((n_t,))] * 2,
               compiler_params=pltpu.CompilerParams(collective_id=collective_id))
    def launch(send_sem, recv_sem):
        x, y, c = lax.axis_index("x"), lax.axis_index("y"), lax.axis_index("c")
        me = 4 * x + 2 * y + c
        peers = _others()
        _handshake(peers)
        for t in range(n_t):
            h = dims[t][0] // 2
            for qx, qy, qc in peers:
                pltpu.make_async_remote_copy(
                    src_ref=srcs[t].at[2 * qx + qy, pl.ds(qc * h, h), :], dst_ref=inboxes[t].at[me],
                    send_sem=send_sem.at[t], recv_sem=recv_sem.at[t], device_id=(qx, qy, qc), device_id_type=MESH).start()
        for t in range(n_t):
            win = inboxes[t].at[pl.ds(0, N_DEV - 1)]
            both = pltpu.make_async_remote_copy(src_ref=win, dst_ref=win, send_sem=send_sem.at[t],
                                                recv_sem=recv_sem.at[t], device_id=peers[0], device_id_type=MESH)
            both.wait_recv()
            both.wait_send()

    launch()
    return [b[...] for b in inboxes]


def _sum_pieces(part, inbox, place, name):
    _, r, cols = part.shape
    h = r // 2
    rows = _chunk_rows(h, cols)
    steps = h // rows

    def body(place_ref, own_ref, in_ref, o_ref):
        dev = place_ref[2]
        own = own_ref[...].astype(F32)
        acc = jnp.zeros((rows, cols), F32)
        for d in range(N_DEV):
            acc = acc + jnp.where(dev == d, own, in_ref[d].astype(F32))
        o_ref[...] = acc

    return pl.pallas_call(
        body, name=name,
        grid_spec=pltpu.PrefetchScalarGridSpec(
            num_scalar_prefetch=1, grid=(steps,),
            in_specs=[pl.BlockSpec((None, rows, cols), lambda i, pr: (pr[0], pr[1] * steps + i, 0)),
                      pl.BlockSpec((N_DEV, rows, cols), lambda i, pr: (0, i, 0))],
            out_specs=pl.BlockSpec((rows, cols), lambda i, pr: (pr[1] * steps + i, 0))),
        out_shape=jax.ShapeDtypeStruct((r, cols), F32),
    )(place, part, inbox)


def _sequencer_swap_halves(halves, name, collective_id):
    n_t = len(halves)
    refs = [jax.new_ref(a, memory_space=pltpu.MemorySpace.HBM) for a in halves]

    @pl.kernel(mesh=plsc.ScalarSubcoreMesh(axis_name="sequencer", num_cores=1), name=name,
               scratch_types=[pltpu.SemaphoreType.DMA((n_t,))] * 2,
               compiler_params=pltpu.CompilerParams(collective_id=collective_id))
    def launch(send_sem, recv_sem):
        x, y, c = lax.axis_index("x"), lax.axis_index("y"), lax.axis_index("c")
        sibling = (x, y, 1 - c)
        _handshake([sibling])
        copies = []
        for t in range(n_t):
            h = halves[t].shape[0] // 2
            win = refs[t].at[pl.ds(c * h, h), :]
            copies.append(pltpu.make_async_remote_copy(src_ref=win, dst_ref=win, send_sem=send_sem.at[t],
                                                       recv_sem=recv_sem.at[t], device_id=sibling, device_id_type=MESH))
            copies[-1].start()
        for cp in copies:
            cp.wait()

    launch()
    return [r[...] for r in refs]


def _cond_rows(c16, w, act, name):
    n_l, dm, wid = w.shape

    def body(c_ref, w_ref, o_ref, a_ref):
        cv = c_ref[...]
        if act:
            cv = cv * _sig(cv)
        a_ref[...] = cv
        o_ref[...] = _dot_f32(cv, w_ref[...])

    return pl.pallas_call(
        body, name=name, grid=(n_l,),
        in_specs=[_full((16, dm)), pl.BlockSpec((None, dm, wid), lambda l: (l, 0, 0))],
        out_specs=[pl.BlockSpec((None, 16, wid), lambda l: (l, 0, 0)), _full((16, dm))],
        out_shape=[jax.ShapeDtypeStruct((n_l, 16, wid), F32), jax.ShapeDtypeStruct((16, dm), F32)],
    )(c16, w)


def _outer_grad(ct, dm, name):
    n_l, kk, wid = dm.shape
    d_rows = ct.shape[0]

    def body(c_ref, d_ref, o_ref):
        o_ref[...] = _dot_f32(c_ref[...], d_ref[...])

    return pl.pallas_call(
        body, name=name, grid=(n_l,),
        in_specs=[_full((d_rows, kk)), pl.BlockSpec((None, kk, wid), lambda l: (l, 0, 0))],
        out_specs=pl.BlockSpec((None, d_rows, wid), lambda l: (l, 0, 0)),
        out_shape=jax.ShapeDtypeStruct((n_l, d_rows, wid), F32),
    )(ct, dm)


def _sum_devices(g, name):
    rows, n = g.shape

    def body(g_ref, o_ref):
        acc = g_ref[0:SUBLANES, :]
        for dev in range(1, N_DEV):
            acc = acc + g_ref[dev * SUBLANES:(dev + 1) * SUBLANES, :]
        o_ref[...] = acc

    return pl.pallas_call(body, name=name, out_shape=jax.ShapeDtypeStruct((SUBLANES, n), F32))(g)


def _adamw(w, g, m, v, name):
    shape = w.shape
    cols = shape[-1]
    rows = w.size // cols
    tr = rows
    for cand in range(SUBLANES, min(rows, 256) + 1, SUBLANES):
        if rows % cand == 0:
            tr = cand
    if rows * cols <= COMM_CHUNK_ELEMS:
        tr = rows
    c1 = 1.0 / (1.0 - ADAM_B1 ** ADAM_STEP)
    c2 = 1.0 / (1.0 - ADAM_B2 ** ADAM_STEP)

    def body(w_ref, g_ref, m_ref, v_ref, d_ref, mo_ref, vo_ref):
        gv = g_ref[...]
        m_new = ADAM_B1 * m_ref[...] + (1.0 - ADAM_B1) * gv
        v_new = ADAM_B2 * v_ref[...] + (1.0 - ADAM_B2) * (gv * gv)
        mo_ref[...] = m_new
        vo_ref[...] = v_new
        d_ref[...] = -ADAM_LR * ((m_new * c1) / (jnp.sqrt(v_new * c2) + ADAM_EPS) + ADAM_WD * w_ref[...])

    spec = pl.BlockSpec((tr, cols), lambda i: (i, 0))
    outs = pl.pallas_call(
        body, name=name, grid=(rows // tr,), in_specs=[spec] * 4, out_specs=[spec] * 3,
        out_shape=[jax.ShapeDtypeStruct((rows, cols), F32)] * 3,
    )(*[a.reshape(rows, cols) for a in (w, g, m, v)])
    return tuple(o.reshape(shape) for o in outs)


def _pad_cols(a, cols):
    return jnp.pad(a, [(0, 0)] * (a.ndim - 1) + [(0, cols - a.shape[-1])])


def _flat8(parts, width):
    v = jnp.concatenate([p.reshape(-1) for p in parts])
    return jnp.pad(v, (0, width - v.shape[0])).reshape(SUBLANES, width // SUBLANES)


KV_SHARD = 514
KV_SHARD_PAD = 640
BIG = ("a_w_in", "a_w_out", "kv_w", "b_w_q", "b_w_out", "up0", "up1", "down0", "down1")


def kernel(x, c, ada_w, ada_b, a_w_in, a_lb_logits, a_norm_g, a_w_out, kv_ada_w, kv_ada_b, kv_w, kv_b_f, k_norm_g, b_w_q, q_norm_g, b_w_out, ffn_w_up, ffn_conv_w, ffn_conv_b, ffn_w_down, loss_target, m_ada_w, m_ada_b, m_a_w_in, m_a_lb_logits, m_a_norm_g, m_a_w_out, m_kv_ada_w, m_kv_ada_b, m_kv_w, m_kv_b_f, m_k_norm_g, m_b_w_q, m_q_norm_g, m_b_w_out, m_ffn_w_up, m_ffn_conv_w, m_ffn_conv_b, m_ffn_w_down, v_ada_w, v_ada_b, v_a_w_in, v_a_lb_logits, v_a_norm_g, v_a_w_out, v_kv_ada_w, v_kv_ada_b, v_kv_w, v_kv_b_f, v_k_norm_g, v_b_w_q, v_q_norm_g, v_b_w_out, v_ffn_w_up, v_ffn_conv_w, v_ffn_conv_b, v_ffn_w_down):
    dm, ff = D_MODEL, D_FF
    ix, iy, ic = lax.axis_index("x"), lax.axis_index("y"), lax.axis_index("c")
    chip = 2 * ix + iy
    dev = 2 * chip + ic

    w1 = 10240
    g1 = _allgather8(_flat8([c, a_lb_logits, ffn_conv_w], w1), "gather_cond").reshape(N_DEV, w1)
    c_all = g1[:, :dm]
    per_chip = g1[0::2]
    lb_logits = per_chip[:, dm:dm + 512].reshape(N_CHIPS, 2, 256).transpose(1, 0, 2).reshape(2, dm)
    conv_w = per_chip[:, dm + 512:dm + 512 + 2 * CONV_W * FFN_COLS].reshape(N_CHIPS, 2, CONV_W, FFN_COLS)
    conv_w = conv_w.transpose(1, 2, 0, 3).reshape(2, CONV_W, 2, ff).transpose(0, 2, 1, 3)
    conv_b = ffn_conv_b.reshape(2, 2, 1, ff)
    lb = jax.nn.softmax(lb_logits, axis=0)[0:1]

    c16 = jnp.pad(c_all, ((0, 8), (0, 0)))
    mod_ada, c_act16 = _cond_rows(c16, ada_w, True, "mod_ada")
    mod_kv, _ = _cond_rows(c16, kv_ada_w[None], True, "mod_kv")
    mine = jnp.concatenate([mod_ada[0, :8], mod_ada[1, :8], mod_kv[0, :8]], axis=1)
    w2 = mine.shape[1]
    g2 = _allgather8(mine, "gather_mod").reshape(N_DEV, 8, w2)[0::2]
    my_rows = lax.dynamic_index_in_dim(g2, dev, axis=1, keepdims=False)
    mod0 = my_rows[:, 0:1536].reshape(6 * dm) + ada_b[0]
    mod1 = my_rows[:, 1536:3072].reshape(6 * dm) + ada_b[1]
    modk = my_rows[:, 3072:3584].reshape(2 * dm) + kv_ada_b
    mods = {"l0": [v.reshape(1, dm) for v in jnp.split(mod0, 6)],
            "l1": [v.reshape(1, dm) for v in jnp.split(mod1, 6)],
            "kv": [v.reshape(1, dm) for v in jnp.split(modk, 2)]}

    local = [(a_w_in, 0), (a_w_out, 0), (_pad_cols(kv_w, KV_SHARD_PAD)[None], 0), (b_w_q, 0), (b_w_out, 0),
             (ffn_w_up, 0), (ffn_w_up, 1), (ffn_w_down, 0), (ffn_w_down, 1)]
    chip_arr = chip.reshape(1).astype(jnp.int32)
    local = dict(zip(BIG, local))
    stages = {"mixer0": ("a_w_in",), "ffn0": ("a_w_out", "up0", "down0"),
              "layer1": ("kv_w", "b_w_q", "b_w_out", "up1", "down1")}
    arriving = {}

    def launch(stage, behind):
        shards = [local[n][0] for n in stages[stage]]
        if behind is not None:
            shards, _ = lax.optimization_barrier((shards, behind))
        own = [_cast_own_block(w, local[n][1], chip_arr, "cast_" + n) for n, w in zip(stages[stage], shards)]
        arriving[stage] = _sequencer_gather(own, "gather_" + stage, 1 + list(stages).index(stage))

    launch("mixer0", None)
    launch("ffn0", mod0)
    rowwise = lambda g: g.reshape(1, -1, dm)

    def weights_at(stage, token):
        if stage == "ffn0":
            launch("layer1", token)
        got, token = lax.optimization_barrier((arriving[stage], token))
        g = dict(zip(stages[stage], got))
        if stage == "mixer0":
            return {"a_w_in": g["a_w_in"]}, token
        if stage == "ffn0":
            return {"a_w_out": rowwise(g["a_w_out"]), "up0": g["up0"], "down0": rowwise(g["down0"])}, token
        s0, s1, s2, s3 = (g["kv_w"][p] for p in range(N_CHIPS))
        second = dm - KV_SHARD
        w_k = jnp.concatenate([s0[:, :KV_SHARD], s1[:, :second]], axis=1)
        w_v = jnp.concatenate([s1[:, second:KV_SHARD], s2[:, :KV_SHARD], s3[:, :KV_SHARD - HEADS]], axis=1)
        w_f = _pad_cols(s3[:, KV_SHARD - HEADS:KV_SHARD], LANES)
        return {"kv_k": w_k[None], "kv_v": w_v[None], "kv_f": w_f[None], "b_w_q": g["b_w_q"],
                "b_w_out": rowwise(g["b_w_out"]), "up1": g["up1"], "down1": rowwise(g["down1"])}, token

    vecs = {"a_norm_g": jnp.tile(a_norm_g, (1, HEADS)), "k_norm_g": jnp.tile(k_norm_g[None], (1, HEADS)),
            "q_norm_g": jnp.tile(q_norm_g, (1, HEADS)), "kv_b_f": _pad_cols(kv_b_f[None], LANES),
            "conv_w0": conv_w[0], "conv_b0": conv_b[0], "conv_w1": conv_w[1], "conv_b1": conv_b[1]}

    sq, grad_x, big, small, marks = _local_step(x[0], loss_target[0], mods, lb, vecs, weights_at)

    gk, gv, gf = big["kv_k"][0], big["kv_v"][0], big["kv_f"][0][:, :HEADS]
    second = dm - KV_SHARD
    kv_blocks = [gk[:, :KV_SHARD], jnp.concatenate([gk[:, KV_SHARD:], gv[:, :KV_SHARD - second]], axis=1),
                 gv[:, KV_SHARD - second:2 * KV_SHARD - second], jnp.concatenate([gv[:, 2 * KV_SHARD - second:], gf], axis=1)]
    kv_grad = jnp.stack([_pad_cols(b, KV_SHARD_PAD) for b in kv_blocks])
    chipwise = lambda g: g.reshape(N_CHIPS, -1, dm)
    parts = dict(zip(BIG, [big["a_w_in"], chipwise(big["a_w_out"]), kv_grad, big["b_w_q"], chipwise(big["b_w_out"]),
                           big["up0"], big["up1"], chipwise(big["down0"]), chipwise(big["down1"])]))
    place = jnp.stack([chip, ic, dev]).astype(jnp.int32)

    served = []
    boxes = {}

    groups = (("up1", "down1"), ("b_w_out", "b_w_q", "kv_w"), ("up0", "down0", "a_w_out"), ("a_w_in",))

    def scatter_group(k):
        mine = [parts[n] for n in groups[k]]
        if served:
            mine, _ = lax.optimization_barrier((mine, served[-1]))
        boxes[k] = _sequencer_scatter(mine, "scatter_grads_%d" % k, 4 + k)
        served.append(boxes[k])

    def sum_group(k, token):
        inboxes, _ = lax.optimization_barrier((boxes[k], token))
        return [_sum_pieces(parts[n], box, place, "sum_" + n) for n, box in zip(groups[k], inboxes)]

    def swap_group(k, halves, behind):
        halves, _ = lax.optimization_barrier((halves, behind))
        return dict(zip(groups[k], _sequencer_swap_halves(halves, "swap_grads_%d" % k, 8 + k)))

    for k in range(3):
        scatter_group(k)
    halves = [sum_group(0, marks["attention_bwd"]), sum_group(1, marks["ffn0_bwd"]), sum_group(2, marks["mixer0_bwd"])]

    fold = lambda a: a.sum(axis=0)
    heads = lambda a: fold(a).reshape(HEADS, HEAD_DIM).sum(axis=0)
    conv_flat = lambda a: a.sum(axis=2).transpose(1, 0, 2)
    pieces = ([fold(a) for a in small["mod_l0"]] + [fold(a) for a in small["mod_l1"]] + [fold(a) for a in small["mod_kv"]]
              + [conv_flat(small["conv0"]), conv_flat(small["conv1"]), heads(small["a_norm_g"]), heads(small["k_norm_g"]),
                 heads(small["q_norm_g"]), fold(small["kv_b_f"]), fold(small["lb"]),
                 0.5 * jnp.sum(sq).reshape(1) / dm])
    w3 = 61440
    small_vec, _ = lax.optimization_barrier((_flat8(pieces, w3), served[2]))
    g3 = _sequencer_allgather8(small_vec, dev, "gather_small", 12)
    served.append(g3)
    scatter_group(3)
    rs = {}
    for k in range(3):
        rs.update(swap_group(k, halves[k], g3))
    tot = _sum_devices(g3, "sum_small").reshape(w3)
    n_mod = 14 * dm
    dmod_all = g3.reshape(N_DEV, w3)[:, :n_mod]
    o = n_mod
    conv_tot = [tot[o + l * 8 * ff: o + (l + 1) * 8 * ff].reshape(4, 2 * ff) for l in range(2)]
    o += 16 * ff
    g_a_norm, g_k_norm, g_q_norm = (tot[o + i * HEAD_DIM: o + (i + 1) * HEAD_DIM] for i in range(3))
    o += 3 * HEAD_DIM
    g_kv_b_f = tot[o:o + HEADS]
    dlb = tot[o + LANES:o + LANES + dm]
    loss = tot[o + LANES + dm]

    ct = _pad_cols(c_act16[:8].T, LANES)
    dmod_pad = jnp.pad(dmod_all, ((0, LANES - N_DEV), (0, 0)))
    cols_ada = jnp.stack([lax.dynamic_slice_in_dim(dmod_pad, l * 6 * dm + chip * 1536, 1536, axis=1) for l in range(2)])
    cols_kv = lax.dynamic_slice_in_dim(dmod_pad, 12 * dm + chip * 512, 512, axis=1)[None]
    g_ada_w = _outer_grad(ct, cols_ada, "grad_ada_w")
    g_kv_ada_w = _outer_grad(ct, cols_kv, "grad_kv_ada_w")[0]

    my_lb = lax.dynamic_slice_in_dim(lb[0], chip * 256, 256)
    l0 = lax.dynamic_slice_in_dim(dlb, chip * 256, 256) * my_lb * (1.0 - my_lb)
    grads = {
        "ada_w": g_ada_w, "ada_b": jnp.stack([tot[:6 * dm], tot[6 * dm:12 * dm]]),
        "a_lb_logits": jnp.stack([l0, -l0]), "a_norm_g": g_a_norm[None],
        "a_w_out": rs["a_w_out"][None], "kv_ada_w": g_kv_ada_w, "kv_ada_b": tot[12 * dm:14 * dm],
        "kv_w": rs["kv_w"][:, :KV_SHARD], "kv_b_f": g_kv_b_f, "k_norm_g": g_k_norm,
        "b_w_q": rs["b_w_q"][None], "q_norm_g": g_q_norm[None], "b_w_out": rs["b_w_out"][None],
        "ffn_w_up": jnp.stack([rs["up0"], rs["up1"]]),
        "ffn_conv_w": jnp.stack([lax.dynamic_slice_in_dim(ct_l[:CONV_W], chip * FFN_COLS, FFN_COLS, axis=1) for ct_l in conv_tot]),
        "ffn_conv_b": jnp.stack([ct_l[CONV_W] for ct_l in conv_tot]),
        "ffn_w_down": jnp.stack([rs["down0"], rs["down1"]]),
    }
    weights = dict(ada_w=ada_w, ada_b=ada_b, a_w_in=a_w_in, a_lb_logits=a_lb_logits, a_norm_g=a_norm_g, a_w_out=a_w_out,
                   kv_ada_w=kv_ada_w, kv_ada_b=kv_ada_b, kv_w=kv_w, kv_b_f=kv_b_f, k_norm_g=k_norm_g, b_w_q=b_w_q,
                   q_norm_g=q_norm_g, b_w_out=b_w_out, ffn_w_up=ffn_w_up, ffn_conv_w=ffn_conv_w, ffn_conv_b=ffn_conv_b,
                   ffn_w_down=ffn_w_down)
    m_in = dict(ada_w=m_ada_w, ada_b=m_ada_b, a_w_in=m_a_w_in, a_lb_logits=m_a_lb_logits, a_norm_g=m_a_norm_g,
                a_w_out=m_a_w_out, kv_ada_w=m_kv_ada_w, kv_ada_b=m_kv_ada_b, kv_w=m_kv_w, kv_b_f=m_kv_b_f,
                k_norm_g=m_k_norm_g, b_w_q=m_b_w_q, q_norm_g=m_q_norm_g, b_w_out=m_b_w_out, ffn_w_up=m_ffn_w_up,
                ffn_conv_w=m_ffn_conv_w, ffn_conv_b=m_ffn_conv_b, ffn_w_down=m_ffn_w_down)
    v_in = dict(ada_w=v_ada_w, ada_b=v_ada_b, a_w_in=v_a_w_in, a_lb_logits=v_a_lb_logits, a_norm_g=v_a_norm_g,
                a_w_out=v_a_w_out, kv_ada_w=v_kv_ada_w, kv_ada_b=v_kv_ada_b, kv_w=v_kv_w, kv_b_f=v_kv_b_f,
                k_norm_g=v_k_norm_g, b_w_q=v_b_w_q, q_norm_g=v_q_norm_g, b_w_out=v_b_w_out, ffn_w_up=v_ffn_w_up,
                ffn_conv_w=v_ffn_conv_w, ffn_conv_b=v_ffn_conv_b, ffn_w_down=v_ffn_w_down)

    names = list(weights)
    step = lambda n: _adamw(weights[n], grads[n], m_in[n], v_in[n], "adamw_" + n)
    grads = {n: g.reshape(weights[n].shape) for n, g in grads.items()}
    upd = {n: step(n) for n in names if n != "a_w_in"}
    last = sum_group(3, [u[0] for u in upd.values()])
    grads["a_w_in"] = swap_group(3, last, last)["a_w_in"][None]
    upd["a_w_in"] = step("a_w_in")
    return (loss, grad_x[None], *[grads[n] for n in names], *[upd[n][0] for n in names],
            *[upd[n][1] for n in names], *[upd[n][2] for n in names])
```

```python
import jax
import jax.numpy as jnp
from jax import lax
from jax.experimental import pallas as pl
from jax.experimental.pallas import tpu as pltpu
from jax.experimental.pallas import tpu_sc as plsc

F32 = jnp.float32
BF16 = jnp.bfloat16

D_MODEL = 1024
HEADS = 8
HEAD_DIM = 128
A_CHUNK = 64
D_FF = 2816
CONV_W = 3
EPS = 1e-6
NEG_INF = -1e30
N_CHIPS = 4
N_DEV = 8

ADAM_LR = 0.001
ADAM_B1 = 0.9
ADAM_B2 = 0.999
ADAM_EPS = 1e-08
ADAM_WD = 0.01
ADAM_STEP = 10

SUBLANES = 8
BF16_ROWS = 16
LANES = 128
HALO = BF16_ROWS
ROW_TILE = 512
TOKEN_TILE_TN = 2048
FFN_COLS = 1408
FFN_ROWS = 256
HGRN_ROWS = 512
ATT_TILE = 512
ATT_SPLIT = 2
ATT_FWD_HEADS = 8
ATT_BWD_HEADS = 8
MESH = pl.DeviceIdType.MESH


def _sig(x):
    return jax.nn.sigmoid(x)


def _dot(a, b):
    return jnp.dot(a, b, preferred_element_type=F32)


def _dot_nt(a, b):
    return lax.dot_general(a, b, (((1,), (1,)), ((), ())), preferred_element_type=F32)


def _dot_tn(a, b):
    return lax.dot_general(a, b, (((0,), (0,)), ((), ())), preferred_element_type=F32)


def _split2(x):
    hi = x.astype(BF16)
    lo = (x - hi.astype(F32)).astype(BF16)
    return hi, lo


def _dot_f32(a, b):
    ah, al = _split2(a)
    bh, bl = _split2(b)
    return _dot(ah, bh) + _dot(ah, bl) + _dot(al, bh)


def _tri_dot(tri, x):
    hi = x.astype(BF16)
    r = x - hi.astype(F32)
    mid = r.astype(BF16)
    lo = (r - mid.astype(F32)).astype(BF16)
    return _dot(tri, hi) + _dot(tri, mid) + _dot(tri, lo)


def _tri(n, upper=False):
    r = lax.broadcasted_iota(jnp.int32, (n, n), 0)
    c = lax.broadcasted_iota(jnp.int32, (n, n), 1)
    keep = (c >= r) if upper else (c <= r)
    return jnp.where(keep, 1.0, 0.0).astype(BF16)


def _colsum8(v):
    rows, n = v.shape
    return v.reshape(rows // SUBLANES, SUBLANES, n).sum(axis=0)


def _full(shape):
    nd = len(shape)
    return pl.BlockSpec(shape, lambda *_: (0,) * nd)


def _tile(n, want):
    t = min(n, want)
    assert n % t == 0, (n, t)
    return t


def _mm_tn(a, d, p_n, name):
    m_rows, k = a.shape
    g_n, _, w_cols = d.shape
    per = p_n // g_n
    n = w_cols // per
    tm = _tile(m_rows, TOKEN_TILE_TN if k <= D_MODEL else ROW_TILE)
    steps = m_rows // tm

    def body(a_ref, d_ref, o_ref, acc):
        m = pl.program_id(1)

        @pl.when(m == 0)
        def _():
            acc[...] = jnp.zeros_like(acc)

        acc[...] += _dot_tn(a_ref[...], d_ref[...])

        @pl.when(m == steps - 1)
        def _():
            o_ref[...] = acc[...].astype(BF16)

    return pl.pallas_call(
        body, name=name, grid=(p_n, steps),
        in_specs=[pl.BlockSpec((tm, k), lambda p, m: (m, 0)),
                  pl.BlockSpec((None, tm, n), lambda p, m: (p // per, m, p % per))],
        out_specs=pl.BlockSpec((None, k, n), lambda p, m: (p, 0, 0)),
        out_shape=jax.ShapeDtypeStruct((p_n, k, n), BF16),
        scratch_shapes=[pltpu.VMEM((k, n), F32)],
    )(a, d)


def _premix_proj(x, shift, scale, w, name):
    s, dm = x.shape
    p_n, _, n = w.shape
    tm = _tile(s, ROW_TILE)

    def body(x_ref, sh_ref, sc_ref, w_ref, h_ref, o_ref):
        xv = x_ref[...]
        inv = lax.rsqrt(jnp.mean(xv * xv, axis=-1, keepdims=True) + EPS)
        h = (xv * inv * (1.0 + sc_ref[...]) + sh_ref[...]).astype(BF16)
        h_ref[...] = h
        for p in range(p_n):
            o_ref[:, p * n:(p + 1) * n] = _dot(h, w_ref[p])

    row = pl.BlockSpec((tm, dm), lambda i: (i, 0))
    vec = _full((1, dm))
    return pl.pallas_call(
        body, name=name, grid=(s // tm,), in_specs=[row, vec, vec, _full(w.shape)],
        out_specs=[row, pl.BlockSpec((tm, p_n * n), lambda i: (i, 0))],
        out_shape=[jax.ShapeDtypeStruct((s, dm), BF16), jax.ShapeDtypeStruct((s, p_n * n), F32)],
    )(x, shift, scale, w)


def _premix_bwd(x, terms, dres, name, branch=None):
    s, dm = x.shape
    tm = _tile(s, ROW_TILE)
    pairs = [pr for _, prs in terms for pr in prs]
    n_in = 2 + len(terms) + 2 * len(pairs) + (2 if branch else 0)

    def body(*refs):
        x_ref, dres_ref = refs[:2]
        sc_refs = refs[2:2 + len(terms)]
        mm_refs = refs[2 + len(terms):2 + len(terms) + 2 * len(pairs)]
        outs = refs[n_in:]

        @pl.when(pl.program_id(0) == 0)
        def _():
            for o in outs[1:1 + 2 * len(terms)]:
                o[...] = jnp.zeros_like(o)
            if branch:
                outs[-1][...] = jnp.zeros_like(outs[-1])

        xv = x_ref[...]
        inv = lax.rsqrt(jnp.mean(xv * xv, axis=-1, keepdims=True) + EPS)
        r = xv * inv
        dx = dres_ref[...]
        k = 0
        for t, (_, prs) in enumerate(terms):
            dh = None
            for d, w in prs:
                d_ref, w_ref = mm_refs[2 * k], mm_refs[2 * k + 1]
                k += 1
                p_n, _, n = w.shape
                per = p_n // d.shape[0]
                for p in range(p_n):
                    part = _dot_nt(d_ref[p // per, :, (p % per) * n:(p % per + 1) * n], w_ref[p])
                    dh = part if dh is None else dh + part
            dr = dh * (1.0 + sc_refs[t][...])
            dx = dx + inv * (dr - r * jnp.mean(dr * r, axis=-1, keepdims=True))
            outs[1 + 2 * t][...] += _colsum8(dh)
            outs[2 + 2 * t][...] += _colsum8(dh * r)
        outs[0][...] = dx
        if branch:
            y_ref, g_ref = refs[n_in - 2:n_in]
            outs[-2][0] = (dx * g_ref[...]).astype(BF16)
            outs[-1][...] += _colsum8(dx * y_ref[...])

    row = pl.BlockSpec((tm, dm), lambda i: (i, 0))
    vec, acc = _full((1, dm)), _full((SUBLANES, dm))
    ins, specs = [x, dres] + [sc for sc, _ in terms], [row, row] + [vec] * len(terms)
    for d, w in pairs:
        ins += [d, w]
        specs += [pl.BlockSpec((d.shape[0], tm, d.shape[2]), lambda i: (0, i, 0)), _full(w.shape)]
    out_shape = [jax.ShapeDtypeStruct((s, dm), F32)] + [jax.ShapeDtypeStruct((SUBLANES, dm), F32)] * (2 * len(terms))
    out_specs = [row] + [acc] * (2 * len(terms))
    if branch:
        ins += list(branch)
        specs += [row, vec]
        out_shape += [jax.ShapeDtypeStruct((1, s, dm), BF16), jax.ShapeDtypeStruct((SUBLANES, dm), F32)]
        out_specs += [pl.BlockSpec((1, tm, dm), lambda i: (0, i, 0)), acc]
    outs = pl.pallas_call(body, name=name, grid=(s // tm,), in_specs=specs, out_specs=out_specs,
                          out_shape=out_shape)(*ins)
    partials = [(outs[1 + 2 * t], outs[2 + 2 * t]) for t in range(len(terms))]
    return (outs[0], partials) + ((outs[-2], outs[-1]) if branch else ())


def _conv_taps(e, w, b):
    return w[2:3] * e + w[1:2] * pltpu.roll(e, 1, 0) + w[0:1] * pltpu.roll(e, 2, 0) + b


def _ffn_specs(s, tm, cb):
    hb = tm // HALO
    last = s // HALO - 1
    main = pl.BlockSpec((2, tm, cb), lambda j, i: (0, i, j))
    prev = pl.BlockSpec((2, HALO, cb), lambda j, i: (0, jnp.maximum(i * hb - 1, 0), j))
    nxt = pl.BlockSpec((2, HALO, cb), lambda j, i: (0, jnp.minimum((i + 1) * hb, last), j))
    wspec = pl.BlockSpec((2, CONV_W, cb), lambda j, i: (0, 0, j))
    bspec = pl.BlockSpec((2, 1, cb), lambda j, i: (0, 0, j))
    return main, prev, nxt, wspec, bspec


def _convglu_bwd(u, c, dffn, w_down, w, name):
    _, s, f = u.shape
    dm = dffn.shape[2]
    tm = _tile(s, 256)
    cb = _tile(f, FFN_COLS)
    steps = s // tm
    n_ext = tm + HALO
    main, _, nxt, wspec, _ = _ffn_specs(s, tm, cb)
    hb = tm // HALO
    last = s // HALO - 1
    d_main = pl.BlockSpec((None, tm, dm), lambda j, i: (0, i, 0))
    d_next = pl.BlockSpec((None, HALO, dm), lambda j, i: (0, jnp.minimum((i + 1) * hb, last), 0))
    wd_spec = pl.BlockSpec((None, cb, dm), lambda j, i: (0, j, 0))

    def body(u_ref, c_ref, cn_ref, d_ref, dn_ref, wd_ref, w_ref, du_ref, acc_ref):
        i = pl.program_id(1)
        notlast = jnp.where(i < steps - 1, 1.0, 0.0)

        @pl.when(i == 0)
        def _():
            acc_ref[...] = jnp.zeros_like(acc_ref)

        gate, val = (jnp.concatenate([c_ref[g].astype(F32), cn_ref[g].astype(F32)], axis=0) for g in range(2))
        wd = wd_ref[...]
        da = jnp.concatenate([_dot_nt(d_ref[...], wd).astype(BF16).astype(F32),
                              _dot_nt(dn_ref[...], wd).astype(BF16).astype(F32) * notlast], axis=0)
        sg = _sig(gate)
        d_val = da * gate * sg
        d_gate = da * val * (sg * (1.0 + gate * (1.0 - sg)))

        def finish(g, d):
            wv = w_ref[g]
            d1, d2 = pltpu.roll(d, n_ext - 1, 0), pltpu.roll(d, n_ext - 2, 0)
            du_ref[g] = (wv[2:3] * d + wv[1:2] * d1 + wv[0:1] * d2)[0:tm].astype(BF16)
            uv = u_ref[g].astype(F32)
            acc_ref[g, 2] += _colsum8(d[0:tm] * uv)
            acc_ref[g, 1] += _colsum8(d1[0:tm] * uv)
            acc_ref[g, 0] += _colsum8(d2[0:tm] * uv)
            acc_ref[g, 3] += _colsum8(d[0:tm])

        finish(0, d_gate)
        finish(1, d_val)

    return pl.pallas_call(
        body, name=name, grid=(f // cb, steps),
        in_specs=[main, main, nxt, d_main, d_next, wd_spec, wspec],
        out_specs=[main, pl.BlockSpec((2, 4, SUBLANES, cb), lambda j, i: (0, 0, 0, j))],
        out_shape=[jax.ShapeDtypeStruct((2, s, f), BF16), jax.ShapeDtypeStruct((2, 4, SUBLANES, f), F32)],
    )(u, c, c, dffn, dffn, w_down, w)


def _hgrn_gates(q_raw, f_raw, lb, tri):
    sf = _sig(f_raw)
    fg = lb + (1.0 - lb) * sf
    b = _tri_dot(tri, jnp.log(fg))
    return q_raw * _sig(q_raw), 1.0 - fg, b, fg, sf


def _hgrn_fwd(proj, lb, norm_g, name):
    s = proj.shape[0]
    tb = _tile(s, HGRN_ROWS)
    n_c = tb // A_CHUNK
    half = A_CHUNK // 2

    def body(q_ref, f_ref, v_ref, g_ref, lb_ref, ng_ref, o_ref, yp_ref, st_ref, state):
        @pl.when(pl.program_id(0) == 0)
        def _():
            state[...] = jnp.zeros_like(state)

        tri = _tri(A_CHUNK)
        causal = lax.broadcasted_iota(jnp.int32, (A_CHUNK, A_CHUNK), 1) <= lax.broadcasted_iota(
            jnp.int32, (A_CHUNK, A_CHUNK), 0)

        def chunk(ci, carry):
            rows = pl.ds(ci * A_CHUNK, A_CHUNK)
            heads = [slice(h * HEAD_DIM, (h + 1) * HEAD_DIM) for h in range(HEADS)]
            qs, k, b, _, _ = _hgrn_gates(q_ref[rows, :], f_ref[rows, :], lb_ref[...], tri)
            b_mid, b_last = b[half:half + 1], b[A_CHUNK - 1:A_CHUNK]
            q_i = (qs * jnp.exp(b - b_mid)).astype(BF16)
            k_i = (k * jnp.exp(b_mid - b)).astype(BF16)
            q_e = (qs * jnp.exp(b)).astype(BF16)
            k_s = (k * jnp.exp(b_last - b)).astype(BF16)
            decay = jnp.exp(b_last)
            vb = v_ref[rows, :].astype(BF16)
            scores = [jnp.where(causal, _dot_nt(q_i[:, cs], k_i[:, cs]), 0.0).astype(BF16) for cs in heads]
            st = [state[h] for h in range(HEADS)]
            outs = [_dot(scores[h], vb[:, cs]) + _dot_nt(q_e[:, cs], st[h].astype(BF16)) for h, cs in enumerate(heads)]
            for h, cs in enumerate(heads):
                st_ref[ci, h] = st[h]
                state[h] = st[h] * decay[:, cs] + _dot_tn(vb[:, cs], k_s[:, cs])
            o = jnp.concatenate(outs, axis=1)
            o_ref[rows, :] = o
            sq = o * o
            inv = jnp.concatenate([jnp.broadcast_to(lax.rsqrt(jnp.mean(sq[:, cs], axis=-1, keepdims=True) + EPS),
                                                    (A_CHUNK, HEAD_DIM)) for cs in heads], axis=1)
            g_raw = g_ref[rows, :]
            yp_ref[rows, :] = (o * inv * ng_ref[...] * (g_raw * _sig(g_raw))).astype(BF16)
            return carry

        for step in range(n_c):
            chunk(step, 0)

    col = lambda j: pl.BlockSpec((tb, D_MODEL), lambda i: (i, j))
    vec = _full((1, D_MODEL))
    return pl.pallas_call(
        body, name=name, grid=(s // tb,), in_specs=[col(0), col(1), col(2), col(3), vec, vec],
        out_specs=[col(0), col(0), pl.BlockSpec((n_c, HEADS, HEAD_DIM, HEAD_DIM), lambda i: (i, 0, 0, 0))],
        out_shape=[jax.ShapeDtypeStruct((s, D_MODEL), F32), jax.ShapeDtypeStruct((s, D_MODEL), BF16),
                   jax.ShapeDtypeStruct((s // A_CHUNK, HEADS, HEAD_DIM, HEAD_DIM), F32)],
        scratch_shapes=[pltpu.VMEM((HEADS, HEAD_DIM, HEAD_DIM), F32)],
    )(proj, proj, proj, proj, lb, norm_g)


def _hgrn_bwd(proj, lb, norm_g, o, states, dout, w_out, name):
    s = proj.shape[0]
    tb = _tile(s, HGRN_ROWS)
    n_c = tb // A_CHUNK
    n_b = s // tb
    half = A_CHUNK // 2

    def body(q_ref, f_ref, v_ref, g_ref, lb_ref, ng_ref, o_ref, st_ref, dout_ref, w_ref, dp_ref, dlb_ref, dng_ref,
             dstate, dyp_ref):
        @pl.when(pl.program_id(0) == 0)
        def _():
            dstate[...] = jnp.zeros_like(dstate)
            dlb_ref[...] = jnp.zeros_like(dlb_ref)
            dng_ref[...] = jnp.zeros_like(dng_ref)

        dyp_ref[...] = _dot_nt(dout_ref[0], w_ref[0])

        tri = _tri(A_CHUNK)
        tri_up = _tri(A_CHUNK, upper=True)
        row_id = lax.broadcasted_iota(jnp.int32, (A_CHUNK, D_MODEL), 0)
        causal = lax.broadcasted_iota(jnp.int32, (A_CHUNK, A_CHUNK), 1) <= lax.broadcasted_iota(
            jnp.int32, (A_CHUNK, A_CHUNK), 0)

        def chunk(cj, carry):
            ci = n_c - 1 - cj
            rows = pl.ds(ci * A_CHUNK, A_CHUNK)
            heads = [slice(h * HEAD_DIM, (h + 1) * HEAD_DIM) for h in range(HEADS)]
            cat = lambda parts: jnp.concatenate(parts, axis=1)
            per_head_mean = lambda a: cat([jnp.broadcast_to(jnp.mean(a[:, cs], axis=-1, keepdims=True),
                                                            (A_CHUNK, HEAD_DIM)) for cs in heads])
            q_raw, lbv = q_ref[rows, :], lb_ref[...]
            qs, k, b, fg, sf = _hgrn_gates(q_raw, f_ref[rows, :], lbv, tri)
            b_mid, b_last = b[half:half + 1], b[A_CHUNK - 1:A_CHUNK]
            e_qi, e_ki, e_q, e_ks = jnp.exp(b - b_mid), jnp.exp(b_mid - b), jnp.exp(b), jnp.exp(b_last - b)
            decay = jnp.exp(b_last)
            q_i, k_i, q_e, k_s = qs * e_qi, k * e_ki, qs * e_q, k * e_ks
            qib, kib, qeb, ksb = q_i.astype(BF16), k_i.astype(BF16), q_e.astype(BF16), k_s.astype(BF16)
            vb = v_ref[rows, :].astype(BF16)
            ov, g_raw, dy, ng = o_ref[rows, :], g_ref[rows, :], dyp_ref[rows, :], ng_ref[...]
            inv = lax.rsqrt(per_head_mean(ov * ov) + EPS)
            nrm = ov * inv
            sg = _sig(g_raw)
            gs = g_raw * sg
            dn = dy * ng * gs
            dng_ref[0:1, :] += jnp.sum(dy * nrm * gs, axis=0, keepdims=True)
            dg_raw = dy * nrm * ng * (sg * (1.0 + g_raw * (1.0 - sg)))
            do = (inv * (dn - nrm * per_head_mean(dn * nrm))).astype(BF16)
            st_prev = [st_ref[ci, h] for h in range(HEADS)]
            dst = [dstate[h] for h in range(HEADS)]
            dstb = [d.astype(BF16) for d in dst]
            scores = [jnp.where(causal, _dot_nt(qib[:, cs], kib[:, cs]), 0.0).astype(BF16) for cs in heads]
            d_scores = [jnp.where(causal, _dot_nt(do[:, cs], vb[:, cs]), 0.0).astype(BF16) for cs in heads]
            dv = cat([_dot_tn(scores[h], do[:, cs]) + _dot_nt(ksb[:, cs], dstb[h]) for h, cs in enumerate(heads)])
            dq_i = cat([_dot(d_scores[h], kib[:, cs]) for h, cs in enumerate(heads)])
            dk_i = cat([_dot_tn(d_scores[h], qib[:, cs]) for h, cs in enumerate(heads)])
            dq_e = cat([_dot(do[:, cs], st_prev[h].astype(BF16)) for h, cs in enumerate(heads)])
            dk_s = cat([_dot(vb[:, cs], dstb[h]) for h, cs in enumerate(heads)])
            d_decay = cat([jnp.sum(st_prev[h] * dst[h], axis=0, keepdims=True) for h in range(HEADS)])
            for h, cs in enumerate(heads):
                dstate[h] = dst[h] * decay[:, cs] + _dot_tn(do[:, cs], qeb[:, cs])
            dq = dq_i * e_qi + dq_e * e_q
            dk = dk_i * e_ki + dk_s * e_ks
            t_qi, t_ki, t_ks = dq_i * q_i, dk_i * k_i, dk_s * k_s
            db = t_qi - t_ki + dq_e * q_e - t_ks
            db_mid = jnp.sum(t_ki - t_qi, axis=0, keepdims=True)
            db_last = jnp.sum(t_ks, axis=0, keepdims=True) + d_decay * decay
            db = db + jnp.where(row_id == half, db_mid, 0.0) + jnp.where(row_id == A_CHUNK - 1, db_last, 0.0)
            dfg = _tri_dot(tri_up, db) / fg - dk
            dlb_ref[0:1, :] += jnp.sum(dfg * (1.0 - sf), axis=0, keepdims=True)
            sq = _sig(q_raw)
            dp_ref[0, rows, :] = (dq * (sq * (1.0 + q_raw * (1.0 - sq)))).astype(BF16)
            dp_ref[1, rows, :] = (dfg * (1.0 - lbv) * sf * (1.0 - sf)).astype(BF16)
            dp_ref[2, rows, :] = dv.astype(BF16)
            dp_ref[3, rows, :] = dg_raw.astype(BF16)
            return carry

        for step in range(n_c):
            chunk(step, 0)

    col = lambda j: pl.BlockSpec((tb, D_MODEL), lambda i: (n_b - 1 - i, j))
    vec = _full((1, D_MODEL))
    acc = _full((SUBLANES, D_MODEL))
    return pl.pallas_call(
        body, name=name, grid=(n_b,),
        in_specs=[col(0), col(1), col(2), col(3), vec, vec, col(0),
                  pl.BlockSpec((n_c, HEADS, HEAD_DIM, HEAD_DIM), lambda i: (n_b - 1 - i, 0, 0, 0)),
                  pl.BlockSpec((1, tb, D_MODEL), lambda i: (0, n_b - 1 - i, 0)), _full(w_out.shape)],
        out_specs=[pl.BlockSpec((4, tb, D_MODEL), lambda i: (0, n_b - 1 - i, 0)), acc, acc],
        out_shape=[jax.ShapeDtypeStruct((4, s, D_MODEL), BF16), jax.ShapeDtypeStruct((SUBLANES, D_MODEL), F32),
                   jax.ShapeDtypeStruct((SUBLANES, D_MODEL), F32)],
        scratch_shapes=[pltpu.VMEM((HEADS, HEAD_DIM, HEAD_DIM), F32), pltpu.VMEM((tb, D_MODEL), F32)],
    )(proj, proj, proj, proj, lb, norm_g, o, states, dout, w_out)


def _head_rms(raw_ref, g_ref, mult, y_ref):
    for h in range(HEADS):
        cs = slice(h * HEAD_DIM, (h + 1) * HEAD_DIM)
        xv = raw_ref[:, cs]
        inv = lax.rsqrt(jnp.mean(xv * xv, axis=-1, keepdims=True) + EPS)
        y_ref[:, cs] = (xv * inv * g_ref[:, cs] * mult).astype(BF16)


def _proj_headnorm(a, w, g, mult, name):
    s, k = a.shape
    p_n, _, n = w.shape
    tm = _tile(s, ROW_TILE)

    def body(a_ref, w_ref, g_ref, raw_ref, y_ref):
        av = a_ref[...]
        for p in range(p_n):
            raw_ref[:, p * n:(p + 1) * n] = _dot(av, w_ref[p])
        _head_rms(raw_ref, g_ref, mult, y_ref)

    row = lambda wid: pl.BlockSpec((tm, wid), lambda i: (i, 0))
    return pl.pallas_call(
        body, name=name, grid=(s // tm,), in_specs=[row(k), _full(w.shape), _full((1, D_MODEL))],
        out_specs=[row(p_n * n), row(D_MODEL)],
        out_shape=[jax.ShapeDtypeStruct((s, p_n * n), F32), jax.ShapeDtypeStruct((s, D_MODEL), BF16)],
    )(a, w, g)


def _kv_proj(hk, w_k, w_v, w_f, g, name):
    s, k = hk.shape
    tm = _tile(s, ROW_TILE)

    def body(h_ref, wk_ref, wv_ref, wf_ref, g_ref, kr_ref, k_ref, v_ref, f_ref):
        hv = h_ref[...]
        kr_ref[...] = _dot(hv, wk_ref[0])
        v_ref[...] = _dot(hv, wv_ref[0]).astype(BF16)
        f_ref[...] = _dot(hv, wf_ref[0])
        _head_rms(kr_ref, g_ref, 1.0, k_ref)

    row = lambda wid: pl.BlockSpec((tm, wid), lambda i: (i, 0))
    return pl.pallas_call(
        body, name=name, grid=(s // tm,),
        in_specs=[row(k), _full(w_k.shape), _full(w_v.shape), _full(w_f.shape), _full((1, D_MODEL))],
        out_specs=[row(D_MODEL), row(D_MODEL), row(D_MODEL), row(LANES)],
        out_shape=[jax.ShapeDtypeStruct((s, D_MODEL), F32), jax.ShapeDtypeStruct((s, D_MODEL), BF16),
                   jax.ShapeDtypeStruct((s, D_MODEL), BF16), jax.ShapeDtypeStruct((s, LANES), F32)],
    )(hk, w_k, w_v, w_f, g)


def _headnorm_bwd(x, g, mult, dy, name, col0=0, extra=None):
    s = x.shape[0]
    tm = _tile(s, ROW_TILE)
    groups = 2 if extra is not None else 1
    head_major = dy.ndim == 3

    def body(*refs):
        x_ref, g_ref, dy_ref = refs[:3]
        dx_ref, dg_ref = refs[-2:]

        @pl.when(pl.program_id(0) == 0)
        def _():
            dg_ref[...] = jnp.zeros_like(dg_ref)

        for h in range(HEADS):
            cs = slice(h * HEAD_DIM, (h + 1) * HEAD_DIM)
            xv, gv = x_ref[:, cs], g_ref[:, cs]
            dyv = dy_ref[h, :, 0:HEAD_DIM] if head_major else dy_ref[:, cs]
            inv = lax.rsqrt(jnp.mean(xv * xv, axis=-1, keepdims=True) + EPS)
            nrm = xv * inv
            dn = dyv * gv * mult
            dg_ref[:, cs] += _colsum8(dyv * nrm * mult)
            dx_ref[0, :, cs] = (inv * (dn - nrm * jnp.mean(dn * nrm, axis=-1, keepdims=True))).astype(BF16)
        if extra is not None:
            dx_ref[1] = refs[3][...]

    row = pl.BlockSpec((tm, D_MODEL), lambda i: (i, 0))
    dy_spec = pl.BlockSpec((HEADS, tm, dy.shape[-1]), lambda i: (0, i, 0)) if head_major else row
    ins = [x, g, dy] + ([extra] if extra is not None else [])
    specs = ([pl.BlockSpec((tm, D_MODEL), lambda i: (i, col0)), _full((1, D_MODEL)), dy_spec]
             + ([row] if extra is not None else []))
    return pl.pallas_call(
        body, name=name, grid=(s // tm,), in_specs=specs,
        out_specs=[pl.BlockSpec((groups, tm, D_MODEL), lambda i: (0, i, 0)), _full((SUBLANES, D_MODEL))],
        out_shape=[jax.ShapeDtypeStruct((groups, s, D_MODEL), BF16), jax.ShapeDtypeStruct((SUBLANES, D_MODEL), F32)],
    )(*ins)


def _log_sigmoid(z):
    return jnp.minimum(z, 0.0) - jnp.log(1.0 + jnp.exp(-jnp.abs(z)))


Q_CUM, Q_ONE, Q_LSE = 0, 3, 6
LOG2E = 1.4426950408889634


def _pieces(v):
    hi = v.astype(BF16).astype(F32)
    mid = (v - hi).astype(BF16).astype(F32)
    lo = ((v - hi) - mid).astype(BF16).astype(F32)
    return hi, mid, lo


def _side(lane, at, v):
    hi, mid, lo = _pieces(v)
    return jnp.where(lane == at, hi, jnp.where(lane == at + 1, mid, jnp.where(lane == at + 2, lo, 0.0)))


def _fcum_fwd(f, bias, name):
    s = f.shape[0]
    tm = _tile(s, ROW_TILE)

    def body(f_ref, b_ref, qa_ref, ka_ref, carry):
        @pl.when(pl.program_id(0) == 0)
        def _():
            carry[...] = jnp.zeros_like(carry)

        cum = _tri_dot(_tri(tm), _log_sigmoid(f_ref[...] + b_ref[...])) + carry[...]
        carry[...] = cum[tm - 1:tm]
        lane = lax.broadcasted_iota(jnp.int32, (tm, LANES), 1)
        ones_q = jnp.where((lane >= Q_ONE) & (lane < Q_LSE), 1.0, 0.0)
        ones_k = jnp.where((lane < Q_ONE) | ((lane >= Q_LSE) & (lane < Q_LSE + 3)), 1.0, 0.0)
        for h in range(HEADS):
            c2 = cum[:, h:h + 1] * LOG2E
            qa_ref[h] = (_side(lane, Q_CUM, c2) + ones_q).astype(BF16)
            ka_ref[h] = (_side(lane, Q_ONE, -c2) + ones_k).astype(BF16)

    side = pl.BlockSpec((HEADS, tm, LANES), lambda i: (0, i, 0))
    return pl.pallas_call(
        body, name=name, grid=(s // tm,),
        in_specs=[pl.BlockSpec((tm, LANES), lambda i: (i, 0)), _full((1, LANES))],
        out_specs=[side, side],
        out_shape=[jax.ShapeDtypeStruct((HEADS, s, LANES), BF16)] * 2,
        scratch_shapes=[pltpu.VMEM((1, LANES), F32)],
    )(f, bias)


def _fcum_bwd(f, bias, dka, dcq, name):
    s = f.shape[0]
    tm = _tile(s, ROW_TILE)
    n_b = s // tm

    def body(f_ref, b_ref, dka_ref, dcq_ref, dz_ref, db_ref, carry):
        @pl.when(pl.program_id(0) == 0)
        def _():
            carry[...] = jnp.zeros_like(carry)
            db_ref[...] = jnp.zeros_like(db_ref)

        lane = lax.broadcasted_iota(jnp.int32, (tm, LANES), 1)
        rows = jnp.concatenate([dcq_ref[h] for h in range(HEADS)] + [jnp.zeros((LANES - HEADS, tm), F32)], axis=0)
        dcum = rows.T
        for h in range(HEADS):
            dcum = dcum - jnp.where(lane == h, dka_ref[h, :, Q_ONE:Q_ONE + 1], 0.0)
        dlf = _tri_dot(_tri(tm, upper=True), dcum) + carry[...]
        carry[...] = dlf[0:1]
        dz = dlf * _sig(-(f_ref[...] + b_ref[...]))
        dz_ref[0] = dz.astype(BF16)
        db_ref[...] += _colsum8(dz)

    return pl.pallas_call(
        body, name=name, grid=(n_b,),
        in_specs=[pl.BlockSpec((tm, LANES), lambda i: (n_b - 1 - i, 0)), _full((1, LANES)),
                  pl.BlockSpec((HEADS, tm, LANES), lambda i: (0, n_b - 1 - i, 0)),
                  pl.BlockSpec((HEADS, 1, tm), lambda i: (0, 0, n_b - 1 - i))],
        out_specs=[pl.BlockSpec((1, tm, LANES), lambda i: (0, n_b - 1 - i, 0)), _full((SUBLANES, LANES))],
        out_shape=[jax.ShapeDtypeStruct((1, s, LANES), BF16), jax.ShapeDtypeStruct((SUBLANES, LANES), F32)],
        scratch_shapes=[pltpu.VMEM((1, LANES), F32)],
    )(f, bias, dka, dcq)


def _causal_pairs(n_t, key_major):
    if key_major:
        pairs = [(qi, ki) for ki in range(n_t) for qi in range(ki, n_t)]
    else:
        pairs = [(qi, ki) for qi in range(n_t) for ki in range(qi + 1)]
    return (jnp.array([p[0] for p in pairs], jnp.int32), jnp.array([p[1] for p in pairs], jnp.int32))


def _lane_const(t, lo, hi, value):
    lane = lax.broadcasted_iota(jnp.int32, (t, LANES), 1)
    return jnp.where((lane >= lo) & (lane < hi), value, 0.0).astype(BF16)


def _att_specs(t, nh):
    qmain = pl.BlockSpec((t, nh * HEAD_DIM), lambda h, p, qt, kt: (qt[p], h))
    kmain = pl.BlockSpec((t, nh * HEAD_DIM), lambda h, p, qt, kt: (kt[p], h))
    qside = pl.BlockSpec((nh, t, LANES), lambda h, p, qt, kt: (h, qt[p], 0))
    kside = pl.BlockSpec((nh, t, LANES), lambda h, p, qt, kt: (h, kt[p], 0))
    return qmain, kmain, qside, kside


def _fox_fwd(q, qa, k, ka, v, qo, name):
    s = q.shape[0]
    t = _tile(s, ATT_TILE)
    sub = t // ATT_SPLIT
    nh = ATT_FWD_HEADS
    qt, kt = _causal_pairs(s // t, key_major=False)

    def body(qt_ref, kt_ref, q_ref, qa_ref, k_ref, ka_ref, v_ref, og_ref, o_ref, y_ref, qab_ref, m_s, l_s, acc_s):
        pid = pl.program_id(1)
        qi, ki = qt_ref[pid], kt_ref[pid]

        @pl.when(ki == 0)
        def _():
            m_s[...] = jnp.full_like(m_s, NEG_INF)
            l_s[...] = jnp.zeros_like(l_s)
            acc_s[...] = jnp.zeros_like(acc_s)

        def step(diagonal):
            for hh in range(nh):
                hc = slice(hh * HEAD_DIM, (hh + 1) * HEAD_DIM)
                kc = jnp.concatenate([k_ref[:, hc], ka_ref[hh]], axis=1)
                vc = jnp.concatenate([v_ref[:, hc], _lane_const(t, 0, 1, 1.0)], axis=1)
                for r in range(ATT_SPLIT):
                    rows = slice(r * sub, (r + 1) * sub)
                    n_k = (r + 1) * sub if diagonal else t
                    sc = _dot_nt(jnp.concatenate([q_ref[rows, hc], qa_ref[hh, rows]], axis=1), kc[:n_k])
                    if diagonal:
                        sc = jnp.where(lax.broadcasted_iota(jnp.int32, (sub, n_k), 1)
                                       <= lax.broadcasted_iota(jnp.int32, (sub, n_k), 0) + r * sub, sc, NEG_INF)
                    m_old = m_s[hh, rows]
                    m_new = jnp.maximum(m_old, jnp.max(sc, axis=-1, keepdims=True))
                    alpha = jnp.exp2(m_old - m_new)
                    pv = _dot(jnp.exp2(sc - m_new[:, 0:1]).astype(BF16), vc[:n_k])
                    acc_s[hh, rows] = alpha * acc_s[hh, rows] + pv[:, :HEAD_DIM]
                    l_s[hh, rows] = alpha * l_s[hh, rows] + pv[:, HEAD_DIM:]
                    m_s[hh, rows] = m_new

        @pl.when(ki < qi)
        def _():
            step(False)

        @pl.when(ki == qi)
        def _():
            step(True)
            lane = lax.broadcasted_iota(jnp.int32, (t, LANES), 1)
            for hh in range(nh):
                hc = slice(hh * HEAD_DIM, (hh + 1) * HEAD_DIM)
                l = l_s[hh, :, 0:1]
                o = acc_s[hh] / l
                o_ref[:, hc] = o
                y_ref[:, hc] = (o * _sig(og_ref[:, hc])).astype(BF16)
                qab_ref[hh] = qa_ref[hh] + _side(lane, Q_LSE, -(m_s[hh, :, 0:1] + jnp.log2(l))).astype(BF16)

    qmain, kmain, qside, kside = _att_specs(t, nh)
    return pl.pallas_call(
        body, name=name,
        grid_spec=pltpu.PrefetchScalarGridSpec(
            num_scalar_prefetch=2, grid=(HEADS // nh, qt.shape[0]),
            in_specs=[qmain, qside, kmain, kside, kmain,
                      pl.BlockSpec((t, nh * HEAD_DIM), lambda h, p, qt, kt: (qt[p], HEADS // nh + h))],
            out_specs=[qmain, qmain, qside],
            scratch_shapes=[pltpu.VMEM((nh, t, LANES), F32), pltpu.VMEM((nh, t, LANES), F32),
                            pltpu.VMEM((nh, t, HEAD_DIM), F32)]),
        out_shape=[jax.ShapeDtypeStruct((s, D_MODEL), F32), jax.ShapeDtypeStruct((s, D_MODEL), BF16),
                   jax.ShapeDtypeStruct((HEADS, s, LANES), BF16)],
    )(qt, kt, q, qa, k, ka, v, qo)


def _fox_gate_bwd(o, qo, dout, w_out, name):
    s = o.shape[0]
    tm = _tile(s, ROW_TILE)

    def body(o_ref, og_ref, dout_ref, w_ref, do_ref, dg_ref, dl_ref):
        ov, dyv = o_ref[...], _dot_nt(dout_ref[0], w_ref[0])
        sg = _sig(og_ref[...])
        do = (dyv * sg).astype(BF16)
        do_ref[...] = do
        dg_ref[...] = (dyv * ov * sg * (1.0 - sg)).astype(BF16)
        prod = do.astype(F32) * ov
        lane = lax.broadcasted_iota(jnp.int32, (tm, LANES), 1)
        for h in range(HEADS):
            delta = jnp.sum(prod[:, h * HEAD_DIM:(h + 1) * HEAD_DIM], axis=-1, keepdims=True)
            dl_ref[h] = _side(lane, 0, delta).astype(BF16)

    row = pl.BlockSpec((tm, D_MODEL), lambda i: (i, 0))
    return pl.pallas_call(
        body, name=name, grid=(s // tm,),
        in_specs=[row, pl.BlockSpec((tm, D_MODEL), lambda i: (i, 1)),
                  pl.BlockSpec((1, tm, D_MODEL), lambda i: (0, i, 0)), _full(w_out.shape)],
        out_specs=[row, row, pl.BlockSpec((HEADS, tm, LANES), lambda i: (0, i, 0))],
        out_shape=[jax.ShapeDtypeStruct((s, D_MODEL), BF16), jax.ShapeDtypeStruct((s, D_MODEL), BF16),
                   jax.ShapeDtypeStruct((HEADS, s, LANES), BF16)],
    )(o, qo, dout, w_out)


def _fox_bwd(q, qab, k, ka, v, do, doa, k_raw, k_gain, name):
    s = q.shape[0]
    t = _tile(s, ATT_TILE)
    n_t = s // t
    sub = t // ATT_SPLIT
    nh = ATT_BWD_HEADS
    qt, kt = _causal_pairs(n_t, key_major=True)

    def body(qt_ref, kt_ref, q_ref, qab_ref, k_ref, ka_ref, v_ref, do_ref, doa_ref, kr_ref, kg_ref, dkr_ref, dkg_ref,
             dv_ref, dka_ref, dq_hbm, dcq_hbm, dk_s, dv_s, dq_ref, dcq_ref):
        group, pid = pl.program_id(0), pl.program_id(1)
        qi, ki = qt_ref[pid], kt_ref[pid]

        @pl.when(pid == 0)
        def _():
            dq_ref[...] = jnp.zeros_like(dq_ref)
            dcq_ref[...] = jnp.zeros_like(dcq_ref)
            dkg_ref[...] = jnp.zeros_like(dkg_ref)

        @pl.when(qi == ki)
        def _():
            dk_s[...] = jnp.zeros_like(dk_s)
            dv_s[...] = jnp.zeros_like(dv_s)

        def step(diagonal):
            for hh in range(nh):
                hc = slice(hh * HEAD_DIM, (hh + 1) * HEAD_DIM)
                kc = jnp.concatenate([k_ref[:, hc], ka_ref[hh]], axis=1)
                vc = jnp.concatenate([v_ref[:, hc], _lane_const(t, 0, 3, -1.0)], axis=1)
                for r in range(ATT_SPLIT):
                    cols = slice(r * sub, (r + 1) * sub)
                    n_k = (r + 1) * sub if diagonal else t
                    qc = jnp.concatenate([q_ref[cols, hc], qab_ref[hh, cols]], axis=1)
                    sc = _dot_nt(kc[:n_k], qc)
                    if diagonal:
                        sc = jnp.where(lax.broadcasted_iota(jnp.int32, (n_k, sub), 0)
                                       <= lax.broadcasted_iota(jnp.int32, (n_k, sub), 1) + r * sub, sc, NEG_INF)
                    p = jnp.exp2(sc)
                    dov = do_ref[cols, hc]
                    dp = _dot_nt(vc[:n_k], jnp.concatenate([dov, doa_ref[hh, cols]], axis=1))
                    ds = (p * dp).astype(BF16)
                    dv_s[hh, 0:n_k] += _dot(p.astype(BF16), dov)
                    dk_s[hh, 0:n_k] += _dot(ds, qc)
                    q_rows = pl.ds(pl.multiple_of(qi * t + r * sub, sub), sub)
                    dq_ref[hh, q_rows, :] += _dot_tn(ds, k_ref[0:n_k, hc])
                    dcq_ref[hh, qi * ATT_SPLIT + r] += jnp.sum(ds.astype(F32), axis=0, keepdims=True)

        @pl.when(qi > ki)
        def _():
            step(False)

        @pl.when(qi == ki)
        def _():
            step(True)

        @pl.when(qi == n_t - 1)
        def _():
            for hh in range(nh):
                hc = slice(hh * HEAD_DIM, (hh + 1) * HEAD_DIM)
                dka_ref[hh] = dk_s[hh, :, HEAD_DIM:]
                dv_ref[:, hc] = dv_s[hh].astype(BF16)
                dk = dk_s[hh, :, :HEAD_DIM] * (1.0 / LOG2E)
                xv = kr_ref[:, hc]
                inv = lax.rsqrt(jnp.mean(xv * xv, axis=-1, keepdims=True) + EPS)
                nrm = xv * inv
                dn = dk * kg_ref[:, hc]
                dkg_ref[:, hc] += _colsum8(dk * nrm)
                dkr_ref[:, hc] = (inv * (dn - nrm * jnp.mean(dn * nrm, axis=-1, keepdims=True))).astype(BF16)

        @pl.when(pid == qt.shape[0] - 1)
        def _():
            pltpu.sync_copy(dq_ref, dq_hbm.at[pl.ds(group * nh, nh)])
            pltpu.sync_copy(dcq_ref, dcq_hbm.at[pl.ds(group * nh, nh)])

    qmain, kmain, qside, kside = _att_specs(t, nh)
    kmain3 = pl.BlockSpec((None, t, nh * HEAD_DIM), lambda h, p, qt, kt: (0, kt[p], h))
    in_hbm = pl.BlockSpec(memory_space=pltpu.HBM)
    return pl.pallas_call(
        body, name=name,
        grid_spec=pltpu.PrefetchScalarGridSpec(
            num_scalar_prefetch=2, grid=(HEADS // nh, qt.shape[0]),
            in_specs=[qmain, qside, kmain, kside, kmain, qmain, qside, kmain,
                      pl.BlockSpec((1, nh * HEAD_DIM), lambda h, p, qt, kt: (0, h))],
            out_specs=[kmain3, pl.BlockSpec((SUBLANES, nh * HEAD_DIM), lambda h, p, qt, kt: (0, h)), kmain3, kside,
                       in_hbm, in_hbm],
            scratch_shapes=[pltpu.VMEM((nh, t, 2 * HEAD_DIM), F32), pltpu.VMEM((nh, t, HEAD_DIM), F32),
                            pltpu.VMEM((nh, s, HEAD_DIM), F32), pltpu.VMEM((nh, s // sub, 1, sub), F32)]),
        out_shape=[jax.ShapeDtypeStruct((1, s, D_MODEL), BF16), jax.ShapeDtypeStruct((SUBLANES, D_MODEL), F32),
                   jax.ShapeDtypeStruct((1, s, D_MODEL), BF16),
                   jax.ShapeDtypeStruct((HEADS, s, LANES), F32), jax.ShapeDtypeStruct((HEADS, s, HEAD_DIM), F32),
                   jax.ShapeDtypeStruct((HEADS, s // sub, 1, sub), F32)],
    )(qt, kt, q, qab, k, ka, v, do, doa, k_raw, k_gain)


def _mm_residual_premix(a, w, x, gate, mods, name):
    s, k = a.shape
    dm = x.shape[1]
    tm = _tile(s, ROW_TILE)

    def body(*refs):
        a_ref, w_ref, x_ref, g_ref = refs[:4]
        mod_refs = refs[4:4 + 2 * len(mods)]
        y_ref, xn_ref = refs[4 + 2 * len(mods):6 + 2 * len(mods)]
        h_refs = refs[6 + 2 * len(mods):]
        y = _dot(a_ref[...], w_ref[0])
        y_ref[...] = y
        xv = x_ref[...] + g_ref[...] * y
        xn_ref[...] = xv
        nrm = xv * lax.rsqrt(jnp.mean(xv * xv, axis=-1, keepdims=True) + EPS)
        for t, h_ref in enumerate(h_refs):
            h_ref[...] = (nrm * (1.0 + mod_refs[2 * t + 1][...]) + mod_refs[2 * t][...]).astype(BF16)

    row = pl.BlockSpec((tm, dm), lambda i: (i, 0))
    vec = _full((1, dm))
    outs = pl.pallas_call(
        body, name=name, grid=(s // tm,),
        in_specs=[pl.BlockSpec((tm, k), lambda i: (i, 0)), _full(w.shape), row, vec] + [vec] * (2 * len(mods)),
        out_specs=[row] * (2 + len(mods)),
        out_shape=[jax.ShapeDtypeStruct((s, dm), F32)] * 2 + [jax.ShapeDtypeStruct((s, dm), BF16)] * len(mods),
    )(a, w, x, gate, *[v for m in mods for v in m])
    return outs[0], outs[1], list(outs[2:])


def _mm_loss_head(a, w, x, gate, target, name):
    s, k = a.shape
    dm = x.shape[1]
    tm = _tile(s, ROW_TILE)

    def body(a_ref, w_ref, x_ref, g_ref, t_ref, sq_ref, do_ref, dy_ref, dg_ref):
        @pl.when(pl.program_id(0) == 0)
        def _():
            sq_ref[...] = jnp.zeros_like(sq_ref)
            dg_ref[...] = jnp.zeros_like(dg_ref)

        y, gv = _dot(a_ref[...], w_ref[0]), g_ref[...]
        err = x_ref[...] + gv * y - t_ref[...]
        sq_ref[...] += _colsum8(err * err)
        dout = err * (1.0 / dm)
        do_ref[...] = dout
        dy_ref[0] = (dout * gv).astype(BF16)
        dg_ref[...] += _colsum8(dout * y)

    row = pl.BlockSpec((tm, dm), lambda i: (i, 0))
    acc = _full((SUBLANES, dm))
    return pl.pallas_call(
        body, name=name, grid=(s // tm,),
        in_specs=[pl.BlockSpec((tm, k), lambda i: (i, 0)), _full(w.shape), row, _full((1, dm)), row],
        out_specs=[acc, row, pl.BlockSpec((1, tm, dm), lambda i: (0, i, 0)), acc],
        out_shape=[jax.ShapeDtypeStruct((SUBLANES, dm), F32), jax.ShapeDtypeStruct((s, dm), F32),
                   jax.ShapeDtypeStruct((1, s, dm), BF16), jax.ShapeDtypeStruct((SUBLANES, dm), F32)],
    )(a, w, x, gate, target)


def _ffn_inner(h, w_up, conv_w, conv_b, tag):
    s, dm = h.shape
    half = w_up.shape[2]
    f = 2 * half
    tm = _tile(s, FFN_ROWS)

    def body(h_ref, w_ref, cw_ref, cb_ref, u_ref, c_ref, a_ref, carry):
        @pl.when(pl.program_id(0) == 0)
        def _():
            carry[...] = jnp.zeros_like(carry)

        hv = h_ref[...]
        for j in range(2):
            cols = slice(j * half, (j + 1) * half)
            conv = []
            for g in range(2):
                ub = _dot(hv, w_ref[2 * g + j]).astype(BF16)
                u_ref[g, :, cols] = ub
                uf = ub.astype(F32)
                e = jnp.concatenate([carry[g, j], uf], axis=0)
                carry[g, j] = uf[tm - SUBLANES:tm]
                conv.append(_conv_taps(e, cw_ref[g][:, cols], cb_ref[g][:, cols])[SUBLANES:])
                c_ref[g, :, cols] = conv[g].astype(BF16)
            a_ref[:, cols] = (conv[0] * _sig(conv[0]) * conv[1]).astype(BF16)

    pair = pl.BlockSpec((2, tm, f), lambda i: (0, i, 0))
    return pl.pallas_call(
        body, name=tag + "_up_convglu", grid=(s // tm,),
        in_specs=[pl.BlockSpec((tm, dm), lambda i: (i, 0)), _full(w_up.shape), _full(conv_w.shape), _full(conv_b.shape)],
        out_specs=[pair, pair, pl.BlockSpec((tm, f), lambda i: (i, 0))],
        out_shape=[jax.ShapeDtypeStruct((2, s, f), BF16)] * 2 + [jax.ShapeDtypeStruct((s, f), BF16)],
        scratch_shapes=[pltpu.VMEM((2, 2, SUBLANES, half), F32)],
    )(h, w_up, conv_w, conv_b)


def _weight_grad_first(a, d, p_n, name):
    return lax.optimization_barrier((_mm_tn(a, d, p_n, name), d))


def _ffn_backward(dx_out, dffn, x_mid, scale, saved, w_up, conv_w, conv_b, w_down, mixer, tag):
    h, u, c, a = saved
    dw_down, dffn = _weight_grad_first(a, dffn, 1, tag + "_down_dw")
    du, dconv = _convglu_bwd(u, c, dffn, w_down, conv_w, tag + "_convglu_bwd")
    dw_up, du = _weight_grad_first(h, du, N_CHIPS, tag + "_up_dw")
    dx_mid, [(dshift, dscale)], dy, dgate_mixer = _premix_bwd(x_mid, [(scale, [(du, w_up)])], dx_out,
                                                              tag + "_premix_bwd", branch=mixer)
    return dx_mid, dy, dgate_mixer, dw_up, dw_down, dict(shift=dshift, scale=dscale, conv=dconv)


def _local_step(x, target, mods, lb, vecs, weights_at):
    m0, m1, mk = mods["l0"], mods["l1"], mods["kv"]
    wts, x = weights_at("mixer0", x)
    h0, proj = _premix_proj(x, m0[0], m0[1], wts["a_w_in"], "l0_premix_in")
    o_a, yp, states = _hgrn_fwd(proj, lb, vecs["a_norm_g"], "l0_hgrn")
    more, yp = weights_at("ffn0", yp)
    wts.update(more)
    y0, x1, [hf0] = _mm_residual_premix(yp, wts["a_w_out"], x, m0[2], [(m0[3], m0[4])], "l0_out")
    u0, c0, a0 = _ffn_inner(hf0, wts["up0"], vecs["conv_w0"], vecs["conv_b0"], "l0_ffn")
    saved0 = (hf0, u0, c0, a0)
    ffn0, x2, [hk, h1] = _mm_residual_premix(a0, wts["down0"], x1, m0[5], [(mk[0], mk[1]), (m1[0], m1[1])],
                                             "l0_ffn_down")
    more, hk = weights_at("layer1", hk)
    wts.update(more)
    k_raw, k_sh, v_sh, f_raw = _kv_proj(hk, wts["kv_k"], wts["kv_v"], wts["kv_f"], vecs["k_norm_g"], "kv_proj")
    qa, ka = _fcum_fwd(f_raw, vecs["kv_b_f"], "kv_fcum")
    q_scale = HEAD_DIM ** -0.5
    qo, q = _proj_headnorm(h1, wts["b_w_q"], vecs["q_norm_g"], q_scale * LOG2E, "l1_q")
    o_b, og, qab = _fox_fwd(q, qa, k_sh, ka, v_sh, qo, "l1_fox")
    y1, x3, [hf1] = _mm_residual_premix(og, wts["b_w_out"], x2, m1[2], [(m1[3], m1[4])], "l1_out")
    u1, c1, a1 = _ffn_inner(hf1, wts["up1"], vecs["conv_w1"], vecs["conv_b1"], "l1_ffn")
    saved1 = (hf1, u1, c1, a1)
    sq, dx4, dffn1, dg2_1 = _mm_loss_head(a1, wts["down1"], x3, m1[5], target, "l1_ffn_down")

    big, small = {}, {}
    dx3, dy1, dg1_1, big["up1"], big["down1"], s_ffn1 = _ffn_backward(
        dx4, dffn1, x3, m1[4], saved1, wts["up1"], vecs["conv_w1"], vecs["conv_b1"], wts["down1"], (y1, m1[2]), "l1_ffn")
    big["b_w_out"], dy1 = _weight_grad_first(og, dy1, 1, "l1_out_dw")
    do_b, dgate_b, doa = _fox_gate_bwd(o_b, qo, dy1, wts["b_w_out"], "l1_out_dx_gate_bwd")
    dk_raw, dkg, dv, dka, dq, dcq = _fox_bwd(q, qab, k_sh, ka, v_sh, do_b, doa, k_raw, vecs["k_norm_g"], "l1_fox_bwd")
    dqo, dqg = _headnorm_bwd(qo, vecs["q_norm_g"], q_scale, dq, "l1_qnorm_bwd", extra=dgate_b)
    big["b_w_q"], dqo = _weight_grad_first(h1, dqo, N_CHIPS, "l1_q_dw")
    dz, dbf = _fcum_bwd(f_raw, vecs["kv_b_f"], dka, dcq.reshape(HEADS, 1, -1), "kv_fcum_bwd")
    big["kv_k"], dk_raw = _weight_grad_first(hk, dk_raw, 1, "kv_k_dw")
    big["kv_v"], dv = _weight_grad_first(hk, dv, 1, "kv_v_dw")
    big["kv_f"], dz = _weight_grad_first(hk, dz, 1, "kv_f_dw")
    kv_pairs = [(dk_raw, wts["kv_k"]), (dv, wts["kv_v"]), (dz, wts["kv_f"])]
    dx2, [(dsh1_1, dsc1_1), (dshk, dsck)], dffn0, dg2_0 = _premix_bwd(
        x2, [(m1[1], [(dqo, wts["b_w_q"])]), (mk[1], kv_pairs)], dx3, "l1_kv_premix_bwd", branch=(ffn0, m0[5]))
    dx1, dy0, dg1_0, big["up0"], big["down0"], s_ffn0 = _ffn_backward(
        dx2, dffn0, x1, m0[4], saved0, wts["up0"], vecs["conv_w0"], vecs["conv_b0"], wts["down0"], (y0, m0[2]), "l0_ffn")
    big["a_w_out"], dy0 = _weight_grad_first(yp, dy0, 1, "l0_out_dw")
    dproj, dlb, dng = _hgrn_bwd(proj, lb, vecs["a_norm_g"], o_a, states, dy0, wts["a_w_out"], "l0_out_dx_hgrn_bwd")
    grad_x, [(dsh1_0, dsc1_0)] = _premix_bwd(x, [(m0[1], [(dproj, wts["a_w_in"])])], dx1, "l0_premix_bwd")
    dproj, _ = lax.optimization_barrier((dproj, (dsh1_0, dsc1_0)))
    big["a_w_in"] = _mm_tn(h0, dproj, N_CHIPS, "l0_in_dw")

    small["mod_l0"] = [dsh1_0, dsc1_0, dg1_0, s_ffn0["shift"], s_ffn0["scale"], dg2_0]
    small["mod_l1"] = [dsh1_1, dsc1_1, dg1_1, s_ffn1["shift"], s_ffn1["scale"], dg2_1]
    small["mod_kv"] = [dshk, dsck]
    small["conv0"], small["conv1"] = s_ffn0["conv"], s_ffn1["conv"]
    small["a_norm_g"], small["k_norm_g"], small["q_norm_g"] = dng, dkg, dqg
    small["kv_b_f"], small["lb"] = dbf, dlb
    marks = {"attention_bwd": dv, "ffn0_bwd": dx1, "mixer0_bwd": grad_x}
    return sq, grad_x, big, small, marks


COMM_CHUNK_ELEMS = 256 * 1024


def _place():
    x, y, c = lax.axis_index("x"), lax.axis_index("y"), lax.axis_index("c")
    chips = [(1 - x, y), (x, 1 - y), (1 - x, 1 - y)]
    return x, y, c, (x, y, 1 - c), chips


def _chunk_rows(rows, cols):
    best = BF16_ROWS
    for r in range(BF16_ROWS, rows + 1, BF16_ROWS):
        if rows % r == 0 and r * cols <= COMM_CHUNK_ELEMS:
            best = r
    assert rows % best == 0, (rows, cols)
    return best


def _allgather8(block, name):
    m_per, n = block.shape

    def body(x_ref, out_ref, send_sems, recv_sems, local_sem):
        x, y, c, sibling, chips = _place()
        me = (x, y, c)

        def rows(px, py, pc):
            return out_ref.at[pl.ds((4 * px + 2 * py + pc) * m_per, m_per), :]

        def copy(k, blk, to, src=None):
            return pltpu.make_async_remote_copy(
                src_ref=rows(*blk) if src is None else src, dst_ref=rows(*blk),
                send_sem=send_sems.at[k], recv_sem=recv_sems.at[k], device_id=to, device_id_type=MESH)

        mine = pltpu.make_async_copy(x_ref, rows(*me), local_sem)
        mine.start()
        first = [copy(0, me, sibling, src=x_ref)]
        first += [copy(1 + j, me, (*chip, c), src=x_ref) for j, chip in enumerate(chips)]
        for cp in first:
            cp.start()
        passed = [copy(4 + j, (*chip, c), sibling) for j, chip in enumerate(chips)]
        for j, chip in enumerate(chips):
            copy(1 + j, (*chip, c), me).wait_recv()
            passed[j].start()
        copy(0, sibling, me).wait_recv()
        for j, chip in enumerate(chips):
            copy(4 + j, (*chip, 1 - c), me).wait_recv()
        for cp in first + passed:
            cp.wait_send()
        mine.wait()

    return pl.pallas_call(
        body, name=name, out_shape=jax.ShapeDtypeStruct((N_DEV * m_per, n), block.dtype),
        in_specs=[pl.BlockSpec(memory_space=pltpu.VMEM)], out_specs=pl.BlockSpec(memory_space=pltpu.VMEM),
        scratch_shapes=[pltpu.SemaphoreType.DMA((7,)), pltpu.SemaphoreType.DMA((7,)), pltpu.SemaphoreType.DMA],
    )(block)


def _cast_own_block(shards, layer, chip, name):
    _, r, cols = shards.shape
    rows = _chunk_rows(r, cols)

    def body(chip_ref, w_ref, o_ref):
        o_ref[...] = w_ref[...].astype(BF16)

    return pl.pallas_call(
        body, name=name,
        grid_spec=pltpu.PrefetchScalarGridSpec(
            num_scalar_prefetch=1, grid=(r // rows,),
            in_specs=[pl.BlockSpec((None, rows, cols), lambda i, chip_ref: (layer, i, 0))],
            out_specs=pl.BlockSpec((None, rows, cols), lambda i, chip_ref: (chip_ref[0], i, 0))),
        out_shape=jax.ShapeDtypeStruct((N_CHIPS, r, cols), BF16),
    )(chip, shards)


def _sequencer_gather(bufs, name, collective_id):
    n_t = len(bufs)
    dims = [b.shape[1:] for b in bufs]
    refs = [jax.new_ref(b, memory_space=pltpu.MemorySpace.HBM) for b in bufs]

    @pl.kernel(mesh=plsc.ScalarSubcoreMesh(axis_name="sequencer", num_cores=1), name=name,
               scratch_types=[pltpu.SemaphoreType.DMA((n_t,)), pltpu.SemaphoreType.DMA((3 * n_t,)),
                              pltpu.SemaphoreType.DMA((n_t,)), pltpu.SemaphoreType.DMA((n_t,))],
               compiler_params=pltpu.CompilerParams(collective_id=collective_id))
    def launch(send_ici, recv_ici, send_d2d, recv_d2d):
        x, y, c, sibling, chips = _place()
        p_me = 2 * x + y
        peers = [sibling] + [(cx, cy, c) for cx, cy in chips]
        barrier = pltpu.get_barrier_semaphore()
        for peer in peers:
            pl.semaphore_signal(barrier, inc=1, device_id=peer, device_id_type=MESH)
        pl.semaphore_wait(barrier, len(peers))

        def waiter(t, sem_s, sem_r):
            win = refs[t].at[pl.ds(0, 3), pl.ds(0, dims[t][0] // 2), :]
            return pltpu.make_async_remote_copy(src_ref=win, dst_ref=win, send_sem=sem_s.at[t], recv_sem=sem_r.at[t],
                                                device_id=sibling, device_id_type=MESH)

        def half_copy(t, chip_idx, to, sem_s, sem_r, k):
            r2 = dims[t][0] // 2
            win = refs[t].at[chip_idx, pl.ds(c * r2, r2), :]
            return pltpu.make_async_remote_copy(src_ref=win, dst_ref=win, send_sem=sem_s.at[t], recv_sem=sem_r.at[k],
                                                device_id=to, device_id_type=MESH)

        for t in range(n_t):
            for j, (cx, cy) in enumerate(chips):
                half_copy(t, p_me, (cx, cy, c), send_ici, recv_ici, 3 * t + j).start()
        for t in range(n_t):
            for j, (cx, cy) in enumerate(chips):
                half_copy(t, 2 * cx + cy, (cx, cy, c), send_ici, recv_ici, 3 * t + j).wait_recv()
                half_copy(t, 2 * cx + cy, sibling, send_d2d, recv_d2d, t).start()
        for t in range(n_t):
            waiter(t, send_d2d, recv_d2d).wait_recv()
            waiter(t, send_ici, recv_ici).wait_send()
            waiter(t, send_d2d, recv_d2d).wait_send()

    launch()
    return [r[...] for r in refs]


def _sequencer_allgather8(block, dev, name, collective_id):
    m_per, n = block.shape
    src = jax.new_ref(block, memory_space=pltpu.MemorySpace.HBM)
    out = jax.empty_ref(jax.ShapeDtypeStruct((N_DEV * m_per, n), block.dtype), memory_space=pltpu.MemorySpace.HBM)

    @pl.kernel(mesh=plsc.ScalarSubcoreMesh(axis_name="sequencer", num_cores=1), name=name,
               scratch_types=[pltpu.SemaphoreType.DMA((7,))] * 2,
               compiler_params=pltpu.CompilerParams(collective_id=collective_id))
    def launch(send_sems, recv_sems):
        x, y, c, sibling, chips = _place()
        me = (x, y, c)
        _handshake([sibling] + [(cx, cy, c) for cx, cy in chips])

        def rows(px, py, pc):
            return out.at[pl.ds((4 * px + 2 * py + pc) * m_per, m_per), :]

        def copy(k, blk, to, from_src=False):
            return pltpu.make_async_remote_copy(
                src_ref=src if from_src else rows(*blk), dst_ref=rows(*blk),
                send_sem=send_sems.at[k], recv_sem=recv_sems.at[k], device_id=to, device_id_type=MESH)

        first = [copy(0, me, sibling, True)] + [copy(1 + j, me, (*chip, c), True) for j, chip in enumerate(chips)]
        for cp in first:
            cp.start()
        passed = [copy(4 + j, (*chip, c), sibling) for j, chip in enumerate(chips)]
        for j, chip in enumerate(chips):
            copy(1 + j, (*chip, c), me).wait_recv()
            passed[j].start()
        copy(0, sibling, me).wait_recv()
        for j, chip in enumerate(chips):
            copy(4 + j, (*chip, 1 - c), me).wait_recv()
        for cp in first + passed:
            cp.wait_send()

    launch()
    return lax.dynamic_update_slice(out[...], block, (dev * m_per, 0))


def _others():
    x, y, c = lax.axis_index("x"), lax.axis_index("y"), lax.axis_index("c")
    flip = lambda v, f: 1 - v if f else v
    return [(flip(x, fx), flip(y, fy), flip(c, fc))
            for fx in (0, 1) for fy in (0, 1) for fc in (0, 1) if (fx, fy, fc) != (0, 0, 0)]


def _handshake(peers):
    barrier = pltpu.get_barrier_semaphore()
    for peer in peers:
        pl.semaphore_signal(barrier, inc=1, device_id=peer, device_id_type=MESH)
    pl.semaphore_wait(barrier, len(peers))


def _sequencer_scatter(parts, name, collective_id):
    n_t = len(parts)
    dims = [p.shape[1:] for p in parts]
    srcs = [jax.new_ref(p, memory_space=pltpu.MemorySpace.HBM) for p in parts]
    inboxes = [jax.empty_ref(jax.ShapeDtypeStruct((N_DEV, r // 2, cols), BF16), memory_space=pltpu.MemorySpace.HBM)
               for r, cols in dims]

    @pl.kernel(mesh=plsc.ScalarSubcoreMesh(axis_name="sequencer", num_cores=1), name=name,
               scratch_types=[pltpu.SemaphoreType.DMA((n_t,))] * 2,
               compiler_params=pltpu.CompilerParams(collective_id=collective_id))
    def launch(send_sem, recv_sem):
        x, y, c = lax.axis_index("x"), lax.axis_index("y"), lax.axis_index("c")
        me = 4 * x + 2 * y + c
        peers = _others()
        _handshake(peers)
        for t in range(n_t):
            h = dims[t][0] // 2
            for qx, qy, qc in peers:
                pltpu.make_async_remote_copy(
                    src_ref=srcs[t].at[2 * qx + qy, pl.ds(qc * h, h), :], dst_ref=inboxes[t].at[me],
                    send_sem=send_sem.at[t], recv_sem=recv_sem.at[t], device_id=(qx, qy, qc), device_id_type=MESH).start()
        for t in range(n_t):
            win = inboxes[t].at[pl.ds(0, N_DEV - 1)]
            both = pltpu.make_async_remote_copy(src_ref=win, dst_ref=win, send_sem=send_sem.at[t],
                                                recv_sem=recv_sem.at[t], device_id=peers[0], device_id_type=MESH)
            both.wait_recv()
            both.wait_send()

    launch()
    return [b[...] for b in inboxes]


def _sum_pieces(part, inbox, place, name):
    _, r, cols = part.shape
    h = r // 2
    rows = _chunk_rows(h, cols)
    steps = h // rows

    def body(place_ref, own_ref, in_ref, o_ref):
        dev = place_ref[2]
        own = own_ref[...].astype(F32)
        acc = jnp.zeros((rows, cols), F32)
        for d in range(N_DEV):
            acc = acc + jnp.where(dev == d, own, in_ref[d].astype(F32))
        o_ref[...] = acc

    return pl.pallas_call(
        body, name=name,
        grid_spec=pltpu.PrefetchScalarGridSpec(
            num_scalar_prefetch=1, grid=(steps,),
            in_specs=[pl.BlockSpec((None, rows, cols), lambda i, pr: (pr[0], pr[1] * steps + i, 0)),
                      pl.BlockSpec((N_DEV, rows, cols), lambda i, pr: (0, i, 0))],
            out_specs=pl.BlockSpec((rows, cols), lambda i, pr: (pr[1] * steps + i, 0))),
        out_shape=jax.ShapeDtypeStruct((r, cols), F32),
    )(place, part, inbox)


def _sequencer_swap_halves(halves, name, collective_id):
    n_t = len(halves)
    refs = [jax.new_ref(a, memory_space=pltpu.MemorySpace.HBM) for a in halves]

    @pl.kernel(mesh=plsc.ScalarSubcoreMesh(axis_name="sequencer", num_cores=1), name=name,
               scratch_types=[pltpu.SemaphoreType.DMA((n_t,))] * 2,
               compiler_params=pltpu.CompilerParams(collective_id=collective_id))
    def launch(send_sem, recv_sem):
        x, y, c = lax.axis_index("x"), lax.axis_index("y"), lax.axis_index("c")
        sibling = (x, y, 1 - c)
        _handshake([sibling])
        copies = []
        for t in range(n_t):
            h = halves[t].shape[0] // 2
            win = refs[t].at[pl.ds(c * h, h), :]
            copies.append(pltpu.make_async_remote_copy(src_ref=win, dst_ref=win, send_sem=send_sem.at[t],
                                                       recv_sem=recv_sem.at[t], device_id=sibling, device_id_type=MESH))
            copies[-1].start()
        for cp in copies:
            cp.wait()

    launch()
    return [r[...] for r in refs]


def _cond_rows(c16, w, act, name):
    n_l, dm, wid = w.shape

    def body(c_ref, w_ref, o_ref, a_ref):
        cv = c_ref[...]
        if act:
            cv = cv * _sig(cv)
        a_ref[...] = cv
        o_ref[...] = _dot_f32(cv, w_ref[...])

    return pl.pallas_call(
        body, name=name, grid=(n_l,),
        in_specs=[_full((16, dm)), pl.BlockSpec((None, dm, wid), lambda l: (l, 0, 0))],
        out_specs=[pl.BlockSpec((None, 16, wid), lambda l: (l, 0, 0)), _full((16, dm))],
        out_shape=[jax.ShapeDtypeStruct((n_l, 16, wid), F32), jax.ShapeDtypeStruct((16, dm), F32)],
    )(c16, w)


def _outer_grad(ct, dm, name):
    n_l, kk, wid = dm.shape
    d_rows = ct.shape[0]

    def body(c_ref, d_ref, o_ref):
        o_ref[...] = _dot_f32(c_ref[...], d_ref[...])

    return pl.pallas_call(
        body, name=name, grid=(n_l,),
        in_specs=[_full((d_rows, kk)), pl.BlockSpec((None, kk, wid), lambda l: (l, 0, 0))],
        out_specs=pl.BlockSpec((None, d_rows, wid), lambda l: (l, 0, 0)),
        out_shape=jax.ShapeDtypeStruct((n_l, d_rows, wid), F32),
    )(ct, dm)


def _sum_devices(g, name):
    rows, n = g.shape

    def body(g_ref, o_ref):
        acc = g_ref[0:SUBLANES, :]
        for dev in range(1, N_DEV):
            acc = acc + g_ref[dev * SUBLANES:(dev + 1) * SUBLANES, :]
        o_ref[...] = acc

    return pl.pallas_call(body, name=name, out_shape=jax.ShapeDtypeStruct((SUBLANES, n), F32))(g)


def _adamw(w, g, m, v, name):
    shape = w.shape
    cols = shape[-1]
    rows = w.size // cols
    tr = rows
    for cand in range(SUBLANES, min(rows, 256) + 1, SUBLANES):
        if rows % cand == 0:
            tr = cand
    if rows * cols <= COMM_CHUNK_ELEMS:
        tr = rows
    c1 = 1.0 / (1.0 - ADAM_B1 ** ADAM_STEP)
    c2 = 1.0 / (1.0 - ADAM_B2 ** ADAM_STEP)

    def body(w_ref, g_ref, m_ref, v_ref, d_ref, mo_ref, vo_ref):
        gv = g_ref[...]
        m_new = ADAM_B1 * m_ref[...] + (1.0 - ADAM_B1) * gv
        v_new = ADAM_B2 * v_ref[...] + (1.0 - ADAM_B2) * (gv * gv)
        mo_ref[...] = m_new
        vo_ref[...] = v_new
        d_ref[...] = -ADAM_LR * ((m_new * c1) / (jnp.sqrt(v_new * c2) + ADAM_EPS) + ADAM_WD * w_ref[...])

    spec = pl.BlockSpec((tr, cols), lambda i: (i, 0))
    outs = pl.pallas_call(
        body, name=name, grid=(rows // tr,), in_specs=[spec] * 4, out_specs=[spec] * 3,
        out_shape=[jax.ShapeDtypeStruct((rows, cols), F32)] * 3,
    )(*[a.reshape(rows, cols) for a in (w, g, m, v)])
    return tuple(o.reshape(shape) for o in outs)


def _pad_cols(a, cols):
    return jnp.pad(a, [(0, 0)] * (a.ndim - 1) + [(0, cols - a.shape[-1])])


def _flat8(parts, width):
    v = jnp.concatenate([p.reshape(-1) for p in parts])
    return jnp.pad(v, (0, width - v.shape[0])).reshape(SUBLANES, width // SUBLANES)


KV_SHARD = 514
KV_SHARD_PAD = 640
BIG = ("a_w_in", "a_w_out", "kv_w", "b_w_q", "b_w_out", "up0", "up1", "down0", "down1")


def kernel(x, c, ada_w, ada_b, a_w_in, a_lb_logits, a_norm_g, a_w_out, kv_ada_w, kv_ada_b, kv_w, kv_b_f, k_norm_g, b_w_q, q_norm_g, b_w_out, ffn_w_up, ffn_conv_w, ffn_conv_b, ffn_w_down, loss_target, m_ada_w, m_ada_b, m_a_w_in, m_a_lb_logits, m_a_norm_g, m_a_w_out, m_kv_ada_w, m_kv_ada_b, m_kv_w, m_kv_b_f, m_k_norm_g, m_b_w_q, m_q_norm_g, m_b_w_out, m_ffn_w_up, m_ffn_conv_w, m_ffn_conv_b, m_ffn_w_down, v_ada_w, v_ada_b, v_a_w_in, v_a_lb_logits, v_a_norm_g, v_a_w_out, v_kv_ada_w, v_kv_ada_b, v_kv_w, v_kv_b_f, v_k_norm_g, v_b_w_q, v_q_norm_g, v_b_w_out, v_ffn_w_up, v_ffn_conv_w, v_ffn_conv_b, v_ffn_w_down):
    dm, ff = D_MODEL, D_FF
    ix, iy, ic = lax.axis_index("x"), lax.axis_index("y"), lax.axis_index("c")
    chip = 2 * ix + iy
    dev = 2 * chip + ic

    w1 = 10240
    g1 = _allgather8(_flat8([c, a_lb_logits, ffn_conv_w], w1), "gather_cond").reshape(N_DEV, w1)
    c_all = g1[:, :dm]
    per_chip = g1[0::2]
    lb_logits = per_chip[:, dm:dm + 512].reshape(N_CHIPS, 2, 256).transpose(1, 0, 2).reshape(2, dm)
    conv_w = per_chip[:, dm + 512:dm + 512 + 2 * CONV_W * FFN_COLS].reshape(N_CHIPS, 2, CONV_W, FFN_COLS)
    conv_w = conv_w.transpose(1, 2, 0, 3).reshape(2, CONV_W, 2, ff).transpose(0, 2, 1, 3)
    conv_b = ffn_conv_b.reshape(2, 2, 1, ff)
    lb = jax.nn.softmax(lb_logits, axis=0)[0:1]

    c16 = jnp.pad(c_all, ((0, 8), (0, 0)))
    mod_ada, c_act16 = _cond_rows(c16, ada_w, True, "mod_ada")
    mod_kv, _ = _cond_rows(c16, kv_ada_w[None], True, "mod_kv")
    mine = jnp.concatenate([mod_ada[0, :8], mod_ada[1, :8], mod_kv[0, :8]], axis=1)
    w2 = mine.shape[1]
    g2 = _allgather8(mine, "gather_mod").reshape(N_DEV, 8, w2)[0::2]
    my_rows = lax.dynamic_index_in_dim(g2, dev, axis=1, keepdims=False)
    mod0 = my_rows[:, 0:1536].reshape(6 * dm) + ada_b[0]
    mod1 = my_rows[:, 1536:3072].reshape(6 * dm) + ada_b[1]
    modk = my_rows[:, 3072:3584].reshape(2 * dm) + kv_ada_b
    mods = {"l0": [v.reshape(1, dm) for v in jnp.split(mod0, 6)],
            "l1": [v.reshape(1, dm) for v in jnp.split(mod1, 6)],
            "kv": [v.reshape(1, dm) for v in jnp.split(modk, 2)]}

    local = [(a_w_in, 0), (a_w_out, 0), (_pad_cols(kv_w, KV_SHARD_PAD)[None], 0), (b_w_q, 0), (b_w_out, 0),
             (ffn_w_up, 0), (ffn_w_up, 1), (ffn_w_down, 0), (ffn_w_down, 1)]
    chip_arr = chip.reshape(1).astype(jnp.int32)
    local = dict(zip(BIG, local))
    stages = {"mixer0": ("a_w_in",), "ffn0": ("a_w_out", "up0", "down0"),
              "layer1": ("kv_w", "b_w_q", "b_w_out", "up1", "down1")}
    arriving = {}

    def launch(stage, behind):
        shards = [local[n][0] for n in stages[stage]]
        if behind is not None:
            shards, _ = lax.optimization_barrier((shards, behind))
        own = [_cast_own_block(w, local[n][1], chip_arr, "cast_" + n) for n, w in zip(stages[stage], shards)]
        arriving[stage] = _sequencer_gather(own, "gather_" + stage, 1 + list(stages).index(stage))

    launch("mixer0", None)
    launch("ffn0", mod0)
    rowwise = lambda g: g.reshape(1, -1, dm)

    def weights_at(stage, token):
        if stage == "ffn0":
            launch("layer1", token)
        got, token = lax.optimization_barrier((arriving[stage], token))
        g = dict(zip(stages[stage], got))
        if stage == "mixer0":
            return {"a_w_in": g["a_w_in"]}, token
        if stage == "ffn0":
            return {"a_w_out": rowwise(g["a_w_out"]), "up0": g["up0"], "down0": rowwise(g["down0"])}, token
        s0, s1, s2, s3 = (g["kv_w"][p] for p in range(N_CHIPS))
        second = dm - KV_SHARD
        w_k = jnp.concatenate([s0[:, :KV_SHARD], s1[:, :second]], axis=1)
        w_v = jnp.concatenate([s1[:, second:KV_SHARD], s2[:, :KV_SHARD], s3[:, :KV_SHARD - HEADS]], axis=1)
        w_f = _pad_cols(s3[:, KV_SHARD - HEADS:KV_SHARD], LANES)
        return {"kv_k": w_k[None], "kv_v": w_v[None], "kv_f": w_f[None], "b_w_q": g["b_w_q"],
                "b_w_out": rowwise(g["b_w_out"]), "up1": g["up1"], "down1": rowwise(g["down1"])}, token

    vecs = {"a_norm_g": jnp.tile(a_norm_g, (1, HEADS)), "k_norm_g": jnp.tile(k_norm_g[None], (1, HEADS)),
            "q_norm_g": jnp.tile(q_norm_g, (1, HEADS)), "kv_b_f": _pad_cols(kv_b_f[None], LANES),
            "conv_w0": conv_w[0], "conv_b0": conv_b[0], "conv_w1": conv_w[1], "conv_b1": conv_b[1]}

    sq, grad_x, big, small, marks = _local_step(x[0], loss_target[0], mods, lb, vecs, weights_at)

    gk, gv, gf = big["kv_k"][0], big["kv_v"][0], big["kv_f"][0][:, :HEADS]
    second = dm - KV_SHARD
    kv_blocks = [gk[:, :KV_SHARD], jnp.concatenate([gk[:, KV_SHARD:], gv[:, :KV_SHARD - second]], axis=1),
                 gv[:, KV_SHARD - second:2 * KV_SHARD - second], jnp.concatenate([gv[:, 2 * KV_SHARD - second:], gf], axis=1)]
    kv_grad = jnp.stack([_pad_cols(b, KV_SHARD_PAD) for b in kv_blocks])
    chipwise = lambda g: g.reshape(N_CHIPS, -1, dm)
    parts = dict(zip(BIG, [big["a_w_in"], chipwise(big["a_w_out"]), kv_grad, big["b_w_q"], chipwise(big["b_w_out"]),
                           big["up0"], big["up1"], chipwise(big["down0"]), chipwise(big["down1"])]))
    place = jnp.stack([chip, ic, dev]).astype(jnp.int32)

    served = []
    boxes = {}

    groups = (("up1", "down1"), ("b_w_out", "b_w_q", "kv_w"), ("up0", "down0", "a_w_out"), ("a_w_in",))

    def scatter_group(k):
        mine = [parts[n] for n in groups[k]]
        if served:
            mine, _ = lax.optimization_barrier((mine, served[-1]))
        boxes[k] = _sequencer_scatter(mine, "scatter_grads_%d" % k, 4 + k)
        served.append(boxes[k])

    def sum_group(k, token):
        inboxes, _ = lax.optimization_barrier((boxes[k], token))
        return [_sum_pieces(parts[n], box, place, "sum_" + n) for n, box in zip(groups[k], inboxes)]

    def swap_group(k, halves, behind):
        halves, _ = lax.optimization_barrier((halves, behind))
        return dict(zip(groups[k], _sequencer_swap_halves(halves, "swap_grads_%d" % k, 8 + k)))

    for k in range(3):
        scatter_group(k)
    halves = [sum_group(0, marks["attention_bwd"]), sum_group(1, marks["ffn0_bwd"]), sum_group(2, marks["mixer0_bwd"])]

    fold = lambda a: a.sum(axis=0)
    heads = lambda a: fold(a).reshape(HEADS, HEAD_DIM).sum(axis=0)
    conv_flat = lambda a: a.sum(axis=2).transpose(1, 0, 2)
    pieces = ([fold(a) for a in small["mod_l0"]] + [fold(a) for a in small["mod_l1"]] + [fold(a) for a in small["mod_kv"]]
              + [conv_flat(small["conv0"]), conv_flat(small["conv1"]), heads(small["a_norm_g"]), heads(small["k_norm_g"]),
                 heads(small["q_norm_g"]), fold(small["kv_b_f"]), fold(small["lb"]),
                 0.5 * jnp.sum(sq).reshape(1) / dm])
    w3 = 61440
    small_vec, _ = lax.optimization_barrier((_flat8(pieces, w3), served[2]))
    g3 = _sequencer_allgather8(small_vec, dev, "gather_small", 12)
    served.append(g3)
    scatter_group(3)
    rs = {}
    for k in range(3):
        rs.update(swap_group(k, halves[k], g3))
    tot = _sum_devices(g3, "sum_small").reshape(w3)
    n_mod = 14 * dm
    dmod_all = g3.reshape(N_DEV, w3)[:, :n_mod]
    o = n_mod
    conv_tot = [tot[o + l * 8 * ff: o + (l + 1) * 8 * ff].reshape(4, 2 * ff) for l in range(2)]
    o += 16 * ff
    g_a_norm, g_k_norm, g_q_norm = (tot[o + i * HEAD_DIM: o + (i + 1) * HEAD_DIM] for i in range(3))
    o += 3 * HEAD_DIM
    g_kv_b_f = tot[o:o + HEADS]
    dlb = tot[o + LANES:o + LANES + dm]
    loss = tot[o + LANES + dm]

    ct = _pad_cols(c_act16[:8].T, LANES)
    dmod_pad = jnp.pad(dmod_all, ((0, LANES - N_DEV), (0, 0)))
    cols_ada = jnp.stack([lax.dynamic_slice_in_dim(dmod_pad, l * 6 * dm + chip * 1536, 1536, axis=1) for l in range(2)])
    cols_kv = lax.dynamic_slice_in_dim(dmod_pad, 12 * dm + chip * 512, 512, axis=1)[None]
    g_ada_w = _outer_grad(ct, cols_ada, "grad_ada_w")
    g_kv_ada_w = _outer_grad(ct, cols_kv, "grad_kv_ada_w")[0]

    my_lb = lax.dynamic_slice_in_dim(lb[0], chip * 256, 256)
    l0 = lax.dynamic_slice_in_dim(dlb, chip * 256, 256) * my_lb * (1.0 - my_lb)
    grads = {
        "ada_w": g_ada_w, "ada_b": jnp.stack([tot[:6 * dm], tot[6 * dm:12 * dm]]),
        "a_lb_logits": jnp.stack([l0, -l0]), "a_norm_g": g_a_norm[None],
        "a_w_out": rs["a_w_out"][None], "kv_ada_w": g_kv_ada_w, "kv_ada_b": tot[12 * dm:14 * dm],
        "kv_w": rs["kv_w"][:, :KV_SHARD], "kv_b_f": g_kv_b_f, "k_norm_g": g_k_norm,
        "b_w_q": rs["b_w_q"][None], "q_norm_g": g_q_norm[None], "b_w_out": rs["b_w_out"][None],
        "ffn_w_up": jnp.stack([rs["up0"], rs["up1"]]),
        "ffn_conv_w": jnp.stack([lax.dynamic_slice_in_dim(ct_l[:CONV_W], chip * FFN_COLS, FFN_COLS, axis=1) for ct_l in conv_tot]),
        "ffn_conv_b": jnp.stack([ct_l[CONV_W] for ct_l in conv_tot]),
        "ffn_w_down": jnp.stack([rs["down0"], rs["down1"]]),
    }
    weights = dict(ada_w=ada_w, ada_b=ada_b, a_w_in=a_w_in, a_lb_logits=a_lb_logits, a_norm_g=a_norm_g, a_w_out=a_w_out,
                   kv_ada_w=kv_ada_w, kv_ada_b=kv_ada_b, kv_w=kv_w, kv_b_f=kv_b_f, k_norm_g=k_norm_g, b_w_q=b_w_q,
                   q_norm_g=q_norm_g, b_w_out=b_w_out, ffn_w_up=ffn_w_up, ffn_conv_w=ffn_conv_w, ffn_conv_b=ffn_conv_b,
                   ffn_w_down=ffn_w_down)
    m_in = dict(ada_w=m_ada_w, ada_b=m_ada_b, a_w_in=m_a_w_in, a_lb_logits=m_a_lb_logits, a_norm_g=m_a_norm_g,
                a_w_out=m_a_w_out, kv_ada_w=m_kv_ada_w, kv_ada_b=m_kv_ada_b, kv_w=m_kv_w, kv_b_f=m_kv_b_f,
                k_norm_g=m_k_norm_g, b_w_q=m_b_w_q, q_norm_g=m_q_norm_g, b_w_out=m_b_w_out, ffn_w_up=m_ffn_w_up,
                ffn_conv_w=m_ffn_conv_w, ffn_conv_b=m_ffn_conv_b, ffn_w_down=m_ffn_w_down)
    v_in = dict(ada_w=v_ada_w, ada_b=v_ada_b, a_w_in=v_a_w_in, a_lb_logits=v_a_lb_logits, a_norm_g=v_a_norm_g,
                a_w_out=v_a_w_out, kv_ada_w=v_kv_ada_w, kv_ada_b=v_kv_ada_b, kv_w=v_kv_w, kv_b_f=v_kv_b_f,
                k_norm_g=v_k_norm_g, b_w_q=v_b_w_q, q_norm_g=v_q_norm_g, b_w_out=v_b_w_out, ffn_w_up=v_ffn_w_up,
                ffn_conv_w=v_ffn_conv_w, ffn_conv_b=v_ffn_conv_b, ffn_w_down=v_ffn_w_down)

    names = list(weights)
    step = lambda n: _adamw(weights[n], grads[n], m_in[n], v_in[n], "adamw_" + n)
    grads = {n: g.reshape(weights[n].shape) for n, g in grads.items()}
    upd = {n: step(n) for n in names if n != "a_w_in"}
    last = sum_group(3, [u[0] for u in upd.values()])
    grads["a_w_in"] = swap_group(3, last, last)["a_w_in"][None]
    upd["a_w_in"] = step("a_w_in")
    return (loss, grad_x[None], *[grads[n] for n in names], *[upd[n][0] for n in names],
            *[upd[n][1] for n in names], *[upd[n][2] for n in names])
```

```python
import jax
import jax.numpy as jnp
from jax import lax
from jax.experimental import pallas as pl
from jax.experimental.pallas import tpu as pltpu
from jax.experimental.pallas import tpu_sc as plsc

F32 = jnp.float32
BF16 = jnp.bfloat16

D_MODEL = 1024
HEADS = 8
HEAD_DIM = 128
A_CHUNK = 64
D_FF = 2816
CONV_W = 3
EPS = 1e-6
NEG_INF = -1e30
N_CHIPS = 4
N_DEV = 8

ADAM_LR = 0.001
ADAM_B1 = 0.9
ADAM_B2 = 0.999
ADAM_EPS = 1e-08
ADAM_WD = 0.01
ADAM_STEP = 10

SUBLANES = 8
BF16_ROWS = 16
LANES = 128
HALO = BF16_ROWS
ROW_TILE = 512
TOKEN_TILE_TN = 2048
FFN_COLS = 1408
FFN_ROWS = 256
HGRN_ROWS = 512
ATT_TILE = 512
ATT_SPLIT = 2
ATT_FWD_HEADS = 8
ATT_BWD_HEADS = 8
MESH = pl.DeviceIdType.MESH


def _sig(x):
    return jax.nn.sigmoid(x)


def _dot(a, b):
    return jnp.dot(a, b, preferred_element_type=F32)


def _dot_nt(a, b):
    return lax.dot_general(a, b, (((1,), (1,)), ((), ())), preferred_element_type=F32)


def _dot_tn(a, b):
    return lax.dot_general(a, b, (((0,), (0,)), ((), ())), preferred_element_type=F32)


def _split2(x):
    hi = x.astype(BF16)
    lo = (x - hi.astype(F32)).astype(BF16)
    return hi, lo


def _dot_f32(a, b):
    ah, al = _split2(a)
    bh, bl = _split2(b)
    return _dot(ah, bh) + _dot(ah, bl) + _dot(al, bh)


def _tri_dot(tri, x):
    hi = x.astype(BF16)
    r = x - hi.astype(F32)
    mid = r.astype(BF16)
    lo = (r - mid.astype(F32)).astype(BF16)
    return _dot(tri, hi) + _dot(tri, mid) + _dot(tri, lo)


def _tri(n, upper=False):
    r = lax.broadcasted_iota(jnp.int32, (n, n), 0)
    c = lax.broadcasted_iota(jnp.int32, (n, n), 1)
    keep = (c >= r) if upper else (c <= r)
    return jnp.where(keep, 1.0, 0.0).astype(BF16)


def _colsum8(v):
    rows, n = v.shape
    return v.reshape(rows // SUBLANES, SUBLANES, n).sum(axis=0)


def _full(shape):
    nd = len(shape)
    return pl.BlockSpec(shape, lambda *_: (0,) * nd)


def _tile(n, want):
    t = min(n, want)
    assert n % t == 0, (n, t)
    return t


def _mm_tn(a, d, p_n, name):
    m_rows, k = a.shape
    g_n, _, w_cols = d.shape
    per = p_n // g_n
    n = w_cols // per
    tm = _tile(m_rows, TOKEN_TILE_TN if k <= D_MODEL else ROW_TILE)
    steps = m_rows // tm

    def body(a_ref, d_ref, o_ref, acc):
        m = pl.program_id(1)

        @pl.when(m == 0)
        def _():
            acc[...] = jnp.zeros_like(acc)

        acc[...] += _dot_tn(a_ref[...], d_ref[...])

        @pl.when(m == steps - 1)
        def _():
            o_ref[...] = acc[...].astype(BF16)

    return pl.pallas_call(
        body, name=name, grid=(p_n, steps),
        in_specs=[pl.BlockSpec((tm, k), lambda p, m: (m, 0)),
                  pl.BlockSpec((None, tm, n), lambda p, m: (p // per, m, p % per))],
        out_specs=pl.BlockSpec((None, k, n), lambda p, m: (p, 0, 0)),
        out_shape=jax.ShapeDtypeStruct((p_n, k, n), BF16),
        scratch_shapes=[pltpu.VMEM((k, n), F32)],
    )(a, d)


def _premix_proj(x, shift, scale, w, name):
    s, dm = x.shape
    p_n, _, n = w.shape
    tm = _tile(s, ROW_TILE)

    def body(x_ref, sh_ref, sc_ref, w_ref, h_ref, o_ref):
        xv = x_ref[...]
        inv = lax.rsqrt(jnp.mean(xv * xv, axis=-1, keepdims=True) + EPS)
        h = (xv * inv * (1.0 + sc_ref[...]) + sh_ref[...]).astype(BF16)
        h_ref[...] = h
        for p in range(p_n):
            o_ref[:, p * n:(p + 1) * n] = _dot(h, w_ref[p])

    row = pl.BlockSpec((tm, dm), lambda i: (i, 0))
    vec = _full((1, dm))
    return pl.pallas_call(
        body, name=name, grid=(s // tm,), in_specs=[row, vec, vec, _full(w.shape)],
        out_specs=[row, pl.BlockSpec((tm, p_n * n), lambda i: (i, 0))],
        out_shape=[jax.ShapeDtypeStruct((s, dm), BF16), jax.ShapeDtypeStruct((s, p_n * n), F32)],
    )(x, shift, scale, w)


def _premix_bwd(x, terms, dres, name, branch=None):
    s, dm = x.shape
    tm = _tile(s, ROW_TILE)
    pairs = [pr for _, prs in terms for pr in prs]
    n_in = 2 + len(terms) + 2 * len(pairs) + (2 if branch else 0)

    def body(*refs):
        x_ref, dres_ref = refs[:2]
        sc_refs = refs[2:2 + len(terms)]
        mm_refs = refs[2 + len(terms):2 + len(terms) + 2 * len(pairs)]
        outs = refs[n_in:]

        @pl.when(pl.program_id(0) == 0)
        def _():
            for o in outs[1:1 + 2 * len(terms)]:
                o[...] = jnp.zeros_like(o)
            if branch:
                outs[-1][...] = jnp.zeros_like(outs[-1])

        xv = x_ref[...]
        inv = lax.rsqrt(jnp.mean(xv * xv, axis=-1, keepdims=True) + EPS)
        r = xv * inv
        dx = dres_ref[...]
        k = 0
        for t, (_, prs) in enumerate(terms):
            dh = None
            for d, w in prs:
                d_ref, w_ref = mm_refs[2 * k], mm_refs[2 * k + 1]
                k += 1
                p_n, _, n = w.shape
                per = p_n // d.shape[0]
                for p in range(p_n):
                    part = _dot_nt(d_ref[p // per, :, (p % per) * n:(p % per + 1) * n], w_ref[p])
                    dh = part if dh is None else dh + part
            dr = dh * (1.0 + sc_refs[t][...])
            dx = dx + inv * (dr - r * jnp.mean(dr * r, axis=-1, keepdims=True))
            outs[1 + 2 * t][...] += _colsum8(dh)
            outs[2 + 2 * t][...] += _colsum8(dh * r)
        outs[0][...] = dx
        if branch:
            y_ref, g_ref = refs[n_in - 2:n_in]
            outs[-2][0] = (dx * g_ref[...]).astype(BF16)
            outs[-1][...] += _colsum8(dx * y_ref[...])

    row = pl.BlockSpec((tm, dm), lambda i: (i, 0))
    vec, acc = _full((1, dm)), _full((SUBLANES, dm))
    ins, specs = [x, dres] + [sc for sc, _ in terms], [row, row] + [vec] * len(terms)
    for d, w in pairs:
        ins += [d, w]
        specs += [pl.BlockSpec((d.shape[0], tm, d.shape[2]), lambda i: (0, i, 0)), _full(w.shape)]
    out_shape = [jax.ShapeDtypeStruct((s, dm), F32)] + [jax.ShapeDtypeStruct((SUBLANES, dm), F32)] * (2 * len(terms))
    out_specs = [row] + [acc] * (2 * len(terms))
    if branch:
        ins += list(branch)
        specs += [row, vec]
        out_shape += [jax.ShapeDtypeStruct((1, s, dm), BF16), jax.ShapeDtypeStruct((SUBLANES, dm), F32)]
        out_specs += [pl.BlockSpec((1, tm, dm), lambda i: (0, i, 0)), acc]
    outs = pl.pallas_call(body, name=name, grid=(s // tm,), in_specs=specs, out_specs=out_specs,
                          out_shape=out_shape)(*ins)
    partials = [(outs[1 + 2 * t], outs[2 + 2 * t]) for t in range(len(terms))]
    return (outs[0], partials) + ((outs[-2], outs[-1]) if branch else ())


def _conv_taps(e, w, b):
    return w[2:3] * e + w[1:2] * pltpu.roll(e, 1, 0) + w[0:1] * pltpu.roll(e, 2, 0) + b


def _ffn_specs(s, tm, cb):
    hb = tm // HALO
    last = s // HALO - 1
    main = pl.BlockSpec((2, tm, cb), lambda j, i: (0, i, j))
    prev = pl.BlockSpec((2, HALO, cb), lambda j, i: (0, jnp.maximum(i * hb - 1, 0), j))
    nxt = pl.BlockSpec((2, HALO, cb), lambda j, i: (0, jnp.minimum((i + 1) * hb, last), j))
    wspec = pl.BlockSpec((2, CONV_W, cb), lambda j, i: (0, 0, j))
    bspec = pl.BlockSpec((2, 1, cb), lambda j, i: (0, 0, j))
    return main, prev, nxt, wspec, bspec


def _convglu_bwd(u, c, dffn, w_down, w, name):
    _, s, f = u.shape
    dm = dffn.shape[2]
    tm = _tile(s, 256)
    cb = _tile(f, FFN_COLS)
    steps = s // tm
    n_ext = tm + HALO
    main, _, nxt, wspec, _ = _ffn_specs(s, tm, cb)
    hb = tm // HALO
    last = s // HALO - 1
    d_main = pl.BlockSpec((None, tm, dm), lambda j, i: (0, i, 0))
    d_next = pl.BlockSpec((None, HALO, dm), lambda j, i: (0, jnp.minimum((i + 1) * hb, last), 0))
    wd_spec = pl.BlockSpec((None, cb, dm), lambda j, i: (0, j, 0))

    def body(u_ref, c_ref, cn_ref, d_ref, dn_ref, wd_ref, w_ref, du_ref, acc_ref):
        i = pl.program_id(1)
        notlast = jnp.where(i < steps - 1, 1.0, 0.0)

        @pl.when(i == 0)
        def _():
            acc_ref[...] = jnp.zeros_like(acc_ref)

        gate, val = (jnp.concatenate([c_ref[g].astype(F32), cn_ref[g].astype(F32)], axis=0) for g in range(2))
        wd = wd_ref[...]
        da = jnp.concatenate([_dot_nt(d_ref[...], wd).astype(BF16).astype(F32),
                              _dot_nt(dn_ref[...], wd).astype(BF16).astype(F32) * notlast], axis=0)
        sg = _sig(gate)
        d_val = da * gate * sg
        d_gate = da * val * (sg * (1.0 + gate * (1.0 - sg)))

        def finish(g, d):
            wv = w_ref[g]
            d1, d2 = pltpu.roll(d, n_ext - 1, 0), pltpu.roll(d, n_ext - 2, 0)
            du_ref[g] = (wv[2:3] * d + wv[1:2] * d1 + wv[0:1] * d2)[0:tm].astype(BF16)
            uv = u_ref[g].astype(F32)
            acc_ref[g, 2] += _colsum8(d[0:tm] * uv)
            acc_ref[g, 1] += _colsum8(d1[0:tm] * uv)
            acc_ref[g, 0] += _colsum8(d2[0:tm] * uv)
            acc_ref[g, 3] += _colsum8(d[0:tm])

        finish(0, d_gate)
        finish(1, d_val)

    return pl.pallas_call(
        body, name=name, grid=(f // cb, steps),
        in_specs=[main, main, nxt, d_main, d_next, wd_spec, wspec],
        out_specs=[main, pl.BlockSpec((2, 4, SUBLANES, cb), lambda j, i: (0, 0, 0, j))],
        out_shape=[jax.ShapeDtypeStruct((2, s, f), BF16), jax.ShapeDtypeStruct((2, 4, SUBLANES, f), F32)],
    )(u, c, c, dffn, dffn, w_down, w)


def _hgrn_gates(q_raw, f_raw, lb, tri):
    sf = _sig(f_raw)
    fg = lb + (1.0 - lb) * sf
    b = _tri_dot(tri, jnp.log(fg))
    return q_raw * _sig(q_raw), 1.0 - fg, b, fg, sf


def _hgrn_fwd(proj, lb, norm_g, name):
    s = proj.shape[0]
    tb = _tile(s, HGRN_ROWS)
    n_c = tb // A_CHUNK
    half = A_CHUNK // 2

    def body(q_ref, f_ref, v_ref, g_ref, lb_ref, ng_ref, o_ref, yp_ref, st_ref, state):
        @pl.when(pl.program_id(0) == 0)
        def _():
            state[...] = jnp.zeros_like(state)

        tri = _tri(A_CHUNK)
        causal = lax.broadcasted_iota(jnp.int32, (A_CHUNK, A_CHUNK), 1) <= lax.broadcasted_iota(
            jnp.int32, (A_CHUNK, A_CHUNK), 0)

        def chunk(ci, carry):
            rows = pl.ds(ci * A_CHUNK, A_CHUNK)
            heads = [slice(h * HEAD_DIM, (h + 1) * HEAD_DIM) for h in range(HEADS)]
            qs, k, b, _, _ = _hgrn_gates(q_ref[rows, :], f_ref[rows, :], lb_ref[...], tri)
            b_mid, b_last = b[half:half + 1], b[A_CHUNK - 1:A_CHUNK]
            q_i = (qs * jnp.exp(b - b_mid)).astype(BF16)
            k_i = (k * jnp.exp(b_mid - b)).astype(BF16)
            q_e = (qs * jnp.exp(b)).astype(BF16)
            k_s = (k * jnp.exp(b_last - b)).astype(BF16)
            decay = jnp.exp(b_last)
            vb = v_ref[rows, :].astype(BF16)
            scores = [jnp.where(causal, _dot_nt(q_i[:, cs], k_i[:, cs]), 0.0).astype(BF16) for cs in heads]
            st = [state[h] for h in range(HEADS)]
            outs = [_dot(scores[h], vb[:, cs]) + _dot_nt(q_e[:, cs], st[h].astype(BF16)) for h, cs in enumerate(heads)]
            for h, cs in enumerate(heads):
                st_ref[ci, h] = st[h]
                state[h] = st[h] * decay[:, cs] + _dot_tn(vb[:, cs], k_s[:, cs])
            o = jnp.concatenate(outs, axis=1)
            o_ref[rows, :] = o
            sq = o * o
            inv = jnp.concatenate([jnp.broadcast_to(lax.rsqrt(jnp.mean(sq[:, cs], axis=-1, keepdims=True) + EPS),
                                                    (A_CHUNK, HEAD_DIM)) for cs in heads], axis=1)
            g_raw = g_ref[rows, :]
            yp_ref[rows, :] = (o * inv * ng_ref[...] * (g_raw * _sig(g_raw))).astype(BF16)
            return carry

        for step in range(n_c):
            chunk(step, 0)

    col = lambda j: pl.BlockSpec((tb, D_MODEL), lambda i: (i, j))
    vec = _full((1, D_MODEL))
    return pl.pallas_call(
        body, name=name, grid=(s // tb,), in_specs=[col(0), col(1), col(2), col(3), vec, vec],
        out_specs=[col(0), col(0), pl.BlockSpec((n_c, HEADS, HEAD_DIM, HEAD_DIM), lambda i: (i, 0, 0, 0))],
        out_shape=[jax.ShapeDtypeStruct((s, D_MODEL), F32), jax.ShapeDtypeStruct((s, D_MODEL), BF16),
                   jax.ShapeDtypeStruct((s // A_CHUNK, HEADS, HEAD_DIM, HEAD_DIM), F32)],
        scratch_shapes=[pltpu.VMEM((HEADS, HEAD_DIM, HEAD_DIM), F32)],
    )(proj, proj, proj, proj, lb, norm_g)


def _hgrn_bwd(proj, lb, norm_g, o, states, dout, w_out, name):
    s = proj.shape[0]
    tb = _tile(s, HGRN_ROWS)
    n_c = tb // A_CHUNK
    n_b = s // tb
    half = A_CHUNK // 2

    def body(q_ref, f_ref, v_ref, g_ref, lb_ref, ng_ref, o_ref, st_ref, dout_ref, w_ref, dp_ref, dlb_ref, dng_ref,
             dstate, dyp_ref):
        @pl.when(pl.program_id(0) == 0)
        def _():
            dstate[...] = jnp.zeros_like(dstate)
            dlb_ref[...] = jnp.zeros_like(dlb_ref)
            dng_ref[...] = jnp.zeros_like(dng_ref)

        dyp_ref[...] = _dot_nt(dout_ref[0], w_ref[0])

        tri = _tri(A_CHUNK)
        tri_up = _tri(A_CHUNK, upper=True)
        row_id = lax.broadcasted_iota(jnp.int32, (A_CHUNK, D_MODEL), 0)
        causal = lax.broadcasted_iota(jnp.int32, (A_CHUNK, A_CHUNK), 1) <= lax.broadcasted_iota(
            jnp.int32, (A_CHUNK, A_CHUNK), 0)

        def chunk(cj, carry):
            ci = n_c - 1 - cj
            rows = pl.ds(ci * A_CHUNK, A_CHUNK)
            heads = [slice(h * HEAD_DIM, (h + 1) * HEAD_DIM) for h in range(HEADS)]
            cat = lambda parts: jnp.concatenate(parts, axis=1)
            per_head_mean = lambda a: cat([jnp.broadcast_to(jnp.mean(a[:, cs], axis=-1, keepdims=True),
                                                            (A_CHUNK, HEAD_DIM)) for cs in heads])
            q_raw, lbv = q_ref[rows, :], lb_ref[...]
            qs, k, b, fg, sf = _hgrn_gates(q_raw, f_ref[rows, :], lbv, tri)
            b_mid, b_last = b[half:half + 1], b[A_CHUNK - 1:A_CHUNK]
            e_qi, e_ki, e_q, e_ks = jnp.exp(b - b_mid), jnp.exp(b_mid - b), jnp.exp(b), jnp.exp(b_last - b)
            decay = jnp.exp(b_last)
            q_i, k_i, q_e, k_s = qs * e_qi, k * e_ki, qs * e_q, k * e_ks
            qib, kib, qeb, ksb = q_i.astype(BF16), k_i.astype(BF16), q_e.astype(BF16), k_s.astype(BF16)
            vb = v_ref[rows, :].astype(BF16)
            ov, g_raw, dy, ng = o_ref[rows, :], g_ref[rows, :], dyp_ref[rows, :], ng_ref[...]
            inv = lax.rsqrt(per_head_mean(ov * ov) + EPS)
            nrm = ov * inv
            sg = _sig(g_raw)
            gs = g_raw * sg
            dn = dy * ng * gs
            dng_ref[0:1, :] += jnp.sum(dy * nrm * gs, axis=0, keepdims=True)
            dg_raw = dy * nrm * ng * (sg * (1.0 + g_raw * (1.0 - sg)))
            do = (inv * (dn - nrm * per_head_mean(dn * nrm))).astype(BF16)
            st_prev = [st_ref[ci, h] for h in range(HEADS)]
            dst = [dstate[h] for h in range(HEADS)]
            dstb = [d.astype(BF16) for d in dst]
            scores = [jnp.where(causal, _dot_nt(qib[:, cs], kib[:, cs]), 0.0).astype(BF16) for cs in heads]
            d_scores = [jnp.where(causal, _dot_nt(do[:, cs], vb[:, cs]), 0.0).astype(BF16) for cs in heads]
            dv = cat([_dot_tn(scores[h], do[:, cs]) + _dot_nt(ksb[:, cs], dstb[h]) for h, cs in enumerate(heads)])
            dq_i = cat([_dot(d_scores[h], kib[:, cs]) for h, cs in enumerate(heads)])
            dk_i = cat([_dot_tn(d_scores[h], qib[:, cs]) for h, cs in enumerate(heads)])
            dq_e = cat([_dot(do[:, cs], st_prev[h].astype(BF16)) for h, cs in enumerate(heads)])
            dk_s = cat([_dot(vb[:, cs], dstb[h]) for h, cs in enumerate(heads)])
            d_decay = cat([jnp.sum(st_prev[h] * dst[h], axis=0, keepdims=True) for h in range(HEADS)])
            for h, cs in enumerate(heads):
                dstate[h] = dst[h] * decay[:, cs] + _dot_tn(do[:, cs], qeb[:, cs])
            dq = dq_i * e_qi + dq_e * e_q
            dk = dk_i * e_ki + dk_s * e_ks
            t_qi, t_ki, t_ks = dq_i * q_i, dk_i * k_i, dk_s * k_s
            db = t_qi - t_ki + dq_e * q_e - t_ks
            db_mid = jnp.sum(t_ki - t_qi, axis=0, keepdims=True)
            db_last = jnp.sum(t_ks, axis=0, keepdims=True) + d_decay * decay
            db = db + jnp.where(row_id == half, db_mid, 0.0) + jnp.where(row_id == A_CHUNK - 1, db_last, 0.0)
            dfg = _tri_dot(tri_up, db) / fg - dk
            dlb_ref[0:1, :] += jnp.sum(dfg * (1.0 - sf), axis=0, keepdims=True)
            sq = _sig(q_raw)
            dp_ref[0, rows, :] = (dq * (sq * (1.0 + q_raw * (1.0 - sq)))).astype(BF16)
            dp_ref[1, rows, :] = (dfg * (1.0 - lbv) * sf * (1.0 - sf)).astype(BF16)
            dp_ref[2, rows, :] = dv.astype(BF16)
            dp_ref[3, rows, :] = dg_raw.astype(BF16)
            return carry

        for step in range(n_c):
            chunk(step, 0)

    col = lambda j: pl.BlockSpec((tb, D_MODEL), lambda i: (n_b - 1 - i, j))
    vec = _full((1, D_MODEL))
    acc = _full((SUBLANES, D_MODEL))
    return pl.pallas_call(
        body, name=name, grid=(n_b,),
        in_specs=[col(0), col(1), col(2), col(3), vec, vec, col(0),
                  pl.BlockSpec((n_c, HEADS, HEAD_DIM, HEAD_DIM), lambda i: (n_b - 1 - i, 0, 0, 0)),
                  pl.BlockSpec((1, tb, D_MODEL), lambda i: (0, n_b - 1 - i, 0)), _full(w_out.shape)],
        out_specs=[pl.BlockSpec((4, tb, D_MODEL), lambda i: (0, n_b - 1 - i, 0)), acc, acc],
        out_shape=[jax.ShapeDtypeStruct((4, s, D_MODEL), BF16), jax.ShapeDtypeStruct((SUBLANES, D_MODEL), F32),
                   jax.ShapeDtypeStruct((SUBLANES, D_MODEL), F32)],
        scratch_shapes=[pltpu.VMEM((HEADS, HEAD_DIM, HEAD_DIM), F32), pltpu.VMEM((tb, D_MODEL), F32)],
    )(proj, proj, proj, proj, lb, norm_g, o, states, dout, w_out)


def _head_rms(raw_ref, g_ref, mult, y_ref):
    for h in range(HEADS):
        cs = slice(h * HEAD_DIM, (h + 1) * HEAD_DIM)
        xv = raw_ref[:, cs]
        inv = lax.rsqrt(jnp.mean(xv * xv, axis=-1, keepdims=True) + EPS)
        y_ref[:, cs] = (xv * inv * g_ref[:, cs] * mult).astype(BF16)


def _proj_headnorm(a, w, g, mult, name):
    s, k = a.shape
    p_n, _, n = w.shape
    tm = _tile(s, ROW_TILE)

    def body(a_ref, w_ref, g_ref, raw_ref, y_ref):
        av = a_ref[...]
        for p in range(p_n):
            raw_ref[:, p * n:(p + 1) * n] = _dot(av, w_ref[p])
        _head_rms(raw_ref, g_ref, mult, y_ref)

    row = lambda wid: pl.BlockSpec((tm, wid), lambda i: (i, 0))
    return pl.pallas_call(
        body, name=name, grid=(s // tm,), in_specs=[row(k), _full(w.shape), _full((1, D_MODEL))],
        out_specs=[row(p_n * n), row(D_MODEL)],
        out_shape=[jax.ShapeDtypeStruct((s, p_n * n), F32), jax.ShapeDtypeStruct((s, D_MODEL), BF16)],
    )(a, w, g)


def _kv_proj(hk, w_k, w_v, w_f, g, name):
    s, k = hk.shape
    tm = _tile(s, ROW_TILE)

    def body(h_ref, wk_ref, wv_ref, wf_ref, g_ref, kr_ref, k_ref, v_ref, f_ref):
        hv = h_ref[...]
        kr_ref[...] = _dot(hv, wk_ref[0])
        v_ref[...] = _dot(hv, wv_ref[0]).astype(BF16)
        f_ref[...] = _dot(hv, wf_ref[0])
        _head_rms(kr_ref, g_ref, 1.0, k_ref)

    row = lambda wid: pl.BlockSpec((tm, wid), lambda i: (i, 0))
    return pl.pallas_call(
        body, name=name, grid=(s // tm,),
        in_specs=[row(k), _full(w_k.shape), _full(w_v.shape), _full(w_f.shape), _full((1, D_MODEL))],
        out_specs=[row(D_MODEL), row(D_MODEL), row(D_MODEL), row(LANES)],
        out_shape=[jax.ShapeDtypeStruct((s, D_MODEL), F32), jax.ShapeDtypeStruct((s, D_MODEL), BF16),
                   jax.ShapeDtypeStruct((s, D_MODEL), BF16), jax.ShapeDtypeStruct((s, LANES), F32)],
    )(hk, w_k, w_v, w_f, g)


def _headnorm_bwd(x, g, mult, dy, name, col0=0, extra=None):
    s = x.shape[0]
    tm = _tile(s, ROW_TILE)
    groups = 2 if extra is not None else 1
    head_major = dy.ndim == 3

    def body(*refs):
        x_ref, g_ref, dy_ref = refs[:3]
        dx_ref, dg_ref = refs[-2:]

        @pl.when(pl.program_id(0) == 0)
        def _():
            dg_ref[...] = jnp.zeros_like(dg_ref)

        for h in range(HEADS):
            cs = slice(h * HEAD_DIM, (h + 1) * HEAD_DIM)
            xv, gv = x_ref[:, cs], g_ref[:, cs]
            dyv = dy_ref[h, :, 0:HEAD_DIM] if head_major else dy_ref[:, cs]
            inv = lax.rsqrt(jnp.mean(xv * xv, axis=-1, keepdims=True) + EPS)
            nrm = xv * inv
            dn = dyv * gv * mult
            dg_ref[:, cs] += _colsum8(dyv * nrm * mult)
            dx_ref[0, :, cs] = (inv * (dn - nrm * jnp.mean(dn * nrm, axis=-1, keepdims=True))).astype(BF16)
        if extra is not None:
            dx_ref[1] = refs[3][...]

    row = pl.BlockSpec((tm, D_MODEL), lambda i: (i, 0))
    dy_spec = pl.BlockSpec((HEADS, tm, dy.shape[-1]), lambda i: (0, i, 0)) if head_major else row
    ins = [x, g, dy] + ([extra] if extra is not None else [])
    specs = ([pl.BlockSpec((tm, D_MODEL), lambda i: (i, col0)), _full((1, D_MODEL)), dy_spec]
             + ([row] if extra is not None else []))
    return pl.pallas_call(
        body, name=name, grid=(s // tm,), in_specs=specs,
        out_specs=[pl.BlockSpec((groups, tm, D_MODEL), lambda i: (0, i, 0)), _full((SUBLANES, D_MODEL))],
        out_shape=[jax.ShapeDtypeStruct((groups, s, D_MODEL), BF16), jax.ShapeDtypeStruct((SUBLANES, D_MODEL), F32)],
    )(*ins)


def _log_sigmoid(z):
    return jnp.minimum(z, 0.0) - jnp.log(1.0 + jnp.exp(-jnp.abs(z)))


Q_CUM, Q_ONE, Q_LSE = 0, 3, 6
LOG2E = 1.4426950408889634


def _pieces(v):
    hi = v.astype(BF16).astype(F32)
    mid = (v - hi).astype(BF16).astype(F32)
    lo = ((v - hi) - mid).astype(BF16).astype(F32)
    return hi, mid, lo


def _side(lane, at, v):
    hi, mid, lo = _pieces(v)
    return jnp.where(lane == at, hi, jnp.where(lane == at + 1, mid, jnp.where(lane == at + 2, lo, 0.0)))


def _fcum_fwd(f, bias, name):
    s = f.shape[0]
    tm = _tile(s, ROW_TILE)

    def body(f_ref, b_ref, qa_ref, ka_ref, carry):
        @pl.when(pl.program_id(0) == 0)
        def _():
            carry[...] = jnp.zeros_like(carry)

        cum = _tri_dot(_tri(tm), _log_sigmoid(f_ref[...] + b_ref[...])) + carry[...]
        carry[...] = cum[tm - 1:tm]
        lane = lax.broadcasted_iota(jnp.int32, (tm, LANES), 1)
        ones_q = jnp.where((lane >= Q_ONE) & (lane < Q_LSE), 1.0, 0.0)
        ones_k = jnp.where((lane < Q_ONE) | ((lane >= Q_LSE) & (lane < Q_LSE + 3)), 1.0, 0.0)
        for h in range(HEADS):
            c2 = cum[:, h:h + 1] * LOG2E
            qa_ref[h] = (_side(lane, Q_CUM, c2) + ones_q).astype(BF16)
            ka_ref[h] = (_side(lane, Q_ONE, -c2) + ones_k).astype(BF16)

    side = pl.BlockSpec((HEADS, tm, LANES), lambda i: (0, i, 0))
    return pl.pallas_call(
        body, name=name, grid=(s // tm,),
        in_specs=[pl.BlockSpec((tm, LANES), lambda i: (i, 0)), _full((1, LANES))],
        out_specs=[side, side],
        out_shape=[jax.ShapeDtypeStruct((HEADS, s, LANES), BF16)] * 2,
        scratch_shapes=[pltpu.VMEM((1, LANES), F32)],
    )(f, bias)


def _fcum_bwd(f, bias, dka, dcq, name):
    s = f.shape[0]
    tm = _tile(s, ROW_TILE)
    n_b = s // tm

    def body(f_ref, b_ref, dka_ref, dcq_ref, dz_ref, db_ref, carry):
        @pl.when(pl.program_id(0) == 0)
        def _():
            carry[...] = jnp.zeros_like(carry)
            db_ref[...] = jnp.zeros_like(db_ref)

        lane = lax.broadcasted_iota(jnp.int32, (tm, LANES), 1)
        rows = jnp.concatenate([dcq_ref[h] for h in range(HEADS)] + [jnp.zeros((LANES - HEADS, tm), F32)], axis=0)
        dcum = rows.T
        for h in range(HEADS):
            dcum = dcum - jnp.where(lane == h, dka_ref[h, :, Q_ONE:Q_ONE + 1], 0.0)
        dlf = _tri_dot(_tri(tm, upper=True), dcum) + carry[...]
        carry[...] = dlf[0:1]
        dz = dlf * _sig(-(f_ref[...] + b_ref[...]))
        dz_ref[0] = dz.astype(BF16)
        db_ref[...] += _colsum8(dz)

    return pl.pallas_call(
        body, name=name, grid=(n_b,),
        in_specs=[pl.BlockSpec((tm, LANES), lambda i: (n_b - 1 - i, 0)), _full((1, LANES)),
                  pl.BlockSpec((HEADS, tm, LANES), lambda i: (0, n_b - 1 - i, 0)),
                  pl.BlockSpec((HEADS, 1, tm), lambda i: (0, 0, n_b - 1 - i))],
        out_specs=[pl.BlockSpec((1, tm, LANES), lambda i: (0, n_b - 1 - i, 0)), _full((SUBLANES, LANES))],
        out_shape=[jax.ShapeDtypeStruct((1, s, LANES), BF16), jax.ShapeDtypeStruct((SUBLANES, LANES), F32)],
        scratch_shapes=[pltpu.VMEM((1, LANES), F32)],
    )(f, bias, dka, dcq)


def _causal_pairs(n_t, key_major):
    if key_major:
        pairs = [(qi, ki) for ki in range(n_t) for qi in range(ki, n_t)]
    else:
        pairs = [(qi, ki) for qi in range(n_t) for ki in range(qi + 1)]
    return (jnp.array([p[0] for p in pairs], jnp.int32), jnp.array([p[1] for p in pairs], jnp.int32))


def _lane_const(t, lo, hi, value):
    lane = lax.broadcasted_iota(jnp.int32, (t, LANES), 1)
    return jnp.where((lane >= lo) & (lane < hi), value, 0.0).astype(BF16)


def _att_specs(t, nh):
    qmain = pl.BlockSpec((t, nh * HEAD_DIM), lambda h, p, qt, kt: (qt[p], h))
    kmain = pl.BlockSpec((t, nh * HEAD_DIM), lambda h, p, qt, kt: (kt[p], h))
    qside = pl.BlockSpec((nh, t, LANES), lambda h, p, qt, kt: (h, qt[p], 0))
    kside = pl.BlockSpec((nh, t, LANES), lambda h, p, qt, kt: (h, kt[p], 0))
    return qmain, kmain, qside, kside


def _fox_fwd(q, qa, k, ka, v, qo, name):
    s = q.shape[0]
    t = _tile(s, ATT_TILE)
    sub = t // ATT_SPLIT
    nh = ATT_FWD_HEADS
    qt, kt = _causal_pairs(s // t, key_major=False)

    def body(qt_ref, kt_ref, q_ref, qa_ref, k_ref, ka_ref, v_ref, og_ref, o_ref, y_ref, qab_ref, m_s, l_s, acc_s):
        pid = pl.program_id(1)
        qi, ki = qt_ref[pid], kt_ref[pid]

        @pl.when(ki == 0)
        def _():
            m_s[...] = jnp.full_like(m_s, NEG_INF)
            l_s[...] = jnp.zeros_like(l_s)
            acc_s[...] = jnp.zeros_like(acc_s)

        def step(diagonal):
            for hh in range(nh):
                hc = slice(hh * HEAD_DIM, (hh + 1) * HEAD_DIM)
                kc = jnp.concatenate([k_ref[:, hc], ka_ref[hh]], axis=1)
                vc = jnp.concatenate([v_ref[:, hc], _lane_const(t, 0, 1, 1.0)], axis=1)
                for r in range(ATT_SPLIT):
                    rows = slice(r * sub, (r + 1) * sub)
                    n_k = (r + 1) * sub if diagonal else t
                    sc = _dot_nt(jnp.concatenate([q_ref[rows, hc], qa_ref[hh, rows]], axis=1), kc[:n_k])
                    if diagonal:
                        sc = jnp.where(lax.broadcasted_iota(jnp.int32, (sub, n_k), 1)
                                       <= lax.broadcasted_iota(jnp.int32, (sub, n_k), 0) + r * sub, sc, NEG_INF)
                    m_old = m_s[hh, rows]
                    m_new = jnp.maximum(m_old, jnp.max(sc, axis=-1, keepdims=True))
                    alpha = jnp.exp2(m_old - m_new)
                    pv = _dot(jnp.exp2(sc - m_new[:, 0:1]).astype(BF16), vc[:n_k])
                    acc_s[hh, rows] = alpha * acc_s[hh, rows] + pv[:, :HEAD_DIM]
                    l_s[hh, rows] = alpha * l_s[hh, rows] + pv[:, HEAD_DIM:]
                    m_s[hh, rows] = m_new

        @pl.when(ki < qi)
        def _():
            step(False)

        @pl.when(ki == qi)
        def _():
            step(True)
            lane = lax.broadcasted_iota(jnp.int32, (t, LANES), 1)
            for hh in range(nh):
                hc = slice(hh * HEAD_DIM, (hh + 1) * HEAD_DIM)
                l = l_s[hh, :, 0:1]
                o = acc_s[hh] / l
                o_ref[:, hc] = o
                y_ref[:, hc] = (o * _sig(og_ref[:, hc])).astype(BF16)
                qab_ref[hh] = qa_ref[hh] + _side(lane, Q_LSE, -(m_s[hh, :, 0:1] + jnp.log2(l))).astype(BF16)

    qmain, kmain, qside, kside = _att_specs(t, nh)
    return pl.pallas_call(
        body, name=name,
        grid_spec=pltpu.PrefetchScalarGridSpec(
            num_scalar_prefetch=2, grid=(HEADS // nh, qt.shape[0]),
            in_specs=[qmain, qside, kmain, kside, kmain,
                      pl.BlockSpec((t, nh * HEAD_DIM), lambda h, p, qt, kt: (qt[p], HEADS // nh + h))],
            out_specs=[qmain, qmain, qside],
            scratch_shapes=[pltpu.VMEM((nh, t, LANES), F32), pltpu.VMEM((nh, t, LANES), F32),
                            pltpu.VMEM((nh, t, HEAD_DIM), F32)]),
        out_shape=[jax.ShapeDtypeStruct((s, D_MODEL), F32), jax.ShapeDtypeStruct((s, D_MODEL), BF16),
                   jax.ShapeDtypeStruct((HEADS, s, LANES), BF16)],
    )(qt, kt, q, qa, k, ka, v, qo)


def _fox_gate_bwd(o, qo, dout, w_out, name):
    s = o.shape[0]
    tm = _tile(s, ROW_TILE)

    def body(o_ref, og_ref, dout_ref, w_ref, do_ref, dg_ref, dl_ref):
        ov, dyv = o_ref[...], _dot_nt(dout_ref[0], w_ref[0])
        sg = _sig(og_ref[...])
        do = (dyv * sg).astype(BF16)
        do_ref[...] = do
        dg_ref[...] = (dyv * ov * sg * (1.0 - sg)).astype(BF16)
        prod = do.astype(F32) * ov
        lane = lax.broadcasted_iota(jnp.int32, (tm, LANES), 1)
        for h in range(HEADS):
            delta = jnp.sum(prod[:, h * HEAD_DIM:(h + 1) * HEAD_DIM], axis=-1, keepdims=True)
            dl_ref[h] = _side(lane, 0, delta).astype(BF16)

    row = pl.BlockSpec((tm, D_MODEL), lambda i: (i, 0))
    return pl.pallas_call(
        body, name=name, grid=(s // tm,),
        in_specs=[row, pl.BlockSpec((tm, D_MODEL), lambda i: (i, 1)),
                  pl.BlockSpec((1, tm, D_MODEL), lambda i: (0, i, 0)), _full(w_out.shape)],
        out_specs=[row, row, pl.BlockSpec((HEADS, tm, LANES), lambda i: (0, i, 0))],
        out_shape=[jax.ShapeDtypeStruct((s, D_MODEL), BF16), jax.ShapeDtypeStruct((s, D_MODEL), BF16),
                   jax.ShapeDtypeStruct((HEADS, s, LANES), BF16)],
    )(o, qo, dout, w_out)


def _fox_bwd(q, qab, k, ka, v, do, doa, k_raw, k_gain, name):
    s = q.shape[0]
    t = _tile(s, ATT_TILE)
    n_t = s // t
    sub = t // ATT_SPLIT
    nh = ATT_BWD_HEADS
    qt, kt = _causal_pairs(n_t, key_major=True)

    def body(qt_ref, kt_ref, q_ref, qab_ref, k_ref, ka_ref, v_ref, do_ref, doa_ref, kr_ref, kg_ref, dkr_ref, dkg_ref,
             dv_ref, dka_ref, dq_hbm, dcq_hbm, dk_s, dv_s, dq_ref, dcq_ref):
        group, pid = pl.program_id(0), pl.program_id(1)
        qi, ki = qt_ref[pid], kt_ref[pid]

        @pl.when(pid == 0)
        def _():
            dq_ref[...] = jnp.zeros_like(dq_ref)
            dcq_ref[...] = jnp.zeros_like(dcq_ref)
            dkg_ref[...] = jnp.zeros_like(dkg_ref)

        @pl.when(qi == ki)
        def _():
            dk_s[...] = jnp.zeros_like(dk_s)
            dv_s[...] = jnp.zeros_like(dv_s)

        def step(diagonal):
            for hh in range(nh):
                hc = slice(hh * HEAD_DIM, (hh + 1) * HEAD_DIM)
                kc = jnp.concatenate([k_ref[:, hc], ka_ref[hh]], axis=1)
                vc = jnp.concatenate([v_ref[:, hc], _lane_const(t, 0, 3, -1.0)], axis=1)
                for r in range(ATT_SPLIT):
                    cols = slice(r * sub, (r + 1) * sub)
                    n_k = (r + 1) * sub if diagonal else t
                    qc = jnp.concatenate([q_ref[cols, hc], qab_ref[hh, cols]], axis=1)
                    sc = _dot_nt(kc[:n_k], qc)
                    if diagonal:
                        sc = jnp.where(lax.broadcasted_iota(jnp.int32, (n_k, sub), 0)
                                       <= lax.broadcasted_iota(jnp.int32, (n_k, sub), 1) + r * sub, sc, NEG_INF)
                    p = jnp.exp2(sc)
                    dov = do_ref[cols, hc]
                    dp = _dot_nt(vc[:n_k], jnp.concatenate([dov, doa_ref[hh, cols]], axis=1))
                    ds = (p * dp).astype(BF16)
                    dv_s[hh, 0:n_k] += _dot(p.astype(BF16), dov)
                    dk_s[hh, 0:n_k] += _dot(ds, qc)
                    q_rows = pl.ds(pl.multiple_of(qi * t + r * sub, sub), sub)
                    dq_ref[hh, q_rows, :] += _dot_tn(ds, k_ref[0:n_k, hc])
                    dcq_ref[hh, qi * ATT_SPLIT + r] += jnp.sum(ds.astype(F32), axis=0, keepdims=True)

        @pl.when(qi > ki)
        def _():
            step(False)

        @pl.when(qi == ki)
        def _():
            step(True)

        @pl.when(qi == n_t - 1)
        def _():
            for hh in range(nh):
                hc = slice(hh * HEAD_DIM, (hh + 1) * HEAD_DIM)
                dka_ref[hh] = dk_s[hh, :, HEAD_DIM:]
                dv_ref[:, hc] = dv_s[hh].astype(BF16)
                dk = dk_s[hh, :, :HEAD_DIM] * (1.0 / LOG2E)
                xv = kr_ref[:, hc]
                inv = lax.rsqrt(jnp.mean(xv * xv, axis=-1, keepdims=True) + EPS)
                nrm = xv * inv
                dn = dk * kg_ref[:, hc]
                dkg_ref[:, hc] += _colsum8(dk * nrm)
                dkr_ref[:, hc] = (inv * (dn - nrm * jnp.mean(dn * nrm, axis=-1, keepdims=True))).astype(BF16)

        @pl.when(pid == qt.shape[0] - 1)
        def _():
            pltpu.sync_copy(dq_ref, dq_hbm.at[pl.ds(group * nh, nh)])
            pltpu.sync_copy(dcq_ref, dcq_hbm.at[pl.ds(group * nh, nh)])

    qmain, kmain, qside, kside = _att_specs(t, nh)
    kmain3 = pl.BlockSpec((None, t, nh * HEAD_DIM), lambda h, p, qt, kt: (0, kt[p], h))
    in_hbm = pl.BlockSpec(memory_space=pltpu.HBM)
    return pl.pallas_call(
        body, name=name,
        grid_spec=pltpu.PrefetchScalarGridSpec(
            num_scalar_prefetch=2, grid=(HEADS // nh, qt.shape[0]),
            in_specs=[qmain, qside, kmain, kside, kmain, qmain, qside, kmain,
                      pl.BlockSpec((1, nh * HEAD_DIM), lambda h, p, qt, kt: (0, h))],
            out_specs=[kmain3, pl.BlockSpec((SUBLANES, nh * HEAD_DIM), lambda h, p, qt, kt: (0, h)), kmain3, kside,
                       in_hbm, in_hbm],
            scratch_shapes=[pltpu.VMEM((nh, t, 2 * HEAD_DIM), F32), pltpu.VMEM((nh, t, HEAD_DIM), F32),
                            pltpu.VMEM((nh, s, HEAD_DIM), F32), pltpu.VMEM((nh, s // sub, 1, sub), F32)]),
        out_shape=[jax.ShapeDtypeStruct((1, s, D_MODEL), BF16), jax.ShapeDtypeStruct((SUBLANES, D_MODEL), F32),
                   jax.ShapeDtypeStruct((1, s, D_MODEL), BF16),
                   jax.ShapeDtypeStruct((HEADS, s, LANES), F32), jax.ShapeDtypeStruct((HEADS, s, HEAD_DIM), F32),
                   jax.ShapeDtypeStruct((HEADS, s // sub, 1, sub), F32)],
    )(qt, kt, q, qab, k, ka, v, do, doa, k_raw, k_gain)


def _mm_residual_premix(a, w, x, gate, mods, name):
    s, k = a.shape
    dm = x.shape[1]
    tm = _tile(s, ROW_TILE)

    def body(*refs):
        a_ref, w_ref, x_ref, g_ref = refs[:4]
        mod_refs = refs[4:4 + 2 * len(mods)]
        y_ref, xn_ref = refs[4 + 2 * len(mods):6 + 2 * len(mods)]
        h_refs = refs[6 + 2 * len(mods):]
        y = _dot(a_ref[...], w_ref[0])
        y_ref[...] = y
        xv = x_ref[...] + g_ref[...] * y
        xn_ref[...] = xv
        nrm = xv * lax.rsqrt(jnp.mean(xv * xv, axis=-1, keepdims=True) + EPS)
        for t, h_ref in enumerate(h_refs):
            h_ref[...] = (nrm * (1.0 + mod_refs[2 * t + 1][...]) + mod_refs[2 * t][...]).astype(BF16)

    row = pl.BlockSpec((tm, dm), lambda i: (i, 0))
    vec = _full((1, dm))
    outs = pl.pallas_call(
        body, name=name, grid=(s // tm,),
        in_specs=[pl.BlockSpec((tm, k), lambda i: (i, 0)), _full(w.shape), row, vec] + [vec] * (2 * len(mods)),
        out_specs=[row] * (2 + len(mods)),
        out_shape=[jax.ShapeDtypeStruct((s, dm), F32)] * 2 + [jax.ShapeDtypeStruct((s, dm), BF16)] * len(mods),
    )(a, w, x, gate, *[v for m in mods for v in m])
    return outs[0], outs[1], list(outs[2:])


def _mm_loss_head(a, w, x, gate, target, name):
    s, k = a.shape
    dm = x.shape[1]
    tm = _tile(s, ROW_TILE)

    def body(a_ref, w_ref, x_ref, g_ref, t_ref, sq_ref, do_ref, dy_ref, dg_ref):
        @pl.when(pl.program_id(0) == 0)
        def _():
            sq_ref[...] = jnp.zeros_like(sq_ref)
            dg_ref[...] = jnp.zeros_like(dg_ref)

        y, gv = _dot(a_ref[...], w_ref[0]), g_ref[...]
        err = x_ref[...] + gv * y - t_ref[...]
        sq_ref[...] += _colsum8(err * err)
        dout = err * (1.0 / dm)
        do_ref[...] = dout
        dy_ref[0] = (dout * gv).astype(BF16)
        dg_ref[...] += _colsum8(dout * y)

    row = pl.BlockSpec((tm, dm), lambda i: (i, 0))
    acc = _full((SUBLANES, dm))
    return pl.pallas_call(
        body, name=name, grid=(s // tm,),
        in_specs=[pl.BlockSpec((tm, k), lambda i: (i, 0)), _full(w.shape), row, _full((1, dm)), row],
        out_specs=[acc, row, pl.BlockSpec((1, tm, dm), lambda i: (0, i, 0)), acc],
        out_shape=[jax.ShapeDtypeStruct((SUBLANES, dm), F32), jax.ShapeDtypeStruct((s, dm), F32),
                   jax.ShapeDtypeStruct((1, s, dm), BF16), jax.ShapeDtypeStruct((SUBLANES, dm), F32)],
    )(a, w, x, gate, target)


def _ffn_inner(h, w_up, conv_w, conv_b, tag):
    s, dm = h.shape
    half = w_up.shape[2]
    f = 2 * half
    tm = _tile(s, FFN_ROWS)

    def body(h_ref, w_ref, cw_ref, cb_ref, u_ref, c_ref, a_ref, carry):
        @pl.when(pl.program_id(0) == 0)
        def _():
            carry[...] = jnp.zeros_like(carry)

        hv = h_ref[...]
        for j in range(2):
            cols = slice(j * half, (j + 1) * half)
            conv = []
            for g in range(2):
                ub = _dot(hv, w_ref[2 * g + j]).astype(BF16)
                u_ref[g, :, cols] = ub
                uf = ub.astype(F32)
                e = jnp.concatenate([carry[g, j], uf], axis=0)
                carry[g, j] = uf[tm - SUBLANES:tm]
                conv.append(_conv_taps(e, cw_ref[g][:, cols], cb_ref[g][:, cols])[SUBLANES:])
                c_ref[g, :, cols] = conv[g].astype(BF16)
            a_ref[:, cols] = (conv[0] * _sig(conv[0]) * conv[1]).astype(BF16)

    pair = pl.BlockSpec((2, tm, f), lambda i: (0, i, 0))
    return pl.pallas_call(
        body, name=tag + "_up_convglu", grid=(s // tm,),
        in_specs=[pl.BlockSpec((tm, dm), lambda i: (i, 0)), _full(w_up.shape), _full(conv_w.shape), _full(conv_b.shape)],
        out_specs=[pair, pair, pl.BlockSpec((tm, f), lambda i: (i, 0))],
        out_shape=[jax.ShapeDtypeStruct((2, s, f), BF16)] * 2 + [jax.ShapeDtypeStruct((s, f), BF16)],
        scratch_shapes=[pltpu.VMEM((2, 2, SUBLANES, half), F32)],
    )(h, w_up, conv_w, conv_b)


def _weight_grad_first(a, d, p_n, name):
    return lax.optimization_barrier((_mm_tn(a, d, p_n, name), d))


def _ffn_backward(dx_out, dffn, x_mid, scale, saved, w_up, conv_w, conv_b, w_down, mixer, tag):
    h, u, c, a = saved
    dw_down, dffn = _weight_grad_first(a, dffn, 1, tag + "_down_dw")
    du, dconv = _convglu_bwd(u, c, dffn, w_down, conv_w, tag + "_convglu_bwd")
    dw_up, du = _weight_grad_first(h, du, N_CHIPS, tag + "_up_dw")
    dx_mid, [(dshift, dscale)], dy, dgate_mixer = _premix_bwd(x_mid, [(scale, [(du, w_up)])], dx_out,
                                                              tag + "_premix_bwd", branch=mixer)
    return dx_mid, dy, dgate_mixer, dw_up, dw_down, dict(shift=dshift, scale=dscale, conv=dconv)


def _local_step(x, target, mods, lb, vecs, weights_at):
    m0, m1, mk = mods["l0"], mods["l1"], mods["kv"]
    wts, x = weights_at("mixer0", x)
    h0, proj = _premix_proj(x, m0[0], m0[1], wts["a_w_in"], "l0_premix_in")
    o_a, yp, states = _hgrn_fwd(proj, lb, vecs["a_norm_g"], "l0_hgrn")
    more, yp = weights_at("ffn0", yp)
    wts.update(more)
    y0, x1, [hf0] = _mm_residual_premix(yp, wts["a_w_out"], x, m0[2], [(m0[3], m0[4])], "l0_out")
    u0, c0, a0 = _ffn_inner(hf0, wts["up0"], vecs["conv_w0"], vecs["conv_b0"], "l0_ffn")
    saved0 = (hf0, u0, c0, a0)
    ffn0, x2, [hk, h1] = _mm_residual_premix(a0, wts["down0"], x1, m0[5], [(mk[0], mk[1]), (m1[0], m1[1])],
                                             "l0_ffn_down")
    more, hk = weights_at("layer1", hk)
    wts.update(more)
    k_raw, k_sh, v_sh, f_raw = _kv_proj(hk, wts["kv_k"], wts["kv_v"], wts["kv_f"], vecs["k_norm_g"], "kv_proj")
    qa, ka = _fcum_fwd(f_raw, vecs["kv_b_f"], "kv_fcum")
    q_scale = HEAD_DIM ** -0.5
    qo, q = _proj_headnorm(h1, wts["b_w_q"], vecs["q_norm_g"], q_scale * LOG2E, "l1_q")
    o_b, og, qab = _fox_fwd(q, qa, k_sh, ka, v_sh, qo, "l1_fox")
    y1, x3, [hf1] = _mm_residual_premix(og, wts["b_w_out"], x2, m1[2], [(m1[3], m1[4])], "l1_out")
    u1, c1, a1 = _ffn_inner(hf1, wts["up1"], vecs["conv_w1"], vecs["conv_b1"], "l1_ffn")
    saved1 = (hf1, u1, c1, a1)
    sq, dx4, dffn1, dg2_1 = _mm_loss_head(a1, wts["down1"], x3, m1[5], target, "l1_ffn_down")

    big, small = {}, {}
    dx3, dy1, dg1_1, big["up1"], big["down1"], s_ffn1 = _ffn_backward(
        dx4, dffn1, x3, m1[4], saved1, wts["up1"], vecs["conv_w1"], vecs["conv_b1"], wts["down1"], (y1, m1[2]), "l1_ffn")
    big["b_w_out"], dy1 = _weight_grad_first(og, dy1, 1, "l1_out_dw")
    do_b, dgate_b, doa = _fox_gate_bwd(o_b, qo, dy1, wts["b_w_out"], "l1_out_dx_gate_bwd")
    dk_raw, dkg, dv, dka, dq, dcq = _fox_bwd(q, qab, k_sh, ka, v_sh, do_b, doa, k_raw, vecs["k_norm_g"], "l1_fox_bwd")
    dqo, dqg = _headnorm_bwd(qo, vecs["q_norm_g"], q_scale, dq, "l1_qnorm_bwd", extra=dgate_b)
    big["b_w_q"], dqo = _weight_grad_first(h1, dqo, N_CHIPS, "l1_q_dw")
    dz, dbf = _fcum_bwd(f_raw, vecs["kv_b_f"], dka, dcq.reshape(HEADS, 1, -1), "kv_fcum_bwd")
    big["kv_k"], dk_raw = _weight_grad_first(hk, dk_raw, 1, "kv_k_dw")
    big["kv_v"], dv = _weight_grad_first(hk, dv, 1, "kv_v_dw")
    big["kv_f"], dz = _weight_grad_first(hk, dz, 1, "kv_f_dw")
    kv_pairs = [(dk_raw, wts["kv_k"]), (dv, wts["kv_v"]), (dz, wts["kv_f"])]
    dx2, [(dsh1_1, dsc1_1), (dshk, dsck)], dffn0, dg2_0 = _premix_bwd(
        x2, [(m1[1], [(dqo, wts["b_w_q"])]), (mk[1], kv_pairs)], dx3, "l1_kv_premix_bwd", branch=(ffn0, m0[5]))
    dx1, dy0, dg1_0, big["up0"], big["down0"], s_ffn0 = _ffn_backward(
        dx2, dffn0, x1, m0[4], saved0, wts["up0"], vecs["conv_w0"], vecs["conv_b0"], wts["down0"], (y0, m0[2]), "l0_ffn")
    big["a_w_out"], dy0 = _weight_grad_first(yp, dy0, 1, "l0_out_dw")
    dproj, dlb, dng = _hgrn_bwd(proj, lb, vecs["a_norm_g"], o_a, states, dy0, wts["a_w_out"], "l0_out_dx_hgrn_bwd")
    grad_x, [(dsh1_0, dsc1_0)] = _premix_bwd(x, [(m0[1], [(dproj, wts["a_w_in"])])], dx1, "l0_premix_bwd")
    dproj, _ = lax.optimization_barrier((dproj, (dsh1_0, dsc1_0)))
    big["a_w_in"] = _mm_tn(h0, dproj, N_CHIPS, "l0_in_dw")

    small["mod_l0"] = [dsh1_0, dsc1_0, dg1_0, s_ffn0["shift"], s_ffn0["scale"], dg2_0]
    small["mod_l1"] = [dsh1_1, dsc1_1, dg1_1, s_ffn1["shift"], s_ffn1["scale"], dg2_1]
    small["mod_kv"] = [dshk, dsck]
    small["conv0"], small["conv1"] = s_ffn0["conv"], s_ffn1["conv"]
    small["a_norm_g"], small["k_norm_g"], small["q_norm_g"] = dng, dkg, dqg
    small["kv_b_f"], small["lb"] = dbf, dlb
    marks = {"attention_bwd": dv, "ffn0_bwd": dx1, "mixer0_bwd": grad_x}
    return sq, grad_x, big, small, marks


COMM_CHUNK_ELEMS = 256 * 1024


def _place():
    x, y, c = lax.axis_index("x"), lax.axis_index("y"), lax.axis_index("c")
    chips = [(1 - x, y), (x, 1 - y), (1 - x, 1 - y)]
    return x, y, c, (x, y, 1 - c), chips


def _chunk_rows(rows, cols):
    best = BF16_ROWS
    for r in range(BF16_ROWS, rows + 1, BF16_ROWS):
        if rows % r == 0 and r * cols <= COMM_CHUNK_ELEMS:
            best = r
    assert rows % best == 0, (rows, cols)
    return best


def _allgather8(block, name):
    m_per, n = block.shape

    def body(x_ref, out_ref, send_sems, recv_sems, local_sem):
        x, y, c, sibling, chips = _place()
        me = (x, y, c)

        def rows(px, py, pc):
            return out_ref.at[pl.ds((4 * px + 2 * py + pc) * m_per, m_per), :]

        def copy(k, blk, to, src=None):
            return pltpu.make_async_remote_copy(
                src_ref=rows(*blk) if src is None else src, dst_ref=rows(*blk),
                send_sem=send_sems.at[k], recv_sem=recv_sems.at[k], device_id=to, device_id_type=MESH)

        mine = pltpu.make_async_copy(x_ref, rows(*me), local_sem)
        mine.start()
        first = [copy(0, me, sibling, src=x_ref)]
        first += [copy(1 + j, me, (*chip, c), src=x_ref) for j, chip in enumerate(chips)]
        for cp in first:
            cp.start()
        passed = [copy(4 + j, (*chip, c), sibling) for j, chip in enumerate(chips)]
        for j, chip in enumerate(chips):
            copy(1 + j, (*chip, c), me).wait_recv()
            passed[j].start()
        copy(0, sibling, me).wait_recv()
        for j, chip in enumerate(chips):
            copy(4 + j, (*chip, 1 - c), me).wait_recv()
        for cp in first + passed:
            cp.wait_send()
        mine.wait()

    return pl.pallas_call(
        body, name=name, out_shape=jax.ShapeDtypeStruct((N_DEV * m_per, n), block.dtype),
        in_specs=[pl.BlockSpec(memory_space=pltpu.VMEM)], out_specs=pl.BlockSpec(memory_space=pltpu.VMEM),
        scratch_shapes=[pltpu.SemaphoreType.DMA((7,)), pltpu.SemaphoreType.DMA((7,)), pltpu.SemaphoreType.DMA],
    )(block)


def _cast_own_block(shards, layer, chip, name):
    _, r, cols = shards.shape
    rows = _chunk_rows(r, cols)

    def body(chip_ref, w_ref, o_ref):
        o_ref[...] = w_ref[...].astype(BF16)

    return pl.pallas_call(
        body, name=name,
        grid_spec=pltpu.PrefetchScalarGridSpec(
            num_scalar_prefetch=1, grid=(r // rows,),
            in_specs=[pl.BlockSpec((None, rows, cols), lambda i, chip_ref: (layer, i, 0))],
            out_specs=pl.BlockSpec((None, rows, cols), lambda i, chip_ref: (chip_ref[0], i, 0))),
        out_shape=jax.ShapeDtypeStruct((N_CHIPS, r, cols), BF16),
    )(chip, shards)


def _sequencer_gather(bufs, name, collective_id):
    n_t = len(bufs)
    dims = [b.shape[1:] for b in bufs]
    refs = [jax.new_ref(b, memory_space=pltpu.MemorySpace.HBM) for b in bufs]

    @pl.kernel(mesh=plsc.ScalarSubcoreMesh(axis_name="sequencer", num_cores=1), name=name,
               scratch_types=[pltpu.SemaphoreType.DMA((n_t,)), pltpu.SemaphoreType.DMA((3 * n_t,)),
                              pltpu.SemaphoreType.DMA((n_t,)), pltpu.SemaphoreType.DMA((n_t,))],
               compiler_params=pltpu.CompilerParams(collective_id=collective_id))
    def launch(send_ici, recv_ici, send_d2d, recv_d2d):
        x, y, c, sibling, chips = _place()
        p_me = 2 * x + y
        peers = [sibling] + [(cx, cy, c) for cx, cy in chips]
        barrier = pltpu.get_barrier_semaphore()
        for peer in peers:
            pl.semaphore_signal(barrier, inc=1, device_id=peer, device_id_type=MESH)
        pl.semaphore_wait(barrier, len(peers))

        def waiter(t, sem_s, sem_r):
            win = refs[t].at[pl.ds(0, 3), pl.ds(0, dims[t][0] // 2), :]
            return pltpu.make_async_remote_copy(src_ref=win, dst_ref=win, send_sem=sem_s.at[t], recv_sem=sem_r.at[t],
                                                device_id=sibling, device_id_type=MESH)

        def half_copy(t, chip_idx, to, sem_s, sem_r, k):
            r2 = dims[t][0] // 2
            win = refs[t].at[chip_idx, pl.ds(c * r2, r2), :]
            return pltpu.make_async_remote_copy(src_ref=win, dst_ref=win, send_sem=sem_s.at[t], recv_sem=sem_r.at[k],
                                                device_id=to, device_id_type=MESH)

        for t in range(n_t):
            for j, (cx, cy) in enumerate(chips):
                half_copy(t, p_me, (cx, cy, c), send_ici, recv_ici, 3 * t + j).start()
        for t in range(n_t):
            for j, (cx, cy) in enumerate(chips):
                half_copy(t, 2 * cx + cy, (cx, cy, c), send_ici, recv_ici, 3 * t + j).wait_recv()
                half_copy(t, 2 * cx + cy, sibling, send_d2d, recv_d2d, t).start()
        for t in range(n_t):
            waiter(t, send_d2d, recv_d2d).wait_recv()
            waiter(t, send_ici, recv_ici).wait_send()
            waiter(t, send_d2d, recv_d2d).wait_send()

    launch()
    return [r[...] for r in refs]


def _sequencer_allgather8(block, dev, name, collective_id):
    m_per, n = block.shape
    src = jax.new_ref(block, memory_space=pltpu.MemorySpace.HBM)
    out = jax.empty_ref(jax.ShapeDtypeStruct((N_DEV * m_per, n), block.dtype), memory_space=pltpu.MemorySpace.HBM)

    @pl.kernel(mesh=plsc.ScalarSubcoreMesh(axis_name="sequencer", num_cores=1), name=name,
               scratch_types=[pltpu.SemaphoreType.DMA((7,))] * 2,
               compiler_params=pltpu.CompilerParams(collective_id=collective_id))
    def launch(send_sems, recv_sems):
        x, y, c, sibling, chips = _place()
        me = (x, y, c)
        _handshake([sibling] + [(cx, cy, c) for cx, cy in chips])

        def rows(px, py, pc):
            return out.at[pl.ds((4 * px + 2 * py + pc) * m_per, m_per), :]

        def copy(k, blk, to, from_src=False):
            return pltpu.make_async_remote_copy(
                src_ref=src if from_src else rows(*blk), dst_ref=rows(*blk),
                send_sem=send_sems.at[k], recv_sem=recv_sems.at[k], device_id=to, device_id_type=MESH)

        first = [copy(0, me, sibling, True)] + [copy(1 + j, me, (*chip, c), True) for j, chip in enumerate(chips)]
        for cp in first:
            cp.start()
        passed = [copy(4 + j, (*chip, c), sibling) for j, chip in enumerate(chips)]
        for j, chip in enumerate(chips):
            copy(1 + j, (*chip, c), me).wait_recv()
            passed[j].start()
        copy(0, sibling, me).wait_recv()
        for j, chip in enumerate(chips):
            copy(4 + j, (*chip, 1 - c), me).wait_recv()
        for cp in first + passed:
            cp.wait_send()

    launch()
    return lax.dynamic_update_slice(out[...], block, (dev * m_per, 0))


def _others():
    x, y, c = lax.axis_index("x"), lax.axis_index("y"), lax.axis_index("c")
    flip = lambda v, f: 1 - v if f else v
    return [(flip(x, fx), flip(y, fy), flip(c, fc))
            for fx in (0, 1) for fy in (0, 1) for fc in (0, 1) if (fx, fy, fc) != (0, 0, 0)]


def _handshake(peers):
    barrier = pltpu.get_barrier_semaphore()
    for peer in peers:
        pl.semaphore_signal(barrier, inc=1, device_id=peer, device_id_type=MESH)
    pl.semaphore_wait(barrier, len(peers))


def _sequencer_scatter(parts, name, collective_id):
    n_t = len(parts)
    dims = [p.shape[1:] for p in parts]
    srcs = [jax.new_ref(p, memory_space=pltpu.MemorySpace.HBM) for p in parts]
    inboxes = [jax.empty_ref(jax.ShapeDtypeStruct((N_DEV - 1, r // 2, cols), BF16), memory_space=pltpu.MemorySpace.HBM)
               for r, cols in dims]

    @pl.kernel(mesh=plsc.ScalarSubcoreMesh(axis_name="sequencer", num_cores=1), name=name,
               scratch_types=[pltpu.SemaphoreType.DMA((n_t,))] * 2,
               compiler_params=pltpu.CompilerParams(collective_id=collective_id))
    def launch(send_sem, recv_sem):
        peers = _others()
        _handshake(peers)
        for t in range(n_t):
            h = dims[t][0] // 2
            for k, (qx, qy, qc) in enumerate(peers):
                pltpu.make_async_remote_copy(
                    src_ref=srcs[t].at[2 * qx + qy, pl.ds(qc * h, h), :], dst_ref=inboxes[t].at[k],
                    send_sem=send_sem.at[t], recv_sem=recv_sem.at[t], device_id=(qx, qy, qc), device_id_type=MESH).start()
        for t in range(n_t):
            win = inboxes[t]
            both = pltpu.make_async_remote_copy(src_ref=win, dst_ref=win, send_sem=send_sem.at[t],
                                                recv_sem=recv_sem.at[t], device_id=peers[0], device_id_type=MESH)
            both.wait_recv()
            both.wait_send()

    launch()
    return [b[...] for b in inboxes]


def _sum_pieces(part, inbox, place, name):
    _, r, cols = part.shape
    h = r // 2
    rows = _chunk_rows(h, cols)
    steps = h // rows

    def body(place_ref, own_ref, in_ref, o_ref):
        acc = own_ref[...].astype(F32)
        for k in range(N_DEV - 1):
            acc = acc + in_ref[k].astype(F32)
        o_ref[...] = acc

    return pl.pallas_call(
        body, name=name,
        grid_spec=pltpu.PrefetchScalarGridSpec(
            num_scalar_prefetch=1, grid=(steps,),
            in_specs=[pl.BlockSpec((None, rows, cols), lambda i, pr: (pr[0], pr[1] * steps + i, 0)),
                      pl.BlockSpec((N_DEV - 1, rows, cols), lambda i, pr: (0, i, 0))],
            out_specs=pl.BlockSpec((rows, cols), lambda i, pr: (pr[1] * steps + i, 0))),
        out_shape=jax.ShapeDtypeStruct((r, cols), F32),
    )(place, part, inbox)


def _sequencer_swap_halves(halves, name, collective_id):
    n_t = len(halves)
    refs = [jax.new_ref(a, memory_space=pltpu.MemorySpace.HBM) for a in halves]

    @pl.kernel(mesh=plsc.ScalarSubcoreMesh(axis_name="sequencer", num_cores=1), name=name,
               scratch_types=[pltpu.SemaphoreType.DMA((n_t,))] * 2,
               compiler_params=pltpu.CompilerParams(collective_id=collective_id))
    def launch(send_sem, recv_sem):
        x, y, c = lax.axis_index("x"), lax.axis_index("y"), lax.axis_index("c")
        sibling = (x, y, 1 - c)
        _handshake([sibling])
        copies = []
        for t in range(n_t):
            h = halves[t].shape[0] // 2
            win = refs[t].at[pl.ds(c * h, h), :]
            copies.append(pltpu.make_async_remote_copy(src_ref=win, dst_ref=win, send_sem=send_sem.at[t],
                                                       recv_sem=recv_sem.at[t], device_id=sibling, device_id_type=MESH))
            copies[-1].start()
        for cp in copies:
            cp.wait()

    launch()
    return [r[...] for r in refs]


def _cond_rows(c16, w, act, name):
    n_l, dm, wid = w.shape

    def body(c_ref, w_ref, o_ref, a_ref):
        cv = c_ref[...]
        if act:
            cv = cv * _sig(cv)
        a_ref[...] = cv
        o_ref[...] = _dot_f32(cv, w_ref[...])

    return pl.pallas_call(
        body, name=name, grid=(n_l,),
        in_specs=[_full((16, dm)), pl.BlockSpec((None, dm, wid), lambda l: (l, 0, 0))],
        out_specs=[pl.BlockSpec((None, 16, wid), lambda l: (l, 0, 0)), _full((16, dm))],
        out_shape=[jax.ShapeDtypeStruct((n_l, 16, wid), F32), jax.ShapeDtypeStruct((16, dm), F32)],
    )(c16, w)


def _outer_grad(ct, dm, name):
    n_l, kk, wid = dm.shape
    d_rows = ct.shape[0]

    def body(c_ref, d_ref, o_ref):
        o_ref[...] = _dot_f32(c_ref[...], d_ref[...])

    return pl.pallas_call(
        body, name=name, grid=(n_l,),
        in_specs=[_full((d_rows, kk)), pl.BlockSpec((None, kk, wid), lambda l: (l, 0, 0))],
        out_specs=pl.BlockSpec((None, d_rows, wid), lambda l: (l, 0, 0)),
        out_shape=jax.ShapeDtypeStruct((n_l, d_rows, wid), F32),
    )(ct, dm)


def _sum_devices(g, name):
    rows, n = g.shape

    def body(g_ref, o_ref):
        acc = g_ref[0:SUBLANES, :]
        for dev in range(1, N_DEV):
            acc = acc + g_ref[dev * SUBLANES:(dev + 1) * SUBLANES, :]
        o_ref[...] = acc

    return pl.pallas_call(body, name=name, out_shape=jax.ShapeDtypeStruct((SUBLANES, n), F32))(g)


def _adamw(w, g, m, v, name):
    shape = w.shape
    cols = shape[-1]
    rows = w.size // cols
    tr = rows
    for cand in range(SUBLANES, min(rows, 256) + 1, SUBLANES):
        if rows % cand == 0:
            tr = cand
    if rows * cols <= COMM_CHUNK_ELEMS:
        tr = rows
    c1 = 1.0 / (1.0 - ADAM_B1 ** ADAM_STEP)
    c2 = 1.0 / (1.0 - ADAM_B2 ** ADAM_STEP)

    def body(w_ref, g_ref, m_ref, v_ref, d_ref, mo_ref, vo_ref):
        gv = g_ref[...]
        m_new = ADAM_B1 * m_ref[...] + (1.0 - ADAM_B1) * gv
        v_new = ADAM_B2 * v_ref[...] + (1.0 - ADAM_B2) * (gv * gv)
        mo_ref[...] = m_new
        vo_ref[...] = v_new
        d_ref[...] = -ADAM_LR * ((m_new * c1) / (jnp.sqrt(v_new * c2) + ADAM_EPS) + ADAM_WD * w_ref[...])

    spec = pl.BlockSpec((tr, cols), lambda i: (i, 0))
    outs = pl.pallas_call(
        body, name=name, grid=(rows // tr,), in_specs=[spec] * 4, out_specs=[spec] * 3,
        out_shape=[jax.ShapeDtypeStruct((rows, cols), F32)] * 3,
    )(*[a.reshape(rows, cols) for a in (w, g, m, v)])
    return tuple(o.reshape(shape) for o in outs)


def _pad_cols(a, cols):
    return jnp.pad(a, [(0, 0)] * (a.ndim - 1) + [(0, cols - a.shape[-1])])


def _flat8(parts, width):
    v = jnp.concatenate([p.reshape(-1) for p in parts])
    return jnp.pad(v, (0, width - v.shape[0])).reshape(SUBLANES, width // SUBLANES)


KV_SHARD = 514
KV_SHARD_PAD = 640
BIG = ("a_w_in", "a_w_out", "kv_w", "b_w_q", "b_w_out", "up0", "up1", "down0", "down1")


def kernel(x, c, ada_w, ada_b, a_w_in, a_lb_logits, a_norm_g, a_w_out, kv_ada_w, kv_ada_b, kv_w, kv_b_f, k_norm_g, b_w_q, q_norm_g, b_w_out, ffn_w_up, ffn_conv_w, ffn_conv_b, ffn_w_down, loss_target, m_ada_w, m_ada_b, m_a_w_in, m_a_lb_logits, m_a_norm_g, m_a_w_out, m_kv_ada_w, m_kv_ada_b, m_kv_w, m_kv_b_f, m_k_norm_g, m_b_w_q, m_q_norm_g, m_b_w_out, m_ffn_w_up, m_ffn_conv_w, m_ffn_conv_b, m_ffn_w_down, v_ada_w, v_ada_b, v_a_w_in, v_a_lb_logits, v_a_norm_g, v_a_w_out, v_kv_ada_w, v_kv_ada_b, v_kv_w, v_kv_b_f, v_k_norm_g, v_b_w_q, v_q_norm_g, v_b_w_out, v_ffn_w_up, v_ffn_conv_w, v_ffn_conv_b, v_ffn_w_down):
    dm, ff = D_MODEL, D_FF
    ix, iy, ic = lax.axis_index("x"), lax.axis_index("y"), lax.axis_index("c")
    chip = 2 * ix + iy
    dev = 2 * chip + ic

    w1 = 10240
    g1 = _allgather8(_flat8([c, a_lb_logits, ffn_conv_w], w1), "gather_cond").reshape(N_DEV, w1)
    c_all = g1[:, :dm]
    per_chip = g1[0::2]
    lb_logits = per_chip[:, dm:dm + 512].reshape(N_CHIPS, 2, 256).transpose(1, 0, 2).reshape(2, dm)
    conv_w = per_chip[:, dm + 512:dm + 512 + 2 * CONV_W * FFN_COLS].reshape(N_CHIPS, 2, CONV_W, FFN_COLS)
    conv_w = conv_w.transpose(1, 2, 0, 3).reshape(2, CONV_W, 2, ff).transpose(0, 2, 1, 3)
    conv_b = ffn_conv_b.reshape(2, 2, 1, ff)
    lb = jax.nn.softmax(lb_logits, axis=0)[0:1]

    c16 = jnp.pad(c_all, ((0, 8), (0, 0)))
    mod_ada, c_act16 = _cond_rows(c16, ada_w, True, "mod_ada")
    mod_kv, _ = _cond_rows(c16, kv_ada_w[None], True, "mod_kv")
    mine = jnp.concatenate([mod_ada[0, :8], mod_ada[1, :8], mod_kv[0, :8]], axis=1)
    w2 = mine.shape[1]
    g2 = _allgather8(mine, "gather_mod").reshape(N_DEV, 8, w2)[0::2]
    my_rows = lax.dynamic_index_in_dim(g2, dev, axis=1, keepdims=False)
    mod0 = my_rows[:, 0:1536].reshape(6 * dm) + ada_b[0]
    mod1 = my_rows[:, 1536:3072].reshape(6 * dm) + ada_b[1]
    modk = my_rows[:, 3072:3584].reshape(2 * dm) + kv_ada_b
    mods = {"l0": [v.reshape(1, dm) for v in jnp.split(mod0, 6)],
            "l1": [v.reshape(1, dm) for v in jnp.split(mod1, 6)],
            "kv": [v.reshape(1, dm) for v in jnp.split(modk, 2)]}

    local = [(a_w_in, 0), (a_w_out, 0), (_pad_cols(kv_w, KV_SHARD_PAD)[None], 0), (b_w_q, 0), (b_w_out, 0),
             (ffn_w_up, 0), (ffn_w_up, 1), (ffn_w_down, 0), (ffn_w_down, 1)]
    chip_arr = chip.reshape(1).astype(jnp.int32)
    local = dict(zip(BIG, local))
    stages = {"mixer0": ("a_w_in",), "ffn0": ("a_w_out", "up0", "down0"),
              "layer1": ("kv_w", "b_w_q", "b_w_out", "up1", "down1")}
    arriving = {}

    def launch(stage, behind):
        shards = [local[n][0] for n in stages[stage]]
        if behind is not None:
            shards, _ = lax.optimization_barrier((shards, behind))
        own = [_cast_own_block(w, local[n][1], chip_arr, "cast_" + n) for n, w in zip(stages[stage], shards)]
        arriving[stage] = _sequencer_gather(own, "gather_" + stage, 1 + list(stages).index(stage))

    launch("mixer0", None)
    launch("ffn0", mod0)
    rowwise = lambda g: g.reshape(1, -1, dm)

    def weights_at(stage, token):
        if stage == "ffn0":
            launch("layer1", token)
        got, token = lax.optimization_barrier((arriving[stage], token))
        g = dict(zip(stages[stage], got))
        if stage == "mixer0":
            return {"a_w_in": g["a_w_in"]}, token
        if stage == "ffn0":
            return {"a_w_out": rowwise(g["a_w_out"]), "up0": g["up0"], "down0": rowwise(g["down0"])}, token
        s0, s1, s2, s3 = (g["kv_w"][p] for p in range(N_CHIPS))
        second = dm - KV_SHARD
        w_k = jnp.concatenate([s0[:, :KV_SHARD], s1[:, :second]], axis=1)
        w_v = jnp.concatenate([s1[:, second:KV_SHARD], s2[:, :KV_SHARD], s3[:, :KV_SHARD - HEADS]], axis=1)
        w_f = _pad_cols(s3[:, KV_SHARD - HEADS:KV_SHARD], LANES)
        return {"kv_k": w_k[None], "kv_v": w_v[None], "kv_f": w_f[None], "b_w_q": g["b_w_q"],
                "b_w_out": rowwise(g["b_w_out"]), "up1": g["up1"], "down1": rowwise(g["down1"])}, token

    vecs = {"a_norm_g": jnp.tile(a_norm_g, (1, HEADS)), "k_norm_g": jnp.tile(k_norm_g[None], (1, HEADS)),
            "q_norm_g": jnp.tile(q_norm_g, (1, HEADS)), "kv_b_f": _pad_cols(kv_b_f[None], LANES),
            "conv_w0": conv_w[0], "conv_b0": conv_b[0], "conv_w1": conv_w[1], "conv_b1": conv_b[1]}

    sq, grad_x, big, small, marks = _local_step(x[0], loss_target[0], mods, lb, vecs, weights_at)

    gk, gv, gf = big["kv_k"][0], big["kv_v"][0], big["kv_f"][0][:, :HEADS]
    second = dm - KV_SHARD
    kv_blocks = [gk[:, :KV_SHARD], jnp.concatenate([gk[:, KV_SHARD:], gv[:, :KV_SHARD - second]], axis=1),
                 gv[:, KV_SHARD - second:2 * KV_SHARD - second], jnp.concatenate([gv[:, 2 * KV_SHARD - second:], gf], axis=1)]
    kv_grad = jnp.stack([_pad_cols(b, KV_SHARD_PAD) for b in kv_blocks])
    chipwise = lambda g: g.reshape(N_CHIPS, -1, dm)
    parts = dict(zip(BIG, [big["a_w_in"], chipwise(big["a_w_out"]), kv_grad, big["b_w_q"], chipwise(big["b_w_out"]),
                           big["up0"], big["up1"], chipwise(big["down0"]), chipwise(big["down1"])]))
    place = jnp.stack([chip, ic, dev]).astype(jnp.int32)

    served = []
    boxes = {}

    groups = (("up1", "down1"), ("b_w_out", "b_w_q", "kv_w"), ("up0", "down0", "a_w_out"), ("a_w_in",))

    def scatter_group(k):
        mine = [parts[n] for n in groups[k]]
        if served:
            mine, _ = lax.optimization_barrier((mine, served[-1]))
        boxes[k] = _sequencer_scatter(mine, "scatter_grads_%d" % k, 4 + k)
        served.append(boxes[k])

    def sum_group(k, token):
        inboxes, _ = lax.optimization_barrier((boxes[k], token))
        return [_sum_pieces(parts[n], box, place, "sum_" + n) for n, box in zip(groups[k], inboxes)]

    def swap_group(k, halves, behind):
        halves, _ = lax.optimization_barrier((halves, behind))
        return dict(zip(groups[k], _sequencer_swap_halves(halves, "swap_grads_%d" % k, 8 + k)))

    for k in range(3):
        scatter_group(k)
    halves = [sum_group(0, marks["attention_bwd"]), sum_group(1, marks["ffn0_bwd"]), sum_group(2, marks["mixer0_bwd"])]

    fold = lambda a: a.sum(axis=0)
    heads = lambda a: fold(a).reshape(HEADS, HEAD_DIM).sum(axis=0)
    conv_flat = lambda a: a.sum(axis=2).transpose(1, 0, 2)
    pieces = ([fold(a) for a in small["mod_l0"]] + [fold(a) for a in small["mod_l1"]] + [fold(a) for a in small["mod_kv"]]
              + [conv_flat(small["conv0"]), conv_flat(small["conv1"]), heads(small["a_norm_g"]), heads(small["k_norm_g"]),
                 heads(small["q_norm_g"]), fold(small["kv_b_f"]), fold(small["lb"]),
                 0.5 * jnp.sum(sq).reshape(1) / dm])
    w3 = 61440
    small_vec, _ = lax.optimization_barrier((_flat8(pieces, w3), served[2]))
    g3 = _sequencer_allgather8(small_vec, dev, "gather_small", 12)
    served.append(g3)
    scatter_group(3)
    rs = {}
    for k in range(3):
        rs.update(swap_group(k, halves[k], g3))
    tot = _sum_devices(g3, "sum_small").reshape(w3)
    n_mod = 14 * dm
    dmod_all = g3.reshape(N_DEV, w3)[:, :n_mod]
    o = n_mod
    conv_tot = [tot[o + l * 8 * ff: o + (l + 1) * 8 * ff].reshape(4, 2 * ff) for l in range(2)]
    o += 16 * ff
    g_a_norm, g_k_norm, g_q_norm = (tot[o + i * HEAD_DIM: o + (i + 1) * HEAD_DIM] for i in range(3))
    o += 3 * HEAD_DIM
    g_kv_b_f = tot[o:o + HEADS]
    dlb = tot[o + LANES:o + LANES + dm]
    loss = tot[o + LANES + dm]

    ct = _pad_cols(c_act16[:8].T, LANES)
    dmod_pad = jnp.pad(dmod_all, ((0, LANES - N_DEV), (0, 0)))
    cols_ada = jnp.stack([lax.dynamic_slice_in_dim(dmod_pad, l * 6 * dm + chip * 1536, 1536, axis=1) for l in range(2)])
    cols_kv = lax.dynamic_slice_in_dim(dmod_pad, 12 * dm + chip * 512, 512, axis=1)[None]
    g_ada_w = _outer_grad(ct, cols_ada, "grad_ada_w")
    g_kv_ada_w = _outer_grad(ct, cols_kv, "grad_kv_ada_w")[0]

    my_lb = lax.dynamic_slice_in_dim(lb[0], chip * 256, 256)
    l0 = lax.dynamic_slice_in_dim(dlb, chip * 256, 256) * my_lb * (1.0 - my_lb)
    grads = {
        "ada_w": g_ada_w, "ada_b": jnp.stack([tot[:6 * dm], tot[6 * dm:12 * dm]]),
        "a_lb_logits": jnp.stack([l0, -l0]), "a_norm_g": g_a_norm[None],
        "a_w_out": rs["a_w_out"][None], "kv_ada_w": g_kv_ada_w, "kv_ada_b": tot[12 * dm:14 * dm],
        "kv_w": rs["kv_w"][:, :KV_SHARD], "kv_b_f": g_kv_b_f, "k_norm_g": g_k_norm,
        "b_w_q": rs["b_w_q"][None], "q_norm_g": g_q_norm[None], "b_w_out": rs["b_w_out"][None],
        "ffn_w_up": jnp.stack([rs["up0"], rs["up1"]]),
        "ffn_conv_w": jnp.stack([lax.dynamic_slice_in_dim(ct_l[:CONV_W], chip * FFN_COLS, FFN_COLS, axis=1) for ct_l in conv_tot]),
        "ffn_conv_b": jnp.stack([ct_l[CONV_W] for ct_l in conv_tot]),
        "ffn_w_down": jnp.stack([rs["down0"], rs["down1"]]),
    }
    weights = dict(ada_w=ada_w, ada_b=ada_b, a_w_in=a_w_in, a_lb_logits=a_lb_logits, a_norm_g=a_norm_g, a_w_out=a_w_out,
                   kv_ada_w=kv_ada_w, kv_ada_b=kv_ada_b, kv_w=kv_w, kv_b_f=kv_b_f, k_norm_g=k_norm_g, b_w_q=b_w_q,
                   q_norm_g=q_norm_g, b_w_out=b_w_out, ffn_w_up=ffn_w_up, ffn_conv_w=ffn_conv_w, ffn_conv_b=ffn_conv_b,
                   ffn_w_down=ffn_w_down)
    m_in = dict(ada_w=m_ada_w, ada_b=m_ada_b, a_w_in=m_a_w_in, a_lb_logits=m_a_lb_logits, a_norm_g=m_a_norm_g,
                a_w_out=m_a_w_out, kv_ada_w=m_kv_ada_w, kv_ada_b=m_kv_ada_b, kv_w=m_kv_w, kv_b_f=m_kv_b_f,
                k_norm_g=m_k_norm_g, b_w_q=m_b_w_q, q_norm_g=m_q_norm_g, b_w_out=m_b_w_out, ffn_w_up=m_ffn_w_up,
                ffn_conv_w=m_ffn_conv_w, ffn_conv_b=m_ffn_conv_b, ffn_w_down=m_ffn_w_down)
    v_in = dict(ada_w=v_ada_w, ada_b=v_ada_b, a_w_in=v_a_w_in, a_lb_logits=v_a_lb_logits, a_norm_g=v_a_norm_g,
                a_w_out=v_a_w_out, kv_ada_w=v_kv_ada_w, kv_ada_b=v_kv_ada_b, kv_w=v_kv_w, kv_b_f=v_kv_b_f,
                k_norm_g=v_k_norm_g, b_w_q=v_b_w_q, q_norm_g=v_q_norm_g, b_w_out=v_b_w_out, ffn_w_up=v_ffn_w_up,
                ffn_conv_w=v_ffn_conv_w, ffn_conv_b=v_ffn_conv_b, ffn_w_down=v_ffn_w_down)

    names = list(weights)
    step = lambda n: _adamw(weights[n], grads[n], m_in[n], v_in[n], "adamw_" + n)
    grads = {n: g.reshape(weights[n].shape) for n, g in grads.items()}
    upd = {n: step(n) for n in names if n != "a_w_in"}
    last = sum_group(3, [u[0] for u in upd.values()])
    grads["a_w_in"] = swap_group(3, last, last)["a_w_in"][None]
    upd["a_w_in"] = step("a_w_in")
    return (loss, grad_x[None], *[grads[n] for n in names], *[upd[n][0] for n in names],
            *[upd[n][1] for n in names], *[upd[n][2] for n in names])
```

```python
import jax
import jax.numpy as jnp
from jax import lax
from jax.experimental import pallas as pl
from jax.experimental.pallas import tpu as pltpu
from jax.experimental.pallas import tpu_sc as plsc

F32 = jnp.float32
BF16 = jnp.bfloat16

D_MODEL = 1024
HEADS = 8
HEAD_DIM = 128
A_CHUNK = 64
D_FF = 2816
CONV_W = 3
EPS = 1e-6
NEG_INF = -1e30
N_CHIPS = 4
N_DEV = 8

ADAM_LR = 0.001
ADAM_B1 = 0.9
ADAM_B2 = 0.999
ADAM_EPS = 1e-08
ADAM_WD = 0.01
ADAM_STEP = 10

SUBLANES = 8
BF16_ROWS = 16
LANES = 128
HALO = BF16_ROWS
ROW_TILE = 512
TOKEN_TILE_TN = 2048
FFN_COLS = 1408
FFN_ROWS = 256
HGRN_ROWS = 512
ATT_TILE = 512
ATT_SPLIT = 2
ATT_FWD_HEADS = 8
ATT_BWD_HEADS = 8
MESH = pl.DeviceIdType.MESH


def _sig(x):
    return jax.nn.sigmoid(x)


def _dot(a, b):
    return jnp.dot(a, b, preferred_element_type=F32)


def _dot_nt(a, b):
    return lax.dot_general(a, b, (((1,), (1,)), ((), ())), preferred_element_type=F32)


def _dot_tn(a, b):
    return lax.dot_general(a, b, (((0,), (0,)), ((), ())), preferred_element_type=F32)


def _split2(x):
    hi = x.astype(BF16)
    lo = (x - hi.astype(F32)).astype(BF16)
    return hi, lo


def _dot_f32(a, b):
    ah, al = _split2(a)
    bh, bl = _split2(b)
    return _dot(ah, bh) + _dot(ah, bl) + _dot(al, bh)


def _tri_dot(tri, x):
    hi = x.astype(BF16)
    r = x - hi.astype(F32)
    mid = r.astype(BF16)
    lo = (r - mid.astype(F32)).astype(BF16)
    return _dot(tri, hi) + _dot(tri, mid) + _dot(tri, lo)


def _tri(n, upper=False):
    r = lax.broadcasted_iota(jnp.int32, (n, n), 0)
    c = lax.broadcasted_iota(jnp.int32, (n, n), 1)
    keep = (c >= r) if upper else (c <= r)
    return jnp.where(keep, 1.0, 0.0).astype(BF16)


def _colsum8(v):
    rows, n = v.shape
    return v.reshape(rows // SUBLANES, SUBLANES, n).sum(axis=0)


def _full(shape):
    nd = len(shape)
    return pl.BlockSpec(shape, lambda *_: (0,) * nd)


def _tile(n, want):
    t = min(n, want)
    assert n % t == 0, (n, t)
    return t


def _mm_tn(a, d, p_n, name):
    m_rows, k = a.shape
    g_n, _, w_cols = d.shape
    per = p_n // g_n
    n = w_cols // per
    tm = _tile(m_rows, TOKEN_TILE_TN if k <= D_MODEL else ROW_TILE)
    steps = m_rows // tm

    def body(a_ref, d_ref, o_ref, acc):
        m = pl.program_id(1)

        @pl.when(m == 0)
        def _():
            acc[...] = jnp.zeros_like(acc)

        acc[...] += _dot_tn(a_ref[...], d_ref[...])

        @pl.when(m == steps - 1)
        def _():
            o_ref[...] = acc[...].astype(BF16)

    return pl.pallas_call(
        body, name=name, grid=(p_n, steps),
        in_specs=[pl.BlockSpec((tm, k), lambda p, m: (m, 0)),
                  pl.BlockSpec((None, tm, n), lambda p, m: (p // per, m, p % per))],
        out_specs=pl.BlockSpec((None, k, n), lambda p, m: (p, 0, 0)),
        out_shape=jax.ShapeDtypeStruct((p_n, k, n), BF16),
        scratch_shapes=[pltpu.VMEM((k, n), F32)],
    )(a, d)


def _premix_proj(x, shift, scale, w, name):
    s, dm = x.shape
    p_n, _, n = w.shape
    tm = _tile(s, ROW_TILE)

    def body(x_ref, sh_ref, sc_ref, w_ref, h_ref, o_ref):
        xv = x_ref[...]
        inv = lax.rsqrt(jnp.mean(xv * xv, axis=-1, keepdims=True) + EPS)
        h = (xv * inv * (1.0 + sc_ref[...]) + sh_ref[...]).astype(BF16)
        h_ref[...] = h
        for p in range(p_n):
            o_ref[:, p * n:(p + 1) * n] = _dot(h, w_ref[p])

    row = pl.BlockSpec((tm, dm), lambda i: (i, 0))
    vec = _full((1, dm))
    return pl.pallas_call(
        body, name=name, grid=(s // tm,), in_specs=[row, vec, vec, _full(w.shape)],
        out_specs=[row, pl.BlockSpec((tm, p_n * n), lambda i: (i, 0))],
        out_shape=[jax.ShapeDtypeStruct((s, dm), BF16), jax.ShapeDtypeStruct((s, p_n * n), F32)],
    )(x, shift, scale, w)


def _premix_bwd(x, terms, dres, name, branch=None):
    s, dm = x.shape
    tm = _tile(s, ROW_TILE)
    pairs = [pr for _, prs in terms for pr in prs]
    n_in = 2 + len(terms) + 2 * len(pairs) + (2 if branch else 0)

    def body(*refs):
        x_ref, dres_ref = refs[:2]
        sc_refs = refs[2:2 + len(terms)]
        mm_refs = refs[2 + len(terms):2 + len(terms) + 2 * len(pairs)]
        outs = refs[n_in:]

        @pl.when(pl.program_id(0) == 0)
        def _():
            for o in outs[1:1 + 2 * len(terms)]:
                o[...] = jnp.zeros_like(o)
            if branch:
                outs[-1][...] = jnp.zeros_like(outs[-1])

        xv = x_ref[...]
        inv = lax.rsqrt(jnp.mean(xv * xv, axis=-1, keepdims=True) + EPS)
        r = xv * inv
        dx = dres_ref[...]
        k = 0
        for t, (_, prs) in enumerate(terms):
            dh = None
            for d, w in prs:
                d_ref, w_ref = mm_refs[2 * k], mm_refs[2 * k + 1]
                k += 1
                p_n, _, n = w.shape
                per = p_n // d.shape[0]
                for p in range(p_n):
                    part = _dot_nt(d_ref[p // per, :, (p % per) * n:(p % per + 1) * n], w_ref[p])
                    dh = part if dh is None else dh + part
            dr = dh * (1.0 + sc_refs[t][...])
            dx = dx + inv * (dr - r * jnp.mean(dr * r, axis=-1, keepdims=True))
            outs[1 + 2 * t][...] += _colsum8(dh)
            outs[2 + 2 * t][...] += _colsum8(dh * r)
        outs[0][...] = dx
        if branch:
            y_ref, g_ref = refs[n_in - 2:n_in]
            outs[-2][0] = (dx * g_ref[...]).astype(BF16)
            outs[-1][...] += _colsum8(dx * y_ref[...])

    row = pl.BlockSpec((tm, dm), lambda i: (i, 0))
    vec, acc = _full((1, dm)), _full((SUBLANES, dm))
    ins, specs = [x, dres] + [sc for sc, _ in terms], [row, row] + [vec] * len(terms)
    for d, w in pairs:
        ins += [d, w]
        specs += [pl.BlockSpec((d.shape[0], tm, d.shape[2]), lambda i: (0, i, 0)), _full(w.shape)]
    out_shape = [jax.ShapeDtypeStruct((s, dm), F32)] + [jax.ShapeDtypeStruct((SUBLANES, dm), F32)] * (2 * len(terms))
    out_specs = [row] + [acc] * (2 * len(terms))
    if branch:
        ins += list(branch)
        specs += [row, vec]
        out_shape += [jax.ShapeDtypeStruct((1, s, dm), BF16), jax.ShapeDtypeStruct((SUBLANES, dm), F32)]
        out_specs += [pl.BlockSpec((1, tm, dm), lambda i: (0, i, 0)), acc]
    outs = pl.pallas_call(body, name=name, grid=(s // tm,), in_specs=specs, out_specs=out_specs,
                          out_shape=out_shape)(*ins)
    partials = [(outs[1 + 2 * t], outs[2 + 2 * t]) for t in range(len(terms))]
    return (outs[0], partials) + ((outs[-2], outs[-1]) if branch else ())


def _conv_taps(e, w, b):
    return w[2:3] * e + w[1:2] * pltpu.roll(e, 1, 0) + w[0:1] * pltpu.roll(e, 2, 0) + b


def _ffn_specs(s, tm, cb):
    hb = tm // HALO
    last = s // HALO - 1
    main = pl.BlockSpec((2, tm, cb), lambda j, i: (0, i, j))
    prev = pl.BlockSpec((2, HALO, cb), lambda j, i: (0, jnp.maximum(i * hb - 1, 0), j))
    nxt = pl.BlockSpec((2, HALO, cb), lambda j, i: (0, jnp.minimum((i + 1) * hb, last), j))
    wspec = pl.BlockSpec((2, CONV_W, cb), lambda j, i: (0, 0, j))
    bspec = pl.BlockSpec((2, 1, cb), lambda j, i: (0, 0, j))
    return main, prev, nxt, wspec, bspec


def _convglu_bwd(u, c, dffn, w_down, w, name):
    _, s, f = u.shape
    dm = dffn.shape[2]
    tm = _tile(s, 256)
    cb = _tile(f, FFN_COLS)
    steps = s // tm
    n_ext = tm + HALO
    main, _, nxt, wspec, _ = _ffn_specs(s, tm, cb)
    hb = tm // HALO
    last = s // HALO - 1
    d_main = pl.BlockSpec((None, tm, dm), lambda j, i: (0, i, 0))
    d_next = pl.BlockSpec((None, HALO, dm), lambda j, i: (0, jnp.minimum((i + 1) * hb, last), 0))
    wd_spec = pl.BlockSpec((None, cb, dm), lambda j, i: (0, j, 0))

    def body(u_ref, c_ref, cn_ref, d_ref, dn_ref, wd_ref, w_ref, du_ref, acc_ref):
        i = pl.program_id(1)
        notlast = jnp.where(i < steps - 1, 1.0, 0.0)

        @pl.when(i == 0)
        def _():
            acc_ref[...] = jnp.zeros_like(acc_ref)

        gate, val = (jnp.concatenate([c_ref[g].astype(F32), cn_ref[g].astype(F32)], axis=0) for g in range(2))
        wd = wd_ref[...]
        da = jnp.concatenate([_dot_nt(d_ref[...], wd).astype(BF16).astype(F32),
                              _dot_nt(dn_ref[...], wd).astype(BF16).astype(F32) * notlast], axis=0)
        sg = _sig(gate)
        d_val = da * gate * sg
        d_gate = da * val * (sg * (1.0 + gate * (1.0 - sg)))

        def finish(g, d):
            wv = w_ref[g]
            d1, d2 = pltpu.roll(d, n_ext - 1, 0), pltpu.roll(d, n_ext - 2, 0)
            du_ref[g] = (wv[2:3] * d + wv[1:2] * d1 + wv[0:1] * d2)[0:tm].astype(BF16)
            uv = u_ref[g].astype(F32)
            acc_ref[g, 2] += _colsum8(d[0:tm] * uv)
            acc_ref[g, 1] += _colsum8(d1[0:tm] * uv)
            acc_ref[g, 0] += _colsum8(d2[0:tm] * uv)
            acc_ref[g, 3] += _colsum8(d[0:tm])

        finish(0, d_gate)
        finish(1, d_val)

    return pl.pallas_call(
        body, name=name, grid=(f // cb, steps),
        in_specs=[main, main, nxt, d_main, d_next, wd_spec, wspec],
        out_specs=[main, pl.BlockSpec((2, 4, SUBLANES, cb), lambda j, i: (0, 0, 0, j))],
        out_shape=[jax.ShapeDtypeStruct((2, s, f), BF16), jax.ShapeDtypeStruct((2, 4, SUBLANES, f), F32)],
    )(u, c, c, dffn, dffn, w_down, w)


def _hgrn_gates(q_raw, f_raw, lb, tri):
    sf = _sig(f_raw)
    fg = lb + (1.0 - lb) * sf
    b = _tri_dot(tri, jnp.log(fg))
    return q_raw * _sig(q_raw), 1.0 - fg, b, fg, sf


def _hgrn_fwd(proj, lb, norm_g, name):
    s = proj.shape[0]
    tb = _tile(s, HGRN_ROWS)
    n_c = tb // A_CHUNK
    half = A_CHUNK // 2

    def body(q_ref, f_ref, v_ref, g_ref, lb_ref, ng_ref, o_ref, yp_ref, st_ref, state):
        @pl.when(pl.program_id(0) == 0)
        def _():
            state[...] = jnp.zeros_like(state)

        tri = _tri(A_CHUNK)
        causal = lax.broadcasted_iota(jnp.int32, (A_CHUNK, A_CHUNK), 1) <= lax.broadcasted_iota(
            jnp.int32, (A_CHUNK, A_CHUNK), 0)

        def chunk(ci, carry):
            rows = pl.ds(ci * A_CHUNK, A_CHUNK)
            heads = [slice(h * HEAD_DIM, (h + 1) * HEAD_DIM) for h in range(HEADS)]
            qs, k, b, _, _ = _hgrn_gates(q_ref[rows, :], f_ref[rows, :], lb_ref[...], tri)
            b_mid, b_last = b[half:half + 1], b[A_CHUNK - 1:A_CHUNK]
            q_i = (qs * jnp.exp(b - b_mid)).astype(BF16)
            k_i = (k * jnp.exp(b_mid - b)).astype(BF16)
            q_e = (qs * jnp.exp(b)).astype(BF16)
            k_s = (k * jnp.exp(b_last - b)).astype(BF16)
            decay = jnp.exp(b_last)
            vb = v_ref[rows, :].astype(BF16)
            scores = [jnp.where(causal, _dot_nt(q_i[:, cs], k_i[:, cs]), 0.0).astype(BF16) for cs in heads]
            st = [state[h] for h in range(HEADS)]
            outs = [_dot(scores[h], vb[:, cs]) + _dot_nt(q_e[:, cs], st[h].astype(BF16)) for h, cs in enumerate(heads)]
            for h, cs in enumerate(heads):
                st_ref[ci, h] = st[h]
                state[h] = st[h] * decay[:, cs] + _dot_tn(vb[:, cs], k_s[:, cs])
            o = jnp.concatenate(outs, axis=1)
            o_ref[rows, :] = o
            sq = o * o
            inv = jnp.concatenate([jnp.broadcast_to(lax.rsqrt(jnp.mean(sq[:, cs], axis=-1, keepdims=True) + EPS),
                                                    (A_CHUNK, HEAD_DIM)) for cs in heads], axis=1)
            g_raw = g_ref[rows, :]
            yp_ref[rows, :] = (o * inv * ng_ref[...] * (g_raw * _sig(g_raw))).astype(BF16)
            return carry

        for step in range(n_c):
            chunk(step, 0)

    col = lambda j: pl.BlockSpec((tb, D_MODEL), lambda i: (i, j))
    vec = _full((1, D_MODEL))
    return pl.pallas_call(
        body, name=name, grid=(s // tb,), in_specs=[col(0), col(1), col(2), col(3), vec, vec],
        out_specs=[col(0), col(0), pl.BlockSpec((n_c, HEADS, HEAD_DIM, HEAD_DIM), lambda i: (i, 0, 0, 0))],
        out_shape=[jax.ShapeDtypeStruct((s, D_MODEL), F32), jax.ShapeDtypeStruct((s, D_MODEL), BF16),
                   jax.ShapeDtypeStruct((s // A_CHUNK, HEADS, HEAD_DIM, HEAD_DIM), F32)],
        scratch_shapes=[pltpu.VMEM((HEADS, HEAD_DIM, HEAD_DIM), F32)],
    )(proj, proj, proj, proj, lb, norm_g)


def _hgrn_bwd(proj, lb, norm_g, o, states, dout, w_out, name):
    s = proj.shape[0]
    tb = _tile(s, HGRN_ROWS)
    n_c = tb // A_CHUNK
    n_b = s // tb
    half = A_CHUNK // 2

    def body(q_ref, f_ref, v_ref, g_ref, lb_ref, ng_ref, o_ref, st_ref, dout_ref, w_ref, dp_ref, dlb_ref, dng_ref,
             dstate, dyp_ref):
        @pl.when(pl.program_id(0) == 0)
        def _():
            dstate[...] = jnp.zeros_like(dstate)
            dlb_ref[...] = jnp.zeros_like(dlb_ref)
            dng_ref[...] = jnp.zeros_like(dng_ref)

        dyp_ref[...] = _dot_nt(dout_ref[0], w_ref[0])

        tri = _tri(A_CHUNK)
        tri_up = _tri(A_CHUNK, upper=True)
        row_id = lax.broadcasted_iota(jnp.int32, (A_CHUNK, D_MODEL), 0)
        causal = lax.broadcasted_iota(jnp.int32, (A_CHUNK, A_CHUNK), 1) <= lax.broadcasted_iota(
            jnp.int32, (A_CHUNK, A_CHUNK), 0)

        def chunk(cj, carry):
            ci = n_c - 1 - cj
            rows = pl.ds(ci * A_CHUNK, A_CHUNK)
            heads = [slice(h * HEAD_DIM, (h + 1) * HEAD_DIM) for h in range(HEADS)]
            cat = lambda parts: jnp.concatenate(parts, axis=1)
            per_head_mean = lambda a: cat([jnp.broadcast_to(jnp.mean(a[:, cs], axis=-1, keepdims=True),
                                                            (A_CHUNK, HEAD_DIM)) for cs in heads])
            q_raw, lbv = q_ref[rows, :], lb_ref[...]
            qs, k, b, fg, sf = _hgrn_gates(q_raw, f_ref[rows, :], lbv, tri)
            b_mid, b_last = b[half:half + 1], b[A_CHUNK - 1:A_CHUNK]
            e_qi, e_ki, e_q, e_ks = jnp.exp(b - b_mid), jnp.exp(b_mid - b), jnp.exp(b), jnp.exp(b_last - b)
            decay = jnp.exp(b_last)
            q_i, k_i, q_e, k_s = qs * e_qi, k * e_ki, qs * e_q, k * e_ks
            qib, kib, qeb, ksb = q_i.astype(BF16), k_i.astype(BF16), q_e.astype(BF16), k_s.astype(BF16)
            vb = v_ref[rows, :].astype(BF16)
            ov, g_raw, dy, ng = o_ref[rows, :], g_ref[rows, :], dyp_ref[rows, :], ng_ref[...]
            inv = lax.rsqrt(per_head_mean(ov * ov) + EPS)
            nrm = ov * inv
            sg = _sig(g_raw)
            gs = g_raw * sg
            dn = dy * ng * gs
            dng_ref[0:1, :] += jnp.sum(dy * nrm * gs, axis=0, keepdims=True)
            dg_raw = dy * nrm * ng * (sg * (1.0 + g_raw * (1.0 - sg)))
            do = (inv * (dn - nrm * per_head_mean(dn * nrm))).astype(BF16)
            st_prev = [st_ref[ci, h] for h in range(HEADS)]
            dst = [dstate[h] for h in range(HEADS)]
            dstb = [d.astype(BF16) for d in dst]
            scores = [jnp.where(causal, _dot_nt(qib[:, cs], kib[:, cs]), 0.0).astype(BF16) for cs in heads]
            d_scores = [jnp.where(causal, _dot_nt(do[:, cs], vb[:, cs]), 0.0).astype(BF16) for cs in heads]
            dv = cat([_dot_tn(scores[h], do[:, cs]) + _dot_nt(ksb[:, cs], dstb[h]) for h, cs in enumerate(heads)])
            dq_i = cat([_dot(d_scores[h], kib[:, cs]) for h, cs in enumerate(heads)])
            dk_i = cat([_dot_tn(d_scores[h], qib[:, cs]) for h, cs in enumerate(heads)])
            dq_e = cat([_dot(do[:, cs], st_prev[h].astype(BF16)) for h, cs in enumerate(heads)])
            dk_s = cat([_dot(vb[:, cs], dstb[h]) for h, cs in enumerate(heads)])
            d_decay = cat([jnp.sum(st_prev[h] * dst[h], axis=0, keepdims=True) for h in range(HEADS)])
            for h, cs in enumerate(heads):
                dstate[h] = dst[h] * decay[:, cs] + _dot_tn(do[:, cs], qeb[:, cs])
            dq = dq_i * e_qi + dq_e * e_q
            dk = dk_i * e_ki + dk_s * e_ks
            t_qi, t_ki, t_ks = dq_i * q_i, dk_i * k_i, dk_s * k_s
            db = t_qi - t_ki + dq_e * q_e - t_ks
            db_mid = jnp.sum(t_ki - t_qi, axis=0, keepdims=True)
            db_last = jnp.sum(t_ks, axis=0, keepdims=True) + d_decay * decay
            db = db + jnp.where(row_id == half, db_mid, 0.0) + jnp.where(row_id == A_CHUNK - 1, db_last, 0.0)
            dfg = _tri_dot(tri_up, db) / fg - dk
            dlb_ref[0:1, :] += jnp.sum(dfg * (1.0 - sf), axis=0, keepdims=True)
            sq = _sig(q_raw)
            dp_ref[0, rows, :] = (dq * (sq * (1.0 + q_raw * (1.0 - sq)))).astype(BF16)
            dp_ref[1, rows, :] = (dfg * (1.0 - lbv) * sf * (1.0 - sf)).astype(BF16)
            dp_ref[2, rows, :] = dv.astype(BF16)
            dp_ref[3, rows, :] = dg_raw.astype(BF16)
            return carry

        for step in range(n_c):
            chunk(step, 0)

    col = lambda j: pl.BlockSpec((tb, D_MODEL), lambda i: (n_b - 1 - i, j))
    vec = _full((1, D_MODEL))
    acc = _full((SUBLANES, D_MODEL))
    return pl.pallas_call(
        body, name=name, grid=(n_b,),
        in_specs=[col(0), col(1), col(2), col(3), vec, vec, col(0),
                  pl.BlockSpec((n_c, HEADS, HEAD_DIM, HEAD_DIM), lambda i: (n_b - 1 - i, 0, 0, 0)),
                  pl.BlockSpec((1, tb, D_MODEL), lambda i: (0, n_b - 1 - i, 0)), _full(w_out.shape)],
        out_specs=[pl.BlockSpec((4, tb, D_MODEL), lambda i: (0, n_b - 1 - i, 0)), acc, acc],
        out_shape=[jax.ShapeDtypeStruct((4, s, D_MODEL), BF16), jax.ShapeDtypeStruct((SUBLANES, D_MODEL), F32),
                   jax.ShapeDtypeStruct((SUBLANES, D_MODEL), F32)],
        scratch_shapes=[pltpu.VMEM((HEADS, HEAD_DIM, HEAD_DIM), F32), pltpu.VMEM((tb, D_MODEL), F32)],
    )(proj, proj, proj, proj, lb, norm_g, o, states, dout, w_out)


def _head_rms(raw_ref, g_ref, mult, y_ref):
    for h in range(HEADS):
        cs = slice(h * HEAD_DIM, (h + 1) * HEAD_DIM)
        xv = raw_ref[:, cs]
        inv = lax.rsqrt(jnp.mean(xv * xv, axis=-1, keepdims=True) + EPS)
        y_ref[:, cs] = (xv * inv * g_ref[:, cs] * mult).astype(BF16)


def _proj_headnorm(a, w, g, mult, name):
    s, k = a.shape
    p_n, _, n = w.shape
    tm = _tile(s, ROW_TILE)

    def body(a_ref, w_ref, g_ref, raw_ref, y_ref):
        av = a_ref[...]
        for p in range(p_n):
            raw_ref[:, p * n:(p + 1) * n] = _dot(av, w_ref[p])
        _head_rms(raw_ref, g_ref, mult, y_ref)

    row = lambda wid: pl.BlockSpec((tm, wid), lambda i: (i, 0))
    return pl.pallas_call(
        body, name=name, grid=(s // tm,), in_specs=[row(k), _full(w.shape), _full((1, D_MODEL))],
        out_specs=[row(p_n * n), row(D_MODEL)],
        out_shape=[jax.ShapeDtypeStruct((s, p_n * n), F32), jax.ShapeDtypeStruct((s, D_MODEL), BF16)],
    )(a, w, g)


def _kv_proj(hk, w_k, w_v, w_f, g, name):
    s, k = hk.shape
    tm = _tile(s, ROW_TILE)

    def body(h_ref, wk_ref, wv_ref, wf_ref, g_ref, kr_ref, k_ref, v_ref, f_ref):
        hv = h_ref[...]
        kr_ref[...] = _dot(hv, wk_ref[0])
        v_ref[...] = _dot(hv, wv_ref[0]).astype(BF16)
        f_ref[...] = _dot(hv, wf_ref[0])
        _head_rms(kr_ref, g_ref, 1.0, k_ref)

    row = lambda wid: pl.BlockSpec((tm, wid), lambda i: (i, 0))
    return pl.pallas_call(
        body, name=name, grid=(s // tm,),
        in_specs=[row(k), _full(w_k.shape), _full(w_v.shape), _full(w_f.shape), _full((1, D_MODEL))],
        out_specs=[row(D_MODEL), row(D_MODEL), row(D_MODEL), row(LANES)],
        out_shape=[jax.ShapeDtypeStruct((s, D_MODEL), F32), jax.ShapeDtypeStruct((s, D_MODEL), BF16),
                   jax.ShapeDtypeStruct((s, D_MODEL), BF16), jax.ShapeDtypeStruct((s, LANES), F32)],
    )(hk, w_k, w_v, w_f, g)


def _headnorm_bwd(x, g, mult, dy, name, col0=0, extra=None):
    s = x.shape[0]
    tm = _tile(s, ROW_TILE)
    groups = 2 if extra is not None else 1
    head_major = dy.ndim == 3

    def body(*refs):
        x_ref, g_ref, dy_ref = refs[:3]
        dx_ref, dg_ref = refs[-2:]

        @pl.when(pl.program_id(0) == 0)
        def _():
            dg_ref[...] = jnp.zeros_like(dg_ref)

        for h in range(HEADS):
            cs = slice(h * HEAD_DIM, (h + 1) * HEAD_DIM)
            xv, gv = x_ref[:, cs], g_ref[:, cs]
            dyv = dy_ref[h, :, 0:HEAD_DIM] if head_major else dy_ref[:, cs]
            inv = lax.rsqrt(jnp.mean(xv * xv, axis=-1, keepdims=True) + EPS)
            nrm = xv * inv
            dn = dyv * gv * mult
            dg_ref[:, cs] += _colsum8(dyv * nrm * mult)
            dx_ref[0, :, cs] = (inv * (dn - nrm * jnp.mean(dn * nrm, axis=-1, keepdims=True))).astype(BF16)
        if extra is not None:
            dx_ref[1] = refs[3][...]

    row = pl.BlockSpec((tm, D_MODEL), lambda i: (i, 0))
    dy_spec = pl.BlockSpec((HEADS, tm, dy.shape[-1]), lambda i: (0, i, 0)) if head_major else row
    ins = [x, g, dy] + ([extra] if extra is not None else [])
    specs = ([pl.BlockSpec((tm, D_MODEL), lambda i: (i, col0)), _full((1, D_MODEL)), dy_spec]
             + ([row] if extra is not None else []))
    return pl.pallas_call(
        body, name=name, grid=(s // tm,), in_specs=specs,
        out_specs=[pl.BlockSpec((groups, tm, D_MODEL), lambda i: (0, i, 0)), _full((SUBLANES, D_MODEL))],
        out_shape=[jax.ShapeDtypeStruct((groups, s, D_MODEL), BF16), jax.ShapeDtypeStruct((SUBLANES, D_MODEL), F32)],
    )(*ins)


def _log_sigmoid(z):
    return jnp.minimum(z, 0.0) - jnp.log(1.0 + jnp.exp(-jnp.abs(z)))


Q_CUM, Q_ONE, Q_LSE = 0, 3, 6
LOG2E = 1.4426950408889634


def _pieces(v):
    hi = v.astype(BF16).astype(F32)
    mid = (v - hi).astype(BF16).astype(F32)
    lo = ((v - hi) - mid).astype(BF16).astype(F32)
    return hi, mid, lo


def _side(lane, at, v):
    hi, mid, lo = _pieces(v)
    return jnp.where(lane == at, hi, jnp.where(lane == at + 1, mid, jnp.where(lane == at + 2, lo, 0.0)))


def _fcum_fwd(f, bias, name):
    s = f.shape[0]
    tm = _tile(s, ROW_TILE)

    def body(f_ref, b_ref, qa_ref, ka_ref, carry):
        @pl.when(pl.program_id(0) == 0)
        def _():
            carry[...] = jnp.zeros_like(carry)

        cum = _tri_dot(_tri(tm), _log_sigmoid(f_ref[...] + b_ref[...])) + carry[...]
        carry[...] = cum[tm - 1:tm]
        lane = lax.broadcasted_iota(jnp.int32, (tm, LANES), 1)
        ones_q = jnp.where((lane >= Q_ONE) & (lane < Q_LSE), 1.0, 0.0)
        ones_k = jnp.where((lane < Q_ONE) | ((lane >= Q_LSE) & (lane < Q_LSE + 3)), 1.0, 0.0)
        for h in range(HEADS):
            c2 = cum[:, h:h + 1] * LOG2E
            qa_ref[h] = (_side(lane, Q_CUM, c2) + ones_q).astype(BF16)
            ka_ref[h] = (_side(lane, Q_ONE, -c2) + ones_k).astype(BF16)

    side = pl.BlockSpec((HEADS, tm, LANES), lambda i: (0, i, 0))
    return pl.pallas_call(
        body, name=name, grid=(s // tm,),
        in_specs=[pl.BlockSpec((tm, LANES), lambda i: (i, 0)), _full((1, LANES))],
        out_specs=[side, side],
        out_shape=[jax.ShapeDtypeStruct((HEADS, s, LANES), BF16)] * 2,
        scratch_shapes=[pltpu.VMEM((1, LANES), F32)],
    )(f, bias)


def _fcum_bwd(f, bias, dka, dcq, name):
    s = f.shape[0]
    tm = _tile(s, ROW_TILE)
    n_b = s // tm

    def body(f_ref, b_ref, dka_ref, dcq_ref, dz_ref, db_ref, carry):
        @pl.when(pl.program_id(0) == 0)
        def _():
            carry[...] = jnp.zeros_like(carry)
            db_ref[...] = jnp.zeros_like(db_ref)

        lane = lax.broadcasted_iota(jnp.int32, (tm, LANES), 1)
        rows = jnp.concatenate([dcq_ref[h] for h in range(HEADS)] + [jnp.zeros((LANES - HEADS, tm), F32)], axis=0)
        dcum = rows.T
        for h in range(HEADS):
            dcum = dcum - jnp.where(lane == h, dka_ref[h, :, Q_ONE:Q_ONE + 1], 0.0)
        dlf = _tri_dot(_tri(tm, upper=True), dcum) + carry[...]
        carry[...] = dlf[0:1]
        dz = dlf * _sig(-(f_ref[...] + b_ref[...]))
        dz_ref[0] = dz.astype(BF16)
        db_ref[...] += _colsum8(dz)

    return pl.pallas_call(
        body, name=name, grid=(n_b,),
        in_specs=[pl.BlockSpec((tm, LANES), lambda i: (n_b - 1 - i, 0)), _full((1, LANES)),
                  pl.BlockSpec((HEADS, tm, LANES), lambda i: (0, n_b - 1 - i, 0)),
                  pl.BlockSpec((HEADS, 1, tm), lambda i: (0, 0, n_b - 1 - i))],
        out_specs=[pl.BlockSpec((1, tm, LANES), lambda i: (0, n_b - 1 - i, 0)), _full((SUBLANES, LANES))],
        out_shape=[jax.ShapeDtypeStruct((1, s, LANES), BF16), jax.ShapeDtypeStruct((SUBLANES, LANES), F32)],
        scratch_shapes=[pltpu.VMEM((1, LANES), F32)],
    )(f, bias, dka, dcq)


def _causal_pairs(n_t, key_major):
    if key_major:
        pairs = [(qi, ki) for ki in range(n_t) for qi in range(ki, n_t)]
    else:
        pairs = [(qi, ki) for qi in range(n_t) for ki in range(qi + 1)]
    return (jnp.array([p[0] for p in pairs], jnp.int32), jnp.array([p[1] for p in pairs], jnp.int32))


def _lane_const(t, lo, hi, value):
    lane = lax.broadcasted_iota(jnp.int32, (t, LANES), 1)
    return jnp.where((lane >= lo) & (lane < hi), value, 0.0).astype(BF16)


def _att_specs(t, nh):
    qmain = pl.BlockSpec((t, nh * HEAD_DIM), lambda h, p, qt, kt: (qt[p], h))
    kmain = pl.BlockSpec((t, nh * HEAD_DIM), lambda h, p, qt, kt: (kt[p], h))
    qside = pl.BlockSpec((nh, t, LANES), lambda h, p, qt, kt: (h, qt[p], 0))
    kside = pl.BlockSpec((nh, t, LANES), lambda h, p, qt, kt: (h, kt[p], 0))
    return qmain, kmain, qside, kside


def _fox_fwd(q, qa, k, ka, v, qo, name):
    s = q.shape[0]
    t = _tile(s, ATT_TILE)
    sub = t // ATT_SPLIT
    nh = ATT_FWD_HEADS
    qt, kt = _causal_pairs(s // t, key_major=False)

    def body(qt_ref, kt_ref, q_ref, qa_ref, k_ref, ka_ref, v_ref, og_ref, o_ref, y_ref, qab_ref, m_s, l_s, acc_s):
        pid = pl.program_id(1)
        qi, ki = qt_ref[pid], kt_ref[pid]

        @pl.when(ki == 0)
        def _():
            m_s[...] = jnp.full_like(m_s, NEG_INF)
            l_s[...] = jnp.zeros_like(l_s)
            acc_s[...] = jnp.zeros_like(acc_s)

        def step(diagonal):
            for hh in range(nh):
                hc = slice(hh * HEAD_DIM, (hh + 1) * HEAD_DIM)
                kc = jnp.concatenate([k_ref[:, hc], ka_ref[hh]], axis=1)
                vc = jnp.concatenate([v_ref[:, hc], _lane_const(t, 0, 1, 1.0)], axis=1)
                for r in range(ATT_SPLIT):
                    rows = slice(r * sub, (r + 1) * sub)
                    n_k = (r + 1) * sub if diagonal else t
                    sc = _dot_nt(jnp.concatenate([q_ref[rows, hc], qa_ref[hh, rows]], axis=1), kc[:n_k])
                    if diagonal:
                        sc = jnp.where(lax.broadcasted_iota(jnp.int32, (sub, n_k), 1)
                                       <= lax.broadcasted_iota(jnp.int32, (sub, n_k), 0) + r * sub, sc, NEG_INF)
                    m_old = m_s[hh, rows]
                    m_new = jnp.maximum(m_old, jnp.max(sc, axis=-1, keepdims=True))
                    alpha = jnp.exp2(m_old - m_new)
                    pv = _dot(jnp.exp2(sc - m_new[:, 0:1]).astype(BF16), vc[:n_k])
                    acc_s[hh, rows] = alpha * acc_s[hh, rows] + pv[:, :HEAD_DIM]
                    l_s[hh, rows] = alpha * l_s[hh, rows] + pv[:, HEAD_DIM:]
                    m_s[hh, rows] = m_new

        @pl.when(ki < qi)
        def _():
            step(False)

        @pl.when(ki == qi)
        def _():
            step(True)
            lane = lax.broadcasted_iota(jnp.int32, (t, LANES), 1)
            for hh in range(nh):
                hc = slice(hh * HEAD_DIM, (hh + 1) * HEAD_DIM)
                l = l_s[hh, :, 0:1]
                o = acc_s[hh] / l
                o_ref[:, hc] = o
                y_ref[:, hc] = (o * _sig(og_ref[:, hc])).astype(BF16)
                qab_ref[hh] = qa_ref[hh] + _side(lane, Q_LSE, -(m_s[hh, :, 0:1] + jnp.log2(l))).astype(BF16)

    qmain, kmain, qside, kside = _att_specs(t, nh)
    return pl.pallas_call(
        body, name=name,
        grid_spec=pltpu.PrefetchScalarGridSpec(
            num_scalar_prefetch=2, grid=(HEADS // nh, qt.shape[0]),
            in_specs=[qmain, qside, kmain, kside, kmain,
                      pl.BlockSpec((t, nh * HEAD_DIM), lambda h, p, qt, kt: (qt[p], HEADS // nh + h))],
            out_specs=[qmain, qmain, qside],
            scratch_shapes=[pltpu.VMEM((nh, t, LANES), F32), pltpu.VMEM((nh, t, LANES), F32),
                            pltpu.VMEM((nh, t, HEAD_DIM), F32)]),
        out_shape=[jax.ShapeDtypeStruct((s, D_MODEL), F32), jax.ShapeDtypeStruct((s, D_MODEL), BF16),
                   jax.ShapeDtypeStruct((HEADS, s, LANES), BF16)],
    )(qt, kt, q, qa, k, ka, v, qo)


def _fox_gate_bwd(o, qo, dout, w_out, name):
    s = o.shape[0]
    tm = _tile(s, ROW_TILE)

    def body(o_ref, og_ref, dout_ref, w_ref, do_ref, dg_ref, dl_ref):
        ov, dyv = o_ref[...], _dot_nt(dout_ref[0], w_ref[0])
        sg = _sig(og_ref[...])
        do = (dyv * sg).astype(BF16)
        do_ref[...] = do
        dg_ref[...] = (dyv * ov * sg * (1.0 - sg)).astype(BF16)
        prod = do.astype(F32) * ov
        lane = lax.broadcasted_iota(jnp.int32, (tm, LANES), 1)
        for h in range(HEADS):
            delta = jnp.sum(prod[:, h * HEAD_DIM:(h + 1) * HEAD_DIM], axis=-1, keepdims=True)
            dl_ref[h] = _side(lane, 0, delta).astype(BF16)

    row = pl.BlockSpec((tm, D_MODEL), lambda i: (i, 0))
    return pl.pallas_call(
        body, name=name, grid=(s // tm,),
        in_specs=[row, pl.BlockSpec((tm, D_MODEL), lambda i: (i, 1)),
                  pl.BlockSpec((1, tm, D_MODEL), lambda i: (0, i, 0)), _full(w_out.shape)],
        out_specs=[row, row, pl.BlockSpec((HEADS, tm, LANES), lambda i: (0, i, 0))],
        out_shape=[jax.ShapeDtypeStruct((s, D_MODEL), BF16), jax.ShapeDtypeStruct((s, D_MODEL), BF16),
                   jax.ShapeDtypeStruct((HEADS, s, LANES), BF16)],
    )(o, qo, dout, w_out)


def _fox_bwd(q, qab, k, ka, v, do, doa, k_raw, k_gain, name):
    s = q.shape[0]
    t = _tile(s, ATT_TILE)
    n_t = s // t
    sub = t // ATT_SPLIT
    nh = ATT_BWD_HEADS
    qt, kt = _causal_pairs(n_t, key_major=True)

    def body(qt_ref, kt_ref, q_ref, qab_ref, k_ref, ka_ref, v_ref, do_ref, doa_ref, kr_ref, kg_ref, dkr_ref, dkg_ref,
             dv_ref, dka_ref, dq_hbm, dcq_hbm, dk_s, dv_s, dq_ref, dcq_ref):
        group, pid = pl.program_id(0), pl.program_id(1)
        qi, ki = qt_ref[pid], kt_ref[pid]

        @pl.when(pid == 0)
        def _():
            dq_ref[...] = jnp.zeros_like(dq_ref)
            dcq_ref[...] = jnp.zeros_like(dcq_ref)
            dkg_ref[...] = jnp.zeros_like(dkg_ref)

        @pl.when(qi == ki)
        def _():
            dk_s[...] = jnp.zeros_like(dk_s)
            dv_s[...] = jnp.zeros_like(dv_s)

        def step(diagonal):
            for hh in range(nh):
                hc = slice(hh * HEAD_DIM, (hh + 1) * HEAD_DIM)
                kc = jnp.concatenate([k_ref[:, hc], ka_ref[hh]], axis=1)
                vc = jnp.concatenate([v_ref[:, hc], _lane_const(t, 0, 3, -1.0)], axis=1)
                for r in range(ATT_SPLIT):
                    cols = slice(r * sub, (r + 1) * sub)
                    n_k = (r + 1) * sub if diagonal else t
                    qc = jnp.concatenate([q_ref[cols, hc], qab_ref[hh, cols]], axis=1)
                    sc = _dot_nt(kc[:n_k], qc)
                    if diagonal:
                        sc = jnp.where(lax.broadcasted_iota(jnp.int32, (n_k, sub), 0)
                                       <= lax.broadcasted_iota(jnp.int32, (n_k, sub), 1) + r * sub, sc, NEG_INF)
                    p = jnp.exp2(sc)
                    dov = do_ref[cols, hc]
                    dp = _dot_nt(vc[:n_k], jnp.concatenate([dov, doa_ref[hh, cols]], axis=1))
                    ds = (p * dp).astype(BF16)
                    dv_s[hh, 0:n_k] += _dot(p.astype(BF16), dov)
                    dk_s[hh, 0:n_k] += _dot(ds, qc)
                    q_rows = pl.ds(pl.multiple_of(qi * t + r * sub, sub), sub)
                    dq_ref[hh, q_rows, :] += _dot_tn(ds, k_ref[0:n_k, hc])
                    dcq_ref[hh, qi * ATT_SPLIT + r] += jnp.sum(ds.astype(F32), axis=0, keepdims=True)

        @pl.when(qi > ki)
        def _():
            step(False)

        @pl.when(qi == ki)
        def _():
            step(True)

        @pl.when(qi == n_t - 1)
        def _():
            for hh in range(nh):
                hc = slice(hh * HEAD_DIM, (hh + 1) * HEAD_DIM)
                dka_ref[hh] = dk_s[hh, :, HEAD_DIM:]
                dv_ref[:, hc] = dv_s[hh].astype(BF16)
                dk = dk_s[hh, :, :HEAD_DIM] * (1.0 / LOG2E)
                xv = kr_ref[:, hc]
                inv = lax.rsqrt(jnp.mean(xv * xv, axis=-1, keepdims=True) + EPS)
                nrm = xv * inv
                dn = dk * kg_ref[:, hc]
                dkg_ref[:, hc] += _colsum8(dk * nrm)
                dkr_ref[:, hc] = (inv * (dn - nrm * jnp.mean(dn * nrm, axis=-1, keepdims=True))).astype(BF16)

        @pl.when(pid == qt.shape[0] - 1)
        def _():
            pltpu.sync_copy(dq_ref, dq_hbm.at[pl.ds(group * nh, nh)])
            pltpu.sync_copy(dcq_ref, dcq_hbm.at[pl.ds(group * nh, nh)])

    qmain, kmain, qside, kside = _att_specs(t, nh)
    kmain3 = pl.BlockSpec((None, t, nh * HEAD_DIM), lambda h, p, qt, kt: (0, kt[p], h))
    in_hbm = pl.BlockSpec(memory_space=pltpu.HBM)
    return pl.pallas_call(
        body, name=name,
        grid_spec=pltpu.PrefetchScalarGridSpec(
            num_scalar_prefetch=2, grid=(HEADS // nh, qt.shape[0]),
            in_specs=[qmain, qside, kmain, kside, kmain, qmain, qside, kmain,
                      pl.BlockSpec((1, nh * HEAD_DIM), lambda h, p, qt, kt: (0, h))],
            out_specs=[kmain3, pl.BlockSpec((SUBLANES, nh * HEAD_DIM), lambda h, p, qt, kt: (0, h)), kmain3, kside,
                       in_hbm, in_hbm],
            scratch_shapes=[pltpu.VMEM((nh, t, 2 * HEAD_DIM), F32), pltpu.VMEM((nh, t, HEAD_DIM), F32),
                            pltpu.VMEM((nh, s, HEAD_DIM), F32), pltpu.VMEM((nh, s // sub, 1, sub), F32)]),
        out_shape=[jax.ShapeDtypeStruct((1, s, D_MODEL), BF16), jax.ShapeDtypeStruct((SUBLANES, D_MODEL), F32),
                   jax.ShapeDtypeStruct((1, s, D_MODEL), BF16),
                   jax.ShapeDtypeStruct((HEADS, s, LANES), F32), jax.ShapeDtypeStruct((HEADS, s, HEAD_DIM), F32),
                   jax.ShapeDtypeStruct((HEADS, s // sub, 1, sub), F32)],
    )(qt, kt, q, qab, k, ka, v, do, doa, k_raw, k_gain)


def _mm_residual_premix(a, w, x, gate, mods, name):
    s, k = a.shape
    dm = x.shape[1]
    tm = _tile(s, ROW_TILE)

    def body(*refs):
        a_ref, w_ref, x_ref, g_ref = refs[:4]
        mod_refs = refs[4:4 + 2 * len(mods)]
        y_ref, xn_ref = refs[4 + 2 * len(mods):6 + 2 * len(mods)]
        h_refs = refs[6 + 2 * len(mods):]
        y = _dot(a_ref[...], w_ref[0])
        y_ref[...] = y
        xv = x_ref[...] + g_ref[...] * y
        xn_ref[...] = xv
        nrm = xv * lax.rsqrt(jnp.mean(xv * xv, axis=-1, keepdims=True) + EPS)
        for t, h_ref in enumerate(h_refs):
            h_ref[...] = (nrm * (1.0 + mod_refs[2 * t + 1][...]) + mod_refs[2 * t][...]).astype(BF16)

    row = pl.BlockSpec((tm, dm), lambda i: (i, 0))
    vec = _full((1, dm))
    outs = pl.pallas_call(
        body, name=name, grid=(s // tm,),
        in_specs=[pl.BlockSpec((tm, k), lambda i: (i, 0)), _full(w.shape), row, vec] + [vec] * (2 * len(mods)),
        out_specs=[row] * (2 + len(mods)),
        out_shape=[jax.ShapeDtypeStruct((s, dm), F32)] * 2 + [jax.ShapeDtypeStruct((s, dm), BF16)] * len(mods),
    )(a, w, x, gate, *[v for m in mods for v in m])
    return outs[0], outs[1], list(outs[2:])


def _mm_loss_head(a, w, x, gate, target, name):
    s, k = a.shape
    dm = x.shape[1]
    tm = _tile(s, ROW_TILE)

    def body(a_ref, w_ref, x_ref, g_ref, t_ref, sq_ref, do_ref, dy_ref, dg_ref):
        @pl.when(pl.program_id(0) == 0)
        def _():
            sq_ref[...] = jnp.zeros_like(sq_ref)
            dg_ref[...] = jnp.zeros_like(dg_ref)

        y, gv = _dot(a_ref[...], w_ref[0]), g_ref[...]
        err = x_ref[...] + gv * y - t_ref[...]
        sq_ref[...] += _colsum8(err * err)
        dout = err * (1.0 / dm)
        do_ref[...] = dout
        dy_ref[0] = (dout * gv).astype(BF16)
        dg_ref[...] += _colsum8(dout * y)

    row = pl.BlockSpec((tm, dm), lambda i: (i, 0))
    acc = _full((SUBLANES, dm))
    return pl.pallas_call(
        body, name=name, grid=(s // tm,),
        in_specs=[pl.BlockSpec((tm, k), lambda i: (i, 0)), _full(w.shape), row, _full((1, dm)), row],
        out_specs=[acc, row, pl.BlockSpec((1, tm, dm), lambda i: (0, i, 0)), acc],
        out_shape=[jax.ShapeDtypeStruct((SUBLANES, dm), F32), jax.ShapeDtypeStruct((s, dm), F32),
                   jax.ShapeDtypeStruct((1, s, dm), BF16), jax.ShapeDtypeStruct((SUBLANES, dm), F32)],
    )(a, w, x, gate, target)


def _ffn_inner(h, w_up, conv_w, conv_b, tag):
    s, dm = h.shape
    half = w_up.shape[2]
    f = 2 * half
    tm = _tile(s, FFN_ROWS)

    def body(h_ref, w_ref, cw_ref, cb_ref, u_ref, c_ref, a_ref, carry):
        @pl.when(pl.program_id(0) == 0)
        def _():
            carry[...] = jnp.zeros_like(carry)

        hv = h_ref[...]
        for j in range(2):
            cols = slice(j * half, (j + 1) * half)
            conv = []
            for g in range(2):
                ub = _dot(hv, w_ref[2 * g + j]).astype(BF16)
                u_ref[g, :, cols] = ub
                uf = ub.astype(F32)
                e = jnp.concatenate([carry[g, j], uf], axis=0)
                carry[g, j] = uf[tm - SUBLANES:tm]
                conv.append(_conv_taps(e, cw_ref[g][:, cols], cb_ref[g][:, cols])[SUBLANES:])
                c_ref[g, :, cols] = conv[g].astype(BF16)
            a_ref[:, cols] = (conv[0] * _sig(conv[0]) * conv[1]).astype(BF16)

    pair = pl.BlockSpec((2, tm, f), lambda i: (0, i, 0))
    return pl.pallas_call(
        body, name=tag + "_up_convglu", grid=(s // tm,),
        in_specs=[pl.BlockSpec((tm, dm), lambda i: (i, 0)), _full(w_up.shape), _full(conv_w.shape), _full(conv_b.shape)],
        out_specs=[pair, pair, pl.BlockSpec((tm, f), lambda i: (i, 0))],
        out_shape=[jax.ShapeDtypeStruct((2, s, f), BF16)] * 2 + [jax.ShapeDtypeStruct((s, f), BF16)],
        scratch_shapes=[pltpu.VMEM((2, 2, SUBLANES, half), F32)],
    )(h, w_up, conv_w, conv_b)


def _weight_grad_first(a, d, p_n, name):
    return lax.optimization_barrier((_mm_tn(a, d, p_n, name), d))


def _ffn_backward(dx_out, dffn, x_mid, scale, saved, w_up, conv_w, conv_b, w_down, mixer, tag):
    h, u, c, a = saved
    dw_down, dffn = _weight_grad_first(a, dffn, 1, tag + "_down_dw")
    du, dconv = _convglu_bwd(u, c, dffn, w_down, conv_w, tag + "_convglu_bwd")
    dw_up, du = _weight_grad_first(h, du, N_CHIPS, tag + "_up_dw")
    dx_mid, [(dshift, dscale)], dy, dgate_mixer = _premix_bwd(x_mid, [(scale, [(du, w_up)])], dx_out,
                                                              tag + "_premix_bwd", branch=mixer)
    return dx_mid, dy, dgate_mixer, dw_up, dw_down, dict(shift=dshift, scale=dscale, conv=dconv)


def _local_step(x, target, mods, lb, vecs, weights_at):
    m0, m1, mk = mods["l0"], mods["l1"], mods["kv"]
    wts, x = weights_at("mixer0", x)
    h0, proj = _premix_proj(x, m0[0], m0[1], wts["a_w_in"], "l0_premix_in")
    o_a, yp, states = _hgrn_fwd(proj, lb, vecs["a_norm_g"], "l0_hgrn")
    y0, x1, [hf0] = _mm_residual_premix(yp, wts["a_w_out"], x, m0[2], [(m0[3], m0[4])], "l0_out")
    more, hf0 = weights_at("ffn0", hf0)
    wts.update(more)
    u0, c0, a0 = _ffn_inner(hf0, wts["up0"], vecs["conv_w0"], vecs["conv_b0"], "l0_ffn")
    saved0 = (hf0, u0, c0, a0)
    ffn0, x2, [hk, h1] = _mm_residual_premix(a0, wts["down0"], x1, m0[5], [(mk[0], mk[1]), (m1[0], m1[1])],
                                             "l0_ffn_down")
    more, hk = weights_at("layer1", hk)
    wts.update(more)
    k_raw, k_sh, v_sh, f_raw = _kv_proj(hk, wts["kv_k"], wts["kv_v"], wts["kv_f"], vecs["k_norm_g"], "kv_proj")
    qa, ka = _fcum_fwd(f_raw, vecs["kv_b_f"], "kv_fcum")
    q_scale = HEAD_DIM ** -0.5
    qo, q = _proj_headnorm(h1, wts["b_w_q"], vecs["q_norm_g"], q_scale * LOG2E, "l1_q")
    o_b, og, qab = _fox_fwd(q, qa, k_sh, ka, v_sh, qo, "l1_fox")
    y1, x3, [hf1] = _mm_residual_premix(og, wts["b_w_out"], x2, m1[2], [(m1[3], m1[4])], "l1_out")
    u1, c1, a1 = _ffn_inner(hf1, wts["up1"], vecs["conv_w1"], vecs["conv_b1"], "l1_ffn")
    saved1 = (hf1, u1, c1, a1)
    sq, dx4, dffn1, dg2_1 = _mm_loss_head(a1, wts["down1"], x3, m1[5], target, "l1_ffn_down")

    big, small = {}, {}
    dx3, dy1, dg1_1, big["up1"], big["down1"], s_ffn1 = _ffn_backward(
        dx4, dffn1, x3, m1[4], saved1, wts["up1"], vecs["conv_w1"], vecs["conv_b1"], wts["down1"], (y1, m1[2]), "l1_ffn")
    big["b_w_out"], dy1 = _weight_grad_first(og, dy1, 1, "l1_out_dw")
    do_b, dgate_b, doa = _fox_gate_bwd(o_b, qo, dy1, wts["b_w_out"], "l1_out_dx_gate_bwd")
    dk_raw, dkg, dv, dka, dq, dcq = _fox_bwd(q, qab, k_sh, ka, v_sh, do_b, doa, k_raw, vecs["k_norm_g"], "l1_fox_bwd")
    dqo, dqg = _headnorm_bwd(qo, vecs["q_norm_g"], q_scale, dq, "l1_qnorm_bwd", extra=dgate_b)
    big["b_w_q"], dqo = _weight_grad_first(h1, dqo, N_CHIPS, "l1_q_dw")
    dz, dbf = _fcum_bwd(f_raw, vecs["kv_b_f"], dka, dcq.reshape(HEADS, 1, -1), "kv_fcum_bwd")
    big["kv_k"], dk_raw = _weight_grad_first(hk, dk_raw, 1, "kv_k_dw")
    big["kv_v"], dv = _weight_grad_first(hk, dv, 1, "kv_v_dw")
    big["kv_f"], dz = _weight_grad_first(hk, dz, 1, "kv_f_dw")
    kv_pairs = [(dk_raw, wts["kv_k"]), (dv, wts["kv_v"]), (dz, wts["kv_f"])]
    dx2, [(dsh1_1, dsc1_1), (dshk, dsck)], dffn0, dg2_0 = _premix_bwd(
        x2, [(m1[1], [(dqo, wts["b_w_q"])]), (mk[1], kv_pairs)], dx3, "l1_kv_premix_bwd", branch=(ffn0, m0[5]))
    dx1, dy0, dg1_0, big["up0"], big["down0"], s_ffn0 = _ffn_backward(
        dx2, dffn0, x1, m0[4], saved0, wts["up0"], vecs["conv_w0"], vecs["conv_b0"], wts["down0"], (y0, m0[2]), "l0_ffn")
    big["a_w_out"], dy0 = _weight_grad_first(yp, dy0, 1, "l0_out_dw")
    dproj, dlb, dng = _hgrn_bwd(proj, lb, vecs["a_norm_g"], o_a, states, dy0, wts["a_w_out"], "l0_out_dx_hgrn_bwd")
    grad_x, [(dsh1_0, dsc1_0)] = _premix_bwd(x, [(m0[1], [(dproj, wts["a_w_in"])])], dx1, "l0_premix_bwd")
    dproj, _ = lax.optimization_barrier((dproj, (dsh1_0, dsc1_0)))
    big["a_w_in"] = _mm_tn(h0, dproj, N_CHIPS, "l0_in_dw")

    small["mod_l0"] = [dsh1_0, dsc1_0, dg1_0, s_ffn0["shift"], s_ffn0["scale"], dg2_0]
    small["mod_l1"] = [dsh1_1, dsc1_1, dg1_1, s_ffn1["shift"], s_ffn1["scale"], dg2_1]
    small["mod_kv"] = [dshk, dsck]
    small["conv0"], small["conv1"] = s_ffn0["conv"], s_ffn1["conv"]
    small["a_norm_g"], small["k_norm_g"], small["q_norm_g"] = dng, dkg, dqg
    small["kv_b_f"], small["lb"] = dbf, dlb
    marks = {"attention_bwd": dv, "ffn0_bwd": dx1, "mixer0_bwd": grad_x}
    return sq, grad_x, big, small, marks


COMM_CHUNK_ELEMS = 256 * 1024


def _place():
    x, y, c = lax.axis_index("x"), lax.axis_index("y"), lax.axis_index("c")
    chips = [(1 - x, y), (x, 1 - y), (1 - x, 1 - y)]
    return x, y, c, (x, y, 1 - c), chips


def _chunk_rows(rows, cols):
    best = BF16_ROWS
    for r in range(BF16_ROWS, rows + 1, BF16_ROWS):
        if rows % r == 0 and r * cols <= COMM_CHUNK_ELEMS:
            best = r
    assert rows % best == 0, (rows, cols)
    return best


def _allgather8(block, name):
    m_per, n = block.shape

    def body(x_ref, out_ref, send_sems, recv_sems, local_sem):
        x, y, c, sibling, chips = _place()
        me = (x, y, c)

        def rows(px, py, pc):
            return out_ref.at[pl.ds((4 * px + 2 * py + pc) * m_per, m_per), :]

        def copy(k, blk, to, src=None):
            return pltpu.make_async_remote_copy(
                src_ref=rows(*blk) if src is None else src, dst_ref=rows(*blk),
                send_sem=send_sems.at[k], recv_sem=recv_sems.at[k], device_id=to, device_id_type=MESH)

        mine = pltpu.make_async_copy(x_ref, rows(*me), local_sem)
        mine.start()
        first = [copy(0, me, sibling, src=x_ref)]
        first += [copy(1 + j, me, (*chip, c), src=x_ref) for j, chip in enumerate(chips)]
        for cp in first:
            cp.start()
        passed = [copy(4 + j, (*chip, c), sibling) for j, chip in enumerate(chips)]
        for j, chip in enumerate(chips):
            copy(1 + j, (*chip, c), me).wait_recv()
            passed[j].start()
        copy(0, sibling, me).wait_recv()
        for j, chip in enumerate(chips):
            copy(4 + j, (*chip, 1 - c), me).wait_recv()
        for cp in first + passed:
            cp.wait_send()
        mine.wait()

    return pl.pallas_call(
        body, name=name, out_shape=jax.ShapeDtypeStruct((N_DEV * m_per, n), block.dtype),
        in_specs=[pl.BlockSpec(memory_space=pltpu.VMEM)], out_specs=pl.BlockSpec(memory_space=pltpu.VMEM),
        scratch_shapes=[pltpu.SemaphoreType.DMA((7,)), pltpu.SemaphoreType.DMA((7,)), pltpu.SemaphoreType.DMA],
    )(block)


def _cast_own_block(shards, layer, chip, name):
    _, r, cols = shards.shape
    rows = _chunk_rows(r, cols)

    def body(chip_ref, w_ref, o_ref):
        o_ref[...] = w_ref[...].astype(BF16)

    return pl.pallas_call(
        body, name=name,
        grid_spec=pltpu.PrefetchScalarGridSpec(
            num_scalar_prefetch=1, grid=(r // rows,),
            in_specs=[pl.BlockSpec((None, rows, cols), lambda i, chip_ref: (layer, i, 0))],
            out_specs=pl.BlockSpec((None, rows, cols), lambda i, chip_ref: (chip_ref[0], i, 0))),
        out_shape=jax.ShapeDtypeStruct((N_CHIPS, r, cols), BF16),
    )(chip, shards)


def _sequencer_gather(bufs, name, collective_id):
    n_t = len(bufs)
    dims = [b.shape[1:] for b in bufs]
    refs = [jax.new_ref(b, memory_space=pltpu.MemorySpace.HBM) for b in bufs]

    @pl.kernel(mesh=plsc.ScalarSubcoreMesh(axis_name="sequencer", num_cores=1), name=name,
               scratch_types=[pltpu.SemaphoreType.DMA((n_t,)), pltpu.SemaphoreType.DMA((3 * n_t,)),
                              pltpu.SemaphoreType.DMA((n_t,)), pltpu.SemaphoreType.DMA((n_t,))],
               compiler_params=pltpu.CompilerParams(collective_id=collective_id))
    def launch(send_ici, recv_ici, send_d2d, recv_d2d):
        x, y, c, sibling, chips = _place()
        p_me = 2 * x + y
        peers = [sibling] + [(cx, cy, c) for cx, cy in chips]
        barrier = pltpu.get_barrier_semaphore()
        for peer in peers:
            pl.semaphore_signal(barrier, inc=1, device_id=peer, device_id_type=MESH)
        pl.semaphore_wait(barrier, len(peers))

        def waiter(t, sem_s, sem_r):
            win = refs[t].at[pl.ds(0, 3), pl.ds(0, dims[t][0] // 2), :]
            return pltpu.make_async_remote_copy(src_ref=win, dst_ref=win, send_sem=sem_s.at[t], recv_sem=sem_r.at[t],
                                                device_id=sibling, device_id_type=MESH)

        def half_copy(t, chip_idx, to, sem_s, sem_r, k):
            r2 = dims[t][0] // 2
            win = refs[t].at[chip_idx, pl.ds(c * r2, r2), :]
            return pltpu.make_async_remote_copy(src_ref=win, dst_ref=win, send_sem=sem_s.at[t], recv_sem=sem_r.at[k],
                                                device_id=to, device_id_type=MESH)

        for t in range(n_t):
            for j, (cx, cy) in enumerate(chips):
                half_copy(t, p_me, (cx, cy, c), send_ici, recv_ici, 3 * t + j).start()
        for t in range(n_t):
            for j, (cx, cy) in enumerate(chips):
                half_copy(t, 2 * cx + cy, (cx, cy, c), send_ici, recv_ici, 3 * t + j).wait_recv()
                half_copy(t, 2 * cx + cy, sibling, send_d2d, recv_d2d, t).start()
        for t in range(n_t):
            waiter(t, send_d2d, recv_d2d).wait_recv()
            waiter(t, send_ici, recv_ici).wait_send()
            waiter(t, send_d2d, recv_d2d).wait_send()

    launch()
    return [r[...] for r in refs]


def _sequencer_allgather8(block, dev, name, collective_id):
    m_per, n = block.shape
    src = jax.new_ref(block, memory_space=pltpu.MemorySpace.HBM)
    out = jax.empty_ref(jax.ShapeDtypeStruct((N_DEV * m_per, n), block.dtype), memory_space=pltpu.MemorySpace.HBM)

    @pl.kernel(mesh=plsc.ScalarSubcoreMesh(axis_name="sequencer", num_cores=1), name=name,
               scratch_types=[pltpu.SemaphoreType.DMA((7,))] * 2,
               compiler_params=pltpu.CompilerParams(collective_id=collective_id))
    def launch(send_sems, recv_sems):
        x, y, c, sibling, chips = _place()
        me = (x, y, c)
        _handshake([sibling] + [(cx, cy, c) for cx, cy in chips])

        def rows(px, py, pc):
            return out.at[pl.ds((4 * px + 2 * py + pc) * m_per, m_per), :]

        def copy(k, blk, to, from_src=False):
            return pltpu.make_async_remote_copy(
                src_ref=src if from_src else rows(*blk), dst_ref=rows(*blk),
                send_sem=send_sems.at[k], recv_sem=recv_sems.at[k], device_id=to, device_id_type=MESH)

        first = [copy(0, me, sibling, True)] + [copy(1 + j, me, (*chip, c), True) for j, chip in enumerate(chips)]
        for cp in first:
            cp.start()
        passed = [copy(4 + j, (*chip, c), sibling) for j, chip in enumerate(chips)]
        for j, chip in enumerate(chips):
            copy(1 + j, (*chip, c), me).wait_recv()
            passed[j].start()
        copy(0, sibling, me).wait_recv()
        for j, chip in enumerate(chips):
            copy(4 + j, (*chip, 1 - c), me).wait_recv()
        for cp in first + passed:
            cp.wait_send()

    launch()
    return lax.dynamic_update_slice(out[...], block, (dev * m_per, 0))


def _others():
    x, y, c = lax.axis_index("x"), lax.axis_index("y"), lax.axis_index("c")
    flip = lambda v, f: 1 - v if f else v
    return [(flip(x, fx), flip(y, fy), flip(c, fc))
            for fx in (0, 1) for fy in (0, 1) for fc in (0, 1) if (fx, fy, fc) != (0, 0, 0)]


def _handshake(peers):
    barrier = pltpu.get_barrier_semaphore()
    for peer in peers:
        pl.semaphore_signal(barrier, inc=1, device_id=peer, device_id_type=MESH)
    pl.semaphore_wait(barrier, len(peers))


def _sequencer_scatter(parts, name, collective_id):
    n_t = len(parts)
    dims = [p.shape[1:] for p in parts]
    srcs = [jax.new_ref(p, memory_space=pltpu.MemorySpace.HBM) for p in parts]
    inboxes = [jax.empty_ref(jax.ShapeDtypeStruct((N_DEV - 1, r // 2, cols), BF16), memory_space=pltpu.MemorySpace.HBM)
               for r, cols in dims]

    @pl.kernel(mesh=plsc.ScalarSubcoreMesh(axis_name="sequencer", num_cores=1), name=name,
               scratch_types=[pltpu.SemaphoreType.DMA((n_t,))] * 2,
               compiler_params=pltpu.CompilerParams(collective_id=collective_id))
    def launch(send_sem, recv_sem):
        peers = _others()
        _handshake(peers)
        for t in range(n_t):
            h = dims[t][0] // 2
            for k, (qx, qy, qc) in enumerate(peers):
                pltpu.make_async_remote_copy(
                    src_ref=srcs[t].at[2 * qx + qy, pl.ds(qc * h, h), :], dst_ref=inboxes[t].at[k],
                    send_sem=send_sem.at[t], recv_sem=recv_sem.at[t], device_id=(qx, qy, qc), device_id_type=MESH).start()
        for t in range(n_t):
            win = inboxes[t]
            both = pltpu.make_async_remote_copy(src_ref=win, dst_ref=win, send_sem=send_sem.at[t],
                                                recv_sem=recv_sem.at[t], device_id=peers[0], device_id_type=MESH)
            both.wait_recv()
            both.wait_send()

    launch()
    return [b[...] for b in inboxes]


def _sum_pieces(part, inbox, place, name):
    _, r, cols = part.shape
    h = r // 2
    rows = _chunk_rows(h, cols)
    steps = h // rows

    def body(place_ref, own_ref, in_ref, o_ref):
        acc = own_ref[...].astype(F32)
        for k in range(N_DEV - 1):
            acc = acc + in_ref[k].astype(F32)
        o_ref[...] = acc

    return pl.pallas_call(
        body, name=name,
        grid_spec=pltpu.PrefetchScalarGridSpec(
            num_scalar_prefetch=1, grid=(steps,),
            in_specs=[pl.BlockSpec((None, rows, cols), lambda i, pr: (pr[0], pr[1] * steps + i, 0)),
                      pl.BlockSpec((N_DEV - 1, rows, cols), lambda i, pr: (0, i, 0))],
            out_specs=pl.BlockSpec((rows, cols), lambda i, pr: (pr[1] * steps + i, 0))),
        out_shape=jax.ShapeDtypeStruct((r, cols), F32),
    )(place, part, inbox)


def _sequencer_swap_halves(halves, name, collective_id):
    n_t = len(halves)
    refs = [jax.new_ref(a, memory_space=pltpu.MemorySpace.HBM) for a in halves]

    @pl.kernel(mesh=plsc.ScalarSubcoreMesh(axis_name="sequencer", num_cores=1), name=name,
               scratch_types=[pltpu.SemaphoreType.DMA((n_t,))] * 2,
               compiler_params=pltpu.CompilerParams(collective_id=collective_id))
    def launch(send_sem, recv_sem):
        x, y, c = lax.axis_index("x"), lax.axis_index("y"), lax.axis_index("c")
        sibling = (x, y, 1 - c)
        _handshake([sibling])
        copies = []
        for t in range(n_t):
            h = halves[t].shape[0] // 2
            win = refs[t].at[pl.ds(c * h, h), :]
            copies.append(pltpu.make_async_remote_copy(src_ref=win, dst_ref=win, send_sem=send_sem.at[t],
                                                       recv_sem=recv_sem.at[t], device_id=sibling, device_id_type=MESH))
            copies[-1].start()
        for cp in copies:
            cp.wait()

    launch()
    return [r[...] for r in refs]


def _cond_rows(c16, w, act, name):
    n_l, dm, wid = w.shape

    def body(c_ref, w_ref, o_ref, a_ref):
        cv = c_ref[...]
        if act:
            cv = cv * _sig(cv)
        a_ref[...] = cv
        o_ref[...] = _dot_f32(cv, w_ref[...])

    return pl.pallas_call(
        body, name=name, grid=(n_l,),
        in_specs=[_full((16, dm)), pl.BlockSpec((None, dm, wid), lambda l: (l, 0, 0))],
        out_specs=[pl.BlockSpec((None, 16, wid), lambda l: (l, 0, 0)), _full((16, dm))],
        out_shape=[jax.ShapeDtypeStruct((n_l, 16, wid), F32), jax.ShapeDtypeStruct((16, dm), F32)],
    )(c16, w)


def _outer_grad(ct, dm, name):
    n_l, kk, wid = dm.shape
    d_rows = ct.shape[0]

    def body(c_ref, d_ref, o_ref):
        o_ref[...] = _dot_f32(c_ref[...], d_ref[...])

    return pl.pallas_call(
        body, name=name, grid=(n_l,),
        in_specs=[_full((d_rows, kk)), pl.BlockSpec((None, kk, wid), lambda l: (l, 0, 0))],
        out_specs=pl.BlockSpec((None, d_rows, wid), lambda l: (l, 0, 0)),
        out_shape=jax.ShapeDtypeStruct((n_l, d_rows, wid), F32),
    )(ct, dm)


def _sum_devices(g, name):
    rows, n = g.shape

    def body(g_ref, o_ref):
        acc = g_ref[0:SUBLANES, :]
        for dev in range(1, N_DEV):
            acc = acc + g_ref[dev * SUBLANES:(dev + 1) * SUBLANES, :]
        o_ref[...] = acc

    return pl.pallas_call(body, name=name, out_shape=jax.ShapeDtypeStruct((SUBLANES, n), F32))(g)


def _adamw(w, g, m, v, name):
    shape = w.shape
    cols = shape[-1]
    rows = w.size // cols
    tr = rows
    for cand in range(SUBLANES, min(rows, 256) + 1, SUBLANES):
        if rows % cand == 0:
            tr = cand
    if rows * cols <= COMM_CHUNK_ELEMS:
        tr = rows
    c1 = 1.0 / (1.0 - ADAM_B1 ** ADAM_STEP)
    c2 = 1.0 / (1.0 - ADAM_B2 ** ADAM_STEP)

    def body(w_ref, g_ref, m_ref, v_ref, d_ref, mo_ref, vo_ref):
        gv = g_ref[...]
        m_new = ADAM_B1 * m_ref[...] + (1.0 - ADAM_B1) * gv
        v_new = ADAM_B2 * v_ref[...] + (1.0 - ADAM_B2) * (gv * gv)
        mo_ref[...] = m_new
        vo_ref[...] = v_new
        d_ref[...] = -ADAM_LR * ((m_new * c1) / (jnp.sqrt(v_new * c2) + ADAM_EPS) + ADAM_WD * w_ref[...])

    spec = pl.BlockSpec((tr, cols), lambda i: (i, 0))
    outs = pl.pallas_call(
        body, name=name, grid=(rows // tr,), in_specs=[spec] * 4, out_specs=[spec] * 3,
        out_shape=[jax.ShapeDtypeStruct((rows, cols), F32)] * 3,
    )(*[a.reshape(rows, cols) for a in (w, g, m, v)])
    return tuple(o.reshape(shape) for o in outs)


def _pad_cols(a, cols):
    return jnp.pad(a, [(0, 0)] * (a.ndim - 1) + [(0, cols - a.shape[-1])])


def _flat8(parts, width):
    v = jnp.concatenate([p.reshape(-1) for p in parts])
    return jnp.pad(v, (0, width - v.shape[0])).reshape(SUBLANES, width // SUBLANES)


KV_SHARD = 514
KV_SHARD_PAD = 640
BIG = ("a_w_in", "a_w_out", "kv_w", "b_w_q", "b_w_out", "up0", "up1", "down0", "down1")


def kernel(x, c, ada_w, ada_b, a_w_in, a_lb_logits, a_norm_g, a_w_out, kv_ada_w, kv_ada_b, kv_w, kv_b_f, k_norm_g, b_w_q, q_norm_g, b_w_out, ffn_w_up, ffn_conv_w, ffn_conv_b, ffn_w_down, loss_target, m_ada_w, m_ada_b, m_a_w_in, m_a_lb_logits, m_a_norm_g, m_a_w_out, m_kv_ada_w, m_kv_ada_b, m_kv_w, m_kv_b_f, m_k_norm_g, m_b_w_q, m_q_norm_g, m_b_w_out, m_ffn_w_up, m_ffn_conv_w, m_ffn_conv_b, m_ffn_w_down, v_ada_w, v_ada_b, v_a_w_in, v_a_lb_logits, v_a_norm_g, v_a_w_out, v_kv_ada_w, v_kv_ada_b, v_kv_w, v_kv_b_f, v_k_norm_g, v_b_w_q, v_q_norm_g, v_b_w_out, v_ffn_w_up, v_ffn_conv_w, v_ffn_conv_b, v_ffn_w_down):
    dm, ff = D_MODEL, D_FF
    ix, iy, ic = lax.axis_index("x"), lax.axis_index("y"), lax.axis_index("c")
    chip = 2 * ix + iy
    dev = 2 * chip + ic

    w1 = 10240
    g1 = _allgather8(_flat8([c, a_lb_logits, ffn_conv_w], w1), "gather_cond").reshape(N_DEV, w1)
    c_all = g1[:, :dm]
    per_chip = g1[0::2]
    lb_logits = per_chip[:, dm:dm + 512].reshape(N_CHIPS, 2, 256).transpose(1, 0, 2).reshape(2, dm)
    conv_w = per_chip[:, dm + 512:dm + 512 + 2 * CONV_W * FFN_COLS].reshape(N_CHIPS, 2, CONV_W, FFN_COLS)
    conv_w = conv_w.transpose(1, 2, 0, 3).reshape(2, CONV_W, 2, ff).transpose(0, 2, 1, 3)
    conv_b = ffn_conv_b.reshape(2, 2, 1, ff)
    lb = jax.nn.softmax(lb_logits, axis=0)[0:1]

    c16 = jnp.pad(c_all, ((0, 8), (0, 0)))
    mod_ada, c_act16 = _cond_rows(c16, ada_w, True, "mod_ada")
    mod_kv, _ = _cond_rows(c16, kv_ada_w[None], True, "mod_kv")
    mine = jnp.concatenate([mod_ada[0, :8], mod_ada[1, :8], mod_kv[0, :8]], axis=1)
    w2 = mine.shape[1]
    g2 = _allgather8(mine, "gather_mod").reshape(N_DEV, 8, w2)[0::2]
    my_rows = lax.dynamic_index_in_dim(g2, dev, axis=1, keepdims=False)
    mod0 = my_rows[:, 0:1536].reshape(6 * dm) + ada_b[0]
    mod1 = my_rows[:, 1536:3072].reshape(6 * dm) + ada_b[1]
    modk = my_rows[:, 3072:3584].reshape(2 * dm) + kv_ada_b
    mods = {"l0": [v.reshape(1, dm) for v in jnp.split(mod0, 6)],
            "l1": [v.reshape(1, dm) for v in jnp.split(mod1, 6)],
            "kv": [v.reshape(1, dm) for v in jnp.split(modk, 2)]}

    local = [(a_w_in, 0), (a_w_out, 0), (_pad_cols(kv_w, KV_SHARD_PAD)[None], 0), (b_w_q, 0), (b_w_out, 0),
             (ffn_w_up, 0), (ffn_w_up, 1), (ffn_w_down, 0), (ffn_w_down, 1)]
    chip_arr = chip.reshape(1).astype(jnp.int32)
    local = dict(zip(BIG, local))
    stages = {"mixer0": ("a_w_in", "a_w_out"), "ffn0": ("up0", "down0"),
              "layer1": ("kv_w", "b_w_q", "b_w_out", "up1", "down1")}
    arriving = {}

    def launch(stage, behind):
        shards = [local[n][0] for n in stages[stage]]
        if behind is not None:
            shards, _ = lax.optimization_barrier((shards, behind))
        own = [_cast_own_block(w, local[n][1], chip_arr, "cast_" + n) for n, w in zip(stages[stage], shards)]
        arriving[stage] = _sequencer_gather(own, "gather_" + stage, 1 + list(stages).index(stage))

    launch("mixer0", None)
    launch("ffn0", mod0)
    rowwise = lambda g: g.reshape(1, -1, dm)

    def weights_at(stage, token):
        if stage == "ffn0":
            launch("layer1", token)
        got, token = lax.optimization_barrier((arriving[stage], token))
        g = dict(zip(stages[stage], got))
        if stage == "mixer0":
            return {"a_w_in": g["a_w_in"], "a_w_out": rowwise(g["a_w_out"])}, token
        if stage == "ffn0":
            return {"up0": g["up0"], "down0": rowwise(g["down0"])}, token
        s0, s1, s2, s3 = (g["kv_w"][p] for p in range(N_CHIPS))
        second = dm - KV_SHARD
        w_k = jnp.concatenate([s0[:, :KV_SHARD], s1[:, :second]], axis=1)
        w_v = jnp.concatenate([s1[:, second:KV_SHARD], s2[:, :KV_SHARD], s3[:, :KV_SHARD - HEADS]], axis=1)
        w_f = _pad_cols(s3[:, KV_SHARD - HEADS:KV_SHARD], LANES)
        return {"kv_k": w_k[None], "kv_v": w_v[None], "kv_f": w_f[None], "b_w_q": g["b_w_q"],
                "b_w_out": rowwise(g["b_w_out"]), "up1": g["up1"], "down1": rowwise(g["down1"])}, token

    vecs = {"a_norm_g": jnp.tile(a_norm_g, (1, HEADS)), "k_norm_g": jnp.tile(k_norm_g[None], (1, HEADS)),
            "q_norm_g": jnp.tile(q_norm_g, (1, HEADS)), "kv_b_f": _pad_cols(kv_b_f[None], LANES),
            "conv_w0": conv_w[0], "conv_b0": conv_b[0], "conv_w1": conv_w[1], "conv_b1": conv_b[1]}

    sq, grad_x, big, small, marks = _local_step(x[0], loss_target[0], mods, lb, vecs, weights_at)

    gk, gv, gf = big["kv_k"][0], big["kv_v"][0], big["kv_f"][0][:, :HEADS]
    second = dm - KV_SHARD
    kv_blocks = [gk[:, :KV_SHARD], jnp.concatenate([gk[:, KV_SHARD:], gv[:, :KV_SHARD - second]], axis=1),
                 gv[:, KV_SHARD - second:2 * KV_SHARD - second], jnp.concatenate([gv[:, 2 * KV_SHARD - second:], gf], axis=1)]
    kv_grad = jnp.stack([_pad_cols(b, KV_SHARD_PAD) for b in kv_blocks])
    chipwise = lambda g: g.reshape(N_CHIPS, -1, dm)
    parts = dict(zip(BIG, [big["a_w_in"], chipwise(big["a_w_out"]), kv_grad, big["b_w_q"], chipwise(big["b_w_out"]),
                           big["up0"], big["up1"], chipwise(big["down0"]), chipwise(big["down1"])]))
    place = jnp.stack([chip, ic, dev]).astype(jnp.int32)

    served = []
    boxes = {}

    groups = (("up1", "down1"), ("b_w_out", "b_w_q", "kv_w"), ("up0", "down0"), ("a_w_out", "a_w_in"))

    def scatter_group(k):
        mine = [parts[n] for n in groups[k]]
        if served:
            mine, _ = lax.optimization_barrier((mine, served[-1]))
        boxes[k] = _sequencer_scatter(mine, "scatter_grads_%d" % k, 4 + k)
        served.append(boxes[k])

    def sum_group(k, token):
        inboxes, _ = lax.optimization_barrier((boxes[k], token))
        return [_sum_pieces(parts[n], box, place, "sum_" + n) for n, box in zip(groups[k], inboxes)]

    def swap_group(k, halves, behind):
        halves, _ = lax.optimization_barrier((halves, behind))
        return dict(zip(groups[k], _sequencer_swap_halves(halves, "swap_grads_%d" % k, 8 + k)))

    for k in range(3):
        scatter_group(k)
    halves = [sum_group(0, marks["attention_bwd"]), sum_group(1, marks["ffn0_bwd"]), sum_group(2, marks["mixer0_bwd"])]

    fold = lambda a: a.sum(axis=0)
    heads = lambda a: fold(a).reshape(HEADS, HEAD_DIM).sum(axis=0)
    conv_flat = lambda a: a.sum(axis=2).transpose(1, 0, 2)
    pieces = ([fold(a) for a in small["mod_l0"]] + [fold(a) for a in small["mod_l1"]] + [fold(a) for a in small["mod_kv"]]
              + [conv_flat(small["conv0"]), conv_flat(small["conv1"]), heads(small["a_norm_g"]), heads(small["k_norm_g"]),
                 heads(small["q_norm_g"]), fold(small["kv_b_f"]), fold(small["lb"]),
                 0.5 * jnp.sum(sq).reshape(1) / dm])
    w3 = 61440
    small_vec, _ = lax.optimization_barrier((_flat8(pieces, w3), served[2]))
    g3 = _sequencer_allgather8(small_vec, dev, "gather_small", 12)
    served.append(g3)
    scatter_group(3)
    rs = {}
    for k in range(3):
        rs.update(swap_group(k, halves[k], g3))
    tot = _sum_devices(g3, "sum_small").reshape(w3)
    n_mod = 14 * dm
    dmod_all = g3.reshape(N_DEV, w3)[:, :n_mod]
    o = n_mod
    conv_tot = [tot[o + l * 8 * ff: o + (l + 1) * 8 * ff].reshape(4, 2 * ff) for l in range(2)]
    o += 16 * ff
    g_a_norm, g_k_norm, g_q_norm = (tot[o + i * HEAD_DIM: o + (i + 1) * HEAD_DIM] for i in range(3))
    o += 3 * HEAD_DIM
    g_kv_b_f = tot[o:o + HEADS]
    dlb = tot[o + LANES:o + LANES + dm]
    loss = tot[o + LANES + dm]

    ct = _pad_cols(c_act16[:8].T, LANES)
    dmod_pad = jnp.pad(dmod_all, ((0, LANES - N_DEV), (0, 0)))
    cols_ada = jnp.stack([lax.dynamic_slice_in_dim(dmod_pad, l * 6 * dm + chip * 1536, 1536, axis=1) for l in range(2)])
    cols_kv = lax.dynamic_slice_in_dim(dmod_pad, 12 * dm + chip * 512, 512, axis=1)[None]
    g_ada_w = _outer_grad(ct, cols_ada, "grad_ada_w")
    g_kv_ada_w = _outer_grad(ct, cols_kv, "grad_kv_ada_w")[0]

    my_lb = lax.dynamic_slice_in_dim(lb[0], chip * 256, 256)
    l0 = lax.dynamic_slice_in_dim(dlb, chip * 256, 256) * my_lb * (1.0 - my_lb)
    grads = {
        "ada_w": g_ada_w, "ada_b": jnp.stack([tot[:6 * dm], tot[6 * dm:12 * dm]]),
        "a_lb_logits": jnp.stack([l0, -l0]), "a_norm_g": g_a_norm[None],
        "kv_ada_w": g_kv_ada_w, "kv_ada_b": tot[12 * dm:14 * dm],
        "kv_w": rs["kv_w"][:, :KV_SHARD], "kv_b_f": g_kv_b_f, "k_norm_g": g_k_norm,
        "b_w_q": rs["b_w_q"][None], "q_norm_g": g_q_norm[None], "b_w_out": rs["b_w_out"][None],
        "ffn_w_up": jnp.stack([rs["up0"], rs["up1"]]),
        "ffn_conv_w": jnp.stack([lax.dynamic_slice_in_dim(ct_l[:CONV_W], chip * FFN_COLS, FFN_COLS, axis=1) for ct_l in conv_tot]),
        "ffn_conv_b": jnp.stack([ct_l[CONV_W] for ct_l in conv_tot]),
        "ffn_w_down": jnp.stack([rs["down0"], rs["down1"]]),
    }
    weights = dict(ada_w=ada_w, ada_b=ada_b, a_w_in=a_w_in, a_lb_logits=a_lb_logits, a_norm_g=a_norm_g, a_w_out=a_w_out,
                   kv_ada_w=kv_ada_w, kv_ada_b=kv_ada_b, kv_w=kv_w, kv_b_f=kv_b_f, k_norm_g=k_norm_g, b_w_q=b_w_q,
                   q_norm_g=q_norm_g, b_w_out=b_w_out, ffn_w_up=ffn_w_up, ffn_conv_w=ffn_conv_w, ffn_conv_b=ffn_conv_b,
                   ffn_w_down=ffn_w_down)
    m_in = dict(ada_w=m_ada_w, ada_b=m_ada_b, a_w_in=m_a_w_in, a_lb_logits=m_a_lb_logits, a_norm_g=m_a_norm_g,
                a_w_out=m_a_w_out, kv_ada_w=m_kv_ada_w, kv_ada_b=m_kv_ada_b, kv_w=m_kv_w, kv_b_f=m_kv_b_f,
                k_norm_g=m_k_norm_g, b_w_q=m_b_w_q, q_norm_g=m_q_norm_g, b_w_out=m_b_w_out, ffn_w_up=m_ffn_w_up,
                ffn_conv_w=m_ffn_conv_w, ffn_conv_b=m_ffn_conv_b, ffn_w_down=m_ffn_w_down)
    v_in = dict(ada_w=v_ada_w, ada_b=v_ada_b, a_w_in=v_a_w_in, a_lb_logits=v_a_lb_logits, a_norm_g=v_a_norm_g,
                a_w_out=v_a_w_out, kv_ada_w=v_kv_ada_w, kv_ada_b=v_kv_ada_b, kv_w=v_kv_w, kv_b_f=v_kv_b_f,
                k_norm_g=v_k_norm_g, b_w_q=v_b_w_q, q_norm_g=v_q_norm_g, b_w_out=v_b_w_out, ffn_w_up=v_ffn_w_up,
                ffn_conv_w=v_ffn_conv_w, ffn_conv_b=v_ffn_conv_b, ffn_w_down=v_ffn_w_down)

    names = list(weights)
    step = lambda n: _adamw(weights[n], grads[n], m_in[n], v_in[n], "adamw_" + n)
    grads = {n: g.reshape(weights[n].shape) for n, g in grads.items()}
    upd = {n: step(n) for n in names if n not in groups[3]}
    last = sum_group(3, [u[0] for u in upd.values()])
    for n, g in swap_group(3, last, last).items():
        grads[n] = g[None]
        upd[n] = step(n)
    return (loss, grad_x[None], *[grads[n] for n in names], *[upd[n][0] for n in names],
            *[upd[n][1] for n in names], *[upd[n][2] for n in names])
```

```python
import jax
import jax.numpy as jnp
from jax import lax
from jax.experimental import pallas as pl
from jax.experimental.pallas import tpu as pltpu
from jax.experimental.pallas import tpu_sc as plsc

F32 = jnp.float32
BF16 = jnp.bfloat16

D_MODEL = 1024
HEADS = 8
HEAD_DIM = 128
A_CHUNK = 64
D_FF = 2816
CONV_W = 3
EPS = 1e-6
NEG_INF = -1e30
N_CHIPS = 4
N_DEV = 8

ADAM_LR = 0.001
ADAM_B1 = 0.9
ADAM_B2 = 0.999
ADAM_EPS = 1e-08
ADAM_WD = 0.01
ADAM_STEP = 10

SUBLANES = 8
BF16_ROWS = 16
LANES = 128
HALO = BF16_ROWS
ROW_TILE = 512
TOKEN_TILE_TN = 2048
FFN_COLS = 1408
FFN_ROWS = 256
HGRN_ROWS = 512
ATT_TILE = 512
ATT_SPLIT = 2
ATT_FWD_HEADS = 8
ATT_BWD_HEADS = 8
MESH = pl.DeviceIdType.MESH


def _sig(x):
    return jax.nn.sigmoid(x)


def _dot(a, b):
    return jnp.dot(a, b, preferred_element_type=F32)


def _dot_nt(a, b):
    return lax.dot_general(a, b, (((1,), (1,)), ((), ())), preferred_element_type=F32)


def _dot_tn(a, b):
    return lax.dot_general(a, b, (((0,), (0,)), ((), ())), preferred_element_type=F32)


def _split2(x):
    hi = x.astype(BF16)
    lo = (x - hi.astype(F32)).astype(BF16)
    return hi, lo


def _dot_f32(a, b):
    ah, al = _split2(a)
    bh, bl = _split2(b)
    return _dot(ah, bh) + _dot(ah, bl) + _dot(al, bh)


def _tri_dot(tri, x):
    hi = x.astype(BF16)
    r = x - hi.astype(F32)
    mid = r.astype(BF16)
    lo = (r - mid.astype(F32)).astype(BF16)
    return _dot(tri, hi) + _dot(tri, mid) + _dot(tri, lo)


def _tri(n, upper=False):
    r = lax.broadcasted_iota(jnp.int32, (n, n), 0)
    c = lax.broadcasted_iota(jnp.int32, (n, n), 1)
    keep = (c >= r) if upper else (c <= r)
    return jnp.where(keep, 1.0, 0.0).astype(BF16)


def _colsum8(v):
    rows, n = v.shape
    return v.reshape(rows // SUBLANES, SUBLANES, n).sum(axis=0)


def _full(shape):
    nd = len(shape)
    return pl.BlockSpec(shape, lambda *_: (0,) * nd)


def _tile(n, want):
    t = min(n, want)
    assert n % t == 0, (n, t)
    return t


def _mm_tn(a, d, p_n, name):
    m_rows, k = a.shape
    g_n, _, w_cols = d.shape
    per = p_n // g_n
    n = w_cols // per
    tm = _tile(m_rows, TOKEN_TILE_TN if k <= D_MODEL else ROW_TILE)
    steps = m_rows // tm

    def body(a_ref, d_ref, o_ref, acc):
        m = pl.program_id(1)

        @pl.when(m == 0)
        def _():
            acc[...] = jnp.zeros_like(acc)

        acc[...] += _dot_tn(a_ref[...], d_ref[...])

        @pl.when(m == steps - 1)
        def _():
            o_ref[...] = acc[...].astype(BF16)

    return pl.pallas_call(
        body, name=name, grid=(p_n, steps),
        in_specs=[pl.BlockSpec((tm, k), lambda p, m: (m, 0)),
                  pl.BlockSpec((None, tm, n), lambda p, m: (p // per, m, p % per))],
        out_specs=pl.BlockSpec((None, k, n), lambda p, m: (p, 0, 0)),
        out_shape=jax.ShapeDtypeStruct((p_n, k, n), BF16),
        scratch_shapes=[pltpu.VMEM((k, n), F32)],
    )(a, d)


def _premix_proj(x, shift, scale, w, name):
    s, dm = x.shape
    p_n, _, n = w.shape
    tm = _tile(s, ROW_TILE)

    def body(x_ref, sh_ref, sc_ref, w_ref, h_ref, o_ref):
        xv = x_ref[...]
        inv = lax.rsqrt(jnp.mean(xv * xv, axis=-1, keepdims=True) + EPS)
        h = (xv * inv * (1.0 + sc_ref[...]) + sh_ref[...]).astype(BF16)
        h_ref[...] = h
        for p in range(p_n):
            o_ref[:, p * n:(p + 1) * n] = _dot(h, w_ref[p])

    row = pl.BlockSpec((tm, dm), lambda i: (i, 0))
    vec = _full((1, dm))
    return pl.pallas_call(
        body, name=name, grid=(s // tm,), in_specs=[row, vec, vec, _full(w.shape)],
        out_specs=[row, pl.BlockSpec((tm, p_n * n), lambda i: (i, 0))],
        out_shape=[jax.ShapeDtypeStruct((s, dm), BF16), jax.ShapeDtypeStruct((s, p_n * n), F32)],
    )(x, shift, scale, w)


def _premix_bwd(x, terms, dres, name, branch=None):
    s, dm = x.shape
    tm = _tile(s, ROW_TILE)
    pairs = [pr for _, prs in terms for pr in prs]
    n_in = 2 + len(terms) + 2 * len(pairs) + (2 if branch else 0)

    def body(*refs):
        x_ref, dres_ref = refs[:2]
        sc_refs = refs[2:2 + len(terms)]
        mm_refs = refs[2 + len(terms):2 + len(terms) + 2 * len(pairs)]
        outs = refs[n_in:]

        @pl.when(pl.program_id(0) == 0)
        def _():
            for o in outs[1:1 + 2 * len(terms)]:
                o[...] = jnp.zeros_like(o)
            if branch:
                outs[-1][...] = jnp.zeros_like(outs[-1])

        xv = x_ref[...]
        inv = lax.rsqrt(jnp.mean(xv * xv, axis=-1, keepdims=True) + EPS)
        r = xv * inv
        dx = dres_ref[...]
        k = 0
        for t, (_, prs) in enumerate(terms):
            dh = None
            for d, w in prs:
                d_ref, w_ref = mm_refs[2 * k], mm_refs[2 * k + 1]
                k += 1
                p_n, _, n = w.shape
                per = p_n // d.shape[0]
                for p in range(p_n):
                    part = _dot_nt(d_ref[p // per, :, (p % per) * n:(p % per + 1) * n], w_ref[p])
                    dh = part if dh is None else dh + part
            dr = dh * (1.0 + sc_refs[t][...])
            dx = dx + inv * (dr - r * jnp.mean(dr * r, axis=-1, keepdims=True))
            outs[1 + 2 * t][...] += _colsum8(dh)
            outs[2 + 2 * t][...] += _colsum8(dh * r)
        outs[0][...] = dx
        if branch:
            y_ref, g_ref = refs[n_in - 2:n_in]
            outs[-2][0] = (dx * g_ref[...]).astype(BF16)
            outs[-1][...] += _colsum8(dx * y_ref[...])

    row = pl.BlockSpec((tm, dm), lambda i: (i, 0))
    vec, acc = _full((1, dm)), _full((SUBLANES, dm))
    ins, specs = [x, dres] + [sc for sc, _ in terms], [row, row] + [vec] * len(terms)
    for d, w in pairs:
        ins += [d, w]
        specs += [pl.BlockSpec((d.shape[0], tm, d.shape[2]), lambda i: (0, i, 0)), _full(w.shape)]
    out_shape = [jax.ShapeDtypeStruct((s, dm), F32)] + [jax.ShapeDtypeStruct((SUBLANES, dm), F32)] * (2 * len(terms))
    out_specs = [row] + [acc] * (2 * len(terms))
    if branch:
        ins += list(branch)
        specs += [row, vec]
        out_shape += [jax.ShapeDtypeStruct((1, s, dm), BF16), jax.ShapeDtypeStruct((SUBLANES, dm), F32)]
        out_specs += [pl.BlockSpec((1, tm, dm), lambda i: (0, i, 0)), acc]
    outs = pl.pallas_call(body, name=name, grid=(s // tm,), in_specs=specs, out_specs=out_specs,
                          out_shape=out_shape)(*ins)
    partials = [(outs[1 + 2 * t], outs[2 + 2 * t]) for t in range(len(terms))]
    return (outs[0], partials) + ((outs[-2], outs[-1]) if branch else ())


def _conv_taps(e, w, b):
    return w[2:3] * e + w[1:2] * pltpu.roll(e, 1, 0) + w[0:1] * pltpu.roll(e, 2, 0) + b


def _ffn_specs(s, tm, cb):
    hb = tm // HALO
    last = s // HALO - 1
    main = pl.BlockSpec((2, tm, cb), lambda j, i: (0, i, j))
    prev = pl.BlockSpec((2, HALO, cb), lambda j, i: (0, jnp.maximum(i * hb - 1, 0), j))
    nxt = pl.BlockSpec((2, HALO, cb), lambda j, i: (0, jnp.minimum((i + 1) * hb, last), j))
    wspec = pl.BlockSpec((2, CONV_W, cb), lambda j, i: (0, 0, j))
    bspec = pl.BlockSpec((2, 1, cb), lambda j, i: (0, 0, j))
    return main, prev, nxt, wspec, bspec


def _convglu_bwd(u, c, dffn, w_down, w, name):
    _, s, f = u.shape
    dm = dffn.shape[2]
    tm = _tile(s, 256)
    cb = _tile(f, FFN_COLS)
    steps = s // tm
    n_ext = tm + HALO
    main, _, nxt, wspec, _ = _ffn_specs(s, tm, cb)
    hb = tm // HALO
    last = s // HALO - 1
    d_main = pl.BlockSpec((None, tm, dm), lambda j, i: (0, i, 0))
    d_next = pl.BlockSpec((None, HALO, dm), lambda j, i: (0, jnp.minimum((i + 1) * hb, last), 0))
    wd_spec = pl.BlockSpec((None, cb, dm), lambda j, i: (0, j, 0))

    def body(u_ref, c_ref, cn_ref, d_ref, dn_ref, wd_ref, w_ref, du_ref, acc_ref):
        i = pl.program_id(1)
        notlast = jnp.where(i < steps - 1, 1.0, 0.0)

        @pl.when(i == 0)
        def _():
            acc_ref[...] = jnp.zeros_like(acc_ref)

        gate, val = (jnp.concatenate([c_ref[g].astype(F32), cn_ref[g].astype(F32)], axis=0) for g in range(2))
        wd = wd_ref[...]
        da = jnp.concatenate([_dot_nt(d_ref[...], wd).astype(BF16).astype(F32),
                              _dot_nt(dn_ref[...], wd).astype(BF16).astype(F32) * notlast], axis=0)
        sg = _sig(gate)
        d_val = da * gate * sg
        d_gate = da * val * (sg * (1.0 + gate * (1.0 - sg)))

        def finish(g, d):
            wv = w_ref[g]
            d1, d2 = pltpu.roll(d, n_ext - 1, 0), pltpu.roll(d, n_ext - 2, 0)
            du_ref[g] = (wv[2:3] * d + wv[1:2] * d1 + wv[0:1] * d2)[0:tm].astype(BF16)
            uv = u_ref[g].astype(F32)
            acc_ref[g, 2] += _colsum8(d[0:tm] * uv)
            acc_ref[g, 1] += _colsum8(d1[0:tm] * uv)
            acc_ref[g, 0] += _colsum8(d2[0:tm] * uv)
            acc_ref[g, 3] += _colsum8(d[0:tm])

        finish(0, d_gate)
        finish(1, d_val)

    return pl.pallas_call(
        body, name=name, grid=(f // cb, steps),
        in_specs=[main, main, nxt, d_main, d_next, wd_spec, wspec],
        out_specs=[main, pl.BlockSpec((2, 4, SUBLANES, cb), lambda j, i: (0, 0, 0, j))],
        out_shape=[jax.ShapeDtypeStruct((2, s, f), BF16), jax.ShapeDtypeStruct((2, 4, SUBLANES, f), F32)],
    )(u, c, c, dffn, dffn, w_down, w)


def _hgrn_gates(q_raw, f_raw, lb, tri):
    sf = _sig(f_raw)
    fg = lb + (1.0 - lb) * sf
    b = _tri_dot(tri, jnp.log(fg))
    return q_raw * _sig(q_raw), 1.0 - fg, b, fg, sf


def _hgrn_fwd(proj, lb, norm_g, name):
    s = proj.shape[0]
    tb = _tile(s, HGRN_ROWS)
    n_c = tb // A_CHUNK
    half = A_CHUNK // 2

    def body(q_ref, f_ref, v_ref, g_ref, lb_ref, ng_ref, o_ref, yp_ref, st_ref, state):
        @pl.when(pl.program_id(0) == 0)
        def _():
            state[...] = jnp.zeros_like(state)

        tri = _tri(A_CHUNK)
        causal = lax.broadcasted_iota(jnp.int32, (A_CHUNK, A_CHUNK), 1) <= lax.broadcasted_iota(
            jnp.int32, (A_CHUNK, A_CHUNK), 0)

        def chunk(ci, carry):
            rows = pl.ds(ci * A_CHUNK, A_CHUNK)
            heads = [slice(h * HEAD_DIM, (h + 1) * HEAD_DIM) for h in range(HEADS)]
            qs, k, b, _, _ = _hgrn_gates(q_ref[rows, :], f_ref[rows, :], lb_ref[...], tri)
            b_mid, b_last = b[half:half + 1], b[A_CHUNK - 1:A_CHUNK]
            q_i = (qs * jnp.exp(b - b_mid)).astype(BF16)
            k_i = (k * jnp.exp(b_mid - b)).astype(BF16)
            q_e = (qs * jnp.exp(b)).astype(BF16)
            k_s = (k * jnp.exp(b_last - b)).astype(BF16)
            decay = jnp.exp(b_last)
            vb = v_ref[rows, :].astype(BF16)
            scores = [jnp.where(causal, _dot_nt(q_i[:, cs], k_i[:, cs]), 0.0).astype(BF16) for cs in heads]
            st = [state[h] for h in range(HEADS)]
            outs = [_dot(scores[h], vb[:, cs]) + _dot_nt(q_e[:, cs], st[h].astype(BF16)) for h, cs in enumerate(heads)]
            for h, cs in enumerate(heads):
                st_ref[ci, h] = st[h]
                state[h] = st[h] * decay[:, cs] + _dot_tn(vb[:, cs], k_s[:, cs])
            o = jnp.concatenate(outs, axis=1)
            o_ref[rows, :] = o
            sq = o * o
            inv = jnp.concatenate([jnp.broadcast_to(lax.rsqrt(jnp.mean(sq[:, cs], axis=-1, keepdims=True) + EPS),
                                                    (A_CHUNK, HEAD_DIM)) for cs in heads], axis=1)
            g_raw = g_ref[rows, :]
            yp_ref[rows, :] = (o * inv * ng_ref[...] * (g_raw * _sig(g_raw))).astype(BF16)
            return carry

        for step in range(n_c):
            chunk(step, 0)

    col = lambda j: pl.BlockSpec((tb, D_MODEL), lambda i: (i, j))
    vec = _full((1, D_MODEL))
    return pl.pallas_call(
        body, name=name, grid=(s // tb,), in_specs=[col(0), col(1), col(2), col(3), vec, vec],
        out_specs=[col(0), col(0), pl.BlockSpec((n_c, HEADS, HEAD_DIM, HEAD_DIM), lambda i: (i, 0, 0, 0))],
        out_shape=[jax.ShapeDtypeStruct((s, D_MODEL), F32), jax.ShapeDtypeStruct((s, D_MODEL), BF16),
                   jax.ShapeDtypeStruct((s // A_CHUNK, HEADS, HEAD_DIM, HEAD_DIM), F32)],
        scratch_shapes=[pltpu.VMEM((HEADS, HEAD_DIM, HEAD_DIM), F32)],
    )(proj, proj, proj, proj, lb, norm_g)


def _hgrn_bwd(proj, lb, norm_g, o, states, dout, w_out, name):
    s = proj.shape[0]
    tb = _tile(s, HGRN_ROWS)
    n_c = tb // A_CHUNK
    n_b = s // tb
    half = A_CHUNK // 2

    def body(q_ref, f_ref, v_ref, g_ref, lb_ref, ng_ref, o_ref, st_ref, dout_ref, w_ref, dp_ref, dlb_ref, dng_ref,
             dstate, dyp_ref):
        @pl.when(pl.program_id(0) == 0)
        def _():
            dstate[...] = jnp.zeros_like(dstate)
            dlb_ref[...] = jnp.zeros_like(dlb_ref)
            dng_ref[...] = jnp.zeros_like(dng_ref)

        dyp_ref[...] = _dot_nt(dout_ref[0], w_ref[0])

        tri = _tri(A_CHUNK)
        tri_up = _tri(A_CHUNK, upper=True)
        row_id = lax.broadcasted_iota(jnp.int32, (A_CHUNK, D_MODEL), 0)
        causal = lax.broadcasted_iota(jnp.int32, (A_CHUNK, A_CHUNK), 1) <= lax.broadcasted_iota(
            jnp.int32, (A_CHUNK, A_CHUNK), 0)

        def chunk(cj, carry):
            ci = n_c - 1 - cj
            rows = pl.ds(ci * A_CHUNK, A_CHUNK)
            heads = [slice(h * HEAD_DIM, (h + 1) * HEAD_DIM) for h in range(HEADS)]
            cat = lambda parts: jnp.concatenate(parts, axis=1)
            per_head_mean = lambda a: cat([jnp.broadcast_to(jnp.mean(a[:, cs], axis=-1, keepdims=True),
                                                            (A_CHUNK, HEAD_DIM)) for cs in heads])
            q_raw, lbv = q_ref[rows, :], lb_ref[...]
            qs, k, b, fg, sf = _hgrn_gates(q_raw, f_ref[rows, :], lbv, tri)
            b_mid, b_last = b[half:half + 1], b[A_CHUNK - 1:A_CHUNK]
            e_qi, e_ki, e_q, e_ks = jnp.exp(b - b_mid), jnp.exp(b_mid - b), jnp.exp(b), jnp.exp(b_last - b)
            decay = jnp.exp(b_last)
            q_i, k_i, q_e, k_s = qs * e_qi, k * e_ki, qs * e_q, k * e_ks
            qib, kib, qeb, ksb = q_i.astype(BF16), k_i.astype(BF16), q_e.astype(BF16), k_s.astype(BF16)
            vb = v_ref[rows, :].astype(BF16)
            ov, g_raw, dy, ng = o_ref[rows, :], g_ref[rows, :], dyp_ref[rows, :], ng_ref[...]
            inv = lax.rsqrt(per_head_mean(ov * ov) + EPS)
            nrm = ov * inv
            sg = _sig(g_raw)
            gs = g_raw * sg
            dn = dy * ng * gs
            dng_ref[0:1, :] += jnp.sum(dy * nrm * gs, axis=0, keepdims=True)
            dg_raw = dy * nrm * ng * (sg * (1.0 + g_raw * (1.0 - sg)))
            do = (inv * (dn - nrm * per_head_mean(dn * nrm))).astype(BF16)
            st_prev = [st_ref[ci, h] for h in range(HEADS)]
            dst = [dstate[h] for h in range(HEADS)]
            dstb = [d.astype(BF16) for d in dst]
            scores = [jnp.where(causal, _dot_nt(qib[:, cs], kib[:, cs]), 0.0).astype(BF16) for cs in heads]
            d_scores = [jnp.where(causal, _dot_nt(do[:, cs], vb[:, cs]), 0.0).astype(BF16) for cs in heads]
            dv = cat([_dot_tn(scores[h], do[:, cs]) + _dot_nt(ksb[:, cs], dstb[h]) for h, cs in enumerate(heads)])
            dq_i = cat([_dot(d_scores[h], kib[:, cs]) for h, cs in enumerate(heads)])
            dk_i = cat([_dot_tn(d_scores[h], qib[:, cs]) for h, cs in enumerate(heads)])
            dq_e = cat([_dot(do[:, cs], st_prev[h].astype(BF16)) for h, cs in enumerate(heads)])
            dk_s = cat([_dot(vb[:, cs], dstb[h]) for h, cs in enumerate(heads)])
            d_decay = cat([jnp.sum(st_prev[h] * dst[h], axis=0, keepdims=True) for h in range(HEADS)])
            for h, cs in enumerate(heads):
                dstate[h] = dst[h] * decay[:, cs] + _dot_tn(do[:, cs], qeb[:, cs])
            dq = dq_i * e_qi + dq_e * e_q
            dk = dk_i * e_ki + dk_s * e_ks
            t_qi, t_ki, t_ks = dq_i * q_i, dk_i * k_i, dk_s * k_s
            db = t_qi - t_ki + dq_e * q_e - t_ks
            db_mid = jnp.sum(t_ki - t_qi, axis=0, keepdims=True)
            db_last = jnp.sum(t_ks, axis=0, keepdims=True) + d_decay * decay
            db = db + jnp.where(row_id == half, db_mid, 0.0) + jnp.where(row_id == A_CHUNK - 1, db_last, 0.0)
            dfg = _tri_dot(tri_up, db) / fg - dk
            dlb_ref[0:1, :] += jnp.sum(dfg * (1.0 - sf), axis=0, keepdims=True)
            sq = _sig(q_raw)
            dp_ref[0, rows, :] = (dq * (sq * (1.0 + q_raw * (1.0 - sq)))).astype(BF16)
            dp_ref[1, rows, :] = (dfg * (1.0 - lbv) * sf * (1.0 - sf)).astype(BF16)
            dp_ref[2, rows, :] = dv.astype(BF16)
            dp_ref[3, rows, :] = dg_raw.astype(BF16)
            return carry

        for step in range(n_c):
            chunk(step, 0)

    col = lambda j: pl.BlockSpec((tb, D_MODEL), lambda i: (n_b - 1 - i, j))
    vec = _full((1, D_MODEL))
    acc = _full((SUBLANES, D_MODEL))
    return pl.pallas_call(
        body, name=name, grid=(n_b,),
        in_specs=[col(0), col(1), col(2), col(3), vec, vec, col(0),
                  pl.BlockSpec((n_c, HEADS, HEAD_DIM, HEAD_DIM), lambda i: (n_b - 1 - i, 0, 0, 0)),
                  pl.BlockSpec((1, tb, D_MODEL), lambda i: (0, n_b - 1 - i, 0)), _full(w_out.shape)],
        out_specs=[pl.BlockSpec((4, tb, D_MODEL), lambda i: (0, n_b - 1 - i, 0)), acc, acc],
        out_shape=[jax.ShapeDtypeStruct((4, s, D_MODEL), BF16), jax.ShapeDtypeStruct((SUBLANES, D_MODEL), F32),
                   jax.ShapeDtypeStruct((SUBLANES, D_MODEL), F32)],
        scratch_shapes=[pltpu.VMEM((HEADS, HEAD_DIM, HEAD_DIM), F32), pltpu.VMEM((tb, D_MODEL), F32)],
    )(proj, proj, proj, proj, lb, norm_g, o, states, dout, w_out)


def _head_rms(raw_ref, g_ref, mult, y_ref):
    for h in range(HEADS):
        cs = slice(h * HEAD_DIM, (h + 1) * HEAD_DIM)
        xv = raw_ref[:, cs]
        inv = lax.rsqrt(jnp.mean(xv * xv, axis=-1, keepdims=True) + EPS)
        y_ref[:, cs] = (xv * inv * g_ref[:, cs] * mult).astype(BF16)


def _proj_headnorm(a, w, g, mult, name):
    s, k = a.shape
    p_n, _, n = w.shape
    tm = _tile(s, ROW_TILE)

    def body(a_ref, w_ref, g_ref, raw_ref, y_ref):
        av = a_ref[...]
        for p in range(p_n):
            raw_ref[:, p * n:(p + 1) * n] = _dot(av, w_ref[p])
        _head_rms(raw_ref, g_ref, mult, y_ref)

    row = lambda wid: pl.BlockSpec((tm, wid), lambda i: (i, 0))
    return pl.pallas_call(
        body, name=name, grid=(s // tm,), in_specs=[row(k), _full(w.shape), _full((1, D_MODEL))],
        out_specs=[row(p_n * n), row(D_MODEL)],
        out_shape=[jax.ShapeDtypeStruct((s, p_n * n), F32), jax.ShapeDtypeStruct((s, D_MODEL), BF16)],
    )(a, w, g)


def _kv_proj(hk, w_k, w_v, w_f, g, name):
    s, k = hk.shape
    tm = _tile(s, ROW_TILE)

    def body(h_ref, wk_ref, wv_ref, wf_ref, g_ref, kr_ref, k_ref, v_ref, f_ref):
        hv = h_ref[...]
        kr_ref[...] = _dot(hv, wk_ref[0])
        v_ref[...] = _dot(hv, wv_ref[0]).astype(BF16)
        f_ref[...] = _dot(hv, wf_ref[0])
        _head_rms(kr_ref, g_ref, 1.0, k_ref)

    row = lambda wid: pl.BlockSpec((tm, wid), lambda i: (i, 0))
    return pl.pallas_call(
        body, name=name, grid=(s // tm,),
        in_specs=[row(k), _full(w_k.shape), _full(w_v.shape), _full(w_f.shape), _full((1, D_MODEL))],
        out_specs=[row(D_MODEL), row(D_MODEL), row(D_MODEL), row(LANES)],
        out_shape=[jax.ShapeDtypeStruct((s, D_MODEL), F32), jax.ShapeDtypeStruct((s, D_MODEL), BF16),
                   jax.ShapeDtypeStruct((s, D_MODEL), BF16), jax.ShapeDtypeStruct((s, LANES), F32)],
    )(hk, w_k, w_v, w_f, g)


def _headnorm_bwd(x, g, mult, dy, name, col0=0, extra=None):
    s = x.shape[0]
    tm = _tile(s, ROW_TILE)
    groups = 2 if extra is not None else 1
    head_major = dy.ndim == 3

    def body(*refs):
        x_ref, g_ref, dy_ref = refs[:3]
        dx_ref, dg_ref = refs[-2:]

        @pl.when(pl.program_id(0) == 0)
        def _():
            dg_ref[...] = jnp.zeros_like(dg_ref)

        for h in range(HEADS):
            cs = slice(h * HEAD_DIM, (h + 1) * HEAD_DIM)
            xv, gv = x_ref[:, cs], g_ref[:, cs]
            dyv = dy_ref[h, :, 0:HEAD_DIM] if head_major else dy_ref[:, cs]
            inv = lax.rsqrt(jnp.mean(xv * xv, axis=-1, keepdims=True) + EPS)
            nrm = xv * inv
            dn = dyv * gv * mult
            dg_ref[:, cs] += _colsum8(dyv * nrm * mult)
            dx_ref[0, :, cs] = (inv * (dn - nrm * jnp.mean(dn * nrm, axis=-1, keepdims=True))).astype(BF16)
        if extra is not None:
            dx_ref[1] = refs[3][...]

    row = pl.BlockSpec((tm, D_MODEL), lambda i: (i, 0))
    dy_spec = pl.BlockSpec((HEADS, tm, dy.shape[-1]), lambda i: (0, i, 0)) if head_major else row
    ins = [x, g, dy] + ([extra] if extra is not None else [])
    specs = ([pl.BlockSpec((tm, D_MODEL), lambda i: (i, col0)), _full((1, D_MODEL)), dy_spec]
             + ([row] if extra is not None else []))
    return pl.pallas_call(
        body, name=name, grid=(s // tm,), in_specs=specs,
        out_specs=[pl.BlockSpec((groups, tm, D_MODEL), lambda i: (0, i, 0)), _full((SUBLANES, D_MODEL))],
        out_shape=[jax.ShapeDtypeStruct((groups, s, D_MODEL), BF16), jax.ShapeDtypeStruct((SUBLANES, D_MODEL), F32)],
    )(*ins)


def _log_sigmoid(z):
    return jnp.minimum(z, 0.0) - jnp.log(1.0 + jnp.exp(-jnp.abs(z)))


Q_CUM, Q_ONE, Q_LSE = 0, 3, 6
LOG2E = 1.4426950408889634


def _pieces(v):
    hi = v.astype(BF16).astype(F32)
    mid = (v - hi).astype(BF16).astype(F32)
    lo = ((v - hi) - mid).astype(BF16).astype(F32)
    return hi, mid, lo


def _side(lane, at, v):
    hi, mid, lo = _pieces(v)
    return jnp.where(lane == at, hi, jnp.where(lane == at + 1, mid, jnp.where(lane == at + 2, lo, 0.0)))


def _fcum_fwd(f, bias, name):
    s = f.shape[0]
    tm = _tile(s, ROW_TILE)

    def body(f_ref, b_ref, qa_ref, ka_ref, carry):
        @pl.when(pl.program_id(0) == 0)
        def _():
            carry[...] = jnp.zeros_like(carry)

        cum = _tri_dot(_tri(tm), _log_sigmoid(f_ref[...] + b_ref[...])) + carry[...]
        carry[...] = cum[tm - 1:tm]
        lane = lax.broadcasted_iota(jnp.int32, (tm, LANES), 1)
        ones_q = jnp.where((lane >= Q_ONE) & (lane < Q_LSE), 1.0, 0.0)
        ones_k = jnp.where((lane < Q_ONE) | ((lane >= Q_LSE) & (lane < Q_LSE + 3)), 1.0, 0.0)
        for h in range(HEADS):
            c2 = cum[:, h:h + 1] * LOG2E
            qa_ref[h] = (_side(lane, Q_CUM, c2) + ones_q).astype(BF16)
            ka_ref[h] = (_side(lane, Q_ONE, -c2) + ones_k).astype(BF16)

    side = pl.BlockSpec((HEADS, tm, LANES), lambda i: (0, i, 0))
    return pl.pallas_call(
        body, name=name, grid=(s // tm,),
        in_specs=[pl.BlockSpec((tm, LANES), lambda i: (i, 0)), _full((1, LANES))],
        out_specs=[side, side],
        out_shape=[jax.ShapeDtypeStruct((HEADS, s, LANES), BF16)] * 2,
        scratch_shapes=[pltpu.VMEM((1, LANES), F32)],
    )(f, bias)


def _fcum_bwd(f, bias, dka, dcq, name):
    s = f.shape[0]
    tm = _tile(s, ROW_TILE)
    n_b = s // tm

    def body(f_ref, b_ref, dka_ref, dcq_ref, dz_ref, db_ref, carry):
        @pl.when(pl.program_id(0) == 0)
        def _():
            carry[...] = jnp.zeros_like(carry)
            db_ref[...] = jnp.zeros_like(db_ref)

        lane = lax.broadcasted_iota(jnp.int32, (tm, LANES), 1)
        rows = jnp.concatenate([dcq_ref[h] for h in range(HEADS)] + [jnp.zeros((LANES - HEADS, tm), F32)], axis=0)
        dcum = rows.T
        for h in range(HEADS):
            dcum = dcum - jnp.where(lane == h, dka_ref[h, :, Q_ONE:Q_ONE + 1], 0.0)
        dlf = _tri_dot(_tri(tm, upper=True), dcum) + carry[...]
        carry[...] = dlf[0:1]
        dz = dlf * _sig(-(f_ref[...] + b_ref[...]))
        dz_ref[0] = dz.astype(BF16)
        db_ref[...] += _colsum8(dz)

    return pl.pallas_call(
        body, name=name, grid=(n_b,),
        in_specs=[pl.BlockSpec((tm, LANES), lambda i: (n_b - 1 - i, 0)), _full((1, LANES)),
                  pl.BlockSpec((HEADS, tm, LANES), lambda i: (0, n_b - 1 - i, 0)),
                  pl.BlockSpec((HEADS, 1, tm), lambda i: (0, 0, n_b - 1 - i))],
        out_specs=[pl.BlockSpec((1, tm, LANES), lambda i: (0, n_b - 1 - i, 0)), _full((SUBLANES, LANES))],
        out_shape=[jax.ShapeDtypeStruct((1, s, LANES), BF16), jax.ShapeDtypeStruct((SUBLANES, LANES), F32)],
        scratch_shapes=[pltpu.VMEM((1, LANES), F32)],
    )(f, bias, dka, dcq)


def _causal_pairs(n_t, key_major):
    if key_major:
        pairs = [(qi, ki) for ki in range(n_t) for qi in range(ki, n_t)]
    else:
        pairs = [(qi, ki) for qi in range(n_t) for ki in range(qi + 1)]
    return (jnp.array([p[0] for p in pairs], jnp.int32), jnp.array([p[1] for p in pairs], jnp.int32))


def _lane_const(t, lo, hi, value):
    lane = lax.broadcasted_iota(jnp.int32, (t, LANES), 1)
    return jnp.where((lane >= lo) & (lane < hi), value, 0.0).astype(BF16)


def _att_specs(t, nh):
    qmain = pl.BlockSpec((t, nh * HEAD_DIM), lambda h, p, qt, kt: (qt[p], h))
    kmain = pl.BlockSpec((t, nh * HEAD_DIM), lambda h, p, qt, kt: (kt[p], h))
    qside = pl.BlockSpec((nh, t, LANES), lambda h, p, qt, kt: (h, qt[p], 0))
    kside = pl.BlockSpec((nh, t, LANES), lambda h, p, qt, kt: (h, kt[p], 0))
    return qmain, kmain, qside, kside


def _fox_fwd(q, qa, k, ka, v, qo, name):
    s = q.shape[0]
    t = _tile(s, ATT_TILE)
    sub = t // ATT_SPLIT
    nh = ATT_FWD_HEADS
    qt, kt = _causal_pairs(s // t, key_major=False)

    def body(qt_ref, kt_ref, q_ref, qa_ref, k_ref, ka_ref, v_ref, og_ref, o_ref, y_ref, qab_ref, m_s, l_s, acc_s):
        pid = pl.program_id(1)
        qi, ki = qt_ref[pid], kt_ref[pid]

        @pl.when(ki == 0)
        def _():
            m_s[...] = jnp.full_like(m_s, NEG_INF)
            l_s[...] = jnp.zeros_like(l_s)
            acc_s[...] = jnp.zeros_like(acc_s)

        def step(diagonal):
            for hh in range(nh):
                hc = slice(hh * HEAD_DIM, (hh + 1) * HEAD_DIM)
                kc = jnp.concatenate([k_ref[:, hc], ka_ref[hh]], axis=1)
                vc = jnp.concatenate([v_ref[:, hc], _lane_const(t, 0, 1, 1.0)], axis=1)
                for r in range(ATT_SPLIT):
                    rows = slice(r * sub, (r + 1) * sub)
                    n_k = (r + 1) * sub if diagonal else t
                    sc = _dot_nt(jnp.concatenate([q_ref[rows, hc], qa_ref[hh, rows]], axis=1), kc[:n_k])
                    if diagonal:
                        sc = jnp.where(lax.broadcasted_iota(jnp.int32, (sub, n_k), 1)
                                       <= lax.broadcasted_iota(jnp.int32, (sub, n_k), 0) + r * sub, sc, NEG_INF)
                    m_old = m_s[hh, rows]
                    m_new = jnp.maximum(m_old, jnp.max(sc, axis=-1, keepdims=True))
                    alpha = jnp.exp2(m_old - m_new)
                    pv = _dot(jnp.exp2(sc - m_new[:, 0:1]).astype(BF16), vc[:n_k])
                    acc_s[hh, rows] = alpha * acc_s[hh, rows] + pv[:, :HEAD_DIM]
                    l_s[hh, rows] = alpha * l_s[hh, rows] + pv[:, HEAD_DIM:]
                    m_s[hh, rows] = m_new

        @pl.when(ki < qi)
        def _():
            step(False)

        @pl.when(ki == qi)
        def _():
            step(True)
            lane = lax.broadcasted_iota(jnp.int32, (t, LANES), 1)
            for hh in range(nh):
                hc = slice(hh * HEAD_DIM, (hh + 1) * HEAD_DIM)
                l = l_s[hh, :, 0:1]
                o = acc_s[hh] / l
                o_ref[:, hc] = o
                y_ref[:, hc] = (o * _sig(og_ref[:, hc])).astype(BF16)
                qab_ref[hh] = qa_ref[hh] + _side(lane, Q_LSE, -(m_s[hh, :, 0:1] + jnp.log2(l))).astype(BF16)

    qmain, kmain, qside, kside = _att_specs(t, nh)
    return pl.pallas_call(
        body, name=name,
        grid_spec=pltpu.PrefetchScalarGridSpec(
            num_scalar_prefetch=2, grid=(HEADS // nh, qt.shape[0]),
            in_specs=[qmain, qside, kmain, kside, kmain,
                      pl.BlockSpec((t, nh * HEAD_DIM), lambda h, p, qt, kt: (qt[p], HEADS // nh + h))],
            out_specs=[qmain, qmain, qside],
            scratch_shapes=[pltpu.VMEM((nh, t, LANES), F32), pltpu.VMEM((nh, t, LANES), F32),
                            pltpu.VMEM((nh, t, HEAD_DIM), F32)]),
        out_shape=[jax.ShapeDtypeStruct((s, D_MODEL), F32), jax.ShapeDtypeStruct((s, D_MODEL), BF16),
                   jax.ShapeDtypeStruct((HEADS, s, LANES), BF16)],
    )(qt, kt, q, qa, k, ka, v, qo)


def _fox_gate_bwd(o, qo, dout, w_out, name):
    s = o.shape[0]
    tm = _tile(s, ROW_TILE)

    def body(o_ref, og_ref, dout_ref, w_ref, do_ref, dg_ref, dl_ref):
        ov, dyv = o_ref[...], _dot_nt(dout_ref[0], w_ref[0])
        sg = _sig(og_ref[...])
        do = (dyv * sg).astype(BF16)
        do_ref[...] = do
        dg_ref[...] = (dyv * ov * sg * (1.0 - sg)).astype(BF16)
        prod = do.astype(F32) * ov
        lane = lax.broadcasted_iota(jnp.int32, (tm, LANES), 1)
        for h in range(HEADS):
            delta = jnp.sum(prod[:, h * HEAD_DIM:(h + 1) * HEAD_DIM], axis=-1, keepdims=True)
            dl_ref[h] = _side(lane, 0, delta).astype(BF16)

    row = pl.BlockSpec((tm, D_MODEL), lambda i: (i, 0))
    return pl.pallas_call(
        body, name=name, grid=(s // tm,),
        in_specs=[row, pl.BlockSpec((tm, D_MODEL), lambda i: (i, 1)),
                  pl.BlockSpec((1, tm, D_MODEL), lambda i: (0, i, 0)), _full(w_out.shape)],
        out_specs=[row, row, pl.BlockSpec((HEADS, tm, LANES), lambda i: (0, i, 0))],
        out_shape=[jax.ShapeDtypeStruct((s, D_MODEL), BF16), jax.ShapeDtypeStruct((s, D_MODEL), BF16),
                   jax.ShapeDtypeStruct((HEADS, s, LANES), BF16)],
    )(o, qo, dout, w_out)


def _fox_bwd(q, qab, k, ka, v, do, doa, k_raw, k_gain, name):
    s = q.shape[0]
    t = _tile(s, ATT_TILE)
    n_t = s // t
    sub = t // ATT_SPLIT
    nh = ATT_BWD_HEADS
    qt, kt = _causal_pairs(n_t, key_major=True)

    def body(qt_ref, kt_ref, q_ref, qab_ref, k_ref, ka_ref, v_ref, do_ref, doa_ref, kr_ref, kg_ref, dkr_ref, dkg_ref,
             dv_ref, dka_ref, dq_hbm, dcq_hbm, dk_s, dv_s, dq_ref, dcq_ref):
        group, pid = pl.program_id(0), pl.program_id(1)
        qi, ki = qt_ref[pid], kt_ref[pid]

        @pl.when(pid == 0)
        def _():
            dq_ref[...] = jnp.zeros_like(dq_ref)
            dcq_ref[...] = jnp.zeros_like(dcq_ref)
            dkg_ref[...] = jnp.zeros_like(dkg_ref)

        @pl.when(qi == ki)
        def _():
            dk_s[...] = jnp.zeros_like(dk_s)
            dv_s[...] = jnp.zeros_like(dv_s)

        def step(diagonal):
            for hh in range(nh):
                hc = slice(hh * HEAD_DIM, (hh + 1) * HEAD_DIM)
                kc = jnp.concatenate([k_ref[:, hc], ka_ref[hh]], axis=1)
                vc = jnp.concatenate([v_ref[:, hc], _lane_const(t, 0, 3, -1.0)], axis=1)
                for r in range(ATT_SPLIT):
                    cols = slice(r * sub, (r + 1) * sub)
                    n_k = (r + 1) * sub if diagonal else t
                    qc = jnp.concatenate([q_ref[cols, hc], qab_ref[hh, cols]], axis=1)
                    sc = _dot_nt(kc[:n_k], qc)
                    if diagonal:
                        sc = jnp.where(lax.broadcasted_iota(jnp.int32, (n_k, sub), 0)
                                       <= lax.broadcasted_iota(jnp.int32, (n_k, sub), 1) + r * sub, sc, NEG_INF)
                    p = jnp.exp2(sc)
                    dov = do_ref[cols, hc]
                    dp = _dot_nt(vc[:n_k], jnp.concatenate([dov, doa_ref[hh, cols]], axis=1))
                    ds = (p * dp).astype(BF16)
                    dv_s[hh, 0:n_k] += _dot(p.astype(BF16), dov)
                    dk_s[hh, 0:n_k] += _dot(ds, qc)
                    q_rows = pl.ds(pl.multiple_of(qi * t + r * sub, sub), sub)
                    dq_ref[hh, q_rows, :] += _dot_tn(ds, k_ref[0:n_k, hc])
                    dcq_ref[hh, qi * ATT_SPLIT + r] += jnp.sum(ds.astype(F32), axis=0, keepdims=True)

        @pl.when(qi > ki)
        def _():
            step(False)

        @pl.when(qi == ki)
        def _():
            step(True)

        @pl.when(qi == n_t - 1)
        def _():
            for hh in range(nh):
                hc = slice(hh * HEAD_DIM, (hh + 1) * HEAD_DIM)
                dka_ref[hh] = dk_s[hh, :, HEAD_DIM:]
                dv_ref[:, hc] = dv_s[hh].astype(BF16)
                dk = dk_s[hh, :, :HEAD_DIM] * (1.0 / LOG2E)
                xv = kr_ref[:, hc]
                inv = lax.rsqrt(jnp.mean(xv * xv, axis=-1, keepdims=True) + EPS)
                nrm = xv * inv
                dn = dk * kg_ref[:, hc]
                dkg_ref[:, hc] += _colsum8(dk * nrm)
                dkr_ref[:, hc] = (inv * (dn - nrm * jnp.mean(dn * nrm, axis=-1, keepdims=True))).astype(BF16)

        @pl.when(pid == qt.shape[0] - 1)
        def _():
            pltpu.sync_copy(dq_ref, dq_hbm.at[pl.ds(group * nh, nh)])
            pltpu.sync_copy(dcq_ref, dcq_hbm.at[pl.ds(group * nh, nh)])

    qmain, kmain, qside, kside = _att_specs(t, nh)
    kmain3 = pl.BlockSpec((None, t, nh * HEAD_DIM), lambda h, p, qt, kt: (0, kt[p], h))
    in_hbm = pl.BlockSpec(memory_space=pltpu.HBM)
    return pl.pallas_call(
        body, name=name,
        grid_spec=pltpu.PrefetchScalarGridSpec(
            num_scalar_prefetch=2, grid=(HEADS // nh, qt.shape[0]),
            in_specs=[qmain, qside, kmain, kside, kmain, qmain, qside, kmain,
                      pl.BlockSpec((1, nh * HEAD_DIM), lambda h, p, qt, kt: (0, h))],
            out_specs=[kmain3, pl.BlockSpec((SUBLANES, nh * HEAD_DIM), lambda h, p, qt, kt: (0, h)), kmain3, kside,
                       in_hbm, in_hbm],
            scratch_shapes=[pltpu.VMEM((nh, t, 2 * HEAD_DIM), F32), pltpu.VMEM((nh, t, HEAD_DIM), F32),
                            pltpu.VMEM((nh, s, HEAD_DIM), F32), pltpu.VMEM((nh, s // sub, 1, sub), F32)]),
        out_shape=[jax.ShapeDtypeStruct((1, s, D_MODEL), BF16), jax.ShapeDtypeStruct((SUBLANES, D_MODEL), F32),
                   jax.ShapeDtypeStruct((1, s, D_MODEL), BF16),
                   jax.ShapeDtypeStruct((HEADS, s, LANES), F32), jax.ShapeDtypeStruct((HEADS, s, HEAD_DIM), F32),
                   jax.ShapeDtypeStruct((HEADS, s // sub, 1, sub), F32)],
    )(qt, kt, q, qab, k, ka, v, do, doa, k_raw, k_gain)


def _mm_residual_premix(a, w, x, gate, mods, name):
    s, k = a.shape
    dm = x.shape[1]
    tm = _tile(s, ROW_TILE)

    def body(*refs):
        a_ref, w_ref, x_ref, g_ref = refs[:4]
        mod_refs = refs[4:4 + 2 * len(mods)]
        y_ref, xn_ref = refs[4 + 2 * len(mods):6 + 2 * len(mods)]
        h_refs = refs[6 + 2 * len(mods):]
        y = _dot(a_ref[...], w_ref[0])
        y_ref[...] = y
        xv = x_ref[...] + g_ref[...] * y
        xn_ref[...] = xv
        nrm = xv * lax.rsqrt(jnp.mean(xv * xv, axis=-1, keepdims=True) + EPS)
        for t, h_ref in enumerate(h_refs):
            h_ref[...] = (nrm * (1.0 + mod_refs[2 * t + 1][...]) + mod_refs[2 * t][...]).astype(BF16)

    row = pl.BlockSpec((tm, dm), lambda i: (i, 0))
    vec = _full((1, dm))
    outs = pl.pallas_call(
        body, name=name, grid=(s // tm,),
        in_specs=[pl.BlockSpec((tm, k), lambda i: (i, 0)), _full(w.shape), row, vec] + [vec] * (2 * len(mods)),
        out_specs=[row] * (2 + len(mods)),
        out_shape=[jax.ShapeDtypeStruct((s, dm), F32)] * 2 + [jax.ShapeDtypeStruct((s, dm), BF16)] * len(mods),
    )(a, w, x, gate, *[v for m in mods for v in m])
    return outs[0], outs[1], list(outs[2:])


def _mm_loss_head(a, w, x, gate, target, name):
    s, k = a.shape
    dm = x.shape[1]
    tm = _tile(s, ROW_TILE)

    def body(a_ref, w_ref, x_ref, g_ref, t_ref, sq_ref, do_ref, dy_ref, dg_ref):
        @pl.when(pl.program_id(0) == 0)
        def _():
            sq_ref[...] = jnp.zeros_like(sq_ref)
            dg_ref[...] = jnp.zeros_like(dg_ref)

        y, gv = _dot(a_ref[...], w_ref[0]), g_ref[...]
        err = x_ref[...] + gv * y - t_ref[...]
        sq_ref[...] += _colsum8(err * err)
        dout = err * (1.0 / dm)
        do_ref[...] = dout
        dy_ref[0] = (dout * gv).astype(BF16)
        dg_ref[...] += _colsum8(dout * y)

    row = pl.BlockSpec((tm, dm), lambda i: (i, 0))
    acc = _full((SUBLANES, dm))
    return pl.pallas_call(
        body, name=name, grid=(s // tm,),
        in_specs=[pl.BlockSpec((tm, k), lambda i: (i, 0)), _full(w.shape), row, _full((1, dm)), row],
        out_specs=[acc, row, pl.BlockSpec((1, tm, dm), lambda i: (0, i, 0)), acc],
        out_shape=[jax.ShapeDtypeStruct((SUBLANES, dm), F32), jax.ShapeDtypeStruct((s, dm), F32),
                   jax.ShapeDtypeStruct((1, s, dm), BF16), jax.ShapeDtypeStruct((SUBLANES, dm), F32)],
    )(a, w, x, gate, target)


def _ffn_inner(h, w_up, conv_w, conv_b, tag):
    s, dm = h.shape
    half = w_up.shape[2]
    f = 2 * half
    tm = _tile(s, FFN_ROWS)

    def body(h_ref, w_ref, cw_ref, cb_ref, u_ref, c_ref, a_ref, carry):
        @pl.when(pl.program_id(0) == 0)
        def _():
            carry[...] = jnp.zeros_like(carry)

        hv = h_ref[...]
        for j in range(2):
            cols = slice(j * half, (j + 1) * half)
            conv = []
            for g in range(2):
                ub = _dot(hv, w_ref[2 * g + j]).astype(BF16)
                u_ref[g, :, cols] = ub
                uf = ub.astype(F32)
                e = jnp.concatenate([carry[g, j], uf], axis=0)
                carry[g, j] = uf[tm - SUBLANES:tm]
                conv.append(_conv_taps(e, cw_ref[g][:, cols], cb_ref[g][:, cols])[SUBLANES:])
                c_ref[g, :, cols] = conv[g].astype(BF16)
            a_ref[:, cols] = (conv[0] * _sig(conv[0]) * conv[1]).astype(BF16)

    pair = pl.BlockSpec((2, tm, f), lambda i: (0, i, 0))
    return pl.pallas_call(
        body, name=tag + "_up_convglu", grid=(s // tm,),
        in_specs=[pl.BlockSpec((tm, dm), lambda i: (i, 0)), _full(w_up.shape), _full(conv_w.shape), _full(conv_b.shape)],
        out_specs=[pair, pair, pl.BlockSpec((tm, f), lambda i: (i, 0))],
        out_shape=[jax.ShapeDtypeStruct((2, s, f), BF16)] * 2 + [jax.ShapeDtypeStruct((s, f), BF16)],
        scratch_shapes=[pltpu.VMEM((2, 2, SUBLANES, half), F32)],
    )(h, w_up, conv_w, conv_b)


def _weight_grad_first(a, d, p_n, name):
    return lax.optimization_barrier((_mm_tn(a, d, p_n, name), d))


def _ffn_backward(dx_out, dffn, x_mid, scale, saved, w_up, conv_w, conv_b, w_down, mixer, tag):
    h, u, c, a = saved
    dw_down, dffn = _weight_grad_first(a, dffn, 1, tag + "_down_dw")
    du, dconv = _convglu_bwd(u, c, dffn, w_down, conv_w, tag + "_convglu_bwd")
    dw_up, du = _weight_grad_first(h, du, N_CHIPS, tag + "_up_dw")
    dx_mid, [(dshift, dscale)], dy, dgate_mixer = _premix_bwd(x_mid, [(scale, [(du, w_up)])], dx_out,
                                                              tag + "_premix_bwd", branch=mixer)
    return dx_mid, dy, dgate_mixer, dw_up, dw_down, dict(shift=dshift, scale=dscale, conv=dconv)


def _local_step(x, target, mods, lb, vecs, weights_at):
    m0, m1, mk = mods["l0"], mods["l1"], mods["kv"]
    wts, x = weights_at("mixer0", x)
    h0, proj = _premix_proj(x, m0[0], m0[1], wts["a_w_in"], "l0_premix_in")
    _, proj = weights_at("launch_layer1", proj)
    o_a, yp, states = _hgrn_fwd(proj, lb, vecs["a_norm_g"], "l0_hgrn")
    y0, x1, [hf0] = _mm_residual_premix(yp, wts["a_w_out"], x, m0[2], [(m0[3], m0[4])], "l0_out")
    more, hf0 = weights_at("ffn0", hf0)
    wts.update(more)
    u0, c0, a0 = _ffn_inner(hf0, wts["up0"], vecs["conv_w0"], vecs["conv_b0"], "l0_ffn")
    saved0 = (hf0, u0, c0, a0)
    ffn0, x2, [hk, h1] = _mm_residual_premix(a0, wts["down0"], x1, m0[5], [(mk[0], mk[1]), (m1[0], m1[1])],
                                             "l0_ffn_down")
    more, hk = weights_at("layer1", hk)
    wts.update(more)
    k_raw, k_sh, v_sh, f_raw = _kv_proj(hk, wts["kv_k"], wts["kv_v"], wts["kv_f"], vecs["k_norm_g"], "kv_proj")
    qa, ka = _fcum_fwd(f_raw, vecs["kv_b_f"], "kv_fcum")
    q_scale = HEAD_DIM ** -0.5
    qo, q = _proj_headnorm(h1, wts["b_w_q"], vecs["q_norm_g"], q_scale * LOG2E, "l1_q")
    o_b, og, qab = _fox_fwd(q, qa, k_sh, ka, v_sh, qo, "l1_fox")
    y1, x3, [hf1] = _mm_residual_premix(og, wts["b_w_out"], x2, m1[2], [(m1[3], m1[4])], "l1_out")
    u1, c1, a1 = _ffn_inner(hf1, wts["up1"], vecs["conv_w1"], vecs["conv_b1"], "l1_ffn")
    saved1 = (hf1, u1, c1, a1)
    sq, dx4, dffn1, dg2_1 = _mm_loss_head(a1, wts["down1"], x3, m1[5], target, "l1_ffn_down")

    big, small = {}, {}
    dx3, dy1, dg1_1, big["up1"], big["down1"], s_ffn1 = _ffn_backward(
        dx4, dffn1, x3, m1[4], saved1, wts["up1"], vecs["conv_w1"], vecs["conv_b1"], wts["down1"], (y1, m1[2]), "l1_ffn")
    big["b_w_out"], dy1 = _weight_grad_first(og, dy1, 1, "l1_out_dw")
    do_b, dgate_b, doa = _fox_gate_bwd(o_b, qo, dy1, wts["b_w_out"], "l1_out_dx_gate_bwd")
    dk_raw, dkg, dv, dka, dq, dcq = _fox_bwd(q, qab, k_sh, ka, v_sh, do_b, doa, k_raw, vecs["k_norm_g"], "l1_fox_bwd")
    dqo, dqg = _headnorm_bwd(qo, vecs["q_norm_g"], q_scale, dq, "l1_qnorm_bwd", extra=dgate_b)
    big["b_w_q"], dqo = _weight_grad_first(h1, dqo, N_CHIPS, "l1_q_dw")
    dz, dbf = _fcum_bwd(f_raw, vecs["kv_b_f"], dka, dcq.reshape(HEADS, 1, -1), "kv_fcum_bwd")
    big["kv_k"], dk_raw = _weight_grad_first(hk, dk_raw, 1, "kv_k_dw")
    big["kv_v"], dv = _weight_grad_first(hk, dv, 1, "kv_v_dw")
    big["kv_f"], dz = _weight_grad_first(hk, dz, 1, "kv_f_dw")
    kv_pairs = [(dk_raw, wts["kv_k"]), (dv, wts["kv_v"]), (dz, wts["kv_f"])]
    dx2, [(dsh1_1, dsc1_1), (dshk, dsck)], dffn0, dg2_0 = _premix_bwd(
        x2, [(m1[1], [(dqo, wts["b_w_q"])]), (mk[1], kv_pairs)], dx3, "l1_kv_premix_bwd", branch=(ffn0, m0[5]))
    dx1, dy0, dg1_0, big["up0"], big["down0"], s_ffn0 = _ffn_backward(
        dx2, dffn0, x1, m0[4], saved0, wts["up0"], vecs["conv_w0"], vecs["conv_b0"], wts["down0"], (y0, m0[2]), "l0_ffn")
    big["a_w_out"], dy0 = _weight_grad_first(yp, dy0, 1, "l0_out_dw")
    dproj, dlb, dng = _hgrn_bwd(proj, lb, vecs["a_norm_g"], o_a, states, dy0, wts["a_w_out"], "l0_out_dx_hgrn_bwd")
    grad_x, [(dsh1_0, dsc1_0)] = _premix_bwd(x, [(m0[1], [(dproj, wts["a_w_in"])])], dx1, "l0_premix_bwd")
    dproj, _ = lax.optimization_barrier((dproj, (dsh1_0, dsc1_0)))
    big["a_w_in"] = _mm_tn(h0, dproj, N_CHIPS, "l0_in_dw")

    small["mod_l0"] = [dsh1_0, dsc1_0, dg1_0, s_ffn0["shift"], s_ffn0["scale"], dg2_0]
    small["mod_l1"] = [dsh1_1, dsc1_1, dg1_1, s_ffn1["shift"], s_ffn1["scale"], dg2_1]
    small["mod_kv"] = [dshk, dsck]
    small["conv0"], small["conv1"] = s_ffn0["conv"], s_ffn1["conv"]
    small["a_norm_g"], small["k_norm_g"], small["q_norm_g"] = dng, dkg, dqg
    small["kv_b_f"], small["lb"] = dbf, dlb
    marks = {"attention_bwd": dv, "ffn0_bwd": dx1, "mixer0_bwd": grad_x}
    return sq, grad_x, big, small, marks


COMM_CHUNK_ELEMS = 256 * 1024


def _place():
    x, y, c = lax.axis_index("x"), lax.axis_index("y"), lax.axis_index("c")
    chips = [(1 - x, y), (x, 1 - y), (1 - x, 1 - y)]
    return x, y, c, (x, y, 1 - c), chips


def _chunk_rows(rows, cols):
    best = BF16_ROWS
    for r in range(BF16_ROWS, rows + 1, BF16_ROWS):
        if rows % r == 0 and r * cols <= COMM_CHUNK_ELEMS:
            best = r
    assert rows % best == 0, (rows, cols)
    return best


def _allgather8(block, name):
    m_per, n = block.shape

    def body(x_ref, out_ref, send_sems, recv_sems, local_sem):
        x, y, c, sibling, chips = _place()
        me = (x, y, c)

        def rows(px, py, pc):
            return out_ref.at[pl.ds((4 * px + 2 * py + pc) * m_per, m_per), :]

        def copy(k, blk, to, src=None):
            return pltpu.make_async_remote_copy(
                src_ref=rows(*blk) if src is None else src, dst_ref=rows(*blk),
                send_sem=send_sems.at[k], recv_sem=recv_sems.at[k], device_id=to, device_id_type=MESH)

        mine = pltpu.make_async_copy(x_ref, rows(*me), local_sem)
        mine.start()
        first = [copy(0, me, sibling, src=x_ref)]
        first += [copy(1 + j, me, (*chip, c), src=x_ref) for j, chip in enumerate(chips)]
        for cp in first:
            cp.start()
        passed = [copy(4 + j, (*chip, c), sibling) for j, chip in enumerate(chips)]
        for j, chip in enumerate(chips):
            copy(1 + j, (*chip, c), me).wait_recv()
            passed[j].start()
        copy(0, sibling, me).wait_recv()
        for j, chip in enumerate(chips):
            copy(4 + j, (*chip, 1 - c), me).wait_recv()
        for cp in first + passed:
            cp.wait_send()
        mine.wait()

    return pl.pallas_call(
        body, name=name, out_shape=jax.ShapeDtypeStruct((N_DEV * m_per, n), block.dtype),
        in_specs=[pl.BlockSpec(memory_space=pltpu.VMEM)], out_specs=pl.BlockSpec(memory_space=pltpu.VMEM),
        scratch_shapes=[pltpu.SemaphoreType.DMA((7,)), pltpu.SemaphoreType.DMA((7,)), pltpu.SemaphoreType.DMA],
    )(block)


def _cast_own_block(shards, layer, chip, name):
    _, r, cols = shards.shape
    rows = _chunk_rows(r, cols)

    def body(chip_ref, w_ref, o_ref):
        o_ref[...] = w_ref[...].astype(BF16)

    return pl.pallas_call(
        body, name=name,
        grid_spec=pltpu.PrefetchScalarGridSpec(
            num_scalar_prefetch=1, grid=(r // rows,),
            in_specs=[pl.BlockSpec((None, rows, cols), lambda i, chip_ref: (layer, i, 0))],
            out_specs=pl.BlockSpec((None, rows, cols), lambda i, chip_ref: (chip_ref[0], i, 0))),
        out_shape=jax.ShapeDtypeStruct((N_CHIPS, r, cols), BF16),
    )(chip, shards)


def _sequencer_gather(bufs, name, collective_id):
    n_t = len(bufs)
    dims = [b.shape[1:] for b in bufs]
    refs = [jax.new_ref(b, memory_space=pltpu.MemorySpace.HBM) for b in bufs]

    @pl.kernel(mesh=plsc.ScalarSubcoreMesh(axis_name="sequencer", num_cores=1), name=name,
               scratch_types=[pltpu.SemaphoreType.DMA((n_t,)), pltpu.SemaphoreType.DMA((3 * n_t,)),
                              pltpu.SemaphoreType.DMA((n_t,)), pltpu.SemaphoreType.DMA((n_t,))],
               compiler_params=pltpu.CompilerParams(collective_id=collective_id))
    def launch(send_ici, recv_ici, send_d2d, recv_d2d):
        x, y, c, sibling, chips = _place()
        p_me = 2 * x + y
        peers = [sibling] + [(cx, cy, c) for cx, cy in chips]
        barrier = pltpu.get_barrier_semaphore()
        for peer in peers:
            pl.semaphore_signal(barrier, inc=1, device_id=peer, device_id_type=MESH)
        pl.semaphore_wait(barrier, len(peers))

        def waiter(t, sem_s, sem_r):
            win = refs[t].at[pl.ds(0, 3), pl.ds(0, dims[t][0] // 2), :]
            return pltpu.make_async_remote_copy(src_ref=win, dst_ref=win, send_sem=sem_s.at[t], recv_sem=sem_r.at[t],
                                                device_id=sibling, device_id_type=MESH)

        def half_copy(t, chip_idx, to, sem_s, sem_r, k):
            r2 = dims[t][0] // 2
            win = refs[t].at[chip_idx, pl.ds(c * r2, r2), :]
            return pltpu.make_async_remote_copy(src_ref=win, dst_ref=win, send_sem=sem_s.at[t], recv_sem=sem_r.at[k],
                                                device_id=to, device_id_type=MESH)

        for t in range(n_t):
            for j, (cx, cy) in enumerate(chips):
                half_copy(t, p_me, (cx, cy, c), send_ici, recv_ici, 3 * t + j).start()
        for t in range(n_t):
            for j, (cx, cy) in enumerate(chips):
                half_copy(t, 2 * cx + cy, (cx, cy, c), send_ici, recv_ici, 3 * t + j).wait_recv()
                half_copy(t, 2 * cx + cy, sibling, send_d2d, recv_d2d, t).start()
        for t in range(n_t):
            waiter(t, send_d2d, recv_d2d).wait_recv()
            waiter(t, send_ici, recv_ici).wait_send()
            waiter(t, send_d2d, recv_d2d).wait_send()

    launch()
    return [r[...] for r in refs]


def _sequencer_allgather8(block, dev, name, collective_id):
    m_per, n = block.shape
    src = jax.new_ref(block, memory_space=pltpu.MemorySpace.HBM)
    out = jax.empty_ref(jax.ShapeDtypeStruct((N_DEV * m_per, n), block.dtype), memory_space=pltpu.MemorySpace.HBM)

    @pl.kernel(mesh=plsc.ScalarSubcoreMesh(axis_name="sequencer", num_cores=1), name=name,
               scratch_types=[pltpu.SemaphoreType.DMA((7,))] * 2,
               compiler_params=pltpu.CompilerParams(collective_id=collective_id))
    def launch(send_sems, recv_sems):
        x, y, c, sibling, chips = _place()
        me = (x, y, c)
        _handshake([sibling] + [(cx, cy, c) for cx, cy in chips])

        def rows(px, py, pc):
            return out.at[pl.ds((4 * px + 2 * py + pc) * m_per, m_per), :]

        def copy(k, blk, to, from_src=False):
            return pltpu.make_async_remote_copy(
                src_ref=src if from_src else rows(*blk), dst_ref=rows(*blk),
                send_sem=send_sems.at[k], recv_sem=recv_sems.at[k], device_id=to, device_id_type=MESH)

        first = [copy(0, me, sibling, True)] + [copy(1 + j, me, (*chip, c), True) for j, chip in enumerate(chips)]
        for cp in first:
            cp.start()
        passed = [copy(4 + j, (*chip, c), sibling) for j, chip in enumerate(chips)]
        for j, chip in enumerate(chips):
            copy(1 + j, (*chip, c), me).wait_recv()
            passed[j].start()
        copy(0, sibling, me).wait_recv()
        for j, chip in enumerate(chips):
            copy(4 + j, (*chip, 1 - c), me).wait_recv()
        for cp in first + passed:
            cp.wait_send()

    launch()
    return lax.dynamic_update_slice(out[...], block, (dev * m_per, 0))


def _others():
    x, y, c = lax.axis_index("x"), lax.axis_index("y"), lax.axis_index("c")
    flip = lambda v, f: 1 - v if f else v
    return [(flip(x, fx), flip(y, fy), flip(c, fc))
            for fx in (0, 1) for fy in (0, 1) for fc in (0, 1) if (fx, fy, fc) != (0, 0, 0)]


def _handshake(peers):
    barrier = pltpu.get_barrier_semaphore()
    for peer in peers:
        pl.semaphore_signal(barrier, inc=1, device_id=peer, device_id_type=MESH)
    pl.semaphore_wait(barrier, len(peers))


def _sequencer_scatter(parts, name, collective_id):
    n_t = len(parts)
    dims = [p.shape[1:] for p in parts]
    srcs = [jax.new_ref(p, memory_space=pltpu.MemorySpace.HBM) for p in parts]
    inboxes = [jax.empty_ref(jax.ShapeDtypeStruct((N_DEV - 1, r // 2, cols), BF16), memory_space=pltpu.MemorySpace.HBM)
               for r, cols in dims]

    @pl.kernel(mesh=plsc.ScalarSubcoreMesh(axis_name="sequencer", num_cores=1), name=name,
               scratch_types=[pltpu.SemaphoreType.DMA((n_t,))] * 2,
               compiler_params=pltpu.CompilerParams(collective_id=collective_id))
    def launch(send_sem, recv_sem):
        peers = _others()
        _handshake(peers)
        for t in range(n_t):
            h = dims[t][0] // 2
            for k, (qx, qy, qc) in enumerate(peers):
                pltpu.make_async_remote_copy(
                    src_ref=srcs[t].at[2 * qx + qy, pl.ds(qc * h, h), :], dst_ref=inboxes[t].at[k],
                    send_sem=send_sem.at[t], recv_sem=recv_sem.at[t], device_id=(qx, qy, qc), device_id_type=MESH).start()
        for t in range(n_t):
            win = inboxes[t]
            both = pltpu.make_async_remote_copy(src_ref=win, dst_ref=win, send_sem=send_sem.at[t],
                                                recv_sem=recv_sem.at[t], device_id=peers[0], device_id_type=MESH)
            both.wait_recv()
            both.wait_send()

    launch()
    return [b[...] for b in inboxes]


def _sum_pieces(part, inbox, place, name):
    _, r, cols = part.shape
    h = r // 2
    rows = _chunk_rows(h, cols)
    steps = h // rows

    def body(place_ref, own_ref, in_ref, o_ref):
        acc = own_ref[...].astype(F32)
        for k in range(N_DEV - 1):
            acc = acc + in_ref[k].astype(F32)
        o_ref[...] = acc

    return pl.pallas_call(
        body, name=name,
        grid_spec=pltpu.PrefetchScalarGridSpec(
            num_scalar_prefetch=1, grid=(steps,),
            in_specs=[pl.BlockSpec((None, rows, cols), lambda i, pr: (pr[0], pr[1] * steps + i, 0)),
                      pl.BlockSpec((N_DEV - 1, rows, cols), lambda i, pr: (0, i, 0))],
            out_specs=pl.BlockSpec((rows, cols), lambda i, pr: (pr[1] * steps + i, 0))),
        out_shape=jax.ShapeDtypeStruct((r, cols), F32),
    )(place, part, inbox)


def _sequencer_swap_halves(halves, name, collective_id):
    n_t = len(halves)
    refs = [jax.new_ref(a, memory_space=pltpu.MemorySpace.HBM) for a in halves]

    @pl.kernel(mesh=plsc.ScalarSubcoreMesh(axis_name="sequencer", num_cores=1), name=name,
               scratch_types=[pltpu.SemaphoreType.DMA((n_t,))] * 2,
               compiler_params=pltpu.CompilerParams(collective_id=collective_id))
    def launch(send_sem, recv_sem):
        x, y, c = lax.axis_index("x"), lax.axis_index("y"), lax.axis_index("c")
        sibling = (x, y, 1 - c)
        _handshake([sibling])
        copies = []
        for t in range(n_t):
            h = halves[t].shape[0] // 2
            win = refs[t].at[pl.ds(c * h, h), :]
            copies.append(pltpu.make_async_remote_copy(src_ref=win, dst_ref=win, send_sem=send_sem.at[t],
                                                       recv_sem=recv_sem.at[t], device_id=sibling, device_id_type=MESH))
            copies[-1].start()
        for cp in copies:
            cp.wait()

    launch()
    return [r[...] for r in refs]


def _cond_rows(c16, w, act, name):
    n_l, dm, wid = w.shape

    def body(c_ref, w_ref, o_ref, a_ref):
        cv = c_ref[...]
        if act:
            cv = cv * _sig(cv)
        a_ref[...] = cv
        o_ref[...] = _dot_f32(cv, w_ref[...])

    return pl.pallas_call(
        body, name=name, grid=(n_l,),
        in_specs=[_full((16, dm)), pl.BlockSpec((None, dm, wid), lambda l: (l, 0, 0))],
        out_specs=[pl.BlockSpec((None, 16, wid), lambda l: (l, 0, 0)), _full((16, dm))],
        out_shape=[jax.ShapeDtypeStruct((n_l, 16, wid), F32), jax.ShapeDtypeStruct((16, dm), F32)],
    )(c16, w)


def _outer_grad(ct, dm, name):
    n_l, kk, wid = dm.shape
    d_rows = ct.shape[0]

    def body(c_ref, d_ref, o_ref):
        o_ref[...] = _dot_f32(c_ref[...], d_ref[...])

    return pl.pallas_call(
        body, name=name, grid=(n_l,),
        in_specs=[_full((d_rows, kk)), pl.BlockSpec((None, kk, wid), lambda l: (l, 0, 0))],
        out_specs=pl.BlockSpec((None, d_rows, wid), lambda l: (l, 0, 0)),
        out_shape=jax.ShapeDtypeStruct((n_l, d_rows, wid), F32),
    )(ct, dm)


def _sum_devices(g, name):
    rows, n = g.shape

    def body(g_ref, o_ref):
        acc = g_ref[0:SUBLANES, :]
        for dev in range(1, N_DEV):
            acc = acc + g_ref[dev * SUBLANES:(dev + 1) * SUBLANES, :]
        o_ref[...] = acc

    return pl.pallas_call(body, name=name, out_shape=jax.ShapeDtypeStruct((SUBLANES, n), F32))(g)


def _adamw(w, g, m, v, name):
    shape = w.shape
    cols = shape[-1]
    rows = w.size // cols
    tr = rows
    for cand in range(SUBLANES, min(rows, 256) + 1, SUBLANES):
        if rows % cand == 0:
            tr = cand
    if rows * cols <= COMM_CHUNK_ELEMS:
        tr = rows
    c1 = 1.0 / (1.0 - ADAM_B1 ** ADAM_STEP)
    c2 = 1.0 / (1.0 - ADAM_B2 ** ADAM_STEP)

    def body(w_ref, g_ref, m_ref, v_ref, d_ref, mo_ref, vo_ref):
        gv = g_ref[...]
        m_new = ADAM_B1 * m_ref[...] + (1.0 - ADAM_B1) * gv
        v_new = ADAM_B2 * v_ref[...] + (1.0 - ADAM_B2) * (gv * gv)
        mo_ref[...] = m_new
        vo_ref[...] = v_new
        d_ref[...] = -ADAM_LR * ((m_new * c1) / (jnp.sqrt(v_new * c2) + ADAM_EPS) + ADAM_WD * w_ref[...])

    spec = pl.BlockSpec((tr, cols), lambda i: (i, 0))
    outs = pl.pallas_call(
        body, name=name, grid=(rows // tr,), in_specs=[spec] * 4, out_specs=[spec] * 3,
        out_shape=[jax.ShapeDtypeStruct((rows, cols), F32)] * 3,
    )(*[a.reshape(rows, cols) for a in (w, g, m, v)])
    return tuple(o.reshape(shape) for o in outs)


def _pad_cols(a, cols):
    return jnp.pad(a, [(0, 0)] * (a.ndim - 1) + [(0, cols - a.shape[-1])])


def _flat8(parts, width):
    v = jnp.concatenate([p.reshape(-1) for p in parts])
    return jnp.pad(v, (0, width - v.shape[0])).reshape(SUBLANES, width // SUBLANES)


KV_SHARD = 514
KV_SHARD_PAD = 640
BIG = ("a_w_in", "a_w_out", "kv_w", "b_w_q", "b_w_out", "up0", "up1", "down0", "down1")


def kernel(x, c, ada_w, ada_b, a_w_in, a_lb_logits, a_norm_g, a_w_out, kv_ada_w, kv_ada_b, kv_w, kv_b_f, k_norm_g, b_w_q, q_norm_g, b_w_out, ffn_w_up, ffn_conv_w, ffn_conv_b, ffn_w_down, loss_target, m_ada_w, m_ada_b, m_a_w_in, m_a_lb_logits, m_a_norm_g, m_a_w_out, m_kv_ada_w, m_kv_ada_b, m_kv_w, m_kv_b_f, m_k_norm_g, m_b_w_q, m_q_norm_g, m_b_w_out, m_ffn_w_up, m_ffn_conv_w, m_ffn_conv_b, m_ffn_w_down, v_ada_w, v_ada_b, v_a_w_in, v_a_lb_logits, v_a_norm_g, v_a_w_out, v_kv_ada_w, v_kv_ada_b, v_kv_w, v_kv_b_f, v_k_norm_g, v_b_w_q, v_q_norm_g, v_b_w_out, v_ffn_w_up, v_ffn_conv_w, v_ffn_conv_b, v_ffn_w_down):
    dm, ff = D_MODEL, D_FF
    ix, iy, ic = lax.axis_index("x"), lax.axis_index("y"), lax.axis_index("c")
    chip = 2 * ix + iy
    dev = 2 * chip + ic

    w1 = 10240
    g1 = _allgather8(_flat8([c, a_lb_logits, ffn_conv_w], w1), "gather_cond").reshape(N_DEV, w1)
    c_all = g1[:, :dm]
    per_chip = g1[0::2]
    lb_logits = per_chip[:, dm:dm + 512].reshape(N_CHIPS, 2, 256).transpose(1, 0, 2).reshape(2, dm)
    conv_w = per_chip[:, dm + 512:dm + 512 + 2 * CONV_W * FFN_COLS].reshape(N_CHIPS, 2, CONV_W, FFN_COLS)
    conv_w = conv_w.transpose(1, 2, 0, 3).reshape(2, CONV_W, 2, ff).transpose(0, 2, 1, 3)
    conv_b = ffn_conv_b.reshape(2, 2, 1, ff)
    lb = jax.nn.softmax(lb_logits, axis=0)[0:1]

    c16 = jnp.pad(c_all, ((0, 8), (0, 0)))
    mod_ada, c_act16 = _cond_rows(c16, ada_w, True, "mod_ada")
    mod_kv, _ = _cond_rows(c16, kv_ada_w[None], True, "mod_kv")
    mine = jnp.concatenate([mod_ada[0, :8], mod_ada[1, :8], mod_kv[0, :8]], axis=1)
    w2 = mine.shape[1]
    g2 = _allgather8(mine, "gather_mod").reshape(N_DEV, 8, w2)[0::2]
    my_rows = lax.dynamic_index_in_dim(g2, dev, axis=1, keepdims=False)
    mod0 = my_rows[:, 0:1536].reshape(6 * dm) + ada_b[0]
    mod1 = my_rows[:, 1536:3072].reshape(6 * dm) + ada_b[1]
    modk = my_rows[:, 3072:3584].reshape(2 * dm) + kv_ada_b
    mods = {"l0": [v.reshape(1, dm) for v in jnp.split(mod0, 6)],
            "l1": [v.reshape(1, dm) for v in jnp.split(mod1, 6)],
            "kv": [v.reshape(1, dm) for v in jnp.split(modk, 2)]}

    local = [(a_w_in, 0), (a_w_out, 0), (_pad_cols(kv_w, KV_SHARD_PAD)[None], 0), (b_w_q, 0), (b_w_out, 0),
             (ffn_w_up, 0), (ffn_w_up, 1), (ffn_w_down, 0), (ffn_w_down, 1)]
    chip_arr = chip.reshape(1).astype(jnp.int32)
    local = dict(zip(BIG, local))
    stages = {"mixer0": ("a_w_in", "a_w_out"), "ffn0": ("up0", "down0"),
              "layer1": ("kv_w", "b_w_q", "b_w_out", "up1", "down1")}
    arriving = {}

    def launch(stage, behind):
        shards = [local[n][0] for n in stages[stage]]
        if behind is not None:
            shards, _ = lax.optimization_barrier((shards, behind))
        own = [_cast_own_block(w, local[n][1], chip_arr, "cast_" + n) for n, w in zip(stages[stage], shards)]
        arriving[stage] = _sequencer_gather(own, "gather_" + stage, 1 + list(stages).index(stage))

    launch("mixer0", None)
    launch("ffn0", mod0)
    rowwise = lambda g: g.reshape(1, -1, dm)

    def weights_at(stage, token):
        if stage == "launch_layer1":
            launch("layer1", token)
            return {}, token
        got, token = lax.optimization_barrier((arriving[stage], token))
        g = dict(zip(stages[stage], got))
        if stage == "mixer0":
            return {"a_w_in": g["a_w_in"], "a_w_out": rowwise(g["a_w_out"])}, token
        if stage == "ffn0":
            return {"up0": g["up0"], "down0": rowwise(g["down0"])}, token
        s0, s1, s2, s3 = (g["kv_w"][p] for p in range(N_CHIPS))
        second = dm - KV_SHARD
        w_k = jnp.concatenate([s0[:, :KV_SHARD], s1[:, :second]], axis=1)
        w_v = jnp.concatenate([s1[:, second:KV_SHARD], s2[:, :KV_SHARD], s3[:, :KV_SHARD - HEADS]], axis=1)
        w_f = _pad_cols(s3[:, KV_SHARD - HEADS:KV_SHARD], LANES)
        return {"kv_k": w_k[None], "kv_v": w_v[None], "kv_f": w_f[None], "b_w_q": g["b_w_q"],
                "b_w_out": rowwise(g["b_w_out"]), "up1": g["up1"], "down1": rowwise(g["down1"])}, token

    vecs = {"a_norm_g": jnp.tile(a_norm_g, (1, HEADS)), "k_norm_g": jnp.tile(k_norm_g[None], (1, HEADS)),
            "q_norm_g": jnp.tile(q_norm_g, (1, HEADS)), "kv_b_f": _pad_cols(kv_b_f[None], LANES),
            "conv_w0": conv_w[0], "conv_b0": conv_b[0], "conv_w1": conv_w[1], "conv_b1": conv_b[1]}

    sq, grad_x, big, small, marks = _local_step(x[0], loss_target[0], mods, lb, vecs, weights_at)

    gk, gv, gf = big["kv_k"][0], big["kv_v"][0], big["kv_f"][0][:, :HEADS]
    second = dm - KV_SHARD
    kv_blocks = [gk[:, :KV_SHARD], jnp.concatenate([gk[:, KV_SHARD:], gv[:, :KV_SHARD - second]], axis=1),
                 gv[:, KV_SHARD - second:2 * KV_SHARD - second], jnp.concatenate([gv[:, 2 * KV_SHARD - second:], gf], axis=1)]
    kv_grad = jnp.stack([_pad_cols(b, KV_SHARD_PAD) for b in kv_blocks])
    chipwise = lambda g: g.reshape(N_CHIPS, -1, dm)
    parts = dict(zip(BIG, [big["a_w_in"], chipwise(big["a_w_out"]), kv_grad, big["b_w_q"], chipwise(big["b_w_out"]),
                           big["up0"], big["up1"], chipwise(big["down0"]), chipwise(big["down1"])]))
    place = jnp.stack([chip, ic, dev]).astype(jnp.int32)

    served = []
    boxes = {}

    groups = (("up1", "down1"), ("b_w_out", "b_w_q", "kv_w"), ("up0", "down0"), ("a_w_out", "a_w_in"))

    def scatter_group(k):
        mine = [parts[n] for n in groups[k]]
        if served:
            mine, _ = lax.optimization_barrier((mine, served[-1]))
        boxes[k] = _sequencer_scatter(mine, "scatter_grads_%d" % k, 4 + k)
        served.append(boxes[k])

    def sum_group(k, token):
        inboxes, _ = lax.optimization_barrier((boxes[k], token))
        return [_sum_pieces(parts[n], box, place, "sum_" + n) for n, box in zip(groups[k], inboxes)]

    def swap_group(k, halves, behind):
        halves, _ = lax.optimization_barrier((halves, behind))
        return dict(zip(groups[k], _sequencer_swap_halves(halves, "swap_grads_%d" % k, 8 + k)))

    for k in range(3):
        scatter_group(k)
    halves = [sum_group(0, marks["attention_bwd"]), sum_group(1, marks["ffn0_bwd"]), sum_group(2, marks["mixer0_bwd"])]

    fold = lambda a: a.sum(axis=0)
    heads = lambda a: fold(a).reshape(HEADS, HEAD_DIM).sum(axis=0)
    conv_flat = lambda a: a.sum(axis=2).transpose(1, 0, 2)
    pieces = ([fold(a) for a in small["mod_l0"]] + [fold(a) for a in small["mod_l1"]] + [fold(a) for a in small["mod_kv"]]
              + [conv_flat(small["conv0"]), conv_flat(small["conv1"]), heads(small["a_norm_g"]), heads(small["k_norm_g"]),
                 heads(small["q_norm_g"]), fold(small["kv_b_f"]), fold(small["lb"]),
                 0.5 * jnp.sum(sq).reshape(1) / dm])
    w3 = 61440
    small_vec, _ = lax.optimization_barrier((_flat8(pieces, w3), served[2]))
    g3 = _sequencer_allgather8(small_vec, dev, "gather_small", 12)
    served.append(g3)
    scatter_group(3)
    rs = {}
    for k in range(3):
        rs.update(swap_group(k, halves[k], g3))
    tot = _sum_devices(g3, "sum_small").reshape(w3)
    n_mod = 14 * dm
    dmod_all = g3.reshape(N_DEV, w3)[:, :n_mod]
    o = n_mod
    conv_tot = [tot[o + l * 8 * ff: o + (l + 1) * 8 * ff].reshape(4, 2 * ff) for l in range(2)]
    o += 16 * ff
    g_a_norm, g_k_norm, g_q_norm = (tot[o + i * HEAD_DIM: o + (i + 1) * HEAD_DIM] for i in range(3))
    o += 3 * HEAD_DIM
    g_kv_b_f = tot[o:o + HEADS]
    dlb = tot[o + LANES:o + LANES + dm]
    loss = tot[o + LANES + dm]

    ct = _pad_cols(c_act16[:8].T, LANES)
    dmod_pad = jnp.pad(dmod_all, ((0, LANES - N_DEV), (0, 0)))
    cols_ada = jnp.stack([lax.dynamic_slice_in_dim(dmod_pad, l * 6 * dm + chip * 1536, 1536, axis=1) for l in range(2)])
    cols_kv = lax.dynamic_slice_in_dim(dmod_pad, 12 * dm + chip * 512, 512, axis=1)[None]
    g_ada_w = _outer_grad(ct, cols_ada, "grad_ada_w")
    g_kv_ada_w = _outer_grad(ct, cols_kv, "grad_kv_ada_w")[0]

    my_lb = lax.dynamic_slice_in_dim(lb[0], chip * 256, 256)
    l0 = lax.dynamic_slice_in_dim(dlb, chip * 256, 256) * my_lb * (1.0 - my_lb)
    grads = {
        "ada_w": g_ada_w, "ada_b": jnp.stack([tot[:6 * dm], tot[6 * dm:12 * dm]]),
        "a_lb_logits": jnp.stack([l0, -l0]), "a_norm_g": g_a_norm[None],
        "kv_ada_w": g_kv_ada_w, "kv_ada_b": tot[12 * dm:14 * dm],
        "kv_w": rs["kv_w"][:, :KV_SHARD], "kv_b_f": g_kv_b_f, "k_norm_g": g_k_norm,
        "b_w_q": rs["b_w_q"][None], "q_norm_g": g_q_norm[None], "b_w_out": rs["b_w_out"][None],
        "ffn_w_up": jnp.stack([rs["up0"], rs["up1"]]),
        "ffn_conv_w": jnp.stack([lax.dynamic_slice_in_dim(ct_l[:CONV_W], chip * FFN_COLS, FFN_COLS, axis=1) for ct_l in conv_tot]),
        "ffn_conv_b": jnp.stack([ct_l[CONV_W] for ct_l in conv_tot]),
        "ffn_w_down": jnp.stack([rs["down0"], rs["down1"]]),
    }
    weights = dict(ada_w=ada_w, ada_b=ada_b, a_w_in=a_w_in, a_lb_logits=a_lb_logits, a_norm_g=a_norm_g, a_w_out=a_w_out,
                   kv_ada_w=kv_ada_w, kv_ada_b=kv_ada_b, kv_w=kv_w, kv_b_f=kv_b_f, k_norm_g=k_norm_g, b_w_q=b_w_q,
                   q_norm_g=q_norm_g, b_w_out=b_w_out, ffn_w_up=ffn_w_up, ffn_conv_w=ffn_conv_w, ffn_conv_b=ffn_conv_b,
                   ffn_w_down=ffn_w_down)
    m_in = dict(ada_w=m_ada_w, ada_b=m_ada_b, a_w_in=m_a_w_in, a_lb_logits=m_a_lb_logits, a_norm_g=m_a_norm_g,
                a_w_out=m_a_w_out, kv_ada_w=m_kv_ada_w, kv_ada_b=m_kv_ada_b, kv_w=m_kv_w, kv_b_f=m_kv_b_f,
                k_norm_g=m_k_norm_g, b_w_q=m_b_w_q, q_norm_g=m_q_norm_g, b_w_out=m_b_w_out, ffn_w_up=m_ffn_w_up,
                ffn_conv_w=m_ffn_conv_w, ffn_conv_b=m_ffn_conv_b, ffn_w_down=m_ffn_w_down)
    v_in = dict(ada_w=v_ada_w, ada_b=v_ada_b, a_w_in=v_a_w_in, a_lb_logits=v_a_lb_logits, a_norm_g=v_a_norm_g,
                a_w_out=v_a_w_out, kv_ada_w=v_kv_ada_w, kv_ada_b=v_kv_ada_b, kv_w=v_kv_w, kv_b_f=v_kv_b_f,
                k_norm_g=v_k_norm_g, b_w_q=v_b_w_q, q_norm_g=v_q_norm_g, b_w_out=v_b_w_out, ffn_w_up=v_ffn_w_up,
                ffn_conv_w=v_ffn_conv_w, ffn_conv_b=v_ffn_conv_b, ffn_w_down=v_ffn_w_down)

    names = list(weights)
    step = lambda n: _adamw(weights[n], grads[n], m_in[n], v_in[n], "adamw_" + n)
    grads = {n: g.reshape(weights[n].shape) for n, g in grads.items()}
    upd = {n: step(n) for n in names if n not in groups[3]}
    last = sum_group(3, [u[0] for u in upd.values()])
    for n, g in swap_group(3, last, last).items():
        grads[n] = g[None]
        upd[n] = step(n)
    return (loss, grad_x[None], *[grads[n] for n in names], *[upd[n][0] for n in names],
            *[upd[n][1] for n in names], *[upd[n][2] for n in names])
```

```python
import jax
import jax.numpy as jnp
from jax import lax
from jax.experimental import pallas as pl
from jax.experimental.pallas import tpu as pltpu
from jax.experimental.pallas import tpu_sc as plsc

F32 = jnp.float32
BF16 = jnp.bfloat16

D_MODEL = 1024
HEADS = 8
HEAD_DIM = 128
A_CHUNK = 64
D_FF = 2816
CONV_W = 3
EPS = 1e-6
NEG_INF = -1e30
N_CHIPS = 4
N_DEV = 8

ADAM_LR = 0.001
ADAM_B1 = 0.9
ADAM_B2 = 0.999
ADAM_EPS = 1e-08
ADAM_WD = 0.01
ADAM_STEP = 10

SUBLANES = 8
BF16_ROWS = 16
LANES = 128
HALO = BF16_ROWS
ROW_TILE = 512
TOKEN_TILE_TN = 2048
FFN_COLS = 1408
FFN_ROWS = 256
HGRN_ROWS = 512
ATT_TILE = 512
ATT_SPLIT = 2
ATT_FWD_HEADS = 8
ATT_BWD_HEADS = 8
MESH = pl.DeviceIdType.MESH


def _sig(x):
    return jax.nn.sigmoid(x)


def _dot(a, b):
    return jnp.dot(a, b, preferred_element_type=F32)


def _dot_nt(a, b):
    return lax.dot_general(a, b, (((1,), (1,)), ((), ())), preferred_element_type=F32)


def _dot_tn(a, b):
    return lax.dot_general(a, b, (((0,), (0,)), ((), ())), preferred_element_type=F32)


def _split2(x):
    hi = x.astype(BF16)
    lo = (x - hi.astype(F32)).astype(BF16)
    return hi, lo


def _dot_f32(a, b):
    ah, al = _split2(a)
    bh, bl = _split2(b)
    return _dot(ah, bh) + _dot(ah, bl) + _dot(al, bh)


def _tri_dot(tri, x):
    hi = x.astype(BF16)
    r = x - hi.astype(F32)
    mid = r.astype(BF16)
    lo = (r - mid.astype(F32)).astype(BF16)
    return _dot(tri, hi) + _dot(tri, mid) + _dot(tri, lo)


def _tri(n, upper=False):
    r = lax.broadcasted_iota(jnp.int32, (n, n), 0)
    c = lax.broadcasted_iota(jnp.int32, (n, n), 1)
    keep = (c >= r) if upper else (c <= r)
    return jnp.where(keep, 1.0, 0.0).astype(BF16)


def _colsum8(v):
    rows, n = v.shape
    return v.reshape(rows // SUBLANES, SUBLANES, n).sum(axis=0)


def _full(shape):
    nd = len(shape)
    return pl.BlockSpec(shape, lambda *_: (0,) * nd)


def _tile(n, want):
    t = min(n, want)
    assert n % t == 0, (n, t)
    return t


def _mm_tn(a, d, p_n, name):
    m_rows, k = a.shape
    g_n, _, w_cols = d.shape
    per = p_n // g_n
    n = w_cols // per
    tm = _tile(m_rows, TOKEN_TILE_TN if k <= D_MODEL else ROW_TILE)
    steps = m_rows // tm

    def body(a_ref, d_ref, o_ref, acc):
        m = pl.program_id(1)

        @pl.when(m == 0)
        def _():
            acc[...] = jnp.zeros_like(acc)

        acc[...] += _dot_tn(a_ref[...], d_ref[...])

        @pl.when(m == steps - 1)
        def _():
            o_ref[...] = acc[...].astype(BF16)

    return pl.pallas_call(
        body, name=name, grid=(p_n, steps),
        in_specs=[pl.BlockSpec((tm, k), lambda p, m: (m, 0)),
                  pl.BlockSpec((None, tm, n), lambda p, m: (p // per, m, p % per))],
        out_specs=pl.BlockSpec((None, k, n), lambda p, m: (p, 0, 0)),
        out_shape=jax.ShapeDtypeStruct((p_n, k, n), BF16),
        scratch_shapes=[pltpu.VMEM((k, n), F32)],
    )(a, d)


def _premix_proj(x, shift, scale, w, name):
    s, dm = x.shape
    p_n, _, n = w.shape
    tm = _tile(s, ROW_TILE)

    def body(x_ref, sh_ref, sc_ref, w_ref, h_ref, o_ref):
        xv = x_ref[...]
        inv = lax.rsqrt(jnp.mean(xv * xv, axis=-1, keepdims=True) + EPS)
        h = (xv * inv * (1.0 + sc_ref[...]) + sh_ref[...]).astype(BF16)
        h_ref[...] = h
        for p in range(p_n):
            o_ref[:, p * n:(p + 1) * n] = _dot(h, w_ref[p])

    row = pl.BlockSpec((tm, dm), lambda i: (i, 0))
    vec = _full((1, dm))
    return pl.pallas_call(
        body, name=name, grid=(s // tm,), in_specs=[row, vec, vec, _full(w.shape)],
        out_specs=[row, pl.BlockSpec((tm, p_n * n), lambda i: (i, 0))],
        out_shape=[jax.ShapeDtypeStruct((s, dm), BF16), jax.ShapeDtypeStruct((s, p_n * n), F32)],
    )(x, shift, scale, w)


def _premix_bwd(x, terms, dres, name, branch=None):
    s, dm = x.shape
    tm = _tile(s, ROW_TILE)
    pairs = [pr for _, prs in terms for pr in prs]
    n_in = 2 + len(terms) + 2 * len(pairs) + (2 if branch else 0)

    def body(*refs):
        x_ref, dres_ref = refs[:2]
        sc_refs = refs[2:2 + len(terms)]
        mm_refs = refs[2 + len(terms):2 + len(terms) + 2 * len(pairs)]
        outs = refs[n_in:]

        @pl.when(pl.program_id(0) == 0)
        def _():
            for o in outs[1:1 + 2 * len(terms)]:
                o[...] = jnp.zeros_like(o)
            if branch:
                outs[-1][...] = jnp.zeros_like(outs[-1])

        xv = x_ref[...]
        inv = lax.rsqrt(jnp.mean(xv * xv, axis=-1, keepdims=True) + EPS)
        r = xv * inv
        dx = dres_ref[...]
        k = 0
        for t, (_, prs) in enumerate(terms):
            dh = None
            for d, w in prs:
                d_ref, w_ref = mm_refs[2 * k], mm_refs[2 * k + 1]
                k += 1
                p_n, _, n = w.shape
                per = p_n // d.shape[0]
                for p in range(p_n):
                    part = _dot_nt(d_ref[p // per, :, (p % per) * n:(p % per + 1) * n], w_ref[p])
                    dh = part if dh is None else dh + part
            dr = dh * (1.0 + sc_refs[t][...])
            dx = dx + inv * (dr - r * jnp.mean(dr * r, axis=-1, keepdims=True))
            outs[1 + 2 * t][...] += _colsum8(dh)
            outs[2 + 2 * t][...] += _colsum8(dh * r)
        outs[0][...] = dx
        if branch:
            y_ref, g_ref = refs[n_in - 2:n_in]
            outs[-2][0] = (dx * g_ref[...]).astype(BF16)
            outs[-1][...] += _colsum8(dx * y_ref[...])

    row = pl.BlockSpec((tm, dm), lambda i: (i, 0))
    vec, acc = _full((1, dm)), _full((SUBLANES, dm))
    ins, specs = [x, dres] + [sc for sc, _ in terms], [row, row] + [vec] * len(terms)
    for d, w in pairs:
        ins += [d, w]
        specs += [pl.BlockSpec((d.shape[0], tm, d.shape[2]), lambda i: (0, i, 0)), _full(w.shape)]
    out_shape = [jax.ShapeDtypeStruct((s, dm), F32)] + [jax.ShapeDtypeStruct((SUBLANES, dm), F32)] * (2 * len(terms))
    out_specs = [row] + [acc] * (2 * len(terms))
    if branch:
        ins += list(branch)
        specs += [row, vec]
        out_shape += [jax.ShapeDtypeStruct((1, s, dm), BF16), jax.ShapeDtypeStruct((SUBLANES, dm), F32)]
        out_specs += [pl.BlockSpec((1, tm, dm), lambda i: (0, i, 0)), acc]
    outs = pl.pallas_call(body, name=name, grid=(s // tm,), in_specs=specs, out_specs=out_specs,
                          out_shape=out_shape)(*ins)
    partials = [(outs[1 + 2 * t], outs[2 + 2 * t]) for t in range(len(terms))]
    return (outs[0], partials) + ((outs[-2], outs[-1]) if branch else ())


def _conv_taps(e, w, b):
    return w[2:3] * e + w[1:2] * pltpu.roll(e, 1, 0) + w[0:1] * pltpu.roll(e, 2, 0) + b


def _ffn_specs(s, tm, cb):
    hb = tm // HALO
    last = s // HALO - 1
    main = pl.BlockSpec((2, tm, cb), lambda j, i: (0, i, j))
    prev = pl.BlockSpec((2, HALO, cb), lambda j, i: (0, jnp.maximum(i * hb - 1, 0), j))
    nxt = pl.BlockSpec((2, HALO, cb), lambda j, i: (0, jnp.minimum((i + 1) * hb, last), j))
    wspec = pl.BlockSpec((2, CONV_W, cb), lambda j, i: (0, 0, j))
    bspec = pl.BlockSpec((2, 1, cb), lambda j, i: (0, 0, j))
    return main, prev, nxt, wspec, bspec


def _convglu_bwd(u, c, dffn, w_down, w, name):
    _, s, f = u.shape
    dm = dffn.shape[2]
    tm = _tile(s, 256)
    cb = _tile(f, FFN_COLS)
    steps = s // tm
    n_ext = tm + HALO
    main, _, nxt, wspec, _ = _ffn_specs(s, tm, cb)
    hb = tm // HALO
    last = s // HALO - 1
    d_main = pl.BlockSpec((None, tm, dm), lambda j, i: (0, i, 0))
    d_next = pl.BlockSpec((None, HALO, dm), lambda j, i: (0, jnp.minimum((i + 1) * hb, last), 0))
    wd_spec = pl.BlockSpec((None, cb, dm), lambda j, i: (0, j, 0))

    def body(u_ref, c_ref, cn_ref, d_ref, dn_ref, wd_ref, w_ref, du_ref, acc_ref):
        i = pl.program_id(1)
        notlast = jnp.where(i < steps - 1, 1.0, 0.0)

        @pl.when(i == 0)
        def _():
            acc_ref[...] = jnp.zeros_like(acc_ref)

        gate, val = (jnp.concatenate([c_ref[g].astype(F32), cn_ref[g].astype(F32)], axis=0) for g in range(2))
        wd = wd_ref[...]
        da = jnp.concatenate([_dot_nt(d_ref[...], wd).astype(BF16).astype(F32),
                              _dot_nt(dn_ref[...], wd).astype(BF16).astype(F32) * notlast], axis=0)
        sg = _sig(gate)
        d_val = da * gate * sg
        d_gate = da * val * (sg * (1.0 + gate * (1.0 - sg)))

        def finish(g, d):
            wv = w_ref[g]
            d1, d2 = pltpu.roll(d, n_ext - 1, 0), pltpu.roll(d, n_ext - 2, 0)
            du_ref[g] = (wv[2:3] * d + wv[1:2] * d1 + wv[0:1] * d2)[0:tm].astype(BF16)
            uv = u_ref[g].astype(F32)
            acc_ref[g, 2] += _colsum8(d[0:tm] * uv)
            acc_ref[g, 1] += _colsum8(d1[0:tm] * uv)
            acc_ref[g, 0] += _colsum8(d2[0:tm] * uv)
            acc_ref[g, 3] += _colsum8(d[0:tm])

        finish(0, d_gate)
        finish(1, d_val)

    return pl.pallas_call(
        body, name=name, grid=(f // cb, steps),
        in_specs=[main, main, nxt, d_main, d_next, wd_spec, wspec],
        out_specs=[main, pl.BlockSpec((2, 4, SUBLANES, cb), lambda j, i: (0, 0, 0, j))],
        out_shape=[jax.ShapeDtypeStruct((2, s, f), BF16), jax.ShapeDtypeStruct((2, 4, SUBLANES, f), F32)],
    )(u, c, c, dffn, dffn, w_down, w)


def _hgrn_gates(q_raw, f_raw, lb, tri):
    sf = _sig(f_raw)
    fg = lb + (1.0 - lb) * sf
    b = _tri_dot(tri, jnp.log(fg))
    return q_raw * _sig(q_raw), 1.0 - fg, b, fg, sf


def _hgrn_fwd(proj, lb, norm_g, name):
    s = proj.shape[0]
    tb = _tile(s, HGRN_ROWS)
    n_c = tb // A_CHUNK
    half = A_CHUNK // 2

    def body(q_ref, f_ref, v_ref, g_ref, lb_ref, ng_ref, o_ref, yp_ref, st_ref, state):
        @pl.when(pl.program_id(0) == 0)
        def _():
            state[...] = jnp.zeros_like(state)

        tri = _tri(A_CHUNK)
        causal = lax.broadcasted_iota(jnp.int32, (A_CHUNK, A_CHUNK), 1) <= lax.broadcasted_iota(
            jnp.int32, (A_CHUNK, A_CHUNK), 0)

        def chunk(ci, carry):
            rows = pl.ds(ci * A_CHUNK, A_CHUNK)
            heads = [slice(h * HEAD_DIM, (h + 1) * HEAD_DIM) for h in range(HEADS)]
            qs, k, b, _, _ = _hgrn_gates(q_ref[rows, :], f_ref[rows, :], lb_ref[...], tri)
            b_mid, b_last = b[half:half + 1], b[A_CHUNK - 1:A_CHUNK]
            q_i = (qs * jnp.exp(b - b_mid)).astype(BF16)
            k_i = (k * jnp.exp(b_mid - b)).astype(BF16)
            q_e = (qs * jnp.exp(b)).astype(BF16)
            k_s = (k * jnp.exp(b_last - b)).astype(BF16)
            decay = jnp.exp(b_last)
            vb = v_ref[rows, :].astype(BF16)
            scores = [jnp.where(causal, _dot_nt(q_i[:, cs], k_i[:, cs]), 0.0).astype(BF16) for cs in heads]
            st = [state[h] for h in range(HEADS)]
            outs = [_dot(scores[h], vb[:, cs]) + _dot_nt(q_e[:, cs], st[h].astype(BF16)) for h, cs in enumerate(heads)]
            for h, cs in enumerate(heads):
                st_ref[ci, h] = st[h]
                state[h] = st[h] * decay[:, cs] + _dot_tn(vb[:, cs], k_s[:, cs])
            o = jnp.concatenate(outs, axis=1)
            o_ref[rows, :] = o
            sq = o * o
            inv = jnp.concatenate([jnp.broadcast_to(lax.rsqrt(jnp.mean(sq[:, cs], axis=-1, keepdims=True) + EPS),
                                                    (A_CHUNK, HEAD_DIM)) for cs in heads], axis=1)
            g_raw = g_ref[rows, :]
            yp_ref[rows, :] = (o * inv * ng_ref[...] * (g_raw * _sig(g_raw))).astype(BF16)
            return carry

        for step in range(n_c):
            chunk(step, 0)

    col = lambda j: pl.BlockSpec((tb, D_MODEL), lambda i: (i, j))
    vec = _full((1, D_MODEL))
    return pl.pallas_call(
        body, name=name, grid=(s // tb,), in_specs=[col(0), col(1), col(2), col(3), vec, vec],
        out_specs=[col(0), col(0), pl.BlockSpec((n_c, HEADS, HEAD_DIM, HEAD_DIM), lambda i: (i, 0, 0, 0))],
        out_shape=[jax.ShapeDtypeStruct((s, D_MODEL), F32), jax.ShapeDtypeStruct((s, D_MODEL), BF16),
                   jax.ShapeDtypeStruct((s // A_CHUNK, HEADS, HEAD_DIM, HEAD_DIM), F32)],
        scratch_shapes=[pltpu.VMEM((HEADS, HEAD_DIM, HEAD_DIM), F32)],
    )(proj, proj, proj, proj, lb, norm_g)


def _hgrn_bwd(proj, lb, norm_g, o, states, dout, w_out, name):
    s = proj.shape[0]
    tb = _tile(s, HGRN_ROWS)
    n_c = tb // A_CHUNK
    n_b = s // tb
    half = A_CHUNK // 2

    def body(q_ref, f_ref, v_ref, g_ref, lb_ref, ng_ref, o_ref, st_ref, dout_ref, w_ref, dp_ref, dlb_ref, dng_ref,
             dstate, dyp_ref):
        @pl.when(pl.program_id(0) == 0)
        def _():
            dstate[...] = jnp.zeros_like(dstate)
            dlb_ref[...] = jnp.zeros_like(dlb_ref)
            dng_ref[...] = jnp.zeros_like(dng_ref)

        dyp_ref[...] = _dot_nt(dout_ref[0], w_ref[0])

        tri = _tri(A_CHUNK)
        tri_up = _tri(A_CHUNK, upper=True)
        row_id = lax.broadcasted_iota(jnp.int32, (A_CHUNK, D_MODEL), 0)
        causal = lax.broadcasted_iota(jnp.int32, (A_CHUNK, A_CHUNK), 1) <= lax.broadcasted_iota(
            jnp.int32, (A_CHUNK, A_CHUNK), 0)

        def chunk(cj, carry):
            ci = n_c - 1 - cj
            rows = pl.ds(ci * A_CHUNK, A_CHUNK)
            heads = [slice(h * HEAD_DIM, (h + 1) * HEAD_DIM) for h in range(HEADS)]
            cat = lambda parts: jnp.concatenate(parts, axis=1)
            per_head_mean = lambda a: cat([jnp.broadcast_to(jnp.mean(a[:, cs], axis=-1, keepdims=True),
                                                            (A_CHUNK, HEAD_DIM)) for cs in heads])
            q_raw, lbv = q_ref[rows, :], lb_ref[...]
            qs, k, b, fg, sf = _hgrn_gates(q_raw, f_ref[rows, :], lbv, tri)
            b_mid, b_last = b[half:half + 1], b[A_CHUNK - 1:A_CHUNK]
            e_qi, e_ki, e_q, e_ks = jnp.exp(b - b_mid), jnp.exp(b_mid - b), jnp.exp(b), jnp.exp(b_last - b)
            decay = jnp.exp(b_last)
            q_i, k_i, q_e, k_s = qs * e_qi, k * e_ki, qs * e_q, k * e_ks
            qib, kib, qeb, ksb = q_i.astype(BF16), k_i.astype(BF16), q_e.astype(BF16), k_s.astype(BF16)
            vb = v_ref[rows, :].astype(BF16)
            ov, g_raw, dy, ng = o_ref[rows, :], g_ref[rows, :], dyp_ref[rows, :], ng_ref[...]
            inv = lax.rsqrt(per_head_mean(ov * ov) + EPS)
            nrm = ov * inv
            sg = _sig(g_raw)
            gs = g_raw * sg
            dn = dy * ng * gs
            dng_ref[0:1, :] += jnp.sum(dy * nrm * gs, axis=0, keepdims=True)
            dg_raw = dy * nrm * ng * (sg * (1.0 + g_raw * (1.0 - sg)))
            do = (inv * (dn - nrm * per_head_mean(dn * nrm))).astype(BF16)
            st_prev = [st_ref[ci, h] for h in range(HEADS)]
            dst = [dstate[h] for h in range(HEADS)]
            dstb = [d.astype(BF16) for d in dst]
            scores = [jnp.where(causal, _dot_nt(qib[:, cs], kib[:, cs]), 0.0).astype(BF16) for cs in heads]
            d_scores = [jnp.where(causal, _dot_nt(do[:, cs], vb[:, cs]), 0.0).astype(BF16) for cs in heads]
            dv = cat([_dot_tn(scores[h], do[:, cs]) + _dot_nt(ksb[:, cs], dstb[h]) for h, cs in enumerate(heads)])
            dq_i = cat([_dot(d_scores[h], kib[:, cs]) for h, cs in enumerate(heads)])
            dk_i = cat([_dot_tn(d_scores[h], qib[:, cs]) for h, cs in enumerate(heads)])
            dq_e = cat([_dot(do[:, cs], st_prev[h].astype(BF16)) for h, cs in enumerate(heads)])
            dk_s = cat([_dot(vb[:, cs], dstb[h]) for h, cs in enumerate(heads)])
            d_decay = cat([jnp.sum(st_prev[h] * dst[h], axis=0, keepdims=True) for h in range(HEADS)])
            for h, cs in enumerate(heads):
                dstate[h] = dst[h] * decay[:, cs] + _dot_tn(do[:, cs], qeb[:, cs])
            dq = dq_i * e_qi + dq_e * e_q
            dk = dk_i * e_ki + dk_s * e_ks
            t_qi, t_ki, t_ks = dq_i * q_i, dk_i * k_i, dk_s * k_s
            db = t_qi - t_ki + dq_e * q_e - t_ks
            db_mid = jnp.sum(t_ki - t_qi, axis=0, keepdims=True)
            db_last = jnp.sum(t_ks, axis=0, keepdims=True) + d_decay * decay
            db = db + jnp.where(row_id == half, db_mid, 0.0) + jnp.where(row_id == A_CHUNK - 1, db_last, 0.0)
            dfg = _tri_dot(tri_up, db) / fg - dk
            dlb_ref[0:1, :] += jnp.sum(dfg * (1.0 - sf), axis=0, keepdims=True)
            sq = _sig(q_raw)
            dp_ref[0, rows, :] = (dq * (sq * (1.0 + q_raw * (1.0 - sq)))).astype(BF16)
            dp_ref[1, rows, :] = (dfg * (1.0 - lbv) * sf * (1.0 - sf)).astype(BF16)
            dp_ref[2, rows, :] = dv.astype(BF16)
            dp_ref[3, rows, :] = dg_raw.astype(BF16)
            return carry

        for step in range(n_c):
            chunk(step, 0)

    col = lambda j: pl.BlockSpec((tb, D_MODEL), lambda i: (n_b - 1 - i, j))
    vec = _full((1, D_MODEL))
    acc = _full((SUBLANES, D_MODEL))
    return pl.pallas_call(
        body, name=name, grid=(n_b,),
        in_specs=[col(0), col(1), col(2), col(3), vec, vec, col(0),
                  pl.BlockSpec((n_c, HEADS, HEAD_DIM, HEAD_DIM), lambda i: (n_b - 1 - i, 0, 0, 0)),
                  pl.BlockSpec((1, tb, D_MODEL), lambda i: (0, n_b - 1 - i, 0)), _full(w_out.shape)],
        out_specs=[pl.BlockSpec((4, tb, D_MODEL), lambda i: (0, n_b - 1 - i, 0)), acc, acc],
        out_shape=[jax.ShapeDtypeStruct((4, s, D_MODEL), BF16), jax.ShapeDtypeStruct((SUBLANES, D_MODEL), F32),
                   jax.ShapeDtypeStruct((SUBLANES, D_MODEL), F32)],
        scratch_shapes=[pltpu.VMEM((HEADS, HEAD_DIM, HEAD_DIM), F32), pltpu.VMEM((tb, D_MODEL), F32)],
    )(proj, proj, proj, proj, lb, norm_g, o, states, dout, w_out)


def _head_rms(raw_ref, g_ref, mult, y_ref):
    for h in range(HEADS):
        cs = slice(h * HEAD_DIM, (h + 1) * HEAD_DIM)
        xv = raw_ref[:, cs]
        inv = lax.rsqrt(jnp.mean(xv * xv, axis=-1, keepdims=True) + EPS)
        y_ref[:, cs] = (xv * inv * g_ref[:, cs] * mult).astype(BF16)


def _proj_headnorm(a, w, g, mult, name):
    s, k = a.shape
    p_n, _, n = w.shape
    tm = _tile(s, ROW_TILE)

    def body(a_ref, w_ref, g_ref, raw_ref, y_ref):
        av = a_ref[...]
        for p in range(p_n):
            raw_ref[:, p * n:(p + 1) * n] = _dot(av, w_ref[p])
        _head_rms(raw_ref, g_ref, mult, y_ref)

    row = lambda wid: pl.BlockSpec((tm, wid), lambda i: (i, 0))
    return pl.pallas_call(
        body, name=name, grid=(s // tm,), in_specs=[row(k), _full(w.shape), _full((1, D_MODEL))],
        out_specs=[row(p_n * n), row(D_MODEL)],
        out_shape=[jax.ShapeDtypeStruct((s, p_n * n), F32), jax.ShapeDtypeStruct((s, D_MODEL), BF16)],
    )(a, w, g)


def _kv_proj(hk, w_k, w_v, w_f, g, name):
    s, k = hk.shape
    tm = _tile(s, ROW_TILE)

    def body(h_ref, wk_ref, wv_ref, wf_ref, g_ref, kr_ref, k_ref, v_ref, f_ref):
        hv = h_ref[...]
        kr_ref[...] = _dot(hv, wk_ref[0])
        v_ref[...] = _dot(hv, wv_ref[0]).astype(BF16)
        f_ref[...] = _dot(hv, wf_ref[0])
        _head_rms(kr_ref, g_ref, 1.0, k_ref)

    row = lambda wid: pl.BlockSpec((tm, wid), lambda i: (i, 0))
    return pl.pallas_call(
        body, name=name, grid=(s // tm,),
        in_specs=[row(k), _full(w_k.shape), _full(w_v.shape), _full(w_f.shape), _full((1, D_MODEL))],
        out_specs=[row(D_MODEL), row(D_MODEL), row(D_MODEL), row(LANES)],
        out_shape=[jax.ShapeDtypeStruct((s, D_MODEL), F32), jax.ShapeDtypeStruct((s, D_MODEL), BF16),
                   jax.ShapeDtypeStruct((s, D_MODEL), BF16), jax.ShapeDtypeStruct((s, LANES), F32)],
    )(hk, w_k, w_v, w_f, g)


def _headnorm_bwd(x, g, mult, dy, name, col0=0, extra=None):
    s = x.shape[0]
    tm = _tile(s, ROW_TILE)
    groups = 2 if extra is not None else 1
    head_major = dy.ndim == 3

    def body(*refs):
        x_ref, g_ref, dy_ref = refs[:3]
        dx_ref, dg_ref = refs[-2:]

        @pl.when(pl.program_id(0) == 0)
        def _():
            dg_ref[...] = jnp.zeros_like(dg_ref)

        for h in range(HEADS):
            cs = slice(h * HEAD_DIM, (h + 1) * HEAD_DIM)
            xv, gv = x_ref[:, cs], g_ref[:, cs]
            dyv = dy_ref[h, :, 0:HEAD_DIM] if head_major else dy_ref[:, cs]
            inv = lax.rsqrt(jnp.mean(xv * xv, axis=-1, keepdims=True) + EPS)
            nrm = xv * inv
            dn = dyv * gv * mult
            dg_ref[:, cs] += _colsum8(dyv * nrm * mult)
            dx_ref[0, :, cs] = (inv * (dn - nrm * jnp.mean(dn * nrm, axis=-1, keepdims=True))).astype(BF16)
        if extra is not None:
            dx_ref[1] = refs[3][...]

    row = pl.BlockSpec((tm, D_MODEL), lambda i: (i, 0))
    dy_spec = pl.BlockSpec((HEADS, tm, dy.shape[-1]), lambda i: (0, i, 0)) if head_major else row
    ins = [x, g, dy] + ([extra] if extra is not None else [])
    specs = ([pl.BlockSpec((tm, D_MODEL), lambda i: (i, col0)), _full((1, D_MODEL)), dy_spec]
             + ([row] if extra is not None else []))
    return pl.pallas_call(
        body, name=name, grid=(s // tm,), in_specs=specs,
        out_specs=[pl.BlockSpec((groups, tm, D_MODEL), lambda i: (0, i, 0)), _full((SUBLANES, D_MODEL))],
        out_shape=[jax.ShapeDtypeStruct((groups, s, D_MODEL), BF16), jax.ShapeDtypeStruct((SUBLANES, D_MODEL), F32)],
    )(*ins)


def _log_sigmoid(z):
    return jnp.minimum(z, 0.0) - jnp.log(1.0 + jnp.exp(-jnp.abs(z)))


Q_CUM, Q_ONE, Q_LSE = 0, 3, 6
LOG2E = 1.4426950408889634


def _pieces(v):
    hi = v.astype(BF16).astype(F32)
    mid = (v - hi).astype(BF16).astype(F32)
    lo = ((v - hi) - mid).astype(BF16).astype(F32)
    return hi, mid, lo


def _side(lane, at, v):
    hi, mid, lo = _pieces(v)
    return jnp.where(lane == at, hi, jnp.where(lane == at + 1, mid, jnp.where(lane == at + 2, lo, 0.0)))


def _fcum_fwd(f, bias, name):
    s = f.shape[0]
    tm = _tile(s, ROW_TILE)

    def body(f_ref, b_ref, qa_ref, ka_ref, carry):
        @pl.when(pl.program_id(0) == 0)
        def _():
            carry[...] = jnp.zeros_like(carry)

        cum = _tri_dot(_tri(tm), _log_sigmoid(f_ref[...] + b_ref[...])) + carry[...]
        carry[...] = cum[tm - 1:tm]
        lane = lax.broadcasted_iota(jnp.int32, (tm, LANES), 1)
        ones_q = jnp.where((lane >= Q_ONE) & (lane < Q_LSE), 1.0, 0.0)
        ones_k = jnp.where((lane < Q_ONE) | ((lane >= Q_LSE) & (lane < Q_LSE + 3)), 1.0, 0.0)
        for h in range(HEADS):
            c2 = cum[:, h:h + 1] * LOG2E
            qa_ref[h] = (_side(lane, Q_CUM, c2) + ones_q).astype(BF16)
            ka_ref[h] = (_side(lane, Q_ONE, -c2) + ones_k).astype(BF16)

    side = pl.BlockSpec((HEADS, tm, LANES), lambda i: (0, i, 0))
    return pl.pallas_call(
        body, name=name, grid=(s // tm,),
        in_specs=[pl.BlockSpec((tm, LANES), lambda i: (i, 0)), _full((1, LANES))],
        out_specs=[side, side],
        out_shape=[jax.ShapeDtypeStruct((HEADS, s, LANES), BF16)] * 2,
        scratch_shapes=[pltpu.VMEM((1, LANES), F32)],
    )(f, bias)


def _fcum_bwd(f, bias, dka, dcq, name):
    s = f.shape[0]
    tm = _tile(s, ROW_TILE)
    n_b = s // tm

    def body(f_ref, b_ref, dka_ref, dcq_ref, dz_ref, db_ref, carry):
        @pl.when(pl.program_id(0) == 0)
        def _():
            carry[...] = jnp.zeros_like(carry)
            db_ref[...] = jnp.zeros_like(db_ref)

        lane = lax.broadcasted_iota(jnp.int32, (tm, LANES), 1)
        rows = jnp.concatenate([dcq_ref[h] for h in range(HEADS)] + [jnp.zeros((LANES - HEADS, tm), F32)], axis=0)
        dcum = rows.T
        for h in range(HEADS):
            dcum = dcum - jnp.where(lane == h, dka_ref[h, :, Q_ONE:Q_ONE + 1], 0.0)
        dlf = _tri_dot(_tri(tm, upper=True), dcum) + carry[...]
        carry[...] = dlf[0:1]
        dz = dlf * _sig(-(f_ref[...] + b_ref[...]))
        dz_ref[0] = dz.astype(BF16)
        db_ref[...] += _colsum8(dz)

    return pl.pallas_call(
        body, name=name, grid=(n_b,),
        in_specs=[pl.BlockSpec((tm, LANES), lambda i: (n_b - 1 - i, 0)), _full((1, LANES)),
                  pl.BlockSpec((HEADS, tm, LANES), lambda i: (0, n_b - 1 - i, 0)),
                  pl.BlockSpec((HEADS, 1, tm), lambda i: (0, 0, n_b - 1 - i))],
        out_specs=[pl.BlockSpec((1, tm, LANES), lambda i: (0, n_b - 1 - i, 0)), _full((SUBLANES, LANES))],
        out_shape=[jax.ShapeDtypeStruct((1, s, LANES), BF16), jax.ShapeDtypeStruct((SUBLANES, LANES), F32)],
        scratch_shapes=[pltpu.VMEM((1, LANES), F32)],
    )(f, bias, dka, dcq)


def _causal_pairs(n_t, key_major):
    if key_major:
        pairs = [(qi, ki) for ki in range(n_t) for qi in range(ki, n_t)]
    else:
        pairs = [(qi, ki) for qi in range(n_t) for ki in range(qi + 1)]
    return (jnp.array([p[0] for p in pairs], jnp.int32), jnp.array([p[1] for p in pairs], jnp.int32))


def _lane_const(t, lo, hi, value):
    lane = lax.broadcasted_iota(jnp.int32, (t, LANES), 1)
    return jnp.where((lane >= lo) & (lane < hi), value, 0.0).astype(BF16)


def _att_specs(t, nh):
    qmain = pl.BlockSpec((t, nh * HEAD_DIM), lambda h, p, qt, kt: (qt[p], h))
    kmain = pl.BlockSpec((t, nh * HEAD_DIM), lambda h, p, qt, kt: (kt[p], h))
    qside = pl.BlockSpec((nh, t, LANES), lambda h, p, qt, kt: (h, qt[p], 0))
    kside = pl.BlockSpec((nh, t, LANES), lambda h, p, qt, kt: (h, kt[p], 0))
    return qmain, kmain, qside, kside


def _fox_fwd(q, qa, k, ka, v, qo, name):
    s = q.shape[0]
    t = _tile(s, ATT_TILE)
    sub = t // ATT_SPLIT
    nh = ATT_FWD_HEADS
    qt, kt = _causal_pairs(s // t, key_major=False)

    def body(qt_ref, kt_ref, q_ref, qa_ref, k_ref, ka_ref, v_ref, og_ref, o_ref, y_ref, qab_ref, m_s, l_s, acc_s):
        pid = pl.program_id(1)
        qi, ki = qt_ref[pid], kt_ref[pid]

        @pl.when(ki == 0)
        def _():
            m_s[...] = jnp.full_like(m_s, NEG_INF)
            l_s[...] = jnp.zeros_like(l_s)
            acc_s[...] = jnp.zeros_like(acc_s)

        def step(diagonal):
            for hh in range(nh):
                hc = slice(hh * HEAD_DIM, (hh + 1) * HEAD_DIM)
                kc = jnp.concatenate([k_ref[:, hc], ka_ref[hh]], axis=1)
                vc = jnp.concatenate([v_ref[:, hc], _lane_const(t, 0, 1, 1.0)], axis=1)
                for r in range(ATT_SPLIT):
                    rows = slice(r * sub, (r + 1) * sub)
                    n_k = (r + 1) * sub if diagonal else t
                    sc = _dot_nt(jnp.concatenate([q_ref[rows, hc], qa_ref[hh, rows]], axis=1), kc[:n_k])
                    if diagonal:
                        sc = jnp.where(lax.broadcasted_iota(jnp.int32, (sub, n_k), 1)
                                       <= lax.broadcasted_iota(jnp.int32, (sub, n_k), 0) + r * sub, sc, NEG_INF)
                    m_old = m_s[hh, rows]
                    m_new = jnp.maximum(m_old, jnp.max(sc, axis=-1, keepdims=True))
                    alpha = jnp.exp2(m_old - m_new)
                    pv = _dot(jnp.exp2(sc - m_new[:, 0:1]).astype(BF16), vc[:n_k])
                    acc_s[hh, rows] = alpha * acc_s[hh, rows] + pv[:, :HEAD_DIM]
                    l_s[hh, rows] = alpha * l_s[hh, rows] + pv[:, HEAD_DIM:]
                    m_s[hh, rows] = m_new

        @pl.when(ki < qi)
        def _():
            step(False)

        @pl.when(ki == qi)
        def _():
            step(True)
            lane = lax.broadcasted_iota(jnp.int32, (t, LANES), 1)
            for hh in range(nh):
                hc = slice(hh * HEAD_DIM, (hh + 1) * HEAD_DIM)
                l = l_s[hh, :, 0:1]
                o = acc_s[hh] / l
                o_ref[:, hc] = o
                y_ref[:, hc] = (o * _sig(og_ref[:, hc])).astype(BF16)
                qab_ref[hh] = qa_ref[hh] + _side(lane, Q_LSE, -(m_s[hh, :, 0:1] + jnp.log2(l))).astype(BF16)

    qmain, kmain, qside, kside = _att_specs(t, nh)
    return pl.pallas_call(
        body, name=name,
        grid_spec=pltpu.PrefetchScalarGridSpec(
            num_scalar_prefetch=2, grid=(HEADS // nh, qt.shape[0]),
            in_specs=[qmain, qside, kmain, kside, kmain,
                      pl.BlockSpec((t, nh * HEAD_DIM), lambda h, p, qt, kt: (qt[p], HEADS // nh + h))],
            out_specs=[qmain, qmain, qside],
            scratch_shapes=[pltpu.VMEM((nh, t, LANES), F32), pltpu.VMEM((nh, t, LANES), F32),
                            pltpu.VMEM((nh, t, HEAD_DIM), F32)]),
        out_shape=[jax.ShapeDtypeStruct((s, D_MODEL), F32), jax.ShapeDtypeStruct((s, D_MODEL), BF16),
                   jax.ShapeDtypeStruct((HEADS, s, LANES), BF16)],
    )(qt, kt, q, qa, k, ka, v, qo)


def _fox_gate_bwd(o, qo, dout, w_out, name):
    s = o.shape[0]
    tm = _tile(s, ROW_TILE)

    def body(o_ref, og_ref, dout_ref, w_ref, do_ref, dg_ref, dl_ref):
        ov, dyv = o_ref[...], _dot_nt(dout_ref[0], w_ref[0])
        sg = _sig(og_ref[...])
        do = (dyv * sg).astype(BF16)
        do_ref[...] = do
        dg_ref[...] = (dyv * ov * sg * (1.0 - sg)).astype(BF16)
        prod = do.astype(F32) * ov
        lane = lax.broadcasted_iota(jnp.int32, (tm, LANES), 1)
        for h in range(HEADS):
            delta = jnp.sum(prod[:, h * HEAD_DIM:(h + 1) * HEAD_DIM], axis=-1, keepdims=True)
            dl_ref[h] = _side(lane, 0, delta).astype(BF16)

    row = pl.BlockSpec((tm, D_MODEL), lambda i: (i, 0))
    return pl.pallas_call(
        body, name=name, grid=(s // tm,),
        in_specs=[row, pl.BlockSpec((tm, D_MODEL), lambda i: (i, 1)),
                  pl.BlockSpec((1, tm, D_MODEL), lambda i: (0, i, 0)), _full(w_out.shape)],
        out_specs=[row, row, pl.BlockSpec((HEADS, tm, LANES), lambda i: (0, i, 0))],
        out_shape=[jax.ShapeDtypeStruct((s, D_MODEL), BF16), jax.ShapeDtypeStruct((s, D_MODEL), BF16),
                   jax.ShapeDtypeStruct((HEADS, s, LANES), BF16)],
    )(o, qo, dout, w_out)


def _fox_bwd(q, qab, k, ka, v, do, doa, k_raw, k_gain, name):
    s = q.shape[0]
    t = _tile(s, ATT_TILE)
    n_t = s // t
    sub = t // ATT_SPLIT
    nh = ATT_BWD_HEADS
    qt, kt = _causal_pairs(n_t, key_major=True)

    def body(qt_ref, kt_ref, q_ref, qab_ref, k_ref, ka_ref, v_ref, do_ref, doa_ref, kr_ref, kg_ref, dkr_ref, dkg_ref,
             dv_ref, dka_ref, dq_hbm, dcq_hbm, dk_s, dv_s, dq_ref, dcq_ref):
        group, pid = pl.program_id(0), pl.program_id(1)
        qi, ki = qt_ref[pid], kt_ref[pid]

        @pl.when(pid == 0)
        def _():
            dq_ref[...] = jnp.zeros_like(dq_ref)
            dcq_ref[...] = jnp.zeros_like(dcq_ref)
            dkg_ref[...] = jnp.zeros_like(dkg_ref)

        @pl.when(qi == ki)
        def _():
            dk_s[...] = jnp.zeros_like(dk_s)
            dv_s[...] = jnp.zeros_like(dv_s)

        def step(diagonal):
            for hh in range(nh):
                hc = slice(hh * HEAD_DIM, (hh + 1) * HEAD_DIM)
                kc = jnp.concatenate([k_ref[:, hc], ka_ref[hh]], axis=1)
                vc = jnp.concatenate([v_ref[:, hc], _lane_const(t, 0, 3, -1.0)], axis=1)
                for r in range(ATT_SPLIT):
                    cols = slice(r * sub, (r + 1) * sub)
                    n_k = (r + 1) * sub if diagonal else t
                    qc = jnp.concatenate([q_ref[cols, hc], qab_ref[hh, cols]], axis=1)
                    sc = _dot_nt(kc[:n_k], qc)
                    if diagonal:
                        sc = jnp.where(lax.broadcasted_iota(jnp.int32, (n_k, sub), 0)
                                       <= lax.broadcasted_iota(jnp.int32, (n_k, sub), 1) + r * sub, sc, NEG_INF)
                    p = jnp.exp2(sc)
                    dov = do_ref[cols, hc]
                    dp = _dot_nt(vc[:n_k], jnp.concatenate([dov, doa_ref[hh, cols]], axis=1))
                    ds = (p * dp).astype(BF16)
                    dv_s[hh, 0:n_k] += _dot(p.astype(BF16), dov)
                    dk_s[hh, 0:n_k] += _dot(ds, qc)
                    q_rows = pl.ds(pl.multiple_of(qi * t + r * sub, sub), sub)
                    dq_ref[hh, q_rows, :] += _dot_tn(ds, k_ref[0:n_k, hc])
                    dcq_ref[hh, qi * ATT_SPLIT + r] += jnp.sum(ds.astype(F32), axis=0, keepdims=True)

        @pl.when(qi > ki)
        def _():
            step(False)

        @pl.when(qi == ki)
        def _():
            step(True)

        @pl.when(qi == n_t - 1)
        def _():
            for hh in range(nh):
                hc = slice(hh * HEAD_DIM, (hh + 1) * HEAD_DIM)
                dka_ref[hh] = dk_s[hh, :, HEAD_DIM:]
                dv_ref[:, hc] = dv_s[hh].astype(BF16)
                dk = dk_s[hh, :, :HEAD_DIM] * (1.0 / LOG2E)
                xv = kr_ref[:, hc]
                inv = lax.rsqrt(jnp.mean(xv * xv, axis=-1, keepdims=True) + EPS)
                nrm = xv * inv
                dn = dk * kg_ref[:, hc]
                dkg_ref[:, hc] += _colsum8(dk * nrm)
                dkr_ref[:, hc] = (inv * (dn - nrm * jnp.mean(dn * nrm, axis=-1, keepdims=True))).astype(BF16)

        @pl.when(pid == qt.shape[0] - 1)
        def _():
            pltpu.sync_copy(dq_ref, dq_hbm.at[pl.ds(group * nh, nh)])
            pltpu.sync_copy(dcq_ref, dcq_hbm.at[pl.ds(group * nh, nh)])

    qmain, kmain, qside, kside = _att_specs(t, nh)
    kmain3 = pl.BlockSpec((None, t, nh * HEAD_DIM), lambda h, p, qt, kt: (0, kt[p], h))
    in_hbm = pl.BlockSpec(memory_space=pltpu.HBM)
    return pl.pallas_call(
        body, name=name,
        grid_spec=pltpu.PrefetchScalarGridSpec(
            num_scalar_prefetch=2, grid=(HEADS // nh, qt.shape[0]),
            in_specs=[qmain, qside, kmain, kside, kmain, qmain, qside, kmain,
                      pl.BlockSpec((1, nh * HEAD_DIM), lambda h, p, qt, kt: (0, h))],
            out_specs=[kmain3, pl.BlockSpec((SUBLANES, nh * HEAD_DIM), lambda h, p, qt, kt: (0, h)), kmain3, kside,
                       in_hbm, in_hbm],
            scratch_shapes=[pltpu.VMEM((nh, t, 2 * HEAD_DIM), F32), pltpu.VMEM((nh, t, HEAD_DIM), F32),
                            pltpu.VMEM((nh, s, HEAD_DIM), F32), pltpu.VMEM((nh, s // sub, 1, sub), F32)]),
        out_shape=[jax.ShapeDtypeStruct((1, s, D_MODEL), BF16), jax.ShapeDtypeStruct((SUBLANES, D_MODEL), F32),
                   jax.ShapeDtypeStruct((1, s, D_MODEL), BF16),
                   jax.ShapeDtypeStruct((HEADS, s, LANES), F32), jax.ShapeDtypeStruct((HEADS, s, HEAD_DIM), F32),
                   jax.ShapeDtypeStruct((HEADS, s // sub, 1, sub), F32)],
    )(qt, kt, q, qab, k, ka, v, do, doa, k_raw, k_gain)


def _mm_residual_premix(a, w, x, gate, mods, name):
    s, k = a.shape
    dm = x.shape[1]
    tm = _tile(s, ROW_TILE)

    def body(*refs):
        a_ref, w_ref, x_ref, g_ref = refs[:4]
        mod_refs = refs[4:4 + 2 * len(mods)]
        y_ref, xn_ref = refs[4 + 2 * len(mods):6 + 2 * len(mods)]
        h_refs = refs[6 + 2 * len(mods):]
        y = _dot(a_ref[...], w_ref[0])
        y_ref[...] = y
        xv = x_ref[...] + g_ref[...] * y
        xn_ref[...] = xv
        nrm = xv * lax.rsqrt(jnp.mean(xv * xv, axis=-1, keepdims=True) + EPS)
        for t, h_ref in enumerate(h_refs):
            h_ref[...] = (nrm * (1.0 + mod_refs[2 * t + 1][...]) + mod_refs[2 * t][...]).astype(BF16)

    row = pl.BlockSpec((tm, dm), lambda i: (i, 0))
    vec = _full((1, dm))
    outs = pl.pallas_call(
        body, name=name, grid=(s // tm,),
        in_specs=[pl.BlockSpec((tm, k), lambda i: (i, 0)), _full(w.shape), row, vec] + [vec] * (2 * len(mods)),
        out_specs=[row] * (2 + len(mods)),
        out_shape=[jax.ShapeDtypeStruct((s, dm), F32)] * 2 + [jax.ShapeDtypeStruct((s, dm), BF16)] * len(mods),
    )(a, w, x, gate, *[v for m in mods for v in m])
    return outs[0], outs[1], list(outs[2:])


def _mm_loss_head(a, w, x, gate, target, name):
    s, k = a.shape
    dm = x.shape[1]
    tm = _tile(s, ROW_TILE)

    def body(a_ref, w_ref, x_ref, g_ref, t_ref, sq_ref, do_ref, dy_ref, dg_ref):
        @pl.when(pl.program_id(0) == 0)
        def _():
            sq_ref[...] = jnp.zeros_like(sq_ref)
            dg_ref[...] = jnp.zeros_like(dg_ref)

        y, gv = _dot(a_ref[...], w_ref[0]), g_ref[...]
        err = x_ref[...] + gv * y - t_ref[...]
        sq_ref[...] += _colsum8(err * err)
        dout = err * (1.0 / dm)
        do_ref[...] = dout
        dy_ref[0] = (dout * gv).astype(BF16)
        dg_ref[...] += _colsum8(dout * y)

    row = pl.BlockSpec((tm, dm), lambda i: (i, 0))
    acc = _full((SUBLANES, dm))
    return pl.pallas_call(
        body, name=name, grid=(s // tm,),
        in_specs=[pl.BlockSpec((tm, k), lambda i: (i, 0)), _full(w.shape), row, _full((1, dm)), row],
        out_specs=[acc, row, pl.BlockSpec((1, tm, dm), lambda i: (0, i, 0)), acc],
        out_shape=[jax.ShapeDtypeStruct((SUBLANES, dm), F32), jax.ShapeDtypeStruct((s, dm), F32),
                   jax.ShapeDtypeStruct((1, s, dm), BF16), jax.ShapeDtypeStruct((SUBLANES, dm), F32)],
    )(a, w, x, gate, target)


def _ffn_inner(h, w_up, conv_w, conv_b, tag):
    s, dm = h.shape
    half = w_up.shape[2]
    f = 2 * half
    tm = _tile(s, FFN_ROWS)

    def body(h_ref, w_ref, cw_ref, cb_ref, u_ref, c_ref, a_ref, carry):
        @pl.when(pl.program_id(0) == 0)
        def _():
            carry[...] = jnp.zeros_like(carry)

        hv = h_ref[...]
        for j in range(2):
            cols = slice(j * half, (j + 1) * half)
            conv = []
            for g in range(2):
                ub = _dot(hv, w_ref[2 * g + j]).astype(BF16)
                u_ref[g, :, cols] = ub
                uf = ub.astype(F32)
                e = jnp.concatenate([carry[g, j], uf], axis=0)
                carry[g, j] = uf[tm - SUBLANES:tm]
                conv.append(_conv_taps(e, cw_ref[g][:, cols], cb_ref[g][:, cols])[SUBLANES:])
                c_ref[g, :, cols] = conv[g].astype(BF16)
            a_ref[:, cols] = (conv[0] * _sig(conv[0]) * conv[1]).astype(BF16)

    pair = pl.BlockSpec((2, tm, f), lambda i: (0, i, 0))
    return pl.pallas_call(
        body, name=tag + "_up_convglu", grid=(s // tm,),
        in_specs=[pl.BlockSpec((tm, dm), lambda i: (i, 0)), _full(w_up.shape), _full(conv_w.shape), _full(conv_b.shape)],
        out_specs=[pair, pair, pl.BlockSpec((tm, f), lambda i: (i, 0))],
        out_shape=[jax.ShapeDtypeStruct((2, s, f), BF16)] * 2 + [jax.ShapeDtypeStruct((s, f), BF16)],
        scratch_shapes=[pltpu.VMEM((2, 2, SUBLANES, half), F32)],
    )(h, w_up, conv_w, conv_b)


def _weight_grad_first(a, d, p_n, name):
    return lax.optimization_barrier((_mm_tn(a, d, p_n, name), d))


def _ffn_backward(dx_out, dffn, x_mid, scale, saved, w_up, conv_w, conv_b, w_down, mixer, tag):
    h, u, c, a = saved
    dw_down, dffn = _weight_grad_first(a, dffn, 1, tag + "_down_dw")
    du, dconv = _convglu_bwd(u, c, dffn, w_down, conv_w, tag + "_convglu_bwd")
    dw_up, du = _weight_grad_first(h, du, N_CHIPS, tag + "_up_dw")
    dx_mid, [(dshift, dscale)], dy, dgate_mixer = _premix_bwd(x_mid, [(scale, [(du, w_up)])], dx_out,
                                                              tag + "_premix_bwd", branch=mixer)
    return dx_mid, dy, dgate_mixer, dw_up, dw_down, dict(shift=dshift, scale=dscale, conv=dconv)


def _local_step(x, target, mods, lb, vecs, weights_at):
    m0, m1, mk = mods["l0"], mods["l1"], mods["kv"]
    wts, x = weights_at("mixer0", x)
    h0, proj = _premix_proj(x, m0[0], m0[1], wts["a_w_in"], "l0_premix_in")
    _, proj = weights_at("launch_layer1", proj)
    o_a, yp, states = _hgrn_fwd(proj, lb, vecs["a_norm_g"], "l0_hgrn")
    more, yp = weights_at("out0", yp)
    wts.update(more)
    y0, x1, [hf0] = _mm_residual_premix(yp, wts["a_w_out"], x, m0[2], [(m0[3], m0[4])], "l0_out")
    more, hf0 = weights_at("ffn0", hf0)
    wts.update(more)
    u0, c0, a0 = _ffn_inner(hf0, wts["up0"], vecs["conv_w0"], vecs["conv_b0"], "l0_ffn")
    saved0 = (hf0, u0, c0, a0)
    ffn0, x2, [hk, h1] = _mm_residual_premix(a0, wts["down0"], x1, m0[5], [(mk[0], mk[1]), (m1[0], m1[1])],
                                             "l0_ffn_down")
    more, hk = weights_at("layer1", hk)
    wts.update(more)
    k_raw, k_sh, v_sh, f_raw = _kv_proj(hk, wts["kv_k"], wts["kv_v"], wts["kv_f"], vecs["k_norm_g"], "kv_proj")
    qa, ka = _fcum_fwd(f_raw, vecs["kv_b_f"], "kv_fcum")
    q_scale = HEAD_DIM ** -0.5
    qo, q = _proj_headnorm(h1, wts["b_w_q"], vecs["q_norm_g"], q_scale * LOG2E, "l1_q")
    o_b, og, qab = _fox_fwd(q, qa, k_sh, ka, v_sh, qo, "l1_fox")
    y1, x3, [hf1] = _mm_residual_premix(og, wts["b_w_out"], x2, m1[2], [(m1[3], m1[4])], "l1_out")
    u1, c1, a1 = _ffn_inner(hf1, wts["up1"], vecs["conv_w1"], vecs["conv_b1"], "l1_ffn")
    saved1 = (hf1, u1, c1, a1)
    sq, dx4, dffn1, dg2_1 = _mm_loss_head(a1, wts["down1"], x3, m1[5], target, "l1_ffn_down")

    big, small = {}, {}
    dx3, dy1, dg1_1, big["up1"], big["down1"], s_ffn1 = _ffn_backward(
        dx4, dffn1, x3, m1[4], saved1, wts["up1"], vecs["conv_w1"], vecs["conv_b1"], wts["down1"], (y1, m1[2]), "l1_ffn")
    big["b_w_out"], dy1 = _weight_grad_first(og, dy1, 1, "l1_out_dw")
    do_b, dgate_b, doa = _fox_gate_bwd(o_b, qo, dy1, wts["b_w_out"], "l1_out_dx_gate_bwd")
    dk_raw, dkg, dv, dka, dq, dcq = _fox_bwd(q, qab, k_sh, ka, v_sh, do_b, doa, k_raw, vecs["k_norm_g"], "l1_fox_bwd")
    dqo, dqg = _headnorm_bwd(qo, vecs["q_norm_g"], q_scale, dq, "l1_qnorm_bwd", extra=dgate_b)
    big["b_w_q"], dqo = _weight_grad_first(h1, dqo, N_CHIPS, "l1_q_dw")
    dz, dbf = _fcum_bwd(f_raw, vecs["kv_b_f"], dka, dcq.reshape(HEADS, 1, -1), "kv_fcum_bwd")
    big["kv_k"], dk_raw = _weight_grad_first(hk, dk_raw, 1, "kv_k_dw")
    big["kv_v"], dv = _weight_grad_first(hk, dv, 1, "kv_v_dw")
    big["kv_f"], dz = _weight_grad_first(hk, dz, 1, "kv_f_dw")
    kv_pairs = [(dk_raw, wts["kv_k"]), (dv, wts["kv_v"]), (dz, wts["kv_f"])]
    dx2, [(dsh1_1, dsc1_1), (dshk, dsck)], dffn0, dg2_0 = _premix_bwd(
        x2, [(m1[1], [(dqo, wts["b_w_q"])]), (mk[1], kv_pairs)], dx3, "l1_kv_premix_bwd", branch=(ffn0, m0[5]))
    dx1, dy0, dg1_0, big["up0"], big["down0"], s_ffn0 = _ffn_backward(
        dx2, dffn0, x1, m0[4], saved0, wts["up0"], vecs["conv_w0"], vecs["conv_b0"], wts["down0"], (y0, m0[2]), "l0_ffn")
    big["a_w_out"], dy0 = _weight_grad_first(yp, dy0, 1, "l0_out_dw")
    dproj, dlb, dng = _hgrn_bwd(proj, lb, vecs["a_norm_g"], o_a, states, dy0, wts["a_w_out"], "l0_out_dx_hgrn_bwd")
    grad_x, [(dsh1_0, dsc1_0)] = _premix_bwd(x, [(m0[1], [(dproj, wts["a_w_in"])])], dx1, "l0_premix_bwd")
    dproj, _ = lax.optimization_barrier((dproj, (dsh1_0, dsc1_0)))
    big["a_w_in"] = _mm_tn(h0, dproj, N_CHIPS, "l0_in_dw")

    small["mod_l0"] = [dsh1_0, dsc1_0, dg1_0, s_ffn0["shift"], s_ffn0["scale"], dg2_0]
    small["mod_l1"] = [dsh1_1, dsc1_1, dg1_1, s_ffn1["shift"], s_ffn1["scale"], dg2_1]
    small["mod_kv"] = [dshk, dsck]
    small["conv0"], small["conv1"] = s_ffn0["conv"], s_ffn1["conv"]
    small["a_norm_g"], small["k_norm_g"], small["q_norm_g"] = dng, dkg, dqg
    small["kv_b_f"], small["lb"] = dbf, dlb
    marks = {"attention_bwd": dv, "ffn0_bwd": dx1, "mixer0_bwd": grad_x}
    return sq, grad_x, big, small, marks


COMM_CHUNK_ELEMS = 256 * 1024


def _place():
    x, y, c = lax.axis_index("x"), lax.axis_index("y"), lax.axis_index("c")
    chips = [(1 - x, y), (x, 1 - y), (1 - x, 1 - y)]
    return x, y, c, (x, y, 1 - c), chips


def _chunk_rows(rows, cols):
    best = BF16_ROWS
    for r in range(BF16_ROWS, rows + 1, BF16_ROWS):
        if rows % r == 0 and r * cols <= COMM_CHUNK_ELEMS:
            best = r
    assert rows % best == 0, (rows, cols)
    return best


def _allgather8(block, name):
    m_per, n = block.shape

    def body(x_ref, out_ref, send_sems, recv_sems, local_sem):
        x, y, c, sibling, chips = _place()
        me = (x, y, c)

        def rows(px, py, pc):
            return out_ref.at[pl.ds((4 * px + 2 * py + pc) * m_per, m_per), :]

        def copy(k, blk, to, src=None):
            return pltpu.make_async_remote_copy(
                src_ref=rows(*blk) if src is None else src, dst_ref=rows(*blk),
                send_sem=send_sems.at[k], recv_sem=recv_sems.at[k], device_id=to, device_id_type=MESH)

        mine = pltpu.make_async_copy(x_ref, rows(*me), local_sem)
        mine.start()
        first = [copy(0, me, sibling, src=x_ref)]
        first += [copy(1 + j, me, (*chip, c), src=x_ref) for j, chip in enumerate(chips)]
        for cp in first:
            cp.start()
        passed = [copy(4 + j, (*chip, c), sibling) for j, chip in enumerate(chips)]
        for j, chip in enumerate(chips):
            copy(1 + j, (*chip, c), me).wait_recv()
            passed[j].start()
        copy(0, sibling, me).wait_recv()
        for j, chip in enumerate(chips):
            copy(4 + j, (*chip, 1 - c), me).wait_recv()
        for cp in first + passed:
            cp.wait_send()
        mine.wait()

    return pl.pallas_call(
        body, name=name, out_shape=jax.ShapeDtypeStruct((N_DEV * m_per, n), block.dtype),
        in_specs=[pl.BlockSpec(memory_space=pltpu.VMEM)], out_specs=pl.BlockSpec(memory_space=pltpu.VMEM),
        scratch_shapes=[pltpu.SemaphoreType.DMA((7,)), pltpu.SemaphoreType.DMA((7,)), pltpu.SemaphoreType.DMA],
    )(block)


def _cast_own_block(shards, layer, chip, name):
    _, r, cols = shards.shape
    rows = _chunk_rows(r, cols)

    def body(chip_ref, w_ref, o_ref):
        o_ref[...] = w_ref[...].astype(BF16)

    return pl.pallas_call(
        body, name=name,
        grid_spec=pltpu.PrefetchScalarGridSpec(
            num_scalar_prefetch=1, grid=(r // rows,),
            in_specs=[pl.BlockSpec((None, rows, cols), lambda i, chip_ref: (layer, i, 0))],
            out_specs=pl.BlockSpec((None, rows, cols), lambda i, chip_ref: (chip_ref[0], i, 0))),
        out_shape=jax.ShapeDtypeStruct((N_CHIPS, r, cols), BF16),
    )(chip, shards)


def _sequencer_gather(bufs, name, collective_id):
    n_t = len(bufs)
    dims = [b.shape[1:] for b in bufs]
    refs = [jax.new_ref(b, memory_space=pltpu.MemorySpace.HBM) for b in bufs]

    @pl.kernel(mesh=plsc.ScalarSubcoreMesh(axis_name="sequencer", num_cores=1), name=name,
               scratch_types=[pltpu.SemaphoreType.DMA((n_t,)), pltpu.SemaphoreType.DMA((3 * n_t,)),
                              pltpu.SemaphoreType.DMA((n_t,)), pltpu.SemaphoreType.DMA((n_t,))],
               compiler_params=pltpu.CompilerParams(collective_id=collective_id))
    def launch(send_ici, recv_ici, send_d2d, recv_d2d):
        x, y, c, sibling, chips = _place()
        p_me = 2 * x + y
        peers = [sibling] + [(cx, cy, c) for cx, cy in chips]
        barrier = pltpu.get_barrier_semaphore()
        for peer in peers:
            pl.semaphore_signal(barrier, inc=1, device_id=peer, device_id_type=MESH)
        pl.semaphore_wait(barrier, len(peers))

        def waiter(t, sem_s, sem_r):
            win = refs[t].at[pl.ds(0, 3), pl.ds(0, dims[t][0] // 2), :]
            return pltpu.make_async_remote_copy(src_ref=win, dst_ref=win, send_sem=sem_s.at[t], recv_sem=sem_r.at[t],
                                                device_id=sibling, device_id_type=MESH)

        def half_copy(t, chip_idx, to, sem_s, sem_r, k):
            r2 = dims[t][0] // 2
            win = refs[t].at[chip_idx, pl.ds(c * r2, r2), :]
            return pltpu.make_async_remote_copy(src_ref=win, dst_ref=win, send_sem=sem_s.at[t], recv_sem=sem_r.at[k],
                                                device_id=to, device_id_type=MESH)

        for t in range(n_t):
            for j, (cx, cy) in enumerate(chips):
                half_copy(t, p_me, (cx, cy, c), send_ici, recv_ici, 3 * t + j).start()
        for t in range(n_t):
            for j, (cx, cy) in enumerate(chips):
                half_copy(t, 2 * cx + cy, (cx, cy, c), send_ici, recv_ici, 3 * t + j).wait_recv()
                half_copy(t, 2 * cx + cy, sibling, send_d2d, recv_d2d, t).start()
        for t in range(n_t):
            waiter(t, send_d2d, recv_d2d).wait_recv()
            waiter(t, send_ici, recv_ici).wait_send()
            waiter(t, send_d2d, recv_d2d).wait_send()

    launch()
    return [r[...] for r in refs]


def _sequencer_allgather8(block, dev, name, collective_id):
    m_per, n = block.shape
    src = jax.new_ref(block, memory_space=pltpu.MemorySpace.HBM)
    out = jax.empty_ref(jax.ShapeDtypeStruct((N_DEV * m_per, n), block.dtype), memory_space=pltpu.MemorySpace.HBM)

    @pl.kernel(mesh=plsc.ScalarSubcoreMesh(axis_name="sequencer", num_cores=1), name=name,
               scratch_types=[pltpu.SemaphoreType.DMA((7,))] * 2,
               compiler_params=pltpu.CompilerParams(collective_id=collective_id))
    def launch(send_sems, recv_sems):
        x, y, c, sibling, chips = _place()
        me = (x, y, c)
        _handshake([sibling] + [(cx, cy, c) for cx, cy in chips])

        def rows(px, py, pc):
            return out.at[pl.ds((4 * px + 2 * py + pc) * m_per, m_per), :]

        def copy(k, blk, to, from_src=False):
            return pltpu.make_async_remote_copy(
                src_ref=src if from_src else rows(*blk), dst_ref=rows(*blk),
                send_sem=send_sems.at[k], recv_sem=recv_sems.at[k], device_id=to, device_id_type=MESH)

        first = [copy(0, me, sibling, True)] + [copy(1 + j, me, (*chip, c), True) for j, chip in enumerate(chips)]
        for cp in first:
            cp.start()
        passed = [copy(4 + j, (*chip, c), sibling) for j, chip in enumerate(chips)]
        for j, chip in enumerate(chips):
            copy(1 + j, (*chip, c), me).wait_recv()
            passed[j].start()
        copy(0, sibling, me).wait_recv()
        for j, chip in enumerate(chips):
            copy(4 + j, (*chip, 1 - c), me).wait_recv()
        for cp in first + passed:
            cp.wait_send()

    launch()
    return lax.dynamic_update_slice(out[...], block, (dev * m_per, 0))


def _others():
    x, y, c = lax.axis_index("x"), lax.axis_index("y"), lax.axis_index("c")
    flip = lambda v, f: 1 - v if f else v
    return [(flip(x, fx), flip(y, fy), flip(c, fc))
            for fx in (0, 1) for fy in (0, 1) for fc in (0, 1) if (fx, fy, fc) != (0, 0, 0)]


def _handshake(peers):
    barrier = pltpu.get_barrier_semaphore()
    for peer in peers:
        pl.semaphore_signal(barrier, inc=1, device_id=peer, device_id_type=MESH)
    pl.semaphore_wait(barrier, len(peers))


def _sequencer_scatter(parts, name, collective_id):
    n_t = len(parts)
    dims = [p.shape[1:] for p in parts]
    srcs = [jax.new_ref(p, memory_space=pltpu.MemorySpace.HBM) for p in parts]
    inboxes = [jax.empty_ref(jax.ShapeDtypeStruct((N_DEV - 1, r // 2, cols), BF16), memory_space=pltpu.MemorySpace.HBM)
               for r, cols in dims]

    @pl.kernel(mesh=plsc.ScalarSubcoreMesh(axis_name="sequencer", num_cores=1), name=name,
               scratch_types=[pltpu.SemaphoreType.DMA((n_t,))] * 2,
               compiler_params=pltpu.CompilerParams(collective_id=collective_id))
    def launch(send_sem, recv_sem):
        peers = _others()
        _handshake(peers)
        for t in range(n_t):
            h = dims[t][0] // 2
            for k, (qx, qy, qc) in enumerate(peers):
                pltpu.make_async_remote_copy(
                    src_ref=srcs[t].at[2 * qx + qy, pl.ds(qc * h, h), :], dst_ref=inboxes[t].at[k],
                    send_sem=send_sem.at[t], recv_sem=recv_sem.at[t], device_id=(qx, qy, qc), device_id_type=MESH).start()
        for t in range(n_t):
            win = inboxes[t]
            both = pltpu.make_async_remote_copy(src_ref=win, dst_ref=win, send_sem=send_sem.at[t],
                                                recv_sem=recv_sem.at[t], device_id=peers[0], device_id_type=MESH)
            both.wait_recv()
            both.wait_send()

    launch()
    return [b[...] for b in inboxes]


def _sum_pieces(part, inbox, place, name):
    _, r, cols = part.shape
    h = r // 2
    rows = _chunk_rows(h, cols)
    steps = h // rows

    def body(place_ref, own_ref, in_ref, o_ref):
        acc = own_ref[...].astype(F32)
        for k in range(N_DEV - 1):
            acc = acc + in_ref[k].astype(F32)
        o_ref[...] = acc

    return pl.pallas_call(
        body, name=name,
        grid_spec=pltpu.PrefetchScalarGridSpec(
            num_scalar_prefetch=1, grid=(steps,),
            in_specs=[pl.BlockSpec((None, rows, cols), lambda i, pr: (pr[0], pr[1] * steps + i, 0)),
                      pl.BlockSpec((N_DEV - 1, rows, cols), lambda i, pr: (0, i, 0))],
            out_specs=pl.BlockSpec((rows, cols), lambda i, pr: (pr[1] * steps + i, 0))),
        out_shape=jax.ShapeDtypeStruct((r, cols), F32),
    )(place, part, inbox)


def _sequencer_swap_halves(halves, name, collective_id):
    n_t = len(halves)
    refs = [jax.new_ref(a, memory_space=pltpu.MemorySpace.HBM) for a in halves]

    @pl.kernel(mesh=plsc.ScalarSubcoreMesh(axis_name="sequencer", num_cores=1), name=name,
               scratch_types=[pltpu.SemaphoreType.DMA((n_t,))] * 2,
               compiler_params=pltpu.CompilerParams(collective_id=collective_id))
    def launch(send_sem, recv_sem):
        x, y, c = lax.axis_index("x"), lax.axis_index("y"), lax.axis_index("c")
        sibling = (x, y, 1 - c)
        _handshake([sibling])
        copies = []
        for t in range(n_t):
            h = halves[t].shape[0] // 2
            win = refs[t].at[pl.ds(c * h, h), :]
            copies.append(pltpu.make_async_remote_copy(src_ref=win, dst_ref=win, send_sem=send_sem.at[t],
                                                       recv_sem=recv_sem.at[t], device_id=sibling, device_id_type=MESH))
            copies[-1].start()
        for cp in copies:
            cp.wait()

    launch()
    return [r[...] for r in refs]


def _cond_rows(c16, w, act, name):
    n_l, dm, wid = w.shape

    def body(c_ref, w_ref, o_ref, a_ref):
        cv = c_ref[...]
        if act:
            cv = cv * _sig(cv)
        a_ref[...] = cv
        o_ref[...] = _dot_f32(cv, w_ref[...])

    return pl.pallas_call(
        body, name=name, grid=(n_l,),
        in_specs=[_full((16, dm)), pl.BlockSpec((None, dm, wid), lambda l: (l, 0, 0))],
        out_specs=[pl.BlockSpec((None, 16, wid), lambda l: (l, 0, 0)), _full((16, dm))],
        out_shape=[jax.ShapeDtypeStruct((n_l, 16, wid), F32), jax.ShapeDtypeStruct((16, dm), F32)],
    )(c16, w)


def _outer_grad(ct, dm, name):
    n_l, kk, wid = dm.shape
    d_rows = ct.shape[0]

    def body(c_ref, d_ref, o_ref):
        o_ref[...] = _dot_f32(c_ref[...], d_ref[...])

    return pl.pallas_call(
        body, name=name, grid=(n_l,),
        in_specs=[_full((d_rows, kk)), pl.BlockSpec((None, kk, wid), lambda l: (l, 0, 0))],
        out_specs=pl.BlockSpec((None, d_rows, wid), lambda l: (l, 0, 0)),
        out_shape=jax.ShapeDtypeStruct((n_l, d_rows, wid), F32),
    )(ct, dm)


def _sum_devices(g, name):
    rows, n = g.shape

    def body(g_ref, o_ref):
        acc = g_ref[0:SUBLANES, :]
        for dev in range(1, N_DEV):
            acc = acc + g_ref[dev * SUBLANES:(dev + 1) * SUBLANES, :]
        o_ref[...] = acc

    return pl.pallas_call(body, name=name, out_shape=jax.ShapeDtypeStruct((SUBLANES, n), F32))(g)


def _adamw(w, g, m, v, name):
    shape = w.shape
    cols = shape[-1]
    rows = w.size // cols
    tr = rows
    for cand in range(SUBLANES, min(rows, 256) + 1, SUBLANES):
        if rows % cand == 0:
            tr = cand
    if rows * cols <= COMM_CHUNK_ELEMS:
        tr = rows
    c1 = 1.0 / (1.0 - ADAM_B1 ** ADAM_STEP)
    c2 = 1.0 / (1.0 - ADAM_B2 ** ADAM_STEP)

    def body(w_ref, g_ref, m_ref, v_ref, d_ref, mo_ref, vo_ref):
        gv = g_ref[...]
        m_new = ADAM_B1 * m_ref[...] + (1.0 - ADAM_B1) * gv
        v_new = ADAM_B2 * v_ref[...] + (1.0 - ADAM_B2) * (gv * gv)
        mo_ref[...] = m_new
        vo_ref[...] = v_new
        d_ref[...] = -ADAM_LR * ((m_new * c1) / (jnp.sqrt(v_new * c2) + ADAM_EPS) + ADAM_WD * w_ref[...])

    spec = pl.BlockSpec((tr, cols), lambda i: (i, 0))
    outs = pl.pallas_call(
        body, name=name, grid=(rows // tr,), in_specs=[spec] * 4, out_specs=[spec] * 3,
        out_shape=[jax.ShapeDtypeStruct((rows, cols), F32)] * 3,
    )(*[a.reshape(rows, cols) for a in (w, g, m, v)])
    return tuple(o.reshape(shape) for o in outs)


def _pad_cols(a, cols):
    return jnp.pad(a, [(0, 0)] * (a.ndim - 1) + [(0, cols - a.shape[-1])])


def _flat8(parts, width):
    v = jnp.concatenate([p.reshape(-1) for p in parts])
    return jnp.pad(v, (0, width - v.shape[0])).reshape(SUBLANES, width // SUBLANES)


KV_SHARD = 514
KV_SHARD_PAD = 640
BIG = ("a_w_in", "a_w_out", "kv_w", "b_w_q", "b_w_out", "up0", "up1", "down0", "down1")


def kernel(x, c, ada_w, ada_b, a_w_in, a_lb_logits, a_norm_g, a_w_out, kv_ada_w, kv_ada_b, kv_w, kv_b_f, k_norm_g, b_w_q, q_norm_g, b_w_out, ffn_w_up, ffn_conv_w, ffn_conv_b, ffn_w_down, loss_target, m_ada_w, m_ada_b, m_a_w_in, m_a_lb_logits, m_a_norm_g, m_a_w_out, m_kv_ada_w, m_kv_ada_b, m_kv_w, m_kv_b_f, m_k_norm_g, m_b_w_q, m_q_norm_g, m_b_w_out, m_ffn_w_up, m_ffn_conv_w, m_ffn_conv_b, m_ffn_w_down, v_ada_w, v_ada_b, v_a_w_in, v_a_lb_logits, v_a_norm_g, v_a_w_out, v_kv_ada_w, v_kv_ada_b, v_kv_w, v_kv_b_f, v_k_norm_g, v_b_w_q, v_q_norm_g, v_b_w_out, v_ffn_w_up, v_ffn_conv_w, v_ffn_conv_b, v_ffn_w_down):
    dm, ff = D_MODEL, D_FF
    ix, iy, ic = lax.axis_index("x"), lax.axis_index("y"), lax.axis_index("c")
    chip = 2 * ix + iy
    dev = 2 * chip + ic

    w1 = 10240
    g1 = _allgather8(_flat8([c, a_lb_logits, ffn_conv_w], w1), "gather_cond").reshape(N_DEV, w1)
    c_all = g1[:, :dm]
    per_chip = g1[0::2]
    lb_logits = per_chip[:, dm:dm + 512].reshape(N_CHIPS, 2, 256).transpose(1, 0, 2).reshape(2, dm)
    conv_w = per_chip[:, dm + 512:dm + 512 + 2 * CONV_W * FFN_COLS].reshape(N_CHIPS, 2, CONV_W, FFN_COLS)
    conv_w = conv_w.transpose(1, 2, 0, 3).reshape(2, CONV_W, 2, ff).transpose(0, 2, 1, 3)
    conv_b = ffn_conv_b.reshape(2, 2, 1, ff)
    lb = jax.nn.softmax(lb_logits, axis=0)[0:1]

    c16 = jnp.pad(c_all, ((0, 8), (0, 0)))
    mod_ada, c_act16 = _cond_rows(c16, ada_w, True, "mod_ada")
    mod_kv, _ = _cond_rows(c16, kv_ada_w[None], True, "mod_kv")
    mine = jnp.concatenate([mod_ada[0, :8], mod_ada[1, :8], mod_kv[0, :8]], axis=1)
    w2 = mine.shape[1]
    g2 = _allgather8(mine, "gather_mod").reshape(N_DEV, 8, w2)[0::2]
    my_rows = lax.dynamic_index_in_dim(g2, dev, axis=1, keepdims=False)
    mod0 = my_rows[:, 0:1536].reshape(6 * dm) + ada_b[0]
    mod1 = my_rows[:, 1536:3072].reshape(6 * dm) + ada_b[1]
    modk = my_rows[:, 3072:3584].reshape(2 * dm) + kv_ada_b
    mods = {"l0": [v.reshape(1, dm) for v in jnp.split(mod0, 6)],
            "l1": [v.reshape(1, dm) for v in jnp.split(mod1, 6)],
            "kv": [v.reshape(1, dm) for v in jnp.split(modk, 2)]}

    local = [(a_w_in, 0), (a_w_out, 0), (_pad_cols(kv_w, KV_SHARD_PAD)[None], 0), (b_w_q, 0), (b_w_out, 0),
             (ffn_w_up, 0), (ffn_w_up, 1), (ffn_w_down, 0), (ffn_w_down, 1)]
    chip_arr = chip.reshape(1).astype(jnp.int32)
    local = dict(zip(BIG, local))
    stages = {"mixer0": ("a_w_in",), "out0": ("a_w_out",), "ffn0": ("up0", "down0"),
              "layer1": ("kv_w", "b_w_q", "b_w_out", "up1", "down1")}
    arriving = {}

    def launch(stage, behind):
        shards = [local[n][0] for n in stages[stage]]
        if behind is not None:
            shards, _ = lax.optimization_barrier((shards, behind))
        own = [_cast_own_block(w, local[n][1], chip_arr, "cast_" + n) for n, w in zip(stages[stage], shards)]
        arriving[stage] = _sequencer_gather(own, "gather_" + stage, 1 + list(stages).index(stage))
        return own

    launch("out0", launch("mixer0", None))
    launch("ffn0", mod0)
    rowwise = lambda g: g.reshape(1, -1, dm)

    def weights_at(stage, token):
        if stage == "launch_layer1":
            launch("layer1", token)
            return {}, token
        got, token = lax.optimization_barrier((arriving[stage], token))
        g = dict(zip(stages[stage], got))
        if stage == "mixer0":
            return {"a_w_in": g["a_w_in"]}, token
        if stage == "out0":
            return {"a_w_out": rowwise(g["a_w_out"])}, token
        if stage == "ffn0":
            return {"up0": g["up0"], "down0": rowwise(g["down0"])}, token
        s0, s1, s2, s3 = (g["kv_w"][p] for p in range(N_CHIPS))
        second = dm - KV_SHARD
        w_k = jnp.concatenate([s0[:, :KV_SHARD], s1[:, :second]], axis=1)
        w_v = jnp.concatenate([s1[:, second:KV_SHARD], s2[:, :KV_SHARD], s3[:, :KV_SHARD - HEADS]], axis=1)
        w_f = _pad_cols(s3[:, KV_SHARD - HEADS:KV_SHARD], LANES)
        return {"kv_k": w_k[None], "kv_v": w_v[None], "kv_f": w_f[None], "b_w_q": g["b_w_q"],
                "b_w_out": rowwise(g["b_w_out"]), "up1": g["up1"], "down1": rowwise(g["down1"])}, token

    vecs = {"a_norm_g": jnp.tile(a_norm_g, (1, HEADS)), "k_norm_g": jnp.tile(k_norm_g[None], (1, HEADS)),
            "q_norm_g": jnp.tile(q_norm_g, (1, HEADS)), "kv_b_f": _pad_cols(kv_b_f[None], LANES),
            "conv_w0": conv_w[0], "conv_b0": conv_b[0], "conv_w1": conv_w[1], "conv_b1": conv_b[1]}

    sq, grad_x, big, small, marks = _local_step(x[0], loss_target[0], mods, lb, vecs, weights_at)

    gk, gv, gf = big["kv_k"][0], big["kv_v"][0], big["kv_f"][0][:, :HEADS]
    second = dm - KV_SHARD
    kv_blocks = [gk[:, :KV_SHARD], jnp.concatenate([gk[:, KV_SHARD:], gv[:, :KV_SHARD - second]], axis=1),
                 gv[:, KV_SHARD - second:2 * KV_SHARD - second], jnp.concatenate([gv[:, 2 * KV_SHARD - second:], gf], axis=1)]
    kv_grad = jnp.stack([_pad_cols(b, KV_SHARD_PAD) for b in kv_blocks])
    chipwise = lambda g: g.reshape(N_CHIPS, -1, dm)
    parts = dict(zip(BIG, [big["a_w_in"], chipwise(big["a_w_out"]), kv_grad, big["b_w_q"], chipwise(big["b_w_out"]),
                           big["up0"], big["up1"], chipwise(big["down0"]), chipwise(big["down1"])]))
    place = jnp.stack([chip, ic, dev]).astype(jnp.int32)

    served = []
    boxes = {}

    groups = (("up1", "down1"), ("b_w_out", "b_w_q", "kv_w"), ("up0", "down0"), ("a_w_out", "a_w_in"))

    def scatter_group(k):
        mine = [parts[n] for n in groups[k]]
        if served:
            mine, _ = lax.optimization_barrier((mine, served[-1]))
        boxes[k] = _sequencer_scatter(mine, "scatter_grads_%d" % k, 5 + k)
        served.append(boxes[k])

    def sum_group(k, token):
        inboxes, _ = lax.optimization_barrier((boxes[k], token))
        return [_sum_pieces(parts[n], box, place, "sum_" + n) for n, box in zip(groups[k], inboxes)]

    def swap_group(k, halves, behind):
        halves, _ = lax.optimization_barrier((halves, behind))
        return dict(zip(groups[k], _sequencer_swap_halves(halves, "swap_grads_%d" % k, 9 + k)))

    for k in range(3):
        scatter_group(k)
    halves = [sum_group(0, marks["attention_bwd"]), sum_group(1, marks["ffn0_bwd"]), sum_group(2, marks["mixer0_bwd"])]

    fold = lambda a: a.sum(axis=0)
    heads = lambda a: fold(a).reshape(HEADS, HEAD_DIM).sum(axis=0)
    conv_flat = lambda a: a.sum(axis=2).transpose(1, 0, 2)
    pieces = ([fold(a) for a in small["mod_l0"]] + [fold(a) for a in small["mod_l1"]] + [fold(a) for a in small["mod_kv"]]
              + [conv_flat(small["conv0"]), conv_flat(small["conv1"]), heads(small["a_norm_g"]), heads(small["k_norm_g"]),
                 heads(small["q_norm_g"]), fold(small["kv_b_f"]), fold(small["lb"]),
                 0.5 * jnp.sum(sq).reshape(1) / dm])
    w3 = 61440
    small_vec, _ = lax.optimization_barrier((_flat8(pieces, w3), served[2]))
    g3 = _sequencer_allgather8(small_vec, dev, "gather_small", 13)
    served.append(g3)
    scatter_group(3)
    rs = {}
    for k in range(3):
        rs.update(swap_group(k, halves[k], g3))
    tot = _sum_devices(g3, "sum_small").reshape(w3)
    n_mod = 14 * dm
    dmod_all = g3.reshape(N_DEV, w3)[:, :n_mod]
    o = n_mod
    conv_tot = [tot[o + l * 8 * ff: o + (l + 1) * 8 * ff].reshape(4, 2 * ff) for l in range(2)]
    o += 16 * ff
    g_a_norm, g_k_norm, g_q_norm = (tot[o + i * HEAD_DIM: o + (i + 1) * HEAD_DIM] for i in range(3))
    o += 3 * HEAD_DIM
    g_kv_b_f = tot[o:o + HEADS]
    dlb = tot[o + LANES:o + LANES + dm]
    loss = tot[o + LANES + dm]

    ct = _pad_cols(c_act16[:8].T, LANES)
    dmod_pad = jnp.pad(dmod_all, ((0, LANES - N_DEV), (0, 0)))
    cols_ada = jnp.stack([lax.dynamic_slice_in_dim(dmod_pad, l * 6 * dm + chip * 1536, 1536, axis=1) for l in range(2)])
    cols_kv = lax.dynamic_slice_in_dim(dmod_pad, 12 * dm + chip * 512, 512, axis=1)[None]
    g_ada_w = _outer_grad(ct, cols_ada, "grad_ada_w")
    g_kv_ada_w = _outer_grad(ct, cols_kv, "grad_kv_ada_w")[0]

    my_lb = lax.dynamic_slice_in_dim(lb[0], chip * 256, 256)
    l0 = lax.dynamic_slice_in_dim(dlb, chip * 256, 256) * my_lb * (1.0 - my_lb)
    grads = {
        "ada_w": g_ada_w, "ada_b": jnp.stack([tot[:6 * dm], tot[6 * dm:12 * dm]]),
        "a_lb_logits": jnp.stack([l0, -l0]), "a_norm_g": g_a_norm[None],
        "kv_ada_w": g_kv_ada_w, "kv_ada_b": tot[12 * dm:14 * dm],
        "kv_w": rs["kv_w"][:, :KV_SHARD], "kv_b_f": g_kv_b_f, "k_norm_g": g_k_norm,
        "b_w_q": rs["b_w_q"][None], "q_norm_g": g_q_norm[None], "b_w_out": rs["b_w_out"][None],
        "ffn_w_up": jnp.stack([rs["up0"], rs["up1"]]),
        "ffn_conv_w": jnp.stack([lax.dynamic_slice_in_dim(ct_l[:CONV_W], chip * FFN_COLS, FFN_COLS, axis=1) for ct_l in conv_tot]),
        "ffn_conv_b": jnp.stack([ct_l[CONV_W] for ct_l in conv_tot]),
        "ffn_w_down": jnp.stack([rs["down0"], rs["down1"]]),
    }
    weights = dict(ada_w=ada_w, ada_b=ada_b, a_w_in=a_w_in, a_lb_logits=a_lb_logits, a_norm_g=a_norm_g, a_w_out=a_w_out,
                   kv_ada_w=kv_ada_w, kv_ada_b=kv_ada_b, kv_w=kv_w, kv_b_f=kv_b_f, k_norm_g=k_norm_g, b_w_q=b_w_q,
                   q_norm_g=q_norm_g, b_w_out=b_w_out, ffn_w_up=ffn_w_up, ffn_conv_w=ffn_conv_w, ffn_conv_b=ffn_conv_b,
                   ffn_w_down=ffn_w_down)
    m_in = dict(ada_w=m_ada_w, ada_b=m_ada_b, a_w_in=m_a_w_in, a_lb_logits=m_a_lb_logits, a_norm_g=m_a_norm_g,
                a_w_out=m_a_w_out, kv_ada_w=m_kv_ada_w, kv_ada_b=m_kv_ada_b, kv_w=m_kv_w, kv_b_f=m_kv_b_f,
                k_norm_g=m_k_norm_g, b_w_q=m_b_w_q, q_norm_g=m_q_norm_g, b_w_out=m_b_w_out, ffn_w_up=m_ffn_w_up,
                ffn_conv_w=m_ffn_conv_w, ffn_conv_b=m_ffn_conv_b, ffn_w_down=m_ffn_w_down)
    v_in = dict(ada_w=v_ada_w, ada_b=v_ada_b, a_w_in=v_a_w_in, a_lb_logits=v_a_lb_logits, a_norm_g=v_a_norm_g,
                a_w_out=v_a_w_out, kv_ada_w=v_kv_ada_w, kv_ada_b=v_kv_ada_b, kv_w=v_kv_w, kv_b_f=v_kv_b_f,
                k_norm_g=v_k_norm_g, b_w_q=v_b_w_q, q_norm_g=v_q_norm_g, b_w_out=v_b_w_out, ffn_w_up=v_ffn_w_up,
                ffn_conv_w=v_ffn_conv_w, ffn_conv_b=v_ffn_conv_b, ffn_w_down=v_ffn_w_down)

    names = list(weights)
    step = lambda n: _adamw(weights[n], grads[n], m_in[n], v_in[n], "adamw_" + n)
    grads = {n: g.reshape(weights[n].shape) for n, g in grads.items()}
    upd = {n: step(n) for n in names if n not in groups[3]}
    last = sum_group(3, [u[0] for u in upd.values()])
    for n, g in swap_group(3, last, last).items():
        grads[n] = g[None]
        upd[n] = step(n)
    return (loss, grad_x[None], *[grads[n] for n in names], *[upd[n][0] for n in names],
            *[upd[n][1] for n in names], *[upd[n][2] for n in names])
```

```python
import jax
import jax.numpy as jnp
from jax import lax
from jax.experimental import pallas as pl
from jax.experimental.pallas import tpu as pltpu
from jax.experimental.pallas import tpu_sc as plsc

F32 = jnp.float32
BF16 = jnp.bfloat16

D_MODEL = 1024
HEADS = 8
HEAD_DIM = 128
A_CHUNK = 64
D_FF = 2816
CONV_W = 3
EPS = 1e-6
NEG_INF = -1e30
N_CHIPS = 4
N_DEV = 8

ADAM_LR = 0.001
ADAM_B1 = 0.9
ADAM_B2 = 0.999
ADAM_EPS = 1e-08
ADAM_WD = 0.01
ADAM_STEP = 10

SUBLANES = 8
BF16_ROWS = 16
LANES = 128
HALO = BF16_ROWS
ROW_TILE = 512
TOKEN_TILE_TN = 2048
FFN_COLS = 1408
FFN_ROWS = 256
HGRN_ROWS = 512
ATT_TILE = 512
ATT_SPLIT = 2
ATT_FWD_HEADS = 8
ATT_BWD_HEADS = 8
MESH = pl.DeviceIdType.MESH


def _sig(x):
    return jax.nn.sigmoid(x)


def _dot(a, b):
    return jnp.dot(a, b, preferred_element_type=F32)


def _dot_nt(a, b):
    return lax.dot_general(a, b, (((1,), (1,)), ((), ())), preferred_element_type=F32)


def _dot_tn(a, b):
    return lax.dot_general(a, b, (((0,), (0,)), ((), ())), preferred_element_type=F32)


def _split2(x):
    hi = x.astype(BF16)
    lo = (x - hi.astype(F32)).astype(BF16)
    return hi, lo


def _dot_f32(a, b):
    ah, al = _split2(a)
    bh, bl = _split2(b)
    return _dot(ah, bh) + _dot(ah, bl) + _dot(al, bh)


def _tri_dot(tri, x):
    hi = x.astype(BF16)
    r = x - hi.astype(F32)
    mid = r.astype(BF16)
    lo = (r - mid.astype(F32)).astype(BF16)
    return _dot(tri, hi) + _dot(tri, mid) + _dot(tri, lo)


def _tri(n, upper=False):
    r = lax.broadcasted_iota(jnp.int32, (n, n), 0)
    c = lax.broadcasted_iota(jnp.int32, (n, n), 1)
    keep = (c >= r) if upper else (c <= r)
    return jnp.where(keep, 1.0, 0.0).astype(BF16)


def _colsum8(v):
    rows, n = v.shape
    return v.reshape(rows // SUBLANES, SUBLANES, n).sum(axis=0)


def _full(shape):
    nd = len(shape)
    return pl.BlockSpec(shape, lambda *_: (0,) * nd)


def _tile(n, want):
    t = min(n, want)
    assert n % t == 0, (n, t)
    return t


def _mm_tn(a, d, p_n, name):
    m_rows, k = a.shape
    g_n, _, w_cols = d.shape
    per = p_n // g_n
    n = w_cols // per
    tm = _tile(m_rows, TOKEN_TILE_TN if k <= D_MODEL else ROW_TILE)
    steps = m_rows // tm

    def body(a_ref, d_ref, o_ref, acc):
        m = pl.program_id(1)

        @pl.when(m == 0)
        def _():
            acc[...] = jnp.zeros_like(acc)

        acc[...] += _dot_tn(a_ref[...], d_ref[...])

        @pl.when(m == steps - 1)
        def _():
            o_ref[...] = acc[...].astype(BF16)

    return pl.pallas_call(
        body, name=name, grid=(p_n, steps),
        in_specs=[pl.BlockSpec((tm, k), lambda p, m: (m, 0)),
                  pl.BlockSpec((None, tm, n), lambda p, m: (p // per, m, p % per))],
        out_specs=pl.BlockSpec((None, k, n), lambda p, m: (p, 0, 0)),
        out_shape=jax.ShapeDtypeStruct((p_n, k, n), BF16),
        scratch_shapes=[pltpu.VMEM((k, n), F32)],
    )(a, d)


def _premix_proj(x, shift, scale, w, name):
    s, dm = x.shape
    p_n, _, n = w.shape
    tm = _tile(s, ROW_TILE)

    def body(x_ref, sh_ref, sc_ref, w_ref, h_ref, o_ref):
        xv = x_ref[...]
        inv = lax.rsqrt(jnp.mean(xv * xv, axis=-1, keepdims=True) + EPS)
        h = (xv * inv * (1.0 + sc_ref[...]) + sh_ref[...]).astype(BF16)
        h_ref[...] = h
        for p in range(p_n):
            o_ref[:, p * n:(p + 1) * n] = _dot(h, w_ref[p])

    row = pl.BlockSpec((tm, dm), lambda i: (i, 0))
    vec = _full((1, dm))
    return pl.pallas_call(
        body, name=name, grid=(s // tm,), in_specs=[row, vec, vec, _full(w.shape)],
        out_specs=[row, pl.BlockSpec((tm, p_n * n), lambda i: (i, 0))],
        out_shape=[jax.ShapeDtypeStruct((s, dm), BF16), jax.ShapeDtypeStruct((s, p_n * n), F32)],
    )(x, shift, scale, w)


def _premix_bwd(x, terms, dres, name, branch=None):
    s, dm = x.shape
    tm = _tile(s, ROW_TILE)
    pairs = [pr for _, prs in terms for pr in prs]
    n_in = 2 + len(terms) + 2 * len(pairs) + (2 if branch else 0)

    def body(*refs):
        x_ref, dres_ref = refs[:2]
        sc_refs = refs[2:2 + len(terms)]
        mm_refs = refs[2 + len(terms):2 + len(terms) + 2 * len(pairs)]
        outs = refs[n_in:]

        @pl.when(pl.program_id(0) == 0)
        def _():
            for o in outs[1:1 + 2 * len(terms)]:
                o[...] = jnp.zeros_like(o)
            if branch:
                outs[-1][...] = jnp.zeros_like(outs[-1])

        xv = x_ref[...]
        inv = lax.rsqrt(jnp.mean(xv * xv, axis=-1, keepdims=True) + EPS)
        r = xv * inv
        dx = dres_ref[...]
        k = 0
        for t, (_, prs) in enumerate(terms):
            dh = None
            for d, w in prs:
                d_ref, w_ref = mm_refs[2 * k], mm_refs[2 * k + 1]
                k += 1
                p_n, _, n = w.shape
                per = p_n // d.shape[0]
                for p in range(p_n):
                    part = _dot_nt(d_ref[p // per, :, (p % per) * n:(p % per + 1) * n], w_ref[p])
                    dh = part if dh is None else dh + part
            dr = dh * (1.0 + sc_refs[t][...])
            dx = dx + inv * (dr - r * jnp.mean(dr * r, axis=-1, keepdims=True))
            outs[1 + 2 * t][...] += _colsum8(dh)
            outs[2 + 2 * t][...] += _colsum8(dh * r)
        outs[0][...] = dx
        if branch:
            y_ref, g_ref = refs[n_in - 2:n_in]
            outs[-2][0] = (dx * g_ref[...]).astype(BF16)
            outs[-1][...] += _colsum8(dx * y_ref[...])

    row = pl.BlockSpec((tm, dm), lambda i: (i, 0))
    vec, acc = _full((1, dm)), _full((SUBLANES, dm))
    ins, specs = [x, dres] + [sc for sc, _ in terms], [row, row] + [vec] * len(terms)
    for d, w in pairs:
        ins += [d, w]
        specs += [pl.BlockSpec((d.shape[0], tm, d.shape[2]), lambda i: (0, i, 0)), _full(w.shape)]
    out_shape = [jax.ShapeDtypeStruct((s, dm), F32)] + [jax.ShapeDtypeStruct((SUBLANES, dm), F32)] * (2 * len(terms))
    out_specs = [row] + [acc] * (2 * len(terms))
    if branch:
        ins += list(branch)
        specs += [row, vec]
        out_shape += [jax.ShapeDtypeStruct((1, s, dm), BF16), jax.ShapeDtypeStruct((SUBLANES, dm), F32)]
        out_specs += [pl.BlockSpec((1, tm, dm), lambda i: (0, i, 0)), acc]
    outs = pl.pallas_call(body, name=name, grid=(s // tm,), in_specs=specs, out_specs=out_specs,
                          out_shape=out_shape)(*ins)
    partials = [(outs[1 + 2 * t], outs[2 + 2 * t]) for t in range(len(terms))]
    return (outs[0], partials) + ((outs[-2], outs[-1]) if branch else ())


def _conv_taps(e, w, b):
    return w[2:3] * e + w[1:2] * pltpu.roll(e, 1, 0) + w[0:1] * pltpu.roll(e, 2, 0) + b


def _ffn_specs(s, tm, cb):
    hb = tm // HALO
    last = s // HALO - 1
    main = pl.BlockSpec((2, tm, cb), lambda j, i: (0, i, j))
    prev = pl.BlockSpec((2, HALO, cb), lambda j, i: (0, jnp.maximum(i * hb - 1, 0), j))
    nxt = pl.BlockSpec((2, HALO, cb), lambda j, i: (0, jnp.minimum((i + 1) * hb, last), j))
    wspec = pl.BlockSpec((2, CONV_W, cb), lambda j, i: (0, 0, j))
    bspec = pl.BlockSpec((2, 1, cb), lambda j, i: (0, 0, j))
    return main, prev, nxt, wspec, bspec


def _convglu_bwd(u, c, dffn, w_down, w, name):
    _, s, f = u.shape
    dm = dffn.shape[2]
    tm = _tile(s, 256)
    cb = _tile(f, FFN_COLS)
    steps = s // tm
    n_ext = tm + HALO
    main, _, nxt, wspec, _ = _ffn_specs(s, tm, cb)
    hb = tm // HALO
    last = s // HALO - 1
    d_main = pl.BlockSpec((None, tm, dm), lambda j, i: (0, i, 0))
    d_next = pl.BlockSpec((None, HALO, dm), lambda j, i: (0, jnp.minimum((i + 1) * hb, last), 0))
    wd_spec = pl.BlockSpec((None, cb, dm), lambda j, i: (0, j, 0))

    def body(u_ref, c_ref, cn_ref, d_ref, dn_ref, wd_ref, w_ref, du_ref, acc_ref):
        i = pl.program_id(1)
        notlast = jnp.where(i < steps - 1, 1.0, 0.0)

        @pl.when(i == 0)
        def _():
            acc_ref[...] = jnp.zeros_like(acc_ref)

        gate, val = (jnp.concatenate([c_ref[g].astype(F32), cn_ref[g].astype(F32)], axis=0) for g in range(2))
        wd = wd_ref[...]
        da = jnp.concatenate([_dot_nt(d_ref[...], wd).astype(BF16).astype(F32),
                              _dot_nt(dn_ref[...], wd).astype(BF16).astype(F32) * notlast], axis=0)
        sg = _sig(gate)
        d_val = da * gate * sg
        d_gate = da * val * (sg * (1.0 + gate * (1.0 - sg)))

        def finish(g, d):
            wv = w_ref[g]
            d1, d2 = pltpu.roll(d, n_ext - 1, 0), pltpu.roll(d, n_ext - 2, 0)
            du_ref[g] = (wv[2:3] * d + wv[1:2] * d1 + wv[0:1] * d2)[0:tm].astype(BF16)
            uv = u_ref[g].astype(F32)
            acc_ref[g, 2] += _colsum8(d[0:tm] * uv)
            acc_ref[g, 1] += _colsum8(d1[0:tm] * uv)
            acc_ref[g, 0] += _colsum8(d2[0:tm] * uv)
            acc_ref[g, 3] += _colsum8(d[0:tm])

        finish(0, d_gate)
        finish(1, d_val)

    return pl.pallas_call(
        body, name=name, grid=(f // cb, steps),
        in_specs=[main, main, nxt, d_main, d_next, wd_spec, wspec],
        out_specs=[main, pl.BlockSpec((2, 4, SUBLANES, cb), lambda j, i: (0, 0, 0, j))],
        out_shape=[jax.ShapeDtypeStruct((2, s, f), BF16), jax.ShapeDtypeStruct((2, 4, SUBLANES, f), F32)],
    )(u, c, c, dffn, dffn, w_down, w)


def _hgrn_gates(q_raw, f_raw, lb, tri):
    sf = _sig(f_raw)
    fg = lb + (1.0 - lb) * sf
    b = _tri_dot(tri, jnp.log(fg))
    return q_raw * _sig(q_raw), 1.0 - fg, b, fg, sf


def _hgrn_fwd(proj, lb, norm_g, name):
    s = proj.shape[0]
    tb = _tile(s, HGRN_ROWS)
    n_c = tb // A_CHUNK
    half = A_CHUNK // 2

    def body(q_ref, f_ref, v_ref, g_ref, lb_ref, ng_ref, o_ref, yp_ref, st_ref, state):
        @pl.when(pl.program_id(0) == 0)
        def _():
            state[...] = jnp.zeros_like(state)

        tri = _tri(A_CHUNK)
        causal = lax.broadcasted_iota(jnp.int32, (A_CHUNK, A_CHUNK), 1) <= lax.broadcasted_iota(
            jnp.int32, (A_CHUNK, A_CHUNK), 0)

        def chunk(ci, carry):
            rows = pl.ds(ci * A_CHUNK, A_CHUNK)
            heads = [slice(h * HEAD_DIM, (h + 1) * HEAD_DIM) for h in range(HEADS)]
            qs, k, b, _, _ = _hgrn_gates(q_ref[rows, :], f_ref[rows, :], lb_ref[...], tri)
            b_mid, b_last = b[half:half + 1], b[A_CHUNK - 1:A_CHUNK]
            q_i = (qs * jnp.exp(b - b_mid)).astype(BF16)
            k_i = (k * jnp.exp(b_mid - b)).astype(BF16)
            q_e = (qs * jnp.exp(b)).astype(BF16)
            k_s = (k * jnp.exp(b_last - b)).astype(BF16)
            decay = jnp.exp(b_last)
            vb = v_ref[rows, :].astype(BF16)
            scores = [jnp.where(causal, _dot_nt(q_i[:, cs], k_i[:, cs]), 0.0).astype(BF16) for cs in heads]
            st = [state[h] for h in range(HEADS)]
            outs = [_dot(scores[h], vb[:, cs]) + _dot_nt(q_e[:, cs], st[h].astype(BF16)) for h, cs in enumerate(heads)]
            for h, cs in enumerate(heads):
                st_ref[ci, h] = st[h]
                state[h] = st[h] * decay[:, cs] + _dot_tn(vb[:, cs], k_s[:, cs])
            o = jnp.concatenate(outs, axis=1)
            o_ref[rows, :] = o
            sq = o * o
            inv = jnp.concatenate([jnp.broadcast_to(lax.rsqrt(jnp.mean(sq[:, cs], axis=-1, keepdims=True) + EPS),
                                                    (A_CHUNK, HEAD_DIM)) for cs in heads], axis=1)
            g_raw = g_ref[rows, :]
            yp_ref[rows, :] = (o * inv * ng_ref[...] * (g_raw * _sig(g_raw))).astype(BF16)
            return carry

        for step in range(n_c):
            chunk(step, 0)

    col = lambda j: pl.BlockSpec((tb, D_MODEL), lambda i: (i, j))
    vec = _full((1, D_MODEL))
    return pl.pallas_call(
        body, name=name, grid=(s // tb,), in_specs=[col(0), col(1), col(2), col(3), vec, vec],
        out_specs=[col(0), col(0), pl.BlockSpec((n_c, HEADS, HEAD_DIM, HEAD_DIM), lambda i: (i, 0, 0, 0))],
        out_shape=[jax.ShapeDtypeStruct((s, D_MODEL), F32), jax.ShapeDtypeStruct((s, D_MODEL), BF16),
                   jax.ShapeDtypeStruct((s // A_CHUNK, HEADS, HEAD_DIM, HEAD_DIM), F32)],
        scratch_shapes=[pltpu.VMEM((HEADS, HEAD_DIM, HEAD_DIM), F32)],
    )(proj, proj, proj, proj, lb, norm_g)


def _hgrn_bwd(proj, lb, norm_g, o, states, dout, w_out, name):
    s = proj.shape[0]
    tb = _tile(s, HGRN_ROWS)
    n_c = tb // A_CHUNK
    n_b = s // tb
    half = A_CHUNK // 2

    def body(q_ref, f_ref, v_ref, g_ref, lb_ref, ng_ref, o_ref, st_ref, dout_ref, w_ref, dp_ref, dlb_ref, dng_ref,
             dstate, dyp_ref):
        @pl.when(pl.program_id(0) == 0)
        def _():
            dstate[...] = jnp.zeros_like(dstate)
            dlb_ref[...] = jnp.zeros_like(dlb_ref)
            dng_ref[...] = jnp.zeros_like(dng_ref)

        dyp_ref[...] = _dot_nt(dout_ref[0], w_ref[0])

        tri = _tri(A_CHUNK)
        tri_up = _tri(A_CHUNK, upper=True)
        row_id = lax.broadcasted_iota(jnp.int32, (A_CHUNK, D_MODEL), 0)
        causal = lax.broadcasted_iota(jnp.int32, (A_CHUNK, A_CHUNK), 1) <= lax.broadcasted_iota(
            jnp.int32, (A_CHUNK, A_CHUNK), 0)

        def chunk(cj, carry):
            ci = n_c - 1 - cj
            rows = pl.ds(ci * A_CHUNK, A_CHUNK)
            heads = [slice(h * HEAD_DIM, (h + 1) * HEAD_DIM) for h in range(HEADS)]
            cat = lambda parts: jnp.concatenate(parts, axis=1)
            per_head_mean = lambda a: cat([jnp.broadcast_to(jnp.mean(a[:, cs], axis=-1, keepdims=True),
                                                            (A_CHUNK, HEAD_DIM)) for cs in heads])
            q_raw, lbv = q_ref[rows, :], lb_ref[...]
            qs, k, b, fg, sf = _hgrn_gates(q_raw, f_ref[rows, :], lbv, tri)
            b_mid, b_last = b[half:half + 1], b[A_CHUNK - 1:A_CHUNK]
            e_qi, e_ki, e_q, e_ks = jnp.exp(b - b_mid), jnp.exp(b_mid - b), jnp.exp(b), jnp.exp(b_last - b)
            decay = jnp.exp(b_last)
            q_i, k_i, q_e, k_s = qs * e_qi, k * e_ki, qs * e_q, k * e_ks
            qib, kib, qeb, ksb = q_i.astype(BF16), k_i.astype(BF16), q_e.astype(BF16), k_s.astype(BF16)
            vb = v_ref[rows, :].astype(BF16)
            ov, g_raw, dy, ng = o_ref[rows, :], g_ref[rows, :], dyp_ref[rows, :], ng_ref[...]
            inv = lax.rsqrt(per_head_mean(ov * ov) + EPS)
            nrm = ov * inv
            sg = _sig(g_raw)
            gs = g_raw * sg
            dn = dy * ng * gs
            dng_ref[0:1, :] += jnp.sum(dy * nrm * gs, axis=0, keepdims=True)
            dg_raw = dy * nrm * ng * (sg * (1.0 + g_raw * (1.0 - sg)))
            do = (inv * (dn - nrm * per_head_mean(dn * nrm))).astype(BF16)
            st_prev = [st_ref[ci, h] for h in range(HEADS)]
            dst = [dstate[h] for h in range(HEADS)]
            dstb = [d.astype(BF16) for d in dst]
            scores = [jnp.where(causal, _dot_nt(qib[:, cs], kib[:, cs]), 0.0).astype(BF16) for cs in heads]
            d_scores = [jnp.where(causal, _dot_nt(do[:, cs], vb[:, cs]), 0.0).astype(BF16) for cs in heads]
            dv = cat([_dot_tn(scores[h], do[:, cs]) + _dot_nt(ksb[:, cs], dstb[h]) for h, cs in enumerate(heads)])
            dq_i = cat([_dot(d_scores[h], kib[:, cs]) for h, cs in enumerate(heads)])
            dk_i = cat([_dot_tn(d_scores[h], qib[:, cs]) for h, cs in enumerate(heads)])
            dq_e = cat([_dot(do[:, cs], st_prev[h].astype(BF16)) for h, cs in enumerate(heads)])
            dk_s = cat([_dot(vb[:, cs], dstb[h]) for h, cs in enumerate(heads)])
            d_decay = cat([jnp.sum(st_prev[h] * dst[h], axis=0, keepdims=True) for h in range(HEADS)])
            for h, cs in enumerate(heads):
                dstate[h] = dst[h] * decay[:, cs] + _dot_tn(do[:, cs], qeb[:, cs])
            dq = dq_i * e_qi + dq_e * e_q
            dk = dk_i * e_ki + dk_s * e_ks
            t_qi, t_ki, t_ks = dq_i * q_i, dk_i * k_i, dk_s * k_s
            db = t_qi - t_ki + dq_e * q_e - t_ks
            db_mid = jnp.sum(t_ki - t_qi, axis=0, keepdims=True)
            db_last = jnp.sum(t_ks, axis=0, keepdims=True) + d_decay * decay
            db = db + jnp.where(row_id == half, db_mid, 0.0) + jnp.where(row_id == A_CHUNK - 1, db_last, 0.0)
            dfg = _tri_dot(tri_up, db) / fg - dk
            dlb_ref[0:1, :] += jnp.sum(dfg * (1.0 - sf), axis=0, keepdims=True)
            sq = _sig(q_raw)
            dp_ref[0, rows, :] = (dq * (sq * (1.0 + q_raw * (1.0 - sq)))).astype(BF16)
            dp_ref[1, rows, :] = (dfg * (1.0 - lbv) * sf * (1.0 - sf)).astype(BF16)
            dp_ref[2, rows, :] = dv.astype(BF16)
            dp_ref[3, rows, :] = dg_raw.astype(BF16)
            return carry

        for step in range(n_c):
            chunk(step, 0)

    col = lambda j: pl.BlockSpec((tb, D_MODEL), lambda i: (n_b - 1 - i, j))
    vec = _full((1, D_MODEL))
    acc = _full((SUBLANES, D_MODEL))
    return pl.pallas_call(
        body, name=name, grid=(n_b,),
        in_specs=[col(0), col(1), col(2), col(3), vec, vec, col(0),
                  pl.BlockSpec((n_c, HEADS, HEAD_DIM, HEAD_DIM), lambda i: (n_b - 1 - i, 0, 0, 0)),
                  pl.BlockSpec((1, tb, D_MODEL), lambda i: (0, n_b - 1 - i, 0)), _full(w_out.shape)],
        out_specs=[pl.BlockSpec((4, tb, D_MODEL), lambda i: (0, n_b - 1 - i, 0)), acc, acc],
        out_shape=[jax.ShapeDtypeStruct((4, s, D_MODEL), BF16), jax.ShapeDtypeStruct((SUBLANES, D_MODEL), F32),
                   jax.ShapeDtypeStruct((SUBLANES, D_MODEL), F32)],
        scratch_shapes=[pltpu.VMEM((HEADS, HEAD_DIM, HEAD_DIM), F32), pltpu.VMEM((tb, D_MODEL), F32)],
    )(proj, proj, proj, proj, lb, norm_g, o, states, dout, w_out)


def _head_rms(raw_ref, g_ref, mult, y_ref):
    for h in range(HEADS):
        cs = slice(h * HEAD_DIM, (h + 1) * HEAD_DIM)
        xv = raw_ref[:, cs]
        inv = lax.rsqrt(jnp.mean(xv * xv, axis=-1, keepdims=True) + EPS)
        y_ref[:, cs] = (xv * inv * g_ref[:, cs] * mult).astype(BF16)


def _proj_headnorm(a, w, g, mult, name):
    s, k = a.shape
    p_n, _, n = w.shape
    tm = _tile(s, ROW_TILE)

    def body(a_ref, w_ref, g_ref, raw_ref, y_ref):
        av = a_ref[...]
        for p in range(p_n):
            raw_ref[:, p * n:(p + 1) * n] = _dot(av, w_ref[p])
        _head_rms(raw_ref, g_ref, mult, y_ref)

    row = lambda wid: pl.BlockSpec((tm, wid), lambda i: (i, 0))
    return pl.pallas_call(
        body, name=name, grid=(s // tm,), in_specs=[row(k), _full(w.shape), _full((1, D_MODEL))],
        out_specs=[row(p_n * n), row(D_MODEL)],
        out_shape=[jax.ShapeDtypeStruct((s, p_n * n), F32), jax.ShapeDtypeStruct((s, D_MODEL), BF16)],
    )(a, w, g)


def _kv_proj(hk, w_k, w_v, w_f, g, name):
    s, k = hk.shape
    tm = _tile(s, ROW_TILE)

    def body(h_ref, wk_ref, wv_ref, wf_ref, g_ref, kr_ref, k_ref, v_ref, f_ref):
        hv = h_ref[...]
        kr_ref[...] = _dot(hv, wk_ref[0])
        v_ref[...] = _dot(hv, wv_ref[0]).astype(BF16)
        f_ref[...] = _dot(hv, wf_ref[0])
        _head_rms(kr_ref, g_ref, 1.0, k_ref)

    row = lambda wid: pl.BlockSpec((tm, wid), lambda i: (i, 0))
    return pl.pallas_call(
        body, name=name, grid=(s // tm,),
        in_specs=[row(k), _full(w_k.shape), _full(w_v.shape), _full(w_f.shape), _full((1, D_MODEL))],
        out_specs=[row(D_MODEL), row(D_MODEL), row(D_MODEL), row(LANES)],
        out_shape=[jax.ShapeDtypeStruct((s, D_MODEL), F32), jax.ShapeDtypeStruct((s, D_MODEL), BF16),
                   jax.ShapeDtypeStruct((s, D_MODEL), BF16), jax.ShapeDtypeStruct((s, LANES), F32)],
    )(hk, w_k, w_v, w_f, g)


def _headnorm_bwd(x, g, mult, dy, name, col0=0, extra=None):
    s = x.shape[0]
    tm = _tile(s, ROW_TILE)
    groups = 2 if extra is not None else 1
    head_major = dy.ndim == 3

    def body(*refs):
        x_ref, g_ref, dy_ref = refs[:3]
        dx_ref, dg_ref = refs[-2:]

        @pl.when(pl.program_id(0) == 0)
        def _():
            dg_ref[...] = jnp.zeros_like(dg_ref)

        for h in range(HEADS):
            cs = slice(h * HEAD_DIM, (h + 1) * HEAD_DIM)
            xv, gv = x_ref[:, cs], g_ref[:, cs]
            dyv = dy_ref[h, :, 0:HEAD_DIM] if head_major else dy_ref[:, cs]
            inv = lax.rsqrt(jnp.mean(xv * xv, axis=-1, keepdims=True) + EPS)
            nrm = xv * inv
            dn = dyv * gv * mult
            dg_ref[:, cs] += _colsum8(dyv * nrm * mult)
            dx_ref[0, :, cs] = (inv * (dn - nrm * jnp.mean(dn * nrm, axis=-1, keepdims=True))).astype(BF16)
        if extra is not None:
            dx_ref[1] = refs[3][...]

    row = pl.BlockSpec((tm, D_MODEL), lambda i: (i, 0))
    dy_spec = pl.BlockSpec((HEADS, tm, dy.shape[-1]), lambda i: (0, i, 0)) if head_major else row
    ins = [x, g, dy] + ([extra] if extra is not None else [])
    specs = ([pl.BlockSpec((tm, D_MODEL), lambda i: (i, col0)), _full((1, D_MODEL)), dy_spec]
             + ([row] if extra is not None else []))
    return pl.pallas_call(
        body, name=name, grid=(s // tm,), in_specs=specs,
        out_specs=[pl.BlockSpec((groups, tm, D_MODEL), lambda i: (0, i, 0)), _full((SUBLANES, D_MODEL))],
        out_shape=[jax.ShapeDtypeStruct((groups, s, D_MODEL), BF16), jax.ShapeDtypeStruct((SUBLANES, D_MODEL), F32)],
    )(*ins)


def _log_sigmoid(z):
    return jnp.minimum(z, 0.0) - jnp.log(1.0 + jnp.exp(-jnp.abs(z)))


Q_CUM, Q_ONE, Q_LSE = 0, 3, 6
LOG2E = 1.4426950408889634


def _pieces(v):
    hi = v.astype(BF16).astype(F32)
    mid = (v - hi).astype(BF16).astype(F32)
    lo = ((v - hi) - mid).astype(BF16).astype(F32)
    return hi, mid, lo


def _side(lane, at, v):
    hi, mid, lo = _pieces(v)
    return jnp.where(lane == at, hi, jnp.where(lane == at + 1, mid, jnp.where(lane == at + 2, lo, 0.0)))


def _fcum_fwd(f, bias, name):
    s = f.shape[0]
    tm = _tile(s, ROW_TILE)

    def body(f_ref, b_ref, qa_ref, ka_ref, carry):
        @pl.when(pl.program_id(0) == 0)
        def _():
            carry[...] = jnp.zeros_like(carry)

        cum = _tri_dot(_tri(tm), _log_sigmoid(f_ref[...] + b_ref[...])) + carry[...]
        carry[...] = cum[tm - 1:tm]
        lane = lax.broadcasted_iota(jnp.int32, (tm, LANES), 1)
        ones_q = jnp.where((lane >= Q_ONE) & (lane < Q_LSE), 1.0, 0.0)
        ones_k = jnp.where((lane < Q_ONE) | ((lane >= Q_LSE) & (lane < Q_LSE + 3)), 1.0, 0.0)
        for h in range(HEADS):
            c2 = cum[:, h:h + 1] * LOG2E
            qa_ref[h] = (_side(lane, Q_CUM, c2) + ones_q).astype(BF16)
            ka_ref[h] = (_side(lane, Q_ONE, -c2) + ones_k).astype(BF16)

    side = pl.BlockSpec((HEADS, tm, LANES), lambda i: (0, i, 0))
    return pl.pallas_call(
        body, name=name, grid=(s // tm,),
        in_specs=[pl.BlockSpec((tm, LANES), lambda i: (i, 0)), _full((1, LANES))],
        out_specs=[side, side],
        out_shape=[jax.ShapeDtypeStruct((HEADS, s, LANES), BF16)] * 2,
        scratch_shapes=[pltpu.VMEM((1, LANES), F32)],
    )(f, bias)


def _fcum_bwd(f, bias, dka, dcq, name):
    s = f.shape[0]
    tm = _tile(s, ROW_TILE)
    n_b = s // tm

    def body(f_ref, b_ref, dka_ref, dcq_ref, dz_ref, db_ref, carry):
        @pl.when(pl.program_id(0) == 0)
        def _():
            carry[...] = jnp.zeros_like(carry)
            db_ref[...] = jnp.zeros_like(db_ref)

        lane = lax.broadcasted_iota(jnp.int32, (tm, LANES), 1)
        rows = jnp.concatenate([dcq_ref[h] for h in range(HEADS)] + [jnp.zeros((LANES - HEADS, tm), F32)], axis=0)
        dcum = rows.T
        for h in range(HEADS):
            dcum = dcum - jnp.where(lane == h, dka_ref[h, :, Q_ONE:Q_ONE + 1], 0.0)
        dlf = _tri_dot(_tri(tm, upper=True), dcum) + carry[...]
        carry[...] = dlf[0:1]
        dz = dlf * _sig(-(f_ref[...] + b_ref[...]))
        dz_ref[0] = dz.astype(BF16)
        db_ref[...] += _colsum8(dz)

    return pl.pallas_call(
        body, name=name, grid=(n_b,),
        in_specs=[pl.BlockSpec((tm, LANES), lambda i: (n_b - 1 - i, 0)), _full((1, LANES)),
                  pl.BlockSpec((HEADS, tm, LANES), lambda i: (0, n_b - 1 - i, 0)),
                  pl.BlockSpec((HEADS, 1, tm), lambda i: (0, 0, n_b - 1 - i))],
        out_specs=[pl.BlockSpec((1, tm, LANES), lambda i: (0, n_b - 1 - i, 0)), _full((SUBLANES, LANES))],
        out_shape=[jax.ShapeDtypeStruct((1, s, LANES), BF16), jax.ShapeDtypeStruct((SUBLANES, LANES), F32)],
        scratch_shapes=[pltpu.VMEM((1, LANES), F32)],
    )(f, bias, dka, dcq)


def _causal_pairs(n_t, key_major):
    if key_major:
        pairs = [(qi, ki) for ki in range(n_t) for qi in range(ki, n_t)]
    else:
        pairs = [(qi, ki) for qi in range(n_t) for ki in range(qi + 1)]
    return (jnp.array([p[0] for p in pairs], jnp.int32), jnp.array([p[1] for p in pairs], jnp.int32))


def _lane_const(t, lo, hi, value):
    lane = lax.broadcasted_iota(jnp.int32, (t, LANES), 1)
    return jnp.where((lane >= lo) & (lane < hi), value, 0.0).astype(BF16)


def _att_specs(t, nh):
    qmain = pl.BlockSpec((t, nh * HEAD_DIM), lambda h, p, qt, kt: (qt[p], h))
    kmain = pl.BlockSpec((t, nh * HEAD_DIM), lambda h, p, qt, kt: (kt[p], h))
    qside = pl.BlockSpec((nh, t, LANES), lambda h, p, qt, kt: (h, qt[p], 0))
    kside = pl.BlockSpec((nh, t, LANES), lambda h, p, qt, kt: (h, kt[p], 0))
    return qmain, kmain, qside, kside


def _fox_fwd(q, qa, k, ka, v, qo, name):
    s = q.shape[0]
    t = _tile(s, ATT_TILE)
    sub = t // ATT_SPLIT
    nh = ATT_FWD_HEADS
    qt, kt = _causal_pairs(s // t, key_major=False)

    def body(qt_ref, kt_ref, q_ref, qa_ref, k_ref, ka_ref, v_ref, og_ref, o_ref, y_ref, qab_ref, m_s, l_s, acc_s):
        pid = pl.program_id(1)
        qi, ki = qt_ref[pid], kt_ref[pid]

        @pl.when(ki == 0)
        def _():
            m_s[...] = jnp.full_like(m_s, NEG_INF)
            l_s[...] = jnp.zeros_like(l_s)
            acc_s[...] = jnp.zeros_like(acc_s)

        def step(diagonal):
            for hh in range(nh):
                hc = slice(hh * HEAD_DIM, (hh + 1) * HEAD_DIM)
                kc = jnp.concatenate([k_ref[:, hc], ka_ref[hh]], axis=1)
                vc = jnp.concatenate([v_ref[:, hc], _lane_const(t, 0, 1, 1.0)], axis=1)
                for r in range(ATT_SPLIT):
                    rows = slice(r * sub, (r + 1) * sub)
                    n_k = (r + 1) * sub if diagonal else t
                    sc = _dot_nt(jnp.concatenate([q_ref[rows, hc], qa_ref[hh, rows]], axis=1), kc[:n_k])
                    if diagonal:
                        sc = jnp.where(lax.broadcasted_iota(jnp.int32, (sub, n_k), 1)
                                       <= lax.broadcasted_iota(jnp.int32, (sub, n_k), 0) + r * sub, sc, NEG_INF)
                    m_old = m_s[hh, rows]
                    m_new = jnp.maximum(m_old, jnp.max(sc, axis=-1, keepdims=True))
                    alpha = jnp.exp2(m_old - m_new)
                    pv = _dot(jnp.exp2(sc - m_new[:, 0:1]).astype(BF16), vc[:n_k])
                    acc_s[hh, rows] = alpha * acc_s[hh, rows] + pv[:, :HEAD_DIM]
                    l_s[hh, rows] = alpha * l_s[hh, rows] + pv[:, HEAD_DIM:]
                    m_s[hh, rows] = m_new

        @pl.when(ki < qi)
        def _():
            step(False)

        @pl.when(ki == qi)
        def _():
            step(True)
            lane = lax.broadcasted_iota(jnp.int32, (t, LANES), 1)
            for hh in range(nh):
                hc = slice(hh * HEAD_DIM, (hh + 1) * HEAD_DIM)
                l = l_s[hh, :, 0:1]
                o = acc_s[hh] / l
                o_ref[:, hc] = o
                y_ref[:, hc] = (o * _sig(og_ref[:, hc])).astype(BF16)
                qab_ref[hh] = qa_ref[hh] + _side(lane, Q_LSE, -(m_s[hh, :, 0:1] + jnp.log2(l))).astype(BF16)

    qmain, kmain, qside, kside = _att_specs(t, nh)
    return pl.pallas_call(
        body, name=name,
        grid_spec=pltpu.PrefetchScalarGridSpec(
            num_scalar_prefetch=2, grid=(HEADS // nh, qt.shape[0]),
            in_specs=[qmain, qside, kmain, kside, kmain,
                      pl.BlockSpec((t, nh * HEAD_DIM), lambda h, p, qt, kt: (qt[p], HEADS // nh + h))],
            out_specs=[qmain, qmain, qside],
            scratch_shapes=[pltpu.VMEM((nh, t, LANES), F32), pltpu.VMEM((nh, t, LANES), F32),
                            pltpu.VMEM((nh, t, HEAD_DIM), F32)]),
        out_shape=[jax.ShapeDtypeStruct((s, D_MODEL), F32), jax.ShapeDtypeStruct((s, D_MODEL), BF16),
                   jax.ShapeDtypeStruct((HEADS, s, LANES), BF16)],
    )(qt, kt, q, qa, k, ka, v, qo)


def _fox_gate_bwd(o, qo, dout, w_out, name):
    s = o.shape[0]
    tm = _tile(s, ROW_TILE)

    def body(o_ref, og_ref, dout_ref, w_ref, do_ref, dg_ref, dl_ref):
        ov, dyv = o_ref[...], _dot_nt(dout_ref[0], w_ref[0])
        sg = _sig(og_ref[...])
        do = (dyv * sg).astype(BF16)
        do_ref[...] = do
        dg_ref[...] = (dyv * ov * sg * (1.0 - sg)).astype(BF16)
        prod = do.astype(F32) * ov
        lane = lax.broadcasted_iota(jnp.int32, (tm, LANES), 1)
        for h in range(HEADS):
            delta = jnp.sum(prod[:, h * HEAD_DIM:(h + 1) * HEAD_DIM], axis=-1, keepdims=True)
            dl_ref[h] = _side(lane, 0, delta).astype(BF16)

    row = pl.BlockSpec((tm, D_MODEL), lambda i: (i, 0))
    return pl.pallas_call(
        body, name=name, grid=(s // tm,),
        in_specs=[row, pl.BlockSpec((tm, D_MODEL), lambda i: (i, 1)),
                  pl.BlockSpec((1, tm, D_MODEL), lambda i: (0, i, 0)), _full(w_out.shape)],
        out_specs=[row, row, pl.BlockSpec((HEADS, tm, LANES), lambda i: (0, i, 0))],
        out_shape=[jax.ShapeDtypeStruct((s, D_MODEL), BF16), jax.ShapeDtypeStruct((s, D_MODEL), BF16),
                   jax.ShapeDtypeStruct((HEADS, s, LANES), BF16)],
    )(o, qo, dout, w_out)


def _fox_bwd(q, qab, k, ka, v, do, doa, k_raw, k_gain, name):
    s = q.shape[0]
    t = _tile(s, ATT_TILE)
    n_t = s // t
    sub = t // ATT_SPLIT
    nh = ATT_BWD_HEADS
    qt, kt = _causal_pairs(n_t, key_major=True)

    def body(qt_ref, kt_ref, q_ref, qab_ref, k_ref, ka_ref, v_ref, do_ref, doa_ref, kr_ref, kg_ref, dkr_ref, dkg_ref,
             dv_ref, dka_ref, dq_hbm, dcq_hbm, dk_s, dv_s, dq_ref, dcq_ref):
        group, pid = pl.program_id(0), pl.program_id(1)
        qi, ki = qt_ref[pid], kt_ref[pid]

        @pl.when(pid == 0)
        def _():
            dq_ref[...] = jnp.zeros_like(dq_ref)
            dcq_ref[...] = jnp.zeros_like(dcq_ref)
            dkg_ref[...] = jnp.zeros_like(dkg_ref)

        @pl.when(qi == ki)
        def _():
            dk_s[...] = jnp.zeros_like(dk_s)
            dv_s[...] = jnp.zeros_like(dv_s)

        def step(diagonal):
            for hh in range(nh):
                hc = slice(hh * HEAD_DIM, (hh + 1) * HEAD_DIM)
                kc = jnp.concatenate([k_ref[:, hc], ka_ref[hh]], axis=1)
                vc = jnp.concatenate([v_ref[:, hc], _lane_const(t, 0, 3, -1.0)], axis=1)
                for r in range(ATT_SPLIT):
                    cols = slice(r * sub, (r + 1) * sub)
                    n_k = (r + 1) * sub if diagonal else t
                    qc = jnp.concatenate([q_ref[cols, hc], qab_ref[hh, cols]], axis=1)
                    sc = _dot_nt(kc[:n_k], qc)
                    if diagonal:
                        sc = jnp.where(lax.broadcasted_iota(jnp.int32, (n_k, sub), 0)
                                       <= lax.broadcasted_iota(jnp.int32, (n_k, sub), 1) + r * sub, sc, NEG_INF)
                    p = jnp.exp2(sc)
                    dov = do_ref[cols, hc]
                    dp = _dot_nt(vc[:n_k], jnp.concatenate([dov, doa_ref[hh, cols]], axis=1))
                    ds = (p * dp).astype(BF16)
                    dv_s[hh, 0:n_k] += _dot(p.astype(BF16), dov)
                    dk_s[hh, 0:n_k] += _dot(ds, qc)
                    q_rows = pl.ds(pl.multiple_of(qi * t + r * sub, sub), sub)
                    dq_ref[hh, q_rows, :] += _dot_tn(ds, k_ref[0:n_k, hc])
                    dcq_ref[hh, qi * ATT_SPLIT + r] += jnp.sum(ds.astype(F32), axis=0, keepdims=True)

        @pl.when(qi > ki)
        def _():
            step(False)

        @pl.when(qi == ki)
        def _():
            step(True)

        @pl.when(qi == n_t - 1)
        def _():
            for hh in range(nh):
                hc = slice(hh * HEAD_DIM, (hh + 1) * HEAD_DIM)
                dka_ref[hh] = dk_s[hh, :, HEAD_DIM:]
                dv_ref[:, hc] = dv_s[hh].astype(BF16)
                dk = dk_s[hh, :, :HEAD_DIM] * (1.0 / LOG2E)
                xv = kr_ref[:, hc]
                inv = lax.rsqrt(jnp.mean(xv * xv, axis=-1, keepdims=True) + EPS)
                nrm = xv * inv
                dn = dk * kg_ref[:, hc]
                dkg_ref[:, hc] += _colsum8(dk * nrm)
                dkr_ref[:, hc] = (inv * (dn - nrm * jnp.mean(dn * nrm, axis=-1, keepdims=True))).astype(BF16)

        @pl.when(pid == qt.shape[0] - 1)
        def _():
            pltpu.sync_copy(dq_ref, dq_hbm.at[pl.ds(group * nh, nh)])
            pltpu.sync_copy(dcq_ref, dcq_hbm.at[pl.ds(group * nh, nh)])

    qmain, kmain, qside, kside = _att_specs(t, nh)
    kmain3 = pl.BlockSpec((None, t, nh * HEAD_DIM), lambda h, p, qt, kt: (0, kt[p], h))
    in_hbm = pl.BlockSpec(memory_space=pltpu.HBM)
    return pl.pallas_call(
        body, name=name,
        grid_spec=pltpu.PrefetchScalarGridSpec(
            num_scalar_prefetch=2, grid=(HEADS // nh, qt.shape[0]),
            in_specs=[qmain, qside, kmain, kside, kmain, qmain, qside, kmain,
                      pl.BlockSpec((1, nh * HEAD_DIM), lambda h, p, qt, kt: (0, h))],
            out_specs=[kmain3, pl.BlockSpec((SUBLANES, nh * HEAD_DIM), lambda h, p, qt, kt: (0, h)), kmain3, kside,
                       in_hbm, in_hbm],
            scratch_shapes=[pltpu.VMEM((nh, t, 2 * HEAD_DIM), F32), pltpu.VMEM((nh, t, HEAD_DIM), F32),
                            pltpu.VMEM((nh, s, HEAD_DIM), F32), pltpu.VMEM((nh, s // sub, 1, sub), F32)]),
        out_shape=[jax.ShapeDtypeStruct((1, s, D_MODEL), BF16), jax.ShapeDtypeStruct((SUBLANES, D_MODEL), F32),
                   jax.ShapeDtypeStruct((1, s, D_MODEL), BF16),
                   jax.ShapeDtypeStruct((HEADS, s, LANES), F32), jax.ShapeDtypeStruct((HEADS, s, HEAD_DIM), F32),
                   jax.ShapeDtypeStruct((HEADS, s // sub, 1, sub), F32)],
    )(qt, kt, q, qab, k, ka, v, do, doa, k_raw, k_gain)


def _mm_residual_premix(a, w, x, gate, mods, name):
    s, k = a.shape
    dm = x.shape[1]
    tm = _tile(s, ROW_TILE)

    def body(*refs):
        a_ref, w_ref, x_ref, g_ref = refs[:4]
        mod_refs = refs[4:4 + 2 * len(mods)]
        y_ref, xn_ref = refs[4 + 2 * len(mods):6 + 2 * len(mods)]
        h_refs = refs[6 + 2 * len(mods):]
        y = _dot(a_ref[...], w_ref[0])
        y_ref[...] = y
        xv = x_ref[...] + g_ref[...] * y
        xn_ref[...] = xv
        nrm = xv * lax.rsqrt(jnp.mean(xv * xv, axis=-1, keepdims=True) + EPS)
        for t, h_ref in enumerate(h_refs):
            h_ref[...] = (nrm * (1.0 + mod_refs[2 * t + 1][...]) + mod_refs[2 * t][...]).astype(BF16)

    row = pl.BlockSpec((tm, dm), lambda i: (i, 0))
    vec = _full((1, dm))
    outs = pl.pallas_call(
        body, name=name, grid=(s // tm,),
        in_specs=[pl.BlockSpec((tm, k), lambda i: (i, 0)), _full(w.shape), row, vec] + [vec] * (2 * len(mods)),
        out_specs=[row] * (2 + len(mods)),
        out_shape=[jax.ShapeDtypeStruct((s, dm), F32)] * 2 + [jax.ShapeDtypeStruct((s, dm), BF16)] * len(mods),
    )(a, w, x, gate, *[v for m in mods for v in m])
    return outs[0], outs[1], list(outs[2:])


def _mm_loss_head(a, w, x, gate, target, name):
    s, k = a.shape
    dm = x.shape[1]
    tm = _tile(s, ROW_TILE)

    def body(a_ref, w_ref, x_ref, g_ref, t_ref, sq_ref, do_ref, dy_ref, dg_ref):
        @pl.when(pl.program_id(0) == 0)
        def _():
            sq_ref[...] = jnp.zeros_like(sq_ref)
            dg_ref[...] = jnp.zeros_like(dg_ref)

        y, gv = _dot(a_ref[...], w_ref[0]), g_ref[...]
        err = x_ref[...] + gv * y - t_ref[...]
        sq_ref[...] += _colsum8(err * err)
        dout = err * (1.0 / dm)
        do_ref[...] = dout
        dy_ref[0] = (dout * gv).astype(BF16)
        dg_ref[...] += _colsum8(dout * y)

    row = pl.BlockSpec((tm, dm), lambda i: (i, 0))
    acc = _full((SUBLANES, dm))
    return pl.pallas_call(
        body, name=name, grid=(s // tm,),
        in_specs=[pl.BlockSpec((tm, k), lambda i: (i, 0)), _full(w.shape), row, _full((1, dm)), row],
        out_specs=[acc, row, pl.BlockSpec((1, tm, dm), lambda i: (0, i, 0)), acc],
        out_shape=[jax.ShapeDtypeStruct((SUBLANES, dm), F32), jax.ShapeDtypeStruct((s, dm), F32),
                   jax.ShapeDtypeStruct((1, s, dm), BF16), jax.ShapeDtypeStruct((SUBLANES, dm), F32)],
    )(a, w, x, gate, target)


def _ffn_inner(h, w_up, conv_w, conv_b, tag):
    s, dm = h.shape
    half = w_up.shape[2]
    f = 2 * half
    tm = _tile(s, FFN_ROWS)

    def body(h_ref, w_ref, cw_ref, cb_ref, u_ref, c_ref, a_ref, carry):
        @pl.when(pl.program_id(0) == 0)
        def _():
            carry[...] = jnp.zeros_like(carry)

        hv = h_ref[...]
        for j in range(2):
            cols = slice(j * half, (j + 1) * half)
            conv = []
            for g in range(2):
                ub = _dot(hv, w_ref[2 * g + j]).astype(BF16)
                u_ref[g, :, cols] = ub
                uf = ub.astype(F32)
                e = jnp.concatenate([carry[g, j], uf], axis=0)
                carry[g, j] = uf[tm - SUBLANES:tm]
                conv.append(_conv_taps(e, cw_ref[g][:, cols], cb_ref[g][:, cols])[SUBLANES:])
                c_ref[g, :, cols] = conv[g].astype(BF16)
            a_ref[:, cols] = (conv[0] * _sig(conv[0]) * conv[1]).astype(BF16)

    pair = pl.BlockSpec((2, tm, f), lambda i: (0, i, 0))
    return pl.pallas_call(
        body, name=tag + "_up_convglu", grid=(s // tm,),
        in_specs=[pl.BlockSpec((tm, dm), lambda i: (i, 0)), _full(w_up.shape), _full(conv_w.shape), _full(conv_b.shape)],
        out_specs=[pair, pair, pl.BlockSpec((tm, f), lambda i: (i, 0))],
        out_shape=[jax.ShapeDtypeStruct((2, s, f), BF16)] * 2 + [jax.ShapeDtypeStruct((s, f), BF16)],
        scratch_shapes=[pltpu.VMEM((2, 2, SUBLANES, half), F32)],
    )(h, w_up, conv_w, conv_b)


def _weight_grad_first(a, d, p_n, name):
    return lax.optimization_barrier((_mm_tn(a, d, p_n, name), d))


def _ffn_backward(dx_out, dffn, x_mid, scale, saved, w_up, conv_w, conv_b, w_down, mixer, tag):
    h, u, c, a = saved
    dw_down, dffn = _weight_grad_first(a, dffn, 1, tag + "_down_dw")
    du, dconv = _convglu_bwd(u, c, dffn, w_down, conv_w, tag + "_convglu_bwd")
    dw_up, du = _weight_grad_first(h, du, N_CHIPS, tag + "_up_dw")
    dx_mid, [(dshift, dscale)], dy, dgate_mixer = _premix_bwd(x_mid, [(scale, [(du, w_up)])], dx_out,
                                                              tag + "_premix_bwd", branch=mixer)
    return dx_mid, dy, dgate_mixer, dw_up, dw_down, dict(shift=dshift, scale=dscale, conv=dconv)


def _local_step(x, target, mods, lb, vecs, weights_at):
    m0, m1, mk = mods["l0"], mods["l1"], mods["kv"]
    wts, x = weights_at("mixer0", x)
    h0, proj = _premix_proj(x, m0[0], m0[1], wts["a_w_in"], "l0_premix_in")
    _, proj = weights_at("launch_layer1", proj)
    o_a, yp, states = _hgrn_fwd(proj, lb, vecs["a_norm_g"], "l0_hgrn")
    more, yp = weights_at("out0", yp)
    wts.update(more)
    y0, x1, [hf0] = _mm_residual_premix(yp, wts["a_w_out"], x, m0[2], [(m0[3], m0[4])], "l0_out")
    more, hf0 = weights_at("ffn0", hf0)
    wts.update(more)
    u0, c0, a0 = _ffn_inner(hf0, wts["up0"], vecs["conv_w0"], vecs["conv_b0"], "l0_ffn")
    saved0 = (hf0, u0, c0, a0)
    ffn0, x2, [hk, h1] = _mm_residual_premix(a0, wts["down0"], x1, m0[5], [(mk[0], mk[1]), (m1[0], m1[1])],
                                             "l0_ffn_down")
    more, hk = weights_at("layer1", hk)
    wts.update(more)
    k_raw, k_sh, v_sh, f_raw = _kv_proj(hk, wts["kv_k"], wts["kv_v"], wts["kv_f"], vecs["k_norm_g"], "kv_proj")
    qa, ka = _fcum_fwd(f_raw, vecs["kv_b_f"], "kv_fcum")
    q_scale = HEAD_DIM ** -0.5
    qo, q = _proj_headnorm(h1, wts["b_w_q"], vecs["q_norm_g"], q_scale * LOG2E, "l1_q")
    o_b, og, qab = _fox_fwd(q, qa, k_sh, ka, v_sh, qo, "l1_fox")
    y1, x3, [hf1] = _mm_residual_premix(og, wts["b_w_out"], x2, m1[2], [(m1[3], m1[4])], "l1_out")
    u1, c1, a1 = _ffn_inner(hf1, wts["up1"], vecs["conv_w1"], vecs["conv_b1"], "l1_ffn")
    saved1 = (hf1, u1, c1, a1)
    sq, dx4, dffn1, dg2_1 = _mm_loss_head(a1, wts["down1"], x3, m1[5], target, "l1_ffn_down")

    big, small = {}, {}
    dx3, dy1, dg1_1, big["up1"], big["down1"], s_ffn1 = _ffn_backward(
        dx4, dffn1, x3, m1[4], saved1, wts["up1"], vecs["conv_w1"], vecs["conv_b1"], wts["down1"], (y1, m1[2]), "l1_ffn")
    big["b_w_out"], dy1 = _weight_grad_first(og, dy1, 1, "l1_out_dw")
    do_b, dgate_b, doa = _fox_gate_bwd(o_b, qo, dy1, wts["b_w_out"], "l1_out_dx_gate_bwd")
    dk_raw, dkg, dv, dka, dq, dcq = _fox_bwd(q, qab, k_sh, ka, v_sh, do_b, doa, k_raw, vecs["k_norm_g"], "l1_fox_bwd")
    dqo, dqg = _headnorm_bwd(qo, vecs["q_norm_g"], q_scale, dq, "l1_qnorm_bwd", extra=dgate_b)
    big["b_w_q"], dqo = _weight_grad_first(h1, dqo, N_CHIPS, "l1_q_dw")
    dz, dbf = _fcum_bwd(f_raw, vecs["kv_b_f"], dka, dcq.reshape(HEADS, 1, -1), "kv_fcum_bwd")
    big["kv_k"], dk_raw = _weight_grad_first(hk, dk_raw, 1, "kv_k_dw")
    big["kv_v"], dv = _weight_grad_first(hk, dv, 1, "kv_v_dw")
    big["kv_f"], dz = _weight_grad_first(hk, dz, 1, "kv_f_dw")
    kv_pairs = [(dk_raw, wts["kv_k"]), (dv, wts["kv_v"]), (dz, wts["kv_f"])]
    dx2, [(dsh1_1, dsc1_1), (dshk, dsck)], dffn0, dg2_0 = _premix_bwd(
        x2, [(m1[1], [(dqo, wts["b_w_q"])]), (mk[1], kv_pairs)], dx3, "l1_kv_premix_bwd", branch=(ffn0, m0[5]))
    dx1, dy0, dg1_0, big["up0"], big["down0"], s_ffn0 = _ffn_backward(
        dx2, dffn0, x1, m0[4], saved0, wts["up0"], vecs["conv_w0"], vecs["conv_b0"], wts["down0"], (y0, m0[2]), "l0_ffn")
    big["a_w_out"], dy0 = _weight_grad_first(yp, dy0, 1, "l0_out_dw")
    dproj, dlb, dng = _hgrn_bwd(proj, lb, vecs["a_norm_g"], o_a, states, dy0, wts["a_w_out"], "l0_out_dx_hgrn_bwd")
    grad_x, [(dsh1_0, dsc1_0)] = _premix_bwd(x, [(m0[1], [(dproj, wts["a_w_in"])])], dx1, "l0_premix_bwd")
    dproj, _ = lax.optimization_barrier((dproj, (dsh1_0, dsc1_0)))
    big["a_w_in"] = _mm_tn(h0, dproj, N_CHIPS, "l0_in_dw")

    small["mod_l0"] = [dsh1_0, dsc1_0, dg1_0, s_ffn0["shift"], s_ffn0["scale"], dg2_0]
    small["mod_l1"] = [dsh1_1, dsc1_1, dg1_1, s_ffn1["shift"], s_ffn1["scale"], dg2_1]
    small["mod_kv"] = [dshk, dsck]
    small["conv0"], small["conv1"] = s_ffn0["conv"], s_ffn1["conv"]
    small["a_norm_g"], small["k_norm_g"], small["q_norm_g"] = dng, dkg, dqg
    small["kv_b_f"], small["lb"] = dbf, dlb
    marks = {"attention_bwd": dv, "ffn0_bwd": dx1, "mixer0_bwd": grad_x}
    return sq, grad_x, big, small, marks


COMM_CHUNK_ELEMS = 256 * 1024


def _place():
    x, y, c = lax.axis_index("x"), lax.axis_index("y"), lax.axis_index("c")
    chips = [(1 - x, y), (x, 1 - y), (1 - x, 1 - y)]
    return x, y, c, (x, y, 1 - c), chips


def _chunk_rows(rows, cols):
    best = BF16_ROWS
    for r in range(BF16_ROWS, rows + 1, BF16_ROWS):
        if rows % r == 0 and r * cols <= COMM_CHUNK_ELEMS:
            best = r
    assert rows % best == 0, (rows, cols)
    return best


def _allgather8(block, name):
    m_per, n = block.shape

    def body(x_ref, out_ref, send_sems, recv_sems, local_sem):
        x, y, c, sibling, chips = _place()
        me = (x, y, c)

        def rows(px, py, pc):
            return out_ref.at[pl.ds((4 * px + 2 * py + pc) * m_per, m_per), :]

        def copy(k, blk, to, src=None):
            return pltpu.make_async_remote_copy(
                src_ref=rows(*blk) if src is None else src, dst_ref=rows(*blk),
                send_sem=send_sems.at[k], recv_sem=recv_sems.at[k], device_id=to, device_id_type=MESH)

        mine = pltpu.make_async_copy(x_ref, rows(*me), local_sem)
        mine.start()
        first = [copy(0, me, sibling, src=x_ref)]
        first += [copy(1 + j, me, (*chip, c), src=x_ref) for j, chip in enumerate(chips)]
        for cp in first:
            cp.start()
        passed = [copy(4 + j, (*chip, c), sibling) for j, chip in enumerate(chips)]
        for j, chip in enumerate(chips):
            copy(1 + j, (*chip, c), me).wait_recv()
            passed[j].start()
        copy(0, sibling, me).wait_recv()
        for j, chip in enumerate(chips):
            copy(4 + j, (*chip, 1 - c), me).wait_recv()
        for cp in first + passed:
            cp.wait_send()
        mine.wait()

    return pl.pallas_call(
        body, name=name, out_shape=jax.ShapeDtypeStruct((N_DEV * m_per, n), block.dtype),
        in_specs=[pl.BlockSpec(memory_space=pltpu.VMEM)], out_specs=pl.BlockSpec(memory_space=pltpu.VMEM),
        scratch_shapes=[pltpu.SemaphoreType.DMA((7,)), pltpu.SemaphoreType.DMA((7,)), pltpu.SemaphoreType.DMA],
    )(block)


def _cast_own_block(shards, layer, chip, name):
    _, r, cols = shards.shape
    rows = _chunk_rows(r, cols)

    def body(chip_ref, w_ref, o_ref):
        o_ref[...] = w_ref[...].astype(BF16)

    return pl.pallas_call(
        body, name=name,
        grid_spec=pltpu.PrefetchScalarGridSpec(
            num_scalar_prefetch=1, grid=(r // rows,),
            in_specs=[pl.BlockSpec((None, rows, cols), lambda i, chip_ref: (layer, i, 0))],
            out_specs=pl.BlockSpec((None, rows, cols), lambda i, chip_ref: (chip_ref[0], i, 0))),
        out_shape=jax.ShapeDtypeStruct((N_CHIPS, r, cols), BF16),
    )(chip, shards)


def _sequencer_gather(bufs, name, collective_id):
    n_t = len(bufs)
    dims = [b.shape[1:] for b in bufs]
    refs = [jax.new_ref(b, memory_space=pltpu.MemorySpace.HBM) for b in bufs]

    @pl.kernel(mesh=plsc.ScalarSubcoreMesh(axis_name="sequencer", num_cores=1), name=name,
               scratch_types=[pltpu.SemaphoreType.DMA((n_t,)), pltpu.SemaphoreType.DMA((3 * n_t,)),
                              pltpu.SemaphoreType.DMA((n_t,)), pltpu.SemaphoreType.DMA((n_t,))],
               compiler_params=pltpu.CompilerParams(collective_id=collective_id))
    def launch(send_ici, recv_ici, send_d2d, recv_d2d):
        x, y, c, sibling, chips = _place()
        p_me = 2 * x + y
        peers = [sibling] + [(cx, cy, c) for cx, cy in chips]
        barrier = pltpu.get_barrier_semaphore()
        for peer in peers:
            pl.semaphore_signal(barrier, inc=1, device_id=peer, device_id_type=MESH)
        pl.semaphore_wait(barrier, len(peers))

        def waiter(t, sem_s, sem_r):
            win = refs[t].at[pl.ds(0, 3), pl.ds(0, dims[t][0] // 2), :]
            return pltpu.make_async_remote_copy(src_ref=win, dst_ref=win, send_sem=sem_s.at[t], recv_sem=sem_r.at[t],
                                                device_id=sibling, device_id_type=MESH)

        def half_copy(t, chip_idx, to, sem_s, sem_r, k):
            r2 = dims[t][0] // 2
            win = refs[t].at[chip_idx, pl.ds(c * r2, r2), :]
            return pltpu.make_async_remote_copy(src_ref=win, dst_ref=win, send_sem=sem_s.at[t], recv_sem=sem_r.at[k],
                                                device_id=to, device_id_type=MESH)

        for t in range(n_t):
            for j, (cx, cy) in enumerate(chips):
                half_copy(t, p_me, (cx, cy, c), send_ici, recv_ici, 3 * t + j).start()
        for t in range(n_t):
            for j, (cx, cy) in enumerate(chips):
                half_copy(t, 2 * cx + cy, (cx, cy, c), send_ici, recv_ici, 3 * t + j).wait_recv()
                half_copy(t, 2 * cx + cy, sibling, send_d2d, recv_d2d, t).start()
        for t in range(n_t):
            waiter(t, send_d2d, recv_d2d).wait_recv()
            waiter(t, send_ici, recv_ici).wait_send()
            waiter(t, send_d2d, recv_d2d).wait_send()

    launch()
    return [r[...] for r in refs]


def _sequencer_allgather8(block, dev, name, collective_id):
    m_per, n = block.shape
    src = jax.new_ref(block, memory_space=pltpu.MemorySpace.HBM)
    out = jax.empty_ref(jax.ShapeDtypeStruct((N_DEV * m_per, n), block.dtype), memory_space=pltpu.MemorySpace.HBM)

    @pl.kernel(mesh=plsc.ScalarSubcoreMesh(axis_name="sequencer", num_cores=1), name=name,
               scratch_types=[pltpu.SemaphoreType.DMA((7,))] * 2,
               compiler_params=pltpu.CompilerParams(collective_id=collective_id))
    def launch(send_sems, recv_sems):
        x, y, c, sibling, chips = _place()
        me = (x, y, c)
        _handshake([sibling] + [(cx, cy, c) for cx, cy in chips])

        def rows(px, py, pc):
            return out.at[pl.ds((4 * px + 2 * py + pc) * m_per, m_per), :]

        def copy(k, blk, to, from_src=False):
            return pltpu.make_async_remote_copy(
                src_ref=src if from_src else rows(*blk), dst_ref=rows(*blk),
                send_sem=send_sems.at[k], recv_sem=recv_sems.at[k], device_id=to, device_id_type=MESH)

        first = [copy(0, me, sibling, True)] + [copy(1 + j, me, (*chip, c), True) for j, chip in enumerate(chips)]
        for cp in first:
            cp.start()
        passed = [copy(4 + j, (*chip, c), sibling) for j, chip in enumerate(chips)]
        for j, chip in enumerate(chips):
            copy(1 + j, (*chip, c), me).wait_recv()
            passed[j].start()
        copy(0, sibling, me).wait_recv()
        for j, chip in enumerate(chips):
            copy(4 + j, (*chip, 1 - c), me).wait_recv()
        for cp in first + passed:
            cp.wait_send()

    launch()
    return lax.dynamic_update_slice(out[...], block, (dev * m_per, 0))


def _others():
    x, y, c = lax.axis_index("x"), lax.axis_index("y"), lax.axis_index("c")
    flip = lambda v, f: 1 - v if f else v
    return [(flip(x, fx), flip(y, fy), flip(c, fc))
            for fx in (0, 1) for fy in (0, 1) for fc in (0, 1) if (fx, fy, fc) != (0, 0, 0)]


def _handshake(peers):
    barrier = pltpu.get_barrier_semaphore()
    for peer in peers:
        pl.semaphore_signal(barrier, inc=1, device_id=peer, device_id_type=MESH)
    pl.semaphore_wait(barrier, len(peers))


def _sequencer_scatter(parts, name, collective_id):
    n_t = len(parts)
    dims = [p.shape[1:] for p in parts]
    srcs = [jax.new_ref(p, memory_space=pltpu.MemorySpace.HBM) for p in parts]
    inboxes = [jax.empty_ref(jax.ShapeDtypeStruct((N_DEV - 1, r // 2, cols), BF16), memory_space=pltpu.MemorySpace.HBM)
               for r, cols in dims]

    @pl.kernel(mesh=plsc.ScalarSubcoreMesh(axis_name="sequencer", num_cores=1), name=name,
               scratch_types=[pltpu.SemaphoreType.DMA((n_t,))] * 2,
               compiler_params=pltpu.CompilerParams(collective_id=collective_id))
    def launch(send_sem, recv_sem):
        peers = _others()
        _handshake(peers)
        for t in range(n_t):
            h = dims[t][0] // 2
            for k, (qx, qy, qc) in enumerate(peers):
                pltpu.make_async_remote_copy(
                    src_ref=srcs[t].at[2 * qx + qy, pl.ds(qc * h, h), :], dst_ref=inboxes[t].at[k],
                    send_sem=send_sem.at[t], recv_sem=recv_sem.at[t], device_id=(qx, qy, qc), device_id_type=MESH).start()
        for t in range(n_t):
            win = inboxes[t]
            both = pltpu.make_async_remote_copy(src_ref=win, dst_ref=win, send_sem=send_sem.at[t],
                                                recv_sem=recv_sem.at[t], device_id=peers[0], device_id_type=MESH)
            both.wait_recv()
            both.wait_send()

    launch()
    return [b[...] for b in inboxes]


def _sum_pieces(part, inbox, place, name):
    _, r, cols = part.shape
    h = r // 2
    rows = _chunk_rows(h, cols)
    steps = h // rows

    def body(place_ref, own_ref, in_ref, o_ref):
        acc = own_ref[...].astype(F32)
        for k in range(N_DEV - 1):
            acc = acc + in_ref[k].astype(F32)
        o_ref[...] = acc

    return pl.pallas_call(
        body, name=name,
        grid_spec=pltpu.PrefetchScalarGridSpec(
            num_scalar_prefetch=1, grid=(steps,),
            in_specs=[pl.BlockSpec((None, rows, cols), lambda i, pr: (pr[0], pr[1] * steps + i, 0)),
                      pl.BlockSpec((N_DEV - 1, rows, cols), lambda i, pr: (0, i, 0))],
            out_specs=pl.BlockSpec((rows, cols), lambda i, pr: (pr[1] * steps + i, 0))),
        out_shape=jax.ShapeDtypeStruct((r, cols), F32),
    )(place, part, inbox)


def _sequencer_swap_halves(halves, name, collective_id):
    n_t = len(halves)
    refs = [jax.new_ref(a, memory_space=pltpu.MemorySpace.HBM) for a in halves]

    @pl.kernel(mesh=plsc.ScalarSubcoreMesh(axis_name="sequencer", num_cores=1), name=name,
               scratch_types=[pltpu.SemaphoreType.DMA((n_t,))] * 2,
               compiler_params=pltpu.CompilerParams(collective_id=collective_id))
    def launch(send_sem, recv_sem):
        x, y, c = lax.axis_index("x"), lax.axis_index("y"), lax.axis_index("c")
        sibling = (x, y, 1 - c)
        _handshake([sibling])
        copies = []
        for t in range(n_t):
            h = halves[t].shape[0] // 2
            win = refs[t].at[pl.ds(c * h, h), :]
            copies.append(pltpu.make_async_remote_copy(src_ref=win, dst_ref=win, send_sem=send_sem.at[t],
                                                       recv_sem=recv_sem.at[t], device_id=sibling, device_id_type=MESH))
            copies[-1].start()
        for cp in copies:
            cp.wait()

    launch()
    return [r[...] for r in refs]


def _cond_rows(c16, w, act, name):
    n_l, dm, wid = w.shape

    def body(c_ref, w_ref, o_ref, a_ref):
        cv = c_ref[...]
        if act:
            cv = cv * _sig(cv)
        a_ref[...] = cv
        o_ref[...] = _dot_f32(cv, w_ref[...])

    return pl.pallas_call(
        body, name=name, grid=(n_l,),
        in_specs=[_full((16, dm)), pl.BlockSpec((None, dm, wid), lambda l: (l, 0, 0))],
        out_specs=[pl.BlockSpec((None, 16, wid), lambda l: (l, 0, 0)), _full((16, dm))],
        out_shape=[jax.ShapeDtypeStruct((n_l, 16, wid), F32), jax.ShapeDtypeStruct((16, dm), F32)],
    )(c16, w)


def _outer_grad(ct, dm, name):
    n_l, kk, wid = dm.shape
    d_rows = ct.shape[0]

    def body(c_ref, d_ref, o_ref):
        o_ref[...] = _dot_f32(c_ref[...], d_ref[...])

    return pl.pallas_call(
        body, name=name, grid=(n_l,),
        in_specs=[_full((d_rows, kk)), pl.BlockSpec((None, kk, wid), lambda l: (l, 0, 0))],
        out_specs=pl.BlockSpec((None, d_rows, wid), lambda l: (l, 0, 0)),
        out_shape=jax.ShapeDtypeStruct((n_l, d_rows, wid), F32),
    )(ct, dm)


def _sum_devices(g, name):
    rows, n = g.shape

    def body(g_ref, o_ref):
        acc = g_ref[0:SUBLANES, :]
        for dev in range(1, N_DEV):
            acc = acc + g_ref[dev * SUBLANES:(dev + 1) * SUBLANES, :]
        o_ref[...] = acc

    return pl.pallas_call(body, name=name, out_shape=jax.ShapeDtypeStruct((SUBLANES, n), F32))(g)


def _adamw(w, g, m, v, name):
    shape = w.shape
    cols = shape[-1]
    rows = w.size // cols
    stacked = isinstance(g, (list, tuple))
    layers = list(g) if stacked else [g.reshape(rows, cols)]
    rows_l = rows // len(layers)
    tr = rows_l
    for cand in range(SUBLANES, min(rows_l, 256) + 1, SUBLANES):
        if rows_l % cand == 0:
            tr = cand
    if rows * cols <= COMM_CHUNK_ELEMS:
        tr = rows_l
    tiles_l = rows_l // tr
    c1 = 1.0 / (1.0 - ADAM_B1 ** ADAM_STEP)
    c2 = 1.0 / (1.0 - ADAM_B2 ** ADAM_STEP)

    def body(*refs):
        w_ref, m_ref, v_ref = refs[:3]
        g_refs = refs[3:3 + len(layers)]
        d_ref, mo_ref, vo_ref = refs[3 + len(layers):6 + len(layers)]
        gv = g_refs[0][...]
        for l in range(1, len(layers)):
            gv = jnp.where(pl.program_id(0) >= l * tiles_l, g_refs[l][...], gv)
        m_new = ADAM_B1 * m_ref[...] + (1.0 - ADAM_B1) * gv
        v_new = ADAM_B2 * v_ref[...] + (1.0 - ADAM_B2) * (gv * gv)
        mo_ref[...] = m_new
        vo_ref[...] = v_new
        if stacked:
            refs[-1][...] = gv
        d_ref[...] = -ADAM_LR * ((m_new * c1) / (jnp.sqrt(v_new * c2) + ADAM_EPS) + ADAM_WD * w_ref[...])

    spec = pl.BlockSpec((tr, cols), lambda i: (i, 0))
    g_specs = [pl.BlockSpec((tr, cols), lambda i, l=l: (jnp.clip(i - l * tiles_l, 0, tiles_l - 1), 0))
               for l in range(len(layers))]
    n_out = 4 if stacked else 3
    outs = pl.pallas_call(
        body, name=name, grid=(rows // tr,), in_specs=[spec] * 3 + g_specs, out_specs=[spec] * n_out,
        out_shape=[jax.ShapeDtypeStruct((rows, cols), F32)] * n_out,
    )(*[a.reshape(rows, cols) for a in (w, m, v)], *layers)
    outs = [o.reshape(shape) for o in outs]
    return (*outs[:3], outs[3] if stacked else g)


def _pad_cols(a, cols):
    return jnp.pad(a, [(0, 0)] * (a.ndim - 1) + [(0, cols - a.shape[-1])])


def _flat8(parts, width):
    v = jnp.concatenate([p.reshape(-1) for p in parts])
    return jnp.pad(v, (0, width - v.shape[0])).reshape(SUBLANES, width // SUBLANES)


KV_SHARD = 514
KV_SHARD_PAD = 640
BIG = ("a_w_in", "a_w_out", "kv_w", "b_w_q", "b_w_out", "up0", "up1", "down0", "down1")


def kernel(x, c, ada_w, ada_b, a_w_in, a_lb_logits, a_norm_g, a_w_out, kv_ada_w, kv_ada_b, kv_w, kv_b_f, k_norm_g, b_w_q, q_norm_g, b_w_out, ffn_w_up, ffn_conv_w, ffn_conv_b, ffn_w_down, loss_target, m_ada_w, m_ada_b, m_a_w_in, m_a_lb_logits, m_a_norm_g, m_a_w_out, m_kv_ada_w, m_kv_ada_b, m_kv_w, m_kv_b_f, m_k_norm_g, m_b_w_q, m_q_norm_g, m_b_w_out, m_ffn_w_up, m_ffn_conv_w, m_ffn_conv_b, m_ffn_w_down, v_ada_w, v_ada_b, v_a_w_in, v_a_lb_logits, v_a_norm_g, v_a_w_out, v_kv_ada_w, v_kv_ada_b, v_kv_w, v_kv_b_f, v_k_norm_g, v_b_w_q, v_q_norm_g, v_b_w_out, v_ffn_w_up, v_ffn_conv_w, v_ffn_conv_b, v_ffn_w_down):
    dm, ff = D_MODEL, D_FF
    ix, iy, ic = lax.axis_index("x"), lax.axis_index("y"), lax.axis_index("c")
    chip = 2 * ix + iy
    dev = 2 * chip + ic

    w1 = 10240
    g1 = _allgather8(_flat8([c, a_lb_logits, ffn_conv_w], w1), "gather_cond").reshape(N_DEV, w1)
    c_all = g1[:, :dm]
    per_chip = g1[0::2]
    lb_logits = per_chip[:, dm:dm + 512].reshape(N_CHIPS, 2, 256).transpose(1, 0, 2).reshape(2, dm)
    conv_w = per_chip[:, dm + 512:dm + 512 + 2 * CONV_W * FFN_COLS].reshape(N_CHIPS, 2, CONV_W, FFN_COLS)
    conv_w = conv_w.transpose(1, 2, 0, 3).reshape(2, CONV_W, 2, ff).transpose(0, 2, 1, 3)
    conv_b = ffn_conv_b.reshape(2, 2, 1, ff)
    lb = jax.nn.softmax(lb_logits, axis=0)[0:1]

    c16 = jnp.pad(c_all, ((0, 8), (0, 0)))
    mod_ada, c_act16 = _cond_rows(c16, ada_w, True, "mod_ada")
    mod_kv, _ = _cond_rows(c16, kv_ada_w[None], True, "mod_kv")
    mine = jnp.concatenate([mod_ada[0, :8], mod_ada[1, :8], mod_kv[0, :8]], axis=1)
    w2 = mine.shape[1]
    g2 = _allgather8(mine, "gather_mod").reshape(N_DEV, 8, w2)[0::2]
    my_rows = lax.dynamic_index_in_dim(g2, dev, axis=1, keepdims=False)
    mod0 = my_rows[:, 0:1536].reshape(6 * dm) + ada_b[0]
    mod1 = my_rows[:, 1536:3072].reshape(6 * dm) + ada_b[1]
    modk = my_rows[:, 3072:3584].reshape(2 * dm) + kv_ada_b
    mods = {"l0": [v.reshape(1, dm) for v in jnp.split(mod0, 6)],
            "l1": [v.reshape(1, dm) for v in jnp.split(mod1, 6)],
            "kv": [v.reshape(1, dm) for v in jnp.split(modk, 2)]}

    local = [(a_w_in, 0), (a_w_out, 0), (_pad_cols(kv_w, KV_SHARD_PAD)[None], 0), (b_w_q, 0), (b_w_out, 0),
             (ffn_w_up, 0), (ffn_w_up, 1), (ffn_w_down, 0), (ffn_w_down, 1)]
    chip_arr = chip.reshape(1).astype(jnp.int32)
    local = dict(zip(BIG, local))
    stages = {"mixer0": ("a_w_in",), "out0": ("a_w_out",), "ffn0": ("up0", "down0"),
              "layer1": ("kv_w", "b_w_q", "b_w_out", "up1", "down1")}
    arriving = {}

    def launch(stage, behind):
        shards = [local[n][0] for n in stages[stage]]
        if behind is not None:
            shards, _ = lax.optimization_barrier((shards, behind))
        own = [_cast_own_block(w, local[n][1], chip_arr, "cast_" + n) for n, w in zip(stages[stage], shards)]
        arriving[stage] = _sequencer_gather(own, "gather_" + stage, 1 + list(stages).index(stage))
        return own

    launch("out0", launch("mixer0", None))
    launch("ffn0", mod0)
    rowwise = lambda g: g.reshape(1, -1, dm)

    def weights_at(stage, token):
        if stage == "launch_layer1":
            launch("layer1", token)
            return {}, token
        got, token = lax.optimization_barrier((arriving[stage], token))
        g = dict(zip(stages[stage], got))
        if stage == "mixer0":
            return {"a_w_in": g["a_w_in"]}, token
        if stage == "out0":
            return {"a_w_out": rowwise(g["a_w_out"])}, token
        if stage == "ffn0":
            return {"up0": g["up0"], "down0": rowwise(g["down0"])}, token
        s0, s1, s2, s3 = (g["kv_w"][p] for p in range(N_CHIPS))
        second = dm - KV_SHARD
        w_k = jnp.concatenate([s0[:, :KV_SHARD], s1[:, :second]], axis=1)
        w_v = jnp.concatenate([s1[:, second:KV_SHARD], s2[:, :KV_SHARD], s3[:, :KV_SHARD - HEADS]], axis=1)
        w_f = _pad_cols(s3[:, KV_SHARD - HEADS:KV_SHARD], LANES)
        return {"kv_k": w_k[None], "kv_v": w_v[None], "kv_f": w_f[None], "b_w_q": g["b_w_q"],
                "b_w_out": rowwise(g["b_w_out"]), "up1": g["up1"], "down1": rowwise(g["down1"])}, token

    vecs = {"a_norm_g": jnp.tile(a_norm_g, (1, HEADS)), "k_norm_g": jnp.tile(k_norm_g[None], (1, HEADS)),
            "q_norm_g": jnp.tile(q_norm_g, (1, HEADS)), "kv_b_f": _pad_cols(kv_b_f[None], LANES),
            "conv_w0": conv_w[0], "conv_b0": conv_b[0], "conv_w1": conv_w[1], "conv_b1": conv_b[1]}

    sq, grad_x, big, small, marks = _local_step(x[0], loss_target[0], mods, lb, vecs, weights_at)

    gk, gv, gf = big["kv_k"][0], big["kv_v"][0], big["kv_f"][0][:, :HEADS]
    second = dm - KV_SHARD
    kv_blocks = [gk[:, :KV_SHARD], jnp.concatenate([gk[:, KV_SHARD:], gv[:, :KV_SHARD - second]], axis=1),
                 gv[:, KV_SHARD - second:2 * KV_SHARD - second], jnp.concatenate([gv[:, 2 * KV_SHARD - second:], gf], axis=1)]
    kv_grad = jnp.stack([_pad_cols(b, KV_SHARD_PAD) for b in kv_blocks])
    chipwise = lambda g: g.reshape(N_CHIPS, -1, dm)
    parts = dict(zip(BIG, [big["a_w_in"], chipwise(big["a_w_out"]), kv_grad, big["b_w_q"], chipwise(big["b_w_out"]),
                           big["up0"], big["up1"], chipwise(big["down0"]), chipwise(big["down1"])]))
    place = jnp.stack([chip, ic, dev]).astype(jnp.int32)

    served = []
    boxes = {}

    groups = (("up1", "down1"), ("b_w_out", "b_w_q", "kv_w"), ("up0", "down0"), ("a_w_out", "a_w_in"))

    def scatter_group(k):
        mine = [parts[n] for n in groups[k]]
        if served:
            mine, _ = lax.optimization_barrier((mine, served[-1]))
        boxes[k] = _sequencer_scatter(mine, "scatter_grads_%d" % k, 5 + k)
        served.append(boxes[k])

    def sum_group(k, token):
        inboxes, _ = lax.optimization_barrier((boxes[k], token))
        return [_sum_pieces(parts[n], box, place, "sum_" + n) for n, box in zip(groups[k], inboxes)]

    def swap_group(k, halves, behind):
        halves, _ = lax.optimization_barrier((halves, behind))
        return dict(zip(groups[k], _sequencer_swap_halves(halves, "swap_grads_%d" % k, 9 + k)))

    for k in range(3):
        scatter_group(k)
    halves = [sum_group(0, marks["attention_bwd"]), sum_group(1, marks["ffn0_bwd"]), sum_group(2, marks["mixer0_bwd"])]

    fold = lambda a: a.sum(axis=0)
    heads = lambda a: fold(a).reshape(HEADS, HEAD_DIM).sum(axis=0)
    conv_flat = lambda a: a.sum(axis=2).transpose(1, 0, 2)
    pieces = ([fold(a) for a in small["mod_l0"]] + [fold(a) for a in small["mod_l1"]] + [fold(a) for a in small["mod_kv"]]
              + [conv_flat(small["conv0"]), conv_flat(small["conv1"]), heads(small["a_norm_g"]), heads(small["k_norm_g"]),
                 heads(small["q_norm_g"]), fold(small["kv_b_f"]), fold(small["lb"]),
                 0.5 * jnp.sum(sq).reshape(1) / dm])
    w3 = 61440
    small_vec, _ = lax.optimization_barrier((_flat8(pieces, w3), served[2]))
    g3 = _sequencer_allgather8(small_vec, dev, "gather_small", 13)
    served.append(g3)
    scatter_group(3)
    rs = {}
    for k in range(3):
        rs.update(swap_group(k, halves[k], g3))
    tot = _sum_devices(g3, "sum_small").reshape(w3)
    n_mod = 14 * dm
    dmod_all = g3.reshape(N_DEV, w3)[:, :n_mod]
    o = n_mod
    conv_tot = [tot[o + l * 8 * ff: o + (l + 1) * 8 * ff].reshape(4, 2 * ff) for l in range(2)]
    o += 16 * ff
    g_a_norm, g_k_norm, g_q_norm = (tot[o + i * HEAD_DIM: o + (i + 1) * HEAD_DIM] for i in range(3))
    o += 3 * HEAD_DIM
    g_kv_b_f = tot[o:o + HEADS]
    dlb = tot[o + LANES:o + LANES + dm]
    loss = tot[o + LANES + dm]

    ct = _pad_cols(c_act16[:8].T, LANES)
    dmod_pad = jnp.pad(dmod_all, ((0, LANES - N_DEV), (0, 0)))
    cols_ada = jnp.stack([lax.dynamic_slice_in_dim(dmod_pad, l * 6 * dm + chip * 1536, 1536, axis=1) for l in range(2)])
    cols_kv = lax.dynamic_slice_in_dim(dmod_pad, 12 * dm + chip * 512, 512, axis=1)[None]
    g_ada_w = _outer_grad(ct, cols_ada, "grad_ada_w")
    g_kv_ada_w = _outer_grad(ct, cols_kv, "grad_kv_ada_w")[0]

    my_lb = lax.dynamic_slice_in_dim(lb[0], chip * 256, 256)
    l0 = lax.dynamic_slice_in_dim(dlb, chip * 256, 256) * my_lb * (1.0 - my_lb)
    grads = {
        "ada_w": g_ada_w, "ada_b": jnp.stack([tot[:6 * dm], tot[6 * dm:12 * dm]]),
        "a_lb_logits": jnp.stack([l0, -l0]), "a_norm_g": g_a_norm[None],
        "kv_ada_w": g_kv_ada_w, "kv_ada_b": tot[12 * dm:14 * dm],
        "kv_w": rs["kv_w"][:, :KV_SHARD], "kv_b_f": g_kv_b_f, "k_norm_g": g_k_norm,
        "b_w_q": rs["b_w_q"][None], "q_norm_g": g_q_norm[None], "b_w_out": rs["b_w_out"][None],
        "ffn_w_up": [rs["up0"], rs["up1"]],
        "ffn_conv_w": jnp.stack([lax.dynamic_slice_in_dim(ct_l[:CONV_W], chip * FFN_COLS, FFN_COLS, axis=1) for ct_l in conv_tot]),
        "ffn_conv_b": jnp.stack([ct_l[CONV_W] for ct_l in conv_tot]),
        "ffn_w_down": [rs["down0"], rs["down1"]],
    }
    weights = dict(ada_w=ada_w, ada_b=ada_b, a_w_in=a_w_in, a_lb_logits=a_lb_logits, a_norm_g=a_norm_g, a_w_out=a_w_out,
                   kv_ada_w=kv_ada_w, kv_ada_b=kv_ada_b, kv_w=kv_w, kv_b_f=kv_b_f, k_norm_g=k_norm_g, b_w_q=b_w_q,
                   q_norm_g=q_norm_g, b_w_out=b_w_out, ffn_w_up=ffn_w_up, ffn_conv_w=ffn_conv_w, ffn_conv_b=ffn_conv_b,
                   ffn_w_down=ffn_w_down)
    m_in = dict(ada_w=m_ada_w, ada_b=m_ada_b, a_w_in=m_a_w_in, a_lb_logits=m_a_lb_logits, a_norm_g=m_a_norm_g,
                a_w_out=m_a_w_out, kv_ada_w=m_kv_ada_w, kv_ada_b=m_kv_ada_b, kv_w=m_kv_w, kv_b_f=m_kv_b_f,
                k_norm_g=m_k_norm_g, b_w_q=m_b_w_q, q_norm_g=m_q_norm_g, b_w_out=m_b_w_out, ffn_w_up=m_ffn_w_up,
                ffn_conv_w=m_ffn_conv_w, ffn_conv_b=m_ffn_conv_b, ffn_w_down=m_ffn_w_down)
    v_in = dict(ada_w=v_ada_w, ada_b=v_ada_b, a_w_in=v_a_w_in, a_lb_logits=v_a_lb_logits, a_norm_g=v_a_norm_g,
                a_w_out=v_a_w_out, kv_ada_w=v_kv_ada_w, kv_ada_b=v_kv_ada_b, kv_w=v_kv_w, kv_b_f=v_kv_b_f,
                k_norm_g=v_k_norm_g, b_w_q=v_b_w_q, q_norm_g=v_q_norm_g, b_w_out=v_b_w_out, ffn_w_up=v_ffn_w_up,
                ffn_conv_w=v_ffn_conv_w, ffn_conv_b=v_ffn_conv_b, ffn_w_down=v_ffn_w_down)

    names = list(weights)
    step = lambda n: _adamw(weights[n], grads[n], m_in[n], v_in[n], "adamw_" + n)
    grads = {n: g if isinstance(g, list) else g.reshape(weights[n].shape) for n, g in grads.items()}
    upd = {n: step(n) for n in names if n not in groups[3]}
    last = sum_group(3, [u[0] for u in upd.values()])
    for n, g in swap_group(3, last, last).items():
        grads[n] = g[None]
        upd[n] = step(n)
    grads = {n: upd[n][3] for n in names}
    return (loss, grad_x[None], *[grads[n] for n in names], *[upd[n][0] for n in names],
            *[upd[n][1] for n in names], *[upd[n][2] for n in names])
```

```python
import jax
import jax.numpy as jnp
from jax import lax
from jax.experimental import pallas as pl
from jax.experimental.pallas import tpu as pltpu
from jax.experimental.pallas import tpu_sc as plsc

F32 = jnp.float32
BF16 = jnp.bfloat16

D_MODEL = 1024
HEADS = 8
HEAD_DIM = 128
A_CHUNK = 64
D_FF = 2816
CONV_W = 3
EPS = 1e-6
NEG_INF = -1e30
N_CHIPS = 4
N_DEV = 8

ADAM_LR = 0.001
ADAM_B1 = 0.9
ADAM_B2 = 0.999
ADAM_EPS = 1e-08
ADAM_WD = 0.01
ADAM_STEP = 10

SUBLANES = 8
BF16_ROWS = 16
LANES = 128
HALO = BF16_ROWS
ROW_TILE = 512
TOKEN_TILE_TN = 2048
FFN_COLS = 1408
FFN_ROWS = 256
HGRN_ROWS = 512
ATT_TILE = 512
ATT_SPLIT = 2
ATT_BWD_SPLIT = 1
ATT_FWD_HEADS = 8
ATT_BWD_HEADS = 8
MESH = pl.DeviceIdType.MESH


def _sig(x):
    return jax.nn.sigmoid(x)


def _dot(a, b):
    return jnp.dot(a, b, preferred_element_type=F32)


def _dot_nt(a, b):
    return lax.dot_general(a, b, (((1,), (1,)), ((), ())), preferred_element_type=F32)


def _dot_tn(a, b):
    return lax.dot_general(a, b, (((0,), (0,)), ((), ())), preferred_element_type=F32)


def _split2(x):
    hi = x.astype(BF16)
    lo = (x - hi.astype(F32)).astype(BF16)
    return hi, lo


def _dot_f32(a, b):
    ah, al = _split2(a)
    bh, bl = _split2(b)
    return _dot(ah, bh) + _dot(ah, bl) + _dot(al, bh)


def _tri_dot(tri, x):
    hi = x.astype(BF16)
    r = x - hi.astype(F32)
    mid = r.astype(BF16)
    lo = (r - mid.astype(F32)).astype(BF16)
    return _dot(tri, hi) + _dot(tri, mid) + _dot(tri, lo)


def _tri(n, upper=False):
    r = lax.broadcasted_iota(jnp.int32, (n, n), 0)
    c = lax.broadcasted_iota(jnp.int32, (n, n), 1)
    keep = (c >= r) if upper else (c <= r)
    return jnp.where(keep, 1.0, 0.0).astype(BF16)


def _colsum8(v):
    rows, n = v.shape
    return v.reshape(rows // SUBLANES, SUBLANES, n).sum(axis=0)


def _full(shape):
    nd = len(shape)
    return pl.BlockSpec(shape, lambda *_: (0,) * nd)


def _tile(n, want):
    t = min(n, want)
    assert n % t == 0, (n, t)
    return t


def _mm_tn(a, d, p_n, name):
    m_rows, k = a.shape
    g_n, _, w_cols = d.shape
    per = p_n // g_n
    n = w_cols // per
    tm = _tile(m_rows, TOKEN_TILE_TN if k <= D_MODEL else ROW_TILE)
    steps = m_rows // tm

    def body(a_ref, d_ref, o_ref, acc):
        m = pl.program_id(1)

        @pl.when(m == 0)
        def _():
            acc[...] = jnp.zeros_like(acc)

        acc[...] += _dot_tn(a_ref[...], d_ref[...])

        @pl.when(m == steps - 1)
        def _():
            o_ref[...] = acc[...].astype(BF16)

    return pl.pallas_call(
        body, name=name, grid=(p_n, steps),
        in_specs=[pl.BlockSpec((tm, k), lambda p, m: (m, 0)),
                  pl.BlockSpec((None, tm, n), lambda p, m: (p // per, m, p % per))],
        out_specs=pl.BlockSpec((None, k, n), lambda p, m: (p, 0, 0)),
        out_shape=jax.ShapeDtypeStruct((p_n, k, n), BF16),
        scratch_shapes=[pltpu.VMEM((k, n), F32)],
    )(a, d)


def _premix_proj(x, shift, scale, w, name):
    s, dm = x.shape
    p_n, _, n = w.shape
    tm = _tile(s, ROW_TILE)

    def body(x_ref, sh_ref, sc_ref, w_ref, h_ref, o_ref):
        xv = x_ref[...]
        inv = lax.rsqrt(jnp.mean(xv * xv, axis=-1, keepdims=True) + EPS)
        h = (xv * inv * (1.0 + sc_ref[...]) + sh_ref[...]).astype(BF16)
        h_ref[...] = h
        for p in range(p_n):
            o_ref[:, p * n:(p + 1) * n] = _dot(h, w_ref[p])

    row = pl.BlockSpec((tm, dm), lambda i: (i, 0))
    vec = _full((1, dm))
    return pl.pallas_call(
        body, name=name, grid=(s // tm,), in_specs=[row, vec, vec, _full(w.shape)],
        out_specs=[row, pl.BlockSpec((tm, p_n * n), lambda i: (i, 0))],
        out_shape=[jax.ShapeDtypeStruct((s, dm), BF16), jax.ShapeDtypeStruct((s, p_n * n), F32)],
    )(x, shift, scale, w)


def _premix_bwd(x, terms, dres, name, branch=None):
    s, dm = x.shape
    tm = _tile(s, ROW_TILE)
    pairs = [pr for _, prs in terms for pr in prs]
    n_in = 2 + len(terms) + 2 * len(pairs) + (2 if branch else 0)

    def body(*refs):
        x_ref, dres_ref = refs[:2]
        sc_refs = refs[2:2 + len(terms)]
        mm_refs = refs[2 + len(terms):2 + len(terms) + 2 * len(pairs)]
        outs = refs[n_in:]

        @pl.when(pl.program_id(0) == 0)
        def _():
            for o in outs[1:1 + 2 * len(terms)]:
                o[...] = jnp.zeros_like(o)
            if branch:
                outs[-1][...] = jnp.zeros_like(outs[-1])

        xv = x_ref[...]
        inv = lax.rsqrt(jnp.mean(xv * xv, axis=-1, keepdims=True) + EPS)
        r = xv * inv
        dx = dres_ref[...]
        k = 0
        for t, (_, prs) in enumerate(terms):
            dh = None
            for d, w in prs:
                d_ref, w_ref = mm_refs[2 * k], mm_refs[2 * k + 1]
                k += 1
                p_n, _, n = w.shape
                per = p_n // d.shape[0]
                for p in range(p_n):
                    part = _dot_nt(d_ref[p // per, :, (p % per) * n:(p % per + 1) * n], w_ref[p])
                    dh = part if dh is None else dh + part
            dr = dh * (1.0 + sc_refs[t][...])
            dx = dx + inv * (dr - r * jnp.mean(dr * r, axis=-1, keepdims=True))
            outs[1 + 2 * t][...] += _colsum8(dh)
            outs[2 + 2 * t][...] += _colsum8(dh * r)
        outs[0][...] = dx
        if branch:
            y_ref, g_ref = refs[n_in - 2:n_in]
            outs[-2][0] = (dx * g_ref[...]).astype(BF16)
            outs[-1][...] += _colsum8(dx * y_ref[...])

    row = pl.BlockSpec((tm, dm), lambda i: (i, 0))
    vec, acc = _full((1, dm)), _full((SUBLANES, dm))
    ins, specs = [x, dres] + [sc for sc, _ in terms], [row, row] + [vec] * len(terms)
    for d, w in pairs:
        ins += [d, w]
        specs += [pl.BlockSpec((d.shape[0], tm, d.shape[2]), lambda i: (0, i, 0)), _full(w.shape)]
    out_shape = [jax.ShapeDtypeStruct((s, dm), F32)] + [jax.ShapeDtypeStruct((SUBLANES, dm), F32)] * (2 * len(terms))
    out_specs = [row] + [acc] * (2 * len(terms))
    if branch:
        ins += list(branch)
        specs += [row, vec]
        out_shape += [jax.ShapeDtypeStruct((1, s, dm), BF16), jax.ShapeDtypeStruct((SUBLANES, dm), F32)]
        out_specs += [pl.BlockSpec((1, tm, dm), lambda i: (0, i, 0)), acc]
    outs = pl.pallas_call(body, name=name, grid=(s // tm,), in_specs=specs, out_specs=out_specs,
                          out_shape=out_shape)(*ins)
    partials = [(outs[1 + 2 * t], outs[2 + 2 * t]) for t in range(len(terms))]
    return (outs[0], partials) + ((outs[-2], outs[-1]) if branch else ())


def _conv_taps(e, w, b):
    return w[2:3] * e + w[1:2] * pltpu.roll(e, 1, 0) + w[0:1] * pltpu.roll(e, 2, 0) + b


def _ffn_specs(s, tm, cb):
    hb = tm // HALO
    last = s // HALO - 1
    main = pl.BlockSpec((2, tm, cb), lambda j, i: (0, i, j))
    prev = pl.BlockSpec((2, HALO, cb), lambda j, i: (0, jnp.maximum(i * hb - 1, 0), j))
    nxt = pl.BlockSpec((2, HALO, cb), lambda j, i: (0, jnp.minimum((i + 1) * hb, last), j))
    wspec = pl.BlockSpec((2, CONV_W, cb), lambda j, i: (0, 0, j))
    bspec = pl.BlockSpec((2, 1, cb), lambda j, i: (0, 0, j))
    return main, prev, nxt, wspec, bspec


def _convglu_bwd(u, c, dffn, w_down, w, name):
    _, s, f = u.shape
    dm = dffn.shape[2]
    tm = _tile(s, 256)
    cb = _tile(f, FFN_COLS)
    steps = s // tm
    n_ext = tm + HALO
    main, _, nxt, wspec, _ = _ffn_specs(s, tm, cb)
    hb = tm // HALO
    last = s // HALO - 1
    d_main = pl.BlockSpec((None, tm, dm), lambda j, i: (0, i, 0))
    d_next = pl.BlockSpec((None, HALO, dm), lambda j, i: (0, jnp.minimum((i + 1) * hb, last), 0))
    wd_spec = pl.BlockSpec((None, cb, dm), lambda j, i: (0, j, 0))

    def body(u_ref, c_ref, cn_ref, d_ref, dn_ref, wd_ref, w_ref, du_ref, acc_ref):
        i = pl.program_id(1)
        notlast = jnp.where(i < steps - 1, 1.0, 0.0)

        @pl.when(i == 0)
        def _():
            acc_ref[...] = jnp.zeros_like(acc_ref)

        gate, val = (jnp.concatenate([c_ref[g].astype(F32), cn_ref[g].astype(F32)], axis=0) for g in range(2))
        wd = wd_ref[...]
        da = jnp.concatenate([_dot_nt(d_ref[...], wd).astype(BF16).astype(F32),
                              _dot_nt(dn_ref[...], wd).astype(BF16).astype(F32) * notlast], axis=0)
        sg = _sig(gate)
        d_val = da * gate * sg
        d_gate = da * val * (sg * (1.0 + gate * (1.0 - sg)))

        def finish(g, d):
            wv = w_ref[g]
            d1, d2 = pltpu.roll(d, n_ext - 1, 0), pltpu.roll(d, n_ext - 2, 0)
            du_ref[g] = (wv[2:3] * d + wv[1:2] * d1 + wv[0:1] * d2)[0:tm].astype(BF16)
            uv = u_ref[g].astype(F32)
            acc_ref[g, 2] += _colsum8(d[0:tm] * uv)
            acc_ref[g, 1] += _colsum8(d1[0:tm] * uv)
            acc_ref[g, 0] += _colsum8(d2[0:tm] * uv)
            acc_ref[g, 3] += _colsum8(d[0:tm])

        finish(0, d_gate)
        finish(1, d_val)

    return pl.pallas_call(
        body, name=name, grid=(f // cb, steps),
        in_specs=[main, main, nxt, d_main, d_next, wd_spec, wspec],
        out_specs=[main, pl.BlockSpec((2, 4, SUBLANES, cb), lambda j, i: (0, 0, 0, j))],
        out_shape=[jax.ShapeDtypeStruct((2, s, f), BF16), jax.ShapeDtypeStruct((2, 4, SUBLANES, f), F32)],
    )(u, c, c, dffn, dffn, w_down, w)


def _hgrn_gates(q_raw, f_raw, lb, tri):
    sf = _sig(f_raw)
    fg = lb + (1.0 - lb) * sf
    b = _tri_dot(tri, jnp.log(fg))
    return q_raw * _sig(q_raw), 1.0 - fg, b, fg, sf


def _hgrn_fwd(proj, lb, norm_g, name):
    s = proj.shape[0]
    tb = _tile(s, HGRN_ROWS)
    n_c = tb // A_CHUNK
    half = A_CHUNK // 2

    def body(q_ref, f_ref, v_ref, g_ref, lb_ref, ng_ref, o_ref, yp_ref, st_ref, state):
        @pl.when(pl.program_id(0) == 0)
        def _():
            state[...] = jnp.zeros_like(state)

        tri = _tri(A_CHUNK)
        causal = lax.broadcasted_iota(jnp.int32, (A_CHUNK, A_CHUNK), 1) <= lax.broadcasted_iota(
            jnp.int32, (A_CHUNK, A_CHUNK), 0)

        def chunk(ci, carry):
            rows = pl.ds(ci * A_CHUNK, A_CHUNK)
            heads = [slice(h * HEAD_DIM, (h + 1) * HEAD_DIM) for h in range(HEADS)]
            qs, k, b, _, _ = _hgrn_gates(q_ref[rows, :], f_ref[rows, :], lb_ref[...], tri)
            b_mid, b_last = b[half:half + 1], b[A_CHUNK - 1:A_CHUNK]
            q_i = (qs * jnp.exp(b - b_mid)).astype(BF16)
            k_i = (k * jnp.exp(b_mid - b)).astype(BF16)
            q_e = (qs * jnp.exp(b)).astype(BF16)
            k_s = (k * jnp.exp(b_last - b)).astype(BF16)
            decay = jnp.exp(b_last)
            vb = v_ref[rows, :].astype(BF16)
            scores = [jnp.where(causal, _dot_nt(q_i[:, cs], k_i[:, cs]), 0.0).astype(BF16) for cs in heads]
            st = [state[h] for h in range(HEADS)]
            outs = [_dot(scores[h], vb[:, cs]) + _dot_nt(q_e[:, cs], st[h].astype(BF16)) for h, cs in enumerate(heads)]
            for h, cs in enumerate(heads):
                st_ref[ci, h] = st[h]
                state[h] = st[h] * decay[:, cs] + _dot_tn(vb[:, cs], k_s[:, cs])
            o = jnp.concatenate(outs, axis=1)
            o_ref[rows, :] = o
            sq = o * o
            inv = jnp.concatenate([jnp.broadcast_to(lax.rsqrt(jnp.mean(sq[:, cs], axis=-1, keepdims=True) + EPS),
                                                    (A_CHUNK, HEAD_DIM)) for cs in heads], axis=1)
            g_raw = g_ref[rows, :]
            yp_ref[rows, :] = (o * inv * ng_ref[...] * (g_raw * _sig(g_raw))).astype(BF16)
            return carry

        for step in range(n_c):
            chunk(step, 0)

    col = lambda j: pl.BlockSpec((tb, D_MODEL), lambda i: (i, j))
    vec = _full((1, D_MODEL))
    return pl.pallas_call(
        body, name=name, grid=(s // tb,), in_specs=[col(0), col(1), col(2), col(3), vec, vec],
        out_specs=[col(0), col(0), pl.BlockSpec((n_c, HEADS, HEAD_DIM, HEAD_DIM), lambda i: (i, 0, 0, 0))],
        out_shape=[jax.ShapeDtypeStruct((s, D_MODEL), F32), jax.ShapeDtypeStruct((s, D_MODEL), BF16),
                   jax.ShapeDtypeStruct((s // A_CHUNK, HEADS, HEAD_DIM, HEAD_DIM), F32)],
        scratch_shapes=[pltpu.VMEM((HEADS, HEAD_DIM, HEAD_DIM), F32)],
    )(proj, proj, proj, proj, lb, norm_g)


def _hgrn_bwd(proj, lb, norm_g, o, states, dout, w_out, name):
    s = proj.shape[0]
    tb = _tile(s, HGRN_ROWS)
    n_c = tb // A_CHUNK
    n_b = s // tb
    half = A_CHUNK // 2

    def body(q_ref, f_ref, v_ref, g_ref, lb_ref, ng_ref, o_ref, st_ref, dout_ref, w_ref, dp_ref, dlb_ref, dng_ref,
             dstate, dyp_ref):
        @pl.when(pl.program_id(0) == 0)
        def _():
            dstate[...] = jnp.zeros_like(dstate)
            dlb_ref[...] = jnp.zeros_like(dlb_ref)
            dng_ref[...] = jnp.zeros_like(dng_ref)

        dyp_ref[...] = _dot_nt(dout_ref[0], w_ref[0])

        tri = _tri(A_CHUNK)
        tri_up = _tri(A_CHUNK, upper=True)
        row_id = lax.broadcasted_iota(jnp.int32, (A_CHUNK, D_MODEL), 0)
        causal = lax.broadcasted_iota(jnp.int32, (A_CHUNK, A_CHUNK), 1) <= lax.broadcasted_iota(
            jnp.int32, (A_CHUNK, A_CHUNK), 0)

        def chunk(cj, carry):
            ci = n_c - 1 - cj
            rows = pl.ds(ci * A_CHUNK, A_CHUNK)
            heads = [slice(h * HEAD_DIM, (h + 1) * HEAD_DIM) for h in range(HEADS)]
            cat = lambda parts: jnp.concatenate(parts, axis=1)
            per_head_mean = lambda a: cat([jnp.broadcast_to(jnp.mean(a[:, cs], axis=-1, keepdims=True),
                                                            (A_CHUNK, HEAD_DIM)) for cs in heads])
            q_raw, lbv = q_ref[rows, :], lb_ref[...]
            qs, k, b, fg, sf = _hgrn_gates(q_raw, f_ref[rows, :], lbv, tri)
            b_mid, b_last = b[half:half + 1], b[A_CHUNK - 1:A_CHUNK]
            e_qi, e_ki, e_q, e_ks = jnp.exp(b - b_mid), jnp.exp(b_mid - b), jnp.exp(b), jnp.exp(b_last - b)
            decay = jnp.exp(b_last)
            q_i, k_i, q_e, k_s = qs * e_qi, k * e_ki, qs * e_q, k * e_ks
            qib, kib, qeb, ksb = q_i.astype(BF16), k_i.astype(BF16), q_e.astype(BF16), k_s.astype(BF16)
            vb = v_ref[rows, :].astype(BF16)
            ov, g_raw, dy, ng = o_ref[rows, :], g_ref[rows, :], dyp_ref[rows, :], ng_ref[...]
            inv = lax.rsqrt(per_head_mean(ov * ov) + EPS)
            nrm = ov * inv
            sg = _sig(g_raw)
            gs = g_raw * sg
            dn = dy * ng * gs
            dng_ref[0:1, :] += jnp.sum(dy * nrm * gs, axis=0, keepdims=True)
            dg_raw = dy * nrm * ng * (sg * (1.0 + g_raw * (1.0 - sg)))
            do = (inv * (dn - nrm * per_head_mean(dn * nrm))).astype(BF16)
            st_prev = [st_ref[ci, h] for h in range(HEADS)]
            dst = [dstate[h] for h in range(HEADS)]
            dstb = [d.astype(BF16) for d in dst]
            scores = [jnp.where(causal, _dot_nt(qib[:, cs], kib[:, cs]), 0.0).astype(BF16) for cs in heads]
            d_scores = [jnp.where(causal, _dot_nt(do[:, cs], vb[:, cs]), 0.0).astype(BF16) for cs in heads]
            dv = cat([_dot_tn(scores[h], do[:, cs]) + _dot_nt(ksb[:, cs], dstb[h]) for h, cs in enumerate(heads)])
            dq_i = cat([_dot(d_scores[h], kib[:, cs]) for h, cs in enumerate(heads)])
            dk_i = cat([_dot_tn(d_scores[h], qib[:, cs]) for h, cs in enumerate(heads)])
            dq_e = cat([_dot(do[:, cs], st_prev[h].astype(BF16)) for h, cs in enumerate(heads)])
            dk_s = cat([_dot(vb[:, cs], dstb[h]) for h, cs in enumerate(heads)])
            d_decay = cat([jnp.sum(st_prev[h] * dst[h], axis=0, keepdims=True) for h in range(HEADS)])
            for h, cs in enumerate(heads):
                dstate[h] = dst[h] * decay[:, cs] + _dot_tn(do[:, cs], qeb[:, cs])
            dq = dq_i * e_qi + dq_e * e_q
            dk = dk_i * e_ki + dk_s * e_ks
            t_qi, t_ki, t_ks = dq_i * q_i, dk_i * k_i, dk_s * k_s
            db = t_qi - t_ki + dq_e * q_e - t_ks
            db_mid = jnp.sum(t_ki - t_qi, axis=0, keepdims=True)
            db_last = jnp.sum(t_ks, axis=0, keepdims=True) + d_decay * decay
            db = db + jnp.where(row_id == half, db_mid, 0.0) + jnp.where(row_id == A_CHUNK - 1, db_last, 0.0)
            dfg = _tri_dot(tri_up, db) / fg - dk
            dlb_ref[0:1, :] += jnp.sum(dfg * (1.0 - sf), axis=0, keepdims=True)
            sq = _sig(q_raw)
            dp_ref[0, rows, :] = (dq * (sq * (1.0 + q_raw * (1.0 - sq)))).astype(BF16)
            dp_ref[1, rows, :] = (dfg * (1.0 - lbv) * sf * (1.0 - sf)).astype(BF16)
            dp_ref[2, rows, :] = dv.astype(BF16)
            dp_ref[3, rows, :] = dg_raw.astype(BF16)
            return carry

        for step in range(n_c):
            chunk(step, 0)

    col = lambda j: pl.BlockSpec((tb, D_MODEL), lambda i: (n_b - 1 - i, j))
    vec = _full((1, D_MODEL))
    acc = _full((SUBLANES, D_MODEL))
    return pl.pallas_call(
        body, name=name, grid=(n_b,),
        in_specs=[col(0), col(1), col(2), col(3), vec, vec, col(0),
                  pl.BlockSpec((n_c, HEADS, HEAD_DIM, HEAD_DIM), lambda i: (n_b - 1 - i, 0, 0, 0)),
                  pl.BlockSpec((1, tb, D_MODEL), lambda i: (0, n_b - 1 - i, 0)), _full(w_out.shape)],
        out_specs=[pl.BlockSpec((4, tb, D_MODEL), lambda i: (0, n_b - 1 - i, 0)), acc, acc],
        out_shape=[jax.ShapeDtypeStruct((4, s, D_MODEL), BF16), jax.ShapeDtypeStruct((SUBLANES, D_MODEL), F32),
                   jax.ShapeDtypeStruct((SUBLANES, D_MODEL), F32)],
        scratch_shapes=[pltpu.VMEM((HEADS, HEAD_DIM, HEAD_DIM), F32), pltpu.VMEM((tb, D_MODEL), F32)],
    )(proj, proj, proj, proj, lb, norm_g, o, states, dout, w_out)


def _head_rms(raw_ref, g_ref, mult, y_ref):
    for h in range(HEADS):
        cs = slice(h * HEAD_DIM, (h + 1) * HEAD_DIM)
        xv = raw_ref[:, cs]
        inv = lax.rsqrt(jnp.mean(xv * xv, axis=-1, keepdims=True) + EPS)
        y_ref[:, cs] = (xv * inv * g_ref[:, cs] * mult).astype(BF16)


def _proj_headnorm(a, w, g, mult, name):
    s, k = a.shape
    p_n, _, n = w.shape
    tm = _tile(s, ROW_TILE)

    def body(a_ref, w_ref, g_ref, raw_ref, y_ref):
        av = a_ref[...]
        for p in range(p_n):
            raw_ref[:, p * n:(p + 1) * n] = _dot(av, w_ref[p])
        _head_rms(raw_ref, g_ref, mult, y_ref)

    row = lambda wid: pl.BlockSpec((tm, wid), lambda i: (i, 0))
    return pl.pallas_call(
        body, name=name, grid=(s // tm,), in_specs=[row(k), _full(w.shape), _full((1, D_MODEL))],
        out_specs=[row(p_n * n), row(D_MODEL)],
        out_shape=[jax.ShapeDtypeStruct((s, p_n * n), F32), jax.ShapeDtypeStruct((s, D_MODEL), BF16)],
    )(a, w, g)


def _kv_proj(hk, w_k, w_v, w_f, g, name):
    s, k = hk.shape
    tm = _tile(s, ROW_TILE)

    def body(h_ref, wk_ref, wv_ref, wf_ref, g_ref, kr_ref, k_ref, v_ref, f_ref):
        hv = h_ref[...]
        kr_ref[...] = _dot(hv, wk_ref[0])
        v_ref[...] = _dot(hv, wv_ref[0]).astype(BF16)
        f_ref[...] = _dot(hv, wf_ref[0])
        _head_rms(kr_ref, g_ref, 1.0, k_ref)

    row = lambda wid: pl.BlockSpec((tm, wid), lambda i: (i, 0))
    return pl.pallas_call(
        body, name=name, grid=(s // tm,),
        in_specs=[row(k), _full(w_k.shape), _full(w_v.shape), _full(w_f.shape), _full((1, D_MODEL))],
        out_specs=[row(D_MODEL), row(D_MODEL), row(D_MODEL), row(LANES)],
        out_shape=[jax.ShapeDtypeStruct((s, D_MODEL), F32), jax.ShapeDtypeStruct((s, D_MODEL), BF16),
                   jax.ShapeDtypeStruct((s, D_MODEL), BF16), jax.ShapeDtypeStruct((s, LANES), F32)],
    )(hk, w_k, w_v, w_f, g)


def _headnorm_bwd(x, g, mult, dy, name, col0=0, extra=None):
    s = x.shape[0]
    tm = _tile(s, ROW_TILE)
    groups = 2 if extra is not None else 1
    head_major = dy.ndim == 3

    def body(*refs):
        x_ref, g_ref, dy_ref = refs[:3]
        dx_ref, dg_ref = refs[-2:]

        @pl.when(pl.program_id(0) == 0)
        def _():
            dg_ref[...] = jnp.zeros_like(dg_ref)

        for h in range(HEADS):
            cs = slice(h * HEAD_DIM, (h + 1) * HEAD_DIM)
            xv, gv = x_ref[:, cs], g_ref[:, cs]
            dyv = dy_ref[h, :, 0:HEAD_DIM] if head_major else dy_ref[:, cs]
            inv = lax.rsqrt(jnp.mean(xv * xv, axis=-1, keepdims=True) + EPS)
            nrm = xv * inv
            dn = dyv * gv * mult
            dg_ref[:, cs] += _colsum8(dyv * nrm * mult)
            dx_ref[0, :, cs] = (inv * (dn - nrm * jnp.mean(dn * nrm, axis=-1, keepdims=True))).astype(BF16)
        if extra is not None:
            dx_ref[1] = refs[3][...]

    row = pl.BlockSpec((tm, D_MODEL), lambda i: (i, 0))
    dy_spec = pl.BlockSpec((HEADS, tm, dy.shape[-1]), lambda i: (0, i, 0)) if head_major else row
    ins = [x, g, dy] + ([extra] if extra is not None else [])
    specs = ([pl.BlockSpec((tm, D_MODEL), lambda i: (i, col0)), _full((1, D_MODEL)), dy_spec]
             + ([row] if extra is not None else []))
    return pl.pallas_call(
        body, name=name, grid=(s // tm,), in_specs=specs,
        out_specs=[pl.BlockSpec((groups, tm, D_MODEL), lambda i: (0, i, 0)), _full((SUBLANES, D_MODEL))],
        out_shape=[jax.ShapeDtypeStruct((groups, s, D_MODEL), BF16), jax.ShapeDtypeStruct((SUBLANES, D_MODEL), F32)],
    )(*ins)


def _log_sigmoid(z):
    return jnp.minimum(z, 0.0) - jnp.log(1.0 + jnp.exp(-jnp.abs(z)))


Q_CUM, Q_ONE, Q_LSE = 0, 3, 6
LOG2E = 1.4426950408889634


def _pieces(v):
    hi = v.astype(BF16).astype(F32)
    mid = (v - hi).astype(BF16).astype(F32)
    lo = ((v - hi) - mid).astype(BF16).astype(F32)
    return hi, mid, lo


def _side(lane, at, v):
    hi, mid, lo = _pieces(v)
    return jnp.where(lane == at, hi, jnp.where(lane == at + 1, mid, jnp.where(lane == at + 2, lo, 0.0)))


def _fcum_fwd(f, bias, name):
    s = f.shape[0]
    tm = _tile(s, ROW_TILE)

    def body(f_ref, b_ref, qa_ref, ka_ref, carry):
        @pl.when(pl.program_id(0) == 0)
        def _():
            carry[...] = jnp.zeros_like(carry)

        cum = _tri_dot(_tri(tm), _log_sigmoid(f_ref[...] + b_ref[...])) + carry[...]
        carry[...] = cum[tm - 1:tm]
        lane = lax.broadcasted_iota(jnp.int32, (tm, LANES), 1)
        ones_q = jnp.where((lane >= Q_ONE) & (lane < Q_LSE), 1.0, 0.0)
        ones_k = jnp.where((lane < Q_ONE) | ((lane >= Q_LSE) & (lane < Q_LSE + 3)), 1.0, 0.0)
        for h in range(HEADS):
            c2 = cum[:, h:h + 1] * LOG2E
            qa_ref[h] = (_side(lane, Q_CUM, c2) + ones_q).astype(BF16)
            ka_ref[h] = (_side(lane, Q_ONE, -c2) + ones_k).astype(BF16)

    side = pl.BlockSpec((HEADS, tm, LANES), lambda i: (0, i, 0))
    return pl.pallas_call(
        body, name=name, grid=(s // tm,),
        in_specs=[pl.BlockSpec((tm, LANES), lambda i: (i, 0)), _full((1, LANES))],
        out_specs=[side, side],
        out_shape=[jax.ShapeDtypeStruct((HEADS, s, LANES), BF16)] * 2,
        scratch_shapes=[pltpu.VMEM((1, LANES), F32)],
    )(f, bias)


def _fcum_bwd(f, bias, dka, dcq, name):
    s = f.shape[0]
    tm = _tile(s, ROW_TILE)
    n_b = s // tm

    def body(f_ref, b_ref, dka_ref, dcq_ref, dz_ref, db_ref, carry):
        @pl.when(pl.program_id(0) == 0)
        def _():
            carry[...] = jnp.zeros_like(carry)
            db_ref[...] = jnp.zeros_like(db_ref)

        lane = lax.broadcasted_iota(jnp.int32, (tm, LANES), 1)
        rows = jnp.concatenate([dcq_ref[h] for h in range(HEADS)] + [jnp.zeros((LANES - HEADS, tm), F32)], axis=0)
        dcum = rows.T
        for h in range(HEADS):
            dcum = dcum - jnp.where(lane == h, dka_ref[h, :, Q_ONE:Q_ONE + 1], 0.0)
        dlf = _tri_dot(_tri(tm, upper=True), dcum) + carry[...]
        carry[...] = dlf[0:1]
        dz = dlf * _sig(-(f_ref[...] + b_ref[...]))
        dz_ref[0] = dz.astype(BF16)
        db_ref[...] += _colsum8(dz)

    return pl.pallas_call(
        body, name=name, grid=(n_b,),
        in_specs=[pl.BlockSpec((tm, LANES), lambda i: (n_b - 1 - i, 0)), _full((1, LANES)),
                  pl.BlockSpec((HEADS, tm, LANES), lambda i: (0, n_b - 1 - i, 0)),
                  pl.BlockSpec((HEADS, 1, tm), lambda i: (0, 0, n_b - 1 - i))],
        out_specs=[pl.BlockSpec((1, tm, LANES), lambda i: (0, n_b - 1 - i, 0)), _full((SUBLANES, LANES))],
        out_shape=[jax.ShapeDtypeStruct((1, s, LANES), BF16), jax.ShapeDtypeStruct((SUBLANES, LANES), F32)],
        scratch_shapes=[pltpu.VMEM((1, LANES), F32)],
    )(f, bias, dka, dcq)


def _causal_pairs(n_t, key_major):
    if key_major:
        pairs = [(qi, ki) for ki in range(n_t) for qi in range(ki, n_t)]
    else:
        pairs = [(qi, ki) for qi in range(n_t) for ki in range(qi + 1)]
    return (jnp.array([p[0] for p in pairs], jnp.int32), jnp.array([p[1] for p in pairs], jnp.int32))


def _lane_const(t, lo, hi, value):
    lane = lax.broadcasted_iota(jnp.int32, (t, LANES), 1)
    return jnp.where((lane >= lo) & (lane < hi), value, 0.0).astype(BF16)


def _att_specs(t, nh):
    qmain = pl.BlockSpec((t, nh * HEAD_DIM), lambda h, p, qt, kt: (qt[p], h))
    kmain = pl.BlockSpec((t, nh * HEAD_DIM), lambda h, p, qt, kt: (kt[p], h))
    qside = pl.BlockSpec((nh, t, LANES), lambda h, p, qt, kt: (h, qt[p], 0))
    kside = pl.BlockSpec((nh, t, LANES), lambda h, p, qt, kt: (h, kt[p], 0))
    return qmain, kmain, qside, kside


def _fox_fwd(q, qa, k, ka, v, qo, name):
    s = q.shape[0]
    t = _tile(s, ATT_TILE)
    sub = t // ATT_SPLIT
    nh = ATT_FWD_HEADS
    qt, kt = _causal_pairs(s // t, key_major=False)

    def body(qt_ref, kt_ref, q_ref, qa_ref, k_ref, ka_ref, v_ref, og_ref, o_ref, y_ref, qab_ref, m_s, l_s, acc_s):
        pid = pl.program_id(1)
        qi, ki = qt_ref[pid], kt_ref[pid]

        @pl.when(ki == 0)
        def _():
            m_s[...] = jnp.full_like(m_s, NEG_INF)
            l_s[...] = jnp.zeros_like(l_s)
            acc_s[...] = jnp.zeros_like(acc_s)

        def step(diagonal):
            for hh in range(nh):
                hc = slice(hh * HEAD_DIM, (hh + 1) * HEAD_DIM)
                kc = jnp.concatenate([k_ref[:, hc], ka_ref[hh]], axis=1)
                vc = jnp.concatenate([v_ref[:, hc], _lane_const(t, 0, 1, 1.0)], axis=1)
                for r in range(ATT_SPLIT):
                    rows = slice(r * sub, (r + 1) * sub)
                    n_k = (r + 1) * sub if diagonal else t
                    sc = _dot_nt(jnp.concatenate([q_ref[rows, hc], qa_ref[hh, rows]], axis=1), kc[:n_k])
                    if diagonal:
                        sc = jnp.where(lax.broadcasted_iota(jnp.int32, (sub, n_k), 1)
                                       <= lax.broadcasted_iota(jnp.int32, (sub, n_k), 0) + r * sub, sc, NEG_INF)
                    m_old = m_s[hh, rows]
                    m_new = jnp.maximum(m_old, jnp.max(sc, axis=-1, keepdims=True))
                    alpha = jnp.exp2(m_old - m_new)
                    pv = _dot(jnp.exp2(sc - m_new[:, 0:1]).astype(BF16), vc[:n_k])
                    acc_s[hh, rows] = alpha * acc_s[hh, rows] + pv[:, :HEAD_DIM]
                    l_s[hh, rows] = alpha * l_s[hh, rows] + pv[:, HEAD_DIM:]
                    m_s[hh, rows] = m_new

        @pl.when(ki < qi)
        def _():
            step(False)

        @pl.when(ki == qi)
        def _():
            step(True)
            lane = lax.broadcasted_iota(jnp.int32, (t, LANES), 1)
            for hh in range(nh):
                hc = slice(hh * HEAD_DIM, (hh + 1) * HEAD_DIM)
                l = l_s[hh, :, 0:1]
                o = acc_s[hh] / l
                o_ref[:, hc] = o
                y_ref[:, hc] = (o * _sig(og_ref[:, hc])).astype(BF16)
                qab_ref[hh] = qa_ref[hh] + _side(lane, Q_LSE, -(m_s[hh, :, 0:1] + jnp.log2(l))).astype(BF16)

    qmain, kmain, qside, kside = _att_specs(t, nh)
    return pl.pallas_call(
        body, name=name,
        grid_spec=pltpu.PrefetchScalarGridSpec(
            num_scalar_prefetch=2, grid=(HEADS // nh, qt.shape[0]),
            in_specs=[qmain, qside, kmain, kside, kmain,
                      pl.BlockSpec((t, nh * HEAD_DIM), lambda h, p, qt, kt: (qt[p], HEADS // nh + h))],
            out_specs=[qmain, qmain, qside],
            scratch_shapes=[pltpu.VMEM((nh, t, LANES), F32), pltpu.VMEM((nh, t, LANES), F32),
                            pltpu.VMEM((nh, t, HEAD_DIM), F32)]),
        out_shape=[jax.ShapeDtypeStruct((s, D_MODEL), F32), jax.ShapeDtypeStruct((s, D_MODEL), BF16),
                   jax.ShapeDtypeStruct((HEADS, s, LANES), BF16)],
    )(qt, kt, q, qa, k, ka, v, qo)


def _fox_gate_bwd(o, qo, dout, w_out, name):
    s = o.shape[0]
    tm = _tile(s, ROW_TILE)

    def body(o_ref, og_ref, dout_ref, w_ref, do_ref, dg_ref, dl_ref):
        ov, dyv = o_ref[...], _dot_nt(dout_ref[0], w_ref[0])
        sg = _sig(og_ref[...])
        do = (dyv * sg).astype(BF16)
        do_ref[...] = do
        dg_ref[...] = (dyv * ov * sg * (1.0 - sg)).astype(BF16)
        prod = do.astype(F32) * ov
        lane = lax.broadcasted_iota(jnp.int32, (tm, LANES), 1)
        for h in range(HEADS):
            delta = jnp.sum(prod[:, h * HEAD_DIM:(h + 1) * HEAD_DIM], axis=-1, keepdims=True)
            dl_ref[h] = _side(lane, 0, delta).astype(BF16)

    row = pl.BlockSpec((tm, D_MODEL), lambda i: (i, 0))
    return pl.pallas_call(
        body, name=name, grid=(s // tm,),
        in_specs=[row, pl.BlockSpec((tm, D_MODEL), lambda i: (i, 1)),
                  pl.BlockSpec((1, tm, D_MODEL), lambda i: (0, i, 0)), _full(w_out.shape)],
        out_specs=[row, row, pl.BlockSpec((HEADS, tm, LANES), lambda i: (0, i, 0))],
        out_shape=[jax.ShapeDtypeStruct((s, D_MODEL), BF16), jax.ShapeDtypeStruct((s, D_MODEL), BF16),
                   jax.ShapeDtypeStruct((HEADS, s, LANES), BF16)],
    )(o, qo, dout, w_out)


def _fox_bwd(q, qab, k, ka, v, do, doa, k_raw, k_gain, name):
    s = q.shape[0]
    t = _tile(s, ATT_TILE)
    n_t = s // t
    sub = t // ATT_BWD_SPLIT
    nh = ATT_BWD_HEADS
    qt, kt = _causal_pairs(n_t, key_major=True)

    def body(qt_ref, kt_ref, q_ref, qab_ref, k_ref, ka_ref, v_ref, do_ref, doa_ref, kr_ref, kg_ref, dkr_ref, dkg_ref,
             dv_ref, dka_ref, dq_hbm, dcq_hbm, dk_s, dv_s, dq_ref, dcq_ref):
        group, pid = pl.program_id(0), pl.program_id(1)
        qi, ki = qt_ref[pid], kt_ref[pid]

        @pl.when(pid == 0)
        def _():
            dq_ref[...] = jnp.zeros_like(dq_ref)
            dcq_ref[...] = jnp.zeros_like(dcq_ref)
            dkg_ref[...] = jnp.zeros_like(dkg_ref)

        @pl.when(qi == ki)
        def _():
            dk_s[...] = jnp.zeros_like(dk_s)
            dv_s[...] = jnp.zeros_like(dv_s)

        def step(diagonal):
            for hh in range(nh):
                hc = slice(hh * HEAD_DIM, (hh + 1) * HEAD_DIM)
                kc = jnp.concatenate([k_ref[:, hc], ka_ref[hh]], axis=1)
                vc = jnp.concatenate([v_ref[:, hc], _lane_const(t, 0, 3, -1.0)], axis=1)
                for r in range(ATT_BWD_SPLIT):
                    cols = slice(r * sub, (r + 1) * sub)
                    n_k = (r + 1) * sub if diagonal else t
                    qc = jnp.concatenate([q_ref[cols, hc], qab_ref[hh, cols]], axis=1)
                    sc = _dot_nt(kc[:n_k], qc)
                    if diagonal:
                        sc = jnp.where(lax.broadcasted_iota(jnp.int32, (n_k, sub), 0)
                                       <= lax.broadcasted_iota(jnp.int32, (n_k, sub), 1) + r * sub, sc, NEG_INF)
                    p = jnp.exp2(sc)
                    dov = do_ref[cols, hc]
                    dp = _dot_nt(vc[:n_k], jnp.concatenate([dov, doa_ref[hh, cols]], axis=1))
                    ds = (p * dp).astype(BF16)
                    dv_s[hh, 0:n_k] += _dot(p.astype(BF16), dov)
                    dk_s[hh, 0:n_k] += _dot(ds, qc)
                    q_rows = pl.ds(pl.multiple_of(qi * t + r * sub, sub), sub)
                    dq_ref[hh, q_rows, :] += _dot_tn(ds, k_ref[0:n_k, hc])
                    dcq_ref[hh, qi * ATT_BWD_SPLIT + r] += jnp.sum(ds.astype(F32), axis=0, keepdims=True)

        @pl.when(qi > ki)
        def _():
            step(False)

        @pl.when(qi == ki)
        def _():
            step(True)

        @pl.when(qi == n_t - 1)
        def _():
            for hh in range(nh):
                hc = slice(hh * HEAD_DIM, (hh + 1) * HEAD_DIM)
                dka_ref[hh] = dk_s[hh, :, HEAD_DIM:]
                dv_ref[:, hc] = dv_s[hh].astype(BF16)
                dk = dk_s[hh, :, :HEAD_DIM] * (1.0 / LOG2E)
                xv = kr_ref[:, hc]
                inv = lax.rsqrt(jnp.mean(xv * xv, axis=-1, keepdims=True) + EPS)
                nrm = xv * inv
                dn = dk * kg_ref[:, hc]
                dkg_ref[:, hc] += _colsum8(dk * nrm)
                dkr_ref[:, hc] = (inv * (dn - nrm * jnp.mean(dn * nrm, axis=-1, keepdims=True))).astype(BF16)

        @pl.when(pid == qt.shape[0] - 1)
        def _():
            pltpu.sync_copy(dq_ref, dq_hbm.at[pl.ds(group * nh, nh)])
            pltpu.sync_copy(dcq_ref, dcq_hbm.at[pl.ds(group * nh, nh)])

    qmain, kmain, qside, kside = _att_specs(t, nh)
    kmain3 = pl.BlockSpec((None, t, nh * HEAD_DIM), lambda h, p, qt, kt: (0, kt[p], h))
    in_hbm = pl.BlockSpec(memory_space=pltpu.HBM)
    return pl.pallas_call(
        body, name=name,
        grid_spec=pltpu.PrefetchScalarGridSpec(
            num_scalar_prefetch=2, grid=(HEADS // nh, qt.shape[0]),
            in_specs=[qmain, qside, kmain, kside, kmain, qmain, qside, kmain,
                      pl.BlockSpec((1, nh * HEAD_DIM), lambda h, p, qt, kt: (0, h))],
            out_specs=[kmain3, pl.BlockSpec((SUBLANES, nh * HEAD_DIM), lambda h, p, qt, kt: (0, h)), kmain3, kside,
                       in_hbm, in_hbm],
            scratch_shapes=[pltpu.VMEM((nh, t, 2 * HEAD_DIM), F32), pltpu.VMEM((nh, t, HEAD_DIM), F32),
                            pltpu.VMEM((nh, s, HEAD_DIM), F32), pltpu.VMEM((nh, s // sub, 1, sub), F32)]),
        out_shape=[jax.ShapeDtypeStruct((1, s, D_MODEL), BF16), jax.ShapeDtypeStruct((SUBLANES, D_MODEL), F32),
                   jax.ShapeDtypeStruct((1, s, D_MODEL), BF16),
                   jax.ShapeDtypeStruct((HEADS, s, LANES), F32), jax.ShapeDtypeStruct((HEADS, s, HEAD_DIM), F32),
                   jax.ShapeDtypeStruct((HEADS, s // sub, 1, sub), F32)],
    )(qt, kt, q, qab, k, ka, v, do, doa, k_raw, k_gain)


def _mm_residual_premix(a, w, x, gate, mods, name):
    s, k = a.shape
    dm = x.shape[1]
    tm = _tile(s, ROW_TILE)

    def body(*refs):
        a_ref, w_ref, x_ref, g_ref = refs[:4]
        mod_refs = refs[4:4 + 2 * len(mods)]
        y_ref, xn_ref = refs[4 + 2 * len(mods):6 + 2 * len(mods)]
        h_refs = refs[6 + 2 * len(mods):]
        y = _dot(a_ref[...], w_ref[0])
        y_ref[...] = y
        xv = x_ref[...] + g_ref[...] * y
        xn_ref[...] = xv
        nrm = xv * lax.rsqrt(jnp.mean(xv * xv, axis=-1, keepdims=True) + EPS)
        for t, h_ref in enumerate(h_refs):
            h_ref[...] = (nrm * (1.0 + mod_refs[2 * t + 1][...]) + mod_refs[2 * t][...]).astype(BF16)

    row = pl.BlockSpec((tm, dm), lambda i: (i, 0))
    vec = _full((1, dm))
    outs = pl.pallas_call(
        body, name=name, grid=(s // tm,),
        in_specs=[pl.BlockSpec((tm, k), lambda i: (i, 0)), _full(w.shape), row, vec] + [vec] * (2 * len(mods)),
        out_specs=[row] * (2 + len(mods)),
        out_shape=[jax.ShapeDtypeStruct((s, dm), F32)] * 2 + [jax.ShapeDtypeStruct((s, dm), BF16)] * len(mods),
    )(a, w, x, gate, *[v for m in mods for v in m])
    return outs[0], outs[1], list(outs[2:])


def _mm_loss_head(a, w, x, gate, target, name):
    s, k = a.shape
    dm = x.shape[1]
    tm = _tile(s, ROW_TILE)

    def body(a_ref, w_ref, x_ref, g_ref, t_ref, sq_ref, do_ref, dy_ref, dg_ref):
        @pl.when(pl.program_id(0) == 0)
        def _():
            sq_ref[...] = jnp.zeros_like(sq_ref)
            dg_ref[...] = jnp.zeros_like(dg_ref)

        y, gv = _dot(a_ref[...], w_ref[0]), g_ref[...]
        err = x_ref[...] + gv * y - t_ref[...]
        sq_ref[...] += _colsum8(err * err)
        dout = err * (1.0 / dm)
        do_ref[...] = dout
        dy_ref[0] = (dout * gv).astype(BF16)
        dg_ref[...] += _colsum8(dout * y)

    row = pl.BlockSpec((tm, dm), lambda i: (i, 0))
    acc = _full((SUBLANES, dm))
    return pl.pallas_call(
        body, name=name, grid=(s // tm,),
        in_specs=[pl.BlockSpec((tm, k), lambda i: (i, 0)), _full(w.shape), row, _full((1, dm)), row],
        out_specs=[acc, row, pl.BlockSpec((1, tm, dm), lambda i: (0, i, 0)), acc],
        out_shape=[jax.ShapeDtypeStruct((SUBLANES, dm), F32), jax.ShapeDtypeStruct((s, dm), F32),
                   jax.ShapeDtypeStruct((1, s, dm), BF16), jax.ShapeDtypeStruct((SUBLANES, dm), F32)],
    )(a, w, x, gate, target)


def _ffn_inner(h, w_up, conv_w, conv_b, tag):
    s, dm = h.shape
    half = w_up.shape[2]
    f = 2 * half
    tm = _tile(s, FFN_ROWS)

    def body(h_ref, w_ref, cw_ref, cb_ref, u_ref, c_ref, a_ref, carry):
        @pl.when(pl.program_id(0) == 0)
        def _():
            carry[...] = jnp.zeros_like(carry)

        hv = h_ref[...]
        for j in range(2):
            cols = slice(j * half, (j + 1) * half)
            conv = []
            for g in range(2):
                ub = _dot(hv, w_ref[2 * g + j]).astype(BF16)
                u_ref[g, :, cols] = ub
                uf = ub.astype(F32)
                e = jnp.concatenate([carry[g, j], uf], axis=0)
                carry[g, j] = uf[tm - SUBLANES:tm]
                conv.append(_conv_taps(e, cw_ref[g][:, cols], cb_ref[g][:, cols])[SUBLANES:])
                c_ref[g, :, cols] = conv[g].astype(BF16)
            a_ref[:, cols] = (conv[0] * _sig(conv[0]) * conv[1]).astype(BF16)

    pair = pl.BlockSpec((2, tm, f), lambda i: (0, i, 0))
    return pl.pallas_call(
        body, name=tag + "_up_convglu", grid=(s // tm,),
        in_specs=[pl.BlockSpec((tm, dm), lambda i: (i, 0)), _full(w_up.shape), _full(conv_w.shape), _full(conv_b.shape)],
        out_specs=[pair, pair, pl.BlockSpec((tm, f), lambda i: (i, 0))],
        out_shape=[jax.ShapeDtypeStruct((2, s, f), BF16)] * 2 + [jax.ShapeDtypeStruct((s, f), BF16)],
        scratch_shapes=[pltpu.VMEM((2, 2, SUBLANES, half), F32)],
    )(h, w_up, conv_w, conv_b)


def _weight_grad_first(a, d, p_n, name):
    return lax.optimization_barrier((_mm_tn(a, d, p_n, name), d))


def _ffn_backward(dx_out, dffn, x_mid, scale, saved, w_up, conv_w, conv_b, w_down, mixer, tag):
    h, u, c, a = saved
    dw_down, dffn = _weight_grad_first(a, dffn, 1, tag + "_down_dw")
    du, dconv = _convglu_bwd(u, c, dffn, w_down, conv_w, tag + "_convglu_bwd")
    dw_up, du = _weight_grad_first(h, du, N_CHIPS, tag + "_up_dw")
    dx_mid, [(dshift, dscale)], dy, dgate_mixer = _premix_bwd(x_mid, [(scale, [(du, w_up)])], dx_out,
                                                              tag + "_premix_bwd", branch=mixer)
    return dx_mid, dy, dgate_mixer, dw_up, dw_down, dict(shift=dshift, scale=dscale, conv=dconv)


def _local_step(x, target, mods, lb, vecs, weights_at):
    m0, m1, mk = mods["l0"], mods["l1"], mods["kv"]
    wts, x = weights_at("mixer0", x)
    h0, proj = _premix_proj(x, m0[0], m0[1], wts["a_w_in"], "l0_premix_in")
    _, proj = weights_at("launch_layer1", proj)
    o_a, yp, states = _hgrn_fwd(proj, lb, vecs["a_norm_g"], "l0_hgrn")
    more, yp = weights_at("out0", yp)
    wts.update(more)
    y0, x1, [hf0] = _mm_residual_premix(yp, wts["a_w_out"], x, m0[2], [(m0[3], m0[4])], "l0_out")
    more, hf0 = weights_at("ffn0", hf0)
    wts.update(more)
    u0, c0, a0 = _ffn_inner(hf0, wts["up0"], vecs["conv_w0"], vecs["conv_b0"], "l0_ffn")
    saved0 = (hf0, u0, c0, a0)
    ffn0, x2, [hk, h1] = _mm_residual_premix(a0, wts["down0"], x1, m0[5], [(mk[0], mk[1]), (m1[0], m1[1])],
                                             "l0_ffn_down")
    more, hk = weights_at("layer1", hk)
    wts.update(more)
    k_raw, k_sh, v_sh, f_raw = _kv_proj(hk, wts["kv_k"], wts["kv_v"], wts["kv_f"], vecs["k_norm_g"], "kv_proj")
    qa, ka = _fcum_fwd(f_raw, vecs["kv_b_f"], "kv_fcum")
    q_scale = HEAD_DIM ** -0.5
    qo, q = _proj_headnorm(h1, wts["b_w_q"], vecs["q_norm_g"], q_scale * LOG2E, "l1_q")
    o_b, og, qab = _fox_fwd(q, qa, k_sh, ka, v_sh, qo, "l1_fox")
    y1, x3, [hf1] = _mm_residual_premix(og, wts["b_w_out"], x2, m1[2], [(m1[3], m1[4])], "l1_out")
    u1, c1, a1 = _ffn_inner(hf1, wts["up1"], vecs["conv_w1"], vecs["conv_b1"], "l1_ffn")
    saved1 = (hf1, u1, c1, a1)
    sq, dx4, dffn1, dg2_1 = _mm_loss_head(a1, wts["down1"], x3, m1[5], target, "l1_ffn_down")

    big, small = {}, {}
    dx3, dy1, dg1_1, big["up1"], big["down1"], s_ffn1 = _ffn_backward(
        dx4, dffn1, x3, m1[4], saved1, wts["up1"], vecs["conv_w1"], vecs["conv_b1"], wts["down1"], (y1, m1[2]), "l1_ffn")
    big["b_w_out"], dy1 = _weight_grad_first(og, dy1, 1, "l1_out_dw")
    do_b, dgate_b, doa = _fox_gate_bwd(o_b, qo, dy1, wts["b_w_out"], "l1_out_dx_gate_bwd")
    dk_raw, dkg, dv, dka, dq, dcq = _fox_bwd(q, qab, k_sh, ka, v_sh, do_b, doa, k_raw, vecs["k_norm_g"], "l1_fox_bwd")
    dqo, dqg = _headnorm_bwd(qo, vecs["q_norm_g"], q_scale, dq, "l1_qnorm_bwd", extra=dgate_b)
    big["b_w_q"], dqo = _weight_grad_first(h1, dqo, N_CHIPS, "l1_q_dw")
    dz, dbf = _fcum_bwd(f_raw, vecs["kv_b_f"], dka, dcq.reshape(HEADS, 1, -1), "kv_fcum_bwd")
    big["kv_k"], dk_raw = _weight_grad_first(hk, dk_raw, 1, "kv_k_dw")
    big["kv_v"], dv = _weight_grad_first(hk, dv, 1, "kv_v_dw")
    big["kv_f"], dz = _weight_grad_first(hk, dz, 1, "kv_f_dw")
    kv_pairs = [(dk_raw, wts["kv_k"]), (dv, wts["kv_v"]), (dz, wts["kv_f"])]
    dx2, [(dsh1_1, dsc1_1), (dshk, dsck)], dffn0, dg2_0 = _premix_bwd(
        x2, [(m1[1], [(dqo, wts["b_w_q"])]), (mk[1], kv_pairs)], dx3, "l1_kv_premix_bwd", branch=(ffn0, m0[5]))
    dx1, dy0, dg1_0, big["up0"], big["down0"], s_ffn0 = _ffn_backward(
        dx2, dffn0, x1, m0[4], saved0, wts["up0"], vecs["conv_w0"], vecs["conv_b0"], wts["down0"], (y0, m0[2]), "l0_ffn")
    big["a_w_out"], dy0 = _weight_grad_first(yp, dy0, 1, "l0_out_dw")
    dproj, dlb, dng = _hgrn_bwd(proj, lb, vecs["a_norm_g"], o_a, states, dy0, wts["a_w_out"], "l0_out_dx_hgrn_bwd")
    grad_x, [(dsh1_0, dsc1_0)] = _premix_bwd(x, [(m0[1], [(dproj, wts["a_w_in"])])], dx1, "l0_premix_bwd")
    dproj, _ = lax.optimization_barrier((dproj, (dsh1_0, dsc1_0)))
    big["a_w_in"] = _mm_tn(h0, dproj, N_CHIPS, "l0_in_dw")

    small["mod_l0"] = [dsh1_0, dsc1_0, dg1_0, s_ffn0["shift"], s_ffn0["scale"], dg2_0]
    small["mod_l1"] = [dsh1_1, dsc1_1, dg1_1, s_ffn1["shift"], s_ffn1["scale"], dg2_1]
    small["mod_kv"] = [dshk, dsck]
    small["conv0"], small["conv1"] = s_ffn0["conv"], s_ffn1["conv"]
    small["a_norm_g"], small["k_norm_g"], small["q_norm_g"] = dng, dkg, dqg
    small["kv_b_f"], small["lb"] = dbf, dlb
    marks = {"attention_bwd": dv, "ffn0_bwd": dx1, "mixer0_bwd": grad_x}
    return sq, grad_x, big, small, marks


COMM_CHUNK_ELEMS = 256 * 1024


def _place():
    x, y, c = lax.axis_index("x"), lax.axis_index("y"), lax.axis_index("c")
    chips = [(1 - x, y), (x, 1 - y), (1 - x, 1 - y)]
    return x, y, c, (x, y, 1 - c), chips


def _chunk_rows(rows, cols):
    best = BF16_ROWS
    for r in range(BF16_ROWS, rows + 1, BF16_ROWS):
        if rows % r == 0 and r * cols <= COMM_CHUNK_ELEMS:
            best = r
    assert rows % best == 0, (rows, cols)
    return best


def _allgather8(block, name):
    m_per, n = block.shape

    def body(x_ref, out_ref, send_sems, recv_sems, local_sem):
        x, y, c, sibling, chips = _place()
        me = (x, y, c)

        def rows(px, py, pc):
            return out_ref.at[pl.ds((4 * px + 2 * py + pc) * m_per, m_per), :]

        def copy(k, blk, to, src=None):
            return pltpu.make_async_remote_copy(
                src_ref=rows(*blk) if src is None else src, dst_ref=rows(*blk),
                send_sem=send_sems.at[k], recv_sem=recv_sems.at[k], device_id=to, device_id_type=MESH)

        mine = pltpu.make_async_copy(x_ref, rows(*me), local_sem)
        mine.start()
        first = [copy(0, me, sibling, src=x_ref)]
        first += [copy(1 + j, me, (*chip, c), src=x_ref) for j, chip in enumerate(chips)]
        for cp in first:
            cp.start()
        passed = [copy(4 + j, (*chip, c), sibling) for j, chip in enumerate(chips)]
        for j, chip in enumerate(chips):
            copy(1 + j, (*chip, c), me).wait_recv()
            passed[j].start()
        copy(0, sibling, me).wait_recv()
        for j, chip in enumerate(chips):
            copy(4 + j, (*chip, 1 - c), me).wait_recv()
        for cp in first + passed:
            cp.wait_send()
        mine.wait()

    return pl.pallas_call(
        body, name=name, out_shape=jax.ShapeDtypeStruct((N_DEV * m_per, n), block.dtype),
        in_specs=[pl.BlockSpec(memory_space=pltpu.VMEM)], out_specs=pl.BlockSpec(memory_space=pltpu.VMEM),
        scratch_shapes=[pltpu.SemaphoreType.DMA((7,)), pltpu.SemaphoreType.DMA((7,)), pltpu.SemaphoreType.DMA],
    )(block)


def _cast_own_block(shards, layer, chip, name):
    _, r, cols = shards.shape
    rows = _chunk_rows(r, cols)

    def body(chip_ref, w_ref, o_ref):
        o_ref[...] = w_ref[...].astype(BF16)

    return pl.pallas_call(
        body, name=name,
        grid_spec=pltpu.PrefetchScalarGridSpec(
            num_scalar_prefetch=1, grid=(r // rows,),
            in_specs=[pl.BlockSpec((None, rows, cols), lambda i, chip_ref: (layer, i, 0))],
            out_specs=pl.BlockSpec((None, rows, cols), lambda i, chip_ref: (chip_ref[0], i, 0))),
        out_shape=jax.ShapeDtypeStruct((N_CHIPS, r, cols), BF16),
    )(chip, shards)


def _sequencer_gather(bufs, name, collective_id):
    n_t = len(bufs)
    dims = [b.shape[1:] for b in bufs]
    refs = [jax.new_ref(b, memory_space=pltpu.MemorySpace.HBM) for b in bufs]

    @pl.kernel(mesh=plsc.ScalarSubcoreMesh(axis_name="sequencer", num_cores=1), name=name,
               scratch_types=[pltpu.SemaphoreType.DMA((n_t,)), pltpu.SemaphoreType.DMA((3 * n_t,)),
                              pltpu.SemaphoreType.DMA((n_t,)), pltpu.SemaphoreType.DMA((n_t,))],
               compiler_params=pltpu.CompilerParams(collective_id=collective_id))
    def launch(send_ici, recv_ici, send_d2d, recv_d2d):
        x, y, c, sibling, chips = _place()
        p_me = 2 * x + y
        peers = [sibling] + [(cx, cy, c) for cx, cy in chips]
        barrier = pltpu.get_barrier_semaphore()
        for peer in peers:
            pl.semaphore_signal(barrier, inc=1, device_id=peer, device_id_type=MESH)
        pl.semaphore_wait(barrier, len(peers))

        def waiter(t, sem_s, sem_r):
            win = refs[t].at[pl.ds(0, 3), pl.ds(0, dims[t][0] // 2), :]
            return pltpu.make_async_remote_copy(src_ref=win, dst_ref=win, send_sem=sem_s.at[t], recv_sem=sem_r.at[t],
                                                device_id=sibling, device_id_type=MESH)

        def half_copy(t, chip_idx, to, sem_s, sem_r, k):
            r2 = dims[t][0] // 2
            win = refs[t].at[chip_idx, pl.ds(c * r2, r2), :]
            return pltpu.make_async_remote_copy(src_ref=win, dst_ref=win, send_sem=sem_s.at[t], recv_sem=sem_r.at[k],
                                                device_id=to, device_id_type=MESH)

        for t in range(n_t):
            for j, (cx, cy) in enumerate(chips):
                half_copy(t, p_me, (cx, cy, c), send_ici, recv_ici, 3 * t + j).start()
        for t in range(n_t):
            for j, (cx, cy) in enumerate(chips):
                half_copy(t, 2 * cx + cy, (cx, cy, c), send_ici, recv_ici, 3 * t + j).wait_recv()
                half_copy(t, 2 * cx + cy, sibling, send_d2d, recv_d2d, t).start()
        for t in range(n_t):
            waiter(t, send_d2d, recv_d2d).wait_recv()
            waiter(t, send_ici, recv_ici).wait_send()
            waiter(t, send_d2d, recv_d2d).wait_send()

    launch()
    return [r[...] for r in refs]


def _sequencer_allgather8(block, dev, name, collective_id):
    m_per, n = block.shape
    src = jax.new_ref(block, memory_space=pltpu.MemorySpace.HBM)
    out = jax.empty_ref(jax.ShapeDtypeStruct((N_DEV * m_per, n), block.dtype), memory_space=pltpu.MemorySpace.HBM)

    @pl.kernel(mesh=plsc.ScalarSubcoreMesh(axis_name="sequencer", num_cores=1), name=name,
               scratch_types=[pltpu.SemaphoreType.DMA((7,))] * 2,
               compiler_params=pltpu.CompilerParams(collective_id=collective_id))
    def launch(send_sems, recv_sems):
        x, y, c, sibling, chips = _place()
        me = (x, y, c)
        _handshake([sibling] + [(cx, cy, c) for cx, cy in chips])

        def rows(px, py, pc):
            return out.at[pl.ds((4 * px + 2 * py + pc) * m_per, m_per), :]

        def copy(k, blk, to, from_src=False):
            return pltpu.make_async_remote_copy(
                src_ref=src if from_src else rows(*blk), dst_ref=rows(*blk),
                send_sem=send_sems.at[k], recv_sem=recv_sems.at[k], device_id=to, device_id_type=MESH)

        first = [copy(0, me, sibling, True)] + [copy(1 + j, me, (*chip, c), True) for j, chip in enumerate(chips)]
        for cp in first:
            cp.start()
        passed = [copy(4 + j, (*chip, c), sibling) for j, chip in enumerate(chips)]
        for j, chip in enumerate(chips):
            copy(1 + j, (*chip, c), me).wait_recv()
            passed[j].start()
        copy(0, sibling, me).wait_recv()
        for j, chip in enumerate(chips):
            copy(4 + j, (*chip, 1 - c), me).wait_recv()
        for cp in first + passed:
            cp.wait_send()

    launch()
    return lax.dynamic_update_slice(out[...], block, (dev * m_per, 0))


def _others():
    x, y, c = lax.axis_index("x"), lax.axis_index("y"), lax.axis_index("c")
    flip = lambda v, f: 1 - v if f else v
    return [(flip(x, fx), flip(y, fy), flip(c, fc))
            for fx in (0, 1) for fy in (0, 1) for fc in (0, 1) if (fx, fy, fc) != (0, 0, 0)]


def _handshake(peers):
    barrier = pltpu.get_barrier_semaphore()
    for peer in peers:
        pl.semaphore_signal(barrier, inc=1, device_id=peer, device_id_type=MESH)
    pl.semaphore_wait(barrier, len(peers))


def _sequencer_scatter(parts, name, collective_id):
    n_t = len(parts)
    dims = [p.shape[1:] for p in parts]
    srcs = [jax.new_ref(p, memory_space=pltpu.MemorySpace.HBM) for p in parts]
    inboxes = [jax.empty_ref(jax.ShapeDtypeStruct((N_DEV - 1, r // 2, cols), BF16), memory_space=pltpu.MemorySpace.HBM)
               for r, cols in dims]

    @pl.kernel(mesh=plsc.ScalarSubcoreMesh(axis_name="sequencer", num_cores=1), name=name,
               scratch_types=[pltpu.SemaphoreType.DMA((n_t,))] * 2,
               compiler_params=pltpu.CompilerParams(collective_id=collective_id))
    def launch(send_sem, recv_sem):
        peers = _others()
        _handshake(peers)
        for t in range(n_t):
            h = dims[t][0] // 2
            for k, (qx, qy, qc) in enumerate(peers):
                pltpu.make_async_remote_copy(
                    src_ref=srcs[t].at[2 * qx + qy, pl.ds(qc * h, h), :], dst_ref=inboxes[t].at[k],
                    send_sem=send_sem.at[t], recv_sem=recv_sem.at[t], device_id=(qx, qy, qc), device_id_type=MESH).start()
        for t in range(n_t):
            win = inboxes[t]
            both = pltpu.make_async_remote_copy(src_ref=win, dst_ref=win, send_sem=send_sem.at[t],
                                                recv_sem=recv_sem.at[t], device_id=peers[0], device_id_type=MESH)
            both.wait_recv()
            both.wait_send()

    launch()
    return [b[...] for b in inboxes]


def _sum_pieces(part, inbox, place, name):
    _, r, cols = part.shape
    h = r // 2
    rows = _chunk_rows(h, cols)
    steps = h // rows

    def body(place_ref, own_ref, in_ref, o_ref):
        acc = own_ref[...].astype(F32)
        for k in range(N_DEV - 1):
            acc = acc + in_ref[k].astype(F32)
        o_ref[...] = acc

    return pl.pallas_call(
        body, name=name,
        grid_spec=pltpu.PrefetchScalarGridSpec(
            num_scalar_prefetch=1, grid=(steps,),
            in_specs=[pl.BlockSpec((None, rows, cols), lambda i, pr: (pr[0], pr[1] * steps + i, 0)),
                      pl.BlockSpec((N_DEV - 1, rows, cols), lambda i, pr: (0, i, 0))],
            out_specs=pl.BlockSpec((rows, cols), lambda i, pr: (pr[1] * steps + i, 0))),
        out_shape=jax.ShapeDtypeStruct((r, cols), F32),
    )(place, part, inbox)


def _sequencer_swap_halves(halves, name, collective_id):
    n_t = len(halves)
    refs = [jax.new_ref(a, memory_space=pltpu.MemorySpace.HBM) for a in halves]

    @pl.kernel(mesh=plsc.ScalarSubcoreMesh(axis_name="sequencer", num_cores=1), name=name,
               scratch_types=[pltpu.SemaphoreType.DMA((n_t,))] * 2,
               compiler_params=pltpu.CompilerParams(collective_id=collective_id))
    def launch(send_sem, recv_sem):
        x, y, c = lax.axis_index("x"), lax.axis_index("y"), lax.axis_index("c")
        sibling = (x, y, 1 - c)
        _handshake([sibling])
        copies = []
        for t in range(n_t):
            h = halves[t].shape[0] // 2
            win = refs[t].at[pl.ds(c * h, h), :]
            copies.append(pltpu.make_async_remote_copy(src_ref=win, dst_ref=win, send_sem=send_sem.at[t],
                                                       recv_sem=recv_sem.at[t], device_id=sibling, device_id_type=MESH))
            copies[-1].start()
        for cp in copies:
            cp.wait()

    launch()
    return [r[...] for r in refs]


def _cond_rows(c16, w, act, name):
    n_l, dm, wid = w.shape

    def body(c_ref, w_ref, o_ref, a_ref):
        cv = c_ref[...]
        if act:
            cv = cv * _sig(cv)
        a_ref[...] = cv
        o_ref[...] = _dot_f32(cv, w_ref[...])

    return pl.pallas_call(
        body, name=name, grid=(n_l,),
        in_specs=[_full((16, dm)), pl.BlockSpec((None, dm, wid), lambda l: (l, 0, 0))],
        out_specs=[pl.BlockSpec((None, 16, wid), lambda l: (l, 0, 0)), _full((16, dm))],
        out_shape=[jax.ShapeDtypeStruct((n_l, 16, wid), F32), jax.ShapeDtypeStruct((16, dm), F32)],
    )(c16, w)


def _outer_grad(ct, dm, name):
    n_l, kk, wid = dm.shape
    d_rows = ct.shape[0]

    def body(c_ref, d_ref, o_ref):
        o_ref[...] = _dot_f32(c_ref[...], d_ref[...])

    return pl.pallas_call(
        body, name=name, grid=(n_l,),
        in_specs=[_full((d_rows, kk)), pl.BlockSpec((None, kk, wid), lambda l: (l, 0, 0))],
        out_specs=pl.BlockSpec((None, d_rows, wid), lambda l: (l, 0, 0)),
        out_shape=jax.ShapeDtypeStruct((n_l, d_rows, wid), F32),
    )(ct, dm)


def _sum_devices(g, name):
    rows, n = g.shape

    def body(g_ref, o_ref):
        acc = g_ref[0:SUBLANES, :]
        for dev in range(1, N_DEV):
            acc = acc + g_ref[dev * SUBLANES:(dev + 1) * SUBLANES, :]
        o_ref[...] = acc

    return pl.pallas_call(body, name=name, out_shape=jax.ShapeDtypeStruct((SUBLANES, n), F32))(g)


def _adamw(w, g, m, v, name):
    shape = w.shape
    cols = shape[-1]
    rows = w.size // cols
    stacked = isinstance(g, (list, tuple))
    layers = list(g) if stacked else [g.reshape(rows, cols)]
    rows_l = rows // len(layers)
    tr = rows_l
    for cand in range(SUBLANES, min(rows_l, 256) + 1, SUBLANES):
        if rows_l % cand == 0:
            tr = cand
    if rows * cols <= COMM_CHUNK_ELEMS:
        tr = rows_l
    tiles_l = rows_l // tr
    c1 = 1.0 / (1.0 - ADAM_B1 ** ADAM_STEP)
    c2 = 1.0 / (1.0 - ADAM_B2 ** ADAM_STEP)

    def body(*refs):
        w_ref, m_ref, v_ref = refs[:3]
        g_refs = refs[3:3 + len(layers)]
        d_ref, mo_ref, vo_ref = refs[3 + len(layers):6 + len(layers)]
        gv = g_refs[0][...]
        for l in range(1, len(layers)):
            gv = jnp.where(pl.program_id(0) >= l * tiles_l, g_refs[l][...], gv)
        m_new = ADAM_B1 * m_ref[...] + (1.0 - ADAM_B1) * gv
        v_new = ADAM_B2 * v_ref[...] + (1.0 - ADAM_B2) * (gv * gv)
        mo_ref[...] = m_new
        vo_ref[...] = v_new
        if stacked:
            refs[-1][...] = gv
        d_ref[...] = -ADAM_LR * ((m_new * c1) / (jnp.sqrt(v_new * c2) + ADAM_EPS) + ADAM_WD * w_ref[...])

    spec = pl.BlockSpec((tr, cols), lambda i: (i, 0))
    g_specs = [pl.BlockSpec((tr, cols), lambda i, l=l: (jnp.clip(i - l * tiles_l, 0, tiles_l - 1), 0))
               for l in range(len(layers))]
    n_out = 4 if stacked else 3
    outs = pl.pallas_call(
        body, name=name, grid=(rows // tr,), in_specs=[spec] * 3 + g_specs, out_specs=[spec] * n_out,
        out_shape=[jax.ShapeDtypeStruct((rows, cols), F32)] * n_out,
    )(*[a.reshape(rows, cols) for a in (w, m, v)], *layers)
    outs = [o.reshape(shape) for o in outs]
    return (*outs[:3], outs[3] if stacked else g)


def _pad_cols(a, cols):
    return jnp.pad(a, [(0, 0)] * (a.ndim - 1) + [(0, cols - a.shape[-1])])


def _flat8(parts, width):
    v = jnp.concatenate([p.reshape(-1) for p in parts])
    return jnp.pad(v, (0, width - v.shape[0])).reshape(SUBLANES, width // SUBLANES)


KV_SHARD = 514
KV_SHARD_PAD = 640
BIG = ("a_w_in", "a_w_out", "kv_w", "b_w_q", "b_w_out", "up0", "up1", "down0", "down1")


def kernel(x, c, ada_w, ada_b, a_w_in, a_lb_logits, a_norm_g, a_w_out, kv_ada_w, kv_ada_b, kv_w, kv_b_f, k_norm_g, b_w_q, q_norm_g, b_w_out, ffn_w_up, ffn_conv_w, ffn_conv_b, ffn_w_down, loss_target, m_ada_w, m_ada_b, m_a_w_in, m_a_lb_logits, m_a_norm_g, m_a_w_out, m_kv_ada_w, m_kv_ada_b, m_kv_w, m_kv_b_f, m_k_norm_g, m_b_w_q, m_q_norm_g, m_b_w_out, m_ffn_w_up, m_ffn_conv_w, m_ffn_conv_b, m_ffn_w_down, v_ada_w, v_ada_b, v_a_w_in, v_a_lb_logits, v_a_norm_g, v_a_w_out, v_kv_ada_w, v_kv_ada_b, v_kv_w, v_kv_b_f, v_k_norm_g, v_b_w_q, v_q_norm_g, v_b_w_out, v_ffn_w_up, v_ffn_conv_w, v_ffn_conv_b, v_ffn_w_down):
    dm, ff = D_MODEL, D_FF
    ix, iy, ic = lax.axis_index("x"), lax.axis_index("y"), lax.axis_index("c")
    chip = 2 * ix + iy
    dev = 2 * chip + ic

    w1 = 10240
    g1 = _allgather8(_flat8([c, a_lb_logits, ffn_conv_w], w1), "gather_cond").reshape(N_DEV, w1)
    c_all = g1[:, :dm]
    per_chip = g1[0::2]
    lb_logits = per_chip[:, dm:dm + 512].reshape(N_CHIPS, 2, 256).transpose(1, 0, 2).reshape(2, dm)
    conv_w = per_chip[:, dm + 512:dm + 512 + 2 * CONV_W * FFN_COLS].reshape(N_CHIPS, 2, CONV_W, FFN_COLS)
    conv_w = conv_w.transpose(1, 2, 0, 3).reshape(2, CONV_W, 2, ff).transpose(0, 2, 1, 3)
    conv_b = ffn_conv_b.reshape(2, 2, 1, ff)
    lb = jax.nn.softmax(lb_logits, axis=0)[0:1]

    c16 = jnp.pad(c_all, ((0, 8), (0, 0)))
    mod_ada, c_act16 = _cond_rows(c16, ada_w, True, "mod_ada")
    mod_kv, _ = _cond_rows(c16, kv_ada_w[None], True, "mod_kv")
    mine = jnp.concatenate([mod_ada[0, :8], mod_ada[1, :8], mod_kv[0, :8]], axis=1)
    w2 = mine.shape[1]
    g2 = _allgather8(mine, "gather_mod").reshape(N_DEV, 8, w2)[0::2]
    my_rows = lax.dynamic_index_in_dim(g2, dev, axis=1, keepdims=False)
    mod0 = my_rows[:, 0:1536].reshape(6 * dm) + ada_b[0]
    mod1 = my_rows[:, 1536:3072].reshape(6 * dm) + ada_b[1]
    modk = my_rows[:, 3072:3584].reshape(2 * dm) + kv_ada_b
    mods = {"l0": [v.reshape(1, dm) for v in jnp.split(mod0, 6)],
            "l1": [v.reshape(1, dm) for v in jnp.split(mod1, 6)],
            "kv": [v.reshape(1, dm) for v in jnp.split(modk, 2)]}

    local = [(a_w_in, 0), (a_w_out, 0), (_pad_cols(kv_w, KV_SHARD_PAD)[None], 0), (b_w_q, 0), (b_w_out, 0),
             (ffn_w_up, 0), (ffn_w_up, 1), (ffn_w_down, 0), (ffn_w_down, 1)]
    chip_arr = chip.reshape(1).astype(jnp.int32)
    local = dict(zip(BIG, local))
    stages = {"mixer0": ("a_w_in",), "out0": ("a_w_out",), "ffn0": ("up0", "down0"),
              "layer1": ("kv_w", "b_w_q", "b_w_out", "up1", "down1")}
    arriving = {}

    def launch(stage, behind):
        shards = [local[n][0] for n in stages[stage]]
        if behind is not None:
            shards, _ = lax.optimization_barrier((shards, behind))
        own = [_cast_own_block(w, local[n][1], chip_arr, "cast_" + n) for n, w in zip(stages[stage], shards)]
        arriving[stage] = _sequencer_gather(own, "gather_" + stage, 1 + list(stages).index(stage))
        return own

    launch("out0", launch("mixer0", None))
    launch("ffn0", mod0)
    rowwise = lambda g: g.reshape(1, -1, dm)

    def weights_at(stage, token):
        if stage == "launch_layer1":
            launch("layer1", token)
            return {}, token
        got, token = lax.optimization_barrier((arriving[stage], token))
        g = dict(zip(stages[stage], got))
        if stage == "mixer0":
            return {"a_w_in": g["a_w_in"]}, token
        if stage == "out0":
            return {"a_w_out": rowwise(g["a_w_out"])}, token
        if stage == "ffn0":
            return {"up0": g["up0"], "down0": rowwise(g["down0"])}, token
        s0, s1, s2, s3 = (g["kv_w"][p] for p in range(N_CHIPS))
        second = dm - KV_SHARD
        w_k = jnp.concatenate([s0[:, :KV_SHARD], s1[:, :second]], axis=1)
        w_v = jnp.concatenate([s1[:, second:KV_SHARD], s2[:, :KV_SHARD], s3[:, :KV_SHARD - HEADS]], axis=1)
        w_f = _pad_cols(s3[:, KV_SHARD - HEADS:KV_SHARD], LANES)
        return {"kv_k": w_k[None], "kv_v": w_v[None], "kv_f": w_f[None], "b_w_q": g["b_w_q"],
                "b_w_out": rowwise(g["b_w_out"]), "up1": g["up1"], "down1": rowwise(g["down1"])}, token

    vecs = {"a_norm_g": jnp.tile(a_norm_g, (1, HEADS)), "k_norm_g": jnp.tile(k_norm_g[None], (1, HEADS)),
            "q_norm_g": jnp.tile(q_norm_g, (1, HEADS)), "kv_b_f": _pad_cols(kv_b_f[None], LANES),
            "conv_w0": conv_w[0], "conv_b0": conv_b[0], "conv_w1": conv_w[1], "conv_b1": conv_b[1]}

    sq, grad_x, big, small, marks = _local_step(x[0], loss_target[0], mods, lb, vecs, weights_at)

    gk, gv, gf = big["kv_k"][0], big["kv_v"][0], big["kv_f"][0][:, :HEADS]
    second = dm - KV_SHARD
    kv_blocks = [gk[:, :KV_SHARD], jnp.concatenate([gk[:, KV_SHARD:], gv[:, :KV_SHARD - second]], axis=1),
                 gv[:, KV_SHARD - second:2 * KV_SHARD - second], jnp.concatenate([gv[:, 2 * KV_SHARD - second:], gf], axis=1)]
    kv_grad = jnp.stack([_pad_cols(b, KV_SHARD_PAD) for b in kv_blocks])
    chipwise = lambda g: g.reshape(N_CHIPS, -1, dm)
    parts = dict(zip(BIG, [big["a_w_in"], chipwise(big["a_w_out"]), kv_grad, big["b_w_q"], chipwise(big["b_w_out"]),
                           big["up0"], big["up1"], chipwise(big["down0"]), chipwise(big["down1"])]))
    place = jnp.stack([chip, ic, dev]).astype(jnp.int32)

    served = []
    boxes = {}

    groups = (("up1", "down1"), ("b_w_out", "b_w_q", "kv_w"), ("up0", "down0"), ("a_w_out", "a_w_in"))

    def scatter_group(k):
        mine = [parts[n] for n in groups[k]]
        if served:
            mine, _ = lax.optimization_barrier((mine, served[-1]))
        boxes[k] = _sequencer_scatter(mine, "scatter_grads_%d" % k, 5 + k)
        served.append(boxes[k])

    def sum_group(k, token):
        inboxes, _ = lax.optimization_barrier((boxes[k], token))
        return [_sum_pieces(parts[n], box, place, "sum_" + n) for n, box in zip(groups[k], inboxes)]

    def swap_group(k, halves, behind):
        halves, _ = lax.optimization_barrier((halves, behind))
        return dict(zip(groups[k], _sequencer_swap_halves(halves, "swap_grads_%d" % k, 9 + k)))

    for k in range(3):
        scatter_group(k)
    halves = [sum_group(0, marks["attention_bwd"]), sum_group(1, marks["ffn0_bwd"]), sum_group(2, marks["mixer0_bwd"])]

    fold = lambda a: a.sum(axis=0)
    heads = lambda a: fold(a).reshape(HEADS, HEAD_DIM).sum(axis=0)
    conv_flat = lambda a: a.sum(axis=2).transpose(1, 0, 2)
    pieces = ([fold(a) for a in small["mod_l0"]] + [fold(a) for a in small["mod_l1"]] + [fold(a) for a in small["mod_kv"]]
              + [conv_flat(small["conv0"]), conv_flat(small["conv1"]), heads(small["a_norm_g"]), heads(small["k_norm_g"]),
                 heads(small["q_norm_g"]), fold(small["kv_b_f"]), fold(small["lb"]),
                 0.5 * jnp.sum(sq).reshape(1) / dm])
    w3 = 61440
    small_vec, _ = lax.optimization_barrier((_flat8(pieces, w3), served[2]))
    g3 = _sequencer_allgather8(small_vec, dev, "gather_small", 13)
    served.append(g3)
    scatter_group(3)
    rs = {}
    for k in range(3):
        rs.update(swap_group(k, halves[k], g3))
    tot = _sum_devices(g3, "sum_small").reshape(w3)
    n_mod = 14 * dm
    dmod_all = g3.reshape(N_DEV, w3)[:, :n_mod]
    o = n_mod
    conv_tot = [tot[o + l * 8 * ff: o + (l + 1) * 8 * ff].reshape(4, 2 * ff) for l in range(2)]
    o += 16 * ff
    g_a_norm, g_k_norm, g_q_norm = (tot[o + i * HEAD_DIM: o + (i + 1) * HEAD_DIM] for i in range(3))
    o += 3 * HEAD_DIM
    g_kv_b_f = tot[o:o + HEADS]
    dlb = tot[o + LANES:o + LANES + dm]
    loss = tot[o + LANES + dm]

    ct = _pad_cols(c_act16[:8].T, LANES)
    dmod_pad = jnp.pad(dmod_all, ((0, LANES - N_DEV), (0, 0)))
    cols_ada = jnp.stack([lax.dynamic_slice_in_dim(dmod_pad, l * 6 * dm + chip * 1536, 1536, axis=1) for l in range(2)])
    cols_kv = lax.dynamic_slice_in_dim(dmod_pad, 12 * dm + chip * 512, 512, axis=1)[None]
    g_ada_w = _outer_grad(ct, cols_ada, "grad_ada_w")
    g_kv_ada_w = _outer_grad(ct, cols_kv, "grad_kv_ada_w")[0]

    my_lb = lax.dynamic_slice_in_dim(lb[0], chip * 256, 256)
    l0 = lax.dynamic_slice_in_dim(dlb, chip * 256, 256) * my_lb * (1.0 - my_lb)
    grads = {
        "ada_w": g_ada_w, "ada_b": jnp.stack([tot[:6 * dm], tot[6 * dm:12 * dm]]),
        "a_lb_logits": jnp.stack([l0, -l0]), "a_norm_g": g_a_norm[None],
        "kv_ada_w": g_kv_ada_w, "kv_ada_b": tot[12 * dm:14 * dm],
        "kv_w": rs["kv_w"][:, :KV_SHARD], "kv_b_f": g_kv_b_f, "k_norm_g": g_k_norm,
        "b_w_q": rs["b_w_q"][None], "q_norm_g": g_q_norm[None], "b_w_out": rs["b_w_out"][None],
        "ffn_w_up": [rs["up0"], rs["up1"]],
        "ffn_conv_w": jnp.stack([lax.dynamic_slice_in_dim(ct_l[:CONV_W], chip * FFN_COLS, FFN_COLS, axis=1) for ct_l in conv_tot]),
        "ffn_conv_b": jnp.stack([ct_l[CONV_W] for ct_l in conv_tot]),
        "ffn_w_down": [rs["down0"], rs["down1"]],
    }
    weights = dict(ada_w=ada_w, ada_b=ada_b, a_w_in=a_w_in, a_lb_logits=a_lb_logits, a_norm_g=a_norm_g, a_w_out=a_w_out,
                   kv_ada_w=kv_ada_w, kv_ada_b=kv_ada_b, kv_w=kv_w, kv_b_f=kv_b_f, k_norm_g=k_norm_g, b_w_q=b_w_q,
                   q_norm_g=q_norm_g, b_w_out=b_w_out, ffn_w_up=ffn_w_up, ffn_conv_w=ffn_conv_w, ffn_conv_b=ffn_conv_b,
                   ffn_w_down=ffn_w_down)
    m_in = dict(ada_w=m_ada_w, ada_b=m_ada_b, a_w_in=m_a_w_in, a_lb_logits=m_a_lb_logits, a_norm_g=m_a_norm_g,
                a_w_out=m_a_w_out, kv_ada_w=m_kv_ada_w, kv_ada_b=m_kv_ada_b, kv_w=m_kv_w, kv_b_f=m_kv_b_f,
                k_norm_g=m_k_norm_g, b_w_q=m_b_w_q, q_norm_g=m_q_norm_g, b_w_out=m_b_w_out, ffn_w_up=m_ffn_w_up,
                ffn_conv_w=m_ffn_conv_w, ffn_conv_b=m_ffn_conv_b, ffn_w_down=m_ffn_w_down)
    v_in = dict(ada_w=v_ada_w, ada_b=v_ada_b, a_w_in=v_a_w_in, a_lb_logits=v_a_lb_logits, a_norm_g=v_a_norm_g,
                a_w_out=v_a_w_out, kv_ada_w=v_kv_ada_w, kv_ada_b=v_kv_ada_b, kv_w=v_kv_w, kv_b_f=v_kv_b_f,
                k_norm_g=v_k_norm_g, b_w_q=v_b_w_q, q_norm_g=v_q_norm_g, b_w_out=v_b_w_out, ffn_w_up=v_ffn_w_up,
                ffn_conv_w=v_ffn_conv_w, ffn_conv_b=v_ffn_conv_b, ffn_w_down=v_ffn_w_down)

    names = list(weights)
    step = lambda n: _adamw(weights[n], grads[n], m_in[n], v_in[n], "adamw_" + n)
    grads = {n: g if isinstance(g, list) else g.reshape(weights[n].shape) for n, g in grads.items()}
    upd = {n: step(n) for n in names if n not in groups[3]}
    last = sum_group(3, [u[0] for u in upd.values()])
    for n, g in swap_group(3, last, last).items():
        grads[n] = g[None]
        upd[n] = step(n)
    grads = {n: upd[n][3] for n in names}
    return (loss, grad_x[None], *[grads[n] for n in names], *[upd[n][0] for n in names],
            *[upd[n][1] for n in names], *[upd[n][2] for n in names])
```
